```python
import math
import jax, jax.numpy as jnp
from jax import lax
import numpy as np

D_MODEL = 1024
BATCH = 16
SEQ = 2048
DEPTH = 1

DN_HEAD_DIM = 128
DN_WIDTH = D_MODEL // 2
DN_HEADS = DN_WIDTH // DN_HEAD_DIM
SHORT_CONV = 4
GLA_VAL_DIM = 128
GLA_WIDTH = D_MODEL - DN_WIDTH
GLA_HEADS = GLA_WIDTH // GLA_VAL_DIM
GLA_KEY_DIM = GLA_VAL_DIM // 2
GLA_GATE_RANK = 16
GLA_TAU = 16.0
MIX_WIDTH = DN_WIDTH + GLA_WIDTH
IN_SPLITS = (DN_WIDTH, DN_WIDTH, DN_WIDTH, DN_WIDTH, DN_HEADS, DN_HEADS,
             GLA_HEADS * GLA_KEY_DIM, GLA_HEADS * GLA_KEY_DIM, GLA_WIDTH, GLA_WIDTH, GLA_GATE_RANK)
IN_WIDTH = sum(IN_SPLITS)
CHUNK = 64
D_FF = 2816
FFN_CONV = 3
ALPHA = (2.0 * DEPTH) ** 0.25
BETA_INIT = (8.0 * DEPTH) ** -0.25
EPS = 1e-6

kernel_name = "hybrid_deltanet_gla_convffn_deepnorm_adaln"


def layer_norm(x, g, b):
    xf = x.astype(jnp.float32)
    mu = xf.mean(-1, keepdims=True)
    var = jnp.square(xf - mu).mean(-1, keepdims=True)
    return ((xf - mu) * lax.rsqrt(var + EPS) * g.astype(jnp.float32) + b.astype(jnp.float32)).astype(x.dtype)


def rms_norm(x, g):
    xf = x.astype(jnp.float32)
    return (xf * lax.rsqrt(jnp.mean(xf * xf, -1, keepdims=True) + EPS) * g.astype(jnp.float32)).astype(x.dtype)


def l2_norm(x):
    return x * lax.rsqrt(jnp.sum(x * x, -1, keepdims=True) + EPS)


def causal_dwconv(x, w):
    k_w, ch = w.shape
    return lax.conv_general_dilated(x, w[:, None, :].astype(x.dtype), window_strides=(1,),
                                    padding=[(k_w - 1, 0)], dimension_numbers=('NWC', 'WIO', 'NWC'),
                                    feature_group_count=ch)


def to_chunks(x):
    b, t, h, d = x.shape
    return x.reshape(b, t // CHUNK, CHUNK, h, d).transpose(1, 0, 3, 2, 4)


def from_chunks(x):
    n, b, h, c, d = x.shape
    return x.transpose(1, 0, 3, 2, 4).reshape(b, n * c, h, d)


def gated_delta_rule(q, k, v, log_a, beta):
    dt = v.dtype
    q, k, v, log_a, beta = (a.astype(jnp.float32) for a in (q, k, v, log_a, beta))
    bsz, _, nh, dk = q.shape
    dv = v.shape[-1]
    qc, kc, vc = to_chunks(q * dk ** -0.5), to_chunks(k), to_chunks(v)
    G = jnp.cumsum(to_chunks(log_a[..., None])[..., 0], axis=-1)
    bc = to_chunks(beta[..., None])[..., 0]
    causal = jnp.tril(jnp.ones((CHUNK, CHUNK), dtype=bool))
    strict = jnp.tril(jnp.ones((CHUNK, CHUNK), dtype=bool), -1)
    decay = jnp.exp(jnp.where(causal, G[..., :, None] - G[..., None, :], -jnp.inf))
    kb = kc * bc[..., None]
    m_low = jnp.where(strict, jnp.einsum('nbhid,nbhjd->nbhij', kb, kc) * decay, 0.0)
    lhs = m_low + jnp.eye(CHUNK, dtype=jnp.float32)
    rhs = jnp.concatenate([vc * bc[..., None], kb * jnp.exp(G)[..., None]], axis=-1)
    sol = lax.linalg.triangular_solve(lhs, rhs, left_side=True, lower=True, unit_diagonal=True)
    u, w = sol[..., :dv], sol[..., dv:]
    attn = jnp.einsum('nbhid,nbhjd->nbhij', qc, kc) * decay
    q_dec = qc * jnp.exp(G)[..., None]
    k_dec = kc * jnp.exp(G[..., -1:] - G)[..., None]
    g_last = jnp.exp(G[..., -1])

    def step(S, inp):
        u_i, w_i, attn_i, qd_i, kd_i, gl_i = inp
        v_new = u_i - jnp.einsum('bhcd,bhde->bhce', w_i, S)
        o = jnp.einsum('bhcd,bhde->bhce', qd_i, S) + jnp.einsum('bhij,bhje->bhie', attn_i, v_new)
        S = S * gl_i[..., None, None] + jnp.einsum('bhcd,bhce->bhde', kd_i, v_new)
        return S, o

    S0 = jnp.zeros((bsz, nh, dk, dv), jnp.float32)
    _, o = lax.scan(step, S0, (u, w, attn, q_dec, k_dec, g_last))
    return from_chunks(o).astype(dt)


def gla_attention(q, k, v, log_alpha):
    dt = v.dtype
    q, k, v, log_alpha = (a.astype(jnp.float32) for a in (q, k, v, log_alpha))
    bsz, _, nh, dk = q.shape
    dv = v.shape[-1]
    qc, kc, vc = to_chunks(q * dk ** -0.5), to_chunks(k), to_chunks(v)
    b = jnp.cumsum(to_chunks(log_alpha), axis=-2)
    causal = jnp.tril(jnp.ones((CHUNK, CHUNK), dtype=bool))
    q_dec = qc * jnp.exp(b)
    attn = jnp.where(causal, jnp.einsum('nbhid,nbhjd->nbhij', q_dec, kc * jnp.exp(-b)), 0.0)
    o_intra = jnp.einsum('nbhij,nbhje->nbhie', attn, vc)
    k_dec = kc * jnp.exp(b[..., -1:, :] - b)
    g_last = jnp.exp(b[..., -1, :])

    def step(S, inp):
        qd_i, kd_i, v_i, oi_i, gl_i = inp
        o = jnp.einsum('bhcd,bhde->bhce', qd_i, S) + oi_i
        S = S * gl_i[..., :, None] + jnp.einsum('bhcd,bhce->bhde', kd_i, v_i)
        return S, o

    S0 = jnp.zeros((bsz, nh, dk, dv), jnp.float32)
    _, o = lax.scan(step, S0, (q_dec, k_dec, vc, o_intra, g_last))
    return from_chunks(o).astype(dt)


def hybrid_mixer(h, w_in, dn_conv, dn_a_log, dn_dt_bias, dn_norm_g, gla_w_gate2, gla_b_gate, gla_norm_g, w_o):
    bsz, t, _ = h.shape
    proj = h @ w_in
    offsets, acc = [], 0
    for s in IN_SPLITS[:-1]:
        acc += s
        offsets.append(acc)
    (dn_q, dn_k, dn_v, dn_z, dn_a, dn_b, gl_q, gl_k, gl_v, gl_g, gl_r) = jnp.split(proj, offsets, axis=-1)
    qkv = jax.nn.silu(causal_dwconv(jnp.concatenate([dn_q, dn_k, dn_v], -1), dn_conv))
    q, k, v = (a.reshape(bsz, t, DN_HEADS, DN_HEAD_DIM) for a in jnp.split(qkv, 3, axis=-1))
    log_a = -jnp.exp(dn_a_log) * jax.nn.softplus(dn_a + dn_dt_bias)
    beta = jax.nn.sigmoid(dn_b)
    o_dn = gated_delta_rule(l2_norm(q), l2_norm(k), v, log_a, beta)
    o_dn = rms_norm(o_dn, dn_norm_g) * jax.nn.silu(dn_z.reshape(bsz, t, DN_HEADS, DN_HEAD_DIM))
    log_alpha = jax.nn.log_sigmoid(gl_r @ gla_w_gate2 + gla_b_gate) / GLA_TAU
    o_gla = gla_attention(gl_q.reshape(bsz, t, GLA_HEADS, GLA_KEY_DIM),
                          gl_k.reshape(bsz, t, GLA_HEADS, GLA_KEY_DIM),
                          gl_v.reshape(bsz, t, GLA_HEADS, GLA_VAL_DIM),
                          log_alpha.reshape(bsz, t, GLA_HEADS, GLA_KEY_DIM))
    o_gla = rms_norm(o_gla, gla_norm_g) * jax.nn.silu(gl_g.reshape(bsz, t, GLA_HEADS, GLA_VAL_DIM))
    o = jnp.concatenate([o_dn.reshape(bsz, t, DN_WIDTH), o_gla.reshape(bsz, t, GLA_WIDTH)], axis=-1)
    return o @ w_o


def conv_ffn(h, w_up, conv_w, conv_b, w_down):
    u = causal_dwconv(h @ w_up, conv_w) + conv_b
    gate, val = jnp.split(u, 2, axis=-1)
    return (jax.nn.silu(gate) * val) @ w_down


def _fwd_setup_inputs(seed: int = 0) -> dict:
    key = jax.random.key(seed)
    ks = jax.random.split(key, 24)
    nrm = lambda k, shape, s: jax.random.normal(k, shape, jnp.float32) * s
    L, D = DEPTH, D_MODEL
    dt = jnp.exp(jax.random.uniform(ks[9], (L, DN_HEADS), jnp.float32, math.log(1e-3), math.log(1e-1)))
    return {
        "x": nrm(ks[0], (BATCH, SEQ, D), 1.0),
        "c": nrm(ks[1], (BATCH, D), 1.0),
        "ln0_g": 1.0 + nrm(ks[2], (D,), 0.02),
        "ln0_b": nrm(ks[3], (D,), 0.02),
        "w_ada": nrm(ks[4], (L, D, 6 * D), 0.1 * D ** -0.5),
        "b_ada": nrm(ks[5], (L, 6 * D), 0.01),
        "w_in": nrm(ks[6], (L, D, IN_WIDTH), D ** -0.5),
        "dn_conv": nrm(ks[7], (L, SHORT_CONV, 3 * DN_WIDTH), SHORT_CONV ** -0.5),
        "dn_a_log": jnp.log(jax.random.uniform(ks[8], (L, DN_HEADS), jnp.float32, 1.0, 16.0)),
        "dn_dt_bias": dt + jnp.log(-jnp.expm1(-dt)),
        "dn_norm_g": 1.0 + nrm(ks[10], (L, DN_HEAD_DIM), 0.02),
        "gla_w_gate2": nrm(ks[11], (L, GLA_GATE_RANK, GLA_HEADS * GLA_KEY_DIM), GLA_GATE_RANK ** -0.5),
        "gla_b_gate": nrm(ks[12], (L, GLA_HEADS * GLA_KEY_DIM), 0.01),
        "gla_norm_g": 1.0 + nrm(ks[13], (L, GLA_VAL_DIM), 0.02),
        "w_o": nrm(ks[14], (L, MIX_WIDTH, D), MIX_WIDTH ** -0.5 * BETA_INIT),
        "ln1_g": 1.0 + nrm(ks[15], (L, D), 0.02),
        "ln1_b": nrm(ks[16], (L, D), 0.02),
        "ffn_w_up": nrm(ks[17], (L, D, 2 * D_FF), D ** -0.5),
        "ffn_conv": nrm(ks[18], (L, FFN_CONV, 2 * D_FF), FFN_CONV ** -0.5),
        "ffn_conv_b": nrm(ks[19], (L, 2 * D_FF), 0.02),
        "ffn_w_down": nrm(ks[20], (L, D_FF, D), D_FF ** -0.5 * BETA_INIT),
        "ln2_g": 1.0 + nrm(ks[21], (L, D), 0.02),
        "ln2_b": nrm(ks[22], (L, D), 0.02),
    }


def _fwd_reference(x, c, ln0_g, ln0_b, w_ada, b_ada, w_in, dn_conv, dn_a_log, dn_dt_bias, dn_norm_g,
              gla_w_gate2, gla_b_gate, gla_norm_g, w_o, ln1_g, ln1_b, ffn_w_up, ffn_conv, ffn_conv_b,
              ffn_w_down, ln2_g, ln2_b):
    x = layer_norm(x, ln0_g, ln0_b)
    cond = jax.nn.silu(c)
    for l in range(DEPTH):
        mod = cond @ w_ada[l] + b_ada[l]
        sh_a, sc_a, gt_a, sh_f, sc_f, gt_f = jnp.split(mod[:, None, :], 6, axis=-1)
        h = x * (1.0 + sc_a) + sh_a
        y = hybrid_mixer(h, w_in[l], dn_conv[l], dn_a_log[l], dn_dt_bias[l], dn_norm_g[l],
                         gla_w_gate2[l], gla_b_gate[l], gla_norm_g[l], w_o[l])
        x = layer_norm(ALPHA * x + (1.0 + gt_a) * y, ln1_g[l], ln1_b[l])
        h = x * (1.0 + sc_f) + sh_f
        y = conv_ffn(h, ffn_w_up[l], ffn_conv[l], ffn_conv_b[l], ffn_w_down[l])
        x = layer_norm(ALPHA * x + (1.0 + gt_f) * y, ln2_g[l], ln2_b[l])
    return x


import jax as _jax
import jax.numpy as _jnp

TWIN_FORMAT = 'train_step'
FWD_PARAMS = ['x', 'c', 'ln0_g', 'ln0_b', 'w_ada', 'b_ada', 'w_in', 'dn_conv', 'dn_a_log', 'dn_dt_bias', 'dn_norm_g', 'gla_w_gate2', 'gla_b_gate', 'gla_norm_g', 'w_o', 'ln1_g', 'ln1_b', 'ffn_w_up', 'ffn_conv', 'ffn_conv_b', 'ffn_w_down', 'ln2_g', 'ln2_b']
TWIN_WEIGHTS = ['ln0_g', 'ln0_b', 'w_ada', 'b_ada', 'w_in', 'dn_conv', 'dn_a_log', 'dn_dt_bias', 'dn_norm_g', 'gla_w_gate2', 'gla_b_gate', 'gla_norm_g', 'w_o', 'ln1_g', 'ln1_b', 'ffn_w_up', 'ffn_conv', 'ffn_conv_b', 'ffn_w_down', 'ln2_g', 'ln2_b']
TWIN_DIFF_INPUT = 'x'
TWIN_INPUTS = ['x', 'c', 'ln0_g', 'ln0_b', 'w_ada', 'b_ada', 'w_in', 'dn_conv', 'dn_a_log', 'dn_dt_bias', 'dn_norm_g', 'gla_w_gate2', 'gla_b_gate', 'gla_norm_g', 'w_o', 'ln1_g', 'ln1_b', 'ffn_w_up', 'ffn_conv', 'ffn_conv_b', 'ffn_w_down', 'ln2_g', 'ln2_b', 'loss_target', 'm_ln0_g', 'm_ln0_b', 'm_w_ada', 'm_b_ada', 'm_w_in', 'm_dn_conv', 'm_dn_a_log', 'm_dn_dt_bias', 'm_dn_norm_g', 'm_gla_w_gate2', 'm_gla_b_gate', 'm_gla_norm_g', 'm_w_o', 'm_ln1_g', 'm_ln1_b', 'm_ffn_w_up', 'm_ffn_conv', 'm_ffn_conv_b', 'm_ffn_w_down', 'm_ln2_g', 'm_ln2_b', 'v_ln0_g', 'v_ln0_b', 'v_w_ada', 'v_b_ada', 'v_w_in', 'v_dn_conv', 'v_dn_a_log', 'v_dn_dt_bias', 'v_dn_norm_g', 'v_gla_w_gate2', 'v_gla_b_gate', 'v_gla_norm_g', 'v_w_o', 'v_ln1_g', 'v_ln1_b', 'v_ffn_w_up', 'v_ffn_conv', 'v_ffn_conv_b', 'v_ffn_w_down', 'v_ln2_g', 'v_ln2_b']
TWIN_OUTPUTS = ['loss', 'grad_x', 'grad_ln0_g', 'grad_ln0_b', 'grad_w_ada', 'grad_b_ada', 'grad_w_in', 'grad_dn_conv', 'grad_dn_a_log', 'grad_dn_dt_bias', 'grad_dn_norm_g', 'grad_gla_w_gate2', 'grad_gla_b_gate', 'grad_gla_norm_g', 'grad_w_o', 'grad_ln1_g', 'grad_ln1_b', 'grad_ffn_w_up', 'grad_ffn_conv', 'grad_ffn_conv_b', 'grad_ffn_w_down', 'grad_ln2_g', 'grad_ln2_b', 'delta_ln0_g', 'delta_ln0_b', 'delta_w_ada', 'delta_b_ada', 'delta_w_in', 'delta_dn_conv', 'delta_dn_a_log', 'delta_dn_dt_bias', 'delta_dn_norm_g', 'delta_gla_w_gate2', 'delta_gla_b_gate', 'delta_gla_norm_g', 'delta_w_o', 'delta_ln1_g', 'delta_ln1_b', 'delta_ffn_w_up', 'delta_ffn_conv', 'delta_ffn_conv_b', 'delta_ffn_w_down', 'delta_ln2_g', 'delta_ln2_b', 'new_m_ln0_g', 'new_m_ln0_b', 'new_m_w_ada', 'new_m_b_ada', 'new_m_w_in', 'new_m_dn_conv', 'new_m_dn_a_log', 'new_m_dn_dt_bias', 'new_m_dn_norm_g', 'new_m_gla_w_gate2', 'new_m_gla_b_gate', 'new_m_gla_norm_g', 'new_m_w_o', 'new_m_ln1_g', 'new_m_ln1_b', 'new_m_ffn_w_up', 'new_m_ffn_conv', 'new_m_ffn_conv_b', 'new_m_ffn_w_down', 'new_m_ln2_g', 'new_m_ln2_b', 'new_v_ln0_g', 'new_v_ln0_b', 'new_v_w_ada', 'new_v_b_ada', 'new_v_w_in', 'new_v_dn_conv', 'new_v_dn_a_log', 'new_v_dn_dt_bias', 'new_v_dn_norm_g', 'new_v_gla_w_gate2', 'new_v_gla_b_gate', 'new_v_gla_norm_g', 'new_v_w_o', 'new_v_ln1_g', 'new_v_ln1_b', 'new_v_ffn_w_up', 'new_v_ffn_conv', 'new_v_ffn_conv_b', 'new_v_ffn_w_down', 'new_v_ln2_g', 'new_v_ln2_b']
TWIN_LEAF_KINDS = {'loss': 'loss', 'grad_x': 'grad_x', 'grad_ln0_g': 'grad_w', 'grad_ln0_b': 'grad_w', 'grad_w_ada': 'grad_w', 'grad_b_ada': 'grad_w', 'grad_w_in': 'grad_w', 'grad_dn_conv': 'grad_w', 'grad_dn_a_log': 'grad_w', 'grad_dn_dt_bias': 'grad_w', 'grad_dn_norm_g': 'grad_w', 'grad_gla_w_gate2': 'grad_w', 'grad_gla_b_gate': 'grad_w', 'grad_gla_norm_g': 'grad_w', 'grad_w_o': 'grad_w', 'grad_ln1_g': 'grad_w', 'grad_ln1_b': 'grad_w', 'grad_ffn_w_up': 'grad_w', 'grad_ffn_conv': 'grad_w', 'grad_ffn_conv_b': 'grad_w', 'grad_ffn_w_down': 'grad_w', 'grad_ln2_g': 'grad_w', 'grad_ln2_b': 'grad_w', 'delta_ln0_g': 'delta_w', 'delta_ln0_b': 'delta_w', 'delta_w_ada': 'delta_w', 'delta_b_ada': 'delta_w', 'delta_w_in': 'delta_w', 'delta_dn_conv': 'delta_w', 'delta_dn_a_log': 'delta_w', 'delta_dn_dt_bias': 'delta_w', 'delta_dn_norm_g': 'delta_w', 'delta_gla_w_gate2': 'delta_w', 'delta_gla_b_gate': 'delta_w', 'delta_gla_norm_g': 'delta_w', 'delta_w_o': 'delta_w', 'delta_ln1_g': 'delta_w', 'delta_ln1_b': 'delta_w', 'delta_ffn_w_up': 'delta_w', 'delta_ffn_conv': 'delta_w', 'delta_ffn_conv_b': 'delta_w', 'delta_ffn_w_down': 'delta_w', 'delta_ln2_g': 'delta_w', 'delta_ln2_b': 'delta_w', 'new_m_ln0_g': 'new_m', 'new_m_ln0_b': 'new_m', 'new_m_w_ada': 'new_m', 'new_m_b_ada': 'new_m', 'new_m_w_in': 'new_m', 'new_m_dn_conv': 'new_m', 'new_m_dn_a_log': 'new_m', 'new_m_dn_dt_bias': 'new_m', 'new_m_dn_norm_g': 'new_m', 'new_m_gla_w_gate2': 'new_m', 'new_m_gla_b_gate': 'new_m', 'new_m_gla_norm_g': 'new_m', 'new_m_w_o': 'new_m', 'new_m_ln1_g': 'new_m', 'new_m_ln1_b': 'new_m', 'new_m_ffn_w_up': 'new_m', 'new_m_ffn_conv': 'new_m', 'new_m_ffn_conv_b': 'new_m', 'new_m_ffn_w_down': 'new_m', 'new_m_ln2_g': 'new_m', 'new_m_ln2_b': 'new_m', 'new_v_ln0_g': 'new_v', 'new_v_ln0_b': 'new_v', 'new_v_w_ada': 'new_v', 'new_v_b_ada': 'new_v', 'new_v_w_in': 'new_v', 'new_v_dn_conv': 'new_v', 'new_v_dn_a_log': 'new_v', 'new_v_dn_dt_bias': 'new_v', 'new_v_dn_norm_g': 'new_v', 'new_v_gla_w_gate2': 'new_v', 'new_v_gla_b_gate': 'new_v', 'new_v_gla_norm_g': 'new_v', 'new_v_w_o': 'new_v', 'new_v_ln1_g': 'new_v', 'new_v_ln1_b': 'new_v', 'new_v_ffn_w_up': 'new_v', 'new_v_ffn_conv': 'new_v', 'new_v_ffn_conv_b': 'new_v', 'new_v_ffn_w_down': 'new_v', 'new_v_ln2_g': 'new_v', 'new_v_ln2_b': 'new_v'}


def _forward(args):
    return _fwd_reference(*[args[k] for k in FWD_PARAMS])


def _output_shape():
    out = _jax.eval_shape(lambda: _forward(_fwd_setup_inputs(0)))
    return out.shape, out.dtype

N_MICROBATCH = 1
ADAM_LR = 0.001
ADAM_B1 = 0.9
ADAM_B2 = 0.999
ADAM_EPS = 1e-08
ADAM_WD = 0.01
ADAM_STEP = 10
PER_EXAMPLE_BATCH_AXIS = {'x': 0, 'c': 0, 'loss_target': 0}
SHARED_INPUTS = []
_WEIGHT_DTYPES = {'ln0_g': _jnp.float32, 'ln0_b': _jnp.float32, 'w_ada': _jnp.float32, 'b_ada': _jnp.float32, 'w_in': _jnp.float32, 'dn_conv': _jnp.float32, 'dn_a_log': _jnp.float32, 'dn_dt_bias': _jnp.float32, 'dn_norm_g': _jnp.float32, 'gla_w_gate2': _jnp.float32, 'gla_b_gate': _jnp.float32, 'gla_norm_g': _jnp.float32, 'w_o': _jnp.float32, 'ln1_g': _jnp.float32, 'ln1_b': _jnp.float32, 'ffn_w_up': _jnp.float32, 'ffn_conv': _jnp.float32, 'ffn_conv_b': _jnp.float32, 'ffn_w_down': _jnp.float32, 'ln2_g': _jnp.float32, 'ln2_b': _jnp.float32}
MOMENT_SCALE = {'ln0_g': 8.076109e-01, 'ln0_b': 4.247208e-01, 'w_ada': 5.235454e-02, 'b_ada': 8.996501e-02, 'w_in': 5.329968e-02, 'dn_conv': 4.344068e-02, 'dn_a_log': 5.533799e-01, 'dn_dt_bias': 5.526547e-01, 'dn_norm_g': 1.472841e-01, 'gla_w_gate2': 9.327260e-03, 'gla_b_gate': 4.165695e-02, 'gla_norm_g': 1.048270e-01, 'w_o': 9.229439e-02, 'ln1_g': 9.444955e-01, 'ln1_b': 4.311951e-01, 'ffn_w_up': 3.149934e-02, 'ffn_conv': 3.172831e-02, 'ffn_conv_b': 3.441782e-02, 'ffn_w_down': 8.697393e-02, 'ln2_g': 3.204731e+01, 'ln2_b': 1.438956e+00}


def _to_microbatches(a, axis):
    t = _jnp.moveaxis(a, axis, 0)
    t = t.reshape((N_MICROBATCH, t.shape[0] // N_MICROBATCH) + t.shape[1:])
    return _jnp.moveaxis(t, 1, axis + 1)


def setup_inputs(seed: int = 0) -> dict:
    inp = _fwd_setup_inputs(seed)
    key = _jax.random.fold_in(_jax.random.key(seed), 7919)
    shape, _ = _output_shape()
    out = dict(inp)
    out["loss_target"] = _jax.random.normal(_jax.random.fold_in(key, 0), shape, _jnp.float32)
    for i, name in enumerate(TWIN_WEIGHTS):
        w = inp[name].astype(_jnp.float32)
        if MOMENT_SCALE is None:
            s = _jnp.sqrt(_jnp.mean(_jnp.square(w)) + 1e-30)
        else:
            s = MOMENT_SCALE[name]
        km, kv = _jax.random.split(_jax.random.fold_in(key, i + 1))
        out[name] = w
        out["m_" + name] = s * _jax.random.normal(km, w.shape, _jnp.float32)
        out["v_" + name] = (s * s) * _jax.random.uniform(kv, w.shape, _jnp.float32, 0.5, 1.5)
    if N_MICROBATCH > 1:
        for name, axis in PER_EXAMPLE_BATCH_AXIS.items():
            out[name] = _to_microbatches(out[name], axis)
    return {'x': out['x'], 'c': out['c'], 'ln0_g': out['ln0_g'], 'ln0_b': out['ln0_b'], 'w_ada': out['w_ada'], 'b_ada': out['b_ada'], 'w_in': out['w_in'], 'dn_conv': out['dn_conv'], 'dn_a_log': out['dn_a_log'], 'dn_dt_bias': out['dn_dt_bias'], 'dn_norm_g': out['dn_norm_g'], 'gla_w_gate2': out['gla_w_gate2'], 'gla_b_gate': out['gla_b_gate'], 'gla_norm_g': out['gla_norm_g'], 'w_o': out['w_o'], 'ln1_g': out['ln1_g'], 'ln1_b': out['ln1_b'], 'ffn_w_up': out['ffn_w_up'], 'ffn_conv': out['ffn_conv'], 'ffn_conv_b': out['ffn_conv_b'], 'ffn_w_down': out['ffn_w_down'], 'ln2_g': out['ln2_g'], 'ln2_b': out['ln2_b'], 'loss_target': out['loss_target'], 'm_ln0_g': out['m_ln0_g'], 'm_ln0_b': out['m_ln0_b'], 'm_w_ada': out['m_w_ada'], 'm_b_ada': out['m_b_ada'], 'm_w_in': out['m_w_in'], 'm_dn_conv': out['m_dn_conv'], 'm_dn_a_log': out['m_dn_a_log'], 'm_dn_dt_bias': out['m_dn_dt_bias'], 'm_dn_norm_g': out['m_dn_norm_g'], 'm_gla_w_gate2': out['m_gla_w_gate2'], 'm_gla_b_gate': out['m_gla_b_gate'], 'm_gla_norm_g': out['m_gla_norm_g'], 'm_w_o': out['m_w_o'], 'm_ln1_g': out['m_ln1_g'], 'm_ln1_b': out['m_ln1_b'], 'm_ffn_w_up': out['m_ffn_w_up'], 'm_ffn_conv': out['m_ffn_conv'], 'm_ffn_conv_b': out['m_ffn_conv_b'], 'm_ffn_w_down': out['m_ffn_w_down'], 'm_ln2_g': out['m_ln2_g'], 'm_ln2_b': out['m_ln2_b'], 'v_ln0_g': out['v_ln0_g'], 'v_ln0_b': out['v_ln0_b'], 'v_w_ada': out['v_w_ada'], 'v_b_ada': out['v_b_ada'], 'v_w_in': out['v_w_in'], 'v_dn_conv': out['v_dn_conv'], 'v_dn_a_log': out['v_dn_a_log'], 'v_dn_dt_bias': out['v_dn_dt_bias'], 'v_dn_norm_g': out['v_dn_norm_g'], 'v_gla_w_gate2': out['v_gla_w_gate2'], 'v_gla_b_gate': out['v_gla_b_gate'], 'v_gla_norm_g': out['v_gla_norm_g'], 'v_w_o': out['v_w_o'], 'v_ln1_g': out['v_ln1_g'], 'v_ln1_b': out['v_ln1_b'], 'v_ffn_w_up': out['v_ffn_w_up'], 'v_ffn_conv': out['v_ffn_conv'], 'v_ffn_conv_b': out['v_ffn_conv_b'], 'v_ffn_w_down': out['v_ffn_w_down'], 'v_ln2_g': out['v_ln2_g'], 'v_ln2_b': out['v_ln2_b']}


def _loss(weights, diff, rest, loss_target):
    with _jax.named_scope("forward"):
        args = {**rest, TWIN_DIFF_INPUT: diff, **{k: w.astype(_WEIGHT_DTYPES[k]) for k, w in weights.items()}}
        y = _forward(args)
    with _jax.named_scope("loss_head"):
        err = _jnp.square(y.astype(_jnp.float32) - loss_target)
        return 0.5 * _jnp.sum(_jnp.mean(err, axis=-1)) if err.ndim else 0.5 * err


def _adamw(w, g, m, v):
    m = ADAM_B1 * m + (1.0 - ADAM_B1) * g
    v = ADAM_B2 * v + (1.0 - ADAM_B2) * _jnp.square(g)
    m_hat = m / (1.0 - ADAM_B1 ** ADAM_STEP)
    v_hat = v / (1.0 - ADAM_B2 ** ADAM_STEP)
    delta = -ADAM_LR * (m_hat / (_jnp.sqrt(v_hat) + ADAM_EPS) + ADAM_WD * w)
    return delta, m, v


def reference(x, c, ln0_g, ln0_b, w_ada, b_ada, w_in, dn_conv, dn_a_log, dn_dt_bias, dn_norm_g, gla_w_gate2, gla_b_gate, gla_norm_g, w_o, ln1_g, ln1_b, ffn_w_up, ffn_conv, ffn_conv_b, ffn_w_down, ln2_g, ln2_b, loss_target, m_ln0_g, m_ln0_b, m_w_ada, m_b_ada, m_w_in, m_dn_conv, m_dn_a_log, m_dn_dt_bias, m_dn_norm_g, m_gla_w_gate2, m_gla_b_gate, m_gla_norm_g, m_w_o, m_ln1_g, m_ln1_b, m_ffn_w_up, m_ffn_conv, m_ffn_conv_b, m_ffn_w_down, m_ln2_g, m_ln2_b, v_ln0_g, v_ln0_b, v_w_ada, v_b_ada, v_w_in, v_dn_conv, v_dn_a_log, v_dn_dt_bias, v_dn_norm_g, v_gla_w_gate2, v_gla_b_gate, v_gla_norm_g, v_w_o, v_ln1_g, v_ln1_b, v_ffn_w_up, v_ffn_conv, v_ffn_conv_b, v_ffn_w_down, v_ln2_g, v_ln2_b):
    given = dict(x=x, c=c, ln0_g=ln0_g, ln0_b=ln0_b, w_ada=w_ada, b_ada=b_ada, w_in=w_in, dn_conv=dn_conv, dn_a_log=dn_a_log, dn_dt_bias=dn_dt_bias, dn_norm_g=dn_norm_g, gla_w_gate2=gla_w_gate2, gla_b_gate=gla_b_gate, gla_norm_g=gla_norm_g, w_o=w_o, ln1_g=ln1_g, ln1_b=ln1_b, ffn_w_up=ffn_w_up, ffn_conv=ffn_conv, ffn_conv_b=ffn_conv_b, ffn_w_down=ffn_w_down, ln2_g=ln2_g, ln2_b=ln2_b, loss_target=loss_target, m_ln0_g=m_ln0_g, m_ln0_b=m_ln0_b, m_w_ada=m_w_ada, m_b_ada=m_b_ada, m_w_in=m_w_in, m_dn_conv=m_dn_conv, m_dn_a_log=m_dn_a_log, m_dn_dt_bias=m_dn_dt_bias, m_dn_norm_g=m_dn_norm_g, m_gla_w_gate2=m_gla_w_gate2, m_gla_b_gate=m_gla_b_gate, m_gla_norm_g=m_gla_norm_g, m_w_o=m_w_o, m_ln1_g=m_ln1_g, m_ln1_b=m_ln1_b, m_ffn_w_up=m_ffn_w_up, m_ffn_conv=m_ffn_conv, m_ffn_conv_b=m_ffn_conv_b, m_ffn_w_down=m_ffn_w_down, m_ln2_g=m_ln2_g, m_ln2_b=m_ln2_b, v_ln0_g=v_ln0_g, v_ln0_b=v_ln0_b, v_w_ada=v_w_ada, v_b_ada=v_b_ada, v_w_in=v_w_in, v_dn_conv=v_dn_conv, v_dn_a_log=v_dn_a_log, v_dn_dt_bias=v_dn_dt_bias, v_dn_norm_g=v_dn_norm_g, v_gla_w_gate2=v_gla_w_gate2, v_gla_b_gate=v_gla_b_gate, v_gla_norm_g=v_gla_norm_g, v_w_o=v_w_o, v_ln1_g=v_ln1_g, v_ln1_b=v_ln1_b, v_ffn_w_up=v_ffn_w_up, v_ffn_conv=v_ffn_conv, v_ffn_conv_b=v_ffn_conv_b, v_ffn_w_down=v_ffn_w_down, v_ln2_g=v_ln2_g, v_ln2_b=v_ln2_b)
    weights = {n: given[n] for n in TWIN_WEIGHTS}
    shared = {n: given[n] for n in SHARED_INPUTS}
    per_example = {n: given[n] for n in ['x', 'c']}
    grad_fn = _jax.value_and_grad(_loss, argnums=(0, 1))

    def one_microbatch(ex, loss_target):
        ex = dict(ex)
        diff = ex.pop(TWIN_DIFF_INPUT)
        return grad_fn(weights, diff, {**shared, **ex}, loss_target)

    if N_MICROBATCH == 1:
        loss, (grad_w, grad_x) = one_microbatch(per_example, given["loss_target"])
    else:
        def body(carry, xs):
            loss_sum, grad_sum = carry
            l_k, (gw_k, gx_k) = one_microbatch(xs[0], xs[1])
            with _jax.named_scope("update"):
                return (loss_sum + l_k, _jax.tree.map(_jnp.add, grad_sum, gw_k)), gx_k

        init = (_jnp.zeros((), _jnp.float32), _jax.tree.map(_jnp.zeros_like, weights))
        (loss, grad_w), grad_x = _jax.lax.scan(body, init, (per_example, given["loss_target"]))
    with _jax.named_scope("update"):
        delta_w, new_m, new_v = {}, {}, {}
        for n in TWIN_WEIGHTS:
            delta_w[n], new_m[n], new_v[n] = _adamw(weights[n], grad_w[n], given["m_" + n], given["v_" + n])
    return (loss, grad_x, *[grad_w[n] for n in TWIN_WEIGHTS], *[delta_w[n] for n in TWIN_WEIGHTS],
            *[new_m[n] for n in TWIN_WEIGHTS], *[new_v[n] for n in TWIN_WEIGHTS])
```

```python
import functools

import jax
import jax.numpy as jnp
from jax import lax
from jax.experimental import pallas as pl
from jax.experimental.pallas import tpu as pltpu

F32 = jnp.float32
MXU_DT = jnp.bfloat16
HI = lax.Precision.HIGHEST
MESH = pl.DeviceIdType.MESH
N_DEV = 8

D = 1024
HEADS = 4
HD = 128
CHUNK = 64
GLA_KEY = 64
GLA_TAU = 16.0
GATE_RANK = 16
D_FF = 2816
IN_W = 3608
ALPHA = 2.0 ** 0.25
EPS = 1e-6
DN_CONV_K = 4
FFN_CONV_K = 3
HALO = 8

P_QKV, P_Z, P_GQ, P_GK, P_GV, P_GG, P_SM, P_W = 0, 1536, 2048, 2560, 3072, 3584, 4096, 4224
SM_A, SM_B, SM_R = 0, 4, 8

ADAM_LR, ADAM_B1, ADAM_B2, ADAM_EPS, ADAM_WD, ADAM_STEP = 0.001, 0.9, 0.999, 1e-08, 0.01, 10

VMEM_LIMIT_V7X = 56 * 1024 * 1024


def _params(sem=None):
    return pltpu.CompilerParams(dimension_semantics=sem, vmem_limit_bytes=VMEM_LIMIT_V7X)


def _dg(a, b, dims, prec=None):
    return lax.dot_general(a, b, (dims, ((), ())), precision=prec, preferred_element_type=F32)


def _dot(a, b, prec=None):
    return _dg(a, b, ((1,), (0,)), prec)


def _dot_nt(a, b, prec=None):
    return _dg(a, b, ((1,), (1,)), prec)


def _dot_tn(a, b, prec=None):
    return _dg(a, b, ((0,), (0,)), prec)


def _iota(shape, dim):
    return lax.broadcasted_iota(jnp.int32, shape, dim)


def _sigmoid(x):
    return jax.nn.sigmoid(x)


def _silu(x):
    return x * _sigmoid(x)


def _softplus(x):
    return jnp.maximum(x, 0.0) + jnp.log(1.0 + jnp.exp(-jnp.abs(x)))


def _ln_stats(x):
    mu = jnp.mean(x, axis=-1, keepdims=True)
    xc = x - mu
    rstd = lax.rsqrt(jnp.mean(xc * xc, axis=-1, keepdims=True) + EPS)
    return xc * rstd, rstd


def _ln_bwd(dxhat, xhat, rstd):
    return rstd * (dxhat - jnp.mean(dxhat, axis=-1, keepdims=True)
                   - xhat * jnp.mean(dxhat * xhat, axis=-1, keepdims=True))


def _tri_inv_impl(m):
    n = m.shape[0]
    r, c = _iota((n, n), 0), _iota((n, n), 1)
    eye = (r == c).astype(F32)
    d = jnp.where((r >> 3) == (c >> 3), m, 0.0)
    d2 = _dot(d, d, HI)
    d4 = _dot(d2, d2, HI)
    inv = _dot(_dot(eye - d, eye + d2, HI), eye + d4, HI)
    shift = 3
    while (1 << shift) < n:
        rb, cb = r >> shift, c >> shift
        off = jnp.where(((rb & 1) == 1) & (cb == rb - 1), m, 0.0)
        inv = inv - _dot(_dot(inv, off, HI), inv, HI)
        shift += 1
    return inv


@jax.custom_vjp
def _tri_inv(m):
    return _tri_inv_impl(m)


def _tri_inv_fwd(m):
    a = _tri_inv_impl(m)
    return a, a


def _tri_inv_bwd(a, da):
    return (-_dot_nt(_dot_tn(a, da, HI), a, HI),)


_tri_inv.defvjp(_tri_inv_fwd, _tri_inv_bwd)


def _dn_chunk(s_list, q, k, v, gates):
    c = q.shape[0]
    r64, c64 = _iota((c, c), 0), _iota((c, c), 1)
    causal = r64 >= c64
    strict = r64 > c64
    tri = causal.astype(F32)
    e0 = (c64 == 0).astype(F32)
    lane = _iota(gates.shape, 1)
    outs, new_s = [], []
    for h in range(HEADS):
        sl = slice(h * HD, (h + 1) * HD)
        qh, kh, vh, s = q[:, sl], k[:, sl], v[:, sl], s_list[h]
        la = jnp.sum(jnp.where(lane == SM_A + h, gates, 0.0), axis=-1, keepdims=True)
        beta = jnp.sum(jnp.where(lane == SM_B + h, gates, 0.0), axis=-1, keepdims=True)
        g_b = _dot(tri, jnp.broadcast_to(la, (c, HD)), HI)
        g_col = _dot(tri, jnp.broadcast_to(la, (c, c)), HI)
        g_row = _dot_nt(e0, g_col, HI)
        g_last = jnp.sum(jnp.broadcast_to(la, (c, HD)), axis=0, keepdims=True)
        decay = jnp.where(causal, jnp.exp(jnp.where(causal, g_col - g_row, 0.0)), 0.0)
        kb = kh * beta
        m_low = jnp.where(strict, _dot_nt(kb, kh) * decay, 0.0)
        a_inv = _tri_inv(m_low)
        u = _dot(a_inv, vh * beta, HI)
        w = _dot(a_inv, kb * jnp.exp(g_b), HI)
        attn = _dot_nt(qh, kh) * decay
        v_new = u - _dot(w, s)
        o = _dot(qh * jnp.exp(g_b), s) + _dot(attn, v_new)
        k_dec = kh * jnp.exp(g_last - g_b)
        new_s.append(s * jnp.exp(g_last) + _dot_tn(k_dec, v_new))
        outs.append(o)
    return jnp.concatenate(outs, axis=-1), new_s


def _gla_chunk(st_list, q, k, v, small, w2, bg):
    c = q.shape[0]
    r64, c64 = _iota((c, c), 0), _iota((c, c), 1)
    causal = r64 >= c64
    tri = causal.astype(F32)
    la_all = -_softplus(-(_dot(small, w2) + bg)) * (1.0 / GLA_TAU)
    outs, new_s = [], []
    for h in range(HEADS):
        sl = slice(h * HD, (h + 1) * HD)
        qh, kh, vh, st, la = q[:, sl], k[:, sl], v[:, sl], st_list[h], la_all[:, sl]
        b = _dot(tri, la, HI)
        b_last = jnp.sum(la, axis=0, keepdims=True)
        q_dec = qh * (GLA_KEY ** -0.5) * jnp.exp(b)
        attn = jnp.where(causal, _dot_nt(q_dec, kh * jnp.exp(-b)), 0.0)
        o = _dot_nt(q_dec, st) + _dot(attn, vh)
        k_dec = kh * jnp.exp(b_last - b)
        new_s.append(st * jnp.exp(b_last) + _dot_tn(vh, k_dec))
        outs.append(o)
    return jnp.concatenate(outs, axis=-1), new_s


def _dn_qkv(y):
    act = _silu(y)
    parts = []
    for i in range(2 * HEADS):
        xh = act[:, i * HD:(i + 1) * HD]
        xh = xh * lax.rsqrt(jnp.sum(xh * xh, axis=-1, keepdims=True) + EPS)
        parts.append(xh * (HD ** -0.5) if i < HEADS else xh)
    qk = jnp.concatenate(parts, axis=-1)
    return qk[:, :HEADS * HD], qk[:, HEADS * HD:], act[:, 2 * HEADS * HD:]


def _dn_gates(small, alog_row, dt_row):
    lane = _iota(small.shape, 1)
    log_a = -jnp.exp(alog_row) * _softplus(small + dt_row)
    return jnp.where(lane < SM_B, log_a, jnp.where(lane < SM_R, _sigmoid(small), 0.0))


def _gate_norm(o, z, grow):
    parts = []
    for h in range(HEADS):
        oh = o[:, h * HD:(h + 1) * HD]
        parts.append(oh * lax.rsqrt(jnp.mean(oh * oh, axis=-1, keepdims=True) + EPS))
    return jnp.concatenate(parts, axis=-1) * grow * _silu(z)


def _conv_rows(xrows, w_ref, k_taps):
    n = xrows.shape[0]
    acc = xrows * w_ref[k_taps - 1:k_taps, :]
    for s in range(1, k_taps):
        acc = acc + pltpu.roll(xrows, s, 0) * w_ref[k_taps - 1 - s:k_taps - s, :]
    return acc


def _shift_up(x, s):
    return x if s == 0 else pltpu.roll(x, x.shape[0] - s, 0)


def _div_tile(n, cap, mult=8):
    best = None
    for t in range(mult, min(n, cap) + 1, mult):
        if n % t == 0:
            best = t
    return best if best is not None else n


def _halo_prev(tt):
    return lambda b, t: (b, jnp.maximum(t * (tt // HALO) - 1, 0))


def _halo_next(tt, t_total):
    return lambda b, t: (b, jnp.minimum((t + 1) * (tt // HALO), t_total // HALO - 1))


def _mm(a, b, mode, out_dtype, name, tm=512, tn=512, tk=None):
    if mode == "nn":
        (m, k), n = a.shape, b.shape[1]
    elif mode == "nt":
        (m, k), n = a.shape, b.shape[0]
    else:
        (k, m), n = a.shape, b.shape[1]
    tm, tn = min(tm, m), min(tn, n)
    tk = k if tk is None else min(tk, k)
    assert m % tm == 0 and n % tn == 0 and k % tk == 0, (name, a.shape, b.shape, tm, tn, tk)
    nk = k // tk
    if mode == "tn":
        a_spec = pl.BlockSpec((tk, tm), lambda i, j, kk: (kk, i))
    else:
        a_spec = pl.BlockSpec((tm, tk), lambda i, j, kk: (i, kk))
    if mode == "nt":
        b_spec = pl.BlockSpec((tn, tk), lambda i, j, kk: (j, kk))
    else:
        b_spec = pl.BlockSpec((tk, tn), lambda i, j, kk: (kk, j))
    dims = {"nn": ((1,), (0,)), "nt": ((1,), (1,)), "tn": ((0,), (0,))}[mode]

    def body(a_ref, b_ref, o_ref, *acc):
        p = _dg(a_ref[...], b_ref[...], dims)
        if nk == 1:
            o_ref[...] = p.astype(out_dtype)
        else:
            kk = pl.program_id(2)

            @pl.when(kk == 0)
            def _():
                acc[0][...] = p

            @pl.when(kk > 0)
            def _():
                acc[0][...] += p

            @pl.when(kk == nk - 1)
            def _():
                o_ref[...] = acc[0][...].astype(out_dtype)

    return pl.pallas_call(
        body, name=name, grid=(m // tm, n // tn, nk),
        in_specs=[a_spec, b_spec],
        out_specs=pl.BlockSpec((tm, tn), lambda i, j, kk: (i, j)),
        out_shape=jax.ShapeDtypeStruct((m, n), out_dtype),
        scratch_shapes=[pltpu.VMEM((tm, tn), F32)] if nk > 1 else [],
        compiler_params=_params(("parallel", "parallel", "arbitrary")),
    )(a, b)


def _ada_fwd(c_all, w_ada, b_cols):
    def body(c_ref, w_ref, b_ref, o_ref):
        cond = _silu(c_ref[...]).astype(MXU_DT)
        o_ref[...] = _dot(cond, w_ref[...].astype(MXU_DT)) + b_ref[...]

    return pl.pallas_call(body, name="ada_fwd", out_shape=jax.ShapeDtypeStruct((c_all.shape[0], w_ada.shape[1]), F32),
                          compiler_params=_params())(c_all, w_ada, b_cols)


def _ada_bwd(c_all, dmod_all, dmod_cols):
    def body(c_ref, da_ref, dc_ref, gw_ref, gb_ref):
        cond = _silu(c_ref[...]).astype(MXU_DT)
        gw_ref[...] = _dot_tn(cond, dc_ref[...].astype(MXU_DT))
        gb_ref[...] = jnp.sum(da_ref[...], axis=0, keepdims=True)

    return pl.pallas_call(
        body, name="ada_bwd",
        out_shape=(jax.ShapeDtypeStruct((c_all.shape[1], dmod_cols.shape[1]), F32),
                   jax.ShapeDtypeStruct((1, dmod_all.shape[1]), F32)),
        compiler_params=_params())(c_all, dmod_all, dmod_cols)


def _tok_spec(tt, width=D):
    return pl.BlockSpec((1, tt, width), lambda b, t: (b, t, 0))


def _vec_spec(width=D):
    return pl.BlockSpec((1, width), lambda b, t: (0, 0))


def _bvec_spec(width=D):
    return pl.BlockSpec((1, 1, width), lambda b, t: (b, 0, 0))


def _ln0_mod(x, g0, b0, sc, sh):
    bsz, t_total, _ = x.shape
    tt = _div_tile(t_total, 256)

    def body(x_ref, g_ref, b_ref, sc_ref, sh_ref, h_ref):
        xh, _ = _ln_stats(x_ref[0])
        x0 = xh * g_ref[...] + b_ref[...]
        h_ref[0] = (x0 * (1.0 + sc_ref[0]) + sh_ref[0]).astype(MXU_DT)

    return pl.pallas_call(
        body, name="ln0_mod", grid=(bsz, t_total // tt),
        in_specs=[_tok_spec(tt), _vec_spec(), _vec_spec(), _bvec_spec(), _bvec_spec()],
        out_specs=_tok_spec(tt), out_shape=jax.ShapeDtypeStruct(x.shape, MXU_DT),
        compiler_params=_params(("parallel", "parallel")))(x, g0, b0, sc, sh)


def _res_ln_mod(x, y, gt, g0, b0, g1, b1, sc, sh):
    bsz, t_total, _ = x.shape
    tt = _div_tile(t_total, 256)

    def body(x_ref, y_ref, gt_ref, g0_ref, b0_ref, g1_ref, b1_ref, sc_ref, sh_ref, r_ref, h_ref):
        xh, _ = _ln_stats(x_ref[0])
        r = ALPHA * (xh * g0_ref[...] + b0_ref[...]) + (1.0 + gt_ref[0]) * y_ref[0]
        r_ref[0] = r
        rh, _ = _ln_stats(r)
        x1 = rh * g1_ref[...] + b1_ref[...]
        h_ref[0] = (x1 * (1.0 + sc_ref[0]) + sh_ref[0]).astype(MXU_DT)

    return pl.pallas_call(
        body, name="res_ln_mod", grid=(bsz, t_total // tt),
        in_specs=[_tok_spec(tt), _tok_spec(tt), _bvec_spec(), _vec_spec(), _vec_spec(), _vec_spec(), _vec_spec(),
                  _bvec_spec(), _bvec_spec()],
        out_specs=(_tok_spec(tt), _tok_spec(tt)),
        out_shape=(jax.ShapeDtypeStruct(x.shape, F32), jax.ShapeDtypeStruct(x.shape, MXU_DT)),
        compiler_params=_params(("parallel", "parallel")))(x, y, gt, g0, b0, g1, b1, sc, sh)


def _final_fwd_bwd(r1, y2, gt, g1, b1, g2, b2, target):
    bsz, t_total, _ = r1.shape
    tt = _div_tile(t_total, 256)

    def body(r1_ref, y2_ref, gt_ref, g1_ref, b1_ref, g2_ref, b2_ref, tg_ref,
             loss_ref, dr2_ref, dy2_ref, dgt_ref, dg2_ref, db2_ref):
        b, t = pl.program_id(0), pl.program_id(1)

        @pl.when((b == 0) & (t == 0))
        def _():
            loss_ref[...] = jnp.zeros_like(loss_ref)
            dg2_ref[...] = jnp.zeros_like(dg2_ref)
            db2_ref[...] = jnp.zeros_like(db2_ref)

        @pl.when(t == 0)
        def _():
            dgt_ref[...] = jnp.zeros_like(dgt_ref)

        rh1, _ = _ln_stats(r1_ref[0])
        x1 = rh1 * g1_ref[...] + b1_ref[...]
        y2 = y2_ref[0]
        gate = 1.0 + gt_ref[0]
        xh2, rstd2 = _ln_stats(ALPHA * x1 + gate * y2)
        err = xh2 * g2_ref[...] + b2_ref[...] - tg_ref[0]
        loss_ref[...] += jnp.sum(err * err, axis=0, keepdims=True)
        dx2 = err * (1.0 / D)
        dg2_ref[...] += jnp.sum(dx2 * xh2, axis=0, keepdims=True)
        db2_ref[...] += jnp.sum(dx2, axis=0, keepdims=True)
        dr2 = _ln_bwd(dx2 * g2_ref[...], xh2, rstd2)
        dr2_ref[0] = dr2
        dy2_ref[0] = (gate * dr2).astype(MXU_DT)
        dgt_ref[0] += jnp.sum(dr2 * y2, axis=0, keepdims=True)

    vec_out = jax.ShapeDtypeStruct((1, D), F32)
    return pl.pallas_call(
        body, name="final_fwd_bwd", grid=(bsz, t_total // tt),
        in_specs=[_tok_spec(tt), _tok_spec(tt), _bvec_spec(), _vec_spec(), _vec_spec(), _vec_spec(), _vec_spec(),
                  _tok_spec(tt)],
        out_specs=(_vec_spec(), _tok_spec(tt), _tok_spec(tt), _bvec_spec(), _vec_spec(), _vec_spec()),
        out_shape=(vec_out, jax.ShapeDtypeStruct(r1.shape, F32), jax.ShapeDtypeStruct(r1.shape, MXU_DT),
                   jax.ShapeDtypeStruct((bsz, 1, D), F32), vec_out, vec_out),
        compiler_params=_params(("arbitrary", "arbitrary")))(r1, y2, gt, g1, b1, g2, b2, target)


def _ln_bwd_call(name, d_res, d_h, src, g, b, sc, y=None, gt=None):
    bsz, t_total, _ = src.shape
    tt = _div_tile(t_total, 256)
    has_y = y is not None

    def body(*refs):
        if has_y:
            (dres_ref, dh_ref, src_ref, g_ref, b_ref, sc_ref, y_ref, gt_ref,
             dsrc_ref, dsc_ref, dsh_ref, dg_ref, db_ref, dy_ref, dgt_ref) = refs
        else:
            (dres_ref, dh_ref, src_ref, g_ref, b_ref, sc_ref,
             dsrc_ref, dsc_ref, dsh_ref, dg_ref, db_ref) = refs
        bi, t = pl.program_id(0), pl.program_id(1)

        @pl.when((bi == 0) & (t == 0))
        def _():
            dg_ref[...] = jnp.zeros_like(dg_ref)
            db_ref[...] = jnp.zeros_like(db_ref)

        @pl.when(t == 0)
        def _():
            dsc_ref[...] = jnp.zeros_like(dsc_ref)
            dsh_ref[...] = jnp.zeros_like(dsh_ref)
            if has_y:
                dgt_ref[...] = jnp.zeros_like(dgt_ref)

        xh, rstd = _ln_stats(src_ref[0])
        xv = xh * g_ref[...] + b_ref[...]
        dh = dh_ref[0]
        dx = ALPHA * dres_ref[0] + dh * (1.0 + sc_ref[0])
        dsc_ref[0] += jnp.sum(dh * xv, axis=0, keepdims=True)
        dsh_ref[0] += jnp.sum(dh, axis=0, keepdims=True)
        dg_ref[...] += jnp.sum(dx * xh, axis=0, keepdims=True)
        db_ref[...] += jnp.sum(dx, axis=0, keepdims=True)
        dsrc = _ln_bwd(dx * g_ref[...], xh, rstd)
        dsrc_ref[0] = dsrc
        if has_y:
            dy_ref[0] = ((1.0 + gt_ref[0]) * dsrc).astype(MXU_DT)
            dgt_ref[0] += jnp.sum(dsrc * y_ref[0], axis=0, keepdims=True)

    vec_out = jax.ShapeDtypeStruct((1, D), F32)
    bvec_out = jax.ShapeDtypeStruct((bsz, 1, D), F32)
    in_specs = [_tok_spec(tt), _tok_spec(tt), _tok_spec(tt), _vec_spec(), _vec_spec(), _bvec_spec()]
    out_specs = [_tok_spec(tt), _bvec_spec(), _bvec_spec(), _vec_spec(), _vec_spec()]
    out_shape = [jax.ShapeDtypeStruct(src.shape, F32), bvec_out, bvec_out, vec_out, vec_out]
    args = [d_res, d_h, src, g, b, sc]
    if has_y:
        in_specs += [_tok_spec(tt), _bvec_spec()]
        out_specs += [_tok_spec(tt), _bvec_spec()]
        out_shape += [jax.ShapeDtypeStruct(src.shape, MXU_DT), bvec_out]
        args += [y, gt]
    return pl.pallas_call(body, name=name, grid=(bsz, t_total // tt), in_specs=in_specs, out_specs=tuple(out_specs),
                          out_shape=tuple(out_shape), compiler_params=_params(("arbitrary", "arbitrary")))(*args)


FFN_TC = 256
FFN_NJ = D_FF // FFN_TC


def _ffn_act_fwd(up, cw, cb):
    bsz, t_total, _ = up.shape
    tt = _div_tile(t_total, 256)
    hp = _halo_prev(tt)

    def body(g_ref, gp_ref, v_ref, vp_ref, wg_ref, wv_ref, bg_ref, bv_ref, o_ref):
        first = pl.program_id(1) == 0

        def conv(prev_ref, tile_ref, w_ref, b_ref):
            prev = jnp.where(first, 0.0, prev_ref[0])
            rows = jnp.concatenate([prev, tile_ref[0]], axis=0)
            return _conv_rows(rows, w_ref, FFN_CONV_K)[HALO:] + b_ref[...]

        o_ref[0] = (_silu(conv(gp_ref, g_ref, wg_ref, bg_ref)) * conv(vp_ref, v_ref, wv_ref, bv_ref)).astype(MXU_DT)

    def tile(off):
        return pl.BlockSpec((1, tt, FFN_TC), lambda b, t, j: (b, t, j + off))

    def halo(off):
        return pl.BlockSpec((1, HALO, FFN_TC), lambda b, t, j: (*hp(b, t), j + off))

    def wspec(rows, off):
        return pl.BlockSpec((rows, FFN_TC), lambda b, t, j: (0, j + off))

    return pl.pallas_call(
        body, name="ffn_act_fwd", grid=(bsz, t_total // tt, FFN_NJ),
        in_specs=[tile(0), halo(0), tile(FFN_NJ), halo(FFN_NJ), wspec(FFN_CONV_K, 0), wspec(FFN_CONV_K, FFN_NJ),
                  wspec(1, 0), wspec(1, FFN_NJ)],
        out_specs=pl.BlockSpec((1, tt, FFN_TC), lambda b, t, j: (b, t, j)),
        out_shape=jax.ShapeDtypeStruct((bsz, t_total, D_FF), MXU_DT),
        compiler_params=_params(("parallel", "parallel", "parallel")))(up, up, up, up, cw, cw, cb, cb)


def _ffn_act_bwd(up, da, cw, cb):
    bsz, t_total, width = up.shape
    tt = _div_tile(t_total, 256)
    nt = t_total // tt
    hp, hn = _halo_prev(tt), _halo_next(tt, t_total)
    nj2 = 2 * FFN_NJ

    def body(xo_ref, xop_ref, xon_ref, xp_ref, xpp_ref, xpn_ref, da_ref, dan_ref, wo_ref, wp_ref, bo_ref, bp_ref,
             dup_ref, dw_ref, db_ref):
        j, b, t = pl.program_id(0), pl.program_id(1), pl.program_id(2)

        @pl.when((b == 0) & (t == 0))
        def _():
            dw_ref[...] = jnp.zeros_like(dw_ref)
            db_ref[...] = jnp.zeros_like(db_ref)

        def rows_of(prev_ref, tile_ref, next_ref):
            prev = jnp.where(t == 0, 0.0, prev_ref[0])
            return jnp.concatenate([prev, tile_ref[0], next_ref[0]], axis=0)

        x_own = rows_of(xop_ref, xo_ref, xon_ref)
        u_own = _conv_rows(x_own, wo_ref, FFN_CONV_K)[HALO:] + bo_ref[...]
        u_par = _conv_rows(rows_of(xpp_ref, xp_ref, xpn_ref), wp_ref, FFN_CONV_K)[HALO:] + bp_ref[...]
        da_ext = jnp.concatenate([da_ref[0], dan_ref[0]], axis=0)
        is_gate = j < FFN_NJ
        g_pre = jnp.where(is_gate, u_own, u_par)
        v_pre = jnp.where(is_gate, u_par, u_own)
        sg = _sigmoid(g_pre)
        du = da_ext * jnp.where(is_gate, v_pre * sg * (1.0 + g_pre * (1.0 - sg)), g_pre * sg)
        valid = (_iota(du.shape, 0) < tt) | (t < nt - 1)
        du = jnp.where(valid, du, 0.0)
        dup = du * wo_ref[FFN_CONV_K - 1:FFN_CONV_K, :]
        for s in range(1, FFN_CONV_K):
            dup = dup + _shift_up(du, s) * wo_ref[FFN_CONV_K - 1 - s:FFN_CONV_K - s, :]
        dup_ref[0] = dup[:tt].astype(MXU_DT)
        du_t = du[:tt]
        db_ref[...] += jnp.sum(du_t, axis=0, keepdims=True)
        for k in range(FFN_CONV_K):
            s = FFN_CONV_K - 1 - k
            xs = (x_own if s == 0 else pltpu.roll(x_own, s, 0))[HALO:HALO + tt]
            dw_ref[k:k + 1, :] += jnp.sum(du_t * xs, axis=0, keepdims=True)

    def partner(j):
        return (j + FFN_NJ) % nj2

    def tile(fn):
        return pl.BlockSpec((1, tt, FFN_TC), lambda j, b, t: (b, t, fn(j)))

    def halo(h, fn):
        return pl.BlockSpec((1, HALO, FFN_TC), lambda j, b, t: (*h(b, t), fn(j)))

    def wspec(rows, fn):
        return pl.BlockSpec((rows, FFN_TC), lambda j, b, t: (0, fn(j)))

    own = lambda j: j
    half = lambda j: j % FFN_NJ
    return pl.pallas_call(
        body, name="ffn_act_bwd", grid=(nj2, bsz, nt),
        in_specs=[tile(own), halo(hp, own), halo(hn, own), tile(partner), halo(hp, partner), halo(hn, partner),
                  tile(half), halo(hn, half), wspec(FFN_CONV_K, own), wspec(FFN_CONV_K, partner),
                  wspec(1, own), wspec(1, partner)],
        out_specs=(tile(own), wspec(FFN_CONV_K, own), wspec(1, own)),
        out_shape=(jax.ShapeDtypeStruct(up.shape, MXU_DT), jax.ShapeDtypeStruct((FFN_CONV_K, width), F32),
                   jax.ShapeDtypeStruct((1, width), F32)),
        compiler_params=_params(("arbitrary", "arbitrary", "arbitrary")))(
            up, up, up, up, up, up, da, da, cw, cw, cb, cb)


QKV_W = 3 * HEADS * HD
SM_BLK = P_SM // 128


def _dn_pre_fwd(proj, conv_w, alog_row, dt_row):
    bsz, t_total, _ = proj.shape
    tt = _div_tile(t_total, 256)
    hp = _halo_prev(tt)

    def body(x_ref, xp_ref, sm_ref, w_ref, al_ref, dt_ref, q_ref, k_ref, v_ref, g_ref):
        prev = jnp.where(pl.program_id(1) == 0, 0.0, xp_ref[0])
        y = _conv_rows(jnp.concatenate([prev, x_ref[0]], axis=0), w_ref, DN_CONV_K)[HALO:]
        q_ref[0], k_ref[0], v_ref[0] = _dn_qkv(y)
        g_ref[0] = _dn_gates(sm_ref[0], al_ref[...], dt_ref[...])

    out512 = jax.ShapeDtypeStruct((bsz, t_total, HEADS * HD), F32)
    return pl.pallas_call(
        body, name="dn_pre_fwd", grid=(bsz, t_total // tt),
        in_specs=[pl.BlockSpec((1, tt, QKV_W), lambda b, t: (b, t, 0)),
                  pl.BlockSpec((1, HALO, QKV_W), lambda b, t: (*hp(b, t), 0)),
                  pl.BlockSpec((1, tt, 128), lambda b, t: (b, t, SM_BLK)),
                  pl.BlockSpec((DN_CONV_K, QKV_W), lambda b, t: (0, 0)), _vec_spec(128), _vec_spec(128)],
        out_specs=(_tok_spec(tt, 512), _tok_spec(tt, 512), _tok_spec(tt, 512), _tok_spec(tt, 128)),
        out_shape=(out512, out512, out512, jax.ShapeDtypeStruct((bsz, t_total, 128), F32)),
        compiler_params=_params(("parallel", "parallel")))(proj, proj, proj, conv_w, alog_row, dt_row)


def _dn_pre_bwd(proj, dq, dk, dv, dgates, conv_w, alog_row, dt_row):
    bsz, t_total, _ = proj.shape
    tt = _div_tile(t_total, 128)
    nt = t_total // tt
    hp, hn = _halo_prev(tt), _halo_next(tt, t_total)

    def body(x_ref, xp_ref, xn_ref, sm_ref, dq_ref, dqn_ref, dk_ref, dkn_ref, dv_ref, dvn_ref, dg_ref,
             w_ref, al_ref, dt_ref, dx_ref, dsm_ref, dw_ref, dal_ref, ddt_ref):
        b, t = pl.program_id(0), pl.program_id(1)

        @pl.when((b == 0) & (t == 0))
        def _():
            dw_ref[...] = jnp.zeros_like(dw_ref)
            dal_ref[...] = jnp.zeros_like(dal_ref)
            ddt_ref[...] = jnp.zeros_like(ddt_ref)

        prev = jnp.where(t == 0, 0.0, xp_ref[0])
        rows = jnp.concatenate([prev, x_ref[0], xn_ref[0]], axis=0)
        y = _conv_rows(rows, w_ref, DN_CONV_K)[HALO:]
        valid = (_iota((tt + HALO, 1), 0) < tt) | (t < nt - 1)

        def ext(tile_ref, next_ref):
            return jnp.where(valid, jnp.concatenate([tile_ref[0], next_ref[0]], axis=0), 0.0)

        _, vjp_qkv = jax.vjp(_dn_qkv, y)
        (dy,) = vjp_qkv((ext(dq_ref, dqn_ref), ext(dk_ref, dkn_ref), ext(dv_ref, dvn_ref)))
        dy = jnp.where(valid, dy, 0.0)
        dx = dy * w_ref[DN_CONV_K - 1:DN_CONV_K, :]
        for s in range(1, DN_CONV_K):
            dx = dx + _shift_up(dy, s) * w_ref[DN_CONV_K - 1 - s:DN_CONV_K - s, :]
        dx_ref[0] = dx[:tt].astype(MXU_DT)
        dy_t = dy[:tt]
        for k in range(DN_CONV_K):
            s = DN_CONV_K - 1 - k
            xs = (rows if s == 0 else pltpu.roll(rows, s, 0))[HALO:HALO + tt]
            dw_ref[k:k + 1, :] += jnp.sum(dy_t * xs, axis=0, keepdims=True)
        _, vjp_g = jax.vjp(_dn_gates, sm_ref[0], al_ref[...], dt_ref[...])
        dsm, dal, ddt = vjp_g(dg_ref[0])
        dsm_ref[0] = dsm
        dal_ref[...] += dal
        ddt_ref[...] += ddt

    def tile(width, blk=0):
        return pl.BlockSpec((1, tt, width), lambda b, t: (b, t, blk))

    def halo(h, width):
        return pl.BlockSpec((1, HALO, width), lambda b, t: (*h(b, t), 0))

    return pl.pallas_call(
        body, name="dn_pre_bwd", grid=(bsz, nt),
        in_specs=[tile(QKV_W), halo(hp, QKV_W), halo(hn, QKV_W), tile(128, SM_BLK),
                  tile(512), halo(hn, 512), tile(512), halo(hn, 512), tile(512), halo(hn, 512), tile(128),
                  pl.BlockSpec((DN_CONV_K, QKV_W), lambda b, t: (0, 0)), _vec_spec(128), _vec_spec(128)],
        out_specs=(tile(QKV_W), tile(128), pl.BlockSpec((DN_CONV_K, QKV_W), lambda b, t: (0, 0)),
                   _vec_spec(128), _vec_spec(128)),
        out_shape=(jax.ShapeDtypeStruct((bsz, t_total, QKV_W), MXU_DT), jax.ShapeDtypeStruct((bsz, t_total, 128), F32),
                   jax.ShapeDtypeStruct((DN_CONV_K, QKV_W), F32), jax.ShapeDtypeStruct((1, 128), F32),
                   jax.ShapeDtypeStruct((1, 128), F32)),
        compiler_params=_params(("arbitrary", "arbitrary")))(
            proj, proj, proj, proj, dq, dq, dk, dk, dv, dv, dgates, conv_w, alog_row, dt_row)


def _state_spec(idx):
    return pl.BlockSpec((1, 1, HEADS, HD, HD), lambda b, c: (b, idx(c), 0, 0, 0))


def _dn_rec_fwd(q, k, v, gates):
    bsz, t_total, _ = q.shape
    nc = t_total // CHUNK

    def body(q_ref, k_ref, v_ref, g_ref, o_ref, ss_ref, s_ref):
        @pl.when(pl.program_id(1) == 0)
        def _():
            s_ref[...] = jnp.zeros_like(s_ref)

        s_list = [s_ref[h] for h in range(HEADS)]
        for h in range(HEADS):
            ss_ref[0, 0, h] = s_list[h]
        o, new_s = _dn_chunk(s_list, q_ref[0], k_ref[0], v_ref[0], g_ref[0])
        o_ref[0] = o
        for h in range(HEADS):
            s_ref[h] = new_s[h]

    return pl.pallas_call(
        body, name="dn_rec_fwd", grid=(bsz, nc),
        in_specs=[_tok_spec(CHUNK, 512)] * 3 + [_tok_spec(CHUNK, 128)],
        out_specs=(_tok_spec(CHUNK, 512), _state_spec(lambda c: c)),
        out_shape=(jax.ShapeDtypeStruct(q.shape, F32), jax.ShapeDtypeStruct((bsz, nc, HEADS, HD, HD), F32)),
        scratch_shapes=[pltpu.VMEM((HEADS, HD, HD), F32)],
        compiler_params=_params(("arbitrary", "arbitrary")))(q, k, v, gates)


def _dn_rec_bwd(q, k, v, gates, states, do):
    bsz, t_total, _ = q.shape
    nc = t_total // CHUNK
    rev = lambda c: nc - 1 - c

    def body(q_ref, k_ref, v_ref, g_ref, ss_ref, do_ref, dq_ref, dk_ref, dv_ref, dg_ref, ds_ref):
        @pl.when(pl.program_id(1) == 0)
        def _():
            ds_ref[...] = jnp.zeros_like(ds_ref)

        s_list = [ss_ref[0, 0, h] for h in range(HEADS)]
        _, vjp = jax.vjp(_dn_chunk, s_list, q_ref[0], k_ref[0], v_ref[0], g_ref[0])
        ds_in, dq, dk, dv, dg = vjp((do_ref[0], [ds_ref[h] for h in range(HEADS)]))
        dq_ref[0], dk_ref[0], dv_ref[0], dg_ref[0] = dq, dk, dv, dg
        for h in range(HEADS):
            ds_ref[h] = ds_in[h]

    def tok(width):
        return pl.BlockSpec((1, CHUNK, width), lambda b, c: (b, rev(c), 0))

    out512 = jax.ShapeDtypeStruct(q.shape, F32)
    return pl.pallas_call(
        body, name="dn_rec_bwd", grid=(bsz, nc),
        in_specs=[tok(512), tok(512), tok(512), tok(128), _state_spec(rev), tok(512)],
        out_specs=(tok(512), tok(512), tok(512), tok(128)),
        out_shape=(out512, out512, out512, jax.ShapeDtypeStruct(gates.shape, F32)),
        scratch_shapes=[pltpu.VMEM((HEADS, HD, HD), F32)],
        compiler_params=_params(("arbitrary", "arbitrary")))(q, k, v, gates, states, do)


GQ_BLK, GK_BLK, GV_BLK = P_GQ // 512, P_GK // 512, P_GV // 512


def _gla_rec_fwd(proj, w2, bg):
    bsz, t_total, _ = proj.shape
    nc = t_total // CHUNK

    def body(q_ref, k_ref, v_ref, sm_ref, w2_ref, bg_ref, o_ref, ss_ref, s_ref):
        @pl.when(pl.program_id(1) == 0)
        def _():
            s_ref[...] = jnp.zeros_like(s_ref)

        s_list = [s_ref[h] for h in range(HEADS)]
        for h in range(HEADS):
            ss_ref[0, 0, h] = s_list[h]
        o, new_s = _gla_chunk(s_list, q_ref[0], k_ref[0], v_ref[0], sm_ref[0], w2_ref[...], bg_ref[...])
        o_ref[0] = o
        for h in range(HEADS):
            s_ref[h] = new_s[h]

    def col(blk, width=512):
        return pl.BlockSpec((1, CHUNK, width), lambda b, c: (b, c, blk))

    return pl.pallas_call(
        body, name="gla_rec_fwd", grid=(bsz, nc),
        in_specs=[col(GQ_BLK), col(GK_BLK), col(GV_BLK), col(SM_BLK, 128),
                  pl.BlockSpec((128, 512), lambda b, c: (0, 0)), _vec_spec(512)],
        out_specs=(_tok_spec(CHUNK, 512), _state_spec(lambda c: c)),
        out_shape=(jax.ShapeDtypeStruct((bsz, t_total, 512), F32),
                   jax.ShapeDtypeStruct((bsz, nc, HEADS, HD, HD), F32)),
        scratch_shapes=[pltpu.VMEM((HEADS, HD, HD), F32)],
        compiler_params=_params(("arbitrary", "arbitrary")))(proj, proj, proj, proj, w2, bg)


def _gla_rec_bwd(proj, w2, bg, states, do, dsm_dn):
    bsz, t_total, _ = proj.shape
    nc = t_total // CHUNK
    rev = lambda c: nc - 1 - c

    def body(q_ref, k_ref, v_ref, sm_ref, w2_ref, bg_ref, ss_ref, do_ref, dsd_ref,
             dq_ref, dk_ref, dv_ref, dsm_ref, dw2_ref, dbg_ref, ds_ref):
        b, c = pl.program_id(0), pl.program_id(1)

        @pl.when((b == 0) & (c == 0))
        def _():
            dw2_ref[...] = jnp.zeros_like(dw2_ref)
            dbg_ref[...] = jnp.zeros_like(dbg_ref)

        @pl.when(c == 0)
        def _():
            ds_ref[...] = jnp.zeros_like(ds_ref)

        s_list = [ss_ref[0, 0, h] for h in range(HEADS)]
        _, vjp = jax.vjp(_gla_chunk, s_list, q_ref[0], k_ref[0], v_ref[0], sm_ref[0], w2_ref[...], bg_ref[...])
        ds_in, dq, dk, dv, dsm, dw2, dbg = vjp((do_ref[0], [ds_ref[h] for h in range(HEADS)]))
        dq_ref[0], dk_ref[0], dv_ref[0] = dq.astype(MXU_DT), dk.astype(MXU_DT), dv.astype(MXU_DT)
        dsm_ref[0] = (dsm + dsd_ref[0]).astype(MXU_DT)
        dw2_ref[...] += dw2
        dbg_ref[...] += dbg
        for h in range(HEADS):
            ds_ref[h] = ds_in[h]

    def col(blk, width=512):
        return pl.BlockSpec((1, CHUNK, width), lambda b, c: (b, rev(c), blk))

    out512 = jax.ShapeDtypeStruct((bsz, t_total, 512), MXU_DT)
    return pl.pallas_call(
        body, name="gla_rec_bwd", grid=(bsz, nc),
        in_specs=[col(GQ_BLK), col(GK_BLK), col(GV_BLK), col(SM_BLK, 128),
                  pl.BlockSpec((128, 512), lambda b, c: (0, 0)), _vec_spec(512), _state_spec(rev), col(0), col(0, 128)],
        out_specs=(col(0), col(0), col(0), col(0, 128), pl.BlockSpec((128, 512), lambda b, c: (0, 0)), _vec_spec(512)),
        out_shape=(out512, out512, out512, jax.ShapeDtypeStruct((bsz, t_total, 128), MXU_DT),
                   jax.ShapeDtypeStruct((128, 512), F32), jax.ShapeDtypeStruct((1, 512), F32)),
        scratch_shapes=[pltpu.VMEM((HEADS, HD, HD), F32)],
        compiler_params=_params(("arbitrary", "arbitrary")))(proj, proj, proj, proj, w2, bg, states, do, dsm_dn)


Z_BLK, GG_BLK = P_Z // 512, P_GG // 512


def _mix_out_fwd(o_dn, o_gla, proj, grow_dn, grow_gla):
    bsz, t_total, _ = o_dn.shape
    tt = _div_tile(t_total, 256)

    def body(od_ref, og_ref, z_ref, gg_ref, gd_ref, gl_ref, o_ref):
        o_ref[0, :, :512] = _gate_norm(od_ref[0], z_ref[0], gd_ref[...]).astype(MXU_DT)
        o_ref[0, :, 512:] = _gate_norm(og_ref[0], gg_ref[0], gl_ref[...]).astype(MXU_DT)

    def col(blk):
        return pl.BlockSpec((1, tt, 512), lambda b, t: (b, t, blk))

    return pl.pallas_call(
        body, name="mix_out_fwd", grid=(bsz, t_total // tt),
        in_specs=[col(0), col(0), col(Z_BLK), col(GG_BLK), _vec_spec(512), _vec_spec(512)],
        out_specs=_tok_spec(tt), out_shape=jax.ShapeDtypeStruct((bsz, t_total, D), MXU_DT),
        compiler_params=_params(("parallel", "parallel")))(o_dn, o_gla, proj, proj, grow_dn, grow_gla)


def _mix_out_bwd(do, o_dn, o_gla, proj, grow_dn, grow_gla):
    bsz, t_total, _ = o_dn.shape
    tt = _div_tile(t_total, 256)

    def body(do_ref, od_ref, og_ref, z_ref, gg_ref, gd_ref, gl_ref,
             dod_ref, dog_ref, dz_ref, dgg_ref, dgd_ref, dgl_ref):
        @pl.when((pl.program_id(0) == 0) & (pl.program_id(1) == 0))
        def _():
            dgd_ref[...] = jnp.zeros_like(dgd_ref)
            dgl_ref[...] = jnp.zeros_like(dgl_ref)

        def one(o_ref, gate_ref, g_ref, ct, do_out, dgate_out, dg_out):
            _, vjp = jax.vjp(_gate_norm, o_ref[0], gate_ref[0], g_ref[...])
            d_o, d_gate, d_row = vjp(ct)
            do_out[0] = d_o
            dgate_out[0] = d_gate.astype(MXU_DT)
            acc = d_row[:, :HD]
            for h in range(1, HEADS):
                acc = acc + d_row[:, h * HD:(h + 1) * HD]
            dg_out[...] += acc

        ct = do_ref[0]
        one(od_ref, z_ref, gd_ref, ct[:, :512], dod_ref, dz_ref, dgd_ref)
        one(og_ref, gg_ref, gl_ref, ct[:, 512:], dog_ref, dgg_ref, dgl_ref)

    def col(blk):
        return pl.BlockSpec((1, tt, 512), lambda b, t: (b, t, blk))

    f512 = jax.ShapeDtypeStruct((bsz, t_total, 512), F32)
    b512 = jax.ShapeDtypeStruct((bsz, t_total, 512), MXU_DT)
    g128 = jax.ShapeDtypeStruct((1, HD), F32)
    return pl.pallas_call(
        body, name="mix_out_bwd", grid=(bsz, t_total // tt),
        in_specs=[_tok_spec(tt), col(0), col(0), col(Z_BLK), col(GG_BLK), _vec_spec(512), _vec_spec(512)],
        out_specs=(col(0), col(0), col(0), col(0), _vec_spec(HD), _vec_spec(HD)),
        out_shape=(f512, f512, b512, b512, g128, g128),
        compiler_params=_params(("arbitrary", "arbitrary")))(do, o_dn, o_gla, proj, proj, grow_dn, grow_gla)


def _sum_slots(x, name):
    n, rows, cols = x.shape
    tr = _div_tile(rows, max(8, (1 << 19) // cols))

    def body(x_ref, o_ref):
        acc = x_ref[0]
        for i in range(1, n):
            acc = acc + x_ref[i]
        o_ref[...] = acc

    return pl.pallas_call(
        body, name=name, grid=(rows // tr,),
        in_specs=[pl.BlockSpec((n, tr, cols), lambda i: (0, i, 0))],
        out_specs=pl.BlockSpec((tr, cols), lambda i: (i, 0)),
        out_shape=jax.ShapeDtypeStruct((rows, cols), F32), compiler_params=_params(("parallel",)))(x)


def _pair_add(g8, r1, core, name):
    _, rows, cols = g8.shape
    tr = _div_tile(rows, max(8, (1 << 19) // cols))
    g42 = g8.reshape(4, 2, rows, cols)

    def body(core_ref, g_ref, r_ref, o_ref):
        o_ref[0] = g_ref[0, 0] + r_ref[0]

    return pl.pallas_call(
        body, name=name,
        grid_spec=pltpu.PrefetchScalarGridSpec(
            num_scalar_prefetch=1, grid=(4, rows // tr),
            in_specs=[pl.BlockSpec((1, 1, tr, cols), lambda s, i, core_ref: (s, core_ref[0], i, 0)),
                      pl.BlockSpec((1, tr, cols), lambda s, i, core_ref: (s, i, 0))],
            out_specs=pl.BlockSpec((1, tr, cols), lambda s, i, core_ref: (s, i, 0))),
        out_shape=jax.ShapeDtypeStruct((4, rows, cols), F32),
        compiler_params=_params(("parallel", "parallel")))(core, g42, r1)


def _chip_add(p4, r2, chip, name):
    _, rows, cols = p4.shape
    tr = _div_tile(rows, max(8, (1 << 19) // cols))

    def body(chip_ref, p_ref, r_ref, o_ref):
        o_ref[...] = ((p_ref[0] + r_ref[0]) + r_ref[1]) + r_ref[2]

    return pl.pallas_call(
        body, name=name,
        grid_spec=pltpu.PrefetchScalarGridSpec(
            num_scalar_prefetch=1, grid=(rows // tr,),
            in_specs=[pl.BlockSpec((1, tr, cols), lambda i, chip_ref: (chip_ref[0], i, 0)),
                      pl.BlockSpec((3, tr, cols), lambda i, chip_ref: (0, i, 0))],
            out_specs=pl.BlockSpec((tr, cols), lambda i, chip_ref: (i, 0))),
        out_shape=jax.ShapeDtypeStruct((rows, cols), F32),
        compiler_params=_params(("parallel",)))(chip, p4, r2)


def _adamw(w, g, m, v, name):
    rows, cols = w.shape
    tr = _div_tile(rows, max(8, (1 << 18) // cols))

    def body(w_ref, g_ref, m_ref, v_ref, d_ref, nm_ref, nv_ref):
        g_ = g_ref[...]
        nm = ADAM_B1 * m_ref[...] + (1.0 - ADAM_B1) * g_
        nv = ADAM_B2 * v_ref[...] + (1.0 - ADAM_B2) * (g_ * g_)
        m_hat = nm / (1.0 - ADAM_B1 ** ADAM_STEP)
        v_hat = nv / (1.0 - ADAM_B2 ** ADAM_STEP)
        d_ref[...] = -ADAM_LR * (m_hat / (jnp.sqrt(v_hat) + ADAM_EPS) + ADAM_WD * w_ref[...])
        nm_ref[...] = nm
        nv_ref[...] = nv

    spec = pl.BlockSpec((tr, cols), lambda i: (i, 0))
    shp = jax.ShapeDtypeStruct((rows, cols), F32)
    return pl.pallas_call(body, name=name, grid=(rows // tr,), in_specs=[spec] * 4, out_specs=(spec,) * 3,
                          out_shape=(shp,) * 3, compiler_params=_params(("parallel",)))(w, g, m, v)


def _position():
    return lax.axis_index("x"), lax.axis_index("y"), lax.axis_index("c")


def _slot(px, py, pc):
    return 4 * px + 2 * py + pc


def _gather_small(x, name):
    rows, cols = x.shape

    def body(x_ref, o_ref, send_sems, recv_sems):
        mx, my, mc = _position()

        def peer(k):
            return (mx ^ ((k >> 2) & 1), my ^ ((k >> 1) & 1), mc ^ (k & 1))

        o_ref[_slot(mx, my, mc)] = x_ref[...]
        sends = []
        for k in range(1, N_DEV):
            cp = pltpu.make_async_remote_copy(src_ref=x_ref, dst_ref=o_ref.at[_slot(mx, my, mc)],
                                              send_sem=send_sems.at[k - 1], recv_sem=recv_sems.at[k - 1],
                                              device_id=peer(k), device_id_type=MESH)
            cp.start()
            sends.append(cp)
        for k in range(1, N_DEV):
            pltpu.make_async_remote_copy(src_ref=x_ref, dst_ref=o_ref.at[_slot(*peer(k))],
                                         send_sem=send_sems.at[k - 1], recv_sem=recv_sems.at[k - 1],
                                         device_id=peer(k), device_id_type=MESH).wait_recv()
        for cp in sends:
            cp.wait_send()

    return pl.pallas_call(
        body, name=name, out_shape=jax.ShapeDtypeStruct((N_DEV, rows, cols), x.dtype),
        in_specs=[pl.BlockSpec(memory_space=pltpu.VMEM)], out_specs=pl.BlockSpec(memory_space=pltpu.VMEM),
        scratch_shapes=[pltpu.SemaphoreType.DMA((N_DEV - 1,)), pltpu.SemaphoreType.DMA((N_DEV - 1,))],
        compiler_params=pltpu.CompilerParams(vmem_limit_bytes=VMEM_LIMIT_V7X))(x)


def _gather_big(shards):
    n = len(shards)

    def body(*refs):
        xs, outs = refs[:n], refs[n:2 * n]
        send_sems, recv_sems, local_sems = refs[2 * n:]
        mx, my, mc = _position()
        me, sibling = (mx, my, mc), (mx, my, 1 - mc)
        chips = [(1 - mx, my), (mx, 1 - my), (1 - mx, 1 - my)]

        def copy(a, k, block, to, src=None):
            dst = outs[a].at[_slot(*block)]
            return pltpu.make_async_remote_copy(src_ref=dst if src is None else src, dst_ref=dst,
                                                send_sem=send_sems.at[7 * a + k], recv_sem=recv_sems.at[7 * a + k],
                                                device_id=to, device_id_type=MESH)

        mine = [pltpu.make_async_copy(xs[a], outs[a].at[_slot(*me)], local_sems.at[a]) for a in range(n)]
        for cp in mine:
            cp.start()
        started = []
        for a in range(n):
            started.append(copy(a, 0, me, sibling, src=xs[a]))
            started += [copy(a, 1 + j, me, (*chip, mc), src=xs[a]) for j, chip in enumerate(chips)]
        for cp in started:
            cp.start()
        for j, chip in enumerate(chips):
            for a in range(n):
                copy(a, 1 + j, (*chip, mc), me).wait_recv()
                fwd = copy(a, 4 + j, (*chip, mc), sibling)
                fwd.start()
                started.append(fwd)
        for a in range(n):
            copy(a, 0, sibling, me).wait_recv()
            for j, chip in enumerate(chips):
                copy(a, 4 + j, (*chip, 1 - mc), me).wait_recv()
        for cp in started:
            cp.wait_send()
        for cp in mine:
            cp.wait()

    any_spec = pl.BlockSpec(memory_space=pl.ANY)
    return pl.pallas_call(
        body, name="gather_weights",
        out_shape=tuple(jax.ShapeDtypeStruct((N_DEV,) + s.shape, s.dtype) for s in shards),
        in_specs=[any_spec] * n, out_specs=(any_spec,) * n,
        scratch_shapes=[pltpu.SemaphoreType.DMA((7 * n,)), pltpu.SemaphoreType.DMA((7 * n,)),
                        pltpu.SemaphoreType.DMA((n,))])(*shards)


def _scatter_sibling(grads):
    n = len(grads)

    def body(*refs):
        gs, outs = refs[:n], refs[n:2 * n]
        send_sems, recv_sems = refs[2 * n:]
        mx, my, mc = _position()
        copies = []
        for a in range(n):
            for s in range(4):
                copies.append(pltpu.make_async_remote_copy(
                    src_ref=gs[a].at[2 * s + (1 - mc)], dst_ref=outs[a].at[s],
                    send_sem=send_sems.at[4 * a + s], recv_sem=recv_sems.at[4 * a + s],
                    device_id=(mx, my, 1 - mc), device_id_type=MESH))
        for cp in copies:
            cp.start()
        for cp in copies:
            cp.wait_recv()
        for cp in copies:
            cp.wait_send()

    any_spec = pl.BlockSpec(memory_space=pl.ANY)
    return pl.pallas_call(
        body, name="scatter_sibling",
        out_shape=tuple(jax.ShapeDtypeStruct((4,) + g.shape[1:], g.dtype) for g in grads),
        in_specs=[any_spec] * n, out_specs=(any_spec,) * n,
        scratch_shapes=[pltpu.SemaphoreType.DMA((4 * n,)), pltpu.SemaphoreType.DMA((4 * n,))])(*grads)


def _scatter_chips(sums):
    n = len(sums)

    def body(*refs):
        ps, outs = refs[:n], refs[n:2 * n]
        send_sems, recv_sems = refs[2 * n:]
        mx, my, mc = _position()
        chips = [(1 - mx, my), (mx, 1 - my), (1 - mx, 1 - my)]
        copies = []
        for a in range(n):
            for k, (cx, cy) in enumerate(chips):
                copies.append(pltpu.make_async_remote_copy(
                    src_ref=ps[a].at[2 * cx + cy], dst_ref=outs[a].at[k],
                    send_sem=send_sems.at[3 * a + k], recv_sem=recv_sems.at[3 * a + k],
                    device_id=(cx, cy, mc), device_id_type=MESH))
        for cp in copies:
            cp.start()
        for cp in copies:
            cp.wait_recv()
        for cp in copies:
            cp.wait_send()

    any_spec = pl.BlockSpec(memory_space=pl.ANY)
    return pl.pallas_call(
        body, name="scatter_chips",
        out_shape=tuple(jax.ShapeDtypeStruct((3,) + p.shape[1:], p.dtype) for p in sums),
        in_specs=[any_spec] * n, out_specs=(any_spec,) * n,
        scratch_shapes=[pltpu.SemaphoreType.DMA((3 * n,)), pltpu.SemaphoreType.DMA((3 * n,))])(*sums)


def _pad_heads(x, axis):
    shp = list(x.shape)
    x4 = x.reshape(shp[:axis] + [HEADS, GLA_KEY] + shp[axis + 1:])
    pad = [(0, 0)] * x4.ndim
    pad[axis + 1] = (0, HD - GLA_KEY)
    return jnp.pad(x4, pad).reshape(shp[:axis] + [HEADS * HD] + shp[axis + 1:])


def _unpad_heads(x, axis):
    shp = list(x.shape)
    x4 = x.reshape(shp[:axis] + [HEADS, HD] + shp[axis + 1:])
    x4 = lax.slice_in_dim(x4, 0, GLA_KEY, axis=axis + 1)
    return x4.reshape(shp[:axis] + [HEADS * GLA_KEY] + shp[axis + 1:])


O_Z_END, O_AB, O_GQ, O_GK, O_GV, O_R = 2048, 2048, 2056, 2312, 2568, 3592


def _pad_in_rows(wt):
    return jnp.concatenate([
        wt[:O_Z_END], _pad_heads(wt[O_GQ:O_GK], 0), _pad_heads(wt[O_GK:O_GV], 0), wt[O_GV:O_R],
        wt[O_AB:O_GQ], wt[O_R:], jnp.zeros((P_W - P_SM - 8 - GATE_RANK, wt.shape[1]), wt.dtype)], axis=0)


def _unpad_in_rows(gt):
    return jnp.concatenate([
        gt[:P_GQ], gt[P_SM:P_SM + 8], _unpad_heads(gt[P_GQ:P_GK], 0), _unpad_heads(gt[P_GK:P_GV], 0),
        gt[P_GV:P_SM], gt[P_SM + 8:P_SM + 8 + GATE_RANK]], axis=0)


def _lane_row(vals, width=128):
    return jnp.pad(vals.reshape(1, -1), ((0, 0), (0, width - vals.size)))


SMALL_NAMES = ["ln0_g", "ln0_b", "b_ada", "dn_conv", "dn_a_log", "dn_dt_bias", "dn_norm_g", "gla_w_gate2",
               "gla_b_gate", "gla_norm_g", "ln1_g", "ln1_b", "ffn_conv", "ffn_conv_b", "ln2_g", "ln2_b"]
WEIGHTS = ["ln0_g", "ln0_b", "w_ada", "b_ada", "w_in", "dn_conv", "dn_a_log", "dn_dt_bias", "dn_norm_g",
           "gla_w_gate2", "gla_b_gate", "gla_norm_g", "w_o", "ln1_g", "ln1_b", "ffn_w_up", "ffn_conv", "ffn_conv_b",
           "ffn_w_down", "ln2_g", "ln2_b"]


def kernel(x, c, ln0_g, ln0_b, w_ada, b_ada, w_in, dn_conv, dn_a_log, dn_dt_bias, dn_norm_g, gla_w_gate2, gla_b_gate, gla_norm_g, w_o, ln1_g, ln1_b, ffn_w_up, ffn_conv, ffn_conv_b, ffn_w_down, ln2_g, ln2_b, loss_target, m_ln0_g, m_ln0_b, m_w_ada, m_b_ada, m_w_in, m_dn_conv, m_dn_a_log, m_dn_dt_bias, m_dn_norm_g, m_gla_w_gate2, m_gla_b_gate, m_gla_norm_g, m_w_o, m_ln1_g, m_ln1_b, m_ffn_w_up, m_ffn_conv, m_ffn_conv_b, m_ffn_w_down, m_ln2_g, m_ln2_b, v_ln0_g, v_ln0_b, v_w_ada, v_b_ada, v_w_in, v_dn_conv, v_dn_a_log, v_dn_dt_bias, v_dn_norm_g, v_gla_w_gate2, v_gla_b_gate, v_gla_norm_g, v_w_o, v_ln1_g, v_ln1_b, v_ffn_w_up, v_ffn_conv, v_ffn_conv_b, v_ffn_w_down, v_ln2_g, v_ln2_b):
    args = dict(locals())
    w_given = {n: args[n] for n in WEIGHTS}
    m_given = {n: args["m_" + n] for n in WEIGHTS}
    v_given = {n: args["v_" + n] for n in WEIGHTS}
    bsz, t_total, _ = x.shape
    ntok = bsz * t_total
    mx, my, mc = _position()
    me = _slot(mx, my, mc)

    pack1 = jnp.concatenate([c.reshape(-1), dn_conv.reshape(-1), gla_w_gate2.reshape(-1), ffn_conv.reshape(-1)])
    n1 = pack1.size
    rows1 = -(-n1 // 1024) * 8
    pack1 = jnp.pad(pack1, (0, rows1 * 128 - n1)).reshape(rows1, 128)
    got1 = _gather_small(pack1, "gather_cond").reshape(N_DEV, -1)
    o1 = bsz * D
    o2 = o1 + dn_conv.size
    o3 = o2 + gla_w_gate2.size
    c_all = got1[:, :o1].reshape(N_DEV * bsz, D)
    dn_conv_f = got1[:, o1:o2].reshape(N_DEV, DN_CONV_K, -1).transpose(1, 0, 2).reshape(DN_CONV_K, QKV_W)
    gate2_f = got1[:, o2:o3].reshape(N_DEV, GATE_RANK, -1).transpose(1, 0, 2).reshape(GATE_RANK, HEADS * GLA_KEY)
    ffn_conv_f = got1[:, o3:n1].reshape(N_DEV, FFN_CONV_K, -1).transpose(1, 0, 2).reshape(FFN_CONV_K, 2 * D_FF)

    win_t = w_in[0].T.astype(MXU_DT)
    wup_t = ffn_w_up[0].T.astype(MXU_DT)
    win_all, wo_all, wup_all, wdn_all = _gather_big(
        [win_t, w_o[0].astype(MXU_DT), wup_t, ffn_w_down[0].astype(MXU_DT)])
    win_p = _pad_in_rows(win_all.reshape(IN_W, D))
    wo_f = wo_all.reshape(D, D)
    wup_f = wup_all.reshape(2 * D_FF, D)
    wdn_f = wdn_all.reshape(D_FF, D)

    ncol = w_ada.shape[2]
    b_cols = lax.dynamic_slice_in_dim(b_ada, me * ncol, ncol, axis=1)
    mod_part = _ada_fwd(c_all, w_ada[0], b_cols)
    mod_all = _gather_small(mod_part.reshape(-1, 128), "gather_mod").reshape(N_DEV, N_DEV * bsz, ncol)
    mod = lax.dynamic_slice_in_dim(mod_all, me * bsz, bsz, axis=1).transpose(1, 0, 2).reshape(bsz, 6, 1, D)
    sh_a, sc_a, gt_a, sh_f, sc_f, gt_f = (mod[:, i] for i in range(6))

    g0, b0 = ln0_g.reshape(1, D), ln0_b.reshape(1, D)
    alog_row, dt_row = _lane_row(dn_a_log[0]), _lane_row(dn_dt_bias[0])
    grow_dn, grow_gla = jnp.tile(dn_norm_g, (1, HEADS)), jnp.tile(gla_norm_g, (1, HEADS))
    w2 = jnp.zeros((128, HEADS * HD), F32).at[SM_R:SM_R + GATE_RANK].set(_pad_heads(gate2_f, 1))
    bg = _pad_heads(gla_b_gate, 1)

    h_a = _ln0_mod(x, g0, b0, sc_a, sh_a)
    proj = _mm(h_a.reshape(ntok, D), win_p, "nt", F32, "mm_proj", tn=1408).reshape(bsz, t_total, P_W)
    q, k, v, gates = _dn_pre_fwd(proj, dn_conv_f, alog_row, dt_row)
    o_dn, s_dn = _dn_rec_fwd(q, k, v, gates)
    o_gla, s_gla = _gla_rec_fwd(proj, w2, bg)
    o_mix = _mix_out_fwd(o_dn, o_gla, proj, grow_dn, grow_gla)
    y = _mm(o_mix.reshape(ntok, D), wo_f, "nn", F32, "mm_wo", tn=1024).reshape(bsz, t_total, D)
    r1, h_f = _res_ln_mod(x, y, gt_a, g0, b0, ln1_g, ln1_b, sc_f, sh_f)
    up = _mm(h_f.reshape(ntok, D), wup_f, "nt", F32, "mm_up", tn=1408).reshape(bsz, t_total, 2 * D_FF)
    act = _ffn_act_fwd(up, ffn_conv_f, ffn_conv_b)
    y2 = _mm(act.reshape(ntok, D_FF), wdn_f, "nn", F32, "mm_down", tn=1024).reshape(bsz, t_total, D)
    loss_rows, dr2, dy2, dgt_f, d_ln2_g, d_ln2_b = _final_fwd_bwd(r1, y2, gt_f, ln1_g, ln1_b, ln2_g, ln2_b, loss_target)
    loss = lax.psum(0.5 * jnp.sum(loss_rows) / D, ("x", "y", "c"))

    dy2_2 = dy2.reshape(ntok, D)
    dact = _mm(dy2_2, wdn_f, "nt", F32, "mm_dact", tn=1408).reshape(bsz, t_total, D_FF)
    g_wdn = _mm(act.reshape(ntok, D_FF), dy2_2, "tn", F32, "mm_gwdn", tm=1408, tn=1024, tk=512)
    dup, d_ffn_conv, d_ffn_conv_b = _ffn_act_bwd(up, dact, ffn_conv_f, ffn_conv_b)
    dup_2 = dup.reshape(ntok, 2 * D_FF)
    dh_f = _mm(dup_2, wup_f, "nn", F32, "mm_dhf", tn=1024, tk=1408).reshape(bsz, t_total, D)
    g_wup_t = _mm(dup_2, h_f.reshape(ntok, D), "tn", F32, "mm_gwup", tm=1408, tn=1024, tk=512)
    dr1, dsc_f, dsh_f, d_ln1_g, d_ln1_b, dy, dgt_a = _ln_bwd_call(
        "ln1_bwd", dr2, dh_f, r1, ln1_g, ln1_b, sc_f, y=y, gt=gt_a)

    dy_2 = dy.reshape(ntok, D)
    do = _mm(dy_2, wo_f, "nt", F32, "mm_do", tn=1024).reshape(bsz, t_total, D)
    g_wo = _mm(o_mix.reshape(ntok, D), dy_2, "tn", F32, "mm_gwo", tm=1024, tn=1024, tk=512)
    do_dn, do_gla, dz, dgg, d_dn_norm, d_gla_norm = _mix_out_bwd(do, o_dn, o_gla, proj, grow_dn, grow_gla)
    dq, dk, dv, dgates = _dn_rec_bwd(q, k, v, gates, s_dn, do_dn)
    dqkv, dsm_dn, d_dn_conv, d_alog_row, d_dt_row = _dn_pre_bwd(proj, dq, dk, dv, dgates, dn_conv_f, alog_row, dt_row)
    dgq, dgk, dgv, dsm, d_w2, d_bg = _gla_rec_bwd(proj, w2, bg, s_gla, do_gla, dsm_dn)
    dproj = jnp.concatenate([dqkv, dz, dgq, dgk, dgv, dgg, dsm], axis=-1).reshape(ntok, P_W)
    dh_a = _mm(dproj, win_p, "nn", F32, "mm_dha", tn=1024, tk=1408).reshape(bsz, t_total, D)
    g_win_p = _mm(dproj, h_a.reshape(ntok, D), "tn", F32, "mm_gwin", tm=1408, tn=1024, tk=512)
    grad_x, dsc_a, dsh_a, d_ln0_g, d_ln0_b = _ln_bwd_call("ln0_bwd", dr1, dh_a, x, g0, b0, sc_a)

    big = [_unpad_in_rows(g_win_p).reshape(N_DEV, -1, D), g_wo.reshape(N_DEV, -1, D),
           g_wup_t.reshape(N_DEV, -1, D), g_wdn.reshape(N_DEV, -1, D)]
    from_sibling = _scatter_sibling(big)
    core = mc.reshape(1).astype(jnp.int32)
    chip_sums = [_pair_add(g8, r1_, core, f"pair_add_{i}") for i, (g8, r1_) in enumerate(zip(big, from_sibling))]
    from_chips = _scatter_chips(chip_sums)
    chip = (2 * mx + my).reshape(1).astype(jnp.int32)
    g_win_t, g_wo_s, g_wup_ts, g_wdn_s = (
        _chip_add(p4, r2_, chip, f"chip_add_{i}") for i, (p4, r2_) in enumerate(zip(chip_sums, from_chips)))

    dmod = jnp.concatenate([dsh_a, dsc_a, dgt_a, dsh_f, dsc_f, dgt_f], axis=1).reshape(-1)
    small_parts = {
        "ln0_g": d_ln0_g, "ln0_b": d_ln0_b, "ln1_g": d_ln1_g, "ln1_b": d_ln1_b, "ln2_g": d_ln2_g, "ln2_b": d_ln2_b,
        "dn_a_log": d_alog_row[:, :HEADS], "dn_dt_bias": d_dt_row[:, :HEADS],
        "dn_norm_g": d_dn_norm, "gla_norm_g": d_gla_norm, "gla_b_gate": _unpad_heads(d_bg, 1),
        "ffn_conv_b": d_ffn_conv_b, "dn_conv": d_dn_conv,
        "gla_w_gate2": _unpad_heads(d_w2[SM_R:SM_R + GATE_RANK], 1), "ffn_conv": d_ffn_conv}
    order = sorted(small_parts)
    flat = jnp.concatenate([small_parts[n].reshape(-1) for n in order] + [dmod])
    n3 = flat.size
    rows3 = -(-n3 // 1024) * 8
    pack3 = jnp.pad(flat, (0, rows3 * 128 - n3)).reshape(rows3, 128)
    got3 = _gather_small(pack3, "gather_small_grads")
    tot3 = _sum_slots(got3, "sum_small_grads").reshape(-1)
    grads = {}
    off = 0
    for n in order:
        size = small_parts[n].size
        grads[n] = tot3[off:off + size]
        off += size
    dmod_all = got3.reshape(N_DEV, -1)[:, off:off + dmod.size].reshape(N_DEV * bsz, 6 * D)
    dmod_cols = lax.dynamic_slice_in_dim(dmod_all, me * ncol, ncol, axis=1)
    g_wada, g_bada = _ada_bwd(c_all, dmod_all, dmod_cols)
    grads["b_ada"] = g_bada

    def col_shard(full, rows):
        part = full.reshape(rows, -1)
        width = part.shape[1] // N_DEV
        return lax.dynamic_slice_in_dim(part, me * width, width, axis=1)

    grads["dn_conv"] = col_shard(grads["dn_conv"], DN_CONV_K)
    grads["gla_w_gate2"] = col_shard(grads["gla_w_gate2"], GATE_RANK)
    grads["ffn_conv"] = col_shard(grads["ffn_conv"], FFN_CONV_K)
    grads = {n: g.reshape(w_given[n].shape) for n, g in grads.items()}
    grads["w_ada"] = g_wada.reshape(w_ada.shape)
    grads["w_in"] = g_win_t.T.reshape(w_in.shape)
    grads["w_o"] = g_wo_s.reshape(w_o.shape)
    grads["ffn_w_up"] = g_wup_ts.T.reshape(ffn_w_up.shape)
    grads["ffn_w_down"] = g_wdn_s.reshape(ffn_w_down.shape)

    delta, new_m, new_v = {}, {}, {}
    for n in ["w_ada", "w_in", "w_o", "ffn_w_up", "ffn_w_down"]:
        shp = w_given[n].shape
        two_d = lambda a: a.reshape(shp[-2], shp[-1])
        d_, m_, v_ = _adamw(two_d(w_given[n]), two_d(grads[n]), two_d(m_given[n]), two_d(v_given[n]), "adamw_" + n)
        delta[n], new_m[n], new_v[n] = d_.reshape(shp), m_.reshape(shp), v_.reshape(shp)

    def pack_small(src):
        flat_ = jnp.concatenate([src[n].reshape(-1) for n in SMALL_NAMES])
        rows_ = -(-flat_.size // 1024) * 8
        return jnp.pad(flat_, (0, rows_ * 128 - flat_.size)).reshape(rows_, 128)

    d_s, m_s, v_s = _adamw(pack_small(w_given), pack_small(grads), pack_small(m_given), pack_small(v_given),
                           "adamw_small")
    off = 0
    for n in SMALL_NAMES:
        size, shp = w_given[n].size, w_given[n].shape
        delta[n] = d_s.reshape(-1)[off:off + size].reshape(shp)
        new_m[n] = m_s.reshape(-1)[off:off + size].reshape(shp)
        new_v[n] = v_s.reshape(-1)[off:off + size].reshape(shp)
        off += size

    return (loss, grad_x, *[grads[n] for n in WEIGHTS], *[delta[n] for n in WEIGHTS],
            *[new_m[n] for n in WEIGHTS], *[new_v[n] for n in WEIGHTS])
```

```python
import functools

import jax
import jax.numpy as jnp
from jax import lax
from jax.experimental import pallas as pl
from jax.experimental.pallas import tpu as pltpu

F32 = jnp.float32
MXU_DT = jnp.bfloat16
HI = lax.Precision.HIGHEST
MESH = pl.DeviceIdType.MESH
N_DEV = 8

D = 1024
HEADS = 4
HD = 128
CHUNK = 64
GLA_KEY = 64
GLA_TAU = 16.0
GATE_RANK = 16
D_FF = 2816
IN_W = 3608
ALPHA = 2.0 ** 0.25
EPS = 1e-6
DN_CONV_K = 4
FFN_CONV_K = 3
HALO = 8

P_QKV, P_Z, P_GQ, P_GK, P_GV, P_GG, P_SM, P_W = 0, 1536, 2048, 2560, 3072, 3584, 4096, 4224
SM_A, SM_B, SM_R = 0, 4, 8

ADAM_LR, ADAM_B1, ADAM_B2, ADAM_EPS, ADAM_WD, ADAM_STEP = 0.001, 0.9, 0.999, 1e-08, 0.01, 10

VMEM_LIMIT_V7X = 56 * 1024 * 1024


def _params(sem=None):
    return pltpu.CompilerParams(dimension_semantics=sem, vmem_limit_bytes=VMEM_LIMIT_V7X)


def _dg(a, b, dims, prec=None):
    return lax.dot_general(a, b, (dims, ((), ())), precision=prec, preferred_element_type=F32)


def _dot(a, b, prec=None):
    return _dg(a, b, ((1,), (0,)), prec)


def _dot_nt(a, b, prec=None):
    return _dg(a, b, ((1,), (1,)), prec)


def _dot_tn(a, b, prec=None):
    return _dg(a, b, ((0,), (0,)), prec)


def _iota(shape, dim):
    return lax.broadcasted_iota(jnp.int32, shape, dim)


def _sigmoid(x):
    return jax.nn.sigmoid(x)


def _silu(x):
    return x * _sigmoid(x)


def _softplus(x):
    return jnp.maximum(x, 0.0) + jnp.log(1.0 + jnp.exp(-jnp.abs(x)))


def _ln_stats(x):
    mu = jnp.mean(x, axis=-1, keepdims=True)
    xc = x - mu
    rstd = lax.rsqrt(jnp.mean(xc * xc, axis=-1, keepdims=True) + EPS)
    return xc * rstd, rstd


def _ln_bwd(dxhat, xhat, rstd):
    return rstd * (dxhat - jnp.mean(dxhat, axis=-1, keepdims=True)
                   - xhat * jnp.mean(dxhat * xhat, axis=-1, keepdims=True))


NN, NT, TN = ((1,), (0,)), ((1,), (1,)), ((0,), (0,))


def _split2(a):
    hi = a.astype(jnp.bfloat16)
    return hi, (a - hi.astype(F32)).astype(jnp.bfloat16)


def _d3(a, b, dims):
    ah, al = _split2(a)
    bh, bl = _split2(b)
    return _dg(ah, bh, dims) + (_dg(ah, bl, dims) + _dg(al, bh, dims))


@jax.custom_vjp
def _dot3(a, b):
    return _d3(a, b, NN)


_dot3.defvjp(lambda a, b: (_d3(a, b, NN), (a, b)),
             lambda res, g: (_d3(g, res[1], NT), _d3(res[0], g, TN)))


def _split3(b):
    b1 = b.astype(jnp.bfloat16)
    r1 = b - b1.astype(F32)
    b2 = r1.astype(jnp.bfloat16)
    return b1, b2, (r1 - b2.astype(F32)).astype(jnp.bfloat16)


def _sum3(fn, b):
    b1, b2, b3 = _split3(b)
    return fn(b1) + (fn(b2) + fn(b3))


@jax.custom_vjp
def _mask_dot(e, b):
    return _sum3(lambda t: _dg(e, t, NN), b)


_mask_dot.defvjp(lambda e, b: (_mask_dot(e, b), e),
                 lambda e, g: (jnp.zeros_like(e), _sum3(lambda t: _dg(e, t, TN), g)))


@jax.custom_vjp
def _mask_dot_nt(e, b):
    return _sum3(lambda t: _dg(e, t, NT), b)


_mask_dot_nt.defvjp(lambda e, b: (_mask_dot_nt(e, b), e),
                    lambda e, g: (jnp.zeros_like(e), _sum3(lambda t: _dg(t, e, TN), g)))


def _tri_inv_impl(ms):
    n = ms[0].shape[0]
    r, c = _iota((n, n), 0), _iota((n, n), 1)
    eye = (r == c).astype(F32)
    diag = (r >> 3) == (c >> 3)
    ds = [jnp.where(diag, m, 0.0) for m in ms]
    d2s = [_d3(d, d, NN) for d in ds]
    d4s = [_d3(d2, d2, NN) for d2 in d2s]
    invs = [_d3(eye - d, eye + d2, NN) for d, d2 in zip(ds, d2s)]
    invs = [_d3(inv, eye + d4, NN) for inv, d4 in zip(invs, d4s)]
    shift = 3
    while (1 << shift) < n:
        rb, cb = r >> shift, c >> shift
        sel = ((rb & 1) == 1) & (cb == rb - 1)
        tmp = [_d3(inv, jnp.where(sel, m, 0.0), NN) for inv, m in zip(invs, ms)]
        invs = [inv - _d3(t, inv, NN) for t, inv in zip(tmp, invs)]
        shift += 1
    return invs


@jax.custom_vjp
def _tri_inv(ms):
    return _tri_inv_impl(ms)


def _tri_inv_fwd(ms):
    invs = _tri_inv_impl(ms)
    return invs, invs


def _tri_inv_bwd(invs, das):
    tmp = [_d3(a, da, TN) for a, da in zip(invs, das)]
    return ([-_d3(t, a, NT) for t, a in zip(tmp, invs)],)


_tri_inv.defvjp(_tri_inv_fwd, _tri_inv_bwd)


def _dn_chunk(s_list, q, k, v, gates):
    nb = len(q)
    c = q[0].shape[0]
    r64, c64 = _iota((c, c), 0), _iota((c, c), 1)
    causal = r64 >= c64
    strict = r64 > c64
    tri = causal.astype(jnp.bfloat16)
    eye = (_iota((HD, HD), 0) == _iota((HD, HD), 1)).astype(jnp.bfloat16)
    lane = _iota(gates[0].shape, 1)
    lane1 = _iota((1, HD), 1)
    g_all = [_mask_dot(tri, g) for g in gates]
    g_all_t = [_mask_dot_nt(eye, g) for g in g_all]
    row = _iota(g_all_t[0].shape, 0)
    last = [jnp.sum(g, axis=0, keepdims=True) for g in gates]
    prob = [(b, h) for b in range(nb) for h in range(HEADS)]
    sl = [slice(h * HD, (h + 1) * HD) for h in range(HEADS)]
    qh = [q[b][:, sl[h]] for b, h in prob]
    kh = [k[b][:, sl[h]] for b, h in prob]
    vh = [v[b][:, sl[h]] for b, h in prob]
    s = [s_list[b][h] for b, h in prob]
    beta = [jnp.sum(jnp.where(lane == SM_B + h, gates[b], 0.0), axis=-1, keepdims=True) for b, h in prob]
    g_c = [jnp.sum(jnp.where(lane == SM_A + h, g_all[b], 0.0), axis=-1, keepdims=True) for b, h in prob]
    g_r = [jnp.sum(jnp.where(row == SM_A + h, g_all_t[b], 0.0), axis=0, keepdims=True) for b, h in prob]
    g_last = [jnp.sum(jnp.where(lane1 == SM_A + h, last[b], 0.0), axis=-1, keepdims=True) for b, h in prob]
    decay = [jnp.where(causal, jnp.exp(jnp.where(causal, gc - gr, 0.0)), 0.0) for gc, gr in zip(g_c, g_r)]
    kb = [k_ * b_ for k_, b_ in zip(kh, beta)]
    m_low = [jnp.where(strict, _dot_nt(kb_, k_) * d_, 0.0) for kb_, k_, d_ in zip(kb, kh, decay)]
    attn = [_dot_nt(q_, k_) * d_ for q_, k_, d_ in zip(qh, kh, decay)]
    a_inv = _tri_inv(m_low)
    eg = [jnp.exp(gc) for gc in g_c]
    uw = [_dot3(a_, jnp.concatenate([v_ * b_, kb_ * e_], axis=1))
          for a_, v_, b_, kb_, e_ in zip(a_inv, vh, beta, kb, eg)]
    v_new = [uw_[:, :HD] - _dot(uw_[:, HD:], s_) for uw_, s_ in zip(uw, s)]
    qs = [_dot(q_ * e_, s_) for q_, e_, s_ in zip(qh, eg, s)]
    o = [qs_ + _dot(a_, vn_) for qs_, a_, vn_ in zip(qs, attn, v_new)]
    k_dec = [k_ * jnp.exp(gl - gc) for k_, gl, gc in zip(kh, g_last, g_c)]
    s_new = [s_ * jnp.exp(gl) + _dot_tn(kd_, vn_) for s_, gl, kd_, vn_ in zip(s, g_last, k_dec, v_new)]
    outs = [jnp.concatenate(o[b * HEADS:(b + 1) * HEADS], axis=-1) for b in range(nb)]
    return outs, [s_new[b * HEADS:(b + 1) * HEADS] for b in range(nb)]


def _gla_chunk(st_list, q, k, v, small, w2, bg):
    nb = len(q)
    c = q[0].shape[0]
    causal = _iota((c, c), 0) >= _iota((c, c), 1)
    tri = causal.astype(jnp.bfloat16)
    la_all = [-_softplus(-(_dot(sm, w2) + bg)) * (1.0 / GLA_TAU) for sm in small]
    b_all = [_mask_dot(tri, la) for la in la_all]
    prob = [(b, h) for b in range(nb) for h in range(HEADS)]
    sl = [slice(h * HD, (h + 1) * HD) for h in range(HEADS)]
    kh = [k[b][:, sl[h]] for b, h in prob]
    vh = [v[b][:, sl[h]] for b, h in prob]
    st = [st_list[b][h] for b, h in prob]
    bc = [b_all[b][:, sl[h]] for b, h in prob]
    b_last = [jnp.sum(la_all[b][:, sl[h]], axis=0, keepdims=True) for b, h in prob]
    q_dec = [q[b][:, sl[h]] * (GLA_KEY ** -0.5) * jnp.exp(bc_) for (b, h), bc_ in zip(prob, bc)]
    attn = [jnp.where(causal, _dot_nt(qd, k_ * jnp.exp(-bc_)), 0.0) for qd, k_, bc_ in zip(q_dec, kh, bc)]
    inter = [_dot_nt(qd, st_) for qd, st_ in zip(q_dec, st)]
    o = [i_ + _dot(a_, v_) for i_, a_, v_ in zip(inter, attn, vh)]
    k_dec = [k_ * jnp.exp(bl - bc_) for k_, bl, bc_ in zip(kh, b_last, bc)]
    s_new = [st_ * jnp.exp(bl) + _dot_tn(v_, kd) for st_, bl, v_, kd in zip(st, b_last, vh, k_dec)]
    outs = [jnp.concatenate(o[b * HEADS:(b + 1) * HEADS], axis=-1) for b in range(nb)]
    return outs, [s_new[b * HEADS:(b + 1) * HEADS] for b in range(nb)]


def _dn_qkv(y):
    act = _silu(y)
    parts = []
    for i in range(2 * HEADS):
        xh = act[:, i * HD:(i + 1) * HD]
        xh = xh * lax.rsqrt(jnp.sum(xh * xh, axis=-1, keepdims=True) + EPS)
        parts.append(xh * (HD ** -0.5) if i < HEADS else xh)
    qk = jnp.concatenate(parts, axis=-1)
    return qk[:, :HEADS * HD], qk[:, HEADS * HD:], act[:, 2 * HEADS * HD:]


def _dn_gates(small, alog_row, dt_row):
    lane = _iota(small.shape, 1)
    log_a = -jnp.exp(alog_row) * _softplus(small + dt_row)
    return jnp.where(lane < SM_B, log_a, jnp.where(lane < SM_R, _sigmoid(small), 0.0))


def _gate_norm(o, z, grow):
    parts = []
    for h in range(HEADS):
        oh = o[:, h * HD:(h + 1) * HD]
        parts.append(oh * lax.rsqrt(jnp.mean(oh * oh, axis=-1, keepdims=True) + EPS))
    return jnp.concatenate(parts, axis=-1) * grow * _silu(z)


def _conv_rows(xrows, w_ref, k_taps):
    n = xrows.shape[0]
    acc = xrows * w_ref[k_taps - 1:k_taps, :]
    for s in range(1, k_taps):
        acc = acc + pltpu.roll(xrows, s, 0) * w_ref[k_taps - 1 - s:k_taps - s, :]
    return acc


def _shift_up(x, s):
    return x if s == 0 else pltpu.roll(x, x.shape[0] - s, 0)


def _div_tile(n, cap, mult=8):
    best = None
    for t in range(mult, min(n, cap) + 1, mult):
        if n % t == 0:
            best = t
    return best if best is not None else n


def _halo_prev(tt):
    return lambda b, t: (b, jnp.maximum(t * (tt // HALO) - 1, 0))


def _halo_next(tt, t_total):
    return lambda b, t: (b, jnp.minimum((t + 1) * (tt // HALO), t_total // HALO - 1))


def _mm(a, b, mode, out_dtype, name, tm=512, tn=512, tk=None):
    if mode == "nn":
        (m, k), n = a.shape, b.shape[1]
    elif mode == "nt":
        (m, k), n = a.shape, b.shape[0]
    else:
        (k, m), n = a.shape, b.shape[1]
    tm, tn = min(tm, m), min(tn, n)
    tk = k if tk is None else min(tk, k)
    assert m % tm == 0 and n % tn == 0 and k % tk == 0, (name, a.shape, b.shape, tm, tn, tk)
    nk = k // tk
    if mode == "tn":
        a_spec = pl.BlockSpec((tk, tm), lambda i, j, kk: (kk, i))
    else:
        a_spec = pl.BlockSpec((tm, tk), lambda i, j, kk: (i, kk))
    if mode == "nt":
        b_spec = pl.BlockSpec((tn, tk), lambda i, j, kk: (j, kk))
    else:
        b_spec = pl.BlockSpec((tk, tn), lambda i, j, kk: (kk, j))
    dims = {"nn": ((1,), (0,)), "nt": ((1,), (1,)), "tn": ((0,), (0,))}[mode]

    def body(a_ref, b_ref, o_ref, *acc):
        p = _dg(a_ref[...], b_ref[...], dims)
        if nk == 1:
            o_ref[...] = p.astype(out_dtype)
        else:
            kk = pl.program_id(2)

            @pl.when(kk == 0)
            def _():
                acc[0][...] = p

            @pl.when(kk > 0)
            def _():
                acc[0][...] += p

            @pl.when(kk == nk - 1)
            def _():
                o_ref[...] = acc[0][...].astype(out_dtype)

    return pl.pallas_call(
        body, name=name, grid=(m // tm, n // tn, nk),
        in_specs=[a_spec, b_spec],
        out_specs=pl.BlockSpec((tm, tn), lambda i, j, kk: (i, j)),
        out_shape=jax.ShapeDtypeStruct((m, n), out_dtype),
        scratch_shapes=[pltpu.VMEM((tm, tn), F32)] if nk > 1 else [],
        compiler_params=_params(("parallel", "parallel", "arbitrary")),
    )(a, b)


def _ada_fwd(c_all, w_ada, b_cols):
    def body(c_ref, w_ref, b_ref, o_ref):
        cond = _silu(c_ref[...]).astype(MXU_DT)
        o_ref[...] = _dot(cond, w_ref[...].astype(MXU_DT)) + b_ref[...]

    return pl.pallas_call(body, name="ada_fwd", out_shape=jax.ShapeDtypeStruct((c_all.shape[0], w_ada.shape[1]), F32),
                          compiler_params=_params())(c_all, w_ada, b_cols)


def _ada_bwd(c_all, dmod_all, dmod_cols):
    def body(c_ref, da_ref, dc_ref, gw_ref, gb_ref):
        cond = _silu(c_ref[...]).astype(MXU_DT)
        gw_ref[...] = _dot_tn(cond, dc_ref[...].astype(MXU_DT))
        gb_ref[...] = jnp.sum(da_ref[...], axis=0, keepdims=True)

    return pl.pallas_call(
        body, name="ada_bwd",
        out_shape=(jax.ShapeDtypeStruct((c_all.shape[1], dmod_cols.shape[1]), F32),
                   jax.ShapeDtypeStruct((1, dmod_all.shape[1]), F32)),
        compiler_params=_params())(c_all, dmod_all, dmod_cols)


def _tok_spec(tt, width=D):
    return pl.BlockSpec((1, tt, width), lambda b, t: (b, t, 0))


def _vec_spec(width=D):
    return pl.BlockSpec((1, width), lambda b, t: (0, 0))


def _bvec_spec(width=D):
    return pl.BlockSpec((1, 1, width), lambda b, t: (b, 0, 0))


def _ln0_mod(x, g0, b0, sc, sh):
    bsz, t_total, _ = x.shape
    tt = _div_tile(t_total, 256)

    def body(x_ref, g_ref, b_ref, sc_ref, sh_ref, h_ref):
        xh, _ = _ln_stats(x_ref[0])
        x0 = xh * g_ref[...] + b_ref[...]
        h_ref[0] = (x0 * (1.0 + sc_ref[0]) + sh_ref[0]).astype(MXU_DT)

    return pl.pallas_call(
        body, name="ln0_mod", grid=(bsz, t_total // tt),
        in_specs=[_tok_spec(tt), _vec_spec(), _vec_spec(), _bvec_spec(), _bvec_spec()],
        out_specs=_tok_spec(tt), out_shape=jax.ShapeDtypeStruct(x.shape, MXU_DT),
        compiler_params=_params(("parallel", "parallel")))(x, g0, b0, sc, sh)


def _res_ln_mod(x, y, gt, g0, b0, g1, b1, sc, sh):
    bsz, t_total, _ = x.shape
    tt = _div_tile(t_total, 256)

    def body(x_ref, y_ref, gt_ref, g0_ref, b0_ref, g1_ref, b1_ref, sc_ref, sh_ref, r_ref, h_ref):
        xh, _ = _ln_stats(x_ref[0])
        r = ALPHA * (xh * g0_ref[...] + b0_ref[...]) + (1.0 + gt_ref[0]) * y_ref[0]
        r_ref[0] = r
        rh, _ = _ln_stats(r)
        x1 = rh * g1_ref[...] + b1_ref[...]
        h_ref[0] = (x1 * (1.0 + sc_ref[0]) + sh_ref[0]).astype(MXU_DT)

    return pl.pallas_call(
        body, name="res_ln_mod", grid=(bsz, t_total // tt),
        in_specs=[_tok_spec(tt), _tok_spec(tt), _bvec_spec(), _vec_spec(), _vec_spec(), _vec_spec(), _vec_spec(),
                  _bvec_spec(), _bvec_spec()],
        out_specs=(_tok_spec(tt), _tok_spec(tt)),
        out_shape=(jax.ShapeDtypeStruct(x.shape, F32), jax.ShapeDtypeStruct(x.shape, MXU_DT)),
        compiler_params=_params(("parallel", "parallel")))(x, y, gt, g0, b0, g1, b1, sc, sh)


def _final_fwd_bwd(r1, y2, gt, g1, b1, g2, b2, target):
    bsz, t_total, _ = r1.shape
    tt = _div_tile(t_total, 256)

    def body(r1_ref, y2_ref, gt_ref, g1_ref, b1_ref, g2_ref, b2_ref, tg_ref,
             loss_ref, dr2_ref, dy2_ref, dgt_ref, dg2_ref, db2_ref):
        b, t = pl.program_id(0), pl.program_id(1)

        @pl.when((b == 0) & (t == 0))
        def _():
            loss_ref[...] = jnp.zeros_like(loss_ref)
            dg2_ref[...] = jnp.zeros_like(dg2_ref)
            db2_ref[...] = jnp.zeros_like(db2_ref)

        @pl.when(t == 0)
        def _():
            dgt_ref[...] = jnp.zeros_like(dgt_ref)

        rh1, _ = _ln_stats(r1_ref[0])
        x1 = rh1 * g1_ref[...] + b1_ref[...]
        y2 = y2_ref[0]
        gate = 1.0 + gt_ref[0]
        xh2, rstd2 = _ln_stats(ALPHA * x1 + gate * y2)
        err = xh2 * g2_ref[...] + b2_ref[...] - tg_ref[0]
        loss_ref[...] += jnp.sum(err * err, axis=0, keepdims=True)
        dx2 = err * (1.0 / D)
        dg2_ref[...] += jnp.sum(dx2 * xh2, axis=0, keepdims=True)
        db2_ref[...] += jnp.sum(dx2, axis=0, keepdims=True)
        dr2 = _ln_bwd(dx2 * g2_ref[...], xh2, rstd2)
        dr2_ref[0] = dr2
        dy2_ref[0] = (gate * dr2).astype(MXU_DT)
        dgt_ref[0] += jnp.sum(dr2 * y2, axis=0, keepdims=True)

    vec_out = jax.ShapeDtypeStruct((1, D), F32)
    return pl.pallas_call(
        body, name="final_fwd_bwd", grid=(bsz, t_total // tt),
        in_specs=[_tok_spec(tt), _tok_spec(tt), _bvec_spec(), _vec_spec(), _vec_spec(), _vec_spec(), _vec_spec(),
                  _tok_spec(tt)],
        out_specs=(_vec_spec(), _tok_spec(tt), _tok_spec(tt), _bvec_spec(), _vec_spec(), _vec_spec()),
        out_shape=(vec_out, jax.ShapeDtypeStruct(r1.shape, F32), jax.ShapeDtypeStruct(r1.shape, MXU_DT),
                   jax.ShapeDtypeStruct((bsz, 1, D), F32), vec_out, vec_out),
        compiler_params=_params(("arbitrary", "arbitrary")))(r1, y2, gt, g1, b1, g2, b2, target)


def _ln_bwd_call(name, d_res, d_h, src, g, b, sc, y=None, gt=None):
    bsz, t_total, _ = src.shape
    tt = _div_tile(t_total, 256)
    has_y = y is not None

    def body(*refs):
        if has_y:
            (dres_ref, dh_ref, src_ref, g_ref, b_ref, sc_ref, y_ref, gt_ref,
             dsrc_ref, dsc_ref, dsh_ref, dg_ref, db_ref, dy_ref, dgt_ref) = refs
        else:
            (dres_ref, dh_ref, src_ref, g_ref, b_ref, sc_ref,
             dsrc_ref, dsc_ref, dsh_ref, dg_ref, db_ref) = refs
        bi, t = pl.program_id(0), pl.program_id(1)

        @pl.when((bi == 0) & (t == 0))
        def _():
            dg_ref[...] = jnp.zeros_like(dg_ref)
            db_ref[...] = jnp.zeros_like(db_ref)

        @pl.when(t == 0)
        def _():
            dsc_ref[...] = jnp.zeros_like(dsc_ref)
            dsh_ref[...] = jnp.zeros_like(dsh_ref)
            if has_y:
                dgt_ref[...] = jnp.zeros_like(dgt_ref)

        xh, rstd = _ln_stats(src_ref[0])
        xv = xh * g_ref[...] + b_ref[...]
        dh = dh_ref[0]
        dx = ALPHA * dres_ref[0] + dh * (1.0 + sc_ref[0])
        dsc_ref[0] += jnp.sum(dh * xv, axis=0, keepdims=True)
        dsh_ref[0] += jnp.sum(dh, axis=0, keepdims=True)
        dg_ref[...] += jnp.sum(dx * xh, axis=0, keepdims=True)
        db_ref[...] += jnp.sum(dx, axis=0, keepdims=True)
        dsrc = _ln_bwd(dx * g_ref[...], xh, rstd)
        dsrc_ref[0] = dsrc
        if has_y:
            dy_ref[0] = ((1.0 + gt_ref[0]) * dsrc).astype(MXU_DT)
            dgt_ref[0] += jnp.sum(dsrc * y_ref[0], axis=0, keepdims=True)

    vec_out = jax.ShapeDtypeStruct((1, D), F32)
    bvec_out = jax.ShapeDtypeStruct((bsz, 1, D), F32)
    in_specs = [_tok_spec(tt), _tok_spec(tt), _tok_spec(tt), _vec_spec(), _vec_spec(), _bvec_spec()]
    out_specs = [_tok_spec(tt), _bvec_spec(), _bvec_spec(), _vec_spec(), _vec_spec()]
    out_shape = [jax.ShapeDtypeStruct(src.shape, F32), bvec_out, bvec_out, vec_out, vec_out]
    args = [d_res, d_h, src, g, b, sc]
    if has_y:
        in_specs += [_tok_spec(tt), _bvec_spec()]
        out_specs += [_tok_spec(tt), _bvec_spec()]
        out_shape += [jax.ShapeDtypeStruct(src.shape, MXU_DT), bvec_out]
        args += [y, gt]
    return pl.pallas_call(body, name=name, grid=(bsz, t_total // tt), in_specs=in_specs, out_specs=tuple(out_specs),
                          out_shape=tuple(out_shape), compiler_params=_params(("arbitrary", "arbitrary")))(*args)


FFN_TC = 256
FFN_NJ = D_FF // FFN_TC


def _ffn_act_fwd(up, cw, cb):
    bsz, t_total, _ = up.shape
    tt = _div_tile(t_total, 256)
    hp = _halo_prev(tt)

    def body(g_ref, gp_ref, v_ref, vp_ref, wg_ref, wv_ref, bg_ref, bv_ref, o_ref):
        first = pl.program_id(1) == 0

        def conv(prev_ref, tile_ref, w_ref, b_ref):
            prev = jnp.where(first, 0.0, prev_ref[0])
            rows = jnp.concatenate([prev, tile_ref[0]], axis=0)
            return _conv_rows(rows, w_ref, FFN_CONV_K)[HALO:] + b_ref[...]

        o_ref[0] = (_silu(conv(gp_ref, g_ref, wg_ref, bg_ref)) * conv(vp_ref, v_ref, wv_ref, bv_ref)).astype(MXU_DT)

    def tile(off):
        return pl.BlockSpec((1, tt, FFN_TC), lambda b, t, j: (b, t, j + off))

    def halo(off):
        return pl.BlockSpec((1, HALO, FFN_TC), lambda b, t, j: (*hp(b, t), j + off))

    def wspec(rows, off):
        return pl.BlockSpec((rows, FFN_TC), lambda b, t, j: (0, j + off))

    return pl.pallas_call(
        body, name="ffn_act_fwd", grid=(bsz, t_total // tt, FFN_NJ),
        in_specs=[tile(0), halo(0), tile(FFN_NJ), halo(FFN_NJ), wspec(FFN_CONV_K, 0), wspec(FFN_CONV_K, FFN_NJ),
                  wspec(1, 0), wspec(1, FFN_NJ)],
        out_specs=pl.BlockSpec((1, tt, FFN_TC), lambda b, t, j: (b, t, j)),
        out_shape=jax.ShapeDtypeStruct((bsz, t_total, D_FF), MXU_DT),
        compiler_params=_params(("parallel", "parallel", "parallel")))(up, up, up, up, cw, cw, cb, cb)


def _ffn_act_bwd(up, da, cw, cb):
    bsz, t_total, width = up.shape
    tt = _div_tile(t_total, 256)
    nt = t_total // tt
    hp, hn = _halo_prev(tt), _halo_next(tt, t_total)
    nj2 = 2 * FFN_NJ

    def body(xo_ref, xop_ref, xon_ref, xp_ref, xpp_ref, xpn_ref, da_ref, dan_ref, wo_ref, wp_ref, bo_ref, bp_ref,
             dup_ref, dw_ref, db_ref):
        j, b, t = pl.program_id(0), pl.program_id(1), pl.program_id(2)

        @pl.when((b == 0) & (t == 0))
        def _():
            dw_ref[...] = jnp.zeros_like(dw_ref)
            db_ref[...] = jnp.zeros_like(db_ref)

        def rows_of(prev_ref, tile_ref, next_ref):
            prev = jnp.where(t == 0, 0.0, prev_ref[0])
            return jnp.concatenate([prev, tile_ref[0], next_ref[0]], axis=0)

        x_own = rows_of(xop_ref, xo_ref, xon_ref)
        u_own = _conv_rows(x_own, wo_ref, FFN_CONV_K)[HALO:] + bo_ref[...]
        u_par = _conv_rows(rows_of(xpp_ref, xp_ref, xpn_ref), wp_ref, FFN_CONV_K)[HALO:] + bp_ref[...]
        da_ext = jnp.concatenate([da_ref[0], dan_ref[0]], axis=0)
        is_gate = j < FFN_NJ
        g_pre = jnp.where(is_gate, u_own, u_par)
        v_pre = jnp.where(is_gate, u_par, u_own)
        sg = _sigmoid(g_pre)
        du = da_ext * jnp.where(is_gate, v_pre * sg * (1.0 + g_pre * (1.0 - sg)), g_pre * sg)
        valid = (_iota(du.shape, 0) < tt) | (t < nt - 1)
        du = jnp.where(valid, du, 0.0)
        dup = du * wo_ref[FFN_CONV_K - 1:FFN_CONV_K, :]
        for s in range(1, FFN_CONV_K):
            dup = dup + _shift_up(du, s) * wo_ref[FFN_CONV_K - 1 - s:FFN_CONV_K - s, :]
        dup_ref[0] = dup[:tt].astype(MXU_DT)
        du_t = du[:tt]
        db_ref[...] += jnp.sum(du_t, axis=0, keepdims=True)
        for k in range(FFN_CONV_K):
            s = FFN_CONV_K - 1 - k
            xs = (x_own if s == 0 else pltpu.roll(x_own, s, 0))[HALO:HALO + tt]
            dw_ref[k:k + 1, :] += jnp.sum(du_t * xs, axis=0, keepdims=True)

    def partner(j):
        return (j + FFN_NJ) % nj2

    def tile(fn):
        return pl.BlockSpec((1, tt, FFN_TC), lambda j, b, t: (b, t, fn(j)))

    def halo(h, fn):
        return pl.BlockSpec((1, HALO, FFN_TC), lambda j, b, t: (*h(b, t), fn(j)))

    def wspec(rows, fn):
        return pl.BlockSpec((rows, FFN_TC), lambda j, b, t: (0, fn(j)))

    own = lambda j: j
    half = lambda j: j % FFN_NJ
    return pl.pallas_call(
        body, name="ffn_act_bwd", grid=(nj2, bsz, nt),
        in_specs=[tile(own), halo(hp, own), halo(hn, own), tile(partner), halo(hp, partner), halo(hn, partner),
                  tile(half), halo(hn, half), wspec(FFN_CONV_K, own), wspec(FFN_CONV_K, partner),
                  wspec(1, own), wspec(1, partner)],
        out_specs=(tile(own), wspec(FFN_CONV_K, own), wspec(1, own)),
        out_shape=(jax.ShapeDtypeStruct(up.shape, MXU_DT), jax.ShapeDtypeStruct((FFN_CONV_K, width), F32),
                   jax.ShapeDtypeStruct((1, width), F32)),
        compiler_params=_params(("arbitrary", "arbitrary", "arbitrary")))(
            up, up, up, up, up, up, da, da, cw, cw, cb, cb)


QKV_W = 3 * HEADS * HD
SM_BLK = P_SM // 128


def _dn_pre_fwd(proj, conv_w, alog_row, dt_row):
    bsz, t_total, _ = proj.shape
    tt = _div_tile(t_total, 256)
    hp = _halo_prev(tt)

    def body(x_ref, xp_ref, sm_ref, w_ref, al_ref, dt_ref, q_ref, k_ref, v_ref, g_ref):
        prev = jnp.where(pl.program_id(1) == 0, 0.0, xp_ref[0])
        y = _conv_rows(jnp.concatenate([prev, x_ref[0]], axis=0), w_ref, DN_CONV_K)[HALO:]
        q_ref[0], k_ref[0], v_ref[0] = _dn_qkv(y)
        g_ref[0] = _dn_gates(sm_ref[0], al_ref[...], dt_ref[...])

    out512 = jax.ShapeDtypeStruct((bsz, t_total, HEADS * HD), F32)
    return pl.pallas_call(
        body, name="dn_pre_fwd", grid=(bsz, t_total // tt),
        in_specs=[pl.BlockSpec((1, tt, QKV_W), lambda b, t: (b, t, 0)),
                  pl.BlockSpec((1, HALO, QKV_W), lambda b, t: (*hp(b, t), 0)),
                  pl.BlockSpec((1, tt, 128), lambda b, t: (b, t, SM_BLK)),
                  pl.BlockSpec((DN_CONV_K, QKV_W), lambda b, t: (0, 0)), _vec_spec(128), _vec_spec(128)],
        out_specs=(_tok_spec(tt, 512), _tok_spec(tt, 512), _tok_spec(tt, 512), _tok_spec(tt, 128)),
        out_shape=(out512, out512, out512, jax.ShapeDtypeStruct((bsz, t_total, 128), F32)),
        compiler_params=_params(("parallel", "parallel")))(proj, proj, proj, conv_w, alog_row, dt_row)


def _dn_pre_bwd(proj, dq, dk, dv, dgates, conv_w, alog_row, dt_row):
    bsz, t_total, _ = proj.shape
    tt = _div_tile(t_total, 128)
    nt = t_total // tt
    hp, hn = _halo_prev(tt), _halo_next(tt, t_total)

    def body(x_ref, xp_ref, xn_ref, sm_ref, dq_ref, dqn_ref, dk_ref, dkn_ref, dv_ref, dvn_ref, dg_ref,
             w_ref, al_ref, dt_ref, dx_ref, dsm_ref, dw_ref, dal_ref, ddt_ref):
        b, t = pl.program_id(0), pl.program_id(1)

        @pl.when((b == 0) & (t == 0))
        def _():
            dw_ref[...] = jnp.zeros_like(dw_ref)
            dal_ref[...] = jnp.zeros_like(dal_ref)
            ddt_ref[...] = jnp.zeros_like(ddt_ref)

        prev = jnp.where(t == 0, 0.0, xp_ref[0])
        rows = jnp.concatenate([prev, x_ref[0], xn_ref[0]], axis=0)
        y = _conv_rows(rows, w_ref, DN_CONV_K)[HALO:]
        valid = (_iota((tt + HALO, 1), 0) < tt) | (t < nt - 1)

        def ext(tile_ref, next_ref):
            return jnp.where(valid, jnp.concatenate([tile_ref[0], next_ref[0]], axis=0), 0.0)

        _, vjp_qkv = jax.vjp(_dn_qkv, y)
        (dy,) = vjp_qkv((ext(dq_ref, dqn_ref), ext(dk_ref, dkn_ref), ext(dv_ref, dvn_ref)))
        dy = jnp.where(valid, dy, 0.0)
        dx = dy * w_ref[DN_CONV_K - 1:DN_CONV_K, :]
        for s in range(1, DN_CONV_K):
            dx = dx + _shift_up(dy, s) * w_ref[DN_CONV_K - 1 - s:DN_CONV_K - s, :]
        dx_ref[0] = dx[:tt].astype(MXU_DT)
        dy_t = dy[:tt]
        for k in range(DN_CONV_K):
            s = DN_CONV_K - 1 - k
            xs = (rows if s == 0 else pltpu.roll(rows, s, 0))[HALO:HALO + tt]
            dw_ref[k:k + 1, :] += jnp.sum(dy_t * xs, axis=0, keepdims=True)
        _, vjp_g = jax.vjp(_dn_gates, sm_ref[0], al_ref[...], dt_ref[...])
        dsm, dal, ddt = vjp_g(dg_ref[0])
        dsm_ref[0] = dsm
        dal_ref[...] += dal
        ddt_ref[...] += ddt

    def tile(width, blk=0):
        return pl.BlockSpec((1, tt, width), lambda b, t: (b, t, blk))

    def halo(h, width):
        return pl.BlockSpec((1, HALO, width), lambda b, t: (*h(b, t), 0))

    return pl.pallas_call(
        body, name="dn_pre_bwd", grid=(bsz, nt),
        in_specs=[tile(QKV_W), halo(hp, QKV_W), halo(hn, QKV_W), tile(128, SM_BLK),
                  tile(512), halo(hn, 512), tile(512), halo(hn, 512), tile(512), halo(hn, 512), tile(128),
                  pl.BlockSpec((DN_CONV_K, QKV_W), lambda b, t: (0, 0)), _vec_spec(128), _vec_spec(128)],
        out_specs=(tile(QKV_W), tile(128), pl.BlockSpec((DN_CONV_K, QKV_W), lambda b, t: (0, 0)),
                   _vec_spec(128), _vec_spec(128)),
        out_shape=(jax.ShapeDtypeStruct((bsz, t_total, QKV_W), MXU_DT), jax.ShapeDtypeStruct((bsz, t_total, 128), F32),
                   jax.ShapeDtypeStruct((DN_CONV_K, QKV_W), F32), jax.ShapeDtypeStruct((1, 128), F32),
                   jax.ShapeDtypeStruct((1, 128), F32)),
        compiler_params=_params(("arbitrary", "arbitrary")))(
            proj, proj, proj, proj, dq, dq, dk, dk, dv, dv, dgates, conv_w, alog_row, dt_row)


def _state_spec(bsz, idx):
    return pl.BlockSpec((bsz, 1, HEADS, HD, HD), lambda c: (0, idx(c), 0, 0, 0))


def _chunk_spec(bsz, width, idx, blk=0):
    return pl.BlockSpec((bsz, CHUNK, width), lambda c: (0, idx(c), blk))


def _dn_rec_fwd(q, k, v, gates):
    bsz, t_total, _ = q.shape
    nc = t_total // CHUNK
    fwd = lambda c: c

    def body(q_ref, k_ref, v_ref, g_ref, o_ref, ss_ref, s_ref):
        @pl.when(pl.program_id(0) == 0)
        def _():
            s_ref[...] = jnp.zeros_like(s_ref)

        seqs = range(bsz)
        s_list = [[s_ref[b * HEADS + h] for h in range(HEADS)] for b in seqs]
        for b in seqs:
            for h in range(HEADS):
                ss_ref[b, 0, h] = s_list[b][h]
        o, new_s = _dn_chunk(s_list, [q_ref[b] for b in seqs], [k_ref[b] for b in seqs],
                             [v_ref[b] for b in seqs], [g_ref[b] for b in seqs])
        for b in seqs:
            o_ref[b] = o[b]
            for h in range(HEADS):
                s_ref[b * HEADS + h] = new_s[b][h]

    return pl.pallas_call(
        body, name="dn_rec_fwd", grid=(nc,),
        in_specs=[_chunk_spec(bsz, 512, fwd)] * 3 + [_chunk_spec(bsz, 128, fwd)],
        out_specs=(_chunk_spec(bsz, 512, fwd), _state_spec(bsz, fwd)),
        out_shape=(jax.ShapeDtypeStruct(q.shape, F32), jax.ShapeDtypeStruct((bsz, nc, HEADS, HD, HD), F32)),
        scratch_shapes=[pltpu.VMEM((bsz * HEADS, HD, HD), F32)],
        compiler_params=_params(("arbitrary",)))(q, k, v, gates)


def _dn_rec_bwd(q, k, v, gates, states, do):
    bsz, t_total, _ = q.shape
    nc = t_total // CHUNK
    rev = lambda c: nc - 1 - c

    def body(q_ref, k_ref, v_ref, g_ref, ss_ref, do_ref, dq_ref, dk_ref, dv_ref, dg_ref, ds_ref):
        @pl.when(pl.program_id(0) == 0)
        def _():
            ds_ref[...] = jnp.zeros_like(ds_ref)

        seqs = range(bsz)
        s_list = [[ss_ref[b, 0, h] for h in range(HEADS)] for b in seqs]
        _, vjp = jax.vjp(_dn_chunk, s_list, [q_ref[b] for b in seqs], [k_ref[b] for b in seqs],
                         [v_ref[b] for b in seqs], [g_ref[b] for b in seqs])
        ds_in, dq, dk, dv, dg = vjp(([do_ref[b] for b in seqs],
                                     [[ds_ref[b * HEADS + h] for h in range(HEADS)] for b in seqs]))
        for b in seqs:
            dq_ref[b], dk_ref[b], dv_ref[b], dg_ref[b] = dq[b], dk[b], dv[b], dg[b]
            for h in range(HEADS):
                ds_ref[b * HEADS + h] = ds_in[b][h]

    tok = lambda width: _chunk_spec(bsz, width, rev)
    out512 = jax.ShapeDtypeStruct(q.shape, F32)
    return pl.pallas_call(
        body, name="dn_rec_bwd", grid=(nc,),
        in_specs=[tok(512), tok(512), tok(512), tok(128), _state_spec(bsz, rev), tok(512)],
        out_specs=(tok(512), tok(512), tok(512), tok(128)),
        out_shape=(out512, out512, out512, jax.ShapeDtypeStruct(gates.shape, F32)),
        scratch_shapes=[pltpu.VMEM((bsz * HEADS, HD, HD), F32)],
        compiler_params=_params(("arbitrary",)))(q, k, v, gates, states, do)


GQ_BLK, GK_BLK, GV_BLK = P_GQ // 512, P_GK // 512, P_GV // 512


def _gla_rec_fwd(proj, w2, bg):
    bsz, t_total, _ = proj.shape
    nc = t_total // CHUNK

    fwd = lambda c: c

    def body(q_ref, k_ref, v_ref, sm_ref, w2_ref, bg_ref, o_ref, ss_ref, s_ref):
        @pl.when(pl.program_id(0) == 0)
        def _():
            s_ref[...] = jnp.zeros_like(s_ref)

        seqs = range(bsz)
        s_list = [[s_ref[b * HEADS + h] for h in range(HEADS)] for b in seqs]
        for b in seqs:
            for h in range(HEADS):
                ss_ref[b, 0, h] = s_list[b][h]
        o, new_s = _gla_chunk(s_list, [q_ref[b] for b in seqs], [k_ref[b] for b in seqs], [v_ref[b] for b in seqs],
                              [sm_ref[b] for b in seqs], w2_ref[...], bg_ref[...])
        for b in seqs:
            o_ref[b] = o[b]
            for h in range(HEADS):
                s_ref[b * HEADS + h] = new_s[b][h]

    col = lambda blk, width=512: _chunk_spec(bsz, width, fwd, blk)
    return pl.pallas_call(
        body, name="gla_rec_fwd", grid=(nc,),
        in_specs=[col(GQ_BLK), col(GK_BLK), col(GV_BLK), col(SM_BLK, 128),
                  pl.BlockSpec((128, 512), lambda c: (0, 0)), pl.BlockSpec((1, 512), lambda c: (0, 0))],
        out_specs=(col(0), _state_spec(bsz, fwd)),
        out_shape=(jax.ShapeDtypeStruct((bsz, t_total, 512), F32),
                   jax.ShapeDtypeStruct((bsz, nc, HEADS, HD, HD), F32)),
        scratch_shapes=[pltpu.VMEM((bsz * HEADS, HD, HD), F32)],
        compiler_params=_params(("arbitrary",)))(proj, proj, proj, proj, w2, bg)


def _gla_rec_bwd(proj, w2, bg, states, do, dsm_dn):
    bsz, t_total, _ = proj.shape
    nc = t_total // CHUNK
    rev = lambda c: nc - 1 - c

    def body(q_ref, k_ref, v_ref, sm_ref, w2_ref, bg_ref, ss_ref, do_ref, dsd_ref,
             dq_ref, dk_ref, dv_ref, dsm_ref, dw2_ref, dbg_ref, ds_ref):
        @pl.when(pl.program_id(0) == 0)
        def _():
            dw2_ref[...] = jnp.zeros_like(dw2_ref)
            dbg_ref[...] = jnp.zeros_like(dbg_ref)
            ds_ref[...] = jnp.zeros_like(ds_ref)

        seqs = range(bsz)
        s_list = [[ss_ref[b, 0, h] for h in range(HEADS)] for b in seqs]
        _, vjp = jax.vjp(_gla_chunk, s_list, [q_ref[b] for b in seqs], [k_ref[b] for b in seqs],
                         [v_ref[b] for b in seqs], [sm_ref[b] for b in seqs], w2_ref[...], bg_ref[...])
        ds_in, dq, dk, dv, dsm, dw2, dbg = vjp(([do_ref[b] for b in seqs],
                                                [[ds_ref[b * HEADS + h] for h in range(HEADS)] for b in seqs]))
        for b in seqs:
            dq_ref[b], dk_ref[b], dv_ref[b] = dq[b].astype(MXU_DT), dk[b].astype(MXU_DT), dv[b].astype(MXU_DT)
            dsm_ref[b] = (dsm[b] + dsd_ref[b]).astype(MXU_DT)
            for h in range(HEADS):
                ds_ref[b * HEADS + h] = ds_in[b][h]
        dw2_ref[...] += dw2
        dbg_ref[...] += dbg

    col = lambda blk, width=512: _chunk_spec(bsz, width, rev, blk)
    w2_spec = pl.BlockSpec((128, 512), lambda c: (0, 0))
    bg_spec = pl.BlockSpec((1, 512), lambda c: (0, 0))
    out512 = jax.ShapeDtypeStruct((bsz, t_total, 512), MXU_DT)
    return pl.pallas_call(
        body, name="gla_rec_bwd", grid=(nc,),
        in_specs=[col(GQ_BLK), col(GK_BLK), col(GV_BLK), col(SM_BLK, 128), w2_spec, bg_spec,
                  _state_spec(bsz, rev), col(0), col(0, 128)],
        out_specs=(col(0), col(0), col(0), col(0, 128), w2_spec, bg_spec),
        out_shape=(out512, out512, out512, jax.ShapeDtypeStruct((bsz, t_total, 128), MXU_DT),
                   jax.ShapeDtypeStruct((128, 512), F32), jax.ShapeDtypeStruct((1, 512), F32)),
        scratch_shapes=[pltpu.VMEM((bsz * HEADS, HD, HD), F32)],
        compiler_params=_params(("arbitrary",)))(proj, proj, proj, proj, w2, bg, states, do, dsm_dn)


Z_BLK, GG_BLK = P_Z // 512, P_GG // 512


def _mix_out_fwd(o_dn, o_gla, proj, grow_dn, grow_gla):
    bsz, t_total, _ = o_dn.shape
    tt = _div_tile(t_total, 256)

    def body(od_ref, og_ref, z_ref, gg_ref, gd_ref, gl_ref, o_ref):
        o_ref[0, :, :512] = _gate_norm(od_ref[0], z_ref[0], gd_ref[...]).astype(MXU_DT)
        o_ref[0, :, 512:] = _gate_norm(og_ref[0], gg_ref[0], gl_ref[...]).astype(MXU_DT)

    def col(blk):
        return pl.BlockSpec((1, tt, 512), lambda b, t: (b, t, blk))

    return pl.pallas_call(
        body, name="mix_out_fwd", grid=(bsz, t_total // tt),
        in_specs=[col(0), col(0), col(Z_BLK), col(GG_BLK), _vec_spec(512), _vec_spec(512)],
        out_specs=_tok_spec(tt), out_shape=jax.ShapeDtypeStruct((bsz, t_total, D), MXU_DT),
        compiler_params=_params(("parallel", "parallel")))(o_dn, o_gla, proj, proj, grow_dn, grow_gla)


def _mix_out_bwd(do, o_dn, o_gla, proj, grow_dn, grow_gla):
    bsz, t_total, _ = o_dn.shape
    tt = _div_tile(t_total, 256)

    def body(do_ref, od_ref, og_ref, z_ref, gg_ref, gd_ref, gl_ref,
             dod_ref, dog_ref, dz_ref, dgg_ref, dgd_ref, dgl_ref):
        @pl.when((pl.program_id(0) == 0) & (pl.program_id(1) == 0))
        def _():
            dgd_ref[...] = jnp.zeros_like(dgd_ref)
            dgl_ref[...] = jnp.zeros_like(dgl_ref)

        def one(o_ref, gate_ref, g_ref, ct, do_out, dgate_out, dg_out):
            _, vjp = jax.vjp(_gate_norm, o_ref[0], gate_ref[0], g_ref[...])
            d_o, d_gate, d_row = vjp(ct)
            do_out[0] = d_o
            dgate_out[0] = d_gate.astype(MXU_DT)
            acc = d_row[:, :HD]
            for h in range(1, HEADS):
                acc = acc + d_row[:, h * HD:(h + 1) * HD]
            dg_out[...] += acc

        ct = do_ref[0]
        one(od_ref, z_ref, gd_ref, ct[:, :512], dod_ref, dz_ref, dgd_ref)
        one(og_ref, gg_ref, gl_ref, ct[:, 512:], dog_ref, dgg_ref, dgl_ref)

    def col(blk):
        return pl.BlockSpec((1, tt, 512), lambda b, t: (b, t, blk))

    f512 = jax.ShapeDtypeStruct((bsz, t_total, 512), F32)
    b512 = jax.ShapeDtypeStruct((bsz, t_total, 512), MXU_DT)
    g128 = jax.ShapeDtypeStruct((1, HD), F32)
    return pl.pallas_call(
        body, name="mix_out_bwd", grid=(bsz, t_total // tt),
        in_specs=[_tok_spec(tt), col(0), col(0), col(Z_BLK), col(GG_BLK), _vec_spec(512), _vec_spec(512)],
        out_specs=(col(0), col(0), col(0), col(0), _vec_spec(HD), _vec_spec(HD)),
        out_shape=(f512, f512, b512, b512, g128, g128),
        compiler_params=_params(("arbitrary", "arbitrary")))(do, o_dn, o_gla, proj, proj, grow_dn, grow_gla)


def _sum_slots(x, name):
    n, rows, cols = x.shape
    tr = _div_tile(rows, max(8, (1 << 19) // cols))

    def body(x_ref, o_ref):
        acc = x_ref[0]
        for i in range(1, n):
            acc = acc + x_ref[i]
        o_ref[...] = acc

    return pl.pallas_call(
        body, name=name, grid=(rows // tr,),
        in_specs=[pl.BlockSpec((n, tr, cols), lambda i: (0, i, 0))],
        out_specs=pl.BlockSpec((tr, cols), lambda i: (i, 0)),
        out_shape=jax.ShapeDtypeStruct((rows, cols), F32), compiler_params=_params(("parallel",)))(x)


def _pair_add(g8, r1, core, name):
    _, rows, cols = g8.shape
    tr = _div_tile(rows, max(8, (1 << 19) // cols))
    g42 = g8.reshape(4, 2, rows, cols)

    def body(core_ref, g_ref, r_ref, o_ref):
        o_ref[0] = g_ref[0, 0] + r_ref[0]

    return pl.pallas_call(
        body, name=name,
        grid_spec=pltpu.PrefetchScalarGridSpec(
            num_scalar_prefetch=1, grid=(4, rows // tr),
            in_specs=[pl.BlockSpec((1, 1, tr, cols), lambda s, i, core_ref: (s, core_ref[0], i, 0)),
                      pl.BlockSpec((1, tr, cols), lambda s, i, core_ref: (s, i, 0))],
            out_specs=pl.BlockSpec((1, tr, cols), lambda s, i, core_ref: (s, i, 0))),
        out_shape=jax.ShapeDtypeStruct((4, rows, cols), F32),
        compiler_params=_params(("parallel", "parallel")))(core, g42, r1)


def _chip_add(p4, r2, chip, name):
    _, rows, cols = p4.shape
    tr = _div_tile(rows, max(8, (1 << 19) // cols))

    def body(chip_ref, p_ref, r_ref, o_ref):
        o_ref[...] = ((p_ref[0] + r_ref[0]) + r_ref[1]) + r_ref[2]

    return pl.pallas_call(
        body, name=name,
        grid_spec=pltpu.PrefetchScalarGridSpec(
            num_scalar_prefetch=1, grid=(rows // tr,),
            in_specs=[pl.BlockSpec((1, tr, cols), lambda i, chip_ref: (chip_ref[0], i, 0)),
                      pl.BlockSpec((3, tr, cols), lambda i, chip_ref: (0, i, 0))],
            out_specs=pl.BlockSpec((tr, cols), lambda i, chip_ref: (i, 0))),
        out_shape=jax.ShapeDtypeStruct((rows, cols), F32),
        compiler_params=_params(("parallel",)))(chip, p4, r2)


def _adamw(w, g, m, v, name):
    rows, cols = w.shape
    tr = _div_tile(rows, max(8, (1 << 18) // cols))

    def body(w_ref, g_ref, m_ref, v_ref, d_ref, nm_ref, nv_ref):
        g_ = g_ref[...]
        nm = ADAM_B1 * m_ref[...] + (1.0 - ADAM_B1) * g_
        nv = ADAM_B2 * v_ref[...] + (1.0 - ADAM_B2) * (g_ * g_)
        m_hat = nm / (1.0 - ADAM_B1 ** ADAM_STEP)
        v_hat = nv / (1.0 - ADAM_B2 ** ADAM_STEP)
        d_ref[...] = -ADAM_LR * (m_hat / (jnp.sqrt(v_hat) + ADAM_EPS) + ADAM_WD * w_ref[...])
        nm_ref[...] = nm
        nv_ref[...] = nv

    spec = pl.BlockSpec((tr, cols), lambda i: (i, 0))
    shp = jax.ShapeDtypeStruct((rows, cols), F32)
    return pl.pallas_call(body, name=name, grid=(rows // tr,), in_specs=[spec] * 4, out_specs=(spec,) * 3,
                          out_shape=(shp,) * 3, compiler_params=_params(("parallel",)))(w, g, m, v)


def _position():
    return lax.axis_index("x"), lax.axis_index("y"), lax.axis_index("c")


def _slot(px, py, pc):
    return 4 * px + 2 * py + pc


def _gather_small(x, name):
    rows, cols = x.shape

    def body(x_ref, o_ref, send_sems, recv_sems):
        mx, my, mc = _position()

        def peer(k):
            return (mx ^ ((k >> 2) & 1), my ^ ((k >> 1) & 1), mc ^ (k & 1))

        o_ref[_slot(mx, my, mc)] = x_ref[...]
        sends = []
        for k in range(1, N_DEV):
            cp = pltpu.make_async_remote_copy(src_ref=x_ref, dst_ref=o_ref.at[_slot(mx, my, mc)],
                                              send_sem=send_sems.at[k - 1], recv_sem=recv_sems.at[k - 1],
                                              device_id=peer(k), device_id_type=MESH)
            cp.start()
            sends.append(cp)
        for k in range(1, N_DEV):
            pltpu.make_async_remote_copy(src_ref=x_ref, dst_ref=o_ref.at[_slot(*peer(k))],
                                         send_sem=send_sems.at[k - 1], recv_sem=recv_sems.at[k - 1],
                                         device_id=peer(k), device_id_type=MESH).wait_recv()
        for cp in sends:
            cp.wait_send()

    return pl.pallas_call(
        body, name=name, out_shape=jax.ShapeDtypeStruct((N_DEV, rows, cols), x.dtype),
        in_specs=[pl.BlockSpec(memory_space=pltpu.VMEM)], out_specs=pl.BlockSpec(memory_space=pltpu.VMEM),
        scratch_shapes=[pltpu.SemaphoreType.DMA((N_DEV - 1,)), pltpu.SemaphoreType.DMA((N_DEV - 1,))],
        compiler_params=pltpu.CompilerParams(vmem_limit_bytes=VMEM_LIMIT_V7X))(x)


def _gather_big(shards):
    n = len(shards)

    def body(*refs):
        xs, outs = refs[:n], refs[n:2 * n]
        send_sems, recv_sems, local_sems = refs[2 * n:]
        mx, my, mc = _position()
        me, sibling = (mx, my, mc), (mx, my, 1 - mc)
        chips = [(1 - mx, my), (mx, 1 - my), (1 - mx, 1 - my)]

        def copy(a, k, block, to, src=None):
            dst = outs[a].at[_slot(*block)]
            return pltpu.make_async_remote_copy(src_ref=dst if src is None else src, dst_ref=dst,
                                                send_sem=send_sems.at[7 * a + k], recv_sem=recv_sems.at[7 * a + k],
                                                device_id=to, device_id_type=MESH)

        mine = [pltpu.make_async_copy(xs[a], outs[a].at[_slot(*me)], local_sems.at[a]) for a in range(n)]
        for cp in mine:
            cp.start()
        started = []
        for a in range(n):
            started.append(copy(a, 0, me, sibling, src=xs[a]))
            started += [copy(a, 1 + j, me, (*chip, mc), src=xs[a]) for j, chip in enumerate(chips)]
        for cp in started:
            cp.start()
        for j, chip in enumerate(chips):
            for a in range(n):
                copy(a, 1 + j, (*chip, mc), me).wait_recv()
                fwd = copy(a, 4 + j, (*chip, mc), sibling)
                fwd.start()
                started.append(fwd)
        for a in range(n):
            copy(a, 0, sibling, me).wait_recv()
            for j, chip in enumerate(chips):
                copy(a, 4 + j, (*chip, 1 - mc), me).wait_recv()
        for cp in started:
            cp.wait_send()
        for cp in mine:
            cp.wait()

    any_spec = pl.BlockSpec(memory_space=pl.ANY)
    return pl.pallas_call(
        body, name="gather_weights",
        out_shape=tuple(jax.ShapeDtypeStruct((N_DEV,) + s.shape, s.dtype) for s in shards),
        in_specs=[any_spec] * n, out_specs=(any_spec,) * n,
        scratch_shapes=[pltpu.SemaphoreType.DMA((7 * n,)), pltpu.SemaphoreType.DMA((7 * n,)),
                        pltpu.SemaphoreType.DMA((n,))])(*shards)


def _scatter_sibling(grads):
    n = len(grads)

    def body(*refs):
        gs, outs = refs[:n], refs[n:2 * n]
        send_sems, recv_sems = refs[2 * n:]
        mx, my, mc = _position()
        copies = []
        for a in range(n):
            for s in range(4):
                copies.append(pltpu.make_async_remote_copy(
                    src_ref=gs[a].at[2 * s + (1 - mc)], dst_ref=outs[a].at[s],
                    send_sem=send_sems.at[4 * a + s], recv_sem=recv_sems.at[4 * a + s],
                    device_id=(mx, my, 1 - mc), device_id_type=MESH))
        for cp in copies:
            cp.start()
        for cp in copies:
            cp.wait_recv()
        for cp in copies:
            cp.wait_send()

    any_spec = pl.BlockSpec(memory_space=pl.ANY)
    return pl.pallas_call(
        body, name="scatter_sibling",
        out_shape=tuple(jax.ShapeDtypeStruct((4,) + g.shape[1:], g.dtype) for g in grads),
        in_specs=[any_spec] * n, out_specs=(any_spec,) * n,
        scratch_shapes=[pltpu.SemaphoreType.DMA((4 * n,)), pltpu.SemaphoreType.DMA((4 * n,))])(*grads)


def _scatter_chips(sums):
    n = len(sums)

    def body(*refs):
        ps, outs = refs[:n], refs[n:2 * n]
        send_sems, recv_sems = refs[2 * n:]
        mx, my, mc = _position()
        chips = [(1 - mx, my), (mx, 1 - my), (1 - mx, 1 - my)]
        copies = []
        for a in range(n):
            for k, (cx, cy) in enumerate(chips):
                copies.append(pltpu.make_async_remote_copy(
                    src_ref=ps[a].at[2 * cx + cy], dst_ref=outs[a].at[k],
                    send_sem=send_sems.at[3 * a + k], recv_sem=recv_sems.at[3 * a + k],
                    device_id=(cx, cy, mc), device_id_type=MESH))
        for cp in copies:
            cp.start()
        for cp in copies:
            cp.wait_recv()
        for cp in copies:
            cp.wait_send()

    any_spec = pl.BlockSpec(memory_space=pl.ANY)
    return pl.pallas_call(
        body, name="scatter_chips",
        out_shape=tuple(jax.ShapeDtypeStruct((3,) + p.shape[1:], p.dtype) for p in sums),
        in_specs=[any_spec] * n, out_specs=(any_spec,) * n,
        scratch_shapes=[pltpu.SemaphoreType.DMA((3 * n,)), pltpu.SemaphoreType.DMA((3 * n,))])(*sums)


def _pad_heads(x, axis):
    shp = list(x.shape)
    x4 = x.reshape(shp[:axis] + [HEADS, GLA_KEY] + shp[axis + 1:])
    pad = [(0, 0)] * x4.ndim
    pad[axis + 1] = (0, HD - GLA_KEY)
    return jnp.pad(x4, pad).reshape(shp[:axis] + [HEADS * HD] + shp[axis + 1:])


def _unpad_heads(x, axis):
    shp = list(x.shape)
    x4 = x.reshape(shp[:axis] + [HEADS, HD] + shp[axis + 1:])
    x4 = lax.slice_in_dim(x4, 0, GLA_KEY, axis=axis + 1)
    return x4.reshape(shp[:axis] + [HEADS * GLA_KEY] + shp[axis + 1:])


O_Z_END, O_AB, O_GQ, O_GK, O_GV, O_R = 2048, 2048, 2056, 2312, 2568, 3592


def _pad_in_rows(wt):
    return jnp.concatenate([
        wt[:O_Z_END], _pad_heads(wt[O_GQ:O_GK], 0), _pad_heads(wt[O_GK:O_GV], 0), wt[O_GV:O_R],
        wt[O_AB:O_GQ], wt[O_R:], jnp.zeros((P_W - P_SM - 8 - GATE_RANK, wt.shape[1]), wt.dtype)], axis=0)


def _unpad_in_rows(gt):
    return jnp.concatenate([
        gt[:P_GQ], gt[P_SM:P_SM + 8], _unpad_heads(gt[P_GQ:P_GK], 0), _unpad_heads(gt[P_GK:P_GV], 0),
        gt[P_GV:P_SM], gt[P_SM + 8:P_SM + 8 + GATE_RANK]], axis=0)


def _lane_row(vals, width=128):
    return jnp.pad(vals.reshape(1, -1), ((0, 0), (0, width - vals.size)))


SMALL_NAMES = ["ln0_g", "ln0_b", "b_ada", "dn_conv", "dn_a_log", "dn_dt_bias", "dn_norm_g", "gla_w_gate2",
               "gla_b_gate", "gla_norm_g", "ln1_g", "ln1_b", "ffn_conv", "ffn_conv_b", "ln2_g", "ln2_b"]
WEIGHTS = ["ln0_g", "ln0_b", "w_ada", "b_ada", "w_in", "dn_conv", "dn_a_log", "dn_dt_bias", "dn_norm_g",
           "gla_w_gate2", "gla_b_gate", "gla_norm_g", "w_o", "ln1_g", "ln1_b", "ffn_w_up", "ffn_conv", "ffn_conv_b",
           "ffn_w_down", "ln2_g", "ln2_b"]


def kernel(x, c, ln0_g, ln0_b, w_ada, b_ada, w_in, dn_conv, dn_a_log, dn_dt_bias, dn_norm_g, gla_w_gate2, gla_b_gate, gla_norm_g, w_o, ln1_g, ln1_b, ffn_w_up, ffn_conv, ffn_conv_b, ffn_w_down, ln2_g, ln2_b, loss_target, m_ln0_g, m_ln0_b, m_w_ada, m_b_ada, m_w_in, m_dn_conv, m_dn_a_log, m_dn_dt_bias, m_dn_norm_g, m_gla_w_gate2, m_gla_b_gate, m_gla_norm_g, m_w_o, m_ln1_g, m_ln1_b, m_ffn_w_up, m_ffn_conv, m_ffn_conv_b, m_ffn_w_down, m_ln2_g, m_ln2_b, v_ln0_g, v_ln0_b, v_w_ada, v_b_ada, v_w_in, v_dn_conv, v_dn_a_log, v_dn_dt_bias, v_dn_norm_g, v_gla_w_gate2, v_gla_b_gate, v_gla_norm_g, v_w_o, v_ln1_g, v_ln1_b, v_ffn_w_up, v_ffn_conv, v_ffn_conv_b, v_ffn_w_down, v_ln2_g, v_ln2_b):
    args = dict(locals())
    w_given = {n: args[n] for n in WEIGHTS}
    m_given = {n: args["m_" + n] for n in WEIGHTS}
    v_given = {n: args["v_" + n] for n in WEIGHTS}
    bsz, t_total, _ = x.shape
    ntok = bsz * t_total
    mx, my, mc = _position()
    me = _slot(mx, my, mc)

    pack1 = jnp.concatenate([c.reshape(-1), dn_conv.reshape(-1), gla_w_gate2.reshape(-1), ffn_conv.reshape(-1)])
    n1 = pack1.size
    rows1 = -(-n1 // 1024) * 8
    pack1 = jnp.pad(pack1, (0, rows1 * 128 - n1)).reshape(rows1, 128)
    got1 = _gather_small(pack1, "gather_cond").reshape(N_DEV, -1)
    o1 = bsz * D
    o2 = o1 + dn_conv.size
    o3 = o2 + gla_w_gate2.size
    c_all = got1[:, :o1].reshape(N_DEV * bsz, D)
    dn_conv_f = got1[:, o1:o2].reshape(N_DEV, DN_CONV_K, -1).transpose(1, 0, 2).reshape(DN_CONV_K, QKV_W)
    gate2_f = got1[:, o2:o3].reshape(N_DEV, GATE_RANK, -1).transpose(1, 0, 2).reshape(GATE_RANK, HEADS * GLA_KEY)
    ffn_conv_f = got1[:, o3:n1].reshape(N_DEV, FFN_CONV_K, -1).transpose(1, 0, 2).reshape(FFN_CONV_K, 2 * D_FF)

    win_t = w_in[0].T.astype(MXU_DT)
    wup_t = ffn_w_up[0].T.astype(MXU_DT)
    win_all, wo_all, wup_all, wdn_all = _gather_big(
        [win_t, w_o[0].astype(MXU_DT), wup_t, ffn_w_down[0].astype(MXU_DT)])
    win_p = _pad_in_rows(win_all.reshape(IN_W, D))
    wo_f = wo_all.reshape(D, D)
    wup_f = wup_all.reshape(2 * D_FF, D)
    wdn_f = wdn_all.reshape(D_FF, D)

    ncol = w_ada.shape[2]
    b_cols = lax.dynamic_slice_in_dim(b_ada, me * ncol, ncol, axis=1)
    mod_part = _ada_fwd(c_all, w_ada[0], b_cols)
    mod_all = _gather_small(mod_part.reshape(-1, 128), "gather_mod").reshape(N_DEV, N_DEV * bsz, ncol)
    mod = lax.dynamic_slice_in_dim(mod_all, me * bsz, bsz, axis=1).transpose(1, 0, 2).reshape(bsz, 6, 1, D)
    sh_a, sc_a, gt_a, sh_f, sc_f, gt_f = (mod[:, i] for i in range(6))

    g0, b0 = ln0_g.reshape(1, D), ln0_b.reshape(1, D)
    alog_row, dt_row = _lane_row(dn_a_log[0]), _lane_row(dn_dt_bias[0])
    grow_dn, grow_gla = jnp.tile(dn_norm_g, (1, HEADS)), jnp.tile(gla_norm_g, (1, HEADS))
    w2 = jnp.zeros((128, HEADS * HD), F32).at[SM_R:SM_R + GATE_RANK].set(_pad_heads(gate2_f, 1))
    bg = _pad_heads(gla_b_gate, 1)

    h_a = _ln0_mod(x, g0, b0, sc_a, sh_a)
    proj = _mm(h_a.reshape(ntok, D), win_p, "nt", F32, "mm_proj", tn=1408).reshape(bsz, t_total, P_W)
    q, k, v, gates = _dn_pre_fwd(proj, dn_conv_f, alog_row, dt_row)
    o_dn, s_dn = _dn_rec_fwd(q, k, v, gates)
    o_gla, s_gla = _gla_rec_fwd(proj, w2, bg)
    o_mix = _mix_out_fwd(o_dn, o_gla, proj, grow_dn, grow_gla)
    y = _mm(o_mix.reshape(ntok, D), wo_f, "nn", F32, "mm_wo", tn=1024).reshape(bsz, t_total, D)
    r1, h_f = _res_ln_mod(x, y, gt_a, g0, b0, ln1_g, ln1_b, sc_f, sh_f)
    up = _mm(h_f.reshape(ntok, D), wup_f, "nt", F32, "mm_up", tn=1408).reshape(bsz, t_total, 2 * D_FF)
    act = _ffn_act_fwd(up, ffn_conv_f, ffn_conv_b)
    y2 = _mm(act.reshape(ntok, D_FF), wdn_f, "nn", F32, "mm_down", tn=1024).reshape(bsz, t_total, D)
    loss_rows, dr2, dy2, dgt_f, d_ln2_g, d_ln2_b = _final_fwd_bwd(r1, y2, gt_f, ln1_g, ln1_b, ln2_g, ln2_b, loss_target)
    loss = lax.psum(0.5 * jnp.sum(loss_rows) / D, ("x", "y", "c"))

    dy2_2 = dy2.reshape(ntok, D)
    dact = _mm(dy2_2, wdn_f, "nt", F32, "mm_dact", tn=1408).reshape(bsz, t_total, D_FF)
    g_wdn = _mm(act.reshape(ntok, D_FF), dy2_2, "tn", F32, "mm_gwdn", tm=1408, tn=1024, tk=512)
    dup, d_ffn_conv, d_ffn_conv_b = _ffn_act_bwd(up, dact, ffn_conv_f, ffn_conv_b)
    dup_2 = dup.reshape(ntok, 2 * D_FF)
    dh_f = _mm(dup_2, wup_f, "nn", F32, "mm_dhf", tn=1024, tk=1408).reshape(bsz, t_total, D)
    g_wup_t = _mm(dup_2, h_f.reshape(ntok, D), "tn", F32, "mm_gwup", tm=1408, tn=1024, tk=512)
    dr1, dsc_f, dsh_f, d_ln1_g, d_ln1_b, dy, dgt_a = _ln_bwd_call(
        "ln1_bwd", dr2, dh_f, r1, ln1_g, ln1_b, sc_f, y=y, gt=gt_a)

    dy_2 = dy.reshape(ntok, D)
    do = _mm(dy_2, wo_f, "nt", F32, "mm_do", tn=1024).reshape(bsz, t_total, D)
    g_wo = _mm(o_mix.reshape(ntok, D), dy_2, "tn", F32, "mm_gwo", tm=1024, tn=1024, tk=512)
    do_dn, do_gla, dz, dgg, d_dn_norm, d_gla_norm = _mix_out_bwd(do, o_dn, o_gla, proj, grow_dn, grow_gla)
    dq, dk, dv, dgates = _dn_rec_bwd(q, k, v, gates, s_dn, do_dn)
    dqkv, dsm_dn, d_dn_conv, d_alog_row, d_dt_row = _dn_pre_bwd(proj, dq, dk, dv, dgates, dn_conv_f, alog_row, dt_row)
    dgq, dgk, dgv, dsm, d_w2, d_bg = _gla_rec_bwd(proj, w2, bg, s_gla, do_gla, dsm_dn)
    dproj = jnp.concatenate([dqkv, dz, dgq, dgk, dgv, dgg, dsm], axis=-1).reshape(ntok, P_W)
    dh_a = _mm(dproj, win_p, "nn", F32, "mm_dha", tn=1024, tk=1408).reshape(bsz, t_total, D)
    g_win_p = _mm(dproj, h_a.reshape(ntok, D), "tn", F32, "mm_gwin", tm=1408, tn=1024, tk=512)
    grad_x, dsc_a, dsh_a, d_ln0_g, d_ln0_b = _ln_bwd_call("ln0_bwd", dr1, dh_a, x, g0, b0, sc_a)

    big = [_unpad_in_rows(g_win_p).reshape(N_DEV, -1, D), g_wo.reshape(N_DEV, -1, D),
           g_wup_t.reshape(N_DEV, -1, D), g_wdn.reshape(N_DEV, -1, D)]
    from_sibling = _scatter_sibling(big)
    core = mc.reshape(1).astype(jnp.int32)
    chip_sums = [_pair_add(g8, r1_, core, f"pair_add_{i}") for i, (g8, r1_) in enumerate(zip(big, from_sibling))]
    from_chips = _scatter_chips(chip_sums)
    chip = (2 * mx + my).reshape(1).astype(jnp.int32)
    g_win_t, g_wo_s, g_wup_ts, g_wdn_s = (
        _chip_add(p4, r2_, chip, f"chip_add_{i}") for i, (p4, r2_) in enumerate(zip(chip_sums, from_chips)))

    dmod = jnp.concatenate([dsh_a, dsc_a, dgt_a, dsh_f, dsc_f, dgt_f], axis=1).reshape(-1)
    small_parts = {
        "ln0_g": d_ln0_g, "ln0_b": d_ln0_b, "ln1_g": d_ln1_g, "ln1_b": d_ln1_b, "ln2_g": d_ln2_g, "ln2_b": d_ln2_b,
        "dn_a_log": d_alog_row[:, :HEADS], "dn_dt_bias": d_dt_row[:, :HEADS],
        "dn_norm_g": d_dn_norm, "gla_norm_g": d_gla_norm, "gla_b_gate": _unpad_heads(d_bg, 1),
        "ffn_conv_b": d_ffn_conv_b, "dn_conv": d_dn_conv,
        "gla_w_gate2": _unpad_heads(d_w2[SM_R:SM_R + GATE_RANK], 1), "ffn_conv": d_ffn_conv}
    order = sorted(small_parts)
    flat = jnp.concatenate([small_parts[n].reshape(-1) for n in order] + [dmod])
    n3 = flat.size
    rows3 = -(-n3 // 1024) * 8
    pack3 = jnp.pad(flat, (0, rows3 * 128 - n3)).reshape(rows3, 128)
    got3 = _gather_small(pack3, "gather_small_grads")
    tot3 = _sum_slots(got3, "sum_small_grads").reshape(-1)
    grads = {}
    off = 0
    for n in order:
        size = small_parts[n].size
        grads[n] = tot3[off:off + size]
        off += size
    dmod_all = got3.reshape(N_DEV, -1)[:, off:off + dmod.size].reshape(N_DEV * bsz, 6 * D)
    dmod_cols = lax.dynamic_slice_in_dim(dmod_all, me * ncol, ncol, axis=1)
    g_wada, g_bada = _ada_bwd(c_all, dmod_all, dmod_cols)
    grads["b_ada"] = g_bada

    def col_shard(full, rows):
        part = full.reshape(rows, -1)
        width = part.shape[1] // N_DEV
        return lax.dynamic_slice_in_dim(part, me * width, width, axis=1)

    grads["dn_conv"] = col_shard(grads["dn_conv"], DN_CONV_K)
    grads["gla_w_gate2"] = col_shard(grads["gla_w_gate2"], GATE_RANK)
    grads["ffn_conv"] = col_shard(grads["ffn_conv"], FFN_CONV_K)
    grads = {n: g.reshape(w_given[n].shape) for n, g in grads.items()}
    grads["w_ada"] = g_wada.reshape(w_ada.shape)
    grads["w_in"] = g_win_t.T.reshape(w_in.shape)
    grads["w_o"] = g_wo_s.reshape(w_o.shape)
    grads["ffn_w_up"] = g_wup_ts.T.reshape(ffn_w_up.shape)
    grads["ffn_w_down"] = g_wdn_s.reshape(ffn_w_down.shape)

    delta, new_m, new_v = {}, {}, {}
    for n in ["w_ada", "w_in", "w_o", "ffn_w_up", "ffn_w_down"]:
        shp = w_given[n].shape
        two_d = lambda a: a.reshape(shp[-2], shp[-1])
        d_, m_, v_ = _adamw(two_d(w_given[n]), two_d(grads[n]), two_d(m_given[n]), two_d(v_given[n]), "adamw_" + n)
        delta[n], new_m[n], new_v[n] = d_.reshape(shp), m_.reshape(shp), v_.reshape(shp)

    def pack_small(src):
        flat_ = jnp.concatenate([src[n].reshape(-1) for n in SMALL_NAMES])
        rows_ = -(-flat_.size // 1024) * 8
        return jnp.pad(flat_, (0, rows_ * 128 - flat_.size)).reshape(rows_, 128)

    d_s, m_s, v_s = _adamw(pack_small(w_given), pack_small(grads), pack_small(m_given), pack_small(v_given),
                           "adamw_small")
    off = 0
    for n in SMALL_NAMES:
        size, shp = w_given[n].size, w_given[n].shape
        delta[n] = d_s.reshape(-1)[off:off + size].reshape(shp)
        new_m[n] = m_s.reshape(-1)[off:off + size].reshape(shp)
        new_v[n] = v_s.reshape(-1)[off:off + size].reshape(shp)
        off += size

    return (loss, grad_x, *[grads[n] for n in WEIGHTS], *[delta[n] for n in WEIGHTS],
            *[new_m[n] for n in WEIGHTS], *[new_v[n] for n in WEIGHTS])
```

```python
import functools

import jax
import jax.numpy as jnp
from jax import lax
from jax.experimental import pallas as pl
from jax.experimental.pallas import tpu as pltpu

F32 = jnp.float32
MXU_DT = jnp.bfloat16
HI = lax.Precision.HIGHEST
MESH = pl.DeviceIdType.MESH
N_DEV = 8

D = 1024
HEADS = 4
HD = 128
CHUNK = 64
GLA_KEY = 64
GLA_TAU = 16.0
GATE_RANK = 16
D_FF = 2816
IN_W = 3608
ALPHA = 2.0 ** 0.25
EPS = 1e-6
DN_CONV_K = 4
FFN_CONV_K = 3
HALO = 8

P_QKV, P_Z, P_GQ, P_GK, P_GV, P_GG, P_SM, P_W = 0, 1536, 2048, 2560, 3072, 3584, 4096, 4224
SM_A, SM_B, SM_R = 0, 4, 8

ADAM_LR, ADAM_B1, ADAM_B2, ADAM_EPS, ADAM_WD, ADAM_STEP = 0.001, 0.9, 0.999, 1e-08, 0.01, 10

VMEM_LIMIT_V7X = 56 * 1024 * 1024


def _params(sem=None):
    return pltpu.CompilerParams(dimension_semantics=sem, vmem_limit_bytes=VMEM_LIMIT_V7X)


def _dg(a, b, dims, prec=None):
    return lax.dot_general(a, b, (dims, ((), ())), precision=prec, preferred_element_type=F32)


def _dot(a, b, prec=None):
    return _dg(a, b, ((1,), (0,)), prec)


def _dot_nt(a, b, prec=None):
    return _dg(a, b, ((1,), (1,)), prec)


def _dot_tn(a, b, prec=None):
    return _dg(a, b, ((0,), (0,)), prec)


def _iota(shape, dim):
    return lax.broadcasted_iota(jnp.int32, shape, dim)


def _sigmoid(x):
    return jax.nn.sigmoid(x)


def _silu(x):
    return x * _sigmoid(x)


def _softplus(x):
    return jnp.maximum(x, 0.0) + jnp.log(1.0 + jnp.exp(-jnp.abs(x)))


def _ln_stats(x):
    mu = jnp.mean(x, axis=-1, keepdims=True)
    xc = x - mu
    rstd = lax.rsqrt(jnp.mean(xc * xc, axis=-1, keepdims=True) + EPS)
    return xc * rstd, rstd


def _ln_bwd(dxhat, xhat, rstd):
    return rstd * (dxhat - jnp.mean(dxhat, axis=-1, keepdims=True)
                   - xhat * jnp.mean(dxhat * xhat, axis=-1, keepdims=True))


NN, NT, TN = ((1,), (0,)), ((1,), (1,)), ((0,), (0,))


def _split2(a):
    hi = a.astype(jnp.bfloat16)
    return hi, (a - hi.astype(F32)).astype(jnp.bfloat16)


def _d3(a, b, dims):
    ah, al = _split2(a)
    bh, bl = _split2(b)
    return _dg(ah, bh, dims) + (_dg(ah, bl, dims) + _dg(al, bh, dims))


@jax.custom_vjp
def _dot3(a, b):
    return _d3(a, b, NN)


_dot3.defvjp(lambda a, b: (_d3(a, b, NN), (a, b)),
             lambda res, g: (_d3(g, res[1], NT), _d3(res[0], g, TN)))


def _split3(b):
    b1 = b.astype(jnp.bfloat16)
    r1 = b - b1.astype(F32)
    b2 = r1.astype(jnp.bfloat16)
    return b1, b2, (r1 - b2.astype(F32)).astype(jnp.bfloat16)


def _sum3(fn, b):
    b1, b2, b3 = _split3(b)
    return fn(b1) + (fn(b2) + fn(b3))


@jax.custom_vjp
def _mask_dot(e, b):
    return _sum3(lambda t: _dg(e, t, NN), b)


_mask_dot.defvjp(lambda e, b: (_mask_dot(e, b), e),
                 lambda e, g: (jnp.zeros_like(e), _sum3(lambda t: _dg(e, t, TN), g)))


@jax.custom_vjp
def _mask_dot_nt(e, b):
    return _sum3(lambda t: _dg(e, t, NT), b)


_mask_dot_nt.defvjp(lambda e, b: (_mask_dot_nt(e, b), e),
                    lambda e, g: (jnp.zeros_like(e), _sum3(lambda t: _dg(t, e, TN), g)))


def _tri_inv_impl(ms):
    n = ms[0].shape[0]
    r, c = _iota((n, n), 0), _iota((n, n), 1)
    eye = (r == c).astype(F32)
    diag = (r >> 3) == (c >> 3)
    ds = [jnp.where(diag, m, 0.0) for m in ms]
    d2s = [_d3(d, d, NN) for d in ds]
    d4s = [_d3(d2, d2, NN) for d2 in d2s]
    invs = [_d3(eye - d, eye + d2, NN) for d, d2 in zip(ds, d2s)]
    invs = [_d3(inv, eye + d4, NN) for inv, d4 in zip(invs, d4s)]
    shift = 3
    while (1 << shift) < n:
        rb, cb = r >> shift, c >> shift
        sel = ((rb & 1) == 1) & (cb == rb - 1)
        tmp = [_d3(inv, jnp.where(sel, m, 0.0), NN) for inv, m in zip(invs, ms)]
        invs = [inv - _d3(t, inv, NN) for t, inv in zip(tmp, invs)]
        shift += 1
    return invs


@jax.custom_vjp
def _tri_inv(ms):
    return _tri_inv_impl(ms)


def _tri_inv_fwd(ms):
    invs = _tri_inv_impl(ms)
    return invs, invs


def _tri_inv_bwd(invs, das):
    tmp = [_d3(a, da, TN) for a, da in zip(invs, das)]
    return ([-_d3(t, a, NT) for t, a in zip(tmp, invs)],)


_tri_inv.defvjp(_tri_inv_fwd, _tri_inv_bwd)


def _dn_chunk(s_list, q, k, v, gates):
    nb = len(q)
    c = q[0].shape[0]
    r64, c64 = _iota((c, c), 0), _iota((c, c), 1)
    causal = r64 >= c64
    strict = r64 > c64
    tri = causal.astype(jnp.bfloat16)
    eye = (_iota((HD, HD), 0) == _iota((HD, HD), 1)).astype(jnp.bfloat16)
    lane = _iota(gates[0].shape, 1)
    lane1 = _iota((1, HD), 1)
    g_all = [_mask_dot(tri, g) for g in gates]
    g_all_t = [_mask_dot_nt(eye, g) for g in g_all]
    row = _iota(g_all_t[0].shape, 0)
    last = [jnp.sum(g, axis=0, keepdims=True) for g in gates]
    prob = [(b, h) for b in range(nb) for h in range(HEADS)]
    sl = [slice(h * HD, (h + 1) * HD) for h in range(HEADS)]
    qh = [q[b][:, sl[h]] for b, h in prob]
    kh = [k[b][:, sl[h]] for b, h in prob]
    vh = [v[b][:, sl[h]] for b, h in prob]
    s = [s_list[b][h] for b, h in prob]
    beta = [jnp.sum(jnp.where(lane == SM_B + h, gates[b], 0.0), axis=-1, keepdims=True) for b, h in prob]
    g_c = [jnp.sum(jnp.where(lane == SM_A + h, g_all[b], 0.0), axis=-1, keepdims=True) for b, h in prob]
    g_r = [jnp.sum(jnp.where(row == SM_A + h, g_all_t[b], 0.0), axis=0, keepdims=True) for b, h in prob]
    g_last = [jnp.sum(jnp.where(lane1 == SM_A + h, last[b], 0.0), axis=-1, keepdims=True) for b, h in prob]
    decay = [jnp.where(causal, jnp.exp(jnp.where(causal, gc - gr, 0.0)), 0.0) for gc, gr in zip(g_c, g_r)]
    kb = [k_ * b_ for k_, b_ in zip(kh, beta)]
    m_low = [jnp.where(strict, _dot_nt(kb_, k_) * d_, 0.0) for kb_, k_, d_ in zip(kb, kh, decay)]
    attn = [_dot_nt(q_, k_) * d_ for q_, k_, d_ in zip(qh, kh, decay)]
    a_inv = _tri_inv(m_low)
    eg = [jnp.exp(gc) for gc in g_c]
    uw = [_dot3(a_, jnp.concatenate([v_ * b_, kb_ * e_], axis=1))
          for a_, v_, b_, kb_, e_ in zip(a_inv, vh, beta, kb, eg)]
    v_new = [uw_[:, :HD] - _dot(uw_[:, HD:], s_) for uw_, s_ in zip(uw, s)]
    qs = [_dot(q_ * e_, s_) for q_, e_, s_ in zip(qh, eg, s)]
    o = [qs_ + _dot(a_, vn_) for qs_, a_, vn_ in zip(qs, attn, v_new)]
    k_dec = [k_ * jnp.exp(gl - gc) for k_, gl, gc in zip(kh, g_last, g_c)]
    s_new = [s_ * jnp.exp(gl) + _dot_tn(kd_, vn_) for s_, gl, kd_, vn_ in zip(s, g_last, k_dec, v_new)]
    outs = [jnp.concatenate(o[b * HEADS:(b + 1) * HEADS], axis=-1) for b in range(nb)]
    return outs, [s_new[b * HEADS:(b + 1) * HEADS] for b in range(nb)]


def _gla_chunk(st_list, q, k, v, small, w2, bg):
    nb = len(q)
    c = q[0].shape[0]
    causal = _iota((c, c), 0) >= _iota((c, c), 1)
    tri = causal.astype(jnp.bfloat16)
    la_all = [-_softplus(-(_dot(sm, w2) + bg)) * (1.0 / GLA_TAU) for sm in small]
    b_all = [_mask_dot(tri, la) for la in la_all]
    prob = [(b, h) for b in range(nb) for h in range(HEADS)]
    sl = [slice(h * HD, (h + 1) * HD) for h in range(HEADS)]
    kh = [k[b][:, sl[h]] for b, h in prob]
    vh = [v[b][:, sl[h]] for b, h in prob]
    st = [st_list[b][h] for b, h in prob]
    bc = [b_all[b][:, sl[h]] for b, h in prob]
    b_last = [jnp.sum(la_all[b][:, sl[h]], axis=0, keepdims=True) for b, h in prob]
    q_dec = [q[b][:, sl[h]] * (GLA_KEY ** -0.5) * jnp.exp(bc_) for (b, h), bc_ in zip(prob, bc)]
    attn = [jnp.where(causal, _dot_nt(qd, k_ * jnp.exp(-bc_)), 0.0) for qd, k_, bc_ in zip(q_dec, kh, bc)]
    inter = [_dot_nt(qd, st_) for qd, st_ in zip(q_dec, st)]
    o = [i_ + _dot(a_, v_) for i_, a_, v_ in zip(inter, attn, vh)]
    k_dec = [k_ * jnp.exp(bl - bc_) for k_, bl, bc_ in zip(kh, b_last, bc)]
    s_new = [st_ * jnp.exp(bl) + _dot_tn(v_, kd) for st_, bl, v_, kd in zip(st, b_last, vh, k_dec)]
    outs = [jnp.concatenate(o[b * HEADS:(b + 1) * HEADS], axis=-1) for b in range(nb)]
    return outs, [s_new[b * HEADS:(b + 1) * HEADS] for b in range(nb)]


def _dn_qkv(y):
    act = _silu(y)
    parts = []
    for i in range(2 * HEADS):
        xh = act[:, i * HD:(i + 1) * HD]
        xh = xh * lax.rsqrt(jnp.sum(xh * xh, axis=-1, keepdims=True) + EPS)
        parts.append(xh * (HD ** -0.5) if i < HEADS else xh)
    qk = jnp.concatenate(parts, axis=-1)
    return qk[:, :HEADS * HD], qk[:, HEADS * HD:], act[:, 2 * HEADS * HD:]


def _dn_gates(small, alog_row, dt_row):
    lane = _iota(small.shape, 1)
    log_a = -jnp.exp(alog_row) * _softplus(small + dt_row)
    return jnp.where(lane < SM_B, log_a, jnp.where(lane < SM_R, _sigmoid(small), 0.0))


def _gate_norm(o, z, grow):
    parts = []
    for h in range(HEADS):
        oh = o[:, h * HD:(h + 1) * HD]
        parts.append(oh * lax.rsqrt(jnp.mean(oh * oh, axis=-1, keepdims=True) + EPS))
    return jnp.concatenate(parts, axis=-1) * grow * _silu(z)


def _conv_rows(xrows, w_ref, k_taps):
    n = xrows.shape[0]
    acc = xrows * w_ref[k_taps - 1:k_taps, :]
    for s in range(1, k_taps):
        acc = acc + pltpu.roll(xrows, s, 0) * w_ref[k_taps - 1 - s:k_taps - s, :]
    return acc


def _shift_up(x, s):
    return x if s == 0 else pltpu.roll(x, x.shape[0] - s, 0)


def _div_tile(n, cap, mult=8):
    best = None
    for t in range(mult, min(n, cap) + 1, mult):
        if n % t == 0:
            best = t
    return best if best is not None else n


def _halo_prev(tt):
    return lambda b, t: (b, jnp.maximum(t * (tt // HALO) - 1, 0))


def _halo_next(tt, t_total):
    return lambda b, t: (b, jnp.minimum((t + 1) * (tt // HALO), t_total // HALO - 1))


def _mm(a, b, mode, out_dtype, name, tm=512, tn=512, tk=None):
    if mode == "nn":
        (m, k), n = a.shape, b.shape[1]
    elif mode == "nt":
        (m, k), n = a.shape, b.shape[0]
    else:
        (k, m), n = a.shape, b.shape[1]
    tm, tn = min(tm, m), min(tn, n)
    tk = k if tk is None else min(tk, k)
    assert m % tm == 0 and n % tn == 0 and k % tk == 0, (name, a.shape, b.shape, tm, tn, tk)
    nk = k // tk
    if mode == "tn":
        a_spec = pl.BlockSpec((tk, tm), lambda i, j, kk: (kk, i))
    else:
        a_spec = pl.BlockSpec((tm, tk), lambda i, j, kk: (i, kk))
    if mode == "nt":
        b_spec = pl.BlockSpec((tn, tk), lambda i, j, kk: (j, kk))
    else:
        b_spec = pl.BlockSpec((tk, tn), lambda i, j, kk: (kk, j))
    dims = {"nn": ((1,), (0,)), "nt": ((1,), (1,)), "tn": ((0,), (0,))}[mode]

    def body(a_ref, b_ref, o_ref, *acc):
        p = _dg(a_ref[...], b_ref[...], dims)
        if nk == 1:
            o_ref[...] = p.astype(out_dtype)
        else:
            kk = pl.program_id(2)

            @pl.when(kk == 0)
            def _():
                acc[0][...] = p

            @pl.when(kk > 0)
            def _():
                acc[0][...] += p

            @pl.when(kk == nk - 1)
            def _():
                o_ref[...] = acc[0][...].astype(out_dtype)

    return pl.pallas_call(
        body, name=name, grid=(m // tm, n // tn, nk),
        in_specs=[a_spec, b_spec],
        out_specs=pl.BlockSpec((tm, tn), lambda i, j, kk: (i, j)),
        out_shape=jax.ShapeDtypeStruct((m, n), out_dtype),
        scratch_shapes=[pltpu.VMEM((tm, tn), F32)] if nk > 1 else [],
        compiler_params=_params(("parallel", "parallel", "arbitrary")),
    )(a, b)


def _ada_fwd(c_all, w_ada, b_cols):
    def body(c_ref, w_ref, b_ref, o_ref):
        cond = _silu(c_ref[...]).astype(MXU_DT)
        o_ref[...] = _dot(cond, w_ref[...].astype(MXU_DT)) + b_ref[...]

    return pl.pallas_call(body, name="ada_fwd", out_shape=jax.ShapeDtypeStruct((c_all.shape[0], w_ada.shape[1]), F32),
                          compiler_params=_params())(c_all, w_ada, b_cols)


def _ada_bwd(c_all, dmod_all, dmod_cols):
    def body(c_ref, da_ref, dc_ref, gw_ref, gb_ref):
        cond = _silu(c_ref[...]).astype(MXU_DT)
        gw_ref[...] = _dot_tn(cond, dc_ref[...].astype(MXU_DT))
        gb_ref[...] = jnp.sum(da_ref[...], axis=0, keepdims=True)

    return pl.pallas_call(
        body, name="ada_bwd",
        out_shape=(jax.ShapeDtypeStruct((c_all.shape[1], dmod_cols.shape[1]), F32),
                   jax.ShapeDtypeStruct((1, dmod_all.shape[1]), F32)),
        compiler_params=_params())(c_all, dmod_all, dmod_cols)


def _tok_spec(tt, width=D):
    return pl.BlockSpec((1, tt, width), lambda b, t: (b, t, 0))


def _vec_spec(width=D):
    return pl.BlockSpec((1, width), lambda b, t: (0, 0))


def _bvec_spec(width=D):
    return pl.BlockSpec((1, 1, width), lambda b, t: (b, 0, 0))


def _ln0_mod(x, g0, b0, sc, sh):
    bsz, t_total, _ = x.shape
    tt = _div_tile(t_total, 256)

    def body(x_ref, g_ref, b_ref, sc_ref, sh_ref, h_ref):
        xh, _ = _ln_stats(x_ref[0])
        x0 = xh * g_ref[...] + b_ref[...]
        h_ref[0] = (x0 * (1.0 + sc_ref[0]) + sh_ref[0]).astype(MXU_DT)

    return pl.pallas_call(
        body, name="ln0_mod", grid=(bsz, t_total // tt),
        in_specs=[_tok_spec(tt), _vec_spec(), _vec_spec(), _bvec_spec(), _bvec_spec()],
        out_specs=_tok_spec(tt), out_shape=jax.ShapeDtypeStruct(x.shape, MXU_DT),
        compiler_params=_params(("parallel", "parallel")))(x, g0, b0, sc, sh)


def _res_ln_mod(x, y, gt, g0, b0, g1, b1, sc, sh):
    bsz, t_total, _ = x.shape
    tt = _div_tile(t_total, 256)

    def body(x_ref, y_ref, gt_ref, g0_ref, b0_ref, g1_ref, b1_ref, sc_ref, sh_ref, r_ref, h_ref):
        xh, _ = _ln_stats(x_ref[0])
        r = ALPHA * (xh * g0_ref[...] + b0_ref[...]) + (1.0 + gt_ref[0]) * y_ref[0]
        r_ref[0] = r
        rh, _ = _ln_stats(r)
        x1 = rh * g1_ref[...] + b1_ref[...]
        h_ref[0] = (x1 * (1.0 + sc_ref[0]) + sh_ref[0]).astype(MXU_DT)

    return pl.pallas_call(
        body, name="res_ln_mod", grid=(bsz, t_total // tt),
        in_specs=[_tok_spec(tt), _tok_spec(tt), _bvec_spec(), _vec_spec(), _vec_spec(), _vec_spec(), _vec_spec(),
                  _bvec_spec(), _bvec_spec()],
        out_specs=(_tok_spec(tt), _tok_spec(tt)),
        out_shape=(jax.ShapeDtypeStruct(x.shape, F32), jax.ShapeDtypeStruct(x.shape, MXU_DT)),
        compiler_params=_params(("parallel", "parallel")))(x, y, gt, g0, b0, g1, b1, sc, sh)


def _final_fwd_bwd(r1, y2, gt, g1, b1, g2, b2, target):
    bsz, t_total, _ = r1.shape
    tt = _div_tile(t_total, 256)

    def body(r1_ref, y2_ref, gt_ref, g1_ref, b1_ref, g2_ref, b2_ref, tg_ref,
             loss_ref, dr2_ref, dy2_ref, dgt_ref, dg2_ref, db2_ref):
        b, t = pl.program_id(0), pl.program_id(1)

        @pl.when((b == 0) & (t == 0))
        def _():
            loss_ref[...] = jnp.zeros_like(loss_ref)
            dg2_ref[...] = jnp.zeros_like(dg2_ref)
            db2_ref[...] = jnp.zeros_like(db2_ref)

        @pl.when(t == 0)
        def _():
            dgt_ref[...] = jnp.zeros_like(dgt_ref)

        rh1, _ = _ln_stats(r1_ref[0])
        x1 = rh1 * g1_ref[...] + b1_ref[...]
        y2 = y2_ref[0]
        gate = 1.0 + gt_ref[0]
        xh2, rstd2 = _ln_stats(ALPHA * x1 + gate * y2)
        err = xh2 * g2_ref[...] + b2_ref[...] - tg_ref[0]
        loss_ref[...] += jnp.sum(err * err, axis=0, keepdims=True)
        dx2 = err * (1.0 / D)
        dg2_ref[...] += jnp.sum(dx2 * xh2, axis=0, keepdims=True)
        db2_ref[...] += jnp.sum(dx2, axis=0, keepdims=True)
        dr2 = _ln_bwd(dx2 * g2_ref[...], xh2, rstd2)
        dr2_ref[0] = dr2
        dy2_ref[0] = (gate * dr2).astype(MXU_DT)
        dgt_ref[0] += jnp.sum(dr2 * y2, axis=0, keepdims=True)

    vec_out = jax.ShapeDtypeStruct((1, D), F32)
    return pl.pallas_call(
        body, name="final_fwd_bwd", grid=(bsz, t_total // tt),
        in_specs=[_tok_spec(tt), _tok_spec(tt), _bvec_spec(), _vec_spec(), _vec_spec(), _vec_spec(), _vec_spec(),
                  _tok_spec(tt)],
        out_specs=(_vec_spec(), _tok_spec(tt), _tok_spec(tt), _bvec_spec(), _vec_spec(), _vec_spec()),
        out_shape=(vec_out, jax.ShapeDtypeStruct(r1.shape, F32), jax.ShapeDtypeStruct(r1.shape, MXU_DT),
                   jax.ShapeDtypeStruct((bsz, 1, D), F32), vec_out, vec_out),
        compiler_params=_params(("arbitrary", "arbitrary")))(r1, y2, gt, g1, b1, g2, b2, target)


def _ln_bwd_call(name, d_res, d_h, src, g, b, sc, y=None, gt=None):
    bsz, t_total, _ = src.shape
    tt = _div_tile(t_total, 256)
    has_y = y is not None

    def body(*refs):
        if has_y:
            (dres_ref, dh_ref, src_ref, g_ref, b_ref, sc_ref, y_ref, gt_ref,
             dsrc_ref, dsc_ref, dsh_ref, dg_ref, db_ref, dy_ref, dgt_ref) = refs
        else:
            (dres_ref, dh_ref, src_ref, g_ref, b_ref, sc_ref,
             dsrc_ref, dsc_ref, dsh_ref, dg_ref, db_ref) = refs
        bi, t = pl.program_id(0), pl.program_id(1)

        @pl.when((bi == 0) & (t == 0))
        def _():
            dg_ref[...] = jnp.zeros_like(dg_ref)
            db_ref[...] = jnp.zeros_like(db_ref)

        @pl.when(t == 0)
        def _():
            dsc_ref[...] = jnp.zeros_like(dsc_ref)
            dsh_ref[...] = jnp.zeros_like(dsh_ref)
            if has_y:
                dgt_ref[...] = jnp.zeros_like(dgt_ref)

        xh, rstd = _ln_stats(src_ref[0])
        xv = xh * g_ref[...] + b_ref[...]
        dh = dh_ref[0]
        dx = ALPHA * dres_ref[0] + dh * (1.0 + sc_ref[0])
        dsc_ref[0] += jnp.sum(dh * xv, axis=0, keepdims=True)
        dsh_ref[0] += jnp.sum(dh, axis=0, keepdims=True)
        dg_ref[...] += jnp.sum(dx * xh, axis=0, keepdims=True)
        db_ref[...] += jnp.sum(dx, axis=0, keepdims=True)
        dsrc = _ln_bwd(dx * g_ref[...], xh, rstd)
        dsrc_ref[0] = dsrc
        if has_y:
            dy_ref[0] = ((1.0 + gt_ref[0]) * dsrc).astype(MXU_DT)
            dgt_ref[0] += jnp.sum(dsrc * y_ref[0], axis=0, keepdims=True)

    vec_out = jax.ShapeDtypeStruct((1, D), F32)
    bvec_out = jax.ShapeDtypeStruct((bsz, 1, D), F32)
    in_specs = [_tok_spec(tt), _tok_spec(tt), _tok_spec(tt), _vec_spec(), _vec_spec(), _bvec_spec()]
    out_specs = [_tok_spec(tt), _bvec_spec(), _bvec_spec(), _vec_spec(), _vec_spec()]
    out_shape = [jax.ShapeDtypeStruct(src.shape, F32), bvec_out, bvec_out, vec_out, vec_out]
    args = [d_res, d_h, src, g, b, sc]
    if has_y:
        in_specs += [_tok_spec(tt), _bvec_spec()]
        out_specs += [_tok_spec(tt), _bvec_spec()]
        out_shape += [jax.ShapeDtypeStruct(src.shape, MXU_DT), bvec_out]
        args += [y, gt]
    return pl.pallas_call(body, name=name, grid=(bsz, t_total // tt), in_specs=in_specs, out_specs=tuple(out_specs),
                          out_shape=tuple(out_shape), compiler_params=_params(("arbitrary", "arbitrary")))(*args)


FFN_TC = 256
FFN_NJ = D_FF // FFN_TC
FFN_PW = 2 * FFN_TC


def _ffn_pair(a, axis):
    shp = list(a.shape)
    a4 = a.reshape(shp[:axis] + [2, FFN_NJ, FFN_TC] + shp[axis + 1:])
    return jnp.swapaxes(a4, axis, axis + 1).reshape(shp)


def _ffn_unpair(a, axis):
    shp = list(a.shape)
    a4 = a.reshape(shp[:axis] + [FFN_NJ, 2, FFN_TC] + shp[axis + 1:])
    return jnp.swapaxes(a4, axis, axis + 1).reshape(shp)


def _ffn_act_fwd(up, cw, cb):
    bsz, t_total, _ = up.shape
    tt = _div_tile(t_total, 256)
    hp = _halo_prev(tt)

    def body(x_ref, xp_ref, w_ref, b_ref, o_ref):
        prev = jnp.where(pl.program_id(1) == 0, 0.0, xp_ref[0])
        rows = jnp.concatenate([prev, x_ref[0]], axis=0)
        u = _conv_rows(rows, w_ref, FFN_CONV_K)[HALO:] + b_ref[...]
        o_ref[0] = (_silu(u[:, :FFN_TC]) * u[:, FFN_TC:]).astype(MXU_DT)

    return pl.pallas_call(
        body, name="ffn_act_fwd", grid=(bsz, t_total // tt, FFN_NJ),
        in_specs=[pl.BlockSpec((1, tt, FFN_PW), lambda b, t, j: (b, t, j)),
                  pl.BlockSpec((1, HALO, FFN_PW), lambda b, t, j: (*hp(b, t), j)),
                  pl.BlockSpec((FFN_CONV_K, FFN_PW), lambda b, t, j: (0, j)),
                  pl.BlockSpec((1, FFN_PW), lambda b, t, j: (0, j))],
        out_specs=pl.BlockSpec((1, tt, FFN_TC), lambda b, t, j: (b, t, j)),
        out_shape=jax.ShapeDtypeStruct((bsz, t_total, D_FF), MXU_DT),
        compiler_params=_params(("parallel", "parallel", "parallel")))(up, up, cw, cb)


def _ffn_act_bwd(up, da, cw, cb):
    bsz, t_total, width = up.shape
    tt = _div_tile(t_total, 256)
    nt = t_total // tt
    hp, hn = _halo_prev(tt), _halo_next(tt, t_total)

    def body(x_ref, xp_ref, xn_ref, da_ref, dan_ref, w_ref, b_ref, dup_ref, dw_ref, db_ref):
        b, t = pl.program_id(1), pl.program_id(2)

        @pl.when((b == 0) & (t == 0))
        def _():
            dw_ref[...] = jnp.zeros_like(dw_ref)
            db_ref[...] = jnp.zeros_like(db_ref)

        prev = jnp.where(t == 0, 0.0, xp_ref[0])
        rows = jnp.concatenate([prev, x_ref[0], xn_ref[0]], axis=0)
        u = _conv_rows(rows, w_ref, FFN_CONV_K)[HALO:] + b_ref[...]
        g_pre, v_pre = u[:, :FFN_TC], u[:, FFN_TC:]
        valid = (_iota((tt + HALO, 1), 0) < tt) | (t < nt - 1)
        da_ext = jnp.where(valid, jnp.concatenate([da_ref[0], dan_ref[0]], axis=0), 0.0)
        sg = _sigmoid(g_pre)
        gs = g_pre * sg
        du = jnp.concatenate([da_ext * v_pre * (sg + gs * (1.0 - sg)), da_ext * gs], axis=1)
        dup = du * w_ref[FFN_CONV_K - 1:FFN_CONV_K, :]
        for s in range(1, FFN_CONV_K):
            dup = dup + _shift_up(du, s) * w_ref[FFN_CONV_K - 1 - s:FFN_CONV_K - s, :]
        dup_ref[0] = dup[:tt].astype(MXU_DT)
        du_t = du[:tt]
        db_ref[...] += jnp.sum(du_t, axis=0, keepdims=True)
        for k in range(FFN_CONV_K):
            s = FFN_CONV_K - 1 - k
            xs = (rows if s == 0 else pltpu.roll(rows, s, 0))[HALO:HALO + tt]
            dw_ref[k:k + 1, :] += jnp.sum(du_t * xs, axis=0, keepdims=True)

    def halo(h, w):
        return pl.BlockSpec((1, HALO, w), lambda j, b, t: (*h(b, t), j))

    wspec = lambda rows_: pl.BlockSpec((rows_, FFN_PW), lambda j, b, t: (0, j))
    tile = pl.BlockSpec((1, tt, FFN_PW), lambda j, b, t: (b, t, j))
    return pl.pallas_call(
        body, name="ffn_act_bwd", grid=(FFN_NJ, bsz, nt),
        in_specs=[tile, halo(hp, FFN_PW), halo(hn, FFN_PW),
                  pl.BlockSpec((1, tt, FFN_TC), lambda j, b, t: (b, t, j)), halo(hn, FFN_TC),
                  wspec(FFN_CONV_K), wspec(1)],
        out_specs=(tile, wspec(FFN_CONV_K), wspec(1)),
        out_shape=(jax.ShapeDtypeStruct(up.shape, MXU_DT), jax.ShapeDtypeStruct((FFN_CONV_K, width), F32),
                   jax.ShapeDtypeStruct((1, width), F32)),
        compiler_params=_params(("arbitrary", "arbitrary", "arbitrary")))(up, up, up, da, da, cw, cb)


QKV_W = 3 * HEADS * HD
SM_BLK = P_SM // 128


def _dn_pre_fwd(proj, conv_w, alog_row, dt_row):
    bsz, t_total, _ = proj.shape
    tt = _div_tile(t_total, 256)
    hp = _halo_prev(tt)

    def body(x_ref, xp_ref, sm_ref, w_ref, al_ref, dt_ref, q_ref, k_ref, v_ref, g_ref):
        prev = jnp.where(pl.program_id(1) == 0, 0.0, xp_ref[0])
        y = _conv_rows(jnp.concatenate([prev, x_ref[0]], axis=0), w_ref, DN_CONV_K)[HALO:]
        q_ref[0], k_ref[0], v_ref[0] = _dn_qkv(y)
        g_ref[0] = _dn_gates(sm_ref[0], al_ref[...], dt_ref[...])

    out512 = jax.ShapeDtypeStruct((bsz, t_total, HEADS * HD), F32)
    return pl.pallas_call(
        body, name="dn_pre_fwd", grid=(bsz, t_total // tt),
        in_specs=[pl.BlockSpec((1, tt, QKV_W), lambda b, t: (b, t, 0)),
                  pl.BlockSpec((1, HALO, QKV_W), lambda b, t: (*hp(b, t), 0)),
                  pl.BlockSpec((1, tt, 128), lambda b, t: (b, t, SM_BLK)),
                  pl.BlockSpec((DN_CONV_K, QKV_W), lambda b, t: (0, 0)), _vec_spec(128), _vec_spec(128)],
        out_specs=(_tok_spec(tt, 512), _tok_spec(tt, 512), _tok_spec(tt, 512), _tok_spec(tt, 128)),
        out_shape=(out512, out512, out512, jax.ShapeDtypeStruct((bsz, t_total, 128), F32)),
        compiler_params=_params(("parallel", "parallel")))(proj, proj, proj, conv_w, alog_row, dt_row)


def _dn_pre_bwd(proj, dq, dk, dv, dgates, conv_w, alog_row, dt_row):
    bsz, t_total, _ = proj.shape
    tt = _div_tile(t_total, 128)
    nt = t_total // tt
    hp, hn = _halo_prev(tt), _halo_next(tt, t_total)

    def body(x_ref, xp_ref, xn_ref, sm_ref, dq_ref, dqn_ref, dk_ref, dkn_ref, dv_ref, dvn_ref, dg_ref,
             w_ref, al_ref, dt_ref, dx_ref, dsm_ref, dw_ref, dal_ref, ddt_ref):
        b, t = pl.program_id(0), pl.program_id(1)

        @pl.when((b == 0) & (t == 0))
        def _():
            dw_ref[...] = jnp.zeros_like(dw_ref)
            dal_ref[...] = jnp.zeros_like(dal_ref)
            ddt_ref[...] = jnp.zeros_like(ddt_ref)

        prev = jnp.where(t == 0, 0.0, xp_ref[0])
        rows = jnp.concatenate([prev, x_ref[0], xn_ref[0]], axis=0)
        y = _conv_rows(rows, w_ref, DN_CONV_K)[HALO:]
        valid = (_iota((tt + HALO, 1), 0) < tt) | (t < nt - 1)

        def ext(tile_ref, next_ref):
            return jnp.where(valid, jnp.concatenate([tile_ref[0], next_ref[0]], axis=0), 0.0)

        _, vjp_qkv = jax.vjp(_dn_qkv, y)
        (dy,) = vjp_qkv((ext(dq_ref, dqn_ref), ext(dk_ref, dkn_ref), ext(dv_ref, dvn_ref)))
        dy = jnp.where(valid, dy, 0.0)
        dx = dy * w_ref[DN_CONV_K - 1:DN_CONV_K, :]
        for s in range(1, DN_CONV_K):
            dx = dx + _shift_up(dy, s) * w_ref[DN_CONV_K - 1 - s:DN_CONV_K - s, :]
        dx_ref[0] = dx[:tt].astype(MXU_DT)
        dy_t = dy[:tt]
        for k in range(DN_CONV_K):
            s = DN_CONV_K - 1 - k
            xs = (rows if s == 0 else pltpu.roll(rows, s, 0))[HALO:HALO + tt]
            dw_ref[k:k + 1, :] += jnp.sum(dy_t * xs, axis=0, keepdims=True)
        _, vjp_g = jax.vjp(_dn_gates, sm_ref[0], al_ref[...], dt_ref[...])
        dsm, dal, ddt = vjp_g(dg_ref[0])
        dsm_ref[0] = dsm
        dal_ref[...] += dal
        ddt_ref[...] += ddt

    def tile(width, blk=0):
        return pl.BlockSpec((1, tt, width), lambda b, t: (b, t, blk))

    def halo(h, width):
        return pl.BlockSpec((1, HALO, width), lambda b, t: (*h(b, t), 0))

    return pl.pallas_call(
        body, name="dn_pre_bwd", grid=(bsz, nt),
        in_specs=[tile(QKV_W), halo(hp, QKV_W), halo(hn, QKV_W), tile(128, SM_BLK),
                  tile(512), halo(hn, 512), tile(512), halo(hn, 512), tile(512), halo(hn, 512), tile(128),
                  pl.BlockSpec((DN_CONV_K, QKV_W), lambda b, t: (0, 0)), _vec_spec(128), _vec_spec(128)],
        out_specs=(tile(QKV_W), tile(128), pl.BlockSpec((DN_CONV_K, QKV_W), lambda b, t: (0, 0)),
                   _vec_spec(128), _vec_spec(128)),
        out_shape=(jax.ShapeDtypeStruct((bsz, t_total, QKV_W), MXU_DT), jax.ShapeDtypeStruct((bsz, t_total, 128), F32),
                   jax.ShapeDtypeStruct((DN_CONV_K, QKV_W), F32), jax.ShapeDtypeStruct((1, 128), F32),
                   jax.ShapeDtypeStruct((1, 128), F32)),
        compiler_params=_params(("arbitrary", "arbitrary")))(
            proj, proj, proj, proj, dq, dq, dk, dk, dv, dv, dgates, conv_w, alog_row, dt_row)


def _state_spec(bsz, idx):
    return pl.BlockSpec((bsz, 1, HEADS, HD, HD), lambda c: (0, idx(c), 0, 0, 0))


def _chunk_spec(bsz, width, idx, blk=0):
    return pl.BlockSpec((bsz, CHUNK, width), lambda c: (0, idx(c), blk))


def _dn_rec_fwd(q, k, v, gates):
    bsz, t_total, _ = q.shape
    nc = t_total // CHUNK
    fwd = lambda c: c

    def body(q_ref, k_ref, v_ref, g_ref, o_ref, ss_ref, s_ref):
        @pl.when(pl.program_id(0) == 0)
        def _():
            s_ref[...] = jnp.zeros_like(s_ref)

        seqs = range(bsz)
        s_list = [[s_ref[b * HEADS + h] for h in range(HEADS)] for b in seqs]
        for b in seqs:
            for h in range(HEADS):
                ss_ref[b, 0, h] = s_list[b][h]
        o, new_s = _dn_chunk(s_list, [q_ref[b] for b in seqs], [k_ref[b] for b in seqs],
                             [v_ref[b] for b in seqs], [g_ref[b] for b in seqs])
        for b in seqs:
            o_ref[b] = o[b]
            for h in range(HEADS):
                s_ref[b * HEADS + h] = new_s[b][h]

    return pl.pallas_call(
        body, name="dn_rec_fwd", grid=(nc,),
        in_specs=[_chunk_spec(bsz, 512, fwd)] * 3 + [_chunk_spec(bsz, 128, fwd)],
        out_specs=(_chunk_spec(bsz, 512, fwd), _state_spec(bsz, fwd)),
        out_shape=(jax.ShapeDtypeStruct(q.shape, F32), jax.ShapeDtypeStruct((bsz, nc, HEADS, HD, HD), F32)),
        scratch_shapes=[pltpu.VMEM((bsz * HEADS, HD, HD), F32)],
        compiler_params=_params(("arbitrary",)))(q, k, v, gates)


def _dn_rec_bwd(q, k, v, gates, states, do):
    bsz, t_total, _ = q.shape
    nc = t_total // CHUNK
    rev = lambda c: nc - 1 - c

    def body(q_ref, k_ref, v_ref, g_ref, ss_ref, do_ref, dq_ref, dk_ref, dv_ref, dg_ref, ds_ref):
        @pl.when(pl.program_id(0) == 0)
        def _():
            ds_ref[...] = jnp.zeros_like(ds_ref)

        seqs = range(bsz)
        s_list = [[ss_ref[b, 0, h] for h in range(HEADS)] for b in seqs]
        _, vjp = jax.vjp(_dn_chunk, s_list, [q_ref[b] for b in seqs], [k_ref[b] for b in seqs],
                         [v_ref[b] for b in seqs], [g_ref[b] for b in seqs])
        ds_in, dq, dk, dv, dg = vjp(([do_ref[b] for b in seqs],
                                     [[ds_ref[b * HEADS + h] for h in range(HEADS)] for b in seqs]))
        for b in seqs:
            dq_ref[b], dk_ref[b], dv_ref[b], dg_ref[b] = dq[b], dk[b], dv[b], dg[b]
            for h in range(HEADS):
                ds_ref[b * HEADS + h] = ds_in[b][h]

    tok = lambda width: _chunk_spec(bsz, width, rev)
    out512 = jax.ShapeDtypeStruct(q.shape, F32)
    return pl.pallas_call(
        body, name="dn_rec_bwd", grid=(nc,),
        in_specs=[tok(512), tok(512), tok(512), tok(128), _state_spec(bsz, rev), tok(512)],
        out_specs=(tok(512), tok(512), tok(512), tok(128)),
        out_shape=(out512, out512, out512, jax.ShapeDtypeStruct(gates.shape, F32)),
        scratch_shapes=[pltpu.VMEM((bsz * HEADS, HD, HD), F32)],
        compiler_params=_params(("arbitrary",)))(q, k, v, gates, states, do)


GQ_BLK, GK_BLK, GV_BLK = P_GQ // 512, P_GK // 512, P_GV // 512


def _gla_rec_fwd(proj, w2, bg):
    bsz, t_total, _ = proj.shape
    nc = t_total // CHUNK

    fwd = lambda c: c

    def body(q_ref, k_ref, v_ref, sm_ref, w2_ref, bg_ref, o_ref, ss_ref, s_ref):
        @pl.when(pl.program_id(0) == 0)
        def _():
            s_ref[...] = jnp.zeros_like(s_ref)

        seqs = range(bsz)
        s_list = [[s_ref[b * HEADS + h] for h in range(HEADS)] for b in seqs]
        for b in seqs:
            for h in range(HEADS):
                ss_ref[b, 0, h] = s_list[b][h]
        o, new_s = _gla_chunk(s_list, [q_ref[b] for b in seqs], [k_ref[b] for b in seqs], [v_ref[b] for b in seqs],
                              [sm_ref[b] for b in seqs], w2_ref[...], bg_ref[...])
        for b in seqs:
            o_ref[b] = o[b]
            for h in range(HEADS):
                s_ref[b * HEADS + h] = new_s[b][h]

    col = lambda blk, width=512: _chunk_spec(bsz, width, fwd, blk)
    return pl.pallas_call(
        body, name="gla_rec_fwd", grid=(nc,),
        in_specs=[col(GQ_BLK), col(GK_BLK), col(GV_BLK), col(SM_BLK, 128),
                  pl.BlockSpec((128, 512), lambda c: (0, 0)), pl.BlockSpec((1, 512), lambda c: (0, 0))],
        out_specs=(col(0), _state_spec(bsz, fwd)),
        out_shape=(jax.ShapeDtypeStruct((bsz, t_total, 512), F32),
                   jax.ShapeDtypeStruct((bsz, nc, HEADS, HD, HD), F32)),
        scratch_shapes=[pltpu.VMEM((bsz * HEADS, HD, HD), F32)],
        compiler_params=_params(("arbitrary",)))(proj, proj, proj, proj, w2, bg)


def _gla_rec_bwd(proj, w2, bg, states, do, dsm_dn):
    bsz, t_total, _ = proj.shape
    nc = t_total // CHUNK
    rev = lambda c: nc - 1 - c

    def body(q_ref, k_ref, v_ref, sm_ref, w2_ref, bg_ref, ss_ref, do_ref, dsd_ref,
             dq_ref, dk_ref, dv_ref, dsm_ref, dw2_ref, dbg_ref, ds_ref):
        @pl.when(pl.program_id(0) == 0)
        def _():
            dw2_ref[...] = jnp.zeros_like(dw2_ref)
            dbg_ref[...] = jnp.zeros_like(dbg_ref)
            ds_ref[...] = jnp.zeros_like(ds_ref)

        seqs = range(bsz)
        s_list = [[ss_ref[b, 0, h] for h in range(HEADS)] for b in seqs]
        _, vjp = jax.vjp(_gla_chunk, s_list, [q_ref[b] for b in seqs], [k_ref[b] for b in seqs],
                         [v_ref[b] for b in seqs], [sm_ref[b] for b in seqs], w2_ref[...], bg_ref[...])
        ds_in, dq, dk, dv, dsm, dw2, dbg = vjp(([do_ref[b] for b in seqs],
                                                [[ds_ref[b * HEADS + h] for h in range(HEADS)] for b in seqs]))
        for b in seqs:
            dq_ref[b], dk_ref[b], dv_ref[b] = dq[b].astype(MXU_DT), dk[b].astype(MXU_DT), dv[b].astype(MXU_DT)
            dsm_ref[b] = (dsm[b] + dsd_ref[b]).astype(MXU_DT)
            for h in range(HEADS):
                ds_ref[b * HEADS + h] = ds_in[b][h]
        dw2_ref[...] += dw2
        dbg_ref[...] += dbg

    col = lambda blk, width=512: _chunk_spec(bsz, width, rev, blk)
    w2_spec = pl.BlockSpec((128, 512), lambda c: (0, 0))
    bg_spec = pl.BlockSpec((1, 512), lambda c: (0, 0))
    out512 = jax.ShapeDtypeStruct((bsz, t_total, 512), MXU_DT)
    return pl.pallas_call(
        body, name="gla_rec_bwd", grid=(nc,),
        in_specs=[col(GQ_BLK), col(GK_BLK), col(GV_BLK), col(SM_BLK, 128), w2_spec, bg_spec,
                  _state_spec(bsz, rev), col(0), col(0, 128)],
        out_specs=(col(0), col(0), col(0), col(0, 128), w2_spec, bg_spec),
        out_shape=(out512, out512, out512, jax.ShapeDtypeStruct((bsz, t_total, 128), MXU_DT),
                   jax.ShapeDtypeStruct((128, 512), F32), jax.ShapeDtypeStruct((1, 512), F32)),
        scratch_shapes=[pltpu.VMEM((bsz * HEADS, HD, HD), F32)],
        compiler_params=_params(("arbitrary",)))(proj, proj, proj, proj, w2, bg, states, do, dsm_dn)


Z_BLK, GG_BLK = P_Z // 512, P_GG // 512


def _mix_out_fwd(o_dn, o_gla, proj, grow_dn, grow_gla):
    bsz, t_total, _ = o_dn.shape
    tt = _div_tile(t_total, 256)

    def body(od_ref, og_ref, z_ref, gg_ref, gd_ref, gl_ref, o_ref):
        o_ref[0, :, :512] = _gate_norm(od_ref[0], z_ref[0], gd_ref[...]).astype(MXU_DT)
        o_ref[0, :, 512:] = _gate_norm(og_ref[0], gg_ref[0], gl_ref[...]).astype(MXU_DT)

    def col(blk):
        return pl.BlockSpec((1, tt, 512), lambda b, t: (b, t, blk))

    return pl.pallas_call(
        body, name="mix_out_fwd", grid=(bsz, t_total // tt),
        in_specs=[col(0), col(0), col(Z_BLK), col(GG_BLK), _vec_spec(512), _vec_spec(512)],
        out_specs=_tok_spec(tt), out_shape=jax.ShapeDtypeStruct((bsz, t_total, D), MXU_DT),
        compiler_params=_params(("parallel", "parallel")))(o_dn, o_gla, proj, proj, grow_dn, grow_gla)


def _mix_out_bwd(do, o_dn, o_gla, proj, grow_dn, grow_gla):
    bsz, t_total, _ = o_dn.shape
    tt = _div_tile(t_total, 256)

    def body(do_ref, od_ref, og_ref, z_ref, gg_ref, gd_ref, gl_ref,
             dod_ref, dog_ref, dz_ref, dgg_ref, dgd_ref, dgl_ref):
        @pl.when((pl.program_id(0) == 0) & (pl.program_id(1) == 0))
        def _():
            dgd_ref[...] = jnp.zeros_like(dgd_ref)
            dgl_ref[...] = jnp.zeros_like(dgl_ref)

        def one(o_ref, gate_ref, g_ref, ct, do_out, dgate_out, dg_out):
            _, vjp = jax.vjp(_gate_norm, o_ref[0], gate_ref[0], g_ref[...])
            d_o, d_gate, d_row = vjp(ct)
            do_out[0] = d_o
            dgate_out[0] = d_gate.astype(MXU_DT)
            acc = d_row[:, :HD]
            for h in range(1, HEADS):
                acc = acc + d_row[:, h * HD:(h + 1) * HD]
            dg_out[...] += acc

        ct = do_ref[0]
        one(od_ref, z_ref, gd_ref, ct[:, :512], dod_ref, dz_ref, dgd_ref)
        one(og_ref, gg_ref, gl_ref, ct[:, 512:], dog_ref, dgg_ref, dgl_ref)

    def col(blk):
        return pl.BlockSpec((1, tt, 512), lambda b, t: (b, t, blk))

    f512 = jax.ShapeDtypeStruct((bsz, t_total, 512), F32)
    b512 = jax.ShapeDtypeStruct((bsz, t_total, 512), MXU_DT)
    g128 = jax.ShapeDtypeStruct((1, HD), F32)
    return pl.pallas_call(
        body, name="mix_out_bwd", grid=(bsz, t_total // tt),
        in_specs=[_tok_spec(tt), col(0), col(0), col(Z_BLK), col(GG_BLK), _vec_spec(512), _vec_spec(512)],
        out_specs=(col(0), col(0), col(0), col(0), _vec_spec(HD), _vec_spec(HD)),
        out_shape=(f512, f512, b512, b512, g128, g128),
        compiler_params=_params(("arbitrary", "arbitrary")))(do, o_dn, o_gla, proj, proj, grow_dn, grow_gla)


def _sum_slots(x, name):
    n, rows, cols = x.shape
    tr = _div_tile(rows, max(8, (1 << 19) // cols))

    def body(x_ref, o_ref):
        acc = x_ref[0]
        for i in range(1, n):
            acc = acc + x_ref[i]
        o_ref[...] = acc

    return pl.pallas_call(
        body, name=name, grid=(rows // tr,),
        in_specs=[pl.BlockSpec((n, tr, cols), lambda i: (0, i, 0))],
        out_specs=pl.BlockSpec((tr, cols), lambda i: (i, 0)),
        out_shape=jax.ShapeDtypeStruct((rows, cols), F32), compiler_params=_params(("parallel",)))(x)


def _pair_add(g8, r1, core, name):
    _, rows, cols = g8.shape
    tr = _div_tile(rows, max(8, (1 << 19) // cols))
    g42 = g8.reshape(4, 2, rows, cols)

    def body(core_ref, g_ref, r_ref, o_ref):
        o_ref[0] = (g_ref[0, 0].astype(F32) + r_ref[0].astype(F32)).astype(o_ref.dtype)

    return pl.pallas_call(
        body, name=name,
        grid_spec=pltpu.PrefetchScalarGridSpec(
            num_scalar_prefetch=1, grid=(4, rows // tr),
            in_specs=[pl.BlockSpec((1, 1, tr, cols), lambda s, i, core_ref: (s, core_ref[0], i, 0)),
                      pl.BlockSpec((1, tr, cols), lambda s, i, core_ref: (s, i, 0))],
            out_specs=pl.BlockSpec((1, tr, cols), lambda s, i, core_ref: (s, i, 0))),
        out_shape=jax.ShapeDtypeStruct((4, rows, cols), g8.dtype),
        compiler_params=_params(("parallel", "parallel")))(core, g42, r1)


def _chip_add(p4, r2, chip, name):
    _, rows, cols = p4.shape
    tr = _div_tile(rows, max(8, (1 << 19) // cols))

    def body(chip_ref, p_ref, r_ref, o_ref):
        f = lambda a: a.astype(F32)
        o_ref[...] = ((f(p_ref[0]) + f(r_ref[0])) + f(r_ref[1])) + f(r_ref[2])

    return pl.pallas_call(
        body, name=name,
        grid_spec=pltpu.PrefetchScalarGridSpec(
            num_scalar_prefetch=1, grid=(rows // tr,),
            in_specs=[pl.BlockSpec((1, tr, cols), lambda i, chip_ref: (chip_ref[0], i, 0)),
                      pl.BlockSpec((3, tr, cols), lambda i, chip_ref: (0, i, 0))],
            out_specs=pl.BlockSpec((tr, cols), lambda i, chip_ref: (i, 0))),
        out_shape=jax.ShapeDtypeStruct((rows, cols), F32),
        compiler_params=_params(("parallel",)))(chip, p4, r2)


def _adamw(w, g, m, v, name):
    rows, cols = w.shape
    tr = _div_tile(rows, max(8, (1 << 18) // cols))

    def body(w_ref, g_ref, m_ref, v_ref, d_ref, nm_ref, nv_ref):
        g_ = g_ref[...]
        nm = ADAM_B1 * m_ref[...] + (1.0 - ADAM_B1) * g_
        nv = ADAM_B2 * v_ref[...] + (1.0 - ADAM_B2) * (g_ * g_)
        m_hat = nm / (1.0 - ADAM_B1 ** ADAM_STEP)
        v_hat = nv / (1.0 - ADAM_B2 ** ADAM_STEP)
        d_ref[...] = -ADAM_LR * (m_hat / (jnp.sqrt(v_hat) + ADAM_EPS) + ADAM_WD * w_ref[...])
        nm_ref[...] = nm
        nv_ref[...] = nv

    spec = pl.BlockSpec((tr, cols), lambda i: (i, 0))
    shp = jax.ShapeDtypeStruct((rows, cols), F32)
    return pl.pallas_call(body, name=name, grid=(rows // tr,), in_specs=[spec] * 4, out_specs=(spec,) * 3,
                          out_shape=(shp,) * 3, compiler_params=_params(("parallel",)))(w, g, m, v)


def _position():
    return lax.axis_index("x"), lax.axis_index("y"), lax.axis_index("c")


def _slot(px, py, pc):
    return 4 * px + 2 * py + pc


def _gather_small(x, name):
    rows, cols = x.shape

    def body(x_ref, o_ref, send_sems, recv_sems):
        mx, my, mc = _position()

        def peer(k):
            return (mx ^ ((k >> 2) & 1), my ^ ((k >> 1) & 1), mc ^ (k & 1))

        o_ref[_slot(mx, my, mc)] = x_ref[...]
        sends = []
        for k in range(1, N_DEV):
            cp = pltpu.make_async_remote_copy(src_ref=x_ref, dst_ref=o_ref.at[_slot(mx, my, mc)],
                                              send_sem=send_sems.at[k - 1], recv_sem=recv_sems.at[k - 1],
                                              device_id=peer(k), device_id_type=MESH)
            cp.start()
            sends.append(cp)
        for k in range(1, N_DEV):
            pltpu.make_async_remote_copy(src_ref=x_ref, dst_ref=o_ref.at[_slot(*peer(k))],
                                         send_sem=send_sems.at[k - 1], recv_sem=recv_sems.at[k - 1],
                                         device_id=peer(k), device_id_type=MESH).wait_recv()
        for cp in sends:
            cp.wait_send()

    return pl.pallas_call(
        body, name=name, out_shape=jax.ShapeDtypeStruct((N_DEV, rows, cols), x.dtype),
        in_specs=[pl.BlockSpec(memory_space=pltpu.VMEM)], out_specs=pl.BlockSpec(memory_space=pltpu.VMEM),
        scratch_shapes=[pltpu.SemaphoreType.DMA((N_DEV - 1,)), pltpu.SemaphoreType.DMA((N_DEV - 1,))],
        compiler_params=pltpu.CompilerParams(vmem_limit_bytes=VMEM_LIMIT_V7X))(x)


def _gather_big(shards):
    n = len(shards)

    def body(*refs):
        xs, outs = refs[:n], refs[n:2 * n]
        send_sems, recv_sems, local_sems = refs[2 * n:]
        mx, my, mc = _position()
        me, sibling = (mx, my, mc), (mx, my, 1 - mc)
        chips = [(1 - mx, my), (mx, 1 - my), (1 - mx, 1 - my)]

        def copy(a, k, block, to, src=None):
            dst = outs[a].at[_slot(*block)]
            return pltpu.make_async_remote_copy(src_ref=dst if src is None else src, dst_ref=dst,
                                                send_sem=send_sems.at[7 * a + k], recv_sem=recv_sems.at[7 * a + k],
                                                device_id=to, device_id_type=MESH)

        mine = [pltpu.make_async_copy(xs[a], outs[a].at[_slot(*me)], local_sems.at[a]) for a in range(n)]
        for cp in mine:
            cp.start()
        started = []
        for a in range(n):
            started.append(copy(a, 0, me, sibling, src=xs[a]))
            started += [copy(a, 1 + j, me, (*chip, mc), src=xs[a]) for j, chip in enumerate(chips)]
        for cp in started:
            cp.start()
        for j, chip in enumerate(chips):
            for a in range(n):
                copy(a, 1 + j, (*chip, mc), me).wait_recv()
                fwd = copy(a, 4 + j, (*chip, mc), sibling)
                fwd.start()
                started.append(fwd)
        for a in range(n):
            copy(a, 0, sibling, me).wait_recv()
            for j, chip in enumerate(chips):
                copy(a, 4 + j, (*chip, 1 - mc), me).wait_recv()
        for cp in started:
            cp.wait_send()
        for cp in mine:
            cp.wait()

    any_spec = pl.BlockSpec(memory_space=pl.ANY)
    return pl.pallas_call(
        body, name="gather_weights",
        out_shape=tuple(jax.ShapeDtypeStruct((N_DEV,) + s.shape, s.dtype) for s in shards),
        in_specs=[any_spec] * n, out_specs=(any_spec,) * n,
        scratch_shapes=[pltpu.SemaphoreType.DMA((7 * n,)), pltpu.SemaphoreType.DMA((7 * n,)),
                        pltpu.SemaphoreType.DMA((n,))])(*shards)


def _scatter_sibling(grads):
    n = len(grads)

    def body(*refs):
        gs, outs = refs[:n], refs[n:2 * n]
        send_sems, recv_sems = refs[2 * n:]
        mx, my, mc = _position()
        copies = []
        for a in range(n):
            for s in range(4):
                copies.append(pltpu.make_async_remote_copy(
                    src_ref=gs[a].at[2 * s + (1 - mc)], dst_ref=outs[a].at[s],
                    send_sem=send_sems.at[4 * a + s], recv_sem=recv_sems.at[4 * a + s],
                    device_id=(mx, my, 1 - mc), device_id_type=MESH))
        for cp in copies:
            cp.start()
        for cp in copies:
            cp.wait_recv()
        for cp in copies:
            cp.wait_send()

    any_spec = pl.BlockSpec(memory_space=pl.ANY)
    return pl.pallas_call(
        body, name="scatter_sibling",
        out_shape=tuple(jax.ShapeDtypeStruct((4,) + g.shape[1:], g.dtype) for g in grads),
        in_specs=[any_spec] * n, out_specs=(any_spec,) * n,
        scratch_shapes=[pltpu.SemaphoreType.DMA((4 * n,)), pltpu.SemaphoreType.DMA((4 * n,))])(*grads)


def _scatter_chips(sums):
    n = len(sums)

    def body(*refs):
        ps, outs = refs[:n], refs[n:2 * n]
        send_sems, recv_sems = refs[2 * n:]
        mx, my, mc = _position()
        chips = [(1 - mx, my), (mx, 1 - my), (1 - mx, 1 - my)]
        copies = []
        for a in range(n):
            for k, (cx, cy) in enumerate(chips):
                copies.append(pltpu.make_async_remote_copy(
                    src_ref=ps[a].at[2 * cx + cy], dst_ref=outs[a].at[k],
                    send_sem=send_sems.at[3 * a + k], recv_sem=recv_sems.at[3 * a + k],
                    device_id=(cx, cy, mc), device_id_type=MESH))
        for cp in copies:
            cp.start()
        for cp in copies:
            cp.wait_recv()
        for cp in copies:
            cp.wait_send()

    any_spec = pl.BlockSpec(memory_space=pl.ANY)
    return pl.pallas_call(
        body, name="scatter_chips",
        out_shape=tuple(jax.ShapeDtypeStruct((3,) + p.shape[1:], p.dtype) for p in sums),
        in_specs=[any_spec] * n, out_specs=(any_spec,) * n,
        scratch_shapes=[pltpu.SemaphoreType.DMA((3 * n,)), pltpu.SemaphoreType.DMA((3 * n,))])(*sums)


def _pad_heads(x, axis):
    shp = list(x.shape)
    x4 = x.reshape(shp[:axis] + [HEADS, GLA_KEY] + shp[axis + 1:])
    pad = [(0, 0)] * x4.ndim
    pad[axis + 1] = (0, HD - GLA_KEY)
    return jnp.pad(x4, pad).reshape(shp[:axis] + [HEADS * HD] + shp[axis + 1:])


def _unpad_heads(x, axis):
    shp = list(x.shape)
    x4 = x.reshape(shp[:axis] + [HEADS, HD] + shp[axis + 1:])
    x4 = lax.slice_in_dim(x4, 0, GLA_KEY, axis=axis + 1)
    return x4.reshape(shp[:axis] + [HEADS * GLA_KEY] + shp[axis + 1:])


O_Z_END, O_AB, O_GQ, O_GK, O_GV, O_R = 2048, 2048, 2056, 2312, 2568, 3592


def _pad_in_rows(wt):
    return jnp.concatenate([
        wt[:O_Z_END], _pad_heads(wt[O_GQ:O_GK], 0), _pad_heads(wt[O_GK:O_GV], 0), wt[O_GV:O_R],
        wt[O_AB:O_GQ], wt[O_R:], jnp.zeros((P_W - P_SM - 8 - GATE_RANK, wt.shape[1]), wt.dtype)], axis=0)


def _unpad_in_rows(gt):
    return jnp.concatenate([
        gt[:P_GQ], gt[P_SM:P_SM + 8], _unpad_heads(gt[P_GQ:P_GK], 0), _unpad_heads(gt[P_GK:P_GV], 0),
        gt[P_GV:P_SM], gt[P_SM + 8:P_SM + 8 + GATE_RANK]], axis=0)


def _lane_row(vals, width=128):
    return jnp.pad(vals.reshape(1, -1), ((0, 0), (0, width - vals.size)))


SMALL_NAMES = ["ln0_g", "ln0_b", "b_ada", "dn_conv", "dn_a_log", "dn_dt_bias", "dn_norm_g", "gla_w_gate2",
               "gla_b_gate", "gla_norm_g", "ln1_g", "ln1_b", "ffn_conv", "ffn_conv_b", "ln2_g", "ln2_b"]
WEIGHTS = ["ln0_g", "ln0_b", "w_ada", "b_ada", "w_in", "dn_conv", "dn_a_log", "dn_dt_bias", "dn_norm_g",
           "gla_w_gate2", "gla_b_gate", "gla_norm_g", "w_o", "ln1_g", "ln1_b", "ffn_w_up", "ffn_conv", "ffn_conv_b",
           "ffn_w_down", "ln2_g", "ln2_b"]


def kernel(x, c, ln0_g, ln0_b, w_ada, b_ada, w_in, dn_conv, dn_a_log, dn_dt_bias, dn_norm_g, gla_w_gate2, gla_b_gate, gla_norm_g, w_o, ln1_g, ln1_b, ffn_w_up, ffn_conv, ffn_conv_b, ffn_w_down, ln2_g, ln2_b, loss_target, m_ln0_g, m_ln0_b, m_w_ada, m_b_ada, m_w_in, m_dn_conv, m_dn_a_log, m_dn_dt_bias, m_dn_norm_g, m_gla_w_gate2, m_gla_b_gate, m_gla_norm_g, m_w_o, m_ln1_g, m_ln1_b, m_ffn_w_up, m_ffn_conv, m_ffn_conv_b, m_ffn_w_down, m_ln2_g, m_ln2_b, v_ln0_g, v_ln0_b, v_w_ada, v_b_ada, v_w_in, v_dn_conv, v_dn_a_log, v_dn_dt_bias, v_dn_norm_g, v_gla_w_gate2, v_gla_b_gate, v_gla_norm_g, v_w_o, v_ln1_g, v_ln1_b, v_ffn_w_up, v_ffn_conv, v_ffn_conv_b, v_ffn_w_down, v_ln2_g, v_ln2_b):
    args = dict(locals())
    w_given = {n: args[n] for n in WEIGHTS}
    m_given = {n: args["m_" + n] for n in WEIGHTS}
    v_given = {n: args["v_" + n] for n in WEIGHTS}
    bsz, t_total, _ = x.shape
    ntok = bsz * t_total
    mx, my, mc = _position()
    me = _slot(mx, my, mc)

    pack1 = jnp.concatenate([c.reshape(-1), dn_conv.reshape(-1), gla_w_gate2.reshape(-1), ffn_conv.reshape(-1)])
    n1 = pack1.size
    rows1 = -(-n1 // 1024) * 8
    pack1 = jnp.pad(pack1, (0, rows1 * 128 - n1)).reshape(rows1, 128)
    got1 = _gather_small(pack1, "gather_cond").reshape(N_DEV, -1)
    o1 = bsz * D
    o2 = o1 + dn_conv.size
    o3 = o2 + gla_w_gate2.size
    c_all = got1[:, :o1].reshape(N_DEV * bsz, D)
    dn_conv_f = got1[:, o1:o2].reshape(N_DEV, DN_CONV_K, -1).transpose(1, 0, 2).reshape(DN_CONV_K, QKV_W)
    gate2_f = got1[:, o2:o3].reshape(N_DEV, GATE_RANK, -1).transpose(1, 0, 2).reshape(GATE_RANK, HEADS * GLA_KEY)
    ffn_conv_f = got1[:, o3:n1].reshape(N_DEV, FFN_CONV_K, -1).transpose(1, 0, 2).reshape(FFN_CONV_K, 2 * D_FF)

    win_t = w_in[0].T.astype(MXU_DT)
    wup_t = ffn_w_up[0].T.astype(MXU_DT)
    win_all, wo_all, wup_all, wdn_all = _gather_big(
        [win_t, w_o[0].astype(MXU_DT), wup_t, ffn_w_down[0].astype(MXU_DT)])
    win_p = _pad_in_rows(win_all.reshape(IN_W, D))
    wo_f = wo_all.reshape(D, D)
    wup_f = _ffn_pair(wup_all.reshape(2 * D_FF, D), 0)
    wdn_f = wdn_all.reshape(D_FF, D)
    cw_p, cb_p = _ffn_pair(ffn_conv_f, 1), _ffn_pair(ffn_conv_b, 1)

    ncol = w_ada.shape[2]
    b_cols = lax.dynamic_slice_in_dim(b_ada, me * ncol, ncol, axis=1)
    mod_part = _ada_fwd(c_all, w_ada[0], b_cols)
    mod_all = _gather_small(mod_part.reshape(-1, 128), "gather_mod").reshape(N_DEV, N_DEV * bsz, ncol)
    mod = lax.dynamic_slice_in_dim(mod_all, me * bsz, bsz, axis=1).transpose(1, 0, 2).reshape(bsz, 6, 1, D)
    sh_a, sc_a, gt_a, sh_f, sc_f, gt_f = (mod[:, i] for i in range(6))

    g0, b0 = ln0_g.reshape(1, D), ln0_b.reshape(1, D)
    alog_row, dt_row = _lane_row(dn_a_log[0]), _lane_row(dn_dt_bias[0])
    grow_dn, grow_gla = jnp.tile(dn_norm_g, (1, HEADS)), jnp.tile(gla_norm_g, (1, HEADS))
    w2 = jnp.zeros((128, HEADS * HD), F32).at[SM_R:SM_R + GATE_RANK].set(_pad_heads(gate2_f, 1))
    bg = _pad_heads(gla_b_gate, 1)

    h_a = _ln0_mod(x, g0, b0, sc_a, sh_a)
    proj = _mm(h_a.reshape(ntok, D), win_p, "nt", F32, "mm_proj", tn=1408).reshape(bsz, t_total, P_W)
    q, k, v, gates = _dn_pre_fwd(proj, dn_conv_f, alog_row, dt_row)
    o_dn, s_dn = _dn_rec_fwd(q, k, v, gates)
    o_gla, s_gla = _gla_rec_fwd(proj, w2, bg)
    o_mix = _mix_out_fwd(o_dn, o_gla, proj, grow_dn, grow_gla)
    y = _mm(o_mix.reshape(ntok, D), wo_f, "nn", F32, "mm_wo", tn=1024).reshape(bsz, t_total, D)
    r1, h_f = _res_ln_mod(x, y, gt_a, g0, b0, ln1_g, ln1_b, sc_f, sh_f)
    up = _mm(h_f.reshape(ntok, D), wup_f, "nt", F32, "mm_up", tn=1408).reshape(bsz, t_total, 2 * D_FF)
    act = _ffn_act_fwd(up, cw_p, cb_p)
    y2 = _mm(act.reshape(ntok, D_FF), wdn_f, "nn", F32, "mm_down", tn=1024).reshape(bsz, t_total, D)
    loss_rows, dr2, dy2, dgt_f, d_ln2_g, d_ln2_b = _final_fwd_bwd(r1, y2, gt_f, ln1_g, ln1_b, ln2_g, ln2_b, loss_target)
    loss = lax.psum(0.5 * jnp.sum(loss_rows) / D, ("x", "y", "c"))

    dy2_2 = dy2.reshape(ntok, D)
    dact = _mm(dy2_2, wdn_f, "nt", F32, "mm_dact", tn=1408).reshape(bsz, t_total, D_FF)
    g_wdn = _mm(act.reshape(ntok, D_FF), dy2_2, "tn", MXU_DT, "mm_gwdn", tm=1408, tn=1024, tk=512)
    dup, d_cw_p, d_cb_p = _ffn_act_bwd(up, dact, cw_p, cb_p)
    d_ffn_conv, d_ffn_conv_b = _ffn_unpair(d_cw_p, 1), _ffn_unpair(d_cb_p, 1)
    dup_2 = dup.reshape(ntok, 2 * D_FF)
    dh_f = _mm(dup_2, wup_f, "nn", F32, "mm_dhf", tn=1024, tk=1408).reshape(bsz, t_total, D)
    g_wup_t = _mm(dup_2, h_f.reshape(ntok, D), "tn", MXU_DT, "mm_gwup", tm=1408, tn=1024, tk=512)
    dr1, dsc_f, dsh_f, d_ln1_g, d_ln1_b, dy, dgt_a = _ln_bwd_call(
        "ln1_bwd", dr2, dh_f, r1, ln1_g, ln1_b, sc_f, y=y, gt=gt_a)

    dy_2 = dy.reshape(ntok, D)
    do = _mm(dy_2, wo_f, "nt", F32, "mm_do", tn=1024).reshape(bsz, t_total, D)
    g_wo = _mm(o_mix.reshape(ntok, D), dy_2, "tn", MXU_DT, "mm_gwo", tm=1024, tn=1024, tk=512)
    do_dn, do_gla, dz, dgg, d_dn_norm, d_gla_norm = _mix_out_bwd(do, o_dn, o_gla, proj, grow_dn, grow_gla)
    dq, dk, dv, dgates = _dn_rec_bwd(q, k, v, gates, s_dn, do_dn)
    dqkv, dsm_dn, d_dn_conv, d_alog_row, d_dt_row = _dn_pre_bwd(proj, dq, dk, dv, dgates, dn_conv_f, alog_row, dt_row)
    dgq, dgk, dgv, dsm, d_w2, d_bg = _gla_rec_bwd(proj, w2, bg, s_gla, do_gla, dsm_dn)
    dproj = jnp.concatenate([dqkv, dz, dgq, dgk, dgv, dgg, dsm], axis=-1).reshape(ntok, P_W)
    dh_a = _mm(dproj, win_p, "nn", F32, "mm_dha", tn=1024, tk=1408).reshape(bsz, t_total, D)
    g_win_p = _mm(dproj, h_a.reshape(ntok, D), "tn", MXU_DT, "mm_gwin", tm=1408, tn=1024, tk=512)
    grad_x, dsc_a, dsh_a, d_ln0_g, d_ln0_b = _ln_bwd_call("ln0_bwd", dr1, dh_a, x, g0, b0, sc_a)

    big = [_unpad_in_rows(g_win_p).reshape(N_DEV, -1, D), g_wo.reshape(N_DEV, -1, D),
           _ffn_unpair(g_wup_t, 0).reshape(N_DEV, -1, D), g_wdn.reshape(N_DEV, -1, D)]
    from_sibling = _scatter_sibling(big)
    core = mc.reshape(1).astype(jnp.int32)
    chip_sums = [_pair_add(g8, r1_, core, f"pair_add_{i}") for i, (g8, r1_) in enumerate(zip(big, from_sibling))]
    from_chips = _scatter_chips(chip_sums)
    chip = (2 * mx + my).reshape(1).astype(jnp.int32)
    g_win_t, g_wo_s, g_wup_ts, g_wdn_s = (
        _chip_add(p4, r2_, chip, f"chip_add_{i}") for i, (p4, r2_) in enumerate(zip(chip_sums, from_chips)))

    dmod = jnp.concatenate([dsh_a, dsc_a, dgt_a, dsh_f, dsc_f, dgt_f], axis=1).reshape(-1)
    small_parts = {
        "ln0_g": d_ln0_g, "ln0_b": d_ln0_b, "ln1_g": d_ln1_g, "ln1_b": d_ln1_b, "ln2_g": d_ln2_g, "ln2_b": d_ln2_b,
        "dn_a_log": d_alog_row[:, :HEADS], "dn_dt_bias": d_dt_row[:, :HEADS],
        "dn_norm_g": d_dn_norm, "gla_norm_g": d_gla_norm, "gla_b_gate": _unpad_heads(d_bg, 1),
        "ffn_conv_b": d_ffn_conv_b, "dn_conv": d_dn_conv,
        "gla_w_gate2": _unpad_heads(d_w2[SM_R:SM_R + GATE_RANK], 1), "ffn_conv": d_ffn_conv}
    order = sorted(small_parts)
    flat = jnp.concatenate([small_parts[n].reshape(-1) for n in order] + [dmod])
    n3 = flat.size
    rows3 = -(-n3 // 1024) * 8
    pack3 = jnp.pad(flat, (0, rows3 * 128 - n3)).reshape(rows3, 128)
    got3 = _gather_small(pack3, "gather_small_grads")
    tot3 = _sum_slots(got3, "sum_small_grads").reshape(-1)
    grads = {}
    off = 0
    for n in order:
        size = small_parts[n].size
        grads[n] = tot3[off:off + size]
        off += size
    dmod_all = got3.reshape(N_DEV, -1)[:, off:off + dmod.size].reshape(N_DEV * bsz, 6 * D)
    dmod_cols = lax.dynamic_slice_in_dim(dmod_all, me * ncol, ncol, axis=1)
    g_wada, g_bada = _ada_bwd(c_all, dmod_all, dmod_cols)
    grads["b_ada"] = g_bada

    def col_shard(full, rows):
        part = full.reshape(rows, -1)
        width = part.shape[1] // N_DEV
        return lax.dynamic_slice_in_dim(part, me * width, width, axis=1)

    grads["dn_conv"] = col_shard(grads["dn_conv"], DN_CONV_K)
    grads["gla_w_gate2"] = col_shard(grads["gla_w_gate2"], GATE_RANK)
    grads["ffn_conv"] = col_shard(grads["ffn_conv"], FFN_CONV_K)
    grads = {n: g.reshape(w_given[n].shape) for n, g in grads.items()}
    grads["w_ada"] = g_wada.reshape(w_ada.shape)
    grads["w_in"] = g_win_t.T.reshape(w_in.shape)
    grads["w_o"] = g_wo_s.reshape(w_o.shape)
    grads["ffn_w_up"] = g_wup_ts.T.reshape(ffn_w_up.shape)
    grads["ffn_w_down"] = g_wdn_s.reshape(ffn_w_down.shape)

    delta, new_m, new_v = {}, {}, {}
    for n in ["w_ada", "w_in", "w_o", "ffn_w_up", "ffn_w_down"]:
        shp = w_given[n].shape
        two_d = lambda a: a.reshape(shp[-2], shp[-1])
        d_, m_, v_ = _adamw(two_d(w_given[n]), two_d(grads[n]), two_d(m_given[n]), two_d(v_given[n]), "adamw_" + n)
        delta[n], new_m[n], new_v[n] = d_.reshape(shp), m_.reshape(shp), v_.reshape(shp)

    def pack_small(src):
        flat_ = jnp.concatenate([src[n].reshape(-1) for n in SMALL_NAMES])
        rows_ = -(-flat_.size // 1024) * 8
        return jnp.pad(flat_, (0, rows_ * 128 - flat_.size)).reshape(rows_, 128)

    d_s, m_s, v_s = _adamw(pack_small(w_given), pack_small(grads), pack_small(m_given), pack_small(v_given),
                           "adamw_small")
    off = 0
    for n in SMALL_NAMES:
        size, shp = w_given[n].size, w_given[n].shape
        delta[n] = d_s.reshape(-1)[off:off + size].reshape(shp)
        new_m[n] = m_s.reshape(-1)[off:off + size].reshape(shp)
        new_v[n] = v_s.reshape(-1)[off:off + size].reshape(shp)
        off += size

    return (loss, grad_x, *[grads[n] for n in WEIGHTS], *[delta[n] for n in WEIGHTS],
            *[new_m[n] for n in WEIGHTS], *[new_v[n] for n in WEIGHTS])
```

```python
import functools

import jax
import jax.numpy as jnp
from jax import lax
from jax.experimental import pallas as pl
from jax.experimental.pallas import tpu as pltpu

F32 = jnp.float32
MXU_DT = jnp.bfloat16
HI = lax.Precision.HIGHEST
MESH = pl.DeviceIdType.MESH
N_DEV = 8

D = 1024
HEADS = 4
HD = 128
CHUNK = 64
GLA_KEY = 64
GLA_TAU = 16.0
GATE_RANK = 16
D_FF = 2816
IN_W = 3608
ALPHA = 2.0 ** 0.25
EPS = 1e-6
DN_CONV_K = 4
FFN_CONV_K = 3
HALO = 8

P_QKV, P_Z, P_GQ, P_GK, P_GV, P_GG, P_SM, P_W = 0, 1536, 2048, 2560, 3072, 3584, 4096, 4224
SM_A, SM_B, SM_R = 0, 4, 8

ADAM_LR, ADAM_B1, ADAM_B2, ADAM_EPS, ADAM_WD, ADAM_STEP = 0.001, 0.9, 0.999, 1e-08, 0.01, 10

VMEM_LIMIT_V7X = 56 * 1024 * 1024


def _params(sem=None):
    return pltpu.CompilerParams(dimension_semantics=sem, vmem_limit_bytes=VMEM_LIMIT_V7X)


def _dg(a, b, dims, prec=None):
    return lax.dot_general(a, b, (dims, ((), ())), precision=prec, preferred_element_type=F32)


def _dot(a, b, prec=None):
    return _dg(a, b, ((1,), (0,)), prec)


def _dot_nt(a, b, prec=None):
    return _dg(a, b, ((1,), (1,)), prec)


def _dot_tn(a, b, prec=None):
    return _dg(a, b, ((0,), (0,)), prec)


def _iota(shape, dim):
    return lax.broadcasted_iota(jnp.int32, shape, dim)


def _sigmoid(x):
    return jax.nn.sigmoid(x)


def _silu(x):
    return x * _sigmoid(x)


def _softplus(x):
    return jnp.maximum(x, 0.0) + jnp.log(1.0 + jnp.exp(-jnp.abs(x)))


def _ln_stats(x):
    mu = jnp.mean(x, axis=-1, keepdims=True)
    xc = x - mu
    rstd = lax.rsqrt(jnp.mean(xc * xc, axis=-1, keepdims=True) + EPS)
    return xc * rstd, rstd


def _ln_bwd(dxhat, xhat, rstd):
    return rstd * (dxhat - jnp.mean(dxhat, axis=-1, keepdims=True)
                   - xhat * jnp.mean(dxhat * xhat, axis=-1, keepdims=True))


NN, NT, TN = ((1,), (0,)), ((1,), (1,)), ((0,), (0,))


def _split2(a):
    hi = a.astype(jnp.bfloat16)
    return hi, (a - hi.astype(F32)).astype(jnp.bfloat16)


def _d3(a, b, dims):
    ah, al = _split2(a)
    bh, bl = _split2(b)
    return _dg(ah, bh, dims) + (_dg(ah, bl, dims) + _dg(al, bh, dims))


@jax.custom_vjp
def _dot3(a, b):
    return _d3(a, b, NN)


_dot3.defvjp(lambda a, b: (_d3(a, b, NN), (a, b)),
             lambda res, g: (_d3(g, res[1], NT), _d3(res[0], g, TN)))


def _split3(b):
    b1 = b.astype(jnp.bfloat16)
    r1 = b - b1.astype(F32)
    b2 = r1.astype(jnp.bfloat16)
    return b1, b2, (r1 - b2.astype(F32)).astype(jnp.bfloat16)


def _sum3(fn, b):
    b1, b2, b3 = _split3(b)
    return fn(b1) + (fn(b2) + fn(b3))


@jax.custom_vjp
def _mask_dot(e, b):
    return _sum3(lambda t: _dg(e, t, NN), b)


_mask_dot.defvjp(lambda e, b: (_mask_dot(e, b), e),
                 lambda e, g: (jnp.zeros_like(e), _sum3(lambda t: _dg(e, t, TN), g)))


@jax.custom_vjp
def _mask_dot_nt(e, b):
    return _sum3(lambda t: _dg(e, t, NT), b)


_mask_dot_nt.defvjp(lambda e, b: (_mask_dot_nt(e, b), e),
                    lambda e, g: (jnp.zeros_like(e), _sum3(lambda t: _dg(t, e, TN), g)))


def _tri_inv_impl(ms):
    n = ms[0].shape[0]
    r, c = _iota((n, n), 0), _iota((n, n), 1)
    eye = (r == c).astype(F32)
    diag = (r >> 3) == (c >> 3)
    ds = [jnp.where(diag, m, 0.0) for m in ms]
    d2s = [_d3(d, d, NN) for d in ds]
    d4s = [_d3(d2, d2, NN) for d2 in d2s]
    invs = [_d3(eye - d, eye + d2, NN) for d, d2 in zip(ds, d2s)]
    invs = [_d3(inv, eye + d4, NN) for inv, d4 in zip(invs, d4s)]
    shift = 3
    while (1 << shift) < n:
        rb, cb = r >> shift, c >> shift
        sel = ((rb & 1) == 1) & (cb == rb - 1)
        tmp = [_d3(inv, jnp.where(sel, m, 0.0), NN) for inv, m in zip(invs, ms)]
        invs = [inv - _d3(t, inv, NN) for t, inv in zip(tmp, invs)]
        shift += 1
    return invs


@jax.custom_vjp
def _tri_inv(ms):
    return _tri_inv_impl(ms)


def _tri_inv_fwd(ms):
    invs = _tri_inv_impl(ms)
    return invs, invs


def _tri_inv_bwd(invs, das):
    tmp = [_d3(a, da, TN) for a, da in zip(invs, das)]
    return ([-_d3(t, a, NT) for t, a in zip(tmp, invs)],)


_tri_inv.defvjp(_tri_inv_fwd, _tri_inv_bwd)


@jax.custom_vjp
def _tri_inv_known(ms, invs):
    return invs


_tri_inv_known.defvjp(lambda ms, invs: (invs, invs),
                      lambda invs, das: (_tri_inv_bwd(invs, das)[0], [jnp.zeros_like(a) for a in invs]))


def _dn_chunk(s_list, q, k, v, gates, inv_known=None, with_inv=False):
    nb = len(q)
    c = q[0].shape[0]
    r64, c64 = _iota((c, c), 0), _iota((c, c), 1)
    causal = r64 >= c64
    strict = r64 > c64
    tri = causal.astype(jnp.bfloat16)
    eye = (_iota((HD, HD), 0) == _iota((HD, HD), 1)).astype(jnp.bfloat16)
    lane = _iota(gates[0].shape, 1)
    lane1 = _iota((1, HD), 1)
    g_all = [_mask_dot(tri, g) for g in gates]
    g_all_t = [_mask_dot_nt(eye, g) for g in g_all]
    row = _iota(g_all_t[0].shape, 0)
    last = [jnp.sum(g, axis=0, keepdims=True) for g in gates]
    prob = [(b, h) for b in range(nb) for h in range(HEADS)]
    sl = [slice(h * HD, (h + 1) * HD) for h in range(HEADS)]
    qh = [q[b][:, sl[h]] for b, h in prob]
    kh = [k[b][:, sl[h]] for b, h in prob]
    vh = [v[b][:, sl[h]] for b, h in prob]
    s = [s_list[b][h] for b, h in prob]
    beta = [jnp.sum(jnp.where(lane == SM_B + h, gates[b], 0.0), axis=-1, keepdims=True) for b, h in prob]
    g_c = [jnp.sum(jnp.where(lane == SM_A + h, g_all[b], 0.0), axis=-1, keepdims=True) for b, h in prob]
    g_r = [jnp.sum(jnp.where(row == SM_A + h, g_all_t[b], 0.0), axis=0, keepdims=True) for b, h in prob]
    g_last = [jnp.sum(jnp.where(lane1 == SM_A + h, last[b], 0.0), axis=-1, keepdims=True) for b, h in prob]
    decay = [jnp.where(causal, jnp.exp(jnp.where(causal, gc - gr, 0.0)), 0.0) for gc, gr in zip(g_c, g_r)]
    kb = [k_ * b_ for k_, b_ in zip(kh, beta)]
    m_low = [jnp.where(strict, _dot_nt(kb_, k_) * d_, 0.0) for kb_, k_, d_ in zip(kb, kh, decay)]
    attn = [_dot_nt(q_, k_) * d_ for q_, k_, d_ in zip(qh, kh, decay)]
    a_inv = _tri_inv(m_low) if inv_known is None else _tri_inv_known(m_low, inv_known)
    eg = [jnp.exp(gc) for gc in g_c]
    uw = [_dot3(a_, jnp.concatenate([v_ * b_, kb_ * e_], axis=1))
          for a_, v_, b_, kb_, e_ in zip(a_inv, vh, beta, kb, eg)]
    v_new = [uw_[:, :HD] - _dot(uw_[:, HD:], s_) for uw_, s_ in zip(uw, s)]
    qs = [_dot(q_ * e_, s_) for q_, e_, s_ in zip(qh, eg, s)]
    o = [qs_ + _dot(a_, vn_) for qs_, a_, vn_ in zip(qs, attn, v_new)]
    k_dec = [k_ * jnp.exp(gl - gc) for k_, gl, gc in zip(kh, g_last, g_c)]
    s_new = [s_ * jnp.exp(gl) + _dot_tn(kd_, vn_) for s_, gl, kd_, vn_ in zip(s, g_last, k_dec, v_new)]
    outs = [jnp.concatenate(o[b * HEADS:(b + 1) * HEADS], axis=-1) for b in range(nb)]
    states = [s_new[b * HEADS:(b + 1) * HEADS] for b in range(nb)]
    return (outs, states, a_inv) if with_inv else (outs, states)


def _gla_chunk(st_list, q, k, v, small, w2, bg):
    nb = len(q)
    c = q[0].shape[0]
    causal = _iota((c, c), 0) >= _iota((c, c), 1)
    tri = causal.astype(jnp.bfloat16)
    la_all = [-_softplus(-(_dot(sm, w2) + bg)) * (1.0 / GLA_TAU) for sm in small]
    b_all = [_mask_dot(tri, la) for la in la_all]
    prob = [(b, h) for b in range(nb) for h in range(HEADS)]
    sl = [slice(h * HD, (h + 1) * HD) for h in range(HEADS)]
    kh = [k[b][:, sl[h]] for b, h in prob]
    vh = [v[b][:, sl[h]] for b, h in prob]
    st = [st_list[b][h] for b, h in prob]
    bc = [b_all[b][:, sl[h]] for b, h in prob]
    b_last = [jnp.sum(la_all[b][:, sl[h]], axis=0, keepdims=True) for b, h in prob]
    q_dec = [q[b][:, sl[h]] * (GLA_KEY ** -0.5) * jnp.exp(bc_) for (b, h), bc_ in zip(prob, bc)]
    attn = [jnp.where(causal, _dot_nt(qd, k_ * jnp.exp(-bc_)), 0.0) for qd, k_, bc_ in zip(q_dec, kh, bc)]
    inter = [_dot_nt(qd, st_) for qd, st_ in zip(q_dec, st)]
    o = [i_ + _dot(a_, v_) for i_, a_, v_ in zip(inter, attn, vh)]
    k_dec = [k_ * jnp.exp(bl - bc_) for k_, bl, bc_ in zip(kh, b_last, bc)]
    s_new = [st_ * jnp.exp(bl) + _dot_tn(v_, kd) for st_, bl, v_, kd in zip(st, b_last, vh, k_dec)]
    outs = [jnp.concatenate(o[b * HEADS:(b + 1) * HEADS], axis=-1) for b in range(nb)]
    return outs, [s_new[b * HEADS:(b + 1) * HEADS] for b in range(nb)]


def _dn_qkv(y):
    act = _silu(y)
    parts = []
    for i in range(2 * HEADS):
        xh = act[:, i * HD:(i + 1) * HD]
        xh = xh * lax.rsqrt(jnp.sum(xh * xh, axis=-1, keepdims=True) + EPS)
        parts.append(xh * (HD ** -0.5) if i < HEADS else xh)
    qk = jnp.concatenate(parts, axis=-1)
    return qk[:, :HEADS * HD], qk[:, HEADS * HD:], act[:, 2 * HEADS * HD:]


def _dn_gates(small, alog_row, dt_row):
    lane = _iota(small.shape, 1)
    log_a = -jnp.exp(alog_row) * _softplus(small + dt_row)
    return jnp.where(lane < SM_B, log_a, jnp.where(lane < SM_R, _sigmoid(small), 0.0))


def _gate_norm(o, z, grow):
    parts = []
    for h in range(HEADS):
        oh = o[:, h * HD:(h + 1) * HD]
        parts.append(oh * lax.rsqrt(jnp.mean(oh * oh, axis=-1, keepdims=True) + EPS))
    return jnp.concatenate(parts, axis=-1) * grow * _silu(z)


def _conv_rows(xrows, w_ref, k_taps):
    n = xrows.shape[0]
    acc = xrows * w_ref[k_taps - 1:k_taps, :]
    for s in range(1, k_taps):
        acc = acc + pltpu.roll(xrows, s, 0) * w_ref[k_taps - 1 - s:k_taps - s, :]
    return acc


def _shift_up(x, s):
    return x if s == 0 else pltpu.roll(x, x.shape[0] - s, 0)


def _div_tile(n, cap, mult=8):
    best = None
    for t in range(mult, min(n, cap) + 1, mult):
        if n % t == 0:
            best = t
    return best if best is not None else n


def _halo_prev(tt):
    return lambda b, t: (b, jnp.maximum(t * (tt // HALO) - 1, 0))


def _halo_next(tt, t_total):
    return lambda b, t: (b, jnp.minimum((t + 1) * (tt // HALO), t_total // HALO - 1))


def _mm(a, b, mode, out_dtype, name, tm=512, tn=512, tk=None):
    if mode == "nn":
        (m, k), n = a.shape, b.shape[1]
    elif mode == "nt":
        (m, k), n = a.shape, b.shape[0]
    else:
        (k, m), n = a.shape, b.shape[1]
    tm, tn = min(tm, m), min(tn, n)
    tk = k if tk is None else min(tk, k)
    assert m % tm == 0 and n % tn == 0 and k % tk == 0, (name, a.shape, b.shape, tm, tn, tk)
    nk = k // tk
    if mode == "tn":
        a_spec = pl.BlockSpec((tk, tm), lambda i, j, kk: (kk, i))
    else:
        a_spec = pl.BlockSpec((tm, tk), lambda i, j, kk: (i, kk))
    if mode == "nt":
        b_spec = pl.BlockSpec((tn, tk), lambda i, j, kk: (j, kk))
    else:
        b_spec = pl.BlockSpec((tk, tn), lambda i, j, kk: (kk, j))
    dims = {"nn": ((1,), (0,)), "nt": ((1,), (1,)), "tn": ((0,), (0,))}[mode]

    def body(a_ref, b_ref, o_ref, *acc):
        p = _dg(a_ref[...], b_ref[...], dims)
        if nk == 1:
            o_ref[...] = p.astype(out_dtype)
        else:
            kk = pl.program_id(2)

            @pl.when(kk == 0)
            def _():
                acc[0][...] = p

            @pl.when(kk > 0)
            def _():
                acc[0][...] += p

            @pl.when(kk == nk - 1)
            def _():
                o_ref[...] = acc[0][...].astype(out_dtype)

    return pl.pallas_call(
        body, name=name, grid=(m // tm, n // tn, nk),
        in_specs=[a_spec, b_spec],
        out_specs=pl.BlockSpec((tm, tn), lambda i, j, kk: (i, j)),
        out_shape=jax.ShapeDtypeStruct((m, n), out_dtype),
        scratch_shapes=[pltpu.VMEM((tm, tn), F32)] if nk > 1 else [],
        compiler_params=_params(("parallel", "parallel", "arbitrary")),
    )(a, b)


def _ada_fwd(c_all, w_ada, b_cols):
    def body(c_ref, w_ref, b_ref, o_ref):
        cond = _silu(c_ref[...]).astype(MXU_DT)
        o_ref[...] = _dot(cond, w_ref[...].astype(MXU_DT)) + b_ref[...]

    return pl.pallas_call(body, name="ada_fwd", out_shape=jax.ShapeDtypeStruct((c_all.shape[0], w_ada.shape[1]), F32),
                          compiler_params=_params())(c_all, w_ada, b_cols)


def _ada_bwd(c_all, dmod_all, dmod_cols):
    def body(c_ref, da_ref, dc_ref, gw_ref, gb_ref):
        cond = _silu(c_ref[...]).astype(MXU_DT)
        gw_ref[...] = _dot_tn(cond, dc_ref[...].astype(MXU_DT))
        gb_ref[...] = jnp.sum(da_ref[...], axis=0, keepdims=True)

    return pl.pallas_call(
        body, name="ada_bwd",
        out_shape=(jax.ShapeDtypeStruct((c_all.shape[1], dmod_cols.shape[1]), F32),
                   jax.ShapeDtypeStruct((1, dmod_all.shape[1]), F32)),
        compiler_params=_params())(c_all, dmod_all, dmod_cols)


def _tok_spec(tt, width=D):
    return pl.BlockSpec((1, tt, width), lambda b, t: (b, t, 0))


def _vec_spec(width=D):
    return pl.BlockSpec((1, width), lambda b, t: (0, 0))


def _bvec_spec(width=D):
    return pl.BlockSpec((1, 1, width), lambda b, t: (b, 0, 0))


def _ln0_mod(x, g0, b0, sc, sh):
    bsz, t_total, _ = x.shape
    tt = _div_tile(t_total, 256)

    def body(x_ref, g_ref, b_ref, sc_ref, sh_ref, h_ref):
        xh, _ = _ln_stats(x_ref[0])
        x0 = xh * g_ref[...] + b_ref[...]
        h_ref[0] = (x0 * (1.0 + sc_ref[0]) + sh_ref[0]).astype(MXU_DT)

    return pl.pallas_call(
        body, name="ln0_mod", grid=(bsz, t_total // tt),
        in_specs=[_tok_spec(tt), _vec_spec(), _vec_spec(), _bvec_spec(), _bvec_spec()],
        out_specs=_tok_spec(tt), out_shape=jax.ShapeDtypeStruct(x.shape, MXU_DT),
        compiler_params=_params(("parallel", "parallel")))(x, g0, b0, sc, sh)


def _res_ln_mod(x, y, gt, g0, b0, g1, b1, sc, sh):
    bsz, t_total, _ = x.shape
    tt = _div_tile(t_total, 256)

    def body(x_ref, y_ref, gt_ref, g0_ref, b0_ref, g1_ref, b1_ref, sc_ref, sh_ref, r_ref, h_ref):
        xh, _ = _ln_stats(x_ref[0])
        r = ALPHA * (xh * g0_ref[...] + b0_ref[...]) + (1.0 + gt_ref[0]) * y_ref[0]
        r_ref[0] = r
        rh, _ = _ln_stats(r)
        x1 = rh * g1_ref[...] + b1_ref[...]
        h_ref[0] = (x1 * (1.0 + sc_ref[0]) + sh_ref[0]).astype(MXU_DT)

    return pl.pallas_call(
        body, name="res_ln_mod", grid=(bsz, t_total // tt),
        in_specs=[_tok_spec(tt), _tok_spec(tt), _bvec_spec(), _vec_spec(), _vec_spec(), _vec_spec(), _vec_spec(),
                  _bvec_spec(), _bvec_spec()],
        out_specs=(_tok_spec(tt), _tok_spec(tt)),
        out_shape=(jax.ShapeDtypeStruct(x.shape, F32), jax.ShapeDtypeStruct(x.shape, MXU_DT)),
        compiler_params=_params(("parallel", "parallel")))(x, y, gt, g0, b0, g1, b1, sc, sh)


def _final_fwd_bwd(r1, y2, gt, g1, b1, g2, b2, target):
    bsz, t_total, _ = r1.shape
    tt = _div_tile(t_total, 256)

    def body(r1_ref, y2_ref, gt_ref, g1_ref, b1_ref, g2_ref, b2_ref, tg_ref,
             loss_ref, dr2_ref, dy2_ref, dgt_ref, dg2_ref, db2_ref):
        b, t = pl.program_id(0), pl.program_id(1)

        @pl.when((b == 0) & (t == 0))
        def _():
            loss_ref[...] = jnp.zeros_like(loss_ref)
            dg2_ref[...] = jnp.zeros_like(dg2_ref)
            db2_ref[...] = jnp.zeros_like(db2_ref)

        @pl.when(t == 0)
        def _():
            dgt_ref[...] = jnp.zeros_like(dgt_ref)

        rh1, _ = _ln_stats(r1_ref[0])
        x1 = rh1 * g1_ref[...] + b1_ref[...]
        y2 = y2_ref[0]
        gate = 1.0 + gt_ref[0]
        xh2, rstd2 = _ln_stats(ALPHA * x1 + gate * y2)
        err = xh2 * g2_ref[...] + b2_ref[...] - tg_ref[0]
        loss_ref[...] += jnp.sum(err * err, axis=0, keepdims=True)
        dx2 = err * (1.0 / D)
        dg2_ref[...] += jnp.sum(dx2 * xh2, axis=0, keepdims=True)
        db2_ref[...] += jnp.sum(dx2, axis=0, keepdims=True)
        dr2 = _ln_bwd(dx2 * g2_ref[...], xh2, rstd2)
        dr2_ref[0] = dr2
        dy2_ref[0] = (gate * dr2).astype(MXU_DT)
        dgt_ref[0] += jnp.sum(dr2 * y2, axis=0, keepdims=True)

    vec_out = jax.ShapeDtypeStruct((1, D), F32)
    return pl.pallas_call(
        body, name="final_fwd_bwd", grid=(bsz, t_total // tt),
        in_specs=[_tok_spec(tt), _tok_spec(tt), _bvec_spec(), _vec_spec(), _vec_spec(), _vec_spec(), _vec_spec(),
                  _tok_spec(tt)],
        out_specs=(_vec_spec(), _tok_spec(tt), _tok_spec(tt), _bvec_spec(), _vec_spec(), _vec_spec()),
        out_shape=(vec_out, jax.ShapeDtypeStruct(r1.shape, F32), jax.ShapeDtypeStruct(r1.shape, MXU_DT),
                   jax.ShapeDtypeStruct((bsz, 1, D), F32), vec_out, vec_out),
        compiler_params=_params(("arbitrary", "arbitrary")))(r1, y2, gt, g1, b1, g2, b2, target)


def _ln_bwd_call(name, d_res, d_h, src, g, b, sc, y=None, gt=None):
    bsz, t_total, _ = src.shape
    tt = _div_tile(t_total, 256)
    has_y = y is not None

    def body(*refs):
        if has_y:
            (dres_ref, dh_ref, src_ref, g_ref, b_ref, sc_ref, y_ref, gt_ref,
             dsrc_ref, dsc_ref, dsh_ref, dg_ref, db_ref, dy_ref, dgt_ref) = refs
        else:
            (dres_ref, dh_ref, src_ref, g_ref, b_ref, sc_ref,
             dsrc_ref, dsc_ref, dsh_ref, dg_ref, db_ref) = refs
        bi, t = pl.program_id(0), pl.program_id(1)

        @pl.when((bi == 0) & (t == 0))
        def _():
            dg_ref[...] = jnp.zeros_like(dg_ref)
            db_ref[...] = jnp.zeros_like(db_ref)

        @pl.when(t == 0)
        def _():
            dsc_ref[...] = jnp.zeros_like(dsc_ref)
            dsh_ref[...] = jnp.zeros_like(dsh_ref)
            if has_y:
                dgt_ref[...] = jnp.zeros_like(dgt_ref)

        xh, rstd = _ln_stats(src_ref[0])
        xv = xh * g_ref[...] + b_ref[...]
        dh = dh_ref[0]
        dx = ALPHA * dres_ref[0] + dh * (1.0 + sc_ref[0])
        dsc_ref[0] += jnp.sum(dh * xv, axis=0, keepdims=True)
        dsh_ref[0] += jnp.sum(dh, axis=0, keepdims=True)
        dg_ref[...] += jnp.sum(dx * xh, axis=0, keepdims=True)
        db_ref[...] += jnp.sum(dx, axis=0, keepdims=True)
        dsrc = _ln_bwd(dx * g_ref[...], xh, rstd)
        dsrc_ref[0] = dsrc
        if has_y:
            dy_ref[0] = ((1.0 + gt_ref[0]) * dsrc).astype(MXU_DT)
            dgt_ref[0] += jnp.sum(dsrc * y_ref[0], axis=0, keepdims=True)

    vec_out = jax.ShapeDtypeStruct((1, D), F32)
    bvec_out = jax.ShapeDtypeStruct((bsz, 1, D), F32)
    in_specs = [_tok_spec(tt), _tok_spec(tt), _tok_spec(tt), _vec_spec(), _vec_spec(), _bvec_spec()]
    out_specs = [_tok_spec(tt), _bvec_spec(), _bvec_spec(), _vec_spec(), _vec_spec()]
    out_shape = [jax.ShapeDtypeStruct(src.shape, F32), bvec_out, bvec_out, vec_out, vec_out]
    args = [d_res, d_h, src, g, b, sc]
    if has_y:
        in_specs += [_tok_spec(tt), _bvec_spec()]
        out_specs += [_tok_spec(tt), _bvec_spec()]
        out_shape += [jax.ShapeDtypeStruct(src.shape, MXU_DT), bvec_out]
        args += [y, gt]
    return pl.pallas_call(body, name=name, grid=(bsz, t_total // tt), in_specs=in_specs, out_specs=tuple(out_specs),
                          out_shape=tuple(out_shape), compiler_params=_params(("arbitrary", "arbitrary")))(*args)


FFN_TC = 256
FFN_NJ = D_FF // FFN_TC
FFN_PW = 2 * FFN_TC


def _ffn_pair(a, axis):
    shp = list(a.shape)
    a4 = a.reshape(shp[:axis] + [2, FFN_NJ, FFN_TC] + shp[axis + 1:])
    return jnp.swapaxes(a4, axis, axis + 1).reshape(shp)


def _ffn_unpair(a, axis):
    shp = list(a.shape)
    a4 = a.reshape(shp[:axis] + [FFN_NJ, 2, FFN_TC] + shp[axis + 1:])
    return jnp.swapaxes(a4, axis, axis + 1).reshape(shp)


def _ffn_act_fwd(up, cw, cb):
    bsz, t_total, _ = up.shape
    tt = _div_tile(t_total, 256)
    hp = _halo_prev(tt)

    def body(x_ref, xp_ref, w_ref, b_ref, o_ref):
        prev = jnp.where(pl.program_id(1) == 0, 0.0, xp_ref[0])
        rows = jnp.concatenate([prev, x_ref[0]], axis=0)
        u = _conv_rows(rows, w_ref, FFN_CONV_K)[HALO:] + b_ref[...]
        o_ref[0] = (_silu(u[:, :FFN_TC]) * u[:, FFN_TC:]).astype(MXU_DT)

    return pl.pallas_call(
        body, name="ffn_act_fwd", grid=(bsz, t_total // tt, FFN_NJ),
        in_specs=[pl.BlockSpec((1, tt, FFN_PW), lambda b, t, j: (b, t, j)),
                  pl.BlockSpec((1, HALO, FFN_PW), lambda b, t, j: (*hp(b, t), j)),
                  pl.BlockSpec((FFN_CONV_K, FFN_PW), lambda b, t, j: (0, j)),
                  pl.BlockSpec((1, FFN_PW), lambda b, t, j: (0, j))],
        out_specs=pl.BlockSpec((1, tt, FFN_TC), lambda b, t, j: (b, t, j)),
        out_shape=jax.ShapeDtypeStruct((bsz, t_total, D_FF), MXU_DT),
        compiler_params=_params(("parallel", "parallel", "parallel")))(up, up, cw, cb)


def _ffn_act_bwd(up, da, cw, cb):
    bsz, t_total, width = up.shape
    tt = _div_tile(t_total, 256)
    nt = t_total // tt
    hp, hn = _halo_prev(tt), _halo_next(tt, t_total)

    def body(x_ref, xp_ref, xn_ref, da_ref, dan_ref, w_ref, b_ref, dup_ref, dw_ref, db_ref):
        b, t = pl.program_id(1), pl.program_id(2)

        @pl.when((b == 0) & (t == 0))
        def _():
            dw_ref[...] = jnp.zeros_like(dw_ref)
            db_ref[...] = jnp.zeros_like(db_ref)

        prev = jnp.where(t == 0, 0.0, xp_ref[0])
        rows = jnp.concatenate([prev, x_ref[0], xn_ref[0]], axis=0)
        u = _conv_rows(rows, w_ref, FFN_CONV_K)[HALO:] + b_ref[...]
        g_pre, v_pre = u[:, :FFN_TC], u[:, FFN_TC:]
        valid = (_iota((tt + HALO, 1), 0) < tt) | (t < nt - 1)
        da_ext = jnp.where(valid, jnp.concatenate([da_ref[0], dan_ref[0]], axis=0), 0.0)
        sg = _sigmoid(g_pre)
        gs = g_pre * sg
        du = jnp.concatenate([da_ext * v_pre * (sg + gs * (1.0 - sg)), da_ext * gs], axis=1)
        dup = du * w_ref[FFN_CONV_K - 1:FFN_CONV_K, :]
        for s in range(1, FFN_CONV_K):
            dup = dup + _shift_up(du, s) * w_ref[FFN_CONV_K - 1 - s:FFN_CONV_K - s, :]
        dup_ref[0] = dup[:tt].astype(MXU_DT)
        du_t = du[:tt]
        db_ref[...] += jnp.sum(du_t, axis=0, keepdims=True)
        for k in range(FFN_CONV_K):
            s = FFN_CONV_K - 1 - k
            xs = (rows if s == 0 else pltpu.roll(rows, s, 0))[HALO:HALO + tt]
            dw_ref[k:k + 1, :] += jnp.sum(du_t * xs, axis=0, keepdims=True)

    def halo(h, w):
        return pl.BlockSpec((1, HALO, w), lambda j, b, t: (*h(b, t), j))

    wspec = lambda rows_: pl.BlockSpec((rows_, FFN_PW), lambda j, b, t: (0, j))
    tile = pl.BlockSpec((1, tt, FFN_PW), lambda j, b, t: (b, t, j))
    return pl.pallas_call(
        body, name="ffn_act_bwd", grid=(FFN_NJ, bsz, nt),
        in_specs=[tile, halo(hp, FFN_PW), halo(hn, FFN_PW),
                  pl.BlockSpec((1, tt, FFN_TC), lambda j, b, t: (b, t, j)), halo(hn, FFN_TC),
                  wspec(FFN_CONV_K), wspec(1)],
        out_specs=(tile, wspec(FFN_CONV_K), wspec(1)),
        out_shape=(jax.ShapeDtypeStruct(up.shape, MXU_DT), jax.ShapeDtypeStruct((FFN_CONV_K, width), F32),
                   jax.ShapeDtypeStruct((1, width), F32)),
        compiler_params=_params(("arbitrary", "arbitrary", "arbitrary")))(up, up, up, da, da, cw, cb)


QKV_W = 3 * HEADS * HD
SM_BLK = P_SM // 128


def _dn_pre_fwd(proj, conv_w, alog_row, dt_row):
    bsz, t_total, _ = proj.shape
    tt = _div_tile(t_total, 256)
    hp = _halo_prev(tt)

    def body(x_ref, xp_ref, sm_ref, w_ref, al_ref, dt_ref, q_ref, k_ref, v_ref, g_ref):
        prev = jnp.where(pl.program_id(1) == 0, 0.0, xp_ref[0])
        y = _conv_rows(jnp.concatenate([prev, x_ref[0]], axis=0), w_ref, DN_CONV_K)[HALO:]
        q_ref[0], k_ref[0], v_ref[0] = _dn_qkv(y)
        g_ref[0] = _dn_gates(sm_ref[0], al_ref[...], dt_ref[...])

    out512 = jax.ShapeDtypeStruct((bsz, t_total, HEADS * HD), F32)
    return pl.pallas_call(
        body, name="dn_pre_fwd", grid=(bsz, t_total // tt),
        in_specs=[pl.BlockSpec((1, tt, QKV_W), lambda b, t: (b, t, 0)),
                  pl.BlockSpec((1, HALO, QKV_W), lambda b, t: (*hp(b, t), 0)),
                  pl.BlockSpec((1, tt, 128), lambda b, t: (b, t, SM_BLK)),
                  pl.BlockSpec((DN_CONV_K, QKV_W), lambda b, t: (0, 0)), _vec_spec(128), _vec_spec(128)],
        out_specs=(_tok_spec(tt, 512), _tok_spec(tt, 512), _tok_spec(tt, 512), _tok_spec(tt, 128)),
        out_shape=(out512, out512, out512, jax.ShapeDtypeStruct((bsz, t_total, 128), F32)),
        compiler_params=_params(("parallel", "parallel")))(proj, proj, proj, conv_w, alog_row, dt_row)


def _dn_pre_bwd(proj, dq, dk, dv, dgates, conv_w, alog_row, dt_row):
    bsz, t_total, _ = proj.shape
    tt = _div_tile(t_total, 128)
    nt = t_total // tt
    hp, hn = _halo_prev(tt), _halo_next(tt, t_total)

    def body(x_ref, xp_ref, xn_ref, sm_ref, dq_ref, dqn_ref, dk_ref, dkn_ref, dv_ref, dvn_ref, dg_ref,
             w_ref, al_ref, dt_ref, dx_ref, dsm_ref, dw_ref, dal_ref, ddt_ref):
        b, t = pl.program_id(0), pl.program_id(1)

        @pl.when((b == 0) & (t == 0))
        def _():
            dw_ref[...] = jnp.zeros_like(dw_ref)
            dal_ref[...] = jnp.zeros_like(dal_ref)
            ddt_ref[...] = jnp.zeros_like(ddt_ref)

        prev = jnp.where(t == 0, 0.0, xp_ref[0])
        rows = jnp.concatenate([prev, x_ref[0], xn_ref[0]], axis=0)
        y = _conv_rows(rows, w_ref, DN_CONV_K)[HALO:]
        valid = (_iota((tt + HALO, 1), 0) < tt) | (t < nt - 1)

        def ext(tile_ref, next_ref):
            return jnp.where(valid, jnp.concatenate([tile_ref[0], next_ref[0]], axis=0), 0.0)

        _, vjp_qkv = jax.vjp(_dn_qkv, y)
        (dy,) = vjp_qkv((ext(dq_ref, dqn_ref), ext(dk_ref, dkn_ref), ext(dv_ref, dvn_ref)))
        dy = jnp.where(valid, dy, 0.0)
        dx = dy * w_ref[DN_CONV_K - 1:DN_CONV_K, :]
        for s in range(1, DN_CONV_K):
            dx = dx + _shift_up(dy, s) * w_ref[DN_CONV_K - 1 - s:DN_CONV_K - s, :]
        dx_ref[0] = dx[:tt].astype(MXU_DT)
        dy_t = dy[:tt]
        for k in range(DN_CONV_K):
            s = DN_CONV_K - 1 - k
            xs = (rows if s == 0 else pltpu.roll(rows, s, 0))[HALO:HALO + tt]
            dw_ref[k:k + 1, :] += jnp.sum(dy_t * xs, axis=0, keepdims=True)
        _, vjp_g = jax.vjp(_dn_gates, sm_ref[0], al_ref[...], dt_ref[...])
        dsm, dal, ddt = vjp_g(dg_ref[0])
        dsm_ref[0] = dsm
        dal_ref[...] += dal
        ddt_ref[...] += ddt

    def tile(width, blk=0):
        return pl.BlockSpec((1, tt, width), lambda b, t: (b, t, blk))

    def halo(h, width):
        return pl.BlockSpec((1, HALO, width), lambda b, t: (*h(b, t), 0))

    return pl.pallas_call(
        body, name="dn_pre_bwd", grid=(bsz, nt),
        in_specs=[tile(QKV_W), halo(hp, QKV_W), halo(hn, QKV_W), tile(128, SM_BLK),
                  tile(512), halo(hn, 512), tile(512), halo(hn, 512), tile(512), halo(hn, 512), tile(128),
                  pl.BlockSpec((DN_CONV_K, QKV_W), lambda b, t: (0, 0)), _vec_spec(128), _vec_spec(128)],
        out_specs=(tile(QKV_W), tile(128), pl.BlockSpec((DN_CONV_K, QKV_W), lambda b, t: (0, 0)),
                   _vec_spec(128), _vec_spec(128)),
        out_shape=(jax.ShapeDtypeStruct((bsz, t_total, QKV_W), MXU_DT), jax.ShapeDtypeStruct((bsz, t_total, 128), F32),
                   jax.ShapeDtypeStruct((DN_CONV_K, QKV_W), F32), jax.ShapeDtypeStruct((1, 128), F32),
                   jax.ShapeDtypeStruct((1, 128), F32)),
        compiler_params=_params(("arbitrary", "arbitrary")))(
            proj, proj, proj, proj, dq, dq, dk, dk, dv, dv, dgates, conv_w, alog_row, dt_row)


def _state_spec(bsz, idx):
    return pl.BlockSpec((bsz, 1, HEADS, HD, HD), lambda c: (0, idx(c), 0, 0, 0))


def _inv_spec(bsz, idx):
    return pl.BlockSpec((bsz, 1, HEADS, CHUNK, CHUNK), lambda c: (0, idx(c), 0, 0, 0))


def _chunk_spec(bsz, width, idx, blk=0):
    return pl.BlockSpec((bsz, CHUNK, width), lambda c: (0, idx(c), blk))


def _dn_rec_fwd(q, k, v, gates):
    bsz, t_total, _ = q.shape
    nc = t_total // CHUNK
    fwd = lambda c: c

    def body(q_ref, k_ref, v_ref, g_ref, o_ref, ss_ref, inv_ref, s_ref):
        @pl.when(pl.program_id(0) == 0)
        def _():
            s_ref[...] = jnp.zeros_like(s_ref)

        seqs = range(bsz)
        s_list = [[s_ref[b * HEADS + h] for h in range(HEADS)] for b in seqs]
        for b in seqs:
            for h in range(HEADS):
                ss_ref[b, 0, h] = s_list[b][h]
        o, new_s, invs = _dn_chunk(s_list, [q_ref[b] for b in seqs], [k_ref[b] for b in seqs],
                                   [v_ref[b] for b in seqs], [g_ref[b] for b in seqs], with_inv=True)
        for b in seqs:
            o_ref[b] = o[b]
            for h in range(HEADS):
                s_ref[b * HEADS + h] = new_s[b][h]
                inv_ref[b, 0, h] = invs[b * HEADS + h]

    return pl.pallas_call(
        body, name="dn_rec_fwd", grid=(nc,),
        in_specs=[_chunk_spec(bsz, 512, fwd)] * 3 + [_chunk_spec(bsz, 128, fwd)],
        out_specs=(_chunk_spec(bsz, 512, fwd), _state_spec(bsz, fwd), _inv_spec(bsz, fwd)),
        out_shape=(jax.ShapeDtypeStruct(q.shape, F32), jax.ShapeDtypeStruct((bsz, nc, HEADS, HD, HD), F32),
                   jax.ShapeDtypeStruct((bsz, nc, HEADS, CHUNK, CHUNK), F32)),
        scratch_shapes=[pltpu.VMEM((bsz * HEADS, HD, HD), F32)],
        compiler_params=_params(("arbitrary",)))(q, k, v, gates)


def _dn_rec_bwd(q, k, v, gates, states, invs, do):
    bsz, t_total, _ = q.shape
    nc = t_total // CHUNK
    rev = lambda c: nc - 1 - c

    def body(q_ref, k_ref, v_ref, g_ref, ss_ref, inv_ref, do_ref, dq_ref, dk_ref, dv_ref, dg_ref, ds_ref):
        @pl.when(pl.program_id(0) == 0)
        def _():
            ds_ref[...] = jnp.zeros_like(ds_ref)

        seqs = range(bsz)
        s_list = [[ss_ref[b, 0, h] for h in range(HEADS)] for b in seqs]
        known = [inv_ref[b, 0, h] for b in seqs for h in range(HEADS)]
        _, vjp = jax.vjp(functools.partial(_dn_chunk, inv_known=known),
                         s_list, [q_ref[b] for b in seqs], [k_ref[b] for b in seqs],
                         [v_ref[b] for b in seqs], [g_ref[b] for b in seqs])
        ds_in, dq, dk, dv, dg = vjp(([do_ref[b] for b in seqs],
                                     [[ds_ref[b * HEADS + h] for h in range(HEADS)] for b in seqs]))
        for b in seqs:
            dq_ref[b], dk_ref[b], dv_ref[b], dg_ref[b] = dq[b], dk[b], dv[b], dg[b]
            for h in range(HEADS):
                ds_ref[b * HEADS + h] = ds_in[b][h]

    tok = lambda width: _chunk_spec(bsz, width, rev)
    out512 = jax.ShapeDtypeStruct(q.shape, F32)
    return pl.pallas_call(
        body, name="dn_rec_bwd", grid=(nc,),
        in_specs=[tok(512), tok(512), tok(512), tok(128), _state_spec(bsz, rev), _inv_spec(bsz, rev), tok(512)],
        out_specs=(tok(512), tok(512), tok(512), tok(128)),
        out_shape=(out512, out512, out512, jax.ShapeDtypeStruct(gates.shape, F32)),
        scratch_shapes=[pltpu.VMEM((bsz * HEADS, HD, HD), F32)],
        compiler_params=_params(("arbitrary",)))(q, k, v, gates, states, invs, do)


GQ_BLK, GK_BLK, GV_BLK = P_GQ // 512, P_GK // 512, P_GV // 512


def _gla_rec_fwd(proj, w2, bg):
    bsz, t_total, _ = proj.shape
    nc = t_total // CHUNK

    fwd = lambda c: c

    def body(q_ref, k_ref, v_ref, sm_ref, w2_ref, bg_ref, o_ref, ss_ref, s_ref):
        @pl.when(pl.program_id(0) == 0)
        def _():
            s_ref[...] = jnp.zeros_like(s_ref)

        seqs = range(bsz)
        s_list = [[s_ref[b * HEADS + h] for h in range(HEADS)] for b in seqs]
        for b in seqs:
            for h in range(HEADS):
                ss_ref[b, 0, h] = s_list[b][h]
        o, new_s = _gla_chunk(s_list, [q_ref[b] for b in seqs], [k_ref[b] for b in seqs], [v_ref[b] for b in seqs],
                              [sm_ref[b] for b in seqs], w2_ref[...], bg_ref[...])
        for b in seqs:
            o_ref[b] = o[b]
            for h in range(HEADS):
                s_ref[b * HEADS + h] = new_s[b][h]

    col = lambda blk, width=512: _chunk_spec(bsz, width, fwd, blk)
    return pl.pallas_call(
        body, name="gla_rec_fwd", grid=(nc,),
        in_specs=[col(GQ_BLK), col(GK_BLK), col(GV_BLK), col(SM_BLK, 128),
                  pl.BlockSpec((128, 512), lambda c: (0, 0)), pl.BlockSpec((1, 512), lambda c: (0, 0))],
        out_specs=(col(0), _state_spec(bsz, fwd)),
        out_shape=(jax.ShapeDtypeStruct((bsz, t_total, 512), F32),
                   jax.ShapeDtypeStruct((bsz, nc, HEADS, HD, HD), F32)),
        scratch_shapes=[pltpu.VMEM((bsz * HEADS, HD, HD), F32)],
        compiler_params=_params(("arbitrary",)))(proj, proj, proj, proj, w2, bg)


def _gla_rec_bwd(proj, w2, bg, states, do, dsm_dn):
    bsz, t_total, _ = proj.shape
    nc = t_total // CHUNK
    rev = lambda c: nc - 1 - c

    def body(q_ref, k_ref, v_ref, sm_ref, w2_ref, bg_ref, ss_ref, do_ref, dsd_ref,
             dq_ref, dk_ref, dv_ref, dsm_ref, dw2_ref, dbg_ref, ds_ref):
        @pl.when(pl.program_id(0) == 0)
        def _():
            dw2_ref[...] = jnp.zeros_like(dw2_ref)
            dbg_ref[...] = jnp.zeros_like(dbg_ref)
            ds_ref[...] = jnp.zeros_like(ds_ref)

        seqs = range(bsz)
        s_list = [[ss_ref[b, 0, h] for h in range(HEADS)] for b in seqs]
        _, vjp = jax.vjp(_gla_chunk, s_list, [q_ref[b] for b in seqs], [k_ref[b] for b in seqs],
                         [v_ref[b] for b in seqs], [sm_ref[b] for b in seqs], w2_ref[...], bg_ref[...])
        ds_in, dq, dk, dv, dsm, dw2, dbg = vjp(([do_ref[b] for b in seqs],
                                                [[ds_ref[b * HEADS + h] for h in range(HEADS)] for b in seqs]))
        for b in seqs:
            dq_ref[b], dk_ref[b], dv_ref[b] = dq[b].astype(MXU_DT), dk[b].astype(MXU_DT), dv[b].astype(MXU_DT)
            dsm_ref[b] = (dsm[b] + dsd_ref[b]).astype(MXU_DT)
            for h in range(HEADS):
                ds_ref[b * HEADS + h] = ds_in[b][h]
        dw2_ref[...] += dw2
        dbg_ref[...] += dbg

    col = lambda blk, width=512: _chunk_spec(bsz, width, rev, blk)
    w2_spec = pl.BlockSpec((128, 512), lambda c: (0, 0))
    bg_spec = pl.BlockSpec((1, 512), lambda c: (0, 0))
    out512 = jax.ShapeDtypeStruct((bsz, t_total, 512), MXU_DT)
    return pl.pallas_call(
        body, name="gla_rec_bwd", grid=(nc,),
        in_specs=[col(GQ_BLK), col(GK_BLK), col(GV_BLK), col(SM_BLK, 128), w2_spec, bg_spec,
                  _state_spec(bsz, rev), col(0), col(0, 128)],
        out_specs=(col(0), col(0), col(0), col(0, 128), w2_spec, bg_spec),
        out_shape=(out512, out512, out512, jax.ShapeDtypeStruct((bsz, t_total, 128), MXU_DT),
                   jax.ShapeDtypeStruct((128, 512), F32), jax.ShapeDtypeStruct((1, 512), F32)),
        scratch_shapes=[pltpu.VMEM((bsz * HEADS, HD, HD), F32)],
        compiler_params=_params(("arbitrary",)))(proj, proj, proj, proj, w2, bg, states, do, dsm_dn)


Z_BLK, GG_BLK = P_Z // 512, P_GG // 512


def _mix_out_fwd(o_dn, o_gla, proj, grow_dn, grow_gla):
    bsz, t_total, _ = o_dn.shape
    tt = _div_tile(t_total, 256)

    def body(od_ref, og_ref, z_ref, gg_ref, gd_ref, gl_ref, o_ref):
        o_ref[0, :, :512] = _gate_norm(od_ref[0], z_ref[0], gd_ref[...]).astype(MXU_DT)
        o_ref[0, :, 512:] = _gate_norm(og_ref[0], gg_ref[0], gl_ref[...]).astype(MXU_DT)

    def col(blk):
        return pl.BlockSpec((1, tt, 512), lambda b, t: (b, t, blk))

    return pl.pallas_call(
        body, name="mix_out_fwd", grid=(bsz, t_total // tt),
        in_specs=[col(0), col(0), col(Z_BLK), col(GG_BLK), _vec_spec(512), _vec_spec(512)],
        out_specs=_tok_spec(tt), out_shape=jax.ShapeDtypeStruct((bsz, t_total, D), MXU_DT),
        compiler_params=_params(("parallel", "parallel")))(o_dn, o_gla, proj, proj, grow_dn, grow_gla)


def _mix_out_bwd(do, o_dn, o_gla, proj, grow_dn, grow_gla):
    bsz, t_total, _ = o_dn.shape
    tt = _div_tile(t_total, 256)

    def body(do_ref, od_ref, og_ref, z_ref, gg_ref, gd_ref, gl_ref,
             dod_ref, dog_ref, dz_ref, dgg_ref, dgd_ref, dgl_ref):
        @pl.when((pl.program_id(0) == 0) & (pl.program_id(1) == 0))
        def _():
            dgd_ref[...] = jnp.zeros_like(dgd_ref)
            dgl_ref[...] = jnp.zeros_like(dgl_ref)

        def one(o_ref, gate_ref, g_ref, ct, do_out, dgate_out, dg_out):
            _, vjp = jax.vjp(_gate_norm, o_ref[0], gate_ref[0], g_ref[...])
            d_o, d_gate, d_row = vjp(ct)
            do_out[0] = d_o
            dgate_out[0] = d_gate.astype(MXU_DT)
            acc = d_row[:, :HD]
            for h in range(1, HEADS):
                acc = acc + d_row[:, h * HD:(h + 1) * HD]
            dg_out[...] += acc

        ct = do_ref[0]
        one(od_ref, z_ref, gd_ref, ct[:, :512], dod_ref, dz_ref, dgd_ref)
        one(og_ref, gg_ref, gl_ref, ct[:, 512:], dog_ref, dgg_ref, dgl_ref)

    def col(blk):
        return pl.BlockSpec((1, tt, 512), lambda b, t: (b, t, blk))

    f512 = jax.ShapeDtypeStruct((bsz, t_total, 512), F32)
    b512 = jax.ShapeDtypeStruct((bsz, t_total, 512), MXU_DT)
    g128 = jax.ShapeDtypeStruct((1, HD), F32)
    return pl.pallas_call(
        body, name="mix_out_bwd", grid=(bsz, t_total // tt),
        in_specs=[_tok_spec(tt), col(0), col(0), col(Z_BLK), col(GG_BLK), _vec_spec(512), _vec_spec(512)],
        out_specs=(col(0), col(0), col(0), col(0), _vec_spec(HD), _vec_spec(HD)),
        out_shape=(f512, f512, b512, b512, g128, g128),
        compiler_params=_params(("arbitrary", "arbitrary")))(do, o_dn, o_gla, proj, proj, grow_dn, grow_gla)


def _sum_slots(x, name):
    n, rows, cols = x.shape
    tr = _div_tile(rows, max(8, (1 << 19) // cols))

    def body(x_ref, o_ref):
        acc = x_ref[0]
        for i in range(1, n):
            acc = acc + x_ref[i]
        o_ref[...] = acc

    return pl.pallas_call(
        body, name=name, grid=(rows // tr,),
        in_specs=[pl.BlockSpec((n, tr, cols), lambda i: (0, i, 0))],
        out_specs=pl.BlockSpec((tr, cols), lambda i: (i, 0)),
        out_shape=jax.ShapeDtypeStruct((rows, cols), F32), compiler_params=_params(("parallel",)))(x)


def _pair_add(g8, r1, core, name):
    _, rows, cols = g8.shape
    tr = _div_tile(rows, max(8, (1 << 19) // cols))
    g42 = g8.reshape(4, 2, rows, cols)

    def body(core_ref, g_ref, r_ref, o_ref):
        o_ref[0] = (g_ref[0, 0].astype(F32) + r_ref[0].astype(F32)).astype(o_ref.dtype)

    return pl.pallas_call(
        body, name=name,
        grid_spec=pltpu.PrefetchScalarGridSpec(
            num_scalar_prefetch=1, grid=(4, rows // tr),
            in_specs=[pl.BlockSpec((1, 1, tr, cols), lambda s, i, core_ref: (s, core_ref[0], i, 0)),
                      pl.BlockSpec((1, tr, cols), lambda s, i, core_ref: (s, i, 0))],
            out_specs=pl.BlockSpec((1, tr, cols), lambda s, i, core_ref: (s, i, 0))),
        out_shape=jax.ShapeDtypeStruct((4, rows, cols), g8.dtype),
        compiler_params=_params(("parallel", "parallel")))(core, g42, r1)


def _chip_add(p4, r2, chip, name):
    _, rows, cols = p4.shape
    tr = _div_tile(rows, max(8, (1 << 19) // cols))

    def body(chip_ref, p_ref, r_ref, o_ref):
        f = lambda a: a.astype(F32)
        o_ref[...] = ((f(p_ref[0]) + f(r_ref[0])) + f(r_ref[1])) + f(r_ref[2])

    return pl.pallas_call(
        body, name=name,
        grid_spec=pltpu.PrefetchScalarGridSpec(
            num_scalar_prefetch=1, grid=(rows // tr,),
            in_specs=[pl.BlockSpec((1, tr, cols), lambda i, chip_ref: (chip_ref[0], i, 0)),
                      pl.BlockSpec((3, tr, cols), lambda i, chip_ref: (0, i, 0))],
            out_specs=pl.BlockSpec((tr, cols), lambda i, chip_ref: (i, 0))),
        out_shape=jax.ShapeDtypeStruct((rows, cols), F32),
        compiler_params=_params(("parallel",)))(chip, p4, r2)


def _adamw(w, g, m, v, name):
    rows, cols = w.shape
    tr = _div_tile(rows, max(8, (1 << 18) // cols))

    def body(w_ref, g_ref, m_ref, v_ref, d_ref, nm_ref, nv_ref):
        g_ = g_ref[...]
        nm = ADAM_B1 * m_ref[...] + (1.0 - ADAM_B1) * g_
        nv = ADAM_B2 * v_ref[...] + (1.0 - ADAM_B2) * (g_ * g_)
        m_hat = nm / (1.0 - ADAM_B1 ** ADAM_STEP)
        v_hat = nv / (1.0 - ADAM_B2 ** ADAM_STEP)
        d_ref[...] = -ADAM_LR * (m_hat / (jnp.sqrt(v_hat) + ADAM_EPS) + ADAM_WD * w_ref[...])
        nm_ref[...] = nm
        nv_ref[...] = nv

    spec = pl.BlockSpec((tr, cols), lambda i: (i, 0))
    shp = jax.ShapeDtypeStruct((rows, cols), F32)
    return pl.pallas_call(body, name=name, grid=(rows // tr,), in_specs=[spec] * 4, out_specs=(spec,) * 3,
                          out_shape=(shp,) * 3, compiler_params=_params(("parallel",)))(w, g, m, v)


def _position():
    return lax.axis_index("x"), lax.axis_index("y"), lax.axis_index("c")


def _slot(px, py, pc):
    return 4 * px + 2 * py + pc


def _gather_small(x, name):
    rows, cols = x.shape

    def body(x_ref, o_ref, send_sems, recv_sems):
        mx, my, mc = _position()

        def peer(k):
            return (mx ^ ((k >> 2) & 1), my ^ ((k >> 1) & 1), mc ^ (k & 1))

        o_ref[_slot(mx, my, mc)] = x_ref[...]
        sends = []
        for k in range(1, N_DEV):
            cp = pltpu.make_async_remote_copy(src_ref=x_ref, dst_ref=o_ref.at[_slot(mx, my, mc)],
                                              send_sem=send_sems.at[k - 1], recv_sem=recv_sems.at[k - 1],
                                              device_id=peer(k), device_id_type=MESH)
            cp.start()
            sends.append(cp)
        for k in range(1, N_DEV):
            pltpu.make_async_remote_copy(src_ref=x_ref, dst_ref=o_ref.at[_slot(*peer(k))],
                                         send_sem=send_sems.at[k - 1], recv_sem=recv_sems.at[k - 1],
                                         device_id=peer(k), device_id_type=MESH).wait_recv()
        for cp in sends:
            cp.wait_send()

    return pl.pallas_call(
        body, name=name, out_shape=jax.ShapeDtypeStruct((N_DEV, rows, cols), x.dtype),
        in_specs=[pl.BlockSpec(memory_space=pltpu.VMEM)], out_specs=pl.BlockSpec(memory_space=pltpu.VMEM),
        scratch_shapes=[pltpu.SemaphoreType.DMA((N_DEV - 1,)), pltpu.SemaphoreType.DMA((N_DEV - 1,))],
        compiler_params=pltpu.CompilerParams(vmem_limit_bytes=VMEM_LIMIT_V7X))(x)


def _gather_big(shards):
    n = len(shards)

    def body(*refs):
        xs, outs = refs[:n], refs[n:2 * n]
        send_sems, recv_sems, local_sems = refs[2 * n:]
        mx, my, mc = _position()
        me, sibling = (mx, my, mc), (mx, my, 1 - mc)
        chips = [(1 - mx, my), (mx, 1 - my), (1 - mx, 1 - my)]

        def copy(a, k, block, to, src=None):
            dst = outs[a].at[_slot(*block)]
            return pltpu.make_async_remote_copy(src_ref=dst if src is None else src, dst_ref=dst,
                                                send_sem=send_sems.at[7 * a + k], recv_sem=recv_sems.at[7 * a + k],
                                                device_id=to, device_id_type=MESH)

        mine = [pltpu.make_async_copy(xs[a], outs[a].at[_slot(*me)], local_sems.at[a]) for a in range(n)]
        for cp in mine:
            cp.start()
        started = []
        for a in range(n):
            started.append(copy(a, 0, me, sibling, src=xs[a]))
            started += [copy(a, 1 + j, me, (*chip, mc), src=xs[a]) for j, chip in enumerate(chips)]
        for cp in started:
            cp.start()
        for j, chip in enumerate(chips):
            for a in range(n):
                copy(a, 1 + j, (*chip, mc), me).wait_recv()
                fwd = copy(a, 4 + j, (*chip, mc), sibling)
                fwd.start()
                started.append(fwd)
        for a in range(n):
            copy(a, 0, sibling, me).wait_recv()
            for j, chip in enumerate(chips):
                copy(a, 4 + j, (*chip, 1 - mc), me).wait_recv()
        for cp in started:
            cp.wait_send()
        for cp in mine:
            cp.wait()

    any_spec = pl.BlockSpec(memory_space=pl.ANY)
    return pl.pallas_call(
        body, name="gather_weights",
        out_shape=tuple(jax.ShapeDtypeStruct((N_DEV,) + s.shape, s.dtype) for s in shards),
        in_specs=[any_spec] * n, out_specs=(any_spec,) * n,
        scratch_shapes=[pltpu.SemaphoreType.DMA((7 * n,)), pltpu.SemaphoreType.DMA((7 * n,)),
                        pltpu.SemaphoreType.DMA((n,))])(*shards)


def _scatter_sibling(grads):
    n = len(grads)

    def body(*refs):
        gs, outs = refs[:n], refs[n:2 * n]
        send_sems, recv_sems = refs[2 * n:]
        mx, my, mc = _position()
        copies = []
        for a in range(n):
            for s in range(4):
                copies.append(pltpu.make_async_remote_copy(
                    src_ref=gs[a].at[2 * s + (1 - mc)], dst_ref=outs[a].at[s],
                    send_sem=send_sems.at[4 * a + s], recv_sem=recv_sems.at[4 * a + s],
                    device_id=(mx, my, 1 - mc), device_id_type=MESH))
        for cp in copies:
            cp.start()
        for cp in copies:
            cp.wait_recv()
        for cp in copies:
            cp.wait_send()

    any_spec = pl.BlockSpec(memory_space=pl.ANY)
    return pl.pallas_call(
        body, name="scatter_sibling",
        out_shape=tuple(jax.ShapeDtypeStruct((4,) + g.shape[1:], g.dtype) for g in grads),
        in_specs=[any_spec] * n, out_specs=(any_spec,) * n,
        scratch_shapes=[pltpu.SemaphoreType.DMA((4 * n,)), pltpu.SemaphoreType.DMA((4 * n,))])(*grads)


def _scatter_chips(sums):
    n = len(sums)

    def body(*refs):
        ps, outs = refs[:n], refs[n:2 * n]
        send_sems, recv_sems = refs[2 * n:]
        mx, my, mc = _position()
        chips = [(1 - mx, my), (mx, 1 - my), (1 - mx, 1 - my)]
        copies = []
        for a in range(n):
            for k, (cx, cy) in enumerate(chips):
                copies.append(pltpu.make_async_remote_copy(
                    src_ref=ps[a].at[2 * cx + cy], dst_ref=outs[a].at[k],
                    send_sem=send_sems.at[3 * a + k], recv_sem=recv_sems.at[3 * a + k],
                    device_id=(cx, cy, mc), device_id_type=MESH))
        for cp in copies:
            cp.start()
        for cp in copies:
            cp.wait_recv()
        for cp in copies:
            cp.wait_send()

    any_spec = pl.BlockSpec(memory_space=pl.ANY)
    return pl.pallas_call(
        body, name="scatter_chips",
        out_shape=tuple(jax.ShapeDtypeStruct((3,) + p.shape[1:], p.dtype) for p in sums),
        in_specs=[any_spec] * n, out_specs=(any_spec,) * n,
        scratch_shapes=[pltpu.SemaphoreType.DMA((3 * n,)), pltpu.SemaphoreType.DMA((3 * n,))])(*sums)


def _pad_heads(x, axis):
    shp = list(x.shape)
    x4 = x.reshape(shp[:axis] + [HEADS, GLA_KEY] + shp[axis + 1:])
    pad = [(0, 0)] * x4.ndim
    pad[axis + 1] = (0, HD - GLA_KEY)
    return jnp.pad(x4, pad).reshape(shp[:axis] + [HEADS * HD] + shp[axis + 1:])


def _unpad_heads(x, axis):
    shp = list(x.shape)
    x4 = x.reshape(shp[:axis] + [HEADS, HD] + shp[axis + 1:])
    x4 = lax.slice_in_dim(x4, 0, GLA_KEY, axis=axis + 1)
    return x4.reshape(shp[:axis] + [HEADS * GLA_KEY] + shp[axis + 1:])


O_Z_END, O_AB, O_GQ, O_GK, O_GV, O_R = 2048, 2048, 2056, 2312, 2568, 3592


def _pad_in_rows(wt):
    return jnp.concatenate([
        wt[:O_Z_END], _pad_heads(wt[O_GQ:O_GK], 0), _pad_heads(wt[O_GK:O_GV], 0), wt[O_GV:O_R],
        wt[O_AB:O_GQ], wt[O_R:], jnp.zeros((P_W - P_SM - 8 - GATE_RANK, wt.shape[1]), wt.dtype)], axis=0)


def _unpad_in_rows(gt):
    return jnp.concatenate([
        gt[:P_GQ], gt[P_SM:P_SM + 8], _unpad_heads(gt[P_GQ:P_GK], 0), _unpad_heads(gt[P_GK:P_GV], 0),
        gt[P_GV:P_SM], gt[P_SM + 8:P_SM + 8 + GATE_RANK]], axis=0)


def _lane_row(vals, width=128):
    return jnp.pad(vals.reshape(1, -1), ((0, 0), (0, width - vals.size)))


SMALL_NAMES = ["ln0_g", "ln0_b", "b_ada", "dn_conv", "dn_a_log", "dn_dt_bias", "dn_norm_g", "gla_w_gate2",
               "gla_b_gate", "gla_norm_g", "ln1_g", "ln1_b", "ffn_conv", "ffn_conv_b", "ln2_g", "ln2_b"]
WEIGHTS = ["ln0_g", "ln0_b", "w_ada", "b_ada", "w_in", "dn_conv", "dn_a_log", "dn_dt_bias", "dn_norm_g",
           "gla_w_gate2", "gla_b_gate", "gla_norm_g", "w_o", "ln1_g", "ln1_b", "ffn_w_up", "ffn_conv", "ffn_conv_b",
           "ffn_w_down", "ln2_g", "ln2_b"]


def kernel(x, c, ln0_g, ln0_b, w_ada, b_ada, w_in, dn_conv, dn_a_log, dn_dt_bias, dn_norm_g, gla_w_gate2, gla_b_gate, gla_norm_g, w_o, ln1_g, ln1_b, ffn_w_up, ffn_conv, ffn_conv_b, ffn_w_down, ln2_g, ln2_b, loss_target, m_ln0_g, m_ln0_b, m_w_ada, m_b_ada, m_w_in, m_dn_conv, m_dn_a_log, m_dn_dt_bias, m_dn_norm_g, m_gla_w_gate2, m_gla_b_gate, m_gla_norm_g, m_w_o, m_ln1_g, m_ln1_b, m_ffn_w_up, m_ffn_conv, m_ffn_conv_b, m_ffn_w_down, m_ln2_g, m_ln2_b, v_ln0_g, v_ln0_b, v_w_ada, v_b_ada, v_w_in, v_dn_conv, v_dn_a_log, v_dn_dt_bias, v_dn_norm_g, v_gla_w_gate2, v_gla_b_gate, v_gla_norm_g, v_w_o, v_ln1_g, v_ln1_b, v_ffn_w_up, v_ffn_conv, v_ffn_conv_b, v_ffn_w_down, v_ln2_g, v_ln2_b):
    args = dict(locals())
    w_given = {n: args[n] for n in WEIGHTS}
    m_given = {n: args["m_" + n] for n in WEIGHTS}
    v_given = {n: args["v_" + n] for n in WEIGHTS}
    bsz, t_total, _ = x.shape
    ntok = bsz * t_total
    mx, my, mc = _position()
    me = _slot(mx, my, mc)

    pack1 = jnp.concatenate([c.reshape(-1), dn_conv.reshape(-1), gla_w_gate2.reshape(-1), ffn_conv.reshape(-1)])
    n1 = pack1.size
    rows1 = -(-n1 // 1024) * 8
    pack1 = jnp.pad(pack1, (0, rows1 * 128 - n1)).reshape(rows1, 128)
    got1 = _gather_small(pack1, "gather_cond").reshape(N_DEV, -1)
    o1 = bsz * D
    o2 = o1 + dn_conv.size
    o3 = o2 + gla_w_gate2.size
    c_all = got1[:, :o1].reshape(N_DEV * bsz, D)
    dn_conv_f = got1[:, o1:o2].reshape(N_DEV, DN_CONV_K, -1).transpose(1, 0, 2).reshape(DN_CONV_K, QKV_W)
    gate2_f = got1[:, o2:o3].reshape(N_DEV, GATE_RANK, -1).transpose(1, 0, 2).reshape(GATE_RANK, HEADS * GLA_KEY)
    ffn_conv_f = got1[:, o3:n1].reshape(N_DEV, FFN_CONV_K, -1).transpose(1, 0, 2).reshape(FFN_CONV_K, 2 * D_FF)

    win_t = w_in[0].T.astype(MXU_DT)
    wup_t = ffn_w_up[0].T.astype(MXU_DT)
    win_all, wo_all, wup_all, wdn_all = _gather_big(
        [win_t, w_o[0].astype(MXU_DT), wup_t, ffn_w_down[0].astype(MXU_DT)])
    win_p = _pad_in_rows(win_all.reshape(IN_W, D))
    wo_f = wo_all.reshape(D, D)
    wup_f = _ffn_pair(wup_all.reshape(2 * D_FF, D), 0)
    wdn_f = wdn_all.reshape(D_FF, D)
    cw_p, cb_p = _ffn_pair(ffn_conv_f, 1), _ffn_pair(ffn_conv_b, 1)

    ncol = w_ada.shape[2]
    b_cols = lax.dynamic_slice_in_dim(b_ada, me * ncol, ncol, axis=1)
    mod_part = _ada_fwd(c_all, w_ada[0], b_cols)
    mod_all = _gather_small(mod_part.reshape(-1, 128), "gather_mod").reshape(N_DEV, N_DEV * bsz, ncol)
    mod = lax.dynamic_slice_in_dim(mod_all, me * bsz, bsz, axis=1).transpose(1, 0, 2).reshape(bsz, 6, 1, D)
    sh_a, sc_a, gt_a, sh_f, sc_f, gt_f = (mod[:, i] for i in range(6))

    g0, b0 = ln0_g.reshape(1, D), ln0_b.reshape(1, D)
    alog_row, dt_row = _lane_row(dn_a_log[0]), _lane_row(dn_dt_bias[0])
    grow_dn, grow_gla = jnp.tile(dn_norm_g, (1, HEADS)), jnp.tile(gla_norm_g, (1, HEADS))
    w2 = jnp.zeros((128, HEADS * HD), F32).at[SM_R:SM_R + GATE_RANK].set(_pad_heads(gate2_f, 1))
    bg = _pad_heads(gla_b_gate, 1)

    h_a = _ln0_mod(x, g0, b0, sc_a, sh_a)
    proj = _mm(h_a.reshape(ntok, D), win_p, "nt", F32, "mm_proj", tm=1024, tn=1408).reshape(bsz, t_total, P_W)
    q, k, v, gates = _dn_pre_fwd(proj, dn_conv_f, alog_row, dt_row)
    o_dn, s_dn, inv_dn = _dn_rec_fwd(q, k, v, gates)
    o_gla, s_gla = _gla_rec_fwd(proj, w2, bg)
    o_mix = _mix_out_fwd(o_dn, o_gla, proj, grow_dn, grow_gla)
    y = _mm(o_mix.reshape(ntok, D), wo_f, "nn", F32, "mm_wo", tm=1024, tn=1024).reshape(bsz, t_total, D)
    r1, h_f = _res_ln_mod(x, y, gt_a, g0, b0, ln1_g, ln1_b, sc_f, sh_f)
    up = _mm(h_f.reshape(ntok, D), wup_f, "nt", F32, "mm_up", tm=1024, tn=1408).reshape(bsz, t_total, 2 * D_FF)
    act = _ffn_act_fwd(up, cw_p, cb_p)
    y2 = _mm(act.reshape(ntok, D_FF), wdn_f, "nn", F32, "mm_down", tm=1024, tn=1024).reshape(bsz, t_total, D)
    loss_rows, dr2, dy2, dgt_f, d_ln2_g, d_ln2_b = _final_fwd_bwd(r1, y2, gt_f, ln1_g, ln1_b, ln2_g, ln2_b, loss_target)
    loss = lax.psum(0.5 * jnp.sum(loss_rows) / D, ("x", "y", "c"))

    dy2_2 = dy2.reshape(ntok, D)
    dact = _mm(dy2_2, wdn_f, "nt", F32, "mm_dact", tm=1024, tn=1408).reshape(bsz, t_total, D_FF)
    g_wdn = _mm(act.reshape(ntok, D_FF), dy2_2, "tn", MXU_DT, "mm_gwdn", tm=1408, tn=1024)
    dup, d_cw_p, d_cb_p = _ffn_act_bwd(up, dact, cw_p, cb_p)
    d_ffn_conv, d_ffn_conv_b = _ffn_unpair(d_cw_p, 1), _ffn_unpair(d_cb_p, 1)
    dup_2 = dup.reshape(ntok, 2 * D_FF)
    dh_f = _mm(dup_2, wup_f, "nn", F32, "mm_dhf", tn=1024).reshape(bsz, t_total, D)
    g_wup_t = _mm(dup_2, h_f.reshape(ntok, D), "tn", MXU_DT, "mm_gwup", tm=1408, tn=1024)
    dr1, dsc_f, dsh_f, d_ln1_g, d_ln1_b, dy, dgt_a = _ln_bwd_call(
        "ln1_bwd", dr2, dh_f, r1, ln1_g, ln1_b, sc_f, y=y, gt=gt_a)

    dy_2 = dy.reshape(ntok, D)
    do = _mm(dy_2, wo_f, "nt", F32, "mm_do", tm=1024, tn=1024).reshape(bsz, t_total, D)
    g_wo = _mm(o_mix.reshape(ntok, D), dy_2, "tn", MXU_DT, "mm_gwo", tm=512, tn=1024)
    do_dn, do_gla, dz, dgg, d_dn_norm, d_gla_norm = _mix_out_bwd(do, o_dn, o_gla, proj, grow_dn, grow_gla)
    dq, dk, dv, dgates = _dn_rec_bwd(q, k, v, gates, s_dn, inv_dn, do_dn)
    dqkv, dsm_dn, d_dn_conv, d_alog_row, d_dt_row = _dn_pre_bwd(proj, dq, dk, dv, dgates, dn_conv_f, alog_row, dt_row)
    dgq, dgk, dgv, dsm, d_w2, d_bg = _gla_rec_bwd(proj, w2, bg, s_gla, do_gla, dsm_dn)
    dproj = jnp.concatenate([dqkv, dz, dgq, dgk, dgv, dgg, dsm], axis=-1).reshape(ntok, P_W)
    dh_a = _mm(dproj, win_p, "nn", F32, "mm_dha", tn=1024).reshape(bsz, t_total, D)
    g_win_p = _mm(dproj, h_a.reshape(ntok, D), "tn", MXU_DT, "mm_gwin", tm=1408, tn=1024)
    grad_x, dsc_a, dsh_a, d_ln0_g, d_ln0_b = _ln_bwd_call("ln0_bwd", dr1, dh_a, x, g0, b0, sc_a)

    big = [_unpad_in_rows(g_win_p).reshape(N_DEV, -1, D), g_wo.reshape(N_DEV, -1, D),
           _ffn_unpair(g_wup_t, 0).reshape(N_DEV, -1, D), g_wdn.reshape(N_DEV, -1, D)]
    from_sibling = _scatter_sibling(big)
    core = mc.reshape(1).astype(jnp.int32)
    chip_sums = [_pair_add(g8, r1_, core, f"pair_add_{i}") for i, (g8, r1_) in enumerate(zip(big, from_sibling))]
    from_chips = _scatter_chips(chip_sums)
    chip = (2 * mx + my).reshape(1).astype(jnp.int32)
    g_win_t, g_wo_s, g_wup_ts, g_wdn_s = (
        _chip_add(p4, r2_, chip, f"chip_add_{i}") for i, (p4, r2_) in enumerate(zip(chip_sums, from_chips)))

    dmod = jnp.concatenate([dsh_a, dsc_a, dgt_a, dsh_f, dsc_f, dgt_f], axis=1).reshape(-1)
    small_parts = {
        "ln0_g": d_ln0_g, "ln0_b": d_ln0_b, "ln1_g": d_ln1_g, "ln1_b": d_ln1_b, "ln2_g": d_ln2_g, "ln2_b": d_ln2_b,
        "dn_a_log": d_alog_row[:, :HEADS], "dn_dt_bias": d_dt_row[:, :HEADS],
        "dn_norm_g": d_dn_norm, "gla_norm_g": d_gla_norm, "gla_b_gate": _unpad_heads(d_bg, 1),
        "ffn_conv_b": d_ffn_conv_b, "dn_conv": d_dn_conv,
        "gla_w_gate2": _unpad_heads(d_w2[SM_R:SM_R + GATE_RANK], 1), "ffn_conv": d_ffn_conv}
    order = sorted(small_parts)
    flat = jnp.concatenate([small_parts[n].reshape(-1) for n in order] + [dmod])
    n3 = flat.size
    rows3 = -(-n3 // 1024) * 8
    pack3 = jnp.pad(flat, (0, rows3 * 128 - n3)).reshape(rows3, 128)
    got3 = _gather_small(pack3, "gather_small_grads")
    tot3 = _sum_slots(got3, "sum_small_grads").reshape(-1)
    grads = {}
    off = 0
    for n in order:
        size = small_parts[n].size
        grads[n] = tot3[off:off + size]
        off += size
    dmod_all = got3.reshape(N_DEV, -1)[:, off:off + dmod.size].reshape(N_DEV * bsz, 6 * D)
    dmod_cols = lax.dynamic_slice_in_dim(dmod_all, me * ncol, ncol, axis=1)
    g_wada, g_bada = _ada_bwd(c_all, dmod_all, dmod_cols)
    grads["b_ada"] = g_bada

    def col_shard(full, rows):
        part = full.reshape(rows, -1)
        width = part.shape[1] // N_DEV
        return lax.dynamic_slice_in_dim(part, me * width, width, axis=1)

    grads["dn_conv"] = col_shard(grads["dn_conv"], DN_CONV_K)
    grads["gla_w_gate2"] = col_shard(grads["gla_w_gate2"], GATE_RANK)
    grads["ffn_conv"] = col_shard(grads["ffn_conv"], FFN_CONV_K)
    grads = {n: g.reshape(w_given[n].shape) for n, g in grads.items()}
    grads["w_ada"] = g_wada.reshape(w_ada.shape)
    grads["w_in"] = g_win_t.T.reshape(w_in.shape)
    grads["w_o"] = g_wo_s.reshape(w_o.shape)
    grads["ffn_w_up"] = g_wup_ts.T.reshape(ffn_w_up.shape)
    grads["ffn_w_down"] = g_wdn_s.reshape(ffn_w_down.shape)

    delta, new_m, new_v = {}, {}, {}
    for n in ["w_ada", "w_in", "w_o", "ffn_w_up", "ffn_w_down"]:
        shp = w_given[n].shape
        two_d = lambda a: a.reshape(shp[-2], shp[-1])
        d_, m_, v_ = _adamw(two_d(w_given[n]), two_d(grads[n]), two_d(m_given[n]), two_d(v_given[n]), "adamw_" + n)
        delta[n], new_m[n], new_v[n] = d_.reshape(shp), m_.reshape(shp), v_.reshape(shp)

    def pack_small(src):
        flat_ = jnp.concatenate([src[n].reshape(-1) for n in SMALL_NAMES])
        rows_ = -(-flat_.size // 1024) * 8
        return jnp.pad(flat_, (0, rows_ * 128 - flat_.size)).reshape(rows_, 128)

    d_s, m_s, v_s = _adamw(pack_small(w_given), pack_small(grads), pack_small(m_given), pack_small(v_given),
                           "adamw_small")
    off = 0
    for n in SMALL_NAMES:
        size, shp = w_given[n].size, w_given[n].shape
        delta[n] = d_s.reshape(-1)[off:off + size].reshape(shp)
        new_m[n] = m_s.reshape(-1)[off:off + size].reshape(shp)
        new_v[n] = v_s.reshape(-1)[off:off + size].reshape(shp)
        off += size

    return (loss, grad_x, *[grads[n] for n in WEIGHTS], *[delta[n] for n in WEIGHTS],
            *[new_m[n] for n in WEIGHTS], *[new_v[n] for n in WEIGHTS])
```

```python
import functools

import jax
import jax.numpy as jnp
from jax import lax
from jax.experimental import pallas as pl
from jax.experimental.pallas import tpu as pltpu

F32 = jnp.float32
MXU_DT = jnp.bfloat16
HI = lax.Precision.HIGHEST
MESH = pl.DeviceIdType.MESH
N_DEV = 8

D = 1024
HEADS = 4
HD = 128
CHUNK = 64
GLA_KEY = 64
GLA_TAU = 16.0
GATE_RANK = 16
D_FF = 2816
IN_W = 3608
ALPHA = 2.0 ** 0.25
EPS = 1e-6
DN_CONV_K = 4
FFN_CONV_K = 3
HALO = 8

P_QKV, P_Z, P_GQ, P_GK, P_GV, P_GG, P_SM, P_W = 0, 1536, 2048, 2560, 3072, 3584, 4096, 4224
SM_A, SM_B, SM_R = 0, 4, 8

ADAM_LR, ADAM_B1, ADAM_B2, ADAM_EPS, ADAM_WD, ADAM_STEP = 0.001, 0.9, 0.999, 1e-08, 0.01, 10

VMEM_LIMIT_V7X = 56 * 1024 * 1024


def _params(sem=None):
    return pltpu.CompilerParams(dimension_semantics=sem, vmem_limit_bytes=VMEM_LIMIT_V7X)


def _dg(a, b, dims, prec=None):
    return lax.dot_general(a, b, (dims, ((), ())), precision=prec, preferred_element_type=F32)


def _dot(a, b, prec=None):
    return _dg(a, b, ((1,), (0,)), prec)


def _dot_nt(a, b, prec=None):
    return _dg(a, b, ((1,), (1,)), prec)


def _dot_tn(a, b, prec=None):
    return _dg(a, b, ((0,), (0,)), prec)


def _iota(shape, dim):
    return lax.broadcasted_iota(jnp.int32, shape, dim)


def _sigmoid(x):
    return jax.nn.sigmoid(x)


def _silu(x):
    return x * _sigmoid(x)


def _softplus(x):
    return jnp.maximum(x, 0.0) + jnp.log(1.0 + jnp.exp(-jnp.abs(x)))


def _ln_stats(x):
    mu = jnp.mean(x, axis=-1, keepdims=True)
    xc = x - mu
    rstd = lax.rsqrt(jnp.mean(xc * xc, axis=-1, keepdims=True) + EPS)
    return xc * rstd, rstd


def _ln_bwd(dxhat, xhat, rstd):
    return rstd * (dxhat - jnp.mean(dxhat, axis=-1, keepdims=True)
                   - xhat * jnp.mean(dxhat * xhat, axis=-1, keepdims=True))


NN, NT, TN = ((1,), (0,)), ((1,), (1,)), ((0,), (0,))


def _split2(a):
    hi = a.astype(jnp.bfloat16)
    return hi, (a - hi.astype(F32)).astype(jnp.bfloat16)


def _d3(a, b, dims):
    ah, al = _split2(a)
    bh, bl = _split2(b)
    return _dg(ah, bh, dims) + (_dg(ah, bl, dims) + _dg(al, bh, dims))


@jax.custom_vjp
def _dot3(a, b):
    return _d3(a, b, NN)


_dot3.defvjp(lambda a, b: (_d3(a, b, NN), (a, b)),
             lambda res, g: (_d3(g, res[1], NT), _d3(res[0], g, TN)))


def _split3(b):
    b1 = b.astype(jnp.bfloat16)
    r1 = b - b1.astype(F32)
    b2 = r1.astype(jnp.bfloat16)
    return b1, b2, (r1 - b2.astype(F32)).astype(jnp.bfloat16)


def _sum3(fn, b):
    b1, b2, b3 = _split3(b)
    return fn(b1) + (fn(b2) + fn(b3))


@jax.custom_vjp
def _mask_dot(e, b):
    return _sum3(lambda t: _dg(e, t, NN), b)


_mask_dot.defvjp(lambda e, b: (_mask_dot(e, b), e),
                 lambda e, g: (jnp.zeros_like(e), _sum3(lambda t: _dg(e, t, TN), g)))


@jax.custom_vjp
def _mask_dot_nt(e, b):
    return _sum3(lambda t: _dg(e, t, NT), b)


_mask_dot_nt.defvjp(lambda e, b: (_mask_dot_nt(e, b), e),
                    lambda e, g: (jnp.zeros_like(e), _sum3(lambda t: _dg(t, e, TN), g)))


def _tri_inv_impl(ms):
    n = ms[0].shape[0]
    r, c = _iota((n, n), 0), _iota((n, n), 1)
    eye = (r == c).astype(F32)
    diag = (r >> 3) == (c >> 3)
    ds = [jnp.where(diag, m, 0.0) for m in ms]
    d2s = [_d3(d, d, NN) for d in ds]
    d4s = [_d3(d2, d2, NN) for d2 in d2s]
    invs = [_d3(eye - d, eye + d2, NN) for d, d2 in zip(ds, d2s)]
    invs = [_d3(inv, eye + d4, NN) for inv, d4 in zip(invs, d4s)]
    shift = 3
    while (1 << shift) < n:
        rb, cb = r >> shift, c >> shift
        sel = ((rb & 1) == 1) & (cb == rb - 1)
        tmp = [_d3(inv, jnp.where(sel, m, 0.0), NN) for inv, m in zip(invs, ms)]
        invs = [inv - _d3(t, inv, NN) for t, inv in zip(tmp, invs)]
        shift += 1
    return invs


@jax.custom_vjp
def _tri_inv(ms):
    return _tri_inv_impl(ms)


def _tri_inv_fwd(ms):
    invs = _tri_inv_impl(ms)
    return invs, invs


def _tri_inv_bwd(invs, das):
    tmp = [_d3(a, da, TN) for a, da in zip(invs, das)]
    return ([-_d3(t, a, NT) for t, a in zip(tmp, invs)],)


_tri_inv.defvjp(_tri_inv_fwd, _tri_inv_bwd)


@jax.custom_vjp
def _tri_inv_known(ms, invs):
    return invs


_tri_inv_known.defvjp(lambda ms, invs: (invs, invs),
                      lambda invs, das: (_tri_inv_bwd(invs, das)[0], [jnp.zeros_like(a) for a in invs]))


def _dn_chunk(s_list, q, k, v, gates, inv_known=None, with_inv=False):
    nb = len(q)
    c = q[0].shape[0]
    r64, c64 = _iota((c, c), 0), _iota((c, c), 1)
    causal = r64 >= c64
    strict = r64 > c64
    tri = causal.astype(jnp.bfloat16)
    eye = (_iota((HD, HD), 0) == _iota((HD, HD), 1)).astype(jnp.bfloat16)
    lane = _iota(gates[0].shape, 1)
    lane1 = _iota((1, HD), 1)
    g_all = [_mask_dot(tri, g) for g in gates]
    g_all_t = [_mask_dot_nt(eye, g) for g in g_all]
    row = _iota(g_all_t[0].shape, 0)
    last = [jnp.sum(g, axis=0, keepdims=True) for g in gates]
    prob = [(b, h) for b in range(nb) for h in range(HEADS)]
    sl = [slice(h * HD, (h + 1) * HD) for h in range(HEADS)]
    qh = [q[b][:, sl[h]] for b, h in prob]
    kh = [k[b][:, sl[h]] for b, h in prob]
    vh = [v[b][:, sl[h]] for b, h in prob]
    s = [s_list[b][h] for b, h in prob]
    beta = [jnp.sum(jnp.where(lane == SM_B + h, gates[b], 0.0), axis=-1, keepdims=True) for b, h in prob]
    g_c = [jnp.sum(jnp.where(lane == SM_A + h, g_all[b], 0.0), axis=-1, keepdims=True) for b, h in prob]
    g_r = [jnp.sum(jnp.where(row == SM_A + h, g_all_t[b], 0.0), axis=0, keepdims=True) for b, h in prob]
    g_last = [jnp.sum(jnp.where(lane1 == SM_A + h, last[b], 0.0), axis=-1, keepdims=True) for b, h in prob]
    decay = [jnp.where(causal, jnp.exp(jnp.where(causal, gc - gr, 0.0)), 0.0) for gc, gr in zip(g_c, g_r)]
    kb = [k_ * b_ for k_, b_ in zip(kh, beta)]
    m_low = [jnp.where(strict, _dot_nt(kb_, k_) * d_, 0.0) for kb_, k_, d_ in zip(kb, kh, decay)]
    attn = [_dot_nt(q_, k_) * d_ for q_, k_, d_ in zip(qh, kh, decay)]
    a_inv = _tri_inv(m_low) if inv_known is None else _tri_inv_known(m_low, inv_known)
    eg = [jnp.exp(gc) for gc in g_c]
    uw = [_dot3(a_, jnp.concatenate([v_ * b_, kb_ * e_], axis=1))
          for a_, v_, b_, kb_, e_ in zip(a_inv, vh, beta, kb, eg)]
    v_new = [uw_[:, :HD] - _dot(uw_[:, HD:], s_) for uw_, s_ in zip(uw, s)]
    qs = [_dot(q_ * e_, s_) for q_, e_, s_ in zip(qh, eg, s)]
    o = [qs_ + _dot(a_, vn_) for qs_, a_, vn_ in zip(qs, attn, v_new)]
    k_dec = [k_ * jnp.exp(gl - gc) for k_, gl, gc in zip(kh, g_last, g_c)]
    s_new = [s_ * jnp.exp(gl) + _dot_tn(kd_, vn_) for s_, gl, kd_, vn_ in zip(s, g_last, k_dec, v_new)]
    outs = [jnp.concatenate(o[b * HEADS:(b + 1) * HEADS], axis=-1) for b in range(nb)]
    states = [s_new[b * HEADS:(b + 1) * HEADS] for b in range(nb)]
    return (outs, states, a_inv) if with_inv else (outs, states)


def _gla_chunk(st_list, q, k, v, small, w2, bg):
    nb = len(q)
    c = q[0].shape[0]
    causal = _iota((c, c), 0) >= _iota((c, c), 1)
    tri = causal.astype(jnp.bfloat16)
    la_all = [-_softplus(-(_dot(sm, w2) + bg)) * (1.0 / GLA_TAU) for sm in small]
    b_all = [_mask_dot(tri, la) for la in la_all]
    prob = [(b, h) for b in range(nb) for h in range(HEADS)]
    sl = [slice(h * HD, (h + 1) * HD) for h in range(HEADS)]
    kh = [k[b][:, sl[h]] for b, h in prob]
    vh = [v[b][:, sl[h]] for b, h in prob]
    st = [st_list[b][h] for b, h in prob]
    bc = [b_all[b][:, sl[h]] for b, h in prob]
    b_last = [jnp.sum(la_all[b][:, sl[h]], axis=0, keepdims=True) for b, h in prob]
    q_dec = [q[b][:, sl[h]] * (GLA_KEY ** -0.5) * jnp.exp(bc_) for (b, h), bc_ in zip(prob, bc)]
    attn = [jnp.where(causal, _dot_nt(qd, k_ * jnp.exp(-bc_)), 0.0) for qd, k_, bc_ in zip(q_dec, kh, bc)]
    inter = [_dot_nt(qd, st_) for qd, st_ in zip(q_dec, st)]
    o = [i_ + _dot(a_, v_) for i_, a_, v_ in zip(inter, attn, vh)]
    k_dec = [k_ * jnp.exp(bl - bc_) for k_, bl, bc_ in zip(kh, b_last, bc)]
    s_new = [st_ * jnp.exp(bl) + _dot_tn(v_, kd) for st_, bl, v_, kd in zip(st, b_last, vh, k_dec)]
    outs = [jnp.concatenate(o[b * HEADS:(b + 1) * HEADS], axis=-1) for b in range(nb)]
    return outs, [s_new[b * HEADS:(b + 1) * HEADS] for b in range(nb)]


def _dn_qkv(y):
    act = _silu(y)
    parts = []
    for i in range(2 * HEADS):
        xh = act[:, i * HD:(i + 1) * HD]
        xh = xh * lax.rsqrt(jnp.sum(xh * xh, axis=-1, keepdims=True) + EPS)
        parts.append(xh * (HD ** -0.5) if i < HEADS else xh)
    qk = jnp.concatenate(parts, axis=-1)
    return qk[:, :HEADS * HD], qk[:, HEADS * HD:], act[:, 2 * HEADS * HD:]


def _dn_gates(small, alog_row, dt_row):
    lane = _iota(small.shape, 1)
    log_a = -jnp.exp(alog_row) * _softplus(small + dt_row)
    return jnp.where(lane < SM_B, log_a, jnp.where(lane < SM_R, _sigmoid(small), 0.0))


def _gate_norm(o, z, grow):
    parts = []
    for h in range(HEADS):
        oh = o[:, h * HD:(h + 1) * HD]
        parts.append(oh * lax.rsqrt(jnp.mean(oh * oh, axis=-1, keepdims=True) + EPS))
    return jnp.concatenate(parts, axis=-1) * grow * _silu(z)


def _conv_rows(xrows, w_ref, k_taps):
    n = xrows.shape[0]
    acc = xrows * w_ref[k_taps - 1:k_taps, :]
    for s in range(1, k_taps):
        acc = acc + pltpu.roll(xrows, s, 0) * w_ref[k_taps - 1 - s:k_taps - s, :]
    return acc


def _shift_up(x, s):
    return x if s == 0 else pltpu.roll(x, x.shape[0] - s, 0)


def _div_tile(n, cap, mult=8):
    best = None
    for t in range(mult, min(n, cap) + 1, mult):
        if n % t == 0:
            best = t
    return best if best is not None else n


def _halo_prev(tt):
    return lambda b, t: (b, jnp.maximum(t * (tt // HALO) - 1, 0))


def _halo_next(tt, t_total):
    return lambda b, t: (b, jnp.minimum((t + 1) * (tt // HALO), t_total // HALO - 1))


def _mm(a, b, mode, out_dtype, name, tm=512, tn=512, tk=None):
    if mode == "nn":
        (m, k), n = a.shape, b.shape[1]
    elif mode == "nt":
        (m, k), n = a.shape, b.shape[0]
    else:
        (k, m), n = a.shape, b.shape[1]
    tm, tn = min(tm, m), min(tn, n)
    tk = k if tk is None else min(tk, k)
    assert m % tm == 0 and n % tn == 0 and k % tk == 0, (name, a.shape, b.shape, tm, tn, tk)
    nk = k // tk
    if mode == "tn":
        a_spec = pl.BlockSpec((tk, tm), lambda i, j, kk: (kk, i))
    else:
        a_spec = pl.BlockSpec((tm, tk), lambda i, j, kk: (i, kk))
    if mode == "nt":
        b_spec = pl.BlockSpec((tn, tk), lambda i, j, kk: (j, kk))
    else:
        b_spec = pl.BlockSpec((tk, tn), lambda i, j, kk: (kk, j))
    dims = {"nn": ((1,), (0,)), "nt": ((1,), (1,)), "tn": ((0,), (0,))}[mode]

    def body(a_ref, b_ref, o_ref, *acc):
        p = _dg(a_ref[...], b_ref[...], dims)
        if nk == 1:
            o_ref[...] = p.astype(out_dtype)
        else:
            kk = pl.program_id(2)

            @pl.when(kk == 0)
            def _():
                acc[0][...] = p

            @pl.when(kk > 0)
            def _():
                acc[0][...] += p

            @pl.when(kk == nk - 1)
            def _():
                o_ref[...] = acc[0][...].astype(out_dtype)

    return pl.pallas_call(
        body, name=name, grid=(m // tm, n // tn, nk),
        in_specs=[a_spec, b_spec],
        out_specs=pl.BlockSpec((tm, tn), lambda i, j, kk: (i, j)),
        out_shape=jax.ShapeDtypeStruct((m, n), out_dtype),
        scratch_shapes=[pltpu.VMEM((tm, tn), F32)] if nk > 1 else [],
        compiler_params=_params(("parallel", "parallel", "arbitrary")),
    )(a, b)


def _ada_fwd(c_all, w_ada, b_cols):
    def body(c_ref, w_ref, b_ref, o_ref):
        cond = _silu(c_ref[...]).astype(MXU_DT)
        o_ref[...] = _dot(cond, w_ref[...].astype(MXU_DT)) + b_ref[...]

    return pl.pallas_call(body, name="ada_fwd", out_shape=jax.ShapeDtypeStruct((c_all.shape[0], w_ada.shape[1]), F32),
                          compiler_params=_params())(c_all, w_ada, b_cols)


def _ada_bwd(c_all, dmod_all, dmod_cols):
    def body(c_ref, da_ref, dc_ref, gw_ref, gb_ref):
        cond = _silu(c_ref[...]).astype(MXU_DT)
        gw_ref[...] = _dot_tn(cond, dc_ref[...].astype(MXU_DT))
        gb_ref[...] = jnp.sum(da_ref[...], axis=0, keepdims=True)

    return pl.pallas_call(
        body, name="ada_bwd",
        out_shape=(jax.ShapeDtypeStruct((c_all.shape[1], dmod_cols.shape[1]), F32),
                   jax.ShapeDtypeStruct((1, dmod_all.shape[1]), F32)),
        compiler_params=_params())(c_all, dmod_all, dmod_cols)


def _tok_spec(tt, width=D):
    return pl.BlockSpec((1, tt, width), lambda b, t: (b, t, 0))


def _vec_spec(width=D):
    return pl.BlockSpec((1, width), lambda b, t: (0, 0))


def _bvec_spec(width=D):
    return pl.BlockSpec((1, 1, width), lambda b, t: (b, 0, 0))


def _ln0_mod(x, g0, b0, sc, sh):
    bsz, t_total, _ = x.shape
    tt = _div_tile(t_total, 256)

    def body(x_ref, g_ref, b_ref, sc_ref, sh_ref, h_ref):
        xh, _ = _ln_stats(x_ref[0])
        x0 = xh * g_ref[...] + b_ref[...]
        h_ref[0] = (x0 * (1.0 + sc_ref[0]) + sh_ref[0]).astype(MXU_DT)

    return pl.pallas_call(
        body, name="ln0_mod", grid=(bsz, t_total // tt),
        in_specs=[_tok_spec(tt), _vec_spec(), _vec_spec(), _bvec_spec(), _bvec_spec()],
        out_specs=_tok_spec(tt), out_shape=jax.ShapeDtypeStruct(x.shape, MXU_DT),
        compiler_params=_params(("parallel", "parallel")))(x, g0, b0, sc, sh)


def _res_ln_mod(x, y, gt, g0, b0, g1, b1, sc, sh):
    bsz, t_total, _ = x.shape
    tt = _div_tile(t_total, 256)

    def body(x_ref, y_ref, gt_ref, g0_ref, b0_ref, g1_ref, b1_ref, sc_ref, sh_ref, r_ref, h_ref):
        xh, _ = _ln_stats(x_ref[0])
        r = ALPHA * (xh * g0_ref[...] + b0_ref[...]) + (1.0 + gt_ref[0]) * y_ref[0]
        r_ref[0] = r
        rh, _ = _ln_stats(r)
        x1 = rh * g1_ref[...] + b1_ref[...]
        h_ref[0] = (x1 * (1.0 + sc_ref[0]) + sh_ref[0]).astype(MXU_DT)

    return pl.pallas_call(
        body, name="res_ln_mod", grid=(bsz, t_total // tt),
        in_specs=[_tok_spec(tt), _tok_spec(tt), _bvec_spec(), _vec_spec(), _vec_spec(), _vec_spec(), _vec_spec(),
                  _bvec_spec(), _bvec_spec()],
        out_specs=(_tok_spec(tt), _tok_spec(tt)),
        out_shape=(jax.ShapeDtypeStruct(x.shape, F32), jax.ShapeDtypeStruct(x.shape, MXU_DT)),
        compiler_params=_params(("parallel", "parallel")))(x, y, gt, g0, b0, g1, b1, sc, sh)


def _final_fwd_bwd(r1, y2, gt, g1, b1, g2, b2, target):
    bsz, t_total, _ = r1.shape
    tt = _div_tile(t_total, 256)

    def body(r1_ref, y2_ref, gt_ref, g1_ref, b1_ref, g2_ref, b2_ref, tg_ref,
             loss_ref, dr2_ref, dy2_ref, dgt_ref, dg2_ref, db2_ref):
        b, t = pl.program_id(0), pl.program_id(1)

        @pl.when((b == 0) & (t == 0))
        def _():
            loss_ref[...] = jnp.zeros_like(loss_ref)
            dg2_ref[...] = jnp.zeros_like(dg2_ref)
            db2_ref[...] = jnp.zeros_like(db2_ref)

        @pl.when(t == 0)
        def _():
            dgt_ref[...] = jnp.zeros_like(dgt_ref)

        rh1, _ = _ln_stats(r1_ref[0])
        x1 = rh1 * g1_ref[...] + b1_ref[...]
        y2 = y2_ref[0]
        gate = 1.0 + gt_ref[0]
        xh2, rstd2 = _ln_stats(ALPHA * x1 + gate * y2)
        err = xh2 * g2_ref[...] + b2_ref[...] - tg_ref[0]
        loss_ref[...] += jnp.sum(err * err, axis=0, keepdims=True)
        dx2 = err * (1.0 / D)
        dg2_ref[...] += jnp.sum(dx2 * xh2, axis=0, keepdims=True)
        db2_ref[...] += jnp.sum(dx2, axis=0, keepdims=True)
        dr2 = _ln_bwd(dx2 * g2_ref[...], xh2, rstd2)
        dr2_ref[0] = dr2
        dy2_ref[0] = (gate * dr2).astype(MXU_DT)
        dgt_ref[0] += jnp.sum(dr2 * y2, axis=0, keepdims=True)

    vec_out = jax.ShapeDtypeStruct((1, D), F32)
    return pl.pallas_call(
        body, name="final_fwd_bwd", grid=(bsz, t_total // tt),
        in_specs=[_tok_spec(tt), _tok_spec(tt), _bvec_spec(), _vec_spec(), _vec_spec(), _vec_spec(), _vec_spec(),
                  _tok_spec(tt)],
        out_specs=(_vec_spec(), _tok_spec(tt), _tok_spec(tt), _bvec_spec(), _vec_spec(), _vec_spec()),
        out_shape=(vec_out, jax.ShapeDtypeStruct(r1.shape, F32), jax.ShapeDtypeStruct(r1.shape, MXU_DT),
                   jax.ShapeDtypeStruct((bsz, 1, D), F32), vec_out, vec_out),
        compiler_params=_params(("arbitrary", "arbitrary")))(r1, y2, gt, g1, b1, g2, b2, target)


def _ln_bwd_call(name, d_res, d_h, src, g, b, sc, y=None, gt=None):
    bsz, t_total, _ = src.shape
    tt = _div_tile(t_total, 256)
    has_y = y is not None

    def body(*refs):
        if has_y:
            (dres_ref, dh_ref, src_ref, g_ref, b_ref, sc_ref, y_ref, gt_ref,
             dsrc_ref, dsc_ref, dsh_ref, dg_ref, db_ref, dy_ref, dgt_ref) = refs
        else:
            (dres_ref, dh_ref, src_ref, g_ref, b_ref, sc_ref,
             dsrc_ref, dsc_ref, dsh_ref, dg_ref, db_ref) = refs
        bi, t = pl.program_id(0), pl.program_id(1)

        @pl.when((bi == 0) & (t == 0))
        def _():
            dg_ref[...] = jnp.zeros_like(dg_ref)
            db_ref[...] = jnp.zeros_like(db_ref)

        @pl.when(t == 0)
        def _():
            dsc_ref[...] = jnp.zeros_like(dsc_ref)
            dsh_ref[...] = jnp.zeros_like(dsh_ref)
            if has_y:
                dgt_ref[...] = jnp.zeros_like(dgt_ref)

        xh, rstd = _ln_stats(src_ref[0])
        xv = xh * g_ref[...] + b_ref[...]
        dh = dh_ref[0]
        dx = ALPHA * dres_ref[0] + dh * (1.0 + sc_ref[0])
        dsc_ref[0] += jnp.sum(dh * xv, axis=0, keepdims=True)
        dsh_ref[0] += jnp.sum(dh, axis=0, keepdims=True)
        dg_ref[...] += jnp.sum(dx * xh, axis=0, keepdims=True)
        db_ref[...] += jnp.sum(dx, axis=0, keepdims=True)
        dsrc = _ln_bwd(dx * g_ref[...], xh, rstd)
        dsrc_ref[0] = dsrc
        if has_y:
            dy_ref[0] = ((1.0 + gt_ref[0]) * dsrc).astype(MXU_DT)
            dgt_ref[0] += jnp.sum(dsrc * y_ref[0], axis=0, keepdims=True)

    vec_out = jax.ShapeDtypeStruct((1, D), F32)
    bvec_out = jax.ShapeDtypeStruct((bsz, 1, D), F32)
    in_specs = [_tok_spec(tt), _tok_spec(tt), _tok_spec(tt), _vec_spec(), _vec_spec(), _bvec_spec()]
    out_specs = [_tok_spec(tt), _bvec_spec(), _bvec_spec(), _vec_spec(), _vec_spec()]
    out_shape = [jax.ShapeDtypeStruct(src.shape, F32), bvec_out, bvec_out, vec_out, vec_out]
    args = [d_res, d_h, src, g, b, sc]
    if has_y:
        in_specs += [_tok_spec(tt), _bvec_spec()]
        out_specs += [_tok_spec(tt), _bvec_spec()]
        out_shape += [jax.ShapeDtypeStruct(src.shape, MXU_DT), bvec_out]
        args += [y, gt]
    return pl.pallas_call(body, name=name, grid=(bsz, t_total // tt), in_specs=in_specs, out_specs=tuple(out_specs),
                          out_shape=tuple(out_shape), compiler_params=_params(("arbitrary", "arbitrary")))(*args)


FFN_TC = 256
FFN_NJ = D_FF // FFN_TC
FFN_PW = 2 * FFN_TC


def _ffn_pair(a, axis):
    shp = list(a.shape)
    a4 = a.reshape(shp[:axis] + [2, FFN_NJ, FFN_TC] + shp[axis + 1:])
    return jnp.swapaxes(a4, axis, axis + 1).reshape(shp)


def _ffn_unpair(a, axis):
    shp = list(a.shape)
    a4 = a.reshape(shp[:axis] + [FFN_NJ, 2, FFN_TC] + shp[axis + 1:])
    return jnp.swapaxes(a4, axis, axis + 1).reshape(shp)


def _ffn_act_fwd(up, cw, cb):
    bsz, t_total, _ = up.shape
    tt = _div_tile(t_total, 256)
    hp = _halo_prev(tt)

    def body(x_ref, xp_ref, w_ref, b_ref, o_ref):
        prev = jnp.where(pl.program_id(1) == 0, 0.0, xp_ref[0])
        rows = jnp.concatenate([prev, x_ref[0]], axis=0)
        u = _conv_rows(rows, w_ref, FFN_CONV_K)[HALO:] + b_ref[...]
        o_ref[0] = (_silu(u[:, :FFN_TC]) * u[:, FFN_TC:]).astype(MXU_DT)

    return pl.pallas_call(
        body, name="ffn_act_fwd", grid=(bsz, t_total // tt, FFN_NJ),
        in_specs=[pl.BlockSpec((1, tt, FFN_PW), lambda b, t, j: (b, t, j)),
                  pl.BlockSpec((1, HALO, FFN_PW), lambda b, t, j: (*hp(b, t), j)),
                  pl.BlockSpec((FFN_CONV_K, FFN_PW), lambda b, t, j: (0, j)),
                  pl.BlockSpec((1, FFN_PW), lambda b, t, j: (0, j))],
        out_specs=pl.BlockSpec((1, tt, FFN_TC), lambda b, t, j: (b, t, j)),
        out_shape=jax.ShapeDtypeStruct((bsz, t_total, D_FF), MXU_DT),
        compiler_params=_params(("parallel", "parallel", "parallel")))(up, up, cw, cb)


def _ffn_act_bwd(up, da, cw, cb):
    bsz, t_total, width = up.shape
    tt = _div_tile(t_total, 256)
    nt = t_total // tt
    hp, hn = _halo_prev(tt), _halo_next(tt, t_total)

    def body(x_ref, xp_ref, xn_ref, da_ref, dan_ref, w_ref, b_ref, dup_ref, dw_ref, db_ref):
        b, t = pl.program_id(1), pl.program_id(2)

        @pl.when((b == 0) & (t == 0))
        def _():
            dw_ref[...] = jnp.zeros_like(dw_ref)
            db_ref[...] = jnp.zeros_like(db_ref)

        prev = jnp.where(t == 0, 0.0, xp_ref[0])
        rows = jnp.concatenate([prev, x_ref[0], xn_ref[0]], axis=0)
        u = _conv_rows(rows, w_ref, FFN_CONV_K)[HALO:] + b_ref[...]
        g_pre, v_pre = u[:, :FFN_TC], u[:, FFN_TC:]
        valid = (_iota((tt + HALO, 1), 0) < tt) | (t < nt - 1)
        da_ext = jnp.where(valid, jnp.concatenate([da_ref[0], dan_ref[0]], axis=0), 0.0)
        sg = _sigmoid(g_pre)
        gs = g_pre * sg
        du = jnp.concatenate([da_ext * v_pre * (sg + gs * (1.0 - sg)), da_ext * gs], axis=1)
        dup = du * w_ref[FFN_CONV_K - 1:FFN_CONV_K, :]
        for s in range(1, FFN_CONV_K):
            dup = dup + _shift_up(du, s) * w_ref[FFN_CONV_K - 1 - s:FFN_CONV_K - s, :]
        dup_ref[0] = dup[:tt].astype(MXU_DT)
        du_t = du[:tt]
        db_ref[...] += jnp.sum(du_t, axis=0, keepdims=True)
        for k in range(FFN_CONV_K):
            s = FFN_CONV_K - 1 - k
            xs = (rows if s == 0 else pltpu.roll(rows, s, 0))[HALO:HALO + tt]
            dw_ref[k:k + 1, :] += jnp.sum(du_t * xs, axis=0, keepdims=True)

    def halo(h, w):
        return pl.BlockSpec((1, HALO, w), lambda j, b, t: (*h(b, t), j))

    wspec = lambda rows_: pl.BlockSpec((rows_, FFN_PW), lambda j, b, t: (0, j))
    tile = pl.BlockSpec((1, tt, FFN_PW), lambda j, b, t: (b, t, j))
    return pl.pallas_call(
        body, name="ffn_act_bwd", grid=(FFN_NJ, bsz, nt),
        in_specs=[tile, halo(hp, FFN_PW), halo(hn, FFN_PW),
                  pl.BlockSpec((1, tt, FFN_TC), lambda j, b, t: (b, t, j)), halo(hn, FFN_TC),
                  wspec(FFN_CONV_K), wspec(1)],
        out_specs=(tile, wspec(FFN_CONV_K), wspec(1)),
        out_shape=(jax.ShapeDtypeStruct(up.shape, MXU_DT), jax.ShapeDtypeStruct((FFN_CONV_K, width), F32),
                   jax.ShapeDtypeStruct((1, width), F32)),
        compiler_params=_params(("arbitrary", "arbitrary", "arbitrary")))(up, up, up, da, da, cw, cb)


QKV_W = 3 * HEADS * HD
SM_BLK = P_SM // 128


def _dn_pre_fwd(proj, conv_w, alog_row, dt_row):
    bsz, t_total, _ = proj.shape
    tt = _div_tile(t_total, 256)
    hp = _halo_prev(tt)

    def body(x_ref, xp_ref, sm_ref, w_ref, al_ref, dt_ref, q_ref, k_ref, v_ref, g_ref):
        prev = jnp.where(pl.program_id(1) == 0, 0.0, xp_ref[0])
        y = _conv_rows(jnp.concatenate([prev, x_ref[0]], axis=0), w_ref, DN_CONV_K)[HALO:]
        q_ref[0], k_ref[0], v_ref[0] = _dn_qkv(y)
        g_ref[0] = _dn_gates(sm_ref[0], al_ref[...], dt_ref[...])

    out512 = jax.ShapeDtypeStruct((bsz, t_total, HEADS * HD), F32)
    return pl.pallas_call(
        body, name="dn_pre_fwd", grid=(bsz, t_total // tt),
        in_specs=[pl.BlockSpec((1, tt, QKV_W), lambda b, t: (b, t, 0)),
                  pl.BlockSpec((1, HALO, QKV_W), lambda b, t: (*hp(b, t), 0)),
                  pl.BlockSpec((1, tt, 128), lambda b, t: (b, t, SM_BLK)),
                  pl.BlockSpec((DN_CONV_K, QKV_W), lambda b, t: (0, 0)), _vec_spec(128), _vec_spec(128)],
        out_specs=(_tok_spec(tt, 512), _tok_spec(tt, 512), _tok_spec(tt, 512), _tok_spec(tt, 128)),
        out_shape=(out512, out512, out512, jax.ShapeDtypeStruct((bsz, t_total, 128), F32)),
        compiler_params=_params(("parallel", "parallel")))(proj, proj, proj, conv_w, alog_row, dt_row)


def _dn_pre_bwd(proj, dq, dk, dv, dgates, conv_w, alog_row, dt_row):
    bsz, t_total, _ = proj.shape
    tt = _div_tile(t_total, 128)
    nt = t_total // tt
    hp, hn = _halo_prev(tt), _halo_next(tt, t_total)

    def body(x_ref, xp_ref, xn_ref, sm_ref, dq_ref, dqn_ref, dk_ref, dkn_ref, dv_ref, dvn_ref, dg_ref,
             w_ref, al_ref, dt_ref, dx_ref, dsm_ref, dw_ref, dal_ref, ddt_ref):
        b, t = pl.program_id(0), pl.program_id(1)

        @pl.when((b == 0) & (t == 0))
        def _():
            dw_ref[...] = jnp.zeros_like(dw_ref)
            dal_ref[...] = jnp.zeros_like(dal_ref)
            ddt_ref[...] = jnp.zeros_like(ddt_ref)

        prev = jnp.where(t == 0, 0.0, xp_ref[0])
        rows = jnp.concatenate([prev, x_ref[0], xn_ref[0]], axis=0)
        y = _conv_rows(rows, w_ref, DN_CONV_K)[HALO:]
        valid = (_iota((tt + HALO, 1), 0) < tt) | (t < nt - 1)

        def ext(tile_ref, next_ref):
            return jnp.where(valid, jnp.concatenate([tile_ref[0], next_ref[0]], axis=0), 0.0)

        _, vjp_qkv = jax.vjp(_dn_qkv, y)
        (dy,) = vjp_qkv((ext(dq_ref, dqn_ref), ext(dk_ref, dkn_ref), ext(dv_ref, dvn_ref)))
        dy = jnp.where(valid, dy, 0.0)
        dx = dy * w_ref[DN_CONV_K - 1:DN_CONV_K, :]
        for s in range(1, DN_CONV_K):
            dx = dx + _shift_up(dy, s) * w_ref[DN_CONV_K - 1 - s:DN_CONV_K - s, :]
        dx_ref[0] = dx[:tt].astype(MXU_DT)
        dy_t = dy[:tt]
        for k in range(DN_CONV_K):
            s = DN_CONV_K - 1 - k
            xs = (rows if s == 0 else pltpu.roll(rows, s, 0))[HALO:HALO + tt]
            dw_ref[k:k + 1, :] += jnp.sum(dy_t * xs, axis=0, keepdims=True)
        _, vjp_g = jax.vjp(_dn_gates, sm_ref[0], al_ref[...], dt_ref[...])
        dsm, dal, ddt = vjp_g(dg_ref[0])
        dsm_ref[0] = dsm
        dal_ref[...] += dal
        ddt_ref[...] += ddt

    def tile(width, blk=0):
        return pl.BlockSpec((1, tt, width), lambda b, t: (b, t, blk))

    def halo(h, width):
        return pl.BlockSpec((1, HALO, width), lambda b, t: (*h(b, t), 0))

    return pl.pallas_call(
        body, name="dn_pre_bwd", grid=(bsz, nt),
        in_specs=[tile(QKV_W), halo(hp, QKV_W), halo(hn, QKV_W), tile(128, SM_BLK),
                  tile(512), halo(hn, 512), tile(512), halo(hn, 512), tile(512), halo(hn, 512), tile(128),
                  pl.BlockSpec((DN_CONV_K, QKV_W), lambda b, t: (0, 0)), _vec_spec(128), _vec_spec(128)],
        out_specs=(tile(QKV_W), tile(128), pl.BlockSpec((DN_CONV_K, QKV_W), lambda b, t: (0, 0)),
                   _vec_spec(128), _vec_spec(128)),
        out_shape=(jax.ShapeDtypeStruct((bsz, t_total, QKV_W), MXU_DT), jax.ShapeDtypeStruct((bsz, t_total, 128), F32),
                   jax.ShapeDtypeStruct((DN_CONV_K, QKV_W), F32), jax.ShapeDtypeStruct((1, 128), F32),
                   jax.ShapeDtypeStruct((1, 128), F32)),
        compiler_params=_params(("arbitrary", "arbitrary")))(
            proj, proj, proj, proj, dq, dq, dk, dk, dv, dv, dgates, conv_w, alog_row, dt_row)


def _state_spec(bsz, idx):
    return pl.BlockSpec((bsz, 1, HEADS, HD, HD), lambda c: (0, idx(c), 0, 0, 0))


def _inv_spec(bsz, idx):
    return pl.BlockSpec((bsz, 1, HEADS, CHUNK, CHUNK), lambda c: (0, idx(c), 0, 0, 0))


def _chunk_spec(bsz, width, idx, blk=0):
    return pl.BlockSpec((bsz, CHUNK, width), lambda c: (0, idx(c), blk))


def _dn_rec_fwd(q, k, v, gates):
    bsz, t_total, _ = q.shape
    nc = t_total // CHUNK
    fwd = lambda c: c

    def body(q_ref, k_ref, v_ref, g_ref, o_ref, ss_ref, inv_ref, s_ref):
        @pl.when(pl.program_id(0) == 0)
        def _():
            s_ref[...] = jnp.zeros_like(s_ref)

        seqs = range(bsz)
        s_list = [[s_ref[b * HEADS + h] for h in range(HEADS)] for b in seqs]
        for b in seqs:
            for h in range(HEADS):
                ss_ref[b, 0, h] = s_list[b][h]
        o, new_s, invs = _dn_chunk(s_list, [q_ref[b] for b in seqs], [k_ref[b] for b in seqs],
                                   [v_ref[b] for b in seqs], [g_ref[b] for b in seqs], with_inv=True)
        for b in seqs:
            o_ref[b] = o[b]
            for h in range(HEADS):
                s_ref[b * HEADS + h] = new_s[b][h]
                inv_ref[b, 0, h] = invs[b * HEADS + h]

    return pl.pallas_call(
        body, name="dn_rec_fwd", grid=(nc,),
        in_specs=[_chunk_spec(bsz, 512, fwd)] * 3 + [_chunk_spec(bsz, 128, fwd)],
        out_specs=(_chunk_spec(bsz, 512, fwd), _state_spec(bsz, fwd), _inv_spec(bsz, fwd)),
        out_shape=(jax.ShapeDtypeStruct(q.shape, F32), jax.ShapeDtypeStruct((bsz, nc, HEADS, HD, HD), F32),
                   jax.ShapeDtypeStruct((bsz, nc, HEADS, CHUNK, CHUNK), F32)),
        scratch_shapes=[pltpu.VMEM((bsz * HEADS, HD, HD), F32)],
        compiler_params=_params(("arbitrary",)))(q, k, v, gates)


def _dn_rec_bwd(q, k, v, gates, states, invs, do):
    bsz, t_total, _ = q.shape
    nc = t_total // CHUNK
    rev = lambda c: nc - 1 - c

    def body(q_ref, k_ref, v_ref, g_ref, ss_ref, inv_ref, do_ref, dq_ref, dk_ref, dv_ref, dg_ref, ds_ref):
        @pl.when(pl.program_id(0) == 0)
        def _():
            ds_ref[...] = jnp.zeros_like(ds_ref)

        seqs = range(bsz)
        s_list = [[ss_ref[b, 0, h] for h in range(HEADS)] for b in seqs]
        known = [inv_ref[b, 0, h] for b in seqs for h in range(HEADS)]
        _, vjp = jax.vjp(functools.partial(_dn_chunk, inv_known=known),
                         s_list, [q_ref[b] for b in seqs], [k_ref[b] for b in seqs],
                         [v_ref[b] for b in seqs], [g_ref[b] for b in seqs])
        ds_in, dq, dk, dv, dg = vjp(([do_ref[b] for b in seqs],
                                     [[ds_ref[b * HEADS + h] for h in range(HEADS)] for b in seqs]))
        for b in seqs:
            dq_ref[b], dk_ref[b], dv_ref[b], dg_ref[b] = dq[b], dk[b], dv[b], dg[b]
            for h in range(HEADS):
                ds_ref[b * HEADS + h] = ds_in[b][h]

    tok = lambda width: _chunk_spec(bsz, width, rev)
    out512 = jax.ShapeDtypeStruct(q.shape, F32)
    return pl.pallas_call(
        body, name="dn_rec_bwd", grid=(nc,),
        in_specs=[tok(512), tok(512), tok(512), tok(128), _state_spec(bsz, rev), _inv_spec(bsz, rev), tok(512)],
        out_specs=(tok(512), tok(512), tok(512), tok(128)),
        out_shape=(out512, out512, out512, jax.ShapeDtypeStruct(gates.shape, F32)),
        scratch_shapes=[pltpu.VMEM((bsz * HEADS, HD, HD), F32)],
        compiler_params=_params(("arbitrary",)))(q, k, v, gates, states, invs, do)


GQ_BLK, GK_BLK, GV_BLK = P_GQ // 512, P_GK // 512, P_GV // 512


def _gla_rec_fwd(proj, w2, bg):
    bsz, t_total, _ = proj.shape
    nc = t_total // CHUNK

    fwd = lambda c: c

    def body(q_ref, k_ref, v_ref, sm_ref, w2_ref, bg_ref, o_ref, ss_ref, s_ref):
        @pl.when(pl.program_id(0) == 0)
        def _():
            s_ref[...] = jnp.zeros_like(s_ref)

        seqs = range(bsz)
        s_list = [[s_ref[b * HEADS + h] for h in range(HEADS)] for b in seqs]
        for b in seqs:
            for h in range(HEADS):
                ss_ref[b, 0, h] = s_list[b][h]
        o, new_s = _gla_chunk(s_list, [q_ref[b] for b in seqs], [k_ref[b] for b in seqs], [v_ref[b] for b in seqs],
                              [sm_ref[b] for b in seqs], w2_ref[...], bg_ref[...])
        for b in seqs:
            o_ref[b] = o[b]
            for h in range(HEADS):
                s_ref[b * HEADS + h] = new_s[b][h]

    col = lambda blk, width=512: _chunk_spec(bsz, width, fwd, blk)
    return pl.pallas_call(
        body, name="gla_rec_fwd", grid=(nc,),
        in_specs=[col(GQ_BLK), col(GK_BLK), col(GV_BLK), col(SM_BLK, 128),
                  pl.BlockSpec((128, 512), lambda c: (0, 0)), pl.BlockSpec((1, 512), lambda c: (0, 0))],
        out_specs=(col(0), _state_spec(bsz, fwd)),
        out_shape=(jax.ShapeDtypeStruct((bsz, t_total, 512), F32),
                   jax.ShapeDtypeStruct((bsz, nc, HEADS, HD, HD), F32)),
        scratch_shapes=[pltpu.VMEM((bsz * HEADS, HD, HD), F32)],
        compiler_params=_params(("arbitrary",)))(proj, proj, proj, proj, w2, bg)


def _gla_rec_bwd(proj, w2, bg, states, do, dsm_dn):
    bsz, t_total, _ = proj.shape
    nc = t_total // CHUNK
    rev = lambda c: nc - 1 - c

    def body(q_ref, k_ref, v_ref, sm_ref, w2_ref, bg_ref, ss_ref, do_ref, dsd_ref,
             dq_ref, dk_ref, dv_ref, dsm_ref, dw2_ref, dbg_ref, ds_ref):
        @pl.when(pl.program_id(0) == 0)
        def _():
            dw2_ref[...] = jnp.zeros_like(dw2_ref)
            dbg_ref[...] = jnp.zeros_like(dbg_ref)
            ds_ref[...] = jnp.zeros_like(ds_ref)

        seqs = range(bsz)
        s_list = [[ss_ref[b, 0, h] for h in range(HEADS)] for b in seqs]
        _, vjp = jax.vjp(_gla_chunk, s_list, [q_ref[b] for b in seqs], [k_ref[b] for b in seqs],
                         [v_ref[b] for b in seqs], [sm_ref[b] for b in seqs], w2_ref[...], bg_ref[...])
        ds_in, dq, dk, dv, dsm, dw2, dbg = vjp(([do_ref[b] for b in seqs],
                                                [[ds_ref[b * HEADS + h] for h in range(HEADS)] for b in seqs]))
        for b in seqs:
            dq_ref[b], dk_ref[b], dv_ref[b] = dq[b].astype(MXU_DT), dk[b].astype(MXU_DT), dv[b].astype(MXU_DT)
            dsm_ref[b] = (dsm[b] + dsd_ref[b]).astype(MXU_DT)
            for h in range(HEADS):
                ds_ref[b * HEADS + h] = ds_in[b][h]
        dw2_ref[...] += dw2
        dbg_ref[...] += dbg

    col = lambda blk, width=512: _chunk_spec(bsz, width, rev, blk)
    w2_spec = pl.BlockSpec((128, 512), lambda c: (0, 0))
    bg_spec = pl.BlockSpec((1, 512), lambda c: (0, 0))
    out512 = jax.ShapeDtypeStruct((bsz, t_total, 512), MXU_DT)
    return pl.pallas_call(
        body, name="gla_rec_bwd", grid=(nc,),
        in_specs=[col(GQ_BLK), col(GK_BLK), col(GV_BLK), col(SM_BLK, 128), w2_spec, bg_spec,
                  _state_spec(bsz, rev), col(0), col(0, 128)],
        out_specs=(col(0), col(0), col(0), col(0, 128), w2_spec, bg_spec),
        out_shape=(out512, out512, out512, jax.ShapeDtypeStruct((bsz, t_total, 128), MXU_DT),
                   jax.ShapeDtypeStruct((128, 512), F32), jax.ShapeDtypeStruct((1, 512), F32)),
        scratch_shapes=[pltpu.VMEM((bsz * HEADS, HD, HD), F32)],
        compiler_params=_params(("arbitrary",)))(proj, proj, proj, proj, w2, bg, states, do, dsm_dn)


Z_BLK, GG_BLK = P_Z // 512, P_GG // 512


def _mix_out_fwd(o_dn, o_gla, proj, grow_dn, grow_gla):
    bsz, t_total, _ = o_dn.shape
    tt = _div_tile(t_total, 256)

    def body(od_ref, og_ref, z_ref, gg_ref, gd_ref, gl_ref, o_ref):
        o_ref[0, :, :512] = _gate_norm(od_ref[0], z_ref[0], gd_ref[...]).astype(MXU_DT)
        o_ref[0, :, 512:] = _gate_norm(og_ref[0], gg_ref[0], gl_ref[...]).astype(MXU_DT)

    def col(blk):
        return pl.BlockSpec((1, tt, 512), lambda b, t: (b, t, blk))

    return pl.pallas_call(
        body, name="mix_out_fwd", grid=(bsz, t_total // tt),
        in_specs=[col(0), col(0), col(Z_BLK), col(GG_BLK), _vec_spec(512), _vec_spec(512)],
        out_specs=_tok_spec(tt), out_shape=jax.ShapeDtypeStruct((bsz, t_total, D), MXU_DT),
        compiler_params=_params(("parallel", "parallel")))(o_dn, o_gla, proj, proj, grow_dn, grow_gla)


def _mix_out_bwd(do, o_dn, o_gla, proj, grow_dn, grow_gla):
    bsz, t_total, _ = o_dn.shape
    tt = _div_tile(t_total, 256)

    def body(do_ref, od_ref, og_ref, z_ref, gg_ref, gd_ref, gl_ref,
             dod_ref, dog_ref, dz_ref, dgg_ref, dgd_ref, dgl_ref):
        @pl.when((pl.program_id(0) == 0) & (pl.program_id(1) == 0))
        def _():
            dgd_ref[...] = jnp.zeros_like(dgd_ref)
            dgl_ref[...] = jnp.zeros_like(dgl_ref)

        def one(o_ref, gate_ref, g_ref, ct, do_out, dgate_out, dg_out):
            _, vjp = jax.vjp(_gate_norm, o_ref[0], gate_ref[0], g_ref[...])
            d_o, d_gate, d_row = vjp(ct)
            do_out[0] = d_o
            dgate_out[0] = d_gate.astype(MXU_DT)
            acc = d_row[:, :HD]
            for h in range(1, HEADS):
                acc = acc + d_row[:, h * HD:(h + 1) * HD]
            dg_out[...] += acc

        ct = do_ref[0]
        one(od_ref, z_ref, gd_ref, ct[:, :512], dod_ref, dz_ref, dgd_ref)
        one(og_ref, gg_ref, gl_ref, ct[:, 512:], dog_ref, dgg_ref, dgl_ref)

    def col(blk):
        return pl.BlockSpec((1, tt, 512), lambda b, t: (b, t, blk))

    f512 = jax.ShapeDtypeStruct((bsz, t_total, 512), F32)
    b512 = jax.ShapeDtypeStruct((bsz, t_total, 512), MXU_DT)
    g128 = jax.ShapeDtypeStruct((1, HD), F32)
    return pl.pallas_call(
        body, name="mix_out_bwd", grid=(bsz, t_total // tt),
        in_specs=[_tok_spec(tt), col(0), col(0), col(Z_BLK), col(GG_BLK), _vec_spec(512), _vec_spec(512)],
        out_specs=(col(0), col(0), col(0), col(0), _vec_spec(HD), _vec_spec(HD)),
        out_shape=(f512, f512, b512, b512, g128, g128),
        compiler_params=_params(("arbitrary", "arbitrary")))(do, o_dn, o_gla, proj, proj, grow_dn, grow_gla)


def _sum_slots(x, name):
    n, rows, cols = x.shape
    tr = _div_tile(rows, max(8, (1 << 19) // cols))

    def body(x_ref, o_ref):
        acc = x_ref[0].astype(F32)
        for i in range(1, n):
            acc = acc + x_ref[i].astype(F32)
        o_ref[...] = acc

    return pl.pallas_call(
        body, name=name, grid=(rows // tr,),
        in_specs=[pl.BlockSpec((n, tr, cols), lambda i: (0, i, 0))],
        out_specs=pl.BlockSpec((tr, cols), lambda i: (i, 0)),
        out_shape=jax.ShapeDtypeStruct((rows, cols), F32), compiler_params=_params(("parallel",)))(x)


def _pair_add(g8, r1, core, name):
    _, rows, cols = g8.shape
    tr = _div_tile(rows, max(8, (1 << 19) // cols))
    g42 = g8.reshape(4, 2, rows, cols)

    def body(core_ref, g_ref, r_ref, o_ref):
        o_ref[0] = (g_ref[0, 0].astype(F32) + r_ref[0].astype(F32)).astype(o_ref.dtype)

    return pl.pallas_call(
        body, name=name,
        grid_spec=pltpu.PrefetchScalarGridSpec(
            num_scalar_prefetch=1, grid=(4, rows // tr),
            in_specs=[pl.BlockSpec((1, 1, tr, cols), lambda s, i, core_ref: (s, core_ref[0], i, 0)),
                      pl.BlockSpec((1, tr, cols), lambda s, i, core_ref: (s, i, 0))],
            out_specs=pl.BlockSpec((1, tr, cols), lambda s, i, core_ref: (s, i, 0))),
        out_shape=jax.ShapeDtypeStruct((4, rows, cols), g8.dtype),
        compiler_params=_params(("parallel", "parallel")))(core, g42, r1)


def _chip_add(p4, r2, chip, name):
    _, rows, cols = p4.shape
    tr = _div_tile(rows, max(8, (1 << 19) // cols))

    def body(chip_ref, p_ref, r_ref, o_ref):
        f = lambda a: a.astype(F32)
        o_ref[...] = ((f(p_ref[0]) + f(r_ref[0])) + f(r_ref[1])) + f(r_ref[2])

    return pl.pallas_call(
        body, name=name,
        grid_spec=pltpu.PrefetchScalarGridSpec(
            num_scalar_prefetch=1, grid=(rows // tr,),
            in_specs=[pl.BlockSpec((1, tr, cols), lambda i, chip_ref: (chip_ref[0], i, 0)),
                      pl.BlockSpec((3, tr, cols), lambda i, chip_ref: (0, i, 0))],
            out_specs=pl.BlockSpec((tr, cols), lambda i, chip_ref: (i, 0))),
        out_shape=jax.ShapeDtypeStruct((rows, cols), F32),
        compiler_params=_params(("parallel",)))(chip, p4, r2)


def _adamw(w, g, m, v, name):
    rows, cols = w.shape
    tr = _div_tile(rows, max(8, (1 << 18) // cols))

    def body(w_ref, g_ref, m_ref, v_ref, d_ref, nm_ref, nv_ref):
        g_ = g_ref[...]
        nm = ADAM_B1 * m_ref[...] + (1.0 - ADAM_B1) * g_
        nv = ADAM_B2 * v_ref[...] + (1.0 - ADAM_B2) * (g_ * g_)
        m_hat = nm / (1.0 - ADAM_B1 ** ADAM_STEP)
        v_hat = nv / (1.0 - ADAM_B2 ** ADAM_STEP)
        d_ref[...] = -ADAM_LR * (m_hat / (jnp.sqrt(v_hat) + ADAM_EPS) + ADAM_WD * w_ref[...])
        nm_ref[...] = nm
        nv_ref[...] = nv

    spec = pl.BlockSpec((tr, cols), lambda i: (i, 0))
    shp = jax.ShapeDtypeStruct((rows, cols), F32)
    return pl.pallas_call(body, name=name, grid=(rows // tr,), in_specs=[spec] * 4, out_specs=(spec,) * 3,
                          out_shape=(shp,) * 3, compiler_params=_params(("parallel",)))(w, g, m, v)


def _position():
    return lax.axis_index("x"), lax.axis_index("y"), lax.axis_index("c")


def _slot(px, py, pc):
    return 4 * px + 2 * py + pc


def _gather_small(x, name):
    rows, cols = x.shape

    def body(x_ref, o_ref, send_sems, recv_sems):
        mx, my, mc = _position()

        def peer(k):
            return (mx ^ ((k >> 2) & 1), my ^ ((k >> 1) & 1), mc ^ (k & 1))

        o_ref[_slot(mx, my, mc)] = x_ref[...]
        sends = []
        for k in range(1, N_DEV):
            cp = pltpu.make_async_remote_copy(src_ref=x_ref, dst_ref=o_ref.at[_slot(mx, my, mc)],
                                              send_sem=send_sems.at[k - 1], recv_sem=recv_sems.at[k - 1],
                                              device_id=peer(k), device_id_type=MESH)
            cp.start()
            sends.append(cp)
        for k in range(1, N_DEV):
            pltpu.make_async_remote_copy(src_ref=x_ref, dst_ref=o_ref.at[_slot(*peer(k))],
                                         send_sem=send_sems.at[k - 1], recv_sem=recv_sems.at[k - 1],
                                         device_id=peer(k), device_id_type=MESH).wait_recv()
        for cp in sends:
            cp.wait_send()

    return pl.pallas_call(
        body, name=name, out_shape=jax.ShapeDtypeStruct((N_DEV, rows, cols), x.dtype),
        in_specs=[pl.BlockSpec(memory_space=pltpu.VMEM)], out_specs=pl.BlockSpec(memory_space=pltpu.VMEM),
        scratch_shapes=[pltpu.SemaphoreType.DMA((N_DEV - 1,)), pltpu.SemaphoreType.DMA((N_DEV - 1,))],
        compiler_params=pltpu.CompilerParams(vmem_limit_bytes=VMEM_LIMIT_V7X))(x)


def _gather_big(shards):
    n = len(shards)

    def body(*refs):
        xs, outs = refs[:n], refs[n:2 * n]
        send_sems, recv_sems, local_sems = refs[2 * n:]
        mx, my, mc = _position()
        me, sibling = (mx, my, mc), (mx, my, 1 - mc)
        chips = [(1 - mx, my), (mx, 1 - my), (1 - mx, 1 - my)]

        def copy(a, k, block, to, src=None):
            dst = outs[a].at[_slot(*block)]
            return pltpu.make_async_remote_copy(src_ref=dst if src is None else src, dst_ref=dst,
                                                send_sem=send_sems.at[7 * a + k], recv_sem=recv_sems.at[7 * a + k],
                                                device_id=to, device_id_type=MESH)

        mine = [pltpu.make_async_copy(xs[a], outs[a].at[_slot(*me)], local_sems.at[a]) for a in range(n)]
        for cp in mine:
            cp.start()
        started = []
        for a in range(n):
            started.append(copy(a, 0, me, sibling, src=xs[a]))
            started += [copy(a, 1 + j, me, (*chip, mc), src=xs[a]) for j, chip in enumerate(chips)]
        for cp in started:
            cp.start()
        for j, chip in enumerate(chips):
            for a in range(n):
                copy(a, 1 + j, (*chip, mc), me).wait_recv()
                fwd = copy(a, 4 + j, (*chip, mc), sibling)
                fwd.start()
                started.append(fwd)
        for a in range(n):
            copy(a, 0, sibling, me).wait_recv()
            for j, chip in enumerate(chips):
                copy(a, 4 + j, (*chip, 1 - mc), me).wait_recv()
        for cp in started:
            cp.wait_send()
        for cp in mine:
            cp.wait()

    any_spec = pl.BlockSpec(memory_space=pl.ANY)
    return pl.pallas_call(
        body, name="gather_weights",
        out_shape=tuple(jax.ShapeDtypeStruct((N_DEV,) + s.shape, s.dtype) for s in shards),
        in_specs=[any_spec] * n, out_specs=(any_spec,) * n,
        scratch_shapes=[pltpu.SemaphoreType.DMA((7 * n,)), pltpu.SemaphoreType.DMA((7 * n,)),
                        pltpu.SemaphoreType.DMA((n,))])(*shards)


def _scatter_sibling(grads):
    n = len(grads)

    def body(*refs):
        gs, outs = refs[:n], refs[n:2 * n]
        send_sems, recv_sems = refs[2 * n:]
        mx, my, mc = _position()
        copies = []
        for a in range(n):
            for s in range(4):
                copies.append(pltpu.make_async_remote_copy(
                    src_ref=gs[a].at[2 * s + (1 - mc)], dst_ref=outs[a].at[s],
                    send_sem=send_sems.at[4 * a + s], recv_sem=recv_sems.at[4 * a + s],
                    device_id=(mx, my, 1 - mc), device_id_type=MESH))
        for cp in copies:
            cp.start()
        for cp in copies:
            cp.wait_recv()
        for cp in copies:
            cp.wait_send()

    any_spec = pl.BlockSpec(memory_space=pl.ANY)
    return pl.pallas_call(
        body, name="scatter_sibling",
        out_shape=tuple(jax.ShapeDtypeStruct((4,) + g.shape[1:], g.dtype) for g in grads),
        in_specs=[any_spec] * n, out_specs=(any_spec,) * n,
        scratch_shapes=[pltpu.SemaphoreType.DMA((4 * n,)), pltpu.SemaphoreType.DMA((4 * n,))])(*grads)


def _scatter_chips(sums):
    n = len(sums)

    def body(*refs):
        ps, outs = refs[:n], refs[n:2 * n]
        send_sems, recv_sems = refs[2 * n:]
        mx, my, mc = _position()
        chips = [(1 - mx, my), (mx, 1 - my), (1 - mx, 1 - my)]
        copies = []
        for a in range(n):
            for k, (cx, cy) in enumerate(chips):
                copies.append(pltpu.make_async_remote_copy(
                    src_ref=ps[a].at[2 * cx + cy], dst_ref=outs[a].at[k],
                    send_sem=send_sems.at[3 * a + k], recv_sem=recv_sems.at[3 * a + k],
                    device_id=(cx, cy, mc), device_id_type=MESH))
        for cp in copies:
            cp.start()
        for cp in copies:
            cp.wait_recv()
        for cp in copies:
            cp.wait_send()

    any_spec = pl.BlockSpec(memory_space=pl.ANY)
    return pl.pallas_call(
        body, name="scatter_chips",
        out_shape=tuple(jax.ShapeDtypeStruct((3,) + p.shape[1:], p.dtype) for p in sums),
        in_specs=[any_spec] * n, out_specs=(any_spec,) * n,
        scratch_shapes=[pltpu.SemaphoreType.DMA((3 * n,)), pltpu.SemaphoreType.DMA((3 * n,))])(*sums)


def _peer(pos, k):
    mx, my, mc = pos
    return (mx ^ ((k >> 2) & 1), my ^ ((k >> 1) & 1), mc ^ (k & 1))


def _exchange_copies(srcs, lands, send_sems, recv_sems, by_owner):
    pos = _position()
    me = _slot(*pos)
    out = []
    for a, (src, land) in enumerate(zip(srcs, lands)):
        for k in range(1, N_DEV):
            peer = _peer(pos, k)
            sems = dict(send_sem=send_sems.at[7 * a + k - 1], recv_sem=recv_sems.at[7 * a + k - 1],
                        device_id=peer, device_id_type=MESH)
            mine = src.at[_slot(*peer)] if by_owner else src
            send = pltpu.make_async_remote_copy(src_ref=mine, dst_ref=land.at[me], **sems)
            recv = pltpu.make_async_remote_copy(src_ref=mine, dst_ref=land.at[_slot(*peer)], **sems)
            out.append((send, recv))
    return out


_HBM_SPEC = pl.BlockSpec(memory_space=pltpu.HBM)
_SEM_SPEC = pl.BlockSpec(memory_space=pltpu.SEMAPHORE)
_DATAFLOW = pltpu.SideEffectType.DATAFLOW_SIDE_EFFECTING


def _exchange_start(name, srcs, slab_shapes, after, by_owner):
    n = len(srcs)
    lands = [pltpu.with_memory_space_constraint(lax.empty((N_DEV,) + s, x.dtype), pltpu.HBM)
             for s, x in zip(slab_shapes, srcs)]
    srcs = [pltpu.with_memory_space_constraint(x, pltpu.HBM) for x in srcs]

    def body(*refs):
        src_refs, land_refs = refs[:n], refs[n:2 * n]
        send_sems, recv_sems = refs[2 * n + 1], refs[2 * n + 2]
        token = refs[-1]
        for send, _ in _exchange_copies(src_refs, land_refs, send_sems, recv_sems, by_owner):
            send.start()
        token[...] = jnp.zeros_like(token)

    outs = pl.pallas_call(
        body, name=name,
        out_shape=(pltpu.SemaphoreType.DMA((7 * n,)), pltpu.SemaphoreType.DMA((7 * n,)),
                   *[pltpu.HBM(x.shape, x.dtype) for x in srcs], *[pltpu.HBM(l.shape, l.dtype) for l in lands],
                   jax.ShapeDtypeStruct((8, 128), F32)),
        in_specs=[_HBM_SPEC] * (2 * n) + [pl.BlockSpec(memory_space=pl.ANY)],
        out_specs=(_SEM_SPEC, _SEM_SPEC, *[_HBM_SPEC] * (2 * n), pl.BlockSpec(memory_space=pltpu.VMEM)),
        input_output_aliases={i: 2 + i for i in range(2 * n)},
        compiler_params=pltpu.CompilerParams(has_side_effects=_DATAFLOW))(*srcs, *lands, after)
    return outs[0], outs[1], list(outs[2:2 + n]), list(outs[2 + n:2 + 2 * n]), outs[-1]


def _exchange_wait(name, send_sems, recv_sems, srcs, lands, after, by_owner):
    n = len(srcs)

    def body(*refs):
        src_refs, land_refs = refs[:n], refs[n:2 * n]
        s_sems, r_sems = refs[2 * n], refs[2 * n + 1]
        for send, recv in _exchange_copies(src_refs, land_refs, s_sems, r_sems, by_owner):
            send.wait_send()
            recv.wait_recv()

    outs = pl.pallas_call(
        body, name=name,
        out_shape=(*[pltpu.HBM(x.shape, x.dtype) for x in srcs], *[pltpu.HBM(l.shape, l.dtype) for l in lands]),
        in_specs=[_HBM_SPEC] * (2 * n) + [_SEM_SPEC, _SEM_SPEC, pl.BlockSpec(memory_space=pl.ANY)],
        out_specs=tuple([_HBM_SPEC] * (2 * n)),
        input_output_aliases={i: i for i in range(2 * n)},
        compiler_params=pltpu.CompilerParams(has_side_effects=_DATAFLOW))(*srcs, *lands, send_sems, recv_sems, after)
    return list(outs[n:])


def _pad_heads(x, axis):
    shp = list(x.shape)
    x4 = x.reshape(shp[:axis] + [HEADS, GLA_KEY] + shp[axis + 1:])
    pad = [(0, 0)] * x4.ndim
    pad[axis + 1] = (0, HD - GLA_KEY)
    return jnp.pad(x4, pad).reshape(shp[:axis] + [HEADS * HD] + shp[axis + 1:])


def _unpad_heads(x, axis):
    shp = list(x.shape)
    x4 = x.reshape(shp[:axis] + [HEADS, HD] + shp[axis + 1:])
    x4 = lax.slice_in_dim(x4, 0, GLA_KEY, axis=axis + 1)
    return x4.reshape(shp[:axis] + [HEADS * GLA_KEY] + shp[axis + 1:])


O_Z_END, O_AB, O_GQ, O_GK, O_GV, O_R = 2048, 2048, 2056, 2312, 2568, 3592


def _pad_in_rows(wt):
    return jnp.concatenate([
        wt[:O_Z_END], _pad_heads(wt[O_GQ:O_GK], 0), _pad_heads(wt[O_GK:O_GV], 0), wt[O_GV:O_R],
        wt[O_AB:O_GQ], wt[O_R:], jnp.zeros((P_W - P_SM - 8 - GATE_RANK, wt.shape[1]), wt.dtype)], axis=0)


def _unpad_in_rows(gt):
    return jnp.concatenate([
        gt[:P_GQ], gt[P_SM:P_SM + 8], _unpad_heads(gt[P_GQ:P_GK], 0), _unpad_heads(gt[P_GK:P_GV], 0),
        gt[P_GV:P_SM], gt[P_SM + 8:P_SM + 8 + GATE_RANK]], axis=0)


def _lane_row(vals, width=128):
    return jnp.pad(vals.reshape(1, -1), ((0, 0), (0, width - vals.size)))


SMALL_NAMES = ["ln0_g", "ln0_b", "b_ada", "dn_conv", "dn_a_log", "dn_dt_bias", "dn_norm_g", "gla_w_gate2",
               "gla_b_gate", "gla_norm_g", "ln1_g", "ln1_b", "ffn_conv", "ffn_conv_b", "ln2_g", "ln2_b"]
WEIGHTS = ["ln0_g", "ln0_b", "w_ada", "b_ada", "w_in", "dn_conv", "dn_a_log", "dn_dt_bias", "dn_norm_g",
           "gla_w_gate2", "gla_b_gate", "gla_norm_g", "w_o", "ln1_g", "ln1_b", "ffn_w_up", "ffn_conv", "ffn_conv_b",
           "ffn_w_down", "ln2_g", "ln2_b"]


def kernel(x, c, ln0_g, ln0_b, w_ada, b_ada, w_in, dn_conv, dn_a_log, dn_dt_bias, dn_norm_g, gla_w_gate2, gla_b_gate, gla_norm_g, w_o, ln1_g, ln1_b, ffn_w_up, ffn_conv, ffn_conv_b, ffn_w_down, ln2_g, ln2_b, loss_target, m_ln0_g, m_ln0_b, m_w_ada, m_b_ada, m_w_in, m_dn_conv, m_dn_a_log, m_dn_dt_bias, m_dn_norm_g, m_gla_w_gate2, m_gla_b_gate, m_gla_norm_g, m_w_o, m_ln1_g, m_ln1_b, m_ffn_w_up, m_ffn_conv, m_ffn_conv_b, m_ffn_w_down, m_ln2_g, m_ln2_b, v_ln0_g, v_ln0_b, v_w_ada, v_b_ada, v_w_in, v_dn_conv, v_dn_a_log, v_dn_dt_bias, v_dn_norm_g, v_gla_w_gate2, v_gla_b_gate, v_gla_norm_g, v_w_o, v_ln1_g, v_ln1_b, v_ffn_w_up, v_ffn_conv, v_ffn_conv_b, v_ffn_w_down, v_ln2_g, v_ln2_b):
    args = dict(locals())
    w_given = {n: args[n] for n in WEIGHTS}
    m_given = {n: args["m_" + n] for n in WEIGHTS}
    v_given = {n: args["v_" + n] for n in WEIGHTS}
    bsz, t_total, _ = x.shape
    ntok = bsz * t_total
    mx, my, mc = _position()
    me = _slot(mx, my, mc)

    pack1 = jnp.concatenate([c.reshape(-1), dn_conv.reshape(-1), gla_w_gate2.reshape(-1), ffn_conv.reshape(-1)])
    n1 = pack1.size
    rows1 = -(-n1 // 1024) * 8
    pack1 = jnp.pad(pack1, (0, rows1 * 128 - n1)).reshape(rows1, 128)
    got1 = _gather_small(pack1, "gather_cond").reshape(N_DEV, -1)
    o1 = bsz * D
    o2 = o1 + dn_conv.size
    o3 = o2 + gla_w_gate2.size
    c_all = got1[:, :o1].reshape(N_DEV * bsz, D)
    dn_conv_f = got1[:, o1:o2].reshape(N_DEV, DN_CONV_K, -1).transpose(1, 0, 2).reshape(DN_CONV_K, QKV_W)
    gate2_f = got1[:, o2:o3].reshape(N_DEV, GATE_RANK, -1).transpose(1, 0, 2).reshape(GATE_RANK, HEADS * GLA_KEY)
    ffn_conv_f = got1[:, o3:n1].reshape(N_DEV, FFN_CONV_K, -1).transpose(1, 0, 2).reshape(FFN_CONV_K, 2 * D_FF)

    win_t = w_in[0].T.astype(MXU_DT)
    wup_t = ffn_w_up[0].T.astype(MXU_DT)
    (win_all,) = _gather_big([win_t])
    win_p = _pad_in_rows(win_all.reshape(IN_W, D))
    late = [w_o[0].astype(MXU_DT), wup_t, ffn_w_down[0].astype(MXU_DT)]
    ag_send, ag_recv, ag_src, ag_land, ag_token = _exchange_start(
        "gather_start", late, [w.shape for w in late], win_all, by_owner=False)
    c_all = c_all + ag_token[0, 0]
    cw_p, cb_p = _ffn_pair(ffn_conv_f, 1), _ffn_pair(ffn_conv_b, 1)

    ncol = w_ada.shape[2]
    b_cols = lax.dynamic_slice_in_dim(b_ada, me * ncol, ncol, axis=1)
    mod_part = _ada_fwd(c_all, w_ada[0], b_cols)
    mod_all = _gather_small(mod_part.reshape(-1, 128), "gather_mod").reshape(N_DEV, N_DEV * bsz, ncol)
    mod = lax.dynamic_slice_in_dim(mod_all, me * bsz, bsz, axis=1).transpose(1, 0, 2).reshape(bsz, 6, 1, D)
    sh_a, sc_a, gt_a, sh_f, sc_f, gt_f = (mod[:, i] for i in range(6))

    g0, b0 = ln0_g.reshape(1, D), ln0_b.reshape(1, D)
    alog_row, dt_row = _lane_row(dn_a_log[0]), _lane_row(dn_dt_bias[0])
    grow_dn, grow_gla = jnp.tile(dn_norm_g, (1, HEADS)), jnp.tile(gla_norm_g, (1, HEADS))
    w2 = jnp.zeros((128, HEADS * HD), F32).at[SM_R:SM_R + GATE_RANK].set(_pad_heads(gate2_f, 1))
    bg = _pad_heads(gla_b_gate, 1)

    h_a = _ln0_mod(x, g0, b0, sc_a, sh_a)
    proj = _mm(h_a.reshape(ntok, D), win_p, "nt", F32, "mm_proj", tm=1024, tn=1408).reshape(bsz, t_total, P_W)
    q, k, v, gates = _dn_pre_fwd(proj, dn_conv_f, alog_row, dt_row)
    o_dn, s_dn, inv_dn = _dn_rec_fwd(q, k, v, gates)
    o_gla, s_gla = _gla_rec_fwd(proj, w2, bg)
    o_mix = _mix_out_fwd(o_dn, o_gla, proj, grow_dn, grow_gla)
    landed = _exchange_wait("gather_wait", ag_send, ag_recv, ag_src, ag_land, o_mix, by_owner=False)
    wo_all, wup_all, wdn_all = (lax.dynamic_update_slice(l, w[None], (me, 0, 0)) for l, w in zip(landed, late))
    wo_f = wo_all.reshape(D, D)
    wup_f = _ffn_pair(wup_all.reshape(2 * D_FF, D), 0)
    wdn_f = wdn_all.reshape(D_FF, D)
    y = _mm(o_mix.reshape(ntok, D), wo_f, "nn", F32, "mm_wo", tm=1024, tn=1024).reshape(bsz, t_total, D)
    r1, h_f = _res_ln_mod(x, y, gt_a, g0, b0, ln1_g, ln1_b, sc_f, sh_f)
    up = _mm(h_f.reshape(ntok, D), wup_f, "nt", F32, "mm_up", tm=1024, tn=1408).reshape(bsz, t_total, 2 * D_FF)
    act = _ffn_act_fwd(up, cw_p, cb_p)
    y2 = _mm(act.reshape(ntok, D_FF), wdn_f, "nn", F32, "mm_down", tm=1024, tn=1024).reshape(bsz, t_total, D)
    loss_rows, dr2, dy2, dgt_f, d_ln2_g, d_ln2_b = _final_fwd_bwd(r1, y2, gt_f, ln1_g, ln1_b, ln2_g, ln2_b, loss_target)
    loss = lax.psum(0.5 * jnp.sum(loss_rows) / D, ("x", "y", "c"))

    dy2_2 = dy2.reshape(ntok, D)
    dact = _mm(dy2_2, wdn_f, "nt", F32, "mm_dact", tm=1024, tn=1408).reshape(bsz, t_total, D_FF)
    g_wdn = _mm(act.reshape(ntok, D_FF), dy2_2, "tn", MXU_DT, "mm_gwdn", tm=1408, tn=1024)
    dup, d_cw_p, d_cb_p = _ffn_act_bwd(up, dact, cw_p, cb_p)
    d_ffn_conv, d_ffn_conv_b = _ffn_unpair(d_cw_p, 1), _ffn_unpair(d_cb_p, 1)
    dup_2 = dup.reshape(ntok, 2 * D_FF)
    dh_f = _mm(dup_2, wup_f, "nn", F32, "mm_dhf", tn=1024).reshape(bsz, t_total, D)
    g_wup_t = _mm(dup_2, h_f.reshape(ntok, D), "tn", MXU_DT, "mm_gwup", tm=1408, tn=1024)
    ffn_parts = [_ffn_unpair(g_wup_t, 0).reshape(N_DEV, -1, D), g_wdn.reshape(N_DEV, -1, D)]
    rs_send, rs_recv, rs_src, rs_land, rs_token = _exchange_start(
        "scatter_start", ffn_parts, [p.shape[1:] for p in ffn_parts], dh_f, by_owner=True)
    dr1, dsc_f, dsh_f, d_ln1_g, d_ln1_b, dy, dgt_a = _ln_bwd_call(
        "ln1_bwd", dr2, dh_f, r1, ln1_g, ln1_b, sc_f + rs_token[0, 0], y=y, gt=gt_a)

    dy_2 = dy.reshape(ntok, D)
    do = _mm(dy_2, wo_f, "nt", F32, "mm_do", tm=1024, tn=1024).reshape(bsz, t_total, D)
    g_wo = _mm(o_mix.reshape(ntok, D), dy_2, "tn", MXU_DT, "mm_gwo", tm=512, tn=1024)
    do_dn, do_gla, dz, dgg, d_dn_norm, d_gla_norm = _mix_out_bwd(do, o_dn, o_gla, proj, grow_dn, grow_gla)
    dq, dk, dv, dgates = _dn_rec_bwd(q, k, v, gates, s_dn, inv_dn, do_dn)
    dqkv, dsm_dn, d_dn_conv, d_alog_row, d_dt_row = _dn_pre_bwd(proj, dq, dk, dv, dgates, dn_conv_f, alog_row, dt_row)
    dgq, dgk, dgv, dsm, d_w2, d_bg = _gla_rec_bwd(proj, w2, bg, s_gla, do_gla, dsm_dn)
    dproj = jnp.concatenate([dqkv, dz, dgq, dgk, dgv, dgg, dsm], axis=-1).reshape(ntok, P_W)
    dh_a = _mm(dproj, win_p, "nn", F32, "mm_dha", tn=1024).reshape(bsz, t_total, D)
    g_win_p = _mm(dproj, h_a.reshape(ntok, D), "tn", MXU_DT, "mm_gwin", tm=1408, tn=1024)
    grad_x, dsc_a, dsh_a, d_ln0_g, d_ln0_b = _ln_bwd_call("ln0_bwd", dr1, dh_a, x, g0, b0, sc_a)

    big = [_unpad_in_rows(g_win_p).reshape(N_DEV, -1, D), g_wo.reshape(N_DEV, -1, D)]
    from_sibling = _scatter_sibling(big)
    core = mc.reshape(1).astype(jnp.int32)
    chip_sums = [_pair_add(g8, r1_, core, f"pair_add_{i}") for i, (g8, r1_) in enumerate(zip(big, from_sibling))]
    from_chips = _scatter_chips(chip_sums)
    chip = (2 * mx + my).reshape(1).astype(jnp.int32)
    g_win_t, g_wo_s = (
        _chip_add(p4, r2_, chip, f"chip_add_{i}") for i, (p4, r2_) in enumerate(zip(chip_sums, from_chips)))
    ffn_landed = _exchange_wait("scatter_wait", rs_send, rs_recv, rs_src, rs_land, grad_x, by_owner=True)
    g_wup_ts, g_wdn_s = (
        _sum_slots(lax.dynamic_update_slice(l, lax.dynamic_slice_in_dim(p, me, 1, axis=0), (me, 0, 0)), f"sum_ffn_{i}")
        for i, (l, p) in enumerate(zip(ffn_landed, ffn_parts)))

    dmod = jnp.concatenate([dsh_a, dsc_a, dgt_a, dsh_f, dsc_f, dgt_f], axis=1).reshape(-1)
    small_parts = {
        "ln0_g": d_ln0_g, "ln0_b": d_ln0_b, "ln1_g": d_ln1_g, "ln1_b": d_ln1_b, "ln2_g": d_ln2_g, "ln2_b": d_ln2_b,
        "dn_a_log": d_alog_row[:, :HEADS], "dn_dt_bias": d_dt_row[:, :HEADS],
        "dn_norm_g": d_dn_norm, "gla_norm_g": d_gla_norm, "gla_b_gate": _unpad_heads(d_bg, 1),
        "ffn_conv_b": d_ffn_conv_b, "dn_conv": d_dn_conv,
        "gla_w_gate2": _unpad_heads(d_w2[SM_R:SM_R + GATE_RANK], 1), "ffn_conv": d_ffn_conv}
    order = sorted(small_parts)
    flat = jnp.concatenate([small_parts[n].reshape(-1) for n in order] + [dmod])
    n3 = flat.size
    rows3 = -(-n3 // 1024) * 8
    pack3 = jnp.pad(flat, (0, rows3 * 128 - n3)).reshape(rows3, 128)
    got3 = _gather_small(pack3, "gather_small_grads")
    tot3 = _sum_slots(got3, "sum_small_grads").reshape(-1)
    grads = {}
    off = 0
    for n in order:
        size = small_parts[n].size
        grads[n] = tot3[off:off + size]
        off += size
    dmod_all = got3.reshape(N_DEV, -1)[:, off:off + dmod.size].reshape(N_DEV * bsz, 6 * D)
    dmod_cols = lax.dynamic_slice_in_dim(dmod_all, me * ncol, ncol, axis=1)
    g_wada, g_bada = _ada_bwd(c_all, dmod_all, dmod_cols)
    grads["b_ada"] = g_bada

    def col_shard(full, rows):
        part = full.reshape(rows, -1)
        width = part.shape[1] // N_DEV
        return lax.dynamic_slice_in_dim(part, me * width, width, axis=1)

    grads["dn_conv"] = col_shard(grads["dn_conv"], DN_CONV_K)
    grads["gla_w_gate2"] = col_shard(grads["gla_w_gate2"], GATE_RANK)
    grads["ffn_conv"] = col_shard(grads["ffn_conv"], FFN_CONV_K)
    grads = {n: g.reshape(w_given[n].shape) for n, g in grads.items()}
    grads["w_ada"] = g_wada.reshape(w_ada.shape)
    grads["w_in"] = g_win_t.T.reshape(w_in.shape)
    grads["w_o"] = g_wo_s.reshape(w_o.shape)
    grads["ffn_w_up"] = g_wup_ts.T.reshape(ffn_w_up.shape)
    grads["ffn_w_down"] = g_wdn_s.reshape(ffn_w_down.shape)

    delta, new_m, new_v = {}, {}, {}
    for n in ["w_ada", "w_in", "w_o", "ffn_w_up", "ffn_w_down"]:
        shp = w_given[n].shape
        two_d = lambda a: a.reshape(shp[-2], shp[-1])
        d_, m_, v_ = _adamw(two_d(w_given[n]), two_d(grads[n]), two_d(m_given[n]), two_d(v_given[n]), "adamw_" + n)
        delta[n], new_m[n], new_v[n] = d_.reshape(shp), m_.reshape(shp), v_.reshape(shp)

    def pack_small(src):
        flat_ = jnp.concatenate([src[n].reshape(-1) for n in SMALL_NAMES])
        rows_ = -(-flat_.size // 1024) * 8
        return jnp.pad(flat_, (0, rows_ * 128 - flat_.size)).reshape(rows_, 128)

    d_s, m_s, v_s = _adamw(pack_small(w_given), pack_small(grads), pack_small(m_given), pack_small(v_given),
                           "adamw_small")
    off = 0
    for n in SMALL_NAMES:
        size, shp = w_given[n].size, w_given[n].shape
        delta[n] = d_s.reshape(-1)[off:off + size].reshape(shp)
        new_m[n] = m_s.reshape(-1)[off:off + size].reshape(shp)
        new_v[n] = v_s.reshape(-1)[off:off + size].reshape(shp)
        off += size

    return (loss, grad_x, *[grads[n] for n in WEIGHTS], *[delta[n] for n in WEIGHTS],
            *[new_m[n] for n in WEIGHTS], *[new_v[n] for n in WEIGHTS])
```

```python
import functools

import jax
import jax.numpy as jnp
from jax import lax
from jax.experimental import pallas as pl
from jax.experimental.pallas import tpu as pltpu

F32 = jnp.float32
MXU_DT = jnp.bfloat16
HI = lax.Precision.HIGHEST
MESH = pl.DeviceIdType.MESH
N_DEV = 8

D = 1024
HEADS = 4
HD = 128
CHUNK = 64
GLA_KEY = 64
GLA_TAU = 16.0
GATE_RANK = 16
D_FF = 2816
IN_W = 3608
ALPHA = 2.0 ** 0.25
EPS = 1e-6
DN_CONV_K = 4
FFN_CONV_K = 3
HALO = 8

P_QKV, P_Z, P_GQ, P_GK, P_GV, P_GG, P_SM, P_W = 0, 1536, 2048, 2560, 3072, 3584, 4096, 4224
SM_A, SM_B, SM_R = 0, 4, 8

ADAM_LR, ADAM_B1, ADAM_B2, ADAM_EPS, ADAM_WD, ADAM_STEP = 0.001, 0.9, 0.999, 1e-08, 0.01, 10

VMEM_LIMIT_V7X = 56 * 1024 * 1024


def _params(sem=None):
    return pltpu.CompilerParams(dimension_semantics=sem, vmem_limit_bytes=VMEM_LIMIT_V7X)


def _dg(a, b, dims, prec=None):
    return lax.dot_general(a, b, (dims, ((), ())), precision=prec, preferred_element_type=F32)


def _dot(a, b, prec=None):
    return _dg(a, b, ((1,), (0,)), prec)


def _dot_nt(a, b, prec=None):
    return _dg(a, b, ((1,), (1,)), prec)


def _dot_tn(a, b, prec=None):
    return _dg(a, b, ((0,), (0,)), prec)


def _iota(shape, dim):
    return lax.broadcasted_iota(jnp.int32, shape, dim)


def _sigmoid(x):
    return jax.nn.sigmoid(x)


def _silu(x):
    return x * _sigmoid(x)


def _softplus(x):
    return jnp.maximum(x, 0.0) + jnp.log(1.0 + jnp.exp(-jnp.abs(x)))


def _ln_stats(x):
    mu = jnp.mean(x, axis=-1, keepdims=True)
    xc = x - mu
    rstd = lax.rsqrt(jnp.mean(xc * xc, axis=-1, keepdims=True) + EPS)
    return xc * rstd, rstd


def _ln_bwd(dxhat, xhat, rstd):
    return rstd * (dxhat - jnp.mean(dxhat, axis=-1, keepdims=True)
                   - xhat * jnp.mean(dxhat * xhat, axis=-1, keepdims=True))


NN, NT, TN = ((1,), (0,)), ((1,), (1,)), ((0,), (0,))


def _split2(a):
    hi = a.astype(jnp.bfloat16)
    return hi, (a - hi.astype(F32)).astype(jnp.bfloat16)


def _d3(a, b, dims):
    ah, al = _split2(a)
    bh, bl = _split2(b)
    return _dg(ah, bh, dims) + (_dg(ah, bl, dims) + _dg(al, bh, dims))


@jax.custom_vjp
def _dot3(a, b):
    return _d3(a, b, NN)


_dot3.defvjp(lambda a, b: (_d3(a, b, NN), (a, b)),
             lambda res, g: (_d3(g, res[1], NT), _d3(res[0], g, TN)))


def _split3(b):
    b1 = b.astype(jnp.bfloat16)
    r1 = b - b1.astype(F32)
    b2 = r1.astype(jnp.bfloat16)
    return b1, b2, (r1 - b2.astype(F32)).astype(jnp.bfloat16)


def _sum3(fn, b):
    b1, b2, b3 = _split3(b)
    return fn(b1) + (fn(b2) + fn(b3))


@jax.custom_vjp
def _mask_dot(e, b):
    return _sum3(lambda t: _dg(e, t, NN), b)


_mask_dot.defvjp(lambda e, b: (_mask_dot(e, b), e),
                 lambda e, g: (jnp.zeros_like(e), _sum3(lambda t: _dg(e, t, TN), g)))


@jax.custom_vjp
def _mask_dot_nt(e, b):
    return _sum3(lambda t: _dg(e, t, NT), b)


_mask_dot_nt.defvjp(lambda e, b: (_mask_dot_nt(e, b), e),
                    lambda e, g: (jnp.zeros_like(e), _sum3(lambda t: _dg(t, e, TN), g)))


def _tri_inv_impl(ms):
    n = ms[0].shape[0]
    r, c = _iota((n, n), 0), _iota((n, n), 1)
    eye = (r == c).astype(F32)
    diag = (r >> 3) == (c >> 3)
    ds = [jnp.where(diag, m, 0.0) for m in ms]
    d2s = [_d3(d, d, NN) for d in ds]
    d4s = [_d3(d2, d2, NN) for d2 in d2s]
    invs = [_d3(eye - d, eye + d2, NN) for d, d2 in zip(ds, d2s)]
    invs = [_d3(inv, eye + d4, NN) for inv, d4 in zip(invs, d4s)]
    shift = 3
    while (1 << shift) < n:
        rb, cb = r >> shift, c >> shift
        sel = ((rb & 1) == 1) & (cb == rb - 1)
        tmp = [_d3(inv, jnp.where(sel, m, 0.0), NN) for inv, m in zip(invs, ms)]
        invs = [inv - _d3(t, inv, NN) for t, inv in zip(tmp, invs)]
        shift += 1
    return invs


@jax.custom_vjp
def _tri_inv(ms):
    return _tri_inv_impl(ms)


def _tri_inv_fwd(ms):
    invs = _tri_inv_impl(ms)
    return invs, invs


def _tri_inv_bwd(invs, das):
    tmp = [_d3(a, da, TN) for a, da in zip(invs, das)]
    return ([-_d3(t, a, NT) for t, a in zip(tmp, invs)],)


_tri_inv.defvjp(_tri_inv_fwd, _tri_inv_bwd)


@jax.custom_vjp
def _tri_inv_known(ms, invs):
    return invs


_tri_inv_known.defvjp(lambda ms, invs: (invs, invs),
                      lambda invs, das: (_tri_inv_bwd(invs, das)[0], [jnp.zeros_like(a) for a in invs]))


def _dn_chunk(s_list, q, k, v, gates, inv_known=None, with_inv=False):
    nb = len(q)
    c = q[0].shape[0]
    r64, c64 = _iota((c, c), 0), _iota((c, c), 1)
    causal = r64 >= c64
    strict = r64 > c64
    tri = causal.astype(jnp.bfloat16)
    eye = (_iota((HD, HD), 0) == _iota((HD, HD), 1)).astype(jnp.bfloat16)
    lane = _iota(gates[0].shape, 1)
    lane1 = _iota((1, HD), 1)
    g_all = [_mask_dot(tri, g) for g in gates]
    g_all_t = [_mask_dot_nt(eye, g) for g in g_all]
    row = _iota(g_all_t[0].shape, 0)
    last = [jnp.sum(g, axis=0, keepdims=True) for g in gates]
    prob = [(b, h) for b in range(nb) for h in range(HEADS)]
    sl = [slice(h * HD, (h + 1) * HD) for h in range(HEADS)]
    qh = [q[b][:, sl[h]] for b, h in prob]
    kh = [k[b][:, sl[h]] for b, h in prob]
    vh = [v[b][:, sl[h]] for b, h in prob]
    s = [s_list[b][h] for b, h in prob]
    beta = [jnp.sum(jnp.where(lane == SM_B + h, gates[b], 0.0), axis=-1, keepdims=True) for b, h in prob]
    g_c = [jnp.sum(jnp.where(lane == SM_A + h, g_all[b], 0.0), axis=-1, keepdims=True) for b, h in prob]
    g_r = [jnp.sum(jnp.where(row == SM_A + h, g_all_t[b], 0.0), axis=0, keepdims=True) for b, h in prob]
    g_last = [jnp.sum(jnp.where(lane1 == SM_A + h, last[b], 0.0), axis=-1, keepdims=True) for b, h in prob]
    decay = [jnp.where(causal, jnp.exp(jnp.where(causal, gc - gr, 0.0)), 0.0) for gc, gr in zip(g_c, g_r)]
    kb = [k_ * b_ for k_, b_ in zip(kh, beta)]
    m_low = [jnp.where(strict, _dot_nt(kb_, k_) * d_, 0.0) for kb_, k_, d_ in zip(kb, kh, decay)]
    attn = [_dot_nt(q_, k_) * d_ for q_, k_, d_ in zip(qh, kh, decay)]
    a_inv = _tri_inv(m_low) if inv_known is None else _tri_inv_known(m_low, inv_known)
    eg = [jnp.exp(gc) for gc in g_c]
    uw = [_dot3(a_, jnp.concatenate([v_ * b_, kb_ * e_], axis=1))
          for a_, v_, b_, kb_, e_ in zip(a_inv, vh, beta, kb, eg)]
    v_new = [uw_[:, :HD] - _dot(uw_[:, HD:], s_) for uw_, s_ in zip(uw, s)]
    qs = [_dot(q_ * e_, s_) for q_, e_, s_ in zip(qh, eg, s)]
    o = [qs_ + _dot(a_, vn_) for qs_, a_, vn_ in zip(qs, attn, v_new)]
    k_dec = [k_ * jnp.exp(gl - gc) for k_, gl, gc in zip(kh, g_last, g_c)]
    s_new = [s_ * jnp.exp(gl) + _dot_tn(kd_, vn_) for s_, gl, kd_, vn_ in zip(s, g_last, k_dec, v_new)]
    outs = [jnp.concatenate(o[b * HEADS:(b + 1) * HEADS], axis=-1) for b in range(nb)]
    states = [s_new[b * HEADS:(b + 1) * HEADS] for b in range(nb)]
    return (outs, states, a_inv) if with_inv else (outs, states)


def _gla_chunk(st_list, q, k, v, small, w2, bg):
    nb = len(q)
    c = q[0].shape[0]
    causal = _iota((c, c), 0) >= _iota((c, c), 1)
    tri = causal.astype(jnp.bfloat16)
    la_all = [-_softplus(-(_dot(sm, w2) + bg)) * (1.0 / GLA_TAU) for sm in small]
    b_all = [_mask_dot(tri, la) for la in la_all]
    prob = [(b, h) for b in range(nb) for h in range(HEADS)]
    sl = [slice(h * HD, (h + 1) * HD) for h in range(HEADS)]
    kh = [k[b][:, sl[h]] for b, h in prob]
    vh = [v[b][:, sl[h]] for b, h in prob]
    st = [st_list[b][h] for b, h in prob]
    bc = [b_all[b][:, sl[h]] for b, h in prob]
    b_last = [jnp.sum(la_all[b][:, sl[h]], axis=0, keepdims=True) for b, h in prob]
    q_dec = [q[b][:, sl[h]] * (GLA_KEY ** -0.5) * jnp.exp(bc_) for (b, h), bc_ in zip(prob, bc)]
    attn = [jnp.where(causal, _dot_nt(qd, k_ * jnp.exp(-bc_)), 0.0) for qd, k_, bc_ in zip(q_dec, kh, bc)]
    inter = [_dot_nt(qd, st_) for qd, st_ in zip(q_dec, st)]
    o = [i_ + _dot(a_, v_) for i_, a_, v_ in zip(inter, attn, vh)]
    k_dec = [k_ * jnp.exp(bl - bc_) for k_, bl, bc_ in zip(kh, b_last, bc)]
    s_new = [st_ * jnp.exp(bl) + _dot_tn(v_, kd) for st_, bl, v_, kd in zip(st, b_last, vh, k_dec)]
    outs = [jnp.concatenate(o[b * HEADS:(b + 1) * HEADS], axis=-1) for b in range(nb)]
    return outs, [s_new[b * HEADS:(b + 1) * HEADS] for b in range(nb)]


def _dn_qkv(y):
    act = _silu(y)
    parts = []
    for i in range(2 * HEADS):
        xh = act[:, i * HD:(i + 1) * HD]
        xh = xh * lax.rsqrt(jnp.sum(xh * xh, axis=-1, keepdims=True) + EPS)
        parts.append(xh * (HD ** -0.5) if i < HEADS else xh)
    qk = jnp.concatenate(parts, axis=-1)
    return qk[:, :HEADS * HD], qk[:, HEADS * HD:], act[:, 2 * HEADS * HD:]


def _dn_gates(small, alog_row, dt_row):
    lane = _iota(small.shape, 1)
    log_a = -jnp.exp(alog_row) * _softplus(small + dt_row)
    return jnp.where(lane < SM_B, log_a, jnp.where(lane < SM_R, _sigmoid(small), 0.0))


def _gate_norm(o, z, grow):
    parts = []
    for h in range(HEADS):
        oh = o[:, h * HD:(h + 1) * HD]
        parts.append(oh * lax.rsqrt(jnp.mean(oh * oh, axis=-1, keepdims=True) + EPS))
    return jnp.concatenate(parts, axis=-1) * grow * _silu(z)


def _conv_rows(xrows, w_ref, k_taps):
    n = xrows.shape[0]
    acc = xrows * w_ref[k_taps - 1:k_taps, :]
    for s in range(1, k_taps):
        acc = acc + pltpu.roll(xrows, s, 0) * w_ref[k_taps - 1 - s:k_taps - s, :]
    return acc


def _shift_up(x, s):
    return x if s == 0 else pltpu.roll(x, x.shape[0] - s, 0)


def _div_tile(n, cap, mult=8):
    best = None
    for t in range(mult, min(n, cap) + 1, mult):
        if n % t == 0:
            best = t
    return best if best is not None else n


def _halo_prev(tt):
    return lambda b, t: (b, jnp.maximum(t * (tt // HALO) - 1, 0))


def _halo_next(tt, t_total):
    return lambda b, t: (b, jnp.minimum((t + 1) * (tt // HALO), t_total // HALO - 1))


def _mm(a, b, mode, out_dtype, name, tm=512, tn=512, tk=None):
    if mode == "nn":
        (m, k), n = a.shape, b.shape[1]
    elif mode == "nt":
        (m, k), n = a.shape, b.shape[0]
    else:
        (k, m), n = a.shape, b.shape[1]
    tm, tn = min(tm, m), min(tn, n)
    tk = k if tk is None else min(tk, k)
    assert m % tm == 0 and n % tn == 0 and k % tk == 0, (name, a.shape, b.shape, tm, tn, tk)
    nk = k // tk
    if mode == "tn":
        a_spec = pl.BlockSpec((tk, tm), lambda i, j, kk: (kk, i))
    else:
        a_spec = pl.BlockSpec((tm, tk), lambda i, j, kk: (i, kk))
    if mode == "nt":
        b_spec = pl.BlockSpec((tn, tk), lambda i, j, kk: (j, kk))
    else:
        b_spec = pl.BlockSpec((tk, tn), lambda i, j, kk: (kk, j))
    dims = {"nn": ((1,), (0,)), "nt": ((1,), (1,)), "tn": ((0,), (0,))}[mode]

    def body(a_ref, b_ref, o_ref, *acc):
        p = _dg(a_ref[...], b_ref[...], dims)
        if nk == 1:
            o_ref[...] = p.astype(out_dtype)
        else:
            kk = pl.program_id(2)

            @pl.when(kk == 0)
            def _():
                acc[0][...] = p

            @pl.when(kk > 0)
            def _():
                acc[0][...] += p

            @pl.when(kk == nk - 1)
            def _():
                o_ref[...] = acc[0][...].astype(out_dtype)

    return pl.pallas_call(
        body, name=name, grid=(m // tm, n // tn, nk),
        in_specs=[a_spec, b_spec],
        out_specs=pl.BlockSpec((tm, tn), lambda i, j, kk: (i, j)),
        out_shape=jax.ShapeDtypeStruct((m, n), out_dtype),
        scratch_shapes=[pltpu.VMEM((tm, tn), F32)] if nk > 1 else [],
        compiler_params=_params(("parallel", "parallel", "arbitrary")),
    )(a, b)


def _ada_fwd(c_all, w_ada, b_cols):
    def body(c_ref, w_ref, b_ref, o_ref):
        cond = _silu(c_ref[...]).astype(MXU_DT)
        o_ref[...] = _dot(cond, w_ref[...].astype(MXU_DT)) + b_ref[...]

    return pl.pallas_call(body, name="ada_fwd", out_shape=jax.ShapeDtypeStruct((c_all.shape[0], w_ada.shape[1]), F32),
                          compiler_params=_params())(c_all, w_ada, b_cols)


def _ada_bwd(c_all, dmod_all, dmod_cols):
    def body(c_ref, da_ref, dc_ref, gw_ref, gb_ref):
        cond = _silu(c_ref[...]).astype(MXU_DT)
        gw_ref[...] = _dot_tn(cond, dc_ref[...].astype(MXU_DT))
        gb_ref[...] = jnp.sum(da_ref[...], axis=0, keepdims=True)

    return pl.pallas_call(
        body, name="ada_bwd",
        out_shape=(jax.ShapeDtypeStruct((c_all.shape[1], dmod_cols.shape[1]), F32),
                   jax.ShapeDtypeStruct((1, dmod_all.shape[1]), F32)),
        compiler_params=_params())(c_all, dmod_all, dmod_cols)


def _tok_spec(tt, width=D):
    return pl.BlockSpec((1, tt, width), lambda b, t: (b, t, 0))


def _vec_spec(width=D):
    return pl.BlockSpec((1, width), lambda b, t: (0, 0))


def _bvec_spec(width=D):
    return pl.BlockSpec((1, 1, width), lambda b, t: (b, 0, 0))


def _ln0_mod(x, g0, b0, sc, sh):
    bsz, t_total, _ = x.shape
    tt = _div_tile(t_total, 256)

    def body(x_ref, g_ref, b_ref, sc_ref, sh_ref, h_ref):
        xh, _ = _ln_stats(x_ref[0])
        x0 = xh * g_ref[...] + b_ref[...]
        h_ref[0] = (x0 * (1.0 + sc_ref[0]) + sh_ref[0]).astype(MXU_DT)

    return pl.pallas_call(
        body, name="ln0_mod", grid=(bsz, t_total // tt),
        in_specs=[_tok_spec(tt), _vec_spec(), _vec_spec(), _bvec_spec(), _bvec_spec()],
        out_specs=_tok_spec(tt), out_shape=jax.ShapeDtypeStruct(x.shape, MXU_DT),
        compiler_params=_params(("parallel", "parallel")))(x, g0, b0, sc, sh)


def _res_ln_mod(x, y, gt, g0, b0, g1, b1, sc, sh):
    bsz, t_total, _ = x.shape
    tt = _div_tile(t_total, 256)

    def body(x_ref, y_ref, gt_ref, g0_ref, b0_ref, g1_ref, b1_ref, sc_ref, sh_ref, r_ref, h_ref):
        xh, _ = _ln_stats(x_ref[0])
        r = ALPHA * (xh * g0_ref[...] + b0_ref[...]) + (1.0 + gt_ref[0]) * y_ref[0]
        r_ref[0] = r
        rh, _ = _ln_stats(r)
        x1 = rh * g1_ref[...] + b1_ref[...]
        h_ref[0] = (x1 * (1.0 + sc_ref[0]) + sh_ref[0]).astype(MXU_DT)

    return pl.pallas_call(
        body, name="res_ln_mod", grid=(bsz, t_total // tt),
        in_specs=[_tok_spec(tt), _tok_spec(tt), _bvec_spec(), _vec_spec(), _vec_spec(), _vec_spec(), _vec_spec(),
                  _bvec_spec(), _bvec_spec()],
        out_specs=(_tok_spec(tt), _tok_spec(tt)),
        out_shape=(jax.ShapeDtypeStruct(x.shape, F32), jax.ShapeDtypeStruct(x.shape, MXU_DT)),
        compiler_params=_params(("parallel", "parallel")))(x, y, gt, g0, b0, g1, b1, sc, sh)


def _final_fwd_bwd(r1, y2, gt, g1, b1, g2, b2, target):
    bsz, t_total, _ = r1.shape
    tt = _div_tile(t_total, 256)

    def body(r1_ref, y2_ref, gt_ref, g1_ref, b1_ref, g2_ref, b2_ref, tg_ref,
             loss_ref, dr2_ref, dy2_ref, dgt_ref, dg2_ref, db2_ref):
        b, t = pl.program_id(0), pl.program_id(1)

        @pl.when((b == 0) & (t == 0))
        def _():
            loss_ref[...] = jnp.zeros_like(loss_ref)
            dg2_ref[...] = jnp.zeros_like(dg2_ref)
            db2_ref[...] = jnp.zeros_like(db2_ref)

        @pl.when(t == 0)
        def _():
            dgt_ref[...] = jnp.zeros_like(dgt_ref)

        rh1, _ = _ln_stats(r1_ref[0])
        x1 = rh1 * g1_ref[...] + b1_ref[...]
        y2 = y2_ref[0]
        gate = 1.0 + gt_ref[0]
        xh2, rstd2 = _ln_stats(ALPHA * x1 + gate * y2)
        err = xh2 * g2_ref[...] + b2_ref[...] - tg_ref[0]
        loss_ref[...] += jnp.sum(err * err, axis=0, keepdims=True)
        dx2 = err * (1.0 / D)
        dg2_ref[...] += jnp.sum(dx2 * xh2, axis=0, keepdims=True)
        db2_ref[...] += jnp.sum(dx2, axis=0, keepdims=True)
        dr2 = _ln_bwd(dx2 * g2_ref[...], xh2, rstd2)
        dr2_ref[0] = dr2
        dy2_ref[0] = (gate * dr2).astype(MXU_DT)
        dgt_ref[0] += jnp.sum(dr2 * y2, axis=0, keepdims=True)

    vec_out = jax.ShapeDtypeStruct((1, D), F32)
    return pl.pallas_call(
        body, name="final_fwd_bwd", grid=(bsz, t_total // tt),
        in_specs=[_tok_spec(tt), _tok_spec(tt), _bvec_spec(), _vec_spec(), _vec_spec(), _vec_spec(), _vec_spec(),
                  _tok_spec(tt)],
        out_specs=(_vec_spec(), _tok_spec(tt), _tok_spec(tt), _bvec_spec(), _vec_spec(), _vec_spec()),
        out_shape=(vec_out, jax.ShapeDtypeStruct(r1.shape, F32), jax.ShapeDtypeStruct(r1.shape, MXU_DT),
                   jax.ShapeDtypeStruct((bsz, 1, D), F32), vec_out, vec_out),
        compiler_params=_params(("arbitrary", "arbitrary")))(r1, y2, gt, g1, b1, g2, b2, target)


def _ln_bwd_call(name, d_res, d_h, src, g, b, sc, y=None, gt=None):
    bsz, t_total, _ = src.shape
    tt = _div_tile(t_total, 256)
    has_y = y is not None

    def body(*refs):
        if has_y:
            (dres_ref, dh_ref, src_ref, g_ref, b_ref, sc_ref, y_ref, gt_ref,
             dsrc_ref, dsc_ref, dsh_ref, dg_ref, db_ref, dy_ref, dgt_ref) = refs
        else:
            (dres_ref, dh_ref, src_ref, g_ref, b_ref, sc_ref,
             dsrc_ref, dsc_ref, dsh_ref, dg_ref, db_ref) = refs
        bi, t = pl.program_id(0), pl.program_id(1)

        @pl.when((bi == 0) & (t == 0))
        def _():
            dg_ref[...] = jnp.zeros_like(dg_ref)
            db_ref[...] = jnp.zeros_like(db_ref)

        @pl.when(t == 0)
        def _():
            dsc_ref[...] = jnp.zeros_like(dsc_ref)
            dsh_ref[...] = jnp.zeros_like(dsh_ref)
            if has_y:
                dgt_ref[...] = jnp.zeros_like(dgt_ref)

        xh, rstd = _ln_stats(src_ref[0])
        xv = xh * g_ref[...] + b_ref[...]
        dh = dh_ref[0]
        dx = ALPHA * dres_ref[0] + dh * (1.0 + sc_ref[0])
        dsc_ref[0] += jnp.sum(dh * xv, axis=0, keepdims=True)
        dsh_ref[0] += jnp.sum(dh, axis=0, keepdims=True)
        dg_ref[...] += jnp.sum(dx * xh, axis=0, keepdims=True)
        db_ref[...] += jnp.sum(dx, axis=0, keepdims=True)
        dsrc = _ln_bwd(dx * g_ref[...], xh, rstd)
        dsrc_ref[0] = dsrc
        if has_y:
            dy_ref[0] = ((1.0 + gt_ref[0]) * dsrc).astype(MXU_DT)
            dgt_ref[0] += jnp.sum(dsrc * y_ref[0], axis=0, keepdims=True)

    vec_out = jax.ShapeDtypeStruct((1, D), F32)
    bvec_out = jax.ShapeDtypeStruct((bsz, 1, D), F32)
    in_specs = [_tok_spec(tt), _tok_spec(tt), _tok_spec(tt), _vec_spec(), _vec_spec(), _bvec_spec()]
    out_specs = [_tok_spec(tt), _bvec_spec(), _bvec_spec(), _vec_spec(), _vec_spec()]
    out_shape = [jax.ShapeDtypeStruct(src.shape, F32), bvec_out, bvec_out, vec_out, vec_out]
    args = [d_res, d_h, src, g, b, sc]
    if has_y:
        in_specs += [_tok_spec(tt), _bvec_spec()]
        out_specs += [_tok_spec(tt), _bvec_spec()]
        out_shape += [jax.ShapeDtypeStruct(src.shape, MXU_DT), bvec_out]
        args += [y, gt]
    return pl.pallas_call(body, name=name, grid=(bsz, t_total // tt), in_specs=in_specs, out_specs=tuple(out_specs),
                          out_shape=tuple(out_shape), compiler_params=_params(("arbitrary", "arbitrary")))(*args)


FFN_TC = 256
FFN_NJ = D_FF // FFN_TC
FFN_PW = 2 * FFN_TC


def _ffn_pair(a, axis):
    shp = list(a.shape)
    a4 = a.reshape(shp[:axis] + [2, FFN_NJ, FFN_TC] + shp[axis + 1:])
    return jnp.swapaxes(a4, axis, axis + 1).reshape(shp)


def _ffn_unpair(a, axis):
    shp = list(a.shape)
    a4 = a.reshape(shp[:axis] + [FFN_NJ, 2, FFN_TC] + shp[axis + 1:])
    return jnp.swapaxes(a4, axis, axis + 1).reshape(shp)


def _ffn_act_fwd(up, cw, cb):
    bsz, t_total, _ = up.shape
    tt = _div_tile(t_total, 256)
    hp = _halo_prev(tt)

    def body(x_ref, xp_ref, w_ref, b_ref, o_ref):
        prev = jnp.where(pl.program_id(1) == 0, 0.0, xp_ref[0])
        rows = jnp.concatenate([prev, x_ref[0]], axis=0)
        u = _conv_rows(rows, w_ref, FFN_CONV_K)[HALO:] + b_ref[...]
        o_ref[0] = (_silu(u[:, :FFN_TC]) * u[:, FFN_TC:]).astype(MXU_DT)

    return pl.pallas_call(
        body, name="ffn_act_fwd", grid=(bsz, t_total // tt, FFN_NJ),
        in_specs=[pl.BlockSpec((1, tt, FFN_PW), lambda b, t, j: (b, t, j)),
                  pl.BlockSpec((1, HALO, FFN_PW), lambda b, t, j: (*hp(b, t), j)),
                  pl.BlockSpec((FFN_CONV_K, FFN_PW), lambda b, t, j: (0, j)),
                  pl.BlockSpec((1, FFN_PW), lambda b, t, j: (0, j))],
        out_specs=pl.BlockSpec((1, tt, FFN_TC), lambda b, t, j: (b, t, j)),
        out_shape=jax.ShapeDtypeStruct((bsz, t_total, D_FF), MXU_DT),
        compiler_params=_params(("parallel", "parallel", "parallel")))(up, up, cw, cb)


def _ffn_act_bwd(up, da, cw, cb):
    bsz, t_total, width = up.shape
    tt = _div_tile(t_total, 256)
    nt = t_total // tt
    hp, hn = _halo_prev(tt), _halo_next(tt, t_total)

    def body(x_ref, xp_ref, xn_ref, da_ref, dan_ref, w_ref, b_ref, dup_ref, dw_ref, db_ref):
        b, t = pl.program_id(1), pl.program_id(2)

        @pl.when((b == 0) & (t == 0))
        def _():
            dw_ref[...] = jnp.zeros_like(dw_ref)
            db_ref[...] = jnp.zeros_like(db_ref)

        prev = jnp.where(t == 0, 0.0, xp_ref[0])
        rows = jnp.concatenate([prev, x_ref[0], xn_ref[0]], axis=0)
        u = _conv_rows(rows, w_ref, FFN_CONV_K)[HALO:] + b_ref[...]
        g_pre, v_pre = u[:, :FFN_TC], u[:, FFN_TC:]
        valid = (_iota((tt + HALO, 1), 0) < tt) | (t < nt - 1)
        da_ext = jnp.where(valid, jnp.concatenate([da_ref[0], dan_ref[0]], axis=0), 0.0)
        sg = _sigmoid(g_pre)
        gs = g_pre * sg
        du = jnp.concatenate([da_ext * v_pre * (sg + gs * (1.0 - sg)), da_ext * gs], axis=1)
        dup = du * w_ref[FFN_CONV_K - 1:FFN_CONV_K, :]
        for s in range(1, FFN_CONV_K):
            dup = dup + _shift_up(du, s) * w_ref[FFN_CONV_K - 1 - s:FFN_CONV_K - s, :]
        dup_ref[0] = dup[:tt].astype(MXU_DT)
        du_t = du[:tt]
        db_ref[...] += jnp.sum(du_t, axis=0, keepdims=True)
        for k in range(FFN_CONV_K):
            s = FFN_CONV_K - 1 - k
            xs = (rows if s == 0 else pltpu.roll(rows, s, 0))[HALO:HALO + tt]
            dw_ref[k:k + 1, :] += jnp.sum(du_t * xs, axis=0, keepdims=True)

    def halo(h, w):
        return pl.BlockSpec((1, HALO, w), lambda j, b, t: (*h(b, t), j))

    wspec = lambda rows_: pl.BlockSpec((rows_, FFN_PW), lambda j, b, t: (0, j))
    tile = pl.BlockSpec((1, tt, FFN_PW), lambda j, b, t: (b, t, j))
    return pl.pallas_call(
        body, name="ffn_act_bwd", grid=(FFN_NJ, bsz, nt),
        in_specs=[tile, halo(hp, FFN_PW), halo(hn, FFN_PW),
                  pl.BlockSpec((1, tt, FFN_TC), lambda j, b, t: (b, t, j)), halo(hn, FFN_TC),
                  wspec(FFN_CONV_K), wspec(1)],
        out_specs=(tile, wspec(FFN_CONV_K), wspec(1)),
        out_shape=(jax.ShapeDtypeStruct(up.shape, MXU_DT), jax.ShapeDtypeStruct((FFN_CONV_K, width), F32),
                   jax.ShapeDtypeStruct((1, width), F32)),
        compiler_params=_params(("arbitrary", "arbitrary", "arbitrary")))(up, up, up, da, da, cw, cb)


QKV_W = 3 * HEADS * HD
SM_BLK = P_SM // 128


def _dn_pre_fwd(proj, conv_w, alog_row, dt_row):
    bsz, t_total, _ = proj.shape
    tt = _div_tile(t_total, 256)
    hp = _halo_prev(tt)

    def body(x_ref, xp_ref, sm_ref, w_ref, al_ref, dt_ref, q_ref, k_ref, v_ref, g_ref):
        prev = jnp.where(pl.program_id(1) == 0, 0.0, xp_ref[0])
        y = _conv_rows(jnp.concatenate([prev, x_ref[0]], axis=0), w_ref, DN_CONV_K)[HALO:]
        q_ref[0], k_ref[0], v_ref[0] = _dn_qkv(y)
        g_ref[0] = _dn_gates(sm_ref[0], al_ref[...], dt_ref[...])

    out512 = jax.ShapeDtypeStruct((bsz, t_total, HEADS * HD), F32)
    return pl.pallas_call(
        body, name="dn_pre_fwd", grid=(bsz, t_total // tt),
        in_specs=[pl.BlockSpec((1, tt, QKV_W), lambda b, t: (b, t, 0)),
                  pl.BlockSpec((1, HALO, QKV_W), lambda b, t: (*hp(b, t), 0)),
                  pl.BlockSpec((1, tt, 128), lambda b, t: (b, t, SM_BLK)),
                  pl.BlockSpec((DN_CONV_K, QKV_W), lambda b, t: (0, 0)), _vec_spec(128), _vec_spec(128)],
        out_specs=(_tok_spec(tt, 512), _tok_spec(tt, 512), _tok_spec(tt, 512), _tok_spec(tt, 128)),
        out_shape=(out512, out512, out512, jax.ShapeDtypeStruct((bsz, t_total, 128), F32)),
        compiler_params=_params(("parallel", "parallel")))(proj, proj, proj, conv_w, alog_row, dt_row)


def _dn_pre_bwd(proj, dq, dk, dv, dgates, conv_w, alog_row, dt_row):
    bsz, t_total, _ = proj.shape
    tt = _div_tile(t_total, 128)
    nt = t_total // tt
    hp, hn = _halo_prev(tt), _halo_next(tt, t_total)

    def body(x_ref, xp_ref, xn_ref, sm_ref, dq_ref, dqn_ref, dk_ref, dkn_ref, dv_ref, dvn_ref, dg_ref,
             w_ref, al_ref, dt_ref, dx_ref, dsm_ref, dw_ref, dal_ref, ddt_ref):
        b, t = pl.program_id(0), pl.program_id(1)

        @pl.when((b == 0) & (t == 0))
        def _():
            dw_ref[...] = jnp.zeros_like(dw_ref)
            dal_ref[...] = jnp.zeros_like(dal_ref)
            ddt_ref[...] = jnp.zeros_like(ddt_ref)

        prev = jnp.where(t == 0, 0.0, xp_ref[0])
        rows = jnp.concatenate([prev, x_ref[0], xn_ref[0]], axis=0)
        y = _conv_rows(rows, w_ref, DN_CONV_K)[HALO:]
        valid = (_iota((tt + HALO, 1), 0) < tt) | (t < nt - 1)

        def ext(tile_ref, next_ref):
            return jnp.where(valid, jnp.concatenate([tile_ref[0], next_ref[0]], axis=0), 0.0)

        _, vjp_qkv = jax.vjp(_dn_qkv, y)
        (dy,) = vjp_qkv((ext(dq_ref, dqn_ref), ext(dk_ref, dkn_ref), ext(dv_ref, dvn_ref)))
        dy = jnp.where(valid, dy, 0.0)
        dx = dy * w_ref[DN_CONV_K - 1:DN_CONV_K, :]
        for s in range(1, DN_CONV_K):
            dx = dx + _shift_up(dy, s) * w_ref[DN_CONV_K - 1 - s:DN_CONV_K - s, :]
        dx_ref[0] = dx[:tt].astype(MXU_DT)
        dy_t = dy[:tt]
        for k in range(DN_CONV_K):
            s = DN_CONV_K - 1 - k
            xs = (rows if s == 0 else pltpu.roll(rows, s, 0))[HALO:HALO + tt]
            dw_ref[k:k + 1, :] += jnp.sum(dy_t * xs, axis=0, keepdims=True)
        _, vjp_g = jax.vjp(_dn_gates, sm_ref[0], al_ref[...], dt_ref[...])
        dsm, dal, ddt = vjp_g(dg_ref[0])
        dsm_ref[0] = dsm
        dal_ref[...] += dal
        ddt_ref[...] += ddt

    def tile(width, blk=0):
        return pl.BlockSpec((1, tt, width), lambda b, t: (b, t, blk))

    def halo(h, width):
        return pl.BlockSpec((1, HALO, width), lambda b, t: (*h(b, t), 0))

    return pl.pallas_call(
        body, name="dn_pre_bwd", grid=(bsz, nt),
        in_specs=[tile(QKV_W), halo(hp, QKV_W), halo(hn, QKV_W), tile(128, SM_BLK),
                  tile(512), halo(hn, 512), tile(512), halo(hn, 512), tile(512), halo(hn, 512), tile(128),
                  pl.BlockSpec((DN_CONV_K, QKV_W), lambda b, t: (0, 0)), _vec_spec(128), _vec_spec(128)],
        out_specs=(tile(QKV_W), tile(128), pl.BlockSpec((DN_CONV_K, QKV_W), lambda b, t: (0, 0)),
                   _vec_spec(128), _vec_spec(128)),
        out_shape=(jax.ShapeDtypeStruct((bsz, t_total, QKV_W), MXU_DT), jax.ShapeDtypeStruct((bsz, t_total, 128), F32),
                   jax.ShapeDtypeStruct((DN_CONV_K, QKV_W), F32), jax.ShapeDtypeStruct((1, 128), F32),
                   jax.ShapeDtypeStruct((1, 128), F32)),
        compiler_params=_params(("arbitrary", "arbitrary")))(
            proj, proj, proj, proj, dq, dq, dk, dk, dv, dv, dgates, conv_w, alog_row, dt_row)


def _state_spec(bsz, idx):
    return pl.BlockSpec((bsz, 1, HEADS, HD, HD), lambda c: (0, idx(c), 0, 0, 0))


def _inv_spec(bsz, idx):
    return pl.BlockSpec((bsz, 1, HEADS, CHUNK, CHUNK), lambda c: (0, idx(c), 0, 0, 0))


def _chunk_spec(bsz, width, idx, blk=0):
    return pl.BlockSpec((bsz, CHUNK, width), lambda c: (0, idx(c), blk))


def _dn_rec_fwd(q, k, v, gates):
    bsz, t_total, _ = q.shape
    nc = t_total // CHUNK
    fwd = lambda c: c

    def body(q_ref, k_ref, v_ref, g_ref, o_ref, ss_ref, inv_ref, s_ref):
        @pl.when(pl.program_id(0) == 0)
        def _():
            s_ref[...] = jnp.zeros_like(s_ref)

        seqs = range(bsz)
        s_list = [[s_ref[b * HEADS + h] for h in range(HEADS)] for b in seqs]
        for b in seqs:
            for h in range(HEADS):
                ss_ref[b, 0, h] = s_list[b][h]
        o, new_s, invs = _dn_chunk(s_list, [q_ref[b] for b in seqs], [k_ref[b] for b in seqs],
                                   [v_ref[b] for b in seqs], [g_ref[b] for b in seqs], with_inv=True)
        for b in seqs:
            o_ref[b] = o[b]
            for h in range(HEADS):
                s_ref[b * HEADS + h] = new_s[b][h]
                inv_ref[b, 0, h] = invs[b * HEADS + h]

    return pl.pallas_call(
        body, name="dn_rec_fwd", grid=(nc,),
        in_specs=[_chunk_spec(bsz, 512, fwd)] * 3 + [_chunk_spec(bsz, 128, fwd)],
        out_specs=(_chunk_spec(bsz, 512, fwd), _state_spec(bsz, fwd), _inv_spec(bsz, fwd)),
        out_shape=(jax.ShapeDtypeStruct(q.shape, F32), jax.ShapeDtypeStruct((bsz, nc, HEADS, HD, HD), F32),
                   jax.ShapeDtypeStruct((bsz, nc, HEADS, CHUNK, CHUNK), F32)),
        scratch_shapes=[pltpu.VMEM((bsz * HEADS, HD, HD), F32)],
        compiler_params=_params(("arbitrary",)))(q, k, v, gates)


def _dn_rec_bwd(q, k, v, gates, states, invs, do):
    bsz, t_total, _ = q.shape
    nc = t_total // CHUNK
    rev = lambda c: nc - 1 - c

    def body(q_ref, k_ref, v_ref, g_ref, ss_ref, inv_ref, do_ref, dq_ref, dk_ref, dv_ref, dg_ref, ds_ref):
        @pl.when(pl.program_id(0) == 0)
        def _():
            ds_ref[...] = jnp.zeros_like(ds_ref)

        seqs = range(bsz)
        s_list = [[ss_ref[b, 0, h] for h in range(HEADS)] for b in seqs]
        known = [inv_ref[b, 0, h] for b in seqs for h in range(HEADS)]
        _, vjp = jax.vjp(functools.partial(_dn_chunk, inv_known=known),
                         s_list, [q_ref[b] for b in seqs], [k_ref[b] for b in seqs],
                         [v_ref[b] for b in seqs], [g_ref[b] for b in seqs])
        ds_in, dq, dk, dv, dg = vjp(([do_ref[b] for b in seqs],
                                     [[ds_ref[b * HEADS + h] for h in range(HEADS)] for b in seqs]))
        for b in seqs:
            dq_ref[b], dk_ref[b], dv_ref[b], dg_ref[b] = dq[b], dk[b], dv[b], dg[b]
            for h in range(HEADS):
                ds_ref[b * HEADS + h] = ds_in[b][h]

    tok = lambda width: _chunk_spec(bsz, width, rev)
    out512 = jax.ShapeDtypeStruct(q.shape, F32)
    return pl.pallas_call(
        body, name="dn_rec_bwd", grid=(nc,),
        in_specs=[tok(512), tok(512), tok(512), tok(128), _state_spec(bsz, rev), _inv_spec(bsz, rev), tok(512)],
        out_specs=(tok(512), tok(512), tok(512), tok(128)),
        out_shape=(out512, out512, out512, jax.ShapeDtypeStruct(gates.shape, F32)),
        scratch_shapes=[pltpu.VMEM((bsz * HEADS, HD, HD), F32)],
        compiler_params=_params(("arbitrary",)))(q, k, v, gates, states, invs, do)


GQ_BLK, GK_BLK, GV_BLK = P_GQ // 512, P_GK // 512, P_GV // 512


def _gla_rec_fwd(proj, w2, bg):
    bsz, t_total, _ = proj.shape
    nc = t_total // CHUNK

    fwd = lambda c: c

    def body(q_ref, k_ref, v_ref, sm_ref, w2_ref, bg_ref, o_ref, ss_ref, s_ref):
        @pl.when(pl.program_id(0) == 0)
        def _():
            s_ref[...] = jnp.zeros_like(s_ref)

        seqs = range(bsz)
        s_list = [[s_ref[b * HEADS + h] for h in range(HEADS)] for b in seqs]
        for b in seqs:
            for h in range(HEADS):
                ss_ref[b, 0, h] = s_list[b][h]
        o, new_s = _gla_chunk(s_list, [q_ref[b] for b in seqs], [k_ref[b] for b in seqs], [v_ref[b] for b in seqs],
                              [sm_ref[b] for b in seqs], w2_ref[...], bg_ref[...])
        for b in seqs:
            o_ref[b] = o[b]
            for h in range(HEADS):
                s_ref[b * HEADS + h] = new_s[b][h]

    col = lambda blk, width=512: _chunk_spec(bsz, width, fwd, blk)
    return pl.pallas_call(
        body, name="gla_rec_fwd", grid=(nc,),
        in_specs=[col(GQ_BLK), col(GK_BLK), col(GV_BLK), col(SM_BLK, 128),
                  pl.BlockSpec((128, 512), lambda c: (0, 0)), pl.BlockSpec((1, 512), lambda c: (0, 0))],
        out_specs=(col(0), _state_spec(bsz, fwd)),
        out_shape=(jax.ShapeDtypeStruct((bsz, t_total, 512), F32),
                   jax.ShapeDtypeStruct((bsz, nc, HEADS, HD, HD), F32)),
        scratch_shapes=[pltpu.VMEM((bsz * HEADS, HD, HD), F32)],
        compiler_params=_params(("arbitrary",)))(proj, proj, proj, proj, w2, bg)


def _gla_rec_bwd(proj, w2, bg, states, do, dsm_dn):
    bsz, t_total, _ = proj.shape
    nc = t_total // CHUNK
    rev = lambda c: nc - 1 - c

    def body(q_ref, k_ref, v_ref, sm_ref, w2_ref, bg_ref, ss_ref, do_ref, dsd_ref,
             dq_ref, dk_ref, dv_ref, dsm_ref, dw2_ref, dbg_ref, ds_ref):
        @pl.when(pl.program_id(0) == 0)
        def _():
            dw2_ref[...] = jnp.zeros_like(dw2_ref)
            dbg_ref[...] = jnp.zeros_like(dbg_ref)
            ds_ref[...] = jnp.zeros_like(ds_ref)

        seqs = range(bsz)
        s_list = [[ss_ref[b, 0, h] for h in range(HEADS)] for b in seqs]
        _, vjp = jax.vjp(_gla_chunk, s_list, [q_ref[b] for b in seqs], [k_ref[b] for b in seqs],
                         [v_ref[b] for b in seqs], [sm_ref[b] for b in seqs], w2_ref[...], bg_ref[...])
        ds_in, dq, dk, dv, dsm, dw2, dbg = vjp(([do_ref[b] for b in seqs],
                                                [[ds_ref[b * HEADS + h] for h in range(HEADS)] for b in seqs]))
        for b in seqs:
            dq_ref[b], dk_ref[b], dv_ref[b] = dq[b].astype(MXU_DT), dk[b].astype(MXU_DT), dv[b].astype(MXU_DT)
            dsm_ref[b] = (dsm[b] + dsd_ref[b]).astype(MXU_DT)
            for h in range(HEADS):
                ds_ref[b * HEADS + h] = ds_in[b][h]
        dw2_ref[...] += dw2
        dbg_ref[...] += dbg

    col = lambda blk, width=512: _chunk_spec(bsz, width, rev, blk)
    w2_spec = pl.BlockSpec((128, 512), lambda c: (0, 0))
    bg_spec = pl.BlockSpec((1, 512), lambda c: (0, 0))
    out512 = jax.ShapeDtypeStruct((bsz, t_total, 512), MXU_DT)
    return pl.pallas_call(
        body, name="gla_rec_bwd", grid=(nc,),
        in_specs=[col(GQ_BLK), col(GK_BLK), col(GV_BLK), col(SM_BLK, 128), w2_spec, bg_spec,
                  _state_spec(bsz, rev), col(0), col(0, 128)],
        out_specs=(col(0), col(0), col(0), col(0, 128), w2_spec, bg_spec),
        out_shape=(out512, out512, out512, jax.ShapeDtypeStruct((bsz, t_total, 128), MXU_DT),
                   jax.ShapeDtypeStruct((128, 512), F32), jax.ShapeDtypeStruct((1, 512), F32)),
        scratch_shapes=[pltpu.VMEM((bsz * HEADS, HD, HD), F32)],
        compiler_params=_params(("arbitrary",)))(proj, proj, proj, proj, w2, bg, states, do, dsm_dn)


Z_BLK, GG_BLK = P_Z // 512, P_GG // 512


def _mix_out_fwd(o_dn, o_gla, proj, grow_dn, grow_gla):
    bsz, t_total, _ = o_dn.shape
    tt = _div_tile(t_total, 256)

    def body(od_ref, og_ref, z_ref, gg_ref, gd_ref, gl_ref, o_ref):
        o_ref[0, :, :512] = _gate_norm(od_ref[0], z_ref[0], gd_ref[...]).astype(MXU_DT)
        o_ref[0, :, 512:] = _gate_norm(og_ref[0], gg_ref[0], gl_ref[...]).astype(MXU_DT)

    def col(blk):
        return pl.BlockSpec((1, tt, 512), lambda b, t: (b, t, blk))

    return pl.pallas_call(
        body, name="mix_out_fwd", grid=(bsz, t_total // tt),
        in_specs=[col(0), col(0), col(Z_BLK), col(GG_BLK), _vec_spec(512), _vec_spec(512)],
        out_specs=_tok_spec(tt), out_shape=jax.ShapeDtypeStruct((bsz, t_total, D), MXU_DT),
        compiler_params=_params(("parallel", "parallel")))(o_dn, o_gla, proj, proj, grow_dn, grow_gla)


def _mix_out_bwd(do, o_dn, o_gla, proj, grow_dn, grow_gla):
    bsz, t_total, _ = o_dn.shape
    tt = _div_tile(t_total, 256)

    def body(do_ref, od_ref, og_ref, z_ref, gg_ref, gd_ref, gl_ref,
             dod_ref, dog_ref, dz_ref, dgg_ref, dgd_ref, dgl_ref):
        @pl.when((pl.program_id(0) == 0) & (pl.program_id(1) == 0))
        def _():
            dgd_ref[...] = jnp.zeros_like(dgd_ref)
            dgl_ref[...] = jnp.zeros_like(dgl_ref)

        def one(o_ref, gate_ref, g_ref, ct, do_out, dgate_out, dg_out):
            _, vjp = jax.vjp(_gate_norm, o_ref[0], gate_ref[0], g_ref[...])
            d_o, d_gate, d_row = vjp(ct)
            do_out[0] = d_o
            dgate_out[0] = d_gate.astype(MXU_DT)
            acc = d_row[:, :HD]
            for h in range(1, HEADS):
                acc = acc + d_row[:, h * HD:(h + 1) * HD]
            dg_out[...] += acc

        ct = do_ref[0]
        one(od_ref, z_ref, gd_ref, ct[:, :512], dod_ref, dz_ref, dgd_ref)
        one(og_ref, gg_ref, gl_ref, ct[:, 512:], dog_ref, dgg_ref, dgl_ref)

    def col(blk):
        return pl.BlockSpec((1, tt, 512), lambda b, t: (b, t, blk))

    f512 = jax.ShapeDtypeStruct((bsz, t_total, 512), F32)
    b512 = jax.ShapeDtypeStruct((bsz, t_total, 512), MXU_DT)
    g128 = jax.ShapeDtypeStruct((1, HD), F32)
    return pl.pallas_call(
        body, name="mix_out_bwd", grid=(bsz, t_total // tt),
        in_specs=[_tok_spec(tt), col(0), col(0), col(Z_BLK), col(GG_BLK), _vec_spec(512), _vec_spec(512)],
        out_specs=(col(0), col(0), col(0), col(0), _vec_spec(HD), _vec_spec(HD)),
        out_shape=(f512, f512, b512, b512, g128, g128),
        compiler_params=_params(("arbitrary", "arbitrary")))(do, o_dn, o_gla, proj, proj, grow_dn, grow_gla)


def _sum_slots(x, name):
    n, rows, cols = x.shape
    tr = _div_tile(rows, max(8, (1 << 19) // cols))

    def body(x_ref, o_ref):
        acc = x_ref[0].astype(F32)
        for i in range(1, n):
            acc = acc + x_ref[i].astype(F32)
        o_ref[...] = acc

    return pl.pallas_call(
        body, name=name, grid=(rows // tr,),
        in_specs=[pl.BlockSpec((n, tr, cols), lambda i: (0, i, 0))],
        out_specs=pl.BlockSpec((tr, cols), lambda i: (i, 0)),
        out_shape=jax.ShapeDtypeStruct((rows, cols), F32), compiler_params=_params(("parallel",)))(x)


def _pair_add(g8, r1, core, name):
    _, rows, cols = g8.shape
    tr = _div_tile(rows, max(8, (1 << 19) // cols))
    g42 = g8.reshape(4, 2, rows, cols)

    def body(core_ref, g_ref, r_ref, o_ref):
        o_ref[0] = (g_ref[0, 0].astype(F32) + r_ref[0].astype(F32)).astype(o_ref.dtype)

    return pl.pallas_call(
        body, name=name,
        grid_spec=pltpu.PrefetchScalarGridSpec(
            num_scalar_prefetch=1, grid=(4, rows // tr),
            in_specs=[pl.BlockSpec((1, 1, tr, cols), lambda s, i, core_ref: (s, core_ref[0], i, 0)),
                      pl.BlockSpec((1, tr, cols), lambda s, i, core_ref: (s, i, 0))],
            out_specs=pl.BlockSpec((1, tr, cols), lambda s, i, core_ref: (s, i, 0))),
        out_shape=jax.ShapeDtypeStruct((4, rows, cols), g8.dtype),
        compiler_params=_params(("parallel", "parallel")))(core, g42, r1)


def _chip_add(p4, r2, chip, name):
    _, rows, cols = p4.shape
    tr = _div_tile(rows, max(8, (1 << 19) // cols))

    def body(chip_ref, p_ref, r_ref, o_ref):
        f = lambda a: a.astype(F32)
        o_ref[...] = ((f(p_ref[0]) + f(r_ref[0])) + f(r_ref[1])) + f(r_ref[2])

    return pl.pallas_call(
        body, name=name,
        grid_spec=pltpu.PrefetchScalarGridSpec(
            num_scalar_prefetch=1, grid=(rows // tr,),
            in_specs=[pl.BlockSpec((1, tr, cols), lambda i, chip_ref: (chip_ref[0], i, 0)),
                      pl.BlockSpec((3, tr, cols), lambda i, chip_ref: (0, i, 0))],
            out_specs=pl.BlockSpec((tr, cols), lambda i, chip_ref: (i, 0))),
        out_shape=jax.ShapeDtypeStruct((rows, cols), F32),
        compiler_params=_params(("parallel",)))(chip, p4, r2)


def _adamw(w, g, m, v, name):
    rows, cols = w.shape
    tr = _div_tile(rows, max(8, (1 << 18) // cols))

    def body(w_ref, g_ref, m_ref, v_ref, d_ref, nm_ref, nv_ref):
        g_ = g_ref[...]
        nm = ADAM_B1 * m_ref[...] + (1.0 - ADAM_B1) * g_
        nv = ADAM_B2 * v_ref[...] + (1.0 - ADAM_B2) * (g_ * g_)
        m_hat = nm / (1.0 - ADAM_B1 ** ADAM_STEP)
        v_hat = nv / (1.0 - ADAM_B2 ** ADAM_STEP)
        d_ref[...] = -ADAM_LR * (m_hat / (jnp.sqrt(v_hat) + ADAM_EPS) + ADAM_WD * w_ref[...])
        nm_ref[...] = nm
        nv_ref[...] = nv

    spec = pl.BlockSpec((tr, cols), lambda i: (i, 0))
    shp = jax.ShapeDtypeStruct((rows, cols), F32)
    return pl.pallas_call(body, name=name, grid=(rows // tr,), in_specs=[spec] * 4, out_specs=(spec,) * 3,
                          out_shape=(shp,) * 3, compiler_params=_params(("parallel",)))(w, g, m, v)


def _position():
    return lax.axis_index("x"), lax.axis_index("y"), lax.axis_index("c")


def _slot(px, py, pc):
    return 4 * px + 2 * py + pc


def _gather_small(x, name):
    rows, cols = x.shape

    def body(x_ref, o_ref, send_sems, recv_sems):
        mx, my, mc = _position()

        def peer(k):
            return (mx ^ ((k >> 2) & 1), my ^ ((k >> 1) & 1), mc ^ (k & 1))

        o_ref[_slot(mx, my, mc)] = x_ref[...]
        sends = []
        for k in range(1, N_DEV):
            cp = pltpu.make_async_remote_copy(src_ref=x_ref, dst_ref=o_ref.at[_slot(mx, my, mc)],
                                              send_sem=send_sems.at[k - 1], recv_sem=recv_sems.at[k - 1],
                                              device_id=peer(k), device_id_type=MESH)
            cp.start()
            sends.append(cp)
        for k in range(1, N_DEV):
            pltpu.make_async_remote_copy(src_ref=x_ref, dst_ref=o_ref.at[_slot(*peer(k))],
                                         send_sem=send_sems.at[k - 1], recv_sem=recv_sems.at[k - 1],
                                         device_id=peer(k), device_id_type=MESH).wait_recv()
        for cp in sends:
            cp.wait_send()

    return pl.pallas_call(
        body, name=name, out_shape=jax.ShapeDtypeStruct((N_DEV, rows, cols), x.dtype),
        in_specs=[pl.BlockSpec(memory_space=pltpu.VMEM)], out_specs=pl.BlockSpec(memory_space=pltpu.VMEM),
        scratch_shapes=[pltpu.SemaphoreType.DMA((N_DEV - 1,)), pltpu.SemaphoreType.DMA((N_DEV - 1,))],
        compiler_params=pltpu.CompilerParams(vmem_limit_bytes=VMEM_LIMIT_V7X))(x)


def _gather_big(shards):
    n = len(shards)

    def body(*refs):
        xs, outs = refs[:n], refs[n:2 * n]
        send_sems, recv_sems, local_sems = refs[2 * n:]
        mx, my, mc = _position()
        me, sibling = (mx, my, mc), (mx, my, 1 - mc)
        chips = [(1 - mx, my), (mx, 1 - my), (1 - mx, 1 - my)]

        def copy(a, k, block, to, src=None):
            dst = outs[a].at[_slot(*block)]
            return pltpu.make_async_remote_copy(src_ref=dst if src is None else src, dst_ref=dst,
                                                send_sem=send_sems.at[7 * a + k], recv_sem=recv_sems.at[7 * a + k],
                                                device_id=to, device_id_type=MESH)

        mine = [pltpu.make_async_copy(xs[a], outs[a].at[_slot(*me)], local_sems.at[a]) for a in range(n)]
        for cp in mine:
            cp.start()
        started = []
        for a in range(n):
            started.append(copy(a, 0, me, sibling, src=xs[a]))
            started += [copy(a, 1 + j, me, (*chip, mc), src=xs[a]) for j, chip in enumerate(chips)]
        for cp in started:
            cp.start()
        for j, chip in enumerate(chips):
            for a in range(n):
                copy(a, 1 + j, (*chip, mc), me).wait_recv()
                fwd = copy(a, 4 + j, (*chip, mc), sibling)
                fwd.start()
                started.append(fwd)
        for a in range(n):
            copy(a, 0, sibling, me).wait_recv()
            for j, chip in enumerate(chips):
                copy(a, 4 + j, (*chip, 1 - mc), me).wait_recv()
        for cp in started:
            cp.wait_send()
        for cp in mine:
            cp.wait()

    any_spec = pl.BlockSpec(memory_space=pl.ANY)
    return pl.pallas_call(
        body, name="gather_weights",
        out_shape=tuple(jax.ShapeDtypeStruct((N_DEV,) + s.shape, s.dtype) for s in shards),
        in_specs=[any_spec] * n, out_specs=(any_spec,) * n,
        scratch_shapes=[pltpu.SemaphoreType.DMA((7 * n,)), pltpu.SemaphoreType.DMA((7 * n,)),
                        pltpu.SemaphoreType.DMA((n,))])(*shards)


def _scatter_sibling(grads):
    n = len(grads)

    def body(*refs):
        gs, outs = refs[:n], refs[n:2 * n]
        send_sems, recv_sems = refs[2 * n:]
        mx, my, mc = _position()
        copies = []
        for a in range(n):
            for s in range(4):
                copies.append(pltpu.make_async_remote_copy(
                    src_ref=gs[a].at[2 * s + (1 - mc)], dst_ref=outs[a].at[s],
                    send_sem=send_sems.at[4 * a + s], recv_sem=recv_sems.at[4 * a + s],
                    device_id=(mx, my, 1 - mc), device_id_type=MESH))
        for cp in copies:
            cp.start()
        for cp in copies:
            cp.wait_recv()
        for cp in copies:
            cp.wait_send()

    any_spec = pl.BlockSpec(memory_space=pl.ANY)
    return pl.pallas_call(
        body, name="scatter_sibling",
        out_shape=tuple(jax.ShapeDtypeStruct((4,) + g.shape[1:], g.dtype) for g in grads),
        in_specs=[any_spec] * n, out_specs=(any_spec,) * n,
        scratch_shapes=[pltpu.SemaphoreType.DMA((4 * n,)), pltpu.SemaphoreType.DMA((4 * n,))])(*grads)


def _scatter_chips(sums):
    n = len(sums)

    def body(*refs):
        ps, outs = refs[:n], refs[n:2 * n]
        send_sems, recv_sems = refs[2 * n:]
        mx, my, mc = _position()
        chips = [(1 - mx, my), (mx, 1 - my), (1 - mx, 1 - my)]
        copies = []
        for a in range(n):
            for k, (cx, cy) in enumerate(chips):
                copies.append(pltpu.make_async_remote_copy(
                    src_ref=ps[a].at[2 * cx + cy], dst_ref=outs[a].at[k],
                    send_sem=send_sems.at[3 * a + k], recv_sem=recv_sems.at[3 * a + k],
                    device_id=(cx, cy, mc), device_id_type=MESH))
        for cp in copies:
            cp.start()
        for cp in copies:
            cp.wait_recv()
        for cp in copies:
            cp.wait_send()

    any_spec = pl.BlockSpec(memory_space=pl.ANY)
    return pl.pallas_call(
        body, name="scatter_chips",
        out_shape=tuple(jax.ShapeDtypeStruct((3,) + p.shape[1:], p.dtype) for p in sums),
        in_specs=[any_spec] * n, out_specs=(any_spec,) * n,
        scratch_shapes=[pltpu.SemaphoreType.DMA((3 * n,)), pltpu.SemaphoreType.DMA((3 * n,))])(*sums)


def _peer(pos, k):
    mx, my, mc = pos
    return (mx ^ ((k >> 2) & 1), my ^ ((k >> 1) & 1), mc ^ (k & 1))


def _exchange_copies(srcs, lands, send_sems, recv_sems, by_owner):
    pos = _position()
    me = _slot(*pos)
    out = []
    for a, (src, land) in enumerate(zip(srcs, lands)):
        for k in range(1, N_DEV):
            peer = _peer(pos, k)
            sems = dict(send_sem=send_sems.at[7 * a + k - 1], recv_sem=recv_sems.at[7 * a + k - 1],
                        device_id=peer, device_id_type=MESH)
            mine = src.at[_slot(*peer)] if by_owner else src
            send = pltpu.make_async_remote_copy(src_ref=mine, dst_ref=land.at[me], **sems)
            recv = pltpu.make_async_remote_copy(src_ref=mine, dst_ref=land.at[_slot(*peer)], **sems)
            out.append((send, recv))
    return out


_HBM_SPEC = pl.BlockSpec(memory_space=pltpu.HBM)
_SEM_SPEC = pl.BlockSpec(memory_space=pltpu.SEMAPHORE)
_DATAFLOW = pltpu.SideEffectType.DATAFLOW_SIDE_EFFECTING


def _exchange_start(name, srcs, slab_shapes, after, by_owner):
    n, na = len(srcs), len(after)
    lands = [pltpu.with_memory_space_constraint(lax.empty((N_DEV,) + s, x.dtype), pltpu.HBM)
             for s, x in zip(slab_shapes, srcs)]
    srcs = [pltpu.with_memory_space_constraint(x, pltpu.HBM) for x in srcs]

    def body(*refs):
        src_refs, land_refs = refs[:n], refs[n:2 * n]
        send_sems, recv_sems = refs[2 * n + na], refs[2 * n + na + 1]
        token = refs[-1]
        for send, _ in _exchange_copies(src_refs, land_refs, send_sems, recv_sems, by_owner):
            send.start()
        token[...] = jnp.zeros_like(token)

    outs = pl.pallas_call(
        body, name=name,
        out_shape=(pltpu.SemaphoreType.DMA((7 * n,)), pltpu.SemaphoreType.DMA((7 * n,)),
                   *[pltpu.HBM(x.shape, x.dtype) for x in srcs], *[pltpu.HBM(l.shape, l.dtype) for l in lands],
                   jax.ShapeDtypeStruct((8, 128), F32)),
        in_specs=[_HBM_SPEC] * (2 * n) + [pl.BlockSpec(memory_space=pl.ANY)] * na,
        out_specs=(_SEM_SPEC, _SEM_SPEC, *[_HBM_SPEC] * (2 * n), pl.BlockSpec(memory_space=pltpu.VMEM)),
        input_output_aliases={i: 2 + i for i in range(2 * n)},
        compiler_params=pltpu.CompilerParams(has_side_effects=_DATAFLOW))(*srcs, *lands, *after)
    return outs[0], outs[1], list(outs[2:2 + n]), list(outs[2 + n:2 + 2 * n]), outs[-1]


def _exchange_wait(name, send_sems, recv_sems, srcs, lands, after, by_owner):
    n = len(srcs)

    def body(*refs):
        src_refs, land_refs = refs[:n], refs[n:2 * n]
        s_sems, r_sems = refs[2 * n], refs[2 * n + 1]
        for send, recv in _exchange_copies(src_refs, land_refs, s_sems, r_sems, by_owner):
            send.wait_send()
            recv.wait_recv()

    outs = pl.pallas_call(
        body, name=name,
        out_shape=(*[pltpu.HBM(x.shape, x.dtype) for x in srcs], *[pltpu.HBM(l.shape, l.dtype) for l in lands]),
        in_specs=[_HBM_SPEC] * (2 * n) + [_SEM_SPEC, _SEM_SPEC, pl.BlockSpec(memory_space=pl.ANY)],
        out_specs=tuple([_HBM_SPEC] * (2 * n)),
        input_output_aliases={i: i for i in range(2 * n)},
        compiler_params=pltpu.CompilerParams(has_side_effects=_DATAFLOW))(*srcs, *lands, send_sems, recv_sems, after)
    return list(outs[n:])


def _pad_heads(x, axis):
    shp = list(x.shape)
    x4 = x.reshape(shp[:axis] + [HEADS, GLA_KEY] + shp[axis + 1:])
    pad = [(0, 0)] * x4.ndim
    pad[axis + 1] = (0, HD - GLA_KEY)
    return jnp.pad(x4, pad).reshape(shp[:axis] + [HEADS * HD] + shp[axis + 1:])


def _unpad_heads(x, axis):
    shp = list(x.shape)
    x4 = x.reshape(shp[:axis] + [HEADS, HD] + shp[axis + 1:])
    x4 = lax.slice_in_dim(x4, 0, GLA_KEY, axis=axis + 1)
    return x4.reshape(shp[:axis] + [HEADS * GLA_KEY] + shp[axis + 1:])


O_Z_END, O_AB, O_GQ, O_GK, O_GV, O_R = 2048, 2048, 2056, 2312, 2568, 3592


def _pad_in_rows(wt):
    return jnp.concatenate([
        wt[:O_Z_END], _pad_heads(wt[O_GQ:O_GK], 0), _pad_heads(wt[O_GK:O_GV], 0), wt[O_GV:O_R],
        wt[O_AB:O_GQ], wt[O_R:], jnp.zeros((P_W - P_SM - 8 - GATE_RANK, wt.shape[1]), wt.dtype)], axis=0)


def _unpad_in_rows(gt):
    return jnp.concatenate([
        gt[:P_GQ], gt[P_SM:P_SM + 8], _unpad_heads(gt[P_GQ:P_GK], 0), _unpad_heads(gt[P_GK:P_GV], 0),
        gt[P_GV:P_SM], gt[P_SM + 8:P_SM + 8 + GATE_RANK]], axis=0)


def _lane_row(vals, width=128):
    return jnp.pad(vals.reshape(1, -1), ((0, 0), (0, width - vals.size)))


SMALL_NAMES = ["ln0_g", "ln0_b", "b_ada", "dn_conv", "dn_a_log", "dn_dt_bias", "dn_norm_g", "gla_w_gate2",
               "gla_b_gate", "gla_norm_g", "ln1_g", "ln1_b", "ffn_conv", "ffn_conv_b", "ln2_g", "ln2_b"]
WEIGHTS = ["ln0_g", "ln0_b", "w_ada", "b_ada", "w_in", "dn_conv", "dn_a_log", "dn_dt_bias", "dn_norm_g",
           "gla_w_gate2", "gla_b_gate", "gla_norm_g", "w_o", "ln1_g", "ln1_b", "ffn_w_up", "ffn_conv", "ffn_conv_b",
           "ffn_w_down", "ln2_g", "ln2_b"]


def kernel(x, c, ln0_g, ln0_b, w_ada, b_ada, w_in, dn_conv, dn_a_log, dn_dt_bias, dn_norm_g, gla_w_gate2, gla_b_gate, gla_norm_g, w_o, ln1_g, ln1_b, ffn_w_up, ffn_conv, ffn_conv_b, ffn_w_down, ln2_g, ln2_b, loss_target, m_ln0_g, m_ln0_b, m_w_ada, m_b_ada, m_w_in, m_dn_conv, m_dn_a_log, m_dn_dt_bias, m_dn_norm_g, m_gla_w_gate2, m_gla_b_gate, m_gla_norm_g, m_w_o, m_ln1_g, m_ln1_b, m_ffn_w_up, m_ffn_conv, m_ffn_conv_b, m_ffn_w_down, m_ln2_g, m_ln2_b, v_ln0_g, v_ln0_b, v_w_ada, v_b_ada, v_w_in, v_dn_conv, v_dn_a_log, v_dn_dt_bias, v_dn_norm_g, v_gla_w_gate2, v_gla_b_gate, v_gla_norm_g, v_w_o, v_ln1_g, v_ln1_b, v_ffn_w_up, v_ffn_conv, v_ffn_conv_b, v_ffn_w_down, v_ln2_g, v_ln2_b):
    args = dict(locals())
    w_given = {n: args[n] for n in WEIGHTS}
    m_given = {n: args["m_" + n] for n in WEIGHTS}
    v_given = {n: args["v_" + n] for n in WEIGHTS}
    bsz, t_total, _ = x.shape
    ntok = bsz * t_total
    mx, my, mc = _position()
    me = _slot(mx, my, mc)

    pack1 = jnp.concatenate([c.reshape(-1), dn_conv.reshape(-1), gla_w_gate2.reshape(-1), ffn_conv.reshape(-1)])
    n1 = pack1.size
    rows1 = -(-n1 // 1024) * 8
    pack1 = jnp.pad(pack1, (0, rows1 * 128 - n1)).reshape(rows1, 128)
    got1 = _gather_small(pack1, "gather_cond").reshape(N_DEV, -1)
    o1 = bsz * D
    o2 = o1 + dn_conv.size
    o3 = o2 + gla_w_gate2.size
    c_all = got1[:, :o1].reshape(N_DEV * bsz, D)
    dn_conv_f = got1[:, o1:o2].reshape(N_DEV, DN_CONV_K, -1).transpose(1, 0, 2).reshape(DN_CONV_K, QKV_W)
    gate2_f = got1[:, o2:o3].reshape(N_DEV, GATE_RANK, -1).transpose(1, 0, 2).reshape(GATE_RANK, HEADS * GLA_KEY)
    ffn_conv_f = got1[:, o3:n1].reshape(N_DEV, FFN_CONV_K, -1).transpose(1, 0, 2).reshape(FFN_CONV_K, 2 * D_FF)

    win_t = w_in[0].T.astype(MXU_DT)
    wup_t = ffn_w_up[0].T.astype(MXU_DT)
    (win_all,) = _gather_big([win_t])
    win_p = _pad_in_rows(win_all.reshape(IN_W, D))
    cw_p, cb_p = _ffn_pair(ffn_conv_f, 1), _ffn_pair(ffn_conv_b, 1)

    ncol = w_ada.shape[2]
    b_cols = lax.dynamic_slice_in_dim(b_ada, me * ncol, ncol, axis=1)
    mod_part = _ada_fwd(c_all, w_ada[0], b_cols)
    mod_all = _gather_small(mod_part.reshape(-1, 128), "gather_mod").reshape(N_DEV, N_DEV * bsz, ncol)
    mod = lax.dynamic_slice_in_dim(mod_all, me * bsz, bsz, axis=1).transpose(1, 0, 2).reshape(bsz, 6, 1, D)
    late = [w_o[0].astype(MXU_DT), wup_t, ffn_w_down[0].astype(MXU_DT)]
    ag_send, ag_recv, ag_src, ag_land, ag_token = _exchange_start(
        "gather_start", late, [w.shape for w in late], [win_all, mod_all], by_owner=False)
    mod = mod + ag_token[0, 0]
    sh_a, sc_a, gt_a, sh_f, sc_f, gt_f = (mod[:, i] for i in range(6))

    g0, b0 = ln0_g.reshape(1, D), ln0_b.reshape(1, D)
    alog_row, dt_row = _lane_row(dn_a_log[0]), _lane_row(dn_dt_bias[0])
    grow_dn, grow_gla = jnp.tile(dn_norm_g, (1, HEADS)), jnp.tile(gla_norm_g, (1, HEADS))
    w2 = jnp.zeros((128, HEADS * HD), F32).at[SM_R:SM_R + GATE_RANK].set(_pad_heads(gate2_f, 1))
    bg = _pad_heads(gla_b_gate, 1)

    h_a = _ln0_mod(x, g0, b0, sc_a, sh_a)
    proj = _mm(h_a.reshape(ntok, D), win_p, "nt", F32, "mm_proj", tm=1024, tn=1408).reshape(bsz, t_total, P_W)
    q, k, v, gates = _dn_pre_fwd(proj, dn_conv_f, alog_row, dt_row)
    o_dn, s_dn, inv_dn = _dn_rec_fwd(q, k, v, gates)
    o_gla, s_gla = _gla_rec_fwd(proj, w2, bg)
    o_mix = _mix_out_fwd(o_dn, o_gla, proj, grow_dn, grow_gla)
    landed = _exchange_wait("gather_wait", ag_send, ag_recv, ag_src, ag_land, o_mix, by_owner=False)
    wo_all, wup_all, wdn_all = (lax.dynamic_update_slice(l, w[None], (me, 0, 0)) for l, w in zip(landed, late))
    wo_f = wo_all.reshape(D, D)
    wup_f = _ffn_pair(wup_all.reshape(2 * D_FF, D), 0)
    wdn_f = wdn_all.reshape(D_FF, D)
    y = _mm(o_mix.reshape(ntok, D), wo_f, "nn", F32, "mm_wo", tm=1024, tn=1024).reshape(bsz, t_total, D)
    r1, h_f = _res_ln_mod(x, y, gt_a, g0, b0, ln1_g, ln1_b, sc_f, sh_f)
    up = _mm(h_f.reshape(ntok, D), wup_f, "nt", F32, "mm_up", tm=1024, tn=1408).reshape(bsz, t_total, 2 * D_FF)
    act = _ffn_act_fwd(up, cw_p, cb_p)
    y2 = _mm(act.reshape(ntok, D_FF), wdn_f, "nn", F32, "mm_down", tm=1024, tn=1024).reshape(bsz, t_total, D)
    loss_rows, dr2, dy2, dgt_f, d_ln2_g, d_ln2_b = _final_fwd_bwd(r1, y2, gt_f, ln1_g, ln1_b, ln2_g, ln2_b, loss_target)
    loss = lax.psum(0.5 * jnp.sum(loss_rows) / D, ("x", "y", "c"))

    dy2_2 = dy2.reshape(ntok, D)
    dact = _mm(dy2_2, wdn_f, "nt", F32, "mm_dact", tm=1024, tn=1408).reshape(bsz, t_total, D_FF)
    g_wdn = _mm(act.reshape(ntok, D_FF), dy2_2, "tn", MXU_DT, "mm_gwdn", tm=1408, tn=1024)
    dup, d_cw_p, d_cb_p = _ffn_act_bwd(up, dact, cw_p, cb_p)
    d_ffn_conv, d_ffn_conv_b = _ffn_unpair(d_cw_p, 1), _ffn_unpair(d_cb_p, 1)
    dup_2 = dup.reshape(ntok, 2 * D_FF)
    dh_f = _mm(dup_2, wup_f, "nn", F32, "mm_dhf", tn=1024).reshape(bsz, t_total, D)
    g_wup_t = _mm(dup_2, h_f.reshape(ntok, D), "tn", MXU_DT, "mm_gwup", tm=1408, tn=1024)
    ffn_parts = [_ffn_unpair(g_wup_t, 0).reshape(N_DEV, -1, D), g_wdn.reshape(N_DEV, -1, D)]
    rs_send, rs_recv, rs_src, rs_land, rs_token = _exchange_start(
        "scatter_start", ffn_parts, [p.shape[1:] for p in ffn_parts], [dh_f], by_owner=True)
    dr1, dsc_f, dsh_f, d_ln1_g, d_ln1_b, dy, dgt_a = _ln_bwd_call(
        "ln1_bwd", dr2, dh_f, r1, ln1_g, ln1_b, sc_f + rs_token[0, 0], y=y, gt=gt_a)

    dy_2 = dy.reshape(ntok, D)
    do = _mm(dy_2, wo_f, "nt", F32, "mm_do", tm=1024, tn=1024).reshape(bsz, t_total, D)
    g_wo = _mm(o_mix.reshape(ntok, D), dy_2, "tn", MXU_DT, "mm_gwo", tm=512, tn=1024)
    do_dn, do_gla, dz, dgg, d_dn_norm, d_gla_norm = _mix_out_bwd(do, o_dn, o_gla, proj, grow_dn, grow_gla)
    dq, dk, dv, dgates = _dn_rec_bwd(q, k, v, gates, s_dn, inv_dn, do_dn)
    dqkv, dsm_dn, d_dn_conv, d_alog_row, d_dt_row = _dn_pre_bwd(proj, dq, dk, dv, dgates, dn_conv_f, alog_row, dt_row)
    dgq, dgk, dgv, dsm, d_w2, d_bg = _gla_rec_bwd(proj, w2, bg, s_gla, do_gla, dsm_dn)
    dproj = jnp.concatenate([dqkv, dz, dgq, dgk, dgv, dgg, dsm], axis=-1).reshape(ntok, P_W)
    dh_a = _mm(dproj, win_p, "nn", F32, "mm_dha", tn=1024).reshape(bsz, t_total, D)
    g_win_p = _mm(dproj, h_a.reshape(ntok, D), "tn", MXU_DT, "mm_gwin", tm=1408, tn=1024)
    grad_x, dsc_a, dsh_a, d_ln0_g, d_ln0_b = _ln_bwd_call("ln0_bwd", dr1, dh_a, x, g0, b0, sc_a)

    big = [_unpad_in_rows(g_win_p).reshape(N_DEV, -1, D), g_wo.reshape(N_DEV, -1, D)]
    from_sibling = _scatter_sibling(big)
    core = mc.reshape(1).astype(jnp.int32)
    chip_sums = [_pair_add(g8, r1_, core, f"pair_add_{i}") for i, (g8, r1_) in enumerate(zip(big, from_sibling))]
    from_chips = _scatter_chips(chip_sums)
    chip = (2 * mx + my).reshape(1).astype(jnp.int32)
    g_win_t, g_wo_s = (
        _chip_add(p4, r2_, chip, f"chip_add_{i}") for i, (p4, r2_) in enumerate(zip(chip_sums, from_chips)))
    ffn_landed = _exchange_wait("scatter_wait", rs_send, rs_recv, rs_src, rs_land, grad_x, by_owner=True)
    g_wup_ts, g_wdn_s = (
        _sum_slots(lax.dynamic_update_slice(l, lax.dynamic_slice_in_dim(p, me, 1, axis=0), (me, 0, 0)), f"sum_ffn_{i}")
        for i, (l, p) in enumerate(zip(ffn_landed, ffn_parts)))

    dmod = jnp.concatenate([dsh_a, dsc_a, dgt_a, dsh_f, dsc_f, dgt_f], axis=1).reshape(-1)
    small_parts = {
        "ln0_g": d_ln0_g, "ln0_b": d_ln0_b, "ln1_g": d_ln1_g, "ln1_b": d_ln1_b, "ln2_g": d_ln2_g, "ln2_b": d_ln2_b,
        "dn_a_log": d_alog_row[:, :HEADS], "dn_dt_bias": d_dt_row[:, :HEADS],
        "dn_norm_g": d_dn_norm, "gla_norm_g": d_gla_norm, "gla_b_gate": _unpad_heads(d_bg, 1),
        "ffn_conv_b": d_ffn_conv_b, "dn_conv": d_dn_conv,
        "gla_w_gate2": _unpad_heads(d_w2[SM_R:SM_R + GATE_RANK], 1), "ffn_conv": d_ffn_conv}
    order = sorted(small_parts)
    flat = jnp.concatenate([small_parts[n].reshape(-1) for n in order] + [dmod])
    n3 = flat.size
    rows3 = -(-n3 // 1024) * 8
    pack3 = jnp.pad(flat, (0, rows3 * 128 - n3)).reshape(rows3, 128)
    got3 = _gather_small(pack3, "gather_small_grads")
    tot3 = _sum_slots(got3, "sum_small_grads").reshape(-1)
    grads = {}
    off = 0
    for n in order:
        size = small_parts[n].size
        grads[n] = tot3[off:off + size]
        off += size
    dmod_all = got3.reshape(N_DEV, -1)[:, off:off + dmod.size].reshape(N_DEV * bsz, 6 * D)
    dmod_cols = lax.dynamic_slice_in_dim(dmod_all, me * ncol, ncol, axis=1)
    g_wada, g_bada = _ada_bwd(c_all, dmod_all, dmod_cols)
    grads["b_ada"] = g_bada

    def col_shard(full, rows):
        part = full.reshape(rows, -1)
        width = part.shape[1] // N_DEV
        return lax.dynamic_slice_in_dim(part, me * width, width, axis=1)

    grads["dn_conv"] = col_shard(grads["dn_conv"], DN_CONV_K)
    grads["gla_w_gate2"] = col_shard(grads["gla_w_gate2"], GATE_RANK)
    grads["ffn_conv"] = col_shard(grads["ffn_conv"], FFN_CONV_K)
    grads = {n: g.reshape(w_given[n].shape) for n, g in grads.items()}
    grads["w_ada"] = g_wada.reshape(w_ada.shape)
    grads["w_in"] = g_win_t.T.reshape(w_in.shape)
    grads["w_o"] = g_wo_s.reshape(w_o.shape)
    grads["ffn_w_up"] = g_wup_ts.T.reshape(ffn_w_up.shape)
    grads["ffn_w_down"] = g_wdn_s.reshape(ffn_w_down.shape)

    delta, new_m, new_v = {}, {}, {}
    for n in ["w_ada", "w_in", "w_o", "ffn_w_up", "ffn_w_down"]:
        shp = w_given[n].shape
        two_d = lambda a: a.reshape(shp[-2], shp[-1])
        d_, m_, v_ = _adamw(two_d(w_given[n]), two_d(grads[n]), two_d(m_given[n]), two_d(v_given[n]), "adamw_" + n)
        delta[n], new_m[n], new_v[n] = d_.reshape(shp), m_.reshape(shp), v_.reshape(shp)

    def pack_small(src):
        flat_ = jnp.concatenate([src[n].reshape(-1) for n in SMALL_NAMES])
        rows_ = -(-flat_.size // 1024) * 8
        return jnp.pad(flat_, (0, rows_ * 128 - flat_.size)).reshape(rows_, 128)

    d_s, m_s, v_s = _adamw(pack_small(w_given), pack_small(grads), pack_small(m_given), pack_small(v_given),
                           "adamw_small")
    off = 0
    for n in SMALL_NAMES:
        size, shp = w_given[n].size, w_given[n].shape
        delta[n] = d_s.reshape(-1)[off:off + size].reshape(shp)
        new_m[n] = m_s.reshape(-1)[off:off + size].reshape(shp)
        new_v[n] = v_s.reshape(-1)[off:off + size].reshape(shp)
        off += size

    return (loss, grad_x, *[grads[n] for n in WEIGHTS], *[delta[n] for n in WEIGHTS],
            *[new_m[n] for n in WEIGHTS], *[new_v[n] for n in WEIGHTS])
```

```python
import functools

import jax
import jax.numpy as jnp
from jax import lax
from jax.experimental import pallas as pl
from jax.experimental.pallas import tpu as pltpu

F32 = jnp.float32
MXU_DT = jnp.bfloat16
MESH = pl.DeviceIdType.MESH
N_DEV = 8

D = 1024
HEADS = 4
HD = 128
CHUNK = 64
GLA_KEY = 64
GLA_TAU = 16.0
GATE_RANK = 16
D_FF = 2816
IN_W = 3608
ALPHA = 2.0 ** 0.25
EPS = 1e-6
DN_CONV_K = 4
FFN_CONV_K = 3
HALO = 8

P_QKV, P_Z, P_GQ, P_GK, P_GV, P_GG, P_SM, P_W = 0, 1536, 2048, 2560, 3072, 3584, 4096, 4224
SM_A, SM_B, SM_R = 0, 4, 8

ADAM_LR, ADAM_B1, ADAM_B2, ADAM_EPS, ADAM_WD, ADAM_STEP = 0.001, 0.9, 0.999, 1e-08, 0.01, 10

VMEM_LIMIT_V7X = 56 * 1024 * 1024


def _params(sem=None):
    return pltpu.CompilerParams(dimension_semantics=sem, vmem_limit_bytes=VMEM_LIMIT_V7X)


def _dg(a, b, dims, prec=None):
    return lax.dot_general(a, b, (dims, ((), ())), precision=prec, preferred_element_type=F32)


def _dot(a, b, prec=None):
    return _dg(a, b, ((1,), (0,)), prec)


def _dot_nt(a, b, prec=None):
    return _dg(a, b, ((1,), (1,)), prec)


def _dot_tn(a, b, prec=None):
    return _dg(a, b, ((0,), (0,)), prec)


def _iota(shape, dim):
    return lax.broadcasted_iota(jnp.int32, shape, dim)


def _sigmoid(x):
    return jax.nn.sigmoid(x)


def _silu(x):
    return x * _sigmoid(x)


def _softplus(x):
    return jnp.maximum(x, 0.0) + jnp.log(1.0 + jnp.exp(-jnp.abs(x)))


def _ln_stats(x):
    mu = jnp.mean(x, axis=-1, keepdims=True)
    xc = x - mu
    rstd = lax.rsqrt(jnp.mean(xc * xc, axis=-1, keepdims=True) + EPS)
    return xc * rstd, rstd


def _ln_bwd(dxhat, xhat, rstd):
    return rstd * (dxhat - jnp.mean(dxhat, axis=-1, keepdims=True)
                   - xhat * jnp.mean(dxhat * xhat, axis=-1, keepdims=True))


NN, NT, TN = ((1,), (0,)), ((1,), (1,)), ((0,), (0,))


def _split2(a):
    hi = a.astype(jnp.bfloat16)
    return hi, (a - hi.astype(F32)).astype(jnp.bfloat16)


def _d3(a, b, dims):
    ah, al = _split2(a)
    bh, bl = _split2(b)
    return _dg(ah, bh, dims) + (_dg(ah, bl, dims) + _dg(al, bh, dims))


@jax.custom_vjp
def _dot3(a, b):
    return _d3(a, b, NN)


_dot3.defvjp(lambda a, b: (_d3(a, b, NN), (a, b)),
             lambda res, g: (_d3(g, res[1], NT), _d3(res[0], g, TN)))


def _split3(b):
    b1 = b.astype(jnp.bfloat16)
    r1 = b - b1.astype(F32)
    b2 = r1.astype(jnp.bfloat16)
    return b1, b2, (r1 - b2.astype(F32)).astype(jnp.bfloat16)


def _sum3(fn, b):
    b1, b2, b3 = _split3(b)
    return fn(b1) + (fn(b2) + fn(b3))


@jax.custom_vjp
def _mask_dot(e, b):
    return _sum3(lambda t: _dg(e, t, NN), b)


_mask_dot.defvjp(lambda e, b: (_mask_dot(e, b), e),
                 lambda e, g: (jnp.zeros_like(e), _sum3(lambda t: _dg(e, t, TN), g)))


@jax.custom_vjp
def _mask_dot_nt(e, b):
    return _sum3(lambda t: _dg(e, t, NT), b)


_mask_dot_nt.defvjp(lambda e, b: (_mask_dot_nt(e, b), e),
                    lambda e, g: (jnp.zeros_like(e), _sum3(lambda t: _dg(t, e, TN), g)))


def _tri_inv_impl(ms):
    n = ms[0].shape[0]
    r, c = _iota((n, n), 0), _iota((n, n), 1)
    eye = (r == c).astype(F32)
    diag = (r >> 3) == (c >> 3)
    ds = [jnp.where(diag, m, 0.0) for m in ms]
    d2s = [_d3(d, d, NN) for d in ds]
    d4s = [_d3(d2, d2, NN) for d2 in d2s]
    invs = [_d3(eye - d, eye + d2, NN) for d, d2 in zip(ds, d2s)]
    invs = [_d3(inv, eye + d4, NN) for inv, d4 in zip(invs, d4s)]
    shift = 3
    while (1 << shift) < n:
        rb, cb = r >> shift, c >> shift
        sel = ((rb & 1) == 1) & (cb == rb - 1)
        tmp = [_d3(inv, jnp.where(sel, m, 0.0), NN) for inv, m in zip(invs, ms)]
        invs = [inv - _d3(t, inv, NN) for t, inv in zip(tmp, invs)]
        shift += 1
    return invs


@jax.custom_vjp
def _tri_inv(ms):
    return _tri_inv_impl(ms)


def _tri_inv_fwd(ms):
    invs = _tri_inv_impl(ms)
    return invs, invs


def _tri_inv_bwd(invs, das):
    tmp = [_d3(a, da, TN) for a, da in zip(invs, das)]
    return ([-_d3(t, a, NT) for t, a in zip(tmp, invs)],)


_tri_inv.defvjp(_tri_inv_fwd, _tri_inv_bwd)


@jax.custom_vjp
def _tri_inv_known(ms, invs):
    return invs


_tri_inv_known.defvjp(lambda ms, invs: (invs, invs),
                      lambda invs, das: (_tri_inv_bwd(invs, das)[0], [jnp.zeros_like(a) for a in invs]))


def _dn_chunk(s_list, q, k, v, gates, inv_known=None, with_inv=False):
    nb = len(q)
    c = q[0].shape[0]
    r64, c64 = _iota((c, c), 0), _iota((c, c), 1)
    causal = r64 >= c64
    strict = r64 > c64
    tri = causal.astype(jnp.bfloat16)
    eye = (_iota((HD, HD), 0) == _iota((HD, HD), 1)).astype(jnp.bfloat16)
    lane = _iota(gates[0].shape, 1)
    lane1 = _iota((1, HD), 1)
    g_all = [_mask_dot(tri, g) for g in gates]
    g_all_t = [_mask_dot_nt(eye, g) for g in g_all]
    row = _iota(g_all_t[0].shape, 0)
    last = [jnp.sum(g, axis=0, keepdims=True) for g in gates]
    prob = [(b, h) for b in range(nb) for h in range(HEADS)]
    sl = [slice(h * HD, (h + 1) * HD) for h in range(HEADS)]
    qh = [q[b][:, sl[h]] for b, h in prob]
    kh = [k[b][:, sl[h]] for b, h in prob]
    vh = [v[b][:, sl[h]] for b, h in prob]
    s = [s_list[b][h] for b, h in prob]
    beta = [jnp.sum(jnp.where(lane == SM_B + h, gates[b], 0.0), axis=-1, keepdims=True) for b, h in prob]
    g_c = [jnp.sum(jnp.where(lane == SM_A + h, g_all[b], 0.0), axis=-1, keepdims=True) for b, h in prob]
    g_r = [jnp.sum(jnp.where(row == SM_A + h, g_all_t[b], 0.0), axis=0, keepdims=True) for b, h in prob]
    g_last = [jnp.sum(jnp.where(lane1 == SM_A + h, last[b], 0.0), axis=-1, keepdims=True) for b, h in prob]
    decay = [jnp.where(causal, jnp.exp(jnp.where(causal, gc - gr, 0.0)), 0.0) for gc, gr in zip(g_c, g_r)]
    kb = [k_ * b_ for k_, b_ in zip(kh, beta)]
    m_low = [jnp.where(strict, _dot_nt(kb_, k_) * d_, 0.0) for kb_, k_, d_ in zip(kb, kh, decay)]
    attn = [_dot_nt(q_, k_) * d_ for q_, k_, d_ in zip(qh, kh, decay)]
    a_inv = _tri_inv(m_low) if inv_known is None else _tri_inv_known(m_low, inv_known)
    eg = [jnp.exp(gc) for gc in g_c]
    uw = [_dot3(a_, jnp.concatenate([v_ * b_, kb_ * e_], axis=1))
          for a_, v_, b_, kb_, e_ in zip(a_inv, vh, beta, kb, eg)]
    v_new = [uw_[:, :HD] - _dot(uw_[:, HD:], s_) for uw_, s_ in zip(uw, s)]
    qs = [_dot(q_ * e_, s_) for q_, e_, s_ in zip(qh, eg, s)]
    o = [qs_ + _dot(a_, vn_) for qs_, a_, vn_ in zip(qs, attn, v_new)]
    k_dec = [k_ * jnp.exp(gl - gc) for k_, gl, gc in zip(kh, g_last, g_c)]
    s_new = [s_ * jnp.exp(gl) + _dot_tn(kd_, vn_) for s_, gl, kd_, vn_ in zip(s, g_last, k_dec, v_new)]
    outs = [jnp.concatenate(o[b * HEADS:(b + 1) * HEADS], axis=-1) for b in range(nb)]
    states = [s_new[b * HEADS:(b + 1) * HEADS] for b in range(nb)]
    return (outs, states, a_inv) if with_inv else (outs, states)


def _gla_chunk(st_list, q, k, v, small, w2, bg):
    nb = len(q)
    c = q[0].shape[0]
    causal = _iota((c, c), 0) >= _iota((c, c), 1)
    tri = causal.astype(jnp.bfloat16)
    la_all = [-_softplus(-(_dot(sm, w2) + bg)) * (1.0 / GLA_TAU) for sm in small]
    b_all = [_mask_dot(tri, la) for la in la_all]
    prob = [(b, h) for b in range(nb) for h in range(HEADS)]
    sl = [slice(h * HD, (h + 1) * HD) for h in range(HEADS)]
    kh = [k[b][:, sl[h]] for b, h in prob]
    vh = [v[b][:, sl[h]] for b, h in prob]
    st = [st_list[b][h] for b, h in prob]
    bc = [b_all[b][:, sl[h]] for b, h in prob]
    b_last = [jnp.sum(la_all[b][:, sl[h]], axis=0, keepdims=True) for b, h in prob]
    q_dec = [q[b][:, sl[h]] * (GLA_KEY ** -0.5) * jnp.exp(bc_) for (b, h), bc_ in zip(prob, bc)]
    attn = [jnp.where(causal, _dot_nt(qd, k_ * jnp.exp(-bc_)), 0.0) for qd, k_, bc_ in zip(q_dec, kh, bc)]
    inter = [_dot_nt(qd, st_) for qd, st_ in zip(q_dec, st)]
    o = [i_ + _dot(a_, v_) for i_, a_, v_ in zip(inter, attn, vh)]
    k_dec = [k_ * jnp.exp(bl - bc_) for k_, bl, bc_ in zip(kh, b_last, bc)]
    s_new = [st_ * jnp.exp(bl) + _dot_tn(v_, kd) for st_, bl, v_, kd in zip(st, b_last, vh, k_dec)]
    outs = [jnp.concatenate(o[b * HEADS:(b + 1) * HEADS], axis=-1) for b in range(nb)]
    return outs, [s_new[b * HEADS:(b + 1) * HEADS] for b in range(nb)]


def _dn_qkv(y):
    act = _silu(y)
    parts = []
    for i in range(2 * HEADS):
        xh = act[:, i * HD:(i + 1) * HD]
        xh = xh * lax.rsqrt(jnp.sum(xh * xh, axis=-1, keepdims=True) + EPS)
        parts.append(xh * (HD ** -0.5) if i < HEADS else xh)
    qk = jnp.concatenate(parts, axis=-1)
    return qk[:, :HEADS * HD], qk[:, HEADS * HD:], act[:, 2 * HEADS * HD:]


def _dn_gates(small, alog_row, dt_row):
    lane = _iota(small.shape, 1)
    log_a = -jnp.exp(alog_row) * _softplus(small + dt_row)
    return jnp.where(lane < SM_B, log_a, jnp.where(lane < SM_R, _sigmoid(small), 0.0))


def _gate_norm(o, z, grow):
    parts = []
    for h in range(HEADS):
        oh = o[:, h * HD:(h + 1) * HD]
        parts.append(oh * lax.rsqrt(jnp.mean(oh * oh, axis=-1, keepdims=True) + EPS))
    return jnp.concatenate(parts, axis=-1) * grow * _silu(z)


def _conv_rows(xrows, w_ref, k_taps):
    n = xrows.shape[0]
    acc = xrows * w_ref[k_taps - 1:k_taps, :]
    for s in range(1, k_taps):
        acc = acc + pltpu.roll(xrows, s, 0) * w_ref[k_taps - 1 - s:k_taps - s, :]
    return acc


def _shift_up(x, s):
    return x if s == 0 else pltpu.roll(x, x.shape[0] - s, 0)


def _div_tile(n, cap, mult=8):
    best = None
    for t in range(mult, min(n, cap) + 1, mult):
        if n % t == 0:
            best = t
    return best if best is not None else n


def _halo_prev(tt):
    return lambda b, t: (b, jnp.maximum(t * (tt // HALO) - 1, 0))


def _halo_next(tt, t_total):
    return lambda b, t: (b, jnp.minimum((t + 1) * (tt // HALO), t_total // HALO - 1))


def _mm(a, b, mode, out_dtype, name, tm=512, tn=512, tk=None):
    if mode == "nn":
        (m, k), n = a.shape, b.shape[1]
    elif mode == "nt":
        (m, k), n = a.shape, b.shape[0]
    else:
        (k, m), n = a.shape, b.shape[1]
    tm, tn = min(tm, m), min(tn, n)
    tk = k if tk is None else min(tk, k)
    assert m % tm == 0 and n % tn == 0 and k % tk == 0, (name, a.shape, b.shape, tm, tn, tk)
    nk = k // tk
    if mode == "tn":
        a_spec = pl.BlockSpec((tk, tm), lambda i, j, kk: (kk, i))
    else:
        a_spec = pl.BlockSpec((tm, tk), lambda i, j, kk: (i, kk))
    if mode == "nt":
        b_spec = pl.BlockSpec((tn, tk), lambda i, j, kk: (j, kk))
    else:
        b_spec = pl.BlockSpec((tk, tn), lambda i, j, kk: (kk, j))
    dims = {"nn": ((1,), (0,)), "nt": ((1,), (1,)), "tn": ((0,), (0,))}[mode]

    def body(a_ref, b_ref, o_ref, *acc):
        p = _dg(a_ref[...], b_ref[...], dims)
        if nk == 1:
            o_ref[...] = p.astype(out_dtype)
        else:
            kk = pl.program_id(2)

            @pl.when(kk == 0)
            def _():
                acc[0][...] = p

            @pl.when(kk > 0)
            def _():
                acc[0][...] += p

            @pl.when(kk == nk - 1)
            def _():
                o_ref[...] = acc[0][...].astype(out_dtype)

    return pl.pallas_call(
        body, name=name, grid=(m // tm, n // tn, nk),
        in_specs=[a_spec, b_spec],
        out_specs=pl.BlockSpec((tm, tn), lambda i, j, kk: (i, j)),
        out_shape=jax.ShapeDtypeStruct((m, n), out_dtype),
        scratch_shapes=[pltpu.VMEM((tm, tn), F32)] if nk > 1 else [],
        compiler_params=_params(("parallel", "parallel", "arbitrary")),
    )(a, b)


def _ada_fwd(c_all, w_ada, b_cols):
    def body(c_ref, w_ref, b_ref, o_ref):
        cond = _silu(c_ref[...]).astype(MXU_DT)
        o_ref[...] = _dot(cond, w_ref[...].astype(MXU_DT)) + b_ref[...]

    return pl.pallas_call(body, name="ada_fwd", out_shape=jax.ShapeDtypeStruct((c_all.shape[0], w_ada.shape[1]), F32),
                          compiler_params=_params())(c_all, w_ada, b_cols)


def _ada_bwd(c_all, dmod_all, dmod_cols):
    def body(c_ref, da_ref, dc_ref, gw_ref, gb_ref):
        cond = _silu(c_ref[...]).astype(MXU_DT)
        gw_ref[...] = _dot_tn(cond, dc_ref[...].astype(MXU_DT))
        gb_ref[...] = jnp.sum(da_ref[...], axis=0, keepdims=True)

    return pl.pallas_call(
        body, name="ada_bwd",
        out_shape=(jax.ShapeDtypeStruct((c_all.shape[1], dmod_cols.shape[1]), F32),
                   jax.ShapeDtypeStruct((1, dmod_all.shape[1]), F32)),
        compiler_params=_params())(c_all, dmod_all, dmod_cols)


def _tok_spec(tt, width=D):
    return pl.BlockSpec((1, tt, width), lambda b, t: (b, t, 0))


def _vec_spec(width=D):
    return pl.BlockSpec((1, width), lambda b, t: (0, 0))


def _bvec_spec(width=D):
    return pl.BlockSpec((1, 1, width), lambda b, t: (b, 0, 0))


def _ln0_mod(x, g0, b0, sc, sh):
    bsz, t_total, _ = x.shape
    tt = _div_tile(t_total, 256)

    def body(x_ref, g_ref, b_ref, sc_ref, sh_ref, h_ref):
        xh, _ = _ln_stats(x_ref[0])
        x0 = xh * g_ref[...] + b_ref[...]
        h_ref[0] = (x0 * (1.0 + sc_ref[0]) + sh_ref[0]).astype(MXU_DT)

    return pl.pallas_call(
        body, name="ln0_mod", grid=(bsz, t_total // tt),
        in_specs=[_tok_spec(tt), _vec_spec(), _vec_spec(), _bvec_spec(), _bvec_spec()],
        out_specs=_tok_spec(tt), out_shape=jax.ShapeDtypeStruct(x.shape, MXU_DT),
        compiler_params=_params(("parallel", "parallel")))(x, g0, b0, sc, sh)


def _res_ln_mod(x, y, gt, g0, b0, g1, b1, sc, sh):
    bsz, t_total, _ = x.shape
    tt = _div_tile(t_total, 256)

    def body(x_ref, y_ref, gt_ref, g0_ref, b0_ref, g1_ref, b1_ref, sc_ref, sh_ref, r_ref, h_ref):
        xh, _ = _ln_stats(x_ref[0])
        r = ALPHA * (xh * g0_ref[...] + b0_ref[...]) + (1.0 + gt_ref[0]) * y_ref[0]
        r_ref[0] = r
        rh, _ = _ln_stats(r)
        x1 = rh * g1_ref[...] + b1_ref[...]
        h_ref[0] = (x1 * (1.0 + sc_ref[0]) + sh_ref[0]).astype(MXU_DT)

    return pl.pallas_call(
        body, name="res_ln_mod", grid=(bsz, t_total // tt),
        in_specs=[_tok_spec(tt), _tok_spec(tt), _bvec_spec(), _vec_spec(), _vec_spec(), _vec_spec(), _vec_spec(),
                  _bvec_spec(), _bvec_spec()],
        out_specs=(_tok_spec(tt), _tok_spec(tt)),
        out_shape=(jax.ShapeDtypeStruct(x.shape, F32), jax.ShapeDtypeStruct(x.shape, MXU_DT)),
        compiler_params=_params(("parallel", "parallel")))(x, y, gt, g0, b0, g1, b1, sc, sh)


def _final_fwd_bwd(r1, y2, gt, g1, b1, g2, b2, target):
    bsz, t_total, _ = r1.shape
    tt = _div_tile(t_total, 256)

    def body(r1_ref, y2_ref, gt_ref, g1_ref, b1_ref, g2_ref, b2_ref, tg_ref,
             loss_ref, dr2_ref, dy2_ref, dgt_ref, dg2_ref, db2_ref):
        b, t = pl.program_id(0), pl.program_id(1)

        @pl.when((b == 0) & (t == 0))
        def _():
            loss_ref[...] = jnp.zeros_like(loss_ref)
            dg2_ref[...] = jnp.zeros_like(dg2_ref)
            db2_ref[...] = jnp.zeros_like(db2_ref)

        @pl.when(t == 0)
        def _():
            dgt_ref[...] = jnp.zeros_like(dgt_ref)

        rh1, _ = _ln_stats(r1_ref[0])
        x1 = rh1 * g1_ref[...] + b1_ref[...]
        y2 = y2_ref[0]
        gate = 1.0 + gt_ref[0]
        xh2, rstd2 = _ln_stats(ALPHA * x1 + gate * y2)
        err = xh2 * g2_ref[...] + b2_ref[...] - tg_ref[0]
        loss_ref[...] += jnp.sum(err * err, axis=0, keepdims=True)
        dx2 = err * (1.0 / D)
        dg2_ref[...] += jnp.sum(dx2 * xh2, axis=0, keepdims=True)
        db2_ref[...] += jnp.sum(dx2, axis=0, keepdims=True)
        dr2 = _ln_bwd(dx2 * g2_ref[...], xh2, rstd2)
        dr2_ref[0] = dr2
        dy2_ref[0] = (gate * dr2).astype(MXU_DT)
        dgt_ref[0] += jnp.sum(dr2 * y2, axis=0, keepdims=True)

    vec_out = jax.ShapeDtypeStruct((1, D), F32)
    return pl.pallas_call(
        body, name="final_fwd_bwd", grid=(bsz, t_total // tt),
        in_specs=[_tok_spec(tt), _tok_spec(tt), _bvec_spec(), _vec_spec(), _vec_spec(), _vec_spec(), _vec_spec(),
                  _tok_spec(tt)],
        out_specs=(_vec_spec(), _tok_spec(tt), _tok_spec(tt), _bvec_spec(), _vec_spec(), _vec_spec()),
        out_shape=(vec_out, jax.ShapeDtypeStruct(r1.shape, F32), jax.ShapeDtypeStruct(r1.shape, MXU_DT),
                   jax.ShapeDtypeStruct((bsz, 1, D), F32), vec_out, vec_out),
        compiler_params=_params(("arbitrary", "arbitrary")))(r1, y2, gt, g1, b1, g2, b2, target)


def _ln_bwd_call(name, d_res, d_h, src, g, b, sc, y=None, gt=None):
    bsz, t_total, _ = src.shape
    tt = _div_tile(t_total, 256)
    has_y = y is not None

    def body(*refs):
        if has_y:
            (dres_ref, dh_ref, src_ref, g_ref, b_ref, sc_ref, y_ref, gt_ref,
             dsrc_ref, dsc_ref, dsh_ref, dg_ref, db_ref, dy_ref, dgt_ref) = refs
        else:
            (dres_ref, dh_ref, src_ref, g_ref, b_ref, sc_ref,
             dsrc_ref, dsc_ref, dsh_ref, dg_ref, db_ref) = refs
        bi, t = pl.program_id(0), pl.program_id(1)

        @pl.when((bi == 0) & (t == 0))
        def _():
            dg_ref[...] = jnp.zeros_like(dg_ref)
            db_ref[...] = jnp.zeros_like(db_ref)

        @pl.when(t == 0)
        def _():
            dsc_ref[...] = jnp.zeros_like(dsc_ref)
            dsh_ref[...] = jnp.zeros_like(dsh_ref)
            if has_y:
                dgt_ref[...] = jnp.zeros_like(dgt_ref)

        xh, rstd = _ln_stats(src_ref[0])
        xv = xh * g_ref[...] + b_ref[...]
        dh = dh_ref[0]
        dx = ALPHA * dres_ref[0] + dh * (1.0 + sc_ref[0])
        dsc_ref[0] += jnp.sum(dh * xv, axis=0, keepdims=True)
        dsh_ref[0] += jnp.sum(dh, axis=0, keepdims=True)
        dg_ref[...] += jnp.sum(dx * xh, axis=0, keepdims=True)
        db_ref[...] += jnp.sum(dx, axis=0, keepdims=True)
        dsrc = _ln_bwd(dx * g_ref[...], xh, rstd)
        dsrc_ref[0] = dsrc
        if has_y:
            dy_ref[0] = ((1.0 + gt_ref[0]) * dsrc).astype(MXU_DT)
            dgt_ref[0] += jnp.sum(dsrc * y_ref[0], axis=0, keepdims=True)

    vec_out = jax.ShapeDtypeStruct((1, D), F32)
    bvec_out = jax.ShapeDtypeStruct((bsz, 1, D), F32)
    in_specs = [_tok_spec(tt), _tok_spec(tt), _tok_spec(tt), _vec_spec(), _vec_spec(), _bvec_spec()]
    out_specs = [_tok_spec(tt), _bvec_spec(), _bvec_spec(), _vec_spec(), _vec_spec()]
    out_shape = [jax.ShapeDtypeStruct(src.shape, F32), bvec_out, bvec_out, vec_out, vec_out]
    args = [d_res, d_h, src, g, b, sc]
    if has_y:
        in_specs += [_tok_spec(tt), _bvec_spec()]
        out_specs += [_tok_spec(tt), _bvec_spec()]
        out_shape += [jax.ShapeDtypeStruct(src.shape, MXU_DT), bvec_out]
        args += [y, gt]
    return pl.pallas_call(body, name=name, grid=(bsz, t_total // tt), in_specs=in_specs, out_specs=tuple(out_specs),
                          out_shape=tuple(out_shape), compiler_params=_params(("arbitrary", "arbitrary")))(*args)


FFN_TC = 256
FFN_NJ = D_FF // FFN_TC
FFN_PW = 2 * FFN_TC


def _ffn_pair(a, axis):
    shp = list(a.shape)
    a4 = a.reshape(shp[:axis] + [2, FFN_NJ, FFN_TC] + shp[axis + 1:])
    return jnp.swapaxes(a4, axis, axis + 1).reshape(shp)


def _ffn_unpair(a, axis):
    shp = list(a.shape)
    a4 = a.reshape(shp[:axis] + [FFN_NJ, 2, FFN_TC] + shp[axis + 1:])
    return jnp.swapaxes(a4, axis, axis + 1).reshape(shp)


def _ffn_act_fwd(up, cw, cb):
    bsz, t_total, _ = up.shape
    tt = _div_tile(t_total, 256)
    hp = _halo_prev(tt)

    def body(x_ref, xp_ref, w_ref, b_ref, o_ref):
        prev = jnp.where(pl.program_id(1) == 0, 0.0, xp_ref[0])
        rows = jnp.concatenate([prev, x_ref[0]], axis=0)
        u = _conv_rows(rows, w_ref, FFN_CONV_K)[HALO:] + b_ref[...]
        o_ref[0] = (_silu(u[:, :FFN_TC]) * u[:, FFN_TC:]).astype(MXU_DT)

    return pl.pallas_call(
        body, name="ffn_act_fwd", grid=(bsz, t_total // tt, FFN_NJ),
        in_specs=[pl.BlockSpec((1, tt, FFN_PW), lambda b, t, j: (b, t, j)),
                  pl.BlockSpec((1, HALO, FFN_PW), lambda b, t, j: (*hp(b, t), j)),
                  pl.BlockSpec((FFN_CONV_K, FFN_PW), lambda b, t, j: (0, j)),
                  pl.BlockSpec((1, FFN_PW), lambda b, t, j: (0, j))],
        out_specs=pl.BlockSpec((1, tt, FFN_TC), lambda b, t, j: (b, t, j)),
        out_shape=jax.ShapeDtypeStruct((bsz, t_total, D_FF), MXU_DT),
        compiler_params=_params(("parallel", "parallel", "parallel")))(up, up, cw, cb)


def _ffn_act_bwd(up, da, cw, cb):
    bsz, t_total, width = up.shape
    tt = _div_tile(t_total, 256)
    nt = t_total // tt
    hp, hn = _halo_prev(tt), _halo_next(tt, t_total)

    def body(x_ref, xp_ref, xn_ref, da_ref, dan_ref, w_ref, b_ref, dup_ref, dw_ref, db_ref):
        b, t = pl.program_id(1), pl.program_id(2)

        @pl.when((b == 0) & (t == 0))
        def _():
            dw_ref[...] = jnp.zeros_like(dw_ref)
            db_ref[...] = jnp.zeros_like(db_ref)

        prev = jnp.where(t == 0, 0.0, xp_ref[0])
        rows = jnp.concatenate([prev, x_ref[0], xn_ref[0]], axis=0)
        u = _conv_rows(rows, w_ref, FFN_CONV_K)[HALO:] + b_ref[...]
        g_pre, v_pre = u[:, :FFN_TC], u[:, FFN_TC:]
        valid = (_iota((tt + HALO, 1), 0) < tt) | (t < nt - 1)
        da_ext = jnp.where(valid, jnp.concatenate([da_ref[0], dan_ref[0]], axis=0), 0.0)
        sg = _sigmoid(g_pre)
        gs = g_pre * sg
        du = jnp.concatenate([da_ext * v_pre * (sg + gs * (1.0 - sg)), da_ext * gs], axis=1)
        dup = du * w_ref[FFN_CONV_K - 1:FFN_CONV_K, :]
        for s in range(1, FFN_CONV_K):
            dup = dup + _shift_up(du, s) * w_ref[FFN_CONV_K - 1 - s:FFN_CONV_K - s, :]
        dup_ref[0] = dup[:tt].astype(MXU_DT)
        du_t = du[:tt]
        db_ref[...] += jnp.sum(du_t, axis=0, keepdims=True)
        for k in range(FFN_CONV_K):
            s = FFN_CONV_K - 1 - k
            xs = (rows if s == 0 else pltpu.roll(rows, s, 0))[HALO:HALO + tt]
            dw_ref[k:k + 1, :] += jnp.sum(du_t * xs, axis=0, keepdims=True)

    def halo(h, w):
        return pl.BlockSpec((1, HALO, w), lambda j, b, t: (*h(b, t), j))

    wspec = lambda rows_: pl.BlockSpec((rows_, FFN_PW), lambda j, b, t: (0, j))
    tile = pl.BlockSpec((1, tt, FFN_PW), lambda j, b, t: (b, t, j))
    return pl.pallas_call(
        body, name="ffn_act_bwd", grid=(FFN_NJ, bsz, nt),
        in_specs=[tile, halo(hp, FFN_PW), halo(hn, FFN_PW),
                  pl.BlockSpec((1, tt, FFN_TC), lambda j, b, t: (b, t, j)), halo(hn, FFN_TC),
                  wspec(FFN_CONV_K), wspec(1)],
        out_specs=(tile, wspec(FFN_CONV_K), wspec(1)),
        out_shape=(jax.ShapeDtypeStruct(up.shape, MXU_DT), jax.ShapeDtypeStruct((FFN_CONV_K, width), F32),
                   jax.ShapeDtypeStruct((1, width), F32)),
        compiler_params=_params(("arbitrary", "arbitrary", "arbitrary")))(up, up, up, da, da, cw, cb)


QKV_W = 3 * HEADS * HD
SM_BLK = P_SM // 128


def _dn_pre_fwd(proj, conv_w, alog_row, dt_row):
    bsz, t_total, _ = proj.shape
    tt = _div_tile(t_total, 256)
    hp = _halo_prev(tt)

    def body(x_ref, xp_ref, sm_ref, w_ref, al_ref, dt_ref, q_ref, k_ref, v_ref, g_ref):
        prev = jnp.where(pl.program_id(1) == 0, 0.0, xp_ref[0])
        y = _conv_rows(jnp.concatenate([prev, x_ref[0]], axis=0), w_ref, DN_CONV_K)[HALO:]
        q_ref[0], k_ref[0], v_ref[0] = _dn_qkv(y)
        g_ref[0] = _dn_gates(sm_ref[0], al_ref[...], dt_ref[...])

    out512 = jax.ShapeDtypeStruct((bsz, t_total, HEADS * HD), F32)
    return pl.pallas_call(
        body, name="dn_pre_fwd", grid=(bsz, t_total // tt),
        in_specs=[pl.BlockSpec((1, tt, QKV_W), lambda b, t: (b, t, 0)),
                  pl.BlockSpec((1, HALO, QKV_W), lambda b, t: (*hp(b, t), 0)),
                  pl.BlockSpec((1, tt, 128), lambda b, t: (b, t, SM_BLK)),
                  pl.BlockSpec((DN_CONV_K, QKV_W), lambda b, t: (0, 0)), _vec_spec(128), _vec_spec(128)],
        out_specs=(_tok_spec(tt, 512), _tok_spec(tt, 512), _tok_spec(tt, 512), _tok_spec(tt, 128)),
        out_shape=(out512, out512, out512, jax.ShapeDtypeStruct((bsz, t_total, 128), F32)),
        compiler_params=_params(("parallel", "parallel")))(proj, proj, proj, conv_w, alog_row, dt_row)


def _dn_pre_bwd(proj, dq, dk, dv, dgates, conv_w, alog_row, dt_row):
    bsz, t_total, _ = proj.shape
    tt = _div_tile(t_total, 128)
    nt = t_total // tt
    hp, hn = _halo_prev(tt), _halo_next(tt, t_total)

    def body(x_ref, xp_ref, xn_ref, sm_ref, dq_ref, dqn_ref, dk_ref, dkn_ref, dv_ref, dvn_ref, dg_ref,
             w_ref, al_ref, dt_ref, dx_ref, dsm_ref, dw_ref, dal_ref, ddt_ref):
        b, t = pl.program_id(0), pl.program_id(1)

        @pl.when((b == 0) & (t == 0))
        def _():
            dw_ref[...] = jnp.zeros_like(dw_ref)
            dal_ref[...] = jnp.zeros_like(dal_ref)
            ddt_ref[...] = jnp.zeros_like(ddt_ref)

        prev = jnp.where(t == 0, 0.0, xp_ref[0])
        rows = jnp.concatenate([prev, x_ref[0], xn_ref[0]], axis=0)
        y = _conv_rows(rows, w_ref, DN_CONV_K)[HALO:]
        valid = (_iota((tt + HALO, 1), 0) < tt) | (t < nt - 1)

        def ext(tile_ref, next_ref):
            return jnp.where(valid, jnp.concatenate([tile_ref[0], next_ref[0]], axis=0), 0.0)

        _, vjp_qkv = jax.vjp(_dn_qkv, y)
        (dy,) = vjp_qkv((ext(dq_ref, dqn_ref), ext(dk_ref, dkn_ref), ext(dv_ref, dvn_ref)))
        dy = jnp.where(valid, dy, 0.0)
        dx = dy * w_ref[DN_CONV_K - 1:DN_CONV_K, :]
        for s in range(1, DN_CONV_K):
            dx = dx + _shift_up(dy, s) * w_ref[DN_CONV_K - 1 - s:DN_CONV_K - s, :]
        dx_ref[0] = dx[:tt].astype(MXU_DT)
        dy_t = dy[:tt]
        for k in range(DN_CONV_K):
            s = DN_CONV_K - 1 - k
            xs = (rows if s == 0 else pltpu.roll(rows, s, 0))[HALO:HALO + tt]
            dw_ref[k:k + 1, :] += jnp.sum(dy_t * xs, axis=0, keepdims=True)
        _, vjp_g = jax.vjp(_dn_gates, sm_ref[0], al_ref[...], dt_ref[...])
        dsm, dal, ddt = vjp_g(dg_ref[0])
        dsm_ref[0] = dsm
        dal_ref[...] += dal
        ddt_ref[...] += ddt

    def tile(width, blk=0):
        return pl.BlockSpec((1, tt, width), lambda b, t: (b, t, blk))

    def halo(h, width):
        return pl.BlockSpec((1, HALO, width), lambda b, t: (*h(b, t), 0))

    return pl.pallas_call(
        body, name="dn_pre_bwd", grid=(bsz, nt),
        in_specs=[tile(QKV_W), halo(hp, QKV_W), halo(hn, QKV_W), tile(128, SM_BLK),
                  tile(512), halo(hn, 512), tile(512), halo(hn, 512), tile(512), halo(hn, 512), tile(128),
                  pl.BlockSpec((DN_CONV_K, QKV_W), lambda b, t: (0, 0)), _vec_spec(128), _vec_spec(128)],
        out_specs=(tile(QKV_W), tile(128), pl.BlockSpec((DN_CONV_K, QKV_W), lambda b, t: (0, 0)),
                   _vec_spec(128), _vec_spec(128)),
        out_shape=(jax.ShapeDtypeStruct((bsz, t_total, QKV_W), MXU_DT), jax.ShapeDtypeStruct((bsz, t_total, 128), F32),
                   jax.ShapeDtypeStruct((DN_CONV_K, QKV_W), F32), jax.ShapeDtypeStruct((1, 128), F32),
                   jax.ShapeDtypeStruct((1, 128), F32)),
        compiler_params=_params(("arbitrary", "arbitrary")))(
            proj, proj, proj, proj, dq, dq, dk, dk, dv, dv, dgates, conv_w, alog_row, dt_row)


def _state_spec(bsz, idx):
    return pl.BlockSpec((bsz, 1, HEADS, HD, HD), lambda c: (0, idx(c), 0, 0, 0))


def _inv_spec(bsz, idx):
    return pl.BlockSpec((bsz, 1, HEADS, CHUNK, CHUNK), lambda c: (0, idx(c), 0, 0, 0))


def _chunk_spec(bsz, width, idx, blk=0):
    return pl.BlockSpec((bsz, CHUNK, width), lambda c: (0, idx(c), blk))


def _dn_rec_fwd(q, k, v, gates):
    bsz, t_total, _ = q.shape
    nc = t_total // CHUNK
    fwd = lambda c: c

    def body(q_ref, k_ref, v_ref, g_ref, o_ref, ss_ref, inv_ref, s_ref):
        @pl.when(pl.program_id(0) == 0)
        def _():
            s_ref[...] = jnp.zeros_like(s_ref)

        seqs = range(bsz)
        s_list = [[s_ref[b * HEADS + h] for h in range(HEADS)] for b in seqs]
        for b in seqs:
            for h in range(HEADS):
                ss_ref[b, 0, h] = s_list[b][h]
        o, new_s, invs = _dn_chunk(s_list, [q_ref[b] for b in seqs], [k_ref[b] for b in seqs],
                                   [v_ref[b] for b in seqs], [g_ref[b] for b in seqs], with_inv=True)
        for b in seqs:
            o_ref[b] = o[b]
            for h in range(HEADS):
                s_ref[b * HEADS + h] = new_s[b][h]
                inv_ref[b, 0, h] = invs[b * HEADS + h]

    return pl.pallas_call(
        body, name="dn_rec_fwd", grid=(nc,),
        in_specs=[_chunk_spec(bsz, 512, fwd)] * 3 + [_chunk_spec(bsz, 128, fwd)],
        out_specs=(_chunk_spec(bsz, 512, fwd), _state_spec(bsz, fwd), _inv_spec(bsz, fwd)),
        out_shape=(jax.ShapeDtypeStruct(q.shape, F32), jax.ShapeDtypeStruct((bsz, nc, HEADS, HD, HD), F32),
                   jax.ShapeDtypeStruct((bsz, nc, HEADS, CHUNK, CHUNK), F32)),
        scratch_shapes=[pltpu.VMEM((bsz * HEADS, HD, HD), F32)],
        compiler_params=_params(("arbitrary",)))(q, k, v, gates)


def _dn_rec_bwd(q, k, v, gates, states, invs, do):
    bsz, t_total, _ = q.shape
    nc = t_total // CHUNK
    rev = lambda c: nc - 1 - c

    def body(q_ref, k_ref, v_ref, g_ref, ss_ref, inv_ref, do_ref, dq_ref, dk_ref, dv_ref, dg_ref, ds_ref):
        @pl.when(pl.program_id(0) == 0)
        def _():
            ds_ref[...] = jnp.zeros_like(ds_ref)

        seqs = range(bsz)
        s_list = [[ss_ref[b, 0, h] for h in range(HEADS)] for b in seqs]
        known = [inv_ref[b, 0, h] for b in seqs for h in range(HEADS)]
        _, vjp = jax.vjp(functools.partial(_dn_chunk, inv_known=known),
                         s_list, [q_ref[b] for b in seqs], [k_ref[b] for b in seqs],
                         [v_ref[b] for b in seqs], [g_ref[b] for b in seqs])
        ds_in, dq, dk, dv, dg = vjp(([do_ref[b] for b in seqs],
                                     [[ds_ref[b * HEADS + h] for h in range(HEADS)] for b in seqs]))
        for b in seqs:
            dq_ref[b], dk_ref[b], dv_ref[b], dg_ref[b] = dq[b], dk[b], dv[b], dg[b]
            for h in range(HEADS):
                ds_ref[b * HEADS + h] = ds_in[b][h]

    tok = lambda width: _chunk_spec(bsz, width, rev)
    out512 = jax.ShapeDtypeStruct(q.shape, F32)
    return pl.pallas_call(
        body, name="dn_rec_bwd", grid=(nc,),
        in_specs=[tok(512), tok(512), tok(512), tok(128), _state_spec(bsz, rev), _inv_spec(bsz, rev), tok(512)],
        out_specs=(tok(512), tok(512), tok(512), tok(128)),
        out_shape=(out512, out512, out512, jax.ShapeDtypeStruct(gates.shape, F32)),
        scratch_shapes=[pltpu.VMEM((bsz * HEADS, HD, HD), F32)],
        compiler_params=_params(("arbitrary",)))(q, k, v, gates, states, invs, do)


GQ_BLK, GK_BLK, GV_BLK = P_GQ // 512, P_GK // 512, P_GV // 512


def _gla_rec_fwd(proj, w2, bg):
    bsz, t_total, _ = proj.shape
    nc = t_total // CHUNK

    fwd = lambda c: c

    def body(q_ref, k_ref, v_ref, sm_ref, w2_ref, bg_ref, o_ref, ss_ref, s_ref):
        @pl.when(pl.program_id(0) == 0)
        def _():
            s_ref[...] = jnp.zeros_like(s_ref)

        seqs = range(bsz)
        s_list = [[s_ref[b * HEADS + h] for h in range(HEADS)] for b in seqs]
        for b in seqs:
            for h in range(HEADS):
                ss_ref[b, 0, h] = s_list[b][h]
        o, new_s = _gla_chunk(s_list, [q_ref[b] for b in seqs], [k_ref[b] for b in seqs], [v_ref[b] for b in seqs],
                              [sm_ref[b] for b in seqs], w2_ref[...], bg_ref[...])
        for b in seqs:
            o_ref[b] = o[b]
            for h in range(HEADS):
                s_ref[b * HEADS + h] = new_s[b][h]

    col = lambda blk, width=512: _chunk_spec(bsz, width, fwd, blk)
    return pl.pallas_call(
        body, name="gla_rec_fwd", grid=(nc,),
        in_specs=[col(GQ_BLK), col(GK_BLK), col(GV_BLK), col(SM_BLK, 128),
                  pl.BlockSpec((128, 512), lambda c: (0, 0)), pl.BlockSpec((1, 512), lambda c: (0, 0))],
        out_specs=(col(0), _state_spec(bsz, fwd)),
        out_shape=(jax.ShapeDtypeStruct((bsz, t_total, 512), F32),
                   jax.ShapeDtypeStruct((bsz, nc, HEADS, HD, HD), F32)),
        scratch_shapes=[pltpu.VMEM((bsz * HEADS, HD, HD), F32)],
        compiler_params=_params(("arbitrary",)))(proj, proj, proj, proj, w2, bg)


def _gla_rec_bwd(proj, w2, bg, states, do, dsm_dn):
    bsz, t_total, _ = proj.shape
    nc = t_total // CHUNK
    rev = lambda c: nc - 1 - c

    def body(q_ref, k_ref, v_ref, sm_ref, w2_ref, bg_ref, ss_ref, do_ref, dsd_ref,
             dq_ref, dk_ref, dv_ref, dsm_ref, dw2_ref, dbg_ref, ds_ref):
        @pl.when(pl.program_id(0) == 0)
        def _():
            dw2_ref[...] = jnp.zeros_like(dw2_ref)
            dbg_ref[...] = jnp.zeros_like(dbg_ref)
            ds_ref[...] = jnp.zeros_like(ds_ref)

        seqs = range(bsz)
        s_list = [[ss_ref[b, 0, h] for h in range(HEADS)] for b in seqs]
        _, vjp = jax.vjp(_gla_chunk, s_list, [q_ref[b] for b in seqs], [k_ref[b] for b in seqs],
                         [v_ref[b] for b in seqs], [sm_ref[b] for b in seqs], w2_ref[...], bg_ref[...])
        ds_in, dq, dk, dv, dsm, dw2, dbg = vjp(([do_ref[b] for b in seqs],
                                                [[ds_ref[b * HEADS + h] for h in range(HEADS)] for b in seqs]))
        for b in seqs:
            dq_ref[b], dk_ref[b], dv_ref[b] = dq[b].astype(MXU_DT), dk[b].astype(MXU_DT), dv[b].astype(MXU_DT)
            dsm_ref[b] = (dsm[b] + dsd_ref[b]).astype(MXU_DT)
            for h in range(HEADS):
                ds_ref[b * HEADS + h] = ds_in[b][h]
        dw2_ref[...] += dw2
        dbg_ref[...] += dbg

    col = lambda blk, width=512: _chunk_spec(bsz, width, rev, blk)
    w2_spec = pl.BlockSpec((128, 512), lambda c: (0, 0))
    bg_spec = pl.BlockSpec((1, 512), lambda c: (0, 0))
    out512 = jax.ShapeDtypeStruct((bsz, t_total, 512), MXU_DT)
    return pl.pallas_call(
        body, name="gla_rec_bwd", grid=(nc,),
        in_specs=[col(GQ_BLK), col(GK_BLK), col(GV_BLK), col(SM_BLK, 128), w2_spec, bg_spec,
                  _state_spec(bsz, rev), col(0), col(0, 128)],
        out_specs=(col(0), col(0), col(0), col(0, 128), w2_spec, bg_spec),
        out_shape=(out512, out512, out512, jax.ShapeDtypeStruct((bsz, t_total, 128), MXU_DT),
                   jax.ShapeDtypeStruct((128, 512), F32), jax.ShapeDtypeStruct((1, 512), F32)),
        scratch_shapes=[pltpu.VMEM((bsz * HEADS, HD, HD), F32)],
        compiler_params=_params(("arbitrary",)))(proj, proj, proj, proj, w2, bg, states, do, dsm_dn)


Z_BLK, GG_BLK = P_Z // 512, P_GG // 512


def _mix_out_fwd(o_dn, o_gla, proj, grow_dn, grow_gla):
    bsz, t_total, _ = o_dn.shape
    tt = _div_tile(t_total, 256)

    def body(od_ref, og_ref, z_ref, gg_ref, gd_ref, gl_ref, o_ref):
        o_ref[0, :, :512] = _gate_norm(od_ref[0], z_ref[0], gd_ref[...]).astype(MXU_DT)
        o_ref[0, :, 512:] = _gate_norm(og_ref[0], gg_ref[0], gl_ref[...]).astype(MXU_DT)

    def col(blk):
        return pl.BlockSpec((1, tt, 512), lambda b, t: (b, t, blk))

    return pl.pallas_call(
        body, name="mix_out_fwd", grid=(bsz, t_total // tt),
        in_specs=[col(0), col(0), col(Z_BLK), col(GG_BLK), _vec_spec(512), _vec_spec(512)],
        out_specs=_tok_spec(tt), out_shape=jax.ShapeDtypeStruct((bsz, t_total, D), MXU_DT),
        compiler_params=_params(("parallel", "parallel")))(o_dn, o_gla, proj, proj, grow_dn, grow_gla)


def _mix_out_bwd(do, o_dn, o_gla, proj, grow_dn, grow_gla):
    bsz, t_total, _ = o_dn.shape
    tt = _div_tile(t_total, 256)

    def body(do_ref, od_ref, og_ref, z_ref, gg_ref, gd_ref, gl_ref,
             dod_ref, dog_ref, dz_ref, dgg_ref, dgd_ref, dgl_ref):
        @pl.when((pl.program_id(0) == 0) & (pl.program_id(1) == 0))
        def _():
            dgd_ref[...] = jnp.zeros_like(dgd_ref)
            dgl_ref[...] = jnp.zeros_like(dgl_ref)

        def one(o_ref, gate_ref, g_ref, ct, do_out, dgate_out, dg_out):
            _, vjp = jax.vjp(_gate_norm, o_ref[0], gate_ref[0], g_ref[...])
            d_o, d_gate, d_row = vjp(ct)
            do_out[0] = d_o
            dgate_out[0] = d_gate.astype(MXU_DT)
            acc = d_row[:, :HD]
            for h in range(1, HEADS):
                acc = acc + d_row[:, h * HD:(h + 1) * HD]
            dg_out[...] += acc

        ct = do_ref[0]
        one(od_ref, z_ref, gd_ref, ct[:, :512], dod_ref, dz_ref, dgd_ref)
        one(og_ref, gg_ref, gl_ref, ct[:, 512:], dog_ref, dgg_ref, dgl_ref)

    def col(blk):
        return pl.BlockSpec((1, tt, 512), lambda b, t: (b, t, blk))

    f512 = jax.ShapeDtypeStruct((bsz, t_total, 512), F32)
    b512 = jax.ShapeDtypeStruct((bsz, t_total, 512), MXU_DT)
    g128 = jax.ShapeDtypeStruct((1, HD), F32)
    return pl.pallas_call(
        body, name="mix_out_bwd", grid=(bsz, t_total // tt),
        in_specs=[_tok_spec(tt), col(0), col(0), col(Z_BLK), col(GG_BLK), _vec_spec(512), _vec_spec(512)],
        out_specs=(col(0), col(0), col(0), col(0), _vec_spec(HD), _vec_spec(HD)),
        out_shape=(f512, f512, b512, b512, g128, g128),
        compiler_params=_params(("arbitrary", "arbitrary")))(do, o_dn, o_gla, proj, proj, grow_dn, grow_gla)


def _sum_slots(x, name):
    n, rows, cols = x.shape
    tr = _div_tile(rows, max(8, (1 << 19) // cols))

    def body(x_ref, o_ref):
        acc = x_ref[0].astype(F32)
        for i in range(1, n):
            acc = acc + x_ref[i].astype(F32)
        o_ref[...] = acc

    return pl.pallas_call(
        body, name=name, grid=(rows // tr,),
        in_specs=[pl.BlockSpec((n, tr, cols), lambda i: (0, i, 0))],
        out_specs=pl.BlockSpec((tr, cols), lambda i: (i, 0)),
        out_shape=jax.ShapeDtypeStruct((rows, cols), F32), compiler_params=_params(("parallel",)))(x)


def _adamw_math(w, g, m, v):
    nm = ADAM_B1 * m + (1.0 - ADAM_B1) * g
    nv = ADAM_B2 * v + (1.0 - ADAM_B2) * (g * g)
    m_hat = nm / (1.0 - ADAM_B1 ** ADAM_STEP)
    v_hat = nv / (1.0 - ADAM_B2 ** ADAM_STEP)
    return -ADAM_LR * (m_hat / (jnp.sqrt(v_hat) + ADAM_EPS) + ADAM_WD * w), nm, nv


def _adamw(w, g, m, v, name):
    _, rows, cols = w.shape
    tr = _div_tile(rows, max(8, (1 << 18) // cols))

    def body(w_ref, g_ref, m_ref, v_ref, d_ref, nm_ref, nv_ref):
        d_ref[...], nm_ref[...], nv_ref[...] = _adamw_math(w_ref[...], g_ref[...], m_ref[...], v_ref[...])

    spec = pl.BlockSpec((1, tr, cols), lambda i: (0, i, 0))
    shp = jax.ShapeDtypeStruct(w.shape, F32)
    return pl.pallas_call(body, name=name, grid=(rows // tr,), in_specs=[spec] * 4, out_specs=(spec,) * 3,
                          out_shape=(shp,) * 3, compiler_params=_params(("parallel",)))(w, g, m, v)


def _adamw_many(ws, gs, ms, vs, name):
    n = len(ws)

    def body(*refs):
        for i in range(n):
            d, nm, nv = _adamw_math(refs[i][...], refs[n + i][...], refs[2 * n + i][...], refs[3 * n + i][...])
            refs[4 * n + i][...] = d
            refs[5 * n + i][...] = nm
            refs[6 * n + i][...] = nv

    shapes = tuple(jax.ShapeDtypeStruct(w.shape, F32) for w in ws)
    outs = pl.pallas_call(body, name=name, out_shape=shapes * 3, compiler_params=_params())(*ws, *gs, *ms, *vs)
    return outs[:n], outs[n:2 * n], outs[2 * n:]


def _position():
    return lax.axis_index("x"), lax.axis_index("y"), lax.axis_index("c")


def _slot(px, py, pc):
    return 4 * px + 2 * py + pc


def _gather_small(x, name):
    rows, cols = x.shape

    def body(x_ref, o_ref, send_sems, recv_sems):
        mx, my, mc = _position()

        def peer(k):
            return (mx ^ ((k >> 2) & 1), my ^ ((k >> 1) & 1), mc ^ (k & 1))

        o_ref[_slot(mx, my, mc)] = x_ref[...]
        sends = []
        for k in range(1, N_DEV):
            cp = pltpu.make_async_remote_copy(src_ref=x_ref, dst_ref=o_ref.at[_slot(mx, my, mc)],
                                              send_sem=send_sems.at[k - 1], recv_sem=recv_sems.at[k - 1],
                                              device_id=peer(k), device_id_type=MESH)
            cp.start()
            sends.append(cp)
        for k in range(1, N_DEV):
            pltpu.make_async_remote_copy(src_ref=x_ref, dst_ref=o_ref.at[_slot(*peer(k))],
                                         send_sem=send_sems.at[k - 1], recv_sem=recv_sems.at[k - 1],
                                         device_id=peer(k), device_id_type=MESH).wait_recv()
        for cp in sends:
            cp.wait_send()

    return pl.pallas_call(
        body, name=name, out_shape=jax.ShapeDtypeStruct((N_DEV, rows, cols), x.dtype),
        in_specs=[pl.BlockSpec(memory_space=pltpu.VMEM)], out_specs=pl.BlockSpec(memory_space=pltpu.VMEM),
        scratch_shapes=[pltpu.SemaphoreType.DMA((N_DEV - 1,)), pltpu.SemaphoreType.DMA((N_DEV - 1,))],
        compiler_params=pltpu.CompilerParams(vmem_limit_bytes=VMEM_LIMIT_V7X))(x)


def _gather_big(shards):
    n = len(shards)

    def body(*refs):
        xs, outs = refs[:n], refs[n:2 * n]
        send_sems, recv_sems, local_sems = refs[2 * n:]
        mx, my, mc = _position()
        me, sibling = (mx, my, mc), (mx, my, 1 - mc)
        chips = [(1 - mx, my), (mx, 1 - my), (1 - mx, 1 - my)]

        def copy(a, k, block, to, src=None):
            dst = outs[a].at[_slot(*block)]
            return pltpu.make_async_remote_copy(src_ref=dst if src is None else src, dst_ref=dst,
                                                send_sem=send_sems.at[7 * a + k], recv_sem=recv_sems.at[7 * a + k],
                                                device_id=to, device_id_type=MESH)

        mine = [pltpu.make_async_copy(xs[a], outs[a].at[_slot(*me)], local_sems.at[a]) for a in range(n)]
        for cp in mine:
            cp.start()
        started = []
        for a in range(n):
            started.append(copy(a, 0, me, sibling, src=xs[a]))
            started += [copy(a, 1 + j, me, (*chip, mc), src=xs[a]) for j, chip in enumerate(chips)]
        for cp in started:
            cp.start()
        for j, chip in enumerate(chips):
            for a in range(n):
                copy(a, 1 + j, (*chip, mc), me).wait_recv()
                fwd = copy(a, 4 + j, (*chip, mc), sibling)
                fwd.start()
                started.append(fwd)
        for a in range(n):
            copy(a, 0, sibling, me).wait_recv()
            for j, chip in enumerate(chips):
                copy(a, 4 + j, (*chip, 1 - mc), me).wait_recv()
        for cp in started:
            cp.wait_send()
        for cp in mine:
            cp.wait()

    any_spec = pl.BlockSpec(memory_space=pl.ANY)
    return pl.pallas_call(
        body, name="gather_weights",
        out_shape=tuple(jax.ShapeDtypeStruct((N_DEV,) + s.shape, s.dtype) for s in shards),
        in_specs=[any_spec] * n, out_specs=(any_spec,) * n,
        scratch_shapes=[pltpu.SemaphoreType.DMA((7 * n,)), pltpu.SemaphoreType.DMA((7 * n,)),
                        pltpu.SemaphoreType.DMA((n,))])(*shards)


def _peer(pos, k):
    mx, my, mc = pos
    return (mx ^ ((k >> 2) & 1), my ^ ((k >> 1) & 1), mc ^ (k & 1))


def _exchange_copies(srcs, lands, send_sems, recv_sems, by_owner):
    pos = _position()
    me = _slot(*pos)
    out = []
    for a, (src, land) in enumerate(zip(srcs, lands)):
        for k in range(1, N_DEV):
            peer = _peer(pos, k)
            sems = dict(send_sem=send_sems.at[7 * a + k - 1], recv_sem=recv_sems.at[7 * a + k - 1],
                        device_id=peer, device_id_type=MESH)
            mine = src.at[_slot(*peer)] if by_owner else src
            send = pltpu.make_async_remote_copy(src_ref=mine, dst_ref=land.at[me], **sems)
            recv = pltpu.make_async_remote_copy(src_ref=mine, dst_ref=land.at[_slot(*peer)], **sems)
            out.append((send, recv))
    return out


_HBM_SPEC = pl.BlockSpec(memory_space=pltpu.HBM)
_SEM_SPEC = pl.BlockSpec(memory_space=pltpu.SEMAPHORE)
_DATAFLOW = pltpu.SideEffectType.DATAFLOW_SIDE_EFFECTING


def _exchange_start(name, srcs, slab_shapes, after, by_owner, carry=()):
    n, na, nc = len(srcs), len(after), len(carry)
    lands = [pltpu.with_memory_space_constraint(lax.empty((N_DEV,) + s, x.dtype), pltpu.HBM)
             for s, x in zip(slab_shapes, srcs)]
    thru = [pltpu.with_memory_space_constraint(x, pltpu.HBM) for x in [*srcs, *lands, *carry]]

    def body(*refs):
        src_refs, land_refs = refs[:n], refs[n:2 * n]
        send_sems, recv_sems = refs[len(thru) + na], refs[len(thru) + na + 1]
        token = refs[-1]
        for send, _ in _exchange_copies(src_refs, land_refs, send_sems, recv_sems, by_owner):
            send.start()
        token[...] = jnp.zeros_like(token)

    outs = pl.pallas_call(
        body, name=name,
        out_shape=(pltpu.SemaphoreType.DMA((7 * n,)), pltpu.SemaphoreType.DMA((7 * n,)),
                   *[pltpu.HBM(x.shape, x.dtype) for x in thru], jax.ShapeDtypeStruct((8, 128), F32)),
        in_specs=[_HBM_SPEC] * len(thru) + [pl.BlockSpec(memory_space=pl.ANY)] * na,
        out_specs=(_SEM_SPEC, _SEM_SPEC, *[_HBM_SPEC] * len(thru), pl.BlockSpec(memory_space=pltpu.VMEM)),
        input_output_aliases={i: 2 + i for i in range(len(thru))},
        compiler_params=pltpu.CompilerParams(has_side_effects=_DATAFLOW))(*thru, *after)
    return (outs[0], outs[1], list(outs[2:2 + n]), list(outs[2 + n:2 + 2 * n]), outs[-1],
            list(outs[2 + 2 * n:2 + 2 * n + nc]))


def _exchange_wait(name, send_sems, recv_sems, srcs, lands, after, by_owner):
    n = len(srcs)

    def body(*refs):
        src_refs, land_refs = refs[:n], refs[n:2 * n]
        s_sems, r_sems = refs[2 * n], refs[2 * n + 1]
        for send, recv in _exchange_copies(src_refs, land_refs, s_sems, r_sems, by_owner):
            send.wait_send()
            recv.wait_recv()

    outs = pl.pallas_call(
        body, name=name,
        out_shape=(*[pltpu.HBM(x.shape, x.dtype) for x in srcs], *[pltpu.HBM(l.shape, l.dtype) for l in lands]),
        in_specs=[_HBM_SPEC] * (2 * n) + [_SEM_SPEC, _SEM_SPEC, pl.BlockSpec(memory_space=pl.ANY)],
        out_specs=tuple([_HBM_SPEC] * (2 * n)),
        input_output_aliases={i: i for i in range(2 * n)},
        compiler_params=pltpu.CompilerParams(has_side_effects=_DATAFLOW))(*srcs, *lands, send_sems, recv_sems, after)
    return list(outs[n:])


def _pad_heads(x, axis):
    shp = list(x.shape)
    x4 = x.reshape(shp[:axis] + [HEADS, GLA_KEY] + shp[axis + 1:])
    pad = [(0, 0)] * x4.ndim
    pad[axis + 1] = (0, HD - GLA_KEY)
    return jnp.pad(x4, pad).reshape(shp[:axis] + [HEADS * HD] + shp[axis + 1:])


def _unpad_heads(x, axis):
    shp = list(x.shape)
    x4 = x.reshape(shp[:axis] + [HEADS, HD] + shp[axis + 1:])
    x4 = lax.slice_in_dim(x4, 0, GLA_KEY, axis=axis + 1)
    return x4.reshape(shp[:axis] + [HEADS * GLA_KEY] + shp[axis + 1:])


O_Z_END, O_AB, O_GQ, O_GK, O_GV, O_R = 2048, 2048, 2056, 2312, 2568, 3592


def _pad_in_rows(wt):
    return jnp.concatenate([
        wt[:O_Z_END], _pad_heads(wt[O_GQ:O_GK], 0), _pad_heads(wt[O_GK:O_GV], 0), wt[O_GV:O_R],
        wt[O_AB:O_GQ], wt[O_R:], jnp.zeros((P_W - P_SM - 8 - GATE_RANK, wt.shape[1]), wt.dtype)], axis=0)


def _unpad_in_rows(gt):
    return jnp.concatenate([
        gt[:P_GQ], gt[P_SM:P_SM + 8], _unpad_heads(gt[P_GQ:P_GK], 0), _unpad_heads(gt[P_GK:P_GV], 0),
        gt[P_GV:P_SM], gt[P_SM + 8:P_SM + 8 + GATE_RANK]], axis=0)


def _lane_row(vals, width=128):
    return jnp.pad(vals.reshape(1, -1), ((0, 0), (0, width - vals.size)))


SMALL_NAMES = ["ln0_g", "ln0_b", "b_ada", "dn_conv", "dn_a_log", "dn_dt_bias", "dn_norm_g", "gla_w_gate2",
               "gla_b_gate", "gla_norm_g", "ln1_g", "ln1_b", "ffn_conv", "ffn_conv_b", "ln2_g", "ln2_b"]
WEIGHTS = ["ln0_g", "ln0_b", "w_ada", "b_ada", "w_in", "dn_conv", "dn_a_log", "dn_dt_bias", "dn_norm_g",
           "gla_w_gate2", "gla_b_gate", "gla_norm_g", "w_o", "ln1_g", "ln1_b", "ffn_w_up", "ffn_conv", "ffn_conv_b",
           "ffn_w_down", "ln2_g", "ln2_b"]


def kernel(x, c, ln0_g, ln0_b, w_ada, b_ada, w_in, dn_conv, dn_a_log, dn_dt_bias, dn_norm_g, gla_w_gate2, gla_b_gate, gla_norm_g, w_o, ln1_g, ln1_b, ffn_w_up, ffn_conv, ffn_conv_b, ffn_w_down, ln2_g, ln2_b, loss_target, m_ln0_g, m_ln0_b, m_w_ada, m_b_ada, m_w_in, m_dn_conv, m_dn_a_log, m_dn_dt_bias, m_dn_norm_g, m_gla_w_gate2, m_gla_b_gate, m_gla_norm_g, m_w_o, m_ln1_g, m_ln1_b, m_ffn_w_up, m_ffn_conv, m_ffn_conv_b, m_ffn_w_down, m_ln2_g, m_ln2_b, v_ln0_g, v_ln0_b, v_w_ada, v_b_ada, v_w_in, v_dn_conv, v_dn_a_log, v_dn_dt_bias, v_dn_norm_g, v_gla_w_gate2, v_gla_b_gate, v_gla_norm_g, v_w_o, v_ln1_g, v_ln1_b, v_ffn_w_up, v_ffn_conv, v_ffn_conv_b, v_ffn_w_down, v_ln2_g, v_ln2_b):
    args = dict(locals())
    w_given = {n: args[n] for n in WEIGHTS}
    m_given = {n: args["m_" + n] for n in WEIGHTS}
    v_given = {n: args["v_" + n] for n in WEIGHTS}
    bsz, t_total, _ = x.shape
    ntok = bsz * t_total
    mx, my, mc = _position()
    me = _slot(mx, my, mc)

    pack1 = jnp.concatenate([c.reshape(-1), dn_conv.reshape(-1), gla_w_gate2.reshape(-1), ffn_conv.reshape(-1)])
    n1 = pack1.size
    rows1 = -(-n1 // 1024) * 8
    pack1 = jnp.pad(pack1, (0, rows1 * 128 - n1)).reshape(rows1, 128)
    got1 = _gather_small(pack1, "gather_cond").reshape(N_DEV, -1)
    o1 = bsz * D
    o2 = o1 + dn_conv.size
    o3 = o2 + gla_w_gate2.size
    c_all = got1[:, :o1].reshape(N_DEV * bsz, D)
    dn_conv_f = got1[:, o1:o2].reshape(N_DEV, DN_CONV_K, -1).transpose(1, 0, 2).reshape(DN_CONV_K, QKV_W)
    gate2_f = got1[:, o2:o3].reshape(N_DEV, GATE_RANK, -1).transpose(1, 0, 2).reshape(GATE_RANK, HEADS * GLA_KEY)
    ffn_conv_f = got1[:, o3:n1].reshape(N_DEV, FFN_CONV_K, -1).transpose(1, 0, 2).reshape(FFN_CONV_K, 2 * D_FF)

    win_t = w_in[0].T.astype(MXU_DT)
    wup_t = ffn_w_up[0].T.astype(MXU_DT)
    (win_all,) = _gather_big([win_t])
    win_p = _pad_in_rows(win_all.reshape(IN_W, D))
    cw_p, cb_p = _ffn_pair(ffn_conv_f, 1), _ffn_pair(ffn_conv_b, 1)

    ncol = w_ada.shape[2]
    b_cols = lax.dynamic_slice_in_dim(b_ada, me * ncol, ncol, axis=1)
    mod_part = _ada_fwd(c_all, w_ada[0], b_cols)
    mod_all = _gather_small(mod_part.reshape(-1, 128), "gather_mod").reshape(N_DEV, N_DEV * bsz, ncol)
    mod = lax.dynamic_slice_in_dim(mod_all, me * bsz, bsz, axis=1).transpose(1, 0, 2).reshape(bsz, 6, 1, D)
    late = [w_o[0].astype(MXU_DT), wup_t, ffn_w_down[0].astype(MXU_DT)]
    ag_send, ag_recv, ag_src, ag_land, ag_token, _ = _exchange_start(
        "gather_start", late, [w.shape for w in late], [win_all, mod_all], by_owner=False)
    mod = mod + ag_token[0, 0]
    sh_a, sc_a, gt_a, sh_f, sc_f, gt_f = (mod[:, i] for i in range(6))

    g0, b0 = ln0_g.reshape(1, D), ln0_b.reshape(1, D)
    alog_row, dt_row = _lane_row(dn_a_log[0]), _lane_row(dn_dt_bias[0])
    grow_dn, grow_gla = jnp.tile(dn_norm_g, (1, HEADS)), jnp.tile(gla_norm_g, (1, HEADS))
    w2 = jnp.zeros((128, HEADS * HD), F32).at[SM_R:SM_R + GATE_RANK].set(_pad_heads(gate2_f, 1))
    bg = _pad_heads(gla_b_gate, 1)

    h_a = _ln0_mod(x, g0, b0, sc_a, sh_a)
    proj = _mm(h_a.reshape(ntok, D), win_p, "nt", F32, "mm_proj", tm=1024, tn=1408).reshape(bsz, t_total, P_W)
    q, k, v, gates = _dn_pre_fwd(proj, dn_conv_f, alog_row, dt_row)
    o_dn, s_dn, inv_dn = _dn_rec_fwd(q, k, v, gates)
    o_gla, s_gla = _gla_rec_fwd(proj, w2, bg)
    o_mix = _mix_out_fwd(o_dn, o_gla, proj, grow_dn, grow_gla)
    landed = _exchange_wait("gather_wait", ag_send, ag_recv, ag_src, ag_land, o_mix, by_owner=False)
    wo_all, wup_all, wdn_all = (lax.dynamic_update_slice(l, w[None], (me, 0, 0)) for l, w in zip(landed, late))
    wo_f = wo_all.reshape(D, D)
    wup_f = _ffn_pair(wup_all.reshape(2 * D_FF, D), 0)
    wdn_f = wdn_all.reshape(D_FF, D)
    y = _mm(o_mix.reshape(ntok, D), wo_f, "nn", F32, "mm_wo", tm=1024, tn=1024).reshape(bsz, t_total, D)
    r1, h_f = _res_ln_mod(x, y, gt_a, g0, b0, ln1_g, ln1_b, sc_f, sh_f)
    up = _mm(h_f.reshape(ntok, D), wup_f, "nt", F32, "mm_up", tm=1024, tn=1408).reshape(bsz, t_total, 2 * D_FF)
    act = _ffn_act_fwd(up, cw_p, cb_p)
    y2 = _mm(act.reshape(ntok, D_FF), wdn_f, "nn", F32, "mm_down", tm=1024, tn=1024).reshape(bsz, t_total, D)
    loss_rows, dr2, dy2, dgt_f, d_ln2_g, d_ln2_b = _final_fwd_bwd(r1, y2, gt_f, ln1_g, ln1_b, ln2_g, ln2_b, loss_target)
    loss_part = (0.5 / D) * jnp.sum(loss_rows)

    dy2_2 = dy2.reshape(ntok, D)
    dact = _mm(dy2_2, wdn_f, "nt", F32, "mm_dact", tm=1024, tn=1408).reshape(bsz, t_total, D_FF)
    g_wdn = _mm(act.reshape(ntok, D_FF), dy2_2, "tn", MXU_DT, "mm_gwdn", tm=1408, tn=1024)
    dup, d_cw_p, d_cb_p = _ffn_act_bwd(up, dact, cw_p, cb_p)
    d_ffn_conv, d_ffn_conv_b = _ffn_unpair(d_cw_p, 1), _ffn_unpair(d_cb_p, 1)
    dup_2 = dup.reshape(ntok, 2 * D_FF)
    dh_f = _mm(dup_2, wup_f, "nn", F32, "mm_dhf", tn=1024).reshape(bsz, t_total, D)
    g_wup_t = _mm(dup_2, h_f.reshape(ntok, D), "tn", MXU_DT, "mm_gwup", tm=1408, tn=1024)
    ffn_parts = [_ffn_unpair(g_wup_t, 0).reshape(N_DEV, -1, D), g_wdn.reshape(N_DEV, -1, D)]
    rs_send, rs_recv, rs_src, rs_land, rs_token, _ = _exchange_start(
        "scatter_start", ffn_parts, [p.shape[1:] for p in ffn_parts], [dh_f], by_owner=True)
    dr1, dsc_f, dsh_f, d_ln1_g, d_ln1_b, dy, dgt_a = _ln_bwd_call(
        "ln1_bwd", dr2, dh_f, r1, ln1_g, ln1_b, sc_f + rs_token[0, 0], y=y, gt=gt_a)

    dy_2 = dy.reshape(ntok, D)
    do = _mm(dy_2, wo_f, "nt", F32, "mm_do", tm=1024, tn=1024).reshape(bsz, t_total, D)
    g_wo = _mm(o_mix.reshape(ntok, D), dy_2, "tn", MXU_DT, "mm_gwo", tm=512, tn=1024)
    do_dn, do_gla, dz, dgg, d_dn_norm, d_gla_norm = _mix_out_bwd(do, o_dn, o_gla, proj, grow_dn, grow_gla)
    dq, dk, dv, dgates = _dn_rec_bwd(q, k, v, gates, s_dn, inv_dn, do_dn)
    dqkv, dsm_dn, d_dn_conv, d_alog_row, d_dt_row = _dn_pre_bwd(proj, dq, dk, dv, dgates, dn_conv_f, alog_row, dt_row)
    dgq, dgk, dgv, dsm, d_w2, d_bg = _gla_rec_bwd(proj, w2, bg, s_gla, do_gla, dsm_dn)
    dproj = jnp.concatenate([dqkv, dz, dgq, dgk, dgv, dgg, dsm], axis=-1).reshape(ntok, P_W)
    g_win_p = _mm(dproj, h_a.reshape(ntok, D), "tn", MXU_DT, "mm_gwin", tm=1408, tn=1024)
    mix_parts = [_unpad_in_rows(g_win_p).reshape(N_DEV, -1, D), g_wo.reshape(N_DEV, -1, D)]
    rs2_send, rs2_recv, rs2_src, rs2_land, rs2_token, (win_p_late,) = _exchange_start(
        "scatter_mix_start", mix_parts, [p.shape[1:] for p in mix_parts], [], by_owner=True, carry=[win_p])
    dh_a = _mm(dproj, win_p_late, "nn", F32, "mm_dha", tn=1024).reshape(bsz, t_total, D)
    grad_x, dsc_a, dsh_a, d_ln0_g, d_ln0_b = _ln_bwd_call(
        "ln0_bwd", dr1, dh_a, x, g0, b0, sc_a + rs2_token[0, 0])

    def owner_sum(landed, parts, tag):
        full = [lax.dynamic_update_slice(l, lax.dynamic_slice_in_dim(p, me, 1, axis=0), (me, 0, 0))
                for l, p in zip(landed, parts)]
        return [_sum_slots(f, f"sum_{tag}_{i}") for i, f in enumerate(full)]

    ffn_landed = _exchange_wait("scatter_wait", rs_send, rs_recv, rs_src, rs_land, grad_x, by_owner=True)
    g_wup_ts, g_wdn_s = owner_sum(ffn_landed, ffn_parts, "ffn")
    mix_landed = _exchange_wait("scatter_mix_wait", rs2_send, rs2_recv, rs2_src, rs2_land, grad_x, by_owner=True)
    g_win_t, g_wo_s = owner_sum(mix_landed, mix_parts, "mix")

    dmod = jnp.concatenate([dsh_a, dsc_a, dgt_a, dsh_f, dsc_f, dgt_f], axis=1).reshape(-1)
    small_parts = {
        "ln0_g": d_ln0_g, "ln0_b": d_ln0_b, "ln1_g": d_ln1_g, "ln1_b": d_ln1_b, "ln2_g": d_ln2_g, "ln2_b": d_ln2_b,
        "dn_a_log": d_alog_row[:, :HEADS], "dn_dt_bias": d_dt_row[:, :HEADS],
        "dn_norm_g": d_dn_norm, "gla_norm_g": d_gla_norm, "gla_b_gate": _unpad_heads(d_bg, 1),
        "ffn_conv_b": d_ffn_conv_b, "dn_conv": d_dn_conv,
        "gla_w_gate2": _unpad_heads(d_w2[SM_R:SM_R + GATE_RANK], 1), "ffn_conv": d_ffn_conv}
    order = sorted(small_parts)
    flat = jnp.concatenate([small_parts[n].reshape(-1) for n in order] + [loss_part.reshape(1), dmod])
    n3 = flat.size
    rows3 = -(-n3 // 1024) * 8
    pack3 = jnp.pad(flat, (0, rows3 * 128 - n3)).reshape(rows3, 128)
    got3 = _gather_small(pack3, "gather_small_grads")
    tot3 = _sum_slots(got3, "sum_small_grads").reshape(-1)
    grads = {}
    off = 0
    for n in order:
        size = small_parts[n].size
        grads[n] = tot3[off:off + size]
        off += size
    loss = tot3[off]
    off += 1
    dmod_all = got3.reshape(N_DEV, -1)[:, off:off + dmod.size].reshape(N_DEV * bsz, 6 * D)
    dmod_cols = lax.dynamic_slice_in_dim(dmod_all, me * ncol, ncol, axis=1)
    g_wada, g_bada = _ada_bwd(c_all, dmod_all, dmod_cols)
    grads["b_ada"] = g_bada

    def col_shard(full, rows):
        part = full.reshape(rows, -1)
        width = part.shape[1] // N_DEV
        return lax.dynamic_slice_in_dim(part, me * width, width, axis=1)

    grads["dn_conv"] = col_shard(grads["dn_conv"], DN_CONV_K)
    grads["gla_w_gate2"] = col_shard(grads["gla_w_gate2"], GATE_RANK)
    grads["ffn_conv"] = col_shard(grads["ffn_conv"], FFN_CONV_K)
    grads = {n: g.reshape(w_given[n].shape) for n, g in grads.items()}
    grads["w_ada"] = g_wada.reshape(w_ada.shape)
    grads["w_in"] = g_win_t.T.reshape(w_in.shape)
    grads["w_o"] = g_wo_s.reshape(w_o.shape)
    grads["ffn_w_up"] = g_wup_ts.T.reshape(ffn_w_up.shape)
    grads["ffn_w_down"] = g_wdn_s.reshape(ffn_w_down.shape)

    delta, new_m, new_v = {}, {}, {}
    for n in ["w_ada", "w_in", "w_o", "ffn_w_up", "ffn_w_down"]:
        delta[n], new_m[n], new_v[n] = _adamw(w_given[n], grads[n], m_given[n], v_given[n], "adamw_" + n)
    d_s, m_s, v_s = _adamw_many(*[[src[n] for n in SMALL_NAMES] for src in (w_given, grads, m_given, v_given)],
                                "adamw_small")
    for i, n in enumerate(SMALL_NAMES):
        delta[n], new_m[n], new_v[n] = d_s[i], m_s[i], v_s[i]

    return (loss, grad_x, *[grads[n] for n in WEIGHTS], *[delta[n] for n in WEIGHTS],
            *[new_m[n] for n in WEIGHTS], *[new_v[n] for n in WEIGHTS])
```

```python
import functools

import jax
import jax.numpy as jnp
from jax import lax
from jax.experimental import pallas as pl
from jax.experimental.pallas import tpu as pltpu

F32 = jnp.float32
MXU_DT = jnp.bfloat16
MESH = pl.DeviceIdType.MESH
N_DEV = 8

D = 1024
HEADS = 4
HD = 128
CHUNK = 64
GLA_KEY = 64
GLA_TAU = 16.0
GATE_RANK = 16
D_FF = 2816
IN_W = 3608
ALPHA = 2.0 ** 0.25
EPS = 1e-6
DN_CONV_K = 4
FFN_CONV_K = 3
HALO = 8

P_QKV, P_Z, P_GQ, P_GK, P_GV, P_GG, P_SM, P_W = 0, 1536, 2048, 2560, 3072, 3584, 4096, 4224
SM_A, SM_B, SM_R = 0, 4, 8

ADAM_LR, ADAM_B1, ADAM_B2, ADAM_EPS, ADAM_WD, ADAM_STEP = 0.001, 0.9, 0.999, 1e-08, 0.01, 10

VMEM_LIMIT_V7X = 56 * 1024 * 1024


def _params(sem=None):
    return pltpu.CompilerParams(dimension_semantics=sem, vmem_limit_bytes=VMEM_LIMIT_V7X)


def _dg(a, b, dims, prec=None):
    return lax.dot_general(a, b, (dims, ((), ())), precision=prec, preferred_element_type=F32)


def _dot(a, b, prec=None):
    return _dg(a, b, ((1,), (0,)), prec)


def _dot_nt(a, b, prec=None):
    return _dg(a, b, ((1,), (1,)), prec)


def _dot_tn(a, b, prec=None):
    return _dg(a, b, ((0,), (0,)), prec)


def _iota(shape, dim):
    return lax.broadcasted_iota(jnp.int32, shape, dim)


def _sigmoid(x):
    return jax.nn.sigmoid(x)


def _silu(x):
    return x * _sigmoid(x)


def _softplus(x):
    return jnp.maximum(x, 0.0) + jnp.log(1.0 + jnp.exp(-jnp.abs(x)))


def _ln_stats(x):
    mu = jnp.mean(x, axis=-1, keepdims=True)
    xc = x - mu
    rstd = lax.rsqrt(jnp.mean(xc * xc, axis=-1, keepdims=True) + EPS)
    return xc * rstd, rstd


def _ln_bwd(dxhat, xhat, rstd):
    return rstd * (dxhat - jnp.mean(dxhat, axis=-1, keepdims=True)
                   - xhat * jnp.mean(dxhat * xhat, axis=-1, keepdims=True))


NN, NT, TN = ((1,), (0,)), ((1,), (1,)), ((0,), (0,))


def _split2(a):
    hi = a.astype(jnp.bfloat16)
    return hi, (a - hi.astype(F32)).astype(jnp.bfloat16)


def _d3(a, b, dims):
    ah, al = _split2(a)
    bh, bl = _split2(b)
    return _dg(ah, bh, dims) + (_dg(ah, bl, dims) + _dg(al, bh, dims))


@jax.custom_vjp
def _dot3(a, b):
    return _d3(a, b, NN)


_dot3.defvjp(lambda a, b: (_d3(a, b, NN), (a, b)),
             lambda res, g: (_d3(g, res[1], NT), _d3(res[0], g, TN)))


def _split3(b):
    b1 = b.astype(jnp.bfloat16)
    r1 = b - b1.astype(F32)
    b2 = r1.astype(jnp.bfloat16)
    return b1, b2, (r1 - b2.astype(F32)).astype(jnp.bfloat16)


def _sum3(fn, b):
    b1, b2, b3 = _split3(b)
    return fn(b1) + (fn(b2) + fn(b3))


@jax.custom_vjp
def _mask_dot(e, b):
    return _sum3(lambda t: _dg(e, t, NN), b)


_mask_dot.defvjp(lambda e, b: (_mask_dot(e, b), e),
                 lambda e, g: (jnp.zeros_like(e), _sum3(lambda t: _dg(e, t, TN), g)))


@jax.custom_vjp
def _mask_dot_nt(e, b):
    return _sum3(lambda t: _dg(e, t, NT), b)


_mask_dot_nt.defvjp(lambda e, b: (_mask_dot_nt(e, b), e),
                    lambda e, g: (jnp.zeros_like(e), _sum3(lambda t: _dg(t, e, TN), g)))


def _tri_inv_impl(ms):
    n = ms[0].shape[0]
    r, c = _iota((n, n), 0), _iota((n, n), 1)
    eye = (r == c).astype(F32)
    diag = (r >> 3) == (c >> 3)
    ds = [jnp.where(diag, m, 0.0) for m in ms]
    d2s = [_d3(d, d, NN) for d in ds]
    d4s = [_d3(d2, d2, NN) for d2 in d2s]
    invs = [_d3(eye - d, eye + d2, NN) for d, d2 in zip(ds, d2s)]
    invs = [_d3(inv, eye + d4, NN) for inv, d4 in zip(invs, d4s)]
    shift = 3
    while (1 << shift) < n:
        rb, cb = r >> shift, c >> shift
        sel = ((rb & 1) == 1) & (cb == rb - 1)
        tmp = [_d3(inv, jnp.where(sel, m, 0.0), NN) for inv, m in zip(invs, ms)]
        invs = [inv - _d3(t, inv, NN) for t, inv in zip(tmp, invs)]
        shift += 1
    return invs


@jax.custom_vjp
def _tri_inv(ms):
    return _tri_inv_impl(ms)


def _tri_inv_fwd(ms):
    invs = _tri_inv_impl(ms)
    return invs, invs


def _tri_inv_bwd(invs, das):
    tmp = [_d3(a, da, TN) for a, da in zip(invs, das)]
    return ([-_d3(t, a, NT) for t, a in zip(tmp, invs)],)


_tri_inv.defvjp(_tri_inv_fwd, _tri_inv_bwd)


@jax.custom_vjp
def _tri_inv_known(ms, invs):
    return invs


_tri_inv_known.defvjp(lambda ms, invs: (invs, invs),
                      lambda invs, das: (_tri_inv_bwd(invs, das)[0], [jnp.zeros_like(a) for a in invs]))


def _dn_chunk(s_list, q, k, v, gates, inv_known=None, with_inv=False):
    nb = len(q)
    c = q[0].shape[0]
    r64, c64 = _iota((c, c), 0), _iota((c, c), 1)
    causal = r64 >= c64
    strict = r64 > c64
    tri = causal.astype(jnp.bfloat16)
    eye = (_iota((HD, HD), 0) == _iota((HD, HD), 1)).astype(jnp.bfloat16)
    lane = _iota(gates[0].shape, 1)
    lane1 = _iota((1, HD), 1)
    g_all = [_mask_dot(tri, g) for g in gates]
    g_all_t = [_mask_dot_nt(eye, g) for g in g_all]
    row = _iota(g_all_t[0].shape, 0)
    last = [jnp.sum(g, axis=0, keepdims=True) for g in gates]
    prob = [(b, h) for b in range(nb) for h in range(HEADS)]
    sl = [slice(h * HD, (h + 1) * HD) for h in range(HEADS)]
    qh = [q[b][:, sl[h]] for b, h in prob]
    kh = [k[b][:, sl[h]] for b, h in prob]
    vh = [v[b][:, sl[h]] for b, h in prob]
    s = [s_list[b][h] for b, h in prob]
    beta = [jnp.sum(jnp.where(lane == SM_B + h, gates[b], 0.0), axis=-1, keepdims=True) for b, h in prob]
    g_c = [jnp.sum(jnp.where(lane == SM_A + h, g_all[b], 0.0), axis=-1, keepdims=True) for b, h in prob]
    g_r = [jnp.sum(jnp.where(row == SM_A + h, g_all_t[b], 0.0), axis=0, keepdims=True) for b, h in prob]
    g_last = [jnp.sum(jnp.where(lane1 == SM_A + h, last[b], 0.0), axis=-1, keepdims=True) for b, h in prob]
    decay = [jnp.where(causal, jnp.exp(jnp.where(causal, gc - gr, 0.0)), 0.0) for gc, gr in zip(g_c, g_r)]
    kb = [k_ * b_ for k_, b_ in zip(kh, beta)]
    m_low = [jnp.where(strict, _dot_nt(kb_, k_) * d_, 0.0) for kb_, k_, d_ in zip(kb, kh, decay)]
    attn = [_dot_nt(q_, k_) * d_ for q_, k_, d_ in zip(qh, kh, decay)]
    a_inv = _tri_inv(m_low) if inv_known is None else _tri_inv_known(m_low, inv_known)
    eg = [jnp.exp(gc) for gc in g_c]
    uw = [_dot3(a_, jnp.concatenate([v_ * b_, kb_ * e_], axis=1))
          for a_, v_, b_, kb_, e_ in zip(a_inv, vh, beta, kb, eg)]
    v_new = [uw_[:, :HD] - _dot(uw_[:, HD:], s_) for uw_, s_ in zip(uw, s)]
    qs = [_dot(q_ * e_, s_) for q_, e_, s_ in zip(qh, eg, s)]
    o = [qs_ + _dot(a_, vn_) for qs_, a_, vn_ in zip(qs, attn, v_new)]
    k_dec = [k_ * jnp.exp(gl - gc) for k_, gl, gc in zip(kh, g_last, g_c)]
    s_new = [s_ * jnp.exp(gl) + _dot_tn(kd_, vn_) for s_, gl, kd_, vn_ in zip(s, g_last, k_dec, v_new)]
    outs = [jnp.concatenate(o[b * HEADS:(b + 1) * HEADS], axis=-1) for b in range(nb)]
    states = [s_new[b * HEADS:(b + 1) * HEADS] for b in range(nb)]
    return (outs, states, a_inv) if with_inv else (outs, states)


def _gla_chunk(st_list, q, k, v, small, w2, bg):
    nb = len(q)
    c = q[0].shape[0]
    causal = _iota((c, c), 0) >= _iota((c, c), 1)
    tri = causal.astype(jnp.bfloat16)
    la_all = [-_softplus(-(_dot(sm, w2) + bg)) * (1.0 / GLA_TAU) for sm in small]
    b_all = [_mask_dot(tri, la) for la in la_all]
    prob = [(b, h) for b in range(nb) for h in range(HEADS)]
    sl = [slice(h * HD, (h + 1) * HD) for h in range(HEADS)]
    kh = [k[b][:, sl[h]] for b, h in prob]
    vh = [v[b][:, sl[h]] for b, h in prob]
    st = [st_list[b][h] for b, h in prob]
    bc = [b_all[b][:, sl[h]] for b, h in prob]
    b_last = [jnp.sum(la_all[b][:, sl[h]], axis=0, keepdims=True) for b, h in prob]
    q_dec = [q[b][:, sl[h]] * (GLA_KEY ** -0.5) * jnp.exp(bc_) for (b, h), bc_ in zip(prob, bc)]
    attn = [jnp.where(causal, _dot_nt(qd, k_ * jnp.exp(-bc_)), 0.0) for qd, k_, bc_ in zip(q_dec, kh, bc)]
    inter = [_dot_nt(qd, st_) for qd, st_ in zip(q_dec, st)]
    o = [i_ + _dot(a_, v_) for i_, a_, v_ in zip(inter, attn, vh)]
    k_dec = [k_ * jnp.exp(bl - bc_) for k_, bl, bc_ in zip(kh, b_last, bc)]
    s_new = [st_ * jnp.exp(bl) + _dot_tn(v_, kd) for st_, bl, v_, kd in zip(st, b_last, vh, k_dec)]
    outs = [jnp.concatenate(o[b * HEADS:(b + 1) * HEADS], axis=-1) for b in range(nb)]
    return outs, [s_new[b * HEADS:(b + 1) * HEADS] for b in range(nb)]


def _dn_qkv(y):
    act = _silu(y)
    parts = []
    for i in range(2 * HEADS):
        xh = act[:, i * HD:(i + 1) * HD]
        xh = xh * lax.rsqrt(jnp.sum(xh * xh, axis=-1, keepdims=True) + EPS)
        parts.append(xh * (HD ** -0.5) if i < HEADS else xh)
    qk = jnp.concatenate(parts, axis=-1)
    return qk[:, :HEADS * HD], qk[:, HEADS * HD:], act[:, 2 * HEADS * HD:]


def _dn_gates(small, alog_row, dt_row):
    lane = _iota(small.shape, 1)
    log_a = -jnp.exp(alog_row) * _softplus(small + dt_row)
    return jnp.where(lane < SM_B, log_a, jnp.where(lane < SM_R, _sigmoid(small), 0.0))


def _gate_norm(o, z, grow):
    parts = []
    for h in range(HEADS):
        oh = o[:, h * HD:(h + 1) * HD]
        parts.append(oh * lax.rsqrt(jnp.mean(oh * oh, axis=-1, keepdims=True) + EPS))
    return jnp.concatenate(parts, axis=-1) * grow * _silu(z)


def _conv_rows(xrows, w_ref, k_taps):
    n = xrows.shape[0]
    acc = xrows * w_ref[k_taps - 1:k_taps, :]
    for s in range(1, k_taps):
        acc = acc + pltpu.roll(xrows, s, 0) * w_ref[k_taps - 1 - s:k_taps - s, :]
    return acc


def _shift_up(x, s):
    return x if s == 0 else pltpu.roll(x, x.shape[0] - s, 0)


def _div_tile(n, cap, mult=8):
    best = None
    for t in range(mult, min(n, cap) + 1, mult):
        if n % t == 0:
            best = t
    return best if best is not None else n


def _halo_prev(tt):
    return lambda b, t: (b, jnp.maximum(t * (tt // HALO) - 1, 0))


def _halo_next(tt, t_total):
    return lambda b, t: (b, jnp.minimum((t + 1) * (tt // HALO), t_total // HALO - 1))


def _mm(a, b, mode, out_dtype, name, tm=512, tn=512, tk=None):
    if mode == "nn":
        (m, k), n = a.shape, b.shape[1]
    elif mode == "nt":
        (m, k), n = a.shape, b.shape[0]
    else:
        (k, m), n = a.shape, b.shape[1]
    tm, tn = min(tm, m), min(tn, n)
    tk = k if tk is None else min(tk, k)
    assert m % tm == 0 and n % tn == 0 and k % tk == 0, (name, a.shape, b.shape, tm, tn, tk)
    nk = k // tk
    if mode == "tn":
        a_spec = pl.BlockSpec((tk, tm), lambda i, j, kk: (kk, i))
    else:
        a_spec = pl.BlockSpec((tm, tk), lambda i, j, kk: (i, kk))
    if mode == "nt":
        b_spec = pl.BlockSpec((tn, tk), lambda i, j, kk: (j, kk))
    else:
        b_spec = pl.BlockSpec((tk, tn), lambda i, j, kk: (kk, j))
    dims = {"nn": ((1,), (0,)), "nt": ((1,), (1,)), "tn": ((0,), (0,))}[mode]

    def body(a_ref, b_ref, o_ref, *acc):
        p = _dg(a_ref[...], b_ref[...], dims)
        if nk == 1:
            o_ref[...] = p.astype(out_dtype)
        else:
            kk = pl.program_id(2)

            @pl.when(kk == 0)
            def _():
                acc[0][...] = p

            @pl.when(kk > 0)
            def _():
                acc[0][...] += p

            @pl.when(kk == nk - 1)
            def _():
                o_ref[...] = acc[0][...].astype(out_dtype)

    return pl.pallas_call(
        body, name=name, grid=(m // tm, n // tn, nk),
        in_specs=[a_spec, b_spec],
        out_specs=pl.BlockSpec((tm, tn), lambda i, j, kk: (i, j)),
        out_shape=jax.ShapeDtypeStruct((m, n), out_dtype),
        scratch_shapes=[pltpu.VMEM((tm, tn), F32)] if nk > 1 else [],
        compiler_params=_params(("parallel", "parallel", "arbitrary")),
    )(a, b)


def _ada_fwd(c_all, w_ada, b_cols):
    def body(c_ref, w_ref, b_ref, o_ref):
        cond = _silu(c_ref[...]).astype(MXU_DT)
        o_ref[...] = _dot(cond, w_ref[...].astype(MXU_DT)) + b_ref[...]

    return pl.pallas_call(body, name="ada_fwd", out_shape=jax.ShapeDtypeStruct((c_all.shape[0], w_ada.shape[1]), F32),
                          compiler_params=_params())(c_all, w_ada, b_cols)


def _ada_bwd(c_all, dmod_all, dmod_cols):
    def body(c_ref, da_ref, dc_ref, gw_ref, gb_ref):
        cond = _silu(c_ref[...]).astype(MXU_DT)
        gw_ref[...] = _dot_tn(cond, dc_ref[...].astype(MXU_DT))
        gb_ref[...] = jnp.sum(da_ref[...], axis=0, keepdims=True)

    return pl.pallas_call(
        body, name="ada_bwd",
        out_shape=(jax.ShapeDtypeStruct((c_all.shape[1], dmod_cols.shape[1]), F32),
                   jax.ShapeDtypeStruct((1, dmod_all.shape[1]), F32)),
        compiler_params=_params())(c_all, dmod_all, dmod_cols)


def _tok_spec(tt, width=D):
    return pl.BlockSpec((1, tt, width), lambda b, t: (b, t, 0))


def _vec_spec(width=D):
    return pl.BlockSpec((1, width), lambda b, t: (0, 0))


def _bvec_spec(width=D):
    return pl.BlockSpec((1, 1, width), lambda b, t: (b, 0, 0))


def _ln0_mod(x, g0, b0, sc, sh):
    bsz, t_total, _ = x.shape
    tt = _div_tile(t_total, 256)

    def body(x_ref, g_ref, b_ref, sc_ref, sh_ref, h_ref):
        xh, _ = _ln_stats(x_ref[0])
        x0 = xh * g_ref[...] + b_ref[...]
        h_ref[0] = (x0 * (1.0 + sc_ref[0]) + sh_ref[0]).astype(MXU_DT)

    return pl.pallas_call(
        body, name="ln0_mod", grid=(bsz, t_total // tt),
        in_specs=[_tok_spec(tt), _vec_spec(), _vec_spec(), _bvec_spec(), _bvec_spec()],
        out_specs=_tok_spec(tt), out_shape=jax.ShapeDtypeStruct(x.shape, MXU_DT),
        compiler_params=_params(("parallel", "parallel")))(x, g0, b0, sc, sh)


def _res_ln_mod(x, y, gt, g0, b0, g1, b1, sc, sh):
    bsz, t_total, _ = x.shape
    tt = _div_tile(t_total, 256)

    def body(x_ref, y_ref, gt_ref, g0_ref, b0_ref, g1_ref, b1_ref, sc_ref, sh_ref, r_ref, h_ref):
        xh, _ = _ln_stats(x_ref[0])
        r = ALPHA * (xh * g0_ref[...] + b0_ref[...]) + (1.0 + gt_ref[0]) * y_ref[0]
        r_ref[0] = r
        rh, _ = _ln_stats(r)
        x1 = rh * g1_ref[...] + b1_ref[...]
        h_ref[0] = (x1 * (1.0 + sc_ref[0]) + sh_ref[0]).astype(MXU_DT)

    return pl.pallas_call(
        body, name="res_ln_mod", grid=(bsz, t_total // tt),
        in_specs=[_tok_spec(tt), _tok_spec(tt), _bvec_spec(), _vec_spec(), _vec_spec(), _vec_spec(), _vec_spec(),
                  _bvec_spec(), _bvec_spec()],
        out_specs=(_tok_spec(tt), _tok_spec(tt)),
        out_shape=(jax.ShapeDtypeStruct(x.shape, F32), jax.ShapeDtypeStruct(x.shape, MXU_DT)),
        compiler_params=_params(("parallel", "parallel")))(x, y, gt, g0, b0, g1, b1, sc, sh)


def _final_fwd_bwd(r1, y2, gt, g1, b1, g2, b2, target):
    bsz, t_total, _ = r1.shape
    tt = _div_tile(t_total, 256)

    def body(r1_ref, y2_ref, gt_ref, g1_ref, b1_ref, g2_ref, b2_ref, tg_ref,
             loss_ref, dr2_ref, dy2_ref, dgt_ref, dg2_ref, db2_ref):
        b, t = pl.program_id(0), pl.program_id(1)

        @pl.when((b == 0) & (t == 0))
        def _():
            loss_ref[...] = jnp.zeros_like(loss_ref)
            dg2_ref[...] = jnp.zeros_like(dg2_ref)
            db2_ref[...] = jnp.zeros_like(db2_ref)

        @pl.when(t == 0)
        def _():
            dgt_ref[...] = jnp.zeros_like(dgt_ref)

        rh1, _ = _ln_stats(r1_ref[0])
        x1 = rh1 * g1_ref[...] + b1_ref[...]
        y2 = y2_ref[0]
        gate = 1.0 + gt_ref[0]
        xh2, rstd2 = _ln_stats(ALPHA * x1 + gate * y2)
        err = xh2 * g2_ref[...] + b2_ref[...] - tg_ref[0]
        loss_ref[...] += jnp.sum(err * err, axis=0, keepdims=True)
        dx2 = err * (1.0 / D)
        dg2_ref[...] += jnp.sum(dx2 * xh2, axis=0, keepdims=True)
        db2_ref[...] += jnp.sum(dx2, axis=0, keepdims=True)
        dr2 = _ln_bwd(dx2 * g2_ref[...], xh2, rstd2)
        dr2_ref[0] = dr2
        dy2_ref[0] = (gate * dr2).astype(MXU_DT)
        dgt_ref[0] += jnp.sum(dr2 * y2, axis=0, keepdims=True)

    vec_out = jax.ShapeDtypeStruct((1, D), F32)
    return pl.pallas_call(
        body, name="final_fwd_bwd", grid=(bsz, t_total // tt),
        in_specs=[_tok_spec(tt), _tok_spec(tt), _bvec_spec(), _vec_spec(), _vec_spec(), _vec_spec(), _vec_spec(),
                  _tok_spec(tt)],
        out_specs=(_vec_spec(), _tok_spec(tt), _tok_spec(tt), _bvec_spec(), _vec_spec(), _vec_spec()),
        out_shape=(vec_out, jax.ShapeDtypeStruct(r1.shape, F32), jax.ShapeDtypeStruct(r1.shape, MXU_DT),
                   jax.ShapeDtypeStruct((bsz, 1, D), F32), vec_out, vec_out),
        compiler_params=_params(("arbitrary", "arbitrary")))(r1, y2, gt, g1, b1, g2, b2, target)


def _ln_bwd_call(name, d_res, d_h, src, g, b, sc, y=None, gt=None):
    bsz, t_total, _ = src.shape
    tt = _div_tile(t_total, 256)
    has_y = y is not None

    def body(*refs):
        if has_y:
            (dres_ref, dh_ref, src_ref, g_ref, b_ref, sc_ref, y_ref, gt_ref,
             dsrc_ref, dsc_ref, dsh_ref, dg_ref, db_ref, dy_ref, dgt_ref) = refs
        else:
            (dres_ref, dh_ref, src_ref, g_ref, b_ref, sc_ref,
             dsrc_ref, dsc_ref, dsh_ref, dg_ref, db_ref) = refs
        bi, t = pl.program_id(0), pl.program_id(1)

        @pl.when((bi == 0) & (t == 0))
        def _():
            dg_ref[...] = jnp.zeros_like(dg_ref)
            db_ref[...] = jnp.zeros_like(db_ref)

        @pl.when(t == 0)
        def _():
            dsc_ref[...] = jnp.zeros_like(dsc_ref)
            dsh_ref[...] = jnp.zeros_like(dsh_ref)
            if has_y:
                dgt_ref[...] = jnp.zeros_like(dgt_ref)

        xh, rstd = _ln_stats(src_ref[0])
        xv = xh * g_ref[...] + b_ref[...]
        dh = dh_ref[0]
        dx = ALPHA * dres_ref[0] + dh * (1.0 + sc_ref[0])
        dsc_ref[0] += jnp.sum(dh * xv, axis=0, keepdims=True)
        dsh_ref[0] += jnp.sum(dh, axis=0, keepdims=True)
        dg_ref[...] += jnp.sum(dx * xh, axis=0, keepdims=True)
        db_ref[...] += jnp.sum(dx, axis=0, keepdims=True)
        dsrc = _ln_bwd(dx * g_ref[...], xh, rstd)
        dsrc_ref[0] = dsrc
        if has_y:
            dy_ref[0] = ((1.0 + gt_ref[0]) * dsrc).astype(MXU_DT)
            dgt_ref[0] += jnp.sum(dsrc * y_ref[0], axis=0, keepdims=True)

    vec_out = jax.ShapeDtypeStruct((1, D), F32)
    bvec_out = jax.ShapeDtypeStruct((bsz, 1, D), F32)
    in_specs = [_tok_spec(tt), _tok_spec(tt), _tok_spec(tt), _vec_spec(), _vec_spec(), _bvec_spec()]
    out_specs = [_tok_spec(tt), _bvec_spec(), _bvec_spec(), _vec_spec(), _vec_spec()]
    out_shape = [jax.ShapeDtypeStruct(src.shape, F32), bvec_out, bvec_out, vec_out, vec_out]
    args = [d_res, d_h, src, g, b, sc]
    if has_y:
        in_specs += [_tok_spec(tt), _bvec_spec()]
        out_specs += [_tok_spec(tt), _bvec_spec()]
        out_shape += [jax.ShapeDtypeStruct(src.shape, MXU_DT), bvec_out]
        args += [y, gt]
    return pl.pallas_call(body, name=name, grid=(bsz, t_total // tt), in_specs=in_specs, out_specs=tuple(out_specs),
                          out_shape=tuple(out_shape), compiler_params=_params(("arbitrary", "arbitrary")))(*args)


FFN_TC = 256
FFN_NJ = D_FF // FFN_TC
FFN_PW = 2 * FFN_TC


def _ffn_pair(a, axis):
    shp = list(a.shape)
    a4 = a.reshape(shp[:axis] + [2, FFN_NJ, FFN_TC] + shp[axis + 1:])
    return jnp.swapaxes(a4, axis, axis + 1).reshape(shp)


def _ffn_unpair(a, axis):
    shp = list(a.shape)
    a4 = a.reshape(shp[:axis] + [FFN_NJ, 2, FFN_TC] + shp[axis + 1:])
    return jnp.swapaxes(a4, axis, axis + 1).reshape(shp)


def _ffn_up_act(h, w_up, cw, cb):
    bsz, t_total, _ = h.shape
    tt = _div_tile(t_total, 256)

    def body(h_ref, wu_ref, w_ref, b_ref, up_ref, o_ref, carry_ref):
        up_t = _dot_nt(h_ref[0], wu_ref[...])
        up_ref[0] = up_t
        prev = jnp.where(pl.program_id(2) == 0, 0.0, carry_ref[...])
        rows = jnp.concatenate([prev, up_t], axis=0)
        u = _conv_rows(rows, w_ref, FFN_CONV_K)[HALO:] + b_ref[...]
        o_ref[0] = (_silu(u[:, :FFN_TC]) * u[:, FFN_TC:]).astype(MXU_DT)
        carry_ref[...] = up_t[tt - HALO:, :]

    return pl.pallas_call(
        body, name="ffn_up_act", grid=(bsz, FFN_NJ, t_total // tt),
        in_specs=[pl.BlockSpec((1, tt, D), lambda b, j, t: (b, t, 0)),
                  pl.BlockSpec((FFN_PW, D), lambda b, j, t: (j, 0)),
                  pl.BlockSpec((FFN_CONV_K, FFN_PW), lambda b, j, t: (0, j)),
                  pl.BlockSpec((1, FFN_PW), lambda b, j, t: (0, j))],
        out_specs=(pl.BlockSpec((1, tt, FFN_PW), lambda b, j, t: (b, t, j)),
                   pl.BlockSpec((1, tt, FFN_TC), lambda b, j, t: (b, t, j))),
        out_shape=(jax.ShapeDtypeStruct((bsz, t_total, 2 * D_FF), F32),
                   jax.ShapeDtypeStruct((bsz, t_total, D_FF), MXU_DT)),
        scratch_shapes=[pltpu.VMEM((HALO, FFN_PW), F32)],
        compiler_params=_params(("parallel", "parallel", "arbitrary")))(h, w_up, cw, cb)


HALO16 = 16


def _ffn_act_bwd(up, dy2, w_down, cw, cb):
    bsz, t_total, width = up.shape
    tt = _div_tile(t_total, 256)
    nt = t_total // tt
    hp, hn = _halo_prev(tt), _halo_next(tt, t_total)

    def body(x_ref, xp_ref, xn_ref, dy_ref, dyn_ref, wd_ref, w_ref, b_ref, dup_ref, dw_ref, db_ref):
        b, t = pl.program_id(1), pl.program_id(2)

        @pl.when((b == 0) & (t == 0))
        def _():
            dw_ref[...] = jnp.zeros_like(dw_ref)
            db_ref[...] = jnp.zeros_like(db_ref)

        prev = jnp.where(t == 0, 0.0, xp_ref[0])
        rows = jnp.concatenate([prev, x_ref[0], xn_ref[0]], axis=0)
        u = _conv_rows(rows, w_ref, FFN_CONV_K)[HALO:] + b_ref[...]
        g_pre, v_pre = u[:, :FFN_TC], u[:, FFN_TC:]
        valid = (_iota((tt + HALO, 1), 0) < tt) | (t < nt - 1)
        da = jnp.concatenate([_dot_nt(dy_ref[0], wd_ref[...]), _dot_nt(dyn_ref[0], wd_ref[...])[:HALO]], axis=0)
        da_ext = jnp.where(valid, da, 0.0)
        sg = _sigmoid(g_pre)
        gs = g_pre * sg
        du = jnp.concatenate([da_ext * v_pre * (sg + gs * (1.0 - sg)), da_ext * gs], axis=1)
        dup = du * w_ref[FFN_CONV_K - 1:FFN_CONV_K, :]
        for s in range(1, FFN_CONV_K):
            dup = dup + _shift_up(du, s) * w_ref[FFN_CONV_K - 1 - s:FFN_CONV_K - s, :]
        dup_ref[0] = dup[:tt].astype(MXU_DT)
        du_t = du[:tt]
        db_ref[...] += jnp.sum(du_t, axis=0, keepdims=True)
        for k in range(FFN_CONV_K):
            s = FFN_CONV_K - 1 - k
            xs = (rows if s == 0 else pltpu.roll(rows, s, 0))[HALO:HALO + tt]
            dw_ref[k:k + 1, :] += jnp.sum(du_t * xs, axis=0, keepdims=True)

    def halo(h, w):
        return pl.BlockSpec((1, HALO, w), lambda j, b, t: (*h(b, t), j))

    wspec = lambda rows_: pl.BlockSpec((rows_, FFN_PW), lambda j, b, t: (0, j))
    tile = pl.BlockSpec((1, tt, FFN_PW), lambda j, b, t: (b, t, j))
    dy_next = lambda j, b, t: (b, jnp.minimum((t + 1) * (tt // HALO16), t_total // HALO16 - 1), 0)
    return pl.pallas_call(
        body, name="ffn_act_bwd", grid=(FFN_NJ, bsz, nt),
        in_specs=[tile, halo(hp, FFN_PW), halo(hn, FFN_PW),
                  pl.BlockSpec((1, tt, D), lambda j, b, t: (b, t, 0)), pl.BlockSpec((1, HALO16, D), dy_next),
                  pl.BlockSpec((FFN_TC, D), lambda j, b, t: (j, 0)), wspec(FFN_CONV_K), wspec(1)],
        out_specs=(tile, wspec(FFN_CONV_K), wspec(1)),
        out_shape=(jax.ShapeDtypeStruct(up.shape, MXU_DT), jax.ShapeDtypeStruct((FFN_CONV_K, width), F32),
                   jax.ShapeDtypeStruct((1, width), F32)),
        compiler_params=_params(("arbitrary", "arbitrary", "arbitrary")))(up, up, up, dy2, dy2, w_down, cw, cb)


QKV_W = 3 * HEADS * HD
SM_BLK = P_SM // 128


def _dn_pre_fwd(proj, conv_w, alog_row, dt_row):
    bsz, t_total, _ = proj.shape
    tt = _div_tile(t_total, 256)
    hp = _halo_prev(tt)

    def body(x_ref, xp_ref, sm_ref, w_ref, al_ref, dt_ref, q_ref, k_ref, v_ref, g_ref):
        prev = jnp.where(pl.program_id(1) == 0, 0.0, xp_ref[0])
        y = _conv_rows(jnp.concatenate([prev, x_ref[0]], axis=0), w_ref, DN_CONV_K)[HALO:]
        q_ref[0], k_ref[0], v_ref[0] = _dn_qkv(y)
        g_ref[0] = _dn_gates(sm_ref[0], al_ref[...], dt_ref[...])

    out512 = jax.ShapeDtypeStruct((bsz, t_total, HEADS * HD), F32)
    return pl.pallas_call(
        body, name="dn_pre_fwd", grid=(bsz, t_total // tt),
        in_specs=[pl.BlockSpec((1, tt, QKV_W), lambda b, t: (b, t, 0)),
                  pl.BlockSpec((1, HALO, QKV_W), lambda b, t: (*hp(b, t), 0)),
                  pl.BlockSpec((1, tt, 128), lambda b, t: (b, t, SM_BLK)),
                  pl.BlockSpec((DN_CONV_K, QKV_W), lambda b, t: (0, 0)), _vec_spec(128), _vec_spec(128)],
        out_specs=(_tok_spec(tt, 512), _tok_spec(tt, 512), _tok_spec(tt, 512), _tok_spec(tt, 128)),
        out_shape=(out512, out512, out512, jax.ShapeDtypeStruct((bsz, t_total, 128), F32)),
        compiler_params=_params(("parallel", "parallel")))(proj, proj, proj, conv_w, alog_row, dt_row)


def _dn_pre_bwd(proj, dq, dk, dv, dgates, conv_w, alog_row, dt_row):
    bsz, t_total, _ = proj.shape
    tt = _div_tile(t_total, 128)
    nt = t_total // tt
    hp, hn = _halo_prev(tt), _halo_next(tt, t_total)

    def body(x_ref, xp_ref, xn_ref, sm_ref, dq_ref, dqn_ref, dk_ref, dkn_ref, dv_ref, dvn_ref, dg_ref,
             w_ref, al_ref, dt_ref, dx_ref, dsm_ref, dw_ref, dal_ref, ddt_ref):
        b, t = pl.program_id(0), pl.program_id(1)

        @pl.when((b == 0) & (t == 0))
        def _():
            dw_ref[...] = jnp.zeros_like(dw_ref)
            dal_ref[...] = jnp.zeros_like(dal_ref)
            ddt_ref[...] = jnp.zeros_like(ddt_ref)

        prev = jnp.where(t == 0, 0.0, xp_ref[0])
        rows = jnp.concatenate([prev, x_ref[0], xn_ref[0]], axis=0)
        y = _conv_rows(rows, w_ref, DN_CONV_K)[HALO:]
        valid = (_iota((tt + HALO, 1), 0) < tt) | (t < nt - 1)

        def ext(tile_ref, next_ref):
            return jnp.where(valid, jnp.concatenate([tile_ref[0], next_ref[0]], axis=0), 0.0)

        _, vjp_qkv = jax.vjp(_dn_qkv, y)
        (dy,) = vjp_qkv((ext(dq_ref, dqn_ref), ext(dk_ref, dkn_ref), ext(dv_ref, dvn_ref)))
        dy = jnp.where(valid, dy, 0.0)
        dx = dy * w_ref[DN_CONV_K - 1:DN_CONV_K, :]
        for s in range(1, DN_CONV_K):
            dx = dx + _shift_up(dy, s) * w_ref[DN_CONV_K - 1 - s:DN_CONV_K - s, :]
        dx_ref[0] = dx[:tt].astype(MXU_DT)
        dy_t = dy[:tt]
        for k in range(DN_CONV_K):
            s = DN_CONV_K - 1 - k
            xs = (rows if s == 0 else pltpu.roll(rows, s, 0))[HALO:HALO + tt]
            dw_ref[k:k + 1, :] += jnp.sum(dy_t * xs, axis=0, keepdims=True)
        _, vjp_g = jax.vjp(_dn_gates, sm_ref[0], al_ref[...], dt_ref[...])
        dsm, dal, ddt = vjp_g(dg_ref[0])
        dsm_ref[0] = dsm
        dal_ref[...] += dal
        ddt_ref[...] += ddt

    def tile(width, blk=0):
        return pl.BlockSpec((1, tt, width), lambda b, t: (b, t, blk))

    def halo(h, width):
        return pl.BlockSpec((1, HALO, width), lambda b, t: (*h(b, t), 0))

    return pl.pallas_call(
        body, name="dn_pre_bwd", grid=(bsz, nt),
        in_specs=[tile(QKV_W), halo(hp, QKV_W), halo(hn, QKV_W), tile(128, SM_BLK),
                  tile(512), halo(hn, 512), tile(512), halo(hn, 512), tile(512), halo(hn, 512), tile(128),
                  pl.BlockSpec((DN_CONV_K, QKV_W), lambda b, t: (0, 0)), _vec_spec(128), _vec_spec(128)],
        out_specs=(tile(QKV_W), tile(128), pl.BlockSpec((DN_CONV_K, QKV_W), lambda b, t: (0, 0)),
                   _vec_spec(128), _vec_spec(128)),
        out_shape=(jax.ShapeDtypeStruct((bsz, t_total, QKV_W), MXU_DT), jax.ShapeDtypeStruct((bsz, t_total, 128), F32),
                   jax.ShapeDtypeStruct((DN_CONV_K, QKV_W), F32), jax.ShapeDtypeStruct((1, 128), F32),
                   jax.ShapeDtypeStruct((1, 128), F32)),
        compiler_params=_params(("arbitrary", "arbitrary")))(
            proj, proj, proj, proj, dq, dq, dk, dk, dv, dv, dgates, conv_w, alog_row, dt_row)


def _state_spec(bsz, idx):
    return pl.BlockSpec((bsz, 1, HEADS, HD, HD), lambda c: (0, idx(c), 0, 0, 0))


def _inv_spec(bsz, idx):
    return pl.BlockSpec((bsz, 1, HEADS, CHUNK, CHUNK), lambda c: (0, idx(c), 0, 0, 0))


def _chunk_spec(bsz, width, idx, blk=0):
    return pl.BlockSpec((bsz, CHUNK, width), lambda c: (0, idx(c), blk))


def _dn_rec_fwd(q, k, v, gates):
    bsz, t_total, _ = q.shape
    nc = t_total // CHUNK
    fwd = lambda c: c

    def body(q_ref, k_ref, v_ref, g_ref, o_ref, ss_ref, inv_ref, s_ref):
        @pl.when(pl.program_id(0) == 0)
        def _():
            s_ref[...] = jnp.zeros_like(s_ref)

        seqs = range(bsz)
        s_list = [[s_ref[b * HEADS + h] for h in range(HEADS)] for b in seqs]
        for b in seqs:
            for h in range(HEADS):
                ss_ref[b, 0, h] = s_list[b][h]
        o, new_s, invs = _dn_chunk(s_list, [q_ref[b] for b in seqs], [k_ref[b] for b in seqs],
                                   [v_ref[b] for b in seqs], [g_ref[b] for b in seqs], with_inv=True)
        for b in seqs:
            o_ref[b] = o[b]
            for h in range(HEADS):
                s_ref[b * HEADS + h] = new_s[b][h]
                inv_ref[b, 0, h] = invs[b * HEADS + h]

    return pl.pallas_call(
        body, name="dn_rec_fwd", grid=(nc,),
        in_specs=[_chunk_spec(bsz, 512, fwd)] * 3 + [_chunk_spec(bsz, 128, fwd)],
        out_specs=(_chunk_spec(bsz, 512, fwd), _state_spec(bsz, fwd), _inv_spec(bsz, fwd)),
        out_shape=(jax.ShapeDtypeStruct(q.shape, F32), jax.ShapeDtypeStruct((bsz, nc, HEADS, HD, HD), F32),
                   jax.ShapeDtypeStruct((bsz, nc, HEADS, CHUNK, CHUNK), F32)),
        scratch_shapes=[pltpu.VMEM((bsz * HEADS, HD, HD), F32)],
        compiler_params=_params(("arbitrary",)))(q, k, v, gates)


def _dn_rec_bwd(q, k, v, gates, states, invs, do):
    bsz, t_total, _ = q.shape
    nc = t_total // CHUNK
    rev = lambda c: nc - 1 - c

    def body(q_ref, k_ref, v_ref, g_ref, ss_ref, inv_ref, do_ref, dq_ref, dk_ref, dv_ref, dg_ref, ds_ref):
        @pl.when(pl.program_id(0) == 0)
        def _():
            ds_ref[...] = jnp.zeros_like(ds_ref)

        seqs = range(bsz)
        s_list = [[ss_ref[b, 0, h] for h in range(HEADS)] for b in seqs]
        known = [inv_ref[b, 0, h] for b in seqs for h in range(HEADS)]
        _, vjp = jax.vjp(functools.partial(_dn_chunk, inv_known=known),
                         s_list, [q_ref[b] for b in seqs], [k_ref[b] for b in seqs],
                         [v_ref[b] for b in seqs], [g_ref[b] for b in seqs])
        ds_in, dq, dk, dv, dg = vjp(([do_ref[b] for b in seqs],
                                     [[ds_ref[b * HEADS + h] for h in range(HEADS)] for b in seqs]))
        for b in seqs:
            dq_ref[b], dk_ref[b], dv_ref[b], dg_ref[b] = dq[b], dk[b], dv[b], dg[b]
            for h in range(HEADS):
                ds_ref[b * HEADS + h] = ds_in[b][h]

    tok = lambda width: _chunk_spec(bsz, width, rev)
    out512 = jax.ShapeDtypeStruct(q.shape, F32)
    return pl.pallas_call(
        body, name="dn_rec_bwd", grid=(nc,),
        in_specs=[tok(512), tok(512), tok(512), tok(128), _state_spec(bsz, rev), _inv_spec(bsz, rev), tok(512)],
        out_specs=(tok(512), tok(512), tok(512), tok(128)),
        out_shape=(out512, out512, out512, jax.ShapeDtypeStruct(gates.shape, F32)),
        scratch_shapes=[pltpu.VMEM((bsz * HEADS, HD, HD), F32)],
        compiler_params=_params(("arbitrary",)))(q, k, v, gates, states, invs, do)


GQ_BLK, GK_BLK, GV_BLK = P_GQ // 512, P_GK // 512, P_GV // 512


def _gla_rec_fwd(proj, w2, bg):
    bsz, t_total, _ = proj.shape
    nc = t_total // CHUNK

    fwd = lambda c: c

    def body(q_ref, k_ref, v_ref, sm_ref, w2_ref, bg_ref, o_ref, ss_ref, s_ref):
        @pl.when(pl.program_id(0) == 0)
        def _():
            s_ref[...] = jnp.zeros_like(s_ref)

        seqs = range(bsz)
        s_list = [[s_ref[b * HEADS + h] for h in range(HEADS)] for b in seqs]
        for b in seqs:
            for h in range(HEADS):
                ss_ref[b, 0, h] = s_list[b][h]
        o, new_s = _gla_chunk(s_list, [q_ref[b] for b in seqs], [k_ref[b] for b in seqs], [v_ref[b] for b in seqs],
                              [sm_ref[b] for b in seqs], w2_ref[...], bg_ref[...])
        for b in seqs:
            o_ref[b] = o[b]
            for h in range(HEADS):
                s_ref[b * HEADS + h] = new_s[b][h]

    col = lambda blk, width=512: _chunk_spec(bsz, width, fwd, blk)
    return pl.pallas_call(
        body, name="gla_rec_fwd", grid=(nc,),
        in_specs=[col(GQ_BLK), col(GK_BLK), col(GV_BLK), col(SM_BLK, 128),
                  pl.BlockSpec((128, 512), lambda c: (0, 0)), pl.BlockSpec((1, 512), lambda c: (0, 0))],
        out_specs=(col(0), _state_spec(bsz, fwd)),
        out_shape=(jax.ShapeDtypeStruct((bsz, t_total, 512), F32),
                   jax.ShapeDtypeStruct((bsz, nc, HEADS, HD, HD), F32)),
        scratch_shapes=[pltpu.VMEM((bsz * HEADS, HD, HD), F32)],
        compiler_params=_params(("arbitrary",)))(proj, proj, proj, proj, w2, bg)


def _gla_rec_bwd(proj, w2, bg, states, do, dsm_dn):
    bsz, t_total, _ = proj.shape
    nc = t_total // CHUNK
    rev = lambda c: nc - 1 - c

    def body(q_ref, k_ref, v_ref, sm_ref, w2_ref, bg_ref, ss_ref, do_ref, dsd_ref,
             dq_ref, dk_ref, dv_ref, dsm_ref, dw2_ref, dbg_ref, ds_ref):
        @pl.when(pl.program_id(0) == 0)
        def _():
            dw2_ref[...] = jnp.zeros_like(dw2_ref)
            dbg_ref[...] = jnp.zeros_like(dbg_ref)
            ds_ref[...] = jnp.zeros_like(ds_ref)

        seqs = range(bsz)
        s_list = [[ss_ref[b, 0, h] for h in range(HEADS)] for b in seqs]
        _, vjp = jax.vjp(_gla_chunk, s_list, [q_ref[b] for b in seqs], [k_ref[b] for b in seqs],
                         [v_ref[b] for b in seqs], [sm_ref[b] for b in seqs], w2_ref[...], bg_ref[...])
        ds_in, dq, dk, dv, dsm, dw2, dbg = vjp(([do_ref[b] for b in seqs],
                                                [[ds_ref[b * HEADS + h] for h in range(HEADS)] for b in seqs]))
        for b in seqs:
            dq_ref[b], dk_ref[b], dv_ref[b] = dq[b].astype(MXU_DT), dk[b].astype(MXU_DT), dv[b].astype(MXU_DT)
            dsm_ref[b] = (dsm[b] + dsd_ref[b]).astype(MXU_DT)
            for h in range(HEADS):
                ds_ref[b * HEADS + h] = ds_in[b][h]
        dw2_ref[...] += dw2
        dbg_ref[...] += dbg

    col = lambda blk, width=512: _chunk_spec(bsz, width, rev, blk)
    w2_spec = pl.BlockSpec((128, 512), lambda c: (0, 0))
    bg_spec = pl.BlockSpec((1, 512), lambda c: (0, 0))
    out512 = jax.ShapeDtypeStruct((bsz, t_total, 512), MXU_DT)
    return pl.pallas_call(
        body, name="gla_rec_bwd", grid=(nc,),
        in_specs=[col(GQ_BLK), col(GK_BLK), col(GV_BLK), col(SM_BLK, 128), w2_spec, bg_spec,
                  _state_spec(bsz, rev), col(0), col(0, 128)],
        out_specs=(col(0), col(0), col(0), col(0, 128), w2_spec, bg_spec),
        out_shape=(out512, out512, out512, jax.ShapeDtypeStruct((bsz, t_total, 128), MXU_DT),
                   jax.ShapeDtypeStruct((128, 512), F32), jax.ShapeDtypeStruct((1, 512), F32)),
        scratch_shapes=[pltpu.VMEM((bsz * HEADS, HD, HD), F32)],
        compiler_params=_params(("arbitrary",)))(proj, proj, proj, proj, w2, bg, states, do, dsm_dn)


Z_BLK, GG_BLK = P_Z // 512, P_GG // 512


def _mix_out_fwd(o_dn, o_gla, proj, grow_dn, grow_gla):
    bsz, t_total, _ = o_dn.shape
    tt = _div_tile(t_total, 256)

    def body(od_ref, og_ref, z_ref, gg_ref, gd_ref, gl_ref, o_ref):
        o_ref[0, :, :512] = _gate_norm(od_ref[0], z_ref[0], gd_ref[...]).astype(MXU_DT)
        o_ref[0, :, 512:] = _gate_norm(og_ref[0], gg_ref[0], gl_ref[...]).astype(MXU_DT)

    def col(blk):
        return pl.BlockSpec((1, tt, 512), lambda b, t: (b, t, blk))

    return pl.pallas_call(
        body, name="mix_out_fwd", grid=(bsz, t_total // tt),
        in_specs=[col(0), col(0), col(Z_BLK), col(GG_BLK), _vec_spec(512), _vec_spec(512)],
        out_specs=_tok_spec(tt), out_shape=jax.ShapeDtypeStruct((bsz, t_total, D), MXU_DT),
        compiler_params=_params(("parallel", "parallel")))(o_dn, o_gla, proj, proj, grow_dn, grow_gla)


def _mix_out_bwd(do, o_dn, o_gla, proj, grow_dn, grow_gla):
    bsz, t_total, _ = o_dn.shape
    tt = _div_tile(t_total, 256)

    def body(do_ref, od_ref, og_ref, z_ref, gg_ref, gd_ref, gl_ref,
             dod_ref, dog_ref, dz_ref, dgg_ref, dgd_ref, dgl_ref):
        @pl.when((pl.program_id(0) == 0) & (pl.program_id(1) == 0))
        def _():
            dgd_ref[...] = jnp.zeros_like(dgd_ref)
            dgl_ref[...] = jnp.zeros_like(dgl_ref)

        def one(o_ref, gate_ref, g_ref, ct, do_out, dgate_out, dg_out):
            _, vjp = jax.vjp(_gate_norm, o_ref[0], gate_ref[0], g_ref[...])
            d_o, d_gate, d_row = vjp(ct)
            do_out[0] = d_o
            dgate_out[0] = d_gate.astype(MXU_DT)
            acc = d_row[:, :HD]
            for h in range(1, HEADS):
                acc = acc + d_row[:, h * HD:(h + 1) * HD]
            dg_out[...] += acc

        ct = do_ref[0]
        one(od_ref, z_ref, gd_ref, ct[:, :512], dod_ref, dz_ref, dgd_ref)
        one(og_ref, gg_ref, gl_ref, ct[:, 512:], dog_ref, dgg_ref, dgl_ref)

    def col(blk):
        return pl.BlockSpec((1, tt, 512), lambda b, t: (b, t, blk))

    f512 = jax.ShapeDtypeStruct((bsz, t_total, 512), F32)
    b512 = jax.ShapeDtypeStruct((bsz, t_total, 512), MXU_DT)
    g128 = jax.ShapeDtypeStruct((1, HD), F32)
    return pl.pallas_call(
        body, name="mix_out_bwd", grid=(bsz, t_total // tt),
        in_specs=[_tok_spec(tt), col(0), col(0), col(Z_BLK), col(GG_BLK), _vec_spec(512), _vec_spec(512)],
        out_specs=(col(0), col(0), col(0), col(0), _vec_spec(HD), _vec_spec(HD)),
        out_shape=(f512, f512, b512, b512, g128, g128),
        compiler_params=_params(("arbitrary", "arbitrary")))(do, o_dn, o_gla, proj, proj, grow_dn, grow_gla)


def _sum_slots(x, name):
    n, rows, cols = x.shape
    tr = _div_tile(rows, max(8, (1 << 19) // cols))

    def body(x_ref, o_ref):
        acc = x_ref[0].astype(F32)
        for i in range(1, n):
            acc = acc + x_ref[i].astype(F32)
        o_ref[...] = acc

    return pl.pallas_call(
        body, name=name, grid=(rows // tr,),
        in_specs=[pl.BlockSpec((n, tr, cols), lambda i: (0, i, 0))],
        out_specs=pl.BlockSpec((tr, cols), lambda i: (i, 0)),
        out_shape=jax.ShapeDtypeStruct((rows, cols), F32), compiler_params=_params(("parallel",)))(x)


def _adamw_math(w, g, m, v):
    nm = ADAM_B1 * m + (1.0 - ADAM_B1) * g
    nv = ADAM_B2 * v + (1.0 - ADAM_B2) * (g * g)
    m_hat = nm / (1.0 - ADAM_B1 ** ADAM_STEP)
    v_hat = nv / (1.0 - ADAM_B2 ** ADAM_STEP)
    return -ADAM_LR * (m_hat / (jnp.sqrt(v_hat) + ADAM_EPS) + ADAM_WD * w), nm, nv


def _adamw(w, g, m, v, name):
    _, rows, cols = w.shape
    tr = _div_tile(rows, max(8, (1 << 18) // cols))

    def body(w_ref, g_ref, m_ref, v_ref, d_ref, nm_ref, nv_ref):
        d_ref[...], nm_ref[...], nv_ref[...] = _adamw_math(w_ref[...], g_ref[...], m_ref[...], v_ref[...])

    spec = pl.BlockSpec((1, tr, cols), lambda i: (0, i, 0))
    shp = jax.ShapeDtypeStruct(w.shape, F32)
    return pl.pallas_call(body, name=name, grid=(rows // tr,), in_specs=[spec] * 4, out_specs=(spec,) * 3,
                          out_shape=(shp,) * 3, compiler_params=_params(("parallel",)))(w, g, m, v)


def _adamw_many(ws, gs, ms, vs, name):
    n = len(ws)

    def body(*refs):
        for i in range(n):
            d, nm, nv = _adamw_math(refs[i][...], refs[n + i][...], refs[2 * n + i][...], refs[3 * n + i][...])
            refs[4 * n + i][...] = d
            refs[5 * n + i][...] = nm
            refs[6 * n + i][...] = nv

    shapes = tuple(jax.ShapeDtypeStruct(w.shape, F32) for w in ws)
    outs = pl.pallas_call(body, name=name, out_shape=shapes * 3, compiler_params=_params())(*ws, *gs, *ms, *vs)
    return outs[:n], outs[n:2 * n], outs[2 * n:]


def _position():
    return lax.axis_index("x"), lax.axis_index("y"), lax.axis_index("c")


def _slot(px, py, pc):
    return 4 * px + 2 * py + pc


def _gather_small(x, name):
    rows, cols = x.shape

    def body(x_ref, o_ref, send_sems, recv_sems):
        mx, my, mc = _position()

        def peer(k):
            return (mx ^ ((k >> 2) & 1), my ^ ((k >> 1) & 1), mc ^ (k & 1))

        o_ref[_slot(mx, my, mc)] = x_ref[...]
        sends = []
        for k in range(1, N_DEV):
            cp = pltpu.make_async_remote_copy(src_ref=x_ref, dst_ref=o_ref.at[_slot(mx, my, mc)],
                                              send_sem=send_sems.at[k - 1], recv_sem=recv_sems.at[k - 1],
                                              device_id=peer(k), device_id_type=MESH)
            cp.start()
            sends.append(cp)
        for k in range(1, N_DEV):
            pltpu.make_async_remote_copy(src_ref=x_ref, dst_ref=o_ref.at[_slot(*peer(k))],
                                         send_sem=send_sems.at[k - 1], recv_sem=recv_sems.at[k - 1],
                                         device_id=peer(k), device_id_type=MESH).wait_recv()
        for cp in sends:
            cp.wait_send()

    return pl.pallas_call(
        body, name=name, out_shape=jax.ShapeDtypeStruct((N_DEV, rows, cols), x.dtype),
        in_specs=[pl.BlockSpec(memory_space=pltpu.VMEM)], out_specs=pl.BlockSpec(memory_space=pltpu.VMEM),
        scratch_shapes=[pltpu.SemaphoreType.DMA((N_DEV - 1,)), pltpu.SemaphoreType.DMA((N_DEV - 1,))],
        compiler_params=pltpu.CompilerParams(vmem_limit_bytes=VMEM_LIMIT_V7X))(x)


def _gather_big(shards):
    n = len(shards)

    def body(*refs):
        xs, outs = refs[:n], refs[n:2 * n]
        send_sems, recv_sems, local_sems = refs[2 * n:]
        mx, my, mc = _position()
        me, sibling = (mx, my, mc), (mx, my, 1 - mc)
        chips = [(1 - mx, my), (mx, 1 - my), (1 - mx, 1 - my)]

        def copy(a, k, block, to, src=None):
            dst = outs[a].at[_slot(*block)]
            return pltpu.make_async_remote_copy(src_ref=dst if src is None else src, dst_ref=dst,
                                                send_sem=send_sems.at[7 * a + k], recv_sem=recv_sems.at[7 * a + k],
                                                device_id=to, device_id_type=MESH)

        mine = [pltpu.make_async_copy(xs[a], outs[a].at[_slot(*me)], local_sems.at[a]) for a in range(n)]
        for cp in mine:
            cp.start()
        started = []
        for a in range(n):
            started.append(copy(a, 0, me, sibling, src=xs[a]))
            started += [copy(a, 1 + j, me, (*chip, mc), src=xs[a]) for j, chip in enumerate(chips)]
        for cp in started:
            cp.start()
        for j, chip in enumerate(chips):
            for a in range(n):
                copy(a, 1 + j, (*chip, mc), me).wait_recv()
                fwd = copy(a, 4 + j, (*chip, mc), sibling)
                fwd.start()
                started.append(fwd)
        for a in range(n):
            copy(a, 0, sibling, me).wait_recv()
            for j, chip in enumerate(chips):
                copy(a, 4 + j, (*chip, 1 - mc), me).wait_recv()
        for cp in started:
            cp.wait_send()
        for cp in mine:
            cp.wait()

    any_spec = pl.BlockSpec(memory_space=pl.ANY)
    return pl.pallas_call(
        body, name="gather_weights",
        out_shape=tuple(jax.ShapeDtypeStruct((N_DEV,) + s.shape, s.dtype) for s in shards),
        in_specs=[any_spec] * n, out_specs=(any_spec,) * n,
        scratch_shapes=[pltpu.SemaphoreType.DMA((7 * n,)), pltpu.SemaphoreType.DMA((7 * n,)),
                        pltpu.SemaphoreType.DMA((n,))])(*shards)


def _peer(pos, k):
    mx, my, mc = pos
    return (mx ^ ((k >> 2) & 1), my ^ ((k >> 1) & 1), mc ^ (k & 1))


def _exchange_copies(srcs, lands, send_sems, recv_sems, by_owner):
    pos = _position()
    me = _slot(*pos)
    out = []
    for a, (src, land) in enumerate(zip(srcs, lands)):
        for k in range(1, N_DEV):
            peer = _peer(pos, k)
            sems = dict(send_sem=send_sems.at[7 * a + k - 1], recv_sem=recv_sems.at[7 * a + k - 1],
                        device_id=peer, device_id_type=MESH)
            mine = src.at[_slot(*peer)] if by_owner else src
            send = pltpu.make_async_remote_copy(src_ref=mine, dst_ref=land.at[me], **sems)
            recv = pltpu.make_async_remote_copy(src_ref=mine, dst_ref=land.at[_slot(*peer)], **sems)
            out.append((send, recv))
    return out


_HBM_SPEC = pl.BlockSpec(memory_space=pltpu.HBM)
_SEM_SPEC = pl.BlockSpec(memory_space=pltpu.SEMAPHORE)
_DATAFLOW = pltpu.SideEffectType.DATAFLOW_SIDE_EFFECTING


def _exchange_start(name, srcs, slab_shapes, after, by_owner, carry=()):
    n, na, nc = len(srcs), len(after), len(carry)
    lands = [pltpu.with_memory_space_constraint(lax.empty((N_DEV,) + s, x.dtype), pltpu.HBM)
             for s, x in zip(slab_shapes, srcs)]
    thru = [pltpu.with_memory_space_constraint(x, pltpu.HBM) for x in [*srcs, *lands, *carry]]

    def body(*refs):
        src_refs, land_refs = refs[:n], refs[n:2 * n]
        send_sems, recv_sems = refs[len(thru) + na], refs[len(thru) + na + 1]
        token = refs[-1]
        for send, _ in _exchange_copies(src_refs, land_refs, send_sems, recv_sems, by_owner):
            send.start()
        token[...] = jnp.zeros_like(token)

    outs = pl.pallas_call(
        body, name=name,
        out_shape=(pltpu.SemaphoreType.DMA((7 * n,)), pltpu.SemaphoreType.DMA((7 * n,)),
                   *[pltpu.HBM(x.shape, x.dtype) for x in thru], jax.ShapeDtypeStruct((8, 128), F32)),
        in_specs=[_HBM_SPEC] * len(thru) + [pl.BlockSpec(memory_space=pl.ANY)] * na,
        out_specs=(_SEM_SPEC, _SEM_SPEC, *[_HBM_SPEC] * len(thru), pl.BlockSpec(memory_space=pltpu.VMEM)),
        input_output_aliases={i: 2 + i for i in range(len(thru))},
        compiler_params=pltpu.CompilerParams(has_side_effects=_DATAFLOW))(*thru, *after)
    return (outs[0], outs[1], list(outs[2:2 + n]), list(outs[2 + n:2 + 2 * n]), outs[-1],
            list(outs[2 + 2 * n:2 + 2 * n + nc]))


def _exchange_wait(name, send_sems, recv_sems, srcs, lands, after, by_owner):
    n = len(srcs)

    def body(*refs):
        src_refs, land_refs = refs[:n], refs[n:2 * n]
        s_sems, r_sems = refs[2 * n], refs[2 * n + 1]
        for send, recv in _exchange_copies(src_refs, land_refs, s_sems, r_sems, by_owner):
            send.wait_send()
            recv.wait_recv()

    outs = pl.pallas_call(
        body, name=name,
        out_shape=(*[pltpu.HBM(x.shape, x.dtype) for x in srcs], *[pltpu.HBM(l.shape, l.dtype) for l in lands]),
        in_specs=[_HBM_SPEC] * (2 * n) + [_SEM_SPEC, _SEM_SPEC, pl.BlockSpec(memory_space=pl.ANY)],
        out_specs=tuple([_HBM_SPEC] * (2 * n)),
        input_output_aliases={i: i for i in range(2 * n)},
        compiler_params=pltpu.CompilerParams(has_side_effects=_DATAFLOW))(*srcs, *lands, send_sems, recv_sems, after)
    return list(outs[n:])


def _pad_heads(x, axis):
    shp = list(x.shape)
    x4 = x.reshape(shp[:axis] + [HEADS, GLA_KEY] + shp[axis + 1:])
    pad = [(0, 0)] * x4.ndim
    pad[axis + 1] = (0, HD - GLA_KEY)
    return jnp.pad(x4, pad).reshape(shp[:axis] + [HEADS * HD] + shp[axis + 1:])


def _unpad_heads(x, axis):
    shp = list(x.shape)
    x4 = x.reshape(shp[:axis] + [HEADS, HD] + shp[axis + 1:])
    x4 = lax.slice_in_dim(x4, 0, GLA_KEY, axis=axis + 1)
    return x4.reshape(shp[:axis] + [HEADS * GLA_KEY] + shp[axis + 1:])


O_Z_END, O_AB, O_GQ, O_GK, O_GV, O_R = 2048, 2048, 2056, 2312, 2568, 3592


def _pad_in_rows(wt):
    return jnp.concatenate([
        wt[:O_Z_END], _pad_heads(wt[O_GQ:O_GK], 0), _pad_heads(wt[O_GK:O_GV], 0), wt[O_GV:O_R],
        wt[O_AB:O_GQ], wt[O_R:], jnp.zeros((P_W - P_SM - 8 - GATE_RANK, wt.shape[1]), wt.dtype)], axis=0)


def _unpad_in_rows(gt):
    return jnp.concatenate([
        gt[:P_GQ], gt[P_SM:P_SM + 8], _unpad_heads(gt[P_GQ:P_GK], 0), _unpad_heads(gt[P_GK:P_GV], 0),
        gt[P_GV:P_SM], gt[P_SM + 8:P_SM + 8 + GATE_RANK]], axis=0)


def _lane_row(vals, width=128):
    return jnp.pad(vals.reshape(1, -1), ((0, 0), (0, width - vals.size)))


SMALL_NAMES = ["ln0_g", "ln0_b", "b_ada", "dn_conv", "dn_a_log", "dn_dt_bias", "dn_norm_g", "gla_w_gate2",
               "gla_b_gate", "gla_norm_g", "ln1_g", "ln1_b", "ffn_conv", "ffn_conv_b", "ln2_g", "ln2_b"]
WEIGHTS = ["ln0_g", "ln0_b", "w_ada", "b_ada", "w_in", "dn_conv", "dn_a_log", "dn_dt_bias", "dn_norm_g",
           "gla_w_gate2", "gla_b_gate", "gla_norm_g", "w_o", "ln1_g", "ln1_b", "ffn_w_up", "ffn_conv", "ffn_conv_b",
           "ffn_w_down", "ln2_g", "ln2_b"]


def kernel(x, c, ln0_g, ln0_b, w_ada, b_ada, w_in, dn_conv, dn_a_log, dn_dt_bias, dn_norm_g, gla_w_gate2, gla_b_gate, gla_norm_g, w_o, ln1_g, ln1_b, ffn_w_up, ffn_conv, ffn_conv_b, ffn_w_down, ln2_g, ln2_b, loss_target, m_ln0_g, m_ln0_b, m_w_ada, m_b_ada, m_w_in, m_dn_conv, m_dn_a_log, m_dn_dt_bias, m_dn_norm_g, m_gla_w_gate2, m_gla_b_gate, m_gla_norm_g, m_w_o, m_ln1_g, m_ln1_b, m_ffn_w_up, m_ffn_conv, m_ffn_conv_b, m_ffn_w_down, m_ln2_g, m_ln2_b, v_ln0_g, v_ln0_b, v_w_ada, v_b_ada, v_w_in, v_dn_conv, v_dn_a_log, v_dn_dt_bias, v_dn_norm_g, v_gla_w_gate2, v_gla_b_gate, v_gla_norm_g, v_w_o, v_ln1_g, v_ln1_b, v_ffn_w_up, v_ffn_conv, v_ffn_conv_b, v_ffn_w_down, v_ln2_g, v_ln2_b):
    args = dict(locals())
    w_given = {n: args[n] for n in WEIGHTS}
    m_given = {n: args["m_" + n] for n in WEIGHTS}
    v_given = {n: args["v_" + n] for n in WEIGHTS}
    bsz, t_total, _ = x.shape
    ntok = bsz * t_total
    mx, my, mc = _position()
    me = _slot(mx, my, mc)

    pack1 = jnp.concatenate([c.reshape(-1), dn_conv.reshape(-1), gla_w_gate2.reshape(-1), ffn_conv.reshape(-1)])
    n1 = pack1.size
    rows1 = -(-n1 // 1024) * 8
    pack1 = jnp.pad(pack1, (0, rows1 * 128 - n1)).reshape(rows1, 128)
    got1 = _gather_small(pack1, "gather_cond").reshape(N_DEV, -1)
    o1 = bsz * D
    o2 = o1 + dn_conv.size
    o3 = o2 + gla_w_gate2.size
    c_all = got1[:, :o1].reshape(N_DEV * bsz, D)
    dn_conv_f = got1[:, o1:o2].reshape(N_DEV, DN_CONV_K, -1).transpose(1, 0, 2).reshape(DN_CONV_K, QKV_W)
    gate2_f = got1[:, o2:o3].reshape(N_DEV, GATE_RANK, -1).transpose(1, 0, 2).reshape(GATE_RANK, HEADS * GLA_KEY)
    ffn_conv_f = got1[:, o3:n1].reshape(N_DEV, FFN_CONV_K, -1).transpose(1, 0, 2).reshape(FFN_CONV_K, 2 * D_FF)

    win_t = w_in[0].T.astype(MXU_DT)
    wup_t = ffn_w_up[0].T.astype(MXU_DT)
    (win_all,) = _gather_big([win_t])
    win_p = _pad_in_rows(win_all.reshape(IN_W, D))
    cw_p, cb_p = _ffn_pair(ffn_conv_f, 1), _ffn_pair(ffn_conv_b, 1)

    ncol = w_ada.shape[2]
    b_cols = lax.dynamic_slice_in_dim(b_ada, me * ncol, ncol, axis=1)
    mod_part = _ada_fwd(c_all, w_ada[0], b_cols)
    mod_all = _gather_small(mod_part.reshape(-1, 128), "gather_mod").reshape(N_DEV, N_DEV * bsz, ncol)
    mod = lax.dynamic_slice_in_dim(mod_all, me * bsz, bsz, axis=1).transpose(1, 0, 2).reshape(bsz, 6, 1, D)
    late = [w_o[0].astype(MXU_DT), wup_t, ffn_w_down[0].astype(MXU_DT)]
    ag_send, ag_recv, ag_src, ag_land, ag_token, _ = _exchange_start(
        "gather_start", late, [w.shape for w in late], [win_all, mod_all], by_owner=False)
    mod = mod + ag_token[0, 0]
    sh_a, sc_a, gt_a, sh_f, sc_f, gt_f = (mod[:, i] for i in range(6))

    g0, b0 = ln0_g.reshape(1, D), ln0_b.reshape(1, D)
    alog_row, dt_row = _lane_row(dn_a_log[0]), _lane_row(dn_dt_bias[0])
    grow_dn, grow_gla = jnp.tile(dn_norm_g, (1, HEADS)), jnp.tile(gla_norm_g, (1, HEADS))
    w2 = jnp.zeros((128, HEADS * HD), F32).at[SM_R:SM_R + GATE_RANK].set(_pad_heads(gate2_f, 1))
    bg = _pad_heads(gla_b_gate, 1)

    h_a = _ln0_mod(x, g0, b0, sc_a, sh_a)
    proj = _mm(h_a.reshape(ntok, D), win_p, "nt", F32, "mm_proj", tm=1024, tn=1408).reshape(bsz, t_total, P_W)
    q, k, v, gates = _dn_pre_fwd(proj, dn_conv_f, alog_row, dt_row)
    o_dn, s_dn, inv_dn = _dn_rec_fwd(q, k, v, gates)
    o_gla, s_gla = _gla_rec_fwd(proj, w2, bg)
    o_mix = _mix_out_fwd(o_dn, o_gla, proj, grow_dn, grow_gla)
    landed = _exchange_wait("gather_wait", ag_send, ag_recv, ag_src, ag_land, o_mix, by_owner=False)
    wo_all, wup_all, wdn_all = (lax.dynamic_update_slice(l, w[None], (me, 0, 0)) for l, w in zip(landed, late))
    wo_f = wo_all.reshape(D, D)
    wup_f = _ffn_pair(wup_all.reshape(2 * D_FF, D), 0)
    wdn_f = wdn_all.reshape(D_FF, D)
    y = _mm(o_mix.reshape(ntok, D), wo_f, "nn", F32, "mm_wo", tm=1024, tn=1024).reshape(bsz, t_total, D)
    r1, h_f = _res_ln_mod(x, y, gt_a, g0, b0, ln1_g, ln1_b, sc_f, sh_f)
    up, act = _ffn_up_act(h_f, wup_f, cw_p, cb_p)
    y2 = _mm(act.reshape(ntok, D_FF), wdn_f, "nn", F32, "mm_down", tm=1024, tn=1024).reshape(bsz, t_total, D)
    loss_rows, dr2, dy2, dgt_f, d_ln2_g, d_ln2_b = _final_fwd_bwd(r1, y2, gt_f, ln1_g, ln1_b, ln2_g, ln2_b, loss_target)
    loss_part = (0.5 / D) * jnp.sum(loss_rows)

    dy2_2 = dy2.reshape(ntok, D)
    g_wdn = _mm(act.reshape(ntok, D_FF), dy2_2, "tn", MXU_DT, "mm_gwdn", tm=1408, tn=1024)
    dup, d_cw_p, d_cb_p = _ffn_act_bwd(up, dy2, wdn_f, cw_p, cb_p)
    d_ffn_conv, d_ffn_conv_b = _ffn_unpair(d_cw_p, 1), _ffn_unpair(d_cb_p, 1)
    dup_2 = dup.reshape(ntok, 2 * D_FF)
    dh_f = _mm(dup_2, wup_f, "nn", F32, "mm_dhf", tn=1024).reshape(bsz, t_total, D)
    g_wup_t = _mm(dup_2, h_f.reshape(ntok, D), "tn", MXU_DT, "mm_gwup", tm=1408, tn=1024)
    ffn_parts = [_ffn_unpair(g_wup_t, 0).reshape(N_DEV, -1, D), g_wdn.reshape(N_DEV, -1, D)]
    rs_send, rs_recv, rs_src, rs_land, rs_token, _ = _exchange_start(
        "scatter_start", ffn_parts, [p.shape[1:] for p in ffn_parts], [dh_f], by_owner=True)
    dr1, dsc_f, dsh_f, d_ln1_g, d_ln1_b, dy, dgt_a = _ln_bwd_call(
        "ln1_bwd", dr2, dh_f, r1, ln1_g, ln1_b, sc_f + rs_token[0, 0], y=y, gt=gt_a)

    dy_2 = dy.reshape(ntok, D)
    do = _mm(dy_2, wo_f, "nt", F32, "mm_do", tm=1024, tn=1024).reshape(bsz, t_total, D)
    g_wo = _mm(o_mix.reshape(ntok, D), dy_2, "tn", MXU_DT, "mm_gwo", tm=512, tn=1024)
    do_dn, do_gla, dz, dgg, d_dn_norm, d_gla_norm = _mix_out_bwd(do, o_dn, o_gla, proj, grow_dn, grow_gla)
    dq, dk, dv, dgates = _dn_rec_bwd(q, k, v, gates, s_dn, inv_dn, do_dn)
    dqkv, dsm_dn, d_dn_conv, d_alog_row, d_dt_row = _dn_pre_bwd(proj, dq, dk, dv, dgates, dn_conv_f, alog_row, dt_row)
    dgq, dgk, dgv, dsm, d_w2, d_bg = _gla_rec_bwd(proj, w2, bg, s_gla, do_gla, dsm_dn)
    dproj = jnp.concatenate([dqkv, dz, dgq, dgk, dgv, dgg, dsm], axis=-1).reshape(ntok, P_W)
    g_win_p = _mm(dproj, h_a.reshape(ntok, D), "tn", MXU_DT, "mm_gwin", tm=1408, tn=1024)
    mix_parts = [_unpad_in_rows(g_win_p).reshape(N_DEV, -1, D), g_wo.reshape(N_DEV, -1, D)]
    rs2_send, rs2_recv, rs2_src, rs2_land, rs2_token, (win_p_late,) = _exchange_start(
        "scatter_mix_start", mix_parts, [p.shape[1:] for p in mix_parts], [], by_owner=True, carry=[win_p])
    dh_a = _mm(dproj, win_p_late, "nn", F32, "mm_dha", tn=1024).reshape(bsz, t_total, D)
    grad_x, dsc_a, dsh_a, d_ln0_g, d_ln0_b = _ln_bwd_call(
        "ln0_bwd", dr1, dh_a, x, g0, b0, sc_a + rs2_token[0, 0])

    def owner_sum(landed, parts, tag):
        full = [lax.dynamic_update_slice(l, lax.dynamic_slice_in_dim(p, me, 1, axis=0), (me, 0, 0))
                for l, p in zip(landed, parts)]
        return [_sum_slots(f, f"sum_{tag}_{i}") for i, f in enumerate(full)]

    ffn_landed = _exchange_wait("scatter_wait", rs_send, rs_recv, rs_src, rs_land, grad_x, by_owner=True)
    g_wup_ts, g_wdn_s = owner_sum(ffn_landed, ffn_parts, "ffn")
    mix_landed = _exchange_wait("scatter_mix_wait", rs2_send, rs2_recv, rs2_src, rs2_land, grad_x, by_owner=True)
    g_win_t, g_wo_s = owner_sum(mix_landed, mix_parts, "mix")

    dmod = jnp.concatenate([dsh_a, dsc_a, dgt_a, dsh_f, dsc_f, dgt_f], axis=1).reshape(-1)
    small_parts = {
        "ln0_g": d_ln0_g, "ln0_b": d_ln0_b, "ln1_g": d_ln1_g, "ln1_b": d_ln1_b, "ln2_g": d_ln2_g, "ln2_b": d_ln2_b,
        "dn_a_log": d_alog_row[:, :HEADS], "dn_dt_bias": d_dt_row[:, :HEADS],
        "dn_norm_g": d_dn_norm, "gla_norm_g": d_gla_norm, "gla_b_gate": _unpad_heads(d_bg, 1),
        "ffn_conv_b": d_ffn_conv_b, "dn_conv": d_dn_conv,
        "gla_w_gate2": _unpad_heads(d_w2[SM_R:SM_R + GATE_RANK], 1), "ffn_conv": d_ffn_conv}
    order = sorted(small_parts)
    flat = jnp.concatenate([small_parts[n].reshape(-1) for n in order] + [loss_part.reshape(1), dmod])
    n3 = flat.size
    rows3 = -(-n3 // 1024) * 8
    pack3 = jnp.pad(flat, (0, rows3 * 128 - n3)).reshape(rows3, 128)
    got3 = _gather_small(pack3, "gather_small_grads")
    tot3 = _sum_slots(got3, "sum_small_grads").reshape(-1)
    grads = {}
    off = 0
    for n in order:
        size = small_parts[n].size
        grads[n] = tot3[off:off + size]
        off += size
    loss = tot3[off]
    off += 1
    dmod_all = got3.reshape(N_DEV, -1)[:, off:off + dmod.size].reshape(N_DEV * bsz, 6 * D)
    dmod_cols = lax.dynamic_slice_in_dim(dmod_all, me * ncol, ncol, axis=1)
    g_wada, g_bada = _ada_bwd(c_all, dmod_all, dmod_cols)
    grads["b_ada"] = g_bada

    def col_shard(full, rows):
        part = full.reshape(rows, -1)
        width = part.shape[1] // N_DEV
        return lax.dynamic_slice_in_dim(part, me * width, width, axis=1)

    grads["dn_conv"] = col_shard(grads["dn_conv"], DN_CONV_K)
    grads["gla_w_gate2"] = col_shard(grads["gla_w_gate2"], GATE_RANK)
    grads["ffn_conv"] = col_shard(grads["ffn_conv"], FFN_CONV_K)
    grads = {n: g.reshape(w_given[n].shape) for n, g in grads.items()}
    grads["w_ada"] = g_wada.reshape(w_ada.shape)
    grads["w_in"] = g_win_t.T.reshape(w_in.shape)
    grads["w_o"] = g_wo_s.reshape(w_o.shape)
    grads["ffn_w_up"] = g_wup_ts.T.reshape(ffn_w_up.shape)
    grads["ffn_w_down"] = g_wdn_s.reshape(ffn_w_down.shape)

    delta, new_m, new_v = {}, {}, {}
    for n in ["w_ada", "w_in", "w_o", "ffn_w_up", "ffn_w_down"]:
        delta[n], new_m[n], new_v[n] = _adamw(w_given[n], grads[n], m_given[n], v_given[n], "adamw_" + n)
    d_s, m_s, v_s = _adamw_many(*[[src[n] for n in SMALL_NAMES] for src in (w_given, grads, m_given, v_given)],
                                "adamw_small")
    for i, n in enumerate(SMALL_NAMES):
        delta[n], new_m[n], new_v[n] = d_s[i], m_s[i], v_s[i]

    return (loss, grad_x, *[grads[n] for n in WEIGHTS], *[delta[n] for n in WEIGHTS],
            *[new_m[n] for n in WEIGHTS], *[new_v[n] for n in WEIGHTS])
```

```python
import functools

import jax
import jax.numpy as jnp
from jax import lax
from jax.experimental import pallas as pl
from jax.experimental.pallas import tpu as pltpu

F32 = jnp.float32
MXU_DT = jnp.bfloat16
MESH = pl.DeviceIdType.MESH
N_DEV = 8

D = 1024
HEADS = 4
HD = 128
CHUNK = 64
GLA_KEY = 64
GLA_TAU = 16.0
GATE_RANK = 16
D_FF = 2816
IN_W = 3608
ALPHA = 2.0 ** 0.25
EPS = 1e-6
DN_CONV_K = 4
FFN_CONV_K = 3
HALO = 8

P_QKV, P_Z, P_GQ, P_GK, P_GV, P_GG, P_SM, P_W = 0, 1536, 2048, 2560, 3072, 3584, 4096, 4224
SM_A, SM_B, SM_R = 0, 4, 8

ADAM_LR, ADAM_B1, ADAM_B2, ADAM_EPS, ADAM_WD, ADAM_STEP = 0.001, 0.9, 0.999, 1e-08, 0.01, 10

VMEM_LIMIT_V7X = 56 * 1024 * 1024


def _params(sem=None):
    return pltpu.CompilerParams(dimension_semantics=sem, vmem_limit_bytes=VMEM_LIMIT_V7X)


def _dg(a, b, dims, prec=None):
    return lax.dot_general(a, b, (dims, ((), ())), precision=prec, preferred_element_type=F32)


def _dot(a, b, prec=None):
    return _dg(a, b, ((1,), (0,)), prec)


def _dot_nt(a, b, prec=None):
    return _dg(a, b, ((1,), (1,)), prec)


def _dot_tn(a, b, prec=None):
    return _dg(a, b, ((0,), (0,)), prec)


def _iota(shape, dim):
    return lax.broadcasted_iota(jnp.int32, shape, dim)


def _sigmoid(x):
    return jax.nn.sigmoid(x)


def _silu(x):
    return x * _sigmoid(x)


def _softplus(x):
    return jnp.maximum(x, 0.0) + jnp.log(1.0 + jnp.exp(-jnp.abs(x)))


def _ln_stats(x):
    mu = jnp.mean(x, axis=-1, keepdims=True)
    xc = x - mu
    rstd = lax.rsqrt(jnp.mean(xc * xc, axis=-1, keepdims=True) + EPS)
    return xc * rstd, rstd


def _ln_bwd(dxhat, xhat, rstd):
    return rstd * (dxhat - jnp.mean(dxhat, axis=-1, keepdims=True)
                   - xhat * jnp.mean(dxhat * xhat, axis=-1, keepdims=True))


NN, NT, TN = ((1,), (0,)), ((1,), (1,)), ((0,), (0,))


def _split2(a):
    hi = a.astype(jnp.bfloat16)
    return hi, (a - hi.astype(F32)).astype(jnp.bfloat16)


def _d3(a, b, dims):
    ah, al = _split2(a)
    bh, bl = _split2(b)
    return _dg(ah, bh, dims) + (_dg(ah, bl, dims) + _dg(al, bh, dims))


@jax.custom_vjp
def _dot3(a, b):
    return _d3(a, b, NN)


_dot3.defvjp(lambda a, b: (_d3(a, b, NN), (a, b)),
             lambda res, g: (_d3(g, res[1], NT), _d3(res[0], g, TN)))


def _split3(b):
    b1 = b.astype(jnp.bfloat16)
    r1 = b - b1.astype(F32)
    b2 = r1.astype(jnp.bfloat16)
    return b1, b2, (r1 - b2.astype(F32)).astype(jnp.bfloat16)


def _sum3(fn, b):
    b1, b2, b3 = _split3(b)
    return fn(b1) + (fn(b2) + fn(b3))


@jax.custom_vjp
def _mask_dot(e, b):
    return _sum3(lambda t: _dg(e, t, NN), b)


_mask_dot.defvjp(lambda e, b: (_mask_dot(e, b), e),
                 lambda e, g: (jnp.zeros_like(e), _sum3(lambda t: _dg(e, t, TN), g)))


@jax.custom_vjp
def _mask_dot_nt(e, b):
    return _sum3(lambda t: _dg(e, t, NT), b)


_mask_dot_nt.defvjp(lambda e, b: (_mask_dot_nt(e, b), e),
                    lambda e, g: (jnp.zeros_like(e), _sum3(lambda t: _dg(t, e, TN), g)))


def _tri_inv_impl(ms):
    n = ms[0].shape[0]
    r, c = _iota((n, n), 0), _iota((n, n), 1)
    eye = (r == c).astype(F32)
    diag = (r >> 3) == (c >> 3)
    ds = [jnp.where(diag, m, 0.0) for m in ms]
    d2s = [_d3(d, d, NN) for d in ds]
    d4s = [_d3(d2, d2, NN) for d2 in d2s]
    invs = [_d3(eye - d, eye + d2, NN) for d, d2 in zip(ds, d2s)]
    invs = [_d3(inv, eye + d4, NN) for inv, d4 in zip(invs, d4s)]
    shift = 3
    while (1 << shift) < n:
        rb, cb = r >> shift, c >> shift
        sel = ((rb & 1) == 1) & (cb == rb - 1)
        tmp = [_d3(inv, jnp.where(sel, m, 0.0), NN) for inv, m in zip(invs, ms)]
        invs = [inv - _d3(t, inv, NN) for t, inv in zip(tmp, invs)]
        shift += 1
    return invs


@jax.custom_vjp
def _tri_inv(ms):
    return _tri_inv_impl(ms)


def _tri_inv_fwd(ms):
    invs = _tri_inv_impl(ms)
    return invs, invs


def _tri_inv_bwd(invs, das):
    tmp = [_d3(a, da, TN) for a, da in zip(invs, das)]
    return ([-_d3(t, a, NT) for t, a in zip(tmp, invs)],)


_tri_inv.defvjp(_tri_inv_fwd, _tri_inv_bwd)


@jax.custom_vjp
def _tri_inv_known(ms, invs):
    return invs


_tri_inv_known.defvjp(lambda ms, invs: (invs, invs),
                      lambda invs, das: (_tri_inv_bwd(invs, das)[0], [jnp.zeros_like(a) for a in invs]))


def _dn_chunk(s_list, q, k, v, gates, inv_known=None, with_inv=False):
    nb = len(q)
    c = q[0].shape[0]
    r64, c64 = _iota((c, c), 0), _iota((c, c), 1)
    causal = r64 >= c64
    strict = r64 > c64
    tri = causal.astype(jnp.bfloat16)
    eye = (_iota((HD, HD), 0) == _iota((HD, HD), 1)).astype(jnp.bfloat16)
    lane = _iota(gates[0].shape, 1)
    lane1 = _iota((1, HD), 1)
    g_all = [_mask_dot(tri, g) for g in gates]
    g_all_t = [_mask_dot_nt(eye, g) for g in g_all]
    row = _iota(g_all_t[0].shape, 0)
    last = [jnp.sum(g, axis=0, keepdims=True) for g in gates]
    prob = [(b, h) for b in range(nb) for h in range(HEADS)]
    sl = [slice(h * HD, (h + 1) * HD) for h in range(HEADS)]
    qh = [q[b][:, sl[h]] for b, h in prob]
    kh = [k[b][:, sl[h]] for b, h in prob]
    vh = [v[b][:, sl[h]] for b, h in prob]
    s = [s_list[b][h] for b, h in prob]
    beta = [jnp.sum(jnp.where(lane == SM_B + h, gates[b], 0.0), axis=-1, keepdims=True) for b, h in prob]
    g_c = [jnp.sum(jnp.where(lane == SM_A + h, g_all[b], 0.0), axis=-1, keepdims=True) for b, h in prob]
    g_r = [jnp.sum(jnp.where(row == SM_A + h, g_all_t[b], 0.0), axis=0, keepdims=True) for b, h in prob]
    g_last = [jnp.sum(jnp.where(lane1 == SM_A + h, last[b], 0.0), axis=-1, keepdims=True) for b, h in prob]
    decay = [jnp.where(causal, jnp.exp(jnp.where(causal, gc - gr, 0.0)), 0.0) for gc, gr in zip(g_c, g_r)]
    kb = [k_ * b_ for k_, b_ in zip(kh, beta)]
    m_low = [jnp.where(strict, _dot_nt(kb_, k_) * d_, 0.0) for kb_, k_, d_ in zip(kb, kh, decay)]
    attn = [_dot_nt(q_, k_) * d_ for q_, k_, d_ in zip(qh, kh, decay)]
    a_inv = _tri_inv(m_low) if inv_known is None else _tri_inv_known(m_low, inv_known)
    eg = [jnp.exp(gc) for gc in g_c]
    uw = [_dot3(a_, jnp.concatenate([v_ * b_, kb_ * e_], axis=1))
          for a_, v_, b_, kb_, e_ in zip(a_inv, vh, beta, kb, eg)]
    v_new = [uw_[:, :HD] - _dot(uw_[:, HD:], s_) for uw_, s_ in zip(uw, s)]
    qs = [_dot(q_ * e_, s_) for q_, e_, s_ in zip(qh, eg, s)]
    o = [qs_ + _dot(a_, vn_) for qs_, a_, vn_ in zip(qs, attn, v_new)]
    k_dec = [k_ * jnp.exp(gl - gc) for k_, gl, gc in zip(kh, g_last, g_c)]
    s_new = [s_ * jnp.exp(gl) + _dot_tn(kd_, vn_) for s_, gl, kd_, vn_ in zip(s, g_last, k_dec, v_new)]
    outs = [jnp.concatenate(o[b * HEADS:(b + 1) * HEADS], axis=-1) for b in range(nb)]
    states = [s_new[b * HEADS:(b + 1) * HEADS] for b in range(nb)]
    return (outs, states, a_inv) if with_inv else (outs, states)


def _gla_chunk(st_list, q, k, v, small, w2, bg):
    nb = len(q)
    c = q[0].shape[0]
    causal = _iota((c, c), 0) >= _iota((c, c), 1)
    tri = causal.astype(jnp.bfloat16)
    la_all = [-_softplus(-(_dot(sm, w2) + bg)) * (1.0 / GLA_TAU) for sm in small]
    b_all = [_mask_dot(tri, la) for la in la_all]
    prob = [(b, h) for b in range(nb) for h in range(HEADS)]
    sl = [slice(h * HD, (h + 1) * HD) for h in range(HEADS)]
    kh = [k[b][:, sl[h]] for b, h in prob]
    vh = [v[b][:, sl[h]] for b, h in prob]
    st = [st_list[b][h] for b, h in prob]
    bc = [b_all[b][:, sl[h]] for b, h in prob]
    b_last = [jnp.sum(la_all[b][:, sl[h]], axis=0, keepdims=True) for b, h in prob]
    q_dec = [q[b][:, sl[h]] * (GLA_KEY ** -0.5) * jnp.exp(bc_) for (b, h), bc_ in zip(prob, bc)]
    attn = [jnp.where(causal, _dot_nt(qd, k_ * jnp.exp(-bc_)), 0.0) for qd, k_, bc_ in zip(q_dec, kh, bc)]
    inter = [_dot_nt(qd, st_) for qd, st_ in zip(q_dec, st)]
    o = [i_ + _dot(a_, v_) for i_, a_, v_ in zip(inter, attn, vh)]
    k_dec = [k_ * jnp.exp(bl - bc_) for k_, bl, bc_ in zip(kh, b_last, bc)]
    s_new = [st_ * jnp.exp(bl) + _dot_tn(v_, kd) for st_, bl, v_, kd in zip(st, b_last, vh, k_dec)]
    outs = [jnp.concatenate(o[b * HEADS:(b + 1) * HEADS], axis=-1) for b in range(nb)]
    return outs, [s_new[b * HEADS:(b + 1) * HEADS] for b in range(nb)]


def _dn_qkv(y):
    act = _silu(y)
    parts = []
    for i in range(2 * HEADS):
        xh = act[:, i * HD:(i + 1) * HD]
        xh = xh * lax.rsqrt(jnp.sum(xh * xh, axis=-1, keepdims=True) + EPS)
        parts.append(xh * (HD ** -0.5) if i < HEADS else xh)
    qk = jnp.concatenate(parts, axis=-1)
    return qk[:, :HEADS * HD], qk[:, HEADS * HD:], act[:, 2 * HEADS * HD:]


def _dn_gates(small, alog_row, dt_row):
    lane = _iota(small.shape, 1)
    log_a = -jnp.exp(alog_row) * _softplus(small + dt_row)
    return jnp.where(lane < SM_B, log_a, jnp.where(lane < SM_R, _sigmoid(small), 0.0))


def _gate_norm(o, z, grow):
    parts = []
    for h in range(HEADS):
        oh = o[:, h * HD:(h + 1) * HD]
        parts.append(oh * lax.rsqrt(jnp.mean(oh * oh, axis=-1, keepdims=True) + EPS))
    return jnp.concatenate(parts, axis=-1) * grow * _silu(z)


def _conv_rows(xrows, w_ref, k_taps):
    n = xrows.shape[0]
    acc = xrows * w_ref[k_taps - 1:k_taps, :]
    for s in range(1, k_taps):
        acc = acc + pltpu.roll(xrows, s, 0) * w_ref[k_taps - 1 - s:k_taps - s, :]
    return acc


def _shift_up(x, s):
    return x if s == 0 else pltpu.roll(x, x.shape[0] - s, 0)


def _div_tile(n, cap, mult=8):
    best = None
    for t in range(mult, min(n, cap) + 1, mult):
        if n % t == 0:
            best = t
    return best if best is not None else n


def _halo_prev(tt):
    return lambda b, t: (b, jnp.maximum(t * (tt // HALO) - 1, 0))


def _halo_next(tt, t_total):
    return lambda b, t: (b, jnp.minimum((t + 1) * (tt // HALO), t_total // HALO - 1))


def _mm(a, b, mode, out_dtype, name, tm=512, tn=512, tk=None):
    if mode == "nn":
        (m, k), n = a.shape, b.shape[1]
    elif mode == "nt":
        (m, k), n = a.shape, b.shape[0]
    else:
        (k, m), n = a.shape, b.shape[1]
    tm, tn = min(tm, m), min(tn, n)
    tk = k if tk is None else min(tk, k)
    assert m % tm == 0 and n % tn == 0 and k % tk == 0, (name, a.shape, b.shape, tm, tn, tk)
    nk = k // tk
    if mode == "tn":
        a_spec = pl.BlockSpec((tk, tm), lambda i, j, kk: (kk, i))
    else:
        a_spec = pl.BlockSpec((tm, tk), lambda i, j, kk: (i, kk))
    if mode == "nt":
        b_spec = pl.BlockSpec((tn, tk), lambda i, j, kk: (j, kk))
    else:
        b_spec = pl.BlockSpec((tk, tn), lambda i, j, kk: (kk, j))
    dims = {"nn": ((1,), (0,)), "nt": ((1,), (1,)), "tn": ((0,), (0,))}[mode]

    def body(a_ref, b_ref, o_ref, *acc):
        p = _dg(a_ref[...], b_ref[...], dims)
        if nk == 1:
            o_ref[...] = p.astype(out_dtype)
        else:
            kk = pl.program_id(2)

            @pl.when(kk == 0)
            def _():
                acc[0][...] = p

            @pl.when(kk > 0)
            def _():
                acc[0][...] += p

            @pl.when(kk == nk - 1)
            def _():
                o_ref[...] = acc[0][...].astype(out_dtype)

    return pl.pallas_call(
        body, name=name, grid=(m // tm, n // tn, nk),
        in_specs=[a_spec, b_spec],
        out_specs=pl.BlockSpec((tm, tn), lambda i, j, kk: (i, j)),
        out_shape=jax.ShapeDtypeStruct((m, n), out_dtype),
        scratch_shapes=[pltpu.VMEM((tm, tn), F32)] if nk > 1 else [],
        compiler_params=_params(("parallel", "parallel", "arbitrary")),
    )(a, b)


def _ada_fwd(c_all, w_ada, b_cols):
    def body(c_ref, w_ref, b_ref, o_ref):
        cond = _silu(c_ref[...]).astype(MXU_DT)
        o_ref[...] = _dot(cond, w_ref[...].astype(MXU_DT)) + b_ref[...]

    return pl.pallas_call(body, name="ada_fwd", out_shape=jax.ShapeDtypeStruct((c_all.shape[0], w_ada.shape[1]), F32),
                          compiler_params=_params())(c_all, w_ada, b_cols)


def _ada_bwd(c_all, dmod_all, dmod_cols):
    def body(c_ref, da_ref, dc_ref, gw_ref, gb_ref):
        cond = _silu(c_ref[...]).astype(MXU_DT)
        gw_ref[...] = _dot_tn(cond, dc_ref[...].astype(MXU_DT))
        gb_ref[...] = jnp.sum(da_ref[...], axis=0, keepdims=True)

    return pl.pallas_call(
        body, name="ada_bwd",
        out_shape=(jax.ShapeDtypeStruct((c_all.shape[1], dmod_cols.shape[1]), F32),
                   jax.ShapeDtypeStruct((1, dmod_all.shape[1]), F32)),
        compiler_params=_params())(c_all, dmod_all, dmod_cols)


def _tok_spec(tt, width=D):
    return pl.BlockSpec((1, tt, width), lambda b, t: (b, t, 0))


def _vec_spec(width=D):
    return pl.BlockSpec((1, width), lambda b, t: (0, 0))


def _bvec_spec(width=D):
    return pl.BlockSpec((1, 1, width), lambda b, t: (b, 0, 0))


def _ln0_mod(x, g0, b0, sc, sh):
    bsz, t_total, _ = x.shape
    tt = _div_tile(t_total, 256)

    def body(x_ref, g_ref, b_ref, sc_ref, sh_ref, h_ref):
        xh, _ = _ln_stats(x_ref[0])
        x0 = xh * g_ref[...] + b_ref[...]
        h_ref[0] = (x0 * (1.0 + sc_ref[0]) + sh_ref[0]).astype(MXU_DT)

    return pl.pallas_call(
        body, name="ln0_mod", grid=(bsz, t_total // tt),
        in_specs=[_tok_spec(tt), _vec_spec(), _vec_spec(), _bvec_spec(), _bvec_spec()],
        out_specs=_tok_spec(tt), out_shape=jax.ShapeDtypeStruct(x.shape, MXU_DT),
        compiler_params=_params(("parallel", "parallel")))(x, g0, b0, sc, sh)


def _res_ln_mod(x, y, gt, g0, b0, g1, b1, sc, sh):
    bsz, t_total, _ = x.shape
    tt = _div_tile(t_total, 256)

    def body(x_ref, y_ref, gt_ref, g0_ref, b0_ref, g1_ref, b1_ref, sc_ref, sh_ref, r_ref, h_ref):
        xh, _ = _ln_stats(x_ref[0])
        r = ALPHA * (xh * g0_ref[...] + b0_ref[...]) + (1.0 + gt_ref[0]) * y_ref[0]
        r_ref[0] = r
        rh, _ = _ln_stats(r)
        x1 = rh * g1_ref[...] + b1_ref[...]
        h_ref[0] = (x1 * (1.0 + sc_ref[0]) + sh_ref[0]).astype(MXU_DT)

    return pl.pallas_call(
        body, name="res_ln_mod", grid=(bsz, t_total // tt),
        in_specs=[_tok_spec(tt), _tok_spec(tt), _bvec_spec(), _vec_spec(), _vec_spec(), _vec_spec(), _vec_spec(),
                  _bvec_spec(), _bvec_spec()],
        out_specs=(_tok_spec(tt), _tok_spec(tt)),
        out_shape=(jax.ShapeDtypeStruct(x.shape, F32), jax.ShapeDtypeStruct(x.shape, MXU_DT)),
        compiler_params=_params(("parallel", "parallel")))(x, y, gt, g0, b0, g1, b1, sc, sh)


def _final_fwd_bwd(r1, y2, gt, g1, b1, g2, b2, target):
    bsz, t_total, _ = r1.shape
    tt = _div_tile(t_total, 256)

    def body(r1_ref, y2_ref, gt_ref, g1_ref, b1_ref, g2_ref, b2_ref, tg_ref,
             loss_ref, dr2_ref, dy2_ref, dgt_ref, dg2_ref, db2_ref):
        b, t = pl.program_id(0), pl.program_id(1)

        @pl.when((b == 0) & (t == 0))
        def _():
            loss_ref[...] = jnp.zeros_like(loss_ref)
            dg2_ref[...] = jnp.zeros_like(dg2_ref)
            db2_ref[...] = jnp.zeros_like(db2_ref)

        @pl.when(t == 0)
        def _():
            dgt_ref[...] = jnp.zeros_like(dgt_ref)

        rh1, _ = _ln_stats(r1_ref[0])
        x1 = rh1 * g1_ref[...] + b1_ref[...]
        y2 = y2_ref[0]
        gate = 1.0 + gt_ref[0]
        xh2, rstd2 = _ln_stats(ALPHA * x1 + gate * y2)
        err = xh2 * g2_ref[...] + b2_ref[...] - tg_ref[0]
        loss_ref[...] += jnp.sum(err * err, axis=0, keepdims=True)
        dx2 = err * (1.0 / D)
        dg2_ref[...] += jnp.sum(dx2 * xh2, axis=0, keepdims=True)
        db2_ref[...] += jnp.sum(dx2, axis=0, keepdims=True)
        dr2 = _ln_bwd(dx2 * g2_ref[...], xh2, rstd2)
        dr2_ref[0] = dr2
        dy2_ref[0] = (gate * dr2).astype(MXU_DT)
        dgt_ref[0] += jnp.sum(dr2 * y2, axis=0, keepdims=True)

    vec_out = jax.ShapeDtypeStruct((1, D), F32)
    return pl.pallas_call(
        body, name="final_fwd_bwd", grid=(bsz, t_total // tt),
        in_specs=[_tok_spec(tt), _tok_spec(tt), _bvec_spec(), _vec_spec(), _vec_spec(), _vec_spec(), _vec_spec(),
                  _tok_spec(tt)],
        out_specs=(_vec_spec(), _tok_spec(tt), _tok_spec(tt), _bvec_spec(), _vec_spec(), _vec_spec()),
        out_shape=(vec_out, jax.ShapeDtypeStruct(r1.shape, F32), jax.ShapeDtypeStruct(r1.shape, MXU_DT),
                   jax.ShapeDtypeStruct((bsz, 1, D), F32), vec_out, vec_out),
        compiler_params=_params(("arbitrary", "arbitrary")))(r1, y2, gt, g1, b1, g2, b2, target)


def _ln_bwd_call(name, d_res, d_h, src, g, b, sc, y=None, gt=None):
    bsz, t_total, _ = src.shape
    tt = _div_tile(t_total, 256)
    has_y = y is not None

    def body(*refs):
        if has_y:
            (dres_ref, dh_ref, src_ref, g_ref, b_ref, sc_ref, y_ref, gt_ref,
             dsrc_ref, dsc_ref, dsh_ref, dg_ref, db_ref, dy_ref, dgt_ref) = refs
        else:
            (dres_ref, dh_ref, src_ref, g_ref, b_ref, sc_ref,
             dsrc_ref, dsc_ref, dsh_ref, dg_ref, db_ref) = refs
        bi, t = pl.program_id(0), pl.program_id(1)

        @pl.when((bi == 0) & (t == 0))
        def _():
            dg_ref[...] = jnp.zeros_like(dg_ref)
            db_ref[...] = jnp.zeros_like(db_ref)

        @pl.when(t == 0)
        def _():
            dsc_ref[...] = jnp.zeros_like(dsc_ref)
            dsh_ref[...] = jnp.zeros_like(dsh_ref)
            if has_y:
                dgt_ref[...] = jnp.zeros_like(dgt_ref)

        xh, rstd = _ln_stats(src_ref[0])
        xv = xh * g_ref[...] + b_ref[...]
        dh = dh_ref[0]
        dx = ALPHA * dres_ref[0] + dh * (1.0 + sc_ref[0])
        dsc_ref[0] += jnp.sum(dh * xv, axis=0, keepdims=True)
        dsh_ref[0] += jnp.sum(dh, axis=0, keepdims=True)
        dg_ref[...] += jnp.sum(dx * xh, axis=0, keepdims=True)
        db_ref[...] += jnp.sum(dx, axis=0, keepdims=True)
        dsrc = _ln_bwd(dx * g_ref[...], xh, rstd)
        dsrc_ref[0] = dsrc
        if has_y:
            dy_ref[0] = ((1.0 + gt_ref[0]) * dsrc).astype(MXU_DT)
            dgt_ref[0] += jnp.sum(dsrc * y_ref[0], axis=0, keepdims=True)

    vec_out = jax.ShapeDtypeStruct((1, D), F32)
    bvec_out = jax.ShapeDtypeStruct((bsz, 1, D), F32)
    in_specs = [_tok_spec(tt), _tok_spec(tt), _tok_spec(tt), _vec_spec(), _vec_spec(), _bvec_spec()]
    out_specs = [_tok_spec(tt), _bvec_spec(), _bvec_spec(), _vec_spec(), _vec_spec()]
    out_shape = [jax.ShapeDtypeStruct(src.shape, F32), bvec_out, bvec_out, vec_out, vec_out]
    args = [d_res, d_h, src, g, b, sc]
    if has_y:
        in_specs += [_tok_spec(tt), _bvec_spec()]
        out_specs += [_tok_spec(tt), _bvec_spec()]
        out_shape += [jax.ShapeDtypeStruct(src.shape, MXU_DT), bvec_out]
        args += [y, gt]
    return pl.pallas_call(body, name=name, grid=(bsz, t_total // tt), in_specs=in_specs, out_specs=tuple(out_specs),
                          out_shape=tuple(out_shape), compiler_params=_params(("arbitrary", "arbitrary")))(*args)


FFN_TC = 256
FFN_NJ = D_FF // FFN_TC
FFN_PW = 2 * FFN_TC


def _ffn_pair(a, axis):
    shp = list(a.shape)
    a4 = a.reshape(shp[:axis] + [2, FFN_NJ, FFN_TC] + shp[axis + 1:])
    return jnp.swapaxes(a4, axis, axis + 1).reshape(shp)


def _ffn_unpair(a, axis):
    shp = list(a.shape)
    a4 = a.reshape(shp[:axis] + [FFN_NJ, 2, FFN_TC] + shp[axis + 1:])
    return jnp.swapaxes(a4, axis, axis + 1).reshape(shp)


def _ffn_up_act(h, w_up, cw, cb):
    bsz, t_total, _ = h.shape
    tt = _div_tile(t_total, 256)

    def body(h_ref, wu_ref, w_ref, b_ref, up_ref, o_ref, carry_ref):
        up_t = _dot_nt(h_ref[0], wu_ref[...])
        up_ref[0] = up_t
        prev = jnp.where(pl.program_id(2) == 0, 0.0, carry_ref[...])
        rows = jnp.concatenate([prev, up_t], axis=0)
        u = _conv_rows(rows, w_ref, FFN_CONV_K)[HALO:] + b_ref[...]
        o_ref[0] = (_silu(u[:, :FFN_TC]) * u[:, FFN_TC:]).astype(MXU_DT)
        carry_ref[...] = up_t[tt - HALO:, :]

    return pl.pallas_call(
        body, name="ffn_up_act", grid=(bsz, FFN_NJ, t_total // tt),
        in_specs=[pl.BlockSpec((1, tt, D), lambda b, j, t: (b, t, 0)),
                  pl.BlockSpec((FFN_PW, D), lambda b, j, t: (j, 0)),
                  pl.BlockSpec((FFN_CONV_K, FFN_PW), lambda b, j, t: (0, j)),
                  pl.BlockSpec((1, FFN_PW), lambda b, j, t: (0, j))],
        out_specs=(pl.BlockSpec((1, tt, FFN_PW), lambda b, j, t: (b, t, j)),
                   pl.BlockSpec((1, tt, FFN_TC), lambda b, j, t: (b, t, j))),
        out_shape=(jax.ShapeDtypeStruct((bsz, t_total, 2 * D_FF), F32),
                   jax.ShapeDtypeStruct((bsz, t_total, D_FF), MXU_DT)),
        scratch_shapes=[pltpu.VMEM((HALO, FFN_PW), F32)],
        compiler_params=_params(("parallel", "parallel", "arbitrary")))(h, w_up, cw, cb)


HALO16 = 16


def _ffn_act_bwd(up, dy2, w_down, cw, cb):
    bsz, t_total, width = up.shape
    tt = _div_tile(t_total, 256)
    nt = t_total // tt
    hp, hn = _halo_prev(tt), _halo_next(tt, t_total)

    def body(x_ref, xp_ref, xn_ref, dy_ref, dyn_ref, wd_ref, w_ref, b_ref, dup_ref, dw_ref, db_ref):
        b, t = pl.program_id(1), pl.program_id(2)

        @pl.when((b == 0) & (t == 0))
        def _():
            dw_ref[...] = jnp.zeros_like(dw_ref)
            db_ref[...] = jnp.zeros_like(db_ref)

        prev = jnp.where(t == 0, 0.0, xp_ref[0])
        rows = jnp.concatenate([prev, x_ref[0], xn_ref[0]], axis=0)
        u = _conv_rows(rows, w_ref, FFN_CONV_K)[HALO:] + b_ref[...]
        g_pre, v_pre = u[:, :FFN_TC], u[:, FFN_TC:]
        valid = (_iota((tt + HALO, 1), 0) < tt) | (t < nt - 1)
        da = jnp.concatenate([_dot_nt(dy_ref[0], wd_ref[...]), _dot_nt(dyn_ref[0], wd_ref[...])[:HALO]], axis=0)
        da_ext = jnp.where(valid, da, 0.0)
        sg = _sigmoid(g_pre)
        gs = g_pre * sg
        du = jnp.concatenate([da_ext * v_pre * (sg + gs * (1.0 - sg)), da_ext * gs], axis=1)
        dup = du * w_ref[FFN_CONV_K - 1:FFN_CONV_K, :]
        for s in range(1, FFN_CONV_K):
            dup = dup + _shift_up(du, s) * w_ref[FFN_CONV_K - 1 - s:FFN_CONV_K - s, :]
        dup_ref[0] = dup[:tt].astype(MXU_DT)
        du_t = du[:tt]
        db_ref[...] += jnp.sum(du_t, axis=0, keepdims=True)
        for k in range(FFN_CONV_K):
            s = FFN_CONV_K - 1 - k
            xs = (rows if s == 0 else pltpu.roll(rows, s, 0))[HALO:HALO + tt]
            dw_ref[k:k + 1, :] += jnp.sum(du_t * xs, axis=0, keepdims=True)

    def halo(h, w):
        return pl.BlockSpec((1, HALO, w), lambda j, b, t: (*h(b, t), j))

    wspec = lambda rows_: pl.BlockSpec((rows_, FFN_PW), lambda j, b, t: (0, j))
    tile = pl.BlockSpec((1, tt, FFN_PW), lambda j, b, t: (b, t, j))
    dy_next = lambda j, b, t: (b, jnp.minimum((t + 1) * (tt // HALO16), t_total // HALO16 - 1), 0)
    return pl.pallas_call(
        body, name="ffn_act_bwd", grid=(FFN_NJ, bsz, nt),
        in_specs=[tile, halo(hp, FFN_PW), halo(hn, FFN_PW),
                  pl.BlockSpec((1, tt, D), lambda j, b, t: (b, t, 0)), pl.BlockSpec((1, HALO16, D), dy_next),
                  pl.BlockSpec((FFN_TC, D), lambda j, b, t: (j, 0)), wspec(FFN_CONV_K), wspec(1)],
        out_specs=(tile, wspec(FFN_CONV_K), wspec(1)),
        out_shape=(jax.ShapeDtypeStruct(up.shape, MXU_DT), jax.ShapeDtypeStruct((FFN_CONV_K, width), F32),
                   jax.ShapeDtypeStruct((1, width), F32)),
        compiler_params=_params(("arbitrary", "arbitrary", "arbitrary")))(up, up, up, dy2, dy2, w_down, cw, cb)


QKV_W = 3 * HEADS * HD
SM_BLK = P_SM // 128


def _dn_pre_fwd(proj, conv_w, alog_row, dt_row):
    bsz, t_total, _ = proj.shape
    tt = _div_tile(t_total, 256)
    hp = _halo_prev(tt)

    def body(x_ref, xp_ref, sm_ref, w_ref, al_ref, dt_ref, q_ref, k_ref, v_ref, g_ref):
        prev = jnp.where(pl.program_id(1) == 0, 0.0, xp_ref[0])
        y = _conv_rows(jnp.concatenate([prev, x_ref[0]], axis=0), w_ref, DN_CONV_K)[HALO:]
        q_ref[0], k_ref[0], v_ref[0] = _dn_qkv(y)
        g_ref[0] = _dn_gates(sm_ref[0], al_ref[...], dt_ref[...])

    out512 = jax.ShapeDtypeStruct((bsz, t_total, HEADS * HD), F32)
    return pl.pallas_call(
        body, name="dn_pre_fwd", grid=(bsz, t_total // tt),
        in_specs=[pl.BlockSpec((1, tt, QKV_W), lambda b, t: (b, t, 0)),
                  pl.BlockSpec((1, HALO, QKV_W), lambda b, t: (*hp(b, t), 0)),
                  pl.BlockSpec((1, tt, 128), lambda b, t: (b, t, SM_BLK)),
                  pl.BlockSpec((DN_CONV_K, QKV_W), lambda b, t: (0, 0)), _vec_spec(128), _vec_spec(128)],
        out_specs=(_tok_spec(tt, 512), _tok_spec(tt, 512), _tok_spec(tt, 512), _tok_spec(tt, 128)),
        out_shape=(out512, out512, out512, jax.ShapeDtypeStruct((bsz, t_total, 128), F32)),
        compiler_params=_params(("parallel", "parallel")))(proj, proj, proj, conv_w, alog_row, dt_row)


def _dn_pre_bwd(proj, dq, dk, dv, dgates, conv_w, alog_row, dt_row):
    bsz, t_total, _ = proj.shape
    tt = _div_tile(t_total, 128)
    nt = t_total // tt
    hp, hn = _halo_prev(tt), _halo_next(tt, t_total)

    def body(x_ref, xp_ref, xn_ref, sm_ref, dq_ref, dqn_ref, dk_ref, dkn_ref, dv_ref, dvn_ref, dg_ref,
             w_ref, al_ref, dt_ref, dx_ref, dsm_ref, dw_ref, dal_ref, ddt_ref):
        b, t = pl.program_id(0), pl.program_id(1)

        @pl.when((b == 0) & (t == 0))
        def _():
            dw_ref[...] = jnp.zeros_like(dw_ref)
            dal_ref[...] = jnp.zeros_like(dal_ref)
            ddt_ref[...] = jnp.zeros_like(ddt_ref)

        prev = jnp.where(t == 0, 0.0, xp_ref[0])
        rows = jnp.concatenate([prev, x_ref[0], xn_ref[0]], axis=0)
        y = _conv_rows(rows, w_ref, DN_CONV_K)[HALO:]
        valid = (_iota((tt + HALO, 1), 0) < tt) | (t < nt - 1)

        def ext(tile_ref, next_ref):
            return jnp.where(valid, jnp.concatenate([tile_ref[0], next_ref[0]], axis=0), 0.0)

        _, vjp_qkv = jax.vjp(_dn_qkv, y)
        (dy,) = vjp_qkv((ext(dq_ref, dqn_ref), ext(dk_ref, dkn_ref), ext(dv_ref, dvn_ref)))
        dy = jnp.where(valid, dy, 0.0)
        dx = dy * w_ref[DN_CONV_K - 1:DN_CONV_K, :]
        for s in range(1, DN_CONV_K):
            dx = dx + _shift_up(dy, s) * w_ref[DN_CONV_K - 1 - s:DN_CONV_K - s, :]
        dx_ref[0] = dx[:tt].astype(MXU_DT)
        dy_t = dy[:tt]
        for k in range(DN_CONV_K):
            s = DN_CONV_K - 1 - k
            xs = (rows if s == 0 else pltpu.roll(rows, s, 0))[HALO:HALO + tt]
            dw_ref[k:k + 1, :] += jnp.sum(dy_t * xs, axis=0, keepdims=True)
        _, vjp_g = jax.vjp(_dn_gates, sm_ref[0], al_ref[...], dt_ref[...])
        dsm, dal, ddt = vjp_g(dg_ref[0])
        dsm_ref[0] = dsm
        dal_ref[...] += dal
        ddt_ref[...] += ddt

    def tile(width, blk=0):
        return pl.BlockSpec((1, tt, width), lambda b, t: (b, t, blk))

    def halo(h, width):
        return pl.BlockSpec((1, HALO, width), lambda b, t: (*h(b, t), 0))

    return pl.pallas_call(
        body, name="dn_pre_bwd", grid=(bsz, nt),
        in_specs=[tile(QKV_W), halo(hp, QKV_W), halo(hn, QKV_W), tile(128, SM_BLK),
                  tile(512), halo(hn, 512), tile(512), halo(hn, 512), tile(512), halo(hn, 512), tile(128),
                  pl.BlockSpec((DN_CONV_K, QKV_W), lambda b, t: (0, 0)), _vec_spec(128), _vec_spec(128)],
        out_specs=(tile(QKV_W), tile(128), pl.BlockSpec((DN_CONV_K, QKV_W), lambda b, t: (0, 0)),
                   _vec_spec(128), _vec_spec(128)),
        out_shape=(jax.ShapeDtypeStruct((bsz, t_total, QKV_W), MXU_DT), jax.ShapeDtypeStruct((bsz, t_total, 128), F32),
                   jax.ShapeDtypeStruct((DN_CONV_K, QKV_W), F32), jax.ShapeDtypeStruct((1, 128), F32),
                   jax.ShapeDtypeStruct((1, 128), F32)),
        compiler_params=_params(("arbitrary", "arbitrary")))(
            proj, proj, proj, proj, dq, dq, dk, dk, dv, dv, dgates, conv_w, alog_row, dt_row)


def _state_spec(bsz, idx):
    return pl.BlockSpec((bsz, 1, HEADS, HD, HD), lambda c: (0, idx(c), 0, 0, 0))


def _inv_spec(bsz, idx):
    return pl.BlockSpec((bsz, 1, HEADS, CHUNK, CHUNK), lambda c: (0, idx(c), 0, 0, 0))


def _chunk_spec(bsz, width, idx, blk=0):
    return pl.BlockSpec((bsz, CHUNK, width), lambda c: (0, idx(c), blk))


def _dn_rec_fwd(q, k, v, gates):
    bsz, t_total, _ = q.shape
    nc = t_total // CHUNK
    fwd = lambda c: c

    def body(q_ref, k_ref, v_ref, g_ref, o_ref, ss_ref, inv_ref, s_ref):
        @pl.when(pl.program_id(0) == 0)
        def _():
            s_ref[...] = jnp.zeros_like(s_ref)

        seqs = range(bsz)
        s_list = [[s_ref[b * HEADS + h] for h in range(HEADS)] for b in seqs]
        for b in seqs:
            for h in range(HEADS):
                ss_ref[b, 0, h] = s_list[b][h]
        o, new_s, invs = _dn_chunk(s_list, [q_ref[b] for b in seqs], [k_ref[b] for b in seqs],
                                   [v_ref[b] for b in seqs], [g_ref[b] for b in seqs], with_inv=True)
        for b in seqs:
            o_ref[b] = o[b]
            for h in range(HEADS):
                s_ref[b * HEADS + h] = new_s[b][h]
                inv_ref[b, 0, h] = invs[b * HEADS + h]

    return pl.pallas_call(
        body, name="dn_rec_fwd", grid=(nc,),
        in_specs=[_chunk_spec(bsz, 512, fwd)] * 3 + [_chunk_spec(bsz, 128, fwd)],
        out_specs=(_chunk_spec(bsz, 512, fwd), _state_spec(bsz, fwd), _inv_spec(bsz, fwd)),
        out_shape=(jax.ShapeDtypeStruct(q.shape, F32), jax.ShapeDtypeStruct((bsz, nc, HEADS, HD, HD), F32),
                   jax.ShapeDtypeStruct((bsz, nc, HEADS, CHUNK, CHUNK), F32)),
        scratch_shapes=[pltpu.VMEM((bsz * HEADS, HD, HD), F32)],
        compiler_params=_params(("arbitrary",)))(q, k, v, gates)


def _dn_rec_bwd(q, k, v, gates, states, invs, do):
    bsz, t_total, _ = q.shape
    nc = t_total // CHUNK
    rev = lambda c: nc - 1 - c

    def body(q_ref, k_ref, v_ref, g_ref, ss_ref, inv_ref, do_ref, dq_ref, dk_ref, dv_ref, dg_ref, ds_ref):
        @pl.when(pl.program_id(0) == 0)
        def _():
            ds_ref[...] = jnp.zeros_like(ds_ref)

        seqs = range(bsz)
        s_list = [[ss_ref[b, 0, h] for h in range(HEADS)] for b in seqs]
        known = [inv_ref[b, 0, h] for b in seqs for h in range(HEADS)]
        _, vjp = jax.vjp(functools.partial(_dn_chunk, inv_known=known),
                         s_list, [q_ref[b] for b in seqs], [k_ref[b] for b in seqs],
                         [v_ref[b] for b in seqs], [g_ref[b] for b in seqs])
        ds_in, dq, dk, dv, dg = vjp(([do_ref[b] for b in seqs],
                                     [[ds_ref[b * HEADS + h] for h in range(HEADS)] for b in seqs]))
        for b in seqs:
            dq_ref[b], dk_ref[b], dv_ref[b], dg_ref[b] = dq[b], dk[b], dv[b], dg[b]
            for h in range(HEADS):
                ds_ref[b * HEADS + h] = ds_in[b][h]

    tok = lambda width: _chunk_spec(bsz, width, rev)
    out512 = jax.ShapeDtypeStruct(q.shape, F32)
    return pl.pallas_call(
        body, name="dn_rec_bwd", grid=(nc,),
        in_specs=[tok(512), tok(512), tok(512), tok(128), _state_spec(bsz, rev), _inv_spec(bsz, rev), tok(512)],
        out_specs=(tok(512), tok(512), tok(512), tok(128)),
        out_shape=(out512, out512, out512, jax.ShapeDtypeStruct(gates.shape, F32)),
        scratch_shapes=[pltpu.VMEM((bsz * HEADS, HD, HD), F32)],
        compiler_params=_params(("arbitrary",)))(q, k, v, gates, states, invs, do)


GQ_BLK, GK_BLK, GV_BLK = P_GQ // 512, P_GK // 512, P_GV // 512


def _gla_rec_fwd(proj, w2, bg):
    bsz, t_total, _ = proj.shape
    nc = t_total // CHUNK

    fwd = lambda c: c

    def body(q_ref, k_ref, v_ref, sm_ref, w2_ref, bg_ref, o_ref, ss_ref, s_ref):
        @pl.when(pl.program_id(0) == 0)
        def _():
            s_ref[...] = jnp.zeros_like(s_ref)

        seqs = range(bsz)
        s_list = [[s_ref[b * HEADS + h] for h in range(HEADS)] for b in seqs]
        for b in seqs:
            for h in range(HEADS):
                ss_ref[b, 0, h] = s_list[b][h]
        o, new_s = _gla_chunk(s_list, [q_ref[b] for b in seqs], [k_ref[b] for b in seqs], [v_ref[b] for b in seqs],
                              [sm_ref[b] for b in seqs], w2_ref[...], bg_ref[...])
        for b in seqs:
            o_ref[b] = o[b]
            for h in range(HEADS):
                s_ref[b * HEADS + h] = new_s[b][h]

    col = lambda blk, width=512: _chunk_spec(bsz, width, fwd, blk)
    return pl.pallas_call(
        body, name="gla_rec_fwd", grid=(nc,),
        in_specs=[col(GQ_BLK), col(GK_BLK), col(GV_BLK), col(SM_BLK, 128),
                  pl.BlockSpec((128, 512), lambda c: (0, 0)), pl.BlockSpec((1, 512), lambda c: (0, 0))],
        out_specs=(col(0), _state_spec(bsz, fwd)),
        out_shape=(jax.ShapeDtypeStruct((bsz, t_total, 512), F32),
                   jax.ShapeDtypeStruct((bsz, nc, HEADS, HD, HD), F32)),
        scratch_shapes=[pltpu.VMEM((bsz * HEADS, HD, HD), F32)],
        compiler_params=_params(("arbitrary",)))(proj, proj, proj, proj, w2, bg)


def _gla_rec_bwd(proj, w2, bg, states, do, dsm_dn):
    bsz, t_total, _ = proj.shape
    nc = t_total // CHUNK
    rev = lambda c: nc - 1 - c

    def body(q_ref, k_ref, v_ref, sm_ref, w2_ref, bg_ref, ss_ref, do_ref, dsd_ref,
             dq_ref, dk_ref, dv_ref, dsm_ref, dw2_ref, dbg_ref, ds_ref):
        @pl.when(pl.program_id(0) == 0)
        def _():
            dw2_ref[...] = jnp.zeros_like(dw2_ref)
            dbg_ref[...] = jnp.zeros_like(dbg_ref)
            ds_ref[...] = jnp.zeros_like(ds_ref)

        seqs = range(bsz)
        s_list = [[ss_ref[b, 0, h] for h in range(HEADS)] for b in seqs]
        _, vjp = jax.vjp(_gla_chunk, s_list, [q_ref[b] for b in seqs], [k_ref[b] for b in seqs],
                         [v_ref[b] for b in seqs], [sm_ref[b] for b in seqs], w2_ref[...], bg_ref[...])
        ds_in, dq, dk, dv, dsm, dw2, dbg = vjp(([do_ref[b] for b in seqs],
                                                [[ds_ref[b * HEADS + h] for h in range(HEADS)] for b in seqs]))
        for b in seqs:
            dq_ref[b], dk_ref[b], dv_ref[b] = dq[b].astype(MXU_DT), dk[b].astype(MXU_DT), dv[b].astype(MXU_DT)
            dsm_ref[b] = (dsm[b] + dsd_ref[b]).astype(MXU_DT)
            for h in range(HEADS):
                ds_ref[b * HEADS + h] = ds_in[b][h]
        dw2_ref[...] += dw2
        dbg_ref[...] += dbg

    col = lambda blk, width=512: _chunk_spec(bsz, width, rev, blk)
    w2_spec = pl.BlockSpec((128, 512), lambda c: (0, 0))
    bg_spec = pl.BlockSpec((1, 512), lambda c: (0, 0))
    out512 = jax.ShapeDtypeStruct((bsz, t_total, 512), MXU_DT)
    return pl.pallas_call(
        body, name="gla_rec_bwd", grid=(nc,),
        in_specs=[col(GQ_BLK), col(GK_BLK), col(GV_BLK), col(SM_BLK, 128), w2_spec, bg_spec,
                  _state_spec(bsz, rev), col(0), col(0, 128)],
        out_specs=(col(0), col(0), col(0), col(0, 128), w2_spec, bg_spec),
        out_shape=(out512, out512, out512, jax.ShapeDtypeStruct((bsz, t_total, 128), MXU_DT),
                   jax.ShapeDtypeStruct((128, 512), F32), jax.ShapeDtypeStruct((1, 512), F32)),
        scratch_shapes=[pltpu.VMEM((bsz * HEADS, HD, HD), F32)],
        compiler_params=_params(("arbitrary",)))(proj, proj, proj, proj, w2, bg, states, do, dsm_dn)


Z_BLK, GG_BLK = P_Z // 512, P_GG // 512


def _mix_out_fwd(o_dn, o_gla, proj, grow_dn, grow_gla):
    bsz, t_total, _ = o_dn.shape
    tt = _div_tile(t_total, 256)

    def body(od_ref, og_ref, z_ref, gg_ref, gd_ref, gl_ref, o_ref):
        o_ref[0, :, :512] = _gate_norm(od_ref[0], z_ref[0], gd_ref[...]).astype(MXU_DT)
        o_ref[0, :, 512:] = _gate_norm(og_ref[0], gg_ref[0], gl_ref[...]).astype(MXU_DT)

    def col(blk):
        return pl.BlockSpec((1, tt, 512), lambda b, t: (b, t, blk))

    return pl.pallas_call(
        body, name="mix_out_fwd", grid=(bsz, t_total // tt),
        in_specs=[col(0), col(0), col(Z_BLK), col(GG_BLK), _vec_spec(512), _vec_spec(512)],
        out_specs=_tok_spec(tt), out_shape=jax.ShapeDtypeStruct((bsz, t_total, D), MXU_DT),
        compiler_params=_params(("parallel", "parallel")))(o_dn, o_gla, proj, proj, grow_dn, grow_gla)


def _mix_out_bwd(do, o_dn, o_gla, proj, grow_dn, grow_gla):
    bsz, t_total, _ = o_dn.shape
    tt = _div_tile(t_total, 256)

    def body(do_ref, od_ref, og_ref, z_ref, gg_ref, gd_ref, gl_ref,
             dod_ref, dog_ref, dz_ref, dgg_ref, dgd_ref, dgl_ref):
        @pl.when((pl.program_id(0) == 0) & (pl.program_id(1) == 0))
        def _():
            dgd_ref[...] = jnp.zeros_like(dgd_ref)
            dgl_ref[...] = jnp.zeros_like(dgl_ref)

        def one(o_ref, gate_ref, g_ref, ct, do_out, dgate_out, dg_out):
            _, vjp = jax.vjp(_gate_norm, o_ref[0], gate_ref[0], g_ref[...])
            d_o, d_gate, d_row = vjp(ct)
            do_out[0] = d_o
            dgate_out[0] = d_gate.astype(MXU_DT)
            acc = d_row[:, :HD]
            for h in range(1, HEADS):
                acc = acc + d_row[:, h * HD:(h + 1) * HD]
            dg_out[...] += acc

        ct = do_ref[0]
        one(od_ref, z_ref, gd_ref, ct[:, :512], dod_ref, dz_ref, dgd_ref)
        one(og_ref, gg_ref, gl_ref, ct[:, 512:], dog_ref, dgg_ref, dgl_ref)

    def col(blk):
        return pl.BlockSpec((1, tt, 512), lambda b, t: (b, t, blk))

    f512 = jax.ShapeDtypeStruct((bsz, t_total, 512), F32)
    b512 = jax.ShapeDtypeStruct((bsz, t_total, 512), MXU_DT)
    g128 = jax.ShapeDtypeStruct((1, HD), F32)
    return pl.pallas_call(
        body, name="mix_out_bwd", grid=(bsz, t_total // tt),
        in_specs=[_tok_spec(tt), col(0), col(0), col(Z_BLK), col(GG_BLK), _vec_spec(512), _vec_spec(512)],
        out_specs=(col(0), col(0), col(0), col(0), _vec_spec(HD), _vec_spec(HD)),
        out_shape=(f512, f512, b512, b512, g128, g128),
        compiler_params=_params(("arbitrary", "arbitrary")))(do, o_dn, o_gla, proj, proj, grow_dn, grow_gla)


def _sum_slots(x, name):
    n, rows, cols = x.shape
    tr = _div_tile(rows, max(8, (1 << 19) // cols))

    def body(x_ref, o_ref):
        acc = x_ref[0].astype(F32)
        for i in range(1, n):
            acc = acc + x_ref[i].astype(F32)
        o_ref[...] = acc

    return pl.pallas_call(
        body, name=name, grid=(rows // tr,),
        in_specs=[pl.BlockSpec((n, tr, cols), lambda i: (0, i, 0))],
        out_specs=pl.BlockSpec((tr, cols), lambda i: (i, 0)),
        out_shape=jax.ShapeDtypeStruct((rows, cols), F32), compiler_params=_params(("parallel",)))(x)


def _adamw_math(w, g, m, v):
    nm = ADAM_B1 * m + (1.0 - ADAM_B1) * g
    nv = ADAM_B2 * v + (1.0 - ADAM_B2) * (g * g)
    m_hat = nm / (1.0 - ADAM_B1 ** ADAM_STEP)
    v_hat = nv / (1.0 - ADAM_B2 ** ADAM_STEP)
    return -ADAM_LR * (m_hat / (jnp.sqrt(v_hat) + ADAM_EPS) + ADAM_WD * w), nm, nv


def _adamw(w, g, m, v, name):
    _, rows, cols = w.shape
    tr = _div_tile(rows, max(8, (1 << 18) // cols))

    def body(w_ref, g_ref, m_ref, v_ref, d_ref, nm_ref, nv_ref):
        d_ref[...], nm_ref[...], nv_ref[...] = _adamw_math(w_ref[...], g_ref[...], m_ref[...], v_ref[...])

    spec = pl.BlockSpec((1, tr, cols), lambda i: (0, i, 0))
    shp = jax.ShapeDtypeStruct(w.shape, F32)
    return pl.pallas_call(body, name=name, grid=(rows // tr,), in_specs=[spec] * 4, out_specs=(spec,) * 3,
                          out_shape=(shp,) * 3, compiler_params=_params(("parallel",)))(w, g, m, v)


def _adamw_many(ws, gs, ms, vs, name):
    n = len(ws)

    def body(*refs):
        for i in range(n):
            d, nm, nv = _adamw_math(refs[i][...], refs[n + i][...], refs[2 * n + i][...], refs[3 * n + i][...])
            refs[4 * n + i][...] = d
            refs[5 * n + i][...] = nm
            refs[6 * n + i][...] = nv

    shapes = tuple(jax.ShapeDtypeStruct(w.shape, F32) for w in ws)
    outs = pl.pallas_call(body, name=name, out_shape=shapes * 3, compiler_params=_params())(*ws, *gs, *ms, *vs)
    return outs[:n], outs[n:2 * n], outs[2 * n:]


def _position():
    return lax.axis_index("x"), lax.axis_index("y"), lax.axis_index("c")


def _slot(px, py, pc):
    return 4 * px + 2 * py + pc


def _gather_small(x, name):
    rows, cols = x.shape

    def body(x_ref, o_ref, send_sems, recv_sems):
        mx, my, mc = _position()

        def peer(k):
            return (mx ^ ((k >> 2) & 1), my ^ ((k >> 1) & 1), mc ^ (k & 1))

        o_ref[_slot(mx, my, mc)] = x_ref[...]
        sends = []
        for k in range(1, N_DEV):
            cp = pltpu.make_async_remote_copy(src_ref=x_ref, dst_ref=o_ref.at[_slot(mx, my, mc)],
                                              send_sem=send_sems.at[k - 1], recv_sem=recv_sems.at[k - 1],
                                              device_id=peer(k), device_id_type=MESH)
            cp.start()
            sends.append(cp)
        for k in range(1, N_DEV):
            pltpu.make_async_remote_copy(src_ref=x_ref, dst_ref=o_ref.at[_slot(*peer(k))],
                                         send_sem=send_sems.at[k - 1], recv_sem=recv_sems.at[k - 1],
                                         device_id=peer(k), device_id_type=MESH).wait_recv()
        for cp in sends:
            cp.wait_send()

    return pl.pallas_call(
        body, name=name, out_shape=jax.ShapeDtypeStruct((N_DEV, rows, cols), x.dtype),
        in_specs=[pl.BlockSpec(memory_space=pltpu.VMEM)], out_specs=pl.BlockSpec(memory_space=pltpu.VMEM),
        scratch_shapes=[pltpu.SemaphoreType.DMA((N_DEV - 1,)), pltpu.SemaphoreType.DMA((N_DEV - 1,))],
        compiler_params=pltpu.CompilerParams(vmem_limit_bytes=VMEM_LIMIT_V7X))(x)


def _gather_big(shards):
    n = len(shards)

    def body(*refs):
        xs, outs = refs[:n], refs[n:2 * n]
        send_sems, recv_sems, local_sems = refs[2 * n:]
        mx, my, mc = _position()
        me, sibling = (mx, my, mc), (mx, my, 1 - mc)
        chips = [(1 - mx, my), (mx, 1 - my), (1 - mx, 1 - my)]

        def copy(a, k, block, to, src=None):
            dst = outs[a].at[_slot(*block)]
            return pltpu.make_async_remote_copy(src_ref=dst if src is None else src, dst_ref=dst,
                                                send_sem=send_sems.at[7 * a + k], recv_sem=recv_sems.at[7 * a + k],
                                                device_id=to, device_id_type=MESH)

        mine = [pltpu.make_async_copy(xs[a], outs[a].at[_slot(*me)], local_sems.at[a]) for a in range(n)]
        for cp in mine:
            cp.start()
        started = []
        for a in range(n):
            started.append(copy(a, 0, me, sibling, src=xs[a]))
            started += [copy(a, 1 + j, me, (*chip, mc), src=xs[a]) for j, chip in enumerate(chips)]
        for cp in started:
            cp.start()
        for j, chip in enumerate(chips):
            for a in range(n):
                copy(a, 1 + j, (*chip, mc), me).wait_recv()
                fwd = copy(a, 4 + j, (*chip, mc), sibling)
                fwd.start()
                started.append(fwd)
        for a in range(n):
            copy(a, 0, sibling, me).wait_recv()
            for j, chip in enumerate(chips):
                copy(a, 4 + j, (*chip, 1 - mc), me).wait_recv()
        for cp in started:
            cp.wait_send()
        for cp in mine:
            cp.wait()

    any_spec = pl.BlockSpec(memory_space=pl.ANY)
    return pl.pallas_call(
        body, name="gather_weights",
        out_shape=tuple(jax.ShapeDtypeStruct((N_DEV,) + s.shape, s.dtype) for s in shards),
        in_specs=[any_spec] * n, out_specs=(any_spec,) * n,
        scratch_shapes=[pltpu.SemaphoreType.DMA((7 * n,)), pltpu.SemaphoreType.DMA((7 * n,)),
                        pltpu.SemaphoreType.DMA((n,))])(*shards)


def _peer(pos, k):
    mx, my, mc = pos
    return (mx ^ ((k >> 2) & 1), my ^ ((k >> 1) & 1), mc ^ (k & 1))


def _exchange_copies(srcs, lands, send_sems, recv_sems, by_owner):
    pos = _position()
    me = _slot(*pos)
    out = []
    for a, (src, land) in enumerate(zip(srcs, lands)):
        for k in range(1, N_DEV):
            peer = _peer(pos, k)
            sems = dict(send_sem=send_sems.at[7 * a + k - 1], recv_sem=recv_sems.at[7 * a + k - 1],
                        device_id=peer, device_id_type=MESH)
            mine = src.at[_slot(*peer)] if by_owner else src
            send = pltpu.make_async_remote_copy(src_ref=mine, dst_ref=land.at[me], **sems)
            recv = pltpu.make_async_remote_copy(src_ref=mine, dst_ref=land.at[_slot(*peer)], **sems)
            out.append((send, recv))
    return out


_HBM_SPEC = pl.BlockSpec(memory_space=pltpu.HBM)
_SEM_SPEC = pl.BlockSpec(memory_space=pltpu.SEMAPHORE)
_DATAFLOW = pltpu.SideEffectType.DATAFLOW_SIDE_EFFECTING


def _exchange_start(name, srcs, slab_shapes, after, by_owner, carry=()):
    n, na, nc = len(srcs), len(after), len(carry)
    lands = [pltpu.with_memory_space_constraint(lax.empty((N_DEV,) + s, x.dtype), pltpu.HBM)
             for s, x in zip(slab_shapes, srcs)]
    thru = [pltpu.with_memory_space_constraint(x, pltpu.HBM) for x in [*srcs, *lands, *carry]]

    def body(*refs):
        src_refs, land_refs = refs[:n], refs[n:2 * n]
        send_sems, recv_sems = refs[len(thru) + na], refs[len(thru) + na + 1]
        token = refs[-1]
        for send, _ in _exchange_copies(src_refs, land_refs, send_sems, recv_sems, by_owner):
            send.start()
        token[...] = jnp.zeros_like(token)

    outs = pl.pallas_call(
        body, name=name,
        out_shape=(pltpu.SemaphoreType.DMA((7 * n,)), pltpu.SemaphoreType.DMA((7 * n,)),
                   *[pltpu.HBM(x.shape, x.dtype) for x in thru], jax.ShapeDtypeStruct((8, 128), F32)),
        in_specs=[_HBM_SPEC] * len(thru) + [pl.BlockSpec(memory_space=pl.ANY)] * na,
        out_specs=(_SEM_SPEC, _SEM_SPEC, *[_HBM_SPEC] * len(thru), pl.BlockSpec(memory_space=pltpu.VMEM)),
        input_output_aliases={i: 2 + i for i in range(len(thru))},
        compiler_params=pltpu.CompilerParams(has_side_effects=_DATAFLOW))(*thru, *after)
    return (outs[0], outs[1], list(outs[2:2 + n]), list(outs[2 + n:2 + 2 * n]), outs[-1],
            list(outs[2 + 2 * n:2 + 2 * n + nc]))


def _exchange_wait(name, send_sems, recv_sems, srcs, lands, after, by_owner):
    n = len(srcs)

    def body(*refs):
        src_refs, land_refs = refs[:n], refs[n:2 * n]
        s_sems, r_sems = refs[2 * n], refs[2 * n + 1]
        for send, recv in _exchange_copies(src_refs, land_refs, s_sems, r_sems, by_owner):
            send.wait_send()
            recv.wait_recv()

    outs = pl.pallas_call(
        body, name=name,
        out_shape=(*[pltpu.HBM(x.shape, x.dtype) for x in srcs], *[pltpu.HBM(l.shape, l.dtype) for l in lands]),
        in_specs=[_HBM_SPEC] * (2 * n) + [_SEM_SPEC, _SEM_SPEC, pl.BlockSpec(memory_space=pl.ANY)],
        out_specs=tuple([_HBM_SPEC] * (2 * n)),
        input_output_aliases={i: i for i in range(2 * n)},
        compiler_params=pltpu.CompilerParams(has_side_effects=_DATAFLOW))(*srcs, *lands, send_sems, recv_sems, after)
    return list(outs[:n]), list(outs[n:])


def _pad_heads(x, axis):
    shp = list(x.shape)
    x4 = x.reshape(shp[:axis] + [HEADS, GLA_KEY] + shp[axis + 1:])
    pad = [(0, 0)] * x4.ndim
    pad[axis + 1] = (0, HD - GLA_KEY)
    return jnp.pad(x4, pad).reshape(shp[:axis] + [HEADS * HD] + shp[axis + 1:])


def _unpad_heads(x, axis):
    shp = list(x.shape)
    x4 = x.reshape(shp[:axis] + [HEADS, HD] + shp[axis + 1:])
    x4 = lax.slice_in_dim(x4, 0, GLA_KEY, axis=axis + 1)
    return x4.reshape(shp[:axis] + [HEADS * GLA_KEY] + shp[axis + 1:])


O_Z_END, O_AB, O_GQ, O_GK, O_GV, O_R = 2048, 2048, 2056, 2312, 2568, 3592


def _pad_in_rows(wt):
    return jnp.concatenate([
        wt[:O_Z_END], _pad_heads(wt[O_GQ:O_GK], 0), _pad_heads(wt[O_GK:O_GV], 0), wt[O_GV:O_R],
        wt[O_AB:O_GQ], wt[O_R:], jnp.zeros((P_W - P_SM - 8 - GATE_RANK, wt.shape[1]), wt.dtype)], axis=0)


def _unpad_in_rows(gt):
    return jnp.concatenate([
        gt[:P_GQ], gt[P_SM:P_SM + 8], _unpad_heads(gt[P_GQ:P_GK], 0), _unpad_heads(gt[P_GK:P_GV], 0),
        gt[P_GV:P_SM], gt[P_SM + 8:P_SM + 8 + GATE_RANK]], axis=0)


def _lane_row(vals, width=128):
    return jnp.pad(vals.reshape(1, -1), ((0, 0), (0, width - vals.size)))


SMALL_NAMES = ["ln0_g", "ln0_b", "b_ada", "dn_conv", "dn_a_log", "dn_dt_bias", "dn_norm_g", "gla_w_gate2",
               "gla_b_gate", "gla_norm_g", "ln1_g", "ln1_b", "ffn_conv", "ffn_conv_b", "ln2_g", "ln2_b"]
WEIGHTS = ["ln0_g", "ln0_b", "w_ada", "b_ada", "w_in", "dn_conv", "dn_a_log", "dn_dt_bias", "dn_norm_g",
           "gla_w_gate2", "gla_b_gate", "gla_norm_g", "w_o", "ln1_g", "ln1_b", "ffn_w_up", "ffn_conv", "ffn_conv_b",
           "ffn_w_down", "ln2_g", "ln2_b"]


def kernel(x, c, ln0_g, ln0_b, w_ada, b_ada, w_in, dn_conv, dn_a_log, dn_dt_bias, dn_norm_g, gla_w_gate2, gla_b_gate, gla_norm_g, w_o, ln1_g, ln1_b, ffn_w_up, ffn_conv, ffn_conv_b, ffn_w_down, ln2_g, ln2_b, loss_target, m_ln0_g, m_ln0_b, m_w_ada, m_b_ada, m_w_in, m_dn_conv, m_dn_a_log, m_dn_dt_bias, m_dn_norm_g, m_gla_w_gate2, m_gla_b_gate, m_gla_norm_g, m_w_o, m_ln1_g, m_ln1_b, m_ffn_w_up, m_ffn_conv, m_ffn_conv_b, m_ffn_w_down, m_ln2_g, m_ln2_b, v_ln0_g, v_ln0_b, v_w_ada, v_b_ada, v_w_in, v_dn_conv, v_dn_a_log, v_dn_dt_bias, v_dn_norm_g, v_gla_w_gate2, v_gla_b_gate, v_gla_norm_g, v_w_o, v_ln1_g, v_ln1_b, v_ffn_w_up, v_ffn_conv, v_ffn_conv_b, v_ffn_w_down, v_ln2_g, v_ln2_b):
    args = dict(locals())
    w_given = {n: args[n] for n in WEIGHTS}
    m_given = {n: args["m_" + n] for n in WEIGHTS}
    v_given = {n: args["v_" + n] for n in WEIGHTS}
    bsz, t_total, _ = x.shape
    ntok = bsz * t_total
    mx, my, mc = _position()
    me = _slot(mx, my, mc)

    pack1 = jnp.concatenate([c.reshape(-1), dn_conv.reshape(-1), gla_w_gate2.reshape(-1), ffn_conv.reshape(-1)])
    n1 = pack1.size
    rows1 = -(-n1 // 1024) * 8
    pack1 = jnp.pad(pack1, (0, rows1 * 128 - n1)).reshape(rows1, 128)
    got1 = _gather_small(pack1, "gather_cond").reshape(N_DEV, -1)
    o1 = bsz * D
    o2 = o1 + dn_conv.size
    o3 = o2 + gla_w_gate2.size
    c_all = got1[:, :o1].reshape(N_DEV * bsz, D)
    dn_conv_f = got1[:, o1:o2].reshape(N_DEV, DN_CONV_K, -1).transpose(1, 0, 2).reshape(DN_CONV_K, QKV_W)
    gate2_f = got1[:, o2:o3].reshape(N_DEV, GATE_RANK, -1).transpose(1, 0, 2).reshape(GATE_RANK, HEADS * GLA_KEY)
    ffn_conv_f = got1[:, o3:n1].reshape(N_DEV, FFN_CONV_K, -1).transpose(1, 0, 2).reshape(FFN_CONV_K, 2 * D_FF)

    win_t = w_in[0].T.astype(MXU_DT)
    wup_t = ffn_w_up[0].T.astype(MXU_DT)
    (win_all,) = _gather_big([win_t])
    win_p = _pad_in_rows(win_all.reshape(IN_W, D))
    cw_p, cb_p = _ffn_pair(ffn_conv_f, 1), _ffn_pair(ffn_conv_b, 1)

    ncol = w_ada.shape[2]
    b_cols = lax.dynamic_slice_in_dim(b_ada, me * ncol, ncol, axis=1)
    mod_part = _ada_fwd(c_all, w_ada[0], b_cols)
    mod_all = _gather_small(mod_part.reshape(-1, 128), "gather_mod").reshape(N_DEV, N_DEV * bsz, ncol)
    mod = lax.dynamic_slice_in_dim(mod_all, me * bsz, bsz, axis=1).transpose(1, 0, 2).reshape(bsz, 6, 1, D)
    late = [w_o[0].astype(MXU_DT), wup_t, ffn_w_down[0].astype(MXU_DT)]
    ag_send, ag_recv, ag_src, ag_land, ag_token, _ = _exchange_start(
        "gather_start", late, [w.shape for w in late], [win_all, mod_all], by_owner=False)
    mod = mod + ag_token[0, 0]
    sh_a, sc_a, gt_a, sh_f, sc_f, gt_f = (mod[:, i] for i in range(6))

    g0, b0 = ln0_g.reshape(1, D), ln0_b.reshape(1, D)
    alog_row, dt_row = _lane_row(dn_a_log[0]), _lane_row(dn_dt_bias[0])
    grow_dn, grow_gla = jnp.tile(dn_norm_g, (1, HEADS)), jnp.tile(gla_norm_g, (1, HEADS))
    w2 = jnp.zeros((128, HEADS * HD), F32).at[SM_R:SM_R + GATE_RANK].set(_pad_heads(gate2_f, 1))
    bg = _pad_heads(gla_b_gate, 1)

    h_a = _ln0_mod(x, g0, b0, sc_a, sh_a)
    proj = _mm(h_a.reshape(ntok, D), win_p, "nt", F32, "mm_proj", tm=1024, tn=1408).reshape(bsz, t_total, P_W)
    q, k, v, gates = _dn_pre_fwd(proj, dn_conv_f, alog_row, dt_row)
    o_dn, s_dn, inv_dn = _dn_rec_fwd(q, k, v, gates)
    o_gla, s_gla = _gla_rec_fwd(proj, w2, bg)
    o_mix = _mix_out_fwd(o_dn, o_gla, proj, grow_dn, grow_gla)
    late, landed = _exchange_wait("gather_wait", ag_send, ag_recv, ag_src, ag_land, o_mix, by_owner=False)
    wo_all, wup_all, wdn_all = (lax.dynamic_update_slice(l, w[None], (me, 0, 0)) for l, w in zip(landed, late))
    wo_f = wo_all.reshape(D, D)
    wup_f = _ffn_pair(wup_all.reshape(2 * D_FF, D), 0)
    wdn_f = wdn_all.reshape(D_FF, D)
    y = _mm(o_mix.reshape(ntok, D), wo_f, "nn", F32, "mm_wo", tm=1024, tn=1024).reshape(bsz, t_total, D)
    r1, h_f = _res_ln_mod(x, y, gt_a, g0, b0, ln1_g, ln1_b, sc_f, sh_f)
    up, act = _ffn_up_act(h_f, wup_f, cw_p, cb_p)
    y2 = _mm(act.reshape(ntok, D_FF), wdn_f, "nn", F32, "mm_down", tm=1024, tn=1024).reshape(bsz, t_total, D)
    loss_rows, dr2, dy2, dgt_f, d_ln2_g, d_ln2_b = _final_fwd_bwd(r1, y2, gt_f, ln1_g, ln1_b, ln2_g, ln2_b, loss_target)
    loss_part = (0.5 / D) * jnp.sum(loss_rows)

    dy2_2 = dy2.reshape(ntok, D)
    g_wdn = _mm(act.reshape(ntok, D_FF), dy2_2, "tn", MXU_DT, "mm_gwdn", tm=1408, tn=1024)
    dup, d_cw_p, d_cb_p = _ffn_act_bwd(up, dy2, wdn_f, cw_p, cb_p)
    d_ffn_conv, d_ffn_conv_b = _ffn_unpair(d_cw_p, 1), _ffn_unpair(d_cb_p, 1)
    dup_2 = dup.reshape(ntok, 2 * D_FF)
    dh_f = _mm(dup_2, wup_f, "nn", F32, "mm_dhf", tn=1024).reshape(bsz, t_total, D)
    g_wup_t = _mm(dup_2, h_f.reshape(ntok, D), "tn", MXU_DT, "mm_gwup", tm=1408, tn=1024)
    ffn_parts = [_ffn_unpair(g_wup_t, 0).reshape(N_DEV, -1, D), g_wdn.reshape(N_DEV, -1, D)]
    rs_send, rs_recv, rs_src, rs_land, rs_token, _ = _exchange_start(
        "scatter_start", ffn_parts, [p.shape[1:] for p in ffn_parts], [dh_f], by_owner=True)
    dr1, dsc_f, dsh_f, d_ln1_g, d_ln1_b, dy, dgt_a = _ln_bwd_call(
        "ln1_bwd", dr2, dh_f, r1, ln1_g, ln1_b, sc_f + rs_token[0, 0], y=y, gt=gt_a)

    dy_2 = dy.reshape(ntok, D)
    do = _mm(dy_2, wo_f, "nt", F32, "mm_do", tm=1024, tn=1024).reshape(bsz, t_total, D)
    g_wo = _mm(o_mix.reshape(ntok, D), dy_2, "tn", MXU_DT, "mm_gwo", tm=512, tn=1024)
    do_dn, do_gla, dz, dgg, d_dn_norm, d_gla_norm = _mix_out_bwd(do, o_dn, o_gla, proj, grow_dn, grow_gla)
    dq, dk, dv, dgates = _dn_rec_bwd(q, k, v, gates, s_dn, inv_dn, do_dn)
    dqkv, dsm_dn, d_dn_conv, d_alog_row, d_dt_row = _dn_pre_bwd(proj, dq, dk, dv, dgates, dn_conv_f, alog_row, dt_row)
    dgq, dgk, dgv, dsm, d_w2, d_bg = _gla_rec_bwd(proj, w2, bg, s_gla, do_gla, dsm_dn)
    dproj = jnp.concatenate([dqkv, dz, dgq, dgk, dgv, dgg, dsm], axis=-1).reshape(ntok, P_W)
    g_win_p = _mm(dproj, h_a.reshape(ntok, D), "tn", MXU_DT, "mm_gwin", tm=1408, tn=1024)
    mix_parts = [_unpad_in_rows(g_win_p).reshape(N_DEV, -1, D), g_wo.reshape(N_DEV, -1, D)]
    rs2_send, rs2_recv, rs2_src, rs2_land, rs2_token, (win_p_late,) = _exchange_start(
        "scatter_mix_start", mix_parts, [p.shape[1:] for p in mix_parts], [], by_owner=True, carry=[win_p])
    dh_a = _mm(dproj, win_p_late, "nn", F32, "mm_dha", tn=1024).reshape(bsz, t_total, D)
    grad_x, dsc_a, dsh_a, d_ln0_g, d_ln0_b = _ln_bwd_call(
        "ln0_bwd", dr1, dh_a, x, g0, b0, sc_a + rs2_token[0, 0])

    def owner_sum(landed, parts, tag):
        full = [lax.dynamic_update_slice(l, lax.dynamic_slice_in_dim(p, me, 1, axis=0), (me, 0, 0))
                for l, p in zip(landed, parts)]
        return [_sum_slots(f, f"sum_{tag}_{i}") for i, f in enumerate(full)]

    ffn_parts, ffn_landed = _exchange_wait("scatter_wait", rs_send, rs_recv, rs_src, rs_land, grad_x, by_owner=True)
    g_wup_ts, g_wdn_s = owner_sum(ffn_landed, ffn_parts, "ffn")
    mix_parts, mix_landed = _exchange_wait("scatter_mix_wait", rs2_send, rs2_recv, rs2_src, rs2_land, grad_x,
                                           by_owner=True)
    g_win_t, g_wo_s = owner_sum(mix_landed, mix_parts, "mix")

    dmod = jnp.concatenate([dsh_a, dsc_a, dgt_a, dsh_f, dsc_f, dgt_f], axis=1).reshape(-1)
    small_parts = {
        "ln0_g": d_ln0_g, "ln0_b": d_ln0_b, "ln1_g": d_ln1_g, "ln1_b": d_ln1_b, "ln2_g": d_ln2_g, "ln2_b": d_ln2_b,
        "dn_a_log": d_alog_row[:, :HEADS], "dn_dt_bias": d_dt_row[:, :HEADS],
        "dn_norm_g": d_dn_norm, "gla_norm_g": d_gla_norm, "gla_b_gate": _unpad_heads(d_bg, 1),
        "ffn_conv_b": d_ffn_conv_b, "dn_conv": d_dn_conv,
        "gla_w_gate2": _unpad_heads(d_w2[SM_R:SM_R + GATE_RANK], 1), "ffn_conv": d_ffn_conv}
    order = sorted(small_parts)
    flat = jnp.concatenate([small_parts[n].reshape(-1) for n in order] + [loss_part.reshape(1), dmod])
    n3 = flat.size
    rows3 = -(-n3 // 1024) * 8
    pack3 = jnp.pad(flat, (0, rows3 * 128 - n3)).reshape(rows3, 128)
    got3 = _gather_small(pack3, "gather_small_grads")
    tot3 = _sum_slots(got3, "sum_small_grads").reshape(-1)
    grads = {}
    off = 0
    for n in order:
        size = small_parts[n].size
        grads[n] = tot3[off:off + size]
        off += size
    loss = tot3[off]
    off += 1
    dmod_all = got3.reshape(N_DEV, -1)[:, off:off + dmod.size].reshape(N_DEV * bsz, 6 * D)
    dmod_cols = lax.dynamic_slice_in_dim(dmod_all, me * ncol, ncol, axis=1)
    g_wada, g_bada = _ada_bwd(c_all, dmod_all, dmod_cols)
    grads["b_ada"] = g_bada

    def col_shard(full, rows):
        part = full.reshape(rows, -1)
        width = part.shape[1] // N_DEV
        return lax.dynamic_slice_in_dim(part, me * width, width, axis=1)

    grads["dn_conv"] = col_shard(grads["dn_conv"], DN_CONV_K)
    grads["gla_w_gate2"] = col_shard(grads["gla_w_gate2"], GATE_RANK)
    grads["ffn_conv"] = col_shard(grads["ffn_conv"], FFN_CONV_K)
    grads = {n: g.reshape(w_given[n].shape) for n, g in grads.items()}
    grads["w_ada"] = g_wada.reshape(w_ada.shape)
    grads["w_in"] = g_win_t.T.reshape(w_in.shape)
    grads["w_o"] = g_wo_s.reshape(w_o.shape)
    grads["ffn_w_up"] = g_wup_ts.T.reshape(ffn_w_up.shape)
    grads["ffn_w_down"] = g_wdn_s.reshape(ffn_w_down.shape)

    delta, new_m, new_v = {}, {}, {}
    for n in ["w_ada", "w_o", "ffn_w_down"]:
        delta[n], new_m[n], new_v[n] = _adamw(w_given[n], grads[n], m_given[n], v_given[n], "adamw_" + n)
    flip = lambda a: jnp.swapaxes(a, 1, 2)
    for n, g_t in (("w_in", g_win_t), ("ffn_w_up", g_wup_ts)):
        upd = _adamw(flip(w_given[n]), g_t[None], flip(m_given[n]), flip(v_given[n]), "adamw_" + n)
        delta[n], new_m[n], new_v[n] = (flip(a) for a in upd)
    d_s, m_s, v_s = _adamw_many(*[[src[n] for n in SMALL_NAMES] for src in (w_given, grads, m_given, v_given)],
                                "adamw_small")
    for i, n in enumerate(SMALL_NAMES):
        delta[n], new_m[n], new_v[n] = d_s[i], m_s[i], v_s[i]

    return (loss, grad_x, *[grads[n] for n in WEIGHTS], *[delta[n] for n in WEIGHTS],
            *[new_m[n] for n in WEIGHTS], *[new_v[n] for n in WEIGHTS])
```

```python
import functools

import jax
import jax.numpy as jnp
from jax import lax
from jax.experimental import pallas as pl
from jax.experimental.pallas import tpu as pltpu

F32 = jnp.float32
MXU_DT = jnp.bfloat16
MESH = pl.DeviceIdType.MESH
N_DEV = 8

D = 1024
HEADS = 4
HD = 128
CHUNK = 64
GLA_KEY = 64
GLA_TAU = 16.0
GATE_RANK = 16
D_FF = 2816
IN_W = 3608
ALPHA = 2.0 ** 0.25
EPS = 1e-6
DN_CONV_K = 4
FFN_CONV_K = 3
HALO = 8

P_QKV, P_Z, P_GQ, P_GK, P_GV, P_GG, P_SM, P_W = 0, 1536, 2048, 2560, 3072, 3584, 4096, 4224
SM_A, SM_B, SM_R = 0, 4, 8

ADAM_LR, ADAM_B1, ADAM_B2, ADAM_EPS, ADAM_WD, ADAM_STEP = 0.001, 0.9, 0.999, 1e-08, 0.01, 10

VMEM_LIMIT_V7X = 56 * 1024 * 1024


def _params(sem=None):
    return pltpu.CompilerParams(dimension_semantics=sem, vmem_limit_bytes=VMEM_LIMIT_V7X)


def _dg(a, b, dims, prec=None):
    return lax.dot_general(a, b, (dims, ((), ())), precision=prec, preferred_element_type=F32)


def _dot(a, b, prec=None):
    return _dg(a, b, ((1,), (0,)), prec)


def _dot_nt(a, b, prec=None):
    return _dg(a, b, ((1,), (1,)), prec)


def _dot_tn(a, b, prec=None):
    return _dg(a, b, ((0,), (0,)), prec)


def _iota(shape, dim):
    return lax.broadcasted_iota(jnp.int32, shape, dim)


def _sigmoid(x):
    return jax.nn.sigmoid(x)


def _silu(x):
    return x * _sigmoid(x)


def _softplus(x):
    return jnp.maximum(x, 0.0) + jnp.log(1.0 + jnp.exp(-jnp.abs(x)))


def _ln_stats(x):
    mu = jnp.mean(x, axis=-1, keepdims=True)
    xc = x - mu
    rstd = lax.rsqrt(jnp.mean(xc * xc, axis=-1, keepdims=True) + EPS)
    return xc * rstd, rstd


def _ln_bwd(dxhat, xhat, rstd):
    return rstd * (dxhat - jnp.mean(dxhat, axis=-1, keepdims=True)
                   - xhat * jnp.mean(dxhat * xhat, axis=-1, keepdims=True))


NN, NT, TN = ((1,), (0,)), ((1,), (1,)), ((0,), (0,))


def _split2(a):
    hi = a.astype(jnp.bfloat16)
    return hi, (a - hi.astype(F32)).astype(jnp.bfloat16)


def _d3(a, b, dims):
    ah, al = _split2(a)
    bh, bl = _split2(b)
    return _dg(ah, bh, dims) + (_dg(ah, bl, dims) + _dg(al, bh, dims))


@jax.custom_vjp
def _dot3(a, b):
    return _d3(a, b, NN)


_dot3.defvjp(lambda a, b: (_d3(a, b, NN), (a, b)),
             lambda res, g: (_d3(g, res[1], NT), _d3(res[0], g, TN)))


def _split3(b):
    b1 = b.astype(jnp.bfloat16)
    r1 = b - b1.astype(F32)
    b2 = r1.astype(jnp.bfloat16)
    return b1, b2, (r1 - b2.astype(F32)).astype(jnp.bfloat16)


def _sum3(fn, b):
    b1, b2, b3 = _split3(b)
    return fn(b1) + (fn(b2) + fn(b3))


@jax.custom_vjp
def _mask_dot(e, b):
    return _sum3(lambda t: _dg(e, t, NN), b)


_mask_dot.defvjp(lambda e, b: (_mask_dot(e, b), e),
                 lambda e, g: (jnp.zeros_like(e), _sum3(lambda t: _dg(e, t, TN), g)))


@jax.custom_vjp
def _mask_dot_nt(e, b):
    return _sum3(lambda t: _dg(e, t, NT), b)


_mask_dot_nt.defvjp(lambda e, b: (_mask_dot_nt(e, b), e),
                    lambda e, g: (jnp.zeros_like(e), _sum3(lambda t: _dg(t, e, TN), g)))


def _tri_inv_impl(ms):
    n = ms[0].shape[0]
    r, c = _iota((n, n), 0), _iota((n, n), 1)
    eye = (r == c).astype(F32)
    diag = (r >> 3) == (c >> 3)
    ds = [jnp.where(diag, m, 0.0) for m in ms]
    d2s = [_d3(d, d, NN) for d in ds]
    d4s = [_d3(d2, d2, NN) for d2 in d2s]
    invs = [_d3(eye - d, eye + d2, NN) for d, d2 in zip(ds, d2s)]
    invs = [_d3(inv, eye + d4, NN) for inv, d4 in zip(invs, d4s)]
    shift = 3
    while (1 << shift) < n:
        rb, cb = r >> shift, c >> shift
        sel = ((rb & 1) == 1) & (cb == rb - 1)
        tmp = [_d3(inv, jnp.where(sel, m, 0.0), NN) for inv, m in zip(invs, ms)]
        invs = [inv - _d3(t, inv, NN) for t, inv in zip(tmp, invs)]
        shift += 1
    return invs


@jax.custom_vjp
def _tri_inv(ms):
    return _tri_inv_impl(ms)


def _tri_inv_fwd(ms):
    invs = _tri_inv_impl(ms)
    return invs, invs


def _tri_inv_bwd(invs, das):
    tmp = [_d3(a, da, TN) for a, da in zip(invs, das)]
    return ([-_d3(t, a, NT) for t, a in zip(tmp, invs)],)


_tri_inv.defvjp(_tri_inv_fwd, _tri_inv_bwd)


@jax.custom_vjp
def _tri_inv_known(ms, invs):
    return invs


_tri_inv_known.defvjp(lambda ms, invs: (invs, invs),
                      lambda invs, das: (_tri_inv_bwd(invs, das)[0], [jnp.zeros_like(a) for a in invs]))


def _dn_chunk(s_list, q, k, v, gates, inv_known=None, with_inv=False):
    nb = len(q)
    c = q[0].shape[0]
    r64, c64 = _iota((c, c), 0), _iota((c, c), 1)
    causal = r64 >= c64
    strict = r64 > c64
    tri = causal.astype(jnp.bfloat16)
    eye = (_iota((HD, HD), 0) == _iota((HD, HD), 1)).astype(jnp.bfloat16)
    lane = _iota(gates[0].shape, 1)
    lane1 = _iota((1, HD), 1)
    g_all = [_mask_dot(tri, g) for g in gates]
    g_all_t = [_mask_dot_nt(eye, g) for g in g_all]
    row = _iota(g_all_t[0].shape, 0)
    last = [jnp.sum(g, axis=0, keepdims=True) for g in gates]
    prob = [(b, h) for b in range(nb) for h in range(HEADS)]
    sl = [slice(h * HD, (h + 1) * HD) for h in range(HEADS)]
    qh = [q[b][:, sl[h]] for b, h in prob]
    kh = [k[b][:, sl[h]] for b, h in prob]
    vh = [v[b][:, sl[h]] for b, h in prob]
    s = [s_list[b][h] for b, h in prob]
    beta = [jnp.sum(jnp.where(lane == SM_B + h, gates[b], 0.0), axis=-1, keepdims=True) for b, h in prob]
    g_c = [jnp.sum(jnp.where(lane == SM_A + h, g_all[b], 0.0), axis=-1, keepdims=True) for b, h in prob]
    g_r = [jnp.sum(jnp.where(row == SM_A + h, g_all_t[b], 0.0), axis=0, keepdims=True) for b, h in prob]
    g_last = [jnp.sum(jnp.where(lane1 == SM_A + h, last[b], 0.0), axis=-1, keepdims=True) for b, h in prob]
    decay = [jnp.where(causal, jnp.exp(jnp.where(causal, gc - gr, 0.0)), 0.0) for gc, gr in zip(g_c, g_r)]
    kb = [k_ * b_ for k_, b_ in zip(kh, beta)]
    m_low = [jnp.where(strict, _dot_nt(kb_, k_) * d_, 0.0) for kb_, k_, d_ in zip(kb, kh, decay)]
    attn = [_dot_nt(q_, k_) * d_ for q_, k_, d_ in zip(qh, kh, decay)]
    a_inv = _tri_inv(m_low) if inv_known is None else _tri_inv_known(m_low, inv_known)
    eg = [jnp.exp(gc) for gc in g_c]
    uw = [_dot3(a_, jnp.concatenate([v_ * b_, kb_ * e_], axis=1))
          for a_, v_, b_, kb_, e_ in zip(a_inv, vh, beta, kb, eg)]
    v_new = [uw_[:, :HD] - _dot(uw_[:, HD:], s_) for uw_, s_ in zip(uw, s)]
    qs = [_dot(q_ * e_, s_) for q_, e_, s_ in zip(qh, eg, s)]
    o = [qs_ + _dot(a_, vn_) for qs_, a_, vn_ in zip(qs, attn, v_new)]
    k_dec = [k_ * jnp.exp(gl - gc) for k_, gl, gc in zip(kh, g_last, g_c)]
    s_new = [s_ * jnp.exp(gl) + _dot_tn(kd_, vn_) for s_, gl, kd_, vn_ in zip(s, g_last, k_dec, v_new)]
    outs = [jnp.concatenate(o[b * HEADS:(b + 1) * HEADS], axis=-1) for b in range(nb)]
    states = [s_new[b * HEADS:(b + 1) * HEADS] for b in range(nb)]
    return (outs, states, a_inv) if with_inv else (outs, states)


def _gla_chunk(st_list, q, k, v, small, w2, bg):
    nb = len(q)
    c = q[0].shape[0]
    causal = _iota((c, c), 0) >= _iota((c, c), 1)
    tri = causal.astype(jnp.bfloat16)
    la_all = [-_softplus(-(_dot(sm, w2) + bg)) * (1.0 / GLA_TAU) for sm in small]
    b_all = [_mask_dot(tri, la) for la in la_all]
    prob = [(b, h) for b in range(nb) for h in range(HEADS)]
    sl = [slice(h * HD, (h + 1) * HD) for h in range(HEADS)]
    kh = [k[b][:, sl[h]] for b, h in prob]
    vh = [v[b][:, sl[h]] for b, h in prob]
    st = [st_list[b][h] for b, h in prob]
    bc = [b_all[b][:, sl[h]] for b, h in prob]
    b_last = [jnp.sum(la_all[b][:, sl[h]], axis=0, keepdims=True) for b, h in prob]
    q_dec = [q[b][:, sl[h]] * (GLA_KEY ** -0.5) * jnp.exp(bc_) for (b, h), bc_ in zip(prob, bc)]
    attn = [jnp.where(causal, _dot_nt(qd, k_ * jnp.exp(-bc_)), 0.0) for qd, k_, bc_ in zip(q_dec, kh, bc)]
    inter = [_dot_nt(qd, st_) for qd, st_ in zip(q_dec, st)]
    o = [i_ + _dot(a_, v_) for i_, a_, v_ in zip(inter, attn, vh)]
    k_dec = [k_ * jnp.exp(bl - bc_) for k_, bl, bc_ in zip(kh, b_last, bc)]
    s_new = [st_ * jnp.exp(bl) + _dot_tn(v_, kd) for st_, bl, v_, kd in zip(st, b_last, vh, k_dec)]
    outs = [jnp.concatenate(o[b * HEADS:(b + 1) * HEADS], axis=-1) for b in range(nb)]
    return outs, [s_new[b * HEADS:(b + 1) * HEADS] for b in range(nb)]


def _dn_qkv(y):
    act = _silu(y)
    parts = []
    for i in range(2 * HEADS):
        xh = act[:, i * HD:(i + 1) * HD]
        xh = xh * lax.rsqrt(jnp.sum(xh * xh, axis=-1, keepdims=True) + EPS)
        parts.append(xh * (HD ** -0.5) if i < HEADS else xh)
    qk = jnp.concatenate(parts, axis=-1)
    return qk[:, :HEADS * HD], qk[:, HEADS * HD:], act[:, 2 * HEADS * HD:]


def _dn_gates(small, alog_row, dt_row):
    lane = _iota(small.shape, 1)
    log_a = -jnp.exp(alog_row) * _softplus(small + dt_row)
    return jnp.where(lane < SM_B, log_a, jnp.where(lane < SM_R, _sigmoid(small), 0.0))


def _gate_norm(o, z, grow):
    parts = []
    for h in range(HEADS):
        oh = o[:, h * HD:(h + 1) * HD]
        parts.append(oh * lax.rsqrt(jnp.mean(oh * oh, axis=-1, keepdims=True) + EPS))
    return jnp.concatenate(parts, axis=-1) * grow * _silu(z)


def _conv_rows(xrows, w_ref, k_taps):
    n = xrows.shape[0]
    acc = xrows * w_ref[k_taps - 1:k_taps, :]
    for s in range(1, k_taps):
        acc = acc + pltpu.roll(xrows, s, 0) * w_ref[k_taps - 1 - s:k_taps - s, :]
    return acc


def _shift_up(x, s):
    return x if s == 0 else pltpu.roll(x, x.shape[0] - s, 0)


def _div_tile(n, cap, mult=8):
    best = None
    for t in range(mult, min(n, cap) + 1, mult):
        if n % t == 0:
            best = t
    return best if best is not None else n


def _halo_prev(tt):
    return lambda b, t: (b, jnp.maximum(t * (tt // HALO) - 1, 0))


def _halo_next(tt, t_total):
    return lambda b, t: (b, jnp.minimum((t + 1) * (tt // HALO), t_total // HALO - 1))


def _mm(a, b, mode, out_dtype, name, tm=512, tn=512, tk=None):
    if mode == "nn":
        (m, k), n = a.shape, b.shape[1]
    elif mode == "nt":
        (m, k), n = a.shape, b.shape[0]
    else:
        (k, m), n = a.shape, b.shape[1]
    tm, tn = min(tm, m), min(tn, n)
    tk = k if tk is None else min(tk, k)
    assert m % tm == 0 and n % tn == 0 and k % tk == 0, (name, a.shape, b.shape, tm, tn, tk)
    nk = k // tk
    if mode == "tn":
        a_spec = pl.BlockSpec((tk, tm), lambda i, j, kk: (kk, i))
    else:
        a_spec = pl.BlockSpec((tm, tk), lambda i, j, kk: (i, kk))
    if mode == "nt":
        b_spec = pl.BlockSpec((tn, tk), lambda i, j, kk: (j, kk))
    else:
        b_spec = pl.BlockSpec((tk, tn), lambda i, j, kk: (kk, j))
    dims = {"nn": ((1,), (0,)), "nt": ((1,), (1,)), "tn": ((0,), (0,))}[mode]

    def body(a_ref, b_ref, o_ref, *acc):
        p = _dg(a_ref[...], b_ref[...], dims)
        if nk == 1:
            o_ref[...] = p.astype(out_dtype)
        else:
            kk = pl.program_id(2)

            @pl.when(kk == 0)
            def _():
                acc[0][...] = p

            @pl.when(kk > 0)
            def _():
                acc[0][...] += p

            @pl.when(kk == nk - 1)
            def _():
                o_ref[...] = acc[0][...].astype(out_dtype)

    return pl.pallas_call(
        body, name=name, grid=(m // tm, n // tn, nk),
        in_specs=[a_spec, b_spec],
        out_specs=pl.BlockSpec((tm, tn), lambda i, j, kk: (i, j)),
        out_shape=jax.ShapeDtypeStruct((m, n), out_dtype),
        scratch_shapes=[pltpu.VMEM((tm, tn), F32)] if nk > 1 else [],
        compiler_params=_params(("parallel", "parallel", "arbitrary")),
    )(a, b)


def _ada_fwd(c_all, w_ada, b_cols):
    def body(c_ref, w_ref, b_ref, o_ref):
        cond = _silu(c_ref[...]).astype(MXU_DT)
        o_ref[...] = _dot(cond, w_ref[...].astype(MXU_DT)) + b_ref[...]

    return pl.pallas_call(body, name="ada_fwd", out_shape=jax.ShapeDtypeStruct((c_all.shape[0], w_ada.shape[1]), F32),
                          compiler_params=_params())(c_all, w_ada, b_cols)


def _ada_bwd(c_all, dmod_all, dmod_cols):
    def body(c_ref, da_ref, dc_ref, gw_ref, gb_ref):
        cond = _silu(c_ref[...]).astype(MXU_DT)
        gw_ref[...] = _dot_tn(cond, dc_ref[...].astype(MXU_DT))
        gb_ref[...] = jnp.sum(da_ref[...], axis=0, keepdims=True)

    return pl.pallas_call(
        body, name="ada_bwd",
        out_shape=(jax.ShapeDtypeStruct((c_all.shape[1], dmod_cols.shape[1]), F32),
                   jax.ShapeDtypeStruct((1, dmod_all.shape[1]), F32)),
        compiler_params=_params())(c_all, dmod_all, dmod_cols)


def _tok_spec(tt, width=D):
    return pl.BlockSpec((1, tt, width), lambda b, t: (b, t, 0))


def _vec_spec(width=D):
    return pl.BlockSpec((1, width), lambda b, t: (0, 0))


def _bvec_spec(width=D):
    return pl.BlockSpec((1, 1, width), lambda b, t: (b, 0, 0))


def _ln0_mod(x, g0, b0, sc, sh):
    bsz, t_total, _ = x.shape
    tt = _div_tile(t_total, 256)

    def body(x_ref, g_ref, b_ref, sc_ref, sh_ref, h_ref):
        xh, _ = _ln_stats(x_ref[0])
        x0 = xh * g_ref[...] + b_ref[...]
        h_ref[0] = (x0 * (1.0 + sc_ref[0]) + sh_ref[0]).astype(MXU_DT)

    return pl.pallas_call(
        body, name="ln0_mod", grid=(bsz, t_total // tt),
        in_specs=[_tok_spec(tt), _vec_spec(), _vec_spec(), _bvec_spec(), _bvec_spec()],
        out_specs=_tok_spec(tt), out_shape=jax.ShapeDtypeStruct(x.shape, MXU_DT),
        compiler_params=_params(("parallel", "parallel")))(x, g0, b0, sc, sh)


def _res_ln_mod(x, y, gt, g0, b0, g1, b1, sc, sh):
    bsz, t_total, _ = x.shape
    tt = _div_tile(t_total, 256)

    def body(x_ref, y_ref, gt_ref, g0_ref, b0_ref, g1_ref, b1_ref, sc_ref, sh_ref, r_ref, h_ref):
        xh, _ = _ln_stats(x_ref[0])
        r = ALPHA * (xh * g0_ref[...] + b0_ref[...]) + (1.0 + gt_ref[0]) * y_ref[0]
        r_ref[0] = r
        rh, _ = _ln_stats(r)
        x1 = rh * g1_ref[...] + b1_ref[...]
        h_ref[0] = (x1 * (1.0 + sc_ref[0]) + sh_ref[0]).astype(MXU_DT)

    return pl.pallas_call(
        body, name="res_ln_mod", grid=(bsz, t_total // tt),
        in_specs=[_tok_spec(tt), _tok_spec(tt), _bvec_spec(), _vec_spec(), _vec_spec(), _vec_spec(), _vec_spec(),
                  _bvec_spec(), _bvec_spec()],
        out_specs=(_tok_spec(tt), _tok_spec(tt)),
        out_shape=(jax.ShapeDtypeStruct(x.shape, F32), jax.ShapeDtypeStruct(x.shape, MXU_DT)),
        compiler_params=_params(("parallel", "parallel")))(x, y, gt, g0, b0, g1, b1, sc, sh)


def _final_fwd_bwd(r1, y2, gt, g1, b1, g2, b2, target):
    bsz, t_total, _ = r1.shape
    tt = _div_tile(t_total, 256)

    def body(r1_ref, y2_ref, gt_ref, g1_ref, b1_ref, g2_ref, b2_ref, tg_ref,
             loss_ref, dr2_ref, dy2_ref, dgt_ref, dg2_ref, db2_ref):
        b, t = pl.program_id(0), pl.program_id(1)

        @pl.when((b == 0) & (t == 0))
        def _():
            loss_ref[...] = jnp.zeros_like(loss_ref)
            dg2_ref[...] = jnp.zeros_like(dg2_ref)
            db2_ref[...] = jnp.zeros_like(db2_ref)

        @pl.when(t == 0)
        def _():
            dgt_ref[...] = jnp.zeros_like(dgt_ref)

        rh1, _ = _ln_stats(r1_ref[0])
        x1 = rh1 * g1_ref[...] + b1_ref[...]
        y2 = y2_ref[0]
        gate = 1.0 + gt_ref[0]
        xh2, rstd2 = _ln_stats(ALPHA * x1 + gate * y2)
        err = xh2 * g2_ref[...] + b2_ref[...] - tg_ref[0]
        loss_ref[...] += jnp.sum(err * err, axis=0, keepdims=True)
        dx2 = err * (1.0 / D)
        dg2_ref[...] += jnp.sum(dx2 * xh2, axis=0, keepdims=True)
        db2_ref[...] += jnp.sum(dx2, axis=0, keepdims=True)
        dr2 = _ln_bwd(dx2 * g2_ref[...], xh2, rstd2)
        dr2_ref[0] = dr2
        dy2_ref[0] = (gate * dr2).astype(MXU_DT)
        dgt_ref[0] += jnp.sum(dr2 * y2, axis=0, keepdims=True)

    vec_out = jax.ShapeDtypeStruct((1, D), F32)
    return pl.pallas_call(
        body, name="final_fwd_bwd", grid=(bsz, t_total // tt),
        in_specs=[_tok_spec(tt), _tok_spec(tt), _bvec_spec(), _vec_spec(), _vec_spec(), _vec_spec(), _vec_spec(),
                  _tok_spec(tt)],
        out_specs=(_vec_spec(), _tok_spec(tt), _tok_spec(tt), _bvec_spec(), _vec_spec(), _vec_spec()),
        out_shape=(vec_out, jax.ShapeDtypeStruct(r1.shape, F32), jax.ShapeDtypeStruct(r1.shape, MXU_DT),
                   jax.ShapeDtypeStruct((bsz, 1, D), F32), vec_out, vec_out),
        compiler_params=_params(("arbitrary", "arbitrary")))(r1, y2, gt, g1, b1, g2, b2, target)


def _ln_bwd_call(name, d_res, d_h, src, g, b, sc, y=None, gt=None):
    bsz, t_total, _ = src.shape
    tt = _div_tile(t_total, 256)
    has_y = y is not None

    def body(*refs):
        if has_y:
            (dres_ref, dh_ref, src_ref, g_ref, b_ref, sc_ref, y_ref, gt_ref,
             dsrc_ref, dsc_ref, dsh_ref, dg_ref, db_ref, dy_ref, dgt_ref) = refs
        else:
            (dres_ref, dh_ref, src_ref, g_ref, b_ref, sc_ref,
             dsrc_ref, dsc_ref, dsh_ref, dg_ref, db_ref) = refs
        bi, t = pl.program_id(0), pl.program_id(1)

        @pl.when((bi == 0) & (t == 0))
        def _():
            dg_ref[...] = jnp.zeros_like(dg_ref)
            db_ref[...] = jnp.zeros_like(db_ref)

        @pl.when(t == 0)
        def _():
            dsc_ref[...] = jnp.zeros_like(dsc_ref)
            dsh_ref[...] = jnp.zeros_like(dsh_ref)
            if has_y:
                dgt_ref[...] = jnp.zeros_like(dgt_ref)

        xh, rstd = _ln_stats(src_ref[0])
        xv = xh * g_ref[...] + b_ref[...]
        dh = dh_ref[0]
        dx = ALPHA * dres_ref[0] + dh * (1.0 + sc_ref[0])
        dsc_ref[0] += jnp.sum(dh * xv, axis=0, keepdims=True)
        dsh_ref[0] += jnp.sum(dh, axis=0, keepdims=True)
        dg_ref[...] += jnp.sum(dx * xh, axis=0, keepdims=True)
        db_ref[...] += jnp.sum(dx, axis=0, keepdims=True)
        dsrc = _ln_bwd(dx * g_ref[...], xh, rstd)
        dsrc_ref[0] = dsrc
        if has_y:
            dy_ref[0] = ((1.0 + gt_ref[0]) * dsrc).astype(MXU_DT)
            dgt_ref[0] += jnp.sum(dsrc * y_ref[0], axis=0, keepdims=True)

    vec_out = jax.ShapeDtypeStruct((1, D), F32)
    bvec_out = jax.ShapeDtypeStruct((bsz, 1, D), F32)
    in_specs = [_tok_spec(tt), _tok_spec(tt), _tok_spec(tt), _vec_spec(), _vec_spec(), _bvec_spec()]
    out_specs = [_tok_spec(tt), _bvec_spec(), _bvec_spec(), _vec_spec(), _vec_spec()]
    out_shape = [jax.ShapeDtypeStruct(src.shape, F32), bvec_out, bvec_out, vec_out, vec_out]
    args = [d_res, d_h, src, g, b, sc]
    if has_y:
        in_specs += [_tok_spec(tt), _bvec_spec()]
        out_specs += [_tok_spec(tt), _bvec_spec()]
        out_shape += [jax.ShapeDtypeStruct(src.shape, MXU_DT), bvec_out]
        args += [y, gt]
    return pl.pallas_call(body, name=name, grid=(bsz, t_total // tt), in_specs=in_specs, out_specs=tuple(out_specs),
                          out_shape=tuple(out_shape), compiler_params=_params(("arbitrary", "arbitrary")))(*args)


FFN_TC = 256
FFN_NJ = D_FF // FFN_TC
FFN_PW = 2 * FFN_TC


def _ffn_pair(a, axis):
    shp = list(a.shape)
    a4 = a.reshape(shp[:axis] + [2, FFN_NJ, FFN_TC] + shp[axis + 1:])
    return jnp.swapaxes(a4, axis, axis + 1).reshape(shp)


def _ffn_unpair(a, axis):
    shp = list(a.shape)
    a4 = a.reshape(shp[:axis] + [FFN_NJ, 2, FFN_TC] + shp[axis + 1:])
    return jnp.swapaxes(a4, axis, axis + 1).reshape(shp)


def _ffn_up_act(h, w_up, cw, cb):
    bsz, t_total, _ = h.shape
    tt = _div_tile(t_total, 256)
    nt = t_total // tt
    ntile = bsz * nt

    def body(h_ref, wu_ref, w_ref, b_ref, up_ref, o_ref, buf_ref, carry_ref):
        s = pl.program_id(1)

        @pl.when(s == 0)
        def _():
            buf_ref[...] = jnp.zeros_like(buf_ref)
            carry_ref[...] = jnp.zeros_like(carry_ref)

        up_prev = buf_ref[(s + 1) % 2]
        prev = jnp.where((s - 1) % nt == 0, 0.0, carry_ref[...])
        rows = jnp.concatenate([prev, up_prev], axis=0)
        u = _conv_rows(rows, w_ref, FFN_CONV_K)[HALO:] + b_ref[...]
        o_ref[0] = (_silu(u[:, :FFN_TC]) * u[:, FFN_TC:]).astype(MXU_DT)
        carry_ref[...] = up_prev[tt - HALO:, :]
        up_t = _dot_nt(h_ref[0], wu_ref[...])
        up_ref[0] = up_t
        buf_ref[s % 2] = up_t

    def this_tile(s):
        s = jnp.minimum(s, ntile - 1)
        return s // nt, s % nt

    def last_tile(s):
        s = jnp.maximum(s - 1, 0)
        return s // nt, s % nt

    return pl.pallas_call(
        body, name="ffn_up_act", grid=(FFN_NJ, ntile + 1),
        in_specs=[pl.BlockSpec((1, tt, D), lambda j, s: (*this_tile(s), 0)),
                  pl.BlockSpec((FFN_PW, D), lambda j, s: (j, 0)),
                  pl.BlockSpec((FFN_CONV_K, FFN_PW), lambda j, s: (0, j)),
                  pl.BlockSpec((1, FFN_PW), lambda j, s: (0, j))],
        out_specs=(pl.BlockSpec((1, tt, FFN_PW), lambda j, s: (*this_tile(s), j)),
                   pl.BlockSpec((1, tt, FFN_TC), lambda j, s: (*last_tile(s), j))),
        out_shape=(jax.ShapeDtypeStruct((bsz, t_total, 2 * D_FF), F32),
                   jax.ShapeDtypeStruct((bsz, t_total, D_FF), MXU_DT)),
        scratch_shapes=[pltpu.VMEM((2, tt, FFN_PW), F32), pltpu.VMEM((HALO, FFN_PW), F32)],
        compiler_params=_params(("parallel", "arbitrary")))(h, w_up, cw, cb)


HALO16 = 16


def _ffn_act_bwd(up, dy2, w_down, cw, cb):
    bsz, t_total, width = up.shape
    tt = _div_tile(t_total, 256)
    nt = t_total // tt
    hp, hn = _halo_prev(tt), _halo_next(tt, t_total)

    def body(x_ref, xp_ref, xn_ref, dy_ref, dyn_ref, wd_ref, w_ref, b_ref, dup_ref, dw_ref, db_ref):
        b, t = pl.program_id(1), pl.program_id(2)

        @pl.when((b == 0) & (t == 0))
        def _():
            dw_ref[...] = jnp.zeros_like(dw_ref)
            db_ref[...] = jnp.zeros_like(db_ref)

        prev = jnp.where(t == 0, 0.0, xp_ref[0])
        rows = jnp.concatenate([prev, x_ref[0], xn_ref[0]], axis=0)
        u = _conv_rows(rows, w_ref, FFN_CONV_K)[HALO:] + b_ref[...]
        g_pre, v_pre = u[:, :FFN_TC], u[:, FFN_TC:]
        valid = (_iota((tt + HALO, 1), 0) < tt) | (t < nt - 1)
        da = jnp.concatenate([_dot_nt(dy_ref[0], wd_ref[...]), _dot_nt(dyn_ref[0], wd_ref[...])[:HALO]], axis=0)
        da_ext = jnp.where(valid, da, 0.0)
        sg = _sigmoid(g_pre)
        gs = g_pre * sg
        du = jnp.concatenate([da_ext * v_pre * (sg + gs * (1.0 - sg)), da_ext * gs], axis=1)
        dup = du * w_ref[FFN_CONV_K - 1:FFN_CONV_K, :]
        for s in range(1, FFN_CONV_K):
            dup = dup + _shift_up(du, s) * w_ref[FFN_CONV_K - 1 - s:FFN_CONV_K - s, :]
        dup_ref[0] = dup[:tt].astype(MXU_DT)
        du_t = du[:tt]
        db_ref[...] += jnp.sum(du_t, axis=0, keepdims=True)
        for k in range(FFN_CONV_K):
            s = FFN_CONV_K - 1 - k
            xs = (rows if s == 0 else pltpu.roll(rows, s, 0))[HALO:HALO + tt]
            dw_ref[k:k + 1, :] += jnp.sum(du_t * xs, axis=0, keepdims=True)

    def halo(h, w):
        return pl.BlockSpec((1, HALO, w), lambda j, b, t: (*h(b, t), j))

    wspec = lambda rows_: pl.BlockSpec((rows_, FFN_PW), lambda j, b, t: (0, j))
    tile = pl.BlockSpec((1, tt, FFN_PW), lambda j, b, t: (b, t, j))
    dy_next = lambda j, b, t: (b, jnp.minimum((t + 1) * (tt // HALO16), t_total // HALO16 - 1), 0)
    return pl.pallas_call(
        body, name="ffn_act_bwd", grid=(FFN_NJ, bsz, nt),
        in_specs=[tile, halo(hp, FFN_PW), halo(hn, FFN_PW),
                  pl.BlockSpec((1, tt, D), lambda j, b, t: (b, t, 0)), pl.BlockSpec((1, HALO16, D), dy_next),
                  pl.BlockSpec((FFN_TC, D), lambda j, b, t: (j, 0)), wspec(FFN_CONV_K), wspec(1)],
        out_specs=(tile, wspec(FFN_CONV_K), wspec(1)),
        out_shape=(jax.ShapeDtypeStruct(up.shape, MXU_DT), jax.ShapeDtypeStruct((FFN_CONV_K, width), F32),
                   jax.ShapeDtypeStruct((1, width), F32)),
        compiler_params=_params(("arbitrary", "arbitrary", "arbitrary")))(up, up, up, dy2, dy2, w_down, cw, cb)


QKV_W = 3 * HEADS * HD
SM_BLK = P_SM // 128


def _dn_pre_fwd(proj, conv_w, alog_row, dt_row):
    bsz, t_total, _ = proj.shape
    tt = _div_tile(t_total, 256)
    hp = _halo_prev(tt)

    def body(x_ref, xp_ref, sm_ref, w_ref, al_ref, dt_ref, q_ref, k_ref, v_ref, g_ref):
        prev = jnp.where(pl.program_id(1) == 0, 0.0, xp_ref[0])
        y = _conv_rows(jnp.concatenate([prev, x_ref[0]], axis=0), w_ref, DN_CONV_K)[HALO:]
        q_ref[0], k_ref[0], v_ref[0] = _dn_qkv(y)
        g_ref[0] = _dn_gates(sm_ref[0], al_ref[...], dt_ref[...])

    out512 = jax.ShapeDtypeStruct((bsz, t_total, HEADS * HD), F32)
    return pl.pallas_call(
        body, name="dn_pre_fwd", grid=(bsz, t_total // tt),
        in_specs=[pl.BlockSpec((1, tt, QKV_W), lambda b, t: (b, t, 0)),
                  pl.BlockSpec((1, HALO, QKV_W), lambda b, t: (*hp(b, t), 0)),
                  pl.BlockSpec((1, tt, 128), lambda b, t: (b, t, SM_BLK)),
                  pl.BlockSpec((DN_CONV_K, QKV_W), lambda b, t: (0, 0)), _vec_spec(128), _vec_spec(128)],
        out_specs=(_tok_spec(tt, 512), _tok_spec(tt, 512), _tok_spec(tt, 512), _tok_spec(tt, 128)),
        out_shape=(out512, out512, out512, jax.ShapeDtypeStruct((bsz, t_total, 128), F32)),
        compiler_params=_params(("parallel", "parallel")))(proj, proj, proj, conv_w, alog_row, dt_row)


def _dn_pre_bwd(proj, dq, dk, dv, dgates, conv_w, alog_row, dt_row):
    bsz, t_total, _ = proj.shape
    tt = _div_tile(t_total, 128)
    nt = t_total // tt
    hp, hn = _halo_prev(tt), _halo_next(tt, t_total)

    def body(x_ref, xp_ref, xn_ref, sm_ref, dq_ref, dqn_ref, dk_ref, dkn_ref, dv_ref, dvn_ref, dg_ref,
             w_ref, al_ref, dt_ref, dx_ref, dsm_ref, dw_ref, dal_ref, ddt_ref):
        b, t = pl.program_id(0), pl.program_id(1)

        @pl.when((b == 0) & (t == 0))
        def _():
            dw_ref[...] = jnp.zeros_like(dw_ref)
            dal_ref[...] = jnp.zeros_like(dal_ref)
            ddt_ref[...] = jnp.zeros_like(ddt_ref)

        prev = jnp.where(t == 0, 0.0, xp_ref[0])
        rows = jnp.concatenate([prev, x_ref[0], xn_ref[0]], axis=0)
        y = _conv_rows(rows, w_ref, DN_CONV_K)[HALO:]
        valid = (_iota((tt + HALO, 1), 0) < tt) | (t < nt - 1)

        def ext(tile_ref, next_ref):
            return jnp.where(valid, jnp.concatenate([tile_ref[0], next_ref[0]], axis=0), 0.0)

        _, vjp_qkv = jax.vjp(_dn_qkv, y)
        (dy,) = vjp_qkv((ext(dq_ref, dqn_ref), ext(dk_ref, dkn_ref), ext(dv_ref, dvn_ref)))
        dy = jnp.where(valid, dy, 0.0)
        dx = dy * w_ref[DN_CONV_K - 1:DN_CONV_K, :]
        for s in range(1, DN_CONV_K):
            dx = dx + _shift_up(dy, s) * w_ref[DN_CONV_K - 1 - s:DN_CONV_K - s, :]
        dx_ref[0] = dx[:tt].astype(MXU_DT)
        dy_t = dy[:tt]
        for k in range(DN_CONV_K):
            s = DN_CONV_K - 1 - k
            xs = (rows if s == 0 else pltpu.roll(rows, s, 0))[HALO:HALO + tt]
            dw_ref[k:k + 1, :] += jnp.sum(dy_t * xs, axis=0, keepdims=True)
        _, vjp_g = jax.vjp(_dn_gates, sm_ref[0], al_ref[...], dt_ref[...])
        dsm, dal, ddt = vjp_g(dg_ref[0])
        dsm_ref[0] = dsm
        dal_ref[...] += dal
        ddt_ref[...] += ddt

    def tile(width, blk=0):
        return pl.BlockSpec((1, tt, width), lambda b, t: (b, t, blk))

    def halo(h, width):
        return pl.BlockSpec((1, HALO, width), lambda b, t: (*h(b, t), 0))

    return pl.pallas_call(
        body, name="dn_pre_bwd", grid=(bsz, nt),
        in_specs=[tile(QKV_W), halo(hp, QKV_W), halo(hn, QKV_W), tile(128, SM_BLK),
                  tile(512), halo(hn, 512), tile(512), halo(hn, 512), tile(512), halo(hn, 512), tile(128),
                  pl.BlockSpec((DN_CONV_K, QKV_W), lambda b, t: (0, 0)), _vec_spec(128), _vec_spec(128)],
        out_specs=(tile(QKV_W), tile(128), pl.BlockSpec((DN_CONV_K, QKV_W), lambda b, t: (0, 0)),
                   _vec_spec(128), _vec_spec(128)),
        out_shape=(jax.ShapeDtypeStruct((bsz, t_total, QKV_W), MXU_DT), jax.ShapeDtypeStruct((bsz, t_total, 128), F32),
                   jax.ShapeDtypeStruct((DN_CONV_K, QKV_W), F32), jax.ShapeDtypeStruct((1, 128), F32),
                   jax.ShapeDtypeStruct((1, 128), F32)),
        compiler_params=_params(("arbitrary", "arbitrary")))(
            proj, proj, proj, proj, dq, dq, dk, dk, dv, dv, dgates, conv_w, alog_row, dt_row)


def _state_spec(bsz, idx):
    return pl.BlockSpec((bsz, 1, HEADS, HD, HD), lambda c: (0, idx(c), 0, 0, 0))


def _inv_spec(bsz, idx):
    return pl.BlockSpec((bsz, 1, HEADS, CHUNK, CHUNK), lambda c: (0, idx(c), 0, 0, 0))


def _chunk_spec(bsz, width, idx, blk=0):
    return pl.BlockSpec((bsz, CHUNK, width), lambda c: (0, idx(c), blk))


def _dn_rec_fwd(q, k, v, gates):
    bsz, t_total, _ = q.shape
    nc = t_total // CHUNK
    fwd = lambda c: c

    def body(q_ref, k_ref, v_ref, g_ref, o_ref, ss_ref, inv_ref, s_ref):
        @pl.when(pl.program_id(0) == 0)
        def _():
            s_ref[...] = jnp.zeros_like(s_ref)

        seqs = range(bsz)
        s_list = [[s_ref[b * HEADS + h] for h in range(HEADS)] for b in seqs]
        for b in seqs:
            for h in range(HEADS):
                ss_ref[b, 0, h] = s_list[b][h]
        o, new_s, invs = _dn_chunk(s_list, [q_ref[b] for b in seqs], [k_ref[b] for b in seqs],
                                   [v_ref[b] for b in seqs], [g_ref[b] for b in seqs], with_inv=True)
        for b in seqs:
            o_ref[b] = o[b]
            for h in range(HEADS):
                s_ref[b * HEADS + h] = new_s[b][h]
                inv_ref[b, 0, h] = invs[b * HEADS + h]

    return pl.pallas_call(
        body, name="dn_rec_fwd", grid=(nc,),
        in_specs=[_chunk_spec(bsz, 512, fwd)] * 3 + [_chunk_spec(bsz, 128, fwd)],
        out_specs=(_chunk_spec(bsz, 512, fwd), _state_spec(bsz, fwd), _inv_spec(bsz, fwd)),
        out_shape=(jax.ShapeDtypeStruct(q.shape, F32), jax.ShapeDtypeStruct((bsz, nc, HEADS, HD, HD), F32),
                   jax.ShapeDtypeStruct((bsz, nc, HEADS, CHUNK, CHUNK), F32)),
        scratch_shapes=[pltpu.VMEM((bsz * HEADS, HD, HD), F32)],
        compiler_params=_params(("arbitrary",)))(q, k, v, gates)


def _dn_rec_bwd(q, k, v, gates, states, invs, do):
    bsz, t_total, _ = q.shape
    nc = t_total // CHUNK
    rev = lambda c: nc - 1 - c

    def body(q_ref, k_ref, v_ref, g_ref, ss_ref, inv_ref, do_ref, dq_ref, dk_ref, dv_ref, dg_ref, ds_ref):
        @pl.when(pl.program_id(0) == 0)
        def _():
            ds_ref[...] = jnp.zeros_like(ds_ref)

        seqs = range(bsz)
        s_list = [[ss_ref[b, 0, h] for h in range(HEADS)] for b in seqs]
        known = [inv_ref[b, 0, h] for b in seqs for h in range(HEADS)]
        _, vjp = jax.vjp(functools.partial(_dn_chunk, inv_known=known),
                         s_list, [q_ref[b] for b in seqs], [k_ref[b] for b in seqs],
                         [v_ref[b] for b in seqs], [g_ref[b] for b in seqs])
        ds_in, dq, dk, dv, dg = vjp(([do_ref[b] for b in seqs],
                                     [[ds_ref[b * HEADS + h] for h in range(HEADS)] for b in seqs]))
        for b in seqs:
            dq_ref[b], dk_ref[b], dv_ref[b], dg_ref[b] = dq[b], dk[b], dv[b], dg[b]
            for h in range(HEADS):
                ds_ref[b * HEADS + h] = ds_in[b][h]

    tok = lambda width: _chunk_spec(bsz, width, rev)
    out512 = jax.ShapeDtypeStruct(q.shape, F32)
    return pl.pallas_call(
        body, name="dn_rec_bwd", grid=(nc,),
        in_specs=[tok(512), tok(512), tok(512), tok(128), _state_spec(bsz, rev), _inv_spec(bsz, rev), tok(512)],
        out_specs=(tok(512), tok(512), tok(512), tok(128)),
        out_shape=(out512, out512, out512, jax.ShapeDtypeStruct(gates.shape, F32)),
        scratch_shapes=[pltpu.VMEM((bsz * HEADS, HD, HD), F32)],
        compiler_params=_params(("arbitrary",)))(q, k, v, gates, states, invs, do)


GQ_BLK, GK_BLK, GV_BLK = P_GQ // 512, P_GK // 512, P_GV // 512


def _gla_rec_fwd(proj, w2, bg):
    bsz, t_total, _ = proj.shape
    nc = t_total // CHUNK

    fwd = lambda c: c

    def body(q_ref, k_ref, v_ref, sm_ref, w2_ref, bg_ref, o_ref, ss_ref, s_ref):
        @pl.when(pl.program_id(0) == 0)
        def _():
            s_ref[...] = jnp.zeros_like(s_ref)

        seqs = range(bsz)
        s_list = [[s_ref[b * HEADS + h] for h in range(HEADS)] for b in seqs]
        for b in seqs:
            for h in range(HEADS):
                ss_ref[b, 0, h] = s_list[b][h]
        o, new_s = _gla_chunk(s_list, [q_ref[b] for b in seqs], [k_ref[b] for b in seqs], [v_ref[b] for b in seqs],
                              [sm_ref[b] for b in seqs], w2_ref[...], bg_ref[...])
        for b in seqs:
            o_ref[b] = o[b]
            for h in range(HEADS):
                s_ref[b * HEADS + h] = new_s[b][h]

    col = lambda blk, width=512: _chunk_spec(bsz, width, fwd, blk)
    return pl.pallas_call(
        body, name="gla_rec_fwd", grid=(nc,),
        in_specs=[col(GQ_BLK), col(GK_BLK), col(GV_BLK), col(SM_BLK, 128),
                  pl.BlockSpec((128, 512), lambda c: (0, 0)), pl.BlockSpec((1, 512), lambda c: (0, 0))],
        out_specs=(col(0), _state_spec(bsz, fwd)),
        out_shape=(jax.ShapeDtypeStruct((bsz, t_total, 512), F32),
                   jax.ShapeDtypeStruct((bsz, nc, HEADS, HD, HD), F32)),
        scratch_shapes=[pltpu.VMEM((bsz * HEADS, HD, HD), F32)],
        compiler_params=_params(("arbitrary",)))(proj, proj, proj, proj, w2, bg)


def _gla_rec_bwd(proj, w2, bg, states, do, dsm_dn):
    bsz, t_total, _ = proj.shape
    nc = t_total // CHUNK
    rev = lambda c: nc - 1 - c

    def body(q_ref, k_ref, v_ref, sm_ref, w2_ref, bg_ref, ss_ref, do_ref, dsd_ref,
             dq_ref, dk_ref, dv_ref, dsm_ref, dw2_ref, dbg_ref, ds_ref):
        @pl.when(pl.program_id(0) == 0)
        def _():
            dw2_ref[...] = jnp.zeros_like(dw2_ref)
            dbg_ref[...] = jnp.zeros_like(dbg_ref)
            ds_ref[...] = jnp.zeros_like(ds_ref)

        seqs = range(bsz)
        s_list = [[ss_ref[b, 0, h] for h in range(HEADS)] for b in seqs]
        _, vjp = jax.vjp(_gla_chunk, s_list, [q_ref[b] for b in seqs], [k_ref[b] for b in seqs],
                         [v_ref[b] for b in seqs], [sm_ref[b] for b in seqs], w2_ref[...], bg_ref[...])
        ds_in, dq, dk, dv, dsm, dw2, dbg = vjp(([do_ref[b] for b in seqs],
                                                [[ds_ref[b * HEADS + h] for h in range(HEADS)] for b in seqs]))
        for b in seqs:
            dq_ref[b], dk_ref[b], dv_ref[b] = dq[b].astype(MXU_DT), dk[b].astype(MXU_DT), dv[b].astype(MXU_DT)
            dsm_ref[b] = (dsm[b] + dsd_ref[b]).astype(MXU_DT)
            for h in range(HEADS):
                ds_ref[b * HEADS + h] = ds_in[b][h]
        dw2_ref[...] += dw2
        dbg_ref[...] += dbg

    col = lambda blk, width=512: _chunk_spec(bsz, width, rev, blk)
    w2_spec = pl.BlockSpec((128, 512), lambda c: (0, 0))
    bg_spec = pl.BlockSpec((1, 512), lambda c: (0, 0))
    out512 = jax.ShapeDtypeStruct((bsz, t_total, 512), MXU_DT)
    return pl.pallas_call(
        body, name="gla_rec_bwd", grid=(nc,),
        in_specs=[col(GQ_BLK), col(GK_BLK), col(GV_BLK), col(SM_BLK, 128), w2_spec, bg_spec,
                  _state_spec(bsz, rev), col(0), col(0, 128)],
        out_specs=(col(0), col(0), col(0), col(0, 128), w2_spec, bg_spec),
        out_shape=(out512, out512, out512, jax.ShapeDtypeStruct((bsz, t_total, 128), MXU_DT),
                   jax.ShapeDtypeStruct((128, 512), F32), jax.ShapeDtypeStruct((1, 512), F32)),
        scratch_shapes=[pltpu.VMEM((bsz * HEADS, HD, HD), F32)],
        compiler_params=_params(("arbitrary",)))(proj, proj, proj, proj, w2, bg, states, do, dsm_dn)


Z_BLK, GG_BLK = P_Z // 512, P_GG // 512


def _mix_out_fwd(o_dn, o_gla, proj, grow_dn, grow_gla):
    bsz, t_total, _ = o_dn.shape
    tt = _div_tile(t_total, 256)

    def body(od_ref, og_ref, z_ref, gg_ref, gd_ref, gl_ref, o_ref):
        o_ref[0, :, :512] = _gate_norm(od_ref[0], z_ref[0], gd_ref[...]).astype(MXU_DT)
        o_ref[0, :, 512:] = _gate_norm(og_ref[0], gg_ref[0], gl_ref[...]).astype(MXU_DT)

    def col(blk):
        return pl.BlockSpec((1, tt, 512), lambda b, t: (b, t, blk))

    return pl.pallas_call(
        body, name="mix_out_fwd", grid=(bsz, t_total // tt),
        in_specs=[col(0), col(0), col(Z_BLK), col(GG_BLK), _vec_spec(512), _vec_spec(512)],
        out_specs=_tok_spec(tt), out_shape=jax.ShapeDtypeStruct((bsz, t_total, D), MXU_DT),
        compiler_params=_params(("parallel", "parallel")))(o_dn, o_gla, proj, proj, grow_dn, grow_gla)


def _mix_out_bwd(do, o_dn, o_gla, proj, grow_dn, grow_gla):
    bsz, t_total, _ = o_dn.shape
    tt = _div_tile(t_total, 256)

    def body(do_ref, od_ref, og_ref, z_ref, gg_ref, gd_ref, gl_ref,
             dod_ref, dog_ref, dz_ref, dgg_ref, dgd_ref, dgl_ref):
        @pl.when((pl.program_id(0) == 0) & (pl.program_id(1) == 0))
        def _():
            dgd_ref[...] = jnp.zeros_like(dgd_ref)
            dgl_ref[...] = jnp.zeros_like(dgl_ref)

        def one(o_ref, gate_ref, g_ref, ct, do_out, dgate_out, dg_out):
            _, vjp = jax.vjp(_gate_norm, o_ref[0], gate_ref[0], g_ref[...])
            d_o, d_gate, d_row = vjp(ct)
            do_out[0] = d_o
            dgate_out[0] = d_gate.astype(MXU_DT)
            acc = d_row[:, :HD]
            for h in range(1, HEADS):
                acc = acc + d_row[:, h * HD:(h + 1) * HD]
            dg_out[...] += acc

        ct = do_ref[0]
        one(od_ref, z_ref, gd_ref, ct[:, :512], dod_ref, dz_ref, dgd_ref)
        one(og_ref, gg_ref, gl_ref, ct[:, 512:], dog_ref, dgg_ref, dgl_ref)

    def col(blk):
        return pl.BlockSpec((1, tt, 512), lambda b, t: (b, t, blk))

    f512 = jax.ShapeDtypeStruct((bsz, t_total, 512), F32)
    b512 = jax.ShapeDtypeStruct((bsz, t_total, 512), MXU_DT)
    g128 = jax.ShapeDtypeStruct((1, HD), F32)
    return pl.pallas_call(
        body, name="mix_out_bwd", grid=(bsz, t_total // tt),
        in_specs=[_tok_spec(tt), col(0), col(0), col(Z_BLK), col(GG_BLK), _vec_spec(512), _vec_spec(512)],
        out_specs=(col(0), col(0), col(0), col(0), _vec_spec(HD), _vec_spec(HD)),
        out_shape=(f512, f512, b512, b512, g128, g128),
        compiler_params=_params(("arbitrary", "arbitrary")))(do, o_dn, o_gla, proj, proj, grow_dn, grow_gla)


def _sum_slots(x, name):
    n, rows, cols = x.shape
    tr = _div_tile(rows, max(8, (1 << 19) // cols))

    def body(x_ref, o_ref):
        acc = x_ref[0].astype(F32)
        for i in range(1, n):
            acc = acc + x_ref[i].astype(F32)
        o_ref[...] = acc

    return pl.pallas_call(
        body, name=name, grid=(rows // tr,),
        in_specs=[pl.BlockSpec((n, tr, cols), lambda i: (0, i, 0))],
        out_specs=pl.BlockSpec((tr, cols), lambda i: (i, 0)),
        out_shape=jax.ShapeDtypeStruct((rows, cols), F32), compiler_params=_params(("parallel",)))(x)


def _adamw_math(w, g, m, v):
    nm = ADAM_B1 * m + (1.0 - ADAM_B1) * g
    nv = ADAM_B2 * v + (1.0 - ADAM_B2) * (g * g)
    m_hat = nm / (1.0 - ADAM_B1 ** ADAM_STEP)
    v_hat = nv / (1.0 - ADAM_B2 ** ADAM_STEP)
    return -ADAM_LR * (m_hat / (jnp.sqrt(v_hat) + ADAM_EPS) + ADAM_WD * w), nm, nv


def _adamw(w, g, m, v, name):
    _, rows, cols = w.shape
    tr = _div_tile(rows, max(8, (1 << 18) // cols))

    def body(w_ref, g_ref, m_ref, v_ref, d_ref, nm_ref, nv_ref):
        d_ref[...], nm_ref[...], nv_ref[...] = _adamw_math(w_ref[...], g_ref[...], m_ref[...], v_ref[...])

    spec = pl.BlockSpec((1, tr, cols), lambda i: (0, i, 0))
    shp = jax.ShapeDtypeStruct(w.shape, F32)
    return pl.pallas_call(body, name=name, grid=(rows // tr,), in_specs=[spec] * 4, out_specs=(spec,) * 3,
                          out_shape=(shp,) * 3, compiler_params=_params(("parallel",)))(w, g, m, v)


def _adamw_many(ws, gs, ms, vs, name):
    n = len(ws)

    def body(*refs):
        for i in range(n):
            d, nm, nv = _adamw_math(refs[i][...], refs[n + i][...], refs[2 * n + i][...], refs[3 * n + i][...])
            refs[4 * n + i][...] = d
            refs[5 * n + i][...] = nm
            refs[6 * n + i][...] = nv

    shapes = tuple(jax.ShapeDtypeStruct(w.shape, F32) for w in ws)
    outs = pl.pallas_call(body, name=name, out_shape=shapes * 3, compiler_params=_params())(*ws, *gs, *ms, *vs)
    return outs[:n], outs[n:2 * n], outs[2 * n:]


def _position():
    return lax.axis_index("x"), lax.axis_index("y"), lax.axis_index("c")


def _slot(px, py, pc):
    return 4 * px + 2 * py + pc


def _gather_small(x, name):
    rows, cols = x.shape

    def body(x_ref, o_ref, send_sems, recv_sems):
        mx, my, mc = _position()

        def peer(k):
            return (mx ^ ((k >> 2) & 1), my ^ ((k >> 1) & 1), mc ^ (k & 1))

        o_ref[_slot(mx, my, mc)] = x_ref[...]
        sends = []
        for k in range(1, N_DEV):
            cp = pltpu.make_async_remote_copy(src_ref=x_ref, dst_ref=o_ref.at[_slot(mx, my, mc)],
                                              send_sem=send_sems.at[k - 1], recv_sem=recv_sems.at[k - 1],
                                              device_id=peer(k), device_id_type=MESH)
            cp.start()
            sends.append(cp)
        for k in range(1, N_DEV):
            pltpu.make_async_remote_copy(src_ref=x_ref, dst_ref=o_ref.at[_slot(*peer(k))],
                                         send_sem=send_sems.at[k - 1], recv_sem=recv_sems.at[k - 1],
                                         device_id=peer(k), device_id_type=MESH).wait_recv()
        for cp in sends:
            cp.wait_send()

    return pl.pallas_call(
        body, name=name, out_shape=jax.ShapeDtypeStruct((N_DEV, rows, cols), x.dtype),
        in_specs=[pl.BlockSpec(memory_space=pltpu.VMEM)], out_specs=pl.BlockSpec(memory_space=pltpu.VMEM),
        scratch_shapes=[pltpu.SemaphoreType.DMA((N_DEV - 1,)), pltpu.SemaphoreType.DMA((N_DEV - 1,))],
        compiler_params=pltpu.CompilerParams(vmem_limit_bytes=VMEM_LIMIT_V7X))(x)


def _gather_big(shards):
    n = len(shards)

    def body(*refs):
        xs, outs = refs[:n], refs[n:2 * n]
        send_sems, recv_sems, local_sems = refs[2 * n:]
        mx, my, mc = _position()
        me, sibling = (mx, my, mc), (mx, my, 1 - mc)
        chips = [(1 - mx, my), (mx, 1 - my), (1 - mx, 1 - my)]

        def copy(a, k, block, to, src=None):
            dst = outs[a].at[_slot(*block)]
            return pltpu.make_async_remote_copy(src_ref=dst if src is None else src, dst_ref=dst,
                                                send_sem=send_sems.at[7 * a + k], recv_sem=recv_sems.at[7 * a + k],
                                                device_id=to, device_id_type=MESH)

        mine = [pltpu.make_async_copy(xs[a], outs[a].at[_slot(*me)], local_sems.at[a]) for a in range(n)]
        for cp in mine:
            cp.start()
        started = []
        for a in range(n):
            started.append(copy(a, 0, me, sibling, src=xs[a]))
            started += [copy(a, 1 + j, me, (*chip, mc), src=xs[a]) for j, chip in enumerate(chips)]
        for cp in started:
            cp.start()
        for j, chip in enumerate(chips):
            for a in range(n):
                copy(a, 1 + j, (*chip, mc), me).wait_recv()
                fwd = copy(a, 4 + j, (*chip, mc), sibling)
                fwd.start()
                started.append(fwd)
        for a in range(n):
            copy(a, 0, sibling, me).wait_recv()
            for j, chip in enumerate(chips):
                copy(a, 4 + j, (*chip, 1 - mc), me).wait_recv()
        for cp in started:
            cp.wait_send()
        for cp in mine:
            cp.wait()

    any_spec = pl.BlockSpec(memory_space=pl.ANY)
    return pl.pallas_call(
        body, name="gather_weights",
        out_shape=tuple(jax.ShapeDtypeStruct((N_DEV,) + s.shape, s.dtype) for s in shards),
        in_specs=[any_spec] * n, out_specs=(any_spec,) * n,
        scratch_shapes=[pltpu.SemaphoreType.DMA((7 * n,)), pltpu.SemaphoreType.DMA((7 * n,)),
                        pltpu.SemaphoreType.DMA((n,))])(*shards)


def _peer(pos, k):
    mx, my, mc = pos
    return (mx ^ ((k >> 2) & 1), my ^ ((k >> 1) & 1), mc ^ (k & 1))


def _exchange_copies(srcs, lands, send_sems, recv_sems, by_owner):
    pos = _position()
    me = _slot(*pos)
    out = []
    for a, (src, land) in enumerate(zip(srcs, lands)):
        for k in range(1, N_DEV):
            peer = _peer(pos, k)
            sems = dict(send_sem=send_sems.at[7 * a + k - 1], recv_sem=recv_sems.at[7 * a + k - 1],
                        device_id=peer, device_id_type=MESH)
            mine = src.at[_slot(*peer)] if by_owner else src
            send = pltpu.make_async_remote_copy(src_ref=mine, dst_ref=land.at[me], **sems)
            recv = pltpu.make_async_remote_copy(src_ref=mine, dst_ref=land.at[_slot(*peer)], **sems)
            out.append((send, recv))
    return out


_HBM_SPEC = pl.BlockSpec(memory_space=pltpu.HBM)
_SEM_SPEC = pl.BlockSpec(memory_space=pltpu.SEMAPHORE)
_DATAFLOW = pltpu.SideEffectType.DATAFLOW_SIDE_EFFECTING


def _exchange_start(name, srcs, slab_shapes, after, by_owner, carry=()):
    n, na, nc = len(srcs), len(after), len(carry)
    lands = [pltpu.with_memory_space_constraint(lax.empty((N_DEV,) + s, x.dtype), pltpu.HBM)
             for s, x in zip(slab_shapes, srcs)]
    thru = [pltpu.with_memory_space_constraint(x, pltpu.HBM) for x in [*srcs, *lands, *carry]]

    def body(*refs):
        src_refs, land_refs = refs[:n], refs[n:2 * n]
        send_sems, recv_sems = refs[len(thru) + na], refs[len(thru) + na + 1]
        token = refs[-1]
        for send, _ in _exchange_copies(src_refs, land_refs, send_sems, recv_sems, by_owner):
            send.start()
        token[...] = jnp.zeros_like(token)

    outs = pl.pallas_call(
        body, name=name,
        out_shape=(pltpu.SemaphoreType.DMA((7 * n,)), pltpu.SemaphoreType.DMA((7 * n,)),
                   *[pltpu.HBM(x.shape, x.dtype) for x in thru], jax.ShapeDtypeStruct((8, 128), F32)),
        in_specs=[_HBM_SPEC] * len(thru) + [pl.BlockSpec(memory_space=pl.ANY)] * na,
        out_specs=(_SEM_SPEC, _SEM_SPEC, *[_HBM_SPEC] * len(thru), pl.BlockSpec(memory_space=pltpu.VMEM)),
        input_output_aliases={i: 2 + i for i in range(len(thru))},
        compiler_params=pltpu.CompilerParams(has_side_effects=_DATAFLOW))(*thru, *after)
    return (outs[0], outs[1], list(outs[2:2 + n]), list(outs[2 + n:2 + 2 * n]), outs[-1],
            list(outs[2 + 2 * n:2 + 2 * n + nc]))


def _exchange_wait(name, send_sems, recv_sems, srcs, lands, after, by_owner):
    n = len(srcs)

    def body(*refs):
        src_refs, land_refs = refs[:n], refs[n:2 * n]
        s_sems, r_sems = refs[2 * n], refs[2 * n + 1]
        for send, recv in _exchange_copies(src_refs, land_refs, s_sems, r_sems, by_owner):
            send.wait_send()
            recv.wait_recv()

    outs = pl.pallas_call(
        body, name=name,
        out_shape=(*[pltpu.HBM(x.shape, x.dtype) for x in srcs], *[pltpu.HBM(l.shape, l.dtype) for l in lands]),
        in_specs=[_HBM_SPEC] * (2 * n) + [_SEM_SPEC, _SEM_SPEC, pl.BlockSpec(memory_space=pl.ANY)],
        out_specs=tuple([_HBM_SPEC] * (2 * n)),
        input_output_aliases={i: i for i in range(2 * n)},
        compiler_params=pltpu.CompilerParams(has_side_effects=_DATAFLOW))(*srcs, *lands, send_sems, recv_sems, after)
    return list(outs[:n]), list(outs[n:])


def _pad_heads(x, axis):
    shp = list(x.shape)
    x4 = x.reshape(shp[:axis] + [HEADS, GLA_KEY] + shp[axis + 1:])
    pad = [(0, 0)] * x4.ndim
    pad[axis + 1] = (0, HD - GLA_KEY)
    return jnp.pad(x4, pad).reshape(shp[:axis] + [HEADS * HD] + shp[axis + 1:])


def _unpad_heads(x, axis):
    shp = list(x.shape)
    x4 = x.reshape(shp[:axis] + [HEADS, HD] + shp[axis + 1:])
    x4 = lax.slice_in_dim(x4, 0, GLA_KEY, axis=axis + 1)
    return x4.reshape(shp[:axis] + [HEADS * GLA_KEY] + shp[axis + 1:])


O_Z_END, O_AB, O_GQ, O_GK, O_GV, O_R = 2048, 2048, 2056, 2312, 2568, 3592


def _pad_in_rows(wt):
    return jnp.concatenate([
        wt[:O_Z_END], _pad_heads(wt[O_GQ:O_GK], 0), _pad_heads(wt[O_GK:O_GV], 0), wt[O_GV:O_R],
        wt[O_AB:O_GQ], wt[O_R:], jnp.zeros((P_W - P_SM - 8 - GATE_RANK, wt.shape[1]), wt.dtype)], axis=0)


def _unpad_in_rows(gt):
    return jnp.concatenate([
        gt[:P_GQ], gt[P_SM:P_SM + 8], _unpad_heads(gt[P_GQ:P_GK], 0), _unpad_heads(gt[P_GK:P_GV], 0),
        gt[P_GV:P_SM], gt[P_SM + 8:P_SM + 8 + GATE_RANK]], axis=0)


def _lane_row(vals, width=128):
    return jnp.pad(vals.reshape(1, -1), ((0, 0), (0, width - vals.size)))


SMALL_NAMES = ["ln0_g", "ln0_b", "b_ada", "dn_conv", "dn_a_log", "dn_dt_bias", "dn_norm_g", "gla_w_gate2",
               "gla_b_gate", "gla_norm_g", "ln1_g", "ln1_b", "ffn_conv", "ffn_conv_b", "ln2_g", "ln2_b"]
WEIGHTS = ["ln0_g", "ln0_b", "w_ada", "b_ada", "w_in", "dn_conv", "dn_a_log", "dn_dt_bias", "dn_norm_g",
           "gla_w_gate2", "gla_b_gate", "gla_norm_g", "w_o", "ln1_g", "ln1_b", "ffn_w_up", "ffn_conv", "ffn_conv_b",
           "ffn_w_down", "ln2_g", "ln2_b"]


def kernel(x, c, ln0_g, ln0_b, w_ada, b_ada, w_in, dn_conv, dn_a_log, dn_dt_bias, dn_norm_g, gla_w_gate2, gla_b_gate, gla_norm_g, w_o, ln1_g, ln1_b, ffn_w_up, ffn_conv, ffn_conv_b, ffn_w_down, ln2_g, ln2_b, loss_target, m_ln0_g, m_ln0_b, m_w_ada, m_b_ada, m_w_in, m_dn_conv, m_dn_a_log, m_dn_dt_bias, m_dn_norm_g, m_gla_w_gate2, m_gla_b_gate, m_gla_norm_g, m_w_o, m_ln1_g, m_ln1_b, m_ffn_w_up, m_ffn_conv, m_ffn_conv_b, m_ffn_w_down, m_ln2_g, m_ln2_b, v_ln0_g, v_ln0_b, v_w_ada, v_b_ada, v_w_in, v_dn_conv, v_dn_a_log, v_dn_dt_bias, v_dn_norm_g, v_gla_w_gate2, v_gla_b_gate, v_gla_norm_g, v_w_o, v_ln1_g, v_ln1_b, v_ffn_w_up, v_ffn_conv, v_ffn_conv_b, v_ffn_w_down, v_ln2_g, v_ln2_b):
    args = dict(locals())
    w_given = {n: args[n] for n in WEIGHTS}
    m_given = {n: args["m_" + n] for n in WEIGHTS}
    v_given = {n: args["v_" + n] for n in WEIGHTS}
    bsz, t_total, _ = x.shape
    ntok = bsz * t_total
    mx, my, mc = _position()
    me = _slot(mx, my, mc)

    pack1 = jnp.concatenate([c.reshape(-1), dn_conv.reshape(-1), gla_w_gate2.reshape(-1), ffn_conv.reshape(-1)])
    n1 = pack1.size
    rows1 = -(-n1 // 1024) * 8
    pack1 = jnp.pad(pack1, (0, rows1 * 128 - n1)).reshape(rows1, 128)
    got1 = _gather_small(pack1, "gather_cond").reshape(N_DEV, -1)
    o1 = bsz * D
    o2 = o1 + dn_conv.size
    o3 = o2 + gla_w_gate2.size
    c_all = got1[:, :o1].reshape(N_DEV * bsz, D)
    dn_conv_f = got1[:, o1:o2].reshape(N_DEV, DN_CONV_K, -1).transpose(1, 0, 2).reshape(DN_CONV_K, QKV_W)
    gate2_f = got1[:, o2:o3].reshape(N_DEV, GATE_RANK, -1).transpose(1, 0, 2).reshape(GATE_RANK, HEADS * GLA_KEY)
    ffn_conv_f = got1[:, o3:n1].reshape(N_DEV, FFN_CONV_K, -1).transpose(1, 0, 2).reshape(FFN_CONV_K, 2 * D_FF)

    win_t = w_in[0].T.astype(MXU_DT)
    wup_t = ffn_w_up[0].T.astype(MXU_DT)
    (win_all,) = _gather_big([win_t])
    win_p = _pad_in_rows(win_all.reshape(IN_W, D))
    cw_p, cb_p = _ffn_pair(ffn_conv_f, 1), _ffn_pair(ffn_conv_b, 1)

    ncol = w_ada.shape[2]
    b_cols = lax.dynamic_slice_in_dim(b_ada, me * ncol, ncol, axis=1)
    mod_part = _ada_fwd(c_all, w_ada[0], b_cols)
    mod_all = _gather_small(mod_part.reshape(-1, 128), "gather_mod").reshape(N_DEV, N_DEV * bsz, ncol)
    mod = lax.dynamic_slice_in_dim(mod_all, me * bsz, bsz, axis=1).transpose(1, 0, 2).reshape(bsz, 6, 1, D)
    late = [w_o[0].astype(MXU_DT), wup_t, ffn_w_down[0].astype(MXU_DT)]
    ag_send, ag_recv, ag_src, ag_land, ag_token, _ = _exchange_start(
        "gather_start", late, [w.shape for w in late], [win_all, mod_all], by_owner=False)
    mod = mod + ag_token[0, 0]
    sh_a, sc_a, gt_a, sh_f, sc_f, gt_f = (mod[:, i] for i in range(6))

    g0, b0 = ln0_g.reshape(1, D), ln0_b.reshape(1, D)
    alog_row, dt_row = _lane_row(dn_a_log[0]), _lane_row(dn_dt_bias[0])
    grow_dn, grow_gla = jnp.tile(dn_norm_g, (1, HEADS)), jnp.tile(gla_norm_g, (1, HEADS))
    w2 = jnp.zeros((128, HEADS * HD), F32).at[SM_R:SM_R + GATE_RANK].set(_pad_heads(gate2_f, 1))
    bg = _pad_heads(gla_b_gate, 1)

    h_a = _ln0_mod(x, g0, b0, sc_a, sh_a)
    proj = _mm(h_a.reshape(ntok, D), win_p, "nt", F32, "mm_proj", tm=1024, tn=1408).reshape(bsz, t_total, P_W)
    q, k, v, gates = _dn_pre_fwd(proj, dn_conv_f, alog_row, dt_row)
    o_dn, s_dn, inv_dn = _dn_rec_fwd(q, k, v, gates)
    o_gla, s_gla = _gla_rec_fwd(proj, w2, bg)
    o_mix = _mix_out_fwd(o_dn, o_gla, proj, grow_dn, grow_gla)
    late, landed = _exchange_wait("gather_wait", ag_send, ag_recv, ag_src, ag_land, o_mix, by_owner=False)
    wo_all, wup_all, wdn_all = (lax.dynamic_update_slice(l, w[None], (me, 0, 0)) for l, w in zip(landed, late))
    wo_f = wo_all.reshape(D, D)
    wup_f = _ffn_pair(wup_all.reshape(2 * D_FF, D), 0)
    wdn_f = wdn_all.reshape(D_FF, D)
    y = _mm(o_mix.reshape(ntok, D), wo_f, "nn", F32, "mm_wo", tm=1024, tn=1024).reshape(bsz, t_total, D)
    r1, h_f = _res_ln_mod(x, y, gt_a, g0, b0, ln1_g, ln1_b, sc_f, sh_f)
    up, act = _ffn_up_act(h_f, wup_f, cw_p, cb_p)
    y2 = _mm(act.reshape(ntok, D_FF), wdn_f, "nn", F32, "mm_down", tm=1024, tn=1024).reshape(bsz, t_total, D)
    loss_rows, dr2, dy2, dgt_f, d_ln2_g, d_ln2_b = _final_fwd_bwd(r1, y2, gt_f, ln1_g, ln1_b, ln2_g, ln2_b, loss_target)
    loss_part = (0.5 / D) * jnp.sum(loss_rows)

    dy2_2 = dy2.reshape(ntok, D)
    g_wdn = _mm(act.reshape(ntok, D_FF), dy2_2, "tn", MXU_DT, "mm_gwdn", tm=1408, tn=1024)
    dup, d_cw_p, d_cb_p = _ffn_act_bwd(up, dy2, wdn_f, cw_p, cb_p)
    d_ffn_conv, d_ffn_conv_b = _ffn_unpair(d_cw_p, 1), _ffn_unpair(d_cb_p, 1)
    dup_2 = dup.reshape(ntok, 2 * D_FF)
    dh_f = _mm(dup_2, wup_f, "nn", F32, "mm_dhf", tn=1024).reshape(bsz, t_total, D)
    g_wup_t = _mm(dup_2, h_f.reshape(ntok, D), "tn", MXU_DT, "mm_gwup", tm=1408, tn=1024)
    ffn_parts = [_ffn_unpair(g_wup_t, 0).reshape(N_DEV, -1, D), g_wdn.reshape(N_DEV, -1, D)]
    rs_send, rs_recv, rs_src, rs_land, rs_token, _ = _exchange_start(
        "scatter_start", ffn_parts, [p.shape[1:] for p in ffn_parts], [dh_f], by_owner=True)
    dr1, dsc_f, dsh_f, d_ln1_g, d_ln1_b, dy, dgt_a = _ln_bwd_call(
        "ln1_bwd", dr2, dh_f, r1, ln1_g, ln1_b, sc_f + rs_token[0, 0], y=y, gt=gt_a)

    dy_2 = dy.reshape(ntok, D)
    do = _mm(dy_2, wo_f, "nt", F32, "mm_do", tm=1024, tn=1024).reshape(bsz, t_total, D)
    g_wo = _mm(o_mix.reshape(ntok, D), dy_2, "tn", MXU_DT, "mm_gwo", tm=512, tn=1024)
    do_dn, do_gla, dz, dgg, d_dn_norm, d_gla_norm = _mix_out_bwd(do, o_dn, o_gla, proj, grow_dn, grow_gla)
    dq, dk, dv, dgates = _dn_rec_bwd(q, k, v, gates, s_dn, inv_dn, do_dn)
    dqkv, dsm_dn, d_dn_conv, d_alog_row, d_dt_row = _dn_pre_bwd(proj, dq, dk, dv, dgates, dn_conv_f, alog_row, dt_row)
    dgq, dgk, dgv, dsm, d_w2, d_bg = _gla_rec_bwd(proj, w2, bg, s_gla, do_gla, dsm_dn)
    dproj = jnp.concatenate([dqkv, dz, dgq, dgk, dgv, dgg, dsm], axis=-1).reshape(ntok, P_W)
    g_win_p = _mm(dproj, h_a.reshape(ntok, D), "tn", MXU_DT, "mm_gwin", tm=1408, tn=1024)
    mix_parts = [_unpad_in_rows(g_win_p).reshape(N_DEV, -1, D), g_wo.reshape(N_DEV, -1, D)]
    rs2_send, rs2_recv, rs2_src, rs2_land, rs2_token, (win_p_late,) = _exchange_start(
        "scatter_mix_start", mix_parts, [p.shape[1:] for p in mix_parts], [], by_owner=True, carry=[win_p])
    dh_a = _mm(dproj, win_p_late, "nn", F32, "mm_dha", tn=1024).reshape(bsz, t_total, D)
    grad_x, dsc_a, dsh_a, d_ln0_g, d_ln0_b = _ln_bwd_call(
        "ln0_bwd", dr1, dh_a, x, g0, b0, sc_a + rs2_token[0, 0])

    def owner_sum(landed, parts, tag):
        full = [lax.dynamic_update_slice(l, lax.dynamic_slice_in_dim(p, me, 1, axis=0), (me, 0, 0))
                for l, p in zip(landed, parts)]
        return [_sum_slots(f, f"sum_{tag}_{i}") for i, f in enumerate(full)]

    ffn_parts, ffn_landed = _exchange_wait("scatter_wait", rs_send, rs_recv, rs_src, rs_land, grad_x, by_owner=True)
    g_wup_ts, g_wdn_s = owner_sum(ffn_landed, ffn_parts, "ffn")
    mix_parts, mix_landed = _exchange_wait("scatter_mix_wait", rs2_send, rs2_recv, rs2_src, rs2_land, grad_x,
                                           by_owner=True)
    g_win_t, g_wo_s = owner_sum(mix_landed, mix_parts, "mix")

    dmod = jnp.concatenate([dsh_a, dsc_a, dgt_a, dsh_f, dsc_f, dgt_f], axis=1).reshape(-1)
    small_parts = {
        "ln0_g": d_ln0_g, "ln0_b": d_ln0_b, "ln1_g": d_ln1_g, "ln1_b": d_ln1_b, "ln2_g": d_ln2_g, "ln2_b": d_ln2_b,
        "dn_a_log": d_alog_row[:, :HEADS], "dn_dt_bias": d_dt_row[:, :HEADS],
        "dn_norm_g": d_dn_norm, "gla_norm_g": d_gla_norm, "gla_b_gate": _unpad_heads(d_bg, 1),
        "ffn_conv_b": d_ffn_conv_b, "dn_conv": d_dn_conv,
        "gla_w_gate2": _unpad_heads(d_w2[SM_R:SM_R + GATE_RANK], 1), "ffn_conv": d_ffn_conv}
    order = sorted(small_parts)
    flat = jnp.concatenate([small_parts[n].reshape(-1) for n in order] + [loss_part.reshape(1), dmod])
    n3 = flat.size
    rows3 = -(-n3 // 1024) * 8
    pack3 = jnp.pad(flat, (0, rows3 * 128 - n3)).reshape(rows3, 128)
    got3 = _gather_small(pack3, "gather_small_grads")
    tot3 = _sum_slots(got3, "sum_small_grads").reshape(-1)
    grads = {}
    off = 0
    for n in order:
        size = small_parts[n].size
        grads[n] = tot3[off:off + size]
        off += size
    loss = tot3[off]
    off += 1
    dmod_all = got3.reshape(N_DEV, -1)[:, off:off + dmod.size].reshape(N_DEV * bsz, 6 * D)
    dmod_cols = lax.dynamic_slice_in_dim(dmod_all, me * ncol, ncol, axis=1)
    g_wada, g_bada = _ada_bwd(c_all, dmod_all, dmod_cols)
    grads["b_ada"] = g_bada

    def col_shard(full, rows):
        part = full.reshape(rows, -1)
        width = part.shape[1] // N_DEV
        return lax.dynamic_slice_in_dim(part, me * width, width, axis=1)

    grads["dn_conv"] = col_shard(grads["dn_conv"], DN_CONV_K)
    grads["gla_w_gate2"] = col_shard(grads["gla_w_gate2"], GATE_RANK)
    grads["ffn_conv"] = col_shard(grads["ffn_conv"], FFN_CONV_K)
    grads = {n: g.reshape(w_given[n].shape) for n, g in grads.items()}
    grads["w_ada"] = g_wada.reshape(w_ada.shape)
    grads["w_in"] = g_win_t.T.reshape(w_in.shape)
    grads["w_o"] = g_wo_s.reshape(w_o.shape)
    grads["ffn_w_up"] = g_wup_ts.T.reshape(ffn_w_up.shape)
    grads["ffn_w_down"] = g_wdn_s.reshape(ffn_w_down.shape)

    delta, new_m, new_v = {}, {}, {}
    for n in ["w_ada", "w_o", "ffn_w_down"]:
        delta[n], new_m[n], new_v[n] = _adamw(w_given[n], grads[n], m_given[n], v_given[n], "adamw_" + n)
    flip = lambda a: jnp.swapaxes(a, 1, 2)
    for n, g_t in (("w_in", g_win_t), ("ffn_w_up", g_wup_ts)):
        upd = _adamw(flip(w_given[n]), g_t[None], flip(m_given[n]), flip(v_given[n]), "adamw_" + n)
        delta[n], new_m[n], new_v[n] = (flip(a) for a in upd)
    d_s, m_s, v_s = _adamw_many(*[[src[n] for n in SMALL_NAMES] for src in (w_given, grads, m_given, v_given)],
                                "adamw_small")
    for i, n in enumerate(SMALL_NAMES):
        delta[n], new_m[n], new_v[n] = d_s[i], m_s[i], v_s[i]

    return (loss, grad_x, *[grads[n] for n in WEIGHTS], *[delta[n] for n in WEIGHTS],
            *[new_m[n] for n in WEIGHTS], *[new_v[n] for n in WEIGHTS])
```

```python
import functools

import jax
import jax.numpy as jnp
from jax import lax
from jax.experimental import pallas as pl
from jax.experimental.pallas import tpu as pltpu

F32 = jnp.float32
MXU_DT = jnp.bfloat16
MESH = pl.DeviceIdType.MESH
N_DEV = 8

D = 1024
HEADS = 4
HD = 128
CHUNK = 64
GLA_KEY = 64
GLA_TAU = 16.0
GATE_RANK = 16
D_FF = 2816
IN_W = 3608
ALPHA = 2.0 ** 0.25
EPS = 1e-6
DN_CONV_K = 4
FFN_CONV_K = 3
HALO = 8

P_QKV, P_Z, P_GQ, P_GK, P_GV, P_GG, P_SM, P_W = 0, 1536, 2048, 2560, 3072, 3584, 4096, 4224
SM_A, SM_B, SM_R = 0, 4, 8

ADAM_LR, ADAM_B1, ADAM_B2, ADAM_EPS, ADAM_WD, ADAM_STEP = 0.001, 0.9, 0.999, 1e-08, 0.01, 10

VMEM_LIMIT_V7X = 56 * 1024 * 1024


def _params(sem=None):
    return pltpu.CompilerParams(dimension_semantics=sem, vmem_limit_bytes=VMEM_LIMIT_V7X)


def _dg(a, b, dims, prec=None):
    return lax.dot_general(a, b, (dims, ((), ())), precision=prec, preferred_element_type=F32)


def _dot(a, b, prec=None):
    return _dg(a, b, ((1,), (0,)), prec)


def _dot_nt(a, b, prec=None):
    return _dg(a, b, ((1,), (1,)), prec)


def _dot_tn(a, b, prec=None):
    return _dg(a, b, ((0,), (0,)), prec)


def _iota(shape, dim):
    return lax.broadcasted_iota(jnp.int32, shape, dim)


def _sigmoid(x):
    return jax.nn.sigmoid(x)


def _silu(x):
    return x * _sigmoid(x)


def _softplus(x):
    return jnp.maximum(x, 0.0) + jnp.log(1.0 + jnp.exp(-jnp.abs(x)))


def _ln_stats(x):
    mu = jnp.mean(x, axis=-1, keepdims=True)
    xc = x - mu
    rstd = lax.rsqrt(jnp.mean(xc * xc, axis=-1, keepdims=True) + EPS)
    return xc * rstd, rstd


def _ln_bwd(dxhat, xhat, rstd):
    return rstd * (dxhat - jnp.mean(dxhat, axis=-1, keepdims=True)
                   - xhat * jnp.mean(dxhat * xhat, axis=-1, keepdims=True))


NN, NT, TN = ((1,), (0,)), ((1,), (1,)), ((0,), (0,))


def _split2(a):
    hi = a.astype(jnp.bfloat16)
    return hi, (a - hi.astype(F32)).astype(jnp.bfloat16)


def _d3(a, b, dims):
    ah, al = _split2(a)
    bh, bl = _split2(b)
    return _dg(ah, bh, dims) + (_dg(ah, bl, dims) + _dg(al, bh, dims))


@jax.custom_vjp
def _dot3(a, b):
    return _d3(a, b, NN)


_dot3.defvjp(lambda a, b: (_d3(a, b, NN), (a, b)),
             lambda res, g: (_d3(g, res[1], NT), _d3(res[0], g, TN)))


def _split3(b):
    b1 = b.astype(jnp.bfloat16)
    r1 = b - b1.astype(F32)
    b2 = r1.astype(jnp.bfloat16)
    return b1, b2, (r1 - b2.astype(F32)).astype(jnp.bfloat16)


def _sum3(fn, b):
    b1, b2, b3 = _split3(b)
    return fn(b1) + (fn(b2) + fn(b3))


@jax.custom_vjp
def _mask_dot(e, b):
    return _sum3(lambda t: _dg(e, t, NN), b)


_mask_dot.defvjp(lambda e, b: (_mask_dot(e, b), e),
                 lambda e, g: (jnp.zeros_like(e), _sum3(lambda t: _dg(e, t, TN), g)))


@jax.custom_vjp
def _mask_dot_nt(e, b):
    return _sum3(lambda t: _dg(e, t, NT), b)


_mask_dot_nt.defvjp(lambda e, b: (_mask_dot_nt(e, b), e),
                    lambda e, g: (jnp.zeros_like(e), _sum3(lambda t: _dg(t, e, TN), g)))


def _tri_inv_impl(ms):
    n = ms[0].shape[0]
    r, c = _iota((n, n), 0), _iota((n, n), 1)
    eye = (r == c).astype(F32)
    diag = (r >> 3) == (c >> 3)
    ds = [jnp.where(diag, m, 0.0) for m in ms]
    d2s = [_d3(d, d, NN) for d in ds]
    d4s = [_d3(d2, d2, NN) for d2 in d2s]
    invs = [_d3(eye - d, eye + d2, NN) for d, d2 in zip(ds, d2s)]
    invs = [_d3(inv, eye + d4, NN) for inv, d4 in zip(invs, d4s)]
    shift = 3
    while (1 << shift) < n:
        rb, cb = r >> shift, c >> shift
        sel = ((rb & 1) == 1) & (cb == rb - 1)
        tmp = [_d3(inv, jnp.where(sel, m, 0.0), NN) for inv, m in zip(invs, ms)]
        invs = [inv - _d3(t, inv, NN) for t, inv in zip(tmp, invs)]
        shift += 1
    return invs


@jax.custom_vjp
def _tri_inv(ms):
    return _tri_inv_impl(ms)


def _tri_inv_fwd(ms):
    invs = _tri_inv_impl(ms)
    return invs, invs


def _tri_inv_bwd(invs, das):
    tmp = [_d3(a, da, TN) for a, da in zip(invs, das)]
    return ([-_d3(t, a, NT) for t, a in zip(tmp, invs)],)


_tri_inv.defvjp(_tri_inv_fwd, _tri_inv_bwd)


@jax.custom_vjp
def _tri_inv_known(ms, invs):
    return invs


_tri_inv_known.defvjp(lambda ms, invs: (invs, invs),
                      lambda invs, das: (_tri_inv_bwd(invs, das)[0], [jnp.zeros_like(a) for a in invs]))


def _dn_chunk(s_list, q, k, v, gates, inv_known=None, with_inv=False):
    nb = len(q)
    c = q[0].shape[0]
    r64, c64 = _iota((c, c), 0), _iota((c, c), 1)
    causal = r64 >= c64
    strict = r64 > c64
    tri = causal.astype(jnp.bfloat16)
    eye = (_iota((HD, HD), 0) == _iota((HD, HD), 1)).astype(jnp.bfloat16)
    lane = _iota(gates[0].shape, 1)
    lane1 = _iota((1, HD), 1)
    g_all = [_mask_dot(tri, g) for g in gates]
    g_all_t = [_mask_dot_nt(eye, g) for g in g_all]
    row = _iota(g_all_t[0].shape, 0)
    last = [jnp.sum(g, axis=0, keepdims=True) for g in gates]
    prob = [(b, h) for b in range(nb) for h in range(HEADS)]
    sl = [slice(h * HD, (h + 1) * HD) for h in range(HEADS)]
    qh = [q[b][:, sl[h]] for b, h in prob]
    kh = [k[b][:, sl[h]] for b, h in prob]
    vh = [v[b][:, sl[h]] for b, h in prob]
    s = [s_list[b][h] for b, h in prob]
    beta = [jnp.sum(jnp.where(lane == SM_B + h, gates[b], 0.0), axis=-1, keepdims=True) for b, h in prob]
    g_c = [jnp.sum(jnp.where(lane == SM_A + h, g_all[b], 0.0), axis=-1, keepdims=True) for b, h in prob]
    g_r = [jnp.sum(jnp.where(row == SM_A + h, g_all_t[b], 0.0), axis=0, keepdims=True) for b, h in prob]
    g_last = [jnp.sum(jnp.where(lane1 == SM_A + h, last[b], 0.0), axis=-1, keepdims=True) for b, h in prob]
    decay = [jnp.where(causal, jnp.exp(jnp.where(causal, gc - gr, 0.0)), 0.0) for gc, gr in zip(g_c, g_r)]
    kb = [k_ * b_ for k_, b_ in zip(kh, beta)]
    m_low = [jnp.where(strict, _dot_nt(kb_, k_) * d_, 0.0) for kb_, k_, d_ in zip(kb, kh, decay)]
    attn = [_dot_nt(q_, k_) * d_ for q_, k_, d_ in zip(qh, kh, decay)]
    a_inv = _tri_inv(m_low) if inv_known is None else _tri_inv_known(m_low, inv_known)
    eg = [jnp.exp(gc) for gc in g_c]
    uw = [_dot3(a_, jnp.concatenate([v_ * b_, kb_ * e_], axis=1))
          for a_, v_, b_, kb_, e_ in zip(a_inv, vh, beta, kb, eg)]
    v_new = [uw_[:, :HD] - _dot(uw_[:, HD:], s_) for uw_, s_ in zip(uw, s)]
    qs = [_dot(q_ * e_, s_) for q_, e_, s_ in zip(qh, eg, s)]
    o = [qs_ + _dot(a_, vn_) for qs_, a_, vn_ in zip(qs, attn, v_new)]
    k_dec = [k_ * jnp.exp(gl - gc) for k_, gl, gc in zip(kh, g_last, g_c)]
    s_new = [s_ * jnp.exp(gl) + _dot_tn(kd_, vn_) for s_, gl, kd_, vn_ in zip(s, g_last, k_dec, v_new)]
    outs = [jnp.concatenate(o[b * HEADS:(b + 1) * HEADS], axis=-1) for b in range(nb)]
    states = [s_new[b * HEADS:(b + 1) * HEADS] for b in range(nb)]
    return (outs, states, a_inv) if with_inv else (outs, states)


def _gla_chunk(st_list, q, k, v, small, w2, bg):
    nb = len(q)
    c = q[0].shape[0]
    causal = _iota((c, c), 0) >= _iota((c, c), 1)
    tri = causal.astype(jnp.bfloat16)
    la_all = [-_softplus(-(_dot(sm, w2) + bg)) * (1.0 / GLA_TAU) for sm in small]
    b_all = [_mask_dot(tri, la) for la in la_all]
    prob = [(b, h) for b in range(nb) for h in range(HEADS)]
    sl = [slice(h * HD, (h + 1) * HD) for h in range(HEADS)]
    kh = [k[b][:, sl[h]] for b, h in prob]
    vh = [v[b][:, sl[h]] for b, h in prob]
    st = [st_list[b][h] for b, h in prob]
    bc = [b_all[b][:, sl[h]] for b, h in prob]
    b_last = [jnp.sum(la_all[b][:, sl[h]], axis=0, keepdims=True) for b, h in prob]
    q_dec = [q[b][:, sl[h]] * (GLA_KEY ** -0.5) * jnp.exp(bc_) for (b, h), bc_ in zip(prob, bc)]
    attn = [jnp.where(causal, _dot_nt(qd, k_ * jnp.exp(-bc_)), 0.0) for qd, k_, bc_ in zip(q_dec, kh, bc)]
    inter = [_dot_nt(qd, st_) for qd, st_ in zip(q_dec, st)]
    o = [i_ + _dot(a_, v_) for i_, a_, v_ in zip(inter, attn, vh)]
    k_dec = [k_ * jnp.exp(bl - bc_) for k_, bl, bc_ in zip(kh, b_last, bc)]
    s_new = [st_ * jnp.exp(bl) + _dot_tn(v_, kd) for st_, bl, v_, kd in zip(st, b_last, vh, k_dec)]
    outs = [jnp.concatenate(o[b * HEADS:(b + 1) * HEADS], axis=-1) for b in range(nb)]
    return outs, [s_new[b * HEADS:(b + 1) * HEADS] for b in range(nb)]


def _dn_qkv(y):
    act = _silu(y)
    parts = []
    for i in range(2 * HEADS):
        xh = act[:, i * HD:(i + 1) * HD]
        xh = xh * lax.rsqrt(jnp.sum(xh * xh, axis=-1, keepdims=True) + EPS)
        parts.append(xh * (HD ** -0.5) if i < HEADS else xh)
    qk = jnp.concatenate(parts, axis=-1)
    return qk[:, :HEADS * HD], qk[:, HEADS * HD:], act[:, 2 * HEADS * HD:]


def _dn_gates(small, alog_row, dt_row):
    lane = _iota(small.shape, 1)
    log_a = -jnp.exp(alog_row) * _softplus(small + dt_row)
    return jnp.where(lane < SM_B, log_a, jnp.where(lane < SM_R, _sigmoid(small), 0.0))


def _gate_norm(o, z, grow):
    parts = []
    for h in range(HEADS):
        oh = o[:, h * HD:(h + 1) * HD]
        parts.append(oh * lax.rsqrt(jnp.mean(oh * oh, axis=-1, keepdims=True) + EPS))
    return jnp.concatenate(parts, axis=-1) * grow * _silu(z)


def _conv_rows(xrows, w_ref, k_taps):
    n = xrows.shape[0]
    acc = xrows * w_ref[k_taps - 1:k_taps, :]
    for s in range(1, k_taps):
        acc = acc + pltpu.roll(xrows, s, 0) * w_ref[k_taps - 1 - s:k_taps - s, :]
    return acc


def _shift_up(x, s):
    return x if s == 0 else pltpu.roll(x, x.shape[0] - s, 0)


def _div_tile(n, cap, mult=8):
    best = None
    for t in range(mult, min(n, cap) + 1, mult):
        if n % t == 0:
            best = t
    return best if best is not None else n


def _halo_prev(tt):
    return lambda b, t: (b, jnp.maximum(t * (tt // HALO) - 1, 0))


def _halo_next(tt, t_total):
    return lambda b, t: (b, jnp.minimum((t + 1) * (tt // HALO), t_total // HALO - 1))


def _mm(a, b, mode, out_dtype, name, tm=512, tn=512, tk=None):
    if mode == "nn":
        (m, k), n = a.shape, b.shape[1]
    elif mode == "nt":
        (m, k), n = a.shape, b.shape[0]
    else:
        (k, m), n = a.shape, b.shape[1]
    tm, tn = min(tm, m), min(tn, n)
    tk = k if tk is None else min(tk, k)
    assert m % tm == 0 and n % tn == 0 and k % tk == 0, (name, a.shape, b.shape, tm, tn, tk)
    nk = k // tk
    if mode == "tn":
        a_spec = pl.BlockSpec((tk, tm), lambda i, j, kk: (kk, i))
    else:
        a_spec = pl.BlockSpec((tm, tk), lambda i, j, kk: (i, kk))
    if mode == "nt":
        b_spec = pl.BlockSpec((tn, tk), lambda i, j, kk: (j, kk))
    else:
        b_spec = pl.BlockSpec((tk, tn), lambda i, j, kk: (kk, j))
    dims = {"nn": ((1,), (0,)), "nt": ((1,), (1,)), "tn": ((0,), (0,))}[mode]

    def body(a_ref, b_ref, o_ref, *acc):
        p = _dg(a_ref[...], b_ref[...], dims)
        if nk == 1:
            o_ref[...] = p.astype(out_dtype)
        else:
            kk = pl.program_id(2)

            @pl.when(kk == 0)
            def _():
                acc[0][...] = p

            @pl.when(kk > 0)
            def _():
                acc[0][...] += p

            @pl.when(kk == nk - 1)
            def _():
                o_ref[...] = acc[0][...].astype(out_dtype)

    return pl.pallas_call(
        body, name=name, grid=(m // tm, n // tn, nk),
        in_specs=[a_spec, b_spec],
        out_specs=pl.BlockSpec((tm, tn), lambda i, j, kk: (i, j)),
        out_shape=jax.ShapeDtypeStruct((m, n), out_dtype),
        scratch_shapes=[pltpu.VMEM((tm, tn), F32)] if nk > 1 else [],
        compiler_params=_params(("parallel", "parallel", "arbitrary")),
    )(a, b)


def _ada_fwd(c_all, w_ada, b_cols):
    def body(c_ref, w_ref, b_ref, o_ref):
        cond = _silu(c_ref[...]).astype(MXU_DT)
        o_ref[...] = _dot(cond, w_ref[...].astype(MXU_DT)) + b_ref[...]

    return pl.pallas_call(body, name="ada_fwd", out_shape=jax.ShapeDtypeStruct((c_all.shape[0], w_ada.shape[1]), F32),
                          compiler_params=_params())(c_all, w_ada, b_cols)


def _ada_bwd(c_all, dmod_all, dmod_cols):
    def body(c_ref, da_ref, dc_ref, gw_ref, gb_ref):
        cond = _silu(c_ref[...]).astype(MXU_DT)
        gw_ref[...] = _dot_tn(cond, dc_ref[...].astype(MXU_DT))
        gb_ref[...] = jnp.sum(da_ref[...], axis=0, keepdims=True)

    return pl.pallas_call(
        body, name="ada_bwd",
        out_shape=(jax.ShapeDtypeStruct((c_all.shape[1], dmod_cols.shape[1]), F32),
                   jax.ShapeDtypeStruct((1, dmod_all.shape[1]), F32)),
        compiler_params=_params())(c_all, dmod_all, dmod_cols)


def _tok_spec(tt, width=D):
    return pl.BlockSpec((1, tt, width), lambda b, t: (b, t, 0))


def _vec_spec(width=D):
    return pl.BlockSpec((1, width), lambda b, t: (0, 0))


def _bvec_spec(width=D):
    return pl.BlockSpec((1, 1, width), lambda b, t: (b, 0, 0))


def _ln0_mod(x, g0, b0, sc, sh):
    bsz, t_total, _ = x.shape
    tt = _div_tile(t_total, 256)

    def body(x_ref, g_ref, b_ref, sc_ref, sh_ref, h_ref):
        xh, _ = _ln_stats(x_ref[0])
        x0 = xh * g_ref[...] + b_ref[...]
        h_ref[0] = (x0 * (1.0 + sc_ref[0]) + sh_ref[0]).astype(MXU_DT)

    return pl.pallas_call(
        body, name="ln0_mod", grid=(bsz, t_total // tt),
        in_specs=[_tok_spec(tt), _vec_spec(), _vec_spec(), _bvec_spec(), _bvec_spec()],
        out_specs=_tok_spec(tt), out_shape=jax.ShapeDtypeStruct(x.shape, MXU_DT),
        compiler_params=_params(("parallel", "parallel")))(x, g0, b0, sc, sh)


def _res_ln_mod(x, y, gt, g0, b0, g1, b1, sc, sh):
    bsz, t_total, _ = x.shape
    tt = _div_tile(t_total, 256)

    def body(x_ref, y_ref, gt_ref, g0_ref, b0_ref, g1_ref, b1_ref, sc_ref, sh_ref, r_ref, h_ref):
        xh, _ = _ln_stats(x_ref[0])
        r = ALPHA * (xh * g0_ref[...] + b0_ref[...]) + (1.0 + gt_ref[0]) * y_ref[0].astype(F32)
        r_ref[0] = r
        rh, _ = _ln_stats(r)
        x1 = rh * g1_ref[...] + b1_ref[...]
        h_ref[0] = (x1 * (1.0 + sc_ref[0]) + sh_ref[0]).astype(MXU_DT)

    return pl.pallas_call(
        body, name="res_ln_mod", grid=(bsz, t_total // tt),
        in_specs=[_tok_spec(tt), _tok_spec(tt), _bvec_spec(), _vec_spec(), _vec_spec(), _vec_spec(), _vec_spec(),
                  _bvec_spec(), _bvec_spec()],
        out_specs=(_tok_spec(tt), _tok_spec(tt)),
        out_shape=(jax.ShapeDtypeStruct(x.shape, F32), jax.ShapeDtypeStruct(x.shape, MXU_DT)),
        compiler_params=_params(("parallel", "parallel")))(x, y, gt, g0, b0, g1, b1, sc, sh)


def _final_fwd_bwd(r1, y2, gt, g1, b1, g2, b2, target):
    bsz, t_total, _ = r1.shape
    tt = _div_tile(t_total, 256)

    def body(r1_ref, y2_ref, gt_ref, g1_ref, b1_ref, g2_ref, b2_ref, tg_ref,
             loss_ref, dr2_ref, dy2_ref, dgt_ref, dg2_ref, db2_ref):
        b, t = pl.program_id(0), pl.program_id(1)

        @pl.when((b == 0) & (t == 0))
        def _():
            loss_ref[...] = jnp.zeros_like(loss_ref)
            dg2_ref[...] = jnp.zeros_like(dg2_ref)
            db2_ref[...] = jnp.zeros_like(db2_ref)

        @pl.when(t == 0)
        def _():
            dgt_ref[...] = jnp.zeros_like(dgt_ref)

        rh1, _ = _ln_stats(r1_ref[0])
        x1 = rh1 * g1_ref[...] + b1_ref[...]
        y2 = y2_ref[0].astype(F32)
        gate = 1.0 + gt_ref[0]
        xh2, rstd2 = _ln_stats(ALPHA * x1 + gate * y2)
        err = xh2 * g2_ref[...] + b2_ref[...] - tg_ref[0]
        loss_ref[...] += jnp.sum(err * err, axis=0, keepdims=True)
        dx2 = err * (1.0 / D)
        dg2_ref[...] += jnp.sum(dx2 * xh2, axis=0, keepdims=True)
        db2_ref[...] += jnp.sum(dx2, axis=0, keepdims=True)
        dr2 = _ln_bwd(dx2 * g2_ref[...], xh2, rstd2)
        dr2_ref[0] = dr2
        dy2_ref[0] = (gate * dr2).astype(MXU_DT)
        dgt_ref[0] += jnp.sum(dr2 * y2, axis=0, keepdims=True)

    vec_out = jax.ShapeDtypeStruct((1, D), F32)
    return pl.pallas_call(
        body, name="final_fwd_bwd", grid=(bsz, t_total // tt),
        in_specs=[_tok_spec(tt), _tok_spec(tt), _bvec_spec(), _vec_spec(), _vec_spec(), _vec_spec(), _vec_spec(),
                  _tok_spec(tt)],
        out_specs=(_vec_spec(), _tok_spec(tt), _tok_spec(tt), _bvec_spec(), _vec_spec(), _vec_spec()),
        out_shape=(vec_out, jax.ShapeDtypeStruct(r1.shape, F32), jax.ShapeDtypeStruct(r1.shape, MXU_DT),
                   jax.ShapeDtypeStruct((bsz, 1, D), F32), vec_out, vec_out),
        compiler_params=_params(("arbitrary", "arbitrary")))(r1, y2, gt, g1, b1, g2, b2, target)


def _ln_bwd_call(name, d_res, d_h, src, g, b, sc, y=None, gt=None):
    bsz, t_total, _ = src.shape
    tt = _div_tile(t_total, 256)
    has_y = y is not None

    def body(*refs):
        if has_y:
            (dres_ref, dh_ref, src_ref, g_ref, b_ref, sc_ref, y_ref, gt_ref,
             dsrc_ref, dsc_ref, dsh_ref, dg_ref, db_ref, dy_ref, dgt_ref) = refs
        else:
            (dres_ref, dh_ref, src_ref, g_ref, b_ref, sc_ref,
             dsrc_ref, dsc_ref, dsh_ref, dg_ref, db_ref) = refs
        bi, t = pl.program_id(0), pl.program_id(1)

        @pl.when((bi == 0) & (t == 0))
        def _():
            dg_ref[...] = jnp.zeros_like(dg_ref)
            db_ref[...] = jnp.zeros_like(db_ref)

        @pl.when(t == 0)
        def _():
            dsc_ref[...] = jnp.zeros_like(dsc_ref)
            dsh_ref[...] = jnp.zeros_like(dsh_ref)
            if has_y:
                dgt_ref[...] = jnp.zeros_like(dgt_ref)

        xh, rstd = _ln_stats(src_ref[0])
        xv = xh * g_ref[...] + b_ref[...]
        dh = dh_ref[0].astype(F32)
        dx = ALPHA * dres_ref[0] + dh * (1.0 + sc_ref[0])
        dsc_ref[0] += jnp.sum(dh * xv, axis=0, keepdims=True)
        dsh_ref[0] += jnp.sum(dh, axis=0, keepdims=True)
        dg_ref[...] += jnp.sum(dx * xh, axis=0, keepdims=True)
        db_ref[...] += jnp.sum(dx, axis=0, keepdims=True)
        dsrc = _ln_bwd(dx * g_ref[...], xh, rstd)
        dsrc_ref[0] = dsrc
        if has_y:
            dy_ref[0] = ((1.0 + gt_ref[0]) * dsrc).astype(MXU_DT)
            dgt_ref[0] += jnp.sum(dsrc * y_ref[0].astype(F32), axis=0, keepdims=True)

    vec_out = jax.ShapeDtypeStruct((1, D), F32)
    bvec_out = jax.ShapeDtypeStruct((bsz, 1, D), F32)
    in_specs = [_tok_spec(tt), _tok_spec(tt), _tok_spec(tt), _vec_spec(), _vec_spec(), _bvec_spec()]
    out_specs = [_tok_spec(tt), _bvec_spec(), _bvec_spec(), _vec_spec(), _vec_spec()]
    out_shape = [jax.ShapeDtypeStruct(src.shape, F32), bvec_out, bvec_out, vec_out, vec_out]
    args = [d_res, d_h, src, g, b, sc]
    if has_y:
        in_specs += [_tok_spec(tt), _bvec_spec()]
        out_specs += [_tok_spec(tt), _bvec_spec()]
        out_shape += [jax.ShapeDtypeStruct(src.shape, MXU_DT), bvec_out]
        args += [y, gt]
    return pl.pallas_call(body, name=name, grid=(bsz, t_total // tt), in_specs=in_specs, out_specs=tuple(out_specs),
                          out_shape=tuple(out_shape), compiler_params=_params(("arbitrary", "arbitrary")))(*args)


FFN_TC = 256
FFN_NJ = D_FF // FFN_TC
FFN_PW = 2 * FFN_TC


def _ffn_pair(a, axis):
    shp = list(a.shape)
    a4 = a.reshape(shp[:axis] + [2, FFN_NJ, FFN_TC] + shp[axis + 1:])
    return jnp.swapaxes(a4, axis, axis + 1).reshape(shp)


def _ffn_unpair(a, axis):
    shp = list(a.shape)
    a4 = a.reshape(shp[:axis] + [FFN_NJ, 2, FFN_TC] + shp[axis + 1:])
    return jnp.swapaxes(a4, axis, axis + 1).reshape(shp)


def _ffn_up_act(h, w_up, cw, cb):
    bsz, t_total, _ = h.shape
    tt = _div_tile(t_total, 256)
    def body(h_ref, wu_ref, w_ref, b_ref, up_ref, o_ref, carry_ref):
        up_t = _dot_nt(h_ref[0], wu_ref[...])
        up_ref[0] = up_t
        prev = jnp.where(pl.program_id(2) == 0, 0.0, carry_ref[...])
        rows = jnp.concatenate([prev, up_t], axis=0)
        u = _conv_rows(rows, w_ref, FFN_CONV_K)[HALO:] + b_ref[...]
        o_ref[0] = (_silu(u[:, :FFN_TC]) * u[:, FFN_TC:]).astype(MXU_DT)
        carry_ref[...] = up_t[tt - HALO:, :]

    return pl.pallas_call(
        body, name="ffn_up_act", grid=(bsz, FFN_NJ, t_total // tt),
        in_specs=[pl.BlockSpec((1, tt, D), lambda b, j, t: (b, t, 0)),
                  pl.BlockSpec((FFN_PW, D), lambda b, j, t: (j, 0)),
                  pl.BlockSpec((FFN_CONV_K, FFN_PW), lambda b, j, t: (0, j)),
                  pl.BlockSpec((1, FFN_PW), lambda b, j, t: (0, j))],
        out_specs=(pl.BlockSpec((1, tt, FFN_PW), lambda b, j, t: (b, t, j)),
                   pl.BlockSpec((1, tt, FFN_TC), lambda b, j, t: (b, t, j))),
        out_shape=(jax.ShapeDtypeStruct((bsz, t_total, 2 * D_FF), F32),
                   jax.ShapeDtypeStruct((bsz, t_total, D_FF), MXU_DT)),
        scratch_shapes=[pltpu.VMEM((HALO, FFN_PW), F32)],
        compiler_params=_params(("parallel", "parallel", "arbitrary")))(h, w_up, cw, cb)


HALO16 = 16


def _ffn_act_bwd(up, dy2, w_down, cw, cb):
    bsz, t_total, width = up.shape
    tt = _div_tile(t_total, 256)
    nt = t_total // tt
    hp, hn = _halo_prev(tt), _halo_next(tt, t_total)

    def body(x_ref, xp_ref, xn_ref, dy_ref, dyn_ref, wd_ref, w_ref, b_ref, dup_ref, dw_ref, db_ref):
        b, t = pl.program_id(1), pl.program_id(2)

        @pl.when((b == 0) & (t == 0))
        def _():
            dw_ref[...] = jnp.zeros_like(dw_ref)
            db_ref[...] = jnp.zeros_like(db_ref)

        prev = jnp.where(t == 0, 0.0, xp_ref[0])
        rows = jnp.concatenate([prev, x_ref[0], xn_ref[0]], axis=0)
        u = _conv_rows(rows, w_ref, FFN_CONV_K)[HALO:] + b_ref[...]
        g_pre, v_pre = u[:, :FFN_TC], u[:, FFN_TC:]
        valid = (_iota((tt + HALO, 1), 0) < tt) | (t < nt - 1)
        da = jnp.concatenate([_dot_nt(dy_ref[0], wd_ref[...]), _dot_nt(dyn_ref[0], wd_ref[...])[:HALO]], axis=0)
        da_ext = jnp.where(valid, da, 0.0)
        sg = _sigmoid(g_pre)
        gs = g_pre * sg
        du = jnp.concatenate([da_ext * v_pre * (sg + gs * (1.0 - sg)), da_ext * gs], axis=1)
        dup = du * w_ref[FFN_CONV_K - 1:FFN_CONV_K, :]
        for s in range(1, FFN_CONV_K):
            dup = dup + _shift_up(du, s) * w_ref[FFN_CONV_K - 1 - s:FFN_CONV_K - s, :]
        dup_ref[0] = dup[:tt].astype(MXU_DT)
        du_t = du[:tt]
        db_ref[...] += jnp.sum(du_t, axis=0, keepdims=True)
        for k in range(FFN_CONV_K):
            s = FFN_CONV_K - 1 - k
            xs = (rows if s == 0 else pltpu.roll(rows, s, 0))[HALO:HALO + tt]
            dw_ref[k:k + 1, :] += jnp.sum(du_t * xs, axis=0, keepdims=True)

    def halo(h, w):
        return pl.BlockSpec((1, HALO, w), lambda j, b, t: (*h(b, t), j))

    wspec = lambda rows_: pl.BlockSpec((rows_, FFN_PW), lambda j, b, t: (0, j))
    tile = pl.BlockSpec((1, tt, FFN_PW), lambda j, b, t: (b, t, j))
    dy_next = lambda j, b, t: (b, jnp.minimum((t + 1) * (tt // HALO16), t_total // HALO16 - 1), 0)
    return pl.pallas_call(
        body, name="ffn_act_bwd", grid=(FFN_NJ, bsz, nt),
        in_specs=[tile, halo(hp, FFN_PW), halo(hn, FFN_PW),
                  pl.BlockSpec((1, tt, D), lambda j, b, t: (b, t, 0)), pl.BlockSpec((1, HALO16, D), dy_next),
                  pl.BlockSpec((FFN_TC, D), lambda j, b, t: (j, 0)), wspec(FFN_CONV_K), wspec(1)],
        out_specs=(tile, wspec(FFN_CONV_K), wspec(1)),
        out_shape=(jax.ShapeDtypeStruct(up.shape, MXU_DT), jax.ShapeDtypeStruct((FFN_CONV_K, width), F32),
                   jax.ShapeDtypeStruct((1, width), F32)),
        compiler_params=_params(("arbitrary", "arbitrary", "arbitrary")))(up, up, up, dy2, dy2, w_down, cw, cb)


QKV_W = 3 * HEADS * HD
SM_BLK = P_SM // 128


def _dn_pre_fwd(proj, conv_w, alog_row, dt_row):
    bsz, t_total, _ = proj.shape
    tt = _div_tile(t_total, 256)
    hp = _halo_prev(tt)

    def body(x_ref, xp_ref, sm_ref, w_ref, al_ref, dt_ref, q_ref, k_ref, v_ref, g_ref):
        prev = jnp.where(pl.program_id(1) == 0, 0.0, xp_ref[0])
        y = _conv_rows(jnp.concatenate([prev, x_ref[0]], axis=0), w_ref, DN_CONV_K)[HALO:]
        q_ref[0], k_ref[0], v_ref[0] = _dn_qkv(y)
        g_ref[0] = _dn_gates(sm_ref[0], al_ref[...], dt_ref[...])

    out512 = jax.ShapeDtypeStruct((bsz, t_total, HEADS * HD), F32)
    return pl.pallas_call(
        body, name="dn_pre_fwd", grid=(bsz, t_total // tt),
        in_specs=[pl.BlockSpec((1, tt, QKV_W), lambda b, t: (b, t, 0)),
                  pl.BlockSpec((1, HALO, QKV_W), lambda b, t: (*hp(b, t), 0)),
                  pl.BlockSpec((1, tt, 128), lambda b, t: (b, t, SM_BLK)),
                  pl.BlockSpec((DN_CONV_K, QKV_W), lambda b, t: (0, 0)), _vec_spec(128), _vec_spec(128)],
        out_specs=(_tok_spec(tt, 512), _tok_spec(tt, 512), _tok_spec(tt, 512), _tok_spec(tt, 128)),
        out_shape=(out512, out512, out512, jax.ShapeDtypeStruct((bsz, t_total, 128), F32)),
        compiler_params=_params(("parallel", "parallel")))(proj, proj, proj, conv_w, alog_row, dt_row)


def _dn_pre_bwd(proj, dq, dk, dv, dgates, conv_w, alog_row, dt_row):
    bsz, t_total, _ = proj.shape
    tt = _div_tile(t_total, 128)
    nt = t_total // tt
    hp, hn = _halo_prev(tt), _halo_next(tt, t_total)

    def body(x_ref, xp_ref, xn_ref, sm_ref, dq_ref, dqn_ref, dk_ref, dkn_ref, dv_ref, dvn_ref, dg_ref,
             w_ref, al_ref, dt_ref, dx_ref, dsm_ref, dw_ref, dal_ref, ddt_ref):
        b, t = pl.program_id(0), pl.program_id(1)

        @pl.when((b == 0) & (t == 0))
        def _():
            dw_ref[...] = jnp.zeros_like(dw_ref)
            dal_ref[...] = jnp.zeros_like(dal_ref)
            ddt_ref[...] = jnp.zeros_like(ddt_ref)

        prev = jnp.where(t == 0, 0.0, xp_ref[0])
        rows = jnp.concatenate([prev, x_ref[0], xn_ref[0]], axis=0)
        y = _conv_rows(rows, w_ref, DN_CONV_K)[HALO:]
        valid = (_iota((tt + HALO, 1), 0) < tt) | (t < nt - 1)

        def ext(tile_ref, next_ref):
            return jnp.where(valid, jnp.concatenate([tile_ref[0], next_ref[0]], axis=0), 0.0)

        _, vjp_qkv = jax.vjp(_dn_qkv, y)
        (dy,) = vjp_qkv((ext(dq_ref, dqn_ref), ext(dk_ref, dkn_ref), ext(dv_ref, dvn_ref)))
        dy = jnp.where(valid, dy, 0.0)
        dx = dy * w_ref[DN_CONV_K - 1:DN_CONV_K, :]
        for s in range(1, DN_CONV_K):
            dx = dx + _shift_up(dy, s) * w_ref[DN_CONV_K - 1 - s:DN_CONV_K - s, :]
        dx_ref[0] = dx[:tt].astype(MXU_DT)
        dy_t = dy[:tt]
        for k in range(DN_CONV_K):
            s = DN_CONV_K - 1 - k
            xs = (rows if s == 0 else pltpu.roll(rows, s, 0))[HALO:HALO + tt]
            dw_ref[k:k + 1, :] += jnp.sum(dy_t * xs, axis=0, keepdims=True)
        _, vjp_g = jax.vjp(_dn_gates, sm_ref[0], al_ref[...], dt_ref[...])
        dsm, dal, ddt = vjp_g(dg_ref[0])
        dsm_ref[0] = dsm
        dal_ref[...] += dal
        ddt_ref[...] += ddt

    def tile(width, blk=0):
        return pl.BlockSpec((1, tt, width), lambda b, t: (b, t, blk))

    def halo(h, width):
        return pl.BlockSpec((1, HALO, width), lambda b, t: (*h(b, t), 0))

    return pl.pallas_call(
        body, name="dn_pre_bwd", grid=(bsz, nt),
        in_specs=[tile(QKV_W), halo(hp, QKV_W), halo(hn, QKV_W), tile(128, SM_BLK),
                  tile(512), halo(hn, 512), tile(512), halo(hn, 512), tile(512), halo(hn, 512), tile(128),
                  pl.BlockSpec((DN_CONV_K, QKV_W), lambda b, t: (0, 0)), _vec_spec(128), _vec_spec(128)],
        out_specs=(tile(QKV_W), tile(128), pl.BlockSpec((DN_CONV_K, QKV_W), lambda b, t: (0, 0)),
                   _vec_spec(128), _vec_spec(128)),
        out_shape=(jax.ShapeDtypeStruct((bsz, t_total, QKV_W), MXU_DT), jax.ShapeDtypeStruct((bsz, t_total, 128), F32),
                   jax.ShapeDtypeStruct((DN_CONV_K, QKV_W), F32), jax.ShapeDtypeStruct((1, 128), F32),
                   jax.ShapeDtypeStruct((1, 128), F32)),
        compiler_params=_params(("arbitrary", "arbitrary")))(
            proj, proj, proj, proj, dq, dq, dk, dk, dv, dv, dgates, conv_w, alog_row, dt_row)


def _state_spec(bsz, idx):
    return pl.BlockSpec((bsz, 1, HEADS, HD, HD), lambda c: (0, idx(c), 0, 0, 0))


def _inv_spec(bsz, idx):
    return pl.BlockSpec((bsz, 1, HEADS, CHUNK, CHUNK), lambda c: (0, idx(c), 0, 0, 0))


def _chunk_spec(bsz, width, idx, blk=0):
    return pl.BlockSpec((bsz, CHUNK, width), lambda c: (0, idx(c), blk))


def _dn_rec_fwd(q, k, v, gates):
    bsz, t_total, _ = q.shape
    nc = t_total // CHUNK
    fwd = lambda c: c

    def body(q_ref, k_ref, v_ref, g_ref, o_ref, ss_ref, inv_ref, s_ref):
        @pl.when(pl.program_id(0) == 0)
        def _():
            s_ref[...] = jnp.zeros_like(s_ref)

        seqs = range(bsz)
        s_list = [[s_ref[b * HEADS + h] for h in range(HEADS)] for b in seqs]
        for b in seqs:
            for h in range(HEADS):
                ss_ref[b, 0, h] = s_list[b][h]
        o, new_s, invs = _dn_chunk(s_list, [q_ref[b] for b in seqs], [k_ref[b] for b in seqs],
                                   [v_ref[b] for b in seqs], [g_ref[b] for b in seqs], with_inv=True)
        for b in seqs:
            o_ref[b] = o[b]
            for h in range(HEADS):
                s_ref[b * HEADS + h] = new_s[b][h]
                inv_ref[b, 0, h] = invs[b * HEADS + h]

    return pl.pallas_call(
        body, name="dn_rec_fwd", grid=(nc,),
        in_specs=[_chunk_spec(bsz, 512, fwd)] * 3 + [_chunk_spec(bsz, 128, fwd)],
        out_specs=(_chunk_spec(bsz, 512, fwd), _state_spec(bsz, fwd), _inv_spec(bsz, fwd)),
        out_shape=(jax.ShapeDtypeStruct(q.shape, F32), jax.ShapeDtypeStruct((bsz, nc, HEADS, HD, HD), F32),
                   jax.ShapeDtypeStruct((bsz, nc, HEADS, CHUNK, CHUNK), F32)),
        scratch_shapes=[pltpu.VMEM((bsz * HEADS, HD, HD), F32)],
        compiler_params=_params(("arbitrary",)))(q, k, v, gates)


def _dn_rec_bwd(q, k, v, gates, states, invs, do):
    bsz, t_total, _ = q.shape
    nc = t_total // CHUNK
    rev = lambda c: nc - 1 - c

    def body(q_ref, k_ref, v_ref, g_ref, ss_ref, inv_ref, do_ref, dq_ref, dk_ref, dv_ref, dg_ref, ds_ref):
        @pl.when(pl.program_id(0) == 0)
        def _():
            ds_ref[...] = jnp.zeros_like(ds_ref)

        seqs = range(bsz)
        s_list = [[ss_ref[b, 0, h] for h in range(HEADS)] for b in seqs]
        known = [inv_ref[b, 0, h] for b in seqs for h in range(HEADS)]
        _, vjp = jax.vjp(functools.partial(_dn_chunk, inv_known=known),
                         s_list, [q_ref[b] for b in seqs], [k_ref[b] for b in seqs],
                         [v_ref[b] for b in seqs], [g_ref[b] for b in seqs])
        ds_in, dq, dk, dv, dg = vjp(([do_ref[b] for b in seqs],
                                     [[ds_ref[b * HEADS + h] for h in range(HEADS)] for b in seqs]))
        for b in seqs:
            dq_ref[b], dk_ref[b], dv_ref[b], dg_ref[b] = dq[b], dk[b], dv[b], dg[b]
            for h in range(HEADS):
                ds_ref[b * HEADS + h] = ds_in[b][h]

    tok = lambda width: _chunk_spec(bsz, width, rev)
    out512 = jax.ShapeDtypeStruct(q.shape, F32)
    return pl.pallas_call(
        body, name="dn_rec_bwd", grid=(nc,),
        in_specs=[tok(512), tok(512), tok(512), tok(128), _state_spec(bsz, rev), _inv_spec(bsz, rev), tok(512)],
        out_specs=(tok(512), tok(512), tok(512), tok(128)),
        out_shape=(out512, out512, out512, jax.ShapeDtypeStruct(gates.shape, F32)),
        scratch_shapes=[pltpu.VMEM((bsz * HEADS, HD, HD), F32)],
        compiler_params=_params(("arbitrary",)))(q, k, v, gates, states, invs, do)


GQ_BLK, GK_BLK, GV_BLK = P_GQ // 512, P_GK // 512, P_GV // 512


def _gla_rec_fwd(proj, w2, bg):
    bsz, t_total, _ = proj.shape
    nc = t_total // CHUNK

    fwd = lambda c: c

    def body(q_ref, k_ref, v_ref, sm_ref, w2_ref, bg_ref, o_ref, ss_ref, s_ref):
        @pl.when(pl.program_id(0) == 0)
        def _():
            s_ref[...] = jnp.zeros_like(s_ref)

        seqs = range(bsz)
        s_list = [[s_ref[b * HEADS + h] for h in range(HEADS)] for b in seqs]
        for b in seqs:
            for h in range(HEADS):
                ss_ref[b, 0, h] = s_list[b][h]
        o, new_s = _gla_chunk(s_list, [q_ref[b] for b in seqs], [k_ref[b] for b in seqs], [v_ref[b] for b in seqs],
                              [sm_ref[b] for b in seqs], w2_ref[...], bg_ref[...])
        for b in seqs:
            o_ref[b] = o[b]
            for h in range(HEADS):
                s_ref[b * HEADS + h] = new_s[b][h]

    col = lambda blk, width=512: _chunk_spec(bsz, width, fwd, blk)
    return pl.pallas_call(
        body, name="gla_rec_fwd", grid=(nc,),
        in_specs=[col(GQ_BLK), col(GK_BLK), col(GV_BLK), col(SM_BLK, 128),
                  pl.BlockSpec((128, 512), lambda c: (0, 0)), pl.BlockSpec((1, 512), lambda c: (0, 0))],
        out_specs=(col(0), _state_spec(bsz, fwd)),
        out_shape=(jax.ShapeDtypeStruct((bsz, t_total, 512), F32),
                   jax.ShapeDtypeStruct((bsz, nc, HEADS, HD, HD), F32)),
        scratch_shapes=[pltpu.VMEM((bsz * HEADS, HD, HD), F32)],
        compiler_params=_params(("arbitrary",)))(proj, proj, proj, proj, w2, bg)


def _gla_rec_bwd(proj, w2, bg, states, do, dsm_dn):
    bsz, t_total, _ = proj.shape
    nc = t_total // CHUNK
    rev = lambda c: nc - 1 - c

    def body(q_ref, k_ref, v_ref, sm_ref, w2_ref, bg_ref, ss_ref, do_ref, dsd_ref,
             dq_ref, dk_ref, dv_ref, dsm_ref, dw2_ref, dbg_ref, ds_ref):
        @pl.when(pl.program_id(0) == 0)
        def _():
            dw2_ref[...] = jnp.zeros_like(dw2_ref)
            dbg_ref[...] = jnp.zeros_like(dbg_ref)
            ds_ref[...] = jnp.zeros_like(ds_ref)

        seqs = range(bsz)
        s_list = [[ss_ref[b, 0, h] for h in range(HEADS)] for b in seqs]
        _, vjp = jax.vjp(_gla_chunk, s_list, [q_ref[b] for b in seqs], [k_ref[b] for b in seqs],
                         [v_ref[b] for b in seqs], [sm_ref[b] for b in seqs], w2_ref[...], bg_ref[...])
        ds_in, dq, dk, dv, dsm, dw2, dbg = vjp(([do_ref[b] for b in seqs],
                                                [[ds_ref[b * HEADS + h] for h in range(HEADS)] for b in seqs]))
        for b in seqs:
            dq_ref[b], dk_ref[b], dv_ref[b] = dq[b].astype(MXU_DT), dk[b].astype(MXU_DT), dv[b].astype(MXU_DT)
            dsm_ref[b] = (dsm[b] + dsd_ref[b]).astype(MXU_DT)
            for h in range(HEADS):
                ds_ref[b * HEADS + h] = ds_in[b][h]
        dw2_ref[...] += dw2
        dbg_ref[...] += dbg

    col = lambda blk, width=512: _chunk_spec(bsz, width, rev, blk)
    w2_spec = pl.BlockSpec((128, 512), lambda c: (0, 0))
    bg_spec = pl.BlockSpec((1, 512), lambda c: (0, 0))
    out512 = jax.ShapeDtypeStruct((bsz, t_total, 512), MXU_DT)
    return pl.pallas_call(
        body, name="gla_rec_bwd", grid=(nc,),
        in_specs=[col(GQ_BLK), col(GK_BLK), col(GV_BLK), col(SM_BLK, 128), w2_spec, bg_spec,
                  _state_spec(bsz, rev), col(0), col(0, 128)],
        out_specs=(col(0), col(0), col(0), col(0, 128), w2_spec, bg_spec),
        out_shape=(out512, out512, out512, jax.ShapeDtypeStruct((bsz, t_total, 128), MXU_DT),
                   jax.ShapeDtypeStruct((128, 512), F32), jax.ShapeDtypeStruct((1, 512), F32)),
        scratch_shapes=[pltpu.VMEM((bsz * HEADS, HD, HD), F32)],
        compiler_params=_params(("arbitrary",)))(proj, proj, proj, proj, w2, bg, states, do, dsm_dn)


Z_BLK, GG_BLK = P_Z // 512, P_GG // 512


def _mix_out_fwd(o_dn, o_gla, proj, grow_dn, grow_gla):
    bsz, t_total, _ = o_dn.shape
    tt = _div_tile(t_total, 256)

    def body(od_ref, og_ref, z_ref, gg_ref, gd_ref, gl_ref, o_ref):
        o_ref[0, :, :512] = _gate_norm(od_ref[0], z_ref[0], gd_ref[...]).astype(MXU_DT)
        o_ref[0, :, 512:] = _gate_norm(og_ref[0], gg_ref[0], gl_ref[...]).astype(MXU_DT)

    def col(blk):
        return pl.BlockSpec((1, tt, 512), lambda b, t: (b, t, blk))

    return pl.pallas_call(
        body, name="mix_out_fwd", grid=(bsz, t_total // tt),
        in_specs=[col(0), col(0), col(Z_BLK), col(GG_BLK), _vec_spec(512), _vec_spec(512)],
        out_specs=_tok_spec(tt), out_shape=jax.ShapeDtypeStruct((bsz, t_total, D), MXU_DT),
        compiler_params=_params(("parallel", "parallel")))(o_dn, o_gla, proj, proj, grow_dn, grow_gla)


def _mix_out_bwd(do, o_dn, o_gla, proj, grow_dn, grow_gla):
    bsz, t_total, _ = o_dn.shape
    tt = _div_tile(t_total, 256)

    def body(do_ref, od_ref, og_ref, z_ref, gg_ref, gd_ref, gl_ref,
             dod_ref, dog_ref, dz_ref, dgg_ref, dgd_ref, dgl_ref):
        @pl.when((pl.program_id(0) == 0) & (pl.program_id(1) == 0))
        def _():
            dgd_ref[...] = jnp.zeros_like(dgd_ref)
            dgl_ref[...] = jnp.zeros_like(dgl_ref)

        def one(o_ref, gate_ref, g_ref, ct, do_out, dgate_out, dg_out):
            _, vjp = jax.vjp(_gate_norm, o_ref[0], gate_ref[0], g_ref[...])
            d_o, d_gate, d_row = vjp(ct)
            do_out[0] = d_o
            dgate_out[0] = d_gate.astype(MXU_DT)
            acc = d_row[:, :HD]
            for h in range(1, HEADS):
                acc = acc + d_row[:, h * HD:(h + 1) * HD]
            dg_out[...] += acc

        ct = do_ref[0].astype(F32)
        one(od_ref, z_ref, gd_ref, ct[:, :512], dod_ref, dz_ref, dgd_ref)
        one(og_ref, gg_ref, gl_ref, ct[:, 512:], dog_ref, dgg_ref, dgl_ref)

    def col(blk):
        return pl.BlockSpec((1, tt, 512), lambda b, t: (b, t, blk))

    f512 = jax.ShapeDtypeStruct((bsz, t_total, 512), F32)
    b512 = jax.ShapeDtypeStruct((bsz, t_total, 512), MXU_DT)
    g128 = jax.ShapeDtypeStruct((1, HD), F32)
    return pl.pallas_call(
        body, name="mix_out_bwd", grid=(bsz, t_total // tt),
        in_specs=[_tok_spec(tt), col(0), col(0), col(Z_BLK), col(GG_BLK), _vec_spec(512), _vec_spec(512)],
        out_specs=(col(0), col(0), col(0), col(0), _vec_spec(HD), _vec_spec(HD)),
        out_shape=(f512, f512, b512, b512, g128, g128),
        compiler_params=_params(("arbitrary", "arbitrary")))(do, o_dn, o_gla, proj, proj, grow_dn, grow_gla)


def _sum_slots(x, name):
    n, rows, cols = x.shape
    tr = _div_tile(rows, max(8, (1 << 19) // cols))

    def body(x_ref, o_ref):
        acc = x_ref[0].astype(F32)
        for i in range(1, n):
            acc = acc + x_ref[i].astype(F32)
        o_ref[...] = acc

    return pl.pallas_call(
        body, name=name, grid=(rows // tr,),
        in_specs=[pl.BlockSpec((n, tr, cols), lambda i: (0, i, 0))],
        out_specs=pl.BlockSpec((tr, cols), lambda i: (i, 0)),
        out_shape=jax.ShapeDtypeStruct((rows, cols), F32), compiler_params=_params(("parallel",)))(x)


def _adamw_math(w, g, m, v):
    nm = ADAM_B1 * m + (1.0 - ADAM_B1) * g
    nv = ADAM_B2 * v + (1.0 - ADAM_B2) * (g * g)
    m_hat = nm / (1.0 - ADAM_B1 ** ADAM_STEP)
    v_hat = nv / (1.0 - ADAM_B2 ** ADAM_STEP)
    return -ADAM_LR * (m_hat / (jnp.sqrt(v_hat) + ADAM_EPS) + ADAM_WD * w), nm, nv


def _adamw(w, g, m, v, name):
    _, rows, cols = w.shape
    tr = _div_tile(rows, max(8, (1 << 18) // cols))

    def body(w_ref, g_ref, m_ref, v_ref, d_ref, nm_ref, nv_ref):
        d_ref[...], nm_ref[...], nv_ref[...] = _adamw_math(w_ref[...], g_ref[...], m_ref[...], v_ref[...])

    spec = pl.BlockSpec((1, tr, cols), lambda i: (0, i, 0))
    shp = jax.ShapeDtypeStruct(w.shape, F32)
    return pl.pallas_call(body, name=name, grid=(rows // tr,), in_specs=[spec] * 4, out_specs=(spec,) * 3,
                          out_shape=(shp,) * 3, compiler_params=_params(("parallel",)))(w, g, m, v)


def _adamw_many(ws, gs, ms, vs, name):
    n = len(ws)

    def body(*refs):
        for i in range(n):
            d, nm, nv = _adamw_math(refs[i][...], refs[n + i][...], refs[2 * n + i][...], refs[3 * n + i][...])
            refs[4 * n + i][...] = d
            refs[5 * n + i][...] = nm
            refs[6 * n + i][...] = nv

    shapes = tuple(jax.ShapeDtypeStruct(w.shape, F32) for w in ws)
    outs = pl.pallas_call(body, name=name, out_shape=shapes * 3, compiler_params=_params())(*ws, *gs, *ms, *vs)
    return outs[:n], outs[n:2 * n], outs[2 * n:]


def _position():
    return lax.axis_index("x"), lax.axis_index("y"), lax.axis_index("c")


def _slot(px, py, pc):
    return 4 * px + 2 * py + pc


def _gather_small(x, name):
    rows, cols = x.shape

    def body(x_ref, o_ref, send_sems, recv_sems):
        mx, my, mc = _position()

        def peer(k):
            return (mx ^ ((k >> 2) & 1), my ^ ((k >> 1) & 1), mc ^ (k & 1))

        o_ref[_slot(mx, my, mc)] = x_ref[...]
        sends = []
        for k in range(1, N_DEV):
            cp = pltpu.make_async_remote_copy(src_ref=x_ref, dst_ref=o_ref.at[_slot(mx, my, mc)],
                                              send_sem=send_sems.at[k - 1], recv_sem=recv_sems.at[k - 1],
                                              device_id=peer(k), device_id_type=MESH)
            cp.start()
            sends.append(cp)
        for k in range(1, N_DEV):
            pltpu.make_async_remote_copy(src_ref=x_ref, dst_ref=o_ref.at[_slot(*peer(k))],
                                         send_sem=send_sems.at[k - 1], recv_sem=recv_sems.at[k - 1],
                                         device_id=peer(k), device_id_type=MESH).wait_recv()
        for cp in sends:
            cp.wait_send()

    return pl.pallas_call(
        body, name=name, out_shape=jax.ShapeDtypeStruct((N_DEV, rows, cols), x.dtype),
        in_specs=[pl.BlockSpec(memory_space=pltpu.VMEM)], out_specs=pl.BlockSpec(memory_space=pltpu.VMEM),
        scratch_shapes=[pltpu.SemaphoreType.DMA((N_DEV - 1,)), pltpu.SemaphoreType.DMA((N_DEV - 1,))],
        compiler_params=pltpu.CompilerParams(vmem_limit_bytes=VMEM_LIMIT_V7X))(x)


def _gather_big(shards):
    n = len(shards)

    def body(*refs):
        xs, outs = refs[:n], refs[n:2 * n]
        send_sems, recv_sems, local_sems = refs[2 * n:]
        mx, my, mc = _position()
        me, sibling = (mx, my, mc), (mx, my, 1 - mc)
        chips = [(1 - mx, my), (mx, 1 - my), (1 - mx, 1 - my)]

        def copy(a, k, block, to, src=None):
            dst = outs[a].at[_slot(*block)]
            return pltpu.make_async_remote_copy(src_ref=dst if src is None else src, dst_ref=dst,
                                                send_sem=send_sems.at[7 * a + k], recv_sem=recv_sems.at[7 * a + k],
                                                device_id=to, device_id_type=MESH)

        mine = [pltpu.make_async_copy(xs[a], outs[a].at[_slot(*me)], local_sems.at[a]) for a in range(n)]
        for cp in mine:
            cp.start()
        started = []
        for a in range(n):
            started.append(copy(a, 0, me, sibling, src=xs[a]))
            started += [copy(a, 1 + j, me, (*chip, mc), src=xs[a]) for j, chip in enumerate(chips)]
        for cp in started:
            cp.start()
        for j, chip in enumerate(chips):
            for a in range(n):
                copy(a, 1 + j, (*chip, mc), me).wait_recv()
                fwd = copy(a, 4 + j, (*chip, mc), sibling)
                fwd.start()
                started.append(fwd)
        for a in range(n):
            copy(a, 0, sibling, me).wait_recv()
            for j, chip in enumerate(chips):
                copy(a, 4 + j, (*chip, 1 - mc), me).wait_recv()
        for cp in started:
            cp.wait_send()
        for cp in mine:
            cp.wait()

    any_spec = pl.BlockSpec(memory_space=pl.ANY)
    return pl.pallas_call(
        body, name="gather_weights",
        out_shape=tuple(jax.ShapeDtypeStruct((N_DEV,) + s.shape, s.dtype) for s in shards),
        in_specs=[any_spec] * n, out_specs=(any_spec,) * n,
        scratch_shapes=[pltpu.SemaphoreType.DMA((7 * n,)), pltpu.SemaphoreType.DMA((7 * n,)),
                        pltpu.SemaphoreType.DMA((n,))])(*shards)


def _peer(pos, k):
    mx, my, mc = pos
    return (mx ^ ((k >> 2) & 1), my ^ ((k >> 1) & 1), mc ^ (k & 1))


def _exchange_copies(srcs, lands, send_sems, recv_sems, by_owner):
    pos = _position()
    me = _slot(*pos)
    out = []
    for a, (src, land) in enumerate(zip(srcs, lands)):
        for k in range(1, N_DEV):
            peer = _peer(pos, k)
            sems = dict(send_sem=send_sems.at[7 * a + k - 1], recv_sem=recv_sems.at[7 * a + k - 1],
                        device_id=peer, device_id_type=MESH)
            mine = src.at[_slot(*peer)] if by_owner else src
            send = pltpu.make_async_remote_copy(src_ref=mine, dst_ref=land.at[me], **sems)
            recv = pltpu.make_async_remote_copy(src_ref=mine, dst_ref=land.at[_slot(*peer)], **sems)
            out.append((send, recv))
    return out


_HBM_SPEC = pl.BlockSpec(memory_space=pltpu.HBM)
_SEM_SPEC = pl.BlockSpec(memory_space=pltpu.SEMAPHORE)
_DATAFLOW = pltpu.SideEffectType.DATAFLOW_SIDE_EFFECTING


def _exchange_start(name, srcs, slab_shapes, after, by_owner, carry=()):
    n, na, nc = len(srcs), len(after), len(carry)
    lands = [pltpu.with_memory_space_constraint(lax.empty((N_DEV,) + s, x.dtype), pltpu.HBM)
             for s, x in zip(slab_shapes, srcs)]
    thru = [pltpu.with_memory_space_constraint(x, pltpu.HBM) for x in [*srcs, *lands, *carry]]

    def body(*refs):
        src_refs, land_refs = refs[:n], refs[n:2 * n]
        send_sems, recv_sems = refs[len(thru) + na], refs[len(thru) + na + 1]
        token = refs[-1]
        for send, _ in _exchange_copies(src_refs, land_refs, send_sems, recv_sems, by_owner):
            send.start()
        token[...] = jnp.zeros_like(token)

    outs = pl.pallas_call(
        body, name=name,
        out_shape=(pltpu.SemaphoreType.DMA((7 * n,)), pltpu.SemaphoreType.DMA((7 * n,)),
                   *[pltpu.HBM(x.shape, x.dtype) for x in thru], jax.ShapeDtypeStruct((8, 128), F32)),
        in_specs=[_HBM_SPEC] * len(thru) + [pl.BlockSpec(memory_space=pl.ANY)] * na,
        out_specs=(_SEM_SPEC, _SEM_SPEC, *[_HBM_SPEC] * len(thru), pl.BlockSpec(memory_space=pltpu.VMEM)),
        input_output_aliases={i: 2 + i for i in range(len(thru))},
        compiler_params=pltpu.CompilerParams(has_side_effects=_DATAFLOW))(*thru, *after)
    return (outs[0], outs[1], list(outs[2:2 + n]), list(outs[2 + n:2 + 2 * n]), outs[-1],
            list(outs[2 + 2 * n:2 + 2 * n + nc]))


def _exchange_wait(name, send_sems, recv_sems, srcs, lands, after, by_owner):
    n = len(srcs)

    def body(*refs):
        src_refs, land_refs = refs[:n], refs[n:2 * n]
        s_sems, r_sems = refs[2 * n], refs[2 * n + 1]
        for send, recv in _exchange_copies(src_refs, land_refs, s_sems, r_sems, by_owner):
            send.wait_send()
            recv.wait_recv()

    outs = pl.pallas_call(
        body, name=name,
        out_shape=(*[pltpu.HBM(x.shape, x.dtype) for x in srcs], *[pltpu.HBM(l.shape, l.dtype) for l in lands]),
        in_specs=[_HBM_SPEC] * (2 * n) + [_SEM_SPEC, _SEM_SPEC, pl.BlockSpec(memory_space=pl.ANY)],
        out_specs=tuple([_HBM_SPEC] * (2 * n)),
        input_output_aliases={i: i for i in range(2 * n)},
        compiler_params=pltpu.CompilerParams(has_side_effects=_DATAFLOW))(*srcs, *lands, send_sems, recv_sems, after)
    return list(outs[:n]), list(outs[n:])


def _pad_heads(x, axis):
    shp = list(x.shape)
    x4 = x.reshape(shp[:axis] + [HEADS, GLA_KEY] + shp[axis + 1:])
    pad = [(0, 0)] * x4.ndim
    pad[axis + 1] = (0, HD - GLA_KEY)
    return jnp.pad(x4, pad).reshape(shp[:axis] + [HEADS * HD] + shp[axis + 1:])


def _unpad_heads(x, axis):
    shp = list(x.shape)
    x4 = x.reshape(shp[:axis] + [HEADS, HD] + shp[axis + 1:])
    x4 = lax.slice_in_dim(x4, 0, GLA_KEY, axis=axis + 1)
    return x4.reshape(shp[:axis] + [HEADS * GLA_KEY] + shp[axis + 1:])


O_Z_END, O_AB, O_GQ, O_GK, O_GV, O_R = 2048, 2048, 2056, 2312, 2568, 3592


def _pad_in_rows(wt):
    return jnp.concatenate([
        wt[:O_Z_END], _pad_heads(wt[O_GQ:O_GK], 0), _pad_heads(wt[O_GK:O_GV], 0), wt[O_GV:O_R],
        wt[O_AB:O_GQ], wt[O_R:], jnp.zeros((P_W - P_SM - 8 - GATE_RANK, wt.shape[1]), wt.dtype)], axis=0)


def _unpad_in_rows(gt):
    return jnp.concatenate([
        gt[:P_GQ], gt[P_SM:P_SM + 8], _unpad_heads(gt[P_GQ:P_GK], 0), _unpad_heads(gt[P_GK:P_GV], 0),
        gt[P_GV:P_SM], gt[P_SM + 8:P_SM + 8 + GATE_RANK]], axis=0)


def _lane_row(vals, width=128):
    return jnp.pad(vals.reshape(1, -1), ((0, 0), (0, width - vals.size)))


SMALL_NAMES = ["ln0_g", "ln0_b", "b_ada", "dn_conv", "dn_a_log", "dn_dt_bias", "dn_norm_g", "gla_w_gate2",
               "gla_b_gate", "gla_norm_g", "ln1_g", "ln1_b", "ffn_conv", "ffn_conv_b", "ln2_g", "ln2_b"]
WEIGHTS = ["ln0_g", "ln0_b", "w_ada", "b_ada", "w_in", "dn_conv", "dn_a_log", "dn_dt_bias", "dn_norm_g",
           "gla_w_gate2", "gla_b_gate", "gla_norm_g", "w_o", "ln1_g", "ln1_b", "ffn_w_up", "ffn_conv", "ffn_conv_b",
           "ffn_w_down", "ln2_g", "ln2_b"]


def kernel(x, c, ln0_g, ln0_b, w_ada, b_ada, w_in, dn_conv, dn_a_log, dn_dt_bias, dn_norm_g, gla_w_gate2, gla_b_gate, gla_norm_g, w_o, ln1_g, ln1_b, ffn_w_up, ffn_conv, ffn_conv_b, ffn_w_down, ln2_g, ln2_b, loss_target, m_ln0_g, m_ln0_b, m_w_ada, m_b_ada, m_w_in, m_dn_conv, m_dn_a_log, m_dn_dt_bias, m_dn_norm_g, m_gla_w_gate2, m_gla_b_gate, m_gla_norm_g, m_w_o, m_ln1_g, m_ln1_b, m_ffn_w_up, m_ffn_conv, m_ffn_conv_b, m_ffn_w_down, m_ln2_g, m_ln2_b, v_ln0_g, v_ln0_b, v_w_ada, v_b_ada, v_w_in, v_dn_conv, v_dn_a_log, v_dn_dt_bias, v_dn_norm_g, v_gla_w_gate2, v_gla_b_gate, v_gla_norm_g, v_w_o, v_ln1_g, v_ln1_b, v_ffn_w_up, v_ffn_conv, v_ffn_conv_b, v_ffn_w_down, v_ln2_g, v_ln2_b):
    args = dict(locals())
    w_given = {n: args[n] for n in WEIGHTS}
    m_given = {n: args["m_" + n] for n in WEIGHTS}
    v_given = {n: args["v_" + n] for n in WEIGHTS}
    bsz, t_total, _ = x.shape
    ntok = bsz * t_total
    mx, my, mc = _position()
    me = _slot(mx, my, mc)

    pack1 = jnp.concatenate([c.reshape(-1), dn_conv.reshape(-1), gla_w_gate2.reshape(-1), ffn_conv.reshape(-1)])
    n1 = pack1.size
    rows1 = -(-n1 // 1024) * 8
    pack1 = jnp.pad(pack1, (0, rows1 * 128 - n1)).reshape(rows1, 128)
    got1 = _gather_small(pack1, "gather_cond").reshape(N_DEV, -1)
    o1 = bsz * D
    o2 = o1 + dn_conv.size
    o3 = o2 + gla_w_gate2.size
    c_all = got1[:, :o1].reshape(N_DEV * bsz, D)
    dn_conv_f = got1[:, o1:o2].reshape(N_DEV, DN_CONV_K, -1).transpose(1, 0, 2).reshape(DN_CONV_K, QKV_W)
    gate2_f = got1[:, o2:o3].reshape(N_DEV, GATE_RANK, -1).transpose(1, 0, 2).reshape(GATE_RANK, HEADS * GLA_KEY)
    ffn_conv_f = got1[:, o3:n1].reshape(N_DEV, FFN_CONV_K, -1).transpose(1, 0, 2).reshape(FFN_CONV_K, 2 * D_FF)

    win_t = w_in[0].T.astype(MXU_DT)
    wup_t = ffn_w_up[0].T.astype(MXU_DT)
    (win_all,) = _gather_big([win_t])
    win_p = _pad_in_rows(win_all.reshape(IN_W, D))
    cw_p, cb_p = _ffn_pair(ffn_conv_f, 1), _ffn_pair(ffn_conv_b, 1)

    ncol = w_ada.shape[2]
    b_cols = lax.dynamic_slice_in_dim(b_ada, me * ncol, ncol, axis=1)
    mod_part = _ada_fwd(c_all, w_ada[0], b_cols)
    mod_all = _gather_small(mod_part.reshape(-1, 128), "gather_mod").reshape(N_DEV, N_DEV * bsz, ncol)
    mod = lax.dynamic_slice_in_dim(mod_all, me * bsz, bsz, axis=1).transpose(1, 0, 2).reshape(bsz, 6, 1, D)
    late = [w_o[0].astype(MXU_DT), wup_t, ffn_w_down[0].astype(MXU_DT)]
    ag_send, ag_recv, ag_src, ag_land, ag_token, _ = _exchange_start(
        "gather_start", late, [w.shape for w in late], [win_all, mod_all], by_owner=False)
    mod = mod + ag_token[0, 0]
    sh_a, sc_a, gt_a, sh_f, sc_f, gt_f = (mod[:, i] for i in range(6))

    g0, b0 = ln0_g.reshape(1, D), ln0_b.reshape(1, D)
    alog_row, dt_row = _lane_row(dn_a_log[0]), _lane_row(dn_dt_bias[0])
    grow_dn, grow_gla = jnp.tile(dn_norm_g, (1, HEADS)), jnp.tile(gla_norm_g, (1, HEADS))
    w2 = jnp.zeros((128, HEADS * HD), F32).at[SM_R:SM_R + GATE_RANK].set(_pad_heads(gate2_f, 1))
    bg = _pad_heads(gla_b_gate, 1)

    h_a = _ln0_mod(x, g0, b0, sc_a, sh_a)
    proj = _mm(h_a.reshape(ntok, D), win_p, "nt", F32, "mm_proj", tm=1024, tn=1408).reshape(bsz, t_total, P_W)
    q, k, v, gates = _dn_pre_fwd(proj, dn_conv_f, alog_row, dt_row)
    o_dn, s_dn, inv_dn = _dn_rec_fwd(q, k, v, gates)
    o_gla, s_gla = _gla_rec_fwd(proj, w2, bg)
    o_mix = _mix_out_fwd(o_dn, o_gla, proj, grow_dn, grow_gla)
    late, landed = _exchange_wait("gather_wait", ag_send, ag_recv, ag_src, ag_land, o_mix, by_owner=False)
    wo_all, wup_all, wdn_all = (lax.dynamic_update_slice(l, w[None], (me, 0, 0)) for l, w in zip(landed, late))
    wo_f = wo_all.reshape(D, D)
    wup_f = _ffn_pair(wup_all.reshape(2 * D_FF, D), 0)
    wdn_f = wdn_all.reshape(D_FF, D)
    y = _mm(o_mix.reshape(ntok, D), wo_f, "nn", MXU_DT, "mm_wo", tm=1024, tn=1024).reshape(bsz, t_total, D)
    r1, h_f = _res_ln_mod(x, y, gt_a, g0, b0, ln1_g, ln1_b, sc_f, sh_f)
    up, act = _ffn_up_act(h_f, wup_f, cw_p, cb_p)
    y2 = _mm(act.reshape(ntok, D_FF), wdn_f, "nn", MXU_DT, "mm_down", tm=1024, tn=1024).reshape(bsz, t_total, D)
    loss_rows, dr2, dy2, dgt_f, d_ln2_g, d_ln2_b = _final_fwd_bwd(r1, y2, gt_f, ln1_g, ln1_b, ln2_g, ln2_b, loss_target)
    loss_part = (0.5 / D) * jnp.sum(loss_rows)

    dy2_2 = dy2.reshape(ntok, D)
    g_wdn = _mm(act.reshape(ntok, D_FF), dy2_2, "tn", MXU_DT, "mm_gwdn", tm=1408, tn=1024)
    dup, d_cw_p, d_cb_p = _ffn_act_bwd(up, dy2, wdn_f, cw_p, cb_p)
    d_ffn_conv, d_ffn_conv_b = _ffn_unpair(d_cw_p, 1), _ffn_unpair(d_cb_p, 1)
    dup_2 = dup.reshape(ntok, 2 * D_FF)
    dh_f = _mm(dup_2, wup_f, "nn", MXU_DT, "mm_dhf", tn=1024).reshape(bsz, t_total, D)
    g_wup_t = _mm(dup_2, h_f.reshape(ntok, D), "tn", MXU_DT, "mm_gwup", tm=1408, tn=1024)
    ffn_parts = [_ffn_unpair(g_wup_t, 0).reshape(N_DEV, -1, D), g_wdn.reshape(N_DEV, -1, D)]
    rs_send, rs_recv, rs_src, rs_land, rs_token, _ = _exchange_start(
        "scatter_start", ffn_parts, [p.shape[1:] for p in ffn_parts], [dh_f], by_owner=True)
    dr1, dsc_f, dsh_f, d_ln1_g, d_ln1_b, dy, dgt_a = _ln_bwd_call(
        "ln1_bwd", dr2, dh_f, r1, ln1_g, ln1_b, sc_f + rs_token[0, 0], y=y, gt=gt_a)

    dy_2 = dy.reshape(ntok, D)
    do = _mm(dy_2, wo_f, "nt", MXU_DT, "mm_do", tm=1024, tn=1024).reshape(bsz, t_total, D)
    g_wo = _mm(o_mix.reshape(ntok, D), dy_2, "tn", MXU_DT, "mm_gwo", tm=512, tn=1024)
    do_dn, do_gla, dz, dgg, d_dn_norm, d_gla_norm = _mix_out_bwd(do, o_dn, o_gla, proj, grow_dn, grow_gla)
    dq, dk, dv, dgates = _dn_rec_bwd(q, k, v, gates, s_dn, inv_dn, do_dn)
    dqkv, dsm_dn, d_dn_conv, d_alog_row, d_dt_row = _dn_pre_bwd(proj, dq, dk, dv, dgates, dn_conv_f, alog_row, dt_row)
    dgq, dgk, dgv, dsm, d_w2, d_bg = _gla_rec_bwd(proj, w2, bg, s_gla, do_gla, dsm_dn)
    dproj = jnp.concatenate([dqkv, dz, dgq, dgk, dgv, dgg, dsm], axis=-1).reshape(ntok, P_W)
    g_win_p = _mm(dproj, h_a.reshape(ntok, D), "tn", MXU_DT, "mm_gwin", tm=1408, tn=1024)
    mix_parts = [_unpad_in_rows(g_win_p).reshape(N_DEV, -1, D), g_wo.reshape(N_DEV, -1, D)]
    rs2_send, rs2_recv, rs2_src, rs2_land, rs2_token, (win_p_late,) = _exchange_start(
        "scatter_mix_start", mix_parts, [p.shape[1:] for p in mix_parts], [], by_owner=True, carry=[win_p])
    dh_a = _mm(dproj, win_p_late, "nn", MXU_DT, "mm_dha", tn=1024).reshape(bsz, t_total, D)
    grad_x, dsc_a, dsh_a, d_ln0_g, d_ln0_b = _ln_bwd_call(
        "ln0_bwd", dr1, dh_a, x, g0, b0, sc_a + rs2_token[0, 0])

    def owner_sum(landed, parts, tag):
        full = [lax.dynamic_update_slice(l, lax.dynamic_slice_in_dim(p, me, 1, axis=0), (me, 0, 0))
                for l, p in zip(landed, parts)]
        return [_sum_slots(f, f"sum_{tag}_{i}") for i, f in enumerate(full)]

    delta, new_m, new_v = {}, {}, {}
    flip = lambda a: jnp.swapaxes(a, 1, 2)

    def update(n, g):
        if g.shape == w_given[n].shape[1:]:
            delta[n], new_m[n], new_v[n] = _adamw(w_given[n], g[None], m_given[n], v_given[n], "adamw_" + n)
        else:
            upd = _adamw(flip(w_given[n]), g[None], flip(m_given[n]), flip(v_given[n]), "adamw_" + n)
            delta[n], new_m[n], new_v[n] = (flip(a) for a in upd)

    ffn_parts, ffn_landed = _exchange_wait("scatter_wait", rs_send, rs_recv, rs_src, rs_land, grad_x, by_owner=True)
    g_wup_ts, g_wdn_s = owner_sum(ffn_landed, ffn_parts, "ffn")
    update("ffn_w_up", g_wup_ts)
    update("ffn_w_down", g_wdn_s)
    ffn_done = 0.0 * (new_v["ffn_w_up"][0, 0, 0] + new_v["ffn_w_down"][0, 0, 0])

    dmod = jnp.concatenate([dsh_a, dsc_a, dgt_a, dsh_f, dsc_f, dgt_f], axis=1).reshape(-1)
    small_parts = {
        "ln0_g": d_ln0_g, "ln0_b": d_ln0_b, "ln1_g": d_ln1_g, "ln1_b": d_ln1_b, "ln2_g": d_ln2_g, "ln2_b": d_ln2_b,
        "dn_a_log": d_alog_row[:, :HEADS], "dn_dt_bias": d_dt_row[:, :HEADS],
        "dn_norm_g": d_dn_norm, "gla_norm_g": d_gla_norm, "gla_b_gate": _unpad_heads(d_bg, 1),
        "ffn_conv_b": d_ffn_conv_b, "dn_conv": d_dn_conv,
        "gla_w_gate2": _unpad_heads(d_w2[SM_R:SM_R + GATE_RANK], 1), "ffn_conv": d_ffn_conv}
    order = sorted(small_parts)
    flat = jnp.concatenate([small_parts[n].reshape(-1) for n in order] + [(loss_part + ffn_done).reshape(1), dmod])
    n3 = flat.size
    rows3 = -(-n3 // 1024) * 8
    pack3 = jnp.pad(flat, (0, rows3 * 128 - n3)).reshape(rows3, 128)
    got3 = _gather_small(pack3, "gather_small_grads")
    tot3 = _sum_slots(got3, "sum_small_grads").reshape(-1)
    grads = {}
    off = 0
    for n in order:
        size = small_parts[n].size
        grads[n] = tot3[off:off + size]
        off += size
    loss = tot3[off]
    off += 1
    dmod_all = got3.reshape(N_DEV, -1)[:, off:off + dmod.size].reshape(N_DEV * bsz, 6 * D)
    dmod_cols = lax.dynamic_slice_in_dim(dmod_all, me * ncol, ncol, axis=1)
    g_wada, g_bada = _ada_bwd(c_all, dmod_all, dmod_cols)
    grads["b_ada"] = g_bada

    def col_shard(full, rows):
        part = full.reshape(rows, -1)
        width = part.shape[1] // N_DEV
        return lax.dynamic_slice_in_dim(part, me * width, width, axis=1)

    grads["dn_conv"] = col_shard(grads["dn_conv"], DN_CONV_K)
    grads["gla_w_gate2"] = col_shard(grads["gla_w_gate2"], GATE_RANK)
    grads["ffn_conv"] = col_shard(grads["ffn_conv"], FFN_CONV_K)
    grads = {n: g.reshape(w_given[n].shape) for n, g in grads.items()}
    mix_parts, mix_landed = _exchange_wait("scatter_mix_wait", rs2_send, rs2_recv, rs2_src, rs2_land, grad_x,
                                           by_owner=True)
    g_win_t, g_wo_s = owner_sum(mix_landed, mix_parts, "mix")
    grads["w_ada"] = g_wada.reshape(w_ada.shape)
    grads["w_in"] = g_win_t.T.reshape(w_in.shape)
    grads["w_o"] = g_wo_s.reshape(w_o.shape)
    grads["ffn_w_up"] = g_wup_ts.T.reshape(ffn_w_up.shape)
    grads["ffn_w_down"] = g_wdn_s.reshape(ffn_w_down.shape)

    update("w_ada", g_wada)
    update("w_o", g_wo_s)
    update("w_in", g_win_t)
    d_s, m_s, v_s = _adamw_many(*[[src[n] for n in SMALL_NAMES] for src in (w_given, grads, m_given, v_given)],
                                "adamw_small")
    for i, n in enumerate(SMALL_NAMES):
        delta[n], new_m[n], new_v[n] = d_s[i], m_s[i], v_s[i]

    return (loss, grad_x, *[grads[n] for n in WEIGHTS], *[delta[n] for n in WEIGHTS],
            *[new_m[n] for n in WEIGHTS], *[new_v[n] for n in WEIGHTS])
```

```python
import functools

import jax
import jax.numpy as jnp
from jax import lax
from jax.experimental import pallas as pl
from jax.experimental.pallas import tpu as pltpu

F32 = jnp.float32
MXU_DT = jnp.bfloat16
MESH = pl.DeviceIdType.MESH
N_DEV = 8

D = 1024
HEADS = 4
HD = 128
CHUNK = 64
GLA_KEY = 64
GLA_TAU = 16.0
GATE_RANK = 16
D_FF = 2816
IN_W = 3608
ALPHA = 2.0 ** 0.25
EPS = 1e-6
DN_CONV_K = 4
FFN_CONV_K = 3
HALO = 8

P_QKV, P_Z, P_GQ, P_GK, P_GV, P_GG, P_SM, P_W = 0, 1536, 2048, 2560, 3072, 3584, 4096, 4224
SM_A, SM_B, SM_R = 0, 4, 8

ADAM_LR, ADAM_B1, ADAM_B2, ADAM_EPS, ADAM_WD, ADAM_STEP = 0.001, 0.9, 0.999, 1e-08, 0.01, 10

VMEM_LIMIT_V7X = 56 * 1024 * 1024


def _params(sem=None):
    return pltpu.CompilerParams(dimension_semantics=sem, vmem_limit_bytes=VMEM_LIMIT_V7X)


def _dg(a, b, dims, prec=None):
    return lax.dot_general(a, b, (dims, ((), ())), precision=prec, preferred_element_type=F32)


def _dot(a, b, prec=None):
    return _dg(a, b, ((1,), (0,)), prec)


def _dot_nt(a, b, prec=None):
    return _dg(a, b, ((1,), (1,)), prec)


def _dot_tn(a, b, prec=None):
    return _dg(a, b, ((0,), (0,)), prec)


def _iota(shape, dim):
    return lax.broadcasted_iota(jnp.int32, shape, dim)


def _sigmoid(x):
    return jax.nn.sigmoid(x)


def _silu(x):
    return x * _sigmoid(x)


def _softplus(x):
    return jnp.maximum(x, 0.0) + jnp.log(1.0 + jnp.exp(-jnp.abs(x)))


def _ln_stats(x):
    mu = jnp.mean(x, axis=-1, keepdims=True)
    xc = x - mu
    rstd = lax.rsqrt(jnp.mean(xc * xc, axis=-1, keepdims=True) + EPS)
    return xc * rstd, rstd


def _ln_bwd(dxhat, xhat, rstd):
    return rstd * (dxhat - jnp.mean(dxhat, axis=-1, keepdims=True)
                   - xhat * jnp.mean(dxhat * xhat, axis=-1, keepdims=True))


NN, NT, TN = ((1,), (0,)), ((1,), (1,)), ((0,), (0,))


def _split2(a):
    hi = a.astype(jnp.bfloat16)
    return hi, (a - hi.astype(F32)).astype(jnp.bfloat16)


def _d3(a, b, dims):
    ah, al = _split2(a)
    bh, bl = _split2(b)
    return _dg(ah, bh, dims) + (_dg(ah, bl, dims) + _dg(al, bh, dims))


@jax.custom_vjp
def _dot3(a, b):
    return _d3(a, b, NN)


_dot3.defvjp(lambda a, b: (_d3(a, b, NN), (a, b)),
             lambda res, g: (_d3(g, res[1], NT), _d3(res[0], g, TN)))


def _split3(b):
    b1 = b.astype(jnp.bfloat16)
    r1 = b - b1.astype(F32)
    b2 = r1.astype(jnp.bfloat16)
    return b1, b2, (r1 - b2.astype(F32)).astype(jnp.bfloat16)


def _sum3(fn, b):
    b1, b2, b3 = _split3(b)
    return fn(b1) + (fn(b2) + fn(b3))


@jax.custom_vjp
def _mask_dot(e, b):
    return _sum3(lambda t: _dg(e, t, NN), b)


_mask_dot.defvjp(lambda e, b: (_mask_dot(e, b), e),
                 lambda e, g: (jnp.zeros_like(e), _sum3(lambda t: _dg(e, t, TN), g)))


@jax.custom_vjp
def _mask_dot_nt(e, b):
    return _sum3(lambda t: _dg(e, t, NT), b)


_mask_dot_nt.defvjp(lambda e, b: (_mask_dot_nt(e, b), e),
                    lambda e, g: (jnp.zeros_like(e), _sum3(lambda t: _dg(t, e, TN), g)))


def _tri_inv_impl(ms):
    n = ms[0].shape[0]
    r, c = _iota((n, n), 0), _iota((n, n), 1)
    eye = (r == c).astype(F32)
    diag = (r >> 3) == (c >> 3)
    ds = [jnp.where(diag, m, 0.0) for m in ms]
    d2s = [_d3(d, d, NN) for d in ds]
    d4s = [_d3(d2, d2, NN) for d2 in d2s]
    invs = [_d3(eye - d, eye + d2, NN) for d, d2 in zip(ds, d2s)]
    invs = [_d3(inv, eye + d4, NN) for inv, d4 in zip(invs, d4s)]
    shift = 3
    while (1 << shift) < n:
        rb, cb = r >> shift, c >> shift
        sel = ((rb & 1) == 1) & (cb == rb - 1)
        tmp = [_d3(inv, jnp.where(sel, m, 0.0), NN) for inv, m in zip(invs, ms)]
        invs = [inv - _d3(t, inv, NN) for t, inv in zip(tmp, invs)]
        shift += 1
    return invs


@jax.custom_vjp
def _tri_inv(ms):
    return _tri_inv_impl(ms)


def _tri_inv_fwd(ms):
    invs = _tri_inv_impl(ms)
    return invs, invs


def _tri_inv_bwd(invs, das):
    tmp = [_d3(a, da, TN) for a, da in zip(invs, das)]
    return ([-_d3(t, a, NT) for t, a in zip(tmp, invs)],)


_tri_inv.defvjp(_tri_inv_fwd, _tri_inv_bwd)


@jax.custom_vjp
def _tri_inv_known(ms, invs):
    return invs


_tri_inv_known.defvjp(lambda ms, invs: (invs, invs),
                      lambda invs, das: (_tri_inv_bwd(invs, das)[0], [jnp.zeros_like(a) for a in invs]))


def _dn_chunk(s_list, q, k, v, gates, inv_known=None, with_inv=False):
    nb = len(q)
    c = q[0].shape[0]
    r64, c64 = _iota((c, c), 0), _iota((c, c), 1)
    causal = r64 >= c64
    strict = r64 > c64
    tri = causal.astype(jnp.bfloat16)
    eye = (_iota((HD, HD), 0) == _iota((HD, HD), 1)).astype(jnp.bfloat16)
    lane = _iota(gates[0].shape, 1)
    lane1 = _iota((1, HD), 1)
    g_all = [_mask_dot(tri, g) for g in gates]
    g_all_t = [_mask_dot_nt(eye, g) for g in g_all]
    row = _iota(g_all_t[0].shape, 0)
    last = [jnp.sum(g, axis=0, keepdims=True) for g in gates]
    prob = [(b, h) for b in range(nb) for h in range(HEADS)]
    sl = [slice(h * HD, (h + 1) * HD) for h in range(HEADS)]
    qh = [q[b][:, sl[h]] for b, h in prob]
    kh = [k[b][:, sl[h]] for b, h in prob]
    vh = [v[b][:, sl[h]] for b, h in prob]
    s = [s_list[b][h] for b, h in prob]
    beta = [jnp.sum(jnp.where(lane == SM_B + h, gates[b], 0.0), axis=-1, keepdims=True) for b, h in prob]
    g_c = [jnp.sum(jnp.where(lane == SM_A + h, g_all[b], 0.0), axis=-1, keepdims=True) for b, h in prob]
    g_r = [jnp.sum(jnp.where(row == SM_A + h, g_all_t[b], 0.0), axis=0, keepdims=True) for b, h in prob]
    g_last = [jnp.sum(jnp.where(lane1 == SM_A + h, last[b], 0.0), axis=-1, keepdims=True) for b, h in prob]
    decay = [jnp.where(causal, jnp.exp(jnp.where(causal, gc - gr, 0.0)), 0.0) for gc, gr in zip(g_c, g_r)]
    kb = [k_ * b_ for k_, b_ in zip(kh, beta)]
    m_low = [jnp.where(strict, _dot_nt(kb_, k_) * d_, 0.0) for kb_, k_, d_ in zip(kb, kh, decay)]
    attn = [_dot_nt(q_, k_) * d_ for q_, k_, d_ in zip(qh, kh, decay)]
    a_inv = _tri_inv(m_low) if inv_known is None else _tri_inv_known(m_low, inv_known)
    eg = [jnp.exp(gc) for gc in g_c]
    uw = [_dot3(a_, jnp.concatenate([v_ * b_, kb_ * e_], axis=1))
          for a_, v_, b_, kb_, e_ in zip(a_inv, vh, beta, kb, eg)]
    v_new = [uw_[:, :HD] - _dot(uw_[:, HD:], s_) for uw_, s_ in zip(uw, s)]
    qs = [_dot(q_ * e_, s_) for q_, e_, s_ in zip(qh, eg, s)]
    o = [qs_ + _dot(a_, vn_) for qs_, a_, vn_ in zip(qs, attn, v_new)]
    k_dec = [k_ * jnp.exp(gl - gc) for k_, gl, gc in zip(kh, g_last, g_c)]
    s_new = [s_ * jnp.exp(gl) + _dot_tn(kd_, vn_) for s_, gl, kd_, vn_ in zip(s, g_last, k_dec, v_new)]
    outs = [jnp.concatenate(o[b * HEADS:(b + 1) * HEADS], axis=-1) for b in range(nb)]
    states = [s_new[b * HEADS:(b + 1) * HEADS] for b in range(nb)]
    return (outs, states, a_inv) if with_inv else (outs, states)


def _gla_chunk(st_list, q, k, v, small, w2, bg):
    nb = len(q)
    c = q[0].shape[0]
    causal = _iota((c, c), 0) >= _iota((c, c), 1)
    tri = causal.astype(jnp.bfloat16)
    la_all = [-_softplus(-(_dot(sm, w2) + bg)) * (1.0 / GLA_TAU) for sm in small]
    b_all = [_mask_dot(tri, la) for la in la_all]
    prob = [(b, h) for b in range(nb) for h in range(HEADS)]
    sl = [slice(h * HD, (h + 1) * HD) for h in range(HEADS)]
    kh = [k[b][:, sl[h]] for b, h in prob]
    vh = [v[b][:, sl[h]] for b, h in prob]
    st = [st_list[b][h] for b, h in prob]
    bc = [b_all[b][:, sl[h]] for b, h in prob]
    b_last = [jnp.sum(la_all[b][:, sl[h]], axis=0, keepdims=True) for b, h in prob]
    q_dec = [q[b][:, sl[h]] * (GLA_KEY ** -0.5) * jnp.exp(bc_) for (b, h), bc_ in zip(prob, bc)]
    attn = [jnp.where(causal, _dot_nt(qd, k_ * jnp.exp(-bc_)), 0.0) for qd, k_, bc_ in zip(q_dec, kh, bc)]
    inter = [_dot_nt(qd, st_) for qd, st_ in zip(q_dec, st)]
    o = [i_ + _dot(a_, v_) for i_, a_, v_ in zip(inter, attn, vh)]
    k_dec = [k_ * jnp.exp(bl - bc_) for k_, bl, bc_ in zip(kh, b_last, bc)]
    s_new = [st_ * jnp.exp(bl) + _dot_tn(v_, kd) for st_, bl, v_, kd in zip(st, b_last, vh, k_dec)]
    outs = [jnp.concatenate(o[b * HEADS:(b + 1) * HEADS], axis=-1) for b in range(nb)]
    return outs, [s_new[b * HEADS:(b + 1) * HEADS] for b in range(nb)]


def _dn_qkv(y):
    act = _silu(y)
    parts = []
    for i in range(2 * HEADS):
        xh = act[:, i * HD:(i + 1) * HD]
        xh = xh * lax.rsqrt(jnp.sum(xh * xh, axis=-1, keepdims=True) + EPS)
        parts.append(xh * (HD ** -0.5) if i < HEADS else xh)
    qk = jnp.concatenate(parts, axis=-1)
    return qk[:, :HEADS * HD], qk[:, HEADS * HD:], act[:, 2 * HEADS * HD:]


def _dn_gates(small, alog_row, dt_row):
    lane = _iota(small.shape, 1)
    log_a = -jnp.exp(alog_row) * _softplus(small + dt_row)
    return jnp.where(lane < SM_B, log_a, jnp.where(lane < SM_R, _sigmoid(small), 0.0))


def _gate_norm(o, z, grow):
    parts = []
    for h in range(HEADS):
        oh = o[:, h * HD:(h + 1) * HD]
        parts.append(oh * lax.rsqrt(jnp.mean(oh * oh, axis=-1, keepdims=True) + EPS))
    return jnp.concatenate(parts, axis=-1) * grow * _silu(z)


def _conv_rows(xrows, w_ref, k_taps):
    n = xrows.shape[0]
    acc = xrows * w_ref[k_taps - 1:k_taps, :]
    for s in range(1, k_taps):
        acc = acc + pltpu.roll(xrows, s, 0) * w_ref[k_taps - 1 - s:k_taps - s, :]
    return acc


def _shift_up(x, s):
    return x if s == 0 else pltpu.roll(x, x.shape[0] - s, 0)


def _div_tile(n, cap, mult=8):
    best = None
    for t in range(mult, min(n, cap) + 1, mult):
        if n % t == 0:
            best = t
    return best if best is not None else n


def _halo_prev(tt):
    return lambda b, t: (b, jnp.maximum(t * (tt // HALO) - 1, 0))


def _halo_next(tt, t_total):
    return lambda b, t: (b, jnp.minimum((t + 1) * (tt // HALO), t_total // HALO - 1))


def _mm(a, b, mode, out_dtype, name, tm=512, tn=512, tk=None):
    if mode == "nn":
        (m, k), n = a.shape, b.shape[1]
    elif mode == "nt":
        (m, k), n = a.shape, b.shape[0]
    else:
        (k, m), n = a.shape, b.shape[1]
    tm, tn = min(tm, m), min(tn, n)
    tk = k if tk is None else min(tk, k)
    assert m % tm == 0 and n % tn == 0 and k % tk == 0, (name, a.shape, b.shape, tm, tn, tk)
    nk = k // tk
    if mode == "tn":
        a_spec = pl.BlockSpec((tk, tm), lambda i, j, kk: (kk, i))
    else:
        a_spec = pl.BlockSpec((tm, tk), lambda i, j, kk: (i, kk))
    if mode == "nt":
        b_spec = pl.BlockSpec((tn, tk), lambda i, j, kk: (j, kk))
    else:
        b_spec = pl.BlockSpec((tk, tn), lambda i, j, kk: (kk, j))
    dims = {"nn": ((1,), (0,)), "nt": ((1,), (1,)), "tn": ((0,), (0,))}[mode]

    def body(a_ref, b_ref, o_ref, *acc):
        p = _dg(a_ref[...], b_ref[...], dims)
        if nk == 1:
            o_ref[...] = p.astype(out_dtype)
        else:
            kk = pl.program_id(2)

            @pl.when(kk == 0)
            def _():
                acc[0][...] = p

            @pl.when(kk > 0)
            def _():
                acc[0][...] += p

            @pl.when(kk == nk - 1)
            def _():
                o_ref[...] = acc[0][...].astype(out_dtype)

    return pl.pallas_call(
        body, name=name, grid=(m // tm, n // tn, nk),
        in_specs=[a_spec, b_spec],
        out_specs=pl.BlockSpec((tm, tn), lambda i, j, kk: (i, j)),
        out_shape=jax.ShapeDtypeStruct((m, n), out_dtype),
        scratch_shapes=[pltpu.VMEM((tm, tn), F32)] if nk > 1 else [],
        compiler_params=_params(("parallel", "parallel", "arbitrary")),
    )(a, b)


def _ada_fwd(c_all, w_ada, b_cols):
    def body(c_ref, w_ref, b_ref, o_ref):
        cond = _silu(c_ref[...]).astype(MXU_DT)
        o_ref[...] = _dot(cond, w_ref[...].astype(MXU_DT)) + b_ref[...]

    return pl.pallas_call(body, name="ada_fwd", out_shape=jax.ShapeDtypeStruct((c_all.shape[0], w_ada.shape[1]), F32),
                          compiler_params=_params())(c_all, w_ada, b_cols)


def _ada_bwd(c_all, dmod_all, dmod_cols):
    def body(c_ref, da_ref, dc_ref, gw_ref, gb_ref):
        cond = _silu(c_ref[...]).astype(MXU_DT)
        gw_ref[...] = _dot_tn(cond, dc_ref[...].astype(MXU_DT))
        gb_ref[...] = jnp.sum(da_ref[...], axis=0, keepdims=True)

    return pl.pallas_call(
        body, name="ada_bwd",
        out_shape=(jax.ShapeDtypeStruct((c_all.shape[1], dmod_cols.shape[1]), F32),
                   jax.ShapeDtypeStruct((1, dmod_all.shape[1]), F32)),
        compiler_params=_params())(c_all, dmod_all, dmod_cols)


def _tok_spec(tt, width=D):
    return pl.BlockSpec((1, tt, width), lambda b, t: (b, t, 0))


def _vec_spec(width=D):
    return pl.BlockSpec((1, width), lambda b, t: (0, 0))


def _bvec_spec(width=D):
    return pl.BlockSpec((1, 1, width), lambda b, t: (b, 0, 0))


def _ln0_mod(x, g0, b0, sc, sh):
    bsz, t_total, _ = x.shape
    tt = _div_tile(t_total, 256)

    def body(x_ref, g_ref, b_ref, sc_ref, sh_ref, h_ref):
        xh, _ = _ln_stats(x_ref[0])
        x0 = xh * g_ref[...] + b_ref[...]
        h_ref[0] = (x0 * (1.0 + sc_ref[0]) + sh_ref[0]).astype(MXU_DT)

    return pl.pallas_call(
        body, name="ln0_mod", grid=(bsz, t_total // tt),
        in_specs=[_tok_spec(tt), _vec_spec(), _vec_spec(), _bvec_spec(), _bvec_spec()],
        out_specs=_tok_spec(tt), out_shape=jax.ShapeDtypeStruct(x.shape, MXU_DT),
        compiler_params=_params(("parallel", "parallel")))(x, g0, b0, sc, sh)


def _res_ln_mod(x, y, gt, g0, b0, g1, b1, sc, sh):
    bsz, t_total, _ = x.shape
    tt = _div_tile(t_total, 256)

    def body(x_ref, y_ref, gt_ref, g0_ref, b0_ref, g1_ref, b1_ref, sc_ref, sh_ref, r_ref, h_ref):
        xh, _ = _ln_stats(x_ref[0])
        r = ALPHA * (xh * g0_ref[...] + b0_ref[...]) + (1.0 + gt_ref[0]) * y_ref[0].astype(F32)
        r_ref[0] = r
        rh, _ = _ln_stats(r)
        x1 = rh * g1_ref[...] + b1_ref[...]
        h_ref[0] = (x1 * (1.0 + sc_ref[0]) + sh_ref[0]).astype(MXU_DT)

    return pl.pallas_call(
        body, name="res_ln_mod", grid=(bsz, t_total // tt),
        in_specs=[_tok_spec(tt), _tok_spec(tt), _bvec_spec(), _vec_spec(), _vec_spec(), _vec_spec(), _vec_spec(),
                  _bvec_spec(), _bvec_spec()],
        out_specs=(_tok_spec(tt), _tok_spec(tt)),
        out_shape=(jax.ShapeDtypeStruct(x.shape, F32), jax.ShapeDtypeStruct(x.shape, MXU_DT)),
        compiler_params=_params(("parallel", "parallel")))(x, y, gt, g0, b0, g1, b1, sc, sh)


def _final_fwd_bwd(r1, y2, gt, g1, b1, g2, b2, target):
    bsz, t_total, _ = r1.shape
    tt = _div_tile(t_total, 256)

    def body(r1_ref, y2_ref, gt_ref, g1_ref, b1_ref, g2_ref, b2_ref, tg_ref,
             loss_ref, dr2_ref, dy2_ref, dgt_ref, dg2_ref, db2_ref):
        b, t = pl.program_id(0), pl.program_id(1)

        @pl.when((b == 0) & (t == 0))
        def _():
            loss_ref[...] = jnp.zeros_like(loss_ref)
            dg2_ref[...] = jnp.zeros_like(dg2_ref)
            db2_ref[...] = jnp.zeros_like(db2_ref)

        @pl.when(t == 0)
        def _():
            dgt_ref[...] = jnp.zeros_like(dgt_ref)

        rh1, _ = _ln_stats(r1_ref[0])
        x1 = rh1 * g1_ref[...] + b1_ref[...]
        y2 = y2_ref[0].astype(F32)
        gate = 1.0 + gt_ref[0]
        xh2, rstd2 = _ln_stats(ALPHA * x1 + gate * y2)
        err = xh2 * g2_ref[...] + b2_ref[...] - tg_ref[0]
        loss_ref[...] += jnp.sum(err * err, axis=0, keepdims=True)
        dx2 = err * (1.0 / D)
        dg2_ref[...] += jnp.sum(dx2 * xh2, axis=0, keepdims=True)
        db2_ref[...] += jnp.sum(dx2, axis=0, keepdims=True)
        dr2 = _ln_bwd(dx2 * g2_ref[...], xh2, rstd2)
        dr2_ref[0] = dr2
        dy2_ref[0] = (gate * dr2).astype(MXU_DT)
        dgt_ref[0] += jnp.sum(dr2 * y2, axis=0, keepdims=True)

    vec_out = jax.ShapeDtypeStruct((1, D), F32)
    return pl.pallas_call(
        body, name="final_fwd_bwd", grid=(bsz, t_total // tt),
        in_specs=[_tok_spec(tt), _tok_spec(tt), _bvec_spec(), _vec_spec(), _vec_spec(), _vec_spec(), _vec_spec(),
                  _tok_spec(tt)],
        out_specs=(_vec_spec(), _tok_spec(tt), _tok_spec(tt), _bvec_spec(), _vec_spec(), _vec_spec()),
        out_shape=(vec_out, jax.ShapeDtypeStruct(r1.shape, F32), jax.ShapeDtypeStruct(r1.shape, MXU_DT),
                   jax.ShapeDtypeStruct((bsz, 1, D), F32), vec_out, vec_out),
        compiler_params=_params(("arbitrary", "arbitrary")))(r1, y2, gt, g1, b1, g2, b2, target)


def _ln_bwd_call(name, d_res, d_h, src, g, b, sc, y=None, gt=None):
    bsz, t_total, _ = src.shape
    tt = _div_tile(t_total, 256)
    has_y = y is not None

    def body(*refs):
        if has_y:
            (dres_ref, dh_ref, src_ref, g_ref, b_ref, sc_ref, y_ref, gt_ref,
             dsrc_ref, dsc_ref, dsh_ref, dg_ref, db_ref, dy_ref, dgt_ref) = refs
        else:
            (dres_ref, dh_ref, src_ref, g_ref, b_ref, sc_ref,
             dsrc_ref, dsc_ref, dsh_ref, dg_ref, db_ref) = refs
        bi, t = pl.program_id(0), pl.program_id(1)

        @pl.when((bi == 0) & (t == 0))
        def _():
            dg_ref[...] = jnp.zeros_like(dg_ref)
            db_ref[...] = jnp.zeros_like(db_ref)

        @pl.when(t == 0)
        def _():
            dsc_ref[...] = jnp.zeros_like(dsc_ref)
            dsh_ref[...] = jnp.zeros_like(dsh_ref)
            if has_y:
                dgt_ref[...] = jnp.zeros_like(dgt_ref)

        xh, rstd = _ln_stats(src_ref[0])
        xv = xh * g_ref[...] + b_ref[...]
        dh = dh_ref[0].astype(F32)
        dx = ALPHA * dres_ref[0] + dh * (1.0 + sc_ref[0])
        dsc_ref[0] += jnp.sum(dh * xv, axis=0, keepdims=True)
        dsh_ref[0] += jnp.sum(dh, axis=0, keepdims=True)
        dg_ref[...] += jnp.sum(dx * xh, axis=0, keepdims=True)
        db_ref[...] += jnp.sum(dx, axis=0, keepdims=True)
        dsrc = _ln_bwd(dx * g_ref[...], xh, rstd)
        dsrc_ref[0] = dsrc
        if has_y:
            dy_ref[0] = ((1.0 + gt_ref[0]) * dsrc).astype(MXU_DT)
            dgt_ref[0] += jnp.sum(dsrc * y_ref[0].astype(F32), axis=0, keepdims=True)

    vec_out = jax.ShapeDtypeStruct((1, D), F32)
    bvec_out = jax.ShapeDtypeStruct((bsz, 1, D), F32)
    in_specs = [_tok_spec(tt), _tok_spec(tt), _tok_spec(tt), _vec_spec(), _vec_spec(), _bvec_spec()]
    out_specs = [_tok_spec(tt), _bvec_spec(), _bvec_spec(), _vec_spec(), _vec_spec()]
    out_shape = [jax.ShapeDtypeStruct(src.shape, F32), bvec_out, bvec_out, vec_out, vec_out]
    args = [d_res, d_h, src, g, b, sc]
    if has_y:
        in_specs += [_tok_spec(tt), _bvec_spec()]
        out_specs += [_tok_spec(tt), _bvec_spec()]
        out_shape += [jax.ShapeDtypeStruct(src.shape, MXU_DT), bvec_out]
        args += [y, gt]
    return pl.pallas_call(body, name=name, grid=(bsz, t_total // tt), in_specs=in_specs, out_specs=tuple(out_specs),
                          out_shape=tuple(out_shape), compiler_params=_params(("arbitrary", "arbitrary")))(*args)


FFN_TC = 256
FFN_NJ = D_FF // FFN_TC
FFN_PW = 2 * FFN_TC


def _ffn_pair(a, axis):
    shp = list(a.shape)
    a4 = a.reshape(shp[:axis] + [2, FFN_NJ, FFN_TC] + shp[axis + 1:])
    return jnp.swapaxes(a4, axis, axis + 1).reshape(shp)


def _ffn_unpair(a, axis):
    shp = list(a.shape)
    a4 = a.reshape(shp[:axis] + [FFN_NJ, 2, FFN_TC] + shp[axis + 1:])
    return jnp.swapaxes(a4, axis, axis + 1).reshape(shp)


def _ffn_up_act(h, w_up, cw, cb):
    bsz, t_total, _ = h.shape
    tt = _div_tile(t_total, 256)
    def body(h_ref, wu_ref, w_ref, b_ref, up_ref, o_ref, carry_ref):
        up_t = _dot_nt(h_ref[0], wu_ref[...])
        up_ref[0] = up_t
        prev = jnp.where(pl.program_id(2) == 0, 0.0, carry_ref[...])
        rows = jnp.concatenate([prev, up_t], axis=0)
        u = _conv_rows(rows, w_ref, FFN_CONV_K)[HALO:] + b_ref[...]
        o_ref[0] = (_silu(u[:, :FFN_TC]) * u[:, FFN_TC:]).astype(MXU_DT)
        carry_ref[...] = up_t[tt - HALO:, :]

    return pl.pallas_call(
        body, name="ffn_up_act", grid=(bsz, FFN_NJ, t_total // tt),
        in_specs=[pl.BlockSpec((1, tt, D), lambda b, j, t: (b, t, 0)),
                  pl.BlockSpec((FFN_PW, D), lambda b, j, t: (j, 0)),
                  pl.BlockSpec((FFN_CONV_K, FFN_PW), lambda b, j, t: (0, j)),
                  pl.BlockSpec((1, FFN_PW), lambda b, j, t: (0, j))],
        out_specs=(pl.BlockSpec((1, tt, FFN_PW), lambda b, j, t: (b, t, j)),
                   pl.BlockSpec((1, tt, FFN_TC), lambda b, j, t: (b, t, j))),
        out_shape=(jax.ShapeDtypeStruct((bsz, t_total, 2 * D_FF), F32),
                   jax.ShapeDtypeStruct((bsz, t_total, D_FF), MXU_DT)),
        scratch_shapes=[pltpu.VMEM((HALO, FFN_PW), F32)],
        compiler_params=_params(("parallel", "parallel", "arbitrary")))(h, w_up, cw, cb)


HALO16 = 16


def _ffn_act_bwd(up, dy2, w_down, cw, cb):
    bsz, t_total, width = up.shape
    tt = _div_tile(t_total, 256)
    nt = t_total // tt
    hp, hn = _halo_prev(tt), _halo_next(tt, t_total)

    def body(x_ref, xp_ref, xn_ref, dy_ref, dyn_ref, wd_ref, w_ref, b_ref, dup_ref, dw_ref, db_ref):
        b, t = pl.program_id(1), pl.program_id(2)

        @pl.when((b == 0) & (t == 0))
        def _():
            dw_ref[...] = jnp.zeros_like(dw_ref)
            db_ref[...] = jnp.zeros_like(db_ref)

        prev = jnp.where(t == 0, 0.0, xp_ref[0])
        rows = jnp.concatenate([prev, x_ref[0], xn_ref[0]], axis=0)
        u = _conv_rows(rows, w_ref, FFN_CONV_K)[HALO:] + b_ref[...]
        g_pre, v_pre = u[:, :FFN_TC], u[:, FFN_TC:]
        valid = (_iota((tt + HALO, 1), 0) < tt) | (t < nt - 1)
        da = jnp.concatenate([_dot_nt(dy_ref[0], wd_ref[...]), _dot_nt(dyn_ref[0], wd_ref[...])[:HALO]], axis=0)
        da_ext = jnp.where(valid, da, 0.0)
        sg = _sigmoid(g_pre)
        gs = g_pre * sg
        du = jnp.concatenate([da_ext * v_pre * (sg + gs * (1.0 - sg)), da_ext * gs], axis=1)
        dup = du * w_ref[FFN_CONV_K - 1:FFN_CONV_K, :]
        for s in range(1, FFN_CONV_K):
            dup = dup + _shift_up(du, s) * w_ref[FFN_CONV_K - 1 - s:FFN_CONV_K - s, :]
        dup_ref[0] = dup[:tt].astype(MXU_DT)
        du_t = du[:tt]
        db_ref[...] += jnp.sum(du_t, axis=0, keepdims=True)
        for k in range(FFN_CONV_K):
            s = FFN_CONV_K - 1 - k
            xs = (rows if s == 0 else pltpu.roll(rows, s, 0))[HALO:HALO + tt]
            dw_ref[k:k + 1, :] += jnp.sum(du_t * xs, axis=0, keepdims=True)

    def halo(h, w):
        return pl.BlockSpec((1, HALO, w), lambda j, b, t: (*h(b, t), j))

    wspec = lambda rows_: pl.BlockSpec((rows_, FFN_PW), lambda j, b, t: (0, j))
    tile = pl.BlockSpec((1, tt, FFN_PW), lambda j, b, t: (b, t, j))
    dy_next = lambda j, b, t: (b, jnp.minimum((t + 1) * (tt // HALO16), t_total // HALO16 - 1), 0)
    return pl.pallas_call(
        body, name="ffn_act_bwd", grid=(FFN_NJ, bsz, nt),
        in_specs=[tile, halo(hp, FFN_PW), halo(hn, FFN_PW),
                  pl.BlockSpec((1, tt, D), lambda j, b, t: (b, t, 0)), pl.BlockSpec((1, HALO16, D), dy_next),
                  pl.BlockSpec((FFN_TC, D), lambda j, b, t: (j, 0)), wspec(FFN_CONV_K), wspec(1)],
        out_specs=(tile, wspec(FFN_CONV_K), wspec(1)),
        out_shape=(jax.ShapeDtypeStruct(up.shape, MXU_DT), jax.ShapeDtypeStruct((FFN_CONV_K, width), F32),
                   jax.ShapeDtypeStruct((1, width), F32)),
        compiler_params=_params(("arbitrary", "arbitrary", "arbitrary")))(up, up, up, dy2, dy2, w_down, cw, cb)


QKV_W = 3 * HEADS * HD
SM_BLK = P_SM // 128


def _dn_pre_fwd(proj, conv_w, alog_row, dt_row):
    bsz, t_total, _ = proj.shape
    tt = _div_tile(t_total, 256)
    hp = _halo_prev(tt)

    def body(x_ref, xp_ref, sm_ref, w_ref, al_ref, dt_ref, q_ref, k_ref, v_ref, g_ref):
        prev = jnp.where(pl.program_id(1) == 0, 0.0, xp_ref[0])
        y = _conv_rows(jnp.concatenate([prev, x_ref[0]], axis=0), w_ref, DN_CONV_K)[HALO:]
        q_ref[0], k_ref[0], v_ref[0] = _dn_qkv(y)
        g_ref[0] = _dn_gates(sm_ref[0], al_ref[...], dt_ref[...])

    out512 = jax.ShapeDtypeStruct((bsz, t_total, HEADS * HD), F32)
    return pl.pallas_call(
        body, name="dn_pre_fwd", grid=(bsz, t_total // tt),
        in_specs=[pl.BlockSpec((1, tt, QKV_W), lambda b, t: (b, t, 0)),
                  pl.BlockSpec((1, HALO, QKV_W), lambda b, t: (*hp(b, t), 0)),
                  pl.BlockSpec((1, tt, 128), lambda b, t: (b, t, SM_BLK)),
                  pl.BlockSpec((DN_CONV_K, QKV_W), lambda b, t: (0, 0)), _vec_spec(128), _vec_spec(128)],
        out_specs=(_tok_spec(tt, 512), _tok_spec(tt, 512), _tok_spec(tt, 512), _tok_spec(tt, 128)),
        out_shape=(out512, out512, out512, jax.ShapeDtypeStruct((bsz, t_total, 128), F32)),
        compiler_params=_params(("parallel", "parallel")))(proj, proj, proj, conv_w, alog_row, dt_row)


def _dn_pre_bwd(proj, dq, dk, dv, dgates, conv_w, alog_row, dt_row):
    bsz, t_total, _ = proj.shape
    tt = _div_tile(t_total, 128)
    nt = t_total // tt
    hp, hn = _halo_prev(tt), _halo_next(tt, t_total)

    def body(x_ref, xp_ref, xn_ref, sm_ref, dq_ref, dqn_ref, dk_ref, dkn_ref, dv_ref, dvn_ref, dg_ref,
             w_ref, al_ref, dt_ref, dx_ref, dsm_ref, dw_ref, dal_ref, ddt_ref):
        b, t = pl.program_id(0), pl.program_id(1)

        @pl.when((b == 0) & (t == 0))
        def _():
            dw_ref[...] = jnp.zeros_like(dw_ref)
            dal_ref[...] = jnp.zeros_like(dal_ref)
            ddt_ref[...] = jnp.zeros_like(ddt_ref)

        prev = jnp.where(t == 0, 0.0, xp_ref[0])
        rows = jnp.concatenate([prev, x_ref[0], xn_ref[0]], axis=0)
        y = _conv_rows(rows, w_ref, DN_CONV_K)[HALO:]
        valid = (_iota((tt + HALO, 1), 0) < tt) | (t < nt - 1)

        def ext(tile_ref, next_ref):
            return jnp.where(valid, jnp.concatenate([tile_ref[0], next_ref[0]], axis=0), 0.0)

        _, vjp_qkv = jax.vjp(_dn_qkv, y)
        (dy,) = vjp_qkv((ext(dq_ref, dqn_ref), ext(dk_ref, dkn_ref), ext(dv_ref, dvn_ref)))
        dy = jnp.where(valid, dy, 0.0)
        dx = dy * w_ref[DN_CONV_K - 1:DN_CONV_K, :]
        for s in range(1, DN_CONV_K):
            dx = dx + _shift_up(dy, s) * w_ref[DN_CONV_K - 1 - s:DN_CONV_K - s, :]
        dx_ref[0] = dx[:tt].astype(MXU_DT)
        dy_t = dy[:tt]
        for k in range(DN_CONV_K):
            s = DN_CONV_K - 1 - k
            xs = (rows if s == 0 else pltpu.roll(rows, s, 0))[HALO:HALO + tt]
            dw_ref[k:k + 1, :] += jnp.sum(dy_t * xs, axis=0, keepdims=True)
        _, vjp_g = jax.vjp(_dn_gates, sm_ref[0], al_ref[...], dt_ref[...])
        dsm, dal, ddt = vjp_g(dg_ref[0])
        dsm_ref[0] = dsm
        dal_ref[...] += dal
        ddt_ref[...] += ddt

    def tile(width, blk=0):
        return pl.BlockSpec((1, tt, width), lambda b, t: (b, t, blk))

    def halo(h, width):
        return pl.BlockSpec((1, HALO, width), lambda b, t: (*h(b, t), 0))

    return pl.pallas_call(
        body, name="dn_pre_bwd", grid=(bsz, nt),
        in_specs=[tile(QKV_W), halo(hp, QKV_W), halo(hn, QKV_W), tile(128, SM_BLK),
                  tile(512), halo(hn, 512), tile(512), halo(hn, 512), tile(512), halo(hn, 512), tile(128),
                  pl.BlockSpec((DN_CONV_K, QKV_W), lambda b, t: (0, 0)), _vec_spec(128), _vec_spec(128)],
        out_specs=(tile(QKV_W), tile(128), pl.BlockSpec((DN_CONV_K, QKV_W), lambda b, t: (0, 0)),
                   _vec_spec(128), _vec_spec(128)),
        out_shape=(jax.ShapeDtypeStruct((bsz, t_total, QKV_W), MXU_DT), jax.ShapeDtypeStruct((bsz, t_total, 128), F32),
                   jax.ShapeDtypeStruct((DN_CONV_K, QKV_W), F32), jax.ShapeDtypeStruct((1, 128), F32),
                   jax.ShapeDtypeStruct((1, 128), F32)),
        compiler_params=_params(("arbitrary", "arbitrary")))(
            proj, proj, proj, proj, dq, dq, dk, dk, dv, dv, dgates, conv_w, alog_row, dt_row)


def _state_spec(bsz, idx):
    return pl.BlockSpec((bsz, 1, HEADS, HD, HD), lambda c: (0, idx(c), 0, 0, 0))


def _inv_spec(bsz, idx):
    return pl.BlockSpec((bsz, 1, HEADS, CHUNK, CHUNK), lambda c: (0, idx(c), 0, 0, 0))


def _chunk_spec(bsz, width, idx, blk=0):
    return pl.BlockSpec((bsz, CHUNK, width), lambda c: (0, idx(c), blk))


def _dn_rec_fwd(q, k, v, gates):
    bsz, t_total, _ = q.shape
    nc = t_total // CHUNK
    fwd = lambda c: c

    def body(q_ref, k_ref, v_ref, g_ref, o_ref, ss_ref, inv_ref, s_ref):
        @pl.when(pl.program_id(0) == 0)
        def _():
            s_ref[...] = jnp.zeros_like(s_ref)

        seqs = range(bsz)
        s_list = [[s_ref[b * HEADS + h] for h in range(HEADS)] for b in seqs]
        for b in seqs:
            for h in range(HEADS):
                ss_ref[b, 0, h] = s_list[b][h]
        o, new_s, invs = _dn_chunk(s_list, [q_ref[b] for b in seqs], [k_ref[b] for b in seqs],
                                   [v_ref[b] for b in seqs], [g_ref[b] for b in seqs], with_inv=True)
        for b in seqs:
            o_ref[b] = o[b]
            for h in range(HEADS):
                s_ref[b * HEADS + h] = new_s[b][h]
                inv_ref[b, 0, h] = invs[b * HEADS + h]

    return pl.pallas_call(
        body, name="dn_rec_fwd", grid=(nc,),
        in_specs=[_chunk_spec(bsz, 512, fwd)] * 3 + [_chunk_spec(bsz, 128, fwd)],
        out_specs=(_chunk_spec(bsz, 512, fwd), _state_spec(bsz, fwd), _inv_spec(bsz, fwd)),
        out_shape=(jax.ShapeDtypeStruct(q.shape, F32), jax.ShapeDtypeStruct((bsz, nc, HEADS, HD, HD), F32),
                   jax.ShapeDtypeStruct((bsz, nc, HEADS, CHUNK, CHUNK), F32)),
        scratch_shapes=[pltpu.VMEM((bsz * HEADS, HD, HD), F32)],
        compiler_params=_params(("arbitrary",)))(q, k, v, gates)


def _dn_rec_bwd(q, k, v, gates, states, invs, do):
    bsz, t_total, _ = q.shape
    nc = t_total // CHUNK
    rev = lambda c: nc - 1 - c

    def body(q_ref, k_ref, v_ref, g_ref, ss_ref, inv_ref, do_ref, dq_ref, dk_ref, dv_ref, dg_ref, ds_ref):
        @pl.when(pl.program_id(0) == 0)
        def _():
            ds_ref[...] = jnp.zeros_like(ds_ref)

        seqs = range(bsz)
        s_list = [[ss_ref[b, 0, h] for h in range(HEADS)] for b in seqs]
        known = [inv_ref[b, 0, h] for b in seqs for h in range(HEADS)]
        _, vjp = jax.vjp(functools.partial(_dn_chunk, inv_known=known),
                         s_list, [q_ref[b] for b in seqs], [k_ref[b] for b in seqs],
                         [v_ref[b] for b in seqs], [g_ref[b] for b in seqs])
        ds_in, dq, dk, dv, dg = vjp(([do_ref[b] for b in seqs],
                                     [[ds_ref[b * HEADS + h] for h in range(HEADS)] for b in seqs]))
        for b in seqs:
            dq_ref[b], dk_ref[b], dv_ref[b], dg_ref[b] = dq[b], dk[b], dv[b], dg[b]
            for h in range(HEADS):
                ds_ref[b * HEADS + h] = ds_in[b][h]

    tok = lambda width: _chunk_spec(bsz, width, rev)
    out512 = jax.ShapeDtypeStruct(q.shape, F32)
    return pl.pallas_call(
        body, name="dn_rec_bwd", grid=(nc,),
        in_specs=[tok(512), tok(512), tok(512), tok(128), _state_spec(bsz, rev), _inv_spec(bsz, rev), tok(512)],
        out_specs=(tok(512), tok(512), tok(512), tok(128)),
        out_shape=(out512, out512, out512, jax.ShapeDtypeStruct(gates.shape, F32)),
        scratch_shapes=[pltpu.VMEM((bsz * HEADS, HD, HD), F32)],
        compiler_params=_params(("arbitrary",)))(q, k, v, gates, states, invs, do)


GQ_BLK, GK_BLK, GV_BLK = P_GQ // 512, P_GK // 512, P_GV // 512


def _gla_rec_fwd(proj, w2, bg):
    bsz, t_total, _ = proj.shape
    nc = t_total // CHUNK

    fwd = lambda c: c

    def body(q_ref, k_ref, v_ref, sm_ref, w2_ref, bg_ref, o_ref, ss_ref, s_ref):
        @pl.when(pl.program_id(0) == 0)
        def _():
            s_ref[...] = jnp.zeros_like(s_ref)

        seqs = range(bsz)
        s_list = [[s_ref[b * HEADS + h] for h in range(HEADS)] for b in seqs]
        for b in seqs:
            for h in range(HEADS):
                ss_ref[b, 0, h] = s_list[b][h]
        o, new_s = _gla_chunk(s_list, [q_ref[b] for b in seqs], [k_ref[b] for b in seqs], [v_ref[b] for b in seqs],
                              [sm_ref[b] for b in seqs], w2_ref[...], bg_ref[...])
        for b in seqs:
            o_ref[b] = o[b]
            for h in range(HEADS):
                s_ref[b * HEADS + h] = new_s[b][h]

    col = lambda blk, width=512: _chunk_spec(bsz, width, fwd, blk)
    return pl.pallas_call(
        body, name="gla_rec_fwd", grid=(nc,),
        in_specs=[col(GQ_BLK), col(GK_BLK), col(GV_BLK), col(SM_BLK, 128),
                  pl.BlockSpec((128, 512), lambda c: (0, 0)), pl.BlockSpec((1, 512), lambda c: (0, 0))],
        out_specs=(col(0), _state_spec(bsz, fwd)),
        out_shape=(jax.ShapeDtypeStruct((bsz, t_total, 512), F32),
                   jax.ShapeDtypeStruct((bsz, nc, HEADS, HD, HD), F32)),
        scratch_shapes=[pltpu.VMEM((bsz * HEADS, HD, HD), F32)],
        compiler_params=_params(("arbitrary",)))(proj, proj, proj, proj, w2, bg)


def _gla_rec_bwd(proj, w2, bg, states, do, dsm_dn):
    bsz, t_total, _ = proj.shape
    nc = t_total // CHUNK
    rev = lambda c: nc - 1 - c

    def body(q_ref, k_ref, v_ref, sm_ref, w2_ref, bg_ref, ss_ref, do_ref, dsd_ref,
             dq_ref, dk_ref, dv_ref, dsm_ref, dw2_ref, dbg_ref, ds_ref):
        @pl.when(pl.program_id(0) == 0)
        def _():
            dw2_ref[...] = jnp.zeros_like(dw2_ref)
            dbg_ref[...] = jnp.zeros_like(dbg_ref)
            ds_ref[...] = jnp.zeros_like(ds_ref)

        seqs = range(bsz)
        s_list = [[ss_ref[b, 0, h] for h in range(HEADS)] for b in seqs]
        _, vjp = jax.vjp(_gla_chunk, s_list, [q_ref[b] for b in seqs], [k_ref[b] for b in seqs],
                         [v_ref[b] for b in seqs], [sm_ref[b] for b in seqs], w2_ref[...], bg_ref[...])
        ds_in, dq, dk, dv, dsm, dw2, dbg = vjp(([do_ref[b] for b in seqs],
                                                [[ds_ref[b * HEADS + h] for h in range(HEADS)] for b in seqs]))
        for b in seqs:
            dq_ref[b], dk_ref[b], dv_ref[b] = dq[b].astype(MXU_DT), dk[b].astype(MXU_DT), dv[b].astype(MXU_DT)
            dsm_ref[b] = (dsm[b] + dsd_ref[b]).astype(MXU_DT)
            for h in range(HEADS):
                ds_ref[b * HEADS + h] = ds_in[b][h]
        dw2_ref[...] += dw2
        dbg_ref[...] += dbg

    col = lambda blk, width=512: _chunk_spec(bsz, width, rev, blk)
    w2_spec = pl.BlockSpec((128, 512), lambda c: (0, 0))
    bg_spec = pl.BlockSpec((1, 512), lambda c: (0, 0))
    out512 = jax.ShapeDtypeStruct((bsz, t_total, 512), MXU_DT)
    return pl.pallas_call(
        body, name="gla_rec_bwd", grid=(nc,),
        in_specs=[col(GQ_BLK), col(GK_BLK), col(GV_BLK), col(SM_BLK, 128), w2_spec, bg_spec,
                  _state_spec(bsz, rev), col(0), col(0, 128)],
        out_specs=(col(0), col(0), col(0), col(0, 128), w2_spec, bg_spec),
        out_shape=(out512, out512, out512, jax.ShapeDtypeStruct((bsz, t_total, 128), MXU_DT),
                   jax.ShapeDtypeStruct((128, 512), F32), jax.ShapeDtypeStruct((1, 512), F32)),
        scratch_shapes=[pltpu.VMEM((bsz * HEADS, HD, HD), F32)],
        compiler_params=_params(("arbitrary",)))(proj, proj, proj, proj, w2, bg, states, do, dsm_dn)


Z_BLK, GG_BLK = P_Z // 512, P_GG // 512


def _mix_out_fwd(o_dn, o_gla, proj, grow_dn, grow_gla):
    bsz, t_total, _ = o_dn.shape
    tt = _div_tile(t_total, 256)

    def body(od_ref, og_ref, z_ref, gg_ref, gd_ref, gl_ref, o_ref):
        o_ref[0, :, :512] = _gate_norm(od_ref[0], z_ref[0], gd_ref[...]).astype(MXU_DT)
        o_ref[0, :, 512:] = _gate_norm(og_ref[0], gg_ref[0], gl_ref[...]).astype(MXU_DT)

    def col(blk):
        return pl.BlockSpec((1, tt, 512), lambda b, t: (b, t, blk))

    return pl.pallas_call(
        body, name="mix_out_fwd", grid=(bsz, t_total // tt),
        in_specs=[col(0), col(0), col(Z_BLK), col(GG_BLK), _vec_spec(512), _vec_spec(512)],
        out_specs=_tok_spec(tt), out_shape=jax.ShapeDtypeStruct((bsz, t_total, D), MXU_DT),
        compiler_params=_params(("parallel", "parallel")))(o_dn, o_gla, proj, proj, grow_dn, grow_gla)


def _mix_out_bwd(do, o_dn, o_gla, proj, grow_dn, grow_gla):
    bsz, t_total, _ = o_dn.shape
    tt = _div_tile(t_total, 256)

    def body(do_ref, od_ref, og_ref, z_ref, gg_ref, gd_ref, gl_ref,
             dod_ref, dog_ref, dz_ref, dgg_ref, dgd_ref, dgl_ref):
        @pl.when((pl.program_id(0) == 0) & (pl.program_id(1) == 0))
        def _():
            dgd_ref[...] = jnp.zeros_like(dgd_ref)
            dgl_ref[...] = jnp.zeros_like(dgl_ref)

        def one(o_ref, gate_ref, g_ref, ct, do_out, dgate_out, dg_out):
            _, vjp = jax.vjp(_gate_norm, o_ref[0], gate_ref[0], g_ref[...])
            d_o, d_gate, d_row = vjp(ct)
            do_out[0] = d_o
            dgate_out[0] = d_gate.astype(MXU_DT)
            acc = d_row[:, :HD]
            for h in range(1, HEADS):
                acc = acc + d_row[:, h * HD:(h + 1) * HD]
            dg_out[...] += acc

        ct = do_ref[0].astype(F32)
        one(od_ref, z_ref, gd_ref, ct[:, :512], dod_ref, dz_ref, dgd_ref)
        one(og_ref, gg_ref, gl_ref, ct[:, 512:], dog_ref, dgg_ref, dgl_ref)

    def col(blk):
        return pl.BlockSpec((1, tt, 512), lambda b, t: (b, t, blk))

    f512 = jax.ShapeDtypeStruct((bsz, t_total, 512), F32)
    b512 = jax.ShapeDtypeStruct((bsz, t_total, 512), MXU_DT)
    g128 = jax.ShapeDtypeStruct((1, HD), F32)
    return pl.pallas_call(
        body, name="mix_out_bwd", grid=(bsz, t_total // tt),
        in_specs=[_tok_spec(tt), col(0), col(0), col(Z_BLK), col(GG_BLK), _vec_spec(512), _vec_spec(512)],
        out_specs=(col(0), col(0), col(0), col(0), _vec_spec(HD), _vec_spec(HD)),
        out_shape=(f512, f512, b512, b512, g128, g128),
        compiler_params=_params(("arbitrary", "arbitrary")))(do, o_dn, o_gla, proj, proj, grow_dn, grow_gla)


def _sum_slots(x, name):
    n, rows, cols = x.shape
    tr = _div_tile(rows, max(8, (1 << 19) // cols))

    def body(x_ref, o_ref):
        acc = x_ref[0].astype(F32)
        for i in range(1, n):
            acc = acc + x_ref[i].astype(F32)
        o_ref[...] = acc

    return pl.pallas_call(
        body, name=name, grid=(rows // tr,),
        in_specs=[pl.BlockSpec((n, tr, cols), lambda i: (0, i, 0))],
        out_specs=pl.BlockSpec((tr, cols), lambda i: (i, 0)),
        out_shape=jax.ShapeDtypeStruct((rows, cols), F32), compiler_params=_params(("parallel",)))(x)


def _adamw_math(w, g, m, v):
    nm = ADAM_B1 * m + (1.0 - ADAM_B1) * g
    nv = ADAM_B2 * v + (1.0 - ADAM_B2) * (g * g)
    m_hat = nm / (1.0 - ADAM_B1 ** ADAM_STEP)
    v_hat = nv / (1.0 - ADAM_B2 ** ADAM_STEP)
    return -ADAM_LR * (m_hat / (jnp.sqrt(v_hat) + ADAM_EPS) + ADAM_WD * w), nm, nv


def _adamw(w, g, m, v, name):
    _, rows, cols = w.shape
    tr = _div_tile(rows, max(8, (1 << 18) // cols))

    def body(w_ref, g_ref, m_ref, v_ref, d_ref, nm_ref, nv_ref):
        d_ref[...], nm_ref[...], nv_ref[...] = _adamw_math(w_ref[...], g_ref[...], m_ref[...], v_ref[...])

    spec = pl.BlockSpec((1, tr, cols), lambda i: (0, i, 0))
    shp = jax.ShapeDtypeStruct(w.shape, F32)
    return pl.pallas_call(body, name=name, grid=(rows // tr,), in_specs=[spec] * 4, out_specs=(spec,) * 3,
                          out_shape=(shp,) * 3, compiler_params=_params(("parallel",)))(w, g, m, v)


def _adamw_many(ws, gs, ms, vs, name):
    n = len(ws)

    def body(*refs):
        for i in range(n):
            d, nm, nv = _adamw_math(refs[i][...], refs[n + i][...], refs[2 * n + i][...], refs[3 * n + i][...])
            refs[4 * n + i][...] = d
            refs[5 * n + i][...] = nm
            refs[6 * n + i][...] = nv

    shapes = tuple(jax.ShapeDtypeStruct(w.shape, F32) for w in ws)
    outs = pl.pallas_call(body, name=name, out_shape=shapes * 3, compiler_params=_params())(*ws, *gs, *ms, *vs)
    return outs[:n], outs[n:2 * n], outs[2 * n:]


def _position():
    return lax.axis_index("x"), lax.axis_index("y"), lax.axis_index("c")


def _slot(px, py, pc):
    return 4 * px + 2 * py + pc


def _gather_small(x, name):
    rows, cols = x.shape

    def body(x_ref, o_ref, send_sems, recv_sems):
        mx, my, mc = _position()

        def peer(k):
            return (mx ^ ((k >> 2) & 1), my ^ ((k >> 1) & 1), mc ^ (k & 1))

        o_ref[_slot(mx, my, mc)] = x_ref[...]
        sends = []
        for k in range(1, N_DEV):
            cp = pltpu.make_async_remote_copy(src_ref=x_ref, dst_ref=o_ref.at[_slot(mx, my, mc)],
                                              send_sem=send_sems.at[k - 1], recv_sem=recv_sems.at[k - 1],
                                              device_id=peer(k), device_id_type=MESH)
            cp.start()
            sends.append(cp)
        for k in range(1, N_DEV):
            pltpu.make_async_remote_copy(src_ref=x_ref, dst_ref=o_ref.at[_slot(*peer(k))],
                                         send_sem=send_sems.at[k - 1], recv_sem=recv_sems.at[k - 1],
                                         device_id=peer(k), device_id_type=MESH).wait_recv()
        for cp in sends:
            cp.wait_send()

    return pl.pallas_call(
        body, name=name, out_shape=jax.ShapeDtypeStruct((N_DEV, rows, cols), x.dtype),
        in_specs=[pl.BlockSpec(memory_space=pltpu.VMEM)], out_specs=pl.BlockSpec(memory_space=pltpu.VMEM),
        scratch_shapes=[pltpu.SemaphoreType.DMA((N_DEV - 1,)), pltpu.SemaphoreType.DMA((N_DEV - 1,))],
        compiler_params=pltpu.CompilerParams(vmem_limit_bytes=VMEM_LIMIT_V7X))(x)


def _gather_big(shards):
    n = len(shards)

    def body(*refs):
        xs, outs = refs[:n], refs[n:2 * n]
        send_sems, recv_sems, local_sems = refs[2 * n:]
        mx, my, mc = _position()
        me, sibling = (mx, my, mc), (mx, my, 1 - mc)
        chips = [(1 - mx, my), (mx, 1 - my), (1 - mx, 1 - my)]

        def copy(a, k, block, to, src=None):
            dst = outs[a].at[_slot(*block)]
            return pltpu.make_async_remote_copy(src_ref=dst if src is None else src, dst_ref=dst,
                                                send_sem=send_sems.at[7 * a + k], recv_sem=recv_sems.at[7 * a + k],
                                                device_id=to, device_id_type=MESH)

        mine = [pltpu.make_async_copy(xs[a], outs[a].at[_slot(*me)], local_sems.at[a]) for a in range(n)]
        for cp in mine:
            cp.start()
        started = []
        for a in range(n):
            started.append(copy(a, 0, me, sibling, src=xs[a]))
            started += [copy(a, 1 + j, me, (*chip, mc), src=xs[a]) for j, chip in enumerate(chips)]
        for cp in started:
            cp.start()
        for j, chip in enumerate(chips):
            for a in range(n):
                copy(a, 1 + j, (*chip, mc), me).wait_recv()
                fwd = copy(a, 4 + j, (*chip, mc), sibling)
                fwd.start()
                started.append(fwd)
        for a in range(n):
            copy(a, 0, sibling, me).wait_recv()
            for j, chip in enumerate(chips):
                copy(a, 4 + j, (*chip, 1 - mc), me).wait_recv()
        for cp in started:
            cp.wait_send()
        for cp in mine:
            cp.wait()

    any_spec = pl.BlockSpec(memory_space=pl.ANY)
    return pl.pallas_call(
        body, name="gather_weights",
        out_shape=tuple(jax.ShapeDtypeStruct((N_DEV,) + s.shape, s.dtype) for s in shards),
        in_specs=[any_spec] * n, out_specs=(any_spec,) * n,
        scratch_shapes=[pltpu.SemaphoreType.DMA((7 * n,)), pltpu.SemaphoreType.DMA((7 * n,)),
                        pltpu.SemaphoreType.DMA((n,))])(*shards)


def _peer(pos, k):
    mx, my, mc = pos
    return (mx ^ ((k >> 2) & 1), my ^ ((k >> 1) & 1), mc ^ (k & 1))


def _exchange_copies(srcs, lands, send_sems, recv_sems, by_owner):
    pos = _position()
    me = _slot(*pos)
    out = []
    for a, (src, land) in enumerate(zip(srcs, lands)):
        for k in range(1, N_DEV):
            peer = _peer(pos, k)
            sems = dict(send_sem=send_sems.at[7 * a + k - 1], recv_sem=recv_sems.at[7 * a + k - 1],
                        device_id=peer, device_id_type=MESH)
            mine = src.at[_slot(*peer)] if by_owner else src
            send = pltpu.make_async_remote_copy(src_ref=mine, dst_ref=land.at[me], **sems)
            recv = pltpu.make_async_remote_copy(src_ref=mine, dst_ref=land.at[_slot(*peer)], **sems)
            out.append((send, recv))
    return out


_HBM_SPEC = pl.BlockSpec(memory_space=pltpu.HBM)
_SEM_SPEC = pl.BlockSpec(memory_space=pltpu.SEMAPHORE)
_DATAFLOW = pltpu.SideEffectType.DATAFLOW_SIDE_EFFECTING


def _exchange_start(name, srcs, slab_shapes, after, by_owner, carry=()):
    n, na, nc = len(srcs), len(after), len(carry)
    lands = [pltpu.with_memory_space_constraint(lax.empty((N_DEV,) + s, x.dtype), pltpu.HBM)
             for s, x in zip(slab_shapes, srcs)]
    thru = [pltpu.with_memory_space_constraint(x, pltpu.HBM) for x in [*srcs, *lands, *carry]]

    def body(*refs):
        src_refs, land_refs = refs[:n], refs[n:2 * n]
        send_sems, recv_sems = refs[len(thru) + na], refs[len(thru) + na + 1]
        token = refs[-1]
        for send, _ in _exchange_copies(src_refs, land_refs, send_sems, recv_sems, by_owner):
            send.start()
        token[...] = jnp.zeros_like(token)

    outs = pl.pallas_call(
        body, name=name,
        out_shape=(pltpu.SemaphoreType.DMA((7 * n,)), pltpu.SemaphoreType.DMA((7 * n,)),
                   *[pltpu.HBM(x.shape, x.dtype) for x in thru], jax.ShapeDtypeStruct((8, 128), F32)),
        in_specs=[_HBM_SPEC] * len(thru) + [pl.BlockSpec(memory_space=pl.ANY)] * na,
        out_specs=(_SEM_SPEC, _SEM_SPEC, *[_HBM_SPEC] * len(thru), pl.BlockSpec(memory_space=pltpu.VMEM)),
        input_output_aliases={i: 2 + i for i in range(len(thru))},
        compiler_params=pltpu.CompilerParams(has_side_effects=_DATAFLOW))(*thru, *after)
    return (outs[0], outs[1], list(outs[2:2 + n]), list(outs[2 + n:2 + 2 * n]), outs[-1],
            list(outs[2 + 2 * n:2 + 2 * n + nc]))


def _exchange_wait(name, send_sems, recv_sems, srcs, lands, after, by_owner):
    n = len(srcs)

    def body(*refs):
        src_refs, land_refs = refs[:n], refs[n:2 * n]
        s_sems, r_sems = refs[2 * n], refs[2 * n + 1]
        for send, recv in _exchange_copies(src_refs, land_refs, s_sems, r_sems, by_owner):
            send.wait_send()
            recv.wait_recv()

    outs = pl.pallas_call(
        body, name=name,
        out_shape=(*[pltpu.HBM(x.shape, x.dtype) for x in srcs], *[pltpu.HBM(l.shape, l.dtype) for l in lands]),
        in_specs=[_HBM_SPEC] * (2 * n) + [_SEM_SPEC, _SEM_SPEC, pl.BlockSpec(memory_space=pl.ANY)],
        out_specs=tuple([_HBM_SPEC] * (2 * n)),
        input_output_aliases={i: i for i in range(2 * n)},
        compiler_params=pltpu.CompilerParams(has_side_effects=_DATAFLOW))(*srcs, *lands, send_sems, recv_sems, after)
    return list(outs[:n]), list(outs[n:])


def _pad_heads(x, axis):
    shp = list(x.shape)
    x4 = x.reshape(shp[:axis] + [HEADS, GLA_KEY] + shp[axis + 1:])
    pad = [(0, 0)] * x4.ndim
    pad[axis + 1] = (0, HD - GLA_KEY)
    return jnp.pad(x4, pad).reshape(shp[:axis] + [HEADS * HD] + shp[axis + 1:])


def _unpad_heads(x, axis):
    shp = list(x.shape)
    x4 = x.reshape(shp[:axis] + [HEADS, HD] + shp[axis + 1:])
    x4 = lax.slice_in_dim(x4, 0, GLA_KEY, axis=axis + 1)
    return x4.reshape(shp[:axis] + [HEADS * GLA_KEY] + shp[axis + 1:])


O_Z_END, O_AB, O_GQ, O_GK, O_GV, O_R = 2048, 2048, 2056, 2312, 2568, 3592


def _padded_row(f):
    if f < O_Z_END:
        return f
    if f < O_GQ:
        return P_SM + (f - O_AB)
    if f < O_GV:
        base, g = (P_GQ, f - O_GQ) if f < O_GK else (P_GK, f - O_GK)
        return base + HD * (g // GLA_KEY) + g % GLA_KEY
    if f < O_R:
        return P_GV + (f - O_GV)
    return P_SM + 8 + (f - O_R)


def _runs(pairs):
    out = []
    for d, s in pairs:
        if out and out[-1][0] + out[-1][2] == d and out[-1][1] + out[-1][2] == s:
            out[-1][2] += 1
        else:
            out.append([d, s, 1])
    return out


def _pad_in_rows(shards):
    wt = shards.reshape(IN_W, D)
    return jnp.concatenate([
        wt[:O_Z_END], _pad_heads(wt[O_GQ:O_GK], 0), _pad_heads(wt[O_GK:O_GV], 0), wt[O_GV:O_R],
        wt[O_AB:O_GQ], wt[O_R:], jnp.zeros((P_W - P_SM - 8 - GATE_RANK, D), wt.dtype)], axis=0)


def _unpad_in_rows(gt):
    per = IN_W // N_DEV
    return jnp.stack([
        jnp.concatenate([gt[src:src + n] for _, src, n in
                         _runs([(f, _padded_row(f)) for f in range(j * per, (j + 1) * per)])], axis=0)
        for j in range(N_DEV)])


def _lane_row(vals, width=128):
    return jnp.pad(vals.reshape(1, -1), ((0, 0), (0, width - vals.size)))


SMALL_NAMES = ["ln0_g", "ln0_b", "b_ada", "dn_conv", "dn_a_log", "dn_dt_bias", "dn_norm_g", "gla_w_gate2",
               "gla_b_gate", "gla_norm_g", "ln1_g", "ln1_b", "ffn_conv", "ffn_conv_b", "ln2_g", "ln2_b"]
WEIGHTS = ["ln0_g", "ln0_b", "w_ada", "b_ada", "w_in", "dn_conv", "dn_a_log", "dn_dt_bias", "dn_norm_g",
           "gla_w_gate2", "gla_b_gate", "gla_norm_g", "w_o", "ln1_g", "ln1_b", "ffn_w_up", "ffn_conv", "ffn_conv_b",
           "ffn_w_down", "ln2_g", "ln2_b"]


def kernel(x, c, ln0_g, ln0_b, w_ada, b_ada, w_in, dn_conv, dn_a_log, dn_dt_bias, dn_norm_g, gla_w_gate2, gla_b_gate, gla_norm_g, w_o, ln1_g, ln1_b, ffn_w_up, ffn_conv, ffn_conv_b, ffn_w_down, ln2_g, ln2_b, loss_target, m_ln0_g, m_ln0_b, m_w_ada, m_b_ada, m_w_in, m_dn_conv, m_dn_a_log, m_dn_dt_bias, m_dn_norm_g, m_gla_w_gate2, m_gla_b_gate, m_gla_norm_g, m_w_o, m_ln1_g, m_ln1_b, m_ffn_w_up, m_ffn_conv, m_ffn_conv_b, m_ffn_w_down, m_ln2_g, m_ln2_b, v_ln0_g, v_ln0_b, v_w_ada, v_b_ada, v_w_in, v_dn_conv, v_dn_a_log, v_dn_dt_bias, v_dn_norm_g, v_gla_w_gate2, v_gla_b_gate, v_gla_norm_g, v_w_o, v_ln1_g, v_ln1_b, v_ffn_w_up, v_ffn_conv, v_ffn_conv_b, v_ffn_w_down, v_ln2_g, v_ln2_b):
    args = dict(locals())
    w_given = {n: args[n] for n in WEIGHTS}
    m_given = {n: args["m_" + n] for n in WEIGHTS}
    v_given = {n: args["v_" + n] for n in WEIGHTS}
    bsz, t_total, _ = x.shape
    ntok = bsz * t_total
    mx, my, mc = _position()
    me = _slot(mx, my, mc)

    pack1 = jnp.concatenate([c.reshape(-1), dn_conv.reshape(-1), gla_w_gate2.reshape(-1), ffn_conv.reshape(-1)])
    n1 = pack1.size
    rows1 = -(-n1 // 1024) * 8
    pack1 = jnp.pad(pack1, (0, rows1 * 128 - n1)).reshape(rows1, 128)
    got1 = _gather_small(pack1, "gather_cond").reshape(N_DEV, -1)
    o1 = bsz * D
    o2 = o1 + dn_conv.size
    o3 = o2 + gla_w_gate2.size
    c_all = got1[:, :o1].reshape(N_DEV * bsz, D)
    dn_conv_f = got1[:, o1:o2].reshape(N_DEV, DN_CONV_K, -1).transpose(1, 0, 2).reshape(DN_CONV_K, QKV_W)
    gate2_f = got1[:, o2:o3].reshape(N_DEV, GATE_RANK, -1).transpose(1, 0, 2).reshape(GATE_RANK, HEADS * GLA_KEY)
    ffn_conv_f = got1[:, o3:n1].reshape(N_DEV, FFN_CONV_K, -1).transpose(1, 0, 2).reshape(FFN_CONV_K, 2 * D_FF)

    win_t = w_in[0].T.astype(MXU_DT)
    wup_t = ffn_w_up[0].T.astype(MXU_DT)
    (win_all,) = _gather_big([win_t])
    win_p = _pad_in_rows(win_all)
    cw_p, cb_p = _ffn_pair(ffn_conv_f, 1), _ffn_pair(ffn_conv_b, 1)

    ncol = w_ada.shape[2]
    b_cols = lax.dynamic_slice_in_dim(b_ada, me * ncol, ncol, axis=1)
    mod_part = _ada_fwd(c_all, w_ada[0], b_cols)
    mod_all = _gather_small(mod_part.reshape(-1, 128), "gather_mod").reshape(N_DEV, N_DEV * bsz, ncol)
    mod = lax.dynamic_slice_in_dim(mod_all, me * bsz, bsz, axis=1).transpose(1, 0, 2).reshape(bsz, 6, 1, D)
    late = [w_o[0].astype(MXU_DT), wup_t, ffn_w_down[0].astype(MXU_DT)]
    ag_send, ag_recv, ag_src, ag_land, ag_token, _ = _exchange_start(
        "gather_start", late, [w.shape for w in late], [win_all, mod_all], by_owner=False)
    mod = mod + ag_token[0, 0]
    sh_a, sc_a, gt_a, sh_f, sc_f, gt_f = (mod[:, i] for i in range(6))

    g0, b0 = ln0_g.reshape(1, D), ln0_b.reshape(1, D)
    alog_row, dt_row = _lane_row(dn_a_log[0]), _lane_row(dn_dt_bias[0])
    grow_dn, grow_gla = jnp.tile(dn_norm_g, (1, HEADS)), jnp.tile(gla_norm_g, (1, HEADS))
    w2 = jnp.zeros((128, HEADS * HD), F32).at[SM_R:SM_R + GATE_RANK].set(_pad_heads(gate2_f, 1))
    bg = _pad_heads(gla_b_gate, 1)

    h_a = _ln0_mod(x, g0, b0, sc_a, sh_a)
    proj = _mm(h_a.reshape(ntok, D), win_p, "nt", F32, "mm_proj", tm=1024, tn=1408).reshape(bsz, t_total, P_W)
    q, k, v, gates = _dn_pre_fwd(proj, dn_conv_f, alog_row, dt_row)
    o_dn, s_dn, inv_dn = _dn_rec_fwd(q, k, v, gates)
    o_gla, s_gla = _gla_rec_fwd(proj, w2, bg)
    o_mix = _mix_out_fwd(o_dn, o_gla, proj, grow_dn, grow_gla)
    late, landed = _exchange_wait("gather_wait", ag_send, ag_recv, ag_src, ag_land, o_mix, by_owner=False)
    wo_all, wup_all, wdn_all = (lax.dynamic_update_slice(l, w[None], (me, 0, 0)) for l, w in zip(landed, late))
    wo_f = wo_all.reshape(D, D)
    wup_f = _ffn_pair(wup_all.reshape(2 * D_FF, D), 0)
    wdn_f = wdn_all.reshape(D_FF, D)
    y = _mm(o_mix.reshape(ntok, D), wo_f, "nn", MXU_DT, "mm_wo", tm=1024, tn=1024).reshape(bsz, t_total, D)
    r1, h_f = _res_ln_mod(x, y, gt_a, g0, b0, ln1_g, ln1_b, sc_f, sh_f)
    up, act = _ffn_up_act(h_f, wup_f, cw_p, cb_p)
    y2 = _mm(act.reshape(ntok, D_FF), wdn_f, "nn", MXU_DT, "mm_down", tm=1024, tn=1024).reshape(bsz, t_total, D)
    loss_rows, dr2, dy2, dgt_f, d_ln2_g, d_ln2_b = _final_fwd_bwd(r1, y2, gt_f, ln1_g, ln1_b, ln2_g, ln2_b, loss_target)
    loss_part = (0.5 / D) * jnp.sum(loss_rows)

    dy2_2 = dy2.reshape(ntok, D)
    g_wdn = _mm(act.reshape(ntok, D_FF), dy2_2, "tn", MXU_DT, "mm_gwdn", tm=1408, tn=1024)
    dup, d_cw_p, d_cb_p = _ffn_act_bwd(up, dy2, wdn_f, cw_p, cb_p)
    d_ffn_conv, d_ffn_conv_b = _ffn_unpair(d_cw_p, 1), _ffn_unpair(d_cb_p, 1)
    dup_2 = dup.reshape(ntok, 2 * D_FF)
    dh_f = _mm(dup_2, wup_f, "nn", MXU_DT, "mm_dhf", tn=1024).reshape(bsz, t_total, D)
    g_wup_t = _mm(dup_2, h_f.reshape(ntok, D), "tn", MXU_DT, "mm_gwup", tm=1408, tn=1024)
    ffn_parts = [_ffn_unpair(g_wup_t, 0).reshape(N_DEV, -1, D), g_wdn.reshape(N_DEV, -1, D)]
    rs_send, rs_recv, rs_src, rs_land, rs_token, _ = _exchange_start(
        "scatter_start", ffn_parts, [p.shape[1:] for p in ffn_parts], [dh_f], by_owner=True)
    dr1, dsc_f, dsh_f, d_ln1_g, d_ln1_b, dy, dgt_a = _ln_bwd_call(
        "ln1_bwd", dr2, dh_f, r1, ln1_g, ln1_b, sc_f + rs_token[0, 0], y=y, gt=gt_a)

    dy_2 = dy.reshape(ntok, D)
    do = _mm(dy_2, wo_f, "nt", MXU_DT, "mm_do", tm=1024, tn=1024).reshape(bsz, t_total, D)
    g_wo = _mm(o_mix.reshape(ntok, D), dy_2, "tn", MXU_DT, "mm_gwo", tm=512, tn=1024)
    do_dn, do_gla, dz, dgg, d_dn_norm, d_gla_norm = _mix_out_bwd(do, o_dn, o_gla, proj, grow_dn, grow_gla)
    dq, dk, dv, dgates = _dn_rec_bwd(q, k, v, gates, s_dn, inv_dn, do_dn)
    dqkv, dsm_dn, d_dn_conv, d_alog_row, d_dt_row = _dn_pre_bwd(proj, dq, dk, dv, dgates, dn_conv_f, alog_row, dt_row)
    dgq, dgk, dgv, dsm, d_w2, d_bg = _gla_rec_bwd(proj, w2, bg, s_gla, do_gla, dsm_dn)
    dproj = jnp.concatenate([dqkv, dz, dgq, dgk, dgv, dgg, dsm], axis=-1).reshape(ntok, P_W)
    g_win_p = _mm(dproj, h_a.reshape(ntok, D), "tn", MXU_DT, "mm_gwin", tm=1408, tn=1024)
    mix_parts = [_unpad_in_rows(g_win_p), g_wo.reshape(N_DEV, -1, D)]
    rs2_send, rs2_recv, rs2_src, rs2_land, rs2_token, (win_p_late,) = _exchange_start(
        "scatter_mix_start", mix_parts, [p.shape[1:] for p in mix_parts], [], by_owner=True, carry=[win_p])
    dh_a = _mm(dproj, win_p_late, "nn", MXU_DT, "mm_dha", tn=1024).reshape(bsz, t_total, D)
    grad_x, dsc_a, dsh_a, d_ln0_g, d_ln0_b = _ln_bwd_call(
        "ln0_bwd", dr1, dh_a, x, g0, b0, sc_a + rs2_token[0, 0])

    def owner_sum(landed, parts, tag):
        full = [lax.dynamic_update_slice(l, lax.dynamic_slice_in_dim(p, me, 1, axis=0), (me, 0, 0))
                for l, p in zip(landed, parts)]
        return [_sum_slots(f, f"sum_{tag}_{i}") for i, f in enumerate(full)]

    delta, new_m, new_v = {}, {}, {}
    flip = lambda a: jnp.swapaxes(a, 1, 2)

    def update(n, g):
        if g.shape == w_given[n].shape[1:]:
            delta[n], new_m[n], new_v[n] = _adamw(w_given[n], g[None], m_given[n], v_given[n], "adamw_" + n)
        else:
            upd = _adamw(flip(w_given[n]), g[None], flip(m_given[n]), flip(v_given[n]), "adamw_" + n)
            delta[n], new_m[n], new_v[n] = (flip(a) for a in upd)

    ffn_parts, ffn_landed = _exchange_wait("scatter_wait", rs_send, rs_recv, rs_src, rs_land, grad_x, by_owner=True)
    g_wup_ts, g_wdn_s = owner_sum(ffn_landed, ffn_parts, "ffn")
    update("ffn_w_up", g_wup_ts)
    update("ffn_w_down", g_wdn_s)
    ffn_done = 0.0 * (new_v["ffn_w_up"][0, 0, 0] + new_v["ffn_w_down"][0, 0, 0])

    dmod = jnp.concatenate([dsh_a, dsc_a, dgt_a, dsh_f, dsc_f, dgt_f], axis=1).reshape(-1)
    small_parts = {
        "ln0_g": d_ln0_g, "ln0_b": d_ln0_b, "ln1_g": d_ln1_g, "ln1_b": d_ln1_b, "ln2_g": d_ln2_g, "ln2_b": d_ln2_b,
        "dn_a_log": d_alog_row[:, :HEADS], "dn_dt_bias": d_dt_row[:, :HEADS],
        "dn_norm_g": d_dn_norm, "gla_norm_g": d_gla_norm, "gla_b_gate": _unpad_heads(d_bg, 1),
        "ffn_conv_b": d_ffn_conv_b, "dn_conv": d_dn_conv,
        "gla_w_gate2": _unpad_heads(d_w2[SM_R:SM_R + GATE_RANK], 1), "ffn_conv": d_ffn_conv}
    order = sorted(small_parts)
    flat = jnp.concatenate([small_parts[n].reshape(-1) for n in order] + [(loss_part + ffn_done).reshape(1), dmod])
    n3 = flat.size
    rows3 = -(-n3 // 1024) * 8
    pack3 = jnp.pad(flat, (0, rows3 * 128 - n3)).reshape(rows3, 128)
    got3 = _gather_small(pack3, "gather_small_grads")
    tot3 = _sum_slots(got3, "sum_small_grads").reshape(-1)
    grads = {}
    off = 0
    for n in order:
        size = small_parts[n].size
        grads[n] = tot3[off:off + size]
        off += size
    loss = tot3[off]
    off += 1
    dmod_all = got3.reshape(N_DEV, -1)[:, off:off + dmod.size].reshape(N_DEV * bsz, 6 * D)
    dmod_cols = lax.dynamic_slice_in_dim(dmod_all, me * ncol, ncol, axis=1)
    g_wada, g_bada = _ada_bwd(c_all, dmod_all, dmod_cols)
    grads["b_ada"] = g_bada

    def col_shard(full, rows):
        part = full.reshape(rows, -1)
        width = part.shape[1] // N_DEV
        return lax.dynamic_slice_in_dim(part, me * width, width, axis=1)

    grads["dn_conv"] = col_shard(grads["dn_conv"], DN_CONV_K)
    grads["gla_w_gate2"] = col_shard(grads["gla_w_gate2"], GATE_RANK)
    grads["ffn_conv"] = col_shard(grads["ffn_conv"], FFN_CONV_K)
    grads = {n: g.reshape(w_given[n].shape) for n, g in grads.items()}
    mix_parts, mix_landed = _exchange_wait("scatter_mix_wait", rs2_send, rs2_recv, rs2_src, rs2_land, grad_x,
                                           by_owner=True)
    g_win_t, g_wo_s = owner_sum(mix_landed, mix_parts, "mix")
    grads["w_ada"] = g_wada.reshape(w_ada.shape)
    grads["w_in"] = g_win_t.T.reshape(w_in.shape)
    grads["w_o"] = g_wo_s.reshape(w_o.shape)
    grads["ffn_w_up"] = g_wup_ts.T.reshape(ffn_w_up.shape)
    grads["ffn_w_down"] = g_wdn_s.reshape(ffn_w_down.shape)

    update("w_ada", g_wada)
    update("w_o", g_wo_s)
    update("w_in", g_win_t)
    d_s, m_s, v_s = _adamw_many(*[[src[n] for n in SMALL_NAMES] for src in (w_given, grads, m_given, v_given)],
                                "adamw_small")
    for i, n in enumerate(SMALL_NAMES):
        delta[n], new_m[n], new_v[n] = d_s[i], m_s[i], v_s[i]

    return (loss, grad_x, *[grads[n] for n in WEIGHTS], *[delta[n] for n in WEIGHTS],
            *[new_m[n] for n in WEIGHTS], *[new_v[n] for n in WEIGHTS])
```

```python
import functools

import jax
import jax.numpy as jnp
from jax import lax
from jax.experimental import pallas as pl
from jax.experimental.pallas import tpu as pltpu

F32 = jnp.float32
MXU_DT = jnp.bfloat16
MESH = pl.DeviceIdType.MESH
N_DEV = 8

D = 1024
HEADS = 4
HD = 128
CHUNK = 64
GLA_KEY = 64
GLA_TAU = 16.0
GATE_RANK = 16
D_FF = 2816
IN_W = 3608
ALPHA = 2.0 ** 0.25
EPS = 1e-6
DN_CONV_K = 4
FFN_CONV_K = 3
HALO = 8
ROW_TILE = 512
FFN_ROW_TILE = 256

P_QKV, P_Z, P_GQ, P_GK, P_GV, P_GG, P_SM, P_W = 0, 1536, 2048, 2560, 3072, 3584, 4096, 4224
SM_A, SM_B, SM_R = 0, 4, 8

ADAM_LR, ADAM_B1, ADAM_B2, ADAM_EPS, ADAM_WD, ADAM_STEP = 0.001, 0.9, 0.999, 1e-08, 0.01, 10

VMEM_LIMIT_V7X = 56 * 1024 * 1024


def _params(sem=None):
    return pltpu.CompilerParams(dimension_semantics=sem, vmem_limit_bytes=VMEM_LIMIT_V7X)


def _dg(a, b, dims, prec=None):
    return lax.dot_general(a, b, (dims, ((), ())), precision=prec, preferred_element_type=F32)


def _dot(a, b, prec=None):
    return _dg(a, b, ((1,), (0,)), prec)


def _dot_nt(a, b, prec=None):
    return _dg(a, b, ((1,), (1,)), prec)


def _dot_tn(a, b, prec=None):
    return _dg(a, b, ((0,), (0,)), prec)


def _iota(shape, dim):
    return lax.broadcasted_iota(jnp.int32, shape, dim)


def _sigmoid(x):
    return jax.nn.sigmoid(x)


def _silu(x):
    return x * _sigmoid(x)


def _softplus(x):
    return jnp.maximum(x, 0.0) + jnp.log(1.0 + jnp.exp(-jnp.abs(x)))


def _ln_stats(x):
    mu = jnp.mean(x, axis=-1, keepdims=True)
    xc = x - mu
    rstd = lax.rsqrt(jnp.mean(xc * xc, axis=-1, keepdims=True) + EPS)
    return xc * rstd, rstd


def _ln_bwd(dxhat, xhat, rstd):
    return rstd * (dxhat - jnp.mean(dxhat, axis=-1, keepdims=True)
                   - xhat * jnp.mean(dxhat * xhat, axis=-1, keepdims=True))


NN, NT, TN = ((1,), (0,)), ((1,), (1,)), ((0,), (0,))


def _split2(a):
    hi = a.astype(jnp.bfloat16)
    return hi, (a - hi.astype(F32)).astype(jnp.bfloat16)


def _d3(a, b, dims):
    ah, al = _split2(a)
    bh, bl = _split2(b)
    return _dg(ah, bh, dims) + (_dg(ah, bl, dims) + _dg(al, bh, dims))


@jax.custom_vjp
def _dot3(a, b):
    return _d3(a, b, NN)


_dot3.defvjp(lambda a, b: (_d3(a, b, NN), (a, b)),
             lambda res, g: (_d3(g, res[1], NT), _d3(res[0], g, TN)))


def _split3(b):
    b1 = b.astype(jnp.bfloat16)
    r1 = b - b1.astype(F32)
    b2 = r1.astype(jnp.bfloat16)
    return b1, b2, (r1 - b2.astype(F32)).astype(jnp.bfloat16)


def _sum3(fn, b):
    b1, b2, b3 = _split3(b)
    return fn(b1) + (fn(b2) + fn(b3))


@jax.custom_vjp
def _mask_dot(e, b):
    return _sum3(lambda t: _dg(e, t, NN), b)


_mask_dot.defvjp(lambda e, b: (_mask_dot(e, b), e),
                 lambda e, g: (jnp.zeros_like(e), _sum3(lambda t: _dg(e, t, TN), g)))


@jax.custom_vjp
def _mask_dot_nt(e, b):
    return _sum3(lambda t: _dg(e, t, NT), b)


_mask_dot_nt.defvjp(lambda e, b: (_mask_dot_nt(e, b), e),
                    lambda e, g: (jnp.zeros_like(e), _sum3(lambda t: _dg(t, e, TN), g)))


def _tri_inv_impl(ms):
    n = ms[0].shape[0]
    r, c = _iota((n, n), 0), _iota((n, n), 1)
    eye = (r == c).astype(F32)
    diag = (r >> 3) == (c >> 3)
    ds = [jnp.where(diag, m, 0.0) for m in ms]
    d2s = [_d3(d, d, NN) for d in ds]
    d4s = [_d3(d2, d2, NN) for d2 in d2s]
    invs = [_d3(eye - d, eye + d2, NN) for d, d2 in zip(ds, d2s)]
    invs = [_d3(inv, eye + d4, NN) for inv, d4 in zip(invs, d4s)]
    shift = 3
    while (1 << shift) < n:
        rb, cb = r >> shift, c >> shift
        sel = ((rb & 1) == 1) & (cb == rb - 1)
        tmp = [_d3(inv, jnp.where(sel, m, 0.0), NN) for inv, m in zip(invs, ms)]
        invs = [inv - _d3(t, inv, NN) for t, inv in zip(tmp, invs)]
        shift += 1
    return invs


@jax.custom_vjp
def _tri_inv(ms):
    return _tri_inv_impl(ms)


def _tri_inv_fwd(ms):
    invs = _tri_inv_impl(ms)
    return invs, invs


def _tri_inv_bwd(invs, das):
    tmp = [_d3(a, da, TN) for a, da in zip(invs, das)]
    return ([-_d3(t, a, NT) for t, a in zip(tmp, invs)],)


_tri_inv.defvjp(_tri_inv_fwd, _tri_inv_bwd)


@jax.custom_vjp
def _tri_inv_known(ms, invs):
    return invs


_tri_inv_known.defvjp(lambda ms, invs: (invs, invs),
                      lambda invs, das: (_tri_inv_bwd(invs, das)[0], [jnp.zeros_like(a) for a in invs]))


def _dn_chunk(s_list, q, k, v, gates, inv_known=None, with_inv=False):
    nb = len(q)
    c = q[0].shape[0]
    r64, c64 = _iota((c, c), 0), _iota((c, c), 1)
    causal = r64 >= c64
    strict = r64 > c64
    tri = causal.astype(jnp.bfloat16)
    eye = (_iota((HD, HD), 0) == _iota((HD, HD), 1)).astype(jnp.bfloat16)
    lane = _iota(gates[0].shape, 1)
    lane1 = _iota((1, HD), 1)
    g_all = [_mask_dot(tri, g) for g in gates]
    g_all_t = [_mask_dot_nt(eye, g) for g in g_all]
    row = _iota(g_all_t[0].shape, 0)
    last = [jnp.sum(g, axis=0, keepdims=True) for g in gates]
    prob = [(b, h) for b in range(nb) for h in range(HEADS)]
    sl = [slice(h * HD, (h + 1) * HD) for h in range(HEADS)]
    qh = [q[b][:, sl[h]] for b, h in prob]
    kh = [k[b][:, sl[h]] for b, h in prob]
    vh = [v[b][:, sl[h]] for b, h in prob]
    s = [s_list[b][h] for b, h in prob]
    beta = [jnp.sum(jnp.where(lane == SM_B + h, gates[b], 0.0), axis=-1, keepdims=True) for b, h in prob]
    g_c = [jnp.sum(jnp.where(lane == SM_A + h, g_all[b], 0.0), axis=-1, keepdims=True) for b, h in prob]
    g_r = [jnp.sum(jnp.where(row == SM_A + h, g_all_t[b], 0.0), axis=0, keepdims=True) for b, h in prob]
    g_last = [jnp.sum(jnp.where(lane1 == SM_A + h, last[b], 0.0), axis=-1, keepdims=True) for b, h in prob]
    decay = [jnp.where(causal, jnp.exp(jnp.where(causal, gc - gr, 0.0)), 0.0) for gc, gr in zip(g_c, g_r)]
    kb = [k_ * b_ for k_, b_ in zip(kh, beta)]
    m_low = [jnp.where(strict, _dot_nt(kb_, k_) * d_, 0.0) for kb_, k_, d_ in zip(kb, kh, decay)]
    attn = [_dot_nt(q_, k_) * d_ for q_, k_, d_ in zip(qh, kh, decay)]
    a_inv = _tri_inv(m_low) if inv_known is None else _tri_inv_known(m_low, inv_known)
    eg = [jnp.exp(gc) for gc in g_c]
    uw = [_dot3(a_, jnp.concatenate([v_ * b_, kb_ * e_], axis=1))
          for a_, v_, b_, kb_, e_ in zip(a_inv, vh, beta, kb, eg)]
    v_new = [uw_[:, :HD] - _dot(uw_[:, HD:], s_) for uw_, s_ in zip(uw, s)]
    qs = [_dot(q_ * e_, s_) for q_, e_, s_ in zip(qh, eg, s)]
    o = [qs_ + _dot(a_, vn_) for qs_, a_, vn_ in zip(qs, attn, v_new)]
    k_dec = [k_ * jnp.exp(gl - gc) for k_, gl, gc in zip(kh, g_last, g_c)]
    s_new = [s_ * jnp.exp(gl) + _dot_tn(kd_, vn_) for s_, gl, kd_, vn_ in zip(s, g_last, k_dec, v_new)]
    outs = [jnp.concatenate(o[b * HEADS:(b + 1) * HEADS], axis=-1) for b in range(nb)]
    states = [s_new[b * HEADS:(b + 1) * HEADS] for b in range(nb)]
    return (outs, states, a_inv) if with_inv else (outs, states)


def _gla_chunk(st_list, q, k, v, small, w2, bg):
    nb = len(q)
    c = q[0].shape[0]
    causal = _iota((c, c), 0) >= _iota((c, c), 1)
    tri = causal.astype(jnp.bfloat16)
    la_all = [-_softplus(-(_dot(sm, w2) + bg)) * (1.0 / GLA_TAU) for sm in small]
    b_all = [_mask_dot(tri, la) for la in la_all]
    prob = [(b, h) for b in range(nb) for h in range(HEADS)]
    sl = [slice(h * HD, (h + 1) * HD) for h in range(HEADS)]
    kh = [k[b][:, sl[h]] for b, h in prob]
    vh = [v[b][:, sl[h]] for b, h in prob]
    st = [st_list[b][h] for b, h in prob]
    bc = [b_all[b][:, sl[h]] for b, h in prob]
    b_last = [jnp.sum(la_all[b][:, sl[h]], axis=0, keepdims=True) for b, h in prob]
    q_dec = [q[b][:, sl[h]] * (GLA_KEY ** -0.5) * jnp.exp(bc_) for (b, h), bc_ in zip(prob, bc)]
    attn = [jnp.where(causal, _dot_nt(qd, k_ * jnp.exp(-bc_)), 0.0) for qd, k_, bc_ in zip(q_dec, kh, bc)]
    inter = [_dot_nt(qd, st_) for qd, st_ in zip(q_dec, st)]
    o = [i_ + _dot(a_, v_) for i_, a_, v_ in zip(inter, attn, vh)]
    k_dec = [k_ * jnp.exp(bl - bc_) for k_, bl, bc_ in zip(kh, b_last, bc)]
    s_new = [st_ * jnp.exp(bl) + _dot_tn(v_, kd) for st_, bl, v_, kd in zip(st, b_last, vh, k_dec)]
    outs = [jnp.concatenate(o[b * HEADS:(b + 1) * HEADS], axis=-1) for b in range(nb)]
    return outs, [s_new[b * HEADS:(b + 1) * HEADS] for b in range(nb)]


def _dn_qkv(y):
    act = _silu(y)
    parts = []
    for i in range(2 * HEADS):
        xh = act[:, i * HD:(i + 1) * HD]
        xh = xh * lax.rsqrt(jnp.sum(xh * xh, axis=-1, keepdims=True) + EPS)
        parts.append(xh * (HD ** -0.5) if i < HEADS else xh)
    qk = jnp.concatenate(parts, axis=-1)
    return qk[:, :HEADS * HD], qk[:, HEADS * HD:], act[:, 2 * HEADS * HD:]


def _dn_gates(small, alog_row, dt_row):
    lane = _iota(small.shape, 1)
    log_a = -jnp.exp(alog_row) * _softplus(small + dt_row)
    return jnp.where(lane < SM_B, log_a, jnp.where(lane < SM_R, _sigmoid(small), 0.0))


def _gate_norm(o, z, grow):
    parts = []
    for h in range(HEADS):
        oh = o[:, h * HD:(h + 1) * HD]
        parts.append(oh * lax.rsqrt(jnp.mean(oh * oh, axis=-1, keepdims=True) + EPS))
    return jnp.concatenate(parts, axis=-1) * grow * _silu(z)


def _conv_rows(xrows, w_ref, k_taps):
    n = xrows.shape[0]
    acc = xrows * w_ref[k_taps - 1:k_taps, :]
    for s in range(1, k_taps):
        acc = acc + pltpu.roll(xrows, s, 0) * w_ref[k_taps - 1 - s:k_taps - s, :]
    return acc


def _shift_up(x, s):
    return x if s == 0 else pltpu.roll(x, x.shape[0] - s, 0)


def _div_tile(n, cap, mult=8):
    best = None
    for t in range(mult, min(n, cap) + 1, mult):
        if n % t == 0:
            best = t
    return best if best is not None else n


def _halo_prev(tt):
    return lambda b, t: (b, jnp.maximum(t * (tt // HALO) - 1, 0))


def _halo_next(tt, t_total):
    return lambda b, t: (b, jnp.minimum((t + 1) * (tt // HALO), t_total // HALO - 1))


def _mm(a, b, mode, out_dtype, name, tm=512, tn=512, tk=None):
    if mode == "nn":
        (m, k), n = a.shape, b.shape[1]
    elif mode == "nt":
        (m, k), n = a.shape, b.shape[0]
    else:
        (k, m), n = a.shape, b.shape[1]
    tm, tn = min(tm, m), min(tn, n)
    tk = k if tk is None else min(tk, k)
    assert m % tm == 0 and n % tn == 0 and k % tk == 0, (name, a.shape, b.shape, tm, tn, tk)
    nk = k // tk
    if mode == "tn":
        a_spec = pl.BlockSpec((tk, tm), lambda i, j, kk: (kk, i))
    else:
        a_spec = pl.BlockSpec((tm, tk), lambda i, j, kk: (i, kk))
    if mode == "nt":
        b_spec = pl.BlockSpec((tn, tk), lambda i, j, kk: (j, kk))
    else:
        b_spec = pl.BlockSpec((tk, tn), lambda i, j, kk: (kk, j))
    dims = {"nn": ((1,), (0,)), "nt": ((1,), (1,)), "tn": ((0,), (0,))}[mode]

    def body(a_ref, b_ref, o_ref, *acc):
        p = _dg(a_ref[...], b_ref[...], dims)
        if nk == 1:
            o_ref[...] = p.astype(out_dtype)
        else:
            kk = pl.program_id(2)

            @pl.when(kk == 0)
            def _():
                acc[0][...] = p

            @pl.when(kk > 0)
            def _():
                acc[0][...] += p

            @pl.when(kk == nk - 1)
            def _():
                o_ref[...] = acc[0][...].astype(out_dtype)

    return pl.pallas_call(
        body, name=name, grid=(m // tm, n // tn, nk),
        in_specs=[a_spec, b_spec],
        out_specs=pl.BlockSpec((tm, tn), lambda i, j, kk: (i, j)),
        out_shape=jax.ShapeDtypeStruct((m, n), out_dtype),
        scratch_shapes=[pltpu.VMEM((tm, tn), F32)] if nk > 1 else [],
        compiler_params=_params(("parallel", "parallel", "arbitrary")),
    )(a, b)


def _ada_fwd(c_all, w_ada, b_cols):
    def body(c_ref, w_ref, b_ref, o_ref):
        cond = _silu(c_ref[...]).astype(MXU_DT)
        o_ref[...] = _dot(cond, w_ref[...].astype(MXU_DT)) + b_ref[...]

    return pl.pallas_call(body, name="ada_fwd", out_shape=jax.ShapeDtypeStruct((c_all.shape[0], w_ada.shape[1]), F32),
                          compiler_params=_params())(c_all, w_ada, b_cols)


def _ada_bwd(c_all, dmod_all, dmod_cols):
    def body(c_ref, da_ref, dc_ref, gw_ref, gb_ref):
        cond = _silu(c_ref[...]).astype(MXU_DT)
        gw_ref[...] = _dot_tn(cond, dc_ref[...].astype(MXU_DT))
        gb_ref[...] = jnp.sum(da_ref[...], axis=0, keepdims=True)

    return pl.pallas_call(
        body, name="ada_bwd",
        out_shape=(jax.ShapeDtypeStruct((c_all.shape[1], dmod_cols.shape[1]), F32),
                   jax.ShapeDtypeStruct((1, dmod_all.shape[1]), F32)),
        compiler_params=_params())(c_all, dmod_all, dmod_cols)


def _tok_spec(tt, width=D):
    return pl.BlockSpec((1, tt, width), lambda b, t: (b, t, 0))


def _vec_spec(width=D):
    return pl.BlockSpec((1, width), lambda b, t: (0, 0))


def _bvec_spec(width=D):
    return pl.BlockSpec((1, 1, width), lambda b, t: (b, 0, 0))


def _ln0_mod(x, g0, b0, sc, sh):
    bsz, t_total, _ = x.shape
    tt = _div_tile(t_total, ROW_TILE)

    def body(x_ref, g_ref, b_ref, sc_ref, sh_ref, h_ref):
        xh, _ = _ln_stats(x_ref[0])
        x0 = xh * g_ref[...] + b_ref[...]
        h_ref[0] = (x0 * (1.0 + sc_ref[0]) + sh_ref[0]).astype(MXU_DT)

    return pl.pallas_call(
        body, name="ln0_mod", grid=(bsz, t_total // tt),
        in_specs=[_tok_spec(tt), _vec_spec(), _vec_spec(), _bvec_spec(), _bvec_spec()],
        out_specs=_tok_spec(tt), out_shape=jax.ShapeDtypeStruct(x.shape, MXU_DT),
        compiler_params=_params(("parallel", "parallel")))(x, g0, b0, sc, sh)


def _res_ln_mod(x, y, gt, g0, b0, g1, b1, sc, sh):
    bsz, t_total, _ = x.shape
    tt = _div_tile(t_total, ROW_TILE)

    def body(x_ref, y_ref, gt_ref, g0_ref, b0_ref, g1_ref, b1_ref, sc_ref, sh_ref, r_ref, h_ref):
        xh, _ = _ln_stats(x_ref[0])
        r = ALPHA * (xh * g0_ref[...] + b0_ref[...]) + (1.0 + gt_ref[0]) * y_ref[0].astype(F32)
        r_ref[0] = r
        rh, _ = _ln_stats(r)
        x1 = rh * g1_ref[...] + b1_ref[...]
        h_ref[0] = (x1 * (1.0 + sc_ref[0]) + sh_ref[0]).astype(MXU_DT)

    return pl.pallas_call(
        body, name="res_ln_mod", grid=(bsz, t_total // tt),
        in_specs=[_tok_spec(tt), _tok_spec(tt), _bvec_spec(), _vec_spec(), _vec_spec(), _vec_spec(), _vec_spec(),
                  _bvec_spec(), _bvec_spec()],
        out_specs=(_tok_spec(tt), _tok_spec(tt)),
        out_shape=(jax.ShapeDtypeStruct(x.shape, F32), jax.ShapeDtypeStruct(x.shape, MXU_DT)),
        compiler_params=_params(("parallel", "parallel")))(x, y, gt, g0, b0, g1, b1, sc, sh)


def _final_fwd_bwd(r1, y2, gt, g1, b1, g2, b2, target):
    bsz, t_total, _ = r1.shape
    tt = _div_tile(t_total, ROW_TILE)

    def body(r1_ref, y2_ref, gt_ref, g1_ref, b1_ref, g2_ref, b2_ref, tg_ref,
             loss_ref, dr2_ref, dy2_ref, dgt_ref, dg2_ref, db2_ref):
        b, t = pl.program_id(0), pl.program_id(1)

        @pl.when((b == 0) & (t == 0))
        def _():
            loss_ref[...] = jnp.zeros_like(loss_ref)
            dg2_ref[...] = jnp.zeros_like(dg2_ref)
            db2_ref[...] = jnp.zeros_like(db2_ref)

        @pl.when(t == 0)
        def _():
            dgt_ref[...] = jnp.zeros_like(dgt_ref)

        rh1, _ = _ln_stats(r1_ref[0])
        x1 = rh1 * g1_ref[...] + b1_ref[...]
        y2 = y2_ref[0].astype(F32)
        gate = 1.0 + gt_ref[0]
        xh2, rstd2 = _ln_stats(ALPHA * x1 + gate * y2)
        err = xh2 * g2_ref[...] + b2_ref[...] - tg_ref[0]
        loss_ref[...] += jnp.sum(err * err, axis=0, keepdims=True)
        dx2 = err * (1.0 / D)
        dg2_ref[...] += jnp.sum(dx2 * xh2, axis=0, keepdims=True)
        db2_ref[...] += jnp.sum(dx2, axis=0, keepdims=True)
        dr2 = _ln_bwd(dx2 * g2_ref[...], xh2, rstd2)
        dr2_ref[0] = dr2
        dy2_ref[0] = (gate * dr2).astype(MXU_DT)
        dgt_ref[0] += jnp.sum(dr2 * y2, axis=0, keepdims=True)

    vec_out = jax.ShapeDtypeStruct((1, D), F32)
    return pl.pallas_call(
        body, name="final_fwd_bwd", grid=(bsz, t_total // tt),
        in_specs=[_tok_spec(tt), _tok_spec(tt), _bvec_spec(), _vec_spec(), _vec_spec(), _vec_spec(), _vec_spec(),
                  _tok_spec(tt)],
        out_specs=(_vec_spec(), _tok_spec(tt), _tok_spec(tt), _bvec_spec(), _vec_spec(), _vec_spec()),
        out_shape=(vec_out, jax.ShapeDtypeStruct(r1.shape, F32), jax.ShapeDtypeStruct(r1.shape, MXU_DT),
                   jax.ShapeDtypeStruct((bsz, 1, D), F32), vec_out, vec_out),
        compiler_params=_params(("arbitrary", "arbitrary")))(r1, y2, gt, g1, b1, g2, b2, target)


def _ln_bwd_call(name, d_res, d_h, src, g, b, sc, y=None, gt=None):
    bsz, t_total, _ = src.shape
    tt = _div_tile(t_total, ROW_TILE)
    has_y = y is not None

    def body(*refs):
        if has_y:
            (dres_ref, dh_ref, src_ref, g_ref, b_ref, sc_ref, y_ref, gt_ref,
             dsrc_ref, dsc_ref, dsh_ref, dg_ref, db_ref, dy_ref, dgt_ref) = refs
        else:
            (dres_ref, dh_ref, src_ref, g_ref, b_ref, sc_ref,
             dsrc_ref, dsc_ref, dsh_ref, dg_ref, db_ref) = refs
        bi, t = pl.program_id(0), pl.program_id(1)

        @pl.when((bi == 0) & (t == 0))
        def _():
            dg_ref[...] = jnp.zeros_like(dg_ref)
            db_ref[...] = jnp.zeros_like(db_ref)

        @pl.when(t == 0)
        def _():
            dsc_ref[...] = jnp.zeros_like(dsc_ref)
            dsh_ref[...] = jnp.zeros_like(dsh_ref)
            if has_y:
                dgt_ref[...] = jnp.zeros_like(dgt_ref)

        xh, rstd = _ln_stats(src_ref[0])
        xv = xh * g_ref[...] + b_ref[...]
        dh = dh_ref[0].astype(F32)
        dx = ALPHA * dres_ref[0] + dh * (1.0 + sc_ref[0])
        dsc_ref[0] += jnp.sum(dh * xv, axis=0, keepdims=True)
        dsh_ref[0] += jnp.sum(dh, axis=0, keepdims=True)
        dg_ref[...] += jnp.sum(dx * xh, axis=0, keepdims=True)
        db_ref[...] += jnp.sum(dx, axis=0, keepdims=True)
        dsrc = _ln_bwd(dx * g_ref[...], xh, rstd)
        dsrc_ref[0] = dsrc
        if has_y:
            dy_ref[0] = ((1.0 + gt_ref[0]) * dsrc).astype(MXU_DT)
            dgt_ref[0] += jnp.sum(dsrc * y_ref[0].astype(F32), axis=0, keepdims=True)

    vec_out = jax.ShapeDtypeStruct((1, D), F32)
    bvec_out = jax.ShapeDtypeStruct((bsz, 1, D), F32)
    in_specs = [_tok_spec(tt), _tok_spec(tt), _tok_spec(tt), _vec_spec(), _vec_spec(), _bvec_spec()]
    out_specs = [_tok_spec(tt), _bvec_spec(), _bvec_spec(), _vec_spec(), _vec_spec()]
    out_shape = [jax.ShapeDtypeStruct(src.shape, F32), bvec_out, bvec_out, vec_out, vec_out]
    args = [d_res, d_h, src, g, b, sc]
    if has_y:
        in_specs += [_tok_spec(tt), _bvec_spec()]
        out_specs += [_tok_spec(tt), _bvec_spec()]
        out_shape += [jax.ShapeDtypeStruct(src.shape, MXU_DT), bvec_out]
        args += [y, gt]
    return pl.pallas_call(body, name=name, grid=(bsz, t_total // tt), in_specs=in_specs, out_specs=tuple(out_specs),
                          out_shape=tuple(out_shape), compiler_params=_params(("arbitrary", "arbitrary")))(*args)


FFN_TC = 256
FFN_NJ = D_FF // FFN_TC
FFN_PW = 2 * FFN_TC


def _ffn_pair(a, axis):
    shp = list(a.shape)
    a4 = a.reshape(shp[:axis] + [2, FFN_NJ, FFN_TC] + shp[axis + 1:])
    return jnp.swapaxes(a4, axis, axis + 1).reshape(shp)


def _ffn_unpair(a, axis):
    shp = list(a.shape)
    a4 = a.reshape(shp[:axis] + [FFN_NJ, 2, FFN_TC] + shp[axis + 1:])
    return jnp.swapaxes(a4, axis, axis + 1).reshape(shp)


def _ffn_up_act(h, w_up, cw, cb):
    bsz, t_total, _ = h.shape
    tt = _div_tile(t_total, FFN_ROW_TILE)
    def body(h_ref, wu_ref, w_ref, b_ref, up_ref, o_ref, carry_ref):
        up_t = _dot_nt(h_ref[0], wu_ref[...])
        up_ref[0] = up_t
        prev = jnp.where(pl.program_id(2) == 0, 0.0, carry_ref[...])
        rows = jnp.concatenate([prev, up_t], axis=0)
        u = _conv_rows(rows, w_ref, FFN_CONV_K)[HALO:] + b_ref[...]
        o_ref[0] = (_silu(u[:, :FFN_TC]) * u[:, FFN_TC:]).astype(MXU_DT)
        carry_ref[...] = up_t[tt - HALO:, :]

    return pl.pallas_call(
        body, name="ffn_up_act", grid=(bsz, FFN_NJ, t_total // tt),
        in_specs=[pl.BlockSpec((1, tt, D), lambda b, j, t: (b, t, 0)),
                  pl.BlockSpec((FFN_PW, D), lambda b, j, t: (j, 0)),
                  pl.BlockSpec((FFN_CONV_K, FFN_PW), lambda b, j, t: (0, j)),
                  pl.BlockSpec((1, FFN_PW), lambda b, j, t: (0, j))],
        out_specs=(pl.BlockSpec((1, tt, FFN_PW), lambda b, j, t: (b, t, j)),
                   pl.BlockSpec((1, tt, FFN_TC), lambda b, j, t: (b, t, j))),
        out_shape=(jax.ShapeDtypeStruct((bsz, t_total, 2 * D_FF), F32),
                   jax.ShapeDtypeStruct((bsz, t_total, D_FF), MXU_DT)),
        scratch_shapes=[pltpu.VMEM((HALO, FFN_PW), F32)],
        compiler_params=_params(("parallel", "parallel", "arbitrary")))(h, w_up, cw, cb)


HALO16 = 16


def _ffn_act_bwd(up, dy2, w_down, cw, cb):
    bsz, t_total, width = up.shape
    tt = _div_tile(t_total, FFN_ROW_TILE)
    nt = t_total // tt
    hp, hn = _halo_prev(tt), _halo_next(tt, t_total)

    def body(x_ref, xp_ref, xn_ref, dy_ref, dyn_ref, wd_ref, w_ref, b_ref, dup_ref, dw_ref, db_ref):
        b, t = pl.program_id(1), pl.program_id(2)

        @pl.when((b == 0) & (t == 0))
        def _():
            dw_ref[...] = jnp.zeros_like(dw_ref)
            db_ref[...] = jnp.zeros_like(db_ref)

        prev = jnp.where(t == 0, 0.0, xp_ref[0])
        rows = jnp.concatenate([prev, x_ref[0], xn_ref[0]], axis=0)
        u = _conv_rows(rows, w_ref, FFN_CONV_K)[HALO:] + b_ref[...]
        g_pre, v_pre = u[:, :FFN_TC], u[:, FFN_TC:]
        valid = (_iota((tt + HALO, 1), 0) < tt) | (t < nt - 1)
        da = jnp.concatenate([_dot_nt(dy_ref[0], wd_ref[...]), _dot_nt(dyn_ref[0], wd_ref[...])[:HALO]], axis=0)
        da_ext = jnp.where(valid, da, 0.0)
        sg = _sigmoid(g_pre)
        gs = g_pre * sg
        du = jnp.concatenate([da_ext * v_pre * (sg + gs * (1.0 - sg)), da_ext * gs], axis=1)
        dup = du * w_ref[FFN_CONV_K - 1:FFN_CONV_K, :]
        for s in range(1, FFN_CONV_K):
            dup = dup + _shift_up(du, s) * w_ref[FFN_CONV_K - 1 - s:FFN_CONV_K - s, :]
        dup_ref[0] = dup[:tt].astype(MXU_DT)
        du_t = du[:tt]
        db_ref[...] += jnp.sum(du_t, axis=0, keepdims=True)
        for k in range(FFN_CONV_K):
            s = FFN_CONV_K - 1 - k
            xs = (rows if s == 0 else pltpu.roll(rows, s, 0))[HALO:HALO + tt]
            dw_ref[k:k + 1, :] += jnp.sum(du_t * xs, axis=0, keepdims=True)

    def halo(h, w):
        return pl.BlockSpec((1, HALO, w), lambda j, b, t: (*h(b, t), j))

    wspec = lambda rows_: pl.BlockSpec((rows_, FFN_PW), lambda j, b, t: (0, j))
    tile = pl.BlockSpec((1, tt, FFN_PW), lambda j, b, t: (b, t, j))
    dy_next = lambda j, b, t: (b, jnp.minimum((t + 1) * (tt // HALO16), t_total // HALO16 - 1), 0)
    return pl.pallas_call(
        body, name="ffn_act_bwd", grid=(FFN_NJ, bsz, nt),
        in_specs=[tile, halo(hp, FFN_PW), halo(hn, FFN_PW),
                  pl.BlockSpec((1, tt, D), lambda j, b, t: (b, t, 0)), pl.BlockSpec((1, HALO16, D), dy_next),
                  pl.BlockSpec((FFN_TC, D), lambda j, b, t: (j, 0)), wspec(FFN_CONV_K), wspec(1)],
        out_specs=(tile, wspec(FFN_CONV_K), wspec(1)),
        out_shape=(jax.ShapeDtypeStruct(up.shape, MXU_DT), jax.ShapeDtypeStruct((FFN_CONV_K, width), F32),
                   jax.ShapeDtypeStruct((1, width), F32)),
        compiler_params=_params(("arbitrary", "arbitrary", "arbitrary")))(up, up, up, dy2, dy2, w_down, cw, cb)


QKV_W = 3 * HEADS * HD
SM_BLK = P_SM // 128


def _dn_pre_fwd(proj, conv_w, alog_row, dt_row):
    bsz, t_total, _ = proj.shape
    tt = _div_tile(t_total, ROW_TILE)
    hp = _halo_prev(tt)

    def body(x_ref, xp_ref, sm_ref, w_ref, al_ref, dt_ref, q_ref, k_ref, v_ref, g_ref):
        prev = jnp.where(pl.program_id(1) == 0, 0.0, xp_ref[0])
        y = _conv_rows(jnp.concatenate([prev, x_ref[0]], axis=0), w_ref, DN_CONV_K)[HALO:]
        q_ref[0], k_ref[0], v_ref[0] = _dn_qkv(y)
        g_ref[0] = _dn_gates(sm_ref[0], al_ref[...], dt_ref[...])

    out512 = jax.ShapeDtypeStruct((bsz, t_total, HEADS * HD), F32)
    return pl.pallas_call(
        body, name="dn_pre_fwd", grid=(bsz, t_total // tt),
        in_specs=[pl.BlockSpec((1, tt, QKV_W), lambda b, t: (b, t, 0)),
                  pl.BlockSpec((1, HALO, QKV_W), lambda b, t: (*hp(b, t), 0)),
                  pl.BlockSpec((1, tt, 128), lambda b, t: (b, t, SM_BLK)),
                  pl.BlockSpec((DN_CONV_K, QKV_W), lambda b, t: (0, 0)), _vec_spec(128), _vec_spec(128)],
        out_specs=(_tok_spec(tt, 512), _tok_spec(tt, 512), _tok_spec(tt, 512), _tok_spec(tt, 128)),
        out_shape=(out512, out512, out512, jax.ShapeDtypeStruct((bsz, t_total, 128), F32)),
        compiler_params=_params(("parallel", "parallel")))(proj, proj, proj, conv_w, alog_row, dt_row)


def _dn_pre_bwd(proj, dq, dk, dv, dgates, conv_w, alog_row, dt_row):
    bsz, t_total, _ = proj.shape
    tt = _div_tile(t_total, 128)
    nt = t_total // tt
    hp, hn = _halo_prev(tt), _halo_next(tt, t_total)

    def body(x_ref, xp_ref, xn_ref, sm_ref, dq_ref, dqn_ref, dk_ref, dkn_ref, dv_ref, dvn_ref, dg_ref,
             w_ref, al_ref, dt_ref, dx_ref, dsm_ref, dw_ref, dal_ref, ddt_ref):
        b, t = pl.program_id(0), pl.program_id(1)

        @pl.when((b == 0) & (t == 0))
        def _():
            dw_ref[...] = jnp.zeros_like(dw_ref)
            dal_ref[...] = jnp.zeros_like(dal_ref)
            ddt_ref[...] = jnp.zeros_like(ddt_ref)

        prev = jnp.where(t == 0, 0.0, xp_ref[0])
        rows = jnp.concatenate([prev, x_ref[0], xn_ref[0]], axis=0)
        y = _conv_rows(rows, w_ref, DN_CONV_K)[HALO:]
        valid = (_iota((tt + HALO, 1), 0) < tt) | (t < nt - 1)

        def ext(tile_ref, next_ref):
            return jnp.where(valid, jnp.concatenate([tile_ref[0], next_ref[0]], axis=0), 0.0)

        _, vjp_qkv = jax.vjp(_dn_qkv, y)
        (dy,) = vjp_qkv((ext(dq_ref, dqn_ref), ext(dk_ref, dkn_ref), ext(dv_ref, dvn_ref)))
        dy = jnp.where(valid, dy, 0.0)
        dx = dy * w_ref[DN_CONV_K - 1:DN_CONV_K, :]
        for s in range(1, DN_CONV_K):
            dx = dx + _shift_up(dy, s) * w_ref[DN_CONV_K - 1 - s:DN_CONV_K - s, :]
        dx_ref[0] = dx[:tt].astype(MXU_DT)
        dy_t = dy[:tt]
        for k in range(DN_CONV_K):
            s = DN_CONV_K - 1 - k
            xs = (rows if s == 0 else pltpu.roll(rows, s, 0))[HALO:HALO + tt]
            dw_ref[k:k + 1, :] += jnp.sum(dy_t * xs, axis=0, keepdims=True)
        _, vjp_g = jax.vjp(_dn_gates, sm_ref[0], al_ref[...], dt_ref[...])
        dsm, dal, ddt = vjp_g(dg_ref[0])
        dsm_ref[0] = dsm
        dal_ref[...] += dal
        ddt_ref[...] += ddt

    def tile(width, blk=0):
        return pl.BlockSpec((1, tt, width), lambda b, t: (b, t, blk))

    def halo(h, width):
        return pl.BlockSpec((1, HALO, width), lambda b, t: (*h(b, t), 0))

    return pl.pallas_call(
        body, name="dn_pre_bwd", grid=(bsz, nt),
        in_specs=[tile(QKV_W), halo(hp, QKV_W), halo(hn, QKV_W), tile(128, SM_BLK),
                  tile(512), halo(hn, 512), tile(512), halo(hn, 512), tile(512), halo(hn, 512), tile(128),
                  pl.BlockSpec((DN_CONV_K, QKV_W), lambda b, t: (0, 0)), _vec_spec(128), _vec_spec(128)],
        out_specs=(tile(QKV_W), tile(128), pl.BlockSpec((DN_CONV_K, QKV_W), lambda b, t: (0, 0)),
                   _vec_spec(128), _vec_spec(128)),
        out_shape=(jax.ShapeDtypeStruct((bsz, t_total, QKV_W), MXU_DT), jax.ShapeDtypeStruct((bsz, t_total, 128), F32),
                   jax.ShapeDtypeStruct((DN_CONV_K, QKV_W), F32), jax.ShapeDtypeStruct((1, 128), F32),
                   jax.ShapeDtypeStruct((1, 128), F32)),
        compiler_params=_params(("arbitrary", "arbitrary")))(
            proj, proj, proj, proj, dq, dq, dk, dk, dv, dv, dgates, conv_w, alog_row, dt_row)


def _state_spec(bsz, idx):
    return pl.BlockSpec((bsz, 1, HEADS, HD, HD), lambda c: (0, idx(c), 0, 0, 0))


def _inv_spec(bsz, idx):
    return pl.BlockSpec((bsz, 1, HEADS, CHUNK, CHUNK), lambda c: (0, idx(c), 0, 0, 0))


def _chunk_spec(bsz, width, idx, blk=0):
    return pl.BlockSpec((bsz, CHUNK, width), lambda c: (0, idx(c), blk))


def _dn_rec_fwd(q, k, v, gates):
    bsz, t_total, _ = q.shape
    nc = t_total // CHUNK
    fwd = lambda c: c

    def body(q_ref, k_ref, v_ref, g_ref, o_ref, ss_ref, inv_ref, s_ref):
        @pl.when(pl.program_id(0) == 0)
        def _():
            s_ref[...] = jnp.zeros_like(s_ref)

        seqs = range(bsz)
        s_list = [[s_ref[b * HEADS + h] for h in range(HEADS)] for b in seqs]
        for b in seqs:
            for h in range(HEADS):
                ss_ref[b, 0, h] = s_list[b][h]
        o, new_s, invs = _dn_chunk(s_list, [q_ref[b] for b in seqs], [k_ref[b] for b in seqs],
                                   [v_ref[b] for b in seqs], [g_ref[b] for b in seqs], with_inv=True)
        for b in seqs:
            o_ref[b] = o[b]
            for h in range(HEADS):
                s_ref[b * HEADS + h] = new_s[b][h]
                inv_ref[b, 0, h] = invs[b * HEADS + h]

    return pl.pallas_call(
        body, name="dn_rec_fwd", grid=(nc,),
        in_specs=[_chunk_spec(bsz, 512, fwd)] * 3 + [_chunk_spec(bsz, 128, fwd)],
        out_specs=(_chunk_spec(bsz, 512, fwd), _state_spec(bsz, fwd), _inv_spec(bsz, fwd)),
        out_shape=(jax.ShapeDtypeStruct(q.shape, F32), jax.ShapeDtypeStruct((bsz, nc, HEADS, HD, HD), F32),
                   jax.ShapeDtypeStruct((bsz, nc, HEADS, CHUNK, CHUNK), F32)),
        scratch_shapes=[pltpu.VMEM((bsz * HEADS, HD, HD), F32)],
        compiler_params=_params(("arbitrary",)))(q, k, v, gates)


def _dn_rec_bwd(q, k, v, gates, states, invs, do):
    bsz, t_total, _ = q.shape
    nc = t_total // CHUNK
    rev = lambda c: nc - 1 - c

    def body(q_ref, k_ref, v_ref, g_ref, ss_ref, inv_ref, do_ref, dq_ref, dk_ref, dv_ref, dg_ref, ds_ref):
        @pl.when(pl.program_id(0) == 0)
        def _():
            ds_ref[...] = jnp.zeros_like(ds_ref)

        seqs = range(bsz)
        s_list = [[ss_ref[b, 0, h] for h in range(HEADS)] for b in seqs]
        known = [inv_ref[b, 0, h] for b in seqs for h in range(HEADS)]
        _, vjp = jax.vjp(functools.partial(_dn_chunk, inv_known=known),
                         s_list, [q_ref[b] for b in seqs], [k_ref[b] for b in seqs],
                         [v_ref[b] for b in seqs], [g_ref[b] for b in seqs])
        ds_in, dq, dk, dv, dg = vjp(([do_ref[b] for b in seqs],
                                     [[ds_ref[b * HEADS + h] for h in range(HEADS)] for b in seqs]))
        for b in seqs:
            dq_ref[b], dk_ref[b], dv_ref[b], dg_ref[b] = dq[b], dk[b], dv[b], dg[b]
            for h in range(HEADS):
                ds_ref[b * HEADS + h] = ds_in[b][h]

    tok = lambda width: _chunk_spec(bsz, width, rev)
    out512 = jax.ShapeDtypeStruct(q.shape, F32)
    return pl.pallas_call(
        body, name="dn_rec_bwd", grid=(nc,),
        in_specs=[tok(512), tok(512), tok(512), tok(128), _state_spec(bsz, rev), _inv_spec(bsz, rev), tok(512)],
        out_specs=(tok(512), tok(512), tok(512), tok(128)),
        out_shape=(out512, out512, out512, jax.ShapeDtypeStruct(gates.shape, F32)),
        scratch_shapes=[pltpu.VMEM((bsz * HEADS, HD, HD), F32)],
        compiler_params=_params(("arbitrary",)))(q, k, v, gates, states, invs, do)


GQ_BLK, GK_BLK, GV_BLK = P_GQ // 512, P_GK // 512, P_GV // 512


def _gla_rec_fwd(proj, w2, bg):
    bsz, t_total, _ = proj.shape
    nc = t_total // CHUNK

    fwd = lambda c: c

    def body(q_ref, k_ref, v_ref, sm_ref, w2_ref, bg_ref, o_ref, ss_ref, s_ref):
        @pl.when(pl.program_id(0) == 0)
        def _():
            s_ref[...] = jnp.zeros_like(s_ref)

        seqs = range(bsz)
        s_list = [[s_ref[b * HEADS + h] for h in range(HEADS)] for b in seqs]
        for b in seqs:
            for h in range(HEADS):
                ss_ref[b, 0, h] = s_list[b][h]
        o, new_s = _gla_chunk(s_list, [q_ref[b] for b in seqs], [k_ref[b] for b in seqs], [v_ref[b] for b in seqs],
                              [sm_ref[b] for b in seqs], w2_ref[...], bg_ref[...])
        for b in seqs:
            o_ref[b] = o[b]
            for h in range(HEADS):
                s_ref[b * HEADS + h] = new_s[b][h]

    col = lambda blk, width=512: _chunk_spec(bsz, width, fwd, blk)
    return pl.pallas_call(
        body, name="gla_rec_fwd", grid=(nc,),
        in_specs=[col(GQ_BLK), col(GK_BLK), col(GV_BLK), col(SM_BLK, 128),
                  pl.BlockSpec((128, 512), lambda c: (0, 0)), pl.BlockSpec((1, 512), lambda c: (0, 0))],
        out_specs=(col(0), _state_spec(bsz, fwd)),
        out_shape=(jax.ShapeDtypeStruct((bsz, t_total, 512), F32),
                   jax.ShapeDtypeStruct((bsz, nc, HEADS, HD, HD), F32)),
        scratch_shapes=[pltpu.VMEM((bsz * HEADS, HD, HD), F32)],
        compiler_params=_params(("arbitrary",)))(proj, proj, proj, proj, w2, bg)


def _gla_rec_bwd(proj, w2, bg, states, do, dsm_dn):
    bsz, t_total, _ = proj.shape
    nc = t_total // CHUNK
    rev = lambda c: nc - 1 - c

    def body(q_ref, k_ref, v_ref, sm_ref, w2_ref, bg_ref, ss_ref, do_ref, dsd_ref,
             dq_ref, dk_ref, dv_ref, dsm_ref, dw2_ref, dbg_ref, ds_ref):
        @pl.when(pl.program_id(0) == 0)
        def _():
            dw2_ref[...] = jnp.zeros_like(dw2_ref)
            dbg_ref[...] = jnp.zeros_like(dbg_ref)
            ds_ref[...] = jnp.zeros_like(ds_ref)

        seqs = range(bsz)
        s_list = [[ss_ref[b, 0, h] for h in range(HEADS)] for b in seqs]
        _, vjp = jax.vjp(_gla_chunk, s_list, [q_ref[b] for b in seqs], [k_ref[b] for b in seqs],
                         [v_ref[b] for b in seqs], [sm_ref[b] for b in seqs], w2_ref[...], bg_ref[...])
        ds_in, dq, dk, dv, dsm, dw2, dbg = vjp(([do_ref[b] for b in seqs],
                                                [[ds_ref[b * HEADS + h] for h in range(HEADS)] for b in seqs]))
        for b in seqs:
            dq_ref[b], dk_ref[b], dv_ref[b] = dq[b].astype(MXU_DT), dk[b].astype(MXU_DT), dv[b].astype(MXU_DT)
            dsm_ref[b] = (dsm[b] + dsd_ref[b]).astype(MXU_DT)
            for h in range(HEADS):
                ds_ref[b * HEADS + h] = ds_in[b][h]
        dw2_ref[...] += dw2
        dbg_ref[...] += dbg

    col = lambda blk, width=512: _chunk_spec(bsz, width, rev, blk)
    w2_spec = pl.BlockSpec((128, 512), lambda c: (0, 0))
    bg_spec = pl.BlockSpec((1, 512), lambda c: (0, 0))
    out512 = jax.ShapeDtypeStruct((bsz, t_total, 512), MXU_DT)
    return pl.pallas_call(
        body, name="gla_rec_bwd", grid=(nc,),
        in_specs=[col(GQ_BLK), col(GK_BLK), col(GV_BLK), col(SM_BLK, 128), w2_spec, bg_spec,
                  _state_spec(bsz, rev), col(0), col(0, 128)],
        out_specs=(col(0), col(0), col(0), col(0, 128), w2_spec, bg_spec),
        out_shape=(out512, out512, out512, jax.ShapeDtypeStruct((bsz, t_total, 128), MXU_DT),
                   jax.ShapeDtypeStruct((128, 512), F32), jax.ShapeDtypeStruct((1, 512), F32)),
        scratch_shapes=[pltpu.VMEM((bsz * HEADS, HD, HD), F32)],
        compiler_params=_params(("arbitrary",)))(proj, proj, proj, proj, w2, bg, states, do, dsm_dn)


Z_BLK, GG_BLK = P_Z // 512, P_GG // 512


def _mix_out_fwd(o_dn, o_gla, proj, grow_dn, grow_gla):
    bsz, t_total, _ = o_dn.shape
    tt = _div_tile(t_total, ROW_TILE)

    def body(od_ref, og_ref, z_ref, gg_ref, gd_ref, gl_ref, o_ref):
        o_ref[0, :, :512] = _gate_norm(od_ref[0], z_ref[0], gd_ref[...]).astype(MXU_DT)
        o_ref[0, :, 512:] = _gate_norm(og_ref[0], gg_ref[0], gl_ref[...]).astype(MXU_DT)

    def col(blk):
        return pl.BlockSpec((1, tt, 512), lambda b, t: (b, t, blk))

    return pl.pallas_call(
        body, name="mix_out_fwd", grid=(bsz, t_total // tt),
        in_specs=[col(0), col(0), col(Z_BLK), col(GG_BLK), _vec_spec(512), _vec_spec(512)],
        out_specs=_tok_spec(tt), out_shape=jax.ShapeDtypeStruct((bsz, t_total, D), MXU_DT),
        compiler_params=_params(("parallel", "parallel")))(o_dn, o_gla, proj, proj, grow_dn, grow_gla)


def _mix_out_bwd(do, o_dn, o_gla, proj, grow_dn, grow_gla):
    bsz, t_total, _ = o_dn.shape
    tt = _div_tile(t_total, ROW_TILE)

    def body(do_ref, od_ref, og_ref, z_ref, gg_ref, gd_ref, gl_ref,
             dod_ref, dog_ref, dz_ref, dgg_ref, dgd_ref, dgl_ref):
        @pl.when((pl.program_id(0) == 0) & (pl.program_id(1) == 0))
        def _():
            dgd_ref[...] = jnp.zeros_like(dgd_ref)
            dgl_ref[...] = jnp.zeros_like(dgl_ref)

        def one(o_ref, gate_ref, g_ref, ct, do_out, dgate_out, dg_out):
            _, vjp = jax.vjp(_gate_norm, o_ref[0], gate_ref[0], g_ref[...])
            d_o, d_gate, d_row = vjp(ct)
            do_out[0] = d_o
            dgate_out[0] = d_gate.astype(MXU_DT)
            acc = d_row[:, :HD]
            for h in range(1, HEADS):
                acc = acc + d_row[:, h * HD:(h + 1) * HD]
            dg_out[...] += acc

        ct = do_ref[0].astype(F32)
        one(od_ref, z_ref, gd_ref, ct[:, :512], dod_ref, dz_ref, dgd_ref)
        one(og_ref, gg_ref, gl_ref, ct[:, 512:], dog_ref, dgg_ref, dgl_ref)

    def col(blk):
        return pl.BlockSpec((1, tt, 512), lambda b, t: (b, t, blk))

    f512 = jax.ShapeDtypeStruct((bsz, t_total, 512), F32)
    b512 = jax.ShapeDtypeStruct((bsz, t_total, 512), MXU_DT)
    g128 = jax.ShapeDtypeStruct((1, HD), F32)
    return pl.pallas_call(
        body, name="mix_out_bwd", grid=(bsz, t_total // tt),
        in_specs=[_tok_spec(tt), col(0), col(0), col(Z_BLK), col(GG_BLK), _vec_spec(512), _vec_spec(512)],
        out_specs=(col(0), col(0), col(0), col(0), _vec_spec(HD), _vec_spec(HD)),
        out_shape=(f512, f512, b512, b512, g128, g128),
        compiler_params=_params(("arbitrary", "arbitrary")))(do, o_dn, o_gla, proj, proj, grow_dn, grow_gla)


def _sum_slots(x, name):
    n, rows, cols = x.shape
    tr = _div_tile(rows, max(8, (1 << 19) // cols))

    def body(x_ref, o_ref):
        acc = x_ref[0].astype(F32)
        for i in range(1, n):
            acc = acc + x_ref[i].astype(F32)
        o_ref[...] = acc

    return pl.pallas_call(
        body, name=name, grid=(rows // tr,),
        in_specs=[pl.BlockSpec((n, tr, cols), lambda i: (0, i, 0))],
        out_specs=pl.BlockSpec((tr, cols), lambda i: (i, 0)),
        out_shape=jax.ShapeDtypeStruct((rows, cols), F32), compiler_params=_params(("parallel",)))(x)


def _adamw_math(w, g, m, v):
    nm = ADAM_B1 * m + (1.0 - ADAM_B1) * g
    nv = ADAM_B2 * v + (1.0 - ADAM_B2) * (g * g)
    m_hat = nm / (1.0 - ADAM_B1 ** ADAM_STEP)
    v_hat = nv / (1.0 - ADAM_B2 ** ADAM_STEP)
    return -ADAM_LR * (m_hat / (jnp.sqrt(v_hat) + ADAM_EPS) + ADAM_WD * w), nm, nv


def _adamw(w, g, m, v, name):
    _, rows, cols = w.shape
    tr = _div_tile(rows, max(8, (1 << 18) // cols))

    def body(w_ref, g_ref, m_ref, v_ref, d_ref, nm_ref, nv_ref):
        d_ref[...], nm_ref[...], nv_ref[...] = _adamw_math(w_ref[...], g_ref[...], m_ref[...], v_ref[...])

    spec = pl.BlockSpec((1, tr, cols), lambda i: (0, i, 0))
    shp = jax.ShapeDtypeStruct(w.shape, F32)
    return pl.pallas_call(body, name=name, grid=(rows // tr,), in_specs=[spec] * 4, out_specs=(spec,) * 3,
                          out_shape=(shp,) * 3, compiler_params=_params(("parallel",)))(w, g, m, v)


def _sum_adamw(parts, w, m, v, name):
    n, rows, cols = parts.shape
    tr = _div_tile(rows, max(8, (1 << 18) // cols))

    def body(p_ref, w_ref, m_ref, v_ref, g_ref, d_ref, nm_ref, nv_ref):
        g = p_ref[0].astype(F32)
        for i in range(1, n):
            g = g + p_ref[i].astype(F32)
        g_ref[...] = g
        d_ref[0], nm_ref[0], nv_ref[0] = _adamw_math(w_ref[0], g, m_ref[0], v_ref[0])

    spec = pl.BlockSpec((1, tr, cols), lambda i: (0, i, 0))
    shp = jax.ShapeDtypeStruct(w.shape, F32)
    return pl.pallas_call(
        body, name=name, grid=(rows // tr,),
        in_specs=[pl.BlockSpec((n, tr, cols), lambda i: (0, i, 0)), spec, spec, spec],
        out_specs=(pl.BlockSpec((tr, cols), lambda i: (i, 0)), spec, spec, spec),
        out_shape=(jax.ShapeDtypeStruct((rows, cols), F32), shp, shp, shp),
        compiler_params=_params(("parallel",)))(parts, w, m, v)


def _adamw_many(ws, gs, ms, vs, name):
    n = len(ws)

    def body(*refs):
        for i in range(n):
            d, nm, nv = _adamw_math(refs[i][...], refs[n + i][...], refs[2 * n + i][...], refs[3 * n + i][...])
            refs[4 * n + i][...] = d
            refs[5 * n + i][...] = nm
            refs[6 * n + i][...] = nv

    shapes = tuple(jax.ShapeDtypeStruct(w.shape, F32) for w in ws)
    outs = pl.pallas_call(body, name=name, out_shape=shapes * 3, compiler_params=_params())(*ws, *gs, *ms, *vs)
    return outs[:n], outs[n:2 * n], outs[2 * n:]


def _position():
    return lax.axis_index("x"), lax.axis_index("y"), lax.axis_index("c")


def _slot(px, py, pc):
    return 4 * px + 2 * py + pc


def _gather_small(x, name):
    rows, cols = x.shape

    def body(x_ref, o_ref, send_sems, recv_sems):
        mx, my, mc = _position()

        def peer(k):
            return (mx ^ ((k >> 2) & 1), my ^ ((k >> 1) & 1), mc ^ (k & 1))

        o_ref[_slot(mx, my, mc)] = x_ref[...]
        sends = []
        for k in range(1, N_DEV):
            cp = pltpu.make_async_remote_copy(src_ref=x_ref, dst_ref=o_ref.at[_slot(mx, my, mc)],
                                              send_sem=send_sems.at[k - 1], recv_sem=recv_sems.at[k - 1],
                                              device_id=peer(k), device_id_type=MESH)
            cp.start()
            sends.append(cp)
        for k in range(1, N_DEV):
            pltpu.make_async_remote_copy(src_ref=x_ref, dst_ref=o_ref.at[_slot(*peer(k))],
                                         send_sem=send_sems.at[k - 1], recv_sem=recv_sems.at[k - 1],
                                         device_id=peer(k), device_id_type=MESH).wait_recv()
        for cp in sends:
            cp.wait_send()

    return pl.pallas_call(
        body, name=name, out_shape=jax.ShapeDtypeStruct((N_DEV, rows, cols), x.dtype),
        in_specs=[pl.BlockSpec(memory_space=pltpu.VMEM)], out_specs=pl.BlockSpec(memory_space=pltpu.VMEM),
        scratch_shapes=[pltpu.SemaphoreType.DMA((N_DEV - 1,)), pltpu.SemaphoreType.DMA((N_DEV - 1,))],
        compiler_params=pltpu.CompilerParams(vmem_limit_bytes=VMEM_LIMIT_V7X))(x)


def _gather_big(shards):
    n = len(shards)

    def body(*refs):
        xs, outs = refs[:n], refs[n:2 * n]
        send_sems, recv_sems, local_sems = refs[2 * n:]
        mx, my, mc = _position()
        me, sibling = (mx, my, mc), (mx, my, 1 - mc)
        chips = [(1 - mx, my), (mx, 1 - my), (1 - mx, 1 - my)]

        def copy(a, k, block, to, src=None):
            dst = outs[a].at[_slot(*block)]
            return pltpu.make_async_remote_copy(src_ref=dst if src is None else src, dst_ref=dst,
                                                send_sem=send_sems.at[7 * a + k], recv_sem=recv_sems.at[7 * a + k],
                                                device_id=to, device_id_type=MESH)

        mine = [pltpu.make_async_copy(xs[a], outs[a].at[_slot(*me)], local_sems.at[a]) for a in range(n)]
        for cp in mine:
            cp.start()
        started = []
        for a in range(n):
            started.append(copy(a, 0, me, sibling, src=xs[a]))
            started += [copy(a, 1 + j, me, (*chip, mc), src=xs[a]) for j, chip in enumerate(chips)]
        for cp in started:
            cp.start()
        for j, chip in enumerate(chips):
            for a in range(n):
                copy(a, 1 + j, (*chip, mc), me).wait_recv()
                fwd = copy(a, 4 + j, (*chip, mc), sibling)
                fwd.start()
                started.append(fwd)
        for a in range(n):
            copy(a, 0, sibling, me).wait_recv()
            for j, chip in enumerate(chips):
                copy(a, 4 + j, (*chip, 1 - mc), me).wait_recv()
        for cp in started:
            cp.wait_send()
        for cp in mine:
            cp.wait()

    any_spec = pl.BlockSpec(memory_space=pl.ANY)
    return pl.pallas_call(
        body, name="gather_weights",
        out_shape=tuple(jax.ShapeDtypeStruct((N_DEV,) + s.shape, s.dtype) for s in shards),
        in_specs=[any_spec] * n, out_specs=(any_spec,) * n,
        scratch_shapes=[pltpu.SemaphoreType.DMA((7 * n,)), pltpu.SemaphoreType.DMA((7 * n,)),
                        pltpu.SemaphoreType.DMA((n,))])(*shards)


def _peer(pos, k):
    mx, my, mc = pos
    return (mx ^ ((k >> 2) & 1), my ^ ((k >> 1) & 1), mc ^ (k & 1))


def _exchange_copies(srcs, lands, send_sems, recv_sems, by_owner):
    pos = _position()
    me = _slot(*pos)
    out = []
    for a, (src, land) in enumerate(zip(srcs, lands)):
        for k in range(1, N_DEV):
            peer = _peer(pos, k)
            sems = dict(send_sem=send_sems.at[7 * a + k - 1], recv_sem=recv_sems.at[7 * a + k - 1],
                        device_id=peer, device_id_type=MESH)
            mine = src.at[_slot(*peer)] if by_owner else src
            send = pltpu.make_async_remote_copy(src_ref=mine, dst_ref=land.at[me], **sems)
            recv = pltpu.make_async_remote_copy(src_ref=mine, dst_ref=land.at[_slot(*peer)], **sems)
            out.append((send, recv))
    return out


_HBM_SPEC = pl.BlockSpec(memory_space=pltpu.HBM)
_SEM_SPEC = pl.BlockSpec(memory_space=pltpu.SEMAPHORE)
_DATAFLOW = pltpu.SideEffectType.DATAFLOW_SIDE_EFFECTING


def _exchange_start(name, srcs, slab_shapes, after, by_owner, carry=()):
    n, na, nc = len(srcs), len(after), len(carry)
    lands = [pltpu.with_memory_space_constraint(lax.empty((N_DEV,) + s, x.dtype), pltpu.HBM)
             for s, x in zip(slab_shapes, srcs)]
    thru = [pltpu.with_memory_space_constraint(x, pltpu.HBM) for x in [*srcs, *lands, *carry]]

    def body(*refs):
        src_refs, land_refs = refs[:n], refs[n:2 * n]
        send_sems, recv_sems = refs[len(thru) + na], refs[len(thru) + na + 1]
        token = refs[-1]
        for send, _ in _exchange_copies(src_refs, land_refs, send_sems, recv_sems, by_owner):
            send.start()
        token[...] = jnp.zeros_like(token)

    outs = pl.pallas_call(
        body, name=name,
        out_shape=(pltpu.SemaphoreType.DMA((7 * n,)), pltpu.SemaphoreType.DMA((7 * n,)),
                   *[pltpu.HBM(x.shape, x.dtype) for x in thru], jax.ShapeDtypeStruct((8, 128), F32)),
        in_specs=[_HBM_SPEC] * len(thru) + [pl.BlockSpec(memory_space=pl.ANY)] * na,
        out_specs=(_SEM_SPEC, _SEM_SPEC, *[_HBM_SPEC] * len(thru), pl.BlockSpec(memory_space=pltpu.VMEM)),
        input_output_aliases={i: 2 + i for i in range(len(thru))},
        compiler_params=pltpu.CompilerParams(has_side_effects=_DATAFLOW))(*thru, *after)
    return (outs[0], outs[1], list(outs[2:2 + n]), list(outs[2 + n:2 + 2 * n]), outs[-1],
            list(outs[2 + 2 * n:2 + 2 * n + nc]))


def _exchange_wait(name, send_sems, recv_sems, srcs, lands, after, by_owner):
    n = len(srcs)

    def body(*refs):
        src_refs, land_refs = refs[:n], refs[n:2 * n]
        s_sems, r_sems = refs[2 * n], refs[2 * n + 1]
        for send, recv in _exchange_copies(src_refs, land_refs, s_sems, r_sems, by_owner):
            send.wait_send()
            recv.wait_recv()

    outs = pl.pallas_call(
        body, name=name,
        out_shape=(*[pltpu.HBM(x.shape, x.dtype) for x in srcs], *[pltpu.HBM(l.shape, l.dtype) for l in lands]),
        in_specs=[_HBM_SPEC] * (2 * n) + [_SEM_SPEC, _SEM_SPEC, pl.BlockSpec(memory_space=pl.ANY)],
        out_specs=tuple([_HBM_SPEC] * (2 * n)),
        input_output_aliases={i: i for i in range(2 * n)},
        compiler_params=pltpu.CompilerParams(has_side_effects=_DATAFLOW))(*srcs, *lands, send_sems, recv_sems, after)
    return list(outs[:n]), list(outs[n:])


def _pad_heads(x, axis):
    shp = list(x.shape)
    x4 = x.reshape(shp[:axis] + [HEADS, GLA_KEY] + shp[axis + 1:])
    pad = [(0, 0)] * x4.ndim
    pad[axis + 1] = (0, HD - GLA_KEY)
    return jnp.pad(x4, pad).reshape(shp[:axis] + [HEADS * HD] + shp[axis + 1:])


def _unpad_heads(x, axis):
    shp = list(x.shape)
    x4 = x.reshape(shp[:axis] + [HEADS, HD] + shp[axis + 1:])
    x4 = lax.slice_in_dim(x4, 0, GLA_KEY, axis=axis + 1)
    return x4.reshape(shp[:axis] + [HEADS * GLA_KEY] + shp[axis + 1:])


O_Z_END, O_AB, O_GQ, O_GK, O_GV, O_R = 2048, 2048, 2056, 2312, 2568, 3592


def _padded_row(f):
    if f < O_Z_END:
        return f
    if f < O_GQ:
        return P_SM + (f - O_AB)
    if f < O_GV:
        base, g = (P_GQ, f - O_GQ) if f < O_GK else (P_GK, f - O_GK)
        return base + HD * (g // GLA_KEY) + g % GLA_KEY
    if f < O_R:
        return P_GV + (f - O_GV)
    return P_SM + 8 + (f - O_R)


def _runs(pairs):
    out = []
    for d, s in pairs:
        if out and out[-1][0] + out[-1][2] == d and out[-1][1] + out[-1][2] == s:
            out[-1][2] += 1
        else:
            out.append([d, s, 1])
    return out


def _pad_in_rows(shards):
    wt = shards.reshape(IN_W, D)
    return jnp.concatenate([
        wt[:O_Z_END], _pad_heads(wt[O_GQ:O_GK], 0), _pad_heads(wt[O_GK:O_GV], 0), wt[O_GV:O_R],
        wt[O_AB:O_GQ], wt[O_R:], jnp.zeros((P_W - P_SM - 8 - GATE_RANK, D), wt.dtype)], axis=0)


def _unpad_in_rows(gt):
    per = IN_W // N_DEV
    return jnp.stack([
        jnp.concatenate([gt[src:src + n] for _, src, n in
                         _runs([(f, _padded_row(f)) for f in range(j * per, (j + 1) * per)])], axis=0)
        for j in range(N_DEV)])


def _lane_row(vals, width=128):
    return jnp.pad(vals.reshape(1, -1), ((0, 0), (0, width - vals.size)))


SMALL_NAMES = ["ln0_g", "ln0_b", "b_ada", "dn_conv", "dn_a_log", "dn_dt_bias", "dn_norm_g", "gla_w_gate2",
               "gla_b_gate", "gla_norm_g", "ln1_g", "ln1_b", "ffn_conv", "ffn_conv_b", "ln2_g", "ln2_b"]
WEIGHTS = ["ln0_g", "ln0_b", "w_ada", "b_ada", "w_in", "dn_conv", "dn_a_log", "dn_dt_bias", "dn_norm_g",
           "gla_w_gate2", "gla_b_gate", "gla_norm_g", "w_o", "ln1_g", "ln1_b", "ffn_w_up", "ffn_conv", "ffn_conv_b",
           "ffn_w_down", "ln2_g", "ln2_b"]


def kernel(x, c, ln0_g, ln0_b, w_ada, b_ada, w_in, dn_conv, dn_a_log, dn_dt_bias, dn_norm_g, gla_w_gate2, gla_b_gate, gla_norm_g, w_o, ln1_g, ln1_b, ffn_w_up, ffn_conv, ffn_conv_b, ffn_w_down, ln2_g, ln2_b, loss_target, m_ln0_g, m_ln0_b, m_w_ada, m_b_ada, m_w_in, m_dn_conv, m_dn_a_log, m_dn_dt_bias, m_dn_norm_g, m_gla_w_gate2, m_gla_b_gate, m_gla_norm_g, m_w_o, m_ln1_g, m_ln1_b, m_ffn_w_up, m_ffn_conv, m_ffn_conv_b, m_ffn_w_down, m_ln2_g, m_ln2_b, v_ln0_g, v_ln0_b, v_w_ada, v_b_ada, v_w_in, v_dn_conv, v_dn_a_log, v_dn_dt_bias, v_dn_norm_g, v_gla_w_gate2, v_gla_b_gate, v_gla_norm_g, v_w_o, v_ln1_g, v_ln1_b, v_ffn_w_up, v_ffn_conv, v_ffn_conv_b, v_ffn_w_down, v_ln2_g, v_ln2_b):
    args = dict(locals())
    w_given = {n: args[n] for n in WEIGHTS}
    m_given = {n: args["m_" + n] for n in WEIGHTS}
    v_given = {n: args["v_" + n] for n in WEIGHTS}
    bsz, t_total, _ = x.shape
    ntok = bsz * t_total
    mx, my, mc = _position()
    me = _slot(mx, my, mc)

    pack1 = jnp.concatenate([c.reshape(-1), dn_conv.reshape(-1), gla_w_gate2.reshape(-1), ffn_conv.reshape(-1)])
    n1 = pack1.size
    rows1 = -(-n1 // 1024) * 8
    pack1 = jnp.pad(pack1, (0, rows1 * 128 - n1)).reshape(rows1, 128)
    got1 = _gather_small(pack1, "gather_cond").reshape(N_DEV, -1)
    o1 = bsz * D
    o2 = o1 + dn_conv.size
    o3 = o2 + gla_w_gate2.size
    c_all = got1[:, :o1].reshape(N_DEV * bsz, D)
    dn_conv_f = got1[:, o1:o2].reshape(N_DEV, DN_CONV_K, -1).transpose(1, 0, 2).reshape(DN_CONV_K, QKV_W)
    gate2_f = got1[:, o2:o3].reshape(N_DEV, GATE_RANK, -1).transpose(1, 0, 2).reshape(GATE_RANK, HEADS * GLA_KEY)
    ffn_conv_f = got1[:, o3:n1].reshape(N_DEV, FFN_CONV_K, -1).transpose(1, 0, 2).reshape(FFN_CONV_K, 2 * D_FF)

    win_t = w_in[0].T.astype(MXU_DT)
    wup_t = ffn_w_up[0].T.astype(MXU_DT)
    (win_all,) = _gather_big([win_t])
    win_p = _pad_in_rows(win_all)
    cw_p, cb_p = _ffn_pair(ffn_conv_f, 1), _ffn_pair(ffn_conv_b, 1)

    ncol = w_ada.shape[2]
    b_cols = lax.dynamic_slice_in_dim(b_ada, me * ncol, ncol, axis=1)
    mod_part = _ada_fwd(c_all, w_ada[0], b_cols)
    mod_all = _gather_small(mod_part.reshape(-1, 128), "gather_mod").reshape(N_DEV, N_DEV * bsz, ncol)
    mod = lax.dynamic_slice_in_dim(mod_all, me * bsz, bsz, axis=1).transpose(1, 0, 2).reshape(bsz, 6, 1, D)
    late = [w_o[0].astype(MXU_DT), wup_t, ffn_w_down[0].astype(MXU_DT)]
    ag_send, ag_recv, ag_src, ag_land, ag_token, _ = _exchange_start(
        "gather_start", late, [w.shape for w in late], [win_all, mod_all], by_owner=False)
    mod = mod + ag_token[0, 0]
    sh_a, sc_a, gt_a, sh_f, sc_f, gt_f = (mod[:, i] for i in range(6))

    g0, b0 = ln0_g.reshape(1, D), ln0_b.reshape(1, D)
    alog_row, dt_row = _lane_row(dn_a_log[0]), _lane_row(dn_dt_bias[0])
    grow_dn, grow_gla = jnp.tile(dn_norm_g, (1, HEADS)), jnp.tile(gla_norm_g, (1, HEADS))
    w2 = jnp.zeros((128, HEADS * HD), F32).at[SM_R:SM_R + GATE_RANK].set(_pad_heads(gate2_f, 1))
    bg = _pad_heads(gla_b_gate, 1)

    h_a = _ln0_mod(x, g0, b0, sc_a, sh_a)
    proj = _mm(h_a.reshape(ntok, D), win_p, "nt", F32, "mm_proj", tm=1024, tn=1408).reshape(bsz, t_total, P_W)
    q, k, v, gates = _dn_pre_fwd(proj, dn_conv_f, alog_row, dt_row)
    o_dn, s_dn, inv_dn = _dn_rec_fwd(q, k, v, gates)
    o_gla, s_gla = _gla_rec_fwd(proj, w2, bg)
    o_mix = _mix_out_fwd(o_dn, o_gla, proj, grow_dn, grow_gla)
    late, landed = _exchange_wait("gather_wait", ag_send, ag_recv, ag_src, ag_land, o_mix, by_owner=False)
    wo_all, wup_all, wdn_all = (lax.dynamic_update_slice(l, w[None], (me, 0, 0)) for l, w in zip(landed, late))
    wo_f = wo_all.reshape(D, D)
    wup_f = _ffn_pair(wup_all.reshape(2 * D_FF, D), 0)
    wdn_f = wdn_all.reshape(D_FF, D)
    y = _mm(o_mix.reshape(ntok, D), wo_f, "nn", MXU_DT, "mm_wo", tm=1024, tn=1024).reshape(bsz, t_total, D)
    r1, h_f = _res_ln_mod(x, y, gt_a, g0, b0, ln1_g, ln1_b, sc_f, sh_f)
    up, act = _ffn_up_act(h_f, wup_f, cw_p, cb_p)
    y2 = _mm(act.reshape(ntok, D_FF), wdn_f, "nn", MXU_DT, "mm_down", tm=1024, tn=1024).reshape(bsz, t_total, D)
    loss_rows, dr2, dy2, dgt_f, d_ln2_g, d_ln2_b = _final_fwd_bwd(r1, y2, gt_f, ln1_g, ln1_b, ln2_g, ln2_b, loss_target)
    loss_part = (0.5 / D) * jnp.sum(loss_rows)

    dy2_2 = dy2.reshape(ntok, D)
    g_wdn = _mm(act.reshape(ntok, D_FF), dy2_2, "tn", MXU_DT, "mm_gwdn", tm=1408, tn=1024)
    dup, d_cw_p, d_cb_p = _ffn_act_bwd(up, dy2, wdn_f, cw_p, cb_p)
    d_ffn_conv, d_ffn_conv_b = _ffn_unpair(d_cw_p, 1), _ffn_unpair(d_cb_p, 1)
    dup_2 = dup.reshape(ntok, 2 * D_FF)
    dh_f = _mm(dup_2, wup_f, "nn", MXU_DT, "mm_dhf", tn=1024).reshape(bsz, t_total, D)
    g_wup_t = _mm(dup_2, h_f.reshape(ntok, D), "tn", MXU_DT, "mm_gwup", tm=1408, tn=1024)
    ffn_parts = [_ffn_unpair(g_wup_t, 0).reshape(N_DEV, -1, D), g_wdn.reshape(N_DEV, -1, D)]
    rs_send, rs_recv, rs_src, rs_land, rs_token, _ = _exchange_start(
        "scatter_start", ffn_parts, [p.shape[1:] for p in ffn_parts], [dh_f], by_owner=True)
    dr1, dsc_f, dsh_f, d_ln1_g, d_ln1_b, dy, dgt_a = _ln_bwd_call(
        "ln1_bwd", dr2, dh_f, r1, ln1_g, ln1_b, sc_f + rs_token[0, 0], y=y, gt=gt_a)

    dy_2 = dy.reshape(ntok, D)
    do = _mm(dy_2, wo_f, "nt", MXU_DT, "mm_do", tm=1024, tn=1024).reshape(bsz, t_total, D)
    g_wo = _mm(o_mix.reshape(ntok, D), dy_2, "tn", MXU_DT, "mm_gwo", tm=512, tn=1024)
    do_dn, do_gla, dz, dgg, d_dn_norm, d_gla_norm = _mix_out_bwd(do, o_dn, o_gla, proj, grow_dn, grow_gla)
    dq, dk, dv, dgates = _dn_rec_bwd(q, k, v, gates, s_dn, inv_dn, do_dn)
    dqkv, dsm_dn, d_dn_conv, d_alog_row, d_dt_row = _dn_pre_bwd(proj, dq, dk, dv, dgates, dn_conv_f, alog_row, dt_row)
    dgq, dgk, dgv, dsm, d_w2, d_bg = _gla_rec_bwd(proj, w2, bg, s_gla, do_gla, dsm_dn)
    dproj = jnp.concatenate([dqkv, dz, dgq, dgk, dgv, dgg, dsm], axis=-1).reshape(ntok, P_W)
    g_win_p = _mm(dproj, h_a.reshape(ntok, D), "tn", MXU_DT, "mm_gwin", tm=1408, tn=1024)
    mix_parts = [_unpad_in_rows(g_win_p), g_wo.reshape(N_DEV, -1, D)]
    rs2_send, rs2_recv, rs2_src, rs2_land, rs2_token, (win_p_late,) = _exchange_start(
        "scatter_mix_start", mix_parts, [p.shape[1:] for p in mix_parts], [], by_owner=True, carry=[win_p])
    dh_a = _mm(dproj, win_p_late, "nn", MXU_DT, "mm_dha", tn=1024).reshape(bsz, t_total, D)
    grad_x, dsc_a, dsh_a, d_ln0_g, d_ln0_b = _ln_bwd_call(
        "ln0_bwd", dr1, dh_a, x, g0, b0, sc_a + rs2_token[0, 0])

    delta, new_m, new_v, big_grads = {}, {}, {}, {}
    flip = lambda a: jnp.swapaxes(a, 1, 2)

    def update_owned(n, landed, mine):
        parts = lax.dynamic_update_slice(landed, lax.dynamic_slice_in_dim(mine, me, 1, axis=0), (me, 0, 0))
        turn = flip if parts.shape[1:] != w_given[n].shape[1:] else (lambda a: a)
        g, d_, m_, v_ = _sum_adamw(parts, turn(w_given[n]), turn(m_given[n]), turn(v_given[n]), "adamw_" + n)
        big_grads[n], delta[n], new_m[n], new_v[n] = turn(g[None]), turn(d_), turn(m_), turn(v_)

    ffn_parts, ffn_landed = _exchange_wait("scatter_wait", rs_send, rs_recv, rs_src, rs_land, grad_x, by_owner=True)
    update_owned("ffn_w_up", ffn_landed[0], ffn_parts[0])
    update_owned("ffn_w_down", ffn_landed[1], ffn_parts[1])
    ffn_done = 0.0 * (new_v["ffn_w_up"][0, 0, 0] + new_v["ffn_w_down"][0, 0, 0])

    dmod = jnp.concatenate([dsh_a, dsc_a, dgt_a, dsh_f, dsc_f, dgt_f], axis=1).reshape(-1)
    small_parts = {
        "ln0_g": d_ln0_g, "ln0_b": d_ln0_b, "ln1_g": d_ln1_g, "ln1_b": d_ln1_b, "ln2_g": d_ln2_g, "ln2_b": d_ln2_b,
        "dn_a_log": d_alog_row[:, :HEADS], "dn_dt_bias": d_dt_row[:, :HEADS],
        "dn_norm_g": d_dn_norm, "gla_norm_g": d_gla_norm, "gla_b_gate": _unpad_heads(d_bg, 1),
        "ffn_conv_b": d_ffn_conv_b, "dn_conv": d_dn_conv,
        "gla_w_gate2": _unpad_heads(d_w2[SM_R:SM_R + GATE_RANK], 1), "ffn_conv": d_ffn_conv}
    order = sorted(small_parts)
    flat = jnp.concatenate([small_parts[n].reshape(-1) for n in order] + [(loss_part + ffn_done).reshape(1), dmod])
    n3 = flat.size
    rows3 = -(-n3 // 1024) * 8
    pack3 = jnp.pad(flat, (0, rows3 * 128 - n3)).reshape(rows3, 128)
    got3 = _gather_small(pack3, "gather_small_grads")
    tot3 = _sum_slots(got3, "sum_small_grads").reshape(-1)
    grads = {}
    off = 0
    for n in order:
        size = small_parts[n].size
        grads[n] = tot3[off:off + size]
        off += size
    loss = tot3[off]
    off += 1
    dmod_all = got3.reshape(N_DEV, -1)[:, off:off + dmod.size].reshape(N_DEV * bsz, 6 * D)
    dmod_cols = lax.dynamic_slice_in_dim(dmod_all, me * ncol, ncol, axis=1)
    g_wada, g_bada = _ada_bwd(c_all, dmod_all, dmod_cols)
    grads["b_ada"] = g_bada

    def col_shard(full, rows):
        part = full.reshape(rows, -1)
        width = part.shape[1] // N_DEV
        return lax.dynamic_slice_in_dim(part, me * width, width, axis=1)

    grads["dn_conv"] = col_shard(grads["dn_conv"], DN_CONV_K)
    grads["gla_w_gate2"] = col_shard(grads["gla_w_gate2"], GATE_RANK)
    grads["ffn_conv"] = col_shard(grads["ffn_conv"], FFN_CONV_K)
    grads = {n: g.reshape(w_given[n].shape) for n, g in grads.items()}
    mix_parts, mix_landed = _exchange_wait("scatter_mix_wait", rs2_send, rs2_recv, rs2_src, rs2_land, grad_x,
                                           by_owner=True)
    update_owned("w_in", mix_landed[0], mix_parts[0])
    update_owned("w_o", mix_landed[1], mix_parts[1])
    grads["w_ada"] = g_wada.reshape(w_ada.shape)
    delta["w_ada"], new_m["w_ada"], new_v["w_ada"] = _adamw(w_ada, grads["w_ada"], m_w_ada, v_w_ada, "adamw_w_ada")
    grads.update(big_grads)
    d_s, m_s, v_s = _adamw_many(*[[src[n] for n in SMALL_NAMES] for src in (w_given, grads, m_given, v_given)],
                                "adamw_small")
    for i, n in enumerate(SMALL_NAMES):
        delta[n], new_m[n], new_v[n] = d_s[i], m_s[i], v_s[i]

    return (loss, grad_x, *[grads[n] for n in WEIGHTS], *[delta[n] for n in WEIGHTS],
            *[new_m[n] for n in WEIGHTS], *[new_v[n] for n in WEIGHTS])
```

```python
import functools

import jax
import jax.numpy as jnp
from jax import lax
from jax.experimental import pallas as pl
from jax.experimental.pallas import tpu as pltpu

F32 = jnp.float32
MXU_DT = jnp.bfloat16
MESH = pl.DeviceIdType.MESH
N_DEV = 8

D = 1024
HEADS = 4
HD = 128
CHUNK = 64
GLA_KEY = 64
GLA_TAU = 16.0
GATE_RANK = 16
D_FF = 2816
IN_W = 3608
ALPHA = 2.0 ** 0.25
EPS = 1e-6
DN_CONV_K = 4
FFN_CONV_K = 3
HALO = 8
ROW_TILE = 512
FFN_ROW_TILE = 1024

P_QKV, P_Z, P_GQ, P_GK, P_GV, P_GG, P_SM, P_W = 0, 1536, 2048, 2560, 3072, 3584, 4096, 4224
SM_A, SM_B, SM_R = 0, 4, 8

ADAM_LR, ADAM_B1, ADAM_B2, ADAM_EPS, ADAM_WD, ADAM_STEP = 0.001, 0.9, 0.999, 1e-08, 0.01, 10

VMEM_LIMIT_V7X = 56 * 1024 * 1024


def _params(sem=None):
    return pltpu.CompilerParams(dimension_semantics=sem, vmem_limit_bytes=VMEM_LIMIT_V7X)


def _dg(a, b, dims, prec=None):
    return lax.dot_general(a, b, (dims, ((), ())), precision=prec, preferred_element_type=F32)


def _dot(a, b, prec=None):
    return _dg(a, b, ((1,), (0,)), prec)


def _dot_nt(a, b, prec=None):
    return _dg(a, b, ((1,), (1,)), prec)


def _dot_tn(a, b, prec=None):
    return _dg(a, b, ((0,), (0,)), prec)


def _iota(shape, dim):
    return lax.broadcasted_iota(jnp.int32, shape, dim)


def _sigmoid(x):
    return jax.nn.sigmoid(x)


def _silu(x):
    return x * _sigmoid(x)


def _softplus(x):
    return jnp.maximum(x, 0.0) + jnp.log(1.0 + jnp.exp(-jnp.abs(x)))


def _ln_stats(x):
    mu = jnp.mean(x, axis=-1, keepdims=True)
    xc = x - mu
    rstd = lax.rsqrt(jnp.mean(xc * xc, axis=-1, keepdims=True) + EPS)
    return xc * rstd, rstd


def _ln_bwd(dxhat, xhat, rstd):
    return rstd * (dxhat - jnp.mean(dxhat, axis=-1, keepdims=True)
                   - xhat * jnp.mean(dxhat * xhat, axis=-1, keepdims=True))


NN, NT, TN = ((1,), (0,)), ((1,), (1,)), ((0,), (0,))


def _split2(a):
    hi = a.astype(jnp.bfloat16)
    return hi, (a - hi.astype(F32)).astype(jnp.bfloat16)


def _d3(a, b, dims):
    ah, al = _split2(a)
    bh, bl = _split2(b)
    return _dg(ah, bh, dims) + (_dg(ah, bl, dims) + _dg(al, bh, dims))


@jax.custom_vjp
def _dot3(a, b):
    return _d3(a, b, NN)


_dot3.defvjp(lambda a, b: (_d3(a, b, NN), (a, b)),
             lambda res, g: (_d3(g, res[1], NT), _d3(res[0], g, TN)))


def _split3(b):
    b1 = b.astype(jnp.bfloat16)
    r1 = b - b1.astype(F32)
    b2 = r1.astype(jnp.bfloat16)
    return b1, b2, (r1 - b2.astype(F32)).astype(jnp.bfloat16)


def _sum3(fn, b):
    b1, b2, b3 = _split3(b)
    return fn(b1) + (fn(b2) + fn(b3))


@jax.custom_vjp
def _mask_dot(e, b):
    return _sum3(lambda t: _dg(e, t, NN), b)


_mask_dot.defvjp(lambda e, b: (_mask_dot(e, b), e),
                 lambda e, g: (jnp.zeros_like(e), _sum3(lambda t: _dg(e, t, TN), g)))


@jax.custom_vjp
def _mask_dot_nt(e, b):
    return _sum3(lambda t: _dg(e, t, NT), b)


_mask_dot_nt.defvjp(lambda e, b: (_mask_dot_nt(e, b), e),
                    lambda e, g: (jnp.zeros_like(e), _sum3(lambda t: _dg(t, e, TN), g)))


def _tri_inv_impl(ms):
    n = ms[0].shape[0]
    r, c = _iota((n, n), 0), _iota((n, n), 1)
    eye = (r == c).astype(F32)
    diag = (r >> 3) == (c >> 3)
    ds = [jnp.where(diag, m, 0.0) for m in ms]
    d2s = [_d3(d, d, NN) for d in ds]
    d4s = [_d3(d2, d2, NN) for d2 in d2s]
    invs = [_d3(eye - d, eye + d2, NN) for d, d2 in zip(ds, d2s)]
    invs = [_d3(inv, eye + d4, NN) for inv, d4 in zip(invs, d4s)]
    shift = 3
    while (1 << shift) < n:
        rb, cb = r >> shift, c >> shift
        sel = ((rb & 1) == 1) & (cb == rb - 1)
        tmp = [_d3(inv, jnp.where(sel, m, 0.0), NN) for inv, m in zip(invs, ms)]
        invs = [inv - _d3(t, inv, NN) for t, inv in zip(tmp, invs)]
        shift += 1
    return invs


@jax.custom_vjp
def _tri_inv(ms):
    return _tri_inv_impl(ms)


def _tri_inv_fwd(ms):
    invs = _tri_inv_impl(ms)
    return invs, invs


def _tri_inv_bwd(invs, das):
    tmp = [_d3(a, da, TN) for a, da in zip(invs, das)]
    return ([-_d3(t, a, NT) for t, a in zip(tmp, invs)],)


_tri_inv.defvjp(_tri_inv_fwd, _tri_inv_bwd)


@jax.custom_vjp
def _tri_inv_known(ms, invs):
    return invs


_tri_inv_known.defvjp(lambda ms, invs: (invs, invs),
                      lambda invs, das: (_tri_inv_bwd(invs, das)[0], [jnp.zeros_like(a) for a in invs]))


def _dn_chunk(s_list, q, k, v, gates, inv_known=None, with_inv=False):
    nb = len(q)
    c = q[0].shape[0]
    r64, c64 = _iota((c, c), 0), _iota((c, c), 1)
    causal = r64 >= c64
    strict = r64 > c64
    tri = causal.astype(jnp.bfloat16)
    eye = (_iota((HD, HD), 0) == _iota((HD, HD), 1)).astype(jnp.bfloat16)
    lane = _iota(gates[0].shape, 1)
    lane1 = _iota((1, HD), 1)
    g_all = [_mask_dot(tri, g) for g in gates]
    g_all_t = [_mask_dot_nt(eye, g) for g in g_all]
    row = _iota(g_all_t[0].shape, 0)
    last = [jnp.sum(g, axis=0, keepdims=True) for g in gates]
    prob = [(b, h) for b in range(nb) for h in range(HEADS)]
    sl = [slice(h * HD, (h + 1) * HD) for h in range(HEADS)]
    qh = [q[b][:, sl[h]] for b, h in prob]
    kh = [k[b][:, sl[h]] for b, h in prob]
    vh = [v[b][:, sl[h]] for b, h in prob]
    s = [s_list[b][h] for b, h in prob]
    beta = [jnp.sum(jnp.where(lane == SM_B + h, gates[b], 0.0), axis=-1, keepdims=True) for b, h in prob]
    g_c = [jnp.sum(jnp.where(lane == SM_A + h, g_all[b], 0.0), axis=-1, keepdims=True) for b, h in prob]
    g_r = [jnp.sum(jnp.where(row == SM_A + h, g_all_t[b], 0.0), axis=0, keepdims=True) for b, h in prob]
    g_last = [jnp.sum(jnp.where(lane1 == SM_A + h, last[b], 0.0), axis=-1, keepdims=True) for b, h in prob]
    decay = [jnp.where(causal, jnp.exp(jnp.where(causal, gc - gr, 0.0)), 0.0) for gc, gr in zip(g_c, g_r)]
    kb = [k_ * b_ for k_, b_ in zip(kh, beta)]
    m_low = [jnp.where(strict, _dot_nt(kb_, k_) * d_, 0.0) for kb_, k_, d_ in zip(kb, kh, decay)]
    attn = [_dot_nt(q_, k_) * d_ for q_, k_, d_ in zip(qh, kh, decay)]
    a_inv = _tri_inv(m_low) if inv_known is None else _tri_inv_known(m_low, inv_known)
    eg = [jnp.exp(gc) for gc in g_c]
    uw = [_dot3(a_, jnp.concatenate([v_ * b_, kb_ * e_], axis=1))
          for a_, v_, b_, kb_, e_ in zip(a_inv, vh, beta, kb, eg)]
    v_new = [uw_[:, :HD] - _dot(uw_[:, HD:], s_) for uw_, s_ in zip(uw, s)]
    qs = [_dot(q_ * e_, s_) for q_, e_, s_ in zip(qh, eg, s)]
    o = [qs_ + _dot(a_, vn_) for qs_, a_, vn_ in zip(qs, attn, v_new)]
    k_dec = [k_ * jnp.exp(gl - gc) for k_, gl, gc in zip(kh, g_last, g_c)]
    s_new = [s_ * jnp.exp(gl) + _dot_tn(kd_, vn_) for s_, gl, kd_, vn_ in zip(s, g_last, k_dec, v_new)]
    outs = [jnp.concatenate(o[b * HEADS:(b + 1) * HEADS], axis=-1) for b in range(nb)]
    states = [s_new[b * HEADS:(b + 1) * HEADS] for b in range(nb)]
    return (outs, states, a_inv) if with_inv else (outs, states)


def _gla_chunk(st_list, q, k, v, small, w2, bg):
    nb = len(q)
    c = q[0].shape[0]
    causal = _iota((c, c), 0) >= _iota((c, c), 1)
    tri = causal.astype(jnp.bfloat16)
    la_all = [-_softplus(-(_dot(sm, w2) + bg)) * (1.0 / GLA_TAU) for sm in small]
    b_all = [_mask_dot(tri, la) for la in la_all]
    prob = [(b, h) for b in range(nb) for h in range(HEADS)]
    sl = [slice(h * HD, (h + 1) * HD) for h in range(HEADS)]
    kh = [k[b][:, sl[h]] for b, h in prob]
    vh = [v[b][:, sl[h]] for b, h in prob]
    st = [st_list[b][h] for b, h in prob]
    bc = [b_all[b][:, sl[h]] for b, h in prob]
    b_last = [jnp.sum(la_all[b][:, sl[h]], axis=0, keepdims=True) for b, h in prob]
    q_dec = [q[b][:, sl[h]] * (GLA_KEY ** -0.5) * jnp.exp(bc_) for (b, h), bc_ in zip(prob, bc)]
    attn = [jnp.where(causal, _dot_nt(qd, k_ * jnp.exp(-bc_)), 0.0) for qd, k_, bc_ in zip(q_dec, kh, bc)]
    inter = [_dot_nt(qd, st_) for qd, st_ in zip(q_dec, st)]
    o = [i_ + _dot(a_, v_) for i_, a_, v_ in zip(inter, attn, vh)]
    k_dec = [k_ * jnp.exp(bl - bc_) for k_, bl, bc_ in zip(kh, b_last, bc)]
    s_new = [st_ * jnp.exp(bl) + _dot_tn(v_, kd) for st_, bl, v_, kd in zip(st, b_last, vh, k_dec)]
    outs = [jnp.concatenate(o[b * HEADS:(b + 1) * HEADS], axis=-1) for b in range(nb)]
    return outs, [s_new[b * HEADS:(b + 1) * HEADS] for b in range(nb)]


def _dn_qkv(y):
    act = _silu(y)
    parts = []
    for i in range(2 * HEADS):
        xh = act[:, i * HD:(i + 1) * HD]
        xh = xh * lax.rsqrt(jnp.sum(xh * xh, axis=-1, keepdims=True) + EPS)
        parts.append(xh * (HD ** -0.5) if i < HEADS else xh)
    qk = jnp.concatenate(parts, axis=-1)
    return qk[:, :HEADS * HD], qk[:, HEADS * HD:], act[:, 2 * HEADS * HD:]


def _dn_gates(small, alog_row, dt_row):
    lane = _iota(small.shape, 1)
    log_a = -jnp.exp(alog_row) * _softplus(small + dt_row)
    return jnp.where(lane < SM_B, log_a, jnp.where(lane < SM_R, _sigmoid(small), 0.0))


def _gate_norm(o, z, grow):
    parts = []
    for h in range(HEADS):
        oh = o[:, h * HD:(h + 1) * HD]
        parts.append(oh * lax.rsqrt(jnp.mean(oh * oh, axis=-1, keepdims=True) + EPS))
    return jnp.concatenate(parts, axis=-1) * grow * _silu(z)


def _conv_rows(xrows, w_ref, k_taps):
    n = xrows.shape[0]
    acc = xrows * w_ref[k_taps - 1:k_taps, :]
    for s in range(1, k_taps):
        acc = acc + pltpu.roll(xrows, s, 0) * w_ref[k_taps - 1 - s:k_taps - s, :]
    return acc


def _shift_up(x, s):
    return x if s == 0 else pltpu.roll(x, x.shape[0] - s, 0)


def _div_tile(n, cap, mult=8):
    best = None
    for t in range(mult, min(n, cap) + 1, mult):
        if n % t == 0:
            best = t
    return best if best is not None else n


def _halo_prev(tt):
    return lambda b, t: (b, jnp.maximum(t * (tt // HALO) - 1, 0))


def _halo_next(tt, t_total):
    return lambda b, t: (b, jnp.minimum((t + 1) * (tt // HALO), t_total // HALO - 1))


def _mm(a, b, mode, out_dtype, name, tm=512, tn=512, tk=None):
    if mode == "nn":
        (m, k), n = a.shape, b.shape[1]
    elif mode == "nt":
        (m, k), n = a.shape, b.shape[0]
    else:
        (k, m), n = a.shape, b.shape[1]
    tm, tn = min(tm, m), min(tn, n)
    tk = k if tk is None else min(tk, k)
    assert m % tm == 0 and n % tn == 0 and k % tk == 0, (name, a.shape, b.shape, tm, tn, tk)
    nk = k // tk
    if mode == "tn":
        a_spec = pl.BlockSpec((tk, tm), lambda i, j, kk: (kk, i))
    else:
        a_spec = pl.BlockSpec((tm, tk), lambda i, j, kk: (i, kk))
    if mode == "nt":
        b_spec = pl.BlockSpec((tn, tk), lambda i, j, kk: (j, kk))
    else:
        b_spec = pl.BlockSpec((tk, tn), lambda i, j, kk: (kk, j))
    dims = {"nn": ((1,), (0,)), "nt": ((1,), (1,)), "tn": ((0,), (0,))}[mode]

    def body(a_ref, b_ref, o_ref, *acc):
        p = _dg(a_ref[...], b_ref[...], dims)
        if nk == 1:
            o_ref[...] = p.astype(out_dtype)
        else:
            kk = pl.program_id(2)

            @pl.when(kk == 0)
            def _():
                acc[0][...] = p

            @pl.when(kk > 0)
            def _():
                acc[0][...] += p

            @pl.when(kk == nk - 1)
            def _():
                o_ref[...] = acc[0][...].astype(out_dtype)

    return pl.pallas_call(
        body, name=name, grid=(m // tm, n // tn, nk),
        in_specs=[a_spec, b_spec],
        out_specs=pl.BlockSpec((tm, tn), lambda i, j, kk: (i, j)),
        out_shape=jax.ShapeDtypeStruct((m, n), out_dtype),
        scratch_shapes=[pltpu.VMEM((tm, tn), F32)] if nk > 1 else [],
        compiler_params=_params(("parallel", "parallel", "arbitrary")),
    )(a, b)


def _ada_fwd(c_all, w_ada, b_cols):
    def body(c_ref, w_ref, b_ref, o_ref):
        cond = _silu(c_ref[...]).astype(MXU_DT)
        o_ref[...] = _dot(cond, w_ref[...].astype(MXU_DT)) + b_ref[...]

    return pl.pallas_call(body, name="ada_fwd", out_shape=jax.ShapeDtypeStruct((c_all.shape[0], w_ada.shape[1]), F32),
                          compiler_params=_params())(c_all, w_ada, b_cols)


def _ada_bwd(c_all, dmod_all, dmod_cols):
    def body(c_ref, da_ref, dc_ref, gw_ref, gb_ref):
        cond = _silu(c_ref[...]).astype(MXU_DT)
        gw_ref[...] = _dot_tn(cond, dc_ref[...].astype(MXU_DT))
        gb_ref[...] = jnp.sum(da_ref[...], axis=0, keepdims=True)

    return pl.pallas_call(
        body, name="ada_bwd",
        out_shape=(jax.ShapeDtypeStruct((c_all.shape[1], dmod_cols.shape[1]), F32),
                   jax.ShapeDtypeStruct((1, dmod_all.shape[1]), F32)),
        compiler_params=_params())(c_all, dmod_all, dmod_cols)


def _tok_spec(tt, width=D):
    return pl.BlockSpec((1, tt, width), lambda b, t: (b, t, 0))


def _vec_spec(width=D):
    return pl.BlockSpec((1, width), lambda b, t: (0, 0))


def _bvec_spec(width=D):
    return pl.BlockSpec((1, 1, width), lambda b, t: (b, 0, 0))


def _ln0_mod(x, g0, b0, sc, sh):
    bsz, t_total, _ = x.shape
    tt = _div_tile(t_total, ROW_TILE)

    def body(x_ref, g_ref, b_ref, sc_ref, sh_ref, h_ref):
        xh, _ = _ln_stats(x_ref[0])
        x0 = xh * g_ref[...] + b_ref[...]
        h_ref[0] = (x0 * (1.0 + sc_ref[0]) + sh_ref[0]).astype(MXU_DT)

    return pl.pallas_call(
        body, name="ln0_mod", grid=(bsz, t_total // tt),
        in_specs=[_tok_spec(tt), _vec_spec(), _vec_spec(), _bvec_spec(), _bvec_spec()],
        out_specs=_tok_spec(tt), out_shape=jax.ShapeDtypeStruct(x.shape, MXU_DT),
        compiler_params=_params(("parallel", "parallel")))(x, g0, b0, sc, sh)


def _res_ln_mod(x, y, gt, g0, b0, g1, b1, sc, sh):
    bsz, t_total, _ = x.shape
    tt = _div_tile(t_total, ROW_TILE)

    def body(x_ref, y_ref, gt_ref, g0_ref, b0_ref, g1_ref, b1_ref, sc_ref, sh_ref, r_ref, h_ref):
        xh, _ = _ln_stats(x_ref[0])
        r = ALPHA * (xh * g0_ref[...] + b0_ref[...]) + (1.0 + gt_ref[0]) * y_ref[0].astype(F32)
        r_ref[0] = r
        rh, _ = _ln_stats(r)
        x1 = rh * g1_ref[...] + b1_ref[...]
        h_ref[0] = (x1 * (1.0 + sc_ref[0]) + sh_ref[0]).astype(MXU_DT)

    return pl.pallas_call(
        body, name="res_ln_mod", grid=(bsz, t_total // tt),
        in_specs=[_tok_spec(tt), _tok_spec(tt), _bvec_spec(), _vec_spec(), _vec_spec(), _vec_spec(), _vec_spec(),
                  _bvec_spec(), _bvec_spec()],
        out_specs=(_tok_spec(tt), _tok_spec(tt)),
        out_shape=(jax.ShapeDtypeStruct(x.shape, F32), jax.ShapeDtypeStruct(x.shape, MXU_DT)),
        compiler_params=_params(("parallel", "parallel")))(x, y, gt, g0, b0, g1, b1, sc, sh)


def _final_fwd_bwd(r1, y2, gt, g1, b1, g2, b2, target):
    bsz, t_total, _ = r1.shape
    tt = _div_tile(t_total, ROW_TILE)

    def body(r1_ref, y2_ref, gt_ref, g1_ref, b1_ref, g2_ref, b2_ref, tg_ref,
             loss_ref, dr2_ref, dy2_ref, dgt_ref, dg2_ref, db2_ref):
        b, t = pl.program_id(0), pl.program_id(1)

        @pl.when((b == 0) & (t == 0))
        def _():
            loss_ref[...] = jnp.zeros_like(loss_ref)
            dg2_ref[...] = jnp.zeros_like(dg2_ref)
            db2_ref[...] = jnp.zeros_like(db2_ref)

        @pl.when(t == 0)
        def _():
            dgt_ref[...] = jnp.zeros_like(dgt_ref)

        rh1, _ = _ln_stats(r1_ref[0])
        x1 = rh1 * g1_ref[...] + b1_ref[...]
        y2 = y2_ref[0].astype(F32)
        gate = 1.0 + gt_ref[0]
        xh2, rstd2 = _ln_stats(ALPHA * x1 + gate * y2)
        err = xh2 * g2_ref[...] + b2_ref[...] - tg_ref[0]
        loss_ref[...] += jnp.sum(err * err, axis=0, keepdims=True)
        dx2 = err * (1.0 / D)
        dg2_ref[...] += jnp.sum(dx2 * xh2, axis=0, keepdims=True)
        db2_ref[...] += jnp.sum(dx2, axis=0, keepdims=True)
        dr2 = _ln_bwd(dx2 * g2_ref[...], xh2, rstd2)
        dr2_ref[0] = dr2
        dy2_ref[0] = (gate * dr2).astype(MXU_DT)
        dgt_ref[0] += jnp.sum(dr2 * y2, axis=0, keepdims=True)

    vec_out = jax.ShapeDtypeStruct((1, D), F32)
    return pl.pallas_call(
        body, name="final_fwd_bwd", grid=(bsz, t_total // tt),
        in_specs=[_tok_spec(tt), _tok_spec(tt), _bvec_spec(), _vec_spec(), _vec_spec(), _vec_spec(), _vec_spec(),
                  _tok_spec(tt)],
        out_specs=(_vec_spec(), _tok_spec(tt), _tok_spec(tt), _bvec_spec(), _vec_spec(), _vec_spec()),
        out_shape=(vec_out, jax.ShapeDtypeStruct(r1.shape, F32), jax.ShapeDtypeStruct(r1.shape, MXU_DT),
                   jax.ShapeDtypeStruct((bsz, 1, D), F32), vec_out, vec_out),
        compiler_params=_params(("arbitrary", "arbitrary")))(r1, y2, gt, g1, b1, g2, b2, target)


def _ln_bwd_call(name, d_res, d_h, src, g, b, sc, y=None, gt=None):
    bsz, t_total, _ = src.shape
    tt = _div_tile(t_total, ROW_TILE)
    has_y = y is not None

    def body(*refs):
        if has_y:
            (dres_ref, dh_ref, src_ref, g_ref, b_ref, sc_ref, y_ref, gt_ref,
             dsrc_ref, dsc_ref, dsh_ref, dg_ref, db_ref, dy_ref, dgt_ref) = refs
        else:
            (dres_ref, dh_ref, src_ref, g_ref, b_ref, sc_ref,
             dsrc_ref, dsc_ref, dsh_ref, dg_ref, db_ref) = refs
        bi, t = pl.program_id(0), pl.program_id(1)

        @pl.when((bi == 0) & (t == 0))
        def _():
            dg_ref[...] = jnp.zeros_like(dg_ref)
            db_ref[...] = jnp.zeros_like(db_ref)

        @pl.when(t == 0)
        def _():
            dsc_ref[...] = jnp.zeros_like(dsc_ref)
            dsh_ref[...] = jnp.zeros_like(dsh_ref)
            if has_y:
                dgt_ref[...] = jnp.zeros_like(dgt_ref)

        xh, rstd = _ln_stats(src_ref[0])
        xv = xh * g_ref[...] + b_ref[...]
        dh = dh_ref[0].astype(F32)
        dx = ALPHA * dres_ref[0] + dh * (1.0 + sc_ref[0])
        dsc_ref[0] += jnp.sum(dh * xv, axis=0, keepdims=True)
        dsh_ref[0] += jnp.sum(dh, axis=0, keepdims=True)
        dg_ref[...] += jnp.sum(dx * xh, axis=0, keepdims=True)
        db_ref[...] += jnp.sum(dx, axis=0, keepdims=True)
        dsrc = _ln_bwd(dx * g_ref[...], xh, rstd)
        dsrc_ref[0] = dsrc
        if has_y:
            dy_ref[0] = ((1.0 + gt_ref[0]) * dsrc).astype(MXU_DT)
            dgt_ref[0] += jnp.sum(dsrc * y_ref[0].astype(F32), axis=0, keepdims=True)

    vec_out = jax.ShapeDtypeStruct((1, D), F32)
    bvec_out = jax.ShapeDtypeStruct((bsz, 1, D), F32)
    in_specs = [_tok_spec(tt), _tok_spec(tt), _tok_spec(tt), _vec_spec(), _vec_spec(), _bvec_spec()]
    out_specs = [_tok_spec(tt), _bvec_spec(), _bvec_spec(), _vec_spec(), _vec_spec()]
    out_shape = [jax.ShapeDtypeStruct(src.shape, F32), bvec_out, bvec_out, vec_out, vec_out]
    args = [d_res, d_h, src, g, b, sc]
    if has_y:
        in_specs += [_tok_spec(tt), _bvec_spec()]
        out_specs += [_tok_spec(tt), _bvec_spec()]
        out_shape += [jax.ShapeDtypeStruct(src.shape, MXU_DT), bvec_out]
        args += [y, gt]
    return pl.pallas_call(body, name=name, grid=(bsz, t_total // tt), in_specs=in_specs, out_specs=tuple(out_specs),
                          out_shape=tuple(out_shape), compiler_params=_params(("arbitrary", "arbitrary")))(*args)


FFN_TC = 256
FFN_NJ = D_FF // FFN_TC
FFN_PW = 2 * FFN_TC


def _ffn_pair(a, axis):
    shp = list(a.shape)
    a4 = a.reshape(shp[:axis] + [2, FFN_NJ, FFN_TC] + shp[axis + 1:])
    return jnp.swapaxes(a4, axis, axis + 1).reshape(shp)


def _ffn_unpair(a, axis):
    shp = list(a.shape)
    a4 = a.reshape(shp[:axis] + [FFN_NJ, 2, FFN_TC] + shp[axis + 1:])
    return jnp.swapaxes(a4, axis, axis + 1).reshape(shp)


def _ffn_up_act(h, w_up, cw, cb):
    bsz, t_total, _ = h.shape
    tt = _div_tile(t_total, FFN_ROW_TILE)
    def body(h_ref, wu_ref, w_ref, b_ref, up_ref, o_ref, carry_ref):
        up_t = _dot_nt(h_ref[0], wu_ref[...])
        up_ref[0] = up_t
        prev = jnp.where(pl.program_id(2) == 0, 0.0, carry_ref[...])
        rows = jnp.concatenate([prev, up_t], axis=0)
        u = _conv_rows(rows, w_ref, FFN_CONV_K)[HALO:] + b_ref[...]
        o_ref[0] = (_silu(u[:, :FFN_TC]) * u[:, FFN_TC:]).astype(MXU_DT)
        carry_ref[...] = up_t[tt - HALO:, :]

    return pl.pallas_call(
        body, name="ffn_up_act", grid=(bsz, FFN_NJ, t_total // tt),
        in_specs=[pl.BlockSpec((1, tt, D), lambda b, j, t: (b, t, 0)),
                  pl.BlockSpec((FFN_PW, D), lambda b, j, t: (j, 0)),
                  pl.BlockSpec((FFN_CONV_K, FFN_PW), lambda b, j, t: (0, j)),
                  pl.BlockSpec((1, FFN_PW), lambda b, j, t: (0, j))],
        out_specs=(pl.BlockSpec((1, tt, FFN_PW), lambda b, j, t: (b, t, j)),
                   pl.BlockSpec((1, tt, FFN_TC), lambda b, j, t: (b, t, j))),
        out_shape=(jax.ShapeDtypeStruct((bsz, t_total, 2 * D_FF), F32),
                   jax.ShapeDtypeStruct((bsz, t_total, D_FF), MXU_DT)),
        scratch_shapes=[pltpu.VMEM((HALO, FFN_PW), F32)],
        compiler_params=_params(("parallel", "parallel", "arbitrary")))(h, w_up, cw, cb)


HALO16 = 16


def _ffn_act_bwd(up, dy2, w_down, cw, cb):
    bsz, t_total, width = up.shape
    tt = _div_tile(t_total, FFN_ROW_TILE)
    nt = t_total // tt
    hp, hn = _halo_prev(tt), _halo_next(tt, t_total)

    def body(x_ref, xp_ref, xn_ref, dy_ref, dyn_ref, wd_ref, w_ref, b_ref, dup_ref, dw_ref, db_ref):
        b, t = pl.program_id(1), pl.program_id(2)

        @pl.when((b == 0) & (t == 0))
        def _():
            dw_ref[...] = jnp.zeros_like(dw_ref)
            db_ref[...] = jnp.zeros_like(db_ref)

        prev = jnp.where(t == 0, 0.0, xp_ref[0])
        rows = jnp.concatenate([prev, x_ref[0], xn_ref[0]], axis=0)
        u = _conv_rows(rows, w_ref, FFN_CONV_K)[HALO:] + b_ref[...]
        g_pre, v_pre = u[:, :FFN_TC], u[:, FFN_TC:]
        valid = (_iota((tt + HALO, 1), 0) < tt) | (t < nt - 1)
        da = jnp.concatenate([_dot_nt(dy_ref[0], wd_ref[...]), _dot_nt(dyn_ref[0], wd_ref[...])[:HALO]], axis=0)
        da_ext = jnp.where(valid, da, 0.0)
        sg = _sigmoid(g_pre)
        gs = g_pre * sg
        du = jnp.concatenate([da_ext * v_pre * (sg + gs * (1.0 - sg)), da_ext * gs], axis=1)
        dup = du * w_ref[FFN_CONV_K - 1:FFN_CONV_K, :]
        for s in range(1, FFN_CONV_K):
            dup = dup + _shift_up(du, s) * w_ref[FFN_CONV_K - 1 - s:FFN_CONV_K - s, :]
        dup_ref[0] = dup[:tt].astype(MXU_DT)
        du_t = du[:tt]
        db_ref[...] += jnp.sum(du_t, axis=0, keepdims=True)
        for k in range(FFN_CONV_K):
            s = FFN_CONV_K - 1 - k
            xs = (rows if s == 0 else pltpu.roll(rows, s, 0))[HALO:HALO + tt]
            dw_ref[k:k + 1, :] += jnp.sum(du_t * xs, axis=0, keepdims=True)

    def halo(h, w):
        return pl.BlockSpec((1, HALO, w), lambda j, b, t: (*h(b, t), j))

    wspec = lambda rows_: pl.BlockSpec((rows_, FFN_PW), lambda j, b, t: (0, j))
    tile = pl.BlockSpec((1, tt, FFN_PW), lambda j, b, t: (b, t, j))
    dy_next = lambda j, b, t: (b, jnp.minimum((t + 1) * (tt // HALO16), t_total // HALO16 - 1), 0)
    return pl.pallas_call(
        body, name="ffn_act_bwd", grid=(FFN_NJ, bsz, nt),
        in_specs=[tile, halo(hp, FFN_PW), halo(hn, FFN_PW),
                  pl.BlockSpec((1, tt, D), lambda j, b, t: (b, t, 0)), pl.BlockSpec((1, HALO16, D), dy_next),
                  pl.BlockSpec((FFN_TC, D), lambda j, b, t: (j, 0)), wspec(FFN_CONV_K), wspec(1)],
        out_specs=(tile, wspec(FFN_CONV_K), wspec(1)),
        out_shape=(jax.ShapeDtypeStruct(up.shape, MXU_DT), jax.ShapeDtypeStruct((FFN_CONV_K, width), F32),
                   jax.ShapeDtypeStruct((1, width), F32)),
        compiler_params=_params(("arbitrary", "arbitrary", "arbitrary")))(up, up, up, dy2, dy2, w_down, cw, cb)


QKV_W = 3 * HEADS * HD
SM_BLK = P_SM // 128


def _dn_pre_fwd(proj, conv_w, alog_row, dt_row):
    bsz, t_total, _ = proj.shape
    tt = _div_tile(t_total, ROW_TILE)
    hp = _halo_prev(tt)

    def body(x_ref, xp_ref, sm_ref, w_ref, al_ref, dt_ref, q_ref, k_ref, v_ref, g_ref):
        prev = jnp.where(pl.program_id(1) == 0, 0.0, xp_ref[0])
        y = _conv_rows(jnp.concatenate([prev, x_ref[0]], axis=0), w_ref, DN_CONV_K)[HALO:]
        q_ref[0], k_ref[0], v_ref[0] = _dn_qkv(y)
        g_ref[0] = _dn_gates(sm_ref[0], al_ref[...], dt_ref[...])

    out512 = jax.ShapeDtypeStruct((bsz, t_total, HEADS * HD), F32)
    return pl.pallas_call(
        body, name="dn_pre_fwd", grid=(bsz, t_total // tt),
        in_specs=[pl.BlockSpec((1, tt, QKV_W), lambda b, t: (b, t, 0)),
                  pl.BlockSpec((1, HALO, QKV_W), lambda b, t: (*hp(b, t), 0)),
                  pl.BlockSpec((1, tt, 128), lambda b, t: (b, t, SM_BLK)),
                  pl.BlockSpec((DN_CONV_K, QKV_W), lambda b, t: (0, 0)), _vec_spec(128), _vec_spec(128)],
        out_specs=(_tok_spec(tt, 512), _tok_spec(tt, 512), _tok_spec(tt, 512), _tok_spec(tt, 128)),
        out_shape=(out512, out512, out512, jax.ShapeDtypeStruct((bsz, t_total, 128), F32)),
        compiler_params=_params(("parallel", "parallel")))(proj, proj, proj, conv_w, alog_row, dt_row)


def _dn_pre_bwd(proj, dq, dk, dv, dgates, conv_w, alog_row, dt_row):
    bsz, t_total, _ = proj.shape
    tt = _div_tile(t_total, 128)
    nt = t_total // tt
    hp, hn = _halo_prev(tt), _halo_next(tt, t_total)

    def body(x_ref, xp_ref, xn_ref, sm_ref, dq_ref, dqn_ref, dk_ref, dkn_ref, dv_ref, dvn_ref, dg_ref,
             w_ref, al_ref, dt_ref, dx_ref, dsm_ref, dw_ref, dal_ref, ddt_ref):
        b, t = pl.program_id(0), pl.program_id(1)

        @pl.when((b == 0) & (t == 0))
        def _():
            dw_ref[...] = jnp.zeros_like(dw_ref)
            dal_ref[...] = jnp.zeros_like(dal_ref)
            ddt_ref[...] = jnp.zeros_like(ddt_ref)

        prev = jnp.where(t == 0, 0.0, xp_ref[0])
        rows = jnp.concatenate([prev, x_ref[0], xn_ref[0]], axis=0)
        y = _conv_rows(rows, w_ref, DN_CONV_K)[HALO:]
        valid = (_iota((tt + HALO, 1), 0) < tt) | (t < nt - 1)

        def ext(tile_ref, next_ref):
            return jnp.where(valid, jnp.concatenate([tile_ref[0], next_ref[0]], axis=0), 0.0)

        _, vjp_qkv = jax.vjp(_dn_qkv, y)
        (dy,) = vjp_qkv((ext(dq_ref, dqn_ref), ext(dk_ref, dkn_ref), ext(dv_ref, dvn_ref)))
        dy = jnp.where(valid, dy, 0.0)
        dx = dy * w_ref[DN_CONV_K - 1:DN_CONV_K, :]
        for s in range(1, DN_CONV_K):
            dx = dx + _shift_up(dy, s) * w_ref[DN_CONV_K - 1 - s:DN_CONV_K - s, :]
        dx_ref[0] = dx[:tt].astype(MXU_DT)
        dy_t = dy[:tt]
        for k in range(DN_CONV_K):
            s = DN_CONV_K - 1 - k
            xs = (rows if s == 0 else pltpu.roll(rows, s, 0))[HALO:HALO + tt]
            dw_ref[k:k + 1, :] += jnp.sum(dy_t * xs, axis=0, keepdims=True)
        _, vjp_g = jax.vjp(_dn_gates, sm_ref[0], al_ref[...], dt_ref[...])
        dsm, dal, ddt = vjp_g(dg_ref[0])
        dsm_ref[0] = dsm
        dal_ref[...] += dal
        ddt_ref[...] += ddt

    def tile(width, blk=0):
        return pl.BlockSpec((1, tt, width), lambda b, t: (b, t, blk))

    def halo(h, width):
        return pl.BlockSpec((1, HALO, width), lambda b, t: (*h(b, t), 0))

    return pl.pallas_call(
        body, name="dn_pre_bwd", grid=(bsz, nt),
        in_specs=[tile(QKV_W), halo(hp, QKV_W), halo(hn, QKV_W), tile(128, SM_BLK),
                  tile(512), halo(hn, 512), tile(512), halo(hn, 512), tile(512), halo(hn, 512), tile(128),
                  pl.BlockSpec((DN_CONV_K, QKV_W), lambda b, t: (0, 0)), _vec_spec(128), _vec_spec(128)],
        out_specs=(tile(QKV_W), tile(128), pl.BlockSpec((DN_CONV_K, QKV_W), lambda b, t: (0, 0)),
                   _vec_spec(128), _vec_spec(128)),
        out_shape=(jax.ShapeDtypeStruct((bsz, t_total, QKV_W), MXU_DT), jax.ShapeDtypeStruct((bsz, t_total, 128), F32),
                   jax.ShapeDtypeStruct((DN_CONV_K, QKV_W), F32), jax.ShapeDtypeStruct((1, 128), F32),
                   jax.ShapeDtypeStruct((1, 128), F32)),
        compiler_params=_params(("arbitrary", "arbitrary")))(
            proj, proj, proj, proj, dq, dq, dk, dk, dv, dv, dgates, conv_w, alog_row, dt_row)


def _state_spec(bsz, idx):
    return pl.BlockSpec((bsz, 1, HEADS, HD, HD), lambda c: (0, idx(c), 0, 0, 0))


def _inv_spec(bsz, idx):
    return pl.BlockSpec((bsz, 1, HEADS, CHUNK, CHUNK), lambda c: (0, idx(c), 0, 0, 0))


def _chunk_spec(bsz, width, idx, blk=0):
    return pl.BlockSpec((bsz, CHUNK, width), lambda c: (0, idx(c), blk))


def _dn_rec_fwd(q, k, v, gates):
    bsz, t_total, _ = q.shape
    nc = t_total // CHUNK
    fwd = lambda c: c

    def body(q_ref, k_ref, v_ref, g_ref, o_ref, ss_ref, inv_ref, s_ref):
        @pl.when(pl.program_id(0) == 0)
        def _():
            s_ref[...] = jnp.zeros_like(s_ref)

        seqs = range(bsz)
        s_list = [[s_ref[b * HEADS + h] for h in range(HEADS)] for b in seqs]
        for b in seqs:
            for h in range(HEADS):
                ss_ref[b, 0, h] = s_list[b][h]
        o, new_s, invs = _dn_chunk(s_list, [q_ref[b] for b in seqs], [k_ref[b] for b in seqs],
                                   [v_ref[b] for b in seqs], [g_ref[b] for b in seqs], with_inv=True)
        for b in seqs:
            o_ref[b] = o[b]
            for h in range(HEADS):
                s_ref[b * HEADS + h] = new_s[b][h]
                inv_ref[b, 0, h] = invs[b * HEADS + h]

    return pl.pallas_call(
        body, name="dn_rec_fwd", grid=(nc,),
        in_specs=[_chunk_spec(bsz, 512, fwd)] * 3 + [_chunk_spec(bsz, 128, fwd)],
        out_specs=(_chunk_spec(bsz, 512, fwd), _state_spec(bsz, fwd), _inv_spec(bsz, fwd)),
        out_shape=(jax.ShapeDtypeStruct(q.shape, F32), jax.ShapeDtypeStruct((bsz, nc, HEADS, HD, HD), F32),
                   jax.ShapeDtypeStruct((bsz, nc, HEADS, CHUNK, CHUNK), F32)),
        scratch_shapes=[pltpu.VMEM((bsz * HEADS, HD, HD), F32)],
        compiler_params=_params(("arbitrary",)))(q, k, v, gates)


def _dn_rec_bwd(q, k, v, gates, states, invs, do):
    bsz, t_total, _ = q.shape
    nc = t_total // CHUNK
    rev = lambda c: nc - 1 - c

    def body(q_ref, k_ref, v_ref, g_ref, ss_ref, inv_ref, do_ref, dq_ref, dk_ref, dv_ref, dg_ref, ds_ref):
        @pl.when(pl.program_id(0) == 0)
        def _():
            ds_ref[...] = jnp.zeros_like(ds_ref)

        seqs = range(bsz)
        s_list = [[ss_ref[b, 0, h] for h in range(HEADS)] for b in seqs]
        known = [inv_ref[b, 0, h] for b in seqs for h in range(HEADS)]
        _, vjp = jax.vjp(functools.partial(_dn_chunk, inv_known=known),
                         s_list, [q_ref[b] for b in seqs], [k_ref[b] for b in seqs],
                         [v_ref[b] for b in seqs], [g_ref[b] for b in seqs])
        ds_in, dq, dk, dv, dg = vjp(([do_ref[b] for b in seqs],
                                     [[ds_ref[b * HEADS + h] for h in range(HEADS)] for b in seqs]))
        for b in seqs:
            dq_ref[b], dk_ref[b], dv_ref[b], dg_ref[b] = dq[b], dk[b], dv[b], dg[b]
            for h in range(HEADS):
                ds_ref[b * HEADS + h] = ds_in[b][h]

    tok = lambda width: _chunk_spec(bsz, width, rev)
    out512 = jax.ShapeDtypeStruct(q.shape, F32)
    return pl.pallas_call(
        body, name="dn_rec_bwd", grid=(nc,),
        in_specs=[tok(512), tok(512), tok(512), tok(128), _state_spec(bsz, rev), _inv_spec(bsz, rev), tok(512)],
        out_specs=(tok(512), tok(512), tok(512), tok(128)),
        out_shape=(out512, out512, out512, jax.ShapeDtypeStruct(gates.shape, F32)),
        scratch_shapes=[pltpu.VMEM((bsz * HEADS, HD, HD), F32)],
        compiler_params=_params(("arbitrary",)))(q, k, v, gates, states, invs, do)


GQ_BLK, GK_BLK, GV_BLK = P_GQ // 512, P_GK // 512, P_GV // 512


def _gla_rec_fwd(proj, w2, bg):
    bsz, t_total, _ = proj.shape
    nc = t_total // CHUNK

    fwd = lambda c: c

    def body(q_ref, k_ref, v_ref, sm_ref, w2_ref, bg_ref, o_ref, ss_ref, s_ref):
        @pl.when(pl.program_id(0) == 0)
        def _():
            s_ref[...] = jnp.zeros_like(s_ref)

        seqs = range(bsz)
        s_list = [[s_ref[b * HEADS + h] for h in range(HEADS)] for b in seqs]
        for b in seqs:
            for h in range(HEADS):
                ss_ref[b, 0, h] = s_list[b][h]
        o, new_s = _gla_chunk(s_list, [q_ref[b] for b in seqs], [k_ref[b] for b in seqs], [v_ref[b] for b in seqs],
                              [sm_ref[b] for b in seqs], w2_ref[...], bg_ref[...])
        for b in seqs:
            o_ref[b] = o[b]
            for h in range(HEADS):
                s_ref[b * HEADS + h] = new_s[b][h]

    col = lambda blk, width=512: _chunk_spec(bsz, width, fwd, blk)
    return pl.pallas_call(
        body, name="gla_rec_fwd", grid=(nc,),
        in_specs=[col(GQ_BLK), col(GK_BLK), col(GV_BLK), col(SM_BLK, 128),
                  pl.BlockSpec((128, 512), lambda c: (0, 0)), pl.BlockSpec((1, 512), lambda c: (0, 0))],
        out_specs=(col(0), _state_spec(bsz, fwd)),
        out_shape=(jax.ShapeDtypeStruct((bsz, t_total, 512), F32),
                   jax.ShapeDtypeStruct((bsz, nc, HEADS, HD, HD), F32)),
        scratch_shapes=[pltpu.VMEM((bsz * HEADS, HD, HD), F32)],
        compiler_params=_params(("arbitrary",)))(proj, proj, proj, proj, w2, bg)


def _gla_rec_bwd(proj, w2, bg, states, do, dsm_dn):
    bsz, t_total, _ = proj.shape
    nc = t_total // CHUNK
    rev = lambda c: nc - 1 - c

    def body(q_ref, k_ref, v_ref, sm_ref, w2_ref, bg_ref, ss_ref, do_ref, dsd_ref,
             dq_ref, dk_ref, dv_ref, dsm_ref, dw2_ref, dbg_ref, ds_ref):
        @pl.when(pl.program_id(0) == 0)
        def _():
            dw2_ref[...] = jnp.zeros_like(dw2_ref)
            dbg_ref[...] = jnp.zeros_like(dbg_ref)
            ds_ref[...] = jnp.zeros_like(ds_ref)

        seqs = range(bsz)
        s_list = [[ss_ref[b, 0, h] for h in range(HEADS)] for b in seqs]
        _, vjp = jax.vjp(_gla_chunk, s_list, [q_ref[b] for b in seqs], [k_ref[b] for b in seqs],
                         [v_ref[b] for b in seqs], [sm_ref[b] for b in seqs], w2_ref[...], bg_ref[...])
        ds_in, dq, dk, dv, dsm, dw2, dbg = vjp(([do_ref[b] for b in seqs],
                                                [[ds_ref[b * HEADS + h] for h in range(HEADS)] for b in seqs]))
        for b in seqs:
            dq_ref[b], dk_ref[b], dv_ref[b] = dq[b].astype(MXU_DT), dk[b].astype(MXU_DT), dv[b].astype(MXU_DT)
            dsm_ref[b] = (dsm[b] + dsd_ref[b]).astype(MXU_DT)
            for h in range(HEADS):
                ds_ref[b * HEADS + h] = ds_in[b][h]
        dw2_ref[...] += dw2
        dbg_ref[...] += dbg

    col = lambda blk, width=512: _chunk_spec(bsz, width, rev, blk)
    w2_spec = pl.BlockSpec((128, 512), lambda c: (0, 0))
    bg_spec = pl.BlockSpec((1, 512), lambda c: (0, 0))
    out512 = jax.ShapeDtypeStruct((bsz, t_total, 512), MXU_DT)
    return pl.pallas_call(
        body, name="gla_rec_bwd", grid=(nc,),
        in_specs=[col(GQ_BLK), col(GK_BLK), col(GV_BLK), col(SM_BLK, 128), w2_spec, bg_spec,
                  _state_spec(bsz, rev), col(0), col(0, 128)],
        out_specs=(col(0), col(0), col(0), col(0, 128), w2_spec, bg_spec),
        out_shape=(out512, out512, out512, jax.ShapeDtypeStruct((bsz, t_total, 128), MXU_DT),
                   jax.ShapeDtypeStruct((128, 512), F32), jax.ShapeDtypeStruct((1, 512), F32)),
        scratch_shapes=[pltpu.VMEM((bsz * HEADS, HD, HD), F32)],
        compiler_params=_params(("arbitrary",)))(proj, proj, proj, proj, w2, bg, states, do, dsm_dn)


Z_BLK, GG_BLK = P_Z // 512, P_GG // 512


def _mix_out_fwd(o_dn, o_gla, proj, grow_dn, grow_gla):
    bsz, t_total, _ = o_dn.shape
    tt = _div_tile(t_total, ROW_TILE)

    def body(od_ref, og_ref, z_ref, gg_ref, gd_ref, gl_ref, o_ref):
        o_ref[0, :, :512] = _gate_norm(od_ref[0], z_ref[0], gd_ref[...]).astype(MXU_DT)
        o_ref[0, :, 512:] = _gate_norm(og_ref[0], gg_ref[0], gl_ref[...]).astype(MXU_DT)

    def col(blk):
        return pl.BlockSpec((1, tt, 512), lambda b, t: (b, t, blk))

    return pl.pallas_call(
        body, name="mix_out_fwd", grid=(bsz, t_total // tt),
        in_specs=[col(0), col(0), col(Z_BLK), col(GG_BLK), _vec_spec(512), _vec_spec(512)],
        out_specs=_tok_spec(tt), out_shape=jax.ShapeDtypeStruct((bsz, t_total, D), MXU_DT),
        compiler_params=_params(("parallel", "parallel")))(o_dn, o_gla, proj, proj, grow_dn, grow_gla)


def _mix_out_bwd(do, o_dn, o_gla, proj, grow_dn, grow_gla):
    bsz, t_total, _ = o_dn.shape
    tt = _div_tile(t_total, ROW_TILE)

    def body(do_ref, od_ref, og_ref, z_ref, gg_ref, gd_ref, gl_ref,
             dod_ref, dog_ref, dz_ref, dgg_ref, dgd_ref, dgl_ref):
        @pl.when((pl.program_id(0) == 0) & (pl.program_id(1) == 0))
        def _():
            dgd_ref[...] = jnp.zeros_like(dgd_ref)
            dgl_ref[...] = jnp.zeros_like(dgl_ref)

        def one(o_ref, gate_ref, g_ref, ct, do_out, dgate_out, dg_out):
            _, vjp = jax.vjp(_gate_norm, o_ref[0], gate_ref[0], g_ref[...])
            d_o, d_gate, d_row = vjp(ct)
            do_out[0] = d_o
            dgate_out[0] = d_gate.astype(MXU_DT)
            acc = d_row[:, :HD]
            for h in range(1, HEADS):
                acc = acc + d_row[:, h * HD:(h + 1) * HD]
            dg_out[...] += acc

        ct = do_ref[0].astype(F32)
        one(od_ref, z_ref, gd_ref, ct[:, :512], dod_ref, dz_ref, dgd_ref)
        one(og_ref, gg_ref, gl_ref, ct[:, 512:], dog_ref, dgg_ref, dgl_ref)

    def col(blk):
        return pl.BlockSpec((1, tt, 512), lambda b, t: (b, t, blk))

    f512 = jax.ShapeDtypeStruct((bsz, t_total, 512), F32)
    b512 = jax.ShapeDtypeStruct((bsz, t_total, 512), MXU_DT)
    g128 = jax.ShapeDtypeStruct((1, HD), F32)
    return pl.pallas_call(
        body, name="mix_out_bwd", grid=(bsz, t_total // tt),
        in_specs=[_tok_spec(tt), col(0), col(0), col(Z_BLK), col(GG_BLK), _vec_spec(512), _vec_spec(512)],
        out_specs=(col(0), col(0), col(0), col(0), _vec_spec(HD), _vec_spec(HD)),
        out_shape=(f512, f512, b512, b512, g128, g128),
        compiler_params=_params(("arbitrary", "arbitrary")))(do, o_dn, o_gla, proj, proj, grow_dn, grow_gla)


def _sum_slots(x, name):
    n, rows, cols = x.shape
    tr = _div_tile(rows, max(8, (1 << 19) // cols))

    def body(x_ref, o_ref):
        acc = x_ref[0].astype(F32)
        for i in range(1, n):
            acc = acc + x_ref[i].astype(F32)
        o_ref[...] = acc

    return pl.pallas_call(
        body, name=name, grid=(rows // tr,),
        in_specs=[pl.BlockSpec((n, tr, cols), lambda i: (0, i, 0))],
        out_specs=pl.BlockSpec((tr, cols), lambda i: (i, 0)),
        out_shape=jax.ShapeDtypeStruct((rows, cols), F32), compiler_params=_params(("parallel",)))(x)


def _adamw_math(w, g, m, v):
    nm = ADAM_B1 * m + (1.0 - ADAM_B1) * g
    nv = ADAM_B2 * v + (1.0 - ADAM_B2) * (g * g)
    m_hat = nm / (1.0 - ADAM_B1 ** ADAM_STEP)
    v_hat = nv / (1.0 - ADAM_B2 ** ADAM_STEP)
    return -ADAM_LR * (m_hat / (jnp.sqrt(v_hat) + ADAM_EPS) + ADAM_WD * w), nm, nv


def _adamw(w, g, m, v, name):
    _, rows, cols = w.shape
    tr = _div_tile(rows, max(8, (1 << 18) // cols))

    def body(w_ref, g_ref, m_ref, v_ref, d_ref, nm_ref, nv_ref):
        d_ref[...], nm_ref[...], nv_ref[...] = _adamw_math(w_ref[...], g_ref[...], m_ref[...], v_ref[...])

    spec = pl.BlockSpec((1, tr, cols), lambda i: (0, i, 0))
    shp = jax.ShapeDtypeStruct(w.shape, F32)
    return pl.pallas_call(body, name=name, grid=(rows // tr,), in_specs=[spec] * 4, out_specs=(spec,) * 3,
                          out_shape=(shp,) * 3, compiler_params=_params(("parallel",)))(w, g, m, v)


def _sum_adamw(parts, w, m, v, name):
    n, rows, cols = parts.shape
    tr = _div_tile(rows, max(8, (1 << 18) // cols))

    def body(p_ref, w_ref, m_ref, v_ref, g_ref, d_ref, nm_ref, nv_ref):
        g = p_ref[0].astype(F32)
        for i in range(1, n):
            g = g + p_ref[i].astype(F32)
        g_ref[...] = g
        d_ref[0], nm_ref[0], nv_ref[0] = _adamw_math(w_ref[0], g, m_ref[0], v_ref[0])

    spec = pl.BlockSpec((1, tr, cols), lambda i: (0, i, 0))
    shp = jax.ShapeDtypeStruct(w.shape, F32)
    return pl.pallas_call(
        body, name=name, grid=(rows // tr,),
        in_specs=[pl.BlockSpec((n, tr, cols), lambda i: (0, i, 0)), spec, spec, spec],
        out_specs=(pl.BlockSpec((tr, cols), lambda i: (i, 0)), spec, spec, spec),
        out_shape=(jax.ShapeDtypeStruct((rows, cols), F32), shp, shp, shp),
        compiler_params=_params(("parallel",)))(parts, w, m, v)


def _adamw_many(ws, gs, ms, vs, name):
    n = len(ws)

    def body(*refs):
        for i in range(n):
            d, nm, nv = _adamw_math(refs[i][...], refs[n + i][...], refs[2 * n + i][...], refs[3 * n + i][...])
            refs[4 * n + i][...] = d
            refs[5 * n + i][...] = nm
            refs[6 * n + i][...] = nv

    shapes = tuple(jax.ShapeDtypeStruct(w.shape, F32) for w in ws)
    outs = pl.pallas_call(body, name=name, out_shape=shapes * 3, compiler_params=_params())(*ws, *gs, *ms, *vs)
    return outs[:n], outs[n:2 * n], outs[2 * n:]


def _position():
    return lax.axis_index("x"), lax.axis_index("y"), lax.axis_index("c")


def _slot(px, py, pc):
    return 4 * px + 2 * py + pc


def _gather_small(x, name):
    rows, cols = x.shape

    def body(x_ref, o_ref, send_sems, recv_sems):
        mx, my, mc = _position()

        def peer(k):
            return (mx ^ ((k >> 2) & 1), my ^ ((k >> 1) & 1), mc ^ (k & 1))

        o_ref[_slot(mx, my, mc)] = x_ref[...]
        sends = []
        for k in range(1, N_DEV):
            cp = pltpu.make_async_remote_copy(src_ref=x_ref, dst_ref=o_ref.at[_slot(mx, my, mc)],
                                              send_sem=send_sems.at[k - 1], recv_sem=recv_sems.at[k - 1],
                                              device_id=peer(k), device_id_type=MESH)
            cp.start()
            sends.append(cp)
        for k in range(1, N_DEV):
            pltpu.make_async_remote_copy(src_ref=x_ref, dst_ref=o_ref.at[_slot(*peer(k))],
                                         send_sem=send_sems.at[k - 1], recv_sem=recv_sems.at[k - 1],
                                         device_id=peer(k), device_id_type=MESH).wait_recv()
        for cp in sends:
            cp.wait_send()

    return pl.pallas_call(
        body, name=name, out_shape=jax.ShapeDtypeStruct((N_DEV, rows, cols), x.dtype),
        in_specs=[pl.BlockSpec(memory_space=pltpu.VMEM)], out_specs=pl.BlockSpec(memory_space=pltpu.VMEM),
        scratch_shapes=[pltpu.SemaphoreType.DMA((N_DEV - 1,)), pltpu.SemaphoreType.DMA((N_DEV - 1,))],
        compiler_params=pltpu.CompilerParams(vmem_limit_bytes=VMEM_LIMIT_V7X))(x)


def _gather_big(shards):
    n = len(shards)

    def body(*refs):
        xs, outs = refs[:n], refs[n:2 * n]
        send_sems, recv_sems, local_sems = refs[2 * n:]
        mx, my, mc = _position()
        me, sibling = (mx, my, mc), (mx, my, 1 - mc)
        chips = [(1 - mx, my), (mx, 1 - my), (1 - mx, 1 - my)]

        def copy(a, k, block, to, src=None):
            dst = outs[a].at[_slot(*block)]
            return pltpu.make_async_remote_copy(src_ref=dst if src is None else src, dst_ref=dst,
                                                send_sem=send_sems.at[7 * a + k], recv_sem=recv_sems.at[7 * a + k],
                                                device_id=to, device_id_type=MESH)

        mine = [pltpu.make_async_copy(xs[a], outs[a].at[_slot(*me)], local_sems.at[a]) for a in range(n)]
        for cp in mine:
            cp.start()
        started = []
        for a in range(n):
            started.append(copy(a, 0, me, sibling, src=xs[a]))
            started += [copy(a, 1 + j, me, (*chip, mc), src=xs[a]) for j, chip in enumerate(chips)]
        for cp in started:
            cp.start()
        for j, chip in enumerate(chips):
            for a in range(n):
                copy(a, 1 + j, (*chip, mc), me).wait_recv()
                fwd = copy(a, 4 + j, (*chip, mc), sibling)
                fwd.start()
                started.append(fwd)
        for a in range(n):
            copy(a, 0, sibling, me).wait_recv()
            for j, chip in enumerate(chips):
                copy(a, 4 + j, (*chip, 1 - mc), me).wait_recv()
        for cp in started:
            cp.wait_send()
        for cp in mine:
            cp.wait()

    any_spec = pl.BlockSpec(memory_space=pl.ANY)
    return pl.pallas_call(
        body, name="gather_weights",
        out_shape=tuple(jax.ShapeDtypeStruct((N_DEV,) + s.shape, s.dtype) for s in shards),
        in_specs=[any_spec] * n, out_specs=(any_spec,) * n,
        scratch_shapes=[pltpu.SemaphoreType.DMA((7 * n,)), pltpu.SemaphoreType.DMA((7 * n,)),
                        pltpu.SemaphoreType.DMA((n,))])(*shards)


def _peer(pos, k):
    mx, my, mc = pos
    return (mx ^ ((k >> 2) & 1), my ^ ((k >> 1) & 1), mc ^ (k & 1))


def _exchange_copies(srcs, lands, send_sems, recv_sems, by_owner):
    pos = _position()
    me = _slot(*pos)
    out = []
    for a, (src, land) in enumerate(zip(srcs, lands)):
        for k in range(1, N_DEV):
            peer = _peer(pos, k)
            sems = dict(send_sem=send_sems.at[7 * a + k - 1], recv_sem=recv_sems.at[7 * a + k - 1],
                        device_id=peer, device_id_type=MESH)
            mine = src.at[_slot(*peer)] if by_owner else src
            send = pltpu.make_async_remote_copy(src_ref=mine, dst_ref=land.at[me], **sems)
            recv = pltpu.make_async_remote_copy(src_ref=mine, dst_ref=land.at[_slot(*peer)], **sems)
            out.append((send, recv))
    return out


_HBM_SPEC = pl.BlockSpec(memory_space=pltpu.HBM)
_SEM_SPEC = pl.BlockSpec(memory_space=pltpu.SEMAPHORE)
_DATAFLOW = pltpu.SideEffectType.DATAFLOW_SIDE_EFFECTING


def _exchange_start(name, srcs, slab_shapes, after, by_owner, carry=()):
    n, na, nc = len(srcs), len(after), len(carry)
    lands = [pltpu.with_memory_space_constraint(lax.empty((N_DEV,) + s, x.dtype), pltpu.HBM)
             for s, x in zip(slab_shapes, srcs)]
    thru = [pltpu.with_memory_space_constraint(x, pltpu.HBM) for x in [*srcs, *lands, *carry]]

    def body(*refs):
        src_refs, land_refs = refs[:n], refs[n:2 * n]
        send_sems, recv_sems = refs[len(thru) + na], refs[len(thru) + na + 1]
        token = refs[-1]
        for send, _ in _exchange_copies(src_refs, land_refs, send_sems, recv_sems, by_owner):
            send.start()
        token[...] = jnp.zeros_like(token)

    outs = pl.pallas_call(
        body, name=name,
        out_shape=(pltpu.SemaphoreType.DMA((7 * n,)), pltpu.SemaphoreType.DMA((7 * n,)),
                   *[pltpu.HBM(x.shape, x.dtype) for x in thru], jax.ShapeDtypeStruct((8, 128), F32)),
        in_specs=[_HBM_SPEC] * len(thru) + [pl.BlockSpec(memory_space=pl.ANY)] * na,
        out_specs=(_SEM_SPEC, _SEM_SPEC, *[_HBM_SPEC] * len(thru), pl.BlockSpec(memory_space=pltpu.VMEM)),
        input_output_aliases={i: 2 + i for i in range(len(thru))},
        compiler_params=pltpu.CompilerParams(has_side_effects=_DATAFLOW))(*thru, *after)
    return (outs[0], outs[1], list(outs[2:2 + n]), list(outs[2 + n:2 + 2 * n]), outs[-1],
            list(outs[2 + 2 * n:2 + 2 * n + nc]))


def _exchange_wait(name, send_sems, recv_sems, srcs, lands, after, by_owner):
    n = len(srcs)

    def body(*refs):
        src_refs, land_refs = refs[:n], refs[n:2 * n]
        s_sems, r_sems = refs[2 * n], refs[2 * n + 1]
        for send, recv in _exchange_copies(src_refs, land_refs, s_sems, r_sems, by_owner):
            send.wait_send()
            recv.wait_recv()

    outs = pl.pallas_call(
        body, name=name,
        out_shape=(*[pltpu.HBM(x.shape, x.dtype) for x in srcs], *[pltpu.HBM(l.shape, l.dtype) for l in lands]),
        in_specs=[_HBM_SPEC] * (2 * n) + [_SEM_SPEC, _SEM_SPEC, pl.BlockSpec(memory_space=pl.ANY)],
        out_specs=tuple([_HBM_SPEC] * (2 * n)),
        input_output_aliases={i: i for i in range(2 * n)},
        compiler_params=pltpu.CompilerParams(has_side_effects=_DATAFLOW))(*srcs, *lands, send_sems, recv_sems, after)
    return list(outs[:n]), list(outs[n:])


def _pad_heads(x, axis):
    shp = list(x.shape)
    x4 = x.reshape(shp[:axis] + [HEADS, GLA_KEY] + shp[axis + 1:])
    pad = [(0, 0)] * x4.ndim
    pad[axis + 1] = (0, HD - GLA_KEY)
    return jnp.pad(x4, pad).reshape(shp[:axis] + [HEADS * HD] + shp[axis + 1:])


def _unpad_heads(x, axis):
    shp = list(x.shape)
    x4 = x.reshape(shp[:axis] + [HEADS, HD] + shp[axis + 1:])
    x4 = lax.slice_in_dim(x4, 0, GLA_KEY, axis=axis + 1)
    return x4.reshape(shp[:axis] + [HEADS * GLA_KEY] + shp[axis + 1:])


O_Z_END, O_AB, O_GQ, O_GK, O_GV, O_R = 2048, 2048, 2056, 2312, 2568, 3592


def _padded_row(f):
    if f < O_Z_END:
        return f
    if f < O_GQ:
        return P_SM + (f - O_AB)
    if f < O_GV:
        base, g = (P_GQ, f - O_GQ) if f < O_GK else (P_GK, f - O_GK)
        return base + HD * (g // GLA_KEY) + g % GLA_KEY
    if f < O_R:
        return P_GV + (f - O_GV)
    return P_SM + 8 + (f - O_R)


def _runs(pairs):
    out = []
    for d, s in pairs:
        if out and out[-1][0] + out[-1][2] == d and out[-1][1] + out[-1][2] == s:
            out[-1][2] += 1
        else:
            out.append([d, s, 1])
    return out


def _pad_in_rows(shards):
    wt = shards.reshape(IN_W, D)
    return jnp.concatenate([
        wt[:O_Z_END], _pad_heads(wt[O_GQ:O_GK], 0), _pad_heads(wt[O_GK:O_GV], 0), wt[O_GV:O_R],
        wt[O_AB:O_GQ], wt[O_R:], jnp.zeros((P_W - P_SM - 8 - GATE_RANK, D), wt.dtype)], axis=0)


def _unpad_in_rows(gt):
    per = IN_W // N_DEV
    return jnp.stack([
        jnp.concatenate([gt[src:src + n] for _, src, n in
                         _runs([(f, _padded_row(f)) for f in range(j * per, (j + 1) * per)])], axis=0)
        for j in range(N_DEV)])


def _lane_row(vals, width=128):
    return jnp.pad(vals.reshape(1, -1), ((0, 0), (0, width - vals.size)))


SMALL_NAMES = ["ln0_g", "ln0_b", "b_ada", "dn_conv", "dn_a_log", "dn_dt_bias", "dn_norm_g", "gla_w_gate2",
               "gla_b_gate", "gla_norm_g", "ln1_g", "ln1_b", "ffn_conv", "ffn_conv_b", "ln2_g", "ln2_b"]
WEIGHTS = ["ln0_g", "ln0_b", "w_ada", "b_ada", "w_in", "dn_conv", "dn_a_log", "dn_dt_bias", "dn_norm_g",
           "gla_w_gate2", "gla_b_gate", "gla_norm_g", "w_o", "ln1_g", "ln1_b", "ffn_w_up", "ffn_conv", "ffn_conv_b",
           "ffn_w_down", "ln2_g", "ln2_b"]


def kernel(x, c, ln0_g, ln0_b, w_ada, b_ada, w_in, dn_conv, dn_a_log, dn_dt_bias, dn_norm_g, gla_w_gate2, gla_b_gate, gla_norm_g, w_o, ln1_g, ln1_b, ffn_w_up, ffn_conv, ffn_conv_b, ffn_w_down, ln2_g, ln2_b, loss_target, m_ln0_g, m_ln0_b, m_w_ada, m_b_ada, m_w_in, m_dn_conv, m_dn_a_log, m_dn_dt_bias, m_dn_norm_g, m_gla_w_gate2, m_gla_b_gate, m_gla_norm_g, m_w_o, m_ln1_g, m_ln1_b, m_ffn_w_up, m_ffn_conv, m_ffn_conv_b, m_ffn_w_down, m_ln2_g, m_ln2_b, v_ln0_g, v_ln0_b, v_w_ada, v_b_ada, v_w_in, v_dn_conv, v_dn_a_log, v_dn_dt_bias, v_dn_norm_g, v_gla_w_gate2, v_gla_b_gate, v_gla_norm_g, v_w_o, v_ln1_g, v_ln1_b, v_ffn_w_up, v_ffn_conv, v_ffn_conv_b, v_ffn_w_down, v_ln2_g, v_ln2_b):
    args = dict(locals())
    w_given = {n: args[n] for n in WEIGHTS}
    m_given = {n: args["m_" + n] for n in WEIGHTS}
    v_given = {n: args["v_" + n] for n in WEIGHTS}
    bsz, t_total, _ = x.shape
    ntok = bsz * t_total
    mx, my, mc = _position()
    me = _slot(mx, my, mc)

    pack1 = jnp.concatenate([c.reshape(-1), dn_conv.reshape(-1), gla_w_gate2.reshape(-1), ffn_conv.reshape(-1)])
    n1 = pack1.size
    rows1 = -(-n1 // 1024) * 8
    pack1 = jnp.pad(pack1, (0, rows1 * 128 - n1)).reshape(rows1, 128)
    got1 = _gather_small(pack1, "gather_cond").reshape(N_DEV, -1)
    o1 = bsz * D
    o2 = o1 + dn_conv.size
    o3 = o2 + gla_w_gate2.size
    c_all = got1[:, :o1].reshape(N_DEV * bsz, D)
    dn_conv_f = got1[:, o1:o2].reshape(N_DEV, DN_CONV_K, -1).transpose(1, 0, 2).reshape(DN_CONV_K, QKV_W)
    gate2_f = got1[:, o2:o3].reshape(N_DEV, GATE_RANK, -1).transpose(1, 0, 2).reshape(GATE_RANK, HEADS * GLA_KEY)
    ffn_conv_f = got1[:, o3:n1].reshape(N_DEV, FFN_CONV_K, -1).transpose(1, 0, 2).reshape(FFN_CONV_K, 2 * D_FF)

    win_t = w_in[0].T.astype(MXU_DT)
    wup_t = ffn_w_up[0].T.astype(MXU_DT)
    (win_all,) = _gather_big([win_t])
    win_p = _pad_in_rows(win_all)
    cw_p, cb_p = _ffn_pair(ffn_conv_f, 1), _ffn_pair(ffn_conv_b, 1)

    ncol = w_ada.shape[2]
    b_cols = lax.dynamic_slice_in_dim(b_ada, me * ncol, ncol, axis=1)
    mod_part = _ada_fwd(c_all, w_ada[0], b_cols)
    mod_all = _gather_small(mod_part.reshape(-1, 128), "gather_mod").reshape(N_DEV, N_DEV * bsz, ncol)
    mod = lax.dynamic_slice_in_dim(mod_all, me * bsz, bsz, axis=1).transpose(1, 0, 2).reshape(bsz, 6, 1, D)
    late = [w_o[0].astype(MXU_DT), wup_t, ffn_w_down[0].astype(MXU_DT)]
    ag_send, ag_recv, ag_src, ag_land, ag_token, _ = _exchange_start(
        "gather_start", late, [w.shape for w in late], [win_all, mod_all], by_owner=False)
    mod = mod + ag_token[0, 0]
    sh_a, sc_a, gt_a, sh_f, sc_f, gt_f = (mod[:, i] for i in range(6))

    g0, b0 = ln0_g.reshape(1, D), ln0_b.reshape(1, D)
    alog_row, dt_row = _lane_row(dn_a_log[0]), _lane_row(dn_dt_bias[0])
    grow_dn, grow_gla = jnp.tile(dn_norm_g, (1, HEADS)), jnp.tile(gla_norm_g, (1, HEADS))
    w2 = jnp.zeros((128, HEADS * HD), F32).at[SM_R:SM_R + GATE_RANK].set(_pad_heads(gate2_f, 1))
    bg = _pad_heads(gla_b_gate, 1)

    h_a = _ln0_mod(x, g0, b0, sc_a, sh_a)
    proj = _mm(h_a.reshape(ntok, D), win_p, "nt", F32, "mm_proj", tm=1024, tn=1408).reshape(bsz, t_total, P_W)
    q, k, v, gates = _dn_pre_fwd(proj, dn_conv_f, alog_row, dt_row)
    o_dn, s_dn, inv_dn = _dn_rec_fwd(q, k, v, gates)
    o_gla, s_gla = _gla_rec_fwd(proj, w2, bg)
    o_mix = _mix_out_fwd(o_dn, o_gla, proj, grow_dn, grow_gla)
    late, landed = _exchange_wait("gather_wait", ag_send, ag_recv, ag_src, ag_land, o_mix, by_owner=False)
    wo_all, wup_all, wdn_all = (lax.dynamic_update_slice(l, w[None], (me, 0, 0)) for l, w in zip(landed, late))
    wo_f = wo_all.reshape(D, D)
    wup_f = _ffn_pair(wup_all.reshape(2 * D_FF, D), 0)
    wdn_f = wdn_all.reshape(D_FF, D)
    y = _mm(o_mix.reshape(ntok, D), wo_f, "nn", MXU_DT, "mm_wo", tm=1024, tn=1024).reshape(bsz, t_total, D)
    r1, h_f = _res_ln_mod(x, y, gt_a, g0, b0, ln1_g, ln1_b, sc_f, sh_f)
    up, act = _ffn_up_act(h_f, wup_f, cw_p, cb_p)
    y2 = _mm(act.reshape(ntok, D_FF), wdn_f, "nn", MXU_DT, "mm_down", tm=1024, tn=1024).reshape(bsz, t_total, D)
    loss_rows, dr2, dy2, dgt_f, d_ln2_g, d_ln2_b = _final_fwd_bwd(r1, y2, gt_f, ln1_g, ln1_b, ln2_g, ln2_b, loss_target)
    loss_part = (0.5 / D) * jnp.sum(loss_rows)

    dy2_2 = dy2.reshape(ntok, D)
    g_wdn = _mm(act.reshape(ntok, D_FF), dy2_2, "tn", MXU_DT, "mm_gwdn", tm=1408, tn=1024)
    dup, d_cw_p, d_cb_p = _ffn_act_bwd(up, dy2, wdn_f, cw_p, cb_p)
    d_ffn_conv, d_ffn_conv_b = _ffn_unpair(d_cw_p, 1), _ffn_unpair(d_cb_p, 1)
    dup_2 = dup.reshape(ntok, 2 * D_FF)
    dh_f = _mm(dup_2, wup_f, "nn", MXU_DT, "mm_dhf", tn=1024).reshape(bsz, t_total, D)
    g_wup_t = _mm(dup_2, h_f.reshape(ntok, D), "tn", MXU_DT, "mm_gwup", tm=1408, tn=1024)
    ffn_parts = [_ffn_unpair(g_wup_t, 0).reshape(N_DEV, -1, D), g_wdn.reshape(N_DEV, -1, D)]
    rs_send, rs_recv, rs_src, rs_land, rs_token, _ = _exchange_start(
        "scatter_start", ffn_parts, [p.shape[1:] for p in ffn_parts], [dh_f], by_owner=True)
    dr1, dsc_f, dsh_f, d_ln1_g, d_ln1_b, dy, dgt_a = _ln_bwd_call(
        "ln1_bwd", dr2, dh_f, r1, ln1_g, ln1_b, sc_f + rs_token[0, 0], y=y, gt=gt_a)

    dy_2 = dy.reshape(ntok, D)
    do = _mm(dy_2, wo_f, "nt", MXU_DT, "mm_do", tm=1024, tn=1024).reshape(bsz, t_total, D)
    g_wo = _mm(o_mix.reshape(ntok, D), dy_2, "tn", MXU_DT, "mm_gwo", tm=512, tn=1024)
    do_dn, do_gla, dz, dgg, d_dn_norm, d_gla_norm = _mix_out_bwd(do, o_dn, o_gla, proj, grow_dn, grow_gla)
    dq, dk, dv, dgates = _dn_rec_bwd(q, k, v, gates, s_dn, inv_dn, do_dn)
    dqkv, dsm_dn, d_dn_conv, d_alog_row, d_dt_row = _dn_pre_bwd(proj, dq, dk, dv, dgates, dn_conv_f, alog_row, dt_row)
    dgq, dgk, dgv, dsm, d_w2, d_bg = _gla_rec_bwd(proj, w2, bg, s_gla, do_gla, dsm_dn)
    dproj = jnp.concatenate([dqkv, dz, dgq, dgk, dgv, dgg, dsm], axis=-1).reshape(ntok, P_W)
    g_win_p = _mm(dproj, h_a.reshape(ntok, D), "tn", MXU_DT, "mm_gwin", tm=1408, tn=1024)
    mix_parts = [_unpad_in_rows(g_win_p), g_wo.reshape(N_DEV, -1, D)]
    rs2_send, rs2_recv, rs2_src, rs2_land, rs2_token, (win_p_late,) = _exchange_start(
        "scatter_mix_start", mix_parts, [p.shape[1:] for p in mix_parts], [], by_owner=True, carry=[win_p])
    dh_a = _mm(dproj, win_p_late, "nn", MXU_DT, "mm_dha", tn=1024).reshape(bsz, t_total, D)
    grad_x, dsc_a, dsh_a, d_ln0_g, d_ln0_b = _ln_bwd_call(
        "ln0_bwd", dr1, dh_a, x, g0, b0, sc_a + rs2_token[0, 0])

    delta, new_m, new_v, big_grads = {}, {}, {}, {}
    flip = lambda a: jnp.swapaxes(a, 1, 2)

    def update_owned(n, landed, mine):
        parts = lax.dynamic_update_slice(landed, lax.dynamic_slice_in_dim(mine, me, 1, axis=0), (me, 0, 0))
        turn = flip if parts.shape[1:] != w_given[n].shape[1:] else (lambda a: a)
        g, d_, m_, v_ = _sum_adamw(parts, turn(w_given[n]), turn(m_given[n]), turn(v_given[n]), "adamw_" + n)
        big_grads[n], delta[n], new_m[n], new_v[n] = turn(g[None]), turn(d_), turn(m_), turn(v_)

    ffn_parts, ffn_landed = _exchange_wait("scatter_wait", rs_send, rs_recv, rs_src, rs_land, grad_x, by_owner=True)
    update_owned("ffn_w_up", ffn_landed[0], ffn_parts[0])
    update_owned("ffn_w_down", ffn_landed[1], ffn_parts[1])
    ffn_done = 0.0 * (new_v["ffn_w_up"][0, 0, 0] + new_v["ffn_w_down"][0, 0, 0])

    dmod = jnp.concatenate([dsh_a, dsc_a, dgt_a, dsh_f, dsc_f, dgt_f], axis=1).reshape(-1)
    small_parts = {
        "ln0_g": d_ln0_g, "ln0_b": d_ln0_b, "ln1_g": d_ln1_g, "ln1_b": d_ln1_b, "ln2_g": d_ln2_g, "ln2_b": d_ln2_b,
        "dn_a_log": d_alog_row[:, :HEADS], "dn_dt_bias": d_dt_row[:, :HEADS],
        "dn_norm_g": d_dn_norm, "gla_norm_g": d_gla_norm, "gla_b_gate": _unpad_heads(d_bg, 1),
        "ffn_conv_b": d_ffn_conv_b, "dn_conv": d_dn_conv,
        "gla_w_gate2": _unpad_heads(d_w2[SM_R:SM_R + GATE_RANK], 1), "ffn_conv": d_ffn_conv}
    order = sorted(small_parts)
    flat = jnp.concatenate([small_parts[n].reshape(-1) for n in order] + [(loss_part + ffn_done).reshape(1), dmod])
    n3 = flat.size
    rows3 = -(-n3 // 1024) * 8
    pack3 = jnp.pad(flat, (0, rows3 * 128 - n3)).reshape(rows3, 128)
    got3 = _gather_small(pack3, "gather_small_grads")
    tot3 = _sum_slots(got3, "sum_small_grads").reshape(-1)
    grads = {}
    off = 0
    for n in order:
        size = small_parts[n].size
        grads[n] = tot3[off:off + size]
        off += size
    loss = tot3[off]
    off += 1
    dmod_all = got3.reshape(N_DEV, -1)[:, off:off + dmod.size].reshape(N_DEV * bsz, 6 * D)
    dmod_cols = lax.dynamic_slice_in_dim(dmod_all, me * ncol, ncol, axis=1)
    g_wada, g_bada = _ada_bwd(c_all, dmod_all, dmod_cols)
    grads["b_ada"] = g_bada

    def col_shard(full, rows):
        part = full.reshape(rows, -1)
        width = part.shape[1] // N_DEV
        return lax.dynamic_slice_in_dim(part, me * width, width, axis=1)

    grads["dn_conv"] = col_shard(grads["dn_conv"], DN_CONV_K)
    grads["gla_w_gate2"] = col_shard(grads["gla_w_gate2"], GATE_RANK)
    grads["ffn_conv"] = col_shard(grads["ffn_conv"], FFN_CONV_K)
    grads = {n: g.reshape(w_given[n].shape) for n, g in grads.items()}
    mix_parts, mix_landed = _exchange_wait("scatter_mix_wait", rs2_send, rs2_recv, rs2_src, rs2_land, grad_x,
                                           by_owner=True)
    update_owned("w_in", mix_landed[0], mix_parts[0])
    update_owned("w_o", mix_landed[1], mix_parts[1])
    grads["w_ada"] = g_wada.reshape(w_ada.shape)
    delta["w_ada"], new_m["w_ada"], new_v["w_ada"] = _adamw(w_ada, grads["w_ada"], m_w_ada, v_w_ada, "adamw_w_ada")
    grads.update(big_grads)
    d_s, m_s, v_s = _adamw_many(*[[src[n] for n in SMALL_NAMES] for src in (w_given, grads, m_given, v_given)],
                                "adamw_small")
    for i, n in enumerate(SMALL_NAMES):
        delta[n], new_m[n], new_v[n] = d_s[i], m_s[i], v_s[i]

    return (loss, grad_x, *[grads[n] for n in WEIGHTS], *[delta[n] for n in WEIGHTS],
            *[new_m[n] for n in WEIGHTS], *[new_v[n] for n in WEIGHTS])
```

```python
import functools

import jax
import jax.numpy as jnp
from jax import lax
from jax.experimental import pallas as pl
from jax.experimental.pallas import tpu as pltpu

F32 = jnp.float32
MXU_DT = jnp.bfloat16
MESH = pl.DeviceIdType.MESH
N_DEV = 8

D = 1024
HEADS = 4
HD = 128
CHUNK = 64
GLA_KEY = 64
GLA_TAU = 16.0
GATE_RANK = 16
D_FF = 2816
IN_W = 3608
ALPHA = 2.0 ** 0.25
EPS = 1e-6
DN_CONV_K = 4
FFN_CONV_K = 3
HALO = 8
ROW_TILE = 1024
FFN_ROW_TILE = 1024

P_QKV, P_Z, P_GQ, P_GK, P_GV, P_GG, P_SM, P_W = 0, 1536, 2048, 2560, 3072, 3584, 4096, 4224
SM_A, SM_B, SM_R = 0, 4, 8

ADAM_LR, ADAM_B1, ADAM_B2, ADAM_EPS, ADAM_WD, ADAM_STEP = 0.001, 0.9, 0.999, 1e-08, 0.01, 10

VMEM_LIMIT_V7X = 56 * 1024 * 1024


def _params(sem=None):
    return pltpu.CompilerParams(dimension_semantics=sem, vmem_limit_bytes=VMEM_LIMIT_V7X)


def _dg(a, b, dims, prec=None):
    return lax.dot_general(a, b, (dims, ((), ())), precision=prec, preferred_element_type=F32)


def _dot(a, b, prec=None):
    return _dg(a, b, ((1,), (0,)), prec)


def _dot_nt(a, b, prec=None):
    return _dg(a, b, ((1,), (1,)), prec)


def _dot_tn(a, b, prec=None):
    return _dg(a, b, ((0,), (0,)), prec)


def _iota(shape, dim):
    return lax.broadcasted_iota(jnp.int32, shape, dim)


def _sigmoid(x):
    return jax.nn.sigmoid(x)


def _silu(x):
    return x * _sigmoid(x)


def _softplus(x):
    return jnp.maximum(x, 0.0) + jnp.log(1.0 + jnp.exp(-jnp.abs(x)))


def _ln_stats(x):
    mu = jnp.mean(x, axis=-1, keepdims=True)
    xc = x - mu
    rstd = lax.rsqrt(jnp.mean(xc * xc, axis=-1, keepdims=True) + EPS)
    return xc * rstd, rstd


def _ln_bwd(dxhat, xhat, rstd):
    return rstd * (dxhat - jnp.mean(dxhat, axis=-1, keepdims=True)
                   - xhat * jnp.mean(dxhat * xhat, axis=-1, keepdims=True))


NN, NT, TN = ((1,), (0,)), ((1,), (1,)), ((0,), (0,))


def _split2(a):
    hi = a.astype(jnp.bfloat16)
    return hi, (a - hi.astype(F32)).astype(jnp.bfloat16)


def _d3(a, b, dims):
    ah, al = _split2(a)
    bh, bl = _split2(b)
    return _dg(ah, bh, dims) + (_dg(ah, bl, dims) + _dg(al, bh, dims))


@jax.custom_vjp
def _dot3(a, b):
    return _d3(a, b, NN)


_dot3.defvjp(lambda a, b: (_d3(a, b, NN), (a, b)),
             lambda res, g: (_d3(g, res[1], NT), _d3(res[0], g, TN)))


def _split3(b):
    b1 = b.astype(jnp.bfloat16)
    r1 = b - b1.astype(F32)
    b2 = r1.astype(jnp.bfloat16)
    return b1, b2, (r1 - b2.astype(F32)).astype(jnp.bfloat16)


def _sum3(fn, b):
    b1, b2, b3 = _split3(b)
    return fn(b1) + (fn(b2) + fn(b3))


@jax.custom_vjp
def _mask_dot(e, b):
    return _sum3(lambda t: _dg(e, t, NN), b)


_mask_dot.defvjp(lambda e, b: (_mask_dot(e, b), e),
                 lambda e, g: (jnp.zeros_like(e), _sum3(lambda t: _dg(e, t, TN), g)))


@jax.custom_vjp
def _mask_dot_nt(e, b):
    return _sum3(lambda t: _dg(e, t, NT), b)


_mask_dot_nt.defvjp(lambda e, b: (_mask_dot_nt(e, b), e),
                    lambda e, g: (jnp.zeros_like(e), _sum3(lambda t: _dg(t, e, TN), g)))


def _tri_inv_impl(ms):
    n = ms[0].shape[0]
    r, c = _iota((n, n), 0), _iota((n, n), 1)
    eye = (r == c).astype(F32)
    diag = (r >> 3) == (c >> 3)
    ds = [jnp.where(diag, m, 0.0) for m in ms]
    d2s = [_d3(d, d, NN) for d in ds]
    d4s = [_d3(d2, d2, NN) for d2 in d2s]
    invs = [_d3(eye - d, eye + d2, NN) for d, d2 in zip(ds, d2s)]
    invs = [_d3(inv, eye + d4, NN) for inv, d4 in zip(invs, d4s)]
    shift = 3
    while (1 << shift) < n:
        rb, cb = r >> shift, c >> shift
        sel = ((rb & 1) == 1) & (cb == rb - 1)
        tmp = [_d3(inv, jnp.where(sel, m, 0.0), NN) for inv, m in zip(invs, ms)]
        invs = [inv - _d3(t, inv, NN) for t, inv in zip(tmp, invs)]
        shift += 1
    return invs


@jax.custom_vjp
def _tri_inv(ms):
    return _tri_inv_impl(ms)


def _tri_inv_fwd(ms):
    invs = _tri_inv_impl(ms)
    return invs, invs


def _tri_inv_bwd(invs, das):
    tmp = [_d3(a, da, TN) for a, da in zip(invs, das)]
    return ([-_d3(t, a, NT) for t, a in zip(tmp, invs)],)


_tri_inv.defvjp(_tri_inv_fwd, _tri_inv_bwd)


@jax.custom_vjp
def _tri_inv_known(ms, invs):
    return invs


_tri_inv_known.defvjp(lambda ms, invs: (invs, invs),
                      lambda invs, das: (_tri_inv_bwd(invs, das)[0], [jnp.zeros_like(a) for a in invs]))


def _dn_chunk(s_list, q, k, v, gates, inv_known=None, with_inv=False):
    nb = len(q)
    c = q[0].shape[0]
    r64, c64 = _iota((c, c), 0), _iota((c, c), 1)
    causal = r64 >= c64
    strict = r64 > c64
    tri = causal.astype(jnp.bfloat16)
    eye = (_iota((HD, HD), 0) == _iota((HD, HD), 1)).astype(jnp.bfloat16)
    lane = _iota(gates[0].shape, 1)
    lane1 = _iota((1, HD), 1)
    g_all = [_mask_dot(tri, g) for g in gates]
    g_all_t = [_mask_dot_nt(eye, g) for g in g_all]
    row = _iota(g_all_t[0].shape, 0)
    last = [jnp.sum(g, axis=0, keepdims=True) for g in gates]
    prob = [(b, h) for b in range(nb) for h in range(HEADS)]
    sl = [slice(h * HD, (h + 1) * HD) for h in range(HEADS)]
    qh = [q[b][:, sl[h]] for b, h in prob]
    kh = [k[b][:, sl[h]] for b, h in prob]
    vh = [v[b][:, sl[h]] for b, h in prob]
    s = [s_list[b][h] for b, h in prob]
    beta = [jnp.sum(jnp.where(lane == SM_B + h, gates[b], 0.0), axis=-1, keepdims=True) for b, h in prob]
    g_c = [jnp.sum(jnp.where(lane == SM_A + h, g_all[b], 0.0), axis=-1, keepdims=True) for b, h in prob]
    g_r = [jnp.sum(jnp.where(row == SM_A + h, g_all_t[b], 0.0), axis=0, keepdims=True) for b, h in prob]
    g_last = [jnp.sum(jnp.where(lane1 == SM_A + h, last[b], 0.0), axis=-1, keepdims=True) for b, h in prob]
    decay = [jnp.where(causal, jnp.exp(jnp.where(causal, gc - gr, 0.0)), 0.0) for gc, gr in zip(g_c, g_r)]
    kb = [k_ * b_ for k_, b_ in zip(kh, beta)]
    m_low = [jnp.where(strict, _dot_nt(kb_, k_) * d_, 0.0) for kb_, k_, d_ in zip(kb, kh, decay)]
    attn = [_dot_nt(q_, k_) * d_ for q_, k_, d_ in zip(qh, kh, decay)]
    a_inv = _tri_inv(m_low) if inv_known is None else _tri_inv_known(m_low, inv_known)
    eg = [jnp.exp(gc) for gc in g_c]
    uw = [_dot3(a_, jnp.concatenate([v_ * b_, kb_ * e_], axis=1))
          for a_, v_, b_, kb_, e_ in zip(a_inv, vh, beta, kb, eg)]
    v_new = [uw_[:, :HD] - _dot(uw_[:, HD:], s_) for uw_, s_ in zip(uw, s)]
    qs = [_dot(q_ * e_, s_) for q_, e_, s_ in zip(qh, eg, s)]
    o = [qs_ + _dot(a_, vn_) for qs_, a_, vn_ in zip(qs, attn, v_new)]
    k_dec = [k_ * jnp.exp(gl - gc) for k_, gl, gc in zip(kh, g_last, g_c)]
    s_new = [s_ * jnp.exp(gl) + _dot_tn(kd_, vn_) for s_, gl, kd_, vn_ in zip(s, g_last, k_dec, v_new)]
    outs = [jnp.concatenate(o[b * HEADS:(b + 1) * HEADS], axis=-1) for b in range(nb)]
    states = [s_new[b * HEADS:(b + 1) * HEADS] for b in range(nb)]
    return (outs, states, a_inv) if with_inv else (outs, states)


def _gla_chunk(st_list, q, k, v, small, w2, bg):
    nb = len(q)
    c = q[0].shape[0]
    causal = _iota((c, c), 0) >= _iota((c, c), 1)
    tri = causal.astype(jnp.bfloat16)
    la_all = [-_softplus(-(_dot(sm, w2) + bg)) * (1.0 / GLA_TAU) for sm in small]
    b_all = [_mask_dot(tri, la) for la in la_all]
    prob = [(b, h) for b in range(nb) for h in range(HEADS)]
    sl = [slice(h * HD, (h + 1) * HD) for h in range(HEADS)]
    kh = [k[b][:, sl[h]] for b, h in prob]
    vh = [v[b][:, sl[h]] for b, h in prob]
    st = [st_list[b][h] for b, h in prob]
    bc = [b_all[b][:, sl[h]] for b, h in prob]
    b_last = [jnp.sum(la_all[b][:, sl[h]], axis=0, keepdims=True) for b, h in prob]
    q_dec = [q[b][:, sl[h]] * (GLA_KEY ** -0.5) * jnp.exp(bc_) for (b, h), bc_ in zip(prob, bc)]
    attn = [jnp.where(causal, _dot_nt(qd, k_ * jnp.exp(-bc_)), 0.0) for qd, k_, bc_ in zip(q_dec, kh, bc)]
    inter = [_dot_nt(qd, st_) for qd, st_ in zip(q_dec, st)]
    o = [i_ + _dot(a_, v_) for i_, a_, v_ in zip(inter, attn, vh)]
    k_dec = [k_ * jnp.exp(bl - bc_) for k_, bl, bc_ in zip(kh, b_last, bc)]
    s_new = [st_ * jnp.exp(bl) + _dot_tn(v_, kd) for st_, bl, v_, kd in zip(st, b_last, vh, k_dec)]
    outs = [jnp.concatenate(o[b * HEADS:(b + 1) * HEADS], axis=-1) for b in range(nb)]
    return outs, [s_new[b * HEADS:(b + 1) * HEADS] for b in range(nb)]


def _dn_qkv(y):
    act = _silu(y)
    parts = []
    for i in range(2 * HEADS):
        xh = act[:, i * HD:(i + 1) * HD]
        xh = xh * lax.rsqrt(jnp.sum(xh * xh, axis=-1, keepdims=True) + EPS)
        parts.append(xh * (HD ** -0.5) if i < HEADS else xh)
    qk = jnp.concatenate(parts, axis=-1)
    return qk[:, :HEADS * HD], qk[:, HEADS * HD:], act[:, 2 * HEADS * HD:]


def _dn_gates(small, alog_row, dt_row):
    lane = _iota(small.shape, 1)
    log_a = -jnp.exp(alog_row) * _softplus(small + dt_row)
    return jnp.where(lane < SM_B, log_a, jnp.where(lane < SM_R, _sigmoid(small), 0.0))


def _gate_norm(o, z, grow):
    parts = []
    for h in range(HEADS):
        oh = o[:, h * HD:(h + 1) * HD]
        parts.append(oh * lax.rsqrt(jnp.mean(oh * oh, axis=-1, keepdims=True) + EPS))
    return jnp.concatenate(parts, axis=-1) * grow * _silu(z)


def _conv_rows(xrows, w_ref, k_taps):
    n = xrows.shape[0]
    acc = xrows * w_ref[k_taps - 1:k_taps, :]
    for s in range(1, k_taps):
        acc = acc + pltpu.roll(xrows, s, 0) * w_ref[k_taps - 1 - s:k_taps - s, :]
    return acc


def _shift_up(x, s):
    return x if s == 0 else pltpu.roll(x, x.shape[0] - s, 0)


def _div_tile(n, cap, mult=8):
    best = None
    for t in range(mult, min(n, cap) + 1, mult):
        if n % t == 0:
            best = t
    return best if best is not None else n


def _halo_prev(tt):
    return lambda b, t: (b, jnp.maximum(t * (tt // HALO) - 1, 0))


def _halo_next(tt, t_total):
    return lambda b, t: (b, jnp.minimum((t + 1) * (tt // HALO), t_total // HALO - 1))


def _mm(a, b, mode, out_dtype, name, tm=512, tn=512, tk=None):
    if mode == "nn":
        (m, k), n = a.shape, b.shape[1]
    elif mode == "nt":
        (m, k), n = a.shape, b.shape[0]
    else:
        (k, m), n = a.shape, b.shape[1]
    tm, tn = min(tm, m), min(tn, n)
    tk = k if tk is None else min(tk, k)
    assert m % tm == 0 and n % tn == 0 and k % tk == 0, (name, a.shape, b.shape, tm, tn, tk)
    nk = k // tk
    if mode == "tn":
        a_spec = pl.BlockSpec((tk, tm), lambda i, j, kk: (kk, i))
    else:
        a_spec = pl.BlockSpec((tm, tk), lambda i, j, kk: (i, kk))
    if mode == "nt":
        b_spec = pl.BlockSpec((tn, tk), lambda i, j, kk: (j, kk))
    else:
        b_spec = pl.BlockSpec((tk, tn), lambda i, j, kk: (kk, j))
    dims = {"nn": ((1,), (0,)), "nt": ((1,), (1,)), "tn": ((0,), (0,))}[mode]

    def body(a_ref, b_ref, o_ref, *acc):
        p = _dg(a_ref[...], b_ref[...], dims)
        if nk == 1:
            o_ref[...] = p.astype(out_dtype)
        else:
            kk = pl.program_id(2)

            @pl.when(kk == 0)
            def _():
                acc[0][...] = p

            @pl.when(kk > 0)
            def _():
                acc[0][...] += p

            @pl.when(kk == nk - 1)
            def _():
                o_ref[...] = acc[0][...].astype(out_dtype)

    return pl.pallas_call(
        body, name=name, grid=(m // tm, n // tn, nk),
        in_specs=[a_spec, b_spec],
        out_specs=pl.BlockSpec((tm, tn), lambda i, j, kk: (i, j)),
        out_shape=jax.ShapeDtypeStruct((m, n), out_dtype),
        scratch_shapes=[pltpu.VMEM((tm, tn), F32)] if nk > 1 else [],
        compiler_params=_params(("parallel", "parallel", "arbitrary")),
    )(a, b)


def _ada_fwd(c_all, w_ada, b_cols):
    def body(c_ref, w_ref, b_ref, o_ref):
        cond = _silu(c_ref[...]).astype(MXU_DT)
        o_ref[...] = _dot(cond, w_ref[...].astype(MXU_DT)) + b_ref[...]

    return pl.pallas_call(body, name="ada_fwd", out_shape=jax.ShapeDtypeStruct((c_all.shape[0], w_ada.shape[1]), F32),
                          compiler_params=_params())(c_all, w_ada, b_cols)


def _ada_bwd(c_all, dmod_all, dmod_cols):
    def body(c_ref, da_ref, dc_ref, gw_ref, gb_ref):
        cond = _silu(c_ref[...]).astype(MXU_DT)
        gw_ref[...] = _dot_tn(cond, dc_ref[...].astype(MXU_DT))
        gb_ref[...] = jnp.sum(da_ref[...], axis=0, keepdims=True)

    return pl.pallas_call(
        body, name="ada_bwd",
        out_shape=(jax.ShapeDtypeStruct((c_all.shape[1], dmod_cols.shape[1]), F32),
                   jax.ShapeDtypeStruct((1, dmod_all.shape[1]), F32)),
        compiler_params=_params())(c_all, dmod_all, dmod_cols)


def _tok_spec(tt, width=D):
    return pl.BlockSpec((1, tt, width), lambda b, t: (b, t, 0))


def _vec_spec(width=D):
    return pl.BlockSpec((1, width), lambda b, t: (0, 0))


def _bvec_spec(width=D):
    return pl.BlockSpec((1, 1, width), lambda b, t: (b, 0, 0))


def _ln0_mod(x, g0, b0, sc, sh):
    bsz, t_total, _ = x.shape
    tt = _div_tile(t_total, ROW_TILE)

    def body(x_ref, g_ref, b_ref, sc_ref, sh_ref, h_ref):
        xh, _ = _ln_stats(x_ref[0])
        x0 = xh * g_ref[...] + b_ref[...]
        h_ref[0] = (x0 * (1.0 + sc_ref[0]) + sh_ref[0]).astype(MXU_DT)

    return pl.pallas_call(
        body, name="ln0_mod", grid=(bsz, t_total // tt),
        in_specs=[_tok_spec(tt), _vec_spec(), _vec_spec(), _bvec_spec(), _bvec_spec()],
        out_specs=_tok_spec(tt), out_shape=jax.ShapeDtypeStruct(x.shape, MXU_DT),
        compiler_params=_params(("parallel", "parallel")))(x, g0, b0, sc, sh)


def _res_ln_mod(x, y, gt, g0, b0, g1, b1, sc, sh):
    bsz, t_total, _ = x.shape
    tt = _div_tile(t_total, ROW_TILE)

    def body(x_ref, y_ref, gt_ref, g0_ref, b0_ref, g1_ref, b1_ref, sc_ref, sh_ref, r_ref, h_ref):
        xh, _ = _ln_stats(x_ref[0])
        r = ALPHA * (xh * g0_ref[...] + b0_ref[...]) + (1.0 + gt_ref[0]) * y_ref[0].astype(F32)
        r_ref[0] = r
        rh, _ = _ln_stats(r)
        x1 = rh * g1_ref[...] + b1_ref[...]
        h_ref[0] = (x1 * (1.0 + sc_ref[0]) + sh_ref[0]).astype(MXU_DT)

    return pl.pallas_call(
        body, name="res_ln_mod", grid=(bsz, t_total // tt),
        in_specs=[_tok_spec(tt), _tok_spec(tt), _bvec_spec(), _vec_spec(), _vec_spec(), _vec_spec(), _vec_spec(),
                  _bvec_spec(), _bvec_spec()],
        out_specs=(_tok_spec(tt), _tok_spec(tt)),
        out_shape=(jax.ShapeDtypeStruct(x.shape, F32), jax.ShapeDtypeStruct(x.shape, MXU_DT)),
        compiler_params=_params(("parallel", "parallel")))(x, y, gt, g0, b0, g1, b1, sc, sh)


def _final_fwd_bwd(r1, y2, gt, g1, b1, g2, b2, target):
    bsz, t_total, _ = r1.shape
    tt = _div_tile(t_total, ROW_TILE)

    def body(r1_ref, y2_ref, gt_ref, g1_ref, b1_ref, g2_ref, b2_ref, tg_ref,
             loss_ref, dr2_ref, dy2_ref, dgt_ref, dg2_ref, db2_ref):
        b, t = pl.program_id(0), pl.program_id(1)

        @pl.when((b == 0) & (t == 0))
        def _():
            loss_ref[...] = jnp.zeros_like(loss_ref)
            dg2_ref[...] = jnp.zeros_like(dg2_ref)
            db2_ref[...] = jnp.zeros_like(db2_ref)

        @pl.when(t == 0)
        def _():
            dgt_ref[...] = jnp.zeros_like(dgt_ref)

        rh1, _ = _ln_stats(r1_ref[0])
        x1 = rh1 * g1_ref[...] + b1_ref[...]
        y2 = y2_ref[0].astype(F32)
        gate = 1.0 + gt_ref[0]
        xh2, rstd2 = _ln_stats(ALPHA * x1 + gate * y2)
        err = xh2 * g2_ref[...] + b2_ref[...] - tg_ref[0]
        loss_ref[...] += jnp.sum(err * err, axis=0, keepdims=True)
        dx2 = err * (1.0 / D)
        dg2_ref[...] += jnp.sum(dx2 * xh2, axis=0, keepdims=True)
        db2_ref[...] += jnp.sum(dx2, axis=0, keepdims=True)
        dr2 = _ln_bwd(dx2 * g2_ref[...], xh2, rstd2)
        dr2_ref[0] = dr2
        dy2_ref[0] = (gate * dr2).astype(MXU_DT)
        dgt_ref[0] += jnp.sum(dr2 * y2, axis=0, keepdims=True)

    vec_out = jax.ShapeDtypeStruct((1, D), F32)
    return pl.pallas_call(
        body, name="final_fwd_bwd", grid=(bsz, t_total // tt),
        in_specs=[_tok_spec(tt), _tok_spec(tt), _bvec_spec(), _vec_spec(), _vec_spec(), _vec_spec(), _vec_spec(),
                  _tok_spec(tt)],
        out_specs=(_vec_spec(), _tok_spec(tt), _tok_spec(tt), _bvec_spec(), _vec_spec(), _vec_spec()),
        out_shape=(vec_out, jax.ShapeDtypeStruct(r1.shape, F32), jax.ShapeDtypeStruct(r1.shape, MXU_DT),
                   jax.ShapeDtypeStruct((bsz, 1, D), F32), vec_out, vec_out),
        compiler_params=_params(("arbitrary", "arbitrary")))(r1, y2, gt, g1, b1, g2, b2, target)


def _ln_bwd_call(name, d_res, d_h, src, g, b, sc, y=None, gt=None):
    bsz, t_total, _ = src.shape
    tt = _div_tile(t_total, ROW_TILE)
    has_y = y is not None

    def body(*refs):
        if has_y:
            (dres_ref, dh_ref, src_ref, g_ref, b_ref, sc_ref, y_ref, gt_ref,
             dsrc_ref, dsc_ref, dsh_ref, dg_ref, db_ref, dy_ref, dgt_ref) = refs
        else:
            (dres_ref, dh_ref, src_ref, g_ref, b_ref, sc_ref,
             dsrc_ref, dsc_ref, dsh_ref, dg_ref, db_ref) = refs
        bi, t = pl.program_id(0), pl.program_id(1)

        @pl.when((bi == 0) & (t == 0))
        def _():
            dg_ref[...] = jnp.zeros_like(dg_ref)
            db_ref[...] = jnp.zeros_like(db_ref)

        @pl.when(t == 0)
        def _():
            dsc_ref[...] = jnp.zeros_like(dsc_ref)
            dsh_ref[...] = jnp.zeros_like(dsh_ref)
            if has_y:
                dgt_ref[...] = jnp.zeros_like(dgt_ref)

        xh, rstd = _ln_stats(src_ref[0])
        xv = xh * g_ref[...] + b_ref[...]
        dh = dh_ref[0].astype(F32)
        dx = ALPHA * dres_ref[0] + dh * (1.0 + sc_ref[0])
        dsc_ref[0] += jnp.sum(dh * xv, axis=0, keepdims=True)
        dsh_ref[0] += jnp.sum(dh, axis=0, keepdims=True)
        dg_ref[...] += jnp.sum(dx * xh, axis=0, keepdims=True)
        db_ref[...] += jnp.sum(dx, axis=0, keepdims=True)
        dsrc = _ln_bwd(dx * g_ref[...], xh, rstd)
        dsrc_ref[0] = dsrc
        if has_y:
            dy_ref[0] = ((1.0 + gt_ref[0]) * dsrc).astype(MXU_DT)
            dgt_ref[0] += jnp.sum(dsrc * y_ref[0].astype(F32), axis=0, keepdims=True)

    vec_out = jax.ShapeDtypeStruct((1, D), F32)
    bvec_out = jax.ShapeDtypeStruct((bsz, 1, D), F32)
    in_specs = [_tok_spec(tt), _tok_spec(tt), _tok_spec(tt), _vec_spec(), _vec_spec(), _bvec_spec()]
    out_specs = [_tok_spec(tt), _bvec_spec(), _bvec_spec(), _vec_spec(), _vec_spec()]
    out_shape = [jax.ShapeDtypeStruct(src.shape, F32), bvec_out, bvec_out, vec_out, vec_out]
    args = [d_res, d_h, src, g, b, sc]
    if has_y:
        in_specs += [_tok_spec(tt), _bvec_spec()]
        out_specs += [_tok_spec(tt), _bvec_spec()]
        out_shape += [jax.ShapeDtypeStruct(src.shape, MXU_DT), bvec_out]
        args += [y, gt]
    return pl.pallas_call(body, name=name, grid=(bsz, t_total // tt), in_specs=in_specs, out_specs=tuple(out_specs),
                          out_shape=tuple(out_shape), compiler_params=_params(("arbitrary", "arbitrary")))(*args)


FFN_TC = 256
FFN_NJ = D_FF // FFN_TC
FFN_PW = 2 * FFN_TC


def _ffn_pair(a, axis):
    shp = list(a.shape)
    a4 = a.reshape(shp[:axis] + [2, FFN_NJ, FFN_TC] + shp[axis + 1:])
    return jnp.swapaxes(a4, axis, axis + 1).reshape(shp)


def _ffn_unpair(a, axis):
    shp = list(a.shape)
    a4 = a.reshape(shp[:axis] + [FFN_NJ, 2, FFN_TC] + shp[axis + 1:])
    return jnp.swapaxes(a4, axis, axis + 1).reshape(shp)


def _ffn_up_act(h, w_up, cw, cb):
    bsz, t_total, _ = h.shape
    tt = _div_tile(t_total, FFN_ROW_TILE)
    def body(h_ref, wu_ref, w_ref, b_ref, up_ref, o_ref, carry_ref):
        up_t = _dot_nt(h_ref[0], wu_ref[...])
        up_ref[0] = up_t
        prev = jnp.where(pl.program_id(2) == 0, 0.0, carry_ref[...])
        rows = jnp.concatenate([prev, up_t], axis=0)
        u = _conv_rows(rows, w_ref, FFN_CONV_K)[HALO:] + b_ref[...]
        o_ref[0] = (_silu(u[:, :FFN_TC]) * u[:, FFN_TC:]).astype(MXU_DT)
        carry_ref[...] = up_t[tt - HALO:, :]

    return pl.pallas_call(
        body, name="ffn_up_act", grid=(bsz, FFN_NJ, t_total // tt),
        in_specs=[pl.BlockSpec((1, tt, D), lambda b, j, t: (b, t, 0)),
                  pl.BlockSpec((FFN_PW, D), lambda b, j, t: (j, 0)),
                  pl.BlockSpec((FFN_CONV_K, FFN_PW), lambda b, j, t: (0, j)),
                  pl.BlockSpec((1, FFN_PW), lambda b, j, t: (0, j))],
        out_specs=(pl.BlockSpec((1, tt, FFN_PW), lambda b, j, t: (b, t, j)),
                   pl.BlockSpec((1, tt, FFN_TC), lambda b, j, t: (b, t, j))),
        out_shape=(jax.ShapeDtypeStruct((bsz, t_total, 2 * D_FF), F32),
                   jax.ShapeDtypeStruct((bsz, t_total, D_FF), MXU_DT)),
        scratch_shapes=[pltpu.VMEM((HALO, FFN_PW), F32)],
        compiler_params=_params(("parallel", "parallel", "arbitrary")))(h, w_up, cw, cb)


HALO16 = 16


def _ffn_act_bwd(up, dy2, w_down, cw, cb):
    bsz, t_total, width = up.shape
    tt = _div_tile(t_total, FFN_ROW_TILE)
    nt = t_total // tt
    hp, hn = _halo_prev(tt), _halo_next(tt, t_total)

    def body(x_ref, xp_ref, xn_ref, dy_ref, dyn_ref, wd_ref, w_ref, b_ref, dup_ref, dw_ref, db_ref):
        b, t = pl.program_id(1), pl.program_id(2)

        @pl.when((b == 0) & (t == 0))
        def _():
            dw_ref[...] = jnp.zeros_like(dw_ref)
            db_ref[...] = jnp.zeros_like(db_ref)

        prev = jnp.where(t == 0, 0.0, xp_ref[0])
        rows = jnp.concatenate([prev, x_ref[0], xn_ref[0]], axis=0)
        u = _conv_rows(rows, w_ref, FFN_CONV_K)[HALO:] + b_ref[...]
        g_pre, v_pre = u[:, :FFN_TC], u[:, FFN_TC:]
        valid = (_iota((tt + HALO, 1), 0) < tt) | (t < nt - 1)
        da = jnp.concatenate([_dot_nt(dy_ref[0], wd_ref[...]), _dot_nt(dyn_ref[0], wd_ref[...])[:HALO]], axis=0)
        da_ext = jnp.where(valid, da, 0.0)
        sg = _sigmoid(g_pre)
        gs = g_pre * sg
        du = jnp.concatenate([da_ext * v_pre * (sg + gs * (1.0 - sg)), da_ext * gs], axis=1)
        dup = du * w_ref[FFN_CONV_K - 1:FFN_CONV_K, :]
        for s in range(1, FFN_CONV_K):
            dup = dup + _shift_up(du, s) * w_ref[FFN_CONV_K - 1 - s:FFN_CONV_K - s, :]
        dup_ref[0] = dup[:tt].astype(MXU_DT)
        du_t = du[:tt]
        db_ref[...] += jnp.sum(du_t, axis=0, keepdims=True)
        for k in range(FFN_CONV_K):
            s = FFN_CONV_K - 1 - k
            xs = (rows if s == 0 else pltpu.roll(rows, s, 0))[HALO:HALO + tt]
            dw_ref[k:k + 1, :] += jnp.sum(du_t * xs, axis=0, keepdims=True)

    def halo(h, w):
        return pl.BlockSpec((1, HALO, w), lambda j, b, t: (*h(b, t), j))

    wspec = lambda rows_: pl.BlockSpec((rows_, FFN_PW), lambda j, b, t: (0, j))
    tile = pl.BlockSpec((1, tt, FFN_PW), lambda j, b, t: (b, t, j))
    dy_next = lambda j, b, t: (b, jnp.minimum((t + 1) * (tt // HALO16), t_total // HALO16 - 1), 0)
    return pl.pallas_call(
        body, name="ffn_act_bwd", grid=(FFN_NJ, bsz, nt),
        in_specs=[tile, halo(hp, FFN_PW), halo(hn, FFN_PW),
                  pl.BlockSpec((1, tt, D), lambda j, b, t: (b, t, 0)), pl.BlockSpec((1, HALO16, D), dy_next),
                  pl.BlockSpec((FFN_TC, D), lambda j, b, t: (j, 0)), wspec(FFN_CONV_K), wspec(1)],
        out_specs=(tile, wspec(FFN_CONV_K), wspec(1)),
        out_shape=(jax.ShapeDtypeStruct(up.shape, MXU_DT), jax.ShapeDtypeStruct((FFN_CONV_K, width), F32),
                   jax.ShapeDtypeStruct((1, width), F32)),
        compiler_params=_params(("arbitrary", "arbitrary", "arbitrary")))(up, up, up, dy2, dy2, w_down, cw, cb)


QKV_W = 3 * HEADS * HD
SM_BLK = P_SM // 128


def _dn_pre_fwd(proj, conv_w, alog_row, dt_row):
    bsz, t_total, _ = proj.shape
    tt = _div_tile(t_total, ROW_TILE)
    hp = _halo_prev(tt)

    def body(x_ref, xp_ref, sm_ref, w_ref, al_ref, dt_ref, q_ref, k_ref, v_ref, g_ref):
        prev = jnp.where(pl.program_id(1) == 0, 0.0, xp_ref[0])
        y = _conv_rows(jnp.concatenate([prev, x_ref[0]], axis=0), w_ref, DN_CONV_K)[HALO:]
        q_ref[0], k_ref[0], v_ref[0] = _dn_qkv(y)
        g_ref[0] = _dn_gates(sm_ref[0], al_ref[...], dt_ref[...])

    out512 = jax.ShapeDtypeStruct((bsz, t_total, HEADS * HD), F32)
    return pl.pallas_call(
        body, name="dn_pre_fwd", grid=(bsz, t_total // tt),
        in_specs=[pl.BlockSpec((1, tt, QKV_W), lambda b, t: (b, t, 0)),
                  pl.BlockSpec((1, HALO, QKV_W), lambda b, t: (*hp(b, t), 0)),
                  pl.BlockSpec((1, tt, 128), lambda b, t: (b, t, SM_BLK)),
                  pl.BlockSpec((DN_CONV_K, QKV_W), lambda b, t: (0, 0)), _vec_spec(128), _vec_spec(128)],
        out_specs=(_tok_spec(tt, 512), _tok_spec(tt, 512), _tok_spec(tt, 512), _tok_spec(tt, 128)),
        out_shape=(out512, out512, out512, jax.ShapeDtypeStruct((bsz, t_total, 128), F32)),
        compiler_params=_params(("parallel", "parallel")))(proj, proj, proj, conv_w, alog_row, dt_row)


def _dn_pre_bwd(proj, dq, dk, dv, dgates, conv_w, alog_row, dt_row):
    bsz, t_total, _ = proj.shape
    tt = _div_tile(t_total, 256)
    nt = t_total // tt
    hp, hn = _halo_prev(tt), _halo_next(tt, t_total)

    def body(x_ref, xp_ref, xn_ref, sm_ref, dq_ref, dqn_ref, dk_ref, dkn_ref, dv_ref, dvn_ref, dg_ref,
             w_ref, al_ref, dt_ref, dx_ref, dsm_ref, dw_ref, dal_ref, ddt_ref):
        b, t = pl.program_id(0), pl.program_id(1)

        @pl.when((b == 0) & (t == 0))
        def _():
            dw_ref[...] = jnp.zeros_like(dw_ref)
            dal_ref[...] = jnp.zeros_like(dal_ref)
            ddt_ref[...] = jnp.zeros_like(ddt_ref)

        prev = jnp.where(t == 0, 0.0, xp_ref[0])
        rows = jnp.concatenate([prev, x_ref[0], xn_ref[0]], axis=0)
        y = _conv_rows(rows, w_ref, DN_CONV_K)[HALO:]
        valid = (_iota((tt + HALO, 1), 0) < tt) | (t < nt - 1)

        def ext(tile_ref, next_ref):
            return jnp.where(valid, jnp.concatenate([tile_ref[0], next_ref[0]], axis=0), 0.0)

        _, vjp_qkv = jax.vjp(_dn_qkv, y)
        (dy,) = vjp_qkv((ext(dq_ref, dqn_ref), ext(dk_ref, dkn_ref), ext(dv_ref, dvn_ref)))
        dy = jnp.where(valid, dy, 0.0)
        dx = dy * w_ref[DN_CONV_K - 1:DN_CONV_K, :]
        for s in range(1, DN_CONV_K):
            dx = dx + _shift_up(dy, s) * w_ref[DN_CONV_K - 1 - s:DN_CONV_K - s, :]
        dx_ref[0] = dx[:tt].astype(MXU_DT)
        dy_t = dy[:tt]
        for k in range(DN_CONV_K):
            s = DN_CONV_K - 1 - k
            xs = (rows if s == 0 else pltpu.roll(rows, s, 0))[HALO:HALO + tt]
            dw_ref[k:k + 1, :] += jnp.sum(dy_t * xs, axis=0, keepdims=True)
        _, vjp_g = jax.vjp(_dn_gates, sm_ref[0], al_ref[...], dt_ref[...])
        dsm, dal, ddt = vjp_g(dg_ref[0])
        dsm_ref[0] = dsm
        dal_ref[...] += dal
        ddt_ref[...] += ddt

    def tile(width, blk=0):
        return pl.BlockSpec((1, tt, width), lambda b, t: (b, t, blk))

    def halo(h, width):
        return pl.BlockSpec((1, HALO, width), lambda b, t: (*h(b, t), 0))

    return pl.pallas_call(
        body, name="dn_pre_bwd", grid=(bsz, nt),
        in_specs=[tile(QKV_W), halo(hp, QKV_W), halo(hn, QKV_W), tile(128, SM_BLK),
                  tile(512), halo(hn, 512), tile(512), halo(hn, 512), tile(512), halo(hn, 512), tile(128),
                  pl.BlockSpec((DN_CONV_K, QKV_W), lambda b, t: (0, 0)), _vec_spec(128), _vec_spec(128)],
        out_specs=(tile(QKV_W), tile(128), pl.BlockSpec((DN_CONV_K, QKV_W), lambda b, t: (0, 0)),
                   _vec_spec(128), _vec_spec(128)),
        out_shape=(jax.ShapeDtypeStruct((bsz, t_total, QKV_W), MXU_DT), jax.ShapeDtypeStruct((bsz, t_total, 128), F32),
                   jax.ShapeDtypeStruct((DN_CONV_K, QKV_W), F32), jax.ShapeDtypeStruct((1, 128), F32),
                   jax.ShapeDtypeStruct((1, 128), F32)),
        compiler_params=_params(("arbitrary", "arbitrary")))(
            proj, proj, proj, proj, dq, dq, dk, dk, dv, dv, dgates, conv_w, alog_row, dt_row)


def _state_spec(bsz, idx):
    return pl.BlockSpec((bsz, 1, HEADS, HD, HD), lambda c: (0, idx(c), 0, 0, 0))


def _inv_spec(bsz, idx):
    return pl.BlockSpec((bsz, 1, HEADS, CHUNK, CHUNK), lambda c: (0, idx(c), 0, 0, 0))


def _chunk_spec(bsz, width, idx, blk=0):
    return pl.BlockSpec((bsz, CHUNK, width), lambda c: (0, idx(c), blk))


def _dn_rec_fwd(q, k, v, gates):
    bsz, t_total, _ = q.shape
    nc = t_total // CHUNK
    fwd = lambda c: c

    def body(q_ref, k_ref, v_ref, g_ref, o_ref, ss_ref, inv_ref, s_ref):
        @pl.when(pl.program_id(0) == 0)
        def _():
            s_ref[...] = jnp.zeros_like(s_ref)

        seqs = range(bsz)
        s_list = [[s_ref[b * HEADS + h] for h in range(HEADS)] for b in seqs]
        for b in seqs:
            for h in range(HEADS):
                ss_ref[b, 0, h] = s_list[b][h]
        o, new_s, invs = _dn_chunk(s_list, [q_ref[b] for b in seqs], [k_ref[b] for b in seqs],
                                   [v_ref[b] for b in seqs], [g_ref[b] for b in seqs], with_inv=True)
        for b in seqs:
            o_ref[b] = o[b]
            for h in range(HEADS):
                s_ref[b * HEADS + h] = new_s[b][h]
                inv_ref[b, 0, h] = invs[b * HEADS + h]

    return pl.pallas_call(
        body, name="dn_rec_fwd", grid=(nc,),
        in_specs=[_chunk_spec(bsz, 512, fwd)] * 3 + [_chunk_spec(bsz, 128, fwd)],
        out_specs=(_chunk_spec(bsz, 512, fwd), _state_spec(bsz, fwd), _inv_spec(bsz, fwd)),
        out_shape=(jax.ShapeDtypeStruct(q.shape, F32), jax.ShapeDtypeStruct((bsz, nc, HEADS, HD, HD), F32),
                   jax.ShapeDtypeStruct((bsz, nc, HEADS, CHUNK, CHUNK), F32)),
        scratch_shapes=[pltpu.VMEM((bsz * HEADS, HD, HD), F32)],
        compiler_params=_params(("arbitrary",)))(q, k, v, gates)


def _dn_rec_bwd(q, k, v, gates, states, invs, do):
    bsz, t_total, _ = q.shape
    nc = t_total // CHUNK
    rev = lambda c: nc - 1 - c

    def body(q_ref, k_ref, v_ref, g_ref, ss_ref, inv_ref, do_ref, dq_ref, dk_ref, dv_ref, dg_ref, ds_ref):
        @pl.when(pl.program_id(0) == 0)
        def _():
            ds_ref[...] = jnp.zeros_like(ds_ref)

        seqs = range(bsz)
        s_list = [[ss_ref[b, 0, h] for h in range(HEADS)] for b in seqs]
        known = [inv_ref[b, 0, h] for b in seqs for h in range(HEADS)]
        _, vjp = jax.vjp(functools.partial(_dn_chunk, inv_known=known),
                         s_list, [q_ref[b] for b in seqs], [k_ref[b] for b in seqs],
                         [v_ref[b] for b in seqs], [g_ref[b] for b in seqs])
        ds_in, dq, dk, dv, dg = vjp(([do_ref[b] for b in seqs],
                                     [[ds_ref[b * HEADS + h] for h in range(HEADS)] for b in seqs]))
        for b in seqs:
            dq_ref[b], dk_ref[b], dv_ref[b], dg_ref[b] = dq[b], dk[b], dv[b], dg[b]
            for h in range(HEADS):
                ds_ref[b * HEADS + h] = ds_in[b][h]

    tok = lambda width: _chunk_spec(bsz, width, rev)
    out512 = jax.ShapeDtypeStruct(q.shape, F32)
    return pl.pallas_call(
        body, name="dn_rec_bwd", grid=(nc,),
        in_specs=[tok(512), tok(512), tok(512), tok(128), _state_spec(bsz, rev), _inv_spec(bsz, rev), tok(512)],
        out_specs=(tok(512), tok(512), tok(512), tok(128)),
        out_shape=(out512, out512, out512, jax.ShapeDtypeStruct(gates.shape, F32)),
        scratch_shapes=[pltpu.VMEM((bsz * HEADS, HD, HD), F32)],
        compiler_params=_params(("arbitrary",)))(q, k, v, gates, states, invs, do)


GQ_BLK, GK_BLK, GV_BLK = P_GQ // 512, P_GK // 512, P_GV // 512


def _gla_rec_fwd(proj, w2, bg):
    bsz, t_total, _ = proj.shape
    nc = t_total // CHUNK

    fwd = lambda c: c

    def body(q_ref, k_ref, v_ref, sm_ref, w2_ref, bg_ref, o_ref, ss_ref, s_ref):
        @pl.when(pl.program_id(0) == 0)
        def _():
            s_ref[...] = jnp.zeros_like(s_ref)

        seqs = range(bsz)
        s_list = [[s_ref[b * HEADS + h] for h in range(HEADS)] for b in seqs]
        for b in seqs:
            for h in range(HEADS):
                ss_ref[b, 0, h] = s_list[b][h]
        o, new_s = _gla_chunk(s_list, [q_ref[b] for b in seqs], [k_ref[b] for b in seqs], [v_ref[b] for b in seqs],
                              [sm_ref[b] for b in seqs], w2_ref[...], bg_ref[...])
        for b in seqs:
            o_ref[b] = o[b]
            for h in range(HEADS):
                s_ref[b * HEADS + h] = new_s[b][h]

    col = lambda blk, width=512: _chunk_spec(bsz, width, fwd, blk)
    return pl.pallas_call(
        body, name="gla_rec_fwd", grid=(nc,),
        in_specs=[col(GQ_BLK), col(GK_BLK), col(GV_BLK), col(SM_BLK, 128),
                  pl.BlockSpec((128, 512), lambda c: (0, 0)), pl.BlockSpec((1, 512), lambda c: (0, 0))],
        out_specs=(col(0), _state_spec(bsz, fwd)),
        out_shape=(jax.ShapeDtypeStruct((bsz, t_total, 512), F32),
                   jax.ShapeDtypeStruct((bsz, nc, HEADS, HD, HD), F32)),
        scratch_shapes=[pltpu.VMEM((bsz * HEADS, HD, HD), F32)],
        compiler_params=_params(("arbitrary",)))(proj, proj, proj, proj, w2, bg)


def _gla_rec_bwd(proj, w2, bg, states, do, dsm_dn):
    bsz, t_total, _ = proj.shape
    nc = t_total // CHUNK
    rev = lambda c: nc - 1 - c

    def body(q_ref, k_ref, v_ref, sm_ref, w2_ref, bg_ref, ss_ref, do_ref, dsd_ref,
             dq_ref, dk_ref, dv_ref, dsm_ref, dw2_ref, dbg_ref, ds_ref):
        @pl.when(pl.program_id(0) == 0)
        def _():
            dw2_ref[...] = jnp.zeros_like(dw2_ref)
            dbg_ref[...] = jnp.zeros_like(dbg_ref)
            ds_ref[...] = jnp.zeros_like(ds_ref)

        seqs = range(bsz)
        s_list = [[ss_ref[b, 0, h] for h in range(HEADS)] for b in seqs]
        _, vjp = jax.vjp(_gla_chunk, s_list, [q_ref[b] for b in seqs], [k_ref[b] for b in seqs],
                         [v_ref[b] for b in seqs], [sm_ref[b] for b in seqs], w2_ref[...], bg_ref[...])
        ds_in, dq, dk, dv, dsm, dw2, dbg = vjp(([do_ref[b] for b in seqs],
                                                [[ds_ref[b * HEADS + h] for h in range(HEADS)] for b in seqs]))
        for b in seqs:
            dq_ref[b], dk_ref[b], dv_ref[b] = dq[b].astype(MXU_DT), dk[b].astype(MXU_DT), dv[b].astype(MXU_DT)
            dsm_ref[b] = (dsm[b] + dsd_ref[b]).astype(MXU_DT)
            for h in range(HEADS):
                ds_ref[b * HEADS + h] = ds_in[b][h]
        dw2_ref[...] += dw2
        dbg_ref[...] += dbg

    col = lambda blk, width=512: _chunk_spec(bsz, width, rev, blk)
    w2_spec = pl.BlockSpec((128, 512), lambda c: (0, 0))
    bg_spec = pl.BlockSpec((1, 512), lambda c: (0, 0))
    out512 = jax.ShapeDtypeStruct((bsz, t_total, 512), MXU_DT)
    return pl.pallas_call(
        body, name="gla_rec_bwd", grid=(nc,),
        in_specs=[col(GQ_BLK), col(GK_BLK), col(GV_BLK), col(SM_BLK, 128), w2_spec, bg_spec,
                  _state_spec(bsz, rev), col(0), col(0, 128)],
        out_specs=(col(0), col(0), col(0), col(0, 128), w2_spec, bg_spec),
        out_shape=(out512, out512, out512, jax.ShapeDtypeStruct((bsz, t_total, 128), MXU_DT),
                   jax.ShapeDtypeStruct((128, 512), F32), jax.ShapeDtypeStruct((1, 512), F32)),
        scratch_shapes=[pltpu.VMEM((bsz * HEADS, HD, HD), F32)],
        compiler_params=_params(("arbitrary",)))(proj, proj, proj, proj, w2, bg, states, do, dsm_dn)


Z_BLK, GG_BLK = P_Z // 512, P_GG // 512


def _mix_out_fwd(o_dn, o_gla, proj, grow_dn, grow_gla):
    bsz, t_total, _ = o_dn.shape
    tt = _div_tile(t_total, ROW_TILE)

    def body(od_ref, og_ref, z_ref, gg_ref, gd_ref, gl_ref, o_ref):
        o_ref[0, :, :512] = _gate_norm(od_ref[0], z_ref[0], gd_ref[...]).astype(MXU_DT)
        o_ref[0, :, 512:] = _gate_norm(og_ref[0], gg_ref[0], gl_ref[...]).astype(MXU_DT)

    def col(blk):
        return pl.BlockSpec((1, tt, 512), lambda b, t: (b, t, blk))

    return pl.pallas_call(
        body, name="mix_out_fwd", grid=(bsz, t_total // tt),
        in_specs=[col(0), col(0), col(Z_BLK), col(GG_BLK), _vec_spec(512), _vec_spec(512)],
        out_specs=_tok_spec(tt), out_shape=jax.ShapeDtypeStruct((bsz, t_total, D), MXU_DT),
        compiler_params=_params(("parallel", "parallel")))(o_dn, o_gla, proj, proj, grow_dn, grow_gla)


def _mix_out_bwd(do, o_dn, o_gla, proj, grow_dn, grow_gla):
    bsz, t_total, _ = o_dn.shape
    tt = _div_tile(t_total, ROW_TILE)

    def body(do_ref, od_ref, og_ref, z_ref, gg_ref, gd_ref, gl_ref,
             dod_ref, dog_ref, dz_ref, dgg_ref, dgd_ref, dgl_ref):
        @pl.when((pl.program_id(0) == 0) & (pl.program_id(1) == 0))
        def _():
            dgd_ref[...] = jnp.zeros_like(dgd_ref)
            dgl_ref[...] = jnp.zeros_like(dgl_ref)

        def one(o_ref, gate_ref, g_ref, ct, do_out, dgate_out, dg_out):
            _, vjp = jax.vjp(_gate_norm, o_ref[0], gate_ref[0], g_ref[...])
            d_o, d_gate, d_row = vjp(ct)
            do_out[0] = d_o
            dgate_out[0] = d_gate.astype(MXU_DT)
            acc = d_row[:, :HD]
            for h in range(1, HEADS):
                acc = acc + d_row[:, h * HD:(h + 1) * HD]
            dg_out[...] += acc

        ct = do_ref[0].astype(F32)
        one(od_ref, z_ref, gd_ref, ct[:, :512], dod_ref, dz_ref, dgd_ref)
        one(og_ref, gg_ref, gl_ref, ct[:, 512:], dog_ref, dgg_ref, dgl_ref)

    def col(blk):
        return pl.BlockSpec((1, tt, 512), lambda b, t: (b, t, blk))

    f512 = jax.ShapeDtypeStruct((bsz, t_total, 512), F32)
    b512 = jax.ShapeDtypeStruct((bsz, t_total, 512), MXU_DT)
    g128 = jax.ShapeDtypeStruct((1, HD), F32)
    return pl.pallas_call(
        body, name="mix_out_bwd", grid=(bsz, t_total // tt),
        in_specs=[_tok_spec(tt), col(0), col(0), col(Z_BLK), col(GG_BLK), _vec_spec(512), _vec_spec(512)],
        out_specs=(col(0), col(0), col(0), col(0), _vec_spec(HD), _vec_spec(HD)),
        out_shape=(f512, f512, b512, b512, g128, g128),
        compiler_params=_params(("arbitrary", "arbitrary")))(do, o_dn, o_gla, proj, proj, grow_dn, grow_gla)


def _sum_slots(x, name):
    n, rows, cols = x.shape
    tr = _div_tile(rows, max(8, (1 << 19) // cols))

    def body(x_ref, o_ref):
        acc = x_ref[0].astype(F32)
        for i in range(1, n):
            acc = acc + x_ref[i].astype(F32)
        o_ref[...] = acc

    return pl.pallas_call(
        body, name=name, grid=(rows // tr,),
        in_specs=[pl.BlockSpec((n, tr, cols), lambda i: (0, i, 0))],
        out_specs=pl.BlockSpec((tr, cols), lambda i: (i, 0)),
        out_shape=jax.ShapeDtypeStruct((rows, cols), F32), compiler_params=_params(("parallel",)))(x)


def _adamw_math(w, g, m, v):
    nm = ADAM_B1 * m + (1.0 - ADAM_B1) * g
    nv = ADAM_B2 * v + (1.0 - ADAM_B2) * (g * g)
    m_hat = nm / (1.0 - ADAM_B1 ** ADAM_STEP)
    v_hat = nv / (1.0 - ADAM_B2 ** ADAM_STEP)
    return -ADAM_LR * (m_hat / (jnp.sqrt(v_hat) + ADAM_EPS) + ADAM_WD * w), nm, nv


def _adamw(w, g, m, v, name):
    _, rows, cols = w.shape
    tr = _div_tile(rows, max(8, (1 << 18) // cols))

    def body(w_ref, g_ref, m_ref, v_ref, d_ref, nm_ref, nv_ref):
        d_ref[...], nm_ref[...], nv_ref[...] = _adamw_math(w_ref[...], g_ref[...], m_ref[...], v_ref[...])

    spec = pl.BlockSpec((1, tr, cols), lambda i: (0, i, 0))
    shp = jax.ShapeDtypeStruct(w.shape, F32)
    return pl.pallas_call(body, name=name, grid=(rows // tr,), in_specs=[spec] * 4, out_specs=(spec,) * 3,
                          out_shape=(shp,) * 3, compiler_params=_params(("parallel",)))(w, g, m, v)


def _sum_adamw(parts, w, m, v, name):
    n, rows, cols = parts.shape
    tr = _div_tile(rows, max(8, (1 << 18) // cols))

    def body(p_ref, w_ref, m_ref, v_ref, g_ref, d_ref, nm_ref, nv_ref):
        g = p_ref[0].astype(F32)
        for i in range(1, n):
            g = g + p_ref[i].astype(F32)
        g_ref[...] = g
        d_ref[0], nm_ref[0], nv_ref[0] = _adamw_math(w_ref[0], g, m_ref[0], v_ref[0])

    spec = pl.BlockSpec((1, tr, cols), lambda i: (0, i, 0))
    shp = jax.ShapeDtypeStruct(w.shape, F32)
    return pl.pallas_call(
        body, name=name, grid=(rows // tr,),
        in_specs=[pl.BlockSpec((n, tr, cols), lambda i: (0, i, 0)), spec, spec, spec],
        out_specs=(pl.BlockSpec((tr, cols), lambda i: (i, 0)), spec, spec, spec),
        out_shape=(jax.ShapeDtypeStruct((rows, cols), F32), shp, shp, shp),
        compiler_params=_params(("parallel",)))(parts, w, m, v)


def _adamw_many(ws, gs, ms, vs, name):
    n = len(ws)

    def body(*refs):
        for i in range(n):
            d, nm, nv = _adamw_math(refs[i][...], refs[n + i][...], refs[2 * n + i][...], refs[3 * n + i][...])
            refs[4 * n + i][...] = d
            refs[5 * n + i][...] = nm
            refs[6 * n + i][...] = nv

    shapes = tuple(jax.ShapeDtypeStruct(w.shape, F32) for w in ws)
    outs = pl.pallas_call(body, name=name, out_shape=shapes * 3, compiler_params=_params())(*ws, *gs, *ms, *vs)
    return outs[:n], outs[n:2 * n], outs[2 * n:]


def _position():
    return lax.axis_index("x"), lax.axis_index("y"), lax.axis_index("c")


def _slot(px, py, pc):
    return 4 * px + 2 * py + pc


def _gather_small(x, name):
    rows, cols = x.shape

    def body(x_ref, o_ref, send_sems, recv_sems):
        mx, my, mc = _position()

        def peer(k):
            return (mx ^ ((k >> 2) & 1), my ^ ((k >> 1) & 1), mc ^ (k & 1))

        o_ref[_slot(mx, my, mc)] = x_ref[...]
        sends = []
        for k in range(1, N_DEV):
            cp = pltpu.make_async_remote_copy(src_ref=x_ref, dst_ref=o_ref.at[_slot(mx, my, mc)],
                                              send_sem=send_sems.at[k - 1], recv_sem=recv_sems.at[k - 1],
                                              device_id=peer(k), device_id_type=MESH)
            cp.start()
            sends.append(cp)
        for k in range(1, N_DEV):
            pltpu.make_async_remote_copy(src_ref=x_ref, dst_ref=o_ref.at[_slot(*peer(k))],
                                         send_sem=send_sems.at[k - 1], recv_sem=recv_sems.at[k - 1],
                                         device_id=peer(k), device_id_type=MESH).wait_recv()
        for cp in sends:
            cp.wait_send()

    return pl.pallas_call(
        body, name=name, out_shape=jax.ShapeDtypeStruct((N_DEV, rows, cols), x.dtype),
        in_specs=[pl.BlockSpec(memory_space=pltpu.VMEM)], out_specs=pl.BlockSpec(memory_space=pltpu.VMEM),
        scratch_shapes=[pltpu.SemaphoreType.DMA((N_DEV - 1,)), pltpu.SemaphoreType.DMA((N_DEV - 1,))],
        compiler_params=pltpu.CompilerParams(vmem_limit_bytes=VMEM_LIMIT_V7X))(x)


def _gather_big(shards):
    n = len(shards)

    def body(*refs):
        xs, outs = refs[:n], refs[n:2 * n]
        send_sems, recv_sems, local_sems = refs[2 * n:]
        mx, my, mc = _position()
        me, sibling = (mx, my, mc), (mx, my, 1 - mc)
        chips = [(1 - mx, my), (mx, 1 - my), (1 - mx, 1 - my)]

        def copy(a, k, block, to, src=None):
            dst = outs[a].at[_slot(*block)]
            return pltpu.make_async_remote_copy(src_ref=dst if src is None else src, dst_ref=dst,
                                                send_sem=send_sems.at[7 * a + k], recv_sem=recv_sems.at[7 * a + k],
                                                device_id=to, device_id_type=MESH)

        mine = [pltpu.make_async_copy(xs[a], outs[a].at[_slot(*me)], local_sems.at[a]) for a in range(n)]
        for cp in mine:
            cp.start()
        started = []
        for a in range(n):
            started.append(copy(a, 0, me, sibling, src=xs[a]))
            started += [copy(a, 1 + j, me, (*chip, mc), src=xs[a]) for j, chip in enumerate(chips)]
        for cp in started:
            cp.start()
        for j, chip in enumerate(chips):
            for a in range(n):
                copy(a, 1 + j, (*chip, mc), me).wait_recv()
                fwd = copy(a, 4 + j, (*chip, mc), sibling)
                fwd.start()
                started.append(fwd)
        for a in range(n):
            copy(a, 0, sibling, me).wait_recv()
            for j, chip in enumerate(chips):
                copy(a, 4 + j, (*chip, 1 - mc), me).wait_recv()
        for cp in started:
            cp.wait_send()
        for cp in mine:
            cp.wait()

    any_spec = pl.BlockSpec(memory_space=pl.ANY)
    return pl.pallas_call(
        body, name="gather_weights",
        out_shape=tuple(jax.ShapeDtypeStruct((N_DEV,) + s.shape, s.dtype) for s in shards),
        in_specs=[any_spec] * n, out_specs=(any_spec,) * n,
        scratch_shapes=[pltpu.SemaphoreType.DMA((7 * n,)), pltpu.SemaphoreType.DMA((7 * n,)),
                        pltpu.SemaphoreType.DMA((n,))])(*shards)


def _peer(pos, k):
    mx, my, mc = pos
    return (mx ^ ((k >> 2) & 1), my ^ ((k >> 1) & 1), mc ^ (k & 1))


def _exchange_copies(srcs, lands, send_sems, recv_sems, by_owner):
    pos = _position()
    me = _slot(*pos)
    out = []
    for a, (src, land) in enumerate(zip(srcs, lands)):
        for k in range(1, N_DEV):
            peer = _peer(pos, k)
            sems = dict(send_sem=send_sems.at[7 * a + k - 1], recv_sem=recv_sems.at[7 * a + k - 1],
                        device_id=peer, device_id_type=MESH)
            mine = src.at[_slot(*peer)] if by_owner else src
            send = pltpu.make_async_remote_copy(src_ref=mine, dst_ref=land.at[me], **sems)
            recv = pltpu.make_async_remote_copy(src_ref=mine, dst_ref=land.at[_slot(*peer)], **sems)
            out.append((send, recv))
    return out


_HBM_SPEC = pl.BlockSpec(memory_space=pltpu.HBM)
_SEM_SPEC = pl.BlockSpec(memory_space=pltpu.SEMAPHORE)
_DATAFLOW = pltpu.SideEffectType.DATAFLOW_SIDE_EFFECTING


def _exchange_start(name, srcs, slab_shapes, after, by_owner, carry=()):
    n, na, nc = len(srcs), len(after), len(carry)
    lands = [pltpu.with_memory_space_constraint(lax.empty((N_DEV,) + s, x.dtype), pltpu.HBM)
             for s, x in zip(slab_shapes, srcs)]
    thru = [pltpu.with_memory_space_constraint(x, pltpu.HBM) for x in [*srcs, *lands, *carry]]

    def body(*refs):
        src_refs, land_refs = refs[:n], refs[n:2 * n]
        send_sems, recv_sems = refs[len(thru) + na], refs[len(thru) + na + 1]
        token = refs[-1]
        for send, _ in _exchange_copies(src_refs, land_refs, send_sems, recv_sems, by_owner):
            send.start()
        token[...] = jnp.zeros_like(token)

    outs = pl.pallas_call(
        body, name=name,
        out_shape=(pltpu.SemaphoreType.DMA((7 * n,)), pltpu.SemaphoreType.DMA((7 * n,)),
                   *[pltpu.HBM(x.shape, x.dtype) for x in thru], jax.ShapeDtypeStruct((8, 128), F32)),
        in_specs=[_HBM_SPEC] * len(thru) + [pl.BlockSpec(memory_space=pl.ANY)] * na,
        out_specs=(_SEM_SPEC, _SEM_SPEC, *[_HBM_SPEC] * len(thru), pl.BlockSpec(memory_space=pltpu.VMEM)),
        input_output_aliases={i: 2 + i for i in range(len(thru))},
        compiler_params=pltpu.CompilerParams(has_side_effects=_DATAFLOW))(*thru, *after)
    return (outs[0], outs[1], list(outs[2:2 + n]), list(outs[2 + n:2 + 2 * n]), outs[-1],
            list(outs[2 + 2 * n:2 + 2 * n + nc]))


def _exchange_wait(name, send_sems, recv_sems, srcs, lands, after, by_owner):
    n = len(srcs)

    def body(*refs):
        src_refs, land_refs = refs[:n], refs[n:2 * n]
        s_sems, r_sems = refs[2 * n], refs[2 * n + 1]
        for send, recv in _exchange_copies(src_refs, land_refs, s_sems, r_sems, by_owner):
            send.wait_send()
            recv.wait_recv()

    outs = pl.pallas_call(
        body, name=name,
        out_shape=(*[pltpu.HBM(x.shape, x.dtype) for x in srcs], *[pltpu.HBM(l.shape, l.dtype) for l in lands]),
        in_specs=[_HBM_SPEC] * (2 * n) + [_SEM_SPEC, _SEM_SPEC, pl.BlockSpec(memory_space=pl.ANY)],
        out_specs=tuple([_HBM_SPEC] * (2 * n)),
        input_output_aliases={i: i for i in range(2 * n)},
        compiler_params=pltpu.CompilerParams(has_side_effects=_DATAFLOW))(*srcs, *lands, send_sems, recv_sems, after)
    return list(outs[:n]), list(outs[n:])


def _pad_heads(x, axis):
    shp = list(x.shape)
    x4 = x.reshape(shp[:axis] + [HEADS, GLA_KEY] + shp[axis + 1:])
    pad = [(0, 0)] * x4.ndim
    pad[axis + 1] = (0, HD - GLA_KEY)
    return jnp.pad(x4, pad).reshape(shp[:axis] + [HEADS * HD] + shp[axis + 1:])


def _unpad_heads(x, axis):
    shp = list(x.shape)
    x4 = x.reshape(shp[:axis] + [HEADS, HD] + shp[axis + 1:])
    x4 = lax.slice_in_dim(x4, 0, GLA_KEY, axis=axis + 1)
    return x4.reshape(shp[:axis] + [HEADS * GLA_KEY] + shp[axis + 1:])


O_Z_END, O_AB, O_GQ, O_GK, O_GV, O_R = 2048, 2048, 2056, 2312, 2568, 3592


def _padded_row(f):
    if f < O_Z_END:
        return f
    if f < O_GQ:
        return P_SM + (f - O_AB)
    if f < O_GV:
        base, g = (P_GQ, f - O_GQ) if f < O_GK else (P_GK, f - O_GK)
        return base + HD * (g // GLA_KEY) + g % GLA_KEY
    if f < O_R:
        return P_GV + (f - O_GV)
    return P_SM + 8 + (f - O_R)


def _runs(pairs):
    out = []
    for d, s in pairs:
        if out and out[-1][0] + out[-1][2] == d and out[-1][1] + out[-1][2] == s:
            out[-1][2] += 1
        else:
            out.append([d, s, 1])
    return out


def _pad_in_rows(shards):
    wt = shards.reshape(IN_W, D)
    return jnp.concatenate([
        wt[:O_Z_END], _pad_heads(wt[O_GQ:O_GK], 0), _pad_heads(wt[O_GK:O_GV], 0), wt[O_GV:O_R],
        wt[O_AB:O_GQ], wt[O_R:], jnp.zeros((P_W - P_SM - 8 - GATE_RANK, D), wt.dtype)], axis=0)


def _unpad_in_rows(gt):
    per = IN_W // N_DEV
    return jnp.stack([
        jnp.concatenate([gt[src:src + n] for _, src, n in
                         _runs([(f, _padded_row(f)) for f in range(j * per, (j + 1) * per)])], axis=0)
        for j in range(N_DEV)])


def _lane_row(vals, width=128):
    return jnp.pad(vals.reshape(1, -1), ((0, 0), (0, width - vals.size)))


SMALL_NAMES = ["ln0_g", "ln0_b", "b_ada", "dn_conv", "dn_a_log", "dn_dt_bias", "dn_norm_g", "gla_w_gate2",
               "gla_b_gate", "gla_norm_g", "ln1_g", "ln1_b", "ffn_conv", "ffn_conv_b", "ln2_g", "ln2_b"]
WEIGHTS = ["ln0_g", "ln0_b", "w_ada", "b_ada", "w_in", "dn_conv", "dn_a_log", "dn_dt_bias", "dn_norm_g",
           "gla_w_gate2", "gla_b_gate", "gla_norm_g", "w_o", "ln1_g", "ln1_b", "ffn_w_up", "ffn_conv", "ffn_conv_b",
           "ffn_w_down", "ln2_g", "ln2_b"]


def kernel(x, c, ln0_g, ln0_b, w_ada, b_ada, w_in, dn_conv, dn_a_log, dn_dt_bias, dn_norm_g, gla_w_gate2, gla_b_gate, gla_norm_g, w_o, ln1_g, ln1_b, ffn_w_up, ffn_conv, ffn_conv_b, ffn_w_down, ln2_g, ln2_b, loss_target, m_ln0_g, m_ln0_b, m_w_ada, m_b_ada, m_w_in, m_dn_conv, m_dn_a_log, m_dn_dt_bias, m_dn_norm_g, m_gla_w_gate2, m_gla_b_gate, m_gla_norm_g, m_w_o, m_ln1_g, m_ln1_b, m_ffn_w_up, m_ffn_conv, m_ffn_conv_b, m_ffn_w_down, m_ln2_g, m_ln2_b, v_ln0_g, v_ln0_b, v_w_ada, v_b_ada, v_w_in, v_dn_conv, v_dn_a_log, v_dn_dt_bias, v_dn_norm_g, v_gla_w_gate2, v_gla_b_gate, v_gla_norm_g, v_w_o, v_ln1_g, v_ln1_b, v_ffn_w_up, v_ffn_conv, v_ffn_conv_b, v_ffn_w_down, v_ln2_g, v_ln2_b):
    args = dict(locals())
    w_given = {n: args[n] for n in WEIGHTS}
    m_given = {n: args["m_" + n] for n in WEIGHTS}
    v_given = {n: args["v_" + n] for n in WEIGHTS}
    bsz, t_total, _ = x.shape
    ntok = bsz * t_total
    mx, my, mc = _position()
    me = _slot(mx, my, mc)

    pack1 = jnp.concatenate([c.reshape(-1), dn_conv.reshape(-1), gla_w_gate2.reshape(-1), ffn_conv.reshape(-1)])
    n1 = pack1.size
    rows1 = -(-n1 // 1024) * 8
    pack1 = jnp.pad(pack1, (0, rows1 * 128 - n1)).reshape(rows1, 128)
    got1 = _gather_small(pack1, "gather_cond").reshape(N_DEV, -1)
    o1 = bsz * D
    o2 = o1 + dn_conv.size
    o3 = o2 + gla_w_gate2.size
    c_all = got1[:, :o1].reshape(N_DEV * bsz, D)
    dn_conv_f = got1[:, o1:o2].reshape(N_DEV, DN_CONV_K, -1).transpose(1, 0, 2).reshape(DN_CONV_K, QKV_W)
    gate2_f = got1[:, o2:o3].reshape(N_DEV, GATE_RANK, -1).transpose(1, 0, 2).reshape(GATE_RANK, HEADS * GLA_KEY)
    ffn_conv_f = got1[:, o3:n1].reshape(N_DEV, FFN_CONV_K, -1).transpose(1, 0, 2).reshape(FFN_CONV_K, 2 * D_FF)

    win_t = w_in[0].T.astype(MXU_DT)
    wup_t = ffn_w_up[0].T.astype(MXU_DT)
    (win_all,) = _gather_big([win_t])
    win_p = _pad_in_rows(win_all)
    cw_p, cb_p = _ffn_pair(ffn_conv_f, 1), _ffn_pair(ffn_conv_b, 1)

    ncol = w_ada.shape[2]
    b_cols = lax.dynamic_slice_in_dim(b_ada, me * ncol, ncol, axis=1)
    mod_part = _ada_fwd(c_all, w_ada[0], b_cols)
    mod_all = _gather_small(mod_part.reshape(-1, 128), "gather_mod").reshape(N_DEV, N_DEV * bsz, ncol)
    mod = lax.dynamic_slice_in_dim(mod_all, me * bsz, bsz, axis=1).transpose(1, 0, 2).reshape(bsz, 6, 1, D)
    late = [w_o[0].astype(MXU_DT), wup_t, ffn_w_down[0].astype(MXU_DT)]
    ag_send, ag_recv, ag_src, ag_land, ag_token, _ = _exchange_start(
        "gather_start", late, [w.shape for w in late], [win_all, mod_all], by_owner=False)
    mod = mod + ag_token[0, 0]
    sh_a, sc_a, gt_a, sh_f, sc_f, gt_f = (mod[:, i] for i in range(6))

    g0, b0 = ln0_g.reshape(1, D), ln0_b.reshape(1, D)
    alog_row, dt_row = _lane_row(dn_a_log[0]), _lane_row(dn_dt_bias[0])
    grow_dn, grow_gla = jnp.tile(dn_norm_g, (1, HEADS)), jnp.tile(gla_norm_g, (1, HEADS))
    w2 = jnp.zeros((128, HEADS * HD), F32).at[SM_R:SM_R + GATE_RANK].set(_pad_heads(gate2_f, 1))
    bg = _pad_heads(gla_b_gate, 1)

    h_a = _ln0_mod(x, g0, b0, sc_a, sh_a)
    proj = _mm(h_a.reshape(ntok, D), win_p, "nt", F32, "mm_proj", tm=1024, tn=1408).reshape(bsz, t_total, P_W)
    q, k, v, gates = _dn_pre_fwd(proj, dn_conv_f, alog_row, dt_row)
    o_dn, s_dn, inv_dn = _dn_rec_fwd(q, k, v, gates)
    o_gla, s_gla = _gla_rec_fwd(proj, w2, bg)
    o_mix = _mix_out_fwd(o_dn, o_gla, proj, grow_dn, grow_gla)
    late, landed = _exchange_wait("gather_wait", ag_send, ag_recv, ag_src, ag_land, o_mix, by_owner=False)
    wo_all, wup_all, wdn_all = (lax.dynamic_update_slice(l, w[None], (me, 0, 0)) for l, w in zip(landed, late))
    wo_f = wo_all.reshape(D, D)
    wup_f = _ffn_pair(wup_all.reshape(2 * D_FF, D), 0)
    wdn_f = wdn_all.reshape(D_FF, D)
    y = _mm(o_mix.reshape(ntok, D), wo_f, "nn", MXU_DT, "mm_wo", tm=1024, tn=1024).reshape(bsz, t_total, D)
    r1, h_f = _res_ln_mod(x, y, gt_a, g0, b0, ln1_g, ln1_b, sc_f, sh_f)
    up, act = _ffn_up_act(h_f, wup_f, cw_p, cb_p)
    y2 = _mm(act.reshape(ntok, D_FF), wdn_f, "nn", MXU_DT, "mm_down", tm=1024, tn=1024).reshape(bsz, t_total, D)
    loss_rows, dr2, dy2, dgt_f, d_ln2_g, d_ln2_b = _final_fwd_bwd(r1, y2, gt_f, ln1_g, ln1_b, ln2_g, ln2_b, loss_target)
    loss_part = (0.5 / D) * jnp.sum(loss_rows)

    dy2_2 = dy2.reshape(ntok, D)
    g_wdn = _mm(act.reshape(ntok, D_FF), dy2_2, "tn", MXU_DT, "mm_gwdn", tm=1408, tn=1024)
    dup, d_cw_p, d_cb_p = _ffn_act_bwd(up, dy2, wdn_f, cw_p, cb_p)
    d_ffn_conv, d_ffn_conv_b = _ffn_unpair(d_cw_p, 1), _ffn_unpair(d_cb_p, 1)
    dup_2 = dup.reshape(ntok, 2 * D_FF)
    dh_f = _mm(dup_2, wup_f, "nn", MXU_DT, "mm_dhf", tn=1024).reshape(bsz, t_total, D)
    g_wup_t = _mm(dup_2, h_f.reshape(ntok, D), "tn", MXU_DT, "mm_gwup", tm=1408, tn=1024)
    ffn_parts = [_ffn_unpair(g_wup_t, 0).reshape(N_DEV, -1, D), g_wdn.reshape(N_DEV, -1, D)]
    rs_send, rs_recv, rs_src, rs_land, rs_token, _ = _exchange_start(
        "scatter_start", ffn_parts, [p.shape[1:] for p in ffn_parts], [dh_f], by_owner=True)
    dr1, dsc_f, dsh_f, d_ln1_g, d_ln1_b, dy, dgt_a = _ln_bwd_call(
        "ln1_bwd", dr2, dh_f, r1, ln1_g, ln1_b, sc_f + rs_token[0, 0], y=y, gt=gt_a)

    dy_2 = dy.reshape(ntok, D)
    do = _mm(dy_2, wo_f, "nt", MXU_DT, "mm_do", tm=1024, tn=1024).reshape(bsz, t_total, D)
    g_wo = _mm(o_mix.reshape(ntok, D), dy_2, "tn", MXU_DT, "mm_gwo", tm=512, tn=1024)
    do_dn, do_gla, dz, dgg, d_dn_norm, d_gla_norm = _mix_out_bwd(do, o_dn, o_gla, proj, grow_dn, grow_gla)
    dq, dk, dv, dgates = _dn_rec_bwd(q, k, v, gates, s_dn, inv_dn, do_dn)
    dqkv, dsm_dn, d_dn_conv, d_alog_row, d_dt_row = _dn_pre_bwd(proj, dq, dk, dv, dgates, dn_conv_f, alog_row, dt_row)
    dgq, dgk, dgv, dsm, d_w2, d_bg = _gla_rec_bwd(proj, w2, bg, s_gla, do_gla, dsm_dn)
    dproj = jnp.concatenate([dqkv, dz, dgq, dgk, dgv, dgg, dsm], axis=-1).reshape(ntok, P_W)
    g_win_p = _mm(dproj, h_a.reshape(ntok, D), "tn", MXU_DT, "mm_gwin", tm=1408, tn=1024)
    mix_parts = [_unpad_in_rows(g_win_p), g_wo.reshape(N_DEV, -1, D)]
    rs2_send, rs2_recv, rs2_src, rs2_land, rs2_token, (win_p_late,) = _exchange_start(
        "scatter_mix_start", mix_parts, [p.shape[1:] for p in mix_parts], [], by_owner=True, carry=[win_p])
    dh_a = _mm(dproj, win_p_late, "nn", MXU_DT, "mm_dha", tn=1024).reshape(bsz, t_total, D)
    grad_x, dsc_a, dsh_a, d_ln0_g, d_ln0_b = _ln_bwd_call(
        "ln0_bwd", dr1, dh_a, x, g0, b0, sc_a + rs2_token[0, 0])

    delta, new_m, new_v, big_grads = {}, {}, {}, {}
    flip = lambda a: jnp.swapaxes(a, 1, 2)

    def update_owned(n, landed, mine):
        parts = lax.dynamic_update_slice(landed, lax.dynamic_slice_in_dim(mine, me, 1, axis=0), (me, 0, 0))
        turn = flip if parts.shape[1:] != w_given[n].shape[1:] else (lambda a: a)
        g, d_, m_, v_ = _sum_adamw(parts, turn(w_given[n]), turn(m_given[n]), turn(v_given[n]), "adamw_" + n)
        big_grads[n], delta[n], new_m[n], new_v[n] = turn(g[None]), turn(d_), turn(m_), turn(v_)

    ffn_parts, ffn_landed = _exchange_wait("scatter_wait", rs_send, rs_recv, rs_src, rs_land, grad_x, by_owner=True)
    update_owned("ffn_w_up", ffn_landed[0], ffn_parts[0])
    update_owned("ffn_w_down", ffn_landed[1], ffn_parts[1])
    ffn_done = 0.0 * (new_v["ffn_w_up"][0, 0, 0] + new_v["ffn_w_down"][0, 0, 0])

    dmod = jnp.concatenate([dsh_a, dsc_a, dgt_a, dsh_f, dsc_f, dgt_f], axis=1).reshape(-1)
    small_parts = {
        "ln0_g": d_ln0_g, "ln0_b": d_ln0_b, "ln1_g": d_ln1_g, "ln1_b": d_ln1_b, "ln2_g": d_ln2_g, "ln2_b": d_ln2_b,
        "dn_a_log": d_alog_row[:, :HEADS], "dn_dt_bias": d_dt_row[:, :HEADS],
        "dn_norm_g": d_dn_norm, "gla_norm_g": d_gla_norm, "gla_b_gate": _unpad_heads(d_bg, 1),
        "ffn_conv_b": d_ffn_conv_b, "dn_conv": d_dn_conv,
        "gla_w_gate2": _unpad_heads(d_w2[SM_R:SM_R + GATE_RANK], 1), "ffn_conv": d_ffn_conv}
    order = sorted(small_parts)
    flat = jnp.concatenate([small_parts[n].reshape(-1) for n in order] + [(loss_part + ffn_done).reshape(1), dmod])
    n3 = flat.size
    rows3 = -(-n3 // 1024) * 8
    pack3 = jnp.pad(flat, (0, rows3 * 128 - n3)).reshape(rows3, 128)
    got3 = _gather_small(pack3, "gather_small_grads")
    tot3 = _sum_slots(got3, "sum_small_grads").reshape(-1)
    grads = {}
    off = 0
    for n in order:
        size = small_parts[n].size
        grads[n] = tot3[off:off + size]
        off += size
    loss = tot3[off]
    off += 1
    dmod_all = got3.reshape(N_DEV, -1)[:, off:off + dmod.size].reshape(N_DEV * bsz, 6 * D)
    dmod_cols = lax.dynamic_slice_in_dim(dmod_all, me * ncol, ncol, axis=1)
    g_wada, g_bada = _ada_bwd(c_all, dmod_all, dmod_cols)
    grads["b_ada"] = g_bada

    def col_shard(full, rows):
        part = full.reshape(rows, -1)
        width = part.shape[1] // N_DEV
        return lax.dynamic_slice_in_dim(part, me * width, width, axis=1)

    grads["dn_conv"] = col_shard(grads["dn_conv"], DN_CONV_K)
    grads["gla_w_gate2"] = col_shard(grads["gla_w_gate2"], GATE_RANK)
    grads["ffn_conv"] = col_shard(grads["ffn_conv"], FFN_CONV_K)
    grads = {n: g.reshape(w_given[n].shape) for n, g in grads.items()}
    mix_parts, mix_landed = _exchange_wait("scatter_mix_wait", rs2_send, rs2_recv, rs2_src, rs2_land, grad_x,
                                           by_owner=True)
    update_owned("w_in", mix_landed[0], mix_parts[0])
    update_owned("w_o", mix_landed[1], mix_parts[1])
    grads["w_ada"] = g_wada.reshape(w_ada.shape)
    delta["w_ada"], new_m["w_ada"], new_v["w_ada"] = _adamw(w_ada, grads["w_ada"], m_w_ada, v_w_ada, "adamw_w_ada")
    grads.update(big_grads)
    d_s, m_s, v_s = _adamw_many(*[[src[n] for n in SMALL_NAMES] for src in (w_given, grads, m_given, v_given)],
                                "adamw_small")
    for i, n in enumerate(SMALL_NAMES):
        delta[n], new_m[n], new_v[n] = d_s[i], m_s[i], v_s[i]

    return (loss, grad_x, *[grads[n] for n in WEIGHTS], *[delta[n] for n in WEIGHTS],
            *[new_m[n] for n in WEIGHTS], *[new_v[n] for n in WEIGHTS])
```

```python
import functools

import jax
import jax.numpy as jnp
from jax import lax
from jax.experimental import pallas as pl
from jax.experimental.pallas import tpu as pltpu

F32 = jnp.float32
MXU_DT = jnp.bfloat16
MESH = pl.DeviceIdType.MESH
N_DEV = 8

D = 1024
HEADS = 4
HD = 128
CHUNK = 64
GLA_KEY = 64
GLA_TAU = 16.0
GATE_RANK = 16
D_FF = 2816
IN_W = 3608
ALPHA = 2.0 ** 0.25
EPS = 1e-6
DN_CONV_K = 4
FFN_CONV_K = 3
HALO = 8
ROW_TILE = 1024
FFN_ROW_TILE = 1024

P_QKV, P_Z, P_GQ, P_GK, P_GV, P_GG, P_SM, P_W = 0, 1536, 2048, 2560, 3072, 3584, 4096, 4224
SM_A, SM_B, SM_R = 0, 4, 8

ADAM_LR, ADAM_B1, ADAM_B2, ADAM_EPS, ADAM_WD, ADAM_STEP = 0.001, 0.9, 0.999, 1e-08, 0.01, 10

VMEM_LIMIT_V7X = 56 * 1024 * 1024


def _params(sem=None):
    return pltpu.CompilerParams(dimension_semantics=sem, vmem_limit_bytes=VMEM_LIMIT_V7X)


def _dg(a, b, dims, prec=None):
    return lax.dot_general(a, b, (dims, ((), ())), precision=prec, preferred_element_type=F32)


def _dot(a, b, prec=None):
    return _dg(a, b, ((1,), (0,)), prec)


def _dot_nt(a, b, prec=None):
    return _dg(a, b, ((1,), (1,)), prec)


def _dot_tn(a, b, prec=None):
    return _dg(a, b, ((0,), (0,)), prec)


def _iota(shape, dim):
    return lax.broadcasted_iota(jnp.int32, shape, dim)


def _sigmoid(x):
    return jax.nn.sigmoid(x)


def _silu(x):
    return x * _sigmoid(x)


def _softplus(x):
    return jnp.maximum(x, 0.0) + jnp.log(1.0 + jnp.exp(-jnp.abs(x)))


def _ln_stats(x):
    mu = jnp.mean(x, axis=-1, keepdims=True)
    xc = x - mu
    rstd = lax.rsqrt(jnp.mean(xc * xc, axis=-1, keepdims=True) + EPS)
    return xc * rstd, rstd


def _ln_bwd(dxhat, xhat, rstd):
    return rstd * (dxhat - jnp.mean(dxhat, axis=-1, keepdims=True)
                   - xhat * jnp.mean(dxhat * xhat, axis=-1, keepdims=True))


NN, NT, TN = ((1,), (0,)), ((1,), (1,)), ((0,), (0,))


def _split2(a):
    hi = a.astype(jnp.bfloat16)
    return hi, (a - hi.astype(F32)).astype(jnp.bfloat16)


def _d3(a, b, dims):
    ah, al = _split2(a)
    bh, bl = _split2(b)
    return _dg(ah, bh, dims) + (_dg(ah, bl, dims) + _dg(al, bh, dims))


@jax.custom_vjp
def _dot3(a, b):
    return _d3(a, b, NN)


_dot3.defvjp(lambda a, b: (_d3(a, b, NN), (a, b)),
             lambda res, g: (_d3(g, res[1], NT), _d3(res[0], g, TN)))


def _split3(b):
    b1 = b.astype(jnp.bfloat16)
    r1 = b - b1.astype(F32)
    b2 = r1.astype(jnp.bfloat16)
    return b1, b2, (r1 - b2.astype(F32)).astype(jnp.bfloat16)


def _sum3(fn, b):
    b1, b2, b3 = _split3(b)
    return fn(b1) + (fn(b2) + fn(b3))


@jax.custom_vjp
def _mask_dot(e, b):
    return _sum3(lambda t: _dg(e, t, NN), b)


_mask_dot.defvjp(lambda e, b: (_mask_dot(e, b), e),
                 lambda e, g: (jnp.zeros_like(e), _sum3(lambda t: _dg(e, t, TN), g)))


@jax.custom_vjp
def _mask_dot_nt(e, b):
    return _sum3(lambda t: _dg(e, t, NT), b)


_mask_dot_nt.defvjp(lambda e, b: (_mask_dot_nt(e, b), e),
                    lambda e, g: (jnp.zeros_like(e), _sum3(lambda t: _dg(t, e, TN), g)))


def _tri_inv_impl(ms):
    n = ms[0].shape[0]
    r, c = _iota((n, n), 0), _iota((n, n), 1)
    eye = (r == c).astype(F32)
    diag = (r >> 3) == (c >> 3)
    ds = [jnp.where(diag, m, 0.0) for m in ms]
    d2s = [_d3(d, d, NN) for d in ds]
    d4s = [_d3(d2, d2, NN) for d2 in d2s]
    invs = [_d3(eye - d, eye + d2, NN) for d, d2 in zip(ds, d2s)]
    invs = [_d3(inv, eye + d4, NN) for inv, d4 in zip(invs, d4s)]
    shift = 3
    while (1 << shift) < n:
        rb, cb = r >> shift, c >> shift
        sel = ((rb & 1) == 1) & (cb == rb - 1)
        tmp = [_d3(inv, jnp.where(sel, m, 0.0), NN) for inv, m in zip(invs, ms)]
        invs = [inv - _d3(t, inv, NN) for t, inv in zip(tmp, invs)]
        shift += 1
    return invs


@jax.custom_vjp
def _tri_inv(ms):
    return _tri_inv_impl(ms)


def _tri_inv_fwd(ms):
    invs = _tri_inv_impl(ms)
    return invs, invs


def _tri_inv_bwd(invs, das):
    tmp = [_d3(a, da, TN) for a, da in zip(invs, das)]
    return ([-_d3(t, a, NT) for t, a in zip(tmp, invs)],)


_tri_inv.defvjp(_tri_inv_fwd, _tri_inv_bwd)


@jax.custom_vjp
def _tri_inv_known(ms, invs):
    return invs


_tri_inv_known.defvjp(lambda ms, invs: (invs, invs),
                      lambda invs, das: (_tri_inv_bwd(invs, das)[0], [jnp.zeros_like(a) for a in invs]))


def _dn_chunk(s_list, q, k, v, gates, inv_known=None, with_inv=False):
    nb = len(q)
    c = q[0].shape[0]
    r64, c64 = _iota((c, c), 0), _iota((c, c), 1)
    causal = r64 >= c64
    strict = r64 > c64
    tri = causal.astype(jnp.bfloat16)
    eye = (_iota((HD, HD), 0) == _iota((HD, HD), 1)).astype(jnp.bfloat16)
    lane = _iota(gates[0].shape, 1)
    lane1 = _iota((1, HD), 1)
    g_all = [_mask_dot(tri, g) for g in gates]
    g_all_t = [_mask_dot_nt(eye, g) for g in g_all]
    row = _iota(g_all_t[0].shape, 0)
    last = [jnp.sum(g, axis=0, keepdims=True) for g in gates]
    prob = [(b, h) for b in range(nb) for h in range(HEADS)]
    sl = [slice(h * HD, (h + 1) * HD) for h in range(HEADS)]
    qh = [q[b][:, sl[h]] for b, h in prob]
    kh = [k[b][:, sl[h]] for b, h in prob]
    vh = [v[b][:, sl[h]] for b, h in prob]
    s = [s_list[b][h] for b, h in prob]
    beta = [jnp.sum(jnp.where(lane == SM_B + h, gates[b], 0.0), axis=-1, keepdims=True) for b, h in prob]
    g_c = [jnp.sum(jnp.where(lane == SM_A + h, g_all[b], 0.0), axis=-1, keepdims=True) for b, h in prob]
    g_r = [jnp.sum(jnp.where(row == SM_A + h, g_all_t[b], 0.0), axis=0, keepdims=True) for b, h in prob]
    g_last = [jnp.sum(jnp.where(lane1 == SM_A + h, last[b], 0.0), axis=-1, keepdims=True) for b, h in prob]
    decay = [jnp.where(causal, jnp.exp(jnp.where(causal, gc - gr, 0.0)), 0.0) for gc, gr in zip(g_c, g_r)]
    kb = [k_ * b_ for k_, b_ in zip(kh, beta)]
    m_low = [jnp.where(strict, _dot_nt(kb_, k_) * d_, 0.0) for kb_, k_, d_ in zip(kb, kh, decay)]
    attn = [_dot_nt(q_, k_) * d_ for q_, k_, d_ in zip(qh, kh, decay)]
    a_inv = _tri_inv(m_low) if inv_known is None else _tri_inv_known(m_low, inv_known)
    eg = [jnp.exp(gc) for gc in g_c]
    uw = [_dot3(a_, jnp.concatenate([v_ * b_, kb_ * e_], axis=1))
          for a_, v_, b_, kb_, e_ in zip(a_inv, vh, beta, kb, eg)]
    v_new = [uw_[:, :HD] - _dot(uw_[:, HD:], s_) for uw_, s_ in zip(uw, s)]
    qs = [_dot(q_ * e_, s_) for q_, e_, s_ in zip(qh, eg, s)]
    o = [qs_ + _dot(a_, vn_) for qs_, a_, vn_ in zip(qs, attn, v_new)]
    k_dec = [k_ * jnp.exp(gl - gc) for k_, gl, gc in zip(kh, g_last, g_c)]
    s_new = [s_ * jnp.exp(gl) + _dot_tn(kd_, vn_) for s_, gl, kd_, vn_ in zip(s, g_last, k_dec, v_new)]
    outs = [jnp.concatenate(o[b * HEADS:(b + 1) * HEADS], axis=-1) for b in range(nb)]
    states = [s_new[b * HEADS:(b + 1) * HEADS] for b in range(nb)]
    return (outs, states, a_inv) if with_inv else (outs, states)


def _gla_chunk(st_list, q, k, v, small, w2, bg):
    nb = len(q)
    c = q[0].shape[0]
    causal = _iota((c, c), 0) >= _iota((c, c), 1)
    tri = causal.astype(jnp.bfloat16)
    la_all = [-_softplus(-(_dot(sm, w2) + bg)) * (1.0 / GLA_TAU) for sm in small]
    b_all = [_mask_dot(tri, la) for la in la_all]
    prob = [(b, h) for b in range(nb) for h in range(HEADS)]
    sl = [slice(h * HD, (h + 1) * HD) for h in range(HEADS)]
    kh = [k[b][:, sl[h]] for b, h in prob]
    vh = [v[b][:, sl[h]] for b, h in prob]
    st = [st_list[b][h] for b, h in prob]
    bc = [b_all[b][:, sl[h]] for b, h in prob]
    b_last = [jnp.sum(la_all[b][:, sl[h]], axis=0, keepdims=True) for b, h in prob]
    q_dec = [q[b][:, sl[h]] * (GLA_KEY ** -0.5) * jnp.exp(bc_) for (b, h), bc_ in zip(prob, bc)]
    attn = [jnp.where(causal, _dot_nt(qd, k_ * jnp.exp(-bc_)), 0.0) for qd, k_, bc_ in zip(q_dec, kh, bc)]
    inter = [_dot_nt(qd, st_) for qd, st_ in zip(q_dec, st)]
    o = [i_ + _dot(a_, v_) for i_, a_, v_ in zip(inter, attn, vh)]
    k_dec = [k_ * jnp.exp(bl - bc_) for k_, bl, bc_ in zip(kh, b_last, bc)]
    s_new = [st_ * jnp.exp(bl) + _dot_tn(v_, kd) for st_, bl, v_, kd in zip(st, b_last, vh, k_dec)]
    outs = [jnp.concatenate(o[b * HEADS:(b + 1) * HEADS], axis=-1) for b in range(nb)]
    return outs, [s_new[b * HEADS:(b + 1) * HEADS] for b in range(nb)]


def _dn_qkv(y):
    act = _silu(y)
    parts = []
    for i in range(2 * HEADS):
        xh = act[:, i * HD:(i + 1) * HD]
        xh = xh * lax.rsqrt(jnp.sum(xh * xh, axis=-1, keepdims=True) + EPS)
        parts.append(xh * (HD ** -0.5) if i < HEADS else xh)
    qk = jnp.concatenate(parts, axis=-1)
    return qk[:, :HEADS * HD], qk[:, HEADS * HD:], act[:, 2 * HEADS * HD:]


def _dn_gates(small, alog_row, dt_row):
    lane = _iota(small.shape, 1)
    log_a = -jnp.exp(alog_row) * _softplus(small + dt_row)
    return jnp.where(lane < SM_B, log_a, jnp.where(lane < SM_R, _sigmoid(small), 0.0))


def _gate_norm(o, z, grow):
    parts = []
    for h in range(HEADS):
        oh = o[:, h * HD:(h + 1) * HD]
        parts.append(oh * lax.rsqrt(jnp.mean(oh * oh, axis=-1, keepdims=True) + EPS))
    return jnp.concatenate(parts, axis=-1) * grow * _silu(z)


def _conv_rows(xrows, w_ref, k_taps):
    n = xrows.shape[0]
    acc = xrows * w_ref[k_taps - 1:k_taps, :]
    for s in range(1, k_taps):
        acc = acc + pltpu.roll(xrows, s, 0) * w_ref[k_taps - 1 - s:k_taps - s, :]
    return acc


def _shift_up(x, s):
    return x if s == 0 else pltpu.roll(x, x.shape[0] - s, 0)


def _div_tile(n, cap, mult=8):
    best = None
    for t in range(mult, min(n, cap) + 1, mult):
        if n % t == 0:
            best = t
    return best if best is not None else n


def _halo_prev(tt):
    return lambda b, t: (b, jnp.maximum(t * (tt // HALO) - 1, 0))


def _halo_next(tt, t_total):
    return lambda b, t: (b, jnp.minimum((t + 1) * (tt // HALO), t_total // HALO - 1))


def _mm(a, b, mode, out_dtype, name, tm=512, tn=512, tk=None):
    if mode == "nn":
        (m, k), n = a.shape, b.shape[1]
    elif mode == "nt":
        (m, k), n = a.shape, b.shape[0]
    else:
        (k, m), n = a.shape, b.shape[1]
    tm, tn = min(tm, m), min(tn, n)
    tk = k if tk is None else min(tk, k)
    assert m % tm == 0 and n % tn == 0 and k % tk == 0, (name, a.shape, b.shape, tm, tn, tk)
    nk = k // tk
    if mode == "tn":
        a_spec = pl.BlockSpec((tk, tm), lambda i, j, kk: (kk, i))
    else:
        a_spec = pl.BlockSpec((tm, tk), lambda i, j, kk: (i, kk))
    if mode == "nt":
        b_spec = pl.BlockSpec((tn, tk), lambda i, j, kk: (j, kk))
    else:
        b_spec = pl.BlockSpec((tk, tn), lambda i, j, kk: (kk, j))
    dims = {"nn": ((1,), (0,)), "nt": ((1,), (1,)), "tn": ((0,), (0,))}[mode]

    def body(a_ref, b_ref, o_ref, *acc):
        p = _dg(a_ref[...], b_ref[...], dims)
        if nk == 1:
            o_ref[...] = p.astype(out_dtype)
        else:
            kk = pl.program_id(2)

            @pl.when(kk == 0)
            def _():
                acc[0][...] = p

            @pl.when(kk > 0)
            def _():
                acc[0][...] += p

            @pl.when(kk == nk - 1)
            def _():
                o_ref[...] = acc[0][...].astype(out_dtype)

    return pl.pallas_call(
        body, name=name, grid=(m // tm, n // tn, nk),
        in_specs=[a_spec, b_spec],
        out_specs=pl.BlockSpec((tm, tn), lambda i, j, kk: (i, j)),
        out_shape=jax.ShapeDtypeStruct((m, n), out_dtype),
        scratch_shapes=[pltpu.VMEM((tm, tn), F32)] if nk > 1 else [],
        compiler_params=_params(("parallel", "parallel", "arbitrary")),
    )(a, b)


def _ada_fwd(c_all, w_ada, b_cols):
    def body(c_ref, w_ref, b_ref, o_ref):
        cond = _silu(c_ref[...]).astype(MXU_DT)
        o_ref[...] = _dot(cond, w_ref[...].astype(MXU_DT)) + b_ref[...]

    return pl.pallas_call(body, name="ada_fwd", out_shape=jax.ShapeDtypeStruct((c_all.shape[0], w_ada.shape[1]), F32),
                          compiler_params=_params())(c_all, w_ada, b_cols)


def _ada_bwd(c_all, dmod_all, dmod_cols):
    def body(c_ref, da_ref, dc_ref, gw_ref, gb_ref):
        cond = _silu(c_ref[...]).astype(MXU_DT)
        gw_ref[...] = _dot_tn(cond, dc_ref[...].astype(MXU_DT))
        gb_ref[...] = jnp.sum(da_ref[...], axis=0, keepdims=True)

    return pl.pallas_call(
        body, name="ada_bwd",
        out_shape=(jax.ShapeDtypeStruct((c_all.shape[1], dmod_cols.shape[1]), F32),
                   jax.ShapeDtypeStruct((1, dmod_all.shape[1]), F32)),
        compiler_params=_params())(c_all, dmod_all, dmod_cols)


def _tok_spec(tt, width=D):
    return pl.BlockSpec((1, tt, width), lambda b, t: (b, t, 0))


def _vec_spec(width=D):
    return pl.BlockSpec((1, width), lambda b, t: (0, 0))


def _bvec_spec(width=D):
    return pl.BlockSpec((1, 1, width), lambda b, t: (b, 0, 0))


def _ln0_mod(x, g0, b0, sc, sh):
    bsz, t_total, _ = x.shape
    tt = _div_tile(t_total, ROW_TILE)

    def body(x_ref, g_ref, b_ref, sc_ref, sh_ref, h_ref):
        xh, _ = _ln_stats(x_ref[0])
        x0 = xh * g_ref[...] + b_ref[...]
        h_ref[0] = (x0 * (1.0 + sc_ref[0]) + sh_ref[0]).astype(MXU_DT)

    return pl.pallas_call(
        body, name="ln0_mod", grid=(bsz, t_total // tt),
        in_specs=[_tok_spec(tt), _vec_spec(), _vec_spec(), _bvec_spec(), _bvec_spec()],
        out_specs=_tok_spec(tt), out_shape=jax.ShapeDtypeStruct(x.shape, MXU_DT),
        compiler_params=_params(("parallel", "parallel")))(x, g0, b0, sc, sh)


def _res_ln_mod(x, y, gt, g0, b0, g1, b1, sc, sh):
    bsz, t_total, _ = x.shape
    tt = _div_tile(t_total, ROW_TILE)

    def body(x_ref, y_ref, gt_ref, g0_ref, b0_ref, g1_ref, b1_ref, sc_ref, sh_ref, r_ref, h_ref):
        xh, _ = _ln_stats(x_ref[0])
        r = ALPHA * (xh * g0_ref[...] + b0_ref[...]) + (1.0 + gt_ref[0]) * y_ref[0].astype(F32)
        r_ref[0] = r
        rh, _ = _ln_stats(r)
        x1 = rh * g1_ref[...] + b1_ref[...]
        h_ref[0] = (x1 * (1.0 + sc_ref[0]) + sh_ref[0]).astype(MXU_DT)

    return pl.pallas_call(
        body, name="res_ln_mod", grid=(bsz, t_total // tt),
        in_specs=[_tok_spec(tt), _tok_spec(tt), _bvec_spec(), _vec_spec(), _vec_spec(), _vec_spec(), _vec_spec(),
                  _bvec_spec(), _bvec_spec()],
        out_specs=(_tok_spec(tt), _tok_spec(tt)),
        out_shape=(jax.ShapeDtypeStruct(x.shape, F32), jax.ShapeDtypeStruct(x.shape, MXU_DT)),
        compiler_params=_params(("parallel", "parallel")))(x, y, gt, g0, b0, g1, b1, sc, sh)


def _final_fwd_bwd(r1, y2, gt, g1, b1, g2, b2, target):
    bsz, t_total, _ = r1.shape
    tt = _div_tile(t_total, ROW_TILE)

    def body(r1_ref, y2_ref, gt_ref, g1_ref, b1_ref, g2_ref, b2_ref, tg_ref,
             loss_ref, dr2_ref, dy2_ref, dgt_ref, dg2_ref, db2_ref):
        b, t = pl.program_id(0), pl.program_id(1)

        @pl.when((b == 0) & (t == 0))
        def _():
            loss_ref[...] = jnp.zeros_like(loss_ref)
            dg2_ref[...] = jnp.zeros_like(dg2_ref)
            db2_ref[...] = jnp.zeros_like(db2_ref)

        @pl.when(t == 0)
        def _():
            dgt_ref[...] = jnp.zeros_like(dgt_ref)

        rh1, _ = _ln_stats(r1_ref[0])
        x1 = rh1 * g1_ref[...] + b1_ref[...]
        y2 = y2_ref[0].astype(F32)
        gate = 1.0 + gt_ref[0]
        xh2, rstd2 = _ln_stats(ALPHA * x1 + gate * y2)
        err = xh2 * g2_ref[...] + b2_ref[...] - tg_ref[0]
        loss_ref[...] += jnp.sum(err * err, axis=0, keepdims=True)
        dx2 = err * (1.0 / D)
        dg2_ref[...] += jnp.sum(dx2 * xh2, axis=0, keepdims=True)
        db2_ref[...] += jnp.sum(dx2, axis=0, keepdims=True)
        dr2 = _ln_bwd(dx2 * g2_ref[...], xh2, rstd2)
        dr2_ref[0] = dr2
        dy2_ref[0] = (gate * dr2).astype(MXU_DT)
        dgt_ref[0] += jnp.sum(dr2 * y2, axis=0, keepdims=True)

    vec_out = jax.ShapeDtypeStruct((1, D), F32)
    return pl.pallas_call(
        body, name="final_fwd_bwd", grid=(bsz, t_total // tt),
        in_specs=[_tok_spec(tt), _tok_spec(tt), _bvec_spec(), _vec_spec(), _vec_spec(), _vec_spec(), _vec_spec(),
                  _tok_spec(tt)],
        out_specs=(_vec_spec(), _tok_spec(tt), _tok_spec(tt), _bvec_spec(), _vec_spec(), _vec_spec()),
        out_shape=(vec_out, jax.ShapeDtypeStruct(r1.shape, F32), jax.ShapeDtypeStruct(r1.shape, MXU_DT),
                   jax.ShapeDtypeStruct((bsz, 1, D), F32), vec_out, vec_out),
        compiler_params=_params(("arbitrary", "arbitrary")))(r1, y2, gt, g1, b1, g2, b2, target)


def _ln_bwd_call(name, d_res, d_h, src, g, b, sc, y=None, gt=None):
    bsz, t_total, _ = src.shape
    tt = _div_tile(t_total, ROW_TILE)
    has_y = y is not None

    def body(*refs):
        if has_y:
            (dres_ref, dh_ref, src_ref, g_ref, b_ref, sc_ref, y_ref, gt_ref,
             dsrc_ref, dsc_ref, dsh_ref, dg_ref, db_ref, dy_ref, dgt_ref) = refs
        else:
            (dres_ref, dh_ref, src_ref, g_ref, b_ref, sc_ref,
             dsrc_ref, dsc_ref, dsh_ref, dg_ref, db_ref) = refs
        bi, t = pl.program_id(0), pl.program_id(1)

        @pl.when((bi == 0) & (t == 0))
        def _():
            dg_ref[...] = jnp.zeros_like(dg_ref)
            db_ref[...] = jnp.zeros_like(db_ref)

        @pl.when(t == 0)
        def _():
            dsc_ref[...] = jnp.zeros_like(dsc_ref)
            dsh_ref[...] = jnp.zeros_like(dsh_ref)
            if has_y:
                dgt_ref[...] = jnp.zeros_like(dgt_ref)

        xh, rstd = _ln_stats(src_ref[0])
        xv = xh * g_ref[...] + b_ref[...]
        dh = dh_ref[0].astype(F32)
        dx = ALPHA * dres_ref[0] + dh * (1.0 + sc_ref[0])
        dsc_ref[0] += jnp.sum(dh * xv, axis=0, keepdims=True)
        dsh_ref[0] += jnp.sum(dh, axis=0, keepdims=True)
        dg_ref[...] += jnp.sum(dx * xh, axis=0, keepdims=True)
        db_ref[...] += jnp.sum(dx, axis=0, keepdims=True)
        dsrc = _ln_bwd(dx * g_ref[...], xh, rstd)
        dsrc_ref[0] = dsrc
        if has_y:
            dy_ref[0] = ((1.0 + gt_ref[0]) * dsrc).astype(MXU_DT)
            dgt_ref[0] += jnp.sum(dsrc * y_ref[0].astype(F32), axis=0, keepdims=True)

    vec_out = jax.ShapeDtypeStruct((1, D), F32)
    bvec_out = jax.ShapeDtypeStruct((bsz, 1, D), F32)
    in_specs = [_tok_spec(tt), _tok_spec(tt), _tok_spec(tt), _vec_spec(), _vec_spec(), _bvec_spec()]
    out_specs = [_tok_spec(tt), _bvec_spec(), _bvec_spec(), _vec_spec(), _vec_spec()]
    out_shape = [jax.ShapeDtypeStruct(src.shape, F32), bvec_out, bvec_out, vec_out, vec_out]
    args = [d_res, d_h, src, g, b, sc]
    if has_y:
        in_specs += [_tok_spec(tt), _bvec_spec()]
        out_specs += [_tok_spec(tt), _bvec_spec()]
        out_shape += [jax.ShapeDtypeStruct(src.shape, MXU_DT), bvec_out]
        args += [y, gt]
    return pl.pallas_call(body, name=name, grid=(bsz, t_total // tt), in_specs=in_specs, out_specs=tuple(out_specs),
                          out_shape=tuple(out_shape), compiler_params=_params(("arbitrary", "arbitrary")))(*args)


FFN_TC = 256
FFN_NJ = D_FF // FFN_TC
FFN_PW = 2 * FFN_TC


def _ffn_pair(a, axis):
    shp = list(a.shape)
    a4 = a.reshape(shp[:axis] + [2, FFN_NJ, FFN_TC] + shp[axis + 1:])
    return jnp.swapaxes(a4, axis, axis + 1).reshape(shp)


def _ffn_unpair(a, axis):
    shp = list(a.shape)
    a4 = a.reshape(shp[:axis] + [FFN_NJ, 2, FFN_TC] + shp[axis + 1:])
    return jnp.swapaxes(a4, axis, axis + 1).reshape(shp)


def _ffn_up_act(h, w_up, cw, cb):
    bsz, t_total, _ = h.shape
    tt = _div_tile(t_total, FFN_ROW_TILE)
    def body(h_ref, wu_ref, w_ref, b_ref, up_ref, u_ref, o_ref, carry_ref):
        up_t = _dot_nt(h_ref[0], wu_ref[...])
        up_ref[0] = up_t
        prev = jnp.where(pl.program_id(2) == 0, 0.0, carry_ref[...])
        rows = jnp.concatenate([prev, up_t], axis=0)
        u = _conv_rows(rows, w_ref, FFN_CONV_K)[HALO:] + b_ref[...]
        u_ref[0] = u
        o_ref[0] = (_silu(u[:, :FFN_TC]) * u[:, FFN_TC:]).astype(MXU_DT)
        carry_ref[...] = up_t[tt - HALO:, :]

    wide = pl.BlockSpec((1, tt, FFN_PW), lambda b, j, t: (b, t, j))
    wide_shape = jax.ShapeDtypeStruct((bsz, t_total, 2 * D_FF), F32)
    return pl.pallas_call(
        body, name="ffn_up_act", grid=(bsz, FFN_NJ, t_total // tt),
        in_specs=[pl.BlockSpec((1, tt, D), lambda b, j, t: (b, t, 0)),
                  pl.BlockSpec((FFN_PW, D), lambda b, j, t: (j, 0)),
                  pl.BlockSpec((FFN_CONV_K, FFN_PW), lambda b, j, t: (0, j)),
                  pl.BlockSpec((1, FFN_PW), lambda b, j, t: (0, j))],
        out_specs=(wide, wide, pl.BlockSpec((1, tt, FFN_TC), lambda b, j, t: (b, t, j))),
        out_shape=(wide_shape, wide_shape, jax.ShapeDtypeStruct((bsz, t_total, D_FF), MXU_DT)),
        scratch_shapes=[pltpu.VMEM((HALO, FFN_PW), F32)],
        compiler_params=_params(("parallel", "parallel", "arbitrary")))(h, w_up, cw, cb)


HALO16 = 16


def _ffn_act_bwd(up, u_conv, dy2, w_down, cw):
    bsz, t_total, width = up.shape
    tt = _div_tile(t_total, FFN_ROW_TILE)
    nt = t_total // tt
    hp, hn = _halo_prev(tt), _halo_next(tt, t_total)

    def body(x_ref, xp_ref, u_ref, un_ref, dy_ref, dyn_ref, wd_ref, w_ref, dup_ref, dw_ref, db_ref):
        b, t = pl.program_id(1), pl.program_id(2)

        @pl.when((b == 0) & (t == 0))
        def _():
            dw_ref[...] = jnp.zeros_like(dw_ref)
            db_ref[...] = jnp.zeros_like(db_ref)

        prev = jnp.where(t == 0, 0.0, xp_ref[0])
        rows = jnp.concatenate([prev, x_ref[0]], axis=0)
        u = jnp.concatenate([u_ref[0], un_ref[0]], axis=0)
        g_pre, v_pre = u[:, :FFN_TC], u[:, FFN_TC:]
        valid = (_iota((tt + HALO, 1), 0) < tt) | (t < nt - 1)
        da = jnp.concatenate([_dot_nt(dy_ref[0], wd_ref[...]), _dot_nt(dyn_ref[0], wd_ref[...])[:HALO]], axis=0)
        da_ext = jnp.where(valid, da, 0.0)
        sg = _sigmoid(g_pre)
        gs = g_pre * sg
        du = jnp.concatenate([da_ext * v_pre * (sg + gs * (1.0 - sg)), da_ext * gs], axis=1)
        dup = du * w_ref[FFN_CONV_K - 1:FFN_CONV_K, :]
        for s in range(1, FFN_CONV_K):
            dup = dup + _shift_up(du, s) * w_ref[FFN_CONV_K - 1 - s:FFN_CONV_K - s, :]
        dup_ref[0] = dup[:tt].astype(MXU_DT)
        du_t = du[:tt]
        db_ref[...] += jnp.sum(du_t, axis=0, keepdims=True)
        for k in range(FFN_CONV_K):
            s = FFN_CONV_K - 1 - k
            xs = (rows if s == 0 else pltpu.roll(rows, s, 0))[HALO:HALO + tt]
            dw_ref[k:k + 1, :] += jnp.sum(du_t * xs, axis=0, keepdims=True)

    def halo(h, w):
        return pl.BlockSpec((1, HALO, w), lambda j, b, t: (*h(b, t), j))

    wspec = lambda rows_: pl.BlockSpec((rows_, FFN_PW), lambda j, b, t: (0, j))
    tile = pl.BlockSpec((1, tt, FFN_PW), lambda j, b, t: (b, t, j))
    dy_next = lambda j, b, t: (b, jnp.minimum((t + 1) * (tt // HALO16), t_total // HALO16 - 1), 0)
    return pl.pallas_call(
        body, name="ffn_act_bwd", grid=(FFN_NJ, bsz, nt),
        in_specs=[tile, halo(hp, FFN_PW), tile, halo(hn, FFN_PW),
                  pl.BlockSpec((1, tt, D), lambda j, b, t: (b, t, 0)), pl.BlockSpec((1, HALO16, D), dy_next),
                  pl.BlockSpec((FFN_TC, D), lambda j, b, t: (j, 0)), wspec(FFN_CONV_K)],
        out_specs=(tile, wspec(FFN_CONV_K), wspec(1)),
        out_shape=(jax.ShapeDtypeStruct(up.shape, MXU_DT), jax.ShapeDtypeStruct((FFN_CONV_K, width), F32),
                   jax.ShapeDtypeStruct((1, width), F32)),
        compiler_params=_params(("arbitrary", "arbitrary", "arbitrary")))(
            up, up, u_conv, u_conv, dy2, dy2, w_down, cw)


QKV_W = 3 * HEADS * HD
SM_BLK = P_SM // 128


def _dn_pre_fwd(proj, conv_w, alog_row, dt_row):
    bsz, t_total, _ = proj.shape
    tt = _div_tile(t_total, ROW_TILE)
    hp = _halo_prev(tt)

    def body(x_ref, xp_ref, sm_ref, w_ref, al_ref, dt_ref, q_ref, k_ref, v_ref, g_ref):
        prev = jnp.where(pl.program_id(1) == 0, 0.0, xp_ref[0])
        y = _conv_rows(jnp.concatenate([prev, x_ref[0]], axis=0), w_ref, DN_CONV_K)[HALO:]
        q_ref[0], k_ref[0], v_ref[0] = _dn_qkv(y)
        g_ref[0] = _dn_gates(sm_ref[0], al_ref[...], dt_ref[...])

    out512 = jax.ShapeDtypeStruct((bsz, t_total, HEADS * HD), F32)
    return pl.pallas_call(
        body, name="dn_pre_fwd", grid=(bsz, t_total // tt),
        in_specs=[pl.BlockSpec((1, tt, QKV_W), lambda b, t: (b, t, 0)),
                  pl.BlockSpec((1, HALO, QKV_W), lambda b, t: (*hp(b, t), 0)),
                  pl.BlockSpec((1, tt, 128), lambda b, t: (b, t, SM_BLK)),
                  pl.BlockSpec((DN_CONV_K, QKV_W), lambda b, t: (0, 0)), _vec_spec(128), _vec_spec(128)],
        out_specs=(_tok_spec(tt, 512), _tok_spec(tt, 512), _tok_spec(tt, 512), _tok_spec(tt, 128)),
        out_shape=(out512, out512, out512, jax.ShapeDtypeStruct((bsz, t_total, 128), F32)),
        compiler_params=_params(("parallel", "parallel")))(proj, proj, proj, conv_w, alog_row, dt_row)


def _dn_pre_bwd(proj, dq, dk, dv, dgates, conv_w, alog_row, dt_row):
    bsz, t_total, _ = proj.shape
    tt = _div_tile(t_total, 256)
    nt = t_total // tt
    hp, hn = _halo_prev(tt), _halo_next(tt, t_total)

    def body(x_ref, xp_ref, xn_ref, sm_ref, dq_ref, dqn_ref, dk_ref, dkn_ref, dv_ref, dvn_ref, dg_ref,
             w_ref, al_ref, dt_ref, dx_ref, dsm_ref, dw_ref, dal_ref, ddt_ref):
        b, t = pl.program_id(0), pl.program_id(1)

        @pl.when((b == 0) & (t == 0))
        def _():
            dw_ref[...] = jnp.zeros_like(dw_ref)
            dal_ref[...] = jnp.zeros_like(dal_ref)
            ddt_ref[...] = jnp.zeros_like(ddt_ref)

        prev = jnp.where(t == 0, 0.0, xp_ref[0])
        rows = jnp.concatenate([prev, x_ref[0], xn_ref[0]], axis=0)
        y = _conv_rows(rows, w_ref, DN_CONV_K)[HALO:]
        valid = (_iota((tt + HALO, 1), 0) < tt) | (t < nt - 1)

        def ext(tile_ref, next_ref):
            return jnp.where(valid, jnp.concatenate([tile_ref[0], next_ref[0]], axis=0), 0.0)

        _, vjp_qkv = jax.vjp(_dn_qkv, y)
        (dy,) = vjp_qkv((ext(dq_ref, dqn_ref), ext(dk_ref, dkn_ref), ext(dv_ref, dvn_ref)))
        dy = jnp.where(valid, dy, 0.0)
        dx = dy * w_ref[DN_CONV_K - 1:DN_CONV_K, :]
        for s in range(1, DN_CONV_K):
            dx = dx + _shift_up(dy, s) * w_ref[DN_CONV_K - 1 - s:DN_CONV_K - s, :]
        dx_ref[0] = dx[:tt].astype(MXU_DT)
        dy_t = dy[:tt]
        for k in range(DN_CONV_K):
            s = DN_CONV_K - 1 - k
            xs = (rows if s == 0 else pltpu.roll(rows, s, 0))[HALO:HALO + tt]
            dw_ref[k:k + 1, :] += jnp.sum(dy_t * xs, axis=0, keepdims=True)
        _, vjp_g = jax.vjp(_dn_gates, sm_ref[0], al_ref[...], dt_ref[...])
        dsm, dal, ddt = vjp_g(dg_ref[0])
        dsm_ref[0] = dsm
        dal_ref[...] += dal
        ddt_ref[...] += ddt

    def tile(width, blk=0):
        return pl.BlockSpec((1, tt, width), lambda b, t: (b, t, blk))

    def halo(h, width):
        return pl.BlockSpec((1, HALO, width), lambda b, t: (*h(b, t), 0))

    return pl.pallas_call(
        body, name="dn_pre_bwd", grid=(bsz, nt),
        in_specs=[tile(QKV_W), halo(hp, QKV_W), halo(hn, QKV_W), tile(128, SM_BLK),
                  tile(512), halo(hn, 512), tile(512), halo(hn, 512), tile(512), halo(hn, 512), tile(128),
                  pl.BlockSpec((DN_CONV_K, QKV_W), lambda b, t: (0, 0)), _vec_spec(128), _vec_spec(128)],
        out_specs=(tile(QKV_W), tile(128), pl.BlockSpec((DN_CONV_K, QKV_W), lambda b, t: (0, 0)),
                   _vec_spec(128), _vec_spec(128)),
        out_shape=(jax.ShapeDtypeStruct((bsz, t_total, QKV_W), MXU_DT), jax.ShapeDtypeStruct((bsz, t_total, 128), F32),
                   jax.ShapeDtypeStruct((DN_CONV_K, QKV_W), F32), jax.ShapeDtypeStruct((1, 128), F32),
                   jax.ShapeDtypeStruct((1, 128), F32)),
        compiler_params=_params(("arbitrary", "arbitrary")))(
            proj, proj, proj, proj, dq, dq, dk, dk, dv, dv, dgates, conv_w, alog_row, dt_row)


def _state_spec(bsz, idx):
    return pl.BlockSpec((bsz, 1, HEADS, HD, HD), lambda c: (0, idx(c), 0, 0, 0))


def _inv_spec(bsz, idx):
    return pl.BlockSpec((bsz, 1, HEADS, CHUNK, CHUNK), lambda c: (0, idx(c), 0, 0, 0))


def _chunk_spec(bsz, width, idx, blk=0):
    return pl.BlockSpec((bsz, CHUNK, width), lambda c: (0, idx(c), blk))


def _dn_rec_fwd(q, k, v, gates):
    bsz, t_total, _ = q.shape
    nc = t_total // CHUNK
    fwd = lambda c: c

    def body(q_ref, k_ref, v_ref, g_ref, o_ref, ss_ref, inv_ref, s_ref):
        @pl.when(pl.program_id(0) == 0)
        def _():
            s_ref[...] = jnp.zeros_like(s_ref)

        seqs = range(bsz)
        s_list = [[s_ref[b * HEADS + h] for h in range(HEADS)] for b in seqs]
        for b in seqs:
            for h in range(HEADS):
                ss_ref[b, 0, h] = s_list[b][h]
        o, new_s, invs = _dn_chunk(s_list, [q_ref[b] for b in seqs], [k_ref[b] for b in seqs],
                                   [v_ref[b] for b in seqs], [g_ref[b] for b in seqs], with_inv=True)
        for b in seqs:
            o_ref[b] = o[b]
            for h in range(HEADS):
                s_ref[b * HEADS + h] = new_s[b][h]
                inv_ref[b, 0, h] = invs[b * HEADS + h]

    return pl.pallas_call(
        body, name="dn_rec_fwd", grid=(nc,),
        in_specs=[_chunk_spec(bsz, 512, fwd)] * 3 + [_chunk_spec(bsz, 128, fwd)],
        out_specs=(_chunk_spec(bsz, 512, fwd), _state_spec(bsz, fwd), _inv_spec(bsz, fwd)),
        out_shape=(jax.ShapeDtypeStruct(q.shape, F32), jax.ShapeDtypeStruct((bsz, nc, HEADS, HD, HD), F32),
                   jax.ShapeDtypeStruct((bsz, nc, HEADS, CHUNK, CHUNK), F32)),
        scratch_shapes=[pltpu.VMEM((bsz * HEADS, HD, HD), F32)],
        compiler_params=_params(("arbitrary",)))(q, k, v, gates)


def _dn_rec_bwd(q, k, v, gates, states, invs, do):
    bsz, t_total, _ = q.shape
    nc = t_total // CHUNK
    rev = lambda c: nc - 1 - c

    def body(q_ref, k_ref, v_ref, g_ref, ss_ref, inv_ref, do_ref, dq_ref, dk_ref, dv_ref, dg_ref, ds_ref):
        @pl.when(pl.program_id(0) == 0)
        def _():
            ds_ref[...] = jnp.zeros_like(ds_ref)

        seqs = range(bsz)
        s_list = [[ss_ref[b, 0, h] for h in range(HEADS)] for b in seqs]
        known = [inv_ref[b, 0, h] for b in seqs for h in range(HEADS)]
        _, vjp = jax.vjp(functools.partial(_dn_chunk, inv_known=known),
                         s_list, [q_ref[b] for b in seqs], [k_ref[b] for b in seqs],
                         [v_ref[b] for b in seqs], [g_ref[b] for b in seqs])
        ds_in, dq, dk, dv, dg = vjp(([do_ref[b] for b in seqs],
                                     [[ds_ref[b * HEADS + h] for h in range(HEADS)] for b in seqs]))
        for b in seqs:
            dq_ref[b], dk_ref[b], dv_ref[b], dg_ref[b] = dq[b], dk[b], dv[b], dg[b]
            for h in range(HEADS):
                ds_ref[b * HEADS + h] = ds_in[b][h]

    tok = lambda width: _chunk_spec(bsz, width, rev)
    out512 = jax.ShapeDtypeStruct(q.shape, F32)
    return pl.pallas_call(
        body, name="dn_rec_bwd", grid=(nc,),
        in_specs=[tok(512), tok(512), tok(512), tok(128), _state_spec(bsz, rev), _inv_spec(bsz, rev), tok(512)],
        out_specs=(tok(512), tok(512), tok(512), tok(128)),
        out_shape=(out512, out512, out512, jax.ShapeDtypeStruct(gates.shape, F32)),
        scratch_shapes=[pltpu.VMEM((bsz * HEADS, HD, HD), F32)],
        compiler_params=_params(("arbitrary",)))(q, k, v, gates, states, invs, do)


GQ_BLK, GK_BLK, GV_BLK = P_GQ // 512, P_GK // 512, P_GV // 512


def _gla_rec_fwd(proj, w2, bg):
    bsz, t_total, _ = proj.shape
    nc = t_total // CHUNK

    fwd = lambda c: c

    def body(q_ref, k_ref, v_ref, sm_ref, w2_ref, bg_ref, o_ref, ss_ref, s_ref):
        @pl.when(pl.program_id(0) == 0)
        def _():
            s_ref[...] = jnp.zeros_like(s_ref)

        seqs = range(bsz)
        s_list = [[s_ref[b * HEADS + h] for h in range(HEADS)] for b in seqs]
        for b in seqs:
            for h in range(HEADS):
                ss_ref[b, 0, h] = s_list[b][h]
        o, new_s = _gla_chunk(s_list, [q_ref[b] for b in seqs], [k_ref[b] for b in seqs], [v_ref[b] for b in seqs],
                              [sm_ref[b] for b in seqs], w2_ref[...], bg_ref[...])
        for b in seqs:
            o_ref[b] = o[b]
            for h in range(HEADS):
                s_ref[b * HEADS + h] = new_s[b][h]

    col = lambda blk, width=512: _chunk_spec(bsz, width, fwd, blk)
    return pl.pallas_call(
        body, name="gla_rec_fwd", grid=(nc,),
        in_specs=[col(GQ_BLK), col(GK_BLK), col(GV_BLK), col(SM_BLK, 128),
                  pl.BlockSpec((128, 512), lambda c: (0, 0)), pl.BlockSpec((1, 512), lambda c: (0, 0))],
        out_specs=(col(0), _state_spec(bsz, fwd)),
        out_shape=(jax.ShapeDtypeStruct((bsz, t_total, 512), F32),
                   jax.ShapeDtypeStruct((bsz, nc, HEADS, HD, HD), F32)),
        scratch_shapes=[pltpu.VMEM((bsz * HEADS, HD, HD), F32)],
        compiler_params=_params(("arbitrary",)))(proj, proj, proj, proj, w2, bg)


def _gla_rec_bwd(proj, w2, bg, states, do, dsm_dn):
    bsz, t_total, _ = proj.shape
    nc = t_total // CHUNK
    rev = lambda c: nc - 1 - c

    def body(q_ref, k_ref, v_ref, sm_ref, w2_ref, bg_ref, ss_ref, do_ref, dsd_ref,
             dq_ref, dk_ref, dv_ref, dsm_ref, dw2_ref, dbg_ref, ds_ref):
        @pl.when(pl.program_id(0) == 0)
        def _():
            dw2_ref[...] = jnp.zeros_like(dw2_ref)
            dbg_ref[...] = jnp.zeros_like(dbg_ref)
            ds_ref[...] = jnp.zeros_like(ds_ref)

        seqs = range(bsz)
        s_list = [[ss_ref[b, 0, h] for h in range(HEADS)] for b in seqs]
        _, vjp = jax.vjp(_gla_chunk, s_list, [q_ref[b] for b in seqs], [k_ref[b] for b in seqs],
                         [v_ref[b] for b in seqs], [sm_ref[b] for b in seqs], w2_ref[...], bg_ref[...])
        ds_in, dq, dk, dv, dsm, dw2, dbg = vjp(([do_ref[b] for b in seqs],
                                                [[ds_ref[b * HEADS + h] for h in range(HEADS)] for b in seqs]))
        for b in seqs:
            dq_ref[b], dk_ref[b], dv_ref[b] = dq[b].astype(MXU_DT), dk[b].astype(MXU_DT), dv[b].astype(MXU_DT)
            dsm_ref[b] = (dsm[b] + dsd_ref[b]).astype(MXU_DT)
            for h in range(HEADS):
                ds_ref[b * HEADS + h] = ds_in[b][h]
        dw2_ref[...] += dw2
        dbg_ref[...] += dbg

    col = lambda blk, width=512: _chunk_spec(bsz, width, rev, blk)
    w2_spec = pl.BlockSpec((128, 512), lambda c: (0, 0))
    bg_spec = pl.BlockSpec((1, 512), lambda c: (0, 0))
    out512 = jax.ShapeDtypeStruct((bsz, t_total, 512), MXU_DT)
    return pl.pallas_call(
        body, name="gla_rec_bwd", grid=(nc,),
        in_specs=[col(GQ_BLK), col(GK_BLK), col(GV_BLK), col(SM_BLK, 128), w2_spec, bg_spec,
                  _state_spec(bsz, rev), col(0), col(0, 128)],
        out_specs=(col(0), col(0), col(0), col(0, 128), w2_spec, bg_spec),
        out_shape=(out512, out512, out512, jax.ShapeDtypeStruct((bsz, t_total, 128), MXU_DT),
                   jax.ShapeDtypeStruct((128, 512), F32), jax.ShapeDtypeStruct((1, 512), F32)),
        scratch_shapes=[pltpu.VMEM((bsz * HEADS, HD, HD), F32)],
        compiler_params=_params(("arbitrary",)))(proj, proj, proj, proj, w2, bg, states, do, dsm_dn)


Z_BLK, GG_BLK = P_Z // 512, P_GG // 512


def _mix_out_fwd(o_dn, o_gla, proj, grow_dn, grow_gla):
    bsz, t_total, _ = o_dn.shape
    tt = _div_tile(t_total, ROW_TILE)

    def body(od_ref, og_ref, z_ref, gg_ref, gd_ref, gl_ref, o_ref):
        o_ref[0, :, :512] = _gate_norm(od_ref[0], z_ref[0], gd_ref[...]).astype(MXU_DT)
        o_ref[0, :, 512:] = _gate_norm(og_ref[0], gg_ref[0], gl_ref[...]).astype(MXU_DT)

    def col(blk):
        return pl.BlockSpec((1, tt, 512), lambda b, t: (b, t, blk))

    return pl.pallas_call(
        body, name="mix_out_fwd", grid=(bsz, t_total // tt),
        in_specs=[col(0), col(0), col(Z_BLK), col(GG_BLK), _vec_spec(512), _vec_spec(512)],
        out_specs=_tok_spec(tt), out_shape=jax.ShapeDtypeStruct((bsz, t_total, D), MXU_DT),
        compiler_params=_params(("parallel", "parallel")))(o_dn, o_gla, proj, proj, grow_dn, grow_gla)


def _mix_out_bwd(do, o_dn, o_gla, proj, grow_dn, grow_gla):
    bsz, t_total, _ = o_dn.shape
    tt = _div_tile(t_total, ROW_TILE)

    def body(do_ref, od_ref, og_ref, z_ref, gg_ref, gd_ref, gl_ref,
             dod_ref, dog_ref, dz_ref, dgg_ref, dgd_ref, dgl_ref):
        @pl.when((pl.program_id(0) == 0) & (pl.program_id(1) == 0))
        def _():
            dgd_ref[...] = jnp.zeros_like(dgd_ref)
            dgl_ref[...] = jnp.zeros_like(dgl_ref)

        def one(o_ref, gate_ref, g_ref, ct, do_out, dgate_out, dg_out):
            _, vjp = jax.vjp(_gate_norm, o_ref[0], gate_ref[0], g_ref[...])
            d_o, d_gate, d_row = vjp(ct)
            do_out[0] = d_o
            dgate_out[0] = d_gate.astype(MXU_DT)
            acc = d_row[:, :HD]
            for h in range(1, HEADS):
                acc = acc + d_row[:, h * HD:(h + 1) * HD]
            dg_out[...] += acc

        ct = do_ref[0].astype(F32)
        one(od_ref, z_ref, gd_ref, ct[:, :512], dod_ref, dz_ref, dgd_ref)
        one(og_ref, gg_ref, gl_ref, ct[:, 512:], dog_ref, dgg_ref, dgl_ref)

    def col(blk):
        return pl.BlockSpec((1, tt, 512), lambda b, t: (b, t, blk))

    f512 = jax.ShapeDtypeStruct((bsz, t_total, 512), F32)
    b512 = jax.ShapeDtypeStruct((bsz, t_total, 512), MXU_DT)
    g128 = jax.ShapeDtypeStruct((1, HD), F32)
    return pl.pallas_call(
        body, name="mix_out_bwd", grid=(bsz, t_total // tt),
        in_specs=[_tok_spec(tt), col(0), col(0), col(Z_BLK), col(GG_BLK), _vec_spec(512), _vec_spec(512)],
        out_specs=(col(0), col(0), col(0), col(0), _vec_spec(HD), _vec_spec(HD)),
        out_shape=(f512, f512, b512, b512, g128, g128),
        compiler_params=_params(("arbitrary", "arbitrary")))(do, o_dn, o_gla, proj, proj, grow_dn, grow_gla)


def _sum_slots(x, name):
    n, rows, cols = x.shape
    tr = _div_tile(rows, max(8, (1 << 19) // cols))

    def body(x_ref, o_ref):
        acc = x_ref[0].astype(F32)
        for i in range(1, n):
            acc = acc + x_ref[i].astype(F32)
        o_ref[...] = acc

    return pl.pallas_call(
        body, name=name, grid=(rows // tr,),
        in_specs=[pl.BlockSpec((n, tr, cols), lambda i: (0, i, 0))],
        out_specs=pl.BlockSpec((tr, cols), lambda i: (i, 0)),
        out_shape=jax.ShapeDtypeStruct((rows, cols), F32), compiler_params=_params(("parallel",)))(x)


def _adamw_math(w, g, m, v):
    nm = ADAM_B1 * m + (1.0 - ADAM_B1) * g
    nv = ADAM_B2 * v + (1.0 - ADAM_B2) * (g * g)
    m_hat = nm / (1.0 - ADAM_B1 ** ADAM_STEP)
    v_hat = nv / (1.0 - ADAM_B2 ** ADAM_STEP)
    return -ADAM_LR * (m_hat / (jnp.sqrt(v_hat) + ADAM_EPS) + ADAM_WD * w), nm, nv


def _adamw(w, g, m, v, name):
    _, rows, cols = w.shape
    tr = _div_tile(rows, max(8, (1 << 18) // cols))

    def body(w_ref, g_ref, m_ref, v_ref, d_ref, nm_ref, nv_ref):
        d_ref[...], nm_ref[...], nv_ref[...] = _adamw_math(w_ref[...], g_ref[...], m_ref[...], v_ref[...])

    spec = pl.BlockSpec((1, tr, cols), lambda i: (0, i, 0))
    shp = jax.ShapeDtypeStruct(w.shape, F32)
    return pl.pallas_call(body, name=name, grid=(rows // tr,), in_specs=[spec] * 4, out_specs=(spec,) * 3,
                          out_shape=(shp,) * 3, compiler_params=_params(("parallel",)))(w, g, m, v)


def _sum_adamw(parts, w, m, v, name):
    n, rows, cols = parts.shape
    tr = _div_tile(rows, max(8, (1 << 18) // cols))

    def body(p_ref, w_ref, m_ref, v_ref, g_ref, d_ref, nm_ref, nv_ref):
        g = p_ref[0].astype(F32)
        for i in range(1, n):
            g = g + p_ref[i].astype(F32)
        g_ref[...] = g
        d_ref[0], nm_ref[0], nv_ref[0] = _adamw_math(w_ref[0], g, m_ref[0], v_ref[0])

    spec = pl.BlockSpec((1, tr, cols), lambda i: (0, i, 0))
    shp = jax.ShapeDtypeStruct(w.shape, F32)
    return pl.pallas_call(
        body, name=name, grid=(rows // tr,),
        in_specs=[pl.BlockSpec((n, tr, cols), lambda i: (0, i, 0)), spec, spec, spec],
        out_specs=(pl.BlockSpec((tr, cols), lambda i: (i, 0)), spec, spec, spec),
        out_shape=(jax.ShapeDtypeStruct((rows, cols), F32), shp, shp, shp),
        compiler_params=_params(("parallel",)))(parts, w, m, v)


def _adamw_many(ws, gs, ms, vs, name):
    n = len(ws)

    def body(*refs):
        for i in range(n):
            d, nm, nv = _adamw_math(refs[i][...], refs[n + i][...], refs[2 * n + i][...], refs[3 * n + i][...])
            refs[4 * n + i][...] = d
            refs[5 * n + i][...] = nm
            refs[6 * n + i][...] = nv

    shapes = tuple(jax.ShapeDtypeStruct(w.shape, F32) for w in ws)
    outs = pl.pallas_call(body, name=name, out_shape=shapes * 3, compiler_params=_params())(*ws, *gs, *ms, *vs)
    return outs[:n], outs[n:2 * n], outs[2 * n:]


def _position():
    return lax.axis_index("x"), lax.axis_index("y"), lax.axis_index("c")


def _slot(px, py, pc):
    return 4 * px + 2 * py + pc


def _gather_small(x, name):
    rows, cols = x.shape

    def body(x_ref, o_ref, send_sems, recv_sems):
        mx, my, mc = _position()

        def peer(k):
            return (mx ^ ((k >> 2) & 1), my ^ ((k >> 1) & 1), mc ^ (k & 1))

        o_ref[_slot(mx, my, mc)] = x_ref[...]
        sends = []
        for k in range(1, N_DEV):
            cp = pltpu.make_async_remote_copy(src_ref=x_ref, dst_ref=o_ref.at[_slot(mx, my, mc)],
                                              send_sem=send_sems.at[k - 1], recv_sem=recv_sems.at[k - 1],
                                              device_id=peer(k), device_id_type=MESH)
            cp.start()
            sends.append(cp)
        for k in range(1, N_DEV):
            pltpu.make_async_remote_copy(src_ref=x_ref, dst_ref=o_ref.at[_slot(*peer(k))],
                                         send_sem=send_sems.at[k - 1], recv_sem=recv_sems.at[k - 1],
                                         device_id=peer(k), device_id_type=MESH).wait_recv()
        for cp in sends:
            cp.wait_send()

    return pl.pallas_call(
        body, name=name, out_shape=jax.ShapeDtypeStruct((N_DEV, rows, cols), x.dtype),
        in_specs=[pl.BlockSpec(memory_space=pltpu.VMEM)], out_specs=pl.BlockSpec(memory_space=pltpu.VMEM),
        scratch_shapes=[pltpu.SemaphoreType.DMA((N_DEV - 1,)), pltpu.SemaphoreType.DMA((N_DEV - 1,))],
        compiler_params=pltpu.CompilerParams(vmem_limit_bytes=VMEM_LIMIT_V7X))(x)


def _gather_big(shards):
    n = len(shards)

    def body(*refs):
        xs, outs = refs[:n], refs[n:2 * n]
        send_sems, recv_sems, local_sems = refs[2 * n:]
        mx, my, mc = _position()
        me, sibling = (mx, my, mc), (mx, my, 1 - mc)
        chips = [(1 - mx, my), (mx, 1 - my), (1 - mx, 1 - my)]

        def copy(a, k, block, to, src=None):
            dst = outs[a].at[_slot(*block)]
            return pltpu.make_async_remote_copy(src_ref=dst if src is None else src, dst_ref=dst,
                                                send_sem=send_sems.at[7 * a + k], recv_sem=recv_sems.at[7 * a + k],
                                                device_id=to, device_id_type=MESH)

        mine = [pltpu.make_async_copy(xs[a], outs[a].at[_slot(*me)], local_sems.at[a]) for a in range(n)]
        for cp in mine:
            cp.start()
        started = []
        for a in range(n):
            started.append(copy(a, 0, me, sibling, src=xs[a]))
            started += [copy(a, 1 + j, me, (*chip, mc), src=xs[a]) for j, chip in enumerate(chips)]
        for cp in started:
            cp.start()
        for j, chip in enumerate(chips):
            for a in range(n):
                copy(a, 1 + j, (*chip, mc), me).wait_recv()
                fwd = copy(a, 4 + j, (*chip, mc), sibling)
                fwd.start()
                started.append(fwd)
        for a in range(n):
            copy(a, 0, sibling, me).wait_recv()
            for j, chip in enumerate(chips):
                copy(a, 4 + j, (*chip, 1 - mc), me).wait_recv()
        for cp in started:
            cp.wait_send()
        for cp in mine:
            cp.wait()

    any_spec = pl.BlockSpec(memory_space=pl.ANY)
    return pl.pallas_call(
        body, name="gather_weights",
        out_shape=tuple(jax.ShapeDtypeStruct((N_DEV,) + s.shape, s.dtype) for s in shards),
        in_specs=[any_spec] * n, out_specs=(any_spec,) * n,
        scratch_shapes=[pltpu.SemaphoreType.DMA((7 * n,)), pltpu.SemaphoreType.DMA((7 * n,)),
                        pltpu.SemaphoreType.DMA((n,))])(*shards)


def _peer(pos, k):
    mx, my, mc = pos
    return (mx ^ ((k >> 2) & 1), my ^ ((k >> 1) & 1), mc ^ (k & 1))


def _exchange_copies(srcs, lands, send_sems, recv_sems, by_owner):
    pos = _position()
    me = _slot(*pos)
    out = []
    for a, (src, land) in enumerate(zip(srcs, lands)):
        for k in range(1, N_DEV):
            peer = _peer(pos, k)
            sems = dict(send_sem=send_sems.at[7 * a + k - 1], recv_sem=recv_sems.at[7 * a + k - 1],
                        device_id=peer, device_id_type=MESH)
            mine = src.at[_slot(*peer)] if by_owner else src
            send = pltpu.make_async_remote_copy(src_ref=mine, dst_ref=land.at[me], **sems)
            recv = pltpu.make_async_remote_copy(src_ref=mine, dst_ref=land.at[_slot(*peer)], **sems)
            out.append((send, recv))
    return out


_HBM_SPEC = pl.BlockSpec(memory_space=pltpu.HBM)
_SEM_SPEC = pl.BlockSpec(memory_space=pltpu.SEMAPHORE)
_DATAFLOW = pltpu.SideEffectType.DATAFLOW_SIDE_EFFECTING


def _exchange_start(name, srcs, slab_shapes, after, by_owner, carry=()):
    n, na, nc = len(srcs), len(after), len(carry)
    lands = [pltpu.with_memory_space_constraint(lax.empty((N_DEV,) + s, x.dtype), pltpu.HBM)
             for s, x in zip(slab_shapes, srcs)]
    thru = [pltpu.with_memory_space_constraint(x, pltpu.HBM) for x in [*srcs, *lands, *carry]]

    def body(*refs):
        src_refs, land_refs = refs[:n], refs[n:2 * n]
        send_sems, recv_sems = refs[len(thru) + na], refs[len(thru) + na + 1]
        token = refs[-1]
        for send, _ in _exchange_copies(src_refs, land_refs, send_sems, recv_sems, by_owner):
            send.start()
        token[...] = jnp.zeros_like(token)

    outs = pl.pallas_call(
        body, name=name,
        out_shape=(pltpu.SemaphoreType.DMA((7 * n,)), pltpu.SemaphoreType.DMA((7 * n,)),
                   *[pltpu.HBM(x.shape, x.dtype) for x in thru], jax.ShapeDtypeStruct((8, 128), F32)),
        in_specs=[_HBM_SPEC] * len(thru) + [pl.BlockSpec(memory_space=pl.ANY)] * na,
        out_specs=(_SEM_SPEC, _SEM_SPEC, *[_HBM_SPEC] * len(thru), pl.BlockSpec(memory_space=pltpu.VMEM)),
        input_output_aliases={i: 2 + i for i in range(len(thru))},
        compiler_params=pltpu.CompilerParams(has_side_effects=_DATAFLOW))(*thru, *after)
    return (outs[0], outs[1], list(outs[2:2 + n]), list(outs[2 + n:2 + 2 * n]), outs[-1],
            list(outs[2 + 2 * n:2 + 2 * n + nc]))


def _exchange_wait(name, send_sems, recv_sems, srcs, lands, after, by_owner):
    n = len(srcs)

    def body(*refs):
        src_refs, land_refs = refs[:n], refs[n:2 * n]
        s_sems, r_sems = refs[2 * n], refs[2 * n + 1]
        for send, recv in _exchange_copies(src_refs, land_refs, s_sems, r_sems, by_owner):
            send.wait_send()
            recv.wait_recv()

    outs = pl.pallas_call(
        body, name=name,
        out_shape=(*[pltpu.HBM(x.shape, x.dtype) for x in srcs], *[pltpu.HBM(l.shape, l.dtype) for l in lands]),
        in_specs=[_HBM_SPEC] * (2 * n) + [_SEM_SPEC, _SEM_SPEC, pl.BlockSpec(memory_space=pl.ANY)],
        out_specs=tuple([_HBM_SPEC] * (2 * n)),
        input_output_aliases={i: i for i in range(2 * n)},
        compiler_params=pltpu.CompilerParams(has_side_effects=_DATAFLOW))(*srcs, *lands, send_sems, recv_sems, after)
    return list(outs[:n]), list(outs[n:])


def _pad_heads(x, axis):
    shp = list(x.shape)
    x4 = x.reshape(shp[:axis] + [HEADS, GLA_KEY] + shp[axis + 1:])
    pad = [(0, 0)] * x4.ndim
    pad[axis + 1] = (0, HD - GLA_KEY)
    return jnp.pad(x4, pad).reshape(shp[:axis] + [HEADS * HD] + shp[axis + 1:])


def _unpad_heads(x, axis):
    shp = list(x.shape)
    x4 = x.reshape(shp[:axis] + [HEADS, HD] + shp[axis + 1:])
    x4 = lax.slice_in_dim(x4, 0, GLA_KEY, axis=axis + 1)
    return x4.reshape(shp[:axis] + [HEADS * GLA_KEY] + shp[axis + 1:])


O_Z_END, O_AB, O_GQ, O_GK, O_GV, O_R = 2048, 2048, 2056, 2312, 2568, 3592


def _padded_row(f):
    if f < O_Z_END:
        return f
    if f < O_GQ:
        return P_SM + (f - O_AB)
    if f < O_GV:
        base, g = (P_GQ, f - O_GQ) if f < O_GK else (P_GK, f - O_GK)
        return base + HD * (g // GLA_KEY) + g % GLA_KEY
    if f < O_R:
        return P_GV + (f - O_GV)
    return P_SM + 8 + (f - O_R)


def _runs(pairs):
    out = []
    for d, s in pairs:
        if out and out[-1][0] + out[-1][2] == d and out[-1][1] + out[-1][2] == s:
            out[-1][2] += 1
        else:
            out.append([d, s, 1])
    return out


def _pad_in_rows(shards):
    wt = shards.reshape(IN_W, D)
    return jnp.concatenate([
        wt[:O_Z_END], _pad_heads(wt[O_GQ:O_GK], 0), _pad_heads(wt[O_GK:O_GV], 0), wt[O_GV:O_R],
        wt[O_AB:O_GQ], wt[O_R:], jnp.zeros((P_W - P_SM - 8 - GATE_RANK, D), wt.dtype)], axis=0)


def _unpad_in_rows(gt):
    per = IN_W // N_DEV
    return jnp.stack([
        jnp.concatenate([gt[src:src + n] for _, src, n in
                         _runs([(f, _padded_row(f)) for f in range(j * per, (j + 1) * per)])], axis=0)
        for j in range(N_DEV)])


def _lane_row(vals, width=128):
    return jnp.pad(vals.reshape(1, -1), ((0, 0), (0, width - vals.size)))


SMALL_NAMES = ["ln0_g", "ln0_b", "b_ada", "dn_conv", "dn_a_log", "dn_dt_bias", "dn_norm_g", "gla_w_gate2",
               "gla_b_gate", "gla_norm_g", "ln1_g", "ln1_b", "ffn_conv", "ffn_conv_b", "ln2_g", "ln2_b"]
WEIGHTS = ["ln0_g", "ln0_b", "w_ada", "b_ada", "w_in", "dn_conv", "dn_a_log", "dn_dt_bias", "dn_norm_g",
           "gla_w_gate2", "gla_b_gate", "gla_norm_g", "w_o", "ln1_g", "ln1_b", "ffn_w_up", "ffn_conv", "ffn_conv_b",
           "ffn_w_down", "ln2_g", "ln2_b"]


def kernel(x, c, ln0_g, ln0_b, w_ada, b_ada, w_in, dn_conv, dn_a_log, dn_dt_bias, dn_norm_g, gla_w_gate2, gla_b_gate, gla_norm_g, w_o, ln1_g, ln1_b, ffn_w_up, ffn_conv, ffn_conv_b, ffn_w_down, ln2_g, ln2_b, loss_target, m_ln0_g, m_ln0_b, m_w_ada, m_b_ada, m_w_in, m_dn_conv, m_dn_a_log, m_dn_dt_bias, m_dn_norm_g, m_gla_w_gate2, m_gla_b_gate, m_gla_norm_g, m_w_o, m_ln1_g, m_ln1_b, m_ffn_w_up, m_ffn_conv, m_ffn_conv_b, m_ffn_w_down, m_ln2_g, m_ln2_b, v_ln0_g, v_ln0_b, v_w_ada, v_b_ada, v_w_in, v_dn_conv, v_dn_a_log, v_dn_dt_bias, v_dn_norm_g, v_gla_w_gate2, v_gla_b_gate, v_gla_norm_g, v_w_o, v_ln1_g, v_ln1_b, v_ffn_w_up, v_ffn_conv, v_ffn_conv_b, v_ffn_w_down, v_ln2_g, v_ln2_b):
    args = dict(locals())
    w_given = {n: args[n] for n in WEIGHTS}
    m_given = {n: args["m_" + n] for n in WEIGHTS}
    v_given = {n: args["v_" + n] for n in WEIGHTS}
    bsz, t_total, _ = x.shape
    ntok = bsz * t_total
    mx, my, mc = _position()
    me = _slot(mx, my, mc)

    pack1 = jnp.concatenate([c.reshape(-1), dn_conv.reshape(-1), gla_w_gate2.reshape(-1), ffn_conv.reshape(-1)])
    n1 = pack1.size
    rows1 = -(-n1 // 1024) * 8
    pack1 = jnp.pad(pack1, (0, rows1 * 128 - n1)).reshape(rows1, 128)
    got1 = _gather_small(pack1, "gather_cond").reshape(N_DEV, -1)
    o1 = bsz * D
    o2 = o1 + dn_conv.size
    o3 = o2 + gla_w_gate2.size
    c_all = got1[:, :o1].reshape(N_DEV * bsz, D)
    dn_conv_f = got1[:, o1:o2].reshape(N_DEV, DN_CONV_K, -1).transpose(1, 0, 2).reshape(DN_CONV_K, QKV_W)
    gate2_f = got1[:, o2:o3].reshape(N_DEV, GATE_RANK, -1).transpose(1, 0, 2).reshape(GATE_RANK, HEADS * GLA_KEY)
    ffn_conv_f = got1[:, o3:n1].reshape(N_DEV, FFN_CONV_K, -1).transpose(1, 0, 2).reshape(FFN_CONV_K, 2 * D_FF)

    win_t = w_in[0].T.astype(MXU_DT)
    wup_t = ffn_w_up[0].T.astype(MXU_DT)
    (win_all,) = _gather_big([win_t])
    win_p = _pad_in_rows(win_all)
    cw_p, cb_p = _ffn_pair(ffn_conv_f, 1), _ffn_pair(ffn_conv_b, 1)

    ncol = w_ada.shape[2]
    b_cols = lax.dynamic_slice_in_dim(b_ada, me * ncol, ncol, axis=1)
    mod_part = _ada_fwd(c_all, w_ada[0], b_cols)
    mod_all = _gather_small(mod_part.reshape(-1, 128), "gather_mod").reshape(N_DEV, N_DEV * bsz, ncol)
    mod = lax.dynamic_slice_in_dim(mod_all, me * bsz, bsz, axis=1).transpose(1, 0, 2).reshape(bsz, 6, 1, D)
    late = [w_o[0].astype(MXU_DT), wup_t, ffn_w_down[0].astype(MXU_DT)]
    ag_send, ag_recv, ag_src, ag_land, ag_token, _ = _exchange_start(
        "gather_start", late, [w.shape for w in late], [win_all, mod_all], by_owner=False)
    mod = mod + ag_token[0, 0]
    sh_a, sc_a, gt_a, sh_f, sc_f, gt_f = (mod[:, i] for i in range(6))

    g0, b0 = ln0_g.reshape(1, D), ln0_b.reshape(1, D)
    alog_row, dt_row = _lane_row(dn_a_log[0]), _lane_row(dn_dt_bias[0])
    grow_dn, grow_gla = jnp.tile(dn_norm_g, (1, HEADS)), jnp.tile(gla_norm_g, (1, HEADS))
    w2 = jnp.zeros((128, HEADS * HD), F32).at[SM_R:SM_R + GATE_RANK].set(_pad_heads(gate2_f, 1))
    bg = _pad_heads(gla_b_gate, 1)

    h_a = _ln0_mod(x, g0, b0, sc_a, sh_a)
    proj = _mm(h_a.reshape(ntok, D), win_p, "nt", F32, "mm_proj", tm=1024, tn=1408).reshape(bsz, t_total, P_W)
    q, k, v, gates = _dn_pre_fwd(proj, dn_conv_f, alog_row, dt_row)
    o_dn, s_dn, inv_dn = _dn_rec_fwd(q, k, v, gates)
    o_gla, s_gla = _gla_rec_fwd(proj, w2, bg)
    o_mix = _mix_out_fwd(o_dn, o_gla, proj, grow_dn, grow_gla)
    late, landed = _exchange_wait("gather_wait", ag_send, ag_recv, ag_src, ag_land, o_mix, by_owner=False)
    wo_all, wup_all, wdn_all = (lax.dynamic_update_slice(l, w[None], (me, 0, 0)) for l, w in zip(landed, late))
    wo_f = wo_all.reshape(D, D)
    wup_f = _ffn_pair(wup_all.reshape(2 * D_FF, D), 0)
    wdn_f = wdn_all.reshape(D_FF, D)
    y = _mm(o_mix.reshape(ntok, D), wo_f, "nn", MXU_DT, "mm_wo", tm=1024, tn=1024).reshape(bsz, t_total, D)
    r1, h_f = _res_ln_mod(x, y, gt_a, g0, b0, ln1_g, ln1_b, sc_f, sh_f)
    up, u_conv, act = _ffn_up_act(h_f, wup_f, cw_p, cb_p)
    y2 = _mm(act.reshape(ntok, D_FF), wdn_f, "nn", MXU_DT, "mm_down", tm=1024, tn=1024).reshape(bsz, t_total, D)
    loss_rows, dr2, dy2, dgt_f, d_ln2_g, d_ln2_b = _final_fwd_bwd(r1, y2, gt_f, ln1_g, ln1_b, ln2_g, ln2_b, loss_target)
    loss_part = (0.5 / D) * jnp.sum(loss_rows)

    dy2_2 = dy2.reshape(ntok, D)
    g_wdn = _mm(act.reshape(ntok, D_FF), dy2_2, "tn", MXU_DT, "mm_gwdn", tm=1408, tn=1024)
    dup, d_cw_p, d_cb_p = _ffn_act_bwd(up, u_conv, dy2, wdn_f, cw_p)
    d_ffn_conv, d_ffn_conv_b = _ffn_unpair(d_cw_p, 1), _ffn_unpair(d_cb_p, 1)
    dup_2 = dup.reshape(ntok, 2 * D_FF)
    dh_f = _mm(dup_2, wup_f, "nn", MXU_DT, "mm_dhf", tn=1024).reshape(bsz, t_total, D)
    g_wup_t = _mm(dup_2, h_f.reshape(ntok, D), "tn", MXU_DT, "mm_gwup", tm=1408, tn=1024)
    ffn_parts = [_ffn_unpair(g_wup_t, 0).reshape(N_DEV, -1, D), g_wdn.reshape(N_DEV, -1, D)]
    rs_send, rs_recv, rs_src, rs_land, rs_token, _ = _exchange_start(
        "scatter_start", ffn_parts, [p.shape[1:] for p in ffn_parts], [dh_f], by_owner=True)
    dr1, dsc_f, dsh_f, d_ln1_g, d_ln1_b, dy, dgt_a = _ln_bwd_call(
        "ln1_bwd", dr2, dh_f, r1, ln1_g, ln1_b, sc_f + rs_token[0, 0], y=y, gt=gt_a)

    dy_2 = dy.reshape(ntok, D)
    do = _mm(dy_2, wo_f, "nt", MXU_DT, "mm_do", tm=1024, tn=1024).reshape(bsz, t_total, D)
    g_wo = _mm(o_mix.reshape(ntok, D), dy_2, "tn", MXU_DT, "mm_gwo", tm=512, tn=1024)
    do_dn, do_gla, dz, dgg, d_dn_norm, d_gla_norm = _mix_out_bwd(do, o_dn, o_gla, proj, grow_dn, grow_gla)
    dq, dk, dv, dgates = _dn_rec_bwd(q, k, v, gates, s_dn, inv_dn, do_dn)
    dqkv, dsm_dn, d_dn_conv, d_alog_row, d_dt_row = _dn_pre_bwd(proj, dq, dk, dv, dgates, dn_conv_f, alog_row, dt_row)
    dgq, dgk, dgv, dsm, d_w2, d_bg = _gla_rec_bwd(proj, w2, bg, s_gla, do_gla, dsm_dn)
    dproj = jnp.concatenate([dqkv, dz, dgq, dgk, dgv, dgg, dsm], axis=-1).reshape(ntok, P_W)
    g_win_p = _mm(dproj, h_a.reshape(ntok, D), "tn", MXU_DT, "mm_gwin", tm=1408, tn=1024)
    mix_parts = [_unpad_in_rows(g_win_p), g_wo.reshape(N_DEV, -1, D)]
    rs2_send, rs2_recv, rs2_src, rs2_land, rs2_token, (win_p_late,) = _exchange_start(
        "scatter_mix_start", mix_parts, [p.shape[1:] for p in mix_parts], [], by_owner=True, carry=[win_p])
    dh_a = _mm(dproj, win_p_late, "nn", MXU_DT, "mm_dha", tn=1024).reshape(bsz, t_total, D)
    grad_x, dsc_a, dsh_a, d_ln0_g, d_ln0_b = _ln_bwd_call(
        "ln0_bwd", dr1, dh_a, x, g0, b0, sc_a + rs2_token[0, 0])

    delta, new_m, new_v, big_grads = {}, {}, {}, {}
    flip = lambda a: jnp.swapaxes(a, 1, 2)

    def update_owned(n, landed, mine):
        parts = lax.dynamic_update_slice(landed, lax.dynamic_slice_in_dim(mine, me, 1, axis=0), (me, 0, 0))
        turn = flip if parts.shape[1:] != w_given[n].shape[1:] else (lambda a: a)
        g, d_, m_, v_ = _sum_adamw(parts, turn(w_given[n]), turn(m_given[n]), turn(v_given[n]), "adamw_" + n)
        big_grads[n], delta[n], new_m[n], new_v[n] = turn(g[None]), turn(d_), turn(m_), turn(v_)

    ffn_parts, ffn_landed = _exchange_wait("scatter_wait", rs_send, rs_recv, rs_src, rs_land, grad_x, by_owner=True)
    update_owned("ffn_w_up", ffn_landed[0], ffn_parts[0])
    update_owned("ffn_w_down", ffn_landed[1], ffn_parts[1])
    ffn_done = 0.0 * (new_v["ffn_w_up"][0, 0, 0] + new_v["ffn_w_down"][0, 0, 0])

    dmod = jnp.concatenate([dsh_a, dsc_a, dgt_a, dsh_f, dsc_f, dgt_f], axis=1).reshape(-1)
    small_parts = {
        "ln0_g": d_ln0_g, "ln0_b": d_ln0_b, "ln1_g": d_ln1_g, "ln1_b": d_ln1_b, "ln2_g": d_ln2_g, "ln2_b": d_ln2_b,
        "dn_a_log": d_alog_row[:, :HEADS], "dn_dt_bias": d_dt_row[:, :HEADS],
        "dn_norm_g": d_dn_norm, "gla_norm_g": d_gla_norm, "gla_b_gate": _unpad_heads(d_bg, 1),
        "ffn_conv_b": d_ffn_conv_b, "dn_conv": d_dn_conv,
        "gla_w_gate2": _unpad_heads(d_w2[SM_R:SM_R + GATE_RANK], 1), "ffn_conv": d_ffn_conv}
    order = sorted(small_parts)
    flat = jnp.concatenate([small_parts[n].reshape(-1) for n in order] + [(loss_part + ffn_done).reshape(1), dmod])
    n3 = flat.size
    rows3 = -(-n3 // 1024) * 8
    pack3 = jnp.pad(flat, (0, rows3 * 128 - n3)).reshape(rows3, 128)
    got3 = _gather_small(pack3, "gather_small_grads")
    tot3 = _sum_slots(got3, "sum_small_grads").reshape(-1)
    grads = {}
    off = 0
    for n in order:
        size = small_parts[n].size
        grads[n] = tot3[off:off + size]
        off += size
    loss = tot3[off]
    off += 1
    dmod_all = got3.reshape(N_DEV, -1)[:, off:off + dmod.size].reshape(N_DEV * bsz, 6 * D)
    dmod_cols = lax.dynamic_slice_in_dim(dmod_all, me * ncol, ncol, axis=1)
    g_wada, g_bada = _ada_bwd(c_all, dmod_all, dmod_cols)
    grads["b_ada"] = g_bada

    def col_shard(full, rows):
        part = full.reshape(rows, -1)
        width = part.shape[1] // N_DEV
        return lax.dynamic_slice_in_dim(part, me * width, width, axis=1)

    grads["dn_conv"] = col_shard(grads["dn_conv"], DN_CONV_K)
    grads["gla_w_gate2"] = col_shard(grads["gla_w_gate2"], GATE_RANK)
    grads["ffn_conv"] = col_shard(grads["ffn_conv"], FFN_CONV_K)
    grads = {n: g.reshape(w_given[n].shape) for n, g in grads.items()}
    mix_parts, mix_landed = _exchange_wait("scatter_mix_wait", rs2_send, rs2_recv, rs2_src, rs2_land, grad_x,
                                           by_owner=True)
    update_owned("w_in", mix_landed[0], mix_parts[0])
    update_owned("w_o", mix_landed[1], mix_parts[1])
    grads["w_ada"] = g_wada.reshape(w_ada.shape)
    delta["w_ada"], new_m["w_ada"], new_v["w_ada"] = _adamw(w_ada, grads["w_ada"], m_w_ada, v_w_ada, "adamw_w_ada")
    grads.update(big_grads)
    d_s, m_s, v_s = _adamw_many(*[[src[n] for n in SMALL_NAMES] for src in (w_given, grads, m_given, v_given)],
                                "adamw_small")
    for i, n in enumerate(SMALL_NAMES):
        delta[n], new_m[n], new_v[n] = d_s[i], m_s[i], v_s[i]

    return (loss, grad_x, *[grads[n] for n in WEIGHTS], *[delta[n] for n in WEIGHTS],
            *[new_m[n] for n in WEIGHTS], *[new_v[n] for n in WEIGHTS])
```

```python
import functools

import jax
import jax.numpy as jnp
from jax import lax
from jax.experimental import pallas as pl
from jax.experimental.pallas import tpu as pltpu

F32 = jnp.float32
MXU_DT = jnp.bfloat16
MESH = pl.DeviceIdType.MESH
N_DEV = 8

D = 1024
HEADS = 4
HD = 128
CHUNK = 64
GLA_KEY = 64
GLA_TAU = 16.0
GATE_RANK = 16
D_FF = 2816
IN_W = 3608
ALPHA = 2.0 ** 0.25
EPS = 1e-6
DN_CONV_K = 4
FFN_CONV_K = 3
HALO = 8
ROW_TILE = 1024
FFN_ROW_TILE = 1024

P_QKV, P_Z, P_GQ, P_GK, P_GV, P_GG, P_SM, P_W = 0, 1536, 2048, 2560, 3072, 3584, 4096, 4224
SM_A, SM_B, SM_R = 0, 4, 8

ADAM_LR, ADAM_B1, ADAM_B2, ADAM_EPS, ADAM_WD, ADAM_STEP = 0.001, 0.9, 0.999, 1e-08, 0.01, 10

VMEM_LIMIT_V7X = 56 * 1024 * 1024


def _params(sem=None):
    return pltpu.CompilerParams(dimension_semantics=sem, vmem_limit_bytes=VMEM_LIMIT_V7X)


def _dg(a, b, dims, prec=None):
    return lax.dot_general(a, b, (dims, ((), ())), precision=prec, preferred_element_type=F32)


def _dot(a, b, prec=None):
    return _dg(a, b, ((1,), (0,)), prec)


def _dot_nt(a, b, prec=None):
    return _dg(a, b, ((1,), (1,)), prec)


def _dot_tn(a, b, prec=None):
    return _dg(a, b, ((0,), (0,)), prec)


def _iota(shape, dim):
    return lax.broadcasted_iota(jnp.int32, shape, dim)


def _sigmoid(x):
    return jax.nn.sigmoid(x)


def _silu(x):
    return x * _sigmoid(x)


def _softplus(x):
    return jnp.maximum(x, 0.0) + jnp.log(1.0 + jnp.exp(-jnp.abs(x)))


def _ln_stats(x):
    mu = jnp.mean(x, axis=-1, keepdims=True)
    xc = x - mu
    rstd = lax.rsqrt(jnp.mean(xc * xc, axis=-1, keepdims=True) + EPS)
    return xc * rstd, rstd


def _ln_bwd(dxhat, xhat, rstd):
    return rstd * (dxhat - jnp.mean(dxhat, axis=-1, keepdims=True)
                   - xhat * jnp.mean(dxhat * xhat, axis=-1, keepdims=True))


NN, NT, TN = ((1,), (0,)), ((1,), (1,)), ((0,), (0,))


def _split2(a):
    hi = a.astype(jnp.bfloat16)
    return hi, (a - hi.astype(F32)).astype(jnp.bfloat16)


def _d3(a, b, dims):
    ah, al = _split2(a)
    bh, bl = _split2(b)
    return _dg(ah, bh, dims) + (_dg(ah, bl, dims) + _dg(al, bh, dims))


@jax.custom_vjp
def _dot3(a, b):
    return _d3(a, b, NN)


_dot3.defvjp(lambda a, b: (_d3(a, b, NN), (a, b)),
             lambda res, g: (_d3(g, res[1], NT), _d3(res[0], g, TN)))


def _split3(b):
    b1 = b.astype(jnp.bfloat16)
    r1 = b - b1.astype(F32)
    b2 = r1.astype(jnp.bfloat16)
    return b1, b2, (r1 - b2.astype(F32)).astype(jnp.bfloat16)


def _sum3(fn, b):
    b1, b2, b3 = _split3(b)
    return fn(b1) + (fn(b2) + fn(b3))


@jax.custom_vjp
def _mask_dot(e, b):
    return _sum3(lambda t: _dg(e, t, NN), b)


_mask_dot.defvjp(lambda e, b: (_mask_dot(e, b), e),
                 lambda e, g: (jnp.zeros_like(e), _sum3(lambda t: _dg(e, t, TN), g)))


@jax.custom_vjp
def _mask_dot_nt(e, b):
    return _sum3(lambda t: _dg(e, t, NT), b)


_mask_dot_nt.defvjp(lambda e, b: (_mask_dot_nt(e, b), e),
                    lambda e, g: (jnp.zeros_like(e), _sum3(lambda t: _dg(t, e, TN), g)))


def _interleave(gens, shares):
    results = [None] * len(gens)
    live = list(range(len(gens)))
    while live:
        for i in list(live):
            for _ in range(shares[i]):
                try:
                    next(gens[i])
                except StopIteration as done:
                    results[i] = done.value
                    live.remove(i)
                    break
    return results


def _tri_inv_stages(ms):
    n = ms[0].shape[0]
    r, c = _iota((n, n), 0), _iota((n, n), 1)
    eye = (r == c).astype(F32)
    diag = (r >> 3) == (c >> 3)
    ds = [jnp.where(diag, m, 0.0) for m in ms]
    d2s = [_d3(d, d, NN) for d in ds]
    yield
    d4s = [_d3(d2, d2, NN) for d2 in d2s]
    invs = [_d3(eye - d, eye + d2, NN) for d, d2 in zip(ds, d2s)]
    yield
    invs = [_d3(inv, eye + d4, NN) for inv, d4 in zip(invs, d4s)]
    yield
    shift = 3
    while (1 << shift) < n:
        rb, cb = r >> shift, c >> shift
        sel = ((rb & 1) == 1) & (cb == rb - 1)
        tmp = [_d3(inv, jnp.where(sel, m, 0.0), NN) for inv, m in zip(invs, ms)]
        yield
        invs = [inv - _d3(t, inv, NN) for t, inv in zip(tmp, invs)]
        yield
        shift += 1
    return invs


def _tri_inv_bwd(invs, das):
    tmp = [_d3(a, da, TN) for a, da in zip(invs, das)]
    return ([-_d3(t, a, NT) for t, a in zip(tmp, invs)],)


@jax.custom_vjp
def _tri_inv_known(ms, invs):
    return invs


_tri_inv_known.defvjp(lambda ms, invs: (invs, invs),
                      lambda invs, das: (_tri_inv_bwd(invs, das)[0], [jnp.zeros_like(a) for a in invs]))


def _dn_chunk(s_list, q, k, v, gates, inv_known=None):
    nb = len(q)
    c = q[0].shape[0]
    r64, c64 = _iota((c, c), 0), _iota((c, c), 1)
    causal = r64 >= c64
    strict = r64 > c64
    tri = causal.astype(jnp.bfloat16)
    eye = (_iota((HD, HD), 0) == _iota((HD, HD), 1)).astype(jnp.bfloat16)
    lane = _iota(gates[0].shape, 1)
    lane1 = _iota((1, HD), 1)
    g_all = [_mask_dot(tri, g) for g in gates]
    yield
    g_all_t = [_mask_dot_nt(eye, g) for g in g_all]
    yield
    row = _iota(g_all_t[0].shape, 0)
    last = [jnp.sum(g, axis=0, keepdims=True) for g in gates]
    prob = [(b, h) for b in range(nb) for h in range(HEADS)]
    sl = [slice(h * HD, (h + 1) * HD) for h in range(HEADS)]
    qh = [q[b][:, sl[h]] for b, h in prob]
    kh = [k[b][:, sl[h]] for b, h in prob]
    vh = [v[b][:, sl[h]] for b, h in prob]
    s = [s_list[b][h] for b, h in prob]
    beta = [jnp.sum(jnp.where(lane == SM_B + h, gates[b], 0.0), axis=-1, keepdims=True) for b, h in prob]
    g_c = [jnp.sum(jnp.where(lane == SM_A + h, g_all[b], 0.0), axis=-1, keepdims=True) for b, h in prob]
    g_r = [jnp.sum(jnp.where(row == SM_A + h, g_all_t[b], 0.0), axis=0, keepdims=True) for b, h in prob]
    g_last = [jnp.sum(jnp.where(lane1 == SM_A + h, last[b], 0.0), axis=-1, keepdims=True) for b, h in prob]
    decay = [jnp.where(causal, jnp.exp(jnp.where(causal, gc - gr, 0.0)), 0.0) for gc, gr in zip(g_c, g_r)]
    kb = [k_ * b_ for k_, b_ in zip(kh, beta)]
    m_low = [jnp.where(strict, _dot_nt(kb_, k_) * d_, 0.0) for kb_, k_, d_ in zip(kb, kh, decay)]
    yield
    attn = [_dot_nt(q_, k_) * d_ for q_, k_, d_ in zip(qh, kh, decay)]
    yield
    if inv_known is None:
        a_inv = yield from _tri_inv_stages(m_low)
    else:
        a_inv = _tri_inv_known(m_low, inv_known)
    eg = [jnp.exp(gc) for gc in g_c]
    uw = [_dot3(a_, jnp.concatenate([v_ * b_, kb_ * e_], axis=1))
          for a_, v_, b_, kb_, e_ in zip(a_inv, vh, beta, kb, eg)]
    yield
    v_new = [uw_[:, :HD] - _dot(uw_[:, HD:], s_) for uw_, s_ in zip(uw, s)]
    yield
    qs = [_dot(q_ * e_, s_) for q_, e_, s_ in zip(qh, eg, s)]
    yield
    o = [qs_ + _dot(a_, vn_) for qs_, a_, vn_ in zip(qs, attn, v_new)]
    yield
    k_dec = [k_ * jnp.exp(gl - gc) for k_, gl, gc in zip(kh, g_last, g_c)]
    s_new = [s_ * jnp.exp(gl) + _dot_tn(kd_, vn_) for s_, gl, kd_, vn_ in zip(s, g_last, k_dec, v_new)]
    outs = [jnp.concatenate(o[b * HEADS:(b + 1) * HEADS], axis=-1) for b in range(nb)]
    states = [s_new[b * HEADS:(b + 1) * HEADS] for b in range(nb)]
    return outs, states, a_inv


def _gla_chunk(st_list, q, k, v, small, w2, bg):
    nb = len(q)
    c = q[0].shape[0]
    causal = _iota((c, c), 0) >= _iota((c, c), 1)
    tri = causal.astype(jnp.bfloat16)
    la_all = [-_softplus(-(_dot(sm, w2) + bg)) * (1.0 / GLA_TAU) for sm in small]
    yield
    b_all = [_mask_dot(tri, la) for la in la_all]
    yield
    prob = [(b, h) for b in range(nb) for h in range(HEADS)]
    sl = [slice(h * HD, (h + 1) * HD) for h in range(HEADS)]
    kh = [k[b][:, sl[h]] for b, h in prob]
    vh = [v[b][:, sl[h]] for b, h in prob]
    st = [st_list[b][h] for b, h in prob]
    bc = [b_all[b][:, sl[h]] for b, h in prob]
    b_last = [jnp.sum(la_all[b][:, sl[h]], axis=0, keepdims=True) for b, h in prob]
    q_dec = [q[b][:, sl[h]] * (GLA_KEY ** -0.5) * jnp.exp(bc_) for (b, h), bc_ in zip(prob, bc)]
    attn = [jnp.where(causal, _dot_nt(qd, k_ * jnp.exp(-bc_)), 0.0) for qd, k_, bc_ in zip(q_dec, kh, bc)]
    yield
    inter = [_dot_nt(qd, st_) for qd, st_ in zip(q_dec, st)]
    yield
    o = [i_ + _dot(a_, v_) for i_, a_, v_ in zip(inter, attn, vh)]
    yield
    k_dec = [k_ * jnp.exp(bl - bc_) for k_, bl, bc_ in zip(kh, b_last, bc)]
    s_new = [st_ * jnp.exp(bl) + _dot_tn(v_, kd) for st_, bl, v_, kd in zip(st, b_last, vh, k_dec)]
    outs = [jnp.concatenate(o[b * HEADS:(b + 1) * HEADS], axis=-1) for b in range(nb)]
    return outs, [s_new[b * HEADS:(b + 1) * HEADS] for b in range(nb)]


def _dn_qkv(y):
    act = _silu(y)
    parts = []
    for i in range(2 * HEADS):
        xh = act[:, i * HD:(i + 1) * HD]
        xh = xh * lax.rsqrt(jnp.sum(xh * xh, axis=-1, keepdims=True) + EPS)
        parts.append(xh * (HD ** -0.5) if i < HEADS else xh)
    qk = jnp.concatenate(parts, axis=-1)
    return qk[:, :HEADS * HD], qk[:, HEADS * HD:], act[:, 2 * HEADS * HD:]


def _dn_gates(small, alog_row, dt_row):
    lane = _iota(small.shape, 1)
    log_a = -jnp.exp(alog_row) * _softplus(small + dt_row)
    return jnp.where(lane < SM_B, log_a, jnp.where(lane < SM_R, _sigmoid(small), 0.0))


def _gate_norm(o, z, grow):
    parts = []
    for h in range(HEADS):
        oh = o[:, h * HD:(h + 1) * HD]
        parts.append(oh * lax.rsqrt(jnp.mean(oh * oh, axis=-1, keepdims=True) + EPS))
    return jnp.concatenate(parts, axis=-1) * grow * _silu(z)


def _conv_rows(xrows, w_ref, k_taps):
    n = xrows.shape[0]
    acc = xrows * w_ref[k_taps - 1:k_taps, :]
    for s in range(1, k_taps):
        acc = acc + pltpu.roll(xrows, s, 0) * w_ref[k_taps - 1 - s:k_taps - s, :]
    return acc


def _shift_up(x, s):
    return x if s == 0 else pltpu.roll(x, x.shape[0] - s, 0)


def _div_tile(n, cap, mult=8):
    best = None
    for t in range(mult, min(n, cap) + 1, mult):
        if n % t == 0:
            best = t
    return best if best is not None else n


def _halo_prev(tt):
    return lambda b, t: (b, jnp.maximum(t * (tt // HALO) - 1, 0))


def _halo_next(tt, t_total):
    return lambda b, t: (b, jnp.minimum((t + 1) * (tt // HALO), t_total // HALO - 1))


def _mm(a, b, mode, out_dtype, name, tm=512, tn=512, tk=None):
    if mode == "nn":
        (m, k), n = a.shape, b.shape[1]
    elif mode == "nt":
        (m, k), n = a.shape, b.shape[0]
    else:
        (k, m), n = a.shape, b.shape[1]
    tm, tn = min(tm, m), min(tn, n)
    tk = k if tk is None else min(tk, k)
    assert m % tm == 0 and n % tn == 0 and k % tk == 0, (name, a.shape, b.shape, tm, tn, tk)
    nk = k // tk
    if mode == "tn":
        a_spec = pl.BlockSpec((tk, tm), lambda i, j, kk: (kk, i))
    else:
        a_spec = pl.BlockSpec((tm, tk), lambda i, j, kk: (i, kk))
    if mode == "nt":
        b_spec = pl.BlockSpec((tn, tk), lambda i, j, kk: (j, kk))
    else:
        b_spec = pl.BlockSpec((tk, tn), lambda i, j, kk: (kk, j))
    dims = {"nn": ((1,), (0,)), "nt": ((1,), (1,)), "tn": ((0,), (0,))}[mode]

    def body(a_ref, b_ref, o_ref, *acc):
        p = _dg(a_ref[...], b_ref[...], dims)
        if nk == 1:
            o_ref[...] = p.astype(out_dtype)
        else:
            kk = pl.program_id(2)

            @pl.when(kk == 0)
            def _():
                acc[0][...] = p

            @pl.when(kk > 0)
            def _():
                acc[0][...] += p

            @pl.when(kk == nk - 1)
            def _():
                o_ref[...] = acc[0][...].astype(out_dtype)

    return pl.pallas_call(
        body, name=name, grid=(m // tm, n // tn, nk),
        in_specs=[a_spec, b_spec],
        out_specs=pl.BlockSpec((tm, tn), lambda i, j, kk: (i, j)),
        out_shape=jax.ShapeDtypeStruct((m, n), out_dtype),
        scratch_shapes=[pltpu.VMEM((tm, tn), F32)] if nk > 1 else [],
        compiler_params=_params(("parallel", "parallel", "arbitrary")),
    )(a, b)


def _ada_fwd(c_all, w_ada, b_cols):
    def body(c_ref, w_ref, b_ref, o_ref):
        cond = _silu(c_ref[...]).astype(MXU_DT)
        o_ref[...] = _dot(cond, w_ref[...].astype(MXU_DT)) + b_ref[...]

    return pl.pallas_call(body, name="ada_fwd", out_shape=jax.ShapeDtypeStruct((c_all.shape[0], w_ada.shape[1]), F32),
                          compiler_params=_params())(c_all, w_ada, b_cols)


def _ada_bwd(c_all, dmod_all, dmod_cols):
    def body(c_ref, da_ref, dc_ref, gw_ref, gb_ref):
        cond = _silu(c_ref[...]).astype(MXU_DT)
        gw_ref[...] = _dot_tn(cond, dc_ref[...].astype(MXU_DT))
        gb_ref[...] = jnp.sum(da_ref[...], axis=0, keepdims=True)

    return pl.pallas_call(
        body, name="ada_bwd",
        out_shape=(jax.ShapeDtypeStruct((c_all.shape[1], dmod_cols.shape[1]), F32),
                   jax.ShapeDtypeStruct((1, dmod_all.shape[1]), F32)),
        compiler_params=_params())(c_all, dmod_all, dmod_cols)


def _tok_spec(tt, width=D):
    return pl.BlockSpec((1, tt, width), lambda b, t: (b, t, 0))


def _vec_spec(width=D):
    return pl.BlockSpec((1, width), lambda b, t: (0, 0))


def _bvec_spec(width=D):
    return pl.BlockSpec((1, 1, width), lambda b, t: (b, 0, 0))


def _ln0_mod(x, g0, b0, sc, sh):
    bsz, t_total, _ = x.shape
    tt = _div_tile(t_total, ROW_TILE)

    def body(x_ref, g_ref, b_ref, sc_ref, sh_ref, h_ref):
        xh, _ = _ln_stats(x_ref[0])
        x0 = xh * g_ref[...] + b_ref[...]
        h_ref[0] = (x0 * (1.0 + sc_ref[0]) + sh_ref[0]).astype(MXU_DT)

    return pl.pallas_call(
        body, name="ln0_mod", grid=(bsz, t_total // tt),
        in_specs=[_tok_spec(tt), _vec_spec(), _vec_spec(), _bvec_spec(), _bvec_spec()],
        out_specs=_tok_spec(tt), out_shape=jax.ShapeDtypeStruct(x.shape, MXU_DT),
        compiler_params=_params(("parallel", "parallel")))(x, g0, b0, sc, sh)


def _res_ln_mod(x, y, gt, g0, b0, g1, b1, sc, sh):
    bsz, t_total, _ = x.shape
    tt = _div_tile(t_total, ROW_TILE)

    def body(x_ref, y_ref, gt_ref, g0_ref, b0_ref, g1_ref, b1_ref, sc_ref, sh_ref, r_ref, h_ref):
        xh, _ = _ln_stats(x_ref[0])
        r = ALPHA * (xh * g0_ref[...] + b0_ref[...]) + (1.0 + gt_ref[0]) * y_ref[0].astype(F32)
        r_ref[0] = r
        rh, _ = _ln_stats(r)
        x1 = rh * g1_ref[...] + b1_ref[...]
        h_ref[0] = (x1 * (1.0 + sc_ref[0]) + sh_ref[0]).astype(MXU_DT)

    return pl.pallas_call(
        body, name="res_ln_mod", grid=(bsz, t_total // tt),
        in_specs=[_tok_spec(tt), _tok_spec(tt), _bvec_spec(), _vec_spec(), _vec_spec(), _vec_spec(), _vec_spec(),
                  _bvec_spec(), _bvec_spec()],
        out_specs=(_tok_spec(tt), _tok_spec(tt)),
        out_shape=(jax.ShapeDtypeStruct(x.shape, F32), jax.ShapeDtypeStruct(x.shape, MXU_DT)),
        compiler_params=_params(("parallel", "parallel")))(x, y, gt, g0, b0, g1, b1, sc, sh)


def _final_fwd_bwd(r1, y2, gt, g1, b1, g2, b2, target):
    bsz, t_total, _ = r1.shape
    tt = _div_tile(t_total, ROW_TILE)

    def body(r1_ref, y2_ref, gt_ref, g1_ref, b1_ref, g2_ref, b2_ref, tg_ref,
             loss_ref, dr2_ref, dy2_ref, dgt_ref, dg2_ref, db2_ref):
        b, t = pl.program_id(0), pl.program_id(1)

        @pl.when((b == 0) & (t == 0))
        def _():
            loss_ref[...] = jnp.zeros_like(loss_ref)
            dg2_ref[...] = jnp.zeros_like(dg2_ref)
            db2_ref[...] = jnp.zeros_like(db2_ref)

        @pl.when(t == 0)
        def _():
            dgt_ref[...] = jnp.zeros_like(dgt_ref)

        rh1, _ = _ln_stats(r1_ref[0])
        x1 = rh1 * g1_ref[...] + b1_ref[...]
        y2 = y2_ref[0].astype(F32)
        gate = 1.0 + gt_ref[0]
        xh2, rstd2 = _ln_stats(ALPHA * x1 + gate * y2)
        err = xh2 * g2_ref[...] + b2_ref[...] - tg_ref[0]
        loss_ref[...] += jnp.sum(err * err, axis=0, keepdims=True)
        dx2 = err * (1.0 / D)
        dg2_ref[...] += jnp.sum(dx2 * xh2, axis=0, keepdims=True)
        db2_ref[...] += jnp.sum(dx2, axis=0, keepdims=True)
        dr2 = _ln_bwd(dx2 * g2_ref[...], xh2, rstd2)
        dr2_ref[0] = dr2
        dy2_ref[0] = (gate * dr2).astype(MXU_DT)
        dgt_ref[0] += jnp.sum(dr2 * y2, axis=0, keepdims=True)

    vec_out = jax.ShapeDtypeStruct((1, D), F32)
    return pl.pallas_call(
        body, name="final_fwd_bwd", grid=(bsz, t_total // tt),
        in_specs=[_tok_spec(tt), _tok_spec(tt), _bvec_spec(), _vec_spec(), _vec_spec(), _vec_spec(), _vec_spec(),
                  _tok_spec(tt)],
        out_specs=(_vec_spec(), _tok_spec(tt), _tok_spec(tt), _bvec_spec(), _vec_spec(), _vec_spec()),
        out_shape=(vec_out, jax.ShapeDtypeStruct(r1.shape, F32), jax.ShapeDtypeStruct(r1.shape, MXU_DT),
                   jax.ShapeDtypeStruct((bsz, 1, D), F32), vec_out, vec_out),
        compiler_params=_params(("arbitrary", "arbitrary")))(r1, y2, gt, g1, b1, g2, b2, target)


def _ln_bwd_call(name, d_res, d_h, src, g, b, sc, y=None, gt=None):
    bsz, t_total, _ = src.shape
    tt = _div_tile(t_total, ROW_TILE)
    has_y = y is not None

    def body(*refs):
        if has_y:
            (dres_ref, dh_ref, src_ref, g_ref, b_ref, sc_ref, y_ref, gt_ref,
             dsrc_ref, dsc_ref, dsh_ref, dg_ref, db_ref, dy_ref, dgt_ref) = refs
        else:
            (dres_ref, dh_ref, src_ref, g_ref, b_ref, sc_ref,
             dsrc_ref, dsc_ref, dsh_ref, dg_ref, db_ref) = refs
        bi, t = pl.program_id(0), pl.program_id(1)

        @pl.when((bi == 0) & (t == 0))
        def _():
            dg_ref[...] = jnp.zeros_like(dg_ref)
            db_ref[...] = jnp.zeros_like(db_ref)

        @pl.when(t == 0)
        def _():
            dsc_ref[...] = jnp.zeros_like(dsc_ref)
            dsh_ref[...] = jnp.zeros_like(dsh_ref)
            if has_y:
                dgt_ref[...] = jnp.zeros_like(dgt_ref)

        xh, rstd = _ln_stats(src_ref[0])
        xv = xh * g_ref[...] + b_ref[...]
        dh = dh_ref[0].astype(F32)
        dx = ALPHA * dres_ref[0] + dh * (1.0 + sc_ref[0])
        dsc_ref[0] += jnp.sum(dh * xv, axis=0, keepdims=True)
        dsh_ref[0] += jnp.sum(dh, axis=0, keepdims=True)
        dg_ref[...] += jnp.sum(dx * xh, axis=0, keepdims=True)
        db_ref[...] += jnp.sum(dx, axis=0, keepdims=True)
        dsrc = _ln_bwd(dx * g_ref[...], xh, rstd)
        dsrc_ref[0] = dsrc
        if has_y:
            dy_ref[0] = ((1.0 + gt_ref[0]) * dsrc).astype(MXU_DT)
            dgt_ref[0] += jnp.sum(dsrc * y_ref[0].astype(F32), axis=0, keepdims=True)

    vec_out = jax.ShapeDtypeStruct((1, D), F32)
    bvec_out = jax.ShapeDtypeStruct((bsz, 1, D), F32)
    in_specs = [_tok_spec(tt), _tok_spec(tt), _tok_spec(tt), _vec_spec(), _vec_spec(), _bvec_spec()]
    out_specs = [_tok_spec(tt), _bvec_spec(), _bvec_spec(), _vec_spec(), _vec_spec()]
    out_shape = [jax.ShapeDtypeStruct(src.shape, F32), bvec_out, bvec_out, vec_out, vec_out]
    args = [d_res, d_h, src, g, b, sc]
    if has_y:
        in_specs += [_tok_spec(tt), _bvec_spec()]
        out_specs += [_tok_spec(tt), _bvec_spec()]
        out_shape += [jax.ShapeDtypeStruct(src.shape, MXU_DT), bvec_out]
        args += [y, gt]
    return pl.pallas_call(body, name=name, grid=(bsz, t_total // tt), in_specs=in_specs, out_specs=tuple(out_specs),
                          out_shape=tuple(out_shape), compiler_params=_params(("arbitrary", "arbitrary")))(*args)


FFN_TC = 256
FFN_NJ = D_FF // FFN_TC
FFN_PW = 2 * FFN_TC


def _ffn_pair(a, axis):
    shp = list(a.shape)
    a4 = a.reshape(shp[:axis] + [2, FFN_NJ, FFN_TC] + shp[axis + 1:])
    return jnp.swapaxes(a4, axis, axis + 1).reshape(shp)


def _ffn_unpair(a, axis):
    shp = list(a.shape)
    a4 = a.reshape(shp[:axis] + [FFN_NJ, 2, FFN_TC] + shp[axis + 1:])
    return jnp.swapaxes(a4, axis, axis + 1).reshape(shp)


def _ffn_up_act(h, w_up, cw, cb):
    bsz, t_total, _ = h.shape
    tt = _div_tile(t_total, FFN_ROW_TILE)
    def body(h_ref, wu_ref, w_ref, b_ref, up_ref, o_ref, carry_ref):
        up_t = _dot_nt(h_ref[0], wu_ref[...])
        up_ref[0] = up_t
        prev = jnp.where(pl.program_id(2) == 0, 0.0, carry_ref[...])
        rows = jnp.concatenate([prev, up_t], axis=0)
        u = _conv_rows(rows, w_ref, FFN_CONV_K)[HALO:] + b_ref[...]
        o_ref[0] = (_silu(u[:, :FFN_TC]) * u[:, FFN_TC:]).astype(MXU_DT)
        carry_ref[...] = up_t[tt - HALO:, :]

    return pl.pallas_call(
        body, name="ffn_up_act", grid=(bsz, FFN_NJ, t_total // tt),
        in_specs=[pl.BlockSpec((1, tt, D), lambda b, j, t: (b, t, 0)),
                  pl.BlockSpec((FFN_PW, D), lambda b, j, t: (j, 0)),
                  pl.BlockSpec((FFN_CONV_K, FFN_PW), lambda b, j, t: (0, j)),
                  pl.BlockSpec((1, FFN_PW), lambda b, j, t: (0, j))],
        out_specs=(pl.BlockSpec((1, tt, FFN_PW), lambda b, j, t: (b, t, j)),
                   pl.BlockSpec((1, tt, FFN_TC), lambda b, j, t: (b, t, j))),
        out_shape=(jax.ShapeDtypeStruct((bsz, t_total, 2 * D_FF), F32),
                   jax.ShapeDtypeStruct((bsz, t_total, D_FF), MXU_DT)),
        scratch_shapes=[pltpu.VMEM((HALO, FFN_PW), F32)],
        compiler_params=_params(("parallel", "parallel", "arbitrary")))(h, w_up, cw, cb)


HALO16 = 16


def _ffn_act_bwd(up, dy2, w_down, cw, cb):
    bsz, t_total, width = up.shape
    tt = _div_tile(t_total, FFN_ROW_TILE)
    nt = t_total // tt
    hp, hn = _halo_prev(tt), _halo_next(tt, t_total)

    def body(x_ref, xp_ref, xn_ref, dy_ref, dyn_ref, wd_ref, w_ref, b_ref, dup_ref, dw_ref, db_ref):
        b, t = pl.program_id(1), pl.program_id(2)

        @pl.when((b == 0) & (t == 0))
        def _():
            dw_ref[...] = jnp.zeros_like(dw_ref)
            db_ref[...] = jnp.zeros_like(db_ref)

        prev = jnp.where(t == 0, 0.0, xp_ref[0])
        rows = jnp.concatenate([prev, x_ref[0], xn_ref[0]], axis=0)
        u = _conv_rows(rows, w_ref, FFN_CONV_K)[HALO:] + b_ref[...]
        g_pre, v_pre = u[:, :FFN_TC], u[:, FFN_TC:]
        valid = (_iota((tt + HALO, 1), 0) < tt) | (t < nt - 1)
        da = jnp.concatenate([_dot_nt(dy_ref[0], wd_ref[...]), _dot_nt(dyn_ref[0], wd_ref[...])[:HALO]], axis=0)
        da_ext = jnp.where(valid, da, 0.0)
        sg = _sigmoid(g_pre)
        gs = g_pre * sg
        du = jnp.concatenate([da_ext * v_pre * (sg + gs * (1.0 - sg)), da_ext * gs], axis=1)
        dup = du * w_ref[FFN_CONV_K - 1:FFN_CONV_K, :]
        for s in range(1, FFN_CONV_K):
            dup = dup + _shift_up(du, s) * w_ref[FFN_CONV_K - 1 - s:FFN_CONV_K - s, :]
        dup_ref[0] = dup[:tt].astype(MXU_DT)
        du_t = du[:tt]
        db_ref[...] += jnp.sum(du_t, axis=0, keepdims=True)
        for k in range(FFN_CONV_K):
            s = FFN_CONV_K - 1 - k
            xs = (rows if s == 0 else pltpu.roll(rows, s, 0))[HALO:HALO + tt]
            dw_ref[k:k + 1, :] += jnp.sum(du_t * xs, axis=0, keepdims=True)

    def halo(h, w):
        return pl.BlockSpec((1, HALO, w), lambda j, b, t: (*h(b, t), j))

    wspec = lambda rows_: pl.BlockSpec((rows_, FFN_PW), lambda j, b, t: (0, j))
    tile = pl.BlockSpec((1, tt, FFN_PW), lambda j, b, t: (b, t, j))
    dy_next = lambda j, b, t: (b, jnp.minimum((t + 1) * (tt // HALO16), t_total // HALO16 - 1), 0)
    return pl.pallas_call(
        body, name="ffn_act_bwd", grid=(FFN_NJ, bsz, nt),
        in_specs=[tile, halo(hp, FFN_PW), halo(hn, FFN_PW),
                  pl.BlockSpec((1, tt, D), lambda j, b, t: (b, t, 0)), pl.BlockSpec((1, HALO16, D), dy_next),
                  pl.BlockSpec((FFN_TC, D), lambda j, b, t: (j, 0)), wspec(FFN_CONV_K), wspec(1)],
        out_specs=(tile, wspec(FFN_CONV_K), wspec(1)),
        out_shape=(jax.ShapeDtypeStruct(up.shape, MXU_DT), jax.ShapeDtypeStruct((FFN_CONV_K, width), F32),
                   jax.ShapeDtypeStruct((1, width), F32)),
        compiler_params=_params(("arbitrary", "arbitrary", "arbitrary")))(up, up, up, dy2, dy2, w_down, cw, cb)


QKV_W = 3 * HEADS * HD
SM_BLK = P_SM // 128


def _dn_pre_fwd(proj, conv_w, alog_row, dt_row):
    bsz, t_total, _ = proj.shape
    tt = _div_tile(t_total, ROW_TILE)
    hp = _halo_prev(tt)

    def body(x_ref, xp_ref, sm_ref, w_ref, al_ref, dt_ref, q_ref, k_ref, v_ref, g_ref):
        prev = jnp.where(pl.program_id(1) == 0, 0.0, xp_ref[0])
        y = _conv_rows(jnp.concatenate([prev, x_ref[0]], axis=0), w_ref, DN_CONV_K)[HALO:]
        q_ref[0], k_ref[0], v_ref[0] = _dn_qkv(y)
        g_ref[0] = _dn_gates(sm_ref[0], al_ref[...], dt_ref[...])

    out512 = jax.ShapeDtypeStruct((bsz, t_total, HEADS * HD), F32)
    return pl.pallas_call(
        body, name="dn_pre_fwd", grid=(bsz, t_total // tt),
        in_specs=[pl.BlockSpec((1, tt, QKV_W), lambda b, t: (b, t, 0)),
                  pl.BlockSpec((1, HALO, QKV_W), lambda b, t: (*hp(b, t), 0)),
                  pl.BlockSpec((1, tt, 128), lambda b, t: (b, t, SM_BLK)),
                  pl.BlockSpec((DN_CONV_K, QKV_W), lambda b, t: (0, 0)), _vec_spec(128), _vec_spec(128)],
        out_specs=(_tok_spec(tt, 512), _tok_spec(tt, 512), _tok_spec(tt, 512), _tok_spec(tt, 128)),
        out_shape=(out512, out512, out512, jax.ShapeDtypeStruct((bsz, t_total, 128), F32)),
        compiler_params=_params(("parallel", "parallel")))(proj, proj, proj, conv_w, alog_row, dt_row)


def _dn_pre_bwd(proj, dq, dk, dv, dgates, dsm_gla, conv_w, alog_row, dt_row):
    bsz, t_total, _ = proj.shape
    tt = _div_tile(t_total, 256)
    nt = t_total // tt
    hp, hn = _halo_prev(tt), _halo_next(tt, t_total)

    def body(x_ref, xp_ref, xn_ref, sm_ref, dq_ref, dqn_ref, dk_ref, dkn_ref, dv_ref, dvn_ref, dg_ref, dso_ref,
             w_ref, al_ref, dt_ref, dx_ref, dsm_ref, dw_ref, dal_ref, ddt_ref):
        b, t = pl.program_id(0), pl.program_id(1)

        @pl.when((b == 0) & (t == 0))
        def _():
            dw_ref[...] = jnp.zeros_like(dw_ref)
            dal_ref[...] = jnp.zeros_like(dal_ref)
            ddt_ref[...] = jnp.zeros_like(ddt_ref)

        prev = jnp.where(t == 0, 0.0, xp_ref[0])
        rows = jnp.concatenate([prev, x_ref[0], xn_ref[0]], axis=0)
        y = _conv_rows(rows, w_ref, DN_CONV_K)[HALO:]
        valid = (_iota((tt + HALO, 1), 0) < tt) | (t < nt - 1)

        def ext(tile_ref, next_ref):
            return jnp.where(valid, jnp.concatenate([tile_ref[0], next_ref[0]], axis=0), 0.0)

        _, vjp_qkv = jax.vjp(_dn_qkv, y)
        (dy,) = vjp_qkv((ext(dq_ref, dqn_ref), ext(dk_ref, dkn_ref), ext(dv_ref, dvn_ref)))
        dy = jnp.where(valid, dy, 0.0)
        dx = dy * w_ref[DN_CONV_K - 1:DN_CONV_K, :]
        for s in range(1, DN_CONV_K):
            dx = dx + _shift_up(dy, s) * w_ref[DN_CONV_K - 1 - s:DN_CONV_K - s, :]
        dx_ref[0] = dx[:tt].astype(MXU_DT)
        dy_t = dy[:tt]
        for k in range(DN_CONV_K):
            s = DN_CONV_K - 1 - k
            xs = (rows if s == 0 else pltpu.roll(rows, s, 0))[HALO:HALO + tt]
            dw_ref[k:k + 1, :] += jnp.sum(dy_t * xs, axis=0, keepdims=True)
        _, vjp_g = jax.vjp(_dn_gates, sm_ref[0], al_ref[...], dt_ref[...])
        dsm, dal, ddt = vjp_g(dg_ref[0])
        dsm_ref[0] = (dsm + dso_ref[0]).astype(MXU_DT)
        dal_ref[...] += dal
        ddt_ref[...] += ddt

    def tile(width, blk=0):
        return pl.BlockSpec((1, tt, width), lambda b, t: (b, t, blk))

    def halo(h, width):
        return pl.BlockSpec((1, HALO, width), lambda b, t: (*h(b, t), 0))

    return pl.pallas_call(
        body, name="dn_pre_bwd", grid=(bsz, nt),
        in_specs=[tile(QKV_W), halo(hp, QKV_W), halo(hn, QKV_W), tile(128, SM_BLK),
                  tile(512), halo(hn, 512), tile(512), halo(hn, 512), tile(512), halo(hn, 512), tile(128), tile(128),
                  pl.BlockSpec((DN_CONV_K, QKV_W), lambda b, t: (0, 0)), _vec_spec(128), _vec_spec(128)],
        out_specs=(tile(QKV_W), tile(128), pl.BlockSpec((DN_CONV_K, QKV_W), lambda b, t: (0, 0)),
                   _vec_spec(128), _vec_spec(128)),
        out_shape=(jax.ShapeDtypeStruct((bsz, t_total, QKV_W), MXU_DT),
                   jax.ShapeDtypeStruct((bsz, t_total, 128), MXU_DT),
                   jax.ShapeDtypeStruct((DN_CONV_K, QKV_W), F32), jax.ShapeDtypeStruct((1, 128), F32),
                   jax.ShapeDtypeStruct((1, 128), F32)),
        compiler_params=_params(("arbitrary", "arbitrary")))(
            proj, proj, proj, proj, dq, dq, dk, dk, dv, dv, dgates, dsm_gla, conv_w, alog_row, dt_row)


def _state_spec(bsz, idx):
    return pl.BlockSpec((bsz, 1, HEADS, HD, HD), lambda c: (0, idx(c), 0, 0, 0))


def _inv_spec(bsz, idx):
    return pl.BlockSpec((bsz, 1, HEADS, CHUNK, CHUNK), lambda c: (0, idx(c), 0, 0, 0))


def _chunk_spec(bsz, width, idx, blk=0):
    return pl.BlockSpec((bsz, CHUNK, width), lambda c: (0, idx(c), blk))


GQ_BLK, GK_BLK, GV_BLK = P_GQ // 512, P_GK // 512, P_GV // 512
REC_SHARES = (3, 1)


def _rec_fwd(q, k, v, gates, proj, w2, bg):
    bsz, t_total, _ = q.shape
    nc = t_total // CHUNK
    fwd = lambda c: c

    def body(q_ref, k_ref, v_ref, g_ref, gq_ref, gk_ref, gv_ref, sm_ref, w2_ref, bg_ref,
             o_ref, ss_ref, inv_ref, go_ref, gss_ref, s_ref, gs_ref):
        @pl.when(pl.program_id(0) == 0)
        def _():
            s_ref[...] = jnp.zeros_like(s_ref)
            gs_ref[...] = jnp.zeros_like(gs_ref)

        seqs = range(bsz)
        heads = range(HEADS)
        s_list = [[s_ref[b * HEADS + h] for h in heads] for b in seqs]
        gs_list = [[gs_ref[b * HEADS + h] for h in heads] for b in seqs]
        for b in seqs:
            for h in heads:
                ss_ref[b, 0, h] = s_list[b][h]
                gss_ref[b, 0, h] = gs_list[b][h]
        per_seq = lambda ref: [ref[b] for b in seqs]
        (o, new_s, invs), (go, new_gs) = _interleave(
            [_dn_chunk(s_list, per_seq(q_ref), per_seq(k_ref), per_seq(v_ref), per_seq(g_ref)),
             _gla_chunk(gs_list, per_seq(gq_ref), per_seq(gk_ref), per_seq(gv_ref), per_seq(sm_ref),
                        w2_ref[...], bg_ref[...])], REC_SHARES)
        for b in seqs:
            o_ref[b] = o[b]
            go_ref[b] = go[b]
            for h in heads:
                s_ref[b * HEADS + h] = new_s[b][h]
                gs_ref[b * HEADS + h] = new_gs[b][h]
                inv_ref[b, 0, h] = invs[b * HEADS + h]

    tok = lambda width, blk=0: _chunk_spec(bsz, width, fwd, blk)
    state = jax.ShapeDtypeStruct((bsz, nc, HEADS, HD, HD), F32)
    out512 = jax.ShapeDtypeStruct((bsz, t_total, 512), F32)
    return pl.pallas_call(
        body, name="rec_fwd", grid=(nc,),
        in_specs=[tok(512), tok(512), tok(512), tok(128),
                  tok(512, GQ_BLK), tok(512, GK_BLK), tok(512, GV_BLK), tok(128, SM_BLK),
                  pl.BlockSpec((128, 512), lambda c: (0, 0)), pl.BlockSpec((1, 512), lambda c: (0, 0))],
        out_specs=(tok(512), _state_spec(bsz, fwd), _inv_spec(bsz, fwd), tok(512), _state_spec(bsz, fwd)),
        out_shape=(out512, state, jax.ShapeDtypeStruct((bsz, nc, HEADS, CHUNK, CHUNK), F32), out512, state),
        scratch_shapes=[pltpu.VMEM((bsz * HEADS, HD, HD), F32), pltpu.VMEM((bsz * HEADS, HD, HD), F32)],
        compiler_params=_params(("arbitrary",)))(q, k, v, gates, proj, proj, proj, proj, w2, bg)


def _rec_bwd(q, k, v, gates, s_dn, inv_dn, do_dn, proj, w2, bg, s_gla, do_gla):
    bsz, t_total, _ = q.shape
    nc = t_total // CHUNK
    rev = lambda c: nc - 1 - c

    def body(q_ref, k_ref, v_ref, g_ref, ss_ref, inv_ref, do_ref,
             gq_ref, gk_ref, gv_ref, sm_ref, w2_ref, bg_ref, gss_ref, gdo_ref,
             dq_ref, dk_ref, dv_ref, dg_ref, dgq_ref, dgk_ref, dgv_ref, dsm_ref, dw2_ref, dbg_ref, ds_ref, gds_ref):
        @pl.when(pl.program_id(0) == 0)
        def _():
            ds_ref[...] = jnp.zeros_like(ds_ref)
            gds_ref[...] = jnp.zeros_like(gds_ref)
            dw2_ref[...] = jnp.zeros_like(dw2_ref)
            dbg_ref[...] = jnp.zeros_like(dbg_ref)

        seqs = range(bsz)
        heads = range(HEADS)
        per_seq = lambda ref: [ref[b] for b in seqs]
        known = [inv_ref[b, 0, h] for b in seqs for h in heads]

        def both(s_list, q_, k_, v_, g_, gs_list, gq_, gk_, gv_, sm_, w2_, bg_):
            (o, new_s, _), (go, new_gs) = _interleave(
                [_dn_chunk(s_list, q_, k_, v_, g_, inv_known=known),
                 _gla_chunk(gs_list, gq_, gk_, gv_, sm_, w2_, bg_)], REC_SHARES)
            return o, new_s, go, new_gs

        _, vjp = jax.vjp(both, [[ss_ref[b, 0, h] for h in heads] for b in seqs],
                         per_seq(q_ref), per_seq(k_ref), per_seq(v_ref), per_seq(g_ref),
                         [[gss_ref[b, 0, h] for h in heads] for b in seqs],
                         per_seq(gq_ref), per_seq(gk_ref), per_seq(gv_ref), per_seq(sm_ref), w2_ref[...], bg_ref[...])
        ds_in, dq, dk, dv, dg, gds_in, dgq, dgk, dgv, dsm, dw2, dbg = vjp(
            (per_seq(do_ref), [[ds_ref[b * HEADS + h] for h in heads] for b in seqs],
             per_seq(gdo_ref), [[gds_ref[b * HEADS + h] for h in heads] for b in seqs]))
        for b in seqs:
            dq_ref[b], dk_ref[b], dv_ref[b], dg_ref[b] = dq[b], dk[b], dv[b], dg[b]
            dgq_ref[b], dgk_ref[b], dgv_ref[b] = dgq[b].astype(MXU_DT), dgk[b].astype(MXU_DT), dgv[b].astype(MXU_DT)
            dsm_ref[b] = dsm[b]
            for h in heads:
                ds_ref[b * HEADS + h] = ds_in[b][h]
                gds_ref[b * HEADS + h] = gds_in[b][h]
        dw2_ref[...] += dw2
        dbg_ref[...] += dbg

    tok = lambda width, blk=0: _chunk_spec(bsz, width, rev, blk)
    w2_spec = pl.BlockSpec((128, 512), lambda c: (0, 0))
    bg_spec = pl.BlockSpec((1, 512), lambda c: (0, 0))
    f512 = jax.ShapeDtypeStruct((bsz, t_total, 512), F32)
    b512 = jax.ShapeDtypeStruct((bsz, t_total, 512), MXU_DT)
    f128 = jax.ShapeDtypeStruct((bsz, t_total, 128), F32)
    return pl.pallas_call(
        body, name="rec_bwd", grid=(nc,),
        in_specs=[tok(512), tok(512), tok(512), tok(128), _state_spec(bsz, rev), _inv_spec(bsz, rev), tok(512),
                  tok(512, GQ_BLK), tok(512, GK_BLK), tok(512, GV_BLK), tok(128, SM_BLK), w2_spec, bg_spec,
                  _state_spec(bsz, rev), tok(512)],
        out_specs=(tok(512), tok(512), tok(512), tok(128), tok(512), tok(512), tok(512), tok(128), w2_spec, bg_spec),
        out_shape=(f512, f512, f512, f128, b512, b512, b512, f128,
                   jax.ShapeDtypeStruct((128, 512), F32), jax.ShapeDtypeStruct((1, 512), F32)),
        scratch_shapes=[pltpu.VMEM((bsz * HEADS, HD, HD), F32), pltpu.VMEM((bsz * HEADS, HD, HD), F32)],
        compiler_params=_params(("arbitrary",)))(
            q, k, v, gates, s_dn, inv_dn, do_dn, proj, proj, proj, proj, w2, bg, s_gla, do_gla)


Z_BLK, GG_BLK = P_Z // 512, P_GG // 512


def _mix_out_fwd(o_dn, o_gla, proj, grow_dn, grow_gla):
    bsz, t_total, _ = o_dn.shape
    tt = _div_tile(t_total, ROW_TILE)

    def body(od_ref, og_ref, z_ref, gg_ref, gd_ref, gl_ref, o_ref):
        o_ref[0, :, :512] = _gate_norm(od_ref[0], z_ref[0], gd_ref[...]).astype(MXU_DT)
        o_ref[0, :, 512:] = _gate_norm(og_ref[0], gg_ref[0], gl_ref[...]).astype(MXU_DT)

    def col(blk):
        return pl.BlockSpec((1, tt, 512), lambda b, t: (b, t, blk))

    return pl.pallas_call(
        body, name="mix_out_fwd", grid=(bsz, t_total // tt),
        in_specs=[col(0), col(0), col(Z_BLK), col(GG_BLK), _vec_spec(512), _vec_spec(512)],
        out_specs=_tok_spec(tt), out_shape=jax.ShapeDtypeStruct((bsz, t_total, D), MXU_DT),
        compiler_params=_params(("parallel", "parallel")))(o_dn, o_gla, proj, proj, grow_dn, grow_gla)


def _mix_out_bwd(do, o_dn, o_gla, proj, grow_dn, grow_gla):
    bsz, t_total, _ = o_dn.shape
    tt = _div_tile(t_total, ROW_TILE)

    def body(do_ref, od_ref, og_ref, z_ref, gg_ref, gd_ref, gl_ref,
             dod_ref, dog_ref, dz_ref, dgg_ref, dgd_ref, dgl_ref):
        @pl.when((pl.program_id(0) == 0) & (pl.program_id(1) == 0))
        def _():
            dgd_ref[...] = jnp.zeros_like(dgd_ref)
            dgl_ref[...] = jnp.zeros_like(dgl_ref)

        def one(o_ref, gate_ref, g_ref, ct, do_out, dgate_out, dg_out):
            _, vjp = jax.vjp(_gate_norm, o_ref[0], gate_ref[0], g_ref[...])
            d_o, d_gate, d_row = vjp(ct)
            do_out[0] = d_o
            dgate_out[0] = d_gate.astype(MXU_DT)
            acc = d_row[:, :HD]
            for h in range(1, HEADS):
                acc = acc + d_row[:, h * HD:(h + 1) * HD]
            dg_out[...] += acc

        ct = do_ref[0].astype(F32)
        one(od_ref, z_ref, gd_ref, ct[:, :512], dod_ref, dz_ref, dgd_ref)
        one(og_ref, gg_ref, gl_ref, ct[:, 512:], dog_ref, dgg_ref, dgl_ref)

    def col(blk):
        return pl.BlockSpec((1, tt, 512), lambda b, t: (b, t, blk))

    f512 = jax.ShapeDtypeStruct((bsz, t_total, 512), F32)
    b512 = jax.ShapeDtypeStruct((bsz, t_total, 512), MXU_DT)
    g128 = jax.ShapeDtypeStruct((1, HD), F32)
    return pl.pallas_call(
        body, name="mix_out_bwd", grid=(bsz, t_total // tt),
        in_specs=[_tok_spec(tt), col(0), col(0), col(Z_BLK), col(GG_BLK), _vec_spec(512), _vec_spec(512)],
        out_specs=(col(0), col(0), col(0), col(0), _vec_spec(HD), _vec_spec(HD)),
        out_shape=(f512, f512, b512, b512, g128, g128),
        compiler_params=_params(("arbitrary", "arbitrary")))(do, o_dn, o_gla, proj, proj, grow_dn, grow_gla)


def _sum_slots(x, name):
    n, rows, cols = x.shape
    tr = _div_tile(rows, max(8, (1 << 19) // cols))

    def body(x_ref, o_ref):
        acc = x_ref[0].astype(F32)
        for i in range(1, n):
            acc = acc + x_ref[i].astype(F32)
        o_ref[...] = acc

    return pl.pallas_call(
        body, name=name, grid=(rows // tr,),
        in_specs=[pl.BlockSpec((n, tr, cols), lambda i: (0, i, 0))],
        out_specs=pl.BlockSpec((tr, cols), lambda i: (i, 0)),
        out_shape=jax.ShapeDtypeStruct((rows, cols), F32), compiler_params=_params(("parallel",)))(x)


def _adamw_math(w, g, m, v):
    nm = ADAM_B1 * m + (1.0 - ADAM_B1) * g
    nv = ADAM_B2 * v + (1.0 - ADAM_B2) * (g * g)
    m_hat = nm / (1.0 - ADAM_B1 ** ADAM_STEP)
    v_hat = nv / (1.0 - ADAM_B2 ** ADAM_STEP)
    return -ADAM_LR * (m_hat / (jnp.sqrt(v_hat) + ADAM_EPS) + ADAM_WD * w), nm, nv


def _adamw(w, g, m, v, name):
    _, rows, cols = w.shape
    tr = _div_tile(rows, max(8, (1 << 18) // cols))

    def body(w_ref, g_ref, m_ref, v_ref, d_ref, nm_ref, nv_ref):
        d_ref[...], nm_ref[...], nv_ref[...] = _adamw_math(w_ref[...], g_ref[...], m_ref[...], v_ref[...])

    spec = pl.BlockSpec((1, tr, cols), lambda i: (0, i, 0))
    shp = jax.ShapeDtypeStruct(w.shape, F32)
    return pl.pallas_call(body, name=name, grid=(rows // tr,), in_specs=[spec] * 4, out_specs=(spec,) * 3,
                          out_shape=(shp,) * 3, compiler_params=_params(("parallel",)))(w, g, m, v)


def _sum_adamw(parts, w, m, v, name):
    n, rows, cols = parts.shape
    tr = _div_tile(rows, max(8, (1 << 18) // cols))

    def body(p_ref, w_ref, m_ref, v_ref, g_ref, d_ref, nm_ref, nv_ref):
        g = p_ref[0].astype(F32)
        for i in range(1, n):
            g = g + p_ref[i].astype(F32)
        g_ref[...] = g
        d_ref[0], nm_ref[0], nv_ref[0] = _adamw_math(w_ref[0], g, m_ref[0], v_ref[0])

    spec = pl.BlockSpec((1, tr, cols), lambda i: (0, i, 0))
    shp = jax.ShapeDtypeStruct(w.shape, F32)
    return pl.pallas_call(
        body, name=name, grid=(rows // tr,),
        in_specs=[pl.BlockSpec((n, tr, cols), lambda i: (0, i, 0)), spec, spec, spec],
        out_specs=(pl.BlockSpec((tr, cols), lambda i: (i, 0)), spec, spec, spec),
        out_shape=(jax.ShapeDtypeStruct((rows, cols), F32), shp, shp, shp),
        compiler_params=_params(("parallel",)))(parts, w, m, v)


def _adamw_many(ws, gs, ms, vs, name):
    n = len(ws)

    def body(*refs):
        for i in range(n):
            d, nm, nv = _adamw_math(refs[i][...], refs[n + i][...], refs[2 * n + i][...], refs[3 * n + i][...])
            refs[4 * n + i][...] = d
            refs[5 * n + i][...] = nm
            refs[6 * n + i][...] = nv

    shapes = tuple(jax.ShapeDtypeStruct(w.shape, F32) for w in ws)
    outs = pl.pallas_call(body, name=name, out_shape=shapes * 3, compiler_params=_params())(*ws, *gs, *ms, *vs)
    return outs[:n], outs[n:2 * n], outs[2 * n:]


def _position():
    return lax.axis_index("x"), lax.axis_index("y"), lax.axis_index("c")


def _slot(px, py, pc):
    return 4 * px + 2 * py + pc


def _gather_small(x, name):
    rows, cols = x.shape

    def body(x_ref, o_ref, send_sems, recv_sems):
        mx, my, mc = _position()

        def peer(k):
            return (mx ^ ((k >> 2) & 1), my ^ ((k >> 1) & 1), mc ^ (k & 1))

        o_ref[_slot(mx, my, mc)] = x_ref[...]
        sends = []
        for k in range(1, N_DEV):
            cp = pltpu.make_async_remote_copy(src_ref=x_ref, dst_ref=o_ref.at[_slot(mx, my, mc)],
                                              send_sem=send_sems.at[k - 1], recv_sem=recv_sems.at[k - 1],
                                              device_id=peer(k), device_id_type=MESH)
            cp.start()
            sends.append(cp)
        for k in range(1, N_DEV):
            pltpu.make_async_remote_copy(src_ref=x_ref, dst_ref=o_ref.at[_slot(*peer(k))],
                                         send_sem=send_sems.at[k - 1], recv_sem=recv_sems.at[k - 1],
                                         device_id=peer(k), device_id_type=MESH).wait_recv()
        for cp in sends:
            cp.wait_send()

    return pl.pallas_call(
        body, name=name, out_shape=jax.ShapeDtypeStruct((N_DEV, rows, cols), x.dtype),
        in_specs=[pl.BlockSpec(memory_space=pltpu.VMEM)], out_specs=pl.BlockSpec(memory_space=pltpu.VMEM),
        scratch_shapes=[pltpu.SemaphoreType.DMA((N_DEV - 1,)), pltpu.SemaphoreType.DMA((N_DEV - 1,))],
        compiler_params=pltpu.CompilerParams(vmem_limit_bytes=VMEM_LIMIT_V7X))(x)


def _gather_big(shards):
    n = len(shards)

    def body(*refs):
        xs, outs = refs[:n], refs[n:2 * n]
        send_sems, recv_sems, local_sems = refs[2 * n:]
        mx, my, mc = _position()
        me, sibling = (mx, my, mc), (mx, my, 1 - mc)
        chips = [(1 - mx, my), (mx, 1 - my), (1 - mx, 1 - my)]

        def copy(a, k, block, to, src=None):
            dst = outs[a].at[_slot(*block)]
            return pltpu.make_async_remote_copy(src_ref=dst if src is None else src, dst_ref=dst,
                                                send_sem=send_sems.at[7 * a + k], recv_sem=recv_sems.at[7 * a + k],
                                                device_id=to, device_id_type=MESH)

        mine = [pltpu.make_async_copy(xs[a], outs[a].at[_slot(*me)], local_sems.at[a]) for a in range(n)]
        for cp in mine:
            cp.start()
        started = []
        for a in range(n):
            started.append(copy(a, 0, me, sibling, src=xs[a]))
            started += [copy(a, 1 + j, me, (*chip, mc), src=xs[a]) for j, chip in enumerate(chips)]
        for cp in started:
            cp.start()
        for j, chip in enumerate(chips):
            for a in range(n):
                copy(a, 1 + j, (*chip, mc), me).wait_recv()
                fwd = copy(a, 4 + j, (*chip, mc), sibling)
                fwd.start()
                started.append(fwd)
        for a in range(n):
            copy(a, 0, sibling, me).wait_recv()
            for j, chip in enumerate(chips):
                copy(a, 4 + j, (*chip, 1 - mc), me).wait_recv()
        for cp in started:
            cp.wait_send()
        for cp in mine:
            cp.wait()

    any_spec = pl.BlockSpec(memory_space=pl.ANY)
    return pl.pallas_call(
        body, name="gather_weights",
        out_shape=tuple(jax.ShapeDtypeStruct((N_DEV,) + s.shape, s.dtype) for s in shards),
        in_specs=[any_spec] * n, out_specs=(any_spec,) * n,
        scratch_shapes=[pltpu.SemaphoreType.DMA((7 * n,)), pltpu.SemaphoreType.DMA((7 * n,)),
                        pltpu.SemaphoreType.DMA((n,))])(*shards)


def _peer(pos, k):
    mx, my, mc = pos
    return (mx ^ ((k >> 2) & 1), my ^ ((k >> 1) & 1), mc ^ (k & 1))


def _exchange_copies(srcs, lands, send_sems, recv_sems, by_owner):
    pos = _position()
    me = _slot(*pos)
    out = []
    for a, (src, land) in enumerate(zip(srcs, lands)):
        for k in range(1, N_DEV):
            peer = _peer(pos, k)
            sems = dict(send_sem=send_sems.at[7 * a + k - 1], recv_sem=recv_sems.at[7 * a + k - 1],
                        device_id=peer, device_id_type=MESH)
            mine = src.at[_slot(*peer)] if by_owner else src
            send = pltpu.make_async_remote_copy(src_ref=mine, dst_ref=land.at[me], **sems)
            recv = pltpu.make_async_remote_copy(src_ref=mine, dst_ref=land.at[_slot(*peer)], **sems)
            out.append((send, recv))
    return out


_HBM_SPEC = pl.BlockSpec(memory_space=pltpu.HBM)
_SEM_SPEC = pl.BlockSpec(memory_space=pltpu.SEMAPHORE)
_DATAFLOW = pltpu.SideEffectType.DATAFLOW_SIDE_EFFECTING


def _exchange_start(name, srcs, slab_shapes, after, by_owner, carry=()):
    n, na, nc = len(srcs), len(after), len(carry)
    lands = [pltpu.with_memory_space_constraint(lax.empty((N_DEV,) + s, x.dtype), pltpu.HBM)
             for s, x in zip(slab_shapes, srcs)]
    thru = [pltpu.with_memory_space_constraint(x, pltpu.HBM) for x in [*srcs, *lands, *carry]]

    def body(*refs):
        src_refs, land_refs = refs[:n], refs[n:2 * n]
        send_sems, recv_sems = refs[len(thru) + na], refs[len(thru) + na + 1]
        token = refs[-1]
        for send, _ in _exchange_copies(src_refs, land_refs, send_sems, recv_sems, by_owner):
            send.start()
        token[...] = jnp.zeros_like(token)

    outs = pl.pallas_call(
        body, name=name,
        out_shape=(pltpu.SemaphoreType.DMA((7 * n,)), pltpu.SemaphoreType.DMA((7 * n,)),
                   *[pltpu.HBM(x.shape, x.dtype) for x in thru], jax.ShapeDtypeStruct((8, 128), F32)),
        in_specs=[_HBM_SPEC] * len(thru) + [pl.BlockSpec(memory_space=pl.ANY)] * na,
        out_specs=(_SEM_SPEC, _SEM_SPEC, *[_HBM_SPEC] * len(thru), pl.BlockSpec(memory_space=pltpu.VMEM)),
        input_output_aliases={i: 2 + i for i in range(len(thru))},
        compiler_params=pltpu.CompilerParams(has_side_effects=_DATAFLOW))(*thru, *after)
    return (outs[0], outs[1], list(outs[2:2 + n]), list(outs[2 + n:2 + 2 * n]), outs[-1],
            list(outs[2 + 2 * n:2 + 2 * n + nc]))


def _exchange_wait(name, send_sems, recv_sems, srcs, lands, after, by_owner):
    n = len(srcs)

    def body(*refs):
        src_refs, land_refs = refs[:n], refs[n:2 * n]
        s_sems, r_sems = refs[2 * n], refs[2 * n + 1]
        for send, recv in _exchange_copies(src_refs, land_refs, s_sems, r_sems, by_owner):
            send.wait_send()
            recv.wait_recv()

    outs = pl.pallas_call(
        body, name=name,
        out_shape=(*[pltpu.HBM(x.shape, x.dtype) for x in srcs], *[pltpu.HBM(l.shape, l.dtype) for l in lands]),
        in_specs=[_HBM_SPEC] * (2 * n) + [_SEM_SPEC, _SEM_SPEC, pl.BlockSpec(memory_space=pl.ANY)],
        out_specs=tuple([_HBM_SPEC] * (2 * n)),
        input_output_aliases={i: i for i in range(2 * n)},
        compiler_params=pltpu.CompilerParams(has_side_effects=_DATAFLOW))(*srcs, *lands, send_sems, recv_sems, after)
    return list(outs[:n]), list(outs[n:])


def _pad_heads(x, axis):
    shp = list(x.shape)
    x4 = x.reshape(shp[:axis] + [HEADS, GLA_KEY] + shp[axis + 1:])
    pad = [(0, 0)] * x4.ndim
    pad[axis + 1] = (0, HD - GLA_KEY)
    return jnp.pad(x4, pad).reshape(shp[:axis] + [HEADS * HD] + shp[axis + 1:])


def _unpad_heads(x, axis):
    shp = list(x.shape)
    x4 = x.reshape(shp[:axis] + [HEADS, HD] + shp[axis + 1:])
    x4 = lax.slice_in_dim(x4, 0, GLA_KEY, axis=axis + 1)
    return x4.reshape(shp[:axis] + [HEADS * GLA_KEY] + shp[axis + 1:])


O_Z_END, O_AB, O_GQ, O_GK, O_GV, O_R = 2048, 2048, 2056, 2312, 2568, 3592


def _padded_row(f):
    if f < O_Z_END:
        return f
    if f < O_GQ:
        return P_SM + (f - O_AB)
    if f < O_GV:
        base, g = (P_GQ, f - O_GQ) if f < O_GK else (P_GK, f - O_GK)
        return base + HD * (g // GLA_KEY) + g % GLA_KEY
    if f < O_R:
        return P_GV + (f - O_GV)
    return P_SM + 8 + (f - O_R)


def _runs(pairs):
    out = []
    for d, s in pairs:
        if out and out[-1][0] + out[-1][2] == d and out[-1][1] + out[-1][2] == s:
            out[-1][2] += 1
        else:
            out.append([d, s, 1])
    return out


def _pad_in_rows(shards):
    wt = shards.reshape(IN_W, D)
    return jnp.concatenate([
        wt[:O_Z_END], _pad_heads(wt[O_GQ:O_GK], 0), _pad_heads(wt[O_GK:O_GV], 0), wt[O_GV:O_R],
        wt[O_AB:O_GQ], wt[O_R:], jnp.zeros((P_W - P_SM - 8 - GATE_RANK, D), wt.dtype)], axis=0)


def _unpad_in_rows(gt):
    per = IN_W // N_DEV
    return jnp.stack([
        jnp.concatenate([gt[src:src + n] for _, src, n in
                         _runs([(f, _padded_row(f)) for f in range(j * per, (j + 1) * per)])], axis=0)
        for j in range(N_DEV)])


def _lane_row(vals, width=128):
    return jnp.pad(vals.reshape(1, -1), ((0, 0), (0, width - vals.size)))


SMALL_NAMES = ["ln0_g", "ln0_b", "b_ada", "dn_conv", "dn_a_log", "dn_dt_bias", "dn_norm_g", "gla_w_gate2",
               "gla_b_gate", "gla_norm_g", "ln1_g", "ln1_b", "ffn_conv", "ffn_conv_b", "ln2_g", "ln2_b"]
WEIGHTS = ["ln0_g", "ln0_b", "w_ada", "b_ada", "w_in", "dn_conv", "dn_a_log", "dn_dt_bias", "dn_norm_g",
           "gla_w_gate2", "gla_b_gate", "gla_norm_g", "w_o", "ln1_g", "ln1_b", "ffn_w_up", "ffn_conv", "ffn_conv_b",
           "ffn_w_down", "ln2_g", "ln2_b"]


def kernel(x, c, ln0_g, ln0_b, w_ada, b_ada, w_in, dn_conv, dn_a_log, dn_dt_bias, dn_norm_g, gla_w_gate2, gla_b_gate, gla_norm_g, w_o, ln1_g, ln1_b, ffn_w_up, ffn_conv, ffn_conv_b, ffn_w_down, ln2_g, ln2_b, loss_target, m_ln0_g, m_ln0_b, m_w_ada, m_b_ada, m_w_in, m_dn_conv, m_dn_a_log, m_dn_dt_bias, m_dn_norm_g, m_gla_w_gate2, m_gla_b_gate, m_gla_norm_g, m_w_o, m_ln1_g, m_ln1_b, m_ffn_w_up, m_ffn_conv, m_ffn_conv_b, m_ffn_w_down, m_ln2_g, m_ln2_b, v_ln0_g, v_ln0_b, v_w_ada, v_b_ada, v_w_in, v_dn_conv, v_dn_a_log, v_dn_dt_bias, v_dn_norm_g, v_gla_w_gate2, v_gla_b_gate, v_gla_norm_g, v_w_o, v_ln1_g, v_ln1_b, v_ffn_w_up, v_ffn_conv, v_ffn_conv_b, v_ffn_w_down, v_ln2_g, v_ln2_b):
    args = dict(locals())
    w_given = {n: args[n] for n in WEIGHTS}
    m_given = {n: args["m_" + n] for n in WEIGHTS}
    v_given = {n: args["v_" + n] for n in WEIGHTS}
    bsz, t_total, _ = x.shape
    ntok = bsz * t_total
    mx, my, mc = _position()
    me = _slot(mx, my, mc)

    pack1 = jnp.concatenate([c.reshape(-1), dn_conv.reshape(-1), gla_w_gate2.reshape(-1), ffn_conv.reshape(-1)])
    n1 = pack1.size
    rows1 = -(-n1 // 1024) * 8
    pack1 = jnp.pad(pack1, (0, rows1 * 128 - n1)).reshape(rows1, 128)
    got1 = _gather_small(pack1, "gather_cond").reshape(N_DEV, -1)
    o1 = bsz * D
    o2 = o1 + dn_conv.size
    o3 = o2 + gla_w_gate2.size
    c_all = got1[:, :o1].reshape(N_DEV * bsz, D)
    dn_conv_f = got1[:, o1:o2].reshape(N_DEV, DN_CONV_K, -1).transpose(1, 0, 2).reshape(DN_CONV_K, QKV_W)
    gate2_f = got1[:, o2:o3].reshape(N_DEV, GATE_RANK, -1).transpose(1, 0, 2).reshape(GATE_RANK, HEADS * GLA_KEY)
    ffn_conv_f = got1[:, o3:n1].reshape(N_DEV, FFN_CONV_K, -1).transpose(1, 0, 2).reshape(FFN_CONV_K, 2 * D_FF)

    win_t = w_in[0].T.astype(MXU_DT)
    wup_t = ffn_w_up[0].T.astype(MXU_DT)
    (win_all,) = _gather_big([win_t])
    win_p = _pad_in_rows(win_all)
    cw_p, cb_p = _ffn_pair(ffn_conv_f, 1), _ffn_pair(ffn_conv_b, 1)

    ncol = w_ada.shape[2]
    b_cols = lax.dynamic_slice_in_dim(b_ada, me * ncol, ncol, axis=1)
    mod_part = _ada_fwd(c_all, w_ada[0], b_cols)
    mod_all = _gather_small(mod_part.reshape(-1, 128), "gather_mod").reshape(N_DEV, N_DEV * bsz, ncol)
    mod = lax.dynamic_slice_in_dim(mod_all, me * bsz, bsz, axis=1).transpose(1, 0, 2).reshape(bsz, 6, 1, D)
    late = [w_o[0].astype(MXU_DT), wup_t, ffn_w_down[0].astype(MXU_DT)]
    ag_send, ag_recv, ag_src, ag_land, ag_token, _ = _exchange_start(
        "gather_start", late, [w.shape for w in late], [win_all, mod_all], by_owner=False)
    mod = mod + ag_token[0, 0]
    sh_a, sc_a, gt_a, sh_f, sc_f, gt_f = (mod[:, i] for i in range(6))

    g0, b0 = ln0_g.reshape(1, D), ln0_b.reshape(1, D)
    alog_row, dt_row = _lane_row(dn_a_log[0]), _lane_row(dn_dt_bias[0])
    grow_dn, grow_gla = jnp.tile(dn_norm_g, (1, HEADS)), jnp.tile(gla_norm_g, (1, HEADS))
    w2 = jnp.zeros((128, HEADS * HD), F32).at[SM_R:SM_R + GATE_RANK].set(_pad_heads(gate2_f, 1))
    bg = _pad_heads(gla_b_gate, 1)

    h_a = _ln0_mod(x, g0, b0, sc_a, sh_a)
    proj = _mm(h_a.reshape(ntok, D), win_p, "nt", F32, "mm_proj", tm=1024, tn=1408).reshape(bsz, t_total, P_W)
    q, k, v, gates = _dn_pre_fwd(proj, dn_conv_f, alog_row, dt_row)
    o_dn, s_dn, inv_dn, o_gla, s_gla = _rec_fwd(q, k, v, gates, proj, w2, bg)
    o_mix = _mix_out_fwd(o_dn, o_gla, proj, grow_dn, grow_gla)
    late, landed = _exchange_wait("gather_wait", ag_send, ag_recv, ag_src, ag_land, o_mix, by_owner=False)
    wo_all, wup_all, wdn_all = (lax.dynamic_update_slice(l, w[None], (me, 0, 0)) for l, w in zip(landed, late))
    wo_f = wo_all.reshape(D, D)
    wup_f = _ffn_pair(wup_all.reshape(2 * D_FF, D), 0)
    wdn_f = wdn_all.reshape(D_FF, D)
    y = _mm(o_mix.reshape(ntok, D), wo_f, "nn", MXU_DT, "mm_wo", tm=1024, tn=1024).reshape(bsz, t_total, D)
    r1, h_f = _res_ln_mod(x, y, gt_a, g0, b0, ln1_g, ln1_b, sc_f, sh_f)
    up, act = _ffn_up_act(h_f, wup_f, cw_p, cb_p)
    y2 = _mm(act.reshape(ntok, D_FF), wdn_f, "nn", MXU_DT, "mm_down", tm=1024, tn=1024).reshape(bsz, t_total, D)
    loss_rows, dr2, dy2, dgt_f, d_ln2_g, d_ln2_b = _final_fwd_bwd(r1, y2, gt_f, ln1_g, ln1_b, ln2_g, ln2_b, loss_target)
    loss_part = (0.5 / D) * jnp.sum(loss_rows)

    dy2_2 = dy2.reshape(ntok, D)
    g_wdn = _mm(act.reshape(ntok, D_FF), dy2_2, "tn", MXU_DT, "mm_gwdn", tm=1408, tn=1024)
    dup, d_cw_p, d_cb_p = _ffn_act_bwd(up, dy2, wdn_f, cw_p, cb_p)
    d_ffn_conv, d_ffn_conv_b = _ffn_unpair(d_cw_p, 1), _ffn_unpair(d_cb_p, 1)
    dup_2 = dup.reshape(ntok, 2 * D_FF)
    dh_f = _mm(dup_2, wup_f, "nn", MXU_DT, "mm_dhf", tn=1024).reshape(bsz, t_total, D)
    g_wup_t = _mm(dup_2, h_f.reshape(ntok, D), "tn", MXU_DT, "mm_gwup", tm=1408, tn=1024)
    ffn_parts = [_ffn_unpair(g_wup_t, 0).reshape(N_DEV, -1, D), g_wdn.reshape(N_DEV, -1, D)]
    rs_send, rs_recv, rs_src, rs_land, rs_token, _ = _exchange_start(
        "scatter_start", ffn_parts, [p.shape[1:] for p in ffn_parts], [dh_f], by_owner=True)
    dr1, dsc_f, dsh_f, d_ln1_g, d_ln1_b, dy, dgt_a = _ln_bwd_call(
        "ln1_bwd", dr2, dh_f, r1, ln1_g, ln1_b, sc_f + rs_token[0, 0], y=y, gt=gt_a)

    dy_2 = dy.reshape(ntok, D)
    do = _mm(dy_2, wo_f, "nt", MXU_DT, "mm_do", tm=1024, tn=1024).reshape(bsz, t_total, D)
    g_wo = _mm(o_mix.reshape(ntok, D), dy_2, "tn", MXU_DT, "mm_gwo", tm=512, tn=1024)
    do_dn, do_gla, dz, dgg, d_dn_norm, d_gla_norm = _mix_out_bwd(do, o_dn, o_gla, proj, grow_dn, grow_gla)
    dq, dk, dv, dgates, dgq, dgk, dgv, dsm_gla, d_w2, d_bg = _rec_bwd(
        q, k, v, gates, s_dn, inv_dn, do_dn, proj, w2, bg, s_gla, do_gla)
    dqkv, dsm, d_dn_conv, d_alog_row, d_dt_row = _dn_pre_bwd(
        proj, dq, dk, dv, dgates, dsm_gla, dn_conv_f, alog_row, dt_row)
    dproj = jnp.concatenate([dqkv, dz, dgq, dgk, dgv, dgg, dsm], axis=-1).reshape(ntok, P_W)
    g_win_p = _mm(dproj, h_a.reshape(ntok, D), "tn", MXU_DT, "mm_gwin", tm=1408, tn=1024)
    mix_parts = [_unpad_in_rows(g_win_p), g_wo.reshape(N_DEV, -1, D)]
    rs2_send, rs2_recv, rs2_src, rs2_land, rs2_token, (win_p_late,) = _exchange_start(
        "scatter_mix_start", mix_parts, [p.shape[1:] for p in mix_parts], [], by_owner=True, carry=[win_p])
    dh_a = _mm(dproj, win_p_late, "nn", MXU_DT, "mm_dha", tn=1024).reshape(bsz, t_total, D)
    grad_x, dsc_a, dsh_a, d_ln0_g, d_ln0_b = _ln_bwd_call(
        "ln0_bwd", dr1, dh_a, x, g0, b0, sc_a + rs2_token[0, 0])

    delta, new_m, new_v, big_grads = {}, {}, {}, {}
    flip = lambda a: jnp.swapaxes(a, 1, 2)

    def update_owned(n, landed, mine):
        parts = lax.dynamic_update_slice(landed, lax.dynamic_slice_in_dim(mine, me, 1, axis=0), (me, 0, 0))
        turn = flip if parts.shape[1:] != w_given[n].shape[1:] else (lambda a: a)
        g, d_, m_, v_ = _sum_adamw(parts, turn(w_given[n]), turn(m_given[n]), turn(v_given[n]), "adamw_" + n)
        big_grads[n], delta[n], new_m[n], new_v[n] = turn(g[None]), turn(d_), turn(m_), turn(v_)

    ffn_parts, ffn_landed = _exchange_wait("scatter_wait", rs_send, rs_recv, rs_src, rs_land, grad_x, by_owner=True)
    update_owned("ffn_w_up", ffn_landed[0], ffn_parts[0])
    update_owned("ffn_w_down", ffn_landed[1], ffn_parts[1])
    ffn_done = 0.0 * (new_v["ffn_w_up"][0, 0, 0] + new_v["ffn_w_down"][0, 0, 0])

    dmod = jnp.concatenate([dsh_a, dsc_a, dgt_a, dsh_f, dsc_f, dgt_f], axis=1).reshape(-1)
    small_parts = {
        "ln0_g": d_ln0_g, "ln0_b": d_ln0_b, "ln1_g": d_ln1_g, "ln1_b": d_ln1_b, "ln2_g": d_ln2_g, "ln2_b": d_ln2_b,
        "dn_a_log": d_alog_row[:, :HEADS], "dn_dt_bias": d_dt_row[:, :HEADS],
        "dn_norm_g": d_dn_norm, "gla_norm_g": d_gla_norm, "gla_b_gate": _unpad_heads(d_bg, 1),
        "ffn_conv_b": d_ffn_conv_b, "dn_conv": d_dn_conv,
        "gla_w_gate2": _unpad_heads(d_w2[SM_R:SM_R + GATE_RANK], 1), "ffn_conv": d_ffn_conv}
    order = sorted(small_parts)
    flat = jnp.concatenate([small_parts[n].reshape(-1) for n in order] + [(loss_part + ffn_done).reshape(1), dmod])
    n3 = flat.size
    rows3 = -(-n3 // 1024) * 8
    pack3 = jnp.pad(flat, (0, rows3 * 128 - n3)).reshape(rows3, 128)
    got3 = _gather_small(pack3, "gather_small_grads")
    tot3 = _sum_slots(got3, "sum_small_grads").reshape(-1)
    grads = {}
    off = 0
    for n in order:
        size = small_parts[n].size
        grads[n] = tot3[off:off + size]
        off += size
    loss = tot3[off]
    off += 1
    dmod_all = got3.reshape(N_DEV, -1)[:, off:off + dmod.size].reshape(N_DEV * bsz, 6 * D)
    dmod_cols = lax.dynamic_slice_in_dim(dmod_all, me * ncol, ncol, axis=1)
    g_wada, g_bada = _ada_bwd(c_all, dmod_all, dmod_cols)
    grads["b_ada"] = g_bada

    def col_shard(full, rows):
        part = full.reshape(rows, -1)
        width = part.shape[1] // N_DEV
        return lax.dynamic_slice_in_dim(part, me * width, width, axis=1)

    grads["dn_conv"] = col_shard(grads["dn_conv"], DN_CONV_K)
    grads["gla_w_gate2"] = col_shard(grads["gla_w_gate2"], GATE_RANK)
    grads["ffn_conv"] = col_shard(grads["ffn_conv"], FFN_CONV_K)
    grads = {n: g.reshape(w_given[n].shape) for n, g in grads.items()}
    mix_parts, mix_landed = _exchange_wait("scatter_mix_wait", rs2_send, rs2_recv, rs2_src, rs2_land, grad_x,
                                           by_owner=True)
    update_owned("w_in", mix_landed[0], mix_parts[0])
    update_owned("w_o", mix_landed[1], mix_parts[1])
    grads["w_ada"] = g_wada.reshape(w_ada.shape)
    delta["w_ada"], new_m["w_ada"], new_v["w_ada"] = _adamw(w_ada, grads["w_ada"], m_w_ada, v_w_ada, "adamw_w_ada")
    grads.update(big_grads)
    d_s, m_s, v_s = _adamw_many(*[[src[n] for n in SMALL_NAMES] for src in (w_given, grads, m_given, v_given)],
                                "adamw_small")
    for i, n in enumerate(SMALL_NAMES):
        delta[n], new_m[n], new_v[n] = d_s[i], m_s[i], v_s[i]

    return (loss, grad_x, *[grads[n] for n in WEIGHTS], *[delta[n] for n in WEIGHTS],
            *[new_m[n] for n in WEIGHTS], *[new_v[n] for n in WEIGHTS])
```

```python
import jax
import jax.numpy as jnp
from jax import lax
from jax.experimental import pallas as pl
from jax.experimental.pallas import tpu as pltpu

F32 = jnp.float32
MXU_DT = jnp.bfloat16
MESH = pl.DeviceIdType.MESH
N_DEV = 8

D = 1024
HEADS = 4
HD = 128
CHUNK = 64
GLA_KEY = 64
GLA_TAU = 16.0
GATE_RANK = 16
D_FF = 2816
IN_W = 3608
ALPHA = 2.0 ** 0.25
EPS = 1e-6
DN_CONV_K = 4
FFN_CONV_K = 3
HALO = 8
ROW_TILE = 1024
FFN_ROW_TILE = 1024

P_QKV, P_Z, P_GQ, P_GK, P_GV, P_GG, P_SM, P_W = 0, 1536, 2048, 2560, 3072, 3584, 4096, 4224
SM_A, SM_B, SM_R = 0, 4, 8

ADAM_LR, ADAM_B1, ADAM_B2, ADAM_EPS, ADAM_WD, ADAM_STEP = 0.001, 0.9, 0.999, 1e-08, 0.01, 10

VMEM_LIMIT_V7X = 56 * 1024 * 1024


def _params(sem=None):
    return pltpu.CompilerParams(dimension_semantics=sem, vmem_limit_bytes=VMEM_LIMIT_V7X)


NN, NT, TN = ((1,), (0,)), ((1,), (1,)), ((0,), (0,))


def _dg(a, b, dims):
    return lax.dot_general(a, b, (dims, ((), ())), preferred_element_type=F32)


def _dot(a, b):
    return _dg(a, b, NN)


def _dot_nt(a, b):
    return _dg(a, b, NT)


def _dot_tn(a, b):
    return _dg(a, b, TN)


def _iota(shape, dim):
    return lax.broadcasted_iota(jnp.int32, shape, dim)


def _sigmoid(x):
    return jax.nn.sigmoid(x)


def _silu(x):
    return x * _sigmoid(x)


def _softplus(x):
    return jnp.maximum(x, 0.0) + jnp.log(1.0 + jnp.exp(-jnp.abs(x)))


def _ln_stats(x):
    mu = jnp.mean(x, axis=-1, keepdims=True)
    xc = x - mu
    rstd = lax.rsqrt(jnp.mean(xc * xc, axis=-1, keepdims=True) + EPS)
    return xc * rstd, rstd


def _ln_bwd(dxhat, xhat, rstd):
    return rstd * (dxhat - jnp.mean(dxhat, axis=-1, keepdims=True)
                   - xhat * jnp.mean(dxhat * xhat, axis=-1, keepdims=True))


def _split2(a):
    hi = a.astype(jnp.bfloat16)
    return hi, (a - hi.astype(F32)).astype(jnp.bfloat16)


def _d3(a, b, dims):
    ah, al = _split2(a)
    bh, bl = _split2(b)
    return _dg(ah, bh, dims) + (_dg(ah, bl, dims) + _dg(al, bh, dims))


@jax.custom_vjp
def _dot3(a, b):
    return _d3(a, b, NN)


_dot3.defvjp(lambda a, b: (_d3(a, b, NN), (a, b)),
             lambda res, g: (_d3(g, res[1], NT), _d3(res[0], g, TN)))


def _split3(b):
    b1 = b.astype(jnp.bfloat16)
    r1 = b - b1.astype(F32)
    b2 = r1.astype(jnp.bfloat16)
    return b1, b2, (r1 - b2.astype(F32)).astype(jnp.bfloat16)


def _sum3(fn, b):
    b1, b2, b3 = _split3(b)
    return fn(b1) + (fn(b2) + fn(b3))


@jax.custom_vjp
def _mask_dot(e, b):
    return _sum3(lambda t: _dg(e, t, NN), b)


_mask_dot.defvjp(lambda e, b: (_mask_dot(e, b), e),
                 lambda e, g: (jnp.zeros_like(e), _sum3(lambda t: _dg(e, t, TN), g)))


@jax.custom_vjp
def _mask_dot_nt(e, b):
    return _sum3(lambda t: _dg(e, t, NT), b)


_mask_dot_nt.defvjp(lambda e, b: (_mask_dot_nt(e, b), e),
                    lambda e, g: (jnp.zeros_like(e), _sum3(lambda t: _dg(t, e, TN), g)))


def _interleave(gens, shares):
    results = [None] * len(gens)
    live = list(range(len(gens)))
    while live:
        for i in list(live):
            for _ in range(shares[i]):
                try:
                    next(gens[i])
                except StopIteration as done:
                    results[i] = done.value
                    live.remove(i)
                    break
    return results


def _tri_inv_stages(ms):
    n = ms[0].shape[0]
    r, c = _iota((n, n), 0), _iota((n, n), 1)
    eye = (r == c).astype(F32)
    diag = (r >> 3) == (c >> 3)
    ds = [jnp.where(diag, m, 0.0) for m in ms]
    d2s = [_d3(d, d, NN) for d in ds]
    yield
    d4s = [_d3(d2, d2, NN) for d2 in d2s]
    invs = [_d3(eye - d, eye + d2, NN) for d, d2 in zip(ds, d2s)]
    yield
    invs = [_d3(inv, eye + d4, NN) for inv, d4 in zip(invs, d4s)]
    yield
    shift = 3
    while (1 << shift) < n:
        rb, cb = r >> shift, c >> shift
        sel = ((rb & 1) == 1) & (cb == rb - 1)
        tmp = [_d3(inv, jnp.where(sel, m, 0.0), NN) for inv, m in zip(invs, ms)]
        yield
        invs = [inv - _d3(t, inv, NN) for t, inv in zip(tmp, invs)]
        yield
        shift += 1
    return invs


def _tri_inv_bwd(invs, das):
    tmp = [_d3(a, da, TN) for a, da in zip(invs, das)]
    return ([-_d3(t, a, NT) for t, a in zip(tmp, invs)],)


@jax.custom_vjp
def _tri_inv_known(ms, invs):
    return invs


_tri_inv_known.defvjp(lambda ms, invs: (invs, invs),
                      lambda invs, das: (_tri_inv_bwd(invs, das)[0], [jnp.zeros_like(a) for a in invs]))


def _dn_chunk(s_list, q, k, v, gates, inv_known=None):
    nb = len(q)
    c = q[0].shape[0]
    r64, c64 = _iota((c, c), 0), _iota((c, c), 1)
    causal = r64 >= c64
    strict = r64 > c64
    tri = causal.astype(jnp.bfloat16)
    eye = (_iota((HD, HD), 0) == _iota((HD, HD), 1)).astype(jnp.bfloat16)
    lane = _iota(gates[0].shape, 1)
    lane1 = _iota((1, HD), 1)
    g_all = [_mask_dot(tri, g) for g in gates]
    yield
    g_all_t = [_mask_dot_nt(eye, g) for g in g_all]
    yield
    row = _iota(g_all_t[0].shape, 0)
    last = [jnp.sum(g, axis=0, keepdims=True) for g in gates]
    prob = [(b, h) for b in range(nb) for h in range(HEADS)]
    sl = [slice(h * HD, (h + 1) * HD) for h in range(HEADS)]
    qh = [q[b][:, sl[h]] for b, h in prob]
    kh = [k[b][:, sl[h]] for b, h in prob]
    vh = [v[b][:, sl[h]] for b, h in prob]
    s = [s_list[b][h] for b, h in prob]
    beta = [jnp.sum(jnp.where(lane == SM_B + h, gates[b], 0.0), axis=-1, keepdims=True) for b, h in prob]
    g_c = [jnp.sum(jnp.where(lane == SM_A + h, g_all[b], 0.0), axis=-1, keepdims=True) for b, h in prob]
    g_r = [jnp.sum(jnp.where(row == SM_A + h, g_all_t[b], 0.0), axis=0, keepdims=True) for b, h in prob]
    g_last = [jnp.sum(jnp.where(lane1 == SM_A + h, last[b], 0.0), axis=-1, keepdims=True) for b, h in prob]
    decay = [jnp.where(causal, jnp.exp(jnp.where(causal, gc - gr, 0.0)), 0.0) for gc, gr in zip(g_c, g_r)]
    kb = [k_ * b_ for k_, b_ in zip(kh, beta)]
    m_low = [jnp.where(strict, _dot_nt(kb_, k_) * d_, 0.0) for kb_, k_, d_ in zip(kb, kh, decay)]
    yield
    attn = [_dot_nt(q_, k_) * d_ for q_, k_, d_ in zip(qh, kh, decay)]
    yield
    if inv_known is None:
        a_inv = yield from _tri_inv_stages(m_low)
    else:
        a_inv = _tri_inv_known(m_low, inv_known)
    eg = [jnp.exp(gc) for gc in g_c]
    uw = [_dot3(a_, jnp.concatenate([v_ * b_, kb_ * e_], axis=1))
          for a_, v_, b_, kb_, e_ in zip(a_inv, vh, beta, kb, eg)]
    yield
    v_new = [uw_[:, :HD] - _dot(uw_[:, HD:], s_) for uw_, s_ in zip(uw, s)]
    yield
    qs = [_dot(q_ * e_, s_) for q_, e_, s_ in zip(qh, eg, s)]
    yield
    o = [qs_ + _dot(a_, vn_) for qs_, a_, vn_ in zip(qs, attn, v_new)]
    yield
    k_dec = [k_ * jnp.exp(gl - gc) for k_, gl, gc in zip(kh, g_last, g_c)]
    s_new = [s_ * jnp.exp(gl) + _dot_tn(kd_, vn_) for s_, gl, kd_, vn_ in zip(s, g_last, k_dec, v_new)]
    outs = [jnp.concatenate(o[b * HEADS:(b + 1) * HEADS], axis=-1) for b in range(nb)]
    states = [s_new[b * HEADS:(b + 1) * HEADS] for b in range(nb)]
    return outs, states, a_inv


def _gla_chunk(st_list, q, k, v, small, w2, bg):
    nb = len(q)
    c = q[0].shape[0]
    causal = _iota((c, c), 0) >= _iota((c, c), 1)
    tri = causal.astype(jnp.bfloat16)
    la_all = [-_softplus(-(_dot(sm, w2) + bg)) * (1.0 / GLA_TAU) for sm in small]
    yield
    b_all = [_mask_dot(tri, la) for la in la_all]
    yield
    prob = [(b, h) for b in range(nb) for h in range(HEADS)]
    sl = [slice(h * HD, (h + 1) * HD) for h in range(HEADS)]
    kh = [k[b][:, sl[h]] for b, h in prob]
    vh = [v[b][:, sl[h]] for b, h in prob]
    st = [st_list[b][h] for b, h in prob]
    bc = [b_all[b][:, sl[h]] for b, h in prob]
    b_last = [jnp.sum(la_all[b][:, sl[h]], axis=0, keepdims=True) for b, h in prob]
    q_dec = [q[b][:, sl[h]] * (GLA_KEY ** -0.5) * jnp.exp(bc_) for (b, h), bc_ in zip(prob, bc)]
    attn = [jnp.where(causal, _dot_nt(qd, k_ * jnp.exp(-bc_)), 0.0) for qd, k_, bc_ in zip(q_dec, kh, bc)]
    yield
    inter = [_dot_nt(qd, st_) for qd, st_ in zip(q_dec, st)]
    yield
    o = [i_ + _dot(a_, v_) for i_, a_, v_ in zip(inter, attn, vh)]
    yield
    k_dec = [k_ * jnp.exp(bl - bc_) for k_, bl, bc_ in zip(kh, b_last, bc)]
    s_new = [st_ * jnp.exp(bl) + _dot_tn(v_, kd) for st_, bl, v_, kd in zip(st, b_last, vh, k_dec)]
    outs = [jnp.concatenate(o[b * HEADS:(b + 1) * HEADS], axis=-1) for b in range(nb)]
    return outs, [s_new[b * HEADS:(b + 1) * HEADS] for b in range(nb)]


def _dn_qkv(y):
    act = _silu(y)
    parts = []
    for i in range(2 * HEADS):
        xh = act[:, i * HD:(i + 1) * HD]
        xh = xh * lax.rsqrt(jnp.sum(xh * xh, axis=-1, keepdims=True) + EPS)
        parts.append(xh * (HD ** -0.5) if i < HEADS else xh)
    qk = jnp.concatenate(parts, axis=-1)
    return qk[:, :HEADS * HD], qk[:, HEADS * HD:], act[:, 2 * HEADS * HD:]


def _dn_gates(small, alog_row, dt_row):
    lane = _iota(small.shape, 1)
    log_a = -jnp.exp(alog_row) * _softplus(small + dt_row)
    return jnp.where(lane < SM_B, log_a, jnp.where(lane < SM_R, _sigmoid(small), 0.0))


def _gate_norm(o, z, grow):
    parts = []
    for h in range(HEADS):
        oh = o[:, h * HD:(h + 1) * HD]
        parts.append(oh * lax.rsqrt(jnp.mean(oh * oh, axis=-1, keepdims=True) + EPS))
    return jnp.concatenate(parts, axis=-1) * grow * _silu(z)


def _conv_rows(xrows, w_ref, k_taps):
    n = xrows.shape[0]
    acc = xrows * w_ref[k_taps - 1:k_taps, :]
    for s in range(1, k_taps):
        acc = acc + pltpu.roll(xrows, s, 0) * w_ref[k_taps - 1 - s:k_taps - s, :]
    return acc


def _shift_up(x, s):
    return x if s == 0 else pltpu.roll(x, x.shape[0] - s, 0)


def _div_tile(n, cap, mult=8):
    best = None
    for t in range(mult, min(n, cap) + 1, mult):
        if n % t == 0:
            best = t
    return best if best is not None else n


def _halo_prev(tt):
    return lambda b, t: (b, jnp.maximum(t * (tt // HALO) - 1, 0))


def _halo_next(tt, t_total):
    return lambda b, t: (b, jnp.minimum((t + 1) * (tt // HALO), t_total // HALO - 1))


def _mm(a, b, mode, out_dtype, name, tm=512, tn=512, tk=None):
    if mode == "nn":
        (m, k), n = a.shape, b.shape[1]
    elif mode == "nt":
        (m, k), n = a.shape, b.shape[0]
    else:
        (k, m), n = a.shape, b.shape[1]
    tm, tn = min(tm, m), min(tn, n)
    tk = k if tk is None else min(tk, k)
    assert m % tm == 0 and n % tn == 0 and k % tk == 0, (name, a.shape, b.shape, tm, tn, tk)
    nk = k // tk
    if mode == "tn":
        a_spec = pl.BlockSpec((tk, tm), lambda i, j, kk: (kk, i))
    else:
        a_spec = pl.BlockSpec((tm, tk), lambda i, j, kk: (i, kk))
    if mode == "nt":
        b_spec = pl.BlockSpec((tn, tk), lambda i, j, kk: (j, kk))
    else:
        b_spec = pl.BlockSpec((tk, tn), lambda i, j, kk: (kk, j))
    dims = {"nn": NN, "nt": NT, "tn": TN}[mode]

    def body(a_ref, b_ref, o_ref, *acc):
        p = _dg(a_ref[...], b_ref[...], dims)
        if nk == 1:
            o_ref[...] = p.astype(out_dtype)
        else:
            kk = pl.program_id(2)

            @pl.when(kk == 0)
            def _():
                acc[0][...] = p

            @pl.when(kk > 0)
            def _():
                acc[0][...] += p

            @pl.when(kk == nk - 1)
            def _():
                o_ref[...] = acc[0][...].astype(out_dtype)

    return pl.pallas_call(
        body, name=name, grid=(m // tm, n // tn, nk),
        in_specs=[a_spec, b_spec],
        out_specs=pl.BlockSpec((tm, tn), lambda i, j, kk: (i, j)),
        out_shape=jax.ShapeDtypeStruct((m, n), out_dtype),
        scratch_shapes=[pltpu.VMEM((tm, tn), F32)] if nk > 1 else [],
        compiler_params=_params(("parallel", "parallel", "arbitrary")),
    )(a, b)


def _ada_fwd(c_all, w_ada, b_cols):
    def body(c_ref, w_ref, b_ref, o_ref):
        cond = _silu(c_ref[...]).astype(MXU_DT)
        o_ref[...] = _dot(cond, w_ref[...].astype(MXU_DT)) + b_ref[...]

    return pl.pallas_call(body, name="ada_fwd", out_shape=jax.ShapeDtypeStruct((c_all.shape[0], w_ada.shape[1]), F32),
                          compiler_params=_params())(c_all, w_ada, b_cols)


def _ada_bwd(c_all, dmod_all, dmod_cols):
    def body(c_ref, da_ref, dc_ref, gw_ref, gb_ref):
        cond = _silu(c_ref[...]).astype(MXU_DT)
        gw_ref[...] = _dot_tn(cond, dc_ref[...].astype(MXU_DT))
        gb_ref[...] = jnp.sum(da_ref[...], axis=0, keepdims=True)

    return pl.pallas_call(
        body, name="ada_bwd",
        out_shape=(jax.ShapeDtypeStruct((c_all.shape[1], dmod_cols.shape[1]), F32),
                   jax.ShapeDtypeStruct((1, dmod_all.shape[1]), F32)),
        compiler_params=_params())(c_all, dmod_all, dmod_cols)


def _tok_spec(tt, width=D):
    return pl.BlockSpec((1, tt, width), lambda b, t: (b, t, 0))


def _vec_spec(width=D):
    return pl.BlockSpec((1, width), lambda b, t: (0, 0))


def _bvec_spec(width=D):
    return pl.BlockSpec((1, 1, width), lambda b, t: (b, 0, 0))


def _ln0_mod(x, g0, b0, sc, sh):
    bsz, t_total, _ = x.shape
    tt = _div_tile(t_total, ROW_TILE)

    def body(x_ref, g_ref, b_ref, sc_ref, sh_ref, h_ref):
        xh, _ = _ln_stats(x_ref[0])
        x0 = xh * g_ref[...] + b_ref[...]
        h_ref[0] = (x0 * (1.0 + sc_ref[0]) + sh_ref[0]).astype(MXU_DT)

    return pl.pallas_call(
        body, name="ln0_mod", grid=(bsz, t_total // tt),
        in_specs=[_tok_spec(tt), _vec_spec(), _vec_spec(), _bvec_spec(), _bvec_spec()],
        out_specs=_tok_spec(tt), out_shape=jax.ShapeDtypeStruct(x.shape, MXU_DT),
        compiler_params=_params(("parallel", "parallel")))(x, g0, b0, sc, sh)


def _res_ln_mod(x, y, gt, g0, b0, g1, b1, sc, sh):
    bsz, t_total, _ = x.shape
    tt = _div_tile(t_total, ROW_TILE)

    def body(x_ref, y_ref, gt_ref, g0_ref, b0_ref, g1_ref, b1_ref, sc_ref, sh_ref, r_ref, h_ref):
        xh, _ = _ln_stats(x_ref[0])
        r = ALPHA * (xh * g0_ref[...] + b0_ref[...]) + (1.0 + gt_ref[0]) * y_ref[0].astype(F32)
        r_ref[0] = r
        rh, _ = _ln_stats(r)
        x1 = rh * g1_ref[...] + b1_ref[...]
        h_ref[0] = (x1 * (1.0 + sc_ref[0]) + sh_ref[0]).astype(MXU_DT)

    return pl.pallas_call(
        body, name="res_ln_mod", grid=(bsz, t_total // tt),
        in_specs=[_tok_spec(tt), _tok_spec(tt), _bvec_spec(), _vec_spec(), _vec_spec(), _vec_spec(), _vec_spec(),
                  _bvec_spec(), _bvec_spec()],
        out_specs=(_tok_spec(tt), _tok_spec(tt)),
        out_shape=(jax.ShapeDtypeStruct(x.shape, F32), jax.ShapeDtypeStruct(x.shape, MXU_DT)),
        compiler_params=_params(("parallel", "parallel")))(x, y, gt, g0, b0, g1, b1, sc, sh)


def _final_fwd_bwd(r1, y2, gt, g1, b1, g2, b2, target):
    bsz, t_total, _ = r1.shape
    tt = _div_tile(t_total, ROW_TILE)

    def body(r1_ref, y2_ref, gt_ref, g1_ref, b1_ref, g2_ref, b2_ref, tg_ref,
             loss_ref, dr2_ref, dy2_ref, dgt_ref, dg2_ref, db2_ref):
        b, t = pl.program_id(0), pl.program_id(1)

        @pl.when((b == 0) & (t == 0))
        def _():
            loss_ref[...] = jnp.zeros_like(loss_ref)
            dg2_ref[...] = jnp.zeros_like(dg2_ref)
            db2_ref[...] = jnp.zeros_like(db2_ref)

        @pl.when(t == 0)
        def _():
            dgt_ref[...] = jnp.zeros_like(dgt_ref)

        rh1, _ = _ln_stats(r1_ref[0])
        x1 = rh1 * g1_ref[...] + b1_ref[...]
        y2 = y2_ref[0].astype(F32)
        gate = 1.0 + gt_ref[0]
        xh2, rstd2 = _ln_stats(ALPHA * x1 + gate * y2)
        err = xh2 * g2_ref[...] + b2_ref[...] - tg_ref[0]
        loss_ref[...] += jnp.sum(err * err, axis=0, keepdims=True)
        dx2 = err * (1.0 / D)
        dg2_ref[...] += jnp.sum(dx2 * xh2, axis=0, keepdims=True)
        db2_ref[...] += jnp.sum(dx2, axis=0, keepdims=True)
        dr2 = _ln_bwd(dx2 * g2_ref[...], xh2, rstd2)
        dr2_ref[0] = dr2
        dy2_ref[0] = (gate * dr2).astype(MXU_DT)
        dgt_ref[0] += jnp.sum(dr2 * y2, axis=0, keepdims=True)

    vec_out = jax.ShapeDtypeStruct((1, D), F32)
    return pl.pallas_call(
        body, name="final_fwd_bwd", grid=(bsz, t_total // tt),
        in_specs=[_tok_spec(tt), _tok_spec(tt), _bvec_spec(), _vec_spec(), _vec_spec(), _vec_spec(), _vec_spec(),
                  _tok_spec(tt)],
        out_specs=(_vec_spec(), _tok_spec(tt), _tok_spec(tt), _bvec_spec(), _vec_spec(), _vec_spec()),
        out_shape=(vec_out, jax.ShapeDtypeStruct(r1.shape, F32), jax.ShapeDtypeStruct(r1.shape, MXU_DT),
                   jax.ShapeDtypeStruct((bsz, 1, D), F32), vec_out, vec_out),
        compiler_params=_params(("arbitrary", "arbitrary")))(r1, y2, gt, g1, b1, g2, b2, target)


def _ln_bwd_call(name, d_res, d_h, src, g, b, sc, y=None, gt=None):
    bsz, t_total, _ = src.shape
    tt = _div_tile(t_total, ROW_TILE)
    has_y = y is not None

    def body(*refs):
        if has_y:
            (dres_ref, dh_ref, src_ref, g_ref, b_ref, sc_ref, y_ref, gt_ref,
             dsrc_ref, dsc_ref, dsh_ref, dg_ref, db_ref, dy_ref, dgt_ref) = refs
        else:
            (dres_ref, dh_ref, src_ref, g_ref, b_ref, sc_ref,
             dsrc_ref, dsc_ref, dsh_ref, dg_ref, db_ref) = refs
        bi, t = pl.program_id(0), pl.program_id(1)

        @pl.when((bi == 0) & (t == 0))
        def _():
            dg_ref[...] = jnp.zeros_like(dg_ref)
            db_ref[...] = jnp.zeros_like(db_ref)

        @pl.when(t == 0)
        def _():
            dsc_ref[...] = jnp.zeros_like(dsc_ref)
            dsh_ref[...] = jnp.zeros_like(dsh_ref)
            if has_y:
                dgt_ref[...] = jnp.zeros_like(dgt_ref)

        xh, rstd = _ln_stats(src_ref[0])
        xv = xh * g_ref[...] + b_ref[...]
        dh = dh_ref[0].astype(F32)
        dx = ALPHA * dres_ref[0] + dh * (1.0 + sc_ref[0])
        dsc_ref[0] += jnp.sum(dh * xv, axis=0, keepdims=True)
        dsh_ref[0] += jnp.sum(dh, axis=0, keepdims=True)
        dg_ref[...] += jnp.sum(dx * xh, axis=0, keepdims=True)
        db_ref[...] += jnp.sum(dx, axis=0, keepdims=True)
        dsrc = _ln_bwd(dx * g_ref[...], xh, rstd)
        dsrc_ref[0] = dsrc
        if has_y:
            dy_ref[0] = ((1.0 + gt_ref[0]) * dsrc).astype(MXU_DT)
            dgt_ref[0] += jnp.sum(dsrc * y_ref[0].astype(F32), axis=0, keepdims=True)

    vec_out = jax.ShapeDtypeStruct((1, D), F32)
    bvec_out = jax.ShapeDtypeStruct((bsz, 1, D), F32)
    in_specs = [_tok_spec(tt), _tok_spec(tt), _tok_spec(tt), _vec_spec(), _vec_spec(), _bvec_spec()]
    out_specs = [_tok_spec(tt), _bvec_spec(), _bvec_spec(), _vec_spec(), _vec_spec()]
    out_shape = [jax.ShapeDtypeStruct(src.shape, F32), bvec_out, bvec_out, vec_out, vec_out]
    args = [d_res, d_h, src, g, b, sc]
    if has_y:
        in_specs += [_tok_spec(tt), _bvec_spec()]
        out_specs += [_tok_spec(tt), _bvec_spec()]
        out_shape += [jax.ShapeDtypeStruct(src.shape, MXU_DT), bvec_out]
        args += [y, gt]
    return pl.pallas_call(body, name=name, grid=(bsz, t_total // tt), in_specs=in_specs, out_specs=tuple(out_specs),
                          out_shape=tuple(out_shape), compiler_params=_params(("arbitrary", "arbitrary")))(*args)


FFN_TC = 256
FFN_NJ = D_FF // FFN_TC
FFN_PW = 2 * FFN_TC


def _ffn_pair(a, axis):
    shp = list(a.shape)
    a4 = a.reshape(shp[:axis] + [2, FFN_NJ, FFN_TC] + shp[axis + 1:])
    return jnp.swapaxes(a4, axis, axis + 1).reshape(shp)


def _ffn_unpair(a, axis):
    shp = list(a.shape)
    a4 = a.reshape(shp[:axis] + [FFN_NJ, 2, FFN_TC] + shp[axis + 1:])
    return jnp.swapaxes(a4, axis, axis + 1).reshape(shp)


def _ffn_up_act(h, w_up, cw, cb):
    bsz, t_total, _ = h.shape
    tt = _div_tile(t_total, FFN_ROW_TILE)
    def body(h_ref, wu_ref, w_ref, b_ref, up_ref, o_ref, carry_ref):
        up_t = _dot_nt(h_ref[0], wu_ref[...])
        up_ref[0] = up_t
        prev = jnp.where(pl.program_id(2) == 0, 0.0, carry_ref[...])
        rows = jnp.concatenate([prev, up_t], axis=0)
        u = _conv_rows(rows, w_ref, FFN_CONV_K)[HALO:] + b_ref[...]
        o_ref[0] = (_silu(u[:, :FFN_TC]) * u[:, FFN_TC:]).astype(MXU_DT)
        carry_ref[...] = up_t[tt - HALO:, :]

    return pl.pallas_call(
        body, name="ffn_up_act", grid=(bsz, FFN_NJ, t_total // tt),
        in_specs=[pl.BlockSpec((1, tt, D), lambda b, j, t: (b, t, 0)),
                  pl.BlockSpec((FFN_PW, D), lambda b, j, t: (j, 0)),
                  pl.BlockSpec((FFN_CONV_K, FFN_PW), lambda b, j, t: (0, j)),
                  pl.BlockSpec((1, FFN_PW), lambda b, j, t: (0, j))],
        out_specs=(pl.BlockSpec((1, tt, FFN_PW), lambda b, j, t: (b, t, j)),
                   pl.BlockSpec((1, tt, FFN_TC), lambda b, j, t: (b, t, j))),
        out_shape=(jax.ShapeDtypeStruct((bsz, t_total, 2 * D_FF), F32),
                   jax.ShapeDtypeStruct((bsz, t_total, D_FF), MXU_DT)),
        scratch_shapes=[pltpu.VMEM((HALO, FFN_PW), F32)],
        compiler_params=_params(("parallel", "parallel", "arbitrary")))(h, w_up, cw, cb)


HALO16 = 16


def _ffn_act_bwd(up, dy2, w_down, cw, cb):
    bsz, t_total, width = up.shape
    tt = _div_tile(t_total, FFN_ROW_TILE)
    nt = t_total // tt
    hp, hn = _halo_prev(tt), _halo_next(tt, t_total)

    def body(x_ref, xp_ref, xn_ref, dy_ref, dyn_ref, wd_ref, w_ref, b_ref, dup_ref, dw_ref, db_ref):
        b, t = pl.program_id(1), pl.program_id(2)

        @pl.when((b == 0) & (t == 0))
        def _():
            dw_ref[...] = jnp.zeros_like(dw_ref)
            db_ref[...] = jnp.zeros_like(db_ref)

        prev = jnp.where(t == 0, 0.0, xp_ref[0])
        rows = jnp.concatenate([prev, x_ref[0], xn_ref[0]], axis=0)
        u = _conv_rows(rows, w_ref, FFN_CONV_K)[HALO:] + b_ref[...]
        g_pre, v_pre = u[:, :FFN_TC], u[:, FFN_TC:]
        valid = (_iota((tt + HALO, 1), 0) < tt) | (t < nt - 1)
        da = jnp.concatenate([_dot_nt(dy_ref[0], wd_ref[...]), _dot_nt(dyn_ref[0], wd_ref[...])[:HALO]], axis=0)
        da_ext = jnp.where(valid, da, 0.0)
        sg = _sigmoid(g_pre)
        gs = g_pre * sg
        du = jnp.concatenate([da_ext * v_pre * (sg + gs * (1.0 - sg)), da_ext * gs], axis=1)
        dup = du * w_ref[FFN_CONV_K - 1:FFN_CONV_K, :]
        for s in range(1, FFN_CONV_K):
            dup = dup + _shift_up(du, s) * w_ref[FFN_CONV_K - 1 - s:FFN_CONV_K - s, :]
        dup_ref[0] = dup[:tt].astype(MXU_DT)
        du_t = du[:tt]
        db_ref[...] += jnp.sum(du_t, axis=0, keepdims=True)
        for k in range(FFN_CONV_K):
            s = FFN_CONV_K - 1 - k
            xs = (rows if s == 0 else pltpu.roll(rows, s, 0))[HALO:HALO + tt]
            dw_ref[k:k + 1, :] += jnp.sum(du_t * xs, axis=0, keepdims=True)

    def halo(h, w):
        return pl.BlockSpec((1, HALO, w), lambda j, b, t: (*h(b, t), j))

    wspec = lambda rows_: pl.BlockSpec((rows_, FFN_PW), lambda j, b, t: (0, j))
    tile = pl.BlockSpec((1, tt, FFN_PW), lambda j, b, t: (b, t, j))
    dy_next = lambda j, b, t: (b, jnp.minimum((t + 1) * (tt // HALO16), t_total // HALO16 - 1), 0)
    return pl.pallas_call(
        body, name="ffn_act_bwd", grid=(FFN_NJ, bsz, nt),
        in_specs=[tile, halo(hp, FFN_PW), halo(hn, FFN_PW),
                  pl.BlockSpec((1, tt, D), lambda j, b, t: (b, t, 0)), pl.BlockSpec((1, HALO16, D), dy_next),
                  pl.BlockSpec((FFN_TC, D), lambda j, b, t: (j, 0)), wspec(FFN_CONV_K), wspec(1)],
        out_specs=(tile, wspec(FFN_CONV_K), wspec(1)),
        out_shape=(jax.ShapeDtypeStruct(up.shape, MXU_DT), jax.ShapeDtypeStruct((FFN_CONV_K, width), F32),
                   jax.ShapeDtypeStruct((1, width), F32)),
        compiler_params=_params(("arbitrary", "arbitrary", "arbitrary")))(up, up, up, dy2, dy2, w_down, cw, cb)


QKV_W = 3 * HEADS * HD
SM_BLK = P_SM // 128


def _dn_pre_fwd(proj, conv_w, alog_row, dt_row):
    bsz, t_total, _ = proj.shape
    tt = _div_tile(t_total, ROW_TILE)
    hp = _halo_prev(tt)

    def body(x_ref, xp_ref, sm_ref, w_ref, al_ref, dt_ref, q_ref, k_ref, v_ref, g_ref):
        prev = jnp.where(pl.program_id(1) == 0, 0.0, xp_ref[0])
        y = _conv_rows(jnp.concatenate([prev, x_ref[0]], axis=0), w_ref, DN_CONV_K)[HALO:]
        q_ref[0], k_ref[0], v_ref[0] = _dn_qkv(y)
        g_ref[0] = _dn_gates(sm_ref[0], al_ref[...], dt_ref[...])

    out512 = jax.ShapeDtypeStruct((bsz, t_total, HEADS * HD), F32)
    return pl.pallas_call(
        body, name="dn_pre_fwd", grid=(bsz, t_total // tt),
        in_specs=[pl.BlockSpec((1, tt, QKV_W), lambda b, t: (b, t, 0)),
                  pl.BlockSpec((1, HALO, QKV_W), lambda b, t: (*hp(b, t), 0)),
                  pl.BlockSpec((1, tt, 128), lambda b, t: (b, t, SM_BLK)),
                  pl.BlockSpec((DN_CONV_K, QKV_W), lambda b, t: (0, 0)), _vec_spec(128), _vec_spec(128)],
        out_specs=(_tok_spec(tt, 512), _tok_spec(tt, 512), _tok_spec(tt, 512), _tok_spec(tt, 128)),
        out_shape=(out512, out512, out512, jax.ShapeDtypeStruct((bsz, t_total, 128), F32)),
        compiler_params=_params(("parallel", "parallel")))(proj, proj, proj, conv_w, alog_row, dt_row)


def _dn_pre_bwd(proj, dq, dk, dv, dgates, dsm_gla, conv_w, alog_row, dt_row):
    bsz, t_total, _ = proj.shape
    tt = _div_tile(t_total, 256)
    nt = t_total // tt
    hp, hn = _halo_prev(tt), _halo_next(tt, t_total)

    def body(x_ref, xp_ref, xn_ref, sm_ref, dq_ref, dqn_ref, dk_ref, dkn_ref, dv_ref, dvn_ref, dg_ref, dso_ref,
             w_ref, al_ref, dt_ref, dx_ref, dsm_ref, dw_ref, dal_ref, ddt_ref):
        b, t = pl.program_id(0), pl.program_id(1)

        @pl.when((b == 0) & (t == 0))
        def _():
            dw_ref[...] = jnp.zeros_like(dw_ref)
            dal_ref[...] = jnp.zeros_like(dal_ref)
            ddt_ref[...] = jnp.zeros_like(ddt_ref)

        prev = jnp.where(t == 0, 0.0, xp_ref[0])
        rows = jnp.concatenate([prev, x_ref[0], xn_ref[0]], axis=0)
        y = _conv_rows(rows, w_ref, DN_CONV_K)[HALO:]
        valid = (_iota((tt + HALO, 1), 0) < tt) | (t < nt - 1)

        def ext(tile_ref, next_ref):
            return jnp.where(valid, jnp.concatenate([tile_ref[0], next_ref[0]], axis=0), 0.0)

        _, vjp_qkv = jax.vjp(_dn_qkv, y)
        (dy,) = vjp_qkv((ext(dq_ref, dqn_ref), ext(dk_ref, dkn_ref), ext(dv_ref, dvn_ref)))
        dy = jnp.where(valid, dy, 0.0)
        dx = dy * w_ref[DN_CONV_K - 1:DN_CONV_K, :]
        for s in range(1, DN_CONV_K):
            dx = dx + _shift_up(dy, s) * w_ref[DN_CONV_K - 1 - s:DN_CONV_K - s, :]
        dx_ref[0] = dx[:tt].astype(MXU_DT)
        dy_t = dy[:tt]
        for k in range(DN_CONV_K):
            s = DN_CONV_K - 1 - k
            xs = (rows if s == 0 else pltpu.roll(rows, s, 0))[HALO:HALO + tt]
            dw_ref[k:k + 1, :] += jnp.sum(dy_t * xs, axis=0, keepdims=True)
        _, vjp_g = jax.vjp(_dn_gates, sm_ref[0], al_ref[...], dt_ref[...])
        dsm, dal, ddt = vjp_g(dg_ref[0])
        dsm_ref[0] = (dsm + dso_ref[0]).astype(MXU_DT)
        dal_ref[...] += dal
        ddt_ref[...] += ddt

    def tile(width, blk=0):
        return pl.BlockSpec((1, tt, width), lambda b, t: (b, t, blk))

    def halo(h, width):
        return pl.BlockSpec((1, HALO, width), lambda b, t: (*h(b, t), 0))

    return pl.pallas_call(
        body, name="dn_pre_bwd", grid=(bsz, nt),
        in_specs=[tile(QKV_W), halo(hp, QKV_W), halo(hn, QKV_W), tile(128, SM_BLK),
                  tile(512), halo(hn, 512), tile(512), halo(hn, 512), tile(512), halo(hn, 512), tile(128), tile(128),
                  pl.BlockSpec((DN_CONV_K, QKV_W), lambda b, t: (0, 0)), _vec_spec(128), _vec_spec(128)],
        out_specs=(tile(QKV_W), tile(128), pl.BlockSpec((DN_CONV_K, QKV_W), lambda b, t: (0, 0)),
                   _vec_spec(128), _vec_spec(128)),
        out_shape=(jax.ShapeDtypeStruct((bsz, t_total, QKV_W), MXU_DT),
                   jax.ShapeDtypeStruct((bsz, t_total, 128), MXU_DT),
                   jax.ShapeDtypeStruct((DN_CONV_K, QKV_W), F32), jax.ShapeDtypeStruct((1, 128), F32),
                   jax.ShapeDtypeStruct((1, 128), F32)),
        compiler_params=_params(("arbitrary", "arbitrary")))(
            proj, proj, proj, proj, dq, dq, dk, dk, dv, dv, dgates, dsm_gla, conv_w, alog_row, dt_row)


def _state_spec(bsz, idx):
    return pl.BlockSpec((bsz, 1, HEADS, HD, HD), lambda c: (0, idx(c), 0, 0, 0))


def _inv_spec(bsz, idx):
    return pl.BlockSpec((bsz, 1, HEADS, CHUNK, CHUNK), lambda c: (0, idx(c), 0, 0, 0))


def _chunk_spec(bsz, width, idx, blk=0):
    return pl.BlockSpec((bsz, CHUNK, width), lambda c: (0, idx(c), blk))


GQ_BLK, GK_BLK, GV_BLK = P_GQ // 512, P_GK // 512, P_GV // 512
REC_SHARES = (3, 1)


def _rec_fwd(q, k, v, gates, proj, w2, bg):
    bsz, t_total, _ = q.shape
    nc = t_total // CHUNK
    fwd = lambda c: c

    def body(q_ref, k_ref, v_ref, g_ref, gq_ref, gk_ref, gv_ref, sm_ref, w2_ref, bg_ref,
             o_ref, ss_ref, inv_ref, go_ref, gss_ref, s_ref, gs_ref):
        @pl.when(pl.program_id(0) == 0)
        def _():
            s_ref[...] = jnp.zeros_like(s_ref)
            gs_ref[...] = jnp.zeros_like(gs_ref)

        seqs = range(bsz)
        heads = range(HEADS)
        s_list = [[s_ref[b * HEADS + h] for h in heads] for b in seqs]
        gs_list = [[gs_ref[b * HEADS + h] for h in heads] for b in seqs]
        for b in seqs:
            for h in heads:
                ss_ref[b, 0, h] = s_list[b][h]
                gss_ref[b, 0, h] = gs_list[b][h]
        per_seq = lambda ref: [ref[b] for b in seqs]
        (o, new_s, invs), (go, new_gs) = _interleave(
            [_dn_chunk(s_list, per_seq(q_ref), per_seq(k_ref), per_seq(v_ref), per_seq(g_ref)),
             _gla_chunk(gs_list, per_seq(gq_ref), per_seq(gk_ref), per_seq(gv_ref), per_seq(sm_ref),
                        w2_ref[...], bg_ref[...])], REC_SHARES)
        for b in seqs:
            o_ref[b] = o[b]
            go_ref[b] = go[b]
            for h in heads:
                s_ref[b * HEADS + h] = new_s[b][h]
                gs_ref[b * HEADS + h] = new_gs[b][h]
                inv_ref[b, 0, h] = invs[b * HEADS + h]

    tok = lambda width, blk=0: _chunk_spec(bsz, width, fwd, blk)
    state = jax.ShapeDtypeStruct((bsz, nc, HEADS, HD, HD), F32)
    out512 = jax.ShapeDtypeStruct((bsz, t_total, 512), F32)
    return pl.pallas_call(
        body, name="rec_fwd", grid=(nc,),
        in_specs=[tok(512), tok(512), tok(512), tok(128),
                  tok(512, GQ_BLK), tok(512, GK_BLK), tok(512, GV_BLK), tok(128, SM_BLK),
                  pl.BlockSpec((128, 512), lambda c: (0, 0)), pl.BlockSpec((1, 512), lambda c: (0, 0))],
        out_specs=(tok(512), _state_spec(bsz, fwd), _inv_spec(bsz, fwd), tok(512), _state_spec(bsz, fwd)),
        out_shape=(out512, state, jax.ShapeDtypeStruct((bsz, nc, HEADS, CHUNK, CHUNK), F32), out512, state),
        scratch_shapes=[pltpu.VMEM((bsz * HEADS, HD, HD), F32), pltpu.VMEM((bsz * HEADS, HD, HD), F32)],
        compiler_params=_params(("arbitrary",)))(q, k, v, gates, proj, proj, proj, proj, w2, bg)


def _rec_bwd(q, k, v, gates, s_dn, inv_dn, do_dn, proj, w2, bg, s_gla, do_gla):
    bsz, t_total, _ = q.shape
    nc = t_total // CHUNK
    rev = lambda c: nc - 1 - c

    def body(q_ref, k_ref, v_ref, g_ref, ss_ref, inv_ref, do_ref,
             gq_ref, gk_ref, gv_ref, sm_ref, w2_ref, bg_ref, gss_ref, gdo_ref,
             dq_ref, dk_ref, dv_ref, dg_ref, dgq_ref, dgk_ref, dgv_ref, dsm_ref, dw2_ref, dbg_ref, ds_ref, gds_ref):
        @pl.when(pl.program_id(0) == 0)
        def _():
            ds_ref[...] = jnp.zeros_like(ds_ref)
            gds_ref[...] = jnp.zeros_like(gds_ref)
            dw2_ref[...] = jnp.zeros_like(dw2_ref)
            dbg_ref[...] = jnp.zeros_like(dbg_ref)

        seqs = range(bsz)
        heads = range(HEADS)
        per_seq = lambda ref: [ref[b] for b in seqs]
        known = [inv_ref[b, 0, h] for b in seqs for h in heads]

        def both(s_list, q_, k_, v_, g_, gs_list, gq_, gk_, gv_, sm_, w2_, bg_):
            (o, new_s, _), (go, new_gs) = _interleave(
                [_dn_chunk(s_list, q_, k_, v_, g_, inv_known=known),
                 _gla_chunk(gs_list, gq_, gk_, gv_, sm_, w2_, bg_)], REC_SHARES)
            return o, new_s, go, new_gs

        _, vjp = jax.vjp(both, [[ss_ref[b, 0, h] for h in heads] for b in seqs],
                         per_seq(q_ref), per_seq(k_ref), per_seq(v_ref), per_seq(g_ref),
                         [[gss_ref[b, 0, h] for h in heads] for b in seqs],
                         per_seq(gq_ref), per_seq(gk_ref), per_seq(gv_ref), per_seq(sm_ref), w2_ref[...], bg_ref[...])
        ds_in, dq, dk, dv, dg, gds_in, dgq, dgk, dgv, dsm, dw2, dbg = vjp(
            (per_seq(do_ref), [[ds_ref[b * HEADS + h] for h in heads] for b in seqs],
             per_seq(gdo_ref), [[gds_ref[b * HEADS + h] for h in heads] for b in seqs]))
        for b in seqs:
            dq_ref[b], dk_ref[b], dv_ref[b], dg_ref[b] = dq[b], dk[b], dv[b], dg[b]
            dgq_ref[b], dgk_ref[b], dgv_ref[b] = dgq[b].astype(MXU_DT), dgk[b].astype(MXU_DT), dgv[b].astype(MXU_DT)
            dsm_ref[b] = dsm[b]
            for h in heads:
                ds_ref[b * HEADS + h] = ds_in[b][h]
                gds_ref[b * HEADS + h] = gds_in[b][h]
        dw2_ref[...] += dw2
        dbg_ref[...] += dbg

    tok = lambda width, blk=0: _chunk_spec(bsz, width, rev, blk)
    w2_spec = pl.BlockSpec((128, 512), lambda c: (0, 0))
    bg_spec = pl.BlockSpec((1, 512), lambda c: (0, 0))
    f512 = jax.ShapeDtypeStruct((bsz, t_total, 512), F32)
    b512 = jax.ShapeDtypeStruct((bsz, t_total, 512), MXU_DT)
    f128 = jax.ShapeDtypeStruct((bsz, t_total, 128), F32)
    return pl.pallas_call(
        body, name="rec_bwd", grid=(nc,),
        in_specs=[tok(512), tok(512), tok(512), tok(128), _state_spec(bsz, rev), _inv_spec(bsz, rev), tok(512),
                  tok(512, GQ_BLK), tok(512, GK_BLK), tok(512, GV_BLK), tok(128, SM_BLK), w2_spec, bg_spec,
                  _state_spec(bsz, rev), tok(512)],
        out_specs=(tok(512), tok(512), tok(512), tok(128), tok(512), tok(512), tok(512), tok(128), w2_spec, bg_spec),
        out_shape=(f512, f512, f512, f128, b512, b512, b512, f128,
                   jax.ShapeDtypeStruct((128, 512), F32), jax.ShapeDtypeStruct((1, 512), F32)),
        scratch_shapes=[pltpu.VMEM((bsz * HEADS, HD, HD), F32), pltpu.VMEM((bsz * HEADS, HD, HD), F32)],
        compiler_params=_params(("arbitrary",)))(
            q, k, v, gates, s_dn, inv_dn, do_dn, proj, proj, proj, proj, w2, bg, s_gla, do_gla)


Z_BLK, GG_BLK = P_Z // 512, P_GG // 512


def _mix_out_fwd(o_dn, o_gla, proj, grow_dn, grow_gla):
    bsz, t_total, _ = o_dn.shape
    tt = _div_tile(t_total, ROW_TILE)

    def body(od_ref, og_ref, z_ref, gg_ref, gd_ref, gl_ref, o_ref):
        o_ref[0, :, :512] = _gate_norm(od_ref[0], z_ref[0], gd_ref[...]).astype(MXU_DT)
        o_ref[0, :, 512:] = _gate_norm(og_ref[0], gg_ref[0], gl_ref[...]).astype(MXU_DT)

    def col(blk):
        return pl.BlockSpec((1, tt, 512), lambda b, t: (b, t, blk))

    return pl.pallas_call(
        body, name="mix_out_fwd", grid=(bsz, t_total // tt),
        in_specs=[col(0), col(0), col(Z_BLK), col(GG_BLK), _vec_spec(512), _vec_spec(512)],
        out_specs=_tok_spec(tt), out_shape=jax.ShapeDtypeStruct((bsz, t_total, D), MXU_DT),
        compiler_params=_params(("parallel", "parallel")))(o_dn, o_gla, proj, proj, grow_dn, grow_gla)


def _mix_out_bwd(do, o_dn, o_gla, proj, grow_dn, grow_gla):
    bsz, t_total, _ = o_dn.shape
    tt = _div_tile(t_total, ROW_TILE)

    def body(do_ref, od_ref, og_ref, z_ref, gg_ref, gd_ref, gl_ref,
             dod_ref, dog_ref, dz_ref, dgg_ref, dgd_ref, dgl_ref):
        @pl.when((pl.program_id(0) == 0) & (pl.program_id(1) == 0))
        def _():
            dgd_ref[...] = jnp.zeros_like(dgd_ref)
            dgl_ref[...] = jnp.zeros_like(dgl_ref)

        def one(o_ref, gate_ref, g_ref, ct, do_out, dgate_out, dg_out):
            _, vjp = jax.vjp(_gate_norm, o_ref[0], gate_ref[0], g_ref[...])
            d_o, d_gate, d_row = vjp(ct)
            do_out[0] = d_o
            dgate_out[0] = d_gate.astype(MXU_DT)
            acc = d_row[:, :HD]
            for h in range(1, HEADS):
                acc = acc + d_row[:, h * HD:(h + 1) * HD]
            dg_out[...] += acc

        ct = do_ref[0].astype(F32)
        one(od_ref, z_ref, gd_ref, ct[:, :512], dod_ref, dz_ref, dgd_ref)
        one(og_ref, gg_ref, gl_ref, ct[:, 512:], dog_ref, dgg_ref, dgl_ref)

    def col(blk):
        return pl.BlockSpec((1, tt, 512), lambda b, t: (b, t, blk))

    f512 = jax.ShapeDtypeStruct((bsz, t_total, 512), F32)
    b512 = jax.ShapeDtypeStruct((bsz, t_total, 512), MXU_DT)
    g128 = jax.ShapeDtypeStruct((1, HD), F32)
    return pl.pallas_call(
        body, name="mix_out_bwd", grid=(bsz, t_total // tt),
        in_specs=[_tok_spec(tt), col(0), col(0), col(Z_BLK), col(GG_BLK), _vec_spec(512), _vec_spec(512)],
        out_specs=(col(0), col(0), col(0), col(0), _vec_spec(HD), _vec_spec(HD)),
        out_shape=(f512, f512, b512, b512, g128, g128),
        compiler_params=_params(("arbitrary", "arbitrary")))(do, o_dn, o_gla, proj, proj, grow_dn, grow_gla)


def _sum_slots(x, name):
    n, rows, cols = x.shape
    tr = _div_tile(rows, max(8, (1 << 19) // cols))

    def body(x_ref, o_ref):
        acc = x_ref[0].astype(F32)
        for i in range(1, n):
            acc = acc + x_ref[i].astype(F32)
        o_ref[...] = acc

    return pl.pallas_call(
        body, name=name, grid=(rows // tr,),
        in_specs=[pl.BlockSpec((n, tr, cols), lambda i: (0, i, 0))],
        out_specs=pl.BlockSpec((tr, cols), lambda i: (i, 0)),
        out_shape=jax.ShapeDtypeStruct((rows, cols), F32), compiler_params=_params(("parallel",)))(x)


def _adamw_math(w, g, m, v):
    nm = ADAM_B1 * m + (1.0 - ADAM_B1) * g
    nv = ADAM_B2 * v + (1.0 - ADAM_B2) * (g * g)
    m_hat = nm / (1.0 - ADAM_B1 ** ADAM_STEP)
    v_hat = nv / (1.0 - ADAM_B2 ** ADAM_STEP)
    return -ADAM_LR * (m_hat / (jnp.sqrt(v_hat) + ADAM_EPS) + ADAM_WD * w), nm, nv


def _adamw(w, g, m, v, name):
    _, rows, cols = w.shape
    tr = _div_tile(rows, max(8, (1 << 18) // cols))

    def body(w_ref, g_ref, m_ref, v_ref, d_ref, nm_ref, nv_ref):
        d_ref[...], nm_ref[...], nv_ref[...] = _adamw_math(w_ref[...], g_ref[...], m_ref[...], v_ref[...])

    spec = pl.BlockSpec((1, tr, cols), lambda i: (0, i, 0))
    shp = jax.ShapeDtypeStruct(w.shape, F32)
    return pl.pallas_call(body, name=name, grid=(rows // tr,), in_specs=[spec] * 4, out_specs=(spec,) * 3,
                          out_shape=(shp,) * 3, compiler_params=_params(("parallel",)))(w, g, m, v)


def _sum_adamw(parts, w, m, v, name):
    n, rows, cols = parts.shape
    tr = _div_tile(rows, max(8, (1 << 18) // cols))

    def body(p_ref, w_ref, m_ref, v_ref, g_ref, d_ref, nm_ref, nv_ref):
        g = p_ref[0].astype(F32)
        for i in range(1, n):
            g = g + p_ref[i].astype(F32)
        g_ref[...] = g
        d_ref[0], nm_ref[0], nv_ref[0] = _adamw_math(w_ref[0], g, m_ref[0], v_ref[0])

    spec = pl.BlockSpec((1, tr, cols), lambda i: (0, i, 0))
    shp = jax.ShapeDtypeStruct(w.shape, F32)
    return pl.pallas_call(
        body, name=name, grid=(rows // tr,),
        in_specs=[pl.BlockSpec((n, tr, cols), lambda i: (0, i, 0)), spec, spec, spec],
        out_specs=(pl.BlockSpec((tr, cols), lambda i: (i, 0)), spec, spec, spec),
        out_shape=(jax.ShapeDtypeStruct((rows, cols), F32), shp, shp, shp),
        compiler_params=_params(("parallel",)))(parts, w, m, v)


def _adamw_many(ws, gs, ms, vs, name):
    n = len(ws)

    def body(*refs):
        for i in range(n):
            d, nm, nv = _adamw_math(refs[i][...], refs[n + i][...], refs[2 * n + i][...], refs[3 * n + i][...])
            refs[4 * n + i][...] = d
            refs[5 * n + i][...] = nm
            refs[6 * n + i][...] = nv

    shapes = tuple(jax.ShapeDtypeStruct(w.shape, F32) for w in ws)
    outs = pl.pallas_call(body, name=name, out_shape=shapes * 3, compiler_params=_params())(*ws, *gs, *ms, *vs)
    return outs[:n], outs[n:2 * n], outs[2 * n:]


def _position():
    return lax.axis_index("x"), lax.axis_index("y"), lax.axis_index("c")


def _slot(px, py, pc):
    return 4 * px + 2 * py + pc


def _gather_small(x, name):
    rows, cols = x.shape

    def body(x_ref, o_ref, send_sems, recv_sems):
        mx, my, mc = _position()

        def peer(k):
            return (mx ^ ((k >> 2) & 1), my ^ ((k >> 1) & 1), mc ^ (k & 1))

        o_ref[_slot(mx, my, mc)] = x_ref[...]
        sends = []
        for k in range(1, N_DEV):
            cp = pltpu.make_async_remote_copy(src_ref=x_ref, dst_ref=o_ref.at[_slot(mx, my, mc)],
                                              send_sem=send_sems.at[k - 1], recv_sem=recv_sems.at[k - 1],
                                              device_id=peer(k), device_id_type=MESH)
            cp.start()
            sends.append(cp)
        for k in range(1, N_DEV):
            pltpu.make_async_remote_copy(src_ref=x_ref, dst_ref=o_ref.at[_slot(*peer(k))],
                                         send_sem=send_sems.at[k - 1], recv_sem=recv_sems.at[k - 1],
                                         device_id=peer(k), device_id_type=MESH).wait_recv()
        for cp in sends:
            cp.wait_send()

    return pl.pallas_call(
        body, name=name, out_shape=jax.ShapeDtypeStruct((N_DEV, rows, cols), x.dtype),
        in_specs=[pl.BlockSpec(memory_space=pltpu.VMEM)], out_specs=pl.BlockSpec(memory_space=pltpu.VMEM),
        scratch_shapes=[pltpu.SemaphoreType.DMA((N_DEV - 1,)), pltpu.SemaphoreType.DMA((N_DEV - 1,))],
        compiler_params=pltpu.CompilerParams(vmem_limit_bytes=VMEM_LIMIT_V7X))(x)


def _gather_big(shards):
    n = len(shards)

    def body(*refs):
        xs, outs = refs[:n], refs[n:2 * n]
        send_sems, recv_sems, local_sems = refs[2 * n:]
        mx, my, mc = _position()
        me, sibling = (mx, my, mc), (mx, my, 1 - mc)
        chips = [(1 - mx, my), (mx, 1 - my), (1 - mx, 1 - my)]

        def copy(a, k, block, to, src=None):
            dst = outs[a].at[_slot(*block)]
            return pltpu.make_async_remote_copy(src_ref=dst if src is None else src, dst_ref=dst,
                                                send_sem=send_sems.at[7 * a + k], recv_sem=recv_sems.at[7 * a + k],
                                                device_id=to, device_id_type=MESH)

        mine = [pltpu.make_async_copy(xs[a], outs[a].at[_slot(*me)], local_sems.at[a]) for a in range(n)]
        for cp in mine:
            cp.start()
        started = []
        for a in range(n):
            started.append(copy(a, 0, me, sibling, src=xs[a]))
            started += [copy(a, 1 + j, me, (*chip, mc), src=xs[a]) for j, chip in enumerate(chips)]
        for cp in started:
            cp.start()
        for j, chip in enumerate(chips):
            for a in range(n):
                copy(a, 1 + j, (*chip, mc), me).wait_recv()
                fwd = copy(a, 4 + j, (*chip, mc), sibling)
                fwd.start()
                started.append(fwd)
        for a in range(n):
            copy(a, 0, sibling, me).wait_recv()
            for j, chip in enumerate(chips):
                copy(a, 4 + j, (*chip, 1 - mc), me).wait_recv()
        for cp in started:
            cp.wait_send()
        for cp in mine:
            cp.wait()

    any_spec = pl.BlockSpec(memory_space=pl.ANY)
    return pl.pallas_call(
        body, name="gather_weights",
        out_shape=tuple(jax.ShapeDtypeStruct((N_DEV,) + s.shape, s.dtype) for s in shards),
        in_specs=[any_spec] * n, out_specs=(any_spec,) * n,
        scratch_shapes=[pltpu.SemaphoreType.DMA((7 * n,)), pltpu.SemaphoreType.DMA((7 * n,)),
                        pltpu.SemaphoreType.DMA((n,))])(*shards)


def _peer(pos, k):
    mx, my, mc = pos
    return (mx ^ ((k >> 2) & 1), my ^ ((k >> 1) & 1), mc ^ (k & 1))


def _exchange_copies(srcs, lands, send_sems, recv_sems, by_owner, arrivals):
    pos = _position()
    me = _slot(*pos)
    out = []
    for a, (src, land) in enumerate(zip(srcs, lands)):
        for k in range(1, N_DEV):
            peer = _peer(pos, k)
            mine = src.at[_slot(*peer)] if by_owner else src
            out.append(pltpu.make_async_remote_copy(
                src_ref=mine, dst_ref=land.at[_slot(*peer) if arrivals else me],
                send_sem=send_sems.at[7 * a + k - 1], recv_sem=recv_sems.at[7 * a + k - 1],
                device_id=peer, device_id_type=MESH))
    return out


_HBM_SPEC = pl.BlockSpec(memory_space=pltpu.HBM)
_SEM_SPEC = pl.BlockSpec(memory_space=pltpu.SEMAPHORE)
_DATAFLOW = pltpu.SideEffectType.DATAFLOW_SIDE_EFFECTING


def _exchange_start(name, srcs, slab_shapes, after, by_owner, carry=()):
    n, na, nc = len(srcs), len(after), len(carry)
    lands = [pltpu.with_memory_space_constraint(lax.empty((N_DEV,) + s, x.dtype), pltpu.HBM)
             for s, x in zip(slab_shapes, srcs)]
    thru = [pltpu.with_memory_space_constraint(x, pltpu.HBM) for x in [*srcs, *lands, *carry]]

    def body(*refs):
        src_refs, land_refs = refs[:n], refs[n:2 * n]
        send_sems, recv_sems = refs[len(thru) + na], refs[len(thru) + na + 1]
        token = refs[-1]
        for send in _exchange_copies(src_refs, land_refs, send_sems, recv_sems, by_owner, arrivals=False):
            send.start()
        token[...] = jnp.zeros_like(token)

    outs = pl.pallas_call(
        body, name=name,
        out_shape=(pltpu.SemaphoreType.DMA((7 * n,)), pltpu.SemaphoreType.DMA((7 * n,)),
                   *[pltpu.HBM(x.shape, x.dtype) for x in thru], jax.ShapeDtypeStruct((8, 128), F32)),
        in_specs=[_HBM_SPEC] * len(thru) + [pl.BlockSpec(memory_space=pl.ANY)] * na,
        out_specs=(_SEM_SPEC, _SEM_SPEC, *[_HBM_SPEC] * len(thru), pl.BlockSpec(memory_space=pltpu.VMEM)),
        input_output_aliases={i: 2 + i for i in range(len(thru))},
        compiler_params=pltpu.CompilerParams(has_side_effects=_DATAFLOW))(*thru, *after)
    return (outs[0], outs[1], list(outs[2:2 + n]), list(outs[2 + n:2 + 2 * n]), outs[-1],
            list(outs[2 + 2 * n:2 + 2 * n + nc]))


def _exchange_wait(name, send_sems, recv_sems, srcs, lands, after, by_owner):
    n = len(srcs)

    def body(*refs):
        src_refs, land_refs = refs[:n], refs[n:2 * n]
        s_sems, r_sems = refs[2 * n], refs[2 * n + 1]
        for send in _exchange_copies(src_refs, land_refs, s_sems, r_sems, by_owner, arrivals=False):
            send.wait_send()
        for recv in _exchange_copies(src_refs, land_refs, s_sems, r_sems, by_owner, arrivals=True):
            recv.wait_recv()

    outs = pl.pallas_call(
        body, name=name,
        out_shape=(*[pltpu.HBM(x.shape, x.dtype) for x in srcs], *[pltpu.HBM(l.shape, l.dtype) for l in lands]),
        in_specs=[_HBM_SPEC] * (2 * n) + [_SEM_SPEC, _SEM_SPEC, pl.BlockSpec(memory_space=pl.ANY)],
        out_specs=tuple([_HBM_SPEC] * (2 * n)),
        input_output_aliases={i: i for i in range(2 * n)},
        compiler_params=pltpu.CompilerParams(has_side_effects=_DATAFLOW))(*srcs, *lands, send_sems, recv_sems, after)
    return list(outs[:n]), list(outs[n:])


def _pad_heads(x, axis):
    shp = list(x.shape)
    x4 = x.reshape(shp[:axis] + [HEADS, GLA_KEY] + shp[axis + 1:])
    pad = [(0, 0)] * x4.ndim
    pad[axis + 1] = (0, HD - GLA_KEY)
    return jnp.pad(x4, pad).reshape(shp[:axis] + [HEADS * HD] + shp[axis + 1:])


def _unpad_heads(x, axis):
    shp = list(x.shape)
    x4 = x.reshape(shp[:axis] + [HEADS, HD] + shp[axis + 1:])
    x4 = lax.slice_in_dim(x4, 0, GLA_KEY, axis=axis + 1)
    return x4.reshape(shp[:axis] + [HEADS * GLA_KEY] + shp[axis + 1:])


O_Z_END, O_AB, O_GQ, O_GK, O_GV, O_R = 2048, 2048, 2056, 2312, 2568, 3592


def _padded_row(f):
    if f < O_Z_END:
        return f
    if f < O_GQ:
        return P_SM + (f - O_AB)
    if f < O_GV:
        base, g = (P_GQ, f - O_GQ) if f < O_GK else (P_GK, f - O_GK)
        return base + HD * (g // GLA_KEY) + g % GLA_KEY
    if f < O_R:
        return P_GV + (f - O_GV)
    return P_SM + 8 + (f - O_R)


def _runs(pairs):
    out = []
    for d, s in pairs:
        if out and out[-1][0] + out[-1][2] == d and out[-1][1] + out[-1][2] == s:
            out[-1][2] += 1
        else:
            out.append([d, s, 1])
    return out


def _pad_in_rows(shards):
    wt = shards.reshape(IN_W, D)
    return jnp.concatenate([
        wt[:O_Z_END], _pad_heads(wt[O_GQ:O_GK], 0), _pad_heads(wt[O_GK:O_GV], 0), wt[O_GV:O_R],
        wt[O_AB:O_GQ], wt[O_R:], jnp.zeros((P_W - P_SM - 8 - GATE_RANK, D), wt.dtype)], axis=0)


def _unpad_in_rows(gt):
    per = IN_W // N_DEV
    return jnp.stack([
        jnp.concatenate([gt[src:src + n] for _, src, n in
                         _runs([(f, _padded_row(f)) for f in range(j * per, (j + 1) * per)])], axis=0)
        for j in range(N_DEV)])


def _lane_row(vals, width=128):
    return jnp.pad(vals.reshape(1, -1), ((0, 0), (0, width - vals.size)))


SMALL_NAMES = ["ln0_g", "ln0_b", "b_ada", "dn_conv", "dn_a_log", "dn_dt_bias", "dn_norm_g", "gla_w_gate2",
               "gla_b_gate", "gla_norm_g", "ln1_g", "ln1_b", "ffn_conv", "ffn_conv_b", "ln2_g", "ln2_b"]
WEIGHTS = ["ln0_g", "ln0_b", "w_ada", "b_ada", "w_in", "dn_conv", "dn_a_log", "dn_dt_bias", "dn_norm_g",
           "gla_w_gate2", "gla_b_gate", "gla_norm_g", "w_o", "ln1_g", "ln1_b", "ffn_w_up", "ffn_conv", "ffn_conv_b",
           "ffn_w_down", "ln2_g", "ln2_b"]


def kernel(x, c, ln0_g, ln0_b, w_ada, b_ada, w_in, dn_conv, dn_a_log, dn_dt_bias, dn_norm_g, gla_w_gate2, gla_b_gate, gla_norm_g, w_o, ln1_g, ln1_b, ffn_w_up, ffn_conv, ffn_conv_b, ffn_w_down, ln2_g, ln2_b, loss_target, m_ln0_g, m_ln0_b, m_w_ada, m_b_ada, m_w_in, m_dn_conv, m_dn_a_log, m_dn_dt_bias, m_dn_norm_g, m_gla_w_gate2, m_gla_b_gate, m_gla_norm_g, m_w_o, m_ln1_g, m_ln1_b, m_ffn_w_up, m_ffn_conv, m_ffn_conv_b, m_ffn_w_down, m_ln2_g, m_ln2_b, v_ln0_g, v_ln0_b, v_w_ada, v_b_ada, v_w_in, v_dn_conv, v_dn_a_log, v_dn_dt_bias, v_dn_norm_g, v_gla_w_gate2, v_gla_b_gate, v_gla_norm_g, v_w_o, v_ln1_g, v_ln1_b, v_ffn_w_up, v_ffn_conv, v_ffn_conv_b, v_ffn_w_down, v_ln2_g, v_ln2_b):
    args = dict(locals())
    w_given = {n: args[n] for n in WEIGHTS}
    m_given = {n: args["m_" + n] for n in WEIGHTS}
    v_given = {n: args["v_" + n] for n in WEIGHTS}
    bsz, t_total, _ = x.shape
    ntok = bsz * t_total
    mx, my, mc = _position()
    me = _slot(mx, my, mc)

    pack1 = jnp.concatenate([c.reshape(-1), dn_conv.reshape(-1), gla_w_gate2.reshape(-1), ffn_conv.reshape(-1)])
    n1 = pack1.size
    rows1 = -(-n1 // 1024) * 8
    pack1 = jnp.pad(pack1, (0, rows1 * 128 - n1)).reshape(rows1, 128)
    got1 = _gather_small(pack1, "gather_cond").reshape(N_DEV, -1)
    o1 = bsz * D
    o2 = o1 + dn_conv.size
    o3 = o2 + gla_w_gate2.size
    c_all = got1[:, :o1].reshape(N_DEV * bsz, D)
    dn_conv_f = got1[:, o1:o2].reshape(N_DEV, DN_CONV_K, -1).transpose(1, 0, 2).reshape(DN_CONV_K, QKV_W)
    gate2_f = got1[:, o2:o3].reshape(N_DEV, GATE_RANK, -1).transpose(1, 0, 2).reshape(GATE_RANK, HEADS * GLA_KEY)
    ffn_conv_f = got1[:, o3:n1].reshape(N_DEV, FFN_CONV_K, -1).transpose(1, 0, 2).reshape(FFN_CONV_K, 2 * D_FF)

    win_t = w_in[0].T.astype(MXU_DT)
    wup_t = ffn_w_up[0].T.astype(MXU_DT)
    (win_all,) = _gather_big([win_t])
    win_p = _pad_in_rows(win_all)
    cw_p, cb_p = _ffn_pair(ffn_conv_f, 1), _ffn_pair(ffn_conv_b, 1)

    ncol = w_ada.shape[2]
    b_cols = lax.dynamic_slice_in_dim(b_ada, me * ncol, ncol, axis=1)
    mod_part = _ada_fwd(c_all, w_ada[0], b_cols)
    mod_all = _gather_small(mod_part.reshape(-1, 128), "gather_mod").reshape(N_DEV, N_DEV * bsz, ncol)
    mod = lax.dynamic_slice_in_dim(mod_all, me * bsz, bsz, axis=1).transpose(1, 0, 2).reshape(bsz, 6, 1, D)
    late = [w_o[0].astype(MXU_DT), wup_t, ffn_w_down[0].astype(MXU_DT)]
    ag_send, ag_recv, ag_src, ag_land, ag_token, _ = _exchange_start(
        "gather_start", late, [w.shape for w in late], [win_all, mod_all], by_owner=False)
    mod = mod + ag_token[0, 0]
    sh_a, sc_a, gt_a, sh_f, sc_f, gt_f = (mod[:, i] for i in range(6))

    g0, b0 = ln0_g.reshape(1, D), ln0_b.reshape(1, D)
    alog_row, dt_row = _lane_row(dn_a_log[0]), _lane_row(dn_dt_bias[0])
    grow_dn, grow_gla = jnp.tile(dn_norm_g, (1, HEADS)), jnp.tile(gla_norm_g, (1, HEADS))
    w2 = jnp.zeros((128, HEADS * HD), F32).at[SM_R:SM_R + GATE_RANK].set(_pad_heads(gate2_f, 1))
    bg = _pad_heads(gla_b_gate, 1)

    h_a = _ln0_mod(x, g0, b0, sc_a, sh_a)
    proj = _mm(h_a.reshape(ntok, D), win_p, "nt", F32, "mm_proj", tm=1024, tn=1408).reshape(bsz, t_total, P_W)
    q, k, v, gates = _dn_pre_fwd(proj, dn_conv_f, alog_row, dt_row)
    o_dn, s_dn, inv_dn, o_gla, s_gla = _rec_fwd(q, k, v, gates, proj, w2, bg)
    o_mix = _mix_out_fwd(o_dn, o_gla, proj, grow_dn, grow_gla)
    late, landed = _exchange_wait("gather_wait", ag_send, ag_recv, ag_src, ag_land, o_mix, by_owner=False)
    wo_all, wup_all, wdn_all = (lax.dynamic_update_slice(l, w[None], (me, 0, 0)) for l, w in zip(landed, late))
    wo_f = wo_all.reshape(D, D)
    wup_f = _ffn_pair(wup_all.reshape(2 * D_FF, D), 0)
    wdn_f = wdn_all.reshape(D_FF, D)
    y = _mm(o_mix.reshape(ntok, D), wo_f, "nn", MXU_DT, "mm_wo", tm=1024, tn=1024).reshape(bsz, t_total, D)
    r1, h_f = _res_ln_mod(x, y, gt_a, g0, b0, ln1_g, ln1_b, sc_f, sh_f)
    up, act = _ffn_up_act(h_f, wup_f, cw_p, cb_p)
    y2 = _mm(act.reshape(ntok, D_FF), wdn_f, "nn", MXU_DT, "mm_down", tm=1024, tn=1024).reshape(bsz, t_total, D)
    loss_rows, dr2, dy2, dgt_f, d_ln2_g, d_ln2_b = _final_fwd_bwd(r1, y2, gt_f, ln1_g, ln1_b, ln2_g, ln2_b, loss_target)
    loss_part = (0.5 / D) * jnp.sum(loss_rows)

    dy2_2 = dy2.reshape(ntok, D)
    g_wdn = _mm(act.reshape(ntok, D_FF), dy2_2, "tn", MXU_DT, "mm_gwdn", tm=1408, tn=1024)
    dup, d_cw_p, d_cb_p = _ffn_act_bwd(up, dy2, wdn_f, cw_p, cb_p)
    d_ffn_conv, d_ffn_conv_b = _ffn_unpair(d_cw_p, 1), _ffn_unpair(d_cb_p, 1)
    dup_2 = dup.reshape(ntok, 2 * D_FF)
    dh_f = _mm(dup_2, wup_f, "nn", MXU_DT, "mm_dhf", tn=1024).reshape(bsz, t_total, D)
    g_wup_t = _mm(dup_2, h_f.reshape(ntok, D), "tn", MXU_DT, "mm_gwup", tm=1408, tn=1024)
    ffn_parts = [_ffn_unpair(g_wup_t, 0).reshape(N_DEV, -1, D), g_wdn.reshape(N_DEV, -1, D)]
    rs_send, rs_recv, rs_src, rs_land, rs_token, _ = _exchange_start(
        "scatter_start", ffn_parts, [p.shape[1:] for p in ffn_parts], [dh_f], by_owner=True)
    dr1, dsc_f, dsh_f, d_ln1_g, d_ln1_b, dy, dgt_a = _ln_bwd_call(
        "ln1_bwd", dr2, dh_f, r1, ln1_g, ln1_b, sc_f + rs_token[0, 0], y=y, gt=gt_a)

    dy_2 = dy.reshape(ntok, D)
    do = _mm(dy_2, wo_f, "nt", MXU_DT, "mm_do", tm=1024, tn=1024).reshape(bsz, t_total, D)
    g_wo = _mm(o_mix.reshape(ntok, D), dy_2, "tn", MXU_DT, "mm_gwo", tm=512, tn=1024)
    do_dn, do_gla, dz, dgg, d_dn_norm, d_gla_norm = _mix_out_bwd(do, o_dn, o_gla, proj, grow_dn, grow_gla)
    dq, dk, dv, dgates, dgq, dgk, dgv, dsm_gla, d_w2, d_bg = _rec_bwd(
        q, k, v, gates, s_dn, inv_dn, do_dn, proj, w2, bg, s_gla, do_gla)
    dqkv, dsm, d_dn_conv, d_alog_row, d_dt_row = _dn_pre_bwd(
        proj, dq, dk, dv, dgates, dsm_gla, dn_conv_f, alog_row, dt_row)
    dproj = jnp.concatenate([dqkv, dz, dgq, dgk, dgv, dgg, dsm], axis=-1).reshape(ntok, P_W)
    g_win_p = _mm(dproj, h_a.reshape(ntok, D), "tn", MXU_DT, "mm_gwin", tm=1408, tn=1024)
    mix_parts = [_unpad_in_rows(g_win_p), g_wo.reshape(N_DEV, -1, D)]
    rs2_send, rs2_recv, rs2_src, rs2_land, rs2_token, (win_p_late,) = _exchange_start(
        "scatter_mix_start", mix_parts, [p.shape[1:] for p in mix_parts], [], by_owner=True, carry=[win_p])
    dh_a = _mm(dproj, win_p_late, "nn", MXU_DT, "mm_dha", tn=1024).reshape(bsz, t_total, D)
    grad_x, dsc_a, dsh_a, d_ln0_g, d_ln0_b = _ln_bwd_call(
        "ln0_bwd", dr1, dh_a, x, g0, b0, sc_a + rs2_token[0, 0])

    delta, new_m, new_v, big_grads = {}, {}, {}, {}
    flip = lambda a: jnp.swapaxes(a, 1, 2)

    def update_owned(n, landed, mine):
        parts = lax.dynamic_update_slice(landed, lax.dynamic_slice_in_dim(mine, me, 1, axis=0), (me, 0, 0))
        turn = flip if parts.shape[1:] != w_given[n].shape[1:] else (lambda a: a)
        g, d_, m_, v_ = _sum_adamw(parts, turn(w_given[n]), turn(m_given[n]), turn(v_given[n]), "adamw_" + n)
        big_grads[n], delta[n], new_m[n], new_v[n] = turn(g[None]), turn(d_), turn(m_), turn(v_)

    ffn_parts, ffn_landed = _exchange_wait("scatter_wait", rs_send, rs_recv, rs_src, rs_land, grad_x, by_owner=True)
    update_owned("ffn_w_up", ffn_landed[0], ffn_parts[0])
    update_owned("ffn_w_down", ffn_landed[1], ffn_parts[1])
    ffn_done = 0.0 * (new_v["ffn_w_up"][0, 0, 0] + new_v["ffn_w_down"][0, 0, 0])

    dmod = jnp.concatenate([dsh_a, dsc_a, dgt_a, dsh_f, dsc_f, dgt_f], axis=1).reshape(-1)
    small_parts = {
        "ln0_g": d_ln0_g, "ln0_b": d_ln0_b, "ln1_g": d_ln1_g, "ln1_b": d_ln1_b, "ln2_g": d_ln2_g, "ln2_b": d_ln2_b,
        "dn_a_log": d_alog_row[:, :HEADS], "dn_dt_bias": d_dt_row[:, :HEADS],
        "dn_norm_g": d_dn_norm, "gla_norm_g": d_gla_norm, "gla_b_gate": _unpad_heads(d_bg, 1),
        "ffn_conv_b": d_ffn_conv_b, "dn_conv": d_dn_conv,
        "gla_w_gate2": _unpad_heads(d_w2[SM_R:SM_R + GATE_RANK], 1), "ffn_conv": d_ffn_conv}
    order = sorted(small_parts)
    flat = jnp.concatenate([small_parts[n].reshape(-1) for n in order] + [(loss_part + ffn_done).reshape(1), dmod])
    n3 = flat.size
    rows3 = -(-n3 // 1024) * 8
    pack3 = jnp.pad(flat, (0, rows3 * 128 - n3)).reshape(rows3, 128)
    got3 = _gather_small(pack3, "gather_small_grads")
    tot3 = _sum_slots(got3, "sum_small_grads").reshape(-1)
    grads = {}
    off = 0
    for n in order:
        size = small_parts[n].size
        grads[n] = tot3[off:off + size]
        off += size
    loss = tot3[off]
    off += 1
    dmod_all = got3.reshape(N_DEV, -1)[:, off:off + dmod.size].reshape(N_DEV * bsz, 6 * D)
    dmod_cols = lax.dynamic_slice_in_dim(dmod_all, me * ncol, ncol, axis=1)
    g_wada, g_bada = _ada_bwd(c_all, dmod_all, dmod_cols)
    grads["b_ada"] = g_bada

    def col_shard(full, rows):
        part = full.reshape(rows, -1)
        width = part.shape[1] // N_DEV
        return lax.dynamic_slice_in_dim(part, me * width, width, axis=1)

    grads["dn_conv"] = col_shard(grads["dn_conv"], DN_CONV_K)
    grads["gla_w_gate2"] = col_shard(grads["gla_w_gate2"], GATE_RANK)
    grads["ffn_conv"] = col_shard(grads["ffn_conv"], FFN_CONV_K)
    grads = {n: g.reshape(w_given[n].shape) for n, g in grads.items()}
    mix_parts, mix_landed = _exchange_wait("scatter_mix_wait", rs2_send, rs2_recv, rs2_src, rs2_land, grad_x,
                                           by_owner=True)
    update_owned("w_in", mix_landed[0], mix_parts[0])
    update_owned("w_o", mix_landed[1], mix_parts[1])
    grads["w_ada"] = g_wada.reshape(w_ada.shape)
    delta["w_ada"], new_m["w_ada"], new_v["w_ada"] = _adamw(w_ada, grads["w_ada"], m_w_ada, v_w_ada, "adamw_w_ada")
    grads.update(big_grads)
    d_s, m_s, v_s = _adamw_many(*[[src[n] for n in SMALL_NAMES] for src in (w_given, grads, m_given, v_given)],
                                "adamw_small")
    for i, n in enumerate(SMALL_NAMES):
        delta[n], new_m[n], new_v[n] = d_s[i], m_s[i], v_s[i]

    return (loss, grad_x, *[grads[n] for n in WEIGHTS], *[delta[n] for n in WEIGHTS],
            *[new_m[n] for n in WEIGHTS], *[new_v[n] for n in WEIGHTS])
```

```python
import jax
import jax.numpy as jnp
from jax import lax
from jax.experimental import pallas as pl
from jax.experimental.pallas import tpu as pltpu

F32 = jnp.float32
MXU_DT = jnp.bfloat16
MESH = pl.DeviceIdType.MESH
N_DEV = 8

D = 1024
HEADS = 4
HD = 128
CHUNK = 64
GLA_KEY = 64
GLA_TAU = 16.0
GATE_RANK = 16
D_FF = 2816
IN_W = 3608
ALPHA = 2.0 ** 0.25
EPS = 1e-6
DN_CONV_K = 4
FFN_CONV_K = 3
HALO = 8
ROW_TILE = 1024
FFN_ROW_TILE = 1024

P_QKV, P_Z, P_GQ, P_GK, P_GV, P_GG, P_SM, P_W = 0, 1536, 2048, 2560, 3072, 3584, 4096, 4224
SM_A, SM_B, SM_R = 0, 4, 8

ADAM_LR, ADAM_B1, ADAM_B2, ADAM_EPS, ADAM_WD, ADAM_STEP = 0.001, 0.9, 0.999, 1e-08, 0.01, 10

VMEM_LIMIT_V7X = 56 * 1024 * 1024


def _params(sem=None):
    return pltpu.CompilerParams(dimension_semantics=sem, vmem_limit_bytes=VMEM_LIMIT_V7X)


NN, NT, TN = ((1,), (0,)), ((1,), (1,)), ((0,), (0,))


def _dg(a, b, dims):
    return lax.dot_general(a, b, (dims, ((), ())), preferred_element_type=F32)


def _dot(a, b):
    return _dg(a, b, NN)


def _dot_nt(a, b):
    return _dg(a, b, NT)


def _dot_tn(a, b):
    return _dg(a, b, TN)


def _iota(shape, dim):
    return lax.broadcasted_iota(jnp.int32, shape, dim)


def _sigmoid(x):
    return jax.nn.sigmoid(x)


def _silu(x):
    return x * _sigmoid(x)


def _softplus(x):
    return jnp.maximum(x, 0.0) + jnp.log(1.0 + jnp.exp(-jnp.abs(x)))


def _ln_stats(x):
    mu = jnp.mean(x, axis=-1, keepdims=True)
    xc = x - mu
    rstd = lax.rsqrt(jnp.mean(xc * xc, axis=-1, keepdims=True) + EPS)
    return xc * rstd, rstd


def _ln_bwd(dxhat, xhat, rstd):
    return rstd * (dxhat - jnp.mean(dxhat, axis=-1, keepdims=True)
                   - xhat * jnp.mean(dxhat * xhat, axis=-1, keepdims=True))


def _split2(a):
    hi = a.astype(jnp.bfloat16)
    return hi, (a - hi.astype(F32)).astype(jnp.bfloat16)


def _d3(a, b, dims):
    ah, al = _split2(a)
    bh, bl = _split2(b)
    return _dg(ah, bh, dims) + (_dg(ah, bl, dims) + _dg(al, bh, dims))


@jax.custom_vjp
def _dot3(a, b):
    return _d3(a, b, NN)


_dot3.defvjp(lambda a, b: (_d3(a, b, NN), (a, b)),
             lambda res, g: (_d3(g, res[1], NT), _d3(res[0], g, TN)))


def _split3(b):
    b1 = b.astype(jnp.bfloat16)
    r1 = b - b1.astype(F32)
    b2 = r1.astype(jnp.bfloat16)
    return b1, b2, (r1 - b2.astype(F32)).astype(jnp.bfloat16)


def _sum3(fn, b):
    b1, b2, b3 = _split3(b)
    return fn(b1) + (fn(b2) + fn(b3))


@jax.custom_vjp
def _mask_dot(e, b):
    return _sum3(lambda t: _dg(e, t, NN), b)


_mask_dot.defvjp(lambda e, b: (_mask_dot(e, b), e),
                 lambda e, g: (jnp.zeros_like(e), _sum3(lambda t: _dg(e, t, TN), g)))


@jax.custom_vjp
def _mask_dot_nt(e, b):
    return _sum3(lambda t: _dg(e, t, NT), b)


_mask_dot_nt.defvjp(lambda e, b: (_mask_dot_nt(e, b), e),
                    lambda e, g: (jnp.zeros_like(e), _sum3(lambda t: _dg(t, e, TN), g)))


def _interleave(gens, shares):
    results = [None] * len(gens)
    live = list(range(len(gens)))
    while live:
        for i in list(live):
            for _ in range(shares[i]):
                try:
                    next(gens[i])
                except StopIteration as done:
                    results[i] = done.value
                    live.remove(i)
                    break
    return results


def _tri_inv_stages(ms):
    n = ms[0].shape[0]
    r, c = _iota((n, n), 0), _iota((n, n), 1)
    eye = (r == c).astype(F32)
    diag = (r >> 3) == (c >> 3)
    ds = [jnp.where(diag, m, 0.0) for m in ms]
    d2s = [_d3(d, d, NN) for d in ds]
    yield
    d4s = [_d3(d2, d2, NN) for d2 in d2s]
    invs = [_d3(eye - d, eye + d2, NN) for d, d2 in zip(ds, d2s)]
    yield
    invs = [_d3(inv, eye + d4, NN) for inv, d4 in zip(invs, d4s)]
    yield
    shift = 3
    while (1 << shift) < n:
        rb, cb = r >> shift, c >> shift
        sel = ((rb & 1) == 1) & (cb == rb - 1)
        tmp = [_d3(inv, jnp.where(sel, m, 0.0), NN) for inv, m in zip(invs, ms)]
        yield
        invs = [inv - _d3(t, inv, NN) for t, inv in zip(tmp, invs)]
        yield
        shift += 1
    return invs


def _tri_inv_bwd(invs, das):
    tmp = [_d3(a, da, TN) for a, da in zip(invs, das)]
    return ([-_d3(t, a, NT) for t, a in zip(tmp, invs)],)


@jax.custom_vjp
def _tri_inv_known(ms, invs):
    return invs


_tri_inv_known.defvjp(lambda ms, invs: (invs, invs),
                      lambda invs, das: (_tri_inv_bwd(invs, das)[0], [jnp.zeros_like(a) for a in invs]))


def _dn_chunk(s_list, q, k, v, gates, inv_known=None):
    nb = len(q)
    c = q[0].shape[0]
    r64, c64 = _iota((c, c), 0), _iota((c, c), 1)
    causal = r64 >= c64
    strict = r64 > c64
    tri = causal.astype(jnp.bfloat16)
    eye = (_iota((HD, HD), 0) == _iota((HD, HD), 1)).astype(jnp.bfloat16)
    lane = _iota(gates[0].shape, 1)
    lane1 = _iota((1, HD), 1)
    g_all = [_mask_dot(tri, g) for g in gates]
    yield
    g_all_t = [_mask_dot_nt(eye, g) for g in g_all]
    yield
    row = _iota(g_all_t[0].shape, 0)
    last = [jnp.sum(g, axis=0, keepdims=True) for g in gates]
    prob = [(b, h) for b in range(nb) for h in range(HEADS)]
    sl = [slice(h * HD, (h + 1) * HD) for h in range(HEADS)]
    qh = [q[b][:, sl[h]] for b, h in prob]
    kh = [k[b][:, sl[h]] for b, h in prob]
    vh = [v[b][:, sl[h]] for b, h in prob]
    s = [s_list[b][h] for b, h in prob]
    beta = [jnp.sum(jnp.where(lane == SM_B + h, gates[b], 0.0), axis=-1, keepdims=True) for b, h in prob]
    g_c = [jnp.sum(jnp.where(lane == SM_A + h, g_all[b], 0.0), axis=-1, keepdims=True) for b, h in prob]
    g_r = [jnp.sum(jnp.where(row == SM_A + h, g_all_t[b], 0.0), axis=0, keepdims=True) for b, h in prob]
    g_last = [jnp.sum(jnp.where(lane1 == SM_A + h, last[b], 0.0), axis=-1, keepdims=True) for b, h in prob]
    decay = [jnp.where(causal, jnp.exp(jnp.where(causal, gc - gr, 0.0)), 0.0) for gc, gr in zip(g_c, g_r)]
    kb = [k_ * b_ for k_, b_ in zip(kh, beta)]
    m_low = [jnp.where(strict, _dot_nt(kb_, k_) * d_, 0.0) for kb_, k_, d_ in zip(kb, kh, decay)]
    yield
    attn = [_dot_nt(q_, k_) * d_ for q_, k_, d_ in zip(qh, kh, decay)]
    yield
    if inv_known is None:
        a_inv = yield from _tri_inv_stages(m_low)
    else:
        a_inv = _tri_inv_known(m_low, inv_known)
    eg = [jnp.exp(gc) for gc in g_c]
    uw = [_dot3(a_, jnp.concatenate([v_ * b_, kb_ * e_], axis=1))
          for a_, v_, b_, kb_, e_ in zip(a_inv, vh, beta, kb, eg)]
    yield
    v_new = [uw_[:, :HD] - _dot(uw_[:, HD:], s_) for uw_, s_ in zip(uw, s)]
    yield
    qs = [_dot(q_ * e_, s_) for q_, e_, s_ in zip(qh, eg, s)]
    yield
    o = [qs_ + _dot(a_, vn_) for qs_, a_, vn_ in zip(qs, attn, v_new)]
    yield
    k_dec = [k_ * jnp.exp(gl - gc) for k_, gl, gc in zip(kh, g_last, g_c)]
    s_new = [s_ * jnp.exp(gl) + _dot_tn(kd_, vn_) for s_, gl, kd_, vn_ in zip(s, g_last, k_dec, v_new)]
    outs = [jnp.concatenate(o[b * HEADS:(b + 1) * HEADS], axis=-1) for b in range(nb)]
    states = [s_new[b * HEADS:(b + 1) * HEADS] for b in range(nb)]
    return outs, states, a_inv


def _gla_chunk(st_list, q, k, v, small, w2, bg):
    nb = len(q)
    c = q[0].shape[0]
    causal = _iota((c, c), 0) >= _iota((c, c), 1)
    tri = causal.astype(jnp.bfloat16)
    la_all = [-_softplus(-(_dot(sm, w2) + bg)) * (1.0 / GLA_TAU) for sm in small]
    yield
    b_all = [_mask_dot(tri, la) for la in la_all]
    yield
    prob = [(b, h) for b in range(nb) for h in range(HEADS)]
    sl = [slice(h * HD, (h + 1) * HD) for h in range(HEADS)]
    kh = [k[b][:, sl[h]] for b, h in prob]
    vh = [v[b][:, sl[h]] for b, h in prob]
    st = [st_list[b][h] for b, h in prob]
    bc = [b_all[b][:, sl[h]] for b, h in prob]
    b_last = [jnp.sum(la_all[b][:, sl[h]], axis=0, keepdims=True) for b, h in prob]
    q_dec = [q[b][:, sl[h]] * (GLA_KEY ** -0.5) * jnp.exp(bc_) for (b, h), bc_ in zip(prob, bc)]
    attn = [jnp.where(causal, _dot_nt(qd, k_ * jnp.exp(-bc_)), 0.0) for qd, k_, bc_ in zip(q_dec, kh, bc)]
    yield
    inter = [_dot_nt(qd, st_) for qd, st_ in zip(q_dec, st)]
    yield
    o = [i_ + _dot(a_, v_) for i_, a_, v_ in zip(inter, attn, vh)]
    yield
    k_dec = [k_ * jnp.exp(bl - bc_) for k_, bl, bc_ in zip(kh, b_last, bc)]
    s_new = [st_ * jnp.exp(bl) + _dot_tn(v_, kd) for st_, bl, v_, kd in zip(st, b_last, vh, k_dec)]
    outs = [jnp.concatenate(o[b * HEADS:(b + 1) * HEADS], axis=-1) for b in range(nb)]
    return outs, [s_new[b * HEADS:(b + 1) * HEADS] for b in range(nb)]


def _dn_qkv(y):
    act = _silu(y)
    parts = []
    for i in range(2 * HEADS):
        xh = act[:, i * HD:(i + 1) * HD]
        xh = xh * lax.rsqrt(jnp.sum(xh * xh, axis=-1, keepdims=True) + EPS)
        parts.append(xh * (HD ** -0.5) if i < HEADS else xh)
    qk = jnp.concatenate(parts, axis=-1)
    return qk[:, :HEADS * HD], qk[:, HEADS * HD:], act[:, 2 * HEADS * HD:]


def _dn_gates(small, alog_row, dt_row):
    lane = _iota(small.shape, 1)
    log_a = -jnp.exp(alog_row) * _softplus(small + dt_row)
    return jnp.where(lane < SM_B, log_a, jnp.where(lane < SM_R, _sigmoid(small), 0.0))


def _gate_norm(o, z, grow):
    parts = []
    for h in range(HEADS):
        oh = o[:, h * HD:(h + 1) * HD]
        parts.append(oh * lax.rsqrt(jnp.mean(oh * oh, axis=-1, keepdims=True) + EPS))
    return jnp.concatenate(parts, axis=-1) * grow * _silu(z)


def _conv_rows(xrows, w_ref, k_taps):
    n = xrows.shape[0]
    acc = xrows * w_ref[k_taps - 1:k_taps, :]
    for s in range(1, k_taps):
        acc = acc + pltpu.roll(xrows, s, 0) * w_ref[k_taps - 1 - s:k_taps - s, :]
    return acc


def _shift_up(x, s):
    return x if s == 0 else pltpu.roll(x, x.shape[0] - s, 0)


def _div_tile(n, cap, mult=8):
    best = None
    for t in range(mult, min(n, cap) + 1, mult):
        if n % t == 0:
            best = t
    return best if best is not None else n


def _halo_prev(tt):
    return lambda b, t: (b, jnp.maximum(t * (tt // HALO) - 1, 0))


def _halo_next(tt, t_total):
    return lambda b, t: (b, jnp.minimum((t + 1) * (tt // HALO), t_total // HALO - 1))


def _mm(a, b, mode, out_dtype, name, tm=512, tn=512, tk=None):
    if mode == "nn":
        (m, k), n = a.shape, b.shape[1]
    elif mode == "nt":
        (m, k), n = a.shape, b.shape[0]
    else:
        (k, m), n = a.shape, b.shape[1]
    tm, tn = min(tm, m), min(tn, n)
    tk = k if tk is None else min(tk, k)
    assert m % tm == 0 and n % tn == 0 and k % tk == 0, (name, a.shape, b.shape, tm, tn, tk)
    nk = k // tk
    if mode == "tn":
        a_spec = pl.BlockSpec((tk, tm), lambda i, j, kk: (kk, i))
    else:
        a_spec = pl.BlockSpec((tm, tk), lambda i, j, kk: (i, kk))
    if mode == "nt":
        b_spec = pl.BlockSpec((tn, tk), lambda i, j, kk: (j, kk))
    else:
        b_spec = pl.BlockSpec((tk, tn), lambda i, j, kk: (kk, j))
    dims = {"nn": NN, "nt": NT, "tn": TN}[mode]

    def body(a_ref, b_ref, o_ref, *acc):
        p = _dg(a_ref[...], b_ref[...], dims)
        if nk == 1:
            o_ref[...] = p.astype(out_dtype)
        else:
            kk = pl.program_id(2)

            @pl.when(kk == 0)
            def _():
                acc[0][...] = p

            @pl.when(kk > 0)
            def _():
                acc[0][...] += p

            @pl.when(kk == nk - 1)
            def _():
                o_ref[...] = acc[0][...].astype(out_dtype)

    return pl.pallas_call(
        body, name=name, grid=(m // tm, n // tn, nk),
        in_specs=[a_spec, b_spec],
        out_specs=pl.BlockSpec((tm, tn), lambda i, j, kk: (i, j)),
        out_shape=jax.ShapeDtypeStruct((m, n), out_dtype),
        scratch_shapes=[pltpu.VMEM((tm, tn), F32)] if nk > 1 else [],
        compiler_params=_params(("parallel", "parallel", "arbitrary")),
    )(a, b)


def _ada_fwd(c_all, w_ada, b_cols):
    def body(c_ref, w_ref, b_ref, o_ref):
        cond = _silu(c_ref[...]).astype(MXU_DT)
        o_ref[...] = _dot(cond, w_ref[...].astype(MXU_DT)) + b_ref[...]

    return pl.pallas_call(body, name="ada_fwd", out_shape=jax.ShapeDtypeStruct((c_all.shape[0], w_ada.shape[1]), F32),
                          compiler_params=_params())(c_all, w_ada, b_cols)


def _ada_bwd(c_all, dmod_all, dmod_cols):
    def body(c_ref, da_ref, dc_ref, gw_ref, gb_ref):
        cond = _silu(c_ref[...]).astype(MXU_DT)
        gw_ref[...] = _dot_tn(cond, dc_ref[...].astype(MXU_DT))
        gb_ref[...] = jnp.sum(da_ref[...], axis=0, keepdims=True)

    return pl.pallas_call(
        body, name="ada_bwd",
        out_shape=(jax.ShapeDtypeStruct((c_all.shape[1], dmod_cols.shape[1]), F32),
                   jax.ShapeDtypeStruct((1, dmod_all.shape[1]), F32)),
        compiler_params=_params())(c_all, dmod_all, dmod_cols)


def _tok_spec(tt, width=D):
    return pl.BlockSpec((1, tt, width), lambda b, t: (b, t, 0))


def _vec_spec(width=D):
    return pl.BlockSpec((1, width), lambda b, t: (0, 0))


def _bvec_spec(width=D):
    return pl.BlockSpec((1, 1, width), lambda b, t: (b, 0, 0))


def _ln0_mod(x, g0, b0, sc, sh):
    bsz, t_total, _ = x.shape
    tt = _div_tile(t_total, ROW_TILE)

    def body(x_ref, g_ref, b_ref, sc_ref, sh_ref, h_ref):
        xh, _ = _ln_stats(x_ref[0])
        x0 = xh * g_ref[...] + b_ref[...]
        h_ref[0] = (x0 * (1.0 + sc_ref[0]) + sh_ref[0]).astype(MXU_DT)

    return pl.pallas_call(
        body, name="ln0_mod", grid=(bsz, t_total // tt),
        in_specs=[_tok_spec(tt), _vec_spec(), _vec_spec(), _bvec_spec(), _bvec_spec()],
        out_specs=_tok_spec(tt), out_shape=jax.ShapeDtypeStruct(x.shape, MXU_DT),
        compiler_params=_params(("parallel", "parallel")))(x, g0, b0, sc, sh)


def _res_ln_mod(x, y, gt, g0, b0, g1, b1, sc, sh):
    bsz, t_total, _ = x.shape
    tt = _div_tile(t_total, ROW_TILE)

    def body(x_ref, y_ref, gt_ref, g0_ref, b0_ref, g1_ref, b1_ref, sc_ref, sh_ref, r_ref, h_ref):
        xh, _ = _ln_stats(x_ref[0])
        r = ALPHA * (xh * g0_ref[...] + b0_ref[...]) + (1.0 + gt_ref[0]) * y_ref[0].astype(F32)
        r_ref[0] = r
        rh, _ = _ln_stats(r)
        x1 = rh * g1_ref[...] + b1_ref[...]
        h_ref[0] = (x1 * (1.0 + sc_ref[0]) + sh_ref[0]).astype(MXU_DT)

    return pl.pallas_call(
        body, name="res_ln_mod", grid=(bsz, t_total // tt),
        in_specs=[_tok_spec(tt), _tok_spec(tt), _bvec_spec(), _vec_spec(), _vec_spec(), _vec_spec(), _vec_spec(),
                  _bvec_spec(), _bvec_spec()],
        out_specs=(_tok_spec(tt), _tok_spec(tt)),
        out_shape=(jax.ShapeDtypeStruct(x.shape, F32), jax.ShapeDtypeStruct(x.shape, MXU_DT)),
        compiler_params=_params(("parallel", "parallel")))(x, y, gt, g0, b0, g1, b1, sc, sh)


def _final_fwd_bwd(r1, y2, gt, g1, b1, g2, b2, target):
    bsz, t_total, _ = r1.shape
    tt = _div_tile(t_total, ROW_TILE)

    def body(r1_ref, y2_ref, gt_ref, g1_ref, b1_ref, g2_ref, b2_ref, tg_ref,
             loss_ref, dr2_ref, dy2_ref, dgt_ref, dg2_ref, db2_ref):
        b, t = pl.program_id(0), pl.program_id(1)

        @pl.when((b == 0) & (t == 0))
        def _():
            loss_ref[...] = jnp.zeros_like(loss_ref)
            dg2_ref[...] = jnp.zeros_like(dg2_ref)
            db2_ref[...] = jnp.zeros_like(db2_ref)

        @pl.when(t == 0)
        def _():
            dgt_ref[...] = jnp.zeros_like(dgt_ref)

        rh1, _ = _ln_stats(r1_ref[0])
        x1 = rh1 * g1_ref[...] + b1_ref[...]
        y2 = y2_ref[0].astype(F32)
        gate = 1.0 + gt_ref[0]
        xh2, rstd2 = _ln_stats(ALPHA * x1 + gate * y2)
        err = xh2 * g2_ref[...] + b2_ref[...] - tg_ref[0]
        loss_ref[...] += jnp.sum(err * err, axis=0, keepdims=True)
        dx2 = err * (1.0 / D)
        dg2_ref[...] += jnp.sum(dx2 * xh2, axis=0, keepdims=True)
        db2_ref[...] += jnp.sum(dx2, axis=0, keepdims=True)
        dr2 = _ln_bwd(dx2 * g2_ref[...], xh2, rstd2)
        dr2_ref[0] = dr2
        dy2_ref[0] = (gate * dr2).astype(MXU_DT)
        dgt_ref[0] += jnp.sum(dr2 * y2, axis=0, keepdims=True)

    vec_out = jax.ShapeDtypeStruct((1, D), F32)
    return pl.pallas_call(
        body, name="final_fwd_bwd", grid=(bsz, t_total // tt),
        in_specs=[_tok_spec(tt), _tok_spec(tt), _bvec_spec(), _vec_spec(), _vec_spec(), _vec_spec(), _vec_spec(),
                  _tok_spec(tt)],
        out_specs=(_vec_spec(), _tok_spec(tt), _tok_spec(tt), _bvec_spec(), _vec_spec(), _vec_spec()),
        out_shape=(vec_out, jax.ShapeDtypeStruct(r1.shape, F32), jax.ShapeDtypeStruct(r1.shape, MXU_DT),
                   jax.ShapeDtypeStruct((bsz, 1, D), F32), vec_out, vec_out),
        compiler_params=_params(("arbitrary", "arbitrary")))(r1, y2, gt, g1, b1, g2, b2, target)


def _ln_bwd_call(name, d_res, d_h, src, g, b, sc, y=None, gt=None):
    bsz, t_total, _ = src.shape
    tt = _div_tile(t_total, ROW_TILE)
    has_y = y is not None

    def body(*refs):
        if has_y:
            (dres_ref, dh_ref, src_ref, g_ref, b_ref, sc_ref, y_ref, gt_ref,
             dsrc_ref, dsc_ref, dsh_ref, dg_ref, db_ref, dy_ref, dgt_ref) = refs
        else:
            (dres_ref, dh_ref, src_ref, g_ref, b_ref, sc_ref,
             dsrc_ref, dsc_ref, dsh_ref, dg_ref, db_ref) = refs
        bi, t = pl.program_id(0), pl.program_id(1)

        @pl.when((bi == 0) & (t == 0))
        def _():
            dg_ref[...] = jnp.zeros_like(dg_ref)
            db_ref[...] = jnp.zeros_like(db_ref)

        @pl.when(t == 0)
        def _():
            dsc_ref[...] = jnp.zeros_like(dsc_ref)
            dsh_ref[...] = jnp.zeros_like(dsh_ref)
            if has_y:
                dgt_ref[...] = jnp.zeros_like(dgt_ref)

        xh, rstd = _ln_stats(src_ref[0])
        xv = xh * g_ref[...] + b_ref[...]
        dh = dh_ref[0].astype(F32)
        dx = ALPHA * dres_ref[0] + dh * (1.0 + sc_ref[0])
        dsc_ref[0] += jnp.sum(dh * xv, axis=0, keepdims=True)
        dsh_ref[0] += jnp.sum(dh, axis=0, keepdims=True)
        dg_ref[...] += jnp.sum(dx * xh, axis=0, keepdims=True)
        db_ref[...] += jnp.sum(dx, axis=0, keepdims=True)
        dsrc = _ln_bwd(dx * g_ref[...], xh, rstd)
        dsrc_ref[0] = dsrc
        if has_y:
            dy_ref[0] = ((1.0 + gt_ref[0]) * dsrc).astype(MXU_DT)
            dgt_ref[0] += jnp.sum(dsrc * y_ref[0].astype(F32), axis=0, keepdims=True)

    vec_out = jax.ShapeDtypeStruct((1, D), F32)
    bvec_out = jax.ShapeDtypeStruct((bsz, 1, D), F32)
    in_specs = [_tok_spec(tt), _tok_spec(tt), _tok_spec(tt), _vec_spec(), _vec_spec(), _bvec_spec()]
    out_specs = [_tok_spec(tt), _bvec_spec(), _bvec_spec(), _vec_spec(), _vec_spec()]
    out_shape = [jax.ShapeDtypeStruct(src.shape, F32), bvec_out, bvec_out, vec_out, vec_out]
    args = [d_res, d_h, src, g, b, sc]
    if has_y:
        in_specs += [_tok_spec(tt), _bvec_spec()]
        out_specs += [_tok_spec(tt), _bvec_spec()]
        out_shape += [jax.ShapeDtypeStruct(src.shape, MXU_DT), bvec_out]
        args += [y, gt]
    return pl.pallas_call(body, name=name, grid=(bsz, t_total // tt), in_specs=in_specs, out_specs=tuple(out_specs),
                          out_shape=tuple(out_shape), compiler_params=_params(("arbitrary", "arbitrary")))(*args)


FFN_TC = 256
FFN_NJ = D_FF // FFN_TC
FFN_PW = 2 * FFN_TC


def _ffn_pair(a, axis):
    shp = list(a.shape)
    a4 = a.reshape(shp[:axis] + [2, FFN_NJ, FFN_TC] + shp[axis + 1:])
    return jnp.swapaxes(a4, axis, axis + 1).reshape(shp)


def _ffn_unpair(a, axis):
    shp = list(a.shape)
    a4 = a.reshape(shp[:axis] + [FFN_NJ, 2, FFN_TC] + shp[axis + 1:])
    return jnp.swapaxes(a4, axis, axis + 1).reshape(shp)


def _ffn_up_act(h, w_up, cw, cb):
    bsz, t_total, _ = h.shape
    tt = _div_tile(t_total, FFN_ROW_TILE)
    def body(h_ref, wu_ref, w_ref, b_ref, up_ref, o_ref, carry_ref):
        up_t = _dot_nt(h_ref[0], wu_ref[...])
        up_ref[0] = up_t
        prev = jnp.where(pl.program_id(2) == 0, 0.0, carry_ref[...])
        rows = jnp.concatenate([prev, up_t], axis=0)
        u = _conv_rows(rows, w_ref, FFN_CONV_K)[HALO:] + b_ref[...]
        o_ref[0] = (_silu(u[:, :FFN_TC]) * u[:, FFN_TC:]).astype(MXU_DT)
        carry_ref[...] = up_t[tt - HALO:, :]

    return pl.pallas_call(
        body, name="ffn_up_act", grid=(bsz, FFN_NJ, t_total // tt),
        in_specs=[pl.BlockSpec((1, tt, D), lambda b, j, t: (b, t, 0)),
                  pl.BlockSpec((FFN_PW, D), lambda b, j, t: (j, 0)),
                  pl.BlockSpec((FFN_CONV_K, FFN_PW), lambda b, j, t: (0, j)),
                  pl.BlockSpec((1, FFN_PW), lambda b, j, t: (0, j))],
        out_specs=(pl.BlockSpec((1, tt, FFN_PW), lambda b, j, t: (b, t, j)),
                   pl.BlockSpec((1, tt, FFN_TC), lambda b, j, t: (b, t, j))),
        out_shape=(jax.ShapeDtypeStruct((bsz, t_total, 2 * D_FF), F32),
                   jax.ShapeDtypeStruct((bsz, t_total, D_FF), MXU_DT)),
        scratch_shapes=[pltpu.VMEM((HALO, FFN_PW), F32)],
        compiler_params=_params(("parallel", "parallel", "arbitrary")))(h, w_up, cw, cb)


HALO16 = 16


def _ffn_act_bwd(up, dy2, w_down, cw, cb):
    bsz, t_total, width = up.shape
    tt = _div_tile(t_total, FFN_ROW_TILE)
    nt = t_total // tt
    hp, hn = _halo_prev(tt), _halo_next(tt, t_total)

    def body(x_ref, xp_ref, xn_ref, dy_ref, dyn_ref, wd_ref, w_ref, b_ref, dup_ref, dw_ref, db_ref):
        b, t = pl.program_id(1), pl.program_id(2)

        @pl.when((b == 0) & (t == 0))
        def _():
            dw_ref[...] = jnp.zeros_like(dw_ref)
            db_ref[...] = jnp.zeros_like(db_ref)

        prev = jnp.where(t == 0, 0.0, xp_ref[0])
        rows = jnp.concatenate([prev, x_ref[0], xn_ref[0]], axis=0)
        u = _conv_rows(rows, w_ref, FFN_CONV_K)[HALO:] + b_ref[...]
        g_pre, v_pre = u[:, :FFN_TC], u[:, FFN_TC:]
        valid = (_iota((tt + HALO, 1), 0) < tt) | (t < nt - 1)
        da = jnp.concatenate([_dot_nt(dy_ref[0], wd_ref[...]), _dot_nt(dyn_ref[0], wd_ref[...])[:HALO]], axis=0)
        da_ext = jnp.where(valid, da, 0.0)
        sg = _sigmoid(g_pre)
        gs = g_pre * sg
        du = jnp.concatenate([da_ext * v_pre * (sg + gs * (1.0 - sg)), da_ext * gs], axis=1)
        dup = du * w_ref[FFN_CONV_K - 1:FFN_CONV_K, :]
        for s in range(1, FFN_CONV_K):
            dup = dup + _shift_up(du, s) * w_ref[FFN_CONV_K - 1 - s:FFN_CONV_K - s, :]
        dup_ref[0] = dup[:tt].astype(MXU_DT)
        du_t = du[:tt]
        db_ref[...] += jnp.sum(du_t, axis=0, keepdims=True)
        for k in range(FFN_CONV_K):
            s = FFN_CONV_K - 1 - k
            xs = (rows if s == 0 else pltpu.roll(rows, s, 0))[HALO:HALO + tt]
            dw_ref[k:k + 1, :] += jnp.sum(du_t * xs, axis=0, keepdims=True)

    def halo(h, w):
        return pl.BlockSpec((1, HALO, w), lambda j, b, t: (*h(b, t), j))

    wspec = lambda rows_: pl.BlockSpec((rows_, FFN_PW), lambda j, b, t: (0, j))
    tile = pl.BlockSpec((1, tt, FFN_PW), lambda j, b, t: (b, t, j))
    dy_next = lambda j, b, t: (b, jnp.minimum((t + 1) * (tt // HALO16), t_total // HALO16 - 1), 0)
    return pl.pallas_call(
        body, name="ffn_act_bwd", grid=(FFN_NJ, bsz, nt),
        in_specs=[tile, halo(hp, FFN_PW), halo(hn, FFN_PW),
                  pl.BlockSpec((1, tt, D), lambda j, b, t: (b, t, 0)), pl.BlockSpec((1, HALO16, D), dy_next),
                  pl.BlockSpec((FFN_TC, D), lambda j, b, t: (j, 0)), wspec(FFN_CONV_K), wspec(1)],
        out_specs=(tile, wspec(FFN_CONV_K), wspec(1)),
        out_shape=(jax.ShapeDtypeStruct(up.shape, MXU_DT), jax.ShapeDtypeStruct((FFN_CONV_K, width), F32),
                   jax.ShapeDtypeStruct((1, width), F32)),
        compiler_params=_params(("arbitrary", "arbitrary", "arbitrary")))(up, up, up, dy2, dy2, w_down, cw, cb)


QKV_W = 3 * HEADS * HD
SM_BLK = P_SM // 128


def _dn_pre_bwd(proj, dq, dk, dv, dgates, dsm_gla, conv_w, alog_row, dt_row):
    bsz, t_total, _ = proj.shape
    tt = _div_tile(t_total, 256)
    nt = t_total // tt
    hp, hn = _halo_prev(tt), _halo_next(tt, t_total)

    def body(x_ref, xp_ref, xn_ref, sm_ref, dq_ref, dqn_ref, dk_ref, dkn_ref, dv_ref, dvn_ref, dg_ref, dso_ref,
             w_ref, al_ref, dt_ref, dx_ref, dsm_ref, dw_ref, dal_ref, ddt_ref):
        b, t = pl.program_id(0), pl.program_id(1)

        @pl.when((b == 0) & (t == 0))
        def _():
            dw_ref[...] = jnp.zeros_like(dw_ref)
            dal_ref[...] = jnp.zeros_like(dal_ref)
            ddt_ref[...] = jnp.zeros_like(ddt_ref)

        prev = jnp.where(t == 0, 0.0, xp_ref[0])
        rows = jnp.concatenate([prev, x_ref[0], xn_ref[0]], axis=0)
        y = _conv_rows(rows, w_ref, DN_CONV_K)[HALO:]
        valid = (_iota((tt + HALO, 1), 0) < tt) | (t < nt - 1)

        def ext(tile_ref, next_ref):
            return jnp.where(valid, jnp.concatenate([tile_ref[0], next_ref[0]], axis=0), 0.0)

        _, vjp_qkv = jax.vjp(_dn_qkv, y)
        (dy,) = vjp_qkv((ext(dq_ref, dqn_ref), ext(dk_ref, dkn_ref), ext(dv_ref, dvn_ref)))
        dy = jnp.where(valid, dy, 0.0)
        dx = dy * w_ref[DN_CONV_K - 1:DN_CONV_K, :]
        for s in range(1, DN_CONV_K):
            dx = dx + _shift_up(dy, s) * w_ref[DN_CONV_K - 1 - s:DN_CONV_K - s, :]
        dx_ref[0] = dx[:tt].astype(MXU_DT)
        dy_t = dy[:tt]
        for k in range(DN_CONV_K):
            s = DN_CONV_K - 1 - k
            xs = (rows if s == 0 else pltpu.roll(rows, s, 0))[HALO:HALO + tt]
            dw_ref[k:k + 1, :] += jnp.sum(dy_t * xs, axis=0, keepdims=True)
        _, vjp_g = jax.vjp(_dn_gates, sm_ref[0], al_ref[...], dt_ref[...])
        dsm, dal, ddt = vjp_g(dg_ref[0])
        dsm_ref[0] = (dsm + dso_ref[0]).astype(MXU_DT)
        dal_ref[...] += dal
        ddt_ref[...] += ddt

    def tile(width, blk=0):
        return pl.BlockSpec((1, tt, width), lambda b, t: (b, t, blk))

    def halo(h, width):
        return pl.BlockSpec((1, HALO, width), lambda b, t: (*h(b, t), 0))

    return pl.pallas_call(
        body, name="dn_pre_bwd", grid=(bsz, nt),
        in_specs=[tile(QKV_W), halo(hp, QKV_W), halo(hn, QKV_W), tile(128, SM_BLK),
                  tile(512), halo(hn, 512), tile(512), halo(hn, 512), tile(512), halo(hn, 512), tile(128), tile(128),
                  pl.BlockSpec((DN_CONV_K, QKV_W), lambda b, t: (0, 0)), _vec_spec(128), _vec_spec(128)],
        out_specs=(tile(QKV_W), tile(128), pl.BlockSpec((DN_CONV_K, QKV_W), lambda b, t: (0, 0)),
                   _vec_spec(128), _vec_spec(128)),
        out_shape=(jax.ShapeDtypeStruct((bsz, t_total, QKV_W), MXU_DT),
                   jax.ShapeDtypeStruct((bsz, t_total, 128), MXU_DT),
                   jax.ShapeDtypeStruct((DN_CONV_K, QKV_W), F32), jax.ShapeDtypeStruct((1, 128), F32),
                   jax.ShapeDtypeStruct((1, 128), F32)),
        compiler_params=_params(("arbitrary", "arbitrary")))(
            proj, proj, proj, proj, dq, dq, dk, dk, dv, dv, dgates, dsm_gla, conv_w, alog_row, dt_row)


def _state_spec(bsz, idx):
    return pl.BlockSpec((bsz, 1, HEADS, HD, HD), lambda c: (0, idx(c), 0, 0, 0))


def _inv_spec(bsz, idx):
    return pl.BlockSpec((bsz, 1, HEADS, CHUNK, CHUNK), lambda c: (0, idx(c), 0, 0, 0))


def _chunk_spec(bsz, width, idx, blk=0):
    return pl.BlockSpec((bsz, CHUNK, width), lambda c: (0, idx(c), blk))


GQ_BLK, GK_BLK, GV_BLK = P_GQ // 512, P_GK // 512, P_GV // 512
REC_SHARES = (3, 1)


def _dn_prep(prev_rows, rows, small, w_ref, alog_row, dt_row):
    ys = [_conv_rows(jnp.concatenate([p, r], axis=0), w_ref, DN_CONV_K)[HALO:] for p, r in zip(prev_rows, rows)]
    yield
    qkv = [_dn_qkv(y) for y in ys]
    yield
    gates = [_dn_gates(s, alog_row, dt_row) for s in small]
    return [t[0] for t in qkv], [t[1] for t in qkv], [t[2] for t in qkv], gates


def _rec_fwd(proj, conv_w, alog_row, dt_row, w2, bg):
    bsz, t_total, _ = proj.shape
    nc = t_total // CHUNK
    fwd = lambda c: c
    nxt = lambda c: jnp.minimum(c + 1, nc - 1)

    def body(x0_ref, xn_ref, xnp_ref, smn_ref, cw_ref, al_ref, dt_ref,
             gq_ref, gk_ref, gv_ref, sm_ref, w2_ref, bg_ref,
             q_out, k_out, v_out, g_out, o_ref, ss_ref, inv_ref, go_ref, gss_ref,
             s_ref, gs_ref, nq_ref, nk_ref, nv_ref, ng_ref):
        seqs = range(bsz)
        heads = range(HEADS)
        per_seq = lambda ref: [ref[b] for b in seqs]

        def keep(prep):
            for b in seqs:
                nq_ref[b], nk_ref[b], nv_ref[b], ng_ref[b] = prep[0][b], prep[1][b], prep[2][b], prep[3][b]

        @pl.when(pl.program_id(0) == 0)
        def _():
            s_ref[...] = jnp.zeros_like(s_ref)
            gs_ref[...] = jnp.zeros_like(gs_ref)
            zeros = [jnp.zeros((HALO, QKV_W), F32) for _ in seqs]
            keep(_interleave([_dn_prep(zeros, per_seq(x0_ref), per_seq(sm_ref), cw_ref, al_ref[...], dt_ref[...])],
                             (1,))[0])

        q, k, v, gates = per_seq(nq_ref), per_seq(nk_ref), per_seq(nv_ref), per_seq(ng_ref)
        s_list = [[s_ref[b * HEADS + h] for h in heads] for b in seqs]
        gs_list = [[gs_ref[b * HEADS + h] for h in heads] for b in seqs]
        for b in seqs:
            q_out[b], k_out[b], v_out[b], g_out[b] = q[b], k[b], v[b], gates[b]
            for h in heads:
                ss_ref[b, 0, h] = s_list[b][h]
                gss_ref[b, 0, h] = gs_list[b][h]
        (o, new_s, invs), (go, new_gs), prep = _interleave(
            [_dn_chunk(s_list, q, k, v, gates),
             _gla_chunk(gs_list, per_seq(gq_ref), per_seq(gk_ref), per_seq(gv_ref), per_seq(sm_ref),
                        w2_ref[...], bg_ref[...]),
             _dn_prep(per_seq(xnp_ref), per_seq(xn_ref), per_seq(smn_ref), cw_ref, al_ref[...], dt_ref[...])],
            REC_SHARES + (1,))
        keep(prep)
        for b in seqs:
            o_ref[b] = o[b]
            go_ref[b] = go[b]
            for h in heads:
                s_ref[b * HEADS + h] = new_s[b][h]
                gs_ref[b * HEADS + h] = new_gs[b][h]
                inv_ref[b, 0, h] = invs[b * HEADS + h]

    tok = lambda width, blk=0, idx=fwd: _chunk_spec(bsz, width, idx, blk)
    state = jax.ShapeDtypeStruct((bsz, nc, HEADS, HD, HD), F32)
    out512 = jax.ShapeDtypeStruct((bsz, t_total, 512), F32)
    vec = lambda width: pl.BlockSpec((1, width), lambda c: (0, 0))
    return pl.pallas_call(
        body, name="rec_fwd", grid=(nc,),
        in_specs=[tok(QKV_W), tok(QKV_W, idx=nxt),
                  pl.BlockSpec((bsz, HALO, QKV_W), lambda c: (0, jnp.maximum(nxt(c) * (CHUNK // HALO) - 1, 0), 0)),
                  tok(128, SM_BLK, idx=nxt), pl.BlockSpec((DN_CONV_K, QKV_W), lambda c: (0, 0)), vec(128), vec(128),
                  tok(512, GQ_BLK), tok(512, GK_BLK), tok(512, GV_BLK), tok(128, SM_BLK),
                  pl.BlockSpec((128, 512), lambda c: (0, 0)), vec(512)],
        out_specs=(tok(512), tok(512), tok(512), tok(128),
                   tok(512), _state_spec(bsz, fwd), _inv_spec(bsz, fwd), tok(512), _state_spec(bsz, fwd)),
        out_shape=(out512, out512, out512, jax.ShapeDtypeStruct((bsz, t_total, 128), F32),
                   out512, state, jax.ShapeDtypeStruct((bsz, nc, HEADS, CHUNK, CHUNK), F32), out512, state),
        scratch_shapes=[pltpu.VMEM((bsz * HEADS, HD, HD), F32), pltpu.VMEM((bsz * HEADS, HD, HD), F32),
                        pltpu.VMEM((bsz, CHUNK, 512), F32), pltpu.VMEM((bsz, CHUNK, 512), F32),
                        pltpu.VMEM((bsz, CHUNK, 512), F32), pltpu.VMEM((bsz, CHUNK, 128), F32)],
        compiler_params=_params(("arbitrary",)))(
            proj, proj, proj, proj, conv_w, alog_row, dt_row, proj, proj, proj, proj, w2, bg)


def _rec_bwd(q, k, v, gates, s_dn, inv_dn, do_dn, proj, w2, bg, s_gla, do_gla):
    bsz, t_total, _ = q.shape
    nc = t_total // CHUNK
    rev = lambda c: nc - 1 - c

    def body(q_ref, k_ref, v_ref, g_ref, ss_ref, inv_ref, do_ref,
             gq_ref, gk_ref, gv_ref, sm_ref, w2_ref, bg_ref, gss_ref, gdo_ref,
             dq_ref, dk_ref, dv_ref, dg_ref, dgq_ref, dgk_ref, dgv_ref, dsm_ref, dw2_ref, dbg_ref, ds_ref, gds_ref):
        @pl.when(pl.program_id(0) == 0)
        def _():
            ds_ref[...] = jnp.zeros_like(ds_ref)
            gds_ref[...] = jnp.zeros_like(gds_ref)
            dw2_ref[...] = jnp.zeros_like(dw2_ref)
            dbg_ref[...] = jnp.zeros_like(dbg_ref)

        seqs = range(bsz)
        heads = range(HEADS)
        per_seq = lambda ref: [ref[b] for b in seqs]
        known = [inv_ref[b, 0, h] for b in seqs for h in heads]

        def both(s_list, q_, k_, v_, g_, gs_list, gq_, gk_, gv_, sm_, w2_, bg_):
            (o, new_s, _), (go, new_gs) = _interleave(
                [_dn_chunk(s_list, q_, k_, v_, g_, inv_known=known),
                 _gla_chunk(gs_list, gq_, gk_, gv_, sm_, w2_, bg_)], REC_SHARES)
            return o, new_s, go, new_gs

        _, vjp = jax.vjp(both, [[ss_ref[b, 0, h] for h in heads] for b in seqs],
                         per_seq(q_ref), per_seq(k_ref), per_seq(v_ref), per_seq(g_ref),
                         [[gss_ref[b, 0, h] for h in heads] for b in seqs],
                         per_seq(gq_ref), per_seq(gk_ref), per_seq(gv_ref), per_seq(sm_ref), w2_ref[...], bg_ref[...])
        ds_in, dq, dk, dv, dg, gds_in, dgq, dgk, dgv, dsm, dw2, dbg = vjp(
            (per_seq(do_ref), [[ds_ref[b * HEADS + h] for h in heads] for b in seqs],
             per_seq(gdo_ref), [[gds_ref[b * HEADS + h] for h in heads] for b in seqs]))
        for b in seqs:
            dq_ref[b], dk_ref[b], dv_ref[b], dg_ref[b] = dq[b], dk[b], dv[b], dg[b]
            dgq_ref[b], dgk_ref[b], dgv_ref[b] = dgq[b].astype(MXU_DT), dgk[b].astype(MXU_DT), dgv[b].astype(MXU_DT)
            dsm_ref[b] = dsm[b]
            for h in heads:
                ds_ref[b * HEADS + h] = ds_in[b][h]
                gds_ref[b * HEADS + h] = gds_in[b][h]
        dw2_ref[...] += dw2
        dbg_ref[...] += dbg

    tok = lambda width, blk=0: _chunk_spec(bsz, width, rev, blk)
    w2_spec = pl.BlockSpec((128, 512), lambda c: (0, 0))
    bg_spec = pl.BlockSpec((1, 512), lambda c: (0, 0))
    f512 = jax.ShapeDtypeStruct((bsz, t_total, 512), F32)
    b512 = jax.ShapeDtypeStruct((bsz, t_total, 512), MXU_DT)
    f128 = jax.ShapeDtypeStruct((bsz, t_total, 128), F32)
    return pl.pallas_call(
        body, name="rec_bwd", grid=(nc,),
        in_specs=[tok(512), tok(512), tok(512), tok(128), _state_spec(bsz, rev), _inv_spec(bsz, rev), tok(512),
                  tok(512, GQ_BLK), tok(512, GK_BLK), tok(512, GV_BLK), tok(128, SM_BLK), w2_spec, bg_spec,
                  _state_spec(bsz, rev), tok(512)],
        out_specs=(tok(512), tok(512), tok(512), tok(128), tok(512), tok(512), tok(512), tok(128), w2_spec, bg_spec),
        out_shape=(f512, f512, f512, f128, b512, b512, b512, f128,
                   jax.ShapeDtypeStruct((128, 512), F32), jax.ShapeDtypeStruct((1, 512), F32)),
        scratch_shapes=[pltpu.VMEM((bsz * HEADS, HD, HD), F32), pltpu.VMEM((bsz * HEADS, HD, HD), F32)],
        compiler_params=_params(("arbitrary",)))(
            q, k, v, gates, s_dn, inv_dn, do_dn, proj, proj, proj, proj, w2, bg, s_gla, do_gla)


Z_BLK, GG_BLK = P_Z // 512, P_GG // 512


def _mix_out_fwd(o_dn, o_gla, proj, grow_dn, grow_gla):
    bsz, t_total, _ = o_dn.shape
    tt = _div_tile(t_total, ROW_TILE)

    def body(od_ref, og_ref, z_ref, gg_ref, gd_ref, gl_ref, o_ref):
        o_ref[0, :, :512] = _gate_norm(od_ref[0], z_ref[0], gd_ref[...]).astype(MXU_DT)
        o_ref[0, :, 512:] = _gate_norm(og_ref[0], gg_ref[0], gl_ref[...]).astype(MXU_DT)

    def col(blk):
        return pl.BlockSpec((1, tt, 512), lambda b, t: (b, t, blk))

    return pl.pallas_call(
        body, name="mix_out_fwd", grid=(bsz, t_total // tt),
        in_specs=[col(0), col(0), col(Z_BLK), col(GG_BLK), _vec_spec(512), _vec_spec(512)],
        out_specs=_tok_spec(tt), out_shape=jax.ShapeDtypeStruct((bsz, t_total, D), MXU_DT),
        compiler_params=_params(("parallel", "parallel")))(o_dn, o_gla, proj, proj, grow_dn, grow_gla)


def _mix_out_bwd(do, o_dn, o_gla, proj, grow_dn, grow_gla):
    bsz, t_total, _ = o_dn.shape
    tt = _div_tile(t_total, ROW_TILE)

    def body(do_ref, od_ref, og_ref, z_ref, gg_ref, gd_ref, gl_ref,
             dod_ref, dog_ref, dz_ref, dgg_ref, dgd_ref, dgl_ref):
        @pl.when((pl.program_id(0) == 0) & (pl.program_id(1) == 0))
        def _():
            dgd_ref[...] = jnp.zeros_like(dgd_ref)
            dgl_ref[...] = jnp.zeros_like(dgl_ref)

        def one(o_ref, gate_ref, g_ref, ct, do_out, dgate_out, dg_out):
            _, vjp = jax.vjp(_gate_norm, o_ref[0], gate_ref[0], g_ref[...])
            d_o, d_gate, d_row = vjp(ct)
            do_out[0] = d_o
            dgate_out[0] = d_gate.astype(MXU_DT)
            acc = d_row[:, :HD]
            for h in range(1, HEADS):
                acc = acc + d_row[:, h * HD:(h + 1) * HD]
            dg_out[...] += acc

        ct = do_ref[0].astype(F32)
        one(od_ref, z_ref, gd_ref, ct[:, :512], dod_ref, dz_ref, dgd_ref)
        one(og_ref, gg_ref, gl_ref, ct[:, 512:], dog_ref, dgg_ref, dgl_ref)

    def col(blk):
        return pl.BlockSpec((1, tt, 512), lambda b, t: (b, t, blk))

    f512 = jax.ShapeDtypeStruct((bsz, t_total, 512), F32)
    b512 = jax.ShapeDtypeStruct((bsz, t_total, 512), MXU_DT)
    g128 = jax.ShapeDtypeStruct((1, HD), F32)
    return pl.pallas_call(
        body, name="mix_out_bwd", grid=(bsz, t_total // tt),
        in_specs=[_tok_spec(tt), col(0), col(0), col(Z_BLK), col(GG_BLK), _vec_spec(512), _vec_spec(512)],
        out_specs=(col(0), col(0), col(0), col(0), _vec_spec(HD), _vec_spec(HD)),
        out_shape=(f512, f512, b512, b512, g128, g128),
        compiler_params=_params(("arbitrary", "arbitrary")))(do, o_dn, o_gla, proj, proj, grow_dn, grow_gla)


def _sum_slots(x, name):
    n, rows, cols = x.shape
    tr = _div_tile(rows, max(8, (1 << 19) // cols))

    def body(x_ref, o_ref):
        acc = x_ref[0].astype(F32)
        for i in range(1, n):
            acc = acc + x_ref[i].astype(F32)
        o_ref[...] = acc

    return pl.pallas_call(
        body, name=name, grid=(rows // tr,),
        in_specs=[pl.BlockSpec((n, tr, cols), lambda i: (0, i, 0))],
        out_specs=pl.BlockSpec((tr, cols), lambda i: (i, 0)),
        out_shape=jax.ShapeDtypeStruct((rows, cols), F32), compiler_params=_params(("parallel",)))(x)


def _adamw_math(w, g, m, v):
    nm = ADAM_B1 * m + (1.0 - ADAM_B1) * g
    nv = ADAM_B2 * v + (1.0 - ADAM_B2) * (g * g)
    m_hat = nm / (1.0 - ADAM_B1 ** ADAM_STEP)
    v_hat = nv / (1.0 - ADAM_B2 ** ADAM_STEP)
    return -ADAM_LR * (m_hat / (jnp.sqrt(v_hat) + ADAM_EPS) + ADAM_WD * w), nm, nv


def _adamw(w, g, m, v, name):
    _, rows, cols = w.shape
    tr = _div_tile(rows, max(8, (1 << 18) // cols))

    def body(w_ref, g_ref, m_ref, v_ref, d_ref, nm_ref, nv_ref):
        d_ref[...], nm_ref[...], nv_ref[...] = _adamw_math(w_ref[...], g_ref[...], m_ref[...], v_ref[...])

    spec = pl.BlockSpec((1, tr, cols), lambda i: (0, i, 0))
    shp = jax.ShapeDtypeStruct(w.shape, F32)
    return pl.pallas_call(body, name=name, grid=(rows // tr,), in_specs=[spec] * 4, out_specs=(spec,) * 3,
                          out_shape=(shp,) * 3, compiler_params=_params(("parallel",)))(w, g, m, v)


def _sum_adamw(parts, w, m, v, name):
    n, rows, cols = parts.shape
    tr = _div_tile(rows, max(8, (1 << 18) // cols))

    def body(p_ref, w_ref, m_ref, v_ref, g_ref, d_ref, nm_ref, nv_ref):
        g = p_ref[0].astype(F32)
        for i in range(1, n):
            g = g + p_ref[i].astype(F32)
        g_ref[...] = g
        d_ref[0], nm_ref[0], nv_ref[0] = _adamw_math(w_ref[0], g, m_ref[0], v_ref[0])

    spec = pl.BlockSpec((1, tr, cols), lambda i: (0, i, 0))
    shp = jax.ShapeDtypeStruct(w.shape, F32)
    return pl.pallas_call(
        body, name=name, grid=(rows // tr,),
        in_specs=[pl.BlockSpec((n, tr, cols), lambda i: (0, i, 0)), spec, spec, spec],
        out_specs=(pl.BlockSpec((tr, cols), lambda i: (i, 0)), spec, spec, spec),
        out_shape=(jax.ShapeDtypeStruct((rows, cols), F32), shp, shp, shp),
        compiler_params=_params(("parallel",)))(parts, w, m, v)


def _adamw_many(ws, gs, ms, vs, name):
    n = len(ws)

    def body(*refs):
        for i in range(n):
            d, nm, nv = _adamw_math(refs[i][...], refs[n + i][...], refs[2 * n + i][...], refs[3 * n + i][...])
            refs[4 * n + i][...] = d
            refs[5 * n + i][...] = nm
            refs[6 * n + i][...] = nv

    shapes = tuple(jax.ShapeDtypeStruct(w.shape, F32) for w in ws)
    outs = pl.pallas_call(body, name=name, out_shape=shapes * 3, compiler_params=_params())(*ws, *gs, *ms, *vs)
    return outs[:n], outs[n:2 * n], outs[2 * n:]


def _position():
    return lax.axis_index("x"), lax.axis_index("y"), lax.axis_index("c")


def _slot(px, py, pc):
    return 4 * px + 2 * py + pc


def _gather_small(x, name):
    rows, cols = x.shape

    def body(x_ref, o_ref, send_sems, recv_sems):
        mx, my, mc = _position()

        def peer(k):
            return (mx ^ ((k >> 2) & 1), my ^ ((k >> 1) & 1), mc ^ (k & 1))

        o_ref[_slot(mx, my, mc)] = x_ref[...]
        sends = []
        for k in range(1, N_DEV):
            cp = pltpu.make_async_remote_copy(src_ref=x_ref, dst_ref=o_ref.at[_slot(mx, my, mc)],
                                              send_sem=send_sems.at[k - 1], recv_sem=recv_sems.at[k - 1],
                                              device_id=peer(k), device_id_type=MESH)
            cp.start()
            sends.append(cp)
        for k in range(1, N_DEV):
            pltpu.make_async_remote_copy(src_ref=x_ref, dst_ref=o_ref.at[_slot(*peer(k))],
                                         send_sem=send_sems.at[k - 1], recv_sem=recv_sems.at[k - 1],
                                         device_id=peer(k), device_id_type=MESH).wait_recv()
        for cp in sends:
            cp.wait_send()

    return pl.pallas_call(
        body, name=name, out_shape=jax.ShapeDtypeStruct((N_DEV, rows, cols), x.dtype),
        in_specs=[pl.BlockSpec(memory_space=pltpu.VMEM)], out_specs=pl.BlockSpec(memory_space=pltpu.VMEM),
        scratch_shapes=[pltpu.SemaphoreType.DMA((N_DEV - 1,)), pltpu.SemaphoreType.DMA((N_DEV - 1,))],
        compiler_params=pltpu.CompilerParams(vmem_limit_bytes=VMEM_LIMIT_V7X))(x)


def _gather_big(shards):
    n = len(shards)

    def body(*refs):
        xs, outs = refs[:n], refs[n:2 * n]
        send_sems, recv_sems, local_sems = refs[2 * n:]
        mx, my, mc = _position()
        me, sibling = (mx, my, mc), (mx, my, 1 - mc)
        chips = [(1 - mx, my), (mx, 1 - my), (1 - mx, 1 - my)]

        def copy(a, k, block, to, src=None):
            dst = outs[a].at[_slot(*block)]
            return pltpu.make_async_remote_copy(src_ref=dst if src is None else src, dst_ref=dst,
                                                send_sem=send_sems.at[7 * a + k], recv_sem=recv_sems.at[7 * a + k],
                                                device_id=to, device_id_type=MESH)

        mine = [pltpu.make_async_copy(xs[a], outs[a].at[_slot(*me)], local_sems.at[a]) for a in range(n)]
        for cp in mine:
            cp.start()
        started = []
        for a in range(n):
            started.append(copy(a, 0, me, sibling, src=xs[a]))
            started += [copy(a, 1 + j, me, (*chip, mc), src=xs[a]) for j, chip in enumerate(chips)]
        for cp in started:
            cp.start()
        for j, chip in enumerate(chips):
            for a in range(n):
                copy(a, 1 + j, (*chip, mc), me).wait_recv()
                fwd = copy(a, 4 + j, (*chip, mc), sibling)
                fwd.start()
                started.append(fwd)
        for a in range(n):
            copy(a, 0, sibling, me).wait_recv()
            for j, chip in enumerate(chips):
                copy(a, 4 + j, (*chip, 1 - mc), me).wait_recv()
        for cp in started:
            cp.wait_send()
        for cp in mine:
            cp.wait()

    any_spec = pl.BlockSpec(memory_space=pl.ANY)
    return pl.pallas_call(
        body, name="gather_weights",
        out_shape=tuple(jax.ShapeDtypeStruct((N_DEV,) + s.shape, s.dtype) for s in shards),
        in_specs=[any_spec] * n, out_specs=(any_spec,) * n,
        scratch_shapes=[pltpu.SemaphoreType.DMA((7 * n,)), pltpu.SemaphoreType.DMA((7 * n,)),
                        pltpu.SemaphoreType.DMA((n,))])(*shards)


def _peer(pos, k):
    mx, my, mc = pos
    return (mx ^ ((k >> 2) & 1), my ^ ((k >> 1) & 1), mc ^ (k & 1))


def _exchange_copies(srcs, lands, send_sems, recv_sems, by_owner, arrivals):
    pos = _position()
    me = _slot(*pos)
    out = []
    for a, (src, land) in enumerate(zip(srcs, lands)):
        for k in range(1, N_DEV):
            peer = _peer(pos, k)
            mine = src.at[_slot(*peer)] if by_owner else src
            out.append(pltpu.make_async_remote_copy(
                src_ref=mine, dst_ref=land.at[_slot(*peer) if arrivals else me],
                send_sem=send_sems.at[7 * a + k - 1], recv_sem=recv_sems.at[7 * a + k - 1],
                device_id=peer, device_id_type=MESH))
    return out


_HBM_SPEC = pl.BlockSpec(memory_space=pltpu.HBM)
_SEM_SPEC = pl.BlockSpec(memory_space=pltpu.SEMAPHORE)
_DATAFLOW = pltpu.SideEffectType.DATAFLOW_SIDE_EFFECTING


def _exchange_start(name, srcs, slab_shapes, after, by_owner, carry=()):
    n, na, nc = len(srcs), len(after), len(carry)
    lands = [pltpu.with_memory_space_constraint(lax.empty((N_DEV,) + s, x.dtype), pltpu.HBM)
             for s, x in zip(slab_shapes, srcs)]
    thru = [pltpu.with_memory_space_constraint(x, pltpu.HBM) for x in [*srcs, *lands, *carry]]

    def body(*refs):
        src_refs, land_refs = refs[:n], refs[n:2 * n]
        send_sems, recv_sems = refs[len(thru) + na], refs[len(thru) + na + 1]
        token = refs[-1]
        for send in _exchange_copies(src_refs, land_refs, send_sems, recv_sems, by_owner, arrivals=False):
            send.start()
        token[...] = jnp.zeros_like(token)

    outs = pl.pallas_call(
        body, name=name,
        out_shape=(pltpu.SemaphoreType.DMA((7 * n,)), pltpu.SemaphoreType.DMA((7 * n,)),
                   *[pltpu.HBM(x.shape, x.dtype) for x in thru], jax.ShapeDtypeStruct((8, 128), F32)),
        in_specs=[_HBM_SPEC] * len(thru) + [pl.BlockSpec(memory_space=pl.ANY)] * na,
        out_specs=(_SEM_SPEC, _SEM_SPEC, *[_HBM_SPEC] * len(thru), pl.BlockSpec(memory_space=pltpu.VMEM)),
        input_output_aliases={i: 2 + i for i in range(len(thru))},
        compiler_params=pltpu.CompilerParams(has_side_effects=_DATAFLOW))(*thru, *after)
    return (outs[0], outs[1], list(outs[2:2 + n]), list(outs[2 + n:2 + 2 * n]), outs[-1],
            list(outs[2 + 2 * n:2 + 2 * n + nc]))


def _exchange_wait(name, send_sems, recv_sems, srcs, lands, after, by_owner):
    n = len(srcs)

    def body(*refs):
        src_refs, land_refs = refs[:n], refs[n:2 * n]
        s_sems, r_sems = refs[2 * n], refs[2 * n + 1]
        for send in _exchange_copies(src_refs, land_refs, s_sems, r_sems, by_owner, arrivals=False):
            send.wait_send()
        for recv in _exchange_copies(src_refs, land_refs, s_sems, r_sems, by_owner, arrivals=True):
            recv.wait_recv()

    outs = pl.pallas_call(
        body, name=name,
        out_shape=(*[pltpu.HBM(x.shape, x.dtype) for x in srcs], *[pltpu.HBM(l.shape, l.dtype) for l in lands]),
        in_specs=[_HBM_SPEC] * (2 * n) + [_SEM_SPEC, _SEM_SPEC, pl.BlockSpec(memory_space=pl.ANY)],
        out_specs=tuple([_HBM_SPEC] * (2 * n)),
        input_output_aliases={i: i for i in range(2 * n)},
        compiler_params=pltpu.CompilerParams(has_side_effects=_DATAFLOW))(*srcs, *lands, send_sems, recv_sems, after)
    return list(outs[:n]), list(outs[n:])


def _pad_heads(x, axis):
    shp = list(x.shape)
    x4 = x.reshape(shp[:axis] + [HEADS, GLA_KEY] + shp[axis + 1:])
    pad = [(0, 0)] * x4.ndim
    pad[axis + 1] = (0, HD - GLA_KEY)
    return jnp.pad(x4, pad).reshape(shp[:axis] + [HEADS * HD] + shp[axis + 1:])


def _unpad_heads(x, axis):
    shp = list(x.shape)
    x4 = x.reshape(shp[:axis] + [HEADS, HD] + shp[axis + 1:])
    x4 = lax.slice_in_dim(x4, 0, GLA_KEY, axis=axis + 1)
    return x4.reshape(shp[:axis] + [HEADS * GLA_KEY] + shp[axis + 1:])


O_Z_END, O_AB, O_GQ, O_GK, O_GV, O_R = 2048, 2048, 2056, 2312, 2568, 3592


def _padded_row(f):
    if f < O_Z_END:
        return f
    if f < O_GQ:
        return P_SM + (f - O_AB)
    if f < O_GV:
        base, g = (P_GQ, f - O_GQ) if f < O_GK else (P_GK, f - O_GK)
        return base + HD * (g // GLA_KEY) + g % GLA_KEY
    if f < O_R:
        return P_GV + (f - O_GV)
    return P_SM + 8 + (f - O_R)


def _runs(pairs):
    out = []
    for d, s in pairs:
        if out and out[-1][0] + out[-1][2] == d and out[-1][1] + out[-1][2] == s:
            out[-1][2] += 1
        else:
            out.append([d, s, 1])
    return out


def _pad_in_rows(shards):
    wt = shards.reshape(IN_W, D)
    return jnp.concatenate([
        wt[:O_Z_END], _pad_heads(wt[O_GQ:O_GK], 0), _pad_heads(wt[O_GK:O_GV], 0), wt[O_GV:O_R],
        wt[O_AB:O_GQ], wt[O_R:], jnp.zeros((P_W - P_SM - 8 - GATE_RANK, D), wt.dtype)], axis=0)


def _unpad_in_rows(gt):
    per = IN_W // N_DEV
    return jnp.stack([
        jnp.concatenate([gt[src:src + n] for _, src, n in
                         _runs([(f, _padded_row(f)) for f in range(j * per, (j + 1) * per)])], axis=0)
        for j in range(N_DEV)])


def _lane_row(vals, width=128):
    return jnp.pad(vals.reshape(1, -1), ((0, 0), (0, width - vals.size)))


SMALL_NAMES = ["ln0_g", "ln0_b", "b_ada", "dn_conv", "dn_a_log", "dn_dt_bias", "dn_norm_g", "gla_w_gate2",
               "gla_b_gate", "gla_norm_g", "ln1_g", "ln1_b", "ffn_conv", "ffn_conv_b", "ln2_g", "ln2_b"]
WEIGHTS = ["ln0_g", "ln0_b", "w_ada", "b_ada", "w_in", "dn_conv", "dn_a_log", "dn_dt_bias", "dn_norm_g",
           "gla_w_gate2", "gla_b_gate", "gla_norm_g", "w_o", "ln1_g", "ln1_b", "ffn_w_up", "ffn_conv", "ffn_conv_b",
           "ffn_w_down", "ln2_g", "ln2_b"]


def kernel(x, c, ln0_g, ln0_b, w_ada, b_ada, w_in, dn_conv, dn_a_log, dn_dt_bias, dn_norm_g, gla_w_gate2, gla_b_gate, gla_norm_g, w_o, ln1_g, ln1_b, ffn_w_up, ffn_conv, ffn_conv_b, ffn_w_down, ln2_g, ln2_b, loss_target, m_ln0_g, m_ln0_b, m_w_ada, m_b_ada, m_w_in, m_dn_conv, m_dn_a_log, m_dn_dt_bias, m_dn_norm_g, m_gla_w_gate2, m_gla_b_gate, m_gla_norm_g, m_w_o, m_ln1_g, m_ln1_b, m_ffn_w_up, m_ffn_conv, m_ffn_conv_b, m_ffn_w_down, m_ln2_g, m_ln2_b, v_ln0_g, v_ln0_b, v_w_ada, v_b_ada, v_w_in, v_dn_conv, v_dn_a_log, v_dn_dt_bias, v_dn_norm_g, v_gla_w_gate2, v_gla_b_gate, v_gla_norm_g, v_w_o, v_ln1_g, v_ln1_b, v_ffn_w_up, v_ffn_conv, v_ffn_conv_b, v_ffn_w_down, v_ln2_g, v_ln2_b):
    args = dict(locals())
    w_given = {n: args[n] for n in WEIGHTS}
    m_given = {n: args["m_" + n] for n in WEIGHTS}
    v_given = {n: args["v_" + n] for n in WEIGHTS}
    bsz, t_total, _ = x.shape
    ntok = bsz * t_total
    mx, my, mc = _position()
    me = _slot(mx, my, mc)

    pack1 = jnp.concatenate([c.reshape(-1), dn_conv.reshape(-1), gla_w_gate2.reshape(-1), ffn_conv.reshape(-1)])
    n1 = pack1.size
    rows1 = -(-n1 // 1024) * 8
    pack1 = jnp.pad(pack1, (0, rows1 * 128 - n1)).reshape(rows1, 128)
    got1 = _gather_small(pack1, "gather_cond").reshape(N_DEV, -1)
    o1 = bsz * D
    o2 = o1 + dn_conv.size
    o3 = o2 + gla_w_gate2.size
    c_all = got1[:, :o1].reshape(N_DEV * bsz, D)
    dn_conv_f = got1[:, o1:o2].reshape(N_DEV, DN_CONV_K, -1).transpose(1, 0, 2).reshape(DN_CONV_K, QKV_W)
    gate2_f = got1[:, o2:o3].reshape(N_DEV, GATE_RANK, -1).transpose(1, 0, 2).reshape(GATE_RANK, HEADS * GLA_KEY)
    ffn_conv_f = got1[:, o3:n1].reshape(N_DEV, FFN_CONV_K, -1).transpose(1, 0, 2).reshape(FFN_CONV_K, 2 * D_FF)

    win_t = w_in[0].T.astype(MXU_DT)
    wup_t = ffn_w_up[0].T.astype(MXU_DT)
    (win_all,) = _gather_big([win_t])
    win_p = _pad_in_rows(win_all)
    cw_p, cb_p = _ffn_pair(ffn_conv_f, 1), _ffn_pair(ffn_conv_b, 1)

    ncol = w_ada.shape[2]
    b_cols = lax.dynamic_slice_in_dim(b_ada, me * ncol, ncol, axis=1)
    mod_part = _ada_fwd(c_all, w_ada[0], b_cols)
    mod_all = _gather_small(mod_part.reshape(-1, 128), "gather_mod").reshape(N_DEV, N_DEV * bsz, ncol)
    mod = lax.dynamic_slice_in_dim(mod_all, me * bsz, bsz, axis=1).transpose(1, 0, 2).reshape(bsz, 6, 1, D)
    late = [w_o[0].astype(MXU_DT), wup_t, ffn_w_down[0].astype(MXU_DT)]
    ag_send, ag_recv, ag_src, ag_land, ag_token, _ = _exchange_start(
        "gather_start", late, [w.shape for w in late], [win_all, mod_all], by_owner=False)
    mod = mod + ag_token[0, 0]
    sh_a, sc_a, gt_a, sh_f, sc_f, gt_f = (mod[:, i] for i in range(6))

    g0, b0 = ln0_g.reshape(1, D), ln0_b.reshape(1, D)
    alog_row, dt_row = _lane_row(dn_a_log[0]), _lane_row(dn_dt_bias[0])
    grow_dn, grow_gla = jnp.tile(dn_norm_g, (1, HEADS)), jnp.tile(gla_norm_g, (1, HEADS))
    w2 = jnp.zeros((128, HEADS * HD), F32).at[SM_R:SM_R + GATE_RANK].set(_pad_heads(gate2_f, 1))
    bg = _pad_heads(gla_b_gate, 1)

    h_a = _ln0_mod(x, g0, b0, sc_a, sh_a)
    proj = _mm(h_a.reshape(ntok, D), win_p, "nt", F32, "mm_proj", tm=1024, tn=1408).reshape(bsz, t_total, P_W)
    q, k, v, gates, o_dn, s_dn, inv_dn, o_gla, s_gla = _rec_fwd(proj, dn_conv_f, alog_row, dt_row, w2, bg)
    o_mix = _mix_out_fwd(o_dn, o_gla, proj, grow_dn, grow_gla)
    late, landed = _exchange_wait("gather_wait", ag_send, ag_recv, ag_src, ag_land, o_mix, by_owner=False)
    wo_all, wup_all, wdn_all = (lax.dynamic_update_slice(l, w[None], (me, 0, 0)) for l, w in zip(landed, late))
    wo_f = wo_all.reshape(D, D)
    wup_f = _ffn_pair(wup_all.reshape(2 * D_FF, D), 0)
    wdn_f = wdn_all.reshape(D_FF, D)
    y = _mm(o_mix.reshape(ntok, D), wo_f, "nn", MXU_DT, "mm_wo", tm=1024, tn=1024).reshape(bsz, t_total, D)
    r1, h_f = _res_ln_mod(x, y, gt_a, g0, b0, ln1_g, ln1_b, sc_f, sh_f)
    up, act = _ffn_up_act(h_f, wup_f, cw_p, cb_p)
    y2 = _mm(act.reshape(ntok, D_FF), wdn_f, "nn", MXU_DT, "mm_down", tm=1024, tn=1024).reshape(bsz, t_total, D)
    loss_rows, dr2, dy2, dgt_f, d_ln2_g, d_ln2_b = _final_fwd_bwd(r1, y2, gt_f, ln1_g, ln1_b, ln2_g, ln2_b, loss_target)
    loss_part = (0.5 / D) * jnp.sum(loss_rows)

    dy2_2 = dy2.reshape(ntok, D)
    g_wdn = _mm(act.reshape(ntok, D_FF), dy2_2, "tn", MXU_DT, "mm_gwdn", tm=1408, tn=1024)
    dup, d_cw_p, d_cb_p = _ffn_act_bwd(up, dy2, wdn_f, cw_p, cb_p)
    d_ffn_conv, d_ffn_conv_b = _ffn_unpair(d_cw_p, 1), _ffn_unpair(d_cb_p, 1)
    dup_2 = dup.reshape(ntok, 2 * D_FF)
    dh_f = _mm(dup_2, wup_f, "nn", MXU_DT, "mm_dhf", tn=1024).reshape(bsz, t_total, D)
    g_wup_t = _mm(dup_2, h_f.reshape(ntok, D), "tn", MXU_DT, "mm_gwup", tm=1408, tn=1024)
    ffn_parts = [_ffn_unpair(g_wup_t, 0).reshape(N_DEV, -1, D), g_wdn.reshape(N_DEV, -1, D)]
    rs_send, rs_recv, rs_src, rs_land, rs_token, _ = _exchange_start(
        "scatter_start", ffn_parts, [p.shape[1:] for p in ffn_parts], [dh_f], by_owner=True)
    dr1, dsc_f, dsh_f, d_ln1_g, d_ln1_b, dy, dgt_a = _ln_bwd_call(
        "ln1_bwd", dr2, dh_f, r1, ln1_g, ln1_b, sc_f + rs_token[0, 0], y=y, gt=gt_a)

    dy_2 = dy.reshape(ntok, D)
    do = _mm(dy_2, wo_f, "nt", MXU_DT, "mm_do", tm=1024, tn=1024).reshape(bsz, t_total, D)
    g_wo = _mm(o_mix.reshape(ntok, D), dy_2, "tn", MXU_DT, "mm_gwo", tm=512, tn=1024)
    do_dn, do_gla, dz, dgg, d_dn_norm, d_gla_norm = _mix_out_bwd(do, o_dn, o_gla, proj, grow_dn, grow_gla)
    dq, dk, dv, dgates, dgq, dgk, dgv, dsm_gla, d_w2, d_bg = _rec_bwd(
        q, k, v, gates, s_dn, inv_dn, do_dn, proj, w2, bg, s_gla, do_gla)
    dqkv, dsm, d_dn_conv, d_alog_row, d_dt_row = _dn_pre_bwd(
        proj, dq, dk, dv, dgates, dsm_gla, dn_conv_f, alog_row, dt_row)
    dproj = jnp.concatenate([dqkv, dz, dgq, dgk, dgv, dgg, dsm], axis=-1).reshape(ntok, P_W)
    g_win_p = _mm(dproj, h_a.reshape(ntok, D), "tn", MXU_DT, "mm_gwin", tm=1408, tn=1024)
    mix_parts = [_unpad_in_rows(g_win_p), g_wo.reshape(N_DEV, -1, D)]
    rs2_send, rs2_recv, rs2_src, rs2_land, rs2_token, (win_p_late,) = _exchange_start(
        "scatter_mix_start", mix_parts, [p.shape[1:] for p in mix_parts], [], by_owner=True, carry=[win_p])
    dh_a = _mm(dproj, win_p_late, "nn", MXU_DT, "mm_dha", tn=1024).reshape(bsz, t_total, D)
    grad_x, dsc_a, dsh_a, d_ln0_g, d_ln0_b = _ln_bwd_call(
        "ln0_bwd", dr1, dh_a, x, g0, b0, sc_a + rs2_token[0, 0])

    delta, new_m, new_v, big_grads = {}, {}, {}, {}
    flip = lambda a: jnp.swapaxes(a, 1, 2)

    def update_owned(n, landed, mine):
        parts = lax.dynamic_update_slice(landed, lax.dynamic_slice_in_dim(mine, me, 1, axis=0), (me, 0, 0))
        turn = flip if parts.shape[1:] != w_given[n].shape[1:] else (lambda a: a)
        g, d_, m_, v_ = _sum_adamw(parts, turn(w_given[n]), turn(m_given[n]), turn(v_given[n]), "adamw_" + n)
        big_grads[n], delta[n], new_m[n], new_v[n] = turn(g[None]), turn(d_), turn(m_), turn(v_)

    ffn_parts, ffn_landed = _exchange_wait("scatter_wait", rs_send, rs_recv, rs_src, rs_land, grad_x, by_owner=True)
    update_owned("ffn_w_up", ffn_landed[0], ffn_parts[0])
    update_owned("ffn_w_down", ffn_landed[1], ffn_parts[1])
    ffn_done = 0.0 * (new_v["ffn_w_up"][0, 0, 0] + new_v["ffn_w_down"][0, 0, 0])

    dmod = jnp.concatenate([dsh_a, dsc_a, dgt_a, dsh_f, dsc_f, dgt_f], axis=1).reshape(-1)
    small_parts = {
        "ln0_g": d_ln0_g, "ln0_b": d_ln0_b, "ln1_g": d_ln1_g, "ln1_b": d_ln1_b, "ln2_g": d_ln2_g, "ln2_b": d_ln2_b,
        "dn_a_log": d_alog_row[:, :HEADS], "dn_dt_bias": d_dt_row[:, :HEADS],
        "dn_norm_g": d_dn_norm, "gla_norm_g": d_gla_norm, "gla_b_gate": _unpad_heads(d_bg, 1),
        "ffn_conv_b": d_ffn_conv_b, "dn_conv": d_dn_conv,
        "gla_w_gate2": _unpad_heads(d_w2[SM_R:SM_R + GATE_RANK], 1), "ffn_conv": d_ffn_conv}
    order = sorted(small_parts)
    flat = jnp.concatenate([small_parts[n].reshape(-1) for n in order] + [(loss_part + ffn_done).reshape(1), dmod])
    n3 = flat.size
    rows3 = -(-n3 // 1024) * 8
    pack3 = jnp.pad(flat, (0, rows3 * 128 - n3)).reshape(rows3, 128)
    got3 = _gather_small(pack3, "gather_small_grads")
    tot3 = _sum_slots(got3, "sum_small_grads").reshape(-1)
    grads = {}
    off = 0
    for n in order:
        size = small_parts[n].size
        grads[n] = tot3[off:off + size]
        off += size
    loss = tot3[off]
    off += 1
    dmod_all = got3.reshape(N_DEV, -1)[:, off:off + dmod.size].reshape(N_DEV * bsz, 6 * D)
    dmod_cols = lax.dynamic_slice_in_dim(dmod_all, me * ncol, ncol, axis=1)
    g_wada, g_bada = _ada_bwd(c_all, dmod_all, dmod_cols)
    grads["b_ada"] = g_bada

    def col_shard(full, rows):
        part = full.reshape(rows, -1)
        width = part.shape[1] // N_DEV
        return lax.dynamic_slice_in_dim(part, me * width, width, axis=1)

    grads["dn_conv"] = col_shard(grads["dn_conv"], DN_CONV_K)
    grads["gla_w_gate2"] = col_shard(grads["gla_w_gate2"], GATE_RANK)
    grads["ffn_conv"] = col_shard(grads["ffn_conv"], FFN_CONV_K)
    grads = {n: g.reshape(w_given[n].shape) for n, g in grads.items()}
    mix_parts, mix_landed = _exchange_wait("scatter_mix_wait", rs2_send, rs2_recv, rs2_src, rs2_land, grad_x,
                                           by_owner=True)
    update_owned("w_in", mix_landed[0], mix_parts[0])
    update_owned("w_o", mix_landed[1], mix_parts[1])
    grads["w_ada"] = g_wada.reshape(w_ada.shape)
    delta["w_ada"], new_m["w_ada"], new_v["w_ada"] = _adamw(w_ada, grads["w_ada"], m_w_ada, v_w_ada, "adamw_w_ada")
    grads.update(big_grads)
    d_s, m_s, v_s = _adamw_many(*[[src[n] for n in SMALL_NAMES] for src in (w_given, grads, m_given, v_given)],
                                "adamw_small")
    for i, n in enumerate(SMALL_NAMES):
        delta[n], new_m[n], new_v[n] = d_s[i], m_s[i], v_s[i]

    return (loss, grad_x, *[grads[n] for n in WEIGHTS], *[delta[n] for n in WEIGHTS],
            *[new_m[n] for n in WEIGHTS], *[new_v[n] for n in WEIGHTS])
```

```python
import jax
import jax.numpy as jnp
from jax import lax
from jax.experimental import pallas as pl
from jax.experimental.pallas import tpu as pltpu

F32 = jnp.float32
MXU_DT = jnp.bfloat16
MESH = pl.DeviceIdType.MESH
N_DEV = 8

D = 1024
HEADS = 4
HD = 128
CHUNK = 64
GLA_KEY = 64
GLA_TAU = 16.0
GATE_RANK = 16
D_FF = 2816
IN_W = 3608
ALPHA = 2.0 ** 0.25
EPS = 1e-6
DN_CONV_K = 4
FFN_CONV_K = 3
HALO = 8
ROW_TILE = 1024
FFN_ROW_TILE = 1024

P_QKV, P_Z, P_GQ, P_GK, P_GV, P_GG, P_SM, P_W = 0, 1536, 2048, 2560, 3072, 3584, 4096, 4224
SM_A, SM_B, SM_R = 0, 4, 8

ADAM_LR, ADAM_B1, ADAM_B2, ADAM_EPS, ADAM_WD, ADAM_STEP = 0.001, 0.9, 0.999, 1e-08, 0.01, 10

VMEM_LIMIT_V7X = 56 * 1024 * 1024


def _params(sem=None):
    return pltpu.CompilerParams(dimension_semantics=sem, vmem_limit_bytes=VMEM_LIMIT_V7X)


NN, NT, TN = ((1,), (0,)), ((1,), (1,)), ((0,), (0,))


def _dg(a, b, dims):
    return lax.dot_general(a, b, (dims, ((), ())), preferred_element_type=F32)


def _dot(a, b):
    return _dg(a, b, NN)


def _dot_nt(a, b):
    return _dg(a, b, NT)


def _dot_tn(a, b):
    return _dg(a, b, TN)


def _iota(shape, dim):
    return lax.broadcasted_iota(jnp.int32, shape, dim)


def _sigmoid(x):
    return jax.nn.sigmoid(x)


def _silu(x):
    return x * _sigmoid(x)


def _softplus(x):
    return jnp.maximum(x, 0.0) + jnp.log(1.0 + jnp.exp(-jnp.abs(x)))


def _ln_stats(x):
    mu = jnp.mean(x, axis=-1, keepdims=True)
    xc = x - mu
    rstd = lax.rsqrt(jnp.mean(xc * xc, axis=-1, keepdims=True) + EPS)
    return xc * rstd, rstd


def _ln_bwd(dxhat, xhat, rstd):
    return rstd * (dxhat - jnp.mean(dxhat, axis=-1, keepdims=True)
                   - xhat * jnp.mean(dxhat * xhat, axis=-1, keepdims=True))


def _split2(a):
    hi = a.astype(jnp.bfloat16)
    return hi, (a - hi.astype(F32)).astype(jnp.bfloat16)


def _d3(a, b, dims):
    ah, al = _split2(a)
    bh, bl = _split2(b)
    return _dg(ah, bh, dims) + (_dg(ah, bl, dims) + _dg(al, bh, dims))


@jax.custom_vjp
def _dot3(a, b):
    return _d3(a, b, NN)


_dot3.defvjp(lambda a, b: (_d3(a, b, NN), (a, b)),
             lambda res, g: (_d3(g, res[1], NT), _d3(res[0], g, TN)))


def _split3(b):
    b1 = b.astype(jnp.bfloat16)
    r1 = b - b1.astype(F32)
    b2 = r1.astype(jnp.bfloat16)
    return b1, b2, (r1 - b2.astype(F32)).astype(jnp.bfloat16)


def _sum3(fn, b):
    b1, b2, b3 = _split3(b)
    return fn(b1) + (fn(b2) + fn(b3))


@jax.custom_vjp
def _mask_dot(e, b):
    return _sum3(lambda t: _dg(e, t, NN), b)


_mask_dot.defvjp(lambda e, b: (_mask_dot(e, b), e),
                 lambda e, g: (jnp.zeros_like(e), _sum3(lambda t: _dg(e, t, TN), g)))


@jax.custom_vjp
def _mask_dot_nt(e, b):
    return _sum3(lambda t: _dg(e, t, NT), b)


_mask_dot_nt.defvjp(lambda e, b: (_mask_dot_nt(e, b), e),
                    lambda e, g: (jnp.zeros_like(e), _sum3(lambda t: _dg(t, e, TN), g)))


def _interleave(gens, shares):
    results = [None] * len(gens)
    live = list(range(len(gens)))
    while live:
        for i in list(live):
            for _ in range(shares[i]):
                try:
                    next(gens[i])
                except StopIteration as done:
                    results[i] = done.value
                    live.remove(i)
                    break
    return results


def _tri_inv_stages(ms):
    n = ms[0].shape[0]
    r, c = _iota((n, n), 0), _iota((n, n), 1)
    eye = (r == c).astype(F32)
    diag = (r >> 3) == (c >> 3)
    ds = [jnp.where(diag, m, 0.0) for m in ms]
    d2s = [_d3(d, d, NN) for d in ds]
    yield
    d4s = [_d3(d2, d2, NN) for d2 in d2s]
    invs = [_d3(eye - d, eye + d2, NN) for d, d2 in zip(ds, d2s)]
    yield
    invs = [_d3(inv, eye + d4, NN) for inv, d4 in zip(invs, d4s)]
    yield
    shift = 3
    while (1 << shift) < n:
        rb, cb = r >> shift, c >> shift
        sel = ((rb & 1) == 1) & (cb == rb - 1)
        tmp = [_d3(inv, jnp.where(sel, m, 0.0), NN) for inv, m in zip(invs, ms)]
        yield
        invs = [inv - _d3(t, inv, NN) for t, inv in zip(tmp, invs)]
        yield
        shift += 1
    return invs


def _tri_inv_bwd(invs, das):
    tmp = [_d3(a, da, TN) for a, da in zip(invs, das)]
    return ([-_d3(t, a, NT) for t, a in zip(tmp, invs)],)


@jax.custom_vjp
def _tri_inv_known(ms, invs):
    return invs


_tri_inv_known.defvjp(lambda ms, invs: (invs, invs),
                      lambda invs, das: (_tri_inv_bwd(invs, das)[0], [jnp.zeros_like(a) for a in invs]))


def _dn_chunk(s_list, q, k, v, gates, inv_known=None):
    nb = len(q)
    c = q[0].shape[0]
    r64, c64 = _iota((c, c), 0), _iota((c, c), 1)
    causal = r64 >= c64
    strict = r64 > c64
    tri = causal.astype(jnp.bfloat16)
    eye = (_iota((HD, HD), 0) == _iota((HD, HD), 1)).astype(jnp.bfloat16)
    lane = _iota(gates[0].shape, 1)
    lane1 = _iota((1, HD), 1)
    g_all = [_mask_dot(tri, g) for g in gates]
    yield
    g_all_t = [_mask_dot_nt(eye, g) for g in g_all]
    yield
    row = _iota(g_all_t[0].shape, 0)
    last = [jnp.sum(g, axis=0, keepdims=True) for g in gates]
    prob = [(b, h) for b in range(nb) for h in range(HEADS)]
    sl = [slice(h * HD, (h + 1) * HD) for h in range(HEADS)]
    qh = [q[b][:, sl[h]] for b, h in prob]
    kh = [k[b][:, sl[h]] for b, h in prob]
    vh = [v[b][:, sl[h]] for b, h in prob]
    s = [s_list[b][h] for b, h in prob]
    beta = [jnp.sum(jnp.where(lane == SM_B + h, gates[b], 0.0), axis=-1, keepdims=True) for b, h in prob]
    g_c = [jnp.sum(jnp.where(lane == SM_A + h, g_all[b], 0.0), axis=-1, keepdims=True) for b, h in prob]
    g_r = [jnp.sum(jnp.where(row == SM_A + h, g_all_t[b], 0.0), axis=0, keepdims=True) for b, h in prob]
    g_last = [jnp.sum(jnp.where(lane1 == SM_A + h, last[b], 0.0), axis=-1, keepdims=True) for b, h in prob]
    decay = [jnp.where(causal, jnp.exp(jnp.where(causal, gc - gr, 0.0)), 0.0) for gc, gr in zip(g_c, g_r)]
    kb = [k_ * b_ for k_, b_ in zip(kh, beta)]
    m_low = [jnp.where(strict, _dot_nt(kb_, k_) * d_, 0.0) for kb_, k_, d_ in zip(kb, kh, decay)]
    yield
    attn = [_dot_nt(q_, k_) * d_ for q_, k_, d_ in zip(qh, kh, decay)]
    yield
    if inv_known is None:
        a_inv = yield from _tri_inv_stages(m_low)
    else:
        a_inv = _tri_inv_known(m_low, inv_known)
    eg = [jnp.exp(gc) for gc in g_c]
    uw = [_dot3(a_, jnp.concatenate([v_ * b_, kb_ * e_], axis=1))
          for a_, v_, b_, kb_, e_ in zip(a_inv, vh, beta, kb, eg)]
    yield
    v_new = [uw_[:, :HD] - _dot(uw_[:, HD:], s_) for uw_, s_ in zip(uw, s)]
    yield
    qs = [_dot(q_ * e_, s_) for q_, e_, s_ in zip(qh, eg, s)]
    yield
    o = [qs_ + _dot(a_, vn_) for qs_, a_, vn_ in zip(qs, attn, v_new)]
    yield
    k_dec = [k_ * jnp.exp(gl - gc) for k_, gl, gc in zip(kh, g_last, g_c)]
    s_new = [s_ * jnp.exp(gl) + _dot_tn(kd_, vn_) for s_, gl, kd_, vn_ in zip(s, g_last, k_dec, v_new)]
    outs = [jnp.concatenate(o[b * HEADS:(b + 1) * HEADS], axis=-1) for b in range(nb)]
    states = [s_new[b * HEADS:(b + 1) * HEADS] for b in range(nb)]
    return outs, states, a_inv


def _gla_chunk(st_list, q, k, v, small, w2, bg):
    nb = len(q)
    c = q[0].shape[0]
    causal = _iota((c, c), 0) >= _iota((c, c), 1)
    tri = causal.astype(jnp.bfloat16)
    la_all = [-_softplus(-(_dot(sm, w2) + bg)) * (1.0 / GLA_TAU) for sm in small]
    yield
    b_all = [_mask_dot(tri, la) for la in la_all]
    yield
    prob = [(b, h) for b in range(nb) for h in range(HEADS)]
    sl = [slice(h * HD, (h + 1) * HD) for h in range(HEADS)]
    kh = [k[b][:, sl[h]] for b, h in prob]
    vh = [v[b][:, sl[h]] for b, h in prob]
    st = [st_list[b][h] for b, h in prob]
    bc = [b_all[b][:, sl[h]] for b, h in prob]
    b_last = [jnp.sum(la_all[b][:, sl[h]], axis=0, keepdims=True) for b, h in prob]
    q_dec = [q[b][:, sl[h]] * (GLA_KEY ** -0.5) * jnp.exp(bc_) for (b, h), bc_ in zip(prob, bc)]
    attn = [jnp.where(causal, _dot_nt(qd, k_ * jnp.exp(-bc_)), 0.0) for qd, k_, bc_ in zip(q_dec, kh, bc)]
    yield
    inter = [_dot_nt(qd, st_) for qd, st_ in zip(q_dec, st)]
    yield
    o = [i_ + _dot(a_, v_) for i_, a_, v_ in zip(inter, attn, vh)]
    yield
    k_dec = [k_ * jnp.exp(bl - bc_) for k_, bl, bc_ in zip(kh, b_last, bc)]
    s_new = [st_ * jnp.exp(bl) + _dot_tn(v_, kd) for st_, bl, v_, kd in zip(st, b_last, vh, k_dec)]
    outs = [jnp.concatenate(o[b * HEADS:(b + 1) * HEADS], axis=-1) for b in range(nb)]
    return outs, [s_new[b * HEADS:(b + 1) * HEADS] for b in range(nb)]


def _dn_qkv(y):
    act = _silu(y)
    parts = []
    for i in range(2 * HEADS):
        xh = act[:, i * HD:(i + 1) * HD]
        xh = xh * lax.rsqrt(jnp.sum(xh * xh, axis=-1, keepdims=True) + EPS)
        parts.append(xh * (HD ** -0.5) if i < HEADS else xh)
    qk = jnp.concatenate(parts, axis=-1)
    return qk[:, :HEADS * HD], qk[:, HEADS * HD:], act[:, 2 * HEADS * HD:]


def _dn_gates(small, alog_row, dt_row):
    lane = _iota(small.shape, 1)
    log_a = -jnp.exp(alog_row) * _softplus(small + dt_row)
    return jnp.where(lane < SM_B, log_a, jnp.where(lane < SM_R, _sigmoid(small), 0.0))


def _gate_norm(o, z, grow):
    parts = []
    for h in range(HEADS):
        oh = o[:, h * HD:(h + 1) * HD]
        parts.append(oh * lax.rsqrt(jnp.mean(oh * oh, axis=-1, keepdims=True) + EPS))
    return jnp.concatenate(parts, axis=-1) * grow * _silu(z)


def _conv_rows(xrows, w_ref, k_taps):
    n = xrows.shape[0]
    acc = xrows * w_ref[k_taps - 1:k_taps, :]
    for s in range(1, k_taps):
        acc = acc + pltpu.roll(xrows, s, 0) * w_ref[k_taps - 1 - s:k_taps - s, :]
    return acc


def _shift_up(x, s):
    return x if s == 0 else pltpu.roll(x, x.shape[0] - s, 0)


def _div_tile(n, cap, mult=8):
    best = None
    for t in range(mult, min(n, cap) + 1, mult):
        if n % t == 0:
            best = t
    return best if best is not None else n


def _halo_prev(tt):
    return lambda b, t: (b, jnp.maximum(t * (tt // HALO) - 1, 0))


def _halo_next(tt, t_total):
    return lambda b, t: (b, jnp.minimum((t + 1) * (tt // HALO), t_total // HALO - 1))


def _mm(a, b, mode, out_dtype, name, tm=512, tn=512, tk=None):
    if mode == "nn":
        (m, k), n = a.shape, b.shape[1]
    elif mode == "nt":
        (m, k), n = a.shape, b.shape[0]
    else:
        (k, m), n = a.shape, b.shape[1]
    tm, tn = min(tm, m), min(tn, n)
    tk = k if tk is None else min(tk, k)
    assert m % tm == 0 and n % tn == 0 and k % tk == 0, (name, a.shape, b.shape, tm, tn, tk)
    nk = k // tk
    if mode == "tn":
        a_spec = pl.BlockSpec((tk, tm), lambda i, j, kk: (kk, i))
    else:
        a_spec = pl.BlockSpec((tm, tk), lambda i, j, kk: (i, kk))
    if mode == "nt":
        b_spec = pl.BlockSpec((tn, tk), lambda i, j, kk: (j, kk))
    else:
        b_spec = pl.BlockSpec((tk, tn), lambda i, j, kk: (kk, j))
    dims = {"nn": NN, "nt": NT, "tn": TN}[mode]

    def body(a_ref, b_ref, o_ref, *acc):
        p = _dg(a_ref[...], b_ref[...], dims)
        if nk == 1:
            o_ref[...] = p.astype(out_dtype)
        else:
            kk = pl.program_id(2)

            @pl.when(kk == 0)
            def _():
                acc[0][...] = p

            @pl.when(kk > 0)
            def _():
                acc[0][...] += p

            @pl.when(kk == nk - 1)
            def _():
                o_ref[...] = acc[0][...].astype(out_dtype)

    return pl.pallas_call(
        body, name=name, grid=(m // tm, n // tn, nk),
        in_specs=[a_spec, b_spec],
        out_specs=pl.BlockSpec((tm, tn), lambda i, j, kk: (i, j)),
        out_shape=jax.ShapeDtypeStruct((m, n), out_dtype),
        scratch_shapes=[pltpu.VMEM((tm, tn), F32)] if nk > 1 else [],
        compiler_params=_params(("parallel", "parallel", "arbitrary")),
    )(a, b)


def _ada_fwd(c_all, w_ada, b_cols):
    def body(c_ref, w_ref, b_ref, o_ref):
        cond = _silu(c_ref[...]).astype(MXU_DT)
        o_ref[...] = _dot(cond, w_ref[...].astype(MXU_DT)) + b_ref[...]

    return pl.pallas_call(body, name="ada_fwd", out_shape=jax.ShapeDtypeStruct((c_all.shape[0], w_ada.shape[1]), F32),
                          compiler_params=_params())(c_all, w_ada, b_cols)


def _ada_bwd(c_all, dmod_all, dmod_cols):
    def body(c_ref, da_ref, dc_ref, gw_ref, gb_ref):
        cond = _silu(c_ref[...]).astype(MXU_DT)
        gw_ref[...] = _dot_tn(cond, dc_ref[...].astype(MXU_DT))
        gb_ref[...] = jnp.sum(da_ref[...], axis=0, keepdims=True)

    return pl.pallas_call(
        body, name="ada_bwd",
        out_shape=(jax.ShapeDtypeStruct((c_all.shape[1], dmod_cols.shape[1]), F32),
                   jax.ShapeDtypeStruct((1, dmod_all.shape[1]), F32)),
        compiler_params=_params())(c_all, dmod_all, dmod_cols)


def _tok_spec(tt, width=D):
    return pl.BlockSpec((1, tt, width), lambda b, t: (b, t, 0))


def _vec_spec(width=D):
    return pl.BlockSpec((1, width), lambda b, t: (0, 0))


def _bvec_spec(width=D):
    return pl.BlockSpec((1, 1, width), lambda b, t: (b, 0, 0))


def _ln0_mod(x, g0, b0, sc, sh):
    bsz, t_total, _ = x.shape
    tt = _div_tile(t_total, ROW_TILE)

    def body(x_ref, g_ref, b_ref, sc_ref, sh_ref, h_ref):
        xh, _ = _ln_stats(x_ref[0])
        x0 = xh * g_ref[...] + b_ref[...]
        h_ref[0] = (x0 * (1.0 + sc_ref[0]) + sh_ref[0]).astype(MXU_DT)

    return pl.pallas_call(
        body, name="ln0_mod", grid=(bsz, t_total // tt),
        in_specs=[_tok_spec(tt), _vec_spec(), _vec_spec(), _bvec_spec(), _bvec_spec()],
        out_specs=_tok_spec(tt), out_shape=jax.ShapeDtypeStruct(x.shape, MXU_DT),
        compiler_params=_params(("parallel", "parallel")))(x, g0, b0, sc, sh)


def _res_ln_mod(x, y, gt, g0, b0, g1, b1, sc, sh):
    bsz, t_total, _ = x.shape
    tt = _div_tile(t_total, ROW_TILE)

    def body(x_ref, y_ref, gt_ref, g0_ref, b0_ref, g1_ref, b1_ref, sc_ref, sh_ref, r_ref, h_ref):
        xh, _ = _ln_stats(x_ref[0])
        r = ALPHA * (xh * g0_ref[...] + b0_ref[...]) + (1.0 + gt_ref[0]) * y_ref[0].astype(F32)
        r_ref[0] = r
        rh, _ = _ln_stats(r)
        x1 = rh * g1_ref[...] + b1_ref[...]
        h_ref[0] = (x1 * (1.0 + sc_ref[0]) + sh_ref[0]).astype(MXU_DT)

    return pl.pallas_call(
        body, name="res_ln_mod", grid=(bsz, t_total // tt),
        in_specs=[_tok_spec(tt), _tok_spec(tt), _bvec_spec(), _vec_spec(), _vec_spec(), _vec_spec(), _vec_spec(),
                  _bvec_spec(), _bvec_spec()],
        out_specs=(_tok_spec(tt), _tok_spec(tt)),
        out_shape=(jax.ShapeDtypeStruct(x.shape, F32), jax.ShapeDtypeStruct(x.shape, MXU_DT)),
        compiler_params=_params(("parallel", "parallel")))(x, y, gt, g0, b0, g1, b1, sc, sh)


def _final_fwd_bwd(r1, y2, gt, g1, b1, g2, b2, target):
    bsz, t_total, _ = r1.shape
    tt = _div_tile(t_total, ROW_TILE)

    def body(r1_ref, y2_ref, gt_ref, g1_ref, b1_ref, g2_ref, b2_ref, tg_ref,
             loss_ref, dr2_ref, dy2_ref, dgt_ref, dg2_ref, db2_ref):
        b, t = pl.program_id(0), pl.program_id(1)

        @pl.when((b == 0) & (t == 0))
        def _():
            loss_ref[...] = jnp.zeros_like(loss_ref)
            dg2_ref[...] = jnp.zeros_like(dg2_ref)
            db2_ref[...] = jnp.zeros_like(db2_ref)

        @pl.when(t == 0)
        def _():
            dgt_ref[...] = jnp.zeros_like(dgt_ref)

        rh1, _ = _ln_stats(r1_ref[0])
        x1 = rh1 * g1_ref[...] + b1_ref[...]
        y2 = y2_ref[0].astype(F32)
        gate = 1.0 + gt_ref[0]
        xh2, rstd2 = _ln_stats(ALPHA * x1 + gate * y2)
        err = xh2 * g2_ref[...] + b2_ref[...] - tg_ref[0]
        loss_ref[...] += jnp.sum(err * err, axis=0, keepdims=True)
        dx2 = err * (1.0 / D)
        dg2_ref[...] += jnp.sum(dx2 * xh2, axis=0, keepdims=True)
        db2_ref[...] += jnp.sum(dx2, axis=0, keepdims=True)
        dr2 = _ln_bwd(dx2 * g2_ref[...], xh2, rstd2)
        dr2_ref[0] = dr2
        dy2_ref[0] = (gate * dr2).astype(MXU_DT)
        dgt_ref[0] += jnp.sum(dr2 * y2, axis=0, keepdims=True)

    vec_out = jax.ShapeDtypeStruct((1, D), F32)
    return pl.pallas_call(
        body, name="final_fwd_bwd", grid=(bsz, t_total // tt),
        in_specs=[_tok_spec(tt), _tok_spec(tt), _bvec_spec(), _vec_spec(), _vec_spec(), _vec_spec(), _vec_spec(),
                  _tok_spec(tt)],
        out_specs=(_vec_spec(), _tok_spec(tt), _tok_spec(tt), _bvec_spec(), _vec_spec(), _vec_spec()),
        out_shape=(vec_out, jax.ShapeDtypeStruct(r1.shape, F32), jax.ShapeDtypeStruct(r1.shape, MXU_DT),
                   jax.ShapeDtypeStruct((bsz, 1, D), F32), vec_out, vec_out),
        compiler_params=_params(("arbitrary", "arbitrary")))(r1, y2, gt, g1, b1, g2, b2, target)


def _ln_bwd_call(name, d_res, d_h, src, g, b, sc, y=None, gt=None):
    bsz, t_total, _ = src.shape
    tt = _div_tile(t_total, ROW_TILE)
    has_y = y is not None

    def body(*refs):
        if has_y:
            (dres_ref, dh_ref, src_ref, g_ref, b_ref, sc_ref, y_ref, gt_ref,
             dsrc_ref, dsc_ref, dsh_ref, dg_ref, db_ref, dy_ref, dgt_ref) = refs
        else:
            (dres_ref, dh_ref, src_ref, g_ref, b_ref, sc_ref,
             dsrc_ref, dsc_ref, dsh_ref, dg_ref, db_ref) = refs
        bi, t = pl.program_id(0), pl.program_id(1)

        @pl.when((bi == 0) & (t == 0))
        def _():
            dg_ref[...] = jnp.zeros_like(dg_ref)
            db_ref[...] = jnp.zeros_like(db_ref)

        @pl.when(t == 0)
        def _():
            dsc_ref[...] = jnp.zeros_like(dsc_ref)
            dsh_ref[...] = jnp.zeros_like(dsh_ref)
            if has_y:
                dgt_ref[...] = jnp.zeros_like(dgt_ref)

        xh, rstd = _ln_stats(src_ref[0])
        xv = xh * g_ref[...] + b_ref[...]
        dh = dh_ref[0].astype(F32)
        dx = ALPHA * dres_ref[0] + dh * (1.0 + sc_ref[0])
        dsc_ref[0] += jnp.sum(dh * xv, axis=0, keepdims=True)
        dsh_ref[0] += jnp.sum(dh, axis=0, keepdims=True)
        dg_ref[...] += jnp.sum(dx * xh, axis=0, keepdims=True)
        db_ref[...] += jnp.sum(dx, axis=0, keepdims=True)
        dsrc = _ln_bwd(dx * g_ref[...], xh, rstd)
        dsrc_ref[0] = dsrc
        if has_y:
            dy_ref[0] = ((1.0 + gt_ref[0]) * dsrc).astype(MXU_DT)
            dgt_ref[0] += jnp.sum(dsrc * y_ref[0].astype(F32), axis=0, keepdims=True)

    vec_out = jax.ShapeDtypeStruct((1, D), F32)
    bvec_out = jax.ShapeDtypeStruct((bsz, 1, D), F32)
    in_specs = [_tok_spec(tt), _tok_spec(tt), _tok_spec(tt), _vec_spec(), _vec_spec(), _bvec_spec()]
    out_specs = [_tok_spec(tt), _bvec_spec(), _bvec_spec(), _vec_spec(), _vec_spec()]
    out_shape = [jax.ShapeDtypeStruct(src.shape, F32), bvec_out, bvec_out, vec_out, vec_out]
    args = [d_res, d_h, src, g, b, sc]
    if has_y:
        in_specs += [_tok_spec(tt), _bvec_spec()]
        out_specs += [_tok_spec(tt), _bvec_spec()]
        out_shape += [jax.ShapeDtypeStruct(src.shape, MXU_DT), bvec_out]
        args += [y, gt]
    return pl.pallas_call(body, name=name, grid=(bsz, t_total // tt), in_specs=in_specs, out_specs=tuple(out_specs),
                          out_shape=tuple(out_shape), compiler_params=_params(("arbitrary", "arbitrary")))(*args)


FFN_TC = 256
FFN_NJ = D_FF // FFN_TC
FFN_PW = 2 * FFN_TC


def _ffn_pair(a, axis):
    shp = list(a.shape)
    a4 = a.reshape(shp[:axis] + [2, FFN_NJ, FFN_TC] + shp[axis + 1:])
    return jnp.swapaxes(a4, axis, axis + 1).reshape(shp)


def _ffn_unpair(a, axis):
    shp = list(a.shape)
    a4 = a.reshape(shp[:axis] + [FFN_NJ, 2, FFN_TC] + shp[axis + 1:])
    return jnp.swapaxes(a4, axis, axis + 1).reshape(shp)


def _ffn_up_act(h, w_up, cw, cb):
    bsz, t_total, _ = h.shape
    tt = _div_tile(t_total, FFN_ROW_TILE)
    def body(h_ref, wu_ref, w_ref, b_ref, up_ref, o_ref, carry_ref):
        up_t = _dot_nt(h_ref[0], wu_ref[...])
        up_ref[0] = up_t
        prev = jnp.where(pl.program_id(2) == 0, 0.0, carry_ref[...])
        rows = jnp.concatenate([prev, up_t], axis=0)
        u = _conv_rows(rows, w_ref, FFN_CONV_K)[HALO:] + b_ref[...]
        o_ref[0] = (_silu(u[:, :FFN_TC]) * u[:, FFN_TC:]).astype(MXU_DT)
        carry_ref[...] = up_t[tt - HALO:, :]

    return pl.pallas_call(
        body, name="ffn_up_act", grid=(bsz, FFN_NJ, t_total // tt),
        in_specs=[pl.BlockSpec((1, tt, D), lambda b, j, t: (b, t, 0)),
                  pl.BlockSpec((FFN_PW, D), lambda b, j, t: (j, 0)),
                  pl.BlockSpec((FFN_CONV_K, FFN_PW), lambda b, j, t: (0, j)),
                  pl.BlockSpec((1, FFN_PW), lambda b, j, t: (0, j))],
        out_specs=(pl.BlockSpec((1, tt, FFN_PW), lambda b, j, t: (b, t, j)),
                   pl.BlockSpec((1, tt, FFN_TC), lambda b, j, t: (b, t, j))),
        out_shape=(jax.ShapeDtypeStruct((bsz, t_total, 2 * D_FF), F32),
                   jax.ShapeDtypeStruct((bsz, t_total, D_FF), MXU_DT)),
        scratch_shapes=[pltpu.VMEM((HALO, FFN_PW), F32)],
        compiler_params=_params(("parallel", "parallel", "arbitrary")))(h, w_up, cw, cb)


HALO16 = 16


def _ffn_act_bwd(up, dy2, w_down, cw, cb):
    bsz, t_total, width = up.shape
    tt = _div_tile(t_total, FFN_ROW_TILE)
    nt = t_total // tt
    hp, hn = _halo_prev(tt), _halo_next(tt, t_total)

    def body(x_ref, xp_ref, xn_ref, dy_ref, dyn_ref, wd_ref, w_ref, b_ref, dup_ref, dw_ref, db_ref):
        b, t = pl.program_id(1), pl.program_id(2)

        @pl.when((b == 0) & (t == 0))
        def _():
            dw_ref[...] = jnp.zeros_like(dw_ref)
            db_ref[...] = jnp.zeros_like(db_ref)

        prev = jnp.where(t == 0, 0.0, xp_ref[0])
        rows = jnp.concatenate([prev, x_ref[0], xn_ref[0]], axis=0)
        u = _conv_rows(rows, w_ref, FFN_CONV_K)[HALO:] + b_ref[...]
        g_pre, v_pre = u[:, :FFN_TC], u[:, FFN_TC:]
        valid = (_iota((tt + HALO, 1), 0) < tt) | (t < nt - 1)
        da = jnp.concatenate([_dot_nt(dy_ref[0], wd_ref[...]), _dot_nt(dyn_ref[0], wd_ref[...])[:HALO]], axis=0)
        da_ext = jnp.where(valid, da, 0.0)
        sg = _sigmoid(g_pre)
        gs = g_pre * sg
        du = jnp.concatenate([da_ext * v_pre * (sg + gs * (1.0 - sg)), da_ext * gs], axis=1)
        dup = du * w_ref[FFN_CONV_K - 1:FFN_CONV_K, :]
        for s in range(1, FFN_CONV_K):
            dup = dup + _shift_up(du, s) * w_ref[FFN_CONV_K - 1 - s:FFN_CONV_K - s, :]
        dup_ref[0] = dup[:tt].astype(MXU_DT)
        du_t = du[:tt]
        db_ref[...] += jnp.sum(du_t, axis=0, keepdims=True)
        for k in range(FFN_CONV_K):
            s = FFN_CONV_K - 1 - k
            xs = (rows if s == 0 else pltpu.roll(rows, s, 0))[HALO:HALO + tt]
            dw_ref[k:k + 1, :] += jnp.sum(du_t * xs, axis=0, keepdims=True)

    def halo(h, w):
        return pl.BlockSpec((1, HALO, w), lambda j, b, t: (*h(b, t), j))

    wspec = lambda rows_: pl.BlockSpec((rows_, FFN_PW), lambda j, b, t: (0, j))
    tile = pl.BlockSpec((1, tt, FFN_PW), lambda j, b, t: (b, t, j))
    dy_next = lambda j, b, t: (b, jnp.minimum((t + 1) * (tt // HALO16), t_total // HALO16 - 1), 0)
    return pl.pallas_call(
        body, name="ffn_act_bwd", grid=(FFN_NJ, bsz, nt),
        in_specs=[tile, halo(hp, FFN_PW), halo(hn, FFN_PW),
                  pl.BlockSpec((1, tt, D), lambda j, b, t: (b, t, 0)), pl.BlockSpec((1, HALO16, D), dy_next),
                  pl.BlockSpec((FFN_TC, D), lambda j, b, t: (j, 0)), wspec(FFN_CONV_K), wspec(1)],
        out_specs=(tile, wspec(FFN_CONV_K), wspec(1)),
        out_shape=(jax.ShapeDtypeStruct(up.shape, MXU_DT), jax.ShapeDtypeStruct((FFN_CONV_K, width), F32),
                   jax.ShapeDtypeStruct((1, width), F32)),
        compiler_params=_params(("arbitrary", "arbitrary", "arbitrary")))(up, up, up, dy2, dy2, w_down, cw, cb)


QKV_W = 3 * HEADS * HD
SM_BLK = P_SM // 128


def _dn_pre_bwd(proj, dq, dk, dv, dgates, dsm_gla, conv_w, alog_row, dt_row):
    bsz, t_total, _ = proj.shape
    tt = _div_tile(t_total, 256)
    nt = t_total // tt
    hp, hn = _halo_prev(tt), _halo_next(tt, t_total)

    def body(x_ref, xp_ref, xn_ref, sm_ref, dq_ref, dqn_ref, dk_ref, dkn_ref, dv_ref, dvn_ref, dg_ref, dso_ref,
             w_ref, al_ref, dt_ref, dx_ref, dsm_ref, dw_ref, dal_ref, ddt_ref):
        b, t = pl.program_id(0), pl.program_id(1)

        @pl.when((b == 0) & (t == 0))
        def _():
            dw_ref[...] = jnp.zeros_like(dw_ref)
            dal_ref[...] = jnp.zeros_like(dal_ref)
            ddt_ref[...] = jnp.zeros_like(ddt_ref)

        prev = jnp.where(t == 0, 0.0, xp_ref[0])
        rows = jnp.concatenate([prev, x_ref[0], xn_ref[0]], axis=0)
        y = _conv_rows(rows, w_ref, DN_CONV_K)[HALO:]
        valid = (_iota((tt + HALO, 1), 0) < tt) | (t < nt - 1)

        def ext(tile_ref, next_ref):
            return jnp.where(valid, jnp.concatenate([tile_ref[0], next_ref[0]], axis=0), 0.0)

        _, vjp_qkv = jax.vjp(_dn_qkv, y)
        (dy,) = vjp_qkv((ext(dq_ref, dqn_ref), ext(dk_ref, dkn_ref), ext(dv_ref, dvn_ref)))
        dy = jnp.where(valid, dy, 0.0)
        dx = dy * w_ref[DN_CONV_K - 1:DN_CONV_K, :]
        for s in range(1, DN_CONV_K):
            dx = dx + _shift_up(dy, s) * w_ref[DN_CONV_K - 1 - s:DN_CONV_K - s, :]
        dx_ref[0] = dx[:tt].astype(MXU_DT)
        dy_t = dy[:tt]
        for k in range(DN_CONV_K):
            s = DN_CONV_K - 1 - k
            xs = (rows if s == 0 else pltpu.roll(rows, s, 0))[HALO:HALO + tt]
            dw_ref[k:k + 1, :] += jnp.sum(dy_t * xs, axis=0, keepdims=True)
        _, vjp_g = jax.vjp(_dn_gates, sm_ref[0], al_ref[...], dt_ref[...])
        dsm, dal, ddt = vjp_g(dg_ref[0])
        dsm_ref[0] = (dsm + dso_ref[0]).astype(MXU_DT)
        dal_ref[...] += dal
        ddt_ref[...] += ddt

    def tile(width, blk=0):
        return pl.BlockSpec((1, tt, width), lambda b, t: (b, t, blk))

    def halo(h, width):
        return pl.BlockSpec((1, HALO, width), lambda b, t: (*h(b, t), 0))

    return pl.pallas_call(
        body, name="dn_pre_bwd", grid=(bsz, nt),
        in_specs=[tile(QKV_W), halo(hp, QKV_W), halo(hn, QKV_W), tile(128, SM_BLK),
                  tile(512), halo(hn, 512), tile(512), halo(hn, 512), tile(512), halo(hn, 512), tile(128), tile(128),
                  pl.BlockSpec((DN_CONV_K, QKV_W), lambda b, t: (0, 0)), _vec_spec(128), _vec_spec(128)],
        out_specs=(tile(QKV_W), tile(128), pl.BlockSpec((DN_CONV_K, QKV_W), lambda b, t: (0, 0)),
                   _vec_spec(128), _vec_spec(128)),
        out_shape=(jax.ShapeDtypeStruct((bsz, t_total, QKV_W), MXU_DT),
                   jax.ShapeDtypeStruct((bsz, t_total, 128), MXU_DT),
                   jax.ShapeDtypeStruct((DN_CONV_K, QKV_W), F32), jax.ShapeDtypeStruct((1, 128), F32),
                   jax.ShapeDtypeStruct((1, 128), F32)),
        compiler_params=_params(("arbitrary", "arbitrary")))(
            proj, proj, proj, proj, dq, dq, dk, dk, dv, dv, dgates, dsm_gla, conv_w, alog_row, dt_row)


def _state_spec(bsz, idx):
    return pl.BlockSpec((bsz, 1, HEADS, HD, HD), lambda c: (0, idx(c), 0, 0, 0))


def _inv_spec(bsz, idx):
    return pl.BlockSpec((bsz, 1, HEADS, CHUNK, CHUNK), lambda c: (0, idx(c), 0, 0, 0))


def _chunk_spec(bsz, width, idx, blk=0):
    return pl.BlockSpec((bsz, CHUNK, width), lambda c: (0, idx(c), blk))


GQ_BLK, GK_BLK, GV_BLK = P_GQ // 512, P_GK // 512, P_GV // 512
REC_SHARES = (3, 1)


def _dn_prep(prev_rows, rows, small, w_ref, alog_row, dt_row):
    ys = [_conv_rows(jnp.concatenate([p, r], axis=0), w_ref, DN_CONV_K)[HALO:] for p, r in zip(prev_rows, rows)]
    yield
    qkv = [_dn_qkv(y) for y in ys]
    yield
    gates = [_dn_gates(s, alog_row, dt_row) for s in small]
    return [t[0] for t in qkv], [t[1] for t in qkv], [t[2] for t in qkv], gates


def _rec_fwd(proj, conv_w, alog_row, dt_row, w2, bg):
    bsz, t_total, _ = proj.shape
    nc = t_total // CHUNK
    fwd = lambda c: c
    nxt = lambda c: jnp.minimum(c + 1, nc - 1)

    def body(x0_ref, xn_ref, xnp_ref, smn_ref, cw_ref, al_ref, dt_ref,
             gq_ref, gk_ref, gv_ref, sm_ref, w2_ref, bg_ref,
             q_out, k_out, v_out, g_out, o_ref, ss_ref, inv_ref, go_ref, gss_ref,
             s_ref, gs_ref, nq_ref, nk_ref, nv_ref, ng_ref):
        seqs = range(bsz)
        heads = range(HEADS)
        per_seq = lambda ref: [ref[b] for b in seqs]

        def keep(prep):
            for b in seqs:
                nq_ref[b], nk_ref[b], nv_ref[b], ng_ref[b] = prep[0][b], prep[1][b], prep[2][b], prep[3][b]

        @pl.when(pl.program_id(0) == 0)
        def _():
            s_ref[...] = jnp.zeros_like(s_ref)
            gs_ref[...] = jnp.zeros_like(gs_ref)
            zeros = [jnp.zeros((HALO, QKV_W), F32) for _ in seqs]
            keep(_interleave([_dn_prep(zeros, per_seq(x0_ref), per_seq(sm_ref), cw_ref, al_ref[...], dt_ref[...])],
                             (1,))[0])

        q, k, v, gates = per_seq(nq_ref), per_seq(nk_ref), per_seq(nv_ref), per_seq(ng_ref)
        s_list = [[s_ref[b * HEADS + h] for h in heads] for b in seqs]
        gs_list = [[gs_ref[b * HEADS + h] for h in heads] for b in seqs]
        for b in seqs:
            q_out[b], k_out[b], v_out[b], g_out[b] = q[b], k[b], v[b], gates[b]
            for h in heads:
                ss_ref[b, 0, h] = s_list[b][h]
                gss_ref[b, 0, h] = gs_list[b][h]
        (o, new_s, invs), (go, new_gs), prep = _interleave(
            [_dn_chunk(s_list, q, k, v, gates),
             _gla_chunk(gs_list, per_seq(gq_ref), per_seq(gk_ref), per_seq(gv_ref), per_seq(sm_ref),
                        w2_ref[...], bg_ref[...]),
             _dn_prep(per_seq(xnp_ref), per_seq(xn_ref), per_seq(smn_ref), cw_ref, al_ref[...], dt_ref[...])],
            REC_SHARES + (1,))
        keep(prep)
        for b in seqs:
            o_ref[b] = o[b]
            go_ref[b] = go[b]
            for h in heads:
                s_ref[b * HEADS + h] = new_s[b][h]
                gs_ref[b * HEADS + h] = new_gs[b][h]
                inv_ref[b, 0, h] = invs[b * HEADS + h]

    tok = lambda width, blk=0, idx=fwd: _chunk_spec(bsz, width, idx, blk)
    state = jax.ShapeDtypeStruct((bsz, nc, HEADS, HD, HD), F32)
    out512 = jax.ShapeDtypeStruct((bsz, t_total, 512), F32)
    vec = lambda width: pl.BlockSpec((1, width), lambda c: (0, 0))
    return pl.pallas_call(
        body, name="rec_fwd", grid=(nc,),
        in_specs=[tok(QKV_W), tok(QKV_W, idx=nxt),
                  pl.BlockSpec((bsz, HALO, QKV_W), lambda c: (0, jnp.maximum(nxt(c) * (CHUNK // HALO) - 1, 0), 0)),
                  tok(128, SM_BLK, idx=nxt), pl.BlockSpec((DN_CONV_K, QKV_W), lambda c: (0, 0)), vec(128), vec(128),
                  tok(512, GQ_BLK), tok(512, GK_BLK), tok(512, GV_BLK), tok(128, SM_BLK),
                  pl.BlockSpec((128, 512), lambda c: (0, 0)), vec(512)],
        out_specs=(tok(512), tok(512), tok(512), tok(128),
                   tok(512), _state_spec(bsz, fwd), _inv_spec(bsz, fwd), tok(512), _state_spec(bsz, fwd)),
        out_shape=(out512, out512, out512, jax.ShapeDtypeStruct((bsz, t_total, 128), F32),
                   out512, state, jax.ShapeDtypeStruct((bsz, nc, HEADS, CHUNK, CHUNK), F32), out512, state),
        scratch_shapes=[pltpu.VMEM((bsz * HEADS, HD, HD), F32), pltpu.VMEM((bsz * HEADS, HD, HD), F32),
                        pltpu.VMEM((bsz, CHUNK, 512), F32), pltpu.VMEM((bsz, CHUNK, 512), F32),
                        pltpu.VMEM((bsz, CHUNK, 512), F32), pltpu.VMEM((bsz, CHUNK, 128), F32)],
        compiler_params=_params(("arbitrary",)))(
            proj, proj, proj, proj, conv_w, alog_row, dt_row, proj, proj, proj, proj, w2, bg)


def _rec_bwd(q, k, v, gates, s_dn, inv_dn, proj, w2, bg, s_gla, do, o_dn, o_gla, grow_dn, grow_gla):
    bsz, t_total, _ = q.shape
    nc = t_total // CHUNK
    rev = lambda c: nc - 1 - c
    nxt = lambda c: jnp.maximum(nc - 2 - c, 0)

    def body(q_ref, k_ref, v_ref, g_ref, ss_ref, inv_ref,
             gq_ref, gk_ref, gv_ref, sm_ref, w2_ref, bg_ref, gss_ref,
             do0_ref, od0_ref, og0_ref, z0_ref, gg0_ref, don_ref, odn_ref, ogn_ref, zn_ref, ggn_ref, gd_ref, gl_ref,
             dq_ref, dk_ref, dv_ref, dg_ref, dgq_ref, dgk_ref, dgv_ref, dsm_ref, dw2_ref, dbg_ref,
             dz_ref, dgg_ref, dgd_ref, dgl_ref,
             ds_ref, gds_ref, cdo_ref, cgdo_ref, cdz_ref, cdgg_ref):
        c = pl.program_id(0)
        seqs = range(bsz)
        heads = range(HEADS)
        per_seq = lambda ref: [ref[b] for b in seqs]

        def gate_bwd(do_r, od_r, og_r, z_r, gg_r, weight):
            for b in seqs:
                ct = do_r[b].astype(F32)
                for o_r, gate_r, g_r, part, keep_o, keep_gate, acc_ref in (
                        (od_r, z_r, gd_ref, ct[:, :512], cdo_ref, cdz_ref, dgd_ref),
                        (og_r, gg_r, gl_ref, ct[:, 512:], cgdo_ref, cdgg_ref, dgl_ref)):
                    _, vjp_gate = jax.vjp(_gate_norm, o_r[b], gate_r[b], g_r[...])
                    d_o, d_gate, d_row = vjp_gate(part)
                    keep_o[b], keep_gate[b] = d_o, d_gate
                    acc = d_row[:, :HD]
                    for h in range(1, HEADS):
                        acc = acc + d_row[:, h * HD:(h + 1) * HD]
                    acc_ref[...] += weight * acc

        @pl.when(c == 0)
        def _():
            ds_ref[...] = jnp.zeros_like(ds_ref)
            gds_ref[...] = jnp.zeros_like(gds_ref)
            dw2_ref[...] = jnp.zeros_like(dw2_ref)
            dbg_ref[...] = jnp.zeros_like(dbg_ref)
            dgd_ref[...] = jnp.zeros_like(dgd_ref)
            dgl_ref[...] = jnp.zeros_like(dgl_ref)
            gate_bwd(do0_ref, od0_ref, og0_ref, z0_ref, gg0_ref, 1.0)

        do_cur, gdo_cur = per_seq(cdo_ref), per_seq(cgdo_ref)
        for b in seqs:
            dz_ref[b] = cdz_ref[b].astype(MXU_DT)
            dgg_ref[b] = cdgg_ref[b].astype(MXU_DT)
        known = [inv_ref[b, 0, h] for b in seqs for h in heads]

        def both(s_list, q_, k_, v_, g_, gs_list, gq_, gk_, gv_, sm_, w2_, bg_):
            (o, new_s, _), (go, new_gs) = _interleave(
                [_dn_chunk(s_list, q_, k_, v_, g_, inv_known=known),
                 _gla_chunk(gs_list, gq_, gk_, gv_, sm_, w2_, bg_)], REC_SHARES)
            return o, new_s, go, new_gs

        _, vjp = jax.vjp(both, [[ss_ref[b, 0, h] for h in heads] for b in seqs],
                         per_seq(q_ref), per_seq(k_ref), per_seq(v_ref), per_seq(g_ref),
                         [[gss_ref[b, 0, h] for h in heads] for b in seqs],
                         per_seq(gq_ref), per_seq(gk_ref), per_seq(gv_ref), per_seq(sm_ref), w2_ref[...], bg_ref[...])
        ds_in, dq, dk, dv, dg, gds_in, dgq, dgk, dgv, dsm, dw2, dbg = vjp(
            (do_cur, [[ds_ref[b * HEADS + h] for h in heads] for b in seqs],
             gdo_cur, [[gds_ref[b * HEADS + h] for h in heads] for b in seqs]))
        gate_bwd(don_ref, odn_ref, ogn_ref, zn_ref, ggn_ref, jnp.where(c < nc - 1, 1.0, 0.0))
        for b in seqs:
            dq_ref[b], dk_ref[b], dv_ref[b], dg_ref[b] = dq[b], dk[b], dv[b], dg[b]
            dgq_ref[b], dgk_ref[b], dgv_ref[b] = dgq[b].astype(MXU_DT), dgk[b].astype(MXU_DT), dgv[b].astype(MXU_DT)
            dsm_ref[b] = dsm[b]
            for h in heads:
                ds_ref[b * HEADS + h] = ds_in[b][h]
                gds_ref[b * HEADS + h] = gds_in[b][h]
        dw2_ref[...] += dw2
        dbg_ref[...] += dbg

    tok = lambda width, blk=0, idx=rev: _chunk_spec(bsz, width, idx, blk)
    w2_spec = pl.BlockSpec((128, 512), lambda c: (0, 0))
    bg_spec = pl.BlockSpec((1, 512), lambda c: (0, 0))
    g128 = pl.BlockSpec((1, HD), lambda c: (0, 0))
    f512 = jax.ShapeDtypeStruct((bsz, t_total, 512), F32)
    b512 = jax.ShapeDtypeStruct((bsz, t_total, 512), MXU_DT)
    f128 = jax.ShapeDtypeStruct((bsz, t_total, 128), F32)
    gate_in = lambda idx: [tok(D, idx=idx), tok(512, idx=idx), tok(512, idx=idx),
                           tok(512, Z_BLK, idx), tok(512, GG_BLK, idx)]
    chunk_scratch = pltpu.VMEM((bsz, CHUNK, 512), F32)
    return pl.pallas_call(
        body, name="rec_bwd", grid=(nc,),
        in_specs=[tok(512), tok(512), tok(512), tok(128), _state_spec(bsz, rev), _inv_spec(bsz, rev),
                  tok(512, GQ_BLK), tok(512, GK_BLK), tok(512, GV_BLK), tok(128, SM_BLK), w2_spec, bg_spec,
                  _state_spec(bsz, rev), *gate_in(rev), *gate_in(nxt), bg_spec, bg_spec],
        out_specs=(tok(512), tok(512), tok(512), tok(128), tok(512), tok(512), tok(512), tok(128), w2_spec, bg_spec,
                   tok(512), tok(512), g128, g128),
        out_shape=(f512, f512, f512, f128, b512, b512, b512, f128,
                   jax.ShapeDtypeStruct((128, 512), F32), jax.ShapeDtypeStruct((1, 512), F32),
                   b512, b512, jax.ShapeDtypeStruct((1, HD), F32), jax.ShapeDtypeStruct((1, HD), F32)),
        scratch_shapes=[pltpu.VMEM((bsz * HEADS, HD, HD), F32), pltpu.VMEM((bsz * HEADS, HD, HD), F32),
                        chunk_scratch, chunk_scratch, chunk_scratch, chunk_scratch],
        compiler_params=_params(("arbitrary",)))(
            q, k, v, gates, s_dn, inv_dn, proj, proj, proj, proj, w2, bg, s_gla,
            do, o_dn, o_gla, proj, proj, do, o_dn, o_gla, proj, proj, grow_dn, grow_gla)


Z_BLK, GG_BLK = P_Z // 512, P_GG // 512


def _mix_out_fwd(o_dn, o_gla, proj, grow_dn, grow_gla):
    bsz, t_total, _ = o_dn.shape
    tt = _div_tile(t_total, ROW_TILE)

    def body(od_ref, og_ref, z_ref, gg_ref, gd_ref, gl_ref, o_ref):
        o_ref[0, :, :512] = _gate_norm(od_ref[0], z_ref[0], gd_ref[...]).astype(MXU_DT)
        o_ref[0, :, 512:] = _gate_norm(og_ref[0], gg_ref[0], gl_ref[...]).astype(MXU_DT)

    def col(blk):
        return pl.BlockSpec((1, tt, 512), lambda b, t: (b, t, blk))

    return pl.pallas_call(
        body, name="mix_out_fwd", grid=(bsz, t_total // tt),
        in_specs=[col(0), col(0), col(Z_BLK), col(GG_BLK), _vec_spec(512), _vec_spec(512)],
        out_specs=_tok_spec(tt), out_shape=jax.ShapeDtypeStruct((bsz, t_total, D), MXU_DT),
        compiler_params=_params(("parallel", "parallel")))(o_dn, o_gla, proj, proj, grow_dn, grow_gla)


def _mix_out_bwd(do, o_dn, o_gla, proj, grow_dn, grow_gla):
    bsz, t_total, _ = o_dn.shape
    tt = _div_tile(t_total, ROW_TILE)

    def body(do_ref, od_ref, og_ref, z_ref, gg_ref, gd_ref, gl_ref,
             dod_ref, dog_ref, dz_ref, dgg_ref, dgd_ref, dgl_ref):
        @pl.when((pl.program_id(0) == 0) & (pl.program_id(1) == 0))
        def _():
            dgd_ref[...] = jnp.zeros_like(dgd_ref)
            dgl_ref[...] = jnp.zeros_like(dgl_ref)

        def one(o_ref, gate_ref, g_ref, ct, do_out, dgate_out, dg_out):
            _, vjp = jax.vjp(_gate_norm, o_ref[0], gate_ref[0], g_ref[...])
            d_o, d_gate, d_row = vjp(ct)
            do_out[0] = d_o
            dgate_out[0] = d_gate.astype(MXU_DT)
            acc = d_row[:, :HD]
            for h in range(1, HEADS):
                acc = acc + d_row[:, h * HD:(h + 1) * HD]
            dg_out[...] += acc

        ct = do_ref[0].astype(F32)
        one(od_ref, z_ref, gd_ref, ct[:, :512], dod_ref, dz_ref, dgd_ref)
        one(og_ref, gg_ref, gl_ref, ct[:, 512:], dog_ref, dgg_ref, dgl_ref)

    def col(blk):
        return pl.BlockSpec((1, tt, 512), lambda b, t: (b, t, blk))

    f512 = jax.ShapeDtypeStruct((bsz, t_total, 512), F32)
    b512 = jax.ShapeDtypeStruct((bsz, t_total, 512), MXU_DT)
    g128 = jax.ShapeDtypeStruct((1, HD), F32)
    return pl.pallas_call(
        body, name="mix_out_bwd", grid=(bsz, t_total // tt),
        in_specs=[_tok_spec(tt), col(0), col(0), col(Z_BLK), col(GG_BLK), _vec_spec(512), _vec_spec(512)],
        out_specs=(col(0), col(0), col(0), col(0), _vec_spec(HD), _vec_spec(HD)),
        out_shape=(f512, f512, b512, b512, g128, g128),
        compiler_params=_params(("arbitrary", "arbitrary")))(do, o_dn, o_gla, proj, proj, grow_dn, grow_gla)


def _sum_slots(x, name):
    n, rows, cols = x.shape
    tr = _div_tile(rows, max(8, (1 << 19) // cols))

    def body(x_ref, o_ref):
        acc = x_ref[0].astype(F32)
        for i in range(1, n):
            acc = acc + x_ref[i].astype(F32)
        o_ref[...] = acc

    return pl.pallas_call(
        body, name=name, grid=(rows // tr,),
        in_specs=[pl.BlockSpec((n, tr, cols), lambda i: (0, i, 0))],
        out_specs=pl.BlockSpec((tr, cols), lambda i: (i, 0)),
        out_shape=jax.ShapeDtypeStruct((rows, cols), F32), compiler_params=_params(("parallel",)))(x)


def _adamw_math(w, g, m, v):
    nm = ADAM_B1 * m + (1.0 - ADAM_B1) * g
    nv = ADAM_B2 * v + (1.0 - ADAM_B2) * (g * g)
    m_hat = nm / (1.0 - ADAM_B1 ** ADAM_STEP)
    v_hat = nv / (1.0 - ADAM_B2 ** ADAM_STEP)
    return -ADAM_LR * (m_hat / (jnp.sqrt(v_hat) + ADAM_EPS) + ADAM_WD * w), nm, nv


def _adamw(w, g, m, v, name):
    _, rows, cols = w.shape
    tr = _div_tile(rows, max(8, (1 << 18) // cols))

    def body(w_ref, g_ref, m_ref, v_ref, d_ref, nm_ref, nv_ref):
        d_ref[...], nm_ref[...], nv_ref[...] = _adamw_math(w_ref[...], g_ref[...], m_ref[...], v_ref[...])

    spec = pl.BlockSpec((1, tr, cols), lambda i: (0, i, 0))
    shp = jax.ShapeDtypeStruct(w.shape, F32)
    return pl.pallas_call(body, name=name, grid=(rows // tr,), in_specs=[spec] * 4, out_specs=(spec,) * 3,
                          out_shape=(shp,) * 3, compiler_params=_params(("parallel",)))(w, g, m, v)


def _sum_adamw(parts, w, m, v, name):
    n, rows, cols = parts.shape
    tr = _div_tile(rows, max(8, (1 << 18) // cols))

    def body(p_ref, w_ref, m_ref, v_ref, g_ref, d_ref, nm_ref, nv_ref):
        g = p_ref[0].astype(F32)
        for i in range(1, n):
            g = g + p_ref[i].astype(F32)
        g_ref[...] = g
        d_ref[0], nm_ref[0], nv_ref[0] = _adamw_math(w_ref[0], g, m_ref[0], v_ref[0])

    spec = pl.BlockSpec((1, tr, cols), lambda i: (0, i, 0))
    shp = jax.ShapeDtypeStruct(w.shape, F32)
    return pl.pallas_call(
        body, name=name, grid=(rows // tr,),
        in_specs=[pl.BlockSpec((n, tr, cols), lambda i: (0, i, 0)), spec, spec, spec],
        out_specs=(pl.BlockSpec((tr, cols), lambda i: (i, 0)), spec, spec, spec),
        out_shape=(jax.ShapeDtypeStruct((rows, cols), F32), shp, shp, shp),
        compiler_params=_params(("parallel",)))(parts, w, m, v)


def _adamw_many(ws, gs, ms, vs, name):
    n = len(ws)

    def body(*refs):
        for i in range(n):
            d, nm, nv = _adamw_math(refs[i][...], refs[n + i][...], refs[2 * n + i][...], refs[3 * n + i][...])
            refs[4 * n + i][...] = d
            refs[5 * n + i][...] = nm
            refs[6 * n + i][...] = nv

    shapes = tuple(jax.ShapeDtypeStruct(w.shape, F32) for w in ws)
    outs = pl.pallas_call(body, name=name, out_shape=shapes * 3, compiler_params=_params())(*ws, *gs, *ms, *vs)
    return outs[:n], outs[n:2 * n], outs[2 * n:]


def _position():
    return lax.axis_index("x"), lax.axis_index("y"), lax.axis_index("c")


def _slot(px, py, pc):
    return 4 * px + 2 * py + pc


def _gather_small(x, name):
    rows, cols = x.shape

    def body(x_ref, o_ref, send_sems, recv_sems):
        mx, my, mc = _position()

        def peer(k):
            return (mx ^ ((k >> 2) & 1), my ^ ((k >> 1) & 1), mc ^ (k & 1))

        o_ref[_slot(mx, my, mc)] = x_ref[...]
        sends = []
        for k in range(1, N_DEV):
            cp = pltpu.make_async_remote_copy(src_ref=x_ref, dst_ref=o_ref.at[_slot(mx, my, mc)],
                                              send_sem=send_sems.at[k - 1], recv_sem=recv_sems.at[k - 1],
                                              device_id=peer(k), device_id_type=MESH)
            cp.start()
            sends.append(cp)
        for k in range(1, N_DEV):
            pltpu.make_async_remote_copy(src_ref=x_ref, dst_ref=o_ref.at[_slot(*peer(k))],
                                         send_sem=send_sems.at[k - 1], recv_sem=recv_sems.at[k - 1],
                                         device_id=peer(k), device_id_type=MESH).wait_recv()
        for cp in sends:
            cp.wait_send()

    return pl.pallas_call(
        body, name=name, out_shape=jax.ShapeDtypeStruct((N_DEV, rows, cols), x.dtype),
        in_specs=[pl.BlockSpec(memory_space=pltpu.VMEM)], out_specs=pl.BlockSpec(memory_space=pltpu.VMEM),
        scratch_shapes=[pltpu.SemaphoreType.DMA((N_DEV - 1,)), pltpu.SemaphoreType.DMA((N_DEV - 1,))],
        compiler_params=pltpu.CompilerParams(vmem_limit_bytes=VMEM_LIMIT_V7X))(x)


def _gather_big(shards):
    n = len(shards)

    def body(*refs):
        xs, outs = refs[:n], refs[n:2 * n]
        send_sems, recv_sems, local_sems = refs[2 * n:]
        mx, my, mc = _position()
        me, sibling = (mx, my, mc), (mx, my, 1 - mc)
        chips = [(1 - mx, my), (mx, 1 - my), (1 - mx, 1 - my)]

        def copy(a, k, block, to, src=None):
            dst = outs[a].at[_slot(*block)]
            return pltpu.make_async_remote_copy(src_ref=dst if src is None else src, dst_ref=dst,
                                                send_sem=send_sems.at[7 * a + k], recv_sem=recv_sems.at[7 * a + k],
                                                device_id=to, device_id_type=MESH)

        mine = [pltpu.make_async_copy(xs[a], outs[a].at[_slot(*me)], local_sems.at[a]) for a in range(n)]
        for cp in mine:
            cp.start()
        started = []
        for a in range(n):
            started.append(copy(a, 0, me, sibling, src=xs[a]))
            started += [copy(a, 1 + j, me, (*chip, mc), src=xs[a]) for j, chip in enumerate(chips)]
        for cp in started:
            cp.start()
        for j, chip in enumerate(chips):
            for a in range(n):
                copy(a, 1 + j, (*chip, mc), me).wait_recv()
                fwd = copy(a, 4 + j, (*chip, mc), sibling)
                fwd.start()
                started.append(fwd)
        for a in range(n):
            copy(a, 0, sibling, me).wait_recv()
            for j, chip in enumerate(chips):
                copy(a, 4 + j, (*chip, 1 - mc), me).wait_recv()
        for cp in started:
            cp.wait_send()
        for cp in mine:
            cp.wait()

    any_spec = pl.BlockSpec(memory_space=pl.ANY)
    return pl.pallas_call(
        body, name="gather_weights",
        out_shape=tuple(jax.ShapeDtypeStruct((N_DEV,) + s.shape, s.dtype) for s in shards),
        in_specs=[any_spec] * n, out_specs=(any_spec,) * n,
        scratch_shapes=[pltpu.SemaphoreType.DMA((7 * n,)), pltpu.SemaphoreType.DMA((7 * n,)),
                        pltpu.SemaphoreType.DMA((n,))])(*shards)


def _peer(pos, k):
    mx, my, mc = pos
    return (mx ^ ((k >> 2) & 1), my ^ ((k >> 1) & 1), mc ^ (k & 1))


def _exchange_copies(srcs, lands, send_sems, recv_sems, by_owner, arrivals):
    pos = _position()
    me = _slot(*pos)
    out = []
    for a, (src, land) in enumerate(zip(srcs, lands)):
        for k in range(1, N_DEV):
            peer = _peer(pos, k)
            mine = src.at[_slot(*peer)] if by_owner else src
            out.append(pltpu.make_async_remote_copy(
                src_ref=mine, dst_ref=land.at[_slot(*peer) if arrivals else me],
                send_sem=send_sems.at[7 * a + k - 1], recv_sem=recv_sems.at[7 * a + k - 1],
                device_id=peer, device_id_type=MESH))
    return out


_HBM_SPEC = pl.BlockSpec(memory_space=pltpu.HBM)
_SEM_SPEC = pl.BlockSpec(memory_space=pltpu.SEMAPHORE)
_DATAFLOW = pltpu.SideEffectType.DATAFLOW_SIDE_EFFECTING


def _exchange_start(name, srcs, slab_shapes, after, by_owner, carry=()):
    n, na, nc = len(srcs), len(after), len(carry)
    lands = [pltpu.with_memory_space_constraint(lax.empty((N_DEV,) + s, x.dtype), pltpu.HBM)
             for s, x in zip(slab_shapes, srcs)]
    thru = [pltpu.with_memory_space_constraint(x, pltpu.HBM) for x in [*srcs, *lands, *carry]]

    def body(*refs):
        src_refs, land_refs = refs[:n], refs[n:2 * n]
        send_sems, recv_sems = refs[len(thru) + na], refs[len(thru) + na + 1]
        token = refs[-1]
        for send in _exchange_copies(src_refs, land_refs, send_sems, recv_sems, by_owner, arrivals=False):
            send.start()
        token[...] = jnp.zeros_like(token)

    outs = pl.pallas_call(
        body, name=name,
        out_shape=(pltpu.SemaphoreType.DMA((7 * n,)), pltpu.SemaphoreType.DMA((7 * n,)),
                   *[pltpu.HBM(x.shape, x.dtype) for x in thru], jax.ShapeDtypeStruct((8, 128), F32)),
        in_specs=[_HBM_SPEC] * len(thru) + [pl.BlockSpec(memory_space=pl.ANY)] * na,
        out_specs=(_SEM_SPEC, _SEM_SPEC, *[_HBM_SPEC] * len(thru), pl.BlockSpec(memory_space=pltpu.VMEM)),
        input_output_aliases={i: 2 + i for i in range(len(thru))},
        compiler_params=pltpu.CompilerParams(has_side_effects=_DATAFLOW))(*thru, *after)
    return (outs[0], outs[1], list(outs[2:2 + n]), list(outs[2 + n:2 + 2 * n]), outs[-1],
            list(outs[2 + 2 * n:2 + 2 * n + nc]))


def _exchange_wait(name, send_sems, recv_sems, srcs, lands, after, by_owner):
    n = len(srcs)

    def body(*refs):
        src_refs, land_refs = refs[:n], refs[n:2 * n]
        s_sems, r_sems = refs[2 * n], refs[2 * n + 1]
        for send in _exchange_copies(src_refs, land_refs, s_sems, r_sems, by_owner, arrivals=False):
            send.wait_send()
        for recv in _exchange_copies(src_refs, land_refs, s_sems, r_sems, by_owner, arrivals=True):
            recv.wait_recv()

    outs = pl.pallas_call(
        body, name=name,
        out_shape=(*[pltpu.HBM(x.shape, x.dtype) for x in srcs], *[pltpu.HBM(l.shape, l.dtype) for l in lands]),
        in_specs=[_HBM_SPEC] * (2 * n) + [_SEM_SPEC, _SEM_SPEC, pl.BlockSpec(memory_space=pl.ANY)],
        out_specs=tuple([_HBM_SPEC] * (2 * n)),
        input_output_aliases={i: i for i in range(2 * n)},
        compiler_params=pltpu.CompilerParams(has_side_effects=_DATAFLOW))(*srcs, *lands, send_sems, recv_sems, after)
    return list(outs[:n]), list(outs[n:])


def _pad_heads(x, axis):
    shp = list(x.shape)
    x4 = x.reshape(shp[:axis] + [HEADS, GLA_KEY] + shp[axis + 1:])
    pad = [(0, 0)] * x4.ndim
    pad[axis + 1] = (0, HD - GLA_KEY)
    return jnp.pad(x4, pad).reshape(shp[:axis] + [HEADS * HD] + shp[axis + 1:])


def _unpad_heads(x, axis):
    shp = list(x.shape)
    x4 = x.reshape(shp[:axis] + [HEADS, HD] + shp[axis + 1:])
    x4 = lax.slice_in_dim(x4, 0, GLA_KEY, axis=axis + 1)
    return x4.reshape(shp[:axis] + [HEADS * GLA_KEY] + shp[axis + 1:])


O_Z_END, O_AB, O_GQ, O_GK, O_GV, O_R = 2048, 2048, 2056, 2312, 2568, 3592


def _padded_row(f):
    if f < O_Z_END:
        return f
    if f < O_GQ:
        return P_SM + (f - O_AB)
    if f < O_GV:
        base, g = (P_GQ, f - O_GQ) if f < O_GK else (P_GK, f - O_GK)
        return base + HD * (g // GLA_KEY) + g % GLA_KEY
    if f < O_R:
        return P_GV + (f - O_GV)
    return P_SM + 8 + (f - O_R)


def _runs(pairs):
    out = []
    for d, s in pairs:
        if out and out[-1][0] + out[-1][2] == d and out[-1][1] + out[-1][2] == s:
            out[-1][2] += 1
        else:
            out.append([d, s, 1])
    return out


def _pad_in_rows(shards):
    wt = shards.reshape(IN_W, D)
    return jnp.concatenate([
        wt[:O_Z_END], _pad_heads(wt[O_GQ:O_GK], 0), _pad_heads(wt[O_GK:O_GV], 0), wt[O_GV:O_R],
        wt[O_AB:O_GQ], wt[O_R:], jnp.zeros((P_W - P_SM - 8 - GATE_RANK, D), wt.dtype)], axis=0)


def _unpad_in_rows(gt):
    per = IN_W // N_DEV
    return jnp.stack([
        jnp.concatenate([gt[src:src + n] for _, src, n in
                         _runs([(f, _padded_row(f)) for f in range(j * per, (j + 1) * per)])], axis=0)
        for j in range(N_DEV)])


def _lane_row(vals, width=128):
    return jnp.pad(vals.reshape(1, -1), ((0, 0), (0, width - vals.size)))


SMALL_NAMES = ["ln0_g", "ln0_b", "b_ada", "dn_conv", "dn_a_log", "dn_dt_bias", "dn_norm_g", "gla_w_gate2",
               "gla_b_gate", "gla_norm_g", "ln1_g", "ln1_b", "ffn_conv", "ffn_conv_b", "ln2_g", "ln2_b"]
WEIGHTS = ["ln0_g", "ln0_b", "w_ada", "b_ada", "w_in", "dn_conv", "dn_a_log", "dn_dt_bias", "dn_norm_g",
           "gla_w_gate2", "gla_b_gate", "gla_norm_g", "w_o", "ln1_g", "ln1_b", "ffn_w_up", "ffn_conv", "ffn_conv_b",
           "ffn_w_down", "ln2_g", "ln2_b"]


def kernel(x, c, ln0_g, ln0_b, w_ada, b_ada, w_in, dn_conv, dn_a_log, dn_dt_bias, dn_norm_g, gla_w_gate2, gla_b_gate, gla_norm_g, w_o, ln1_g, ln1_b, ffn_w_up, ffn_conv, ffn_conv_b, ffn_w_down, ln2_g, ln2_b, loss_target, m_ln0_g, m_ln0_b, m_w_ada, m_b_ada, m_w_in, m_dn_conv, m_dn_a_log, m_dn_dt_bias, m_dn_norm_g, m_gla_w_gate2, m_gla_b_gate, m_gla_norm_g, m_w_o, m_ln1_g, m_ln1_b, m_ffn_w_up, m_ffn_conv, m_ffn_conv_b, m_ffn_w_down, m_ln2_g, m_ln2_b, v_ln0_g, v_ln0_b, v_w_ada, v_b_ada, v_w_in, v_dn_conv, v_dn_a_log, v_dn_dt_bias, v_dn_norm_g, v_gla_w_gate2, v_gla_b_gate, v_gla_norm_g, v_w_o, v_ln1_g, v_ln1_b, v_ffn_w_up, v_ffn_conv, v_ffn_conv_b, v_ffn_w_down, v_ln2_g, v_ln2_b):
    args = dict(locals())
    w_given = {n: args[n] for n in WEIGHTS}
    m_given = {n: args["m_" + n] for n in WEIGHTS}
    v_given = {n: args["v_" + n] for n in WEIGHTS}
    bsz, t_total, _ = x.shape
    ntok = bsz * t_total
    mx, my, mc = _position()
    me = _slot(mx, my, mc)

    pack1 = jnp.concatenate([c.reshape(-1), dn_conv.reshape(-1), gla_w_gate2.reshape(-1), ffn_conv.reshape(-1)])
    n1 = pack1.size
    rows1 = -(-n1 // 1024) * 8
    pack1 = jnp.pad(pack1, (0, rows1 * 128 - n1)).reshape(rows1, 128)
    got1 = _gather_small(pack1, "gather_cond").reshape(N_DEV, -1)
    o1 = bsz * D
    o2 = o1 + dn_conv.size
    o3 = o2 + gla_w_gate2.size
    c_all = got1[:, :o1].reshape(N_DEV * bsz, D)
    dn_conv_f = got1[:, o1:o2].reshape(N_DEV, DN_CONV_K, -1).transpose(1, 0, 2).reshape(DN_CONV_K, QKV_W)
    gate2_f = got1[:, o2:o3].reshape(N_DEV, GATE_RANK, -1).transpose(1, 0, 2).reshape(GATE_RANK, HEADS * GLA_KEY)
    ffn_conv_f = got1[:, o3:n1].reshape(N_DEV, FFN_CONV_K, -1).transpose(1, 0, 2).reshape(FFN_CONV_K, 2 * D_FF)

    win_t = w_in[0].T.astype(MXU_DT)
    wup_t = ffn_w_up[0].T.astype(MXU_DT)
    (win_all,) = _gather_big([win_t])
    win_p = _pad_in_rows(win_all)
    cw_p, cb_p = _ffn_pair(ffn_conv_f, 1), _ffn_pair(ffn_conv_b, 1)

    ncol = w_ada.shape[2]
    b_cols = lax.dynamic_slice_in_dim(b_ada, me * ncol, ncol, axis=1)
    mod_part = _ada_fwd(c_all, w_ada[0], b_cols)
    mod_all = _gather_small(mod_part.reshape(-1, 128), "gather_mod").reshape(N_DEV, N_DEV * bsz, ncol)
    mod = lax.dynamic_slice_in_dim(mod_all, me * bsz, bsz, axis=1).transpose(1, 0, 2).reshape(bsz, 6, 1, D)
    late = [w_o[0].astype(MXU_DT), wup_t, ffn_w_down[0].astype(MXU_DT)]
    ag_send, ag_recv, ag_src, ag_land, ag_token, _ = _exchange_start(
        "gather_start", late, [w.shape for w in late], [win_all, mod_all], by_owner=False)
    mod = mod + ag_token[0, 0]
    sh_a, sc_a, gt_a, sh_f, sc_f, gt_f = (mod[:, i] for i in range(6))

    g0, b0 = ln0_g.reshape(1, D), ln0_b.reshape(1, D)
    alog_row, dt_row = _lane_row(dn_a_log[0]), _lane_row(dn_dt_bias[0])
    grow_dn, grow_gla = jnp.tile(dn_norm_g, (1, HEADS)), jnp.tile(gla_norm_g, (1, HEADS))
    w2 = jnp.zeros((128, HEADS * HD), F32).at[SM_R:SM_R + GATE_RANK].set(_pad_heads(gate2_f, 1))
    bg = _pad_heads(gla_b_gate, 1)

    h_a = _ln0_mod(x, g0, b0, sc_a, sh_a)
    proj = _mm(h_a.reshape(ntok, D), win_p, "nt", F32, "mm_proj", tm=1024, tn=1408).reshape(bsz, t_total, P_W)
    q, k, v, gates, o_dn, s_dn, inv_dn, o_gla, s_gla = _rec_fwd(proj, dn_conv_f, alog_row, dt_row, w2, bg)
    o_mix = _mix_out_fwd(o_dn, o_gla, proj, grow_dn, grow_gla)
    late, landed = _exchange_wait("gather_wait", ag_send, ag_recv, ag_src, ag_land, o_mix, by_owner=False)
    wo_all, wup_all, wdn_all = (lax.dynamic_update_slice(l, w[None], (me, 0, 0)) for l, w in zip(landed, late))
    wo_f = wo_all.reshape(D, D)
    wup_f = _ffn_pair(wup_all.reshape(2 * D_FF, D), 0)
    wdn_f = wdn_all.reshape(D_FF, D)
    y = _mm(o_mix.reshape(ntok, D), wo_f, "nn", MXU_DT, "mm_wo", tm=1024, tn=1024).reshape(bsz, t_total, D)
    r1, h_f = _res_ln_mod(x, y, gt_a, g0, b0, ln1_g, ln1_b, sc_f, sh_f)
    up, act = _ffn_up_act(h_f, wup_f, cw_p, cb_p)
    y2 = _mm(act.reshape(ntok, D_FF), wdn_f, "nn", MXU_DT, "mm_down", tm=1024, tn=1024).reshape(bsz, t_total, D)
    loss_rows, dr2, dy2, dgt_f, d_ln2_g, d_ln2_b = _final_fwd_bwd(r1, y2, gt_f, ln1_g, ln1_b, ln2_g, ln2_b, loss_target)
    loss_part = (0.5 / D) * jnp.sum(loss_rows)

    dy2_2 = dy2.reshape(ntok, D)
    g_wdn = _mm(act.reshape(ntok, D_FF), dy2_2, "tn", MXU_DT, "mm_gwdn", tm=1408, tn=1024)
    dup, d_cw_p, d_cb_p = _ffn_act_bwd(up, dy2, wdn_f, cw_p, cb_p)
    d_ffn_conv, d_ffn_conv_b = _ffn_unpair(d_cw_p, 1), _ffn_unpair(d_cb_p, 1)
    dup_2 = dup.reshape(ntok, 2 * D_FF)
    dh_f = _mm(dup_2, wup_f, "nn", MXU_DT, "mm_dhf", tn=1024).reshape(bsz, t_total, D)
    g_wup_t = _mm(dup_2, h_f.reshape(ntok, D), "tn", MXU_DT, "mm_gwup", tm=1408, tn=1024)
    ffn_parts = [_ffn_unpair(g_wup_t, 0).reshape(N_DEV, -1, D), g_wdn.reshape(N_DEV, -1, D)]
    rs_send, rs_recv, rs_src, rs_land, rs_token, _ = _exchange_start(
        "scatter_start", ffn_parts, [p.shape[1:] for p in ffn_parts], [dh_f], by_owner=True)
    dr1, dsc_f, dsh_f, d_ln1_g, d_ln1_b, dy, dgt_a = _ln_bwd_call(
        "ln1_bwd", dr2, dh_f, r1, ln1_g, ln1_b, sc_f + rs_token[0, 0], y=y, gt=gt_a)

    dy_2 = dy.reshape(ntok, D)
    do = _mm(dy_2, wo_f, "nt", MXU_DT, "mm_do", tm=1024, tn=1024).reshape(bsz, t_total, D)
    g_wo = _mm(o_mix.reshape(ntok, D), dy_2, "tn", MXU_DT, "mm_gwo", tm=512, tn=1024)
    dq, dk, dv, dgates, dgq, dgk, dgv, dsm_gla, d_w2, d_bg, dz, dgg, d_dn_norm, d_gla_norm = _rec_bwd(
        q, k, v, gates, s_dn, inv_dn, proj, w2, bg, s_gla, do, o_dn, o_gla, grow_dn, grow_gla)
    dqkv, dsm, d_dn_conv, d_alog_row, d_dt_row = _dn_pre_bwd(
        proj, dq, dk, dv, dgates, dsm_gla, dn_conv_f, alog_row, dt_row)
    dproj = jnp.concatenate([dqkv, dz, dgq, dgk, dgv, dgg, dsm], axis=-1).reshape(ntok, P_W)
    g_win_p = _mm(dproj, h_a.reshape(ntok, D), "tn", MXU_DT, "mm_gwin", tm=1408, tn=1024)
    mix_parts = [_unpad_in_rows(g_win_p), g_wo.reshape(N_DEV, -1, D)]
    rs2_send, rs2_recv, rs2_src, rs2_land, rs2_token, (win_p_late,) = _exchange_start(
        "scatter_mix_start", mix_parts, [p.shape[1:] for p in mix_parts], [], by_owner=True, carry=[win_p])
    dh_a = _mm(dproj, win_p_late, "nn", MXU_DT, "mm_dha", tn=1024).reshape(bsz, t_total, D)
    grad_x, dsc_a, dsh_a, d_ln0_g, d_ln0_b = _ln_bwd_call(
        "ln0_bwd", dr1, dh_a, x, g0, b0, sc_a + rs2_token[0, 0])

    delta, new_m, new_v, big_grads = {}, {}, {}, {}
    flip = lambda a: jnp.swapaxes(a, 1, 2)

    def update_owned(n, landed, mine):
        parts = lax.dynamic_update_slice(landed, lax.dynamic_slice_in_dim(mine, me, 1, axis=0), (me, 0, 0))
        turn = flip if parts.shape[1:] != w_given[n].shape[1:] else (lambda a: a)
        g, d_, m_, v_ = _sum_adamw(parts, turn(w_given[n]), turn(m_given[n]), turn(v_given[n]), "adamw_" + n)
        big_grads[n], delta[n], new_m[n], new_v[n] = turn(g[None]), turn(d_), turn(m_), turn(v_)

    ffn_parts, ffn_landed = _exchange_wait("scatter_wait", rs_send, rs_recv, rs_src, rs_land, grad_x, by_owner=True)
    update_owned("ffn_w_up", ffn_landed[0], ffn_parts[0])
    update_owned("ffn_w_down", ffn_landed[1], ffn_parts[1])
    ffn_done = 0.0 * (new_v["ffn_w_up"][0, 0, 0] + new_v["ffn_w_down"][0, 0, 0])

    dmod = jnp.concatenate([dsh_a, dsc_a, dgt_a, dsh_f, dsc_f, dgt_f], axis=1).reshape(-1)
    small_parts = {
        "ln0_g": d_ln0_g, "ln0_b": d_ln0_b, "ln1_g": d_ln1_g, "ln1_b": d_ln1_b, "ln2_g": d_ln2_g, "ln2_b": d_ln2_b,
        "dn_a_log": d_alog_row[:, :HEADS], "dn_dt_bias": d_dt_row[:, :HEADS],
        "dn_norm_g": d_dn_norm, "gla_norm_g": d_gla_norm, "gla_b_gate": _unpad_heads(d_bg, 1),
        "ffn_conv_b": d_ffn_conv_b, "dn_conv": d_dn_conv,
        "gla_w_gate2": _unpad_heads(d_w2[SM_R:SM_R + GATE_RANK], 1), "ffn_conv": d_ffn_conv}
    order = sorted(small_parts)
    flat = jnp.concatenate([small_parts[n].reshape(-1) for n in order] + [(loss_part + ffn_done).reshape(1), dmod])
    n3 = flat.size
    rows3 = -(-n3 // 1024) * 8
    pack3 = jnp.pad(flat, (0, rows3 * 128 - n3)).reshape(rows3, 128)
    got3 = _gather_small(pack3, "gather_small_grads")
    tot3 = _sum_slots(got3, "sum_small_grads").reshape(-1)
    grads = {}
    off = 0
    for n in order:
        size = small_parts[n].size
        grads[n] = tot3[off:off + size]
        off += size
    loss = tot3[off]
    off += 1
    dmod_all = got3.reshape(N_DEV, -1)[:, off:off + dmod.size].reshape(N_DEV * bsz, 6 * D)
    dmod_cols = lax.dynamic_slice_in_dim(dmod_all, me * ncol, ncol, axis=1)
    g_wada, g_bada = _ada_bwd(c_all, dmod_all, dmod_cols)
    grads["b_ada"] = g_bada

    def col_shard(full, rows):
        part = full.reshape(rows, -1)
        width = part.shape[1] // N_DEV
        return lax.dynamic_slice_in_dim(part, me * width, width, axis=1)

    grads["dn_conv"] = col_shard(grads["dn_conv"], DN_CONV_K)
    grads["gla_w_gate2"] = col_shard(grads["gla_w_gate2"], GATE_RANK)
    grads["ffn_conv"] = col_shard(grads["ffn_conv"], FFN_CONV_K)
    grads = {n: g.reshape(w_given[n].shape) for n, g in grads.items()}
    mix_parts, mix_landed = _exchange_wait("scatter_mix_wait", rs2_send, rs2_recv, rs2_src, rs2_land, grad_x,
                                           by_owner=True)
    update_owned("w_in", mix_landed[0], mix_parts[0])
    update_owned("w_o", mix_landed[1], mix_parts[1])
    grads["w_ada"] = g_wada.reshape(w_ada.shape)
    delta["w_ada"], new_m["w_ada"], new_v["w_ada"] = _adamw(w_ada, grads["w_ada"], m_w_ada, v_w_ada, "adamw_w_ada")
    grads.update(big_grads)
    d_s, m_s, v_s = _adamw_many(*[[src[n] for n in SMALL_NAMES] for src in (w_given, grads, m_given, v_given)],
                                "adamw_small")
    for i, n in enumerate(SMALL_NAMES):
        delta[n], new_m[n], new_v[n] = d_s[i], m_s[i], v_s[i]

    return (loss, grad_x, *[grads[n] for n in WEIGHTS], *[delta[n] for n in WEIGHTS],
            *[new_m[n] for n in WEIGHTS], *[new_v[n] for n in WEIGHTS])
```

```python
import jax
import jax.numpy as jnp
from jax import lax
from jax.experimental import pallas as pl
from jax.experimental.pallas import tpu as pltpu

F32 = jnp.float32
MXU_DT = jnp.bfloat16
MESH = pl.DeviceIdType.MESH
N_DEV = 8

D = 1024
HEADS = 4
HD = 128
CHUNK = 64
GLA_KEY = 64
GLA_TAU = 16.0
GATE_RANK = 16
D_FF = 2816
IN_W = 3608
ALPHA = 2.0 ** 0.25
EPS = 1e-6
DN_CONV_K = 4
FFN_CONV_K = 3
HALO = 8
ROW_TILE = 1024
FFN_ROW_TILE = 1024

P_QKV, P_Z, P_GQ, P_GK, P_GV, P_GG, P_SM, P_W = 0, 1536, 2048, 2560, 3072, 3584, 4096, 4224
SM_A, SM_B, SM_R = 0, 4, 8

ADAM_LR, ADAM_B1, ADAM_B2, ADAM_EPS, ADAM_WD, ADAM_STEP = 0.001, 0.9, 0.999, 1e-08, 0.01, 10

VMEM_LIMIT_V7X = 56 * 1024 * 1024


def _params(sem=None):
    return pltpu.CompilerParams(dimension_semantics=sem, vmem_limit_bytes=VMEM_LIMIT_V7X)


NN, NT, TN = ((1,), (0,)), ((1,), (1,)), ((0,), (0,))


def _dg(a, b, dims):
    return lax.dot_general(a, b, (dims, ((), ())), preferred_element_type=F32)


def _mxu(a, b, dims):
    return _dg(a.astype(MXU_DT), b.astype(MXU_DT), dims)


@jax.custom_vjp
def _dot(a, b):
    return _mxu(a, b, NN)


_dot.defvjp(lambda a, b: (_mxu(a, b, NN), (a, b)),
            lambda res, g: (_mxu(g, res[1], NT).astype(res[0].dtype), _mxu(res[0], g, TN).astype(res[1].dtype)))


@jax.custom_vjp
def _dot_nt(a, b):
    return _mxu(a, b, NT)


_dot_nt.defvjp(lambda a, b: (_mxu(a, b, NT), (a, b)),
               lambda res, g: (_mxu(g, res[1], NN).astype(res[0].dtype), _mxu(g, res[0], TN).astype(res[1].dtype)))


@jax.custom_vjp
def _dot_tn(a, b):
    return _mxu(a, b, TN)


_dot_tn.defvjp(lambda a, b: (_mxu(a, b, TN), (a, b)),
               lambda res, g: (_mxu(res[1], g, NT).astype(res[0].dtype), _mxu(res[0], g, NN).astype(res[1].dtype)))


def _iota(shape, dim):
    return lax.broadcasted_iota(jnp.int32, shape, dim)


def _sigmoid(x):
    return jax.nn.sigmoid(x)


def _silu(x):
    return x * _sigmoid(x)


def _softplus(x):
    return jnp.maximum(x, 0.0) + jnp.log(1.0 + jnp.exp(-jnp.abs(x)))


def _ln_stats(x):
    mu = jnp.mean(x, axis=-1, keepdims=True)
    xc = x - mu
    rstd = lax.rsqrt(jnp.mean(xc * xc, axis=-1, keepdims=True) + EPS)
    return xc * rstd, rstd


def _ln_bwd(dxhat, xhat, rstd):
    return rstd * (dxhat - jnp.mean(dxhat, axis=-1, keepdims=True)
                   - xhat * jnp.mean(dxhat * xhat, axis=-1, keepdims=True))


def _split2(a):
    hi = a.astype(jnp.bfloat16)
    return hi, (a - hi.astype(F32)).astype(jnp.bfloat16)


def _d3(a, b, dims):
    ah, al = _split2(a)
    bh, bl = _split2(b)
    return _dg(ah, bh, dims) + (_dg(ah, bl, dims) + _dg(al, bh, dims))


@jax.custom_vjp
def _dot3(a, b):
    return _d3(a, b, NN)


_dot3.defvjp(lambda a, b: (_d3(a, b, NN), (a, b)),
             lambda res, g: (_d3(g, res[1], NT), _d3(res[0], g, TN)))


def _split3(b):
    b1 = b.astype(jnp.bfloat16)
    r1 = b - b1.astype(F32)
    b2 = r1.astype(jnp.bfloat16)
    return b1, b2, (r1 - b2.astype(F32)).astype(jnp.bfloat16)


def _sum3(fn, b):
    b1, b2, b3 = _split3(b)
    return fn(b1) + (fn(b2) + fn(b3))


@jax.custom_vjp
def _mask_dot(e, b):
    return _sum3(lambda t: _dg(e, t, NN), b)


_mask_dot.defvjp(lambda e, b: (_mask_dot(e, b), e),
                 lambda e, g: (jnp.zeros_like(e), _sum3(lambda t: _dg(e, t, TN), g)))


@jax.custom_vjp
def _mask_dot_nt(e, b):
    return _sum3(lambda t: _dg(e, t, NT), b)


_mask_dot_nt.defvjp(lambda e, b: (_mask_dot_nt(e, b), e),
                    lambda e, g: (jnp.zeros_like(e), _sum3(lambda t: _dg(t, e, TN), g)))


def _interleave(gens, shares):
    results = [None] * len(gens)
    live = list(range(len(gens)))
    while live:
        for i in list(live):
            for _ in range(shares[i]):
                try:
                    next(gens[i])
                except StopIteration as done:
                    results[i] = done.value
                    live.remove(i)
                    break
    return results


def _tri_inv_stages(ms):
    n = ms[0].shape[0]
    r, c = _iota((n, n), 0), _iota((n, n), 1)
    eye = (r == c).astype(F32)
    diag = (r >> 3) == (c >> 3)
    ds = [jnp.where(diag, m, 0.0) for m in ms]
    d2s = [_d3(d, d, NN) for d in ds]
    yield
    d4s = [_d3(d2, d2, NN) for d2 in d2s]
    invs = [_d3(eye - d, eye + d2, NN) for d, d2 in zip(ds, d2s)]
    yield
    invs = [_d3(inv, eye + d4, NN) for inv, d4 in zip(invs, d4s)]
    yield
    shift = 3
    while (1 << shift) < n:
        rb, cb = r >> shift, c >> shift
        sel = ((rb & 1) == 1) & (cb == rb - 1)
        tmp = [_d3(inv, jnp.where(sel, m, 0.0), NN) for inv, m in zip(invs, ms)]
        yield
        invs = [inv - _d3(t, inv, NN) for t, inv in zip(tmp, invs)]
        yield
        shift += 1
    return invs


def _tri_inv_bwd(invs, das):
    tmp = [_d3(a, da, TN) for a, da in zip(invs, das)]
    return ([-_d3(t, a, NT) for t, a in zip(tmp, invs)],)


@jax.custom_vjp
def _tri_inv_known(ms, invs):
    return invs


_tri_inv_known.defvjp(lambda ms, invs: (invs, invs),
                      lambda invs, das: (_tri_inv_bwd(invs, das)[0], [jnp.zeros_like(a) for a in invs]))


def _dn_chunk(s_list, q, k, v, gates, inv_known=None):
    nb = len(q)
    c = q[0].shape[0]
    r64, c64 = _iota((c, c), 0), _iota((c, c), 1)
    causal = r64 >= c64
    strict = r64 > c64
    tri = causal.astype(jnp.bfloat16)
    eye = (_iota((HD, HD), 0) == _iota((HD, HD), 1)).astype(jnp.bfloat16)
    lane = _iota(gates[0].shape, 1)
    lane1 = _iota((1, HD), 1)
    g_all = [_mask_dot(tri, g) for g in gates]
    yield
    g_all_t = [_mask_dot_nt(eye, g) for g in g_all]
    yield
    row = _iota(g_all_t[0].shape, 0)
    last = [jnp.sum(g, axis=0, keepdims=True) for g in gates]
    prob = [(b, h) for b in range(nb) for h in range(HEADS)]
    sl = [slice(h * HD, (h + 1) * HD) for h in range(HEADS)]
    qh = [q[b][:, sl[h]] for b, h in prob]
    kh = [k[b][:, sl[h]] for b, h in prob]
    vh = [v[b][:, sl[h]] for b, h in prob]
    s = [s_list[b][h] for b, h in prob]
    beta = [jnp.sum(jnp.where(lane == SM_B + h, gates[b], 0.0), axis=-1, keepdims=True) for b, h in prob]
    g_c = [jnp.sum(jnp.where(lane == SM_A + h, g_all[b], 0.0), axis=-1, keepdims=True) for b, h in prob]
    g_r = [jnp.sum(jnp.where(row == SM_A + h, g_all_t[b], 0.0), axis=0, keepdims=True) for b, h in prob]
    g_last = [jnp.sum(jnp.where(lane1 == SM_A + h, last[b], 0.0), axis=-1, keepdims=True) for b, h in prob]
    decay = [jnp.where(causal, jnp.exp(jnp.where(causal, gc - gr, 0.0)), 0.0) for gc, gr in zip(g_c, g_r)]
    kb = [k_ * b_ for k_, b_ in zip(kh, beta)]
    m_low = [jnp.where(strict, _dot_nt(kb_, k_) * d_, 0.0) for kb_, k_, d_ in zip(kb, kh, decay)]
    yield
    attn = [_dot_nt(q_, k_) * d_ for q_, k_, d_ in zip(qh, kh, decay)]
    yield
    if inv_known is None:
        a_inv = yield from _tri_inv_stages(m_low)
    else:
        a_inv = _tri_inv_known(m_low, inv_known)
    eg = [jnp.exp(gc) for gc in g_c]
    uw = [_dot3(a_, jnp.concatenate([v_ * b_, kb_ * e_], axis=1))
          for a_, v_, b_, kb_, e_ in zip(a_inv, vh, beta, kb, eg)]
    yield
    v_new = [uw_[:, :HD] - _dot(uw_[:, HD:], s_) for uw_, s_ in zip(uw, s)]
    yield
    qs = [_dot(q_ * e_, s_) for q_, e_, s_ in zip(qh, eg, s)]
    yield
    o = [qs_ + _dot(a_, vn_) for qs_, a_, vn_ in zip(qs, attn, v_new)]
    yield
    k_dec = [k_ * jnp.exp(gl - gc) for k_, gl, gc in zip(kh, g_last, g_c)]
    s_new = [s_ * jnp.exp(gl) + _dot_tn(kd_, vn_) for s_, gl, kd_, vn_ in zip(s, g_last, k_dec, v_new)]
    outs = [jnp.concatenate(o[b * HEADS:(b + 1) * HEADS], axis=-1) for b in range(nb)]
    states = [s_new[b * HEADS:(b + 1) * HEADS] for b in range(nb)]
    return outs, states, a_inv


def _gla_chunk(st_list, q, k, v, small, w2, bg):
    nb = len(q)
    c = q[0].shape[0]
    causal = _iota((c, c), 0) >= _iota((c, c), 1)
    tri = causal.astype(jnp.bfloat16)
    la_all = [-_softplus(-(_dot(sm, w2) + bg)) * (1.0 / GLA_TAU) for sm in small]
    yield
    b_all = [_mask_dot(tri, la) for la in la_all]
    yield
    prob = [(b, h) for b in range(nb) for h in range(HEADS)]
    sl = [slice(h * HD, (h + 1) * HD) for h in range(HEADS)]
    kh = [k[b][:, sl[h]] for b, h in prob]
    vh = [v[b][:, sl[h]] for b, h in prob]
    st = [st_list[b][h] for b, h in prob]
    bc = [b_all[b][:, sl[h]] for b, h in prob]
    b_last = [jnp.sum(la_all[b][:, sl[h]], axis=0, keepdims=True) for b, h in prob]
    q_dec = [q[b][:, sl[h]] * (GLA_KEY ** -0.5) * jnp.exp(bc_) for (b, h), bc_ in zip(prob, bc)]
    attn = [jnp.where(causal, _dot_nt(qd, k_ * jnp.exp(-bc_)), 0.0) for qd, k_, bc_ in zip(q_dec, kh, bc)]
    yield
    inter = [_dot_nt(qd, st_) for qd, st_ in zip(q_dec, st)]
    yield
    o = [i_ + _dot(a_, v_) for i_, a_, v_ in zip(inter, attn, vh)]
    yield
    k_dec = [k_ * jnp.exp(bl - bc_) for k_, bl, bc_ in zip(kh, b_last, bc)]
    s_new = [st_ * jnp.exp(bl) + _dot_tn(v_, kd) for st_, bl, v_, kd in zip(st, b_last, vh, k_dec)]
    outs = [jnp.concatenate(o[b * HEADS:(b + 1) * HEADS], axis=-1) for b in range(nb)]
    return outs, [s_new[b * HEADS:(b + 1) * HEADS] for b in range(nb)]


def _dn_qkv(y):
    act = _silu(y)
    parts = []
    for i in range(2 * HEADS):
        xh = act[:, i * HD:(i + 1) * HD]
        xh = xh * lax.rsqrt(jnp.sum(xh * xh, axis=-1, keepdims=True) + EPS)
        parts.append(xh * (HD ** -0.5) if i < HEADS else xh)
    qk = jnp.concatenate(parts, axis=-1)
    return qk[:, :HEADS * HD], qk[:, HEADS * HD:], act[:, 2 * HEADS * HD:]


def _dn_gates(small, alog_row, dt_row):
    lane = _iota(small.shape, 1)
    log_a = -jnp.exp(alog_row) * _softplus(small + dt_row)
    return jnp.where(lane < SM_B, log_a, jnp.where(lane < SM_R, _sigmoid(small), 0.0))


def _gate_norm(o, z, grow):
    parts = []
    for h in range(HEADS):
        oh = o[:, h * HD:(h + 1) * HD]
        parts.append(oh * lax.rsqrt(jnp.mean(oh * oh, axis=-1, keepdims=True) + EPS))
    return jnp.concatenate(parts, axis=-1) * grow * _silu(z)


def _conv_rows(xrows, w_ref, k_taps):
    n = xrows.shape[0]
    acc = xrows * w_ref[k_taps - 1:k_taps, :]
    for s in range(1, k_taps):
        acc = acc + pltpu.roll(xrows, s, 0) * w_ref[k_taps - 1 - s:k_taps - s, :]
    return acc


def _shift_up(x, s):
    return x if s == 0 else pltpu.roll(x, x.shape[0] - s, 0)


def _div_tile(n, cap, mult=8):
    best = None
    for t in range(mult, min(n, cap) + 1, mult):
        if n % t == 0:
            best = t
    return best if best is not None else n


def _halo_prev(tt):
    return lambda b, t: (b, jnp.maximum(t * (tt // HALO) - 1, 0))


def _halo_next(tt, t_total):
    return lambda b, t: (b, jnp.minimum((t + 1) * (tt // HALO), t_total // HALO - 1))


def _mm(a, b, mode, out_dtype, name, tm=512, tn=512, tk=None):
    if mode == "nn":
        (m, k), n = a.shape, b.shape[1]
    elif mode == "nt":
        (m, k), n = a.shape, b.shape[0]
    else:
        (k, m), n = a.shape, b.shape[1]
    tm, tn = min(tm, m), min(tn, n)
    tk = k if tk is None else min(tk, k)
    assert m % tm == 0 and n % tn == 0 and k % tk == 0, (name, a.shape, b.shape, tm, tn, tk)
    nk = k // tk
    if mode == "tn":
        a_spec = pl.BlockSpec((tk, tm), lambda i, j, kk: (kk, i))
    else:
        a_spec = pl.BlockSpec((tm, tk), lambda i, j, kk: (i, kk))
    if mode == "nt":
        b_spec = pl.BlockSpec((tn, tk), lambda i, j, kk: (j, kk))
    else:
        b_spec = pl.BlockSpec((tk, tn), lambda i, j, kk: (kk, j))
    dims = {"nn": NN, "nt": NT, "tn": TN}[mode]

    def body(a_ref, b_ref, o_ref, *acc):
        p = _dg(a_ref[...], b_ref[...], dims)
        if nk == 1:
            o_ref[...] = p.astype(out_dtype)
        else:
            kk = pl.program_id(2)

            @pl.when(kk == 0)
            def _():
                acc[0][...] = p

            @pl.when(kk > 0)
            def _():
                acc[0][...] += p

            @pl.when(kk == nk - 1)
            def _():
                o_ref[...] = acc[0][...].astype(out_dtype)

    return pl.pallas_call(
        body, name=name, grid=(m // tm, n // tn, nk),
        in_specs=[a_spec, b_spec],
        out_specs=pl.BlockSpec((tm, tn), lambda i, j, kk: (i, j)),
        out_shape=jax.ShapeDtypeStruct((m, n), out_dtype),
        scratch_shapes=[pltpu.VMEM((tm, tn), F32)] if nk > 1 else [],
        compiler_params=_params(("parallel", "parallel", "arbitrary")),
    )(a, b)


def _ada_fwd(c_all, w_ada, b_cols):
    def body(c_ref, w_ref, b_ref, o_ref):
        cond = _silu(c_ref[...]).astype(MXU_DT)
        o_ref[...] = _dot(cond, w_ref[...].astype(MXU_DT)) + b_ref[...]

    return pl.pallas_call(body, name="ada_fwd", out_shape=jax.ShapeDtypeStruct((c_all.shape[0], w_ada.shape[1]), F32),
                          compiler_params=_params())(c_all, w_ada, b_cols)


def _ada_bwd(c_all, dmod_all, dmod_cols):
    def body(c_ref, da_ref, dc_ref, gw_ref, gb_ref):
        cond = _silu(c_ref[...]).astype(MXU_DT)
        gw_ref[...] = _dot_tn(cond, dc_ref[...].astype(MXU_DT))
        gb_ref[...] = jnp.sum(da_ref[...], axis=0, keepdims=True)

    return pl.pallas_call(
        body, name="ada_bwd",
        out_shape=(jax.ShapeDtypeStruct((c_all.shape[1], dmod_cols.shape[1]), F32),
                   jax.ShapeDtypeStruct((1, dmod_all.shape[1]), F32)),
        compiler_params=_params())(c_all, dmod_all, dmod_cols)


def _tok_spec(tt, width=D):
    return pl.BlockSpec((1, tt, width), lambda b, t: (b, t, 0))


def _vec_spec(width=D):
    return pl.BlockSpec((1, width), lambda b, t: (0, 0))


def _bvec_spec(width=D):
    return pl.BlockSpec((1, 1, width), lambda b, t: (b, 0, 0))


def _ln0_mod(x, g0, b0, sc, sh):
    bsz, t_total, _ = x.shape
    tt = _div_tile(t_total, ROW_TILE)

    def body(x_ref, g_ref, b_ref, sc_ref, sh_ref, h_ref):
        xh, _ = _ln_stats(x_ref[0])
        x0 = xh * g_ref[...] + b_ref[...]
        h_ref[0] = (x0 * (1.0 + sc_ref[0]) + sh_ref[0]).astype(MXU_DT)

    return pl.pallas_call(
        body, name="ln0_mod", grid=(bsz, t_total // tt),
        in_specs=[_tok_spec(tt), _vec_spec(), _vec_spec(), _bvec_spec(), _bvec_spec()],
        out_specs=_tok_spec(tt), out_shape=jax.ShapeDtypeStruct(x.shape, MXU_DT),
        compiler_params=_params(("parallel", "parallel")))(x, g0, b0, sc, sh)


def _res_ln_mod(x, y, gt, g0, b0, g1, b1, sc, sh):
    bsz, t_total, _ = x.shape
    tt = _div_tile(t_total, ROW_TILE)

    def body(x_ref, y_ref, gt_ref, g0_ref, b0_ref, g1_ref, b1_ref, sc_ref, sh_ref, r_ref, h_ref):
        xh, _ = _ln_stats(x_ref[0])
        r = ALPHA * (xh * g0_ref[...] + b0_ref[...]) + (1.0 + gt_ref[0]) * y_ref[0].astype(F32)
        r_ref[0] = r
        rh, _ = _ln_stats(r)
        x1 = rh * g1_ref[...] + b1_ref[...]
        h_ref[0] = (x1 * (1.0 + sc_ref[0]) + sh_ref[0]).astype(MXU_DT)

    return pl.pallas_call(
        body, name="res_ln_mod", grid=(bsz, t_total // tt),
        in_specs=[_tok_spec(tt), _tok_spec(tt), _bvec_spec(), _vec_spec(), _vec_spec(), _vec_spec(), _vec_spec(),
                  _bvec_spec(), _bvec_spec()],
        out_specs=(_tok_spec(tt), _tok_spec(tt)),
        out_shape=(jax.ShapeDtypeStruct(x.shape, F32), jax.ShapeDtypeStruct(x.shape, MXU_DT)),
        compiler_params=_params(("parallel", "parallel")))(x, y, gt, g0, b0, g1, b1, sc, sh)


def _final_fwd_bwd(r1, y2, gt, g1, b1, g2, b2, target):
    bsz, t_total, _ = r1.shape
    tt = _div_tile(t_total, ROW_TILE)

    def body(r1_ref, y2_ref, gt_ref, g1_ref, b1_ref, g2_ref, b2_ref, tg_ref,
             loss_ref, dr2_ref, dy2_ref, dgt_ref, dg2_ref, db2_ref):
        b, t = pl.program_id(0), pl.program_id(1)

        @pl.when((b == 0) & (t == 0))
        def _():
            loss_ref[...] = jnp.zeros_like(loss_ref)
            dg2_ref[...] = jnp.zeros_like(dg2_ref)
            db2_ref[...] = jnp.zeros_like(db2_ref)

        @pl.when(t == 0)
        def _():
            dgt_ref[...] = jnp.zeros_like(dgt_ref)

        rh1, _ = _ln_stats(r1_ref[0])
        x1 = rh1 * g1_ref[...] + b1_ref[...]
        y2 = y2_ref[0].astype(F32)
        gate = 1.0 + gt_ref[0]
        xh2, rstd2 = _ln_stats(ALPHA * x1 + gate * y2)
        err = xh2 * g2_ref[...] + b2_ref[...] - tg_ref[0]
        loss_ref[...] += jnp.sum(err * err, axis=0, keepdims=True)
        dx2 = err * (1.0 / D)
        dg2_ref[...] += jnp.sum(dx2 * xh2, axis=0, keepdims=True)
        db2_ref[...] += jnp.sum(dx2, axis=0, keepdims=True)
        dr2 = _ln_bwd(dx2 * g2_ref[...], xh2, rstd2)
        dr2_ref[0] = dr2
        dy2_ref[0] = (gate * dr2).astype(MXU_DT)
        dgt_ref[0] += jnp.sum(dr2 * y2, axis=0, keepdims=True)

    vec_out = jax.ShapeDtypeStruct((1, D), F32)
    return pl.pallas_call(
        body, name="final_fwd_bwd", grid=(bsz, t_total // tt),
        in_specs=[_tok_spec(tt), _tok_spec(tt), _bvec_spec(), _vec_spec(), _vec_spec(), _vec_spec(), _vec_spec(),
                  _tok_spec(tt)],
        out_specs=(_vec_spec(), _tok_spec(tt), _tok_spec(tt), _bvec_spec(), _vec_spec(), _vec_spec()),
        out_shape=(vec_out, jax.ShapeDtypeStruct(r1.shape, F32), jax.ShapeDtypeStruct(r1.shape, MXU_DT),
                   jax.ShapeDtypeStruct((bsz, 1, D), F32), vec_out, vec_out),
        compiler_params=_params(("arbitrary", "arbitrary")))(r1, y2, gt, g1, b1, g2, b2, target)


def _ln_bwd_call(name, d_res, d_h, src, g, b, sc, y=None, gt=None):
    bsz, t_total, _ = src.shape
    tt = _div_tile(t_total, ROW_TILE)
    has_y = y is not None

    def body(*refs):
        if has_y:
            (dres_ref, dh_ref, src_ref, g_ref, b_ref, sc_ref, y_ref, gt_ref,
             dsrc_ref, dsc_ref, dsh_ref, dg_ref, db_ref, dy_ref, dgt_ref) = refs
        else:
            (dres_ref, dh_ref, src_ref, g_ref, b_ref, sc_ref,
             dsrc_ref, dsc_ref, dsh_ref, dg_ref, db_ref) = refs
        bi, t = pl.program_id(0), pl.program_id(1)

        @pl.when((bi == 0) & (t == 0))
        def _():
            dg_ref[...] = jnp.zeros_like(dg_ref)
            db_ref[...] = jnp.zeros_like(db_ref)

        @pl.when(t == 0)
        def _():
            dsc_ref[...] = jnp.zeros_like(dsc_ref)
            dsh_ref[...] = jnp.zeros_like(dsh_ref)
            if has_y:
                dgt_ref[...] = jnp.zeros_like(dgt_ref)

        xh, rstd = _ln_stats(src_ref[0])
        xv = xh * g_ref[...] + b_ref[...]
        dh = dh_ref[0].astype(F32)
        dx = ALPHA * dres_ref[0] + dh * (1.0 + sc_ref[0])
        dsc_ref[0] += jnp.sum(dh * xv, axis=0, keepdims=True)
        dsh_ref[0] += jnp.sum(dh, axis=0, keepdims=True)
        dg_ref[...] += jnp.sum(dx * xh, axis=0, keepdims=True)
        db_ref[...] += jnp.sum(dx, axis=0, keepdims=True)
        dsrc = _ln_bwd(dx * g_ref[...], xh, rstd)
        dsrc_ref[0] = dsrc
        if has_y:
            dy_ref[0] = ((1.0 + gt_ref[0]) * dsrc).astype(MXU_DT)
            dgt_ref[0] += jnp.sum(dsrc * y_ref[0].astype(F32), axis=0, keepdims=True)

    vec_out = jax.ShapeDtypeStruct((1, D), F32)
    bvec_out = jax.ShapeDtypeStruct((bsz, 1, D), F32)
    in_specs = [_tok_spec(tt), _tok_spec(tt), _tok_spec(tt), _vec_spec(), _vec_spec(), _bvec_spec()]
    out_specs = [_tok_spec(tt), _bvec_spec(), _bvec_spec(), _vec_spec(), _vec_spec()]
    out_shape = [jax.ShapeDtypeStruct(src.shape, F32), bvec_out, bvec_out, vec_out, vec_out]
    args = [d_res, d_h, src, g, b, sc]
    if has_y:
        in_specs += [_tok_spec(tt), _bvec_spec()]
        out_specs += [_tok_spec(tt), _bvec_spec()]
        out_shape += [jax.ShapeDtypeStruct(src.shape, MXU_DT), bvec_out]
        args += [y, gt]
    return pl.pallas_call(body, name=name, grid=(bsz, t_total // tt), in_specs=in_specs, out_specs=tuple(out_specs),
                          out_shape=tuple(out_shape), compiler_params=_params(("arbitrary", "arbitrary")))(*args)


FFN_TC = 256
FFN_NJ = D_FF // FFN_TC
FFN_PW = 2 * FFN_TC


def _ffn_pair(a, axis):
    shp = list(a.shape)
    a4 = a.reshape(shp[:axis] + [2, FFN_NJ, FFN_TC] + shp[axis + 1:])
    return jnp.swapaxes(a4, axis, axis + 1).reshape(shp)


def _ffn_unpair(a, axis):
    shp = list(a.shape)
    a4 = a.reshape(shp[:axis] + [FFN_NJ, 2, FFN_TC] + shp[axis + 1:])
    return jnp.swapaxes(a4, axis, axis + 1).reshape(shp)


def _ffn_up_act(h, w_up, cw, cb):
    bsz, t_total, _ = h.shape
    tt = _div_tile(t_total, FFN_ROW_TILE)
    def body(h_ref, wu_ref, w_ref, b_ref, up_ref, o_ref, carry_ref):
        up_t = _dot_nt(h_ref[0], wu_ref[...])
        up_ref[0] = up_t
        prev = jnp.where(pl.program_id(2) == 0, 0.0, carry_ref[...])
        rows = jnp.concatenate([prev, up_t], axis=0)
        u = _conv_rows(rows, w_ref, FFN_CONV_K)[HALO:] + b_ref[...]
        o_ref[0] = (_silu(u[:, :FFN_TC]) * u[:, FFN_TC:]).astype(MXU_DT)
        carry_ref[...] = up_t[tt - HALO:, :]

    return pl.pallas_call(
        body, name="ffn_up_act", grid=(bsz, FFN_NJ, t_total // tt),
        in_specs=[pl.BlockSpec((1, tt, D), lambda b, j, t: (b, t, 0)),
                  pl.BlockSpec((FFN_PW, D), lambda b, j, t: (j, 0)),
                  pl.BlockSpec((FFN_CONV_K, FFN_PW), lambda b, j, t: (0, j)),
                  pl.BlockSpec((1, FFN_PW), lambda b, j, t: (0, j))],
        out_specs=(pl.BlockSpec((1, tt, FFN_PW), lambda b, j, t: (b, t, j)),
                   pl.BlockSpec((1, tt, FFN_TC), lambda b, j, t: (b, t, j))),
        out_shape=(jax.ShapeDtypeStruct((bsz, t_total, 2 * D_FF), F32),
                   jax.ShapeDtypeStruct((bsz, t_total, D_FF), MXU_DT)),
        scratch_shapes=[pltpu.VMEM((HALO, FFN_PW), F32)],
        compiler_params=_params(("parallel", "parallel", "arbitrary")))(h, w_up, cw, cb)


HALO16 = 16


def _ffn_act_bwd(up, dy2, w_down, cw, cb):
    bsz, t_total, width = up.shape
    tt = _div_tile(t_total, FFN_ROW_TILE)
    nt = t_total // tt
    hp, hn = _halo_prev(tt), _halo_next(tt, t_total)

    def body(x_ref, xp_ref, xn_ref, dy_ref, dyn_ref, wd_ref, w_ref, b_ref, dup_ref, dw_ref, db_ref):
        b, t = pl.program_id(1), pl.program_id(2)

        @pl.when((b == 0) & (t == 0))
        def _():
            dw_ref[...] = jnp.zeros_like(dw_ref)
            db_ref[...] = jnp.zeros_like(db_ref)

        prev = jnp.where(t == 0, 0.0, xp_ref[0])
        rows = jnp.concatenate([prev, x_ref[0], xn_ref[0]], axis=0)
        u = _conv_rows(rows, w_ref, FFN_CONV_K)[HALO:] + b_ref[...]
        g_pre, v_pre = u[:, :FFN_TC], u[:, FFN_TC:]
        valid = (_iota((tt + HALO, 1), 0) < tt) | (t < nt - 1)
        da = jnp.concatenate([_dot_nt(dy_ref[0], wd_ref[...]), _dot_nt(dyn_ref[0], wd_ref[...])[:HALO]], axis=0)
        da_ext = jnp.where(valid, da, 0.0)
        sg = _sigmoid(g_pre)
        gs = g_pre * sg
        du = jnp.concatenate([da_ext * v_pre * (sg + gs * (1.0 - sg)), da_ext * gs], axis=1)
        dup = du * w_ref[FFN_CONV_K - 1:FFN_CONV_K, :]
        for s in range(1, FFN_CONV_K):
            dup = dup + _shift_up(du, s) * w_ref[FFN_CONV_K - 1 - s:FFN_CONV_K - s, :]
        dup_ref[0] = dup[:tt].astype(MXU_DT)
        du_t = du[:tt]
        db_ref[...] += jnp.sum(du_t, axis=0, keepdims=True)
        for k in range(FFN_CONV_K):
            s = FFN_CONV_K - 1 - k
            xs = (rows if s == 0 else pltpu.roll(rows, s, 0))[HALO:HALO + tt]
            dw_ref[k:k + 1, :] += jnp.sum(du_t * xs, axis=0, keepdims=True)

    def halo(h, w):
        return pl.BlockSpec((1, HALO, w), lambda j, b, t: (*h(b, t), j))

    wspec = lambda rows_: pl.BlockSpec((rows_, FFN_PW), lambda j, b, t: (0, j))
    tile = pl.BlockSpec((1, tt, FFN_PW), lambda j, b, t: (b, t, j))
    dy_next = lambda j, b, t: (b, jnp.minimum((t + 1) * (tt // HALO16), t_total // HALO16 - 1), 0)
    return pl.pallas_call(
        body, name="ffn_act_bwd", grid=(FFN_NJ, bsz, nt),
        in_specs=[tile, halo(hp, FFN_PW), halo(hn, FFN_PW),
                  pl.BlockSpec((1, tt, D), lambda j, b, t: (b, t, 0)), pl.BlockSpec((1, HALO16, D), dy_next),
                  pl.BlockSpec((FFN_TC, D), lambda j, b, t: (j, 0)), wspec(FFN_CONV_K), wspec(1)],
        out_specs=(tile, wspec(FFN_CONV_K), wspec(1)),
        out_shape=(jax.ShapeDtypeStruct(up.shape, MXU_DT), jax.ShapeDtypeStruct((FFN_CONV_K, width), F32),
                   jax.ShapeDtypeStruct((1, width), F32)),
        compiler_params=_params(("arbitrary", "arbitrary", "arbitrary")))(up, up, up, dy2, dy2, w_down, cw, cb)


QKV_W = 3 * HEADS * HD
SM_BLK = P_SM // 128


def _dn_pre_bwd(proj, dq, dk, dv, dgates, dsm_gla, conv_w, alog_row, dt_row):
    bsz, t_total, _ = proj.shape
    tt = _div_tile(t_total, 256)
    nt = t_total // tt
    hp, hn = _halo_prev(tt), _halo_next(tt, t_total)

    def body(x_ref, xp_ref, xn_ref, sm_ref, dq_ref, dqn_ref, dk_ref, dkn_ref, dv_ref, dvn_ref, dg_ref, dso_ref,
             w_ref, al_ref, dt_ref, dx_ref, dsm_ref, dw_ref, dal_ref, ddt_ref):
        b, t = pl.program_id(0), pl.program_id(1)

        @pl.when((b == 0) & (t == 0))
        def _():
            dw_ref[...] = jnp.zeros_like(dw_ref)
            dal_ref[...] = jnp.zeros_like(dal_ref)
            ddt_ref[...] = jnp.zeros_like(ddt_ref)

        prev = jnp.where(t == 0, 0.0, xp_ref[0])
        rows = jnp.concatenate([prev, x_ref[0], xn_ref[0]], axis=0)
        y = _conv_rows(rows, w_ref, DN_CONV_K)[HALO:]
        valid = (_iota((tt + HALO, 1), 0) < tt) | (t < nt - 1)

        def ext(tile_ref, next_ref):
            return jnp.where(valid, jnp.concatenate([tile_ref[0], next_ref[0]], axis=0), 0.0)

        _, vjp_qkv = jax.vjp(_dn_qkv, y)
        (dy,) = vjp_qkv((ext(dq_ref, dqn_ref), ext(dk_ref, dkn_ref), ext(dv_ref, dvn_ref)))
        dy = jnp.where(valid, dy, 0.0)
        dx = dy * w_ref[DN_CONV_K - 1:DN_CONV_K, :]
        for s in range(1, DN_CONV_K):
            dx = dx + _shift_up(dy, s) * w_ref[DN_CONV_K - 1 - s:DN_CONV_K - s, :]
        dx_ref[0] = dx[:tt].astype(MXU_DT)
        dy_t = dy[:tt]
        for k in range(DN_CONV_K):
            s = DN_CONV_K - 1 - k
            xs = (rows if s == 0 else pltpu.roll(rows, s, 0))[HALO:HALO + tt]
            dw_ref[k:k + 1, :] += jnp.sum(dy_t * xs, axis=0, keepdims=True)
        _, vjp_g = jax.vjp(_dn_gates, sm_ref[0], al_ref[...], dt_ref[...])
        dsm, dal, ddt = vjp_g(dg_ref[0])
        dsm_ref[0] = (dsm + dso_ref[0]).astype(MXU_DT)
        dal_ref[...] += dal
        ddt_ref[...] += ddt

    def tile(width, blk=0):
        return pl.BlockSpec((1, tt, width), lambda b, t: (b, t, blk))

    def halo(h, width):
        return pl.BlockSpec((1, HALO, width), lambda b, t: (*h(b, t), 0))

    return pl.pallas_call(
        body, name="dn_pre_bwd", grid=(bsz, nt),
        in_specs=[tile(QKV_W), halo(hp, QKV_W), halo(hn, QKV_W), tile(128, SM_BLK),
                  tile(512), halo(hn, 512), tile(512), halo(hn, 512), tile(512), halo(hn, 512), tile(128), tile(128),
                  pl.BlockSpec((DN_CONV_K, QKV_W), lambda b, t: (0, 0)), _vec_spec(128), _vec_spec(128)],
        out_specs=(tile(QKV_W), tile(128), pl.BlockSpec((DN_CONV_K, QKV_W), lambda b, t: (0, 0)),
                   _vec_spec(128), _vec_spec(128)),
        out_shape=(jax.ShapeDtypeStruct((bsz, t_total, QKV_W), MXU_DT),
                   jax.ShapeDtypeStruct((bsz, t_total, 128), MXU_DT),
                   jax.ShapeDtypeStruct((DN_CONV_K, QKV_W), F32), jax.ShapeDtypeStruct((1, 128), F32),
                   jax.ShapeDtypeStruct((1, 128), F32)),
        compiler_params=_params(("arbitrary", "arbitrary")))(
            proj, proj, proj, proj, dq, dq, dk, dk, dv, dv, dgates, dsm_gla, conv_w, alog_row, dt_row)


def _state_spec(bsz, idx):
    return pl.BlockSpec((bsz, 1, HEADS, HD, HD), lambda c: (0, idx(c), 0, 0, 0))


def _inv_spec(bsz, idx):
    return pl.BlockSpec((bsz, 1, HEADS, CHUNK, CHUNK), lambda c: (0, idx(c), 0, 0, 0))


def _chunk_spec(bsz, width, idx, blk=0):
    return pl.BlockSpec((bsz, CHUNK, width), lambda c: (0, idx(c), blk))


GQ_BLK, GK_BLK, GV_BLK = P_GQ // 512, P_GK // 512, P_GV // 512
REC_SHARES = (3, 1)


def _dn_prep(prev_rows, rows, small, w_ref, alog_row, dt_row):
    ys = [_conv_rows(jnp.concatenate([p, r], axis=0), w_ref, DN_CONV_K)[HALO:] for p, r in zip(prev_rows, rows)]
    yield
    qkv = [_dn_qkv(y) for y in ys]
    yield
    gates = [_dn_gates(s, alog_row, dt_row) for s in small]
    return [t[0] for t in qkv], [t[1] for t in qkv], [t[2] for t in qkv], gates


def _rec_fwd(proj, conv_w, alog_row, dt_row, w2, bg):
    bsz, t_total, _ = proj.shape
    nc = t_total // CHUNK
    fwd = lambda c: c
    nxt = lambda c: jnp.minimum(c + 1, nc - 1)

    def body(x0_ref, xn_ref, xnp_ref, smn_ref, cw_ref, al_ref, dt_ref,
             gq_ref, gk_ref, gv_ref, sm_ref, w2_ref, bg_ref,
             q_out, k_out, v_out, g_out, o_ref, ss_ref, inv_ref, go_ref, gss_ref,
             s_ref, gs_ref, nq_ref, nk_ref, nv_ref, ng_ref):
        seqs = range(bsz)
        heads = range(HEADS)
        per_seq = lambda ref: [ref[b] for b in seqs]

        def keep(prep):
            for b in seqs:
                nq_ref[b], nk_ref[b], nv_ref[b], ng_ref[b] = prep[0][b], prep[1][b], prep[2][b], prep[3][b]

        @pl.when(pl.program_id(0) == 0)
        def _():
            s_ref[...] = jnp.zeros_like(s_ref)
            gs_ref[...] = jnp.zeros_like(gs_ref)
            zeros = [jnp.zeros((HALO, QKV_W), F32) for _ in seqs]
            keep(_interleave([_dn_prep(zeros, per_seq(x0_ref), per_seq(sm_ref), cw_ref, al_ref[...], dt_ref[...])],
                             (1,))[0])

        q, k, v, gates = per_seq(nq_ref), per_seq(nk_ref), per_seq(nv_ref), per_seq(ng_ref)
        s_list = [[s_ref[b * HEADS + h] for h in heads] for b in seqs]
        gs_list = [[gs_ref[b * HEADS + h] for h in heads] for b in seqs]
        for b in seqs:
            q_out[b], k_out[b], v_out[b], g_out[b] = q[b], k[b], v[b], gates[b]
            for h in heads:
                ss_ref[b, 0, h] = s_list[b][h]
                gss_ref[b, 0, h] = gs_list[b][h]
        (o, new_s, invs), (go, new_gs), prep = _interleave(
            [_dn_chunk(s_list, q, k, v, gates),
             _gla_chunk(gs_list, per_seq(gq_ref), per_seq(gk_ref), per_seq(gv_ref), per_seq(sm_ref),
                        w2_ref[...], bg_ref[...]),
             _dn_prep(per_seq(xnp_ref), per_seq(xn_ref), per_seq(smn_ref), cw_ref, al_ref[...], dt_ref[...])],
            REC_SHARES + (1,))
        keep(prep)
        for b in seqs:
            o_ref[b] = o[b]
            go_ref[b] = go[b]
            for h in heads:
                s_ref[b * HEADS + h] = new_s[b][h]
                gs_ref[b * HEADS + h] = new_gs[b][h]
                inv_ref[b, 0, h] = invs[b * HEADS + h]

    tok = lambda width, blk=0, idx=fwd: _chunk_spec(bsz, width, idx, blk)
    state = jax.ShapeDtypeStruct((bsz, nc, HEADS, HD, HD), F32)
    out512 = jax.ShapeDtypeStruct((bsz, t_total, 512), F32)
    vec = lambda width: pl.BlockSpec((1, width), lambda c: (0, 0))
    return pl.pallas_call(
        body, name="rec_fwd", grid=(nc,),
        in_specs=[tok(QKV_W), tok(QKV_W, idx=nxt),
                  pl.BlockSpec((bsz, HALO, QKV_W), lambda c: (0, jnp.maximum(nxt(c) * (CHUNK // HALO) - 1, 0), 0)),
                  tok(128, SM_BLK, idx=nxt), pl.BlockSpec((DN_CONV_K, QKV_W), lambda c: (0, 0)), vec(128), vec(128),
                  tok(512, GQ_BLK), tok(512, GK_BLK), tok(512, GV_BLK), tok(128, SM_BLK),
                  pl.BlockSpec((128, 512), lambda c: (0, 0)), vec(512)],
        out_specs=(tok(512), tok(512), tok(512), tok(128),
                   tok(512), _state_spec(bsz, fwd), _inv_spec(bsz, fwd), tok(512), _state_spec(bsz, fwd)),
        out_shape=(out512, out512, out512, jax.ShapeDtypeStruct((bsz, t_total, 128), F32),
                   out512, state, jax.ShapeDtypeStruct((bsz, nc, HEADS, CHUNK, CHUNK), F32), out512, state),
        scratch_shapes=[pltpu.VMEM((bsz * HEADS, HD, HD), F32), pltpu.VMEM((bsz * HEADS, HD, HD), F32),
                        pltpu.VMEM((bsz, CHUNK, 512), F32), pltpu.VMEM((bsz, CHUNK, 512), F32),
                        pltpu.VMEM((bsz, CHUNK, 512), F32), pltpu.VMEM((bsz, CHUNK, 128), F32)],
        compiler_params=_params(("arbitrary",)))(
            proj, proj, proj, proj, conv_w, alog_row, dt_row, proj, proj, proj, proj, w2, bg)


def _rec_bwd(q, k, v, gates, s_dn, inv_dn, proj, w2, bg, s_gla, do, o_dn, o_gla, grow_dn, grow_gla):
    bsz, t_total, _ = q.shape
    nc = t_total // CHUNK
    rev = lambda c: nc - 1 - c
    nxt = lambda c: jnp.maximum(nc - 2 - c, 0)

    def body(q_ref, k_ref, v_ref, g_ref, ss_ref, inv_ref,
             gq_ref, gk_ref, gv_ref, sm_ref, w2_ref, bg_ref, gss_ref,
             do0_ref, od0_ref, og0_ref, z0_ref, gg0_ref, don_ref, odn_ref, ogn_ref, zn_ref, ggn_ref, gd_ref, gl_ref,
             dq_ref, dk_ref, dv_ref, dg_ref, dgq_ref, dgk_ref, dgv_ref, dsm_ref, dw2_ref, dbg_ref,
             dz_ref, dgg_ref, dgd_ref, dgl_ref,
             ds_ref, gds_ref, cdo_ref, cgdo_ref, cdz_ref, cdgg_ref):
        c = pl.program_id(0)
        seqs = range(bsz)
        heads = range(HEADS)
        per_seq = lambda ref: [ref[b] for b in seqs]

        def gate_bwd(do_r, od_r, og_r, z_r, gg_r, weight):
            for b in seqs:
                ct = do_r[b].astype(F32)
                for o_r, gate_r, g_r, part, keep_o, keep_gate, acc_ref in (
                        (od_r, z_r, gd_ref, ct[:, :512], cdo_ref, cdz_ref, dgd_ref),
                        (og_r, gg_r, gl_ref, ct[:, 512:], cgdo_ref, cdgg_ref, dgl_ref)):
                    _, vjp_gate = jax.vjp(_gate_norm, o_r[b], gate_r[b], g_r[...])
                    d_o, d_gate, d_row = vjp_gate(part)
                    keep_o[b], keep_gate[b] = d_o, d_gate
                    acc = d_row[:, :HD]
                    for h in range(1, HEADS):
                        acc = acc + d_row[:, h * HD:(h + 1) * HD]
                    acc_ref[...] += weight * acc

        @pl.when(c == 0)
        def _():
            ds_ref[...] = jnp.zeros_like(ds_ref)
            gds_ref[...] = jnp.zeros_like(gds_ref)
            dw2_ref[...] = jnp.zeros_like(dw2_ref)
            dbg_ref[...] = jnp.zeros_like(dbg_ref)
            dgd_ref[...] = jnp.zeros_like(dgd_ref)
            dgl_ref[...] = jnp.zeros_like(dgl_ref)
            gate_bwd(do0_ref, od0_ref, og0_ref, z0_ref, gg0_ref, 1.0)

        do_cur, gdo_cur = per_seq(cdo_ref), per_seq(cgdo_ref)
        for b in seqs:
            dz_ref[b] = cdz_ref[b].astype(MXU_DT)
            dgg_ref[b] = cdgg_ref[b].astype(MXU_DT)
        known = [inv_ref[b, 0, h] for b in seqs for h in heads]

        def both(s_list, q_, k_, v_, g_, gs_list, gq_, gk_, gv_, sm_, w2_, bg_):
            (o, new_s, _), (go, new_gs) = _interleave(
                [_dn_chunk(s_list, q_, k_, v_, g_, inv_known=known),
                 _gla_chunk(gs_list, gq_, gk_, gv_, sm_, w2_, bg_)], REC_SHARES)
            return o, new_s, go, new_gs

        _, vjp = jax.vjp(both, [[ss_ref[b, 0, h] for h in heads] for b in seqs],
                         per_seq(q_ref), per_seq(k_ref), per_seq(v_ref), per_seq(g_ref),
                         [[gss_ref[b, 0, h] for h in heads] for b in seqs],
                         per_seq(gq_ref), per_seq(gk_ref), per_seq(gv_ref), per_seq(sm_ref), w2_ref[...], bg_ref[...])
        ds_in, dq, dk, dv, dg, gds_in, dgq, dgk, dgv, dsm, dw2, dbg = vjp(
            (do_cur, [[ds_ref[b * HEADS + h] for h in heads] for b in seqs],
             gdo_cur, [[gds_ref[b * HEADS + h] for h in heads] for b in seqs]))
        gate_bwd(don_ref, odn_ref, ogn_ref, zn_ref, ggn_ref, jnp.where(c < nc - 1, 1.0, 0.0))
        for b in seqs:
            dq_ref[b], dk_ref[b], dv_ref[b], dg_ref[b] = dq[b], dk[b], dv[b], dg[b]
            dgq_ref[b], dgk_ref[b], dgv_ref[b] = dgq[b].astype(MXU_DT), dgk[b].astype(MXU_DT), dgv[b].astype(MXU_DT)
            dsm_ref[b] = dsm[b]
            for h in heads:
                ds_ref[b * HEADS + h] = ds_in[b][h]
                gds_ref[b * HEADS + h] = gds_in[b][h]
        dw2_ref[...] += dw2
        dbg_ref[...] += dbg

    tok = lambda width, blk=0, idx=rev: _chunk_spec(bsz, width, idx, blk)
    w2_spec = pl.BlockSpec((128, 512), lambda c: (0, 0))
    bg_spec = pl.BlockSpec((1, 512), lambda c: (0, 0))
    g128 = pl.BlockSpec((1, HD), lambda c: (0, 0))
    f512 = jax.ShapeDtypeStruct((bsz, t_total, 512), F32)
    b512 = jax.ShapeDtypeStruct((bsz, t_total, 512), MXU_DT)
    f128 = jax.ShapeDtypeStruct((bsz, t_total, 128), F32)
    gate_in = lambda idx: [tok(D, idx=idx), tok(512, idx=idx), tok(512, idx=idx),
                           tok(512, Z_BLK, idx), tok(512, GG_BLK, idx)]
    chunk_scratch = pltpu.VMEM((bsz, CHUNK, 512), F32)
    return pl.pallas_call(
        body, name="rec_bwd", grid=(nc,),
        in_specs=[tok(512), tok(512), tok(512), tok(128), _state_spec(bsz, rev), _inv_spec(bsz, rev),
                  tok(512, GQ_BLK), tok(512, GK_BLK), tok(512, GV_BLK), tok(128, SM_BLK), w2_spec, bg_spec,
                  _state_spec(bsz, rev), *gate_in(rev), *gate_in(nxt), bg_spec, bg_spec],
        out_specs=(tok(512), tok(512), tok(512), tok(128), tok(512), tok(512), tok(512), tok(128), w2_spec, bg_spec,
                   tok(512), tok(512), g128, g128),
        out_shape=(f512, f512, f512, f128, b512, b512, b512, f128,
                   jax.ShapeDtypeStruct((128, 512), F32), jax.ShapeDtypeStruct((1, 512), F32),
                   b512, b512, jax.ShapeDtypeStruct((1, HD), F32), jax.ShapeDtypeStruct((1, HD), F32)),
        scratch_shapes=[pltpu.VMEM((bsz * HEADS, HD, HD), F32), pltpu.VMEM((bsz * HEADS, HD, HD), F32),
                        chunk_scratch, chunk_scratch, chunk_scratch, chunk_scratch],
        compiler_params=_params(("arbitrary",)))(
            q, k, v, gates, s_dn, inv_dn, proj, proj, proj, proj, w2, bg, s_gla,
            do, o_dn, o_gla, proj, proj, do, o_dn, o_gla, proj, proj, grow_dn, grow_gla)


Z_BLK, GG_BLK = P_Z // 512, P_GG // 512


def _mix_out_fwd(o_dn, o_gla, proj, grow_dn, grow_gla):
    bsz, t_total, _ = o_dn.shape
    tt = _div_tile(t_total, ROW_TILE)

    def body(od_ref, og_ref, z_ref, gg_ref, gd_ref, gl_ref, o_ref):
        o_ref[0, :, :512] = _gate_norm(od_ref[0], z_ref[0], gd_ref[...]).astype(MXU_DT)
        o_ref[0, :, 512:] = _gate_norm(og_ref[0], gg_ref[0], gl_ref[...]).astype(MXU_DT)

    def col(blk):
        return pl.BlockSpec((1, tt, 512), lambda b, t: (b, t, blk))

    return pl.pallas_call(
        body, name="mix_out_fwd", grid=(bsz, t_total // tt),
        in_specs=[col(0), col(0), col(Z_BLK), col(GG_BLK), _vec_spec(512), _vec_spec(512)],
        out_specs=_tok_spec(tt), out_shape=jax.ShapeDtypeStruct((bsz, t_total, D), MXU_DT),
        compiler_params=_params(("parallel", "parallel")))(o_dn, o_gla, proj, proj, grow_dn, grow_gla)


def _sum_slots(x, name):
    n, rows, cols = x.shape
    tr = _div_tile(rows, max(8, (1 << 19) // cols))

    def body(x_ref, o_ref):
        acc = x_ref[0].astype(F32)
        for i in range(1, n):
            acc = acc + x_ref[i].astype(F32)
        o_ref[...] = acc

    return pl.pallas_call(
        body, name=name, grid=(rows // tr,),
        in_specs=[pl.BlockSpec((n, tr, cols), lambda i: (0, i, 0))],
        out_specs=pl.BlockSpec((tr, cols), lambda i: (i, 0)),
        out_shape=jax.ShapeDtypeStruct((rows, cols), F32), compiler_params=_params(("parallel",)))(x)


def _adamw_math(w, g, m, v):
    nm = ADAM_B1 * m + (1.0 - ADAM_B1) * g
    nv = ADAM_B2 * v + (1.0 - ADAM_B2) * (g * g)
    m_hat = nm / (1.0 - ADAM_B1 ** ADAM_STEP)
    v_hat = nv / (1.0 - ADAM_B2 ** ADAM_STEP)
    return -ADAM_LR * (m_hat / (jnp.sqrt(v_hat) + ADAM_EPS) + ADAM_WD * w), nm, nv


def _adamw(w, g, m, v, name):
    _, rows, cols = w.shape
    tr = _div_tile(rows, max(8, (1 << 18) // cols))

    def body(w_ref, g_ref, m_ref, v_ref, d_ref, nm_ref, nv_ref):
        d_ref[...], nm_ref[...], nv_ref[...] = _adamw_math(w_ref[...], g_ref[...], m_ref[...], v_ref[...])

    spec = pl.BlockSpec((1, tr, cols), lambda i: (0, i, 0))
    shp = jax.ShapeDtypeStruct(w.shape, F32)
    return pl.pallas_call(body, name=name, grid=(rows // tr,), in_specs=[spec] * 4, out_specs=(spec,) * 3,
                          out_shape=(shp,) * 3, compiler_params=_params(("parallel",)))(w, g, m, v)


def _sum_adamw(parts, w, m, v, name):
    n, rows, cols = parts.shape
    tr = _div_tile(rows, max(8, (1 << 18) // cols))

    def body(p_ref, w_ref, m_ref, v_ref, g_ref, d_ref, nm_ref, nv_ref):
        g = p_ref[0].astype(F32)
        for i in range(1, n):
            g = g + p_ref[i].astype(F32)
        g_ref[...] = g
        d_ref[0], nm_ref[0], nv_ref[0] = _adamw_math(w_ref[0], g, m_ref[0], v_ref[0])

    spec = pl.BlockSpec((1, tr, cols), lambda i: (0, i, 0))
    shp = jax.ShapeDtypeStruct(w.shape, F32)
    return pl.pallas_call(
        body, name=name, grid=(rows // tr,),
        in_specs=[pl.BlockSpec((n, tr, cols), lambda i: (0, i, 0)), spec, spec, spec],
        out_specs=(pl.BlockSpec((tr, cols), lambda i: (i, 0)), spec, spec, spec),
        out_shape=(jax.ShapeDtypeStruct((rows, cols), F32), shp, shp, shp),
        compiler_params=_params(("parallel",)))(parts, w, m, v)


def _adamw_many(ws, gs, ms, vs, name):
    n = len(ws)

    def body(*refs):
        for i in range(n):
            d, nm, nv = _adamw_math(refs[i][...], refs[n + i][...], refs[2 * n + i][...], refs[3 * n + i][...])
            refs[4 * n + i][...] = d
            refs[5 * n + i][...] = nm
            refs[6 * n + i][...] = nv

    shapes = tuple(jax.ShapeDtypeStruct(w.shape, F32) for w in ws)
    outs = pl.pallas_call(body, name=name, out_shape=shapes * 3, compiler_params=_params())(*ws, *gs, *ms, *vs)
    return outs[:n], outs[n:2 * n], outs[2 * n:]


def _position():
    return lax.axis_index("x"), lax.axis_index("y"), lax.axis_index("c")


def _slot(px, py, pc):
    return 4 * px + 2 * py + pc


def _gather_small(x, name):
    rows, cols = x.shape

    def body(x_ref, o_ref, send_sems, recv_sems):
        mx, my, mc = _position()

        def peer(k):
            return (mx ^ ((k >> 2) & 1), my ^ ((k >> 1) & 1), mc ^ (k & 1))

        o_ref[_slot(mx, my, mc)] = x_ref[...]
        sends = []
        for k in range(1, N_DEV):
            cp = pltpu.make_async_remote_copy(src_ref=x_ref, dst_ref=o_ref.at[_slot(mx, my, mc)],
                                              send_sem=send_sems.at[k - 1], recv_sem=recv_sems.at[k - 1],
                                              device_id=peer(k), device_id_type=MESH)
            cp.start()
            sends.append(cp)
        for k in range(1, N_DEV):
            pltpu.make_async_remote_copy(src_ref=x_ref, dst_ref=o_ref.at[_slot(*peer(k))],
                                         send_sem=send_sems.at[k - 1], recv_sem=recv_sems.at[k - 1],
                                         device_id=peer(k), device_id_type=MESH).wait_recv()
        for cp in sends:
            cp.wait_send()

    return pl.pallas_call(
        body, name=name, out_shape=jax.ShapeDtypeStruct((N_DEV, rows, cols), x.dtype),
        in_specs=[pl.BlockSpec(memory_space=pltpu.VMEM)], out_specs=pl.BlockSpec(memory_space=pltpu.VMEM),
        scratch_shapes=[pltpu.SemaphoreType.DMA((N_DEV - 1,)), pltpu.SemaphoreType.DMA((N_DEV - 1,))],
        compiler_params=pltpu.CompilerParams(vmem_limit_bytes=VMEM_LIMIT_V7X))(x)


def _gather_big(shards):
    n = len(shards)

    def body(*refs):
        xs, outs = refs[:n], refs[n:2 * n]
        send_sems, recv_sems, local_sems = refs[2 * n:]
        mx, my, mc = _position()
        me, sibling = (mx, my, mc), (mx, my, 1 - mc)
        chips = [(1 - mx, my), (mx, 1 - my), (1 - mx, 1 - my)]

        def copy(a, k, block, to, src=None):
            dst = outs[a].at[_slot(*block)]
            return pltpu.make_async_remote_copy(src_ref=dst if src is None else src, dst_ref=dst,
                                                send_sem=send_sems.at[7 * a + k], recv_sem=recv_sems.at[7 * a + k],
                                                device_id=to, device_id_type=MESH)

        mine = [pltpu.make_async_copy(xs[a], outs[a].at[_slot(*me)], local_sems.at[a]) for a in range(n)]
        for cp in mine:
            cp.start()
        started = []
        for a in range(n):
            started.append(copy(a, 0, me, sibling, src=xs[a]))
            started += [copy(a, 1 + j, me, (*chip, mc), src=xs[a]) for j, chip in enumerate(chips)]
        for cp in started:
            cp.start()
        for j, chip in enumerate(chips):
            for a in range(n):
                copy(a, 1 + j, (*chip, mc), me).wait_recv()
                fwd = copy(a, 4 + j, (*chip, mc), sibling)
                fwd.start()
                started.append(fwd)
        for a in range(n):
            copy(a, 0, sibling, me).wait_recv()
            for j, chip in enumerate(chips):
                copy(a, 4 + j, (*chip, 1 - mc), me).wait_recv()
        for cp in started:
            cp.wait_send()
        for cp in mine:
            cp.wait()

    any_spec = pl.BlockSpec(memory_space=pl.ANY)
    return pl.pallas_call(
        body, name="gather_weights",
        out_shape=tuple(jax.ShapeDtypeStruct((N_DEV,) + s.shape, s.dtype) for s in shards),
        in_specs=[any_spec] * n, out_specs=(any_spec,) * n,
        scratch_shapes=[pltpu.SemaphoreType.DMA((7 * n,)), pltpu.SemaphoreType.DMA((7 * n,)),
                        pltpu.SemaphoreType.DMA((n,))])(*shards)


def _peer(pos, k):
    mx, my, mc = pos
    return (mx ^ ((k >> 2) & 1), my ^ ((k >> 1) & 1), mc ^ (k & 1))


def _exchange_copies(srcs, lands, send_sems, recv_sems, by_owner, arrivals):
    pos = _position()
    me = _slot(*pos)
    out = []
    for a, (src, land) in enumerate(zip(srcs, lands)):
        for k in range(1, N_DEV):
            peer = _peer(pos, k)
            mine = src.at[_slot(*peer)] if by_owner else src
            out.append(pltpu.make_async_remote_copy(
                src_ref=mine, dst_ref=land.at[_slot(*peer) if arrivals else me],
                send_sem=send_sems.at[7 * a + k - 1], recv_sem=recv_sems.at[7 * a + k - 1],
                device_id=peer, device_id_type=MESH))
    return out


_HBM_SPEC = pl.BlockSpec(memory_space=pltpu.HBM)
_SEM_SPEC = pl.BlockSpec(memory_space=pltpu.SEMAPHORE)
_DATAFLOW = pltpu.SideEffectType.DATAFLOW_SIDE_EFFECTING


def _exchange_start(name, srcs, slab_shapes, after, by_owner, carry=()):
    n, na, nc = len(srcs), len(after), len(carry)
    lands = [pltpu.with_memory_space_constraint(lax.empty((N_DEV,) + s, x.dtype), pltpu.HBM)
             for s, x in zip(slab_shapes, srcs)]
    thru = [pltpu.with_memory_space_constraint(x, pltpu.HBM) for x in [*srcs, *lands, *carry]]

    def body(*refs):
        src_refs, land_refs = refs[:n], refs[n:2 * n]
        send_sems, recv_sems = refs[len(thru) + na], refs[len(thru) + na + 1]
        token = refs[-1]
        for send in _exchange_copies(src_refs, land_refs, send_sems, recv_sems, by_owner, arrivals=False):
            send.start()
        token[...] = jnp.zeros_like(token)

    outs = pl.pallas_call(
        body, name=name,
        out_shape=(pltpu.SemaphoreType.DMA((7 * n,)), pltpu.SemaphoreType.DMA((7 * n,)),
                   *[pltpu.HBM(x.shape, x.dtype) for x in thru], jax.ShapeDtypeStruct((8, 128), F32)),
        in_specs=[_HBM_SPEC] * len(thru) + [pl.BlockSpec(memory_space=pl.ANY)] * na,
        out_specs=(_SEM_SPEC, _SEM_SPEC, *[_HBM_SPEC] * len(thru), pl.BlockSpec(memory_space=pltpu.VMEM)),
        input_output_aliases={i: 2 + i for i in range(len(thru))},
        compiler_params=pltpu.CompilerParams(has_side_effects=_DATAFLOW))(*thru, *after)
    return (outs[0], outs[1], list(outs[2:2 + n]), list(outs[2 + n:2 + 2 * n]), outs[-1],
            list(outs[2 + 2 * n:2 + 2 * n + nc]))


def _exchange_wait(name, send_sems, recv_sems, srcs, lands, after, by_owner):
    n = len(srcs)

    def body(*refs):
        src_refs, land_refs = refs[:n], refs[n:2 * n]
        s_sems, r_sems = refs[2 * n], refs[2 * n + 1]
        for send in _exchange_copies(src_refs, land_refs, s_sems, r_sems, by_owner, arrivals=False):
            send.wait_send()
        for recv in _exchange_copies(src_refs, land_refs, s_sems, r_sems, by_owner, arrivals=True):
            recv.wait_recv()

    outs = pl.pallas_call(
        body, name=name,
        out_shape=(*[pltpu.HBM(x.shape, x.dtype) for x in srcs], *[pltpu.HBM(l.shape, l.dtype) for l in lands]),
        in_specs=[_HBM_SPEC] * (2 * n) + [_SEM_SPEC, _SEM_SPEC, pl.BlockSpec(memory_space=pl.ANY)],
        out_specs=tuple([_HBM_SPEC] * (2 * n)),
        input_output_aliases={i: i for i in range(2 * n)},
        compiler_params=pltpu.CompilerParams(has_side_effects=_DATAFLOW))(*srcs, *lands, send_sems, recv_sems, after)
    return list(outs[:n]), list(outs[n:])


def _pad_heads(x, axis):
    shp = list(x.shape)
    x4 = x.reshape(shp[:axis] + [HEADS, GLA_KEY] + shp[axis + 1:])
    pad = [(0, 0)] * x4.ndim
    pad[axis + 1] = (0, HD - GLA_KEY)
    return jnp.pad(x4, pad).reshape(shp[:axis] + [HEADS * HD] + shp[axis + 1:])


def _unpad_heads(x, axis):
    shp = list(x.shape)
    x4 = x.reshape(shp[:axis] + [HEADS, HD] + shp[axis + 1:])
    x4 = lax.slice_in_dim(x4, 0, GLA_KEY, axis=axis + 1)
    return x4.reshape(shp[:axis] + [HEADS * GLA_KEY] + shp[axis + 1:])


O_Z_END, O_AB, O_GQ, O_GK, O_GV, O_R = 2048, 2048, 2056, 2312, 2568, 3592


def _padded_row(f):
    if f < O_Z_END:
        return f
    if f < O_GQ:
        return P_SM + (f - O_AB)
    if f < O_GV:
        base, g = (P_GQ, f - O_GQ) if f < O_GK else (P_GK, f - O_GK)
        return base + HD * (g // GLA_KEY) + g % GLA_KEY
    if f < O_R:
        return P_GV + (f - O_GV)
    return P_SM + 8 + (f - O_R)


def _runs(pairs):
    out = []
    for d, s in pairs:
        if out and out[-1][0] + out[-1][2] == d and out[-1][1] + out[-1][2] == s:
            out[-1][2] += 1
        else:
            out.append([d, s, 1])
    return out


def _pad_in_rows(shards):
    wt = shards.reshape(IN_W, D)
    return jnp.concatenate([
        wt[:O_Z_END], _pad_heads(wt[O_GQ:O_GK], 0), _pad_heads(wt[O_GK:O_GV], 0), wt[O_GV:O_R],
        wt[O_AB:O_GQ], wt[O_R:], jnp.zeros((P_W - P_SM - 8 - GATE_RANK, D), wt.dtype)], axis=0)


def _unpad_in_rows(gt):
    per = IN_W // N_DEV
    return jnp.stack([
        jnp.concatenate([gt[src:src + n] for _, src, n in
                         _runs([(f, _padded_row(f)) for f in range(j * per, (j + 1) * per)])], axis=0)
        for j in range(N_DEV)])


def _lane_row(vals, width=128):
    return jnp.pad(vals.reshape(1, -1), ((0, 0), (0, width - vals.size)))


SMALL_NAMES = ["ln0_g", "ln0_b", "b_ada", "dn_conv", "dn_a_log", "dn_dt_bias", "dn_norm_g", "gla_w_gate2",
               "gla_b_gate", "gla_norm_g", "ln1_g", "ln1_b", "ffn_conv", "ffn_conv_b", "ln2_g", "ln2_b"]
WEIGHTS = ["ln0_g", "ln0_b", "w_ada", "b_ada", "w_in", "dn_conv", "dn_a_log", "dn_dt_bias", "dn_norm_g",
           "gla_w_gate2", "gla_b_gate", "gla_norm_g", "w_o", "ln1_g", "ln1_b", "ffn_w_up", "ffn_conv", "ffn_conv_b",
           "ffn_w_down", "ln2_g", "ln2_b"]


def kernel(x, c, ln0_g, ln0_b, w_ada, b_ada, w_in, dn_conv, dn_a_log, dn_dt_bias, dn_norm_g, gla_w_gate2, gla_b_gate, gla_norm_g, w_o, ln1_g, ln1_b, ffn_w_up, ffn_conv, ffn_conv_b, ffn_w_down, ln2_g, ln2_b, loss_target, m_ln0_g, m_ln0_b, m_w_ada, m_b_ada, m_w_in, m_dn_conv, m_dn_a_log, m_dn_dt_bias, m_dn_norm_g, m_gla_w_gate2, m_gla_b_gate, m_gla_norm_g, m_w_o, m_ln1_g, m_ln1_b, m_ffn_w_up, m_ffn_conv, m_ffn_conv_b, m_ffn_w_down, m_ln2_g, m_ln2_b, v_ln0_g, v_ln0_b, v_w_ada, v_b_ada, v_w_in, v_dn_conv, v_dn_a_log, v_dn_dt_bias, v_dn_norm_g, v_gla_w_gate2, v_gla_b_gate, v_gla_norm_g, v_w_o, v_ln1_g, v_ln1_b, v_ffn_w_up, v_ffn_conv, v_ffn_conv_b, v_ffn_w_down, v_ln2_g, v_ln2_b):
    args = dict(locals())
    w_given = {n: args[n] for n in WEIGHTS}
    m_given = {n: args["m_" + n] for n in WEIGHTS}
    v_given = {n: args["v_" + n] for n in WEIGHTS}
    bsz, t_total, _ = x.shape
    ntok = bsz * t_total
    mx, my, mc = _position()
    me = _slot(mx, my, mc)

    pack1 = jnp.concatenate([c.reshape(-1), dn_conv.reshape(-1), gla_w_gate2.reshape(-1), ffn_conv.reshape(-1)])
    n1 = pack1.size
    rows1 = -(-n1 // 1024) * 8
    pack1 = jnp.pad(pack1, (0, rows1 * 128 - n1)).reshape(rows1, 128)
    got1 = _gather_small(pack1, "gather_cond").reshape(N_DEV, -1)
    o1 = bsz * D
    o2 = o1 + dn_conv.size
    o3 = o2 + gla_w_gate2.size
    c_all = got1[:, :o1].reshape(N_DEV * bsz, D)
    dn_conv_f = got1[:, o1:o2].reshape(N_DEV, DN_CONV_K, -1).transpose(1, 0, 2).reshape(DN_CONV_K, QKV_W)
    gate2_f = got1[:, o2:o3].reshape(N_DEV, GATE_RANK, -1).transpose(1, 0, 2).reshape(GATE_RANK, HEADS * GLA_KEY)
    ffn_conv_f = got1[:, o3:n1].reshape(N_DEV, FFN_CONV_K, -1).transpose(1, 0, 2).reshape(FFN_CONV_K, 2 * D_FF)

    win_t = w_in[0].T.astype(MXU_DT)
    wup_t = ffn_w_up[0].T.astype(MXU_DT)
    (win_all,) = _gather_big([win_t])
    win_p = _pad_in_rows(win_all)
    cw_p, cb_p = _ffn_pair(ffn_conv_f, 1), _ffn_pair(ffn_conv_b, 1)

    ncol = w_ada.shape[2]
    b_cols = lax.dynamic_slice_in_dim(b_ada, me * ncol, ncol, axis=1)
    mod_part = _ada_fwd(c_all, w_ada[0], b_cols)
    mod_all = _gather_small(mod_part.reshape(-1, 128), "gather_mod").reshape(N_DEV, N_DEV * bsz, ncol)
    mod = lax.dynamic_slice_in_dim(mod_all, me * bsz, bsz, axis=1).transpose(1, 0, 2).reshape(bsz, 6, 1, D)
    late = [w_o[0].astype(MXU_DT), wup_t, ffn_w_down[0].astype(MXU_DT)]
    ag_send, ag_recv, ag_src, ag_land, ag_token, _ = _exchange_start(
        "gather_start", late, [w.shape for w in late], [win_all, mod_all], by_owner=False)
    mod = mod + ag_token[0, 0]
    sh_a, sc_a, gt_a, sh_f, sc_f, gt_f = (mod[:, i] for i in range(6))

    g0, b0 = ln0_g.reshape(1, D), ln0_b.reshape(1, D)
    alog_row, dt_row = _lane_row(dn_a_log[0]), _lane_row(dn_dt_bias[0])
    grow_dn, grow_gla = jnp.tile(dn_norm_g, (1, HEADS)), jnp.tile(gla_norm_g, (1, HEADS))
    w2 = jnp.zeros((128, HEADS * HD), F32).at[SM_R:SM_R + GATE_RANK].set(_pad_heads(gate2_f, 1))
    bg = _pad_heads(gla_b_gate, 1)

    h_a = _ln0_mod(x, g0, b0, sc_a, sh_a)
    proj = _mm(h_a.reshape(ntok, D), win_p, "nt", F32, "mm_proj", tm=1024, tn=1408).reshape(bsz, t_total, P_W)
    q, k, v, gates, o_dn, s_dn, inv_dn, o_gla, s_gla = _rec_fwd(proj, dn_conv_f, alog_row, dt_row, w2, bg)
    o_mix = _mix_out_fwd(o_dn, o_gla, proj, grow_dn, grow_gla)
    late, landed = _exchange_wait("gather_wait", ag_send, ag_recv, ag_src, ag_land, o_mix, by_owner=False)
    wo_all, wup_all, wdn_all = (lax.dynamic_update_slice(l, w[None], (me, 0, 0)) for l, w in zip(landed, late))
    wo_f = wo_all.reshape(D, D)
    wup_f = _ffn_pair(wup_all.reshape(2 * D_FF, D), 0)
    wdn_f = wdn_all.reshape(D_FF, D)
    y = _mm(o_mix.reshape(ntok, D), wo_f, "nn", MXU_DT, "mm_wo", tm=1024, tn=1024).reshape(bsz, t_total, D)
    r1, h_f = _res_ln_mod(x, y, gt_a, g0, b0, ln1_g, ln1_b, sc_f, sh_f)
    up, act = _ffn_up_act(h_f, wup_f, cw_p, cb_p)
    y2 = _mm(act.reshape(ntok, D_FF), wdn_f, "nn", MXU_DT, "mm_down", tm=1024, tn=1024).reshape(bsz, t_total, D)
    loss_rows, dr2, dy2, dgt_f, d_ln2_g, d_ln2_b = _final_fwd_bwd(r1, y2, gt_f, ln1_g, ln1_b, ln2_g, ln2_b, loss_target)
    loss_part = (0.5 / D) * jnp.sum(loss_rows)

    dy2_2 = dy2.reshape(ntok, D)
    g_wdn = _mm(act.reshape(ntok, D_FF), dy2_2, "tn", MXU_DT, "mm_gwdn", tm=1408, tn=1024)
    dup, d_cw_p, d_cb_p = _ffn_act_bwd(up, dy2, wdn_f, cw_p, cb_p)
    d_ffn_conv, d_ffn_conv_b = _ffn_unpair(d_cw_p, 1), _ffn_unpair(d_cb_p, 1)
    dup_2 = dup.reshape(ntok, 2 * D_FF)
    dh_f = _mm(dup_2, wup_f, "nn", MXU_DT, "mm_dhf", tn=1024).reshape(bsz, t_total, D)
    g_wup_t = _mm(dup_2, h_f.reshape(ntok, D), "tn", MXU_DT, "mm_gwup", tm=1408, tn=1024)
    ffn_parts = [_ffn_unpair(g_wup_t, 0).reshape(N_DEV, -1, D), g_wdn.reshape(N_DEV, -1, D)]
    rs_send, rs_recv, rs_src, rs_land, rs_token, _ = _exchange_start(
        "scatter_start", ffn_parts, [p.shape[1:] for p in ffn_parts], [dh_f], by_owner=True)
    dr1, dsc_f, dsh_f, d_ln1_g, d_ln1_b, dy, dgt_a = _ln_bwd_call(
        "ln1_bwd", dr2, dh_f, r1, ln1_g, ln1_b, sc_f + rs_token[0, 0], y=y, gt=gt_a)

    dy_2 = dy.reshape(ntok, D)
    do = _mm(dy_2, wo_f, "nt", MXU_DT, "mm_do", tm=1024, tn=1024).reshape(bsz, t_total, D)
    g_wo = _mm(o_mix.reshape(ntok, D), dy_2, "tn", MXU_DT, "mm_gwo", tm=512, tn=1024)
    dq, dk, dv, dgates, dgq, dgk, dgv, dsm_gla, d_w2, d_bg, dz, dgg, d_dn_norm, d_gla_norm = _rec_bwd(
        q, k, v, gates, s_dn, inv_dn, proj, w2, bg, s_gla, do, o_dn, o_gla, grow_dn, grow_gla)
    dqkv, dsm, d_dn_conv, d_alog_row, d_dt_row = _dn_pre_bwd(
        proj, dq, dk, dv, dgates, dsm_gla, dn_conv_f, alog_row, dt_row)
    dproj = jnp.concatenate([dqkv, dz, dgq, dgk, dgv, dgg, dsm], axis=-1).reshape(ntok, P_W)
    g_win_p = _mm(dproj, h_a.reshape(ntok, D), "tn", MXU_DT, "mm_gwin", tm=1408, tn=1024)
    mix_parts = [_unpad_in_rows(g_win_p), g_wo.reshape(N_DEV, -1, D)]
    rs2_send, rs2_recv, rs2_src, rs2_land, rs2_token, (win_p_late,) = _exchange_start(
        "scatter_mix_start", mix_parts, [p.shape[1:] for p in mix_parts], [], by_owner=True, carry=[win_p])
    dh_a = _mm(dproj, win_p_late, "nn", MXU_DT, "mm_dha", tn=1024).reshape(bsz, t_total, D)
    grad_x, dsc_a, dsh_a, d_ln0_g, d_ln0_b = _ln_bwd_call(
        "ln0_bwd", dr1, dh_a, x, g0, b0, sc_a + rs2_token[0, 0])

    delta, new_m, new_v, big_grads = {}, {}, {}, {}
    flip = lambda a: jnp.swapaxes(a, 1, 2)

    def update_owned(n, landed, mine):
        parts = lax.dynamic_update_slice(landed, lax.dynamic_slice_in_dim(mine, me, 1, axis=0), (me, 0, 0))
        turn = flip if parts.shape[1:] != w_given[n].shape[1:] else (lambda a: a)
        g, d_, m_, v_ = _sum_adamw(parts, turn(w_given[n]), turn(m_given[n]), turn(v_given[n]), "adamw_" + n)
        big_grads[n], delta[n], new_m[n], new_v[n] = turn(g[None]), turn(d_), turn(m_), turn(v_)

    ffn_parts, ffn_landed = _exchange_wait("scatter_wait", rs_send, rs_recv, rs_src, rs_land, grad_x, by_owner=True)
    update_owned("ffn_w_up", ffn_landed[0], ffn_parts[0])
    update_owned("ffn_w_down", ffn_landed[1], ffn_parts[1])
    ffn_done = 0.0 * (new_v["ffn_w_up"][0, 0, 0] + new_v["ffn_w_down"][0, 0, 0])

    dmod = jnp.concatenate([dsh_a, dsc_a, dgt_a, dsh_f, dsc_f, dgt_f], axis=1).reshape(-1)
    small_parts = {
        "ln0_g": d_ln0_g, "ln0_b": d_ln0_b, "ln1_g": d_ln1_g, "ln1_b": d_ln1_b, "ln2_g": d_ln2_g, "ln2_b": d_ln2_b,
        "dn_a_log": d_alog_row[:, :HEADS], "dn_dt_bias": d_dt_row[:, :HEADS],
        "dn_norm_g": d_dn_norm, "gla_norm_g": d_gla_norm, "gla_b_gate": _unpad_heads(d_bg, 1),
        "ffn_conv_b": d_ffn_conv_b, "dn_conv": d_dn_conv,
        "gla_w_gate2": _unpad_heads(d_w2[SM_R:SM_R + GATE_RANK], 1), "ffn_conv": d_ffn_conv}
    order = sorted(small_parts)
    flat = jnp.concatenate([small_parts[n].reshape(-1) for n in order] + [(loss_part + ffn_done).reshape(1), dmod])
    n3 = flat.size
    rows3 = -(-n3 // 1024) * 8
    pack3 = jnp.pad(flat, (0, rows3 * 128 - n3)).reshape(rows3, 128)
    got3 = _gather_small(pack3, "gather_small_grads")
    tot3 = _sum_slots(got3, "sum_small_grads").reshape(-1)
    grads = {}
    off = 0
    for n in order:
        size = small_parts[n].size
        grads[n] = tot3[off:off + size]
        off += size
    loss = tot3[off]
    off += 1
    dmod_all = got3.reshape(N_DEV, -1)[:, off:off + dmod.size].reshape(N_DEV * bsz, 6 * D)
    dmod_cols = lax.dynamic_slice_in_dim(dmod_all, me * ncol, ncol, axis=1)
    g_wada, g_bada = _ada_bwd(c_all, dmod_all, dmod_cols)
    grads["b_ada"] = g_bada

    def col_shard(full, rows):
        part = full.reshape(rows, -1)
        width = part.shape[1] // N_DEV
        return lax.dynamic_slice_in_dim(part, me * width, width, axis=1)

    grads["dn_conv"] = col_shard(grads["dn_conv"], DN_CONV_K)
    grads["gla_w_gate2"] = col_shard(grads["gla_w_gate2"], GATE_RANK)
    grads["ffn_conv"] = col_shard(grads["ffn_conv"], FFN_CONV_K)
    grads = {n: g.reshape(w_given[n].shape) for n, g in grads.items()}
    mix_parts, mix_landed = _exchange_wait("scatter_mix_wait", rs2_send, rs2_recv, rs2_src, rs2_land, grad_x,
                                           by_owner=True)
    update_owned("w_in", mix_landed[0], mix_parts[0])
    update_owned("w_o", mix_landed[1], mix_parts[1])
    grads["w_ada"] = g_wada.reshape(w_ada.shape)
    delta["w_ada"], new_m["w_ada"], new_v["w_ada"] = _adamw(w_ada, grads["w_ada"], m_w_ada, v_w_ada, "adamw_w_ada")
    grads.update(big_grads)
    d_s, m_s, v_s = _adamw_many(*[[src[n] for n in SMALL_NAMES] for src in (w_given, grads, m_given, v_given)],
                                "adamw_small")
    for i, n in enumerate(SMALL_NAMES):
        delta[n], new_m[n], new_v[n] = d_s[i], m_s[i], v_s[i]

    return (loss, grad_x, *[grads[n] for n in WEIGHTS], *[delta[n] for n in WEIGHTS],
            *[new_m[n] for n in WEIGHTS], *[new_v[n] for n in WEIGHTS])
```
